```python
import jax, jax.numpy as jnp
from jax import lax
import numpy as np

D_MODEL = 2048
BATCH = 8
SEQ = 2048
DEPTH = 1

HEAD_DIM = 64
N_Q_HEADS = 32
N_KV_HEADS = 4
GROUP = N_Q_HEADS // N_KV_HEADS
WINDOW = 128
BLOCK = 128
D_ATTN = N_Q_HEADS * HEAD_DIM
D_KV = N_KV_HEADS * HEAD_DIM
D_RNN = 2560
N_RNN_BLOCKS = 16
RNN_BLOCK_W = D_RNN // N_RNN_BLOCKS
RNN_CONV_W = 4
LRU_C = 8.0
D_FF = 3 * D_MODEL
FFN_CONV_W = 3
LN_EPS = 1e-5
ALPHA = (2 * DEPTH) ** 0.25
BETA = (8 * DEPTH) ** -0.25
SPLIT_SIZES = (D_ATTN, D_KV, D_KV, D_RNN, D_RNN, 2 * D_MODEL)
SPLIT_POINTS = tuple(int(s) for s in np.cumsum(SPLIT_SIZES)[:-1])
D_IN = int(sum(SPLIT_SIZES))
V_START = D_ATTN + D_KV

kernel_name = "hybrid_swa_rglru_gated_merge_deepnorm"


def alibi_slopes(n_heads):
    h = np.arange(1, n_heads + 1, dtype=np.float32)
    return jnp.asarray(2.0 ** (-8.0 * h / n_heads), dtype=jnp.float32)


def layer_norm(x, g, b):
    x32 = x.astype(jnp.float32)
    mu = jnp.mean(x32, axis=-1, keepdims=True)
    var = jnp.mean(jnp.square(x32 - mu), axis=-1, keepdims=True)
    y = (x32 - mu) * lax.rsqrt(var + LN_EPS) * g.astype(jnp.float32) + b.astype(jnp.float32)
    return y.astype(x.dtype)


def causal_dwconv(x, w, b):
    width = w.shape[0]
    s = x.shape[1]
    xp = jnp.pad(x, ((0, 0), (width - 1, 0), (0, 0)))
    y = sum(xp[:, k:k + s] * w[k] for k in range(width))
    return y + b


def sliding_window_attention(q, k, v, sinks):
    bsz, s = q.shape[0], q.shape[1]
    nb = s // BLOCK
    qb = q.reshape(bsz, nb, BLOCK, N_KV_HEADS, GROUP, HEAD_DIM)

    def band(t):
        tb = t.reshape(bsz, nb, BLOCK, N_KV_HEADS, HEAD_DIM)
        prev = jnp.pad(tb, ((0, 0), (1, 0), (0, 0), (0, 0), (0, 0)))[:, :-1]
        return jnp.concatenate([prev, tb], axis=2)

    kb, vb = band(k), band(v)
    scores = jnp.einsum('bnqhgd,bnkhd->bhgnqk', qb, kb).astype(jnp.float32) * (HEAD_DIM ** -0.5)
    qi = jnp.arange(BLOCK)[:, None]
    kj = jnp.arange(2 * BLOCK)[None, :]
    dist = BLOCK + qi - kj
    blk = jnp.arange(nb)[:, None, None]
    key_pos = (blk - 1) * BLOCK + kj
    valid = (dist >= 0) & (dist < WINDOW) & (key_pos >= 0)
    slopes = alibi_slopes(N_Q_HEADS).reshape(N_KV_HEADS, GROUP, 1, 1, 1)
    scores = scores - slopes * dist.astype(jnp.float32)
    scores = jnp.where(valid, scores, -jnp.inf)
    sink = sinks.astype(jnp.float32).reshape(N_KV_HEADS, GROUP, 1, 1, 1)
    m = jnp.maximum(jnp.max(scores, axis=-1, keepdims=True), sink)
    p = jnp.exp(scores - m)
    denom = jnp.sum(p, axis=-1, keepdims=True) + jnp.exp(sink - m)
    p = (p / denom).astype(v.dtype)
    o = jnp.einsum('bhgnqk,bnkhd->bnqhgd', p, vb)
    return o.reshape(bsz, s, D_ATTN)


def rg_lru(x, w_a, b_a, w_i, b_i, lam):
    bsz, s, _ = x.shape
    xb = x.reshape(bsz, s, N_RNN_BLOCKS, RNN_BLOCK_W)
    r = jax.nn.sigmoid(jnp.einsum('bsnc,ncd->bsnd', xb, w_a).reshape(bsz, s, D_RNN) + b_a)
    i = jax.nn.sigmoid(jnp.einsum('bsnc,ncd->bsnd', xb, w_i).reshape(bsz, s, D_RNN) + b_i)
    log_a = -LRU_C * r.astype(jnp.float32) * jax.nn.softplus(-lam.astype(jnp.float32))
    a = jnp.exp(log_a)
    u = jnp.sqrt(-jnp.expm1(2.0 * log_a)) * (i * x).astype(jnp.float32)

    def combine(left, right):
        a_l, b_l = left
        a_r, b_r = right
        return a_l * a_r, a_r * b_l + b_r

    _, h = lax.associative_scan(combine, (a, u), axis=1)
    return h.astype(x.dtype)


def _fwd_setup_inputs(seed: int = 0) -> dict:
    key = jax.random.key(seed)
    ks = jax.random.split(key, 24)
    f32 = jnp.float32
    L = DEPTH
    nrm = lambda k, shape, scale: jax.random.normal(k, shape, f32) * scale
    x = jax.random.normal(ks[0], (BATCH, SEQ, D_MODEL), f32)
    w_in = nrm(ks[1], (L, D_MODEL, D_IN), D_MODEL ** -0.5)
    w_in = w_in.at[:, :, V_START:V_START + D_KV].multiply(BETA)
    b_gate = nrm(ks[2], (L, 2 * D_MODEL), 0.1)
    rnn_conv_w = nrm(ks[3], (L, RNN_CONV_W, D_RNN), RNN_CONV_W ** -0.5)
    rnn_conv_b = nrm(ks[4], (L, D_RNN), 0.02)
    lru_wa = nrm(ks[5], (L, N_RNN_BLOCKS, RNN_BLOCK_W, RNN_BLOCK_W), RNN_BLOCK_W ** -0.5)
    lru_ba = nrm(ks[6], (L, D_RNN), 0.02)
    lru_wi = nrm(ks[7], (L, N_RNN_BLOCKS, RNN_BLOCK_W, RNN_BLOCK_W), RNN_BLOCK_W ** -0.5)
    lru_bi = nrm(ks[8], (L, D_RNN), 0.02)
    a_c = jax.random.uniform(ks[9], (L, D_RNN), f32, minval=0.9, maxval=0.999)
    a0 = a_c ** (1.0 / LRU_C)
    lru_lambda = jnp.log(a0) - jnp.log1p(-a0)
    attn_sinks = nrm(ks[10], (L, N_Q_HEADS), 0.5)
    w_attn_proj = nrm(ks[11], (L, D_ATTN, D_MODEL), BETA * D_ATTN ** -0.5)
    w_rnn_proj = nrm(ks[12], (L, D_RNN, D_MODEL), BETA * D_RNN ** -0.5)
    w_out = nrm(ks[13], (L, D_MODEL, D_MODEL), BETA * D_MODEL ** -0.5)
    ln1_g = 1.0 + nrm(ks[14], (L, D_MODEL), 0.02)
    ln1_b = nrm(ks[15], (L, D_MODEL), 0.02)
    ffn_w_up = nrm(ks[16], (L, D_MODEL, D_FF), D_MODEL ** -0.5)
    ffn_w_gate = nrm(ks[17], (L, D_MODEL, D_FF), D_MODEL ** -0.5)
    ffn_conv_w = nrm(ks[18], (L, FFN_CONV_W, D_FF), FFN_CONV_W ** -0.5)
    ffn_conv_b = nrm(ks[19], (L, D_FF), 0.02)
    ffn_w_down = nrm(ks[20], (L, D_FF, D_MODEL), BETA * D_FF ** -0.5)
    ln2_g = 1.0 + nrm(ks[21], (L, D_MODEL), 0.02)
    ln2_b = nrm(ks[22], (L, D_MODEL), 0.02)
    return {"x": x, "w_in": w_in, "b_gate": b_gate, "rnn_conv_w": rnn_conv_w,
            "rnn_conv_b": rnn_conv_b, "lru_wa": lru_wa, "lru_ba": lru_ba, "lru_wi": lru_wi,
            "lru_bi": lru_bi, "lru_lambda": lru_lambda, "attn_sinks": attn_sinks,
            "w_attn_proj": w_attn_proj, "w_rnn_proj": w_rnn_proj, "w_out": w_out,
            "ln1_g": ln1_g, "ln1_b": ln1_b, "ffn_w_up": ffn_w_up, "ffn_w_gate": ffn_w_gate,
            "ffn_conv_w": ffn_conv_w, "ffn_conv_b": ffn_conv_b, "ffn_w_down": ffn_w_down,
            "ln2_g": ln2_g, "ln2_b": ln2_b}


def _fwd_reference(x, w_in, b_gate, rnn_conv_w, rnn_conv_b, lru_wa, lru_ba, lru_wi, lru_bi,
              lru_lambda, attn_sinks, w_attn_proj, w_rnn_proj, w_out, ln1_g, ln1_b,
              ffn_w_up, ffn_w_gate, ffn_conv_w, ffn_conv_b, ffn_w_down, ln2_g, ln2_b):
    bsz, s, _ = x.shape
    for l in range(DEPTH):
        proj = x @ w_in[l]
        q, k, v, rx, ry, gl = jnp.split(proj, SPLIT_POINTS, axis=-1)
        q = q.reshape(bsz, s, N_Q_HEADS, HEAD_DIM)
        k = k.reshape(bsz, s, N_KV_HEADS, HEAD_DIM)
        v = v.reshape(bsz, s, N_KV_HEADS, HEAD_DIM)
        y_attn = sliding_window_attention(q, k, v, attn_sinks[l]) @ w_attn_proj[l]
        rx = causal_dwconv(rx, rnn_conv_w[l], rnn_conv_b[l])
        hr = rg_lru(rx, lru_wa[l], lru_ba[l], lru_wi[l], lru_bi[l], lru_lambda[l])
        y_rnn = (hr * jax.nn.gelu(ry, approximate=True)) @ w_rnn_proj[l]
        g_attn, g_rnn = jnp.split(jax.nn.sigmoid(gl + b_gate[l]), 2, axis=-1)
        mix = (g_attn * y_attn + g_rnn * y_rnn) @ w_out[l]
        x = layer_norm(ALPHA * x + mix, ln1_g[l], ln1_b[l])
        up = x @ ffn_w_up[l]
        gate = causal_dwconv(x @ ffn_w_gate[l], ffn_conv_w[l], ffn_conv_b[l])
        f = (jax.nn.gelu(gate, approximate=True) * up) @ ffn_w_down[l]
        x = layer_norm(ALPHA * x + f, ln2_g[l], ln2_b[l])
    return x


import jax as _jax
import jax.numpy as _jnp

TWIN_FORMAT = 'train_step'
FWD_PARAMS = ['x', 'w_in', 'b_gate', 'rnn_conv_w', 'rnn_conv_b', 'lru_wa', 'lru_ba', 'lru_wi', 'lru_bi', 'lru_lambda', 'attn_sinks', 'w_attn_proj', 'w_rnn_proj', 'w_out', 'ln1_g', 'ln1_b', 'ffn_w_up', 'ffn_w_gate', 'ffn_conv_w', 'ffn_conv_b', 'ffn_w_down', 'ln2_g', 'ln2_b']
TWIN_WEIGHTS = ['w_in', 'b_gate', 'rnn_conv_w', 'rnn_conv_b', 'lru_wa', 'lru_ba', 'lru_wi', 'lru_bi', 'lru_lambda', 'attn_sinks', 'w_attn_proj', 'w_rnn_proj', 'w_out', 'ln1_g', 'ln1_b', 'ffn_w_up', 'ffn_w_gate', 'ffn_conv_w', 'ffn_conv_b', 'ffn_w_down', 'ln2_g', 'ln2_b']
TWIN_DIFF_INPUT = 'x'
TWIN_INPUTS = ['x', 'w_in', 'b_gate', 'rnn_conv_w', 'rnn_conv_b', 'lru_wa', 'lru_ba', 'lru_wi', 'lru_bi', 'lru_lambda', 'attn_sinks', 'w_attn_proj', 'w_rnn_proj', 'w_out', 'ln1_g', 'ln1_b', 'ffn_w_up', 'ffn_w_gate', 'ffn_conv_w', 'ffn_conv_b', 'ffn_w_down', 'ln2_g', 'ln2_b', 'loss_target', 'm_w_in', 'm_b_gate', 'm_rnn_conv_w', 'm_rnn_conv_b', 'm_lru_wa', 'm_lru_ba', 'm_lru_wi', 'm_lru_bi', 'm_lru_lambda', 'm_attn_sinks', 'm_w_attn_proj', 'm_w_rnn_proj', 'm_w_out', 'm_ln1_g', 'm_ln1_b', 'm_ffn_w_up', 'm_ffn_w_gate', 'm_ffn_conv_w', 'm_ffn_conv_b', 'm_ffn_w_down', 'm_ln2_g', 'm_ln2_b', 'v_w_in', 'v_b_gate', 'v_rnn_conv_w', 'v_rnn_conv_b', 'v_lru_wa', 'v_lru_ba', 'v_lru_wi', 'v_lru_bi', 'v_lru_lambda', 'v_attn_sinks', 'v_w_attn_proj', 'v_w_rnn_proj', 'v_w_out', 'v_ln1_g', 'v_ln1_b', 'v_ffn_w_up', 'v_ffn_w_gate', 'v_ffn_conv_w', 'v_ffn_conv_b', 'v_ffn_w_down', 'v_ln2_g', 'v_ln2_b']
TWIN_OUTPUTS = ['loss', 'grad_x', 'grad_w_in', 'grad_b_gate', 'grad_rnn_conv_w', 'grad_rnn_conv_b', 'grad_lru_wa', 'grad_lru_ba', 'grad_lru_wi', 'grad_lru_bi', 'grad_lru_lambda', 'grad_attn_sinks', 'grad_w_attn_proj', 'grad_w_rnn_proj', 'grad_w_out', 'grad_ln1_g', 'grad_ln1_b', 'grad_ffn_w_up', 'grad_ffn_w_gate', 'grad_ffn_conv_w', 'grad_ffn_conv_b', 'grad_ffn_w_down', 'grad_ln2_g', 'grad_ln2_b', 'delta_w_in', 'delta_b_gate', 'delta_rnn_conv_w', 'delta_rnn_conv_b', 'delta_lru_wa', 'delta_lru_ba', 'delta_lru_wi', 'delta_lru_bi', 'delta_lru_lambda', 'delta_attn_sinks', 'delta_w_attn_proj', 'delta_w_rnn_proj', 'delta_w_out', 'delta_ln1_g', 'delta_ln1_b', 'delta_ffn_w_up', 'delta_ffn_w_gate', 'delta_ffn_conv_w', 'delta_ffn_conv_b', 'delta_ffn_w_down', 'delta_ln2_g', 'delta_ln2_b', 'new_m_w_in', 'new_m_b_gate', 'new_m_rnn_conv_w', 'new_m_rnn_conv_b', 'new_m_lru_wa', 'new_m_lru_ba', 'new_m_lru_wi', 'new_m_lru_bi', 'new_m_lru_lambda', 'new_m_attn_sinks', 'new_m_w_attn_proj', 'new_m_w_rnn_proj', 'new_m_w_out', 'new_m_ln1_g', 'new_m_ln1_b', 'new_m_ffn_w_up', 'new_m_ffn_w_gate', 'new_m_ffn_conv_w', 'new_m_ffn_conv_b', 'new_m_ffn_w_down', 'new_m_ln2_g', 'new_m_ln2_b', 'new_v_w_in', 'new_v_b_gate', 'new_v_rnn_conv_w', 'new_v_rnn_conv_b', 'new_v_lru_wa', 'new_v_lru_ba', 'new_v_lru_wi', 'new_v_lru_bi', 'new_v_lru_lambda', 'new_v_attn_sinks', 'new_v_w_attn_proj', 'new_v_w_rnn_proj', 'new_v_w_out', 'new_v_ln1_g', 'new_v_ln1_b', 'new_v_ffn_w_up', 'new_v_ffn_w_gate', 'new_v_ffn_conv_w', 'new_v_ffn_conv_b', 'new_v_ffn_w_down', 'new_v_ln2_g', 'new_v_ln2_b']
TWIN_LEAF_KINDS = {'loss': 'loss', 'grad_x': 'grad_x', 'grad_w_in': 'grad_w', 'grad_b_gate': 'grad_w', 'grad_rnn_conv_w': 'grad_w', 'grad_rnn_conv_b': 'grad_w', 'grad_lru_wa': 'grad_w', 'grad_lru_ba': 'grad_w', 'grad_lru_wi': 'grad_w', 'grad_lru_bi': 'grad_w', 'grad_lru_lambda': 'grad_w', 'grad_attn_sinks': 'grad_w', 'grad_w_attn_proj': 'grad_w', 'grad_w_rnn_proj': 'grad_w', 'grad_w_out': 'grad_w', 'grad_ln1_g': 'grad_w', 'grad_ln1_b': 'grad_w', 'grad_ffn_w_up': 'grad_w', 'grad_ffn_w_gate': 'grad_w', 'grad_ffn_conv_w': 'grad_w', 'grad_ffn_conv_b': 'grad_w', 'grad_ffn_w_down': 'grad_w', 'grad_ln2_g': 'grad_w', 'grad_ln2_b': 'grad_w', 'delta_w_in': 'delta_w', 'delta_b_gate': 'delta_w', 'delta_rnn_conv_w': 'delta_w', 'delta_rnn_conv_b': 'delta_w', 'delta_lru_wa': 'delta_w', 'delta_lru_ba': 'delta_w', 'delta_lru_wi': 'delta_w', 'delta_lru_bi': 'delta_w', 'delta_lru_lambda': 'delta_w', 'delta_attn_sinks': 'delta_w', 'delta_w_attn_proj': 'delta_w', 'delta_w_rnn_proj': 'delta_w', 'delta_w_out': 'delta_w', 'delta_ln1_g': 'delta_w', 'delta_ln1_b': 'delta_w', 'delta_ffn_w_up': 'delta_w', 'delta_ffn_w_gate': 'delta_w', 'delta_ffn_conv_w': 'delta_w', 'delta_ffn_conv_b': 'delta_w', 'delta_ffn_w_down': 'delta_w', 'delta_ln2_g': 'delta_w', 'delta_ln2_b': 'delta_w', 'new_m_w_in': 'new_m', 'new_m_b_gate': 'new_m', 'new_m_rnn_conv_w': 'new_m', 'new_m_rnn_conv_b': 'new_m', 'new_m_lru_wa': 'new_m', 'new_m_lru_ba': 'new_m', 'new_m_lru_wi': 'new_m', 'new_m_lru_bi': 'new_m', 'new_m_lru_lambda': 'new_m', 'new_m_attn_sinks': 'new_m', 'new_m_w_attn_proj': 'new_m', 'new_m_w_rnn_proj': 'new_m', 'new_m_w_out': 'new_m', 'new_m_ln1_g': 'new_m', 'new_m_ln1_b': 'new_m', 'new_m_ffn_w_up': 'new_m', 'new_m_ffn_w_gate': 'new_m', 'new_m_ffn_conv_w': 'new_m', 'new_m_ffn_conv_b': 'new_m', 'new_m_ffn_w_down': 'new_m', 'new_m_ln2_g': 'new_m', 'new_m_ln2_b': 'new_m', 'new_v_w_in': 'new_v', 'new_v_b_gate': 'new_v', 'new_v_rnn_conv_w': 'new_v', 'new_v_rnn_conv_b': 'new_v', 'new_v_lru_wa': 'new_v', 'new_v_lru_ba': 'new_v', 'new_v_lru_wi': 'new_v', 'new_v_lru_bi': 'new_v', 'new_v_lru_lambda': 'new_v', 'new_v_attn_sinks': 'new_v', 'new_v_w_attn_proj': 'new_v', 'new_v_w_rnn_proj': 'new_v', 'new_v_w_out': 'new_v', 'new_v_ln1_g': 'new_v', 'new_v_ln1_b': 'new_v', 'new_v_ffn_w_up': 'new_v', 'new_v_ffn_w_gate': 'new_v', 'new_v_ffn_conv_w': 'new_v', 'new_v_ffn_conv_b': 'new_v', 'new_v_ffn_w_down': 'new_v', 'new_v_ln2_g': 'new_v', 'new_v_ln2_b': 'new_v'}


def _forward(args):
    return _fwd_reference(*[args[k] for k in FWD_PARAMS])


def _output_shape():
    out = _jax.eval_shape(lambda: _forward(_fwd_setup_inputs(0)))
    return out.shape, out.dtype

N_MICROBATCH = 1
ADAM_LR = 0.001
ADAM_B1 = 0.9
ADAM_B2 = 0.999
ADAM_EPS = 1e-08
ADAM_WD = 0.01
ADAM_STEP = 10
PER_EXAMPLE_BATCH_AXIS = {'x': 0, 'loss_target': 0}
SHARED_INPUTS = []
_WEIGHT_DTYPES = {'w_in': _jnp.float32, 'b_gate': _jnp.float32, 'rnn_conv_w': _jnp.float32, 'rnn_conv_b': _jnp.float32, 'lru_wa': _jnp.float32, 'lru_ba': _jnp.float32, 'lru_wi': _jnp.float32, 'lru_bi': _jnp.float32, 'lru_lambda': _jnp.float32, 'attn_sinks': _jnp.float32, 'w_attn_proj': _jnp.float32, 'w_rnn_proj': _jnp.float32, 'w_out': _jnp.float32, 'ln1_g': _jnp.float32, 'ln1_b': _jnp.float32, 'ffn_w_up': _jnp.float32, 'ffn_w_gate': _jnp.float32, 'ffn_conv_w': _jnp.float32, 'ffn_conv_b': _jnp.float32, 'ffn_w_down': _jnp.float32, 'ln2_g': _jnp.float32, 'ln2_b': _jnp.float32}
MOMENT_SCALE = {'w_in': 3.139538e-03, 'b_gate': 1.367818e-03, 'rnn_conv_w': 4.183166e-03, 'rnn_conv_b': 5.210235e-02, 'lru_wa': 1.378698e-03, 'lru_ba': 1.077610e-03, 'lru_wi': 2.460497e-03, 'lru_bi': 1.498425e-03, 'lru_lambda': 2.055272e-03, 'attn_sinks': 3.290671e-03, 'w_attn_proj': 3.367660e-03, 'w_rnn_proj': 7.975745e-03, 'w_out': 8.471690e-03, 'ln1_g': 2.690051e-01, 'ln1_b': 1.394262e-01, 'ffn_w_up': 1.124118e-02, 'ffn_w_gate': 1.161844e-02, 'ffn_conv_w': 1.167449e-02, 'ffn_conv_b': 1.128078e-02, 'ffn_w_down': 3.275929e-02, 'ln2_g': 8.005737e+00, 'ln2_b': 2.178613e-01}


def _to_microbatches(a, axis):
    t = _jnp.moveaxis(a, axis, 0)
    t = t.reshape((N_MICROBATCH, t.shape[0] // N_MICROBATCH) + t.shape[1:])
    return _jnp.moveaxis(t, 1, axis + 1)


def setup_inputs(seed: int = 0) -> dict:
    inp = _fwd_setup_inputs(seed)
    key = _jax.random.fold_in(_jax.random.key(seed), 7919)
    shape, _ = _output_shape()
    out = dict(inp)
    out["loss_target"] = _jax.random.normal(_jax.random.fold_in(key, 0), shape, _jnp.float32)
    for i, name in enumerate(TWIN_WEIGHTS):
        w = inp[name].astype(_jnp.float32)
        if MOMENT_SCALE is None:
            s = _jnp.sqrt(_jnp.mean(_jnp.square(w)) + 1e-30)
        else:
            s = MOMENT_SCALE[name]
        km, kv = _jax.random.split(_jax.random.fold_in(key, i + 1))
        out[name] = w
        out["m_" + name] = s * _jax.random.normal(km, w.shape, _jnp.float32)
        out["v_" + name] = (s * s) * _jax.random.uniform(kv, w.shape, _jnp.float32, 0.5, 1.5)
    if N_MICROBATCH > 1:
        for name, axis in PER_EXAMPLE_BATCH_AXIS.items():
            out[name] = _to_microbatches(out[name], axis)
    return {'x': out['x'], 'w_in': out['w_in'], 'b_gate': out['b_gate'], 'rnn_conv_w': out['rnn_conv_w'], 'rnn_conv_b': out['rnn_conv_b'], 'lru_wa': out['lru_wa'], 'lru_ba': out['lru_ba'], 'lru_wi': out['lru_wi'], 'lru_bi': out['lru_bi'], 'lru_lambda': out['lru_lambda'], 'attn_sinks': out['attn_sinks'], 'w_attn_proj': out['w_attn_proj'], 'w_rnn_proj': out['w_rnn_proj'], 'w_out': out['w_out'], 'ln1_g': out['ln1_g'], 'ln1_b': out['ln1_b'], 'ffn_w_up': out['ffn_w_up'], 'ffn_w_gate': out['ffn_w_gate'], 'ffn_conv_w': out['ffn_conv_w'], 'ffn_conv_b': out['ffn_conv_b'], 'ffn_w_down': out['ffn_w_down'], 'ln2_g': out['ln2_g'], 'ln2_b': out['ln2_b'], 'loss_target': out['loss_target'], 'm_w_in': out['m_w_in'], 'm_b_gate': out['m_b_gate'], 'm_rnn_conv_w': out['m_rnn_conv_w'], 'm_rnn_conv_b': out['m_rnn_conv_b'], 'm_lru_wa': out['m_lru_wa'], 'm_lru_ba': out['m_lru_ba'], 'm_lru_wi': out['m_lru_wi'], 'm_lru_bi': out['m_lru_bi'], 'm_lru_lambda': out['m_lru_lambda'], 'm_attn_sinks': out['m_attn_sinks'], 'm_w_attn_proj': out['m_w_attn_proj'], 'm_w_rnn_proj': out['m_w_rnn_proj'], 'm_w_out': out['m_w_out'], 'm_ln1_g': out['m_ln1_g'], 'm_ln1_b': out['m_ln1_b'], 'm_ffn_w_up': out['m_ffn_w_up'], 'm_ffn_w_gate': out['m_ffn_w_gate'], 'm_ffn_conv_w': out['m_ffn_conv_w'], 'm_ffn_conv_b': out['m_ffn_conv_b'], 'm_ffn_w_down': out['m_ffn_w_down'], 'm_ln2_g': out['m_ln2_g'], 'm_ln2_b': out['m_ln2_b'], 'v_w_in': out['v_w_in'], 'v_b_gate': out['v_b_gate'], 'v_rnn_conv_w': out['v_rnn_conv_w'], 'v_rnn_conv_b': out['v_rnn_conv_b'], 'v_lru_wa': out['v_lru_wa'], 'v_lru_ba': out['v_lru_ba'], 'v_lru_wi': out['v_lru_wi'], 'v_lru_bi': out['v_lru_bi'], 'v_lru_lambda': out['v_lru_lambda'], 'v_attn_sinks': out['v_attn_sinks'], 'v_w_attn_proj': out['v_w_attn_proj'], 'v_w_rnn_proj': out['v_w_rnn_proj'], 'v_w_out': out['v_w_out'], 'v_ln1_g': out['v_ln1_g'], 'v_ln1_b': out['v_ln1_b'], 'v_ffn_w_up': out['v_ffn_w_up'], 'v_ffn_w_gate': out['v_ffn_w_gate'], 'v_ffn_conv_w': out['v_ffn_conv_w'], 'v_ffn_conv_b': out['v_ffn_conv_b'], 'v_ffn_w_down': out['v_ffn_w_down'], 'v_ln2_g': out['v_ln2_g'], 'v_ln2_b': out['v_ln2_b']}


def _loss(weights, diff, rest, loss_target):
    with _jax.named_scope("forward"):
        args = {**rest, TWIN_DIFF_INPUT: diff, **{k: w.astype(_WEIGHT_DTYPES[k]) for k, w in weights.items()}}
        y = _forward(args)
    with _jax.named_scope("loss_head"):
        err = _jnp.square(y.astype(_jnp.float32) - loss_target)
        return 0.5 * _jnp.sum(_jnp.mean(err, axis=-1)) if err.ndim else 0.5 * err


def _adamw(w, g, m, v):
    m = ADAM_B1 * m + (1.0 - ADAM_B1) * g
    v = ADAM_B2 * v + (1.0 - ADAM_B2) * _jnp.square(g)
    m_hat = m / (1.0 - ADAM_B1 ** ADAM_STEP)
    v_hat = v / (1.0 - ADAM_B2 ** ADAM_STEP)
    delta = -ADAM_LR * (m_hat / (_jnp.sqrt(v_hat) + ADAM_EPS) + ADAM_WD * w)
    return delta, m, v


def reference(x, w_in, b_gate, rnn_conv_w, rnn_conv_b, lru_wa, lru_ba, lru_wi, lru_bi, lru_lambda, attn_sinks, w_attn_proj, w_rnn_proj, w_out, ln1_g, ln1_b, ffn_w_up, ffn_w_gate, ffn_conv_w, ffn_conv_b, ffn_w_down, ln2_g, ln2_b, loss_target, m_w_in, m_b_gate, m_rnn_conv_w, m_rnn_conv_b, m_lru_wa, m_lru_ba, m_lru_wi, m_lru_bi, m_lru_lambda, m_attn_sinks, m_w_attn_proj, m_w_rnn_proj, m_w_out, m_ln1_g, m_ln1_b, m_ffn_w_up, m_ffn_w_gate, m_ffn_conv_w, m_ffn_conv_b, m_ffn_w_down, m_ln2_g, m_ln2_b, v_w_in, v_b_gate, v_rnn_conv_w, v_rnn_conv_b, v_lru_wa, v_lru_ba, v_lru_wi, v_lru_bi, v_lru_lambda, v_attn_sinks, v_w_attn_proj, v_w_rnn_proj, v_w_out, v_ln1_g, v_ln1_b, v_ffn_w_up, v_ffn_w_gate, v_ffn_conv_w, v_ffn_conv_b, v_ffn_w_down, v_ln2_g, v_ln2_b):
    given = dict(x=x, w_in=w_in, b_gate=b_gate, rnn_conv_w=rnn_conv_w, rnn_conv_b=rnn_conv_b, lru_wa=lru_wa, lru_ba=lru_ba, lru_wi=lru_wi, lru_bi=lru_bi, lru_lambda=lru_lambda, attn_sinks=attn_sinks, w_attn_proj=w_attn_proj, w_rnn_proj=w_rnn_proj, w_out=w_out, ln1_g=ln1_g, ln1_b=ln1_b, ffn_w_up=ffn_w_up, ffn_w_gate=ffn_w_gate, ffn_conv_w=ffn_conv_w, ffn_conv_b=ffn_conv_b, ffn_w_down=ffn_w_down, ln2_g=ln2_g, ln2_b=ln2_b, loss_target=loss_target, m_w_in=m_w_in, m_b_gate=m_b_gate, m_rnn_conv_w=m_rnn_conv_w, m_rnn_conv_b=m_rnn_conv_b, m_lru_wa=m_lru_wa, m_lru_ba=m_lru_ba, m_lru_wi=m_lru_wi, m_lru_bi=m_lru_bi, m_lru_lambda=m_lru_lambda, m_attn_sinks=m_attn_sinks, m_w_attn_proj=m_w_attn_proj, m_w_rnn_proj=m_w_rnn_proj, m_w_out=m_w_out, m_ln1_g=m_ln1_g, m_ln1_b=m_ln1_b, m_ffn_w_up=m_ffn_w_up, m_ffn_w_gate=m_ffn_w_gate, m_ffn_conv_w=m_ffn_conv_w, m_ffn_conv_b=m_ffn_conv_b, m_ffn_w_down=m_ffn_w_down, m_ln2_g=m_ln2_g, m_ln2_b=m_ln2_b, v_w_in=v_w_in, v_b_gate=v_b_gate, v_rnn_conv_w=v_rnn_conv_w, v_rnn_conv_b=v_rnn_conv_b, v_lru_wa=v_lru_wa, v_lru_ba=v_lru_ba, v_lru_wi=v_lru_wi, v_lru_bi=v_lru_bi, v_lru_lambda=v_lru_lambda, v_attn_sinks=v_attn_sinks, v_w_attn_proj=v_w_attn_proj, v_w_rnn_proj=v_w_rnn_proj, v_w_out=v_w_out, v_ln1_g=v_ln1_g, v_ln1_b=v_ln1_b, v_ffn_w_up=v_ffn_w_up, v_ffn_w_gate=v_ffn_w_gate, v_ffn_conv_w=v_ffn_conv_w, v_ffn_conv_b=v_ffn_conv_b, v_ffn_w_down=v_ffn_w_down, v_ln2_g=v_ln2_g, v_ln2_b=v_ln2_b)
    weights = {n: given[n] for n in TWIN_WEIGHTS}
    shared = {n: given[n] for n in SHARED_INPUTS}
    per_example = {n: given[n] for n in ['x']}
    grad_fn = _jax.value_and_grad(_loss, argnums=(0, 1))

    def one_microbatch(ex, loss_target):
        ex = dict(ex)
        diff = ex.pop(TWIN_DIFF_INPUT)
        return grad_fn(weights, diff, {**shared, **ex}, loss_target)

    if N_MICROBATCH == 1:
        loss, (grad_w, grad_x) = one_microbatch(per_example, given["loss_target"])
    else:
        def body(carry, xs):
            loss_sum, grad_sum = carry
            l_k, (gw_k, gx_k) = one_microbatch(xs[0], xs[1])
            with _jax.named_scope("update"):
                return (loss_sum + l_k, _jax.tree.map(_jnp.add, grad_sum, gw_k)), gx_k

        init = (_jnp.zeros((), _jnp.float32), _jax.tree.map(_jnp.zeros_like, weights))
        (loss, grad_w), grad_x = _jax.lax.scan(body, init, (per_example, given["loss_target"]))
    with _jax.named_scope("update"):
        delta_w, new_m, new_v = {}, {}, {}
        for n in TWIN_WEIGHTS:
            delta_w[n], new_m[n], new_v[n] = _adamw(weights[n], grad_w[n], given["m_" + n], given["v_" + n])
    return (loss, grad_x, *[grad_w[n] for n in TWIN_WEIGHTS], *[delta_w[n] for n in TWIN_WEIGHTS],
            *[new_m[n] for n in TWIN_WEIGHTS], *[new_v[n] for n in TWIN_WEIGHTS])
```

```python
import math

import jax
import jax.numpy as jnp
from jax import lax
from jax.experimental import pallas as pl
from jax.experimental.pallas import tpu as pltpu

F32 = jnp.float32
BF16 = jnp.bfloat16
MXU_DTYPE = jnp.bfloat16

N_DEV = 8
S = 2048
D = 2048
HEAD_DIM = 64
N_KV = 4
GROUP = 8
BLOCK = 128
D_KV = N_KV * HEAD_DIM
D_RNN = 2560
RNN_GROUP = 640
N_RNN_GROUPS = D_RNN // RNN_GROUP
RNN_BLOCK_W = 160
RNN_CONV_W = 4
LRU_C = 8.0
D_FF = 6144
FFN_CONV_W = 3
D_IN = 11776
OFF_K = 2048
OFF_V = 2304
OFF_RX = 2560
OFF_RY = 5120
OFF_GA = 7680
OFF_GR = 9728
LN_EPS = 1e-5
ALPHA = 2.0 ** 0.25
ADAM_LR = 0.001
ADAM_B1 = 0.9
ADAM_B2 = 0.999
ADAM_EPS = 1e-08
ADAM_WD = 0.01
ADAM_STEP = 10
NEG = -1e30
VMEM_LIMIT = 56 * 1024 * 1024
MESH = pl.DeviceIdType.MESH
GELU_C = math.sqrt(2.0 / math.pi)
FLAT_TILE = 1024


def _cparams(*sem):
    return pltpu.CompilerParams(dimension_semantics=sem or None, vmem_limit_bytes=VMEM_LIMIT)


def _gelu(x):
    x2 = x * x
    t = jnp.tanh(GELU_C * (x + 0.044715 * x * x2))
    g = 0.5 * x * (1.0 + t)
    dg = 0.5 * (1.0 + t) + 0.5 * x * (1.0 - t * t) * (GELU_C * (1.0 + 3.0 * 0.044715 * x2))
    return g, dg


def _sigmoid(x):
    return 1.0 / (1.0 + jnp.exp(-x))


def _softplus(x):
    z = jnp.exp(-jnp.abs(x))
    small = z * (1.0 - z * (0.5 - z * (1.0 / 3.0 - 0.25 * z)))
    return jnp.maximum(x, 0.0) + jnp.where(z < 0.02, small, jnp.log(1.0 + z))


def _one_minus_exp(x):
    series = -x * (1.0 + x * (0.5 + x * (1.0 / 6.0 + x * (1.0 / 24.0))))
    return jnp.where(x > -0.03, series, 1.0 - jnp.exp(x))


def _colsum(v):
    return jnp.sum(v, axis=0, keepdims=True)


def _mm(a, b, *, tm, tn, tk, name, ta=False, tb=False, out_dtype=F32, b_block=None, add=None, add_scale=1.0):
    if ta:
        k_dim, m_dim = a.shape
    else:
        m_dim, k_dim = a.shape
    if b_block is None:
        n_dim = b.shape[0] if tb else b.shape[1]
    else:
        n_dim = b.shape[1] if tb else b.shape[0] * b_block
    assert m_dim % tm == 0 and n_dim % tn == 0 and k_dim % tk == 0, (name, m_dim, n_dim, k_dim)
    nk = k_dim // tk
    dims = (((0 if ta else 1,), (1 if tb else 0,)), ((), ()))
    has_add = add is not None

    def body(*refs):
        a_ref, b_ref = refs[0], refs[1]
        add_ref = refs[2] if has_add else None
        o_ref = refs[3 if has_add else 2]

        def product():
            return lax.dot_general(a_ref[...].astype(MXU_DTYPE), b_ref[...].astype(MXU_DTYPE), dims,
                                   preferred_element_type=F32)

        def finish(acc):
            if has_add:
                acc = acc + add_scale * add_ref[...]
            o_ref[...] = acc.astype(out_dtype)

        if nk == 1:
            finish(product())
        else:
            acc_ref = refs[-1]
            k = pl.program_id(2)

            @pl.when(k == 0)
            def _():
                acc_ref[...] = jnp.zeros_like(acc_ref)

            acc_ref[...] += product()

            @pl.when(k == nk - 1)
            def _():
                finish(acc_ref[...])

    if ta:
        a_spec = pl.BlockSpec((tk, tm), lambda i, j, k: (k, i))
    else:
        a_spec = pl.BlockSpec((tm, tk), lambda i, j, k: (i, k))
    if b_block is None:
        if tb:
            b_spec = pl.BlockSpec((tn, tk), lambda i, j, k: (j, k))
        else:
            b_spec = pl.BlockSpec((tk, tn), lambda i, j, k: (k, j))
    elif tb:
        assert b_block % tk == 0
        b_spec = pl.BlockSpec((None, tn, tk), lambda i, j, k: ((k * tk) // b_block, j, ((k * tk) % b_block) // tk))
    else:
        assert b_block % tn == 0
        b_spec = pl.BlockSpec((None, tk, tn), lambda i, j, k: ((j * tn) // b_block, k, ((j * tn) % b_block) // tn))
    in_specs = [a_spec, b_spec]
    operands = [a, b]
    if has_add:
        in_specs.append(pl.BlockSpec((tm, tn), lambda i, j, k: (i, j)))
        operands.append(add)
    return pl.pallas_call(
        body,
        name=name,
        grid=(m_dim // tm, n_dim // tn, nk),
        in_specs=in_specs,
        out_specs=pl.BlockSpec((tm, tn), lambda i, j, k: (i, j)),
        out_shape=jax.ShapeDtypeStruct((m_dim, n_dim), out_dtype),
        scratch_shapes=[pltpu.VMEM((tm, tn), F32)] if nk > 1 else [],
        compiler_params=_cparams("parallel", "parallel", "arbitrary"),
    )(*operands)


def _attn_bias(h):
    row = lax.broadcasted_iota(jnp.int32, (GROUP * BLOCK, 2 * BLOCK), 0)
    col = lax.broadcasted_iota(jnp.int32, (GROUP * BLOCK, 2 * BLOCK), 1)
    dist = BLOCK + (row & (BLOCK - 1)) - col
    head = h * GROUP + (row >> 7) + 1
    slope = jnp.exp(head.astype(F32) * (-0.25 * math.log(2.0)))
    return jnp.where((dist >= 0) & (dist < BLOCK), -slope * dist.astype(F32), NEG)


def _attn_probs(qn, kb, bias, sink, first_block):
    s = lax.dot_general(qn, kb, (((1,), (1,)), ((), ())), preferred_element_type=F32) * (HEAD_DIM ** -0.5) + bias
    col = lax.broadcasted_iota(jnp.int32, s.shape, 1)
    s = jnp.where(first_block & (col < BLOCK), NEG, s)
    m = jnp.maximum(jnp.max(s, axis=-1, keepdims=True), sink)
    e = jnp.exp(s - m)
    e_sink = jnp.exp(sink - m)
    inv = 1.0 / (jnp.sum(e, axis=-1, keepdims=True) + e_sink)
    return e * inv, e_sink * inv


def _attn_fwd(q, kp, vp, sink_rows):
    rows = GROUP * BLOCK

    def body(q_ref, k_ref, v_ref, sink_ref, o_ref, bias_ref):
        bias_ref[...] = _attn_bias(pl.program_id(0))
        sink = sink_ref[...]

        def step(n, carry):
            r0 = pl.multiple_of(n * BLOCK, BLOCK)
            qn = q_ref[:, pl.ds(r0, BLOCK), :].reshape(rows, HEAD_DIM)
            kb = k_ref[pl.ds(r0, 2 * BLOCK), :]
            vb = v_ref[pl.ds(r0, 2 * BLOCK), :]
            p, _ = _attn_probs(qn, kb, bias_ref[...], sink, n == 0)
            o = jnp.dot(p.astype(MXU_DTYPE), vb, preferred_element_type=F32)
            o_ref[:, pl.ds(r0, BLOCK), :] = o.reshape(GROUP, BLOCK, HEAD_DIM).astype(o_ref.dtype)
            return carry

        lax.fori_loop(0, S // BLOCK, step, 0)

    return pl.pallas_call(
        body,
        name="attn_fwd",
        grid=(N_KV,),
        in_specs=[
            pl.BlockSpec((None, GROUP, S, HEAD_DIM), lambda h: (h, 0, 0, 0)),
            pl.BlockSpec((None, BLOCK + S, HEAD_DIM), lambda h: (h, 0, 0)),
            pl.BlockSpec((None, BLOCK + S, HEAD_DIM), lambda h: (h, 0, 0)),
            pl.BlockSpec((None, rows, 1), lambda h: (h, 0, 0)),
        ],
        out_specs=pl.BlockSpec((None, GROUP, S, HEAD_DIM), lambda h: (h, 0, 0, 0)),
        out_shape=jax.ShapeDtypeStruct((N_KV, GROUP, S, HEAD_DIM), MXU_DTYPE),
        scratch_shapes=[pltpu.VMEM((rows, 2 * BLOCK), F32)],
        compiler_params=_cparams("parallel"),
    )(q, kp, vp, sink_rows)


def _attn_bwd(q, kp, vp, sink_rows, do):
    rows = GROUP * BLOCK

    def body(q_ref, k_ref, v_ref, sink_ref, do_ref, dq_ref, dk_ref, dv_ref, ds_ref, bias_ref):
        bias_ref[...] = _attn_bias(pl.program_id(0))
        sink = sink_ref[...]
        dk_ref[...] = jnp.zeros_like(dk_ref)
        dv_ref[...] = jnp.zeros_like(dv_ref)

        def step(n, dsink):
            r0 = pl.multiple_of(n * BLOCK, BLOCK)
            qn = q_ref[:, pl.ds(r0, BLOCK), :].reshape(rows, HEAD_DIM)
            don = do_ref[:, pl.ds(r0, BLOCK), :].reshape(rows, HEAD_DIM)
            kb = k_ref[pl.ds(r0, 2 * BLOCK), :]
            vb = v_ref[pl.ds(r0, 2 * BLOCK), :]
            p, p_sink = _attn_probs(qn, kb, bias_ref[...], sink, n == 0)
            dp = lax.dot_general(don, vb, (((1,), (1,)), ((), ())), preferred_element_type=F32)
            delta = jnp.sum(p * dp, axis=-1, keepdims=True)
            ds = (p * (dp - delta) * (HEAD_DIM ** -0.5)).astype(MXU_DTYPE)
            dq = jnp.dot(ds, kb, preferred_element_type=F32)
            dq_ref[:, pl.ds(r0, BLOCK), :] = dq.reshape(GROUP, BLOCK, HEAD_DIM).astype(dq_ref.dtype)
            dk_ref[pl.ds(r0, 2 * BLOCK), :] += lax.dot_general(ds, qn, (((0,), (0,)), ((), ())),
                                                              preferred_element_type=F32)
            dv_ref[pl.ds(r0, 2 * BLOCK), :] += lax.dot_general(p.astype(MXU_DTYPE), don, (((0,), (0,)), ((), ())),
                                                              preferred_element_type=F32)
            return dsink - p_sink * delta

        dsink = lax.fori_loop(0, S // BLOCK, step, jnp.zeros((rows, 1), F32))
        for g in range(GROUP):
            ds_ref[g:g + 1, :] = _colsum(dsink[g * BLOCK:(g + 1) * BLOCK, :])

    hm = pl.BlockSpec((None, GROUP, S, HEAD_DIM), lambda h: (h, 0, 0, 0))
    kv = pl.BlockSpec((None, BLOCK + S, HEAD_DIM), lambda h: (h, 0, 0))
    return pl.pallas_call(
        body,
        name="attn_bwd",
        grid=(N_KV,),
        in_specs=[hm, kv, kv, pl.BlockSpec((None, rows, 1), lambda h: (h, 0, 0)), hm],
        out_specs=[hm, kv, kv, pl.BlockSpec((None, GROUP, 1), lambda h: (h, 0, 0))],
        out_shape=[
            jax.ShapeDtypeStruct((N_KV, GROUP, S, HEAD_DIM), MXU_DTYPE),
            jax.ShapeDtypeStruct((N_KV, BLOCK + S, HEAD_DIM), F32),
            jax.ShapeDtypeStruct((N_KV, BLOCK + S, HEAD_DIM), F32),
            jax.ShapeDtypeStruct((N_KV, GROUP, 1), F32),
        ],
        scratch_shapes=[pltpu.VMEM((rows, 2 * BLOCK), F32)],
        compiler_params=_cparams("parallel"),
    )(q, kp, vp, sink_rows, do)


PAD = 8
CHUNK = 256


def _past_taps(xpad_ref, r0, width):
    ext = xpad_ref[pl.ds(r0, CHUNK + PAD), :]
    taps = []
    for k in range(width):
        back = width - 1 - k
        taps.append((ext if back == 0 else pltpu.roll(ext, back, 0))[PAD:, :])
    return taps


def _future_taps(xpad_ref, r0, width):
    ext = xpad_ref[pl.ds(r0, CHUNK + PAD), :]
    taps = []
    for ahead in range(width):
        taps.append((ext if ahead == 0 else pltpu.roll(ext, CHUNK + PAD - ahead, 0))[:CHUNK, :])
    return taps


def _conv_fwd(src, col0, w, b, *, tc, name):
    width, c_dim = w.shape

    def body(x_ref, w_ref, b_ref, o_ref, xpad_ref):
        xpad_ref[pl.ds(0, PAD), :] = jnp.zeros((PAD, tc), F32)
        xpad_ref[pl.ds(PAD, S), :] = x_ref[...]
        wv = w_ref[...]
        bv = b_ref[...]

        def step(ci, carry):
            r0 = pl.multiple_of(ci * CHUNK, CHUNK)
            taps = _past_taps(xpad_ref, r0, width)
            y = bv + taps[0] * wv[0:1, :]
            for k in range(1, width):
                y = y + taps[k] * wv[k:k + 1, :]
            o_ref[pl.ds(r0, CHUNK), :] = y
            return carry

        lax.fori_loop(0, S // CHUNK, step, 0)

    return pl.pallas_call(
        body,
        name=name,
        grid=(c_dim // tc,),
        in_specs=[
            pl.BlockSpec((S, tc), lambda j: (0, col0 // tc + j)),
            pl.BlockSpec((width, tc), lambda j: (0, j)),
            pl.BlockSpec((1, tc), lambda j: (0, j)),
        ],
        out_specs=pl.BlockSpec((S, tc), lambda j: (0, j)),
        out_shape=jax.ShapeDtypeStruct((S, c_dim), F32),
        scratch_shapes=[pltpu.VMEM((S + PAD, tc), F32)],
        compiler_params=_cparams("parallel"),
    )(src, w, b)


def _conv_bwd(dy, src, col0, w, *, tc, name):
    width, c_dim = w.shape

    def body(dy_ref, x_ref, w_ref, dx_ref, dw_ref, db_ref, xpad_ref, dpad_ref):
        xpad_ref[pl.ds(0, PAD), :] = jnp.zeros((PAD, tc), F32)
        xpad_ref[pl.ds(PAD, S), :] = x_ref[...]
        dpad_ref[pl.ds(0, S), :] = dy_ref[...]
        dpad_ref[pl.ds(S, PAD), :] = jnp.zeros((PAD, tc), F32)
        wv = w_ref[...]

        def step(ci, acc):
            r0 = pl.multiple_of(ci * CHUNK, CHUNK)
            past = _past_taps(xpad_ref, r0, width)
            ahead = _future_taps(dpad_ref, r0, width)
            d = ahead[0]
            dx = d * wv[width - 1:width, :]
            for j in range(1, width):
                dx = dx + ahead[j] * wv[width - 1 - j:width - j, :]
            dx_ref[pl.ds(r0, CHUNK), :] = dx.astype(dx_ref.dtype)
            return tuple(acc[k] + _colsum(past[k] * d) for k in range(width)) + (acc[width] + _colsum(d),)

        zero = jnp.zeros((1, tc), F32)
        acc = lax.fori_loop(0, S // CHUNK, step, (zero,) * (width + 1))
        for k in range(width):
            dw_ref[k:k + 1, :] = acc[k]
        db_ref[...] = acc[width]

    return pl.pallas_call(
        body,
        name=name,
        grid=(c_dim // tc,),
        in_specs=[
            pl.BlockSpec((S, tc), lambda j: (0, j)),
            pl.BlockSpec((S, tc), lambda j: (0, col0 // tc + j)),
            pl.BlockSpec((width, tc), lambda j: (0, j)),
        ],
        out_specs=[
            pl.BlockSpec((S, tc), lambda j: (0, j)),
            pl.BlockSpec((width, tc), lambda j: (0, j)),
            pl.BlockSpec((1, tc), lambda j: (0, j)),
        ],
        out_shape=[
            jax.ShapeDtypeStruct((S, c_dim), MXU_DTYPE),
            jax.ShapeDtypeStruct((width, c_dim), F32),
            jax.ShapeDtypeStruct((1, c_dim), F32),
        ],
        scratch_shapes=[pltpu.VMEM((S + PAD, tc), F32), pltpu.VMEM((S + PAD, tc), F32)],
        compiler_params=_cparams("parallel"),
    )(dy, src, w)


SCAN_TC = 256


def _lru_gates(rxc, wa, wi, ba, bi):
    tm = 512

    def body(x_ref, wa_ref, wi_ref, ba_ref, bi_ref, r_ref, i_ref):
        xv = x_ref[...].astype(MXU_DTYPE)
        r_ref[...] = _sigmoid(jnp.dot(xv, wa_ref[...].astype(MXU_DTYPE), preferred_element_type=F32) + ba_ref[...])
        i_ref[...] = _sigmoid(jnp.dot(xv, wi_ref[...].astype(MXU_DTYPE), preferred_element_type=F32) + bi_ref[...])

    x_spec = pl.BlockSpec((tm, RNN_GROUP), lambda g, i: (i, g))
    w_spec = pl.BlockSpec((None, RNN_GROUP, RNN_GROUP), lambda g, i: (g, 0, 0))
    b_spec = pl.BlockSpec((1, RNN_GROUP), lambda g, i: (0, g))
    return pl.pallas_call(
        body,
        name="lru_gates",
        grid=(N_RNN_GROUPS, S // tm),
        in_specs=[x_spec, w_spec, w_spec, b_spec, b_spec],
        out_specs=[x_spec, x_spec],
        out_shape=[jax.ShapeDtypeStruct((S, D_RNN), F32)] * 2,
        compiler_params=_cparams("parallel", "parallel"),
    )(rxc, wa, wi, ba, bi)


def _scan_down(a, u, row):
    for d in (1, 2, 4):
        a_s = jnp.where(row >= d, pltpu.roll(a, d, 0), 1.0)
        u_s = jnp.where(row >= d, pltpu.roll(u, d, 0), 0.0)
        u = a * u_s + u
        a = a * a_s
    return a, u


def _scan_up(a, u, row):
    for d in (1, 2, 4):
        a_s = jnp.where(row < 8 - d, pltpu.roll(a, 8 - d, 0), 1.0)
        u_s = jnp.where(row < 8 - d, pltpu.roll(u, 8 - d, 0), 0.0)
        u = a * u_s + u
        a = a * a_s
    return a, u


def _lru_scan_fwd(r, i, rxc, proj, lam):
    tc = SCAN_TC

    def body(r_ref, i_ref, x_ref, ry_ref, lam_ref, h_ref, y_ref):
        rate = LRU_C * _softplus(-lam_ref[...])
        row = lax.broadcasted_iota(jnp.int32, (8, tc), 0)

        def step(ci, carry):
            r0 = pl.multiple_of(ci * 16, 16)
            log_a = -rate * r_ref[pl.ds(r0, 16), :]
            a16 = jnp.exp(log_a)
            u16 = jnp.sqrt(_one_minus_exp(2.0 * log_a)) * (i_ref[pl.ds(r0, 16), :] * x_ref[pl.ds(r0, 16), :])
            hs = []
            for half in range(2):
                a_cum, h0 = _scan_down(a16[8 * half:8 * half + 8, :], u16[8 * half:8 * half + 8, :], row)
                h = a_cum * carry + h0
                carry = jnp.broadcast_to(h[7:8, :], (8, tc))
                hs.append(h)
            h16 = jnp.concatenate(hs, axis=0)
            h_ref[pl.ds(r0, 16), :] = h16
            y_ref[pl.ds(r0, 16), :] = (h16 * _gelu(ry_ref[pl.ds(r0, 16), :])[0]).astype(y_ref.dtype)
            return carry

        lax.fori_loop(0, S // 16, step, jnp.zeros((8, tc), F32))

    col = pl.BlockSpec((S, tc), lambda j: (0, j))
    return pl.pallas_call(
        body,
        name="lru_scan_fwd",
        grid=(D_RNN // tc,),
        in_specs=[col, col, col, pl.BlockSpec((S, tc), lambda j: (0, OFF_RY // tc + j)),
                  pl.BlockSpec((1, tc), lambda j: (0, j))],
        out_specs=[col, col],
        out_shape=[jax.ShapeDtypeStruct((S, D_RNN), F32), jax.ShapeDtypeStruct((S, D_RNN), MXU_DTYPE)],
        compiler_params=_cparams("parallel"),
    )(r, i, rxc, proj, lam)


def _lru_scan_bwd(dy, proj, h, r, i, rxc, lam):
    tc = SCAN_TC

    def body(dy_ref, ry_ref, h_ref, r_ref, i_ref, x_ref, lam_ref,
             dry_ref, dzr_ref, dzi_ref, dx_ref, dba_ref, dbi_ref, dlam_ref, a_ref, dh_ref, hp_ref):
        lam_v = lam_ref[...]
        rate = LRU_C * _softplus(-lam_v)
        dlam_scale = LRU_C * _sigmoid(-lam_v)
        row = lax.broadcasted_iota(jnp.int32, (8, tc), 0)
        hp_ref[pl.ds(0, PAD), :] = jnp.zeros((PAD, tc), F32)
        hp_ref[pl.ds(PAD, S), :] = h_ref[...]
        a_ref[pl.ds(S, PAD), :] = jnp.zeros((PAD, tc), F32)

        def prep(ci, carry):
            r0 = pl.multiple_of(ci * CHUNK, CHUNK)
            a_ref[pl.ds(r0, CHUNK), :] = jnp.exp(-rate * r_ref[pl.ds(r0, CHUNK), :])
            ge, dge = _gelu(ry_ref[pl.ds(r0, CHUNK), :])
            dyv = dy_ref[pl.ds(r0, CHUNK), :]
            dh_ref[pl.ds(r0, CHUNK), :] = dyv * ge
            dry_ref[pl.ds(r0, CHUNK), :] = (dyv * h_ref[pl.ds(r0, CHUNK), :] * dge).astype(dry_ref.dtype)
            return carry

        lax.fori_loop(0, S // CHUNK, prep, 0)

        def step(ci, state):
            carry, dba, dbi, dlam = state
            r0 = pl.multiple_of(S - 16 - ci * 16, 16)
            a_ext = a_ref[pl.ds(r0, 24), :]
            a_next = pltpu.roll(a_ext, 23, 0)
            h_prev = pltpu.roll(hp_ref[pl.ds(r0, 24), :], 1, 0)
            dh16 = dh_ref[pl.ds(r0, 16), :]
            gs = [None, None]
            for half in (1, 0):
                lo = 8 * half
                c_cum, g0 = _scan_up(a_next[lo:lo + 8, :], dh16[lo:lo + 8, :], row)
                g = c_cum * carry + g0
                carry = jnp.broadcast_to(g[0:1, :], (8, tc))
                gs[half] = g
            g16 = jnp.concatenate(gs, axis=0)
            a16 = a_ext[0:16, :]
            r16 = r_ref[pl.ds(r0, 16), :]
            i16 = i_ref[pl.ds(r0, 16), :]
            x16 = x_ref[pl.ds(r0, 16), :]
            a2 = a16 * a16
            sq = jnp.sqrt(_one_minus_exp(-2.0 * rate * r16))
            dx_ref[pl.ds(r0, 16), :] = g16 * sq * i16
            dzi = g16 * sq * x16 * i16 * (1.0 - i16)
            dlog_a = g16 * h_prev[8:24, :] * a16 - g16 * i16 * x16 * a2 / sq
            dzr = -rate * dlog_a * r16 * (1.0 - r16)
            dzr_ref[pl.ds(r0, 16), :] = dzr.astype(dzr_ref.dtype)
            dzi_ref[pl.ds(r0, 16), :] = dzi.astype(dzi_ref.dtype)
            return carry, dba + _colsum(dzr), dbi + _colsum(dzi), dlam + _colsum(dlog_a * r16)

        zero = jnp.zeros((1, tc), F32)
        _, dba, dbi, dlam = lax.fori_loop(0, S // 16, step, (jnp.zeros((8, tc), F32), zero, zero, zero))
        dba_ref[...] = dba
        dbi_ref[...] = dbi
        dlam_ref[...] = dlam * dlam_scale

    col = pl.BlockSpec((S, tc), lambda j: (0, j))
    vec = pl.BlockSpec((1, tc), lambda j: (0, j))
    return pl.pallas_call(
        body,
        name="lru_scan_bwd",
        grid=(D_RNN // tc,),
        in_specs=[col, pl.BlockSpec((S, tc), lambda j: (0, OFF_RY // tc + j)), col, col, col, col, vec],
        out_specs=[col, col, col, col, vec, vec, vec],
        out_shape=[jax.ShapeDtypeStruct((S, D_RNN), MXU_DTYPE)] * 3 + [jax.ShapeDtypeStruct((S, D_RNN), F32)]
        + [jax.ShapeDtypeStruct((1, D_RNN), F32)] * 3,
        scratch_shapes=[pltpu.VMEM((S + PAD, tc), F32), pltpu.VMEM((S, tc), F32), pltpu.VMEM((S + PAD, tc), F32)],
        compiler_params=_cparams("parallel"),
    )(dy, proj, h, r, i, rxc, lam)


def _lru_gate_wgrad(rxc, dzr, dzi):
    def body(x_ref, dzr_ref, dzi_ref, dwa_ref, dwi_ref):
        xv = x_ref[...].astype(MXU_DTYPE)
        dims = (((0,), (0,)), ((), ()))
        dwa_ref[...] = lax.dot_general(xv, dzr_ref[...], dims, preferred_element_type=F32)
        dwi_ref[...] = lax.dot_general(xv, dzi_ref[...], dims, preferred_element_type=F32)

    col = pl.BlockSpec((S, RNN_GROUP), lambda g: (0, g))
    w_spec = pl.BlockSpec((None, RNN_GROUP, RNN_GROUP), lambda g: (g, 0, 0))
    return pl.pallas_call(
        body,
        name="lru_gate_wgrad",
        grid=(N_RNN_GROUPS,),
        in_specs=[col, col, col],
        out_specs=[w_spec, w_spec],
        out_shape=[jax.ShapeDtypeStruct((N_RNN_GROUPS, RNN_GROUP, RNN_GROUP), F32)] * 2,
        compiler_params=_cparams("parallel"),
    )(rxc, dzr, dzi)


def _lru_gate_xgrad(dzr, dzi, wa, wi, dx_in):
    tm = 512

    def body(dzr_ref, dzi_ref, wa_ref, wi_ref, dx_ref, o_ref):
        dims = (((1,), (1,)), ((), ()))
        o_ref[...] = (dx_ref[...]
                      + lax.dot_general(dzr_ref[...], wa_ref[...].astype(MXU_DTYPE), dims, preferred_element_type=F32)
                      + lax.dot_general(dzi_ref[...], wi_ref[...].astype(MXU_DTYPE), dims, preferred_element_type=F32))

    x_spec = pl.BlockSpec((tm, RNN_GROUP), lambda g, i: (i, g))
    w_spec = pl.BlockSpec((None, RNN_GROUP, RNN_GROUP), lambda g, i: (g, 0, 0))
    return pl.pallas_call(
        body,
        name="lru_gate_xgrad",
        grid=(N_RNN_GROUPS, S // tm),
        in_specs=[x_spec, x_spec, w_spec, w_spec, x_spec],
        out_specs=x_spec,
        out_shape=jax.ShapeDtypeStruct((S, D_RNN), F32),
        compiler_params=_cparams("parallel", "parallel"),
    )(dzr, dzi, wa, wi, dx_in)


def _gate_fwd(y_attn, y_rnn, proj, b_gate):
    t = 512

    def body(ya_ref, yr_ref, ga_ref, gr_ref, ba_ref, br_ref, o_ref):
        o_ref[...] = (_sigmoid(ga_ref[...] + ba_ref[...]) * ya_ref[...]
                      + _sigmoid(gr_ref[...] + br_ref[...]) * yr_ref[...]).astype(o_ref.dtype)

    tile = pl.BlockSpec((t, t), lambda i, j: (i, j))
    return pl.pallas_call(
        body,
        name="gate_fwd",
        grid=(S // t, D // t),
        in_specs=[tile, tile,
                  pl.BlockSpec((t, t), lambda i, j: (i, OFF_GA // t + j)),
                  pl.BlockSpec((t, t), lambda i, j: (i, OFF_GR // t + j)),
                  pl.BlockSpec((1, t), lambda i, j: (0, j)),
                  pl.BlockSpec((1, t), lambda i, j: (0, D // t + j))],
        out_specs=tile,
        out_shape=jax.ShapeDtypeStruct((S, D), MXU_DTYPE),
        compiler_params=_cparams("parallel", "parallel"),
    )(y_attn, y_rnn, proj, proj, b_gate, b_gate)


def _gate_bwd(dmix, y_attn, y_rnn, proj, b_gate):
    t = 512

    def body(dm_ref, ya_ref, yr_ref, ga_ref, gr_ref, ba_ref, br_ref,
             dya_ref, dyr_ref, dga_ref, dgr_ref, dba_ref, dbr_ref):
        @pl.when(pl.program_id(1) == 0)
        def _():
            dba_ref[...] = jnp.zeros_like(dba_ref)
            dbr_ref[...] = jnp.zeros_like(dbr_ref)

        dm = dm_ref[...]
        ga = _sigmoid(ga_ref[...] + ba_ref[...])
        gr = _sigmoid(gr_ref[...] + br_ref[...])
        dya_ref[...] = (dm * ga).astype(dya_ref.dtype)
        dyr_ref[...] = (dm * gr).astype(dyr_ref.dtype)
        dga = dm * ya_ref[...] * ga * (1.0 - ga)
        dgr = dm * yr_ref[...] * gr * (1.0 - gr)
        dga_ref[...] = dga.astype(dga_ref.dtype)
        dgr_ref[...] = dgr.astype(dgr_ref.dtype)
        dba_ref[...] += _colsum(dga)
        dbr_ref[...] += _colsum(dgr)

    tile = pl.BlockSpec((t, t), lambda j, i: (i, j))
    vec = pl.BlockSpec((1, t), lambda j, i: (0, j))
    return pl.pallas_call(
        body,
        name="gate_bwd",
        grid=(D // t, S // t),
        in_specs=[tile, tile, tile,
                  pl.BlockSpec((t, t), lambda j, i: (i, OFF_GA // t + j)),
                  pl.BlockSpec((t, t), lambda j, i: (i, OFF_GR // t + j)),
                  vec,
                  pl.BlockSpec((1, t), lambda j, i: (0, D // t + j))],
        out_specs=[tile, tile, tile, tile, vec, vec],
        out_shape=[jax.ShapeDtypeStruct((S, D), MXU_DTYPE)] * 4 + [jax.ShapeDtypeStruct((1, D), F32)] * 2,
        compiler_params=_cparams("parallel", "arbitrary"),
    )(dmix, y_attn, y_rnn, proj, proj, b_gate, b_gate)


LN_TM = 256


def _ln_stats(pre):
    mu = jnp.mean(pre, axis=-1, keepdims=True)
    xc = pre - mu
    rstd = lax.rsqrt(jnp.mean(xc * xc, axis=-1, keepdims=True) + LN_EPS)
    return xc * rstd, rstd


def _ln_input_grad(dy, xhat, rstd, g):
    dyg = dy * g
    return rstd * (dyg - jnp.mean(dyg, axis=-1, keepdims=True)
                   - xhat * jnp.mean(dyg * xhat, axis=-1, keepdims=True))


def _ln_fwd(res, branch, g, b):
    def body(res_ref, br_ref, g_ref, b_ref, y_ref, yb_ref, xhat_ref, rstd_ref):
        xhat, rstd = _ln_stats(ALPHA * res_ref[...] + br_ref[...])
        y = xhat * g_ref[...] + b_ref[...]
        y_ref[...] = y
        yb_ref[...] = y.astype(yb_ref.dtype)
        xhat_ref[...] = xhat
        rstd_ref[...] = rstd

    tile = pl.BlockSpec((LN_TM, D), lambda i: (i, 0))
    vec = pl.BlockSpec((1, D), lambda i: (0, 0))
    return pl.pallas_call(
        body,
        name="ln_fwd",
        grid=(S // LN_TM,),
        in_specs=[tile, tile, vec, vec],
        out_specs=[tile, tile, tile, pl.BlockSpec((LN_TM, 1), lambda i: (i, 0))],
        out_shape=[jax.ShapeDtypeStruct((S, D), F32), jax.ShapeDtypeStruct((S, D), MXU_DTYPE),
                   jax.ShapeDtypeStruct((S, D), F32), jax.ShapeDtypeStruct((S, 1), F32)],
        compiler_params=_cparams("parallel"),
    )(res, branch, g, b)


def _ln_bwd(dy_a, dy_b, xhat, rstd, g):
    def body(da_ref, db_in_ref, xhat_ref, rstd_ref, g_ref, dp_ref, dpb_ref, dg_ref, db_ref):
        @pl.when(pl.program_id(0) == 0)
        def _():
            dg_ref[...] = jnp.zeros_like(dg_ref)
            db_ref[...] = jnp.zeros_like(db_ref)

        dy = da_ref[...] + ALPHA * db_in_ref[...]
        xhat = xhat_ref[...]
        dp = _ln_input_grad(dy, xhat, rstd_ref[...], g_ref[...])
        dp_ref[...] = dp
        dpb_ref[...] = dp.astype(dpb_ref.dtype)
        dg_ref[...] += _colsum(dy * xhat)
        db_ref[...] += _colsum(dy)

    tile = pl.BlockSpec((LN_TM, D), lambda i: (i, 0))
    vec = pl.BlockSpec((1, D), lambda i: (0, 0))
    return pl.pallas_call(
        body,
        name="ln_bwd",
        grid=(S // LN_TM,),
        in_specs=[tile, tile, tile, pl.BlockSpec((LN_TM, 1), lambda i: (i, 0)), vec],
        out_specs=[tile, tile, vec, vec],
        out_shape=[jax.ShapeDtypeStruct((S, D), F32), jax.ShapeDtypeStruct((S, D), MXU_DTYPE),
                   jax.ShapeDtypeStruct((1, D), F32), jax.ShapeDtypeStruct((1, D), F32)],
        compiler_params=_cparams("arbitrary"),
    )(dy_a, dy_b, xhat, rstd, g)


def _ln_loss_bwd(res, branch, g, b, target):
    def body(res_ref, br_ref, g_ref, b_ref, t_ref, loss_ref, dp_ref, dpb_ref, dg_ref, db_ref):
        @pl.when(pl.program_id(0) == 0)
        def _():
            loss_ref[...] = jnp.zeros_like(loss_ref)
            dg_ref[...] = jnp.zeros_like(dg_ref)
            db_ref[...] = jnp.zeros_like(db_ref)

        xhat, rstd = _ln_stats(ALPHA * res_ref[...] + br_ref[...])
        gv = g_ref[...]
        err = xhat * gv + b_ref[...] - t_ref[...]
        loss_ref[...] += (0.5 / D) * jnp.sum(_colsum(err * err), axis=1, keepdims=True)
        dy = err * (1.0 / D)
        dp = _ln_input_grad(dy, xhat, rstd, gv)
        dp_ref[...] = dp
        dpb_ref[...] = dp.astype(dpb_ref.dtype)
        dg_ref[...] += _colsum(dy * xhat)
        db_ref[...] += _colsum(dy)

    tile = pl.BlockSpec((LN_TM, D), lambda i: (i, 0))
    vec = pl.BlockSpec((1, D), lambda i: (0, 0))
    return pl.pallas_call(
        body,
        name="ln_loss_bwd",
        grid=(S // LN_TM,),
        in_specs=[tile, tile, vec, vec, tile],
        out_specs=[pl.BlockSpec((1, 1), lambda i: (0, 0)), tile, tile, vec, vec],
        out_shape=[jax.ShapeDtypeStruct((1, 1), F32), jax.ShapeDtypeStruct((S, D), F32),
                   jax.ShapeDtypeStruct((S, D), MXU_DTYPE),
                   jax.ShapeDtypeStruct((1, D), F32), jax.ShapeDtypeStruct((1, D), F32)],
        compiler_params=_cparams("arbitrary"),
    )(res, branch, g, b, target)


FFN_TC = 256


def _ffn_act_fwd(up, gpre, w, b):
    tc = FFN_TC

    def body(up_ref, x_ref, w_ref, b_ref, o_ref, xpad_ref):
        xpad_ref[pl.ds(0, PAD), :] = jnp.zeros((PAD, tc), F32)
        xpad_ref[pl.ds(PAD, S), :] = x_ref[...]
        wv = w_ref[...]
        bv = b_ref[...]

        def step(ci, carry):
            r0 = pl.multiple_of(ci * CHUNK, CHUNK)
            taps = _past_taps(xpad_ref, r0, FFN_CONV_W)
            gate = bv + taps[0] * wv[0:1, :] + taps[1] * wv[1:2, :] + taps[2] * wv[2:3, :]
            o_ref[pl.ds(r0, CHUNK), :] = (_gelu(gate)[0] * up_ref[pl.ds(r0, CHUNK), :]).astype(o_ref.dtype)
            return carry

        lax.fori_loop(0, S // CHUNK, step, 0)

    col = pl.BlockSpec((S, tc), lambda j: (0, j))
    return pl.pallas_call(
        body,
        name="ffn_act_fwd",
        grid=(D_FF // tc,),
        in_specs=[col, col, pl.BlockSpec((FFN_CONV_W, tc), lambda j: (0, j)), pl.BlockSpec((1, tc), lambda j: (0, j))],
        out_specs=col,
        out_shape=jax.ShapeDtypeStruct((S, D_FF), MXU_DTYPE),
        scratch_shapes=[pltpu.VMEM((S + PAD, tc), F32)],
        compiler_params=_cparams("parallel"),
    )(up, gpre, w, b)


def _ffn_act_bwd(dfin, up, gpre, w, b):
    tc = FFN_TC
    width = FFN_CONV_W

    def body(df_ref, up_ref, x_ref, w_ref, b_ref, dup_ref, dx_ref, dw_ref, db_ref, xpad_ref, dpad_ref):
        xpad_ref[pl.ds(0, PAD), :] = jnp.zeros((PAD, tc), F32)
        xpad_ref[pl.ds(PAD, S), :] = x_ref[...]
        dpad_ref[pl.ds(S, PAD), :] = jnp.zeros((PAD, tc), F32)
        wv = w_ref[...]
        bv = b_ref[...]

        def gate_grad(ci, acc):
            r0 = pl.multiple_of(ci * CHUNK, CHUNK)
            taps = _past_taps(xpad_ref, r0, width)
            gate = bv + taps[0] * wv[0:1, :] + taps[1] * wv[1:2, :] + taps[2] * wv[2:3, :]
            ge, dge = _gelu(gate)
            df = df_ref[pl.ds(r0, CHUNK), :]
            dup_ref[pl.ds(r0, CHUNK), :] = (df * ge).astype(dup_ref.dtype)
            d = df * up_ref[pl.ds(r0, CHUNK), :] * dge
            dpad_ref[pl.ds(r0, CHUNK), :] = d
            return tuple(acc[k] + _colsum(taps[k] * d) for k in range(width)) + (acc[width] + _colsum(d),)

        zero = jnp.zeros((1, tc), F32)
        acc = lax.fori_loop(0, S // CHUNK, gate_grad, (zero,) * (width + 1))
        for k in range(width):
            dw_ref[k:k + 1, :] = acc[k]
        db_ref[...] = acc[width]

        def input_grad(ci, carry):
            r0 = pl.multiple_of(ci * CHUNK, CHUNK)
            ahead = _future_taps(dpad_ref, r0, width)
            dx = ahead[0] * wv[2:3, :] + ahead[1] * wv[1:2, :] + ahead[2] * wv[0:1, :]
            dx_ref[pl.ds(r0, CHUNK), :] = dx.astype(dx_ref.dtype)
            return carry

        lax.fori_loop(0, S // CHUNK, input_grad, 0)

    col = pl.BlockSpec((S, tc), lambda j: (0, j))
    w_spec = pl.BlockSpec((width, tc), lambda j: (0, j))
    vec = pl.BlockSpec((1, tc), lambda j: (0, j))
    return pl.pallas_call(
        body,
        name="ffn_act_bwd",
        grid=(D_FF // tc,),
        in_specs=[col, col, col, w_spec, vec],
        out_specs=[col, col, w_spec, vec],
        out_shape=[jax.ShapeDtypeStruct((S, D_FF), MXU_DTYPE)] * 2
        + [jax.ShapeDtypeStruct((width, D_FF), F32), jax.ShapeDtypeStruct((1, D_FF), F32)],
        scratch_shapes=[pltpu.VMEM((S + PAD, tc), F32), pltpu.VMEM((S + PAD, tc), F32)],
        compiler_params=_cparams("parallel"),
    )(dfin, up, gpre, w, b)


def _adamw_update(w, g, m, v):
    m = ADAM_B1 * m + (1.0 - ADAM_B1) * g
    v = ADAM_B2 * v + (1.0 - ADAM_B2) * (g * g)
    m_hat = m / (1.0 - ADAM_B1 ** ADAM_STEP)
    v_hat = v / (1.0 - ADAM_B2 ** ADAM_STEP)
    delta = -ADAM_LR * (m_hat / (jnp.sqrt(v_hat) + ADAM_EPS) + ADAM_WD * w)
    return delta, m, v


def _reduce_adamw(w, m, v, g_own, recv, *, tr, name):
    r_dim, c_dim = w.shape

    def body(w_ref, m_ref, v_ref, g_ref, recv_ref, grad_ref, delta_ref, nm_ref, nv_ref):
        g = g_ref[...]
        for k in range(1, N_DEV):
            g = g + recv_ref[k].astype(F32)
        delta, nm, nv = _adamw_update(w_ref[...], g, m_ref[...], v_ref[...])
        grad_ref[...] = g
        delta_ref[...] = delta
        nm_ref[...] = nm
        nv_ref[...] = nv

    tile = pl.BlockSpec((tr, c_dim), lambda i: (i, 0))
    return pl.pallas_call(
        body,
        name=name,
        grid=(r_dim // tr,),
        in_specs=[tile, tile, tile, tile, pl.BlockSpec((N_DEV, tr, c_dim), lambda i: (0, i, 0))],
        out_specs=[tile] * 4,
        out_shape=[jax.ShapeDtypeStruct((r_dim, c_dim), F32)] * 4,
        compiler_params=_cparams("parallel"),
    )(w, m, v, g_own, recv)


def _adamw_flat(w, m, v, g):
    r_dim = w.shape[0]
    tr = r_dim // 4 if r_dim % 32 == 0 else r_dim

    def body(w_ref, m_ref, v_ref, g_ref, delta_ref, nm_ref, nv_ref):
        delta, nm, nv = _adamw_update(w_ref[...], g_ref[...], m_ref[...], v_ref[...])
        delta_ref[...] = delta
        nm_ref[...] = nm
        nv_ref[...] = nv

    tile = pl.BlockSpec((tr, 128), lambda i: (i, 0))
    return pl.pallas_call(
        body,
        name="adamw_small",
        grid=(r_dim // tr,),
        in_specs=[tile] * 4,
        out_specs=[tile] * 3,
        out_shape=[jax.ShapeDtypeStruct((r_dim, 128), F32)] * 3,
        compiler_params=_cparams("parallel"),
    )(w, m, v, g)


def _coords():
    return lax.axis_index("x"), lax.axis_index("y"), lax.axis_index("c")


def _flip(coord, bit):
    return 1 - coord if bit else coord


def _relative(k):
    x, y, c = _coords()
    return _flip(x, k & 4), _flip(y, k & 2), _flip(c, k & 1)


def _index(pos):
    return 4 * pos[0] + 2 * pos[1] + pos[2]


def _all_gather(shards):
    n = len(shards)

    def body(*refs):
        ins, outs = refs[:n], refs[n:2 * n]
        send_sems, recv_sems, local_sems = refs[2 * n:]
        me = _relative(0)
        sibling = _relative(1)
        far = (4, 2, 6)

        def copy(w, k, block, to, src=None):
            slot = outs[w].at[_index(block)]
            return pltpu.make_async_remote_copy(
                src_ref=slot if src is None else src, dst_ref=slot,
                send_sem=send_sems.at[w * 7 + k], recv_sem=recv_sems.at[w * 7 + k],
                device_id=to, device_id_type=MESH)

        started = []
        for w in range(n):
            mine = pltpu.make_async_copy(ins[w], outs[w].at[_index(me)], local_sems.at[w])
            mine.start()
            started.append(mine)
        sends = []
        for j, k in enumerate(far):
            for w in range(n):
                sends.append(copy(w, 1 + j, me, _relative(k), src=ins[w]))
                sends[-1].start()
        for w in range(n):
            sends.append(copy(w, 0, me, sibling, src=ins[w]))
            sends[-1].start()
        for j, k in enumerate(far):
            for w in range(n):
                copy(w, 1 + j, _relative(k), me).wait_recv()
                sends.append(copy(w, 4 + j, _relative(k), sibling))
                sends[-1].start()
        for w in range(n):
            copy(w, 0, sibling, me).wait_recv()
        for j, k in enumerate(far):
            for w in range(n):
                copy(w, 4 + j, _relative(k | 1), me).wait_recv()
        for cp in sends:
            cp.wait_send()
        for mine in started:
            mine.wait()

    hbm = pl.BlockSpec(memory_space=pltpu.HBM)
    return pl.pallas_call(
        body,
        name="gather_weights",
        in_specs=[hbm] * n,
        out_specs=[hbm] * n,
        out_shape=[jax.ShapeDtypeStruct((N_DEV,) + s.shape, s.dtype) for s in shards],
        scratch_shapes=[pltpu.SemaphoreType.DMA((7 * n,)), pltpu.SemaphoreType.DMA((7 * n,)),
                        pltpu.SemaphoreType.DMA((n,))],
    )(*shards)


def _reduce_scatter(sends):
    n = len(sends)

    def body(*refs):
        ins, outs = refs[:n], refs[n:2 * n]
        send_sems, recv_sems = refs[2 * n:]

        def copy(w, k):
            return pltpu.make_async_remote_copy(
                src_ref=ins[w].at[_index(_relative(k))], dst_ref=outs[w].at[k],
                send_sem=send_sems.at[w * 7 + k - 1], recv_sem=recv_sems.at[w * 7 + k - 1],
                device_id=_relative(k), device_id_type=MESH)

        copies = [copy(w, k) for w in range(n) for k in (4, 2, 6, 1, 5, 3, 7)]
        for cp in copies:
            cp.start()
        for cp in copies:
            cp.wait()

    hbm = pl.BlockSpec(memory_space=pltpu.HBM)
    return pl.pallas_call(
        body,
        name="reduce_scatter_grads",
        in_specs=[hbm] * n,
        out_specs=[hbm] * n,
        out_shape=[jax.ShapeDtypeStruct(s.shape, s.dtype) for s in sends],
        scratch_shapes=[pltpu.SemaphoreType.DMA((7 * n,)), pltpu.SemaphoreType.DMA((7 * n,))],
    )(*sends)


def _all_reduce_small(v):
    r_dim = v.shape[1]

    def body(v_ref, out_ref, stage_ref, a_send, a_recv, b_send, b_recv):
        me = _index(_relative(0))

        def to_owner(k):
            return pltpu.make_async_remote_copy(
                src_ref=v_ref.at[_index(_relative(k))], dst_ref=stage_ref.at[k],
                send_sem=a_send.at[k - 1], recv_sem=a_recv.at[k - 1], device_id=_relative(k), device_id_type=MESH)

        def to_all(k):
            return pltpu.make_async_remote_copy(
                src_ref=out_ref.at[me], dst_ref=out_ref.at[me],
                send_sem=b_send.at[k - 1], recv_sem=b_recv.at[k - 1], device_id=_relative(k), device_id_type=MESH)

        def from_owner(k):
            slot = out_ref.at[_index(_relative(k))]
            return pltpu.make_async_remote_copy(
                src_ref=slot, dst_ref=slot,
                send_sem=b_send.at[k - 1], recv_sem=b_recv.at[k - 1], device_id=_relative(k), device_id_type=MESH)

        for k in range(1, N_DEV):
            to_owner(k).start()
        acc = v_ref[me]
        for k in range(1, N_DEV):
            to_owner(k).wait_recv()
            acc = acc + stage_ref[k]
        out_ref[me] = acc
        for k in range(1, N_DEV):
            to_all(k).start()
        for k in range(1, N_DEV):
            from_owner(k).wait_recv()
        for k in range(1, N_DEV):
            to_owner(k).wait_send()
            to_all(k).wait_send()

    vmem = pl.BlockSpec(memory_space=pltpu.VMEM)
    return pl.pallas_call(
        body,
        name="all_reduce_small",
        in_specs=[vmem],
        out_specs=vmem,
        out_shape=jax.ShapeDtypeStruct(v.shape, F32),
        scratch_shapes=[pltpu.VMEM((N_DEV, r_dim, 128), F32)] + [pltpu.SemaphoreType.DMA((N_DEV - 1,))] * 4,
        compiler_params=pltpu.CompilerParams(vmem_limit_bytes=VMEM_LIMIT),
    )(v)


def _pack(arrays, total=None):
    parts = []
    for a in arrays:
        flat = a.reshape(-1)
        parts.append(jnp.pad(flat, (0, -flat.size % FLAT_TILE)))
    size = sum(p.size for p in parts)
    if total is not None:
        parts.append(jnp.zeros((total - size,), F32))
    return jnp.concatenate(parts)


def _unpack(flat, shapes):
    out, off = [], 0
    for shape in shapes:
        size = math.prod(shape)
        out.append(flat[off:off + size].reshape(shape))
        off += size + (-size % FLAT_TILE)
    return out


def _block_diag(w):
    w4 = w.reshape(N_RNN_GROUPS, 4, RNN_BLOCK_W, RNN_BLOCK_W)
    eye = jnp.eye(4, dtype=w.dtype)
    return (w4[:, :, :, None, :] * eye[None, :, None, :, None]).reshape(N_RNN_GROUPS, RNN_GROUP, RNN_GROUP)


def _diag_blocks(wg):
    w5 = wg.reshape(N_RNN_GROUPS, 4, RNN_BLOCK_W, 4, RNN_BLOCK_W)
    return jnp.stack([w5[:, b, :, b, :] for b in range(4)], axis=1).reshape(16, RNN_BLOCK_W, RNN_BLOCK_W)


def _heads_major(t, n_heads):
    return t.reshape(S, n_heads, HEAD_DIM).transpose(1, 0, 2)


def _heads_minor(t):
    return t.transpose(1, 0, 2).reshape(S, t.shape[0] * HEAD_DIM)


def _forward_backward(x2, target, wts):
    xb = x2.astype(MXU_DTYPE)
    proj = _mm(xb, wts["w_in"], tm=1024, tn=512, tk=D, name="proj")

    q = _heads_major(proj[:, :OFF_K], N_KV * GROUP).reshape(N_KV, GROUP, S, HEAD_DIM).astype(MXU_DTYPE)
    front = ((0, 0), (BLOCK, 0), (0, 0))
    kp = jnp.pad(_heads_major(proj[:, OFF_K:OFF_V], N_KV), front).astype(MXU_DTYPE)
    vp = jnp.pad(_heads_major(proj[:, OFF_V:OFF_RX], N_KV), front).astype(MXU_DTYPE)
    sink_rows = jnp.repeat(wts["attn_sinks"].reshape(N_KV, GROUP, 1), BLOCK, axis=1).reshape(N_KV, GROUP * BLOCK, 1)
    o = _heads_minor(_attn_fwd(q, kp, vp, sink_rows).reshape(N_KV * GROUP, S, HEAD_DIM))

    rxc = _conv_fwd(proj, OFF_RX, wts["rnn_conv_w"], wts["rnn_conv_b"], tc=512, name="rnn_conv_fwd")
    r, i = _lru_gates(rxc, wts["lru_wa"], wts["lru_wi"], wts["lru_ba"], wts["lru_bi"])
    h, yrin = _lru_scan_fwd(r, i, rxc, proj, wts["lru_lambda"])

    y_attn = _mm(o, wts["w_attn_proj"], tm=1024, tn=1024, tk=D, name="attn_proj")
    y_rnn = _mm(yrin, wts["w_rnn_proj"], tm=1024, tn=1024, tk=D_RNN, name="rnn_proj")
    mixin = _gate_fwd(y_attn, y_rnn, proj, wts["b_gate"])
    mix = _mm(mixin, wts["w_out"], tm=1024, tn=1024, tk=D, name="mix_out")
    x1, x1b, xhat1, rstd1 = _ln_fwd(x2, mix, wts["ln1_g"], wts["ln1_b"])

    up = _mm(x1b, wts["ffn_w_up"], tm=1024, tn=768, tk=D, b_block=768, name="ffn_up")
    gpre = _mm(x1b, wts["ffn_w_gate"], tm=1024, tn=768, tk=D, b_block=768, name="ffn_gate")
    fin = _ffn_act_fwd(up, gpre, wts["ffn_conv_w"], wts["ffn_conv_b"])
    f = _mm(fin, wts["ffn_w_down"], tm=1024, tn=1024, tk=2048, name="ffn_down")
    loss, dpre2, dpre2b, d_ln2_g, d_ln2_b = _ln_loss_bwd(x1, f, wts["ln2_g"], wts["ln2_b"], target)

    grads = {"ln2_g": d_ln2_g, "ln2_b": d_ln2_b}
    grads["ffn_w_down"] = _mm(fin, dpre2b, ta=True, tm=1024, tn=1024, tk=S, name="d_ffn_w_down")
    dfin = _mm(dpre2b, wts["ffn_w_down"], tb=True, tm=1024, tn=1024, tk=D, name="d_fin")
    dup, dgpre, grads["ffn_conv_w"], grads["ffn_conv_b"] = _ffn_act_bwd(
        dfin, up, gpre, wts["ffn_conv_w"], wts["ffn_conv_b"])
    grads["ffn_w_up"] = _mm(x1b, dup, ta=True, tm=1024, tn=1024, tk=S, name="d_ffn_w_up")
    grads["ffn_w_gate"] = _mm(x1b, dgpre, ta=True, tm=1024, tn=1024, tk=S, name="d_ffn_w_gate")
    dx1 = _mm(dup, wts["ffn_w_up"], tb=True, tm=1024, tn=1024, tk=768, b_block=768, name="d_x1_up")
    dx1 = _mm(dgpre, wts["ffn_w_gate"], tb=True, tm=1024, tn=1024, tk=768, b_block=768, add=dx1, name="d_x1_gate")
    dpre1, dpre1b, grads["ln1_g"], grads["ln1_b"] = _ln_bwd(dx1, dpre2, xhat1, rstd1, wts["ln1_g"])

    grads["w_out"] = _mm(mixin, dpre1b, ta=True, tm=1024, tn=1024, tk=S, name="d_w_out")
    dmix = _mm(dpre1b, wts["w_out"], tb=True, tm=1024, tn=1024, tk=D, name="d_mixin")
    dya, dyr, dgl_a, dgl_r, db_a, db_r = _gate_bwd(dmix, y_attn, y_rnn, proj, wts["b_gate"])
    grads["b_gate"] = jnp.concatenate([db_a, db_r], axis=1)
    grads["w_attn_proj"] = _mm(o, dya, ta=True, tm=1024, tn=1024, tk=S, name="d_w_attn_proj")
    grads["w_rnn_proj"] = _mm(yrin, dyr, ta=True, tm=1280, tn=1024, tk=S, name="d_w_rnn_proj")
    do = _mm(dya, wts["w_attn_proj"], tb=True, tm=1024, tn=1024, tk=D, out_dtype=MXU_DTYPE, name="d_o")
    dyrin = _mm(dyr, wts["w_rnn_proj"], tb=True, tm=1024, tn=1280, tk=D, name="d_yrin")

    dry, dzr, dzi, drxc_in, grads["lru_ba"], grads["lru_bi"], grads["lru_lambda"] = _lru_scan_bwd(
        dyrin, proj, h, r, i, rxc, wts["lru_lambda"])
    grads["lru_wa"], grads["lru_wi"] = _lru_gate_wgrad(rxc, dzr, dzi)
    drxc = _lru_gate_xgrad(dzr, dzi, wts["lru_wa"], wts["lru_wi"], drxc_in)
    drx, grads["rnn_conv_w"], grads["rnn_conv_b"] = _conv_bwd(drxc, proj, OFF_RX, wts["rnn_conv_w"], tc=512,
                                                             name="rnn_conv_bwd")

    do_hm = _heads_major(do, N_KV * GROUP).reshape(N_KV, GROUP, S, HEAD_DIM)
    dq, dk, dv, dsink = _attn_bwd(q, kp, vp, sink_rows, do_hm)
    grads["attn_sinks"] = dsink.reshape(1, N_KV * GROUP)
    dproj = jnp.concatenate([
        _heads_minor(dq.reshape(N_KV * GROUP, S, HEAD_DIM)),
        _heads_minor(dk[:, BLOCK:, :]).astype(MXU_DTYPE),
        _heads_minor(dv[:, BLOCK:, :]).astype(MXU_DTYPE),
        drx, dry, dgl_a, dgl_r], axis=1)
    dx = _mm(dproj, wts["w_in"], tb=True, tm=1024, tn=1024, tk=512, add=dpre1, add_scale=ALPHA, name="d_x")
    grads["w_in"] = _mm(xb, dproj, ta=True, tm=1024, tn=512, tk=S, name="d_w_in")
    return loss, dx, grads


SHARDED = (
    ("w_in", "cols", 128), ("w_attn_proj", "rows", 128), ("w_rnn_proj", "rows", 160), ("w_out", "rows", 128),
    ("ffn_w_up", "cols", 256), ("ffn_w_gate", "cols", 256), ("ffn_w_down", "rows", 256),
)
SMALL_REPLICATED = ("b_gate", "rnn_conv_b", "lru_wa", "lru_ba", "lru_wi", "lru_bi", "lru_lambda", "attn_sinks",
                    "ln1_g", "ln1_b", "ffn_conv_b", "ln2_g", "ln2_b")
SMALL_SHARDED = ("rnn_conv_w", "ffn_conv_w")
WEIGHTS = ("w_in", "b_gate", "rnn_conv_w", "rnn_conv_b", "lru_wa", "lru_ba", "lru_wi", "lru_bi", "lru_lambda",
           "attn_sinks", "w_attn_proj", "w_rnn_proj", "w_out", "ln1_g", "ln1_b", "ffn_w_up", "ffn_w_gate",
           "ffn_conv_w", "ffn_conv_b", "ffn_w_down", "ln2_g", "ln2_b")
REDUCE_ROWS = 864


def _natural(gathered, how):
    n, r, c = gathered.shape
    if how == "rows":
        return gathered.reshape(n * r, c)
    return gathered.transpose(1, 0, 2).reshape(r, n * c)


def _blocks(full, how):
    if how == "rows":
        return full.reshape(N_DEV, full.shape[0] // N_DEV, full.shape[1])
    return full.reshape(full.shape[0], N_DEV, full.shape[1] // N_DEV).transpose(1, 0, 2)


def kernel(x, w_in, b_gate, rnn_conv_w, rnn_conv_b, lru_wa, lru_ba, lru_wi, lru_bi, lru_lambda, attn_sinks, w_attn_proj, w_rnn_proj, w_out, ln1_g, ln1_b, ffn_w_up, ffn_w_gate, ffn_conv_w, ffn_conv_b, ffn_w_down, ln2_g, ln2_b, loss_target, m_w_in, m_b_gate, m_rnn_conv_w, m_rnn_conv_b, m_lru_wa, m_lru_ba, m_lru_wi, m_lru_bi, m_lru_lambda, m_attn_sinks, m_w_attn_proj, m_w_rnn_proj, m_w_out, m_ln1_g, m_ln1_b, m_ffn_w_up, m_ffn_w_gate, m_ffn_conv_w, m_ffn_conv_b, m_ffn_w_down, m_ln2_g, m_ln2_b, v_w_in, v_b_gate, v_rnn_conv_w, v_rnn_conv_b, v_lru_wa, v_lru_ba, v_lru_wi, v_lru_bi, v_lru_lambda, v_attn_sinks, v_w_attn_proj, v_w_rnn_proj, v_w_out, v_ln1_g, v_ln1_b, v_ffn_w_up, v_ffn_w_gate, v_ffn_conv_w, v_ffn_conv_b, v_ffn_w_down, v_ln2_g, v_ln2_b):
    given = dict(locals())
    wsh = {n: given[n][0] for n in WEIGHTS}
    msh = {n: given["m_" + n][0] for n in WEIGHTS}
    vsh = {n: given["v_" + n][0] for n in WEIGHTS}
    me = 4 * lax.axis_index("x") + 2 * lax.axis_index("y") + lax.axis_index("c")

    shards = [wsh[n].astype(MXU_DTYPE) for n, _, _ in SHARDED] + [wsh[n] for n in SMALL_SHARDED]
    gathered = _all_gather(shards)
    wts = {}
    for (n, how, _), g in zip(SHARDED, gathered):
        wts[n] = g if n in ("ffn_w_up", "ffn_w_gate") else _natural(g, how)
    for n, g in zip(SMALL_SHARDED, gathered[len(SHARDED):]):
        wts[n] = _natural(g, "cols")
    for n in SMALL_REPLICATED:
        wts[n] = wsh[n].reshape(1, -1) if wsh[n].ndim == 1 else wsh[n]
    wts["lru_wa"] = _block_diag(wsh["lru_wa"])
    wts["lru_wi"] = _block_diag(wsh["lru_wi"])

    loss, dx, grads = _forward_backward(x[0], loss_target[0], wts)
    grads["lru_wa"] = _diag_blocks(grads["lru_wa"])
    grads["lru_wi"] = _diag_blocks(grads["lru_wi"])

    sends, owns = [], []
    for n, how, _ in SHARDED:
        blocks = _blocks(grads[n], how)
        sends.append(blocks.astype(BF16))
        owns.append(lax.dynamic_index_in_dim(blocks, me, axis=0, keepdims=False))
    recvs = _reduce_scatter(sends)
    out = {}
    for (n, _, tr), own, recv in zip(SHARDED, owns, recvs):
        out[n] = _reduce_adamw(wsh[n], msh[n], vsh[n], own, recv, tr=tr, name="adamw_" + n)

    small = SMALL_REPLICATED + SMALL_SHARDED
    total = _all_reduce_small(
        _pack([loss] + [grads[n] for n in small], N_DEV * REDUCE_ROWS * 128).reshape(N_DEV, REDUCE_ROWS, 128))
    summed = _unpack(total.reshape(-1), [(1, 1)] + [grads[n].shape for n in small])
    loss_total = summed[0].reshape(())
    g_small = dict(zip(small, summed[1:]))
    for n in SMALL_SHARDED:
        width = wsh[n].shape[1]
        g_small[n] = lax.dynamic_slice_in_dim(g_small[n], me * width, width, axis=1)
    g_small = {n: g_small[n].reshape(wsh[n].shape) for n in small}
    flat = [_pack([d[n] for n in small]).reshape(-1, 128) for d in (wsh, msh, vsh, g_small)]
    results = _adamw_flat(*flat)
    shapes = [wsh[n].shape for n in small]
    for n, delta, nm, nv in zip(small, *[_unpack(res.reshape(-1), shapes) for res in results]):
        out[n] = (g_small[n], delta, nm, nv)

    outputs = [loss_total, dx[None]]
    for kind in range(4):
        outputs += [out[n][kind][None] for n in WEIGHTS]
    return tuple(outputs)
```

```python
import math

import jax
import jax.numpy as jnp
from jax import lax
from jax.experimental import pallas as pl
from jax.experimental.pallas import tpu as pltpu

F32 = jnp.float32
BF16 = jnp.bfloat16
MXU_DTYPE = jnp.bfloat16

N_DEV = 8
S = 2048
D = 2048
HEAD_DIM = 64
N_KV = 4
GROUP = 8
BLOCK = 128
D_KV = N_KV * HEAD_DIM
D_RNN = 2560
RNN_GROUP = 640
N_RNN_GROUPS = D_RNN // RNN_GROUP
RNN_BLOCK_W = 160
RNN_CONV_W = 4
LRU_C = 8.0
D_FF = 6144
FFN_CONV_W = 3
D_IN = 11776
OFF_K = 2048
OFF_V = 2304
OFF_RX = 2560
OFF_RY = 5120
OFF_GA = 7680
OFF_GR = 9728
LN_EPS = 1e-5
ALPHA = 2.0 ** 0.25
ADAM_LR = 0.001
ADAM_B1 = 0.9
ADAM_B2 = 0.999
ADAM_EPS = 1e-08
ADAM_WD = 0.01
ADAM_STEP = 10
NEG = -1e30
VMEM_LIMIT = 56 * 1024 * 1024
MESH = pl.DeviceIdType.MESH
GELU_C = math.sqrt(2.0 / math.pi)
FLAT_TILE = 1024


def _cparams(*sem):
    return pltpu.CompilerParams(dimension_semantics=sem or None, vmem_limit_bytes=VMEM_LIMIT)


def _call(body, *, name, grid, in_specs, out_specs, out_shape, operands, semantics, scratch_shapes=(), side=None):
    single = not isinstance(out_shape, (list, tuple))
    out_shape = [out_shape] if single else list(out_shape)
    out_specs = [out_specs] if single else list(out_specs)
    in_specs = list(in_specs)
    scratch_shapes = list(scratch_shapes)
    if side is None:
        res = pl.pallas_call(
            body, name=name, grid=grid, in_specs=in_specs, out_specs=out_specs, out_shape=out_shape,
            scratch_shapes=scratch_shapes, compiler_params=_cparams(*semantics))(*operands)
        return res[0] if single else res
    n_in, n_out, n_scr = len(in_specs), len(out_shape), len(scratch_shapes)
    s_in, s_out = len(side.operands), len(side.out_shape)
    hbm = pl.BlockSpec(memory_space=pltpu.HBM)

    def with_copies(*refs):
        core_in, side_in = refs[:n_in], refs[n_in:n_in + s_in]
        o0 = n_in + s_in
        core_out, side_out = refs[o0:o0 + n_out], refs[o0 + n_out:o0 + n_out + s_out]
        c0 = o0 + n_out + s_out
        core_scr, sems = refs[c0:c0 + n_scr], refs[c0 + n_scr:]
        first, last = None, None
        for d, size in enumerate(grid):
            at_start, at_end = pl.program_id(d) == 0, pl.program_id(d) == size - 1
            first = at_start if first is None else first & at_start
            last = at_end if last is None else last & at_end

        @pl.when(first)
        def _():
            side.start(side_in, side_out, sems)

        body(*core_in, *core_out, *core_scr)

        @pl.when(last)
        def _():
            side.finish(side_in, side_out, sems)

    res = pl.pallas_call(
        with_copies, name=name, grid=grid,
        in_specs=in_specs + [hbm] * s_in, out_specs=out_specs + [hbm] * s_out,
        out_shape=out_shape + list(side.out_shape),
        scratch_shapes=scratch_shapes + list(side.sems),
        input_output_aliases={n_in + i: n_out + o for i, o in side.aliases.items()},
        compiler_params=_cparams(*(("arbitrary",) * len(grid))))(*operands, *side.operands)
    side.done(res[n_out:])
    return res[0] if single else res[:n_out]


def _run_side(side, name):
    def body(*refs):
        s_in, s_out = len(side.operands), len(side.out_shape)
        side.start(refs[:s_in], refs[s_in:s_in + s_out], refs[s_in + s_out:])
        side.finish(refs[:s_in], refs[s_in:s_in + s_out], refs[s_in + s_out:])

    hbm = pl.BlockSpec(memory_space=pltpu.HBM)
    res = pl.pallas_call(
        body, name=name, in_specs=[hbm] * len(side.operands), out_specs=[hbm] * len(side.out_shape),
        out_shape=list(side.out_shape), scratch_shapes=list(side.sems),
        input_output_aliases=dict(side.aliases))(*side.operands)
    side.done(res)


def _gelu(x):
    x2 = x * x
    t = jnp.tanh(GELU_C * (x + 0.044715 * x * x2))
    g = 0.5 * x * (1.0 + t)
    dg = 0.5 * (1.0 + t) + 0.5 * x * (1.0 - t * t) * (GELU_C * (1.0 + 3.0 * 0.044715 * x2))
    return g, dg


def _sigmoid(x):
    return 1.0 / (1.0 + jnp.exp(-x))


def _softplus(x):
    z = jnp.exp(-jnp.abs(x))
    small = z * (1.0 - z * (0.5 - z * (1.0 / 3.0 - 0.25 * z)))
    return jnp.maximum(x, 0.0) + jnp.where(z < 0.02, small, jnp.log(1.0 + z))


def _one_minus_exp(x):
    series = -x * (1.0 + x * (0.5 + x * (1.0 / 6.0 + x * (1.0 / 24.0))))
    return jnp.where(x > -0.03, series, 1.0 - jnp.exp(x))


def _colsum(v):
    return jnp.sum(v, axis=0, keepdims=True)


def _mm(a, b, *, tm, tn, tk, name, ta=False, tb=False, out_dtype=F32, b_block=None, add=None, add_scale=1.0,
        side=None):
    if ta:
        k_dim, m_dim = a.shape
    else:
        m_dim, k_dim = a.shape
    if b_block is None:
        n_dim = b.shape[0] if tb else b.shape[1]
    else:
        n_dim = b.shape[1] if tb else b.shape[0] * b_block
    assert m_dim % tm == 0 and n_dim % tn == 0 and k_dim % tk == 0, (name, m_dim, n_dim, k_dim)
    nk = k_dim // tk
    dims = (((0 if ta else 1,), (1 if tb else 0,)), ((), ()))
    has_add = add is not None

    def body(*refs):
        a_ref, b_ref = refs[0], refs[1]
        add_ref = refs[2] if has_add else None
        o_ref = refs[3 if has_add else 2]

        def product():
            return lax.dot_general(a_ref[...].astype(MXU_DTYPE), b_ref[...].astype(MXU_DTYPE), dims,
                                   preferred_element_type=F32)

        def finish(acc):
            if has_add:
                acc = acc + add_scale * add_ref[...]
            o_ref[...] = acc.astype(out_dtype)

        if nk == 1:
            finish(product())
        else:
            acc_ref = refs[-1]
            k = pl.program_id(2)

            @pl.when(k == 0)
            def _():
                acc_ref[...] = jnp.zeros_like(acc_ref)

            acc_ref[...] += product()

            @pl.when(k == nk - 1)
            def _():
                finish(acc_ref[...])

    if ta:
        a_spec = pl.BlockSpec((tk, tm), lambda i, j, k: (k, i))
    else:
        a_spec = pl.BlockSpec((tm, tk), lambda i, j, k: (i, k))
    if b_block is None:
        if tb:
            b_spec = pl.BlockSpec((tn, tk), lambda i, j, k: (j, k))
        else:
            b_spec = pl.BlockSpec((tk, tn), lambda i, j, k: (k, j))
    elif tb:
        assert b_block % tk == 0
        b_spec = pl.BlockSpec((None, tn, tk), lambda i, j, k: ((k * tk) // b_block, j, ((k * tk) % b_block) // tk))
    else:
        assert b_block % tn == 0
        b_spec = pl.BlockSpec((None, tk, tn), lambda i, j, k: ((j * tn) // b_block, k, ((j * tn) % b_block) // tn))
    in_specs = [a_spec, b_spec]
    operands = [a, b]
    if has_add:
        in_specs.append(pl.BlockSpec((tm, tn), lambda i, j, k: (i, j)))
        operands.append(add)
    return _call(
        body,
        name=name,
        grid=(m_dim // tm, n_dim // tn, nk),
        in_specs=in_specs,
        out_specs=pl.BlockSpec((tm, tn), lambda i, j, k: (i, j)),
        out_shape=jax.ShapeDtypeStruct((m_dim, n_dim), out_dtype),
        scratch_shapes=[pltpu.VMEM((tm, tn), F32)] if nk > 1 else [],
        semantics=("parallel", "parallel", "arbitrary"),
        operands=tuple(operands),
        side=side,
    )


def _attn_bias(h):
    row = lax.broadcasted_iota(jnp.int32, (GROUP * BLOCK, 2 * BLOCK), 0)
    col = lax.broadcasted_iota(jnp.int32, (GROUP * BLOCK, 2 * BLOCK), 1)
    dist = BLOCK + (row & (BLOCK - 1)) - col
    head = h * GROUP + (row >> 7) + 1
    slope = jnp.exp(head.astype(F32) * (-0.25 * math.log(2.0)))
    return jnp.where((dist >= 0) & (dist < BLOCK), -slope * dist.astype(F32), NEG)


def _attn_probs(qn, kb, bias, sink, first_block):
    s = lax.dot_general(qn, kb, (((1,), (1,)), ((), ())), preferred_element_type=F32) * (HEAD_DIM ** -0.5) + bias
    col = lax.broadcasted_iota(jnp.int32, s.shape, 1)
    s = jnp.where(first_block & (col < BLOCK), NEG, s)
    m = jnp.maximum(jnp.max(s, axis=-1, keepdims=True), sink)
    e = jnp.exp(s - m)
    e_sink = jnp.exp(sink - m)
    inv = 1.0 / (jnp.sum(e, axis=-1, keepdims=True) + e_sink)
    return e * inv, e_sink * inv


def _attn_fwd(q, kp, vp, sink_rows, side=None):
    rows = GROUP * BLOCK

    def body(q_ref, k_ref, v_ref, sink_ref, o_ref, bias_ref):
        bias_ref[...] = _attn_bias(pl.program_id(0))
        sink = sink_ref[...]

        def step(n, carry):
            r0 = pl.multiple_of(n * BLOCK, BLOCK)
            qn = q_ref[:, pl.ds(r0, BLOCK), :].reshape(rows, HEAD_DIM)
            kb = k_ref[pl.ds(r0, 2 * BLOCK), :]
            vb = v_ref[pl.ds(r0, 2 * BLOCK), :]
            p, _ = _attn_probs(qn, kb, bias_ref[...], sink, n == 0)
            o = jnp.dot(p.astype(MXU_DTYPE), vb, preferred_element_type=F32)
            o_ref[:, pl.ds(r0, BLOCK), :] = o.reshape(GROUP, BLOCK, HEAD_DIM).astype(o_ref.dtype)
            return carry

        lax.fori_loop(0, S // BLOCK, step, 0)

    return _call(
        body,
        name="attn_fwd",
        grid=(N_KV,),
        in_specs=[
            pl.BlockSpec((None, GROUP, S, HEAD_DIM), lambda h: (h, 0, 0, 0)),
            pl.BlockSpec((None, BLOCK + S, HEAD_DIM), lambda h: (h, 0, 0)),
            pl.BlockSpec((None, BLOCK + S, HEAD_DIM), lambda h: (h, 0, 0)),
            pl.BlockSpec((None, rows, 1), lambda h: (h, 0, 0)),
        ],
        out_specs=pl.BlockSpec((None, GROUP, S, HEAD_DIM), lambda h: (h, 0, 0, 0)),
        out_shape=jax.ShapeDtypeStruct((N_KV, GROUP, S, HEAD_DIM), MXU_DTYPE),
        scratch_shapes=[pltpu.VMEM((rows, 2 * BLOCK), F32)],
        semantics=("parallel",),
        operands=(q, kp, vp, sink_rows),
        side=side,
    )


def _attn_bwd(q, kp, vp, sink_rows, do, side=None):
    rows = GROUP * BLOCK

    def body(q_ref, k_ref, v_ref, sink_ref, do_ref, dq_ref, dk_ref, dv_ref, ds_ref, bias_ref):
        bias_ref[...] = _attn_bias(pl.program_id(0))
        sink = sink_ref[...]
        dk_ref[...] = jnp.zeros_like(dk_ref)
        dv_ref[...] = jnp.zeros_like(dv_ref)

        def step(n, dsink):
            r0 = pl.multiple_of(n * BLOCK, BLOCK)
            qn = q_ref[:, pl.ds(r0, BLOCK), :].reshape(rows, HEAD_DIM)
            don = do_ref[:, pl.ds(r0, BLOCK), :].reshape(rows, HEAD_DIM)
            kb = k_ref[pl.ds(r0, 2 * BLOCK), :]
            vb = v_ref[pl.ds(r0, 2 * BLOCK), :]
            p, p_sink = _attn_probs(qn, kb, bias_ref[...], sink, n == 0)
            dp = lax.dot_general(don, vb, (((1,), (1,)), ((), ())), preferred_element_type=F32)
            delta = jnp.sum(p * dp, axis=-1, keepdims=True)
            ds = (p * (dp - delta) * (HEAD_DIM ** -0.5)).astype(MXU_DTYPE)
            dq = jnp.dot(ds, kb, preferred_element_type=F32)
            dq_ref[:, pl.ds(r0, BLOCK), :] = dq.reshape(GROUP, BLOCK, HEAD_DIM).astype(dq_ref.dtype)
            dk_ref[pl.ds(r0, 2 * BLOCK), :] += lax.dot_general(ds, qn, (((0,), (0,)), ((), ())),
                                                              preferred_element_type=F32)
            dv_ref[pl.ds(r0, 2 * BLOCK), :] += lax.dot_general(p.astype(MXU_DTYPE), don, (((0,), (0,)), ((), ())),
                                                              preferred_element_type=F32)
            return dsink - p_sink * delta

        dsink = lax.fori_loop(0, S // BLOCK, step, jnp.zeros((rows, 1), F32))
        for g in range(GROUP):
            ds_ref[g:g + 1, :] = _colsum(dsink[g * BLOCK:(g + 1) * BLOCK, :])

    hm = pl.BlockSpec((None, GROUP, S, HEAD_DIM), lambda h: (h, 0, 0, 0))
    kv = pl.BlockSpec((None, BLOCK + S, HEAD_DIM), lambda h: (h, 0, 0))
    return _call(
        body,
        name="attn_bwd",
        grid=(N_KV,),
        in_specs=[hm, kv, kv, pl.BlockSpec((None, rows, 1), lambda h: (h, 0, 0)), hm],
        out_specs=[hm, kv, kv, pl.BlockSpec((None, GROUP, 1), lambda h: (h, 0, 0))],
        out_shape=[
            jax.ShapeDtypeStruct((N_KV, GROUP, S, HEAD_DIM), MXU_DTYPE),
            jax.ShapeDtypeStruct((N_KV, BLOCK + S, HEAD_DIM), F32),
            jax.ShapeDtypeStruct((N_KV, BLOCK + S, HEAD_DIM), F32),
            jax.ShapeDtypeStruct((N_KV, GROUP, 1), F32),
        ],
        scratch_shapes=[pltpu.VMEM((rows, 2 * BLOCK), F32)],
        semantics=("parallel",),
        operands=(q, kp, vp, sink_rows, do),
        side=side,
    )


PAD = 8
CHUNK = 256


def _past_taps(xpad_ref, r0, width):
    ext = xpad_ref[pl.ds(r0, CHUNK + PAD), :]
    taps = []
    for k in range(width):
        back = width - 1 - k
        taps.append((ext if back == 0 else pltpu.roll(ext, back, 0))[PAD:, :])
    return taps


def _future_taps(xpad_ref, r0, width):
    ext = xpad_ref[pl.ds(r0, CHUNK + PAD), :]
    taps = []
    for ahead in range(width):
        taps.append((ext if ahead == 0 else pltpu.roll(ext, CHUNK + PAD - ahead, 0))[:CHUNK, :])
    return taps


def _conv_fwd(src, col0, w, b, *, tc, name, side=None):
    width, c_dim = w.shape

    def body(x_ref, w_ref, b_ref, o_ref, xpad_ref):
        xpad_ref[pl.ds(0, PAD), :] = jnp.zeros((PAD, tc), F32)
        xpad_ref[pl.ds(PAD, S), :] = x_ref[...]
        wv = w_ref[...]
        bv = b_ref[...]

        def step(ci, carry):
            r0 = pl.multiple_of(ci * CHUNK, CHUNK)
            taps = _past_taps(xpad_ref, r0, width)
            y = bv + taps[0] * wv[0:1, :]
            for k in range(1, width):
                y = y + taps[k] * wv[k:k + 1, :]
            o_ref[pl.ds(r0, CHUNK), :] = y
            return carry

        lax.fori_loop(0, S // CHUNK, step, 0)

    return _call(
        body,
        name=name,
        grid=(c_dim // tc,),
        in_specs=[
            pl.BlockSpec((S, tc), lambda j: (0, col0 // tc + j)),
            pl.BlockSpec((width, tc), lambda j: (0, j)),
            pl.BlockSpec((1, tc), lambda j: (0, j)),
        ],
        out_specs=pl.BlockSpec((S, tc), lambda j: (0, j)),
        out_shape=jax.ShapeDtypeStruct((S, c_dim), F32),
        scratch_shapes=[pltpu.VMEM((S + PAD, tc), F32)],
        semantics=("parallel",),
        operands=(src, w, b),
        side=side,
    )


def _conv_bwd(dy, src, col0, w, *, tc, name, side=None):
    width, c_dim = w.shape

    def body(dy_ref, x_ref, w_ref, dx_ref, dw_ref, db_ref, xpad_ref, dpad_ref):
        xpad_ref[pl.ds(0, PAD), :] = jnp.zeros((PAD, tc), F32)
        xpad_ref[pl.ds(PAD, S), :] = x_ref[...]
        dpad_ref[pl.ds(0, S), :] = dy_ref[...]
        dpad_ref[pl.ds(S, PAD), :] = jnp.zeros((PAD, tc), F32)
        wv = w_ref[...]

        def step(ci, acc):
            r0 = pl.multiple_of(ci * CHUNK, CHUNK)
            past = _past_taps(xpad_ref, r0, width)
            ahead = _future_taps(dpad_ref, r0, width)
            d = ahead[0]
            dx = d * wv[width - 1:width, :]
            for j in range(1, width):
                dx = dx + ahead[j] * wv[width - 1 - j:width - j, :]
            dx_ref[pl.ds(r0, CHUNK), :] = dx.astype(dx_ref.dtype)
            return tuple(acc[k] + _colsum(past[k] * d) for k in range(width)) + (acc[width] + _colsum(d),)

        zero = jnp.zeros((1, tc), F32)
        acc = lax.fori_loop(0, S // CHUNK, step, (zero,) * (width + 1))
        for k in range(width):
            dw_ref[k:k + 1, :] = acc[k]
        db_ref[...] = acc[width]

    return _call(
        body,
        name=name,
        grid=(c_dim // tc,),
        in_specs=[
            pl.BlockSpec((S, tc), lambda j: (0, j)),
            pl.BlockSpec((S, tc), lambda j: (0, col0 // tc + j)),
            pl.BlockSpec((width, tc), lambda j: (0, j)),
        ],
        out_specs=[
            pl.BlockSpec((S, tc), lambda j: (0, j)),
            pl.BlockSpec((width, tc), lambda j: (0, j)),
            pl.BlockSpec((1, tc), lambda j: (0, j)),
        ],
        out_shape=[
            jax.ShapeDtypeStruct((S, c_dim), MXU_DTYPE),
            jax.ShapeDtypeStruct((width, c_dim), F32),
            jax.ShapeDtypeStruct((1, c_dim), F32),
        ],
        scratch_shapes=[pltpu.VMEM((S + PAD, tc), F32), pltpu.VMEM((S + PAD, tc), F32)],
        semantics=("parallel",),
        operands=(dy, src, w),
        side=side,
    )


SCAN_TC = 256


def _lru_gates(rxc, wa, wi, ba, bi, side=None):
    tm = 512

    def body(x_ref, wa_ref, wi_ref, ba_ref, bi_ref, r_ref, i_ref):
        xv = x_ref[...].astype(MXU_DTYPE)
        r_ref[...] = _sigmoid(jnp.dot(xv, wa_ref[...].astype(MXU_DTYPE), preferred_element_type=F32) + ba_ref[...])
        i_ref[...] = _sigmoid(jnp.dot(xv, wi_ref[...].astype(MXU_DTYPE), preferred_element_type=F32) + bi_ref[...])

    x_spec = pl.BlockSpec((tm, RNN_GROUP), lambda g, i: (i, g))
    w_spec = pl.BlockSpec((None, RNN_GROUP, RNN_GROUP), lambda g, i: (g, 0, 0))
    b_spec = pl.BlockSpec((1, RNN_GROUP), lambda g, i: (0, g))
    return _call(
        body,
        name="lru_gates",
        grid=(N_RNN_GROUPS, S // tm),
        in_specs=[x_spec, w_spec, w_spec, b_spec, b_spec],
        out_specs=[x_spec, x_spec],
        out_shape=[jax.ShapeDtypeStruct((S, D_RNN), F32)] * 2,
        semantics=("parallel", "parallel"),
        operands=(rxc, wa, wi, ba, bi),
        side=side,
    )


def _scan_down(a, u, row):
    for d in (1, 2, 4):
        a_s = jnp.where(row >= d, pltpu.roll(a, d, 0), 1.0)
        u_s = jnp.where(row >= d, pltpu.roll(u, d, 0), 0.0)
        u = a * u_s + u
        a = a * a_s
    return a, u


def _scan_up(a, u, row):
    for d in (1, 2, 4):
        a_s = jnp.where(row < 8 - d, pltpu.roll(a, 8 - d, 0), 1.0)
        u_s = jnp.where(row < 8 - d, pltpu.roll(u, 8 - d, 0), 0.0)
        u = a * u_s + u
        a = a * a_s
    return a, u


def _lru_scan_fwd(r, i, rxc, proj, lam, side=None):
    tc = SCAN_TC

    def body(r_ref, i_ref, x_ref, ry_ref, lam_ref, h_ref, y_ref):
        rate = LRU_C * _softplus(-lam_ref[...])
        row = lax.broadcasted_iota(jnp.int32, (8, tc), 0)

        def step(ci, carry):
            r0 = pl.multiple_of(ci * 16, 16)
            log_a = -rate * r_ref[pl.ds(r0, 16), :]
            a16 = jnp.exp(log_a)
            u16 = jnp.sqrt(_one_minus_exp(2.0 * log_a)) * (i_ref[pl.ds(r0, 16), :] * x_ref[pl.ds(r0, 16), :])
            hs = []
            for half in range(2):
                a_cum, h0 = _scan_down(a16[8 * half:8 * half + 8, :], u16[8 * half:8 * half + 8, :], row)
                h = a_cum * carry + h0
                carry = jnp.broadcast_to(h[7:8, :], (8, tc))
                hs.append(h)
            h16 = jnp.concatenate(hs, axis=0)
            h_ref[pl.ds(r0, 16), :] = h16
            y_ref[pl.ds(r0, 16), :] = (h16 * _gelu(ry_ref[pl.ds(r0, 16), :])[0]).astype(y_ref.dtype)
            return carry

        lax.fori_loop(0, S // 16, step, jnp.zeros((8, tc), F32))

    col = pl.BlockSpec((S, tc), lambda j: (0, j))
    return _call(
        body,
        name="lru_scan_fwd",
        grid=(D_RNN // tc,),
        in_specs=[col, col, col, pl.BlockSpec((S, tc), lambda j: (0, OFF_RY // tc + j)),
                  pl.BlockSpec((1, tc), lambda j: (0, j))],
        out_specs=[col, col],
        out_shape=[jax.ShapeDtypeStruct((S, D_RNN), F32), jax.ShapeDtypeStruct((S, D_RNN), MXU_DTYPE)],
        semantics=("parallel",),
        operands=(r, i, rxc, proj, lam),
        side=side,
    )


def _lru_scan_bwd(dy, proj, h, r, i, rxc, lam, side=None):
    tc = SCAN_TC

    def body(dy_ref, ry_ref, h_ref, r_ref, i_ref, x_ref, lam_ref,
             dry_ref, dzr_ref, dzi_ref, dx_ref, dba_ref, dbi_ref, dlam_ref, a_ref, dh_ref, hp_ref):
        lam_v = lam_ref[...]
        rate = LRU_C * _softplus(-lam_v)
        dlam_scale = LRU_C * _sigmoid(-lam_v)
        row = lax.broadcasted_iota(jnp.int32, (8, tc), 0)
        hp_ref[pl.ds(0, PAD), :] = jnp.zeros((PAD, tc), F32)
        hp_ref[pl.ds(PAD, S), :] = h_ref[...]
        a_ref[pl.ds(S, PAD), :] = jnp.zeros((PAD, tc), F32)

        def prep(ci, carry):
            r0 = pl.multiple_of(ci * CHUNK, CHUNK)
            a_ref[pl.ds(r0, CHUNK), :] = jnp.exp(-rate * r_ref[pl.ds(r0, CHUNK), :])
            ge, dge = _gelu(ry_ref[pl.ds(r0, CHUNK), :])
            dyv = dy_ref[pl.ds(r0, CHUNK), :]
            dh_ref[pl.ds(r0, CHUNK), :] = dyv * ge
            dry_ref[pl.ds(r0, CHUNK), :] = (dyv * h_ref[pl.ds(r0, CHUNK), :] * dge).astype(dry_ref.dtype)
            return carry

        lax.fori_loop(0, S // CHUNK, prep, 0)

        def step(ci, state):
            carry, dba, dbi, dlam = state
            r0 = pl.multiple_of(S - 16 - ci * 16, 16)
            a_ext = a_ref[pl.ds(r0, 24), :]
            a_next = pltpu.roll(a_ext, 23, 0)
            h_prev = pltpu.roll(hp_ref[pl.ds(r0, 24), :], 1, 0)
            dh16 = dh_ref[pl.ds(r0, 16), :]
            gs = [None, None]
            for half in (1, 0):
                lo = 8 * half
                c_cum, g0 = _scan_up(a_next[lo:lo + 8, :], dh16[lo:lo + 8, :], row)
                g = c_cum * carry + g0
                carry = jnp.broadcast_to(g[0:1, :], (8, tc))
                gs[half] = g
            g16 = jnp.concatenate(gs, axis=0)
            a16 = a_ext[0:16, :]
            r16 = r_ref[pl.ds(r0, 16), :]
            i16 = i_ref[pl.ds(r0, 16), :]
            x16 = x_ref[pl.ds(r0, 16), :]
            a2 = a16 * a16
            sq = jnp.sqrt(_one_minus_exp(-2.0 * rate * r16))
            dx_ref[pl.ds(r0, 16), :] = g16 * sq * i16
            dzi = g16 * sq * x16 * i16 * (1.0 - i16)
            dlog_a = g16 * h_prev[8:24, :] * a16 - g16 * i16 * x16 * a2 / sq
            dzr = -rate * dlog_a * r16 * (1.0 - r16)
            dzr_ref[pl.ds(r0, 16), :] = dzr.astype(dzr_ref.dtype)
            dzi_ref[pl.ds(r0, 16), :] = dzi.astype(dzi_ref.dtype)
            return carry, dba + _colsum(dzr), dbi + _colsum(dzi), dlam + _colsum(dlog_a * r16)

        zero = jnp.zeros((1, tc), F32)
        _, dba, dbi, dlam = lax.fori_loop(0, S // 16, step, (jnp.zeros((8, tc), F32), zero, zero, zero))
        dba_ref[...] = dba
        dbi_ref[...] = dbi
        dlam_ref[...] = dlam * dlam_scale

    col = pl.BlockSpec((S, tc), lambda j: (0, j))
    vec = pl.BlockSpec((1, tc), lambda j: (0, j))
    return _call(
        body,
        name="lru_scan_bwd",
        grid=(D_RNN // tc,),
        in_specs=[col, pl.BlockSpec((S, tc), lambda j: (0, OFF_RY // tc + j)), col, col, col, col, vec],
        out_specs=[col, col, col, col, vec, vec, vec],
        out_shape=[jax.ShapeDtypeStruct((S, D_RNN), MXU_DTYPE)] * 3 + [jax.ShapeDtypeStruct((S, D_RNN), F32)]
        + [jax.ShapeDtypeStruct((1, D_RNN), F32)] * 3,
        scratch_shapes=[pltpu.VMEM((S + PAD, tc), F32), pltpu.VMEM((S, tc), F32), pltpu.VMEM((S + PAD, tc), F32)],
        semantics=("parallel",),
        operands=(dy, proj, h, r, i, rxc, lam),
        side=side,
    )


def _lru_gate_wgrad(rxc, dzr, dzi, side=None):
    def body(x_ref, dzr_ref, dzi_ref, dwa_ref, dwi_ref):
        xv = x_ref[...].astype(MXU_DTYPE)
        dims = (((0,), (0,)), ((), ()))
        dwa_ref[...] = lax.dot_general(xv, dzr_ref[...], dims, preferred_element_type=F32)
        dwi_ref[...] = lax.dot_general(xv, dzi_ref[...], dims, preferred_element_type=F32)

    col = pl.BlockSpec((S, RNN_GROUP), lambda g: (0, g))
    w_spec = pl.BlockSpec((None, RNN_GROUP, RNN_GROUP), lambda g: (g, 0, 0))
    return _call(
        body,
        name="lru_gate_wgrad",
        grid=(N_RNN_GROUPS,),
        in_specs=[col, col, col],
        out_specs=[w_spec, w_spec],
        out_shape=[jax.ShapeDtypeStruct((N_RNN_GROUPS, RNN_GROUP, RNN_GROUP), F32)] * 2,
        semantics=("parallel",),
        operands=(rxc, dzr, dzi),
        side=side,
    )


def _lru_gate_xgrad(dzr, dzi, wa, wi, dx_in, side=None):
    tm = 512

    def body(dzr_ref, dzi_ref, wa_ref, wi_ref, dx_ref, o_ref):
        dims = (((1,), (1,)), ((), ()))
        o_ref[...] = (dx_ref[...]
                      + lax.dot_general(dzr_ref[...], wa_ref[...].astype(MXU_DTYPE), dims, preferred_element_type=F32)
                      + lax.dot_general(dzi_ref[...], wi_ref[...].astype(MXU_DTYPE), dims, preferred_element_type=F32))

    x_spec = pl.BlockSpec((tm, RNN_GROUP), lambda g, i: (i, g))
    w_spec = pl.BlockSpec((None, RNN_GROUP, RNN_GROUP), lambda g, i: (g, 0, 0))
    return _call(
        body,
        name="lru_gate_xgrad",
        grid=(N_RNN_GROUPS, S // tm),
        in_specs=[x_spec, x_spec, w_spec, w_spec, x_spec],
        out_specs=x_spec,
        out_shape=jax.ShapeDtypeStruct((S, D_RNN), F32),
        semantics=("parallel", "parallel"),
        operands=(dzr, dzi, wa, wi, dx_in),
        side=side,
    )


def _gate_fwd(y_attn, y_rnn, proj, b_gate, side=None):
    t = 512

    def body(ya_ref, yr_ref, ga_ref, gr_ref, ba_ref, br_ref, o_ref):
        o_ref[...] = (_sigmoid(ga_ref[...] + ba_ref[...]) * ya_ref[...]
                      + _sigmoid(gr_ref[...] + br_ref[...]) * yr_ref[...]).astype(o_ref.dtype)

    tile = pl.BlockSpec((t, t), lambda i, j: (i, j))
    return _call(
        body,
        name="gate_fwd",
        grid=(S // t, D // t),
        in_specs=[tile, tile,
                  pl.BlockSpec((t, t), lambda i, j: (i, OFF_GA // t + j)),
                  pl.BlockSpec((t, t), lambda i, j: (i, OFF_GR // t + j)),
                  pl.BlockSpec((1, t), lambda i, j: (0, j)),
                  pl.BlockSpec((1, t), lambda i, j: (0, D // t + j))],
        out_specs=tile,
        out_shape=jax.ShapeDtypeStruct((S, D), MXU_DTYPE),
        semantics=("parallel", "parallel"),
        operands=(y_attn, y_rnn, proj, proj, b_gate, b_gate),
        side=side,
    )


def _gate_bwd(dmix, y_attn, y_rnn, proj, b_gate, side=None):
    t = 512

    def body(dm_ref, ya_ref, yr_ref, ga_ref, gr_ref, ba_ref, br_ref,
             dya_ref, dyr_ref, dga_ref, dgr_ref, dba_ref, dbr_ref):
        @pl.when(pl.program_id(1) == 0)
        def _():
            dba_ref[...] = jnp.zeros_like(dba_ref)
            dbr_ref[...] = jnp.zeros_like(dbr_ref)

        dm = dm_ref[...]
        ga = _sigmoid(ga_ref[...] + ba_ref[...])
        gr = _sigmoid(gr_ref[...] + br_ref[...])
        dya_ref[...] = (dm * ga).astype(dya_ref.dtype)
        dyr_ref[...] = (dm * gr).astype(dyr_ref.dtype)
        dga = dm * ya_ref[...] * ga * (1.0 - ga)
        dgr = dm * yr_ref[...] * gr * (1.0 - gr)
        dga_ref[...] = dga.astype(dga_ref.dtype)
        dgr_ref[...] = dgr.astype(dgr_ref.dtype)
        dba_ref[...] += _colsum(dga)
        dbr_ref[...] += _colsum(dgr)

    tile = pl.BlockSpec((t, t), lambda j, i: (i, j))
    vec = pl.BlockSpec((1, t), lambda j, i: (0, j))
    return _call(
        body,
        name="gate_bwd",
        grid=(D // t, S // t),
        in_specs=[tile, tile, tile,
                  pl.BlockSpec((t, t), lambda j, i: (i, OFF_GA // t + j)),
                  pl.BlockSpec((t, t), lambda j, i: (i, OFF_GR // t + j)),
                  vec,
                  pl.BlockSpec((1, t), lambda j, i: (0, D // t + j))],
        out_specs=[tile, tile, tile, tile, vec, vec],
        out_shape=[jax.ShapeDtypeStruct((S, D), MXU_DTYPE)] * 4 + [jax.ShapeDtypeStruct((1, D), F32)] * 2,
        semantics=("parallel", "arbitrary"),
        operands=(dmix, y_attn, y_rnn, proj, proj, b_gate, b_gate),
        side=side,
    )


LN_TM = 256


def _ln_stats(pre):
    mu = jnp.mean(pre, axis=-1, keepdims=True)
    xc = pre - mu
    rstd = lax.rsqrt(jnp.mean(xc * xc, axis=-1, keepdims=True) + LN_EPS)
    return xc * rstd, rstd


def _ln_input_grad(dy, xhat, rstd, g):
    dyg = dy * g
    return rstd * (dyg - jnp.mean(dyg, axis=-1, keepdims=True)
                   - xhat * jnp.mean(dyg * xhat, axis=-1, keepdims=True))


def _ln_fwd(res, branch, g, b, side=None):
    def body(res_ref, br_ref, g_ref, b_ref, y_ref, yb_ref, xhat_ref, rstd_ref):
        xhat, rstd = _ln_stats(ALPHA * res_ref[...] + br_ref[...])
        y = xhat * g_ref[...] + b_ref[...]
        y_ref[...] = y
        yb_ref[...] = y.astype(yb_ref.dtype)
        xhat_ref[...] = xhat
        rstd_ref[...] = rstd

    tile = pl.BlockSpec((LN_TM, D), lambda i: (i, 0))
    vec = pl.BlockSpec((1, D), lambda i: (0, 0))
    return _call(
        body,
        name="ln_fwd",
        grid=(S // LN_TM,),
        in_specs=[tile, tile, vec, vec],
        out_specs=[tile, tile, tile, pl.BlockSpec((LN_TM, 1), lambda i: (i, 0))],
        out_shape=[jax.ShapeDtypeStruct((S, D), F32), jax.ShapeDtypeStruct((S, D), MXU_DTYPE),
                   jax.ShapeDtypeStruct((S, D), F32), jax.ShapeDtypeStruct((S, 1), F32)],
        semantics=("parallel",),
        operands=(res, branch, g, b),
        side=side,
    )


def _ln_bwd(dy_a, dy_b, xhat, rstd, g, side=None):
    def body(da_ref, db_in_ref, xhat_ref, rstd_ref, g_ref, dp_ref, dpb_ref, dg_ref, db_ref):
        @pl.when(pl.program_id(0) == 0)
        def _():
            dg_ref[...] = jnp.zeros_like(dg_ref)
            db_ref[...] = jnp.zeros_like(db_ref)

        dy = da_ref[...] + ALPHA * db_in_ref[...]
        xhat = xhat_ref[...]
        dp = _ln_input_grad(dy, xhat, rstd_ref[...], g_ref[...])
        dp_ref[...] = dp
        dpb_ref[...] = dp.astype(dpb_ref.dtype)
        dg_ref[...] += _colsum(dy * xhat)
        db_ref[...] += _colsum(dy)

    tile = pl.BlockSpec((LN_TM, D), lambda i: (i, 0))
    vec = pl.BlockSpec((1, D), lambda i: (0, 0))
    return _call(
        body,
        name="ln_bwd",
        grid=(S // LN_TM,),
        in_specs=[tile, tile, tile, pl.BlockSpec((LN_TM, 1), lambda i: (i, 0)), vec],
        out_specs=[tile, tile, vec, vec],
        out_shape=[jax.ShapeDtypeStruct((S, D), F32), jax.ShapeDtypeStruct((S, D), MXU_DTYPE),
                   jax.ShapeDtypeStruct((1, D), F32), jax.ShapeDtypeStruct((1, D), F32)],
        semantics=("arbitrary",),
        operands=(dy_a, dy_b, xhat, rstd, g),
        side=side,
    )


def _ln_loss_bwd(res, branch, g, b, target, side=None):
    def body(res_ref, br_ref, g_ref, b_ref, t_ref, loss_ref, dp_ref, dpb_ref, dg_ref, db_ref):
        @pl.when(pl.program_id(0) == 0)
        def _():
            loss_ref[...] = jnp.zeros_like(loss_ref)
            dg_ref[...] = jnp.zeros_like(dg_ref)
            db_ref[...] = jnp.zeros_like(db_ref)

        xhat, rstd = _ln_stats(ALPHA * res_ref[...] + br_ref[...])
        gv = g_ref[...]
        err = xhat * gv + b_ref[...] - t_ref[...]
        loss_ref[...] += (0.5 / D) * jnp.sum(_colsum(err * err), axis=1, keepdims=True)
        dy = err * (1.0 / D)
        dp = _ln_input_grad(dy, xhat, rstd, gv)
        dp_ref[...] = dp
        dpb_ref[...] = dp.astype(dpb_ref.dtype)
        dg_ref[...] += _colsum(dy * xhat)
        db_ref[...] += _colsum(dy)

    tile = pl.BlockSpec((LN_TM, D), lambda i: (i, 0))
    vec = pl.BlockSpec((1, D), lambda i: (0, 0))
    return _call(
        body,
        name="ln_loss_bwd",
        grid=(S // LN_TM,),
        in_specs=[tile, tile, vec, vec, tile],
        out_specs=[pl.BlockSpec((1, 1), lambda i: (0, 0)), tile, tile, vec, vec],
        out_shape=[jax.ShapeDtypeStruct((1, 1), F32), jax.ShapeDtypeStruct((S, D), F32),
                   jax.ShapeDtypeStruct((S, D), MXU_DTYPE),
                   jax.ShapeDtypeStruct((1, D), F32), jax.ShapeDtypeStruct((1, D), F32)],
        semantics=("arbitrary",),
        operands=(res, branch, g, b, target),
        side=side,
    )


FFN_TC = 256


def _ffn_act_fwd(up, gpre, w, b, side=None):
    tc = FFN_TC

    def body(up_ref, x_ref, w_ref, b_ref, o_ref, xpad_ref):
        xpad_ref[pl.ds(0, PAD), :] = jnp.zeros((PAD, tc), F32)
        xpad_ref[pl.ds(PAD, S), :] = x_ref[...]
        wv = w_ref[...]
        bv = b_ref[...]

        def step(ci, carry):
            r0 = pl.multiple_of(ci * CHUNK, CHUNK)
            taps = _past_taps(xpad_ref, r0, FFN_CONV_W)
            gate = bv + taps[0] * wv[0:1, :] + taps[1] * wv[1:2, :] + taps[2] * wv[2:3, :]
            o_ref[pl.ds(r0, CHUNK), :] = (_gelu(gate)[0] * up_ref[pl.ds(r0, CHUNK), :]).astype(o_ref.dtype)
            return carry

        lax.fori_loop(0, S // CHUNK, step, 0)

    col = pl.BlockSpec((S, tc), lambda j: (0, j))
    return _call(
        body,
        name="ffn_act_fwd",
        grid=(D_FF // tc,),
        in_specs=[col, col, pl.BlockSpec((FFN_CONV_W, tc), lambda j: (0, j)), pl.BlockSpec((1, tc), lambda j: (0, j))],
        out_specs=col,
        out_shape=jax.ShapeDtypeStruct((S, D_FF), MXU_DTYPE),
        scratch_shapes=[pltpu.VMEM((S + PAD, tc), F32)],
        semantics=("parallel",),
        operands=(up, gpre, w, b),
        side=side,
    )


def _ffn_act_bwd(dfin, up, gpre, w, b, side=None):
    tc = FFN_TC
    width = FFN_CONV_W

    def body(df_ref, up_ref, x_ref, w_ref, b_ref, dup_ref, dx_ref, dw_ref, db_ref, xpad_ref, dpad_ref):
        xpad_ref[pl.ds(0, PAD), :] = jnp.zeros((PAD, tc), F32)
        xpad_ref[pl.ds(PAD, S), :] = x_ref[...]
        dpad_ref[pl.ds(S, PAD), :] = jnp.zeros((PAD, tc), F32)
        wv = w_ref[...]
        bv = b_ref[...]

        def gate_grad(ci, acc):
            r0 = pl.multiple_of(ci * CHUNK, CHUNK)
            taps = _past_taps(xpad_ref, r0, width)
            gate = bv + taps[0] * wv[0:1, :] + taps[1] * wv[1:2, :] + taps[2] * wv[2:3, :]
            ge, dge = _gelu(gate)
            df = df_ref[pl.ds(r0, CHUNK), :]
            dup_ref[pl.ds(r0, CHUNK), :] = (df * ge).astype(dup_ref.dtype)
            d = df * up_ref[pl.ds(r0, CHUNK), :] * dge
            dpad_ref[pl.ds(r0, CHUNK), :] = d
            return tuple(acc[k] + _colsum(taps[k] * d) for k in range(width)) + (acc[width] + _colsum(d),)

        zero = jnp.zeros((1, tc), F32)
        acc = lax.fori_loop(0, S // CHUNK, gate_grad, (zero,) * (width + 1))
        for k in range(width):
            dw_ref[k:k + 1, :] = acc[k]
        db_ref[...] = acc[width]

        def input_grad(ci, carry):
            r0 = pl.multiple_of(ci * CHUNK, CHUNK)
            ahead = _future_taps(dpad_ref, r0, width)
            dx = ahead[0] * wv[2:3, :] + ahead[1] * wv[1:2, :] + ahead[2] * wv[0:1, :]
            dx_ref[pl.ds(r0, CHUNK), :] = dx.astype(dx_ref.dtype)
            return carry

        lax.fori_loop(0, S // CHUNK, input_grad, 0)

    col = pl.BlockSpec((S, tc), lambda j: (0, j))
    w_spec = pl.BlockSpec((width, tc), lambda j: (0, j))
    vec = pl.BlockSpec((1, tc), lambda j: (0, j))
    return _call(
        body,
        name="ffn_act_bwd",
        grid=(D_FF // tc,),
        in_specs=[col, col, col, w_spec, vec],
        out_specs=[col, col, w_spec, vec],
        out_shape=[jax.ShapeDtypeStruct((S, D_FF), MXU_DTYPE)] * 2
        + [jax.ShapeDtypeStruct((width, D_FF), F32), jax.ShapeDtypeStruct((1, D_FF), F32)],
        scratch_shapes=[pltpu.VMEM((S + PAD, tc), F32), pltpu.VMEM((S + PAD, tc), F32)],
        semantics=("parallel",),
        operands=(dfin, up, gpre, w, b),
        side=side,
    )


def _adamw_update(w, g, m, v):
    m = ADAM_B1 * m + (1.0 - ADAM_B1) * g
    v = ADAM_B2 * v + (1.0 - ADAM_B2) * (g * g)
    m_hat = m / (1.0 - ADAM_B1 ** ADAM_STEP)
    v_hat = v / (1.0 - ADAM_B2 ** ADAM_STEP)
    delta = -ADAM_LR * (m_hat / (jnp.sqrt(v_hat) + ADAM_EPS) + ADAM_WD * w)
    return delta, m, v


def _reduce_adamw(w, m, v, g_own, recv, *, tr, name, side=None):
    r_dim, c_dim = w.shape

    def body(w_ref, m_ref, v_ref, g_ref, recv_ref, grad_ref, delta_ref, nm_ref, nv_ref):
        g = g_ref[...]
        for k in range(1, N_DEV):
            g = g + recv_ref[k].astype(F32)
        delta, nm, nv = _adamw_update(w_ref[...], g, m_ref[...], v_ref[...])
        grad_ref[...] = g
        delta_ref[...] = delta
        nm_ref[...] = nm
        nv_ref[...] = nv

    tile = pl.BlockSpec((tr, c_dim), lambda i: (i, 0))
    return _call(
        body,
        name=name,
        grid=(r_dim // tr,),
        in_specs=[tile, tile, tile, tile, pl.BlockSpec((N_DEV, tr, c_dim), lambda i: (0, i, 0))],
        out_specs=[tile] * 4,
        out_shape=[jax.ShapeDtypeStruct((r_dim, c_dim), F32)] * 4,
        semantics=("parallel",),
        operands=(w, m, v, g_own, recv),
        side=side,
    )


def _adamw_flat(w, m, v, g, side=None):
    r_dim = w.shape[0]
    tr = r_dim // 4 if r_dim % 32 == 0 else r_dim

    def body(w_ref, m_ref, v_ref, g_ref, delta_ref, nm_ref, nv_ref):
        delta, nm, nv = _adamw_update(w_ref[...], g_ref[...], m_ref[...], v_ref[...])
        delta_ref[...] = delta
        nm_ref[...] = nm
        nv_ref[...] = nv

    tile = pl.BlockSpec((tr, 128), lambda i: (i, 0))
    return _call(
        body,
        name="adamw_small",
        grid=(r_dim // tr,),
        in_specs=[tile] * 4,
        out_specs=[tile] * 3,
        out_shape=[jax.ShapeDtypeStruct((r_dim, 128), F32)] * 3,
        semantics=("parallel",),
        operands=(w, m, v, g),
        side=side,
    )


def _coords():
    return lax.axis_index("x"), lax.axis_index("y"), lax.axis_index("c")


def _flip(coord, bit):
    return 1 - coord if bit else coord


def _relative(k):
    x, y, c = _coords()
    return _flip(x, k & 4), _flip(y, k & 2), _flip(c, k & 1)


def _index(pos):
    return 4 * pos[0] + 2 * pos[1] + pos[2]


FAR = (4, 2, 6)
AG_US_PER_MB = 44.0
RS_US_PER_MB = 87.0
ROW_ALIGN = 16


def _chunks(items, cursor, us, us_per_mb, through=None):
    budget = float("inf") if us is None else us / us_per_mb * 2 ** 20
    names = list(items)
    if through is not None:
        names = names[:names.index(through) + 1]
    chunks = []
    for name in names:
        arr = items[name]
        r_dim, c_dim = arr.shape[-2:]
        row_bytes = c_dim * arr.dtype.itemsize
        while cursor[name] < r_dim and budget > 0:
            rows = r_dim - cursor[name]
            if r_dim > ROW_ALIGN and budget < rows * row_bytes:
                rows = min(rows, max(ROW_ALIGN, int(budget // row_bytes) // ROW_ALIGN * ROW_ALIGN))
            chunks.append((name, cursor[name], rows))
            cursor[name] += rows
            budget -= rows * row_bytes
    return chunks


class _Gather:
    def __init__(self, shards):
        self.shards = dict(shards)
        self.bufs = {n: None for n in self.shards}
        self.cursor = {n: 0 for n in self.shards}

    def take(self, us=None, through=None):
        chunks = _chunks(self.shards, self.cursor, us, AG_US_PER_MB, through)
        return _GatherSide(self, chunks) if chunks else None

    def get(self, name):
        chunks = _chunks(self.shards, self.cursor, None, AG_US_PER_MB, through=name)
        if chunks:
            _run_side(_GatherSide(self, chunks), "gather_" + name)
        return self.bufs[name]


class _GatherSide:
    def __init__(self, owner, chunks):
        self.owner, self.chunks = owner, chunks
        self.names = list(dict.fromkeys(n for n, _, _ in chunks))
        old = [n for n in self.names if owner.bufs[n] is not None]
        self.operands = [owner.shards[n] for n in self.names] + [owner.bufs[n] for n in old]
        self.out_shape = [jax.ShapeDtypeStruct((N_DEV,) + owner.shards[n].shape, owner.shards[n].dtype)
                          for n in self.names]
        self.aliases = {len(self.names) + i: self.names.index(n) for i, n in enumerate(old)}
        self.sems = [pltpu.SemaphoreType.DMA((7 * len(chunks),)), pltpu.SemaphoreType.DMA((7 * len(chunks),)),
                     pltpu.SemaphoreType.DMA((len(chunks),))]

    def _copy(self, ins, outs, sems, ci, k, block, to, from_shard=False):
        name, r0, rows = self.chunks[ci]
        w = self.names.index(name)
        slot = outs[w].at[_index(block), pl.ds(r0, rows)]
        return pltpu.make_async_remote_copy(
            src_ref=ins[w].at[pl.ds(r0, rows)] if from_shard else slot, dst_ref=slot,
            send_sem=sems[0].at[7 * ci + k], recv_sem=sems[1].at[7 * ci + k], device_id=to, device_id_type=MESH)

    def _own(self, ins, outs, sems, ci):
        name, r0, rows = self.chunks[ci]
        w = self.names.index(name)
        return pltpu.make_async_copy(ins[w].at[pl.ds(r0, rows)], outs[w].at[_index(_relative(0)), pl.ds(r0, rows)],
                                     sems[2].at[ci])

    def start(self, ins, outs, sems):
        me, sibling = _relative(0), _relative(1)
        for ci in range(len(self.chunks)):
            self._own(ins, outs, sems, ci).start()
        for j, k in enumerate(FAR):
            for ci in range(len(self.chunks)):
                self._copy(ins, outs, sems, ci, 1 + j, me, _relative(k), from_shard=True).start()
        for ci in range(len(self.chunks)):
            self._copy(ins, outs, sems, ci, 0, me, sibling, from_shard=True).start()

    def finish(self, ins, outs, sems):
        me, sibling = _relative(0), _relative(1)
        n = len(self.chunks)
        for j, k in enumerate(FAR):
            for ci in range(n):
                self._copy(ins, outs, sems, ci, 1 + j, _relative(k), me).wait_recv()
                self._copy(ins, outs, sems, ci, 4 + j, _relative(k), sibling).start()
        for ci in range(n):
            self._copy(ins, outs, sems, ci, 0, sibling, me).wait_recv()
        for j, k in enumerate(FAR):
            for ci in range(n):
                self._copy(ins, outs, sems, ci, 4 + j, _relative(k | 1), me).wait_recv()
        for ci in range(n):
            self._copy(ins, outs, sems, ci, 0, me, sibling, from_shard=True).wait_send()
            for j, k in enumerate(FAR):
                self._copy(ins, outs, sems, ci, 1 + j, me, _relative(k), from_shard=True).wait_send()
                self._copy(ins, outs, sems, ci, 4 + j, _relative(k), sibling).wait_send()
            self._own(ins, outs, sems, ci).wait()

    def done(self, results):
        for n, buf in zip(self.names, results):
            self.owner.bufs[n] = buf


class _Scatter:
    def __init__(self, me):
        self.me = me
        self.sends, self.owns, self.bufs, self.cursor = {}, {}, {}, {}

    def add(self, name, blocks):
        self.sends[name] = blocks.astype(BF16)
        self.owns[name] = lax.dynamic_index_in_dim(blocks, self.me, axis=0, keepdims=False)
        self.bufs[name] = None
        self.cursor[name] = 0

    def take(self, us=None):
        chunks = _chunks(self.sends, self.cursor, us, RS_US_PER_MB)
        return _ScatterSide(self, chunks) if chunks else None

    def get(self, name):
        chunks = _chunks(self.sends, self.cursor, None, RS_US_PER_MB, through=name)
        if chunks:
            _run_side(_ScatterSide(self, chunks), "scatter_" + name)
        return self.owns[name], self.bufs[name]


class _ScatterSide:
    ORDER = (4, 2, 6, 1, 5, 3, 7)

    def __init__(self, owner, chunks):
        self.owner, self.chunks = owner, chunks
        self.names = list(dict.fromkeys(n for n, _, _ in chunks))
        old = [n for n in self.names if owner.bufs[n] is not None]
        self.operands = [owner.sends[n] for n in self.names] + [owner.bufs[n] for n in old]
        self.out_shape = [jax.ShapeDtypeStruct(owner.sends[n].shape, BF16) for n in self.names]
        self.aliases = {len(self.names) + i: self.names.index(n) for i, n in enumerate(old)}
        self.sems = [pltpu.SemaphoreType.DMA((7 * len(chunks),)), pltpu.SemaphoreType.DMA((7 * len(chunks),))]

    def _copy(self, ins, outs, sems, ci, k):
        name, r0, rows = self.chunks[ci]
        w = self.names.index(name)
        return pltpu.make_async_remote_copy(
            src_ref=ins[w].at[_index(_relative(k)), pl.ds(r0, rows)], dst_ref=outs[w].at[k, pl.ds(r0, rows)],
            send_sem=sems[0].at[7 * ci + k - 1], recv_sem=sems[1].at[7 * ci + k - 1],
            device_id=_relative(k), device_id_type=MESH)

    def start(self, ins, outs, sems):
        for ci in range(len(self.chunks)):
            for k in self.ORDER:
                self._copy(ins, outs, sems, ci, k).start()

    def finish(self, ins, outs, sems):
        for ci in range(len(self.chunks)):
            for k in self.ORDER:
                self._copy(ins, outs, sems, ci, k).wait()

    def done(self, results):
        for n, buf in zip(self.names, results):
            self.owner.bufs[n] = buf


def _all_reduce_small(v):
    r_dim = v.shape[1]

    def body(v_ref, out_ref, stage_ref, a_send, a_recv, b_send, b_recv):
        me = _index(_relative(0))

        def to_owner(k):
            return pltpu.make_async_remote_copy(
                src_ref=v_ref.at[_index(_relative(k))], dst_ref=stage_ref.at[k],
                send_sem=a_send.at[k - 1], recv_sem=a_recv.at[k - 1], device_id=_relative(k), device_id_type=MESH)

        def to_all(k):
            return pltpu.make_async_remote_copy(
                src_ref=out_ref.at[me], dst_ref=out_ref.at[me],
                send_sem=b_send.at[k - 1], recv_sem=b_recv.at[k - 1], device_id=_relative(k), device_id_type=MESH)

        def from_owner(k):
            slot = out_ref.at[_index(_relative(k))]
            return pltpu.make_async_remote_copy(
                src_ref=slot, dst_ref=slot,
                send_sem=b_send.at[k - 1], recv_sem=b_recv.at[k - 1], device_id=_relative(k), device_id_type=MESH)

        for k in range(1, N_DEV):
            to_owner(k).start()
        acc = v_ref[me]
        for k in range(1, N_DEV):
            to_owner(k).wait_recv()
            acc = acc + stage_ref[k]
        out_ref[me] = acc
        for k in range(1, N_DEV):
            to_all(k).start()
        for k in range(1, N_DEV):
            from_owner(k).wait_recv()
        for k in range(1, N_DEV):
            to_owner(k).wait_send()
            to_all(k).wait_send()

    vmem = pl.BlockSpec(memory_space=pltpu.VMEM)
    return pl.pallas_call(
        body,
        name="all_reduce_small",
        in_specs=[vmem],
        out_specs=vmem,
        out_shape=jax.ShapeDtypeStruct(v.shape, F32),
        scratch_shapes=[pltpu.VMEM((N_DEV, r_dim, 128), F32)] + [pltpu.SemaphoreType.DMA((N_DEV - 1,))] * 4,
        compiler_params=pltpu.CompilerParams(vmem_limit_bytes=VMEM_LIMIT),
    )(v)


def _pack(arrays, total=None):
    parts = []
    for a in arrays:
        flat = a.reshape(-1)
        parts.append(jnp.pad(flat, (0, -flat.size % FLAT_TILE)))
    size = sum(p.size for p in parts)
    if total is not None:
        parts.append(jnp.zeros((total - size,), F32))
    return jnp.concatenate(parts)


def _unpack(flat, shapes):
    out, off = [], 0
    for shape in shapes:
        size = math.prod(shape)
        out.append(flat[off:off + size].reshape(shape))
        off += size + (-size % FLAT_TILE)
    return out


def _block_diag(w):
    w4 = w.reshape(N_RNN_GROUPS, 4, RNN_BLOCK_W, RNN_BLOCK_W)
    eye = jnp.eye(4, dtype=w.dtype)
    return (w4[:, :, :, None, :] * eye[None, :, None, :, None]).reshape(N_RNN_GROUPS, RNN_GROUP, RNN_GROUP)


def _diag_blocks(wg):
    w5 = wg.reshape(N_RNN_GROUPS, 4, RNN_BLOCK_W, 4, RNN_BLOCK_W)
    return jnp.stack([w5[:, b, :, b, :] for b in range(4)], axis=1).reshape(16, RNN_BLOCK_W, RNN_BLOCK_W)


def _heads_major(t, n_heads):
    return t.reshape(S, n_heads, HEAD_DIM).transpose(1, 0, 2)


def _heads_minor(t):
    return t.transpose(1, 0, 2).reshape(S, t.shape[0] * HEAD_DIM)


def _natural(gathered, how):
    n, r, c = gathered.shape
    if how == "rows":
        return gathered.reshape(n * r, c)
    return gathered.transpose(1, 0, 2).reshape(r, n * c)


def _blocks(full, how):
    if how == "rows":
        return full.reshape(N_DEV, full.shape[0] // N_DEV, full.shape[1])
    return full.reshape(full.shape[0], N_DEV, full.shape[1] // N_DEV).transpose(1, 0, 2)


def _forward_backward(x2, target, small, gather, scatter):
    xb = x2.astype(MXU_DTYPE)
    w_in = _natural(gather.get("w_in"), "cols")
    proj = _mm(xb, w_in, tm=1024, tn=512, tk=D, name="proj", side=gather.take(110))

    q = _heads_major(proj[:, :OFF_K], N_KV * GROUP).reshape(N_KV, GROUP, S, HEAD_DIM).astype(MXU_DTYPE)
    front = ((0, 0), (BLOCK, 0), (0, 0))
    kp = jnp.pad(_heads_major(proj[:, OFF_K:OFF_V], N_KV), front).astype(MXU_DTYPE)
    vp = jnp.pad(_heads_major(proj[:, OFF_V:OFF_RX], N_KV), front).astype(MXU_DTYPE)
    sink_rows = jnp.repeat(small["attn_sinks"].reshape(N_KV, GROUP, 1), BLOCK, axis=1).reshape(N_KV, GROUP * BLOCK, 1)
    o_hm = _attn_fwd(q, kp, vp, sink_rows, side=gather.take(120))
    o = _heads_minor(o_hm.reshape(N_KV * GROUP, S, HEAD_DIM))

    rconv_w = _natural(gather.get("rnn_conv_w"), "cols")
    rxc = _conv_fwd(proj, OFF_RX, rconv_w, small["rnn_conv_b"], tc=512, name="rnn_conv_fwd", side=gather.take(18))
    r, i = _lru_gates(rxc, small["lru_wa"], small["lru_wi"], small["lru_ba"], small["lru_bi"], side=gather.take(33))
    h, yrin = _lru_scan_fwd(r, i, rxc, proj, small["lru_lambda"], side=gather.take(53))

    w_ap = _natural(gather.get("w_attn_proj"), "rows")
    w_rp = _natural(gather.get("w_rnn_proj"), "rows")
    y_attn = _mm(o, w_ap, tm=1024, tn=1024, tk=D, name="attn_proj", side=gather.take(22))
    y_rnn = _mm(yrin, w_rp, tm=1024, tn=1024, tk=D_RNN, name="rnn_proj", side=gather.take(27))
    mixin = _gate_fwd(y_attn, y_rnn, proj, small["b_gate"], side=gather.take(25))
    w_out = _natural(gather.get("w_out"), "rows")
    mix = _mm(mixin, w_out, tm=1024, tn=1024, tk=D, name="mix_out", side=gather.take(22))
    x1, x1b, xhat1, rstd1 = _ln_fwd(x2, mix, small["ln1_g"], small["ln1_b"], side=gather.take(23))

    w_up = gather.get("ffn_w_up")
    up = _mm(x1b, w_up, tm=1024, tn=768, tk=D, b_block=768, name="ffn_up", side=gather.take(58))
    w_gate = gather.get("ffn_w_gate")
    gpre = _mm(x1b, w_gate, tm=1024, tn=768, tk=D, b_block=768, name="ffn_gate", side=gather.take(58))
    fconv_w = _natural(gather.get("ffn_conv_w"), "cols")
    fin = _ffn_act_fwd(up, gpre, fconv_w, small["ffn_conv_b"], side=gather.take())
    w_down = _natural(gather.get("ffn_w_down"), "rows")
    f = _mm(fin, w_down, tm=1024, tn=1024, tk=2048, name="ffn_down")
    loss, dpre2, dpre2b, d_ln2_g, d_ln2_b = _ln_loss_bwd(x1, f, small["ln2_g"], small["ln2_b"], target)

    grads = {"ln2_g": d_ln2_g, "ln2_b": d_ln2_b}
    scatter.add("ffn_w_down", _blocks(_mm(fin, dpre2b, ta=True, tm=1024, tn=1024, tk=S, name="d_ffn_w_down"), "rows"))
    dfin = _mm(dpre2b, w_down, tb=True, tm=1024, tn=1024, tk=D, name="d_fin", side=scatter.take(57))
    dup, dgpre, grads["ffn_conv_w"], grads["ffn_conv_b"] = _ffn_act_bwd(
        dfin, up, gpre, fconv_w, small["ffn_conv_b"], side=scatter.take(85))
    g_up = _mm(x1b, dup, ta=True, tm=1024, tn=1024, tk=S, name="d_ffn_w_up", side=scatter.take(57))
    scatter.add("ffn_w_up", _blocks(g_up, "cols"))
    g_gate = _mm(x1b, dgpre, ta=True, tm=1024, tn=1024, tk=S, name="d_ffn_w_gate", side=scatter.take(56))
    scatter.add("ffn_w_gate", _blocks(g_gate, "cols"))
    dx1 = _mm(dup, w_up, tb=True, tm=1024, tn=1024, tk=768, b_block=768, name="d_x1_up", side=scatter.take(68))
    dx1 = _mm(dgpre, w_gate, tb=True, tm=1024, tn=1024, tk=768, b_block=768, add=dx1, name="d_x1_gate",
              side=scatter.take(70))
    dpre1, dpre1b, grads["ln1_g"], grads["ln1_b"] = _ln_bwd(dx1, dpre2, xhat1, rstd1, small["ln1_g"],
                                                            side=scatter.take(24))

    g_out = _mm(mixin, dpre1b, ta=True, tm=1024, tn=1024, tk=S, name="d_w_out", side=scatter.take(26))
    scatter.add("w_out", _blocks(g_out, "rows"))
    dmix = _mm(dpre1b, w_out, tb=True, tm=1024, tn=1024, tk=D, name="d_mixin", side=scatter.take(22))
    dya, dyr, dgl_a, dgl_r, db_a, db_r = _gate_bwd(dmix, y_attn, y_rnn, proj, small["b_gate"], side=scatter.take(36))
    grads["b_gate"] = jnp.concatenate([db_a, db_r], axis=1)
    g_ap = _mm(o, dya, ta=True, tm=1024, tn=1024, tk=S, name="d_w_attn_proj", side=scatter.take(38))
    scatter.add("w_attn_proj", _blocks(g_ap, "rows"))
    g_rp = _mm(yrin, dyr, ta=True, tm=1280, tn=1024, tk=S, name="d_w_rnn_proj", side=scatter.take(27))
    scatter.add("w_rnn_proj", _blocks(g_rp, "rows"))
    do = _mm(dya, w_ap, tb=True, tm=1024, tn=1024, tk=D, out_dtype=MXU_DTYPE, name="d_o", side=scatter.take(22))
    dyrin = _mm(dyr, w_rp, tb=True, tm=1024, tn=1280, tk=D, name="d_yrin", side=scatter.take(27))

    dry, dzr, dzi, drxc_in, grads["lru_ba"], grads["lru_bi"], grads["lru_lambda"] = _lru_scan_bwd(
        dyrin, proj, h, r, i, rxc, small["lru_lambda"], side=scatter.take(94))
    grads["lru_wa"], grads["lru_wi"] = _lru_gate_wgrad(rxc, dzr, dzi, side=scatter.take(22))
    drxc = _lru_gate_xgrad(dzr, dzi, small["lru_wa"], small["lru_wi"], drxc_in, side=scatter.take(33))
    drx, grads["rnn_conv_w"], grads["rnn_conv_b"] = _conv_bwd(drxc, proj, OFF_RX, rconv_w, tc=512,
                                                             name="rnn_conv_bwd", side=scatter.take(29))

    do_hm = _heads_major(do, N_KV * GROUP).reshape(N_KV, GROUP, S, HEAD_DIM)
    dq, dk, dv, dsink = _attn_bwd(q, kp, vp, sink_rows, do_hm, side=scatter.take(220))
    grads["attn_sinks"] = dsink.reshape(1, N_KV * GROUP)
    dproj = jnp.concatenate([
        _heads_minor(dq.reshape(N_KV * GROUP, S, HEAD_DIM)),
        _heads_minor(dk[:, BLOCK:, :]).astype(MXU_DTYPE),
        _heads_minor(dv[:, BLOCK:, :]).astype(MXU_DTYPE),
        drx, dry, dgl_a, dgl_r], axis=1)
    g_in = _mm(xb, dproj, ta=True, tm=1024, tn=512, tk=S, name="d_w_in", side=scatter.take(110))
    scatter.add("w_in", _blocks(g_in, "cols"))
    dx = _mm(dproj, w_in, tb=True, tm=1024, tn=1024, tk=512, add=dpre1, add_scale=ALPHA, name="d_x",
             side=scatter.take(150))
    return loss, dx, grads


SHARDED = (
    ("w_in", "cols", 128), ("w_attn_proj", "rows", 128), ("w_rnn_proj", "rows", 160), ("w_out", "rows", 128),
    ("ffn_w_up", "cols", 256), ("ffn_w_gate", "cols", 256), ("ffn_w_down", "rows", 256),
)
SMALL_REPLICATED = ("b_gate", "rnn_conv_b", "lru_wa", "lru_ba", "lru_wi", "lru_bi", "lru_lambda", "attn_sinks",
                    "ln1_g", "ln1_b", "ffn_conv_b", "ln2_g", "ln2_b")
SMALL_SHARDED = ("rnn_conv_w", "ffn_conv_w")
WEIGHTS = ("w_in", "b_gate", "rnn_conv_w", "rnn_conv_b", "lru_wa", "lru_ba", "lru_wi", "lru_bi", "lru_lambda",
           "attn_sinks", "w_attn_proj", "w_rnn_proj", "w_out", "ln1_g", "ln1_b", "ffn_w_up", "ffn_w_gate",
           "ffn_conv_w", "ffn_conv_b", "ffn_w_down", "ln2_g", "ln2_b")
REDUCE_ROWS = 864
ADAMW_US = {"w_attn_proj": 9, "w_rnn_proj": 11, "w_out": 9, "ffn_w_up": 26, "ffn_w_gate": 26, "ffn_w_down": 28}


def kernel(x, w_in, b_gate, rnn_conv_w, rnn_conv_b, lru_wa, lru_ba, lru_wi, lru_bi, lru_lambda, attn_sinks, w_attn_proj, w_rnn_proj, w_out, ln1_g, ln1_b, ffn_w_up, ffn_w_gate, ffn_conv_w, ffn_conv_b, ffn_w_down, ln2_g, ln2_b, loss_target, m_w_in, m_b_gate, m_rnn_conv_w, m_rnn_conv_b, m_lru_wa, m_lru_ba, m_lru_wi, m_lru_bi, m_lru_lambda, m_attn_sinks, m_w_attn_proj, m_w_rnn_proj, m_w_out, m_ln1_g, m_ln1_b, m_ffn_w_up, m_ffn_w_gate, m_ffn_conv_w, m_ffn_conv_b, m_ffn_w_down, m_ln2_g, m_ln2_b, v_w_in, v_b_gate, v_rnn_conv_w, v_rnn_conv_b, v_lru_wa, v_lru_ba, v_lru_wi, v_lru_bi, v_lru_lambda, v_attn_sinks, v_w_attn_proj, v_w_rnn_proj, v_w_out, v_ln1_g, v_ln1_b, v_ffn_w_up, v_ffn_w_gate, v_ffn_conv_w, v_ffn_conv_b, v_ffn_w_down, v_ln2_g, v_ln2_b):
    given = dict(locals())
    wsh = {n: given[n][0] for n in WEIGHTS}
    msh = {n: given["m_" + n][0] for n in WEIGHTS}
    vsh = {n: given["v_" + n][0] for n in WEIGHTS}
    me = 4 * lax.axis_index("x") + 2 * lax.axis_index("y") + lax.axis_index("c")

    order = ("w_in", "rnn_conv_w", "ffn_conv_w", "w_attn_proj", "w_rnn_proj", "w_out", "ffn_w_up", "ffn_w_gate",
             "ffn_w_down")
    gather = _Gather({n: wsh[n] if n in SMALL_SHARDED else wsh[n].astype(MXU_DTYPE) for n in order})
    _run_side(gather.take(through="ffn_conv_w"), "gather_first")
    small = {n: wsh[n].reshape(1, -1) if wsh[n].ndim == 1 else wsh[n] for n in SMALL_REPLICATED}
    small["lru_wa"] = _block_diag(wsh["lru_wa"])
    small["lru_wi"] = _block_diag(wsh["lru_wi"])
    scatter = _Scatter(me)

    loss, dx, grads = _forward_backward(x[0], loss_target[0], small, gather, scatter)
    grads["lru_wa"] = _diag_blocks(grads["lru_wa"])
    grads["lru_wi"] = _diag_blocks(grads["lru_wi"])

    small_names = SMALL_REPLICATED + SMALL_SHARDED
    total = _all_reduce_small(
        _pack([loss] + [grads[n] for n in small_names], N_DEV * REDUCE_ROWS * 128).reshape(N_DEV, REDUCE_ROWS, 128))
    summed = _unpack(total.reshape(-1), [(1, 1)] + [grads[n].shape for n in small_names])
    loss_total = summed[0].reshape(())
    g_small = dict(zip(small_names, summed[1:]))
    for n in SMALL_SHARDED:
        width = wsh[n].shape[1]
        g_small[n] = lax.dynamic_slice_in_dim(g_small[n], me * width, width, axis=1)
    g_small = {n: g_small[n].reshape(wsh[n].shape) for n in small_names}
    flat = [_pack([d[n] for n in small_names]).reshape(-1, 128) for d in (wsh, msh, vsh, g_small)]
    results = _adamw_flat(*flat, side=scatter.take(9))
    out = {}
    shapes = [wsh[n].shape for n in small_names]
    for n, delta, nm, nv in zip(small_names, *[_unpack(res.reshape(-1), shapes) for res in results]):
        out[n] = (g_small[n], delta, nm, nv)

    tile_rows = {n: tr for n, _, tr in SHARDED}
    for n in list(scatter.sends):
        own, recv = scatter.get(n)
        out[n] = _reduce_adamw(wsh[n], msh[n], vsh[n], own, recv, tr=tile_rows[n], name="adamw_" + n,
                               side=scatter.take(0.9 * ADAMW_US[n]) if n in ADAMW_US else None)

    outputs = [loss_total, dx[None]]
    for kind in range(4):
        outputs += [out[n][kind][None] for n in WEIGHTS]
    return tuple(outputs)
```

```python
import math

import jax
import jax.numpy as jnp
from jax import lax
from jax.experimental import pallas as pl
from jax.experimental.pallas import tpu as pltpu

F32 = jnp.float32
BF16 = jnp.bfloat16
MXU_DTYPE = jnp.bfloat16

N_DEV = 8
S = 2048
D = 2048
HEAD_DIM = 64
N_KV = 4
GROUP = 8
BLOCK = 128
D_KV = N_KV * HEAD_DIM
D_RNN = 2560
RNN_GROUP = 640
N_RNN_GROUPS = D_RNN // RNN_GROUP
RNN_BLOCK_W = 160
RNN_CONV_W = 4
LRU_C = 8.0
D_FF = 6144
FFN_CONV_W = 3
D_IN = 11776
OFF_K = 2048
OFF_V = 2304
OFF_RX = 2560
OFF_RY = 5120
OFF_GA = 7680
OFF_GR = 9728
LN_EPS = 1e-5
ALPHA = 2.0 ** 0.25
ADAM_LR = 0.001
ADAM_B1 = 0.9
ADAM_B2 = 0.999
ADAM_EPS = 1e-08
ADAM_WD = 0.01
ADAM_STEP = 10
NEG = -1e30
VMEM_LIMIT = 56 * 1024 * 1024
MESH = pl.DeviceIdType.MESH
GELU_C = math.sqrt(2.0 / math.pi)


def _cparams(*sem):
    return pltpu.CompilerParams(dimension_semantics=sem or None, vmem_limit_bytes=VMEM_LIMIT)


def _call(body, *, name, grid, in_specs, out_specs, out_shape, operands, semantics, scratch_shapes=(), side=None):
    single = not isinstance(out_shape, (list, tuple))
    out_shape = [out_shape] if single else list(out_shape)
    out_specs = [out_specs] if single else list(out_specs)
    in_specs = list(in_specs)
    scratch_shapes = list(scratch_shapes)
    if side is None:
        res = pl.pallas_call(
            body, name=name, grid=grid, in_specs=in_specs, out_specs=out_specs, out_shape=out_shape,
            scratch_shapes=scratch_shapes, compiler_params=_cparams(*semantics))(*operands)
        return res[0] if single else res
    n_in, n_out, n_scr = len(in_specs), len(out_shape), len(scratch_shapes)
    s_in, s_out = len(side.operands), len(side.out_shape)
    hbm = pl.BlockSpec(memory_space=pltpu.HBM)

    def with_copies(*refs):
        core_in, side_in = refs[:n_in], refs[n_in:n_in + s_in]
        o0 = n_in + s_in
        core_out, side_out = refs[o0:o0 + n_out], refs[o0 + n_out:o0 + n_out + s_out]
        c0 = o0 + n_out + s_out
        core_scr, sems = refs[c0:c0 + n_scr], refs[c0 + n_scr:]
        first, last = None, None
        for d, size in enumerate(grid):
            at_start, at_end = pl.program_id(d) == 0, pl.program_id(d) == size - 1
            first = at_start if first is None else first & at_start
            last = at_end if last is None else last & at_end

        @pl.when(first)
        def _():
            side.start(side_in, side_out, sems)

        body(*core_in, *core_out, *core_scr)

        @pl.when(last)
        def _():
            side.finish(side_in, side_out, sems)

    res = pl.pallas_call(
        with_copies, name=name, grid=grid,
        in_specs=in_specs + [hbm] * s_in, out_specs=out_specs + [hbm] * s_out,
        out_shape=out_shape + list(side.out_shape),
        scratch_shapes=scratch_shapes + list(side.sems),
        input_output_aliases={n_in + i: n_out + o for i, o in side.aliases.items()},
        compiler_params=_cparams(*(("arbitrary",) * len(grid))))(*operands, *side.operands)
    side.done(res[n_out:])
    return res[0] if single else res[:n_out]


def _run_side(side, name):
    def body(*refs):
        s_in, s_out = len(side.operands), len(side.out_shape)
        side.start(refs[:s_in], refs[s_in:s_in + s_out], refs[s_in + s_out:])
        side.finish(refs[:s_in], refs[s_in:s_in + s_out], refs[s_in + s_out:])

    hbm = pl.BlockSpec(memory_space=pltpu.HBM)
    res = pl.pallas_call(
        body, name=name, in_specs=[hbm] * len(side.operands), out_specs=[hbm] * len(side.out_shape),
        out_shape=list(side.out_shape), scratch_shapes=list(side.sems),
        input_output_aliases=dict(side.aliases))(*side.operands)
    side.done(res)


def _gelu(x):
    x2 = x * x
    t = jnp.tanh(GELU_C * (x + 0.044715 * x * x2))
    g = 0.5 * x * (1.0 + t)
    dg = 0.5 * (1.0 + t) + 0.5 * x * (1.0 - t * t) * (GELU_C * (1.0 + 3.0 * 0.044715 * x2))
    return g, dg


def _sigmoid(x):
    return 1.0 / (1.0 + jnp.exp(-x))


def _softplus(x):
    z = jnp.exp(-jnp.abs(x))
    small = z * (1.0 - z * (0.5 - z * (1.0 / 3.0 - 0.25 * z)))
    return jnp.maximum(x, 0.0) + jnp.where(z < 0.02, small, jnp.log(1.0 + z))


def _one_minus_exp(x):
    series = -x * (1.0 + x * (0.5 + x * (1.0 / 6.0 + x * (1.0 / 24.0))))
    return jnp.where(x > -0.03, series, 1.0 - jnp.exp(x))


def _colsum(v):
    return jnp.sum(v, axis=0, keepdims=True)


def _mm(a, b, *, tm, tn, tk, name, ta=False, tb=False, out_dtype=F32, b_block=None, out_block=None, add=None,
        add_scale=1.0, side=None):
    out_dtypes = out_dtype if isinstance(out_dtype, tuple) else (out_dtype,)
    if ta:
        k_dim, m_dim = a.shape
    else:
        m_dim, k_dim = a.shape
    if b_block is None:
        n_dim = b.shape[0] if tb else b.shape[1]
    else:
        n_dim = b.shape[1] if tb else b.shape[0] * b_block
    assert m_dim % tm == 0 and n_dim % tn == 0 and k_dim % tk == 0, (name, m_dim, n_dim, k_dim)
    nk = k_dim // tk
    dims = (((0 if ta else 1,), (1 if tb else 0,)), ((), ()))
    has_add = add is not None

    def body(*refs):
        a_ref, b_ref = refs[0], refs[1]
        add_ref = refs[2] if has_add else None
        first_out = 3 if has_add else 2
        o_refs = refs[first_out:first_out + len(out_dtypes)]

        def product():
            return lax.dot_general(a_ref[...].astype(MXU_DTYPE), b_ref[...].astype(MXU_DTYPE), dims,
                                   preferred_element_type=F32)

        def finish(acc):
            if has_add:
                acc = acc + add_scale * add_ref[...]
            for o_ref in o_refs:
                o_ref[...] = acc.astype(o_ref.dtype)

        if nk == 1:
            finish(product())
        else:
            acc_ref = refs[-1]
            k = pl.program_id(2)

            @pl.when(k == 0)
            def _():
                acc_ref[...] = jnp.zeros_like(acc_ref)

            acc_ref[...] += product()

            @pl.when(k == nk - 1)
            def _():
                finish(acc_ref[...])

    if ta:
        a_spec = pl.BlockSpec((tk, tm), lambda i, j, k: (k, i))
    else:
        a_spec = pl.BlockSpec((tm, tk), lambda i, j, k: (i, k))
    if b_block is None:
        if tb:
            b_spec = pl.BlockSpec((tn, tk), lambda i, j, k: (j, k))
        else:
            b_spec = pl.BlockSpec((tk, tn), lambda i, j, k: (k, j))
    elif tb:
        assert b_block % tk == 0
        b_spec = pl.BlockSpec((None, tn, tk), lambda i, j, k: ((k * tk) // b_block, j, ((k * tk) % b_block) // tk))
    else:
        assert b_block % tn == 0
        b_spec = pl.BlockSpec((None, tk, tn), lambda i, j, k: ((j * tn) // b_block, k, ((j * tn) % b_block) // tn))
    in_specs = [a_spec, b_spec]
    operands = [a, b]
    if has_add:
        in_specs.append(pl.BlockSpec((tm, tn), lambda i, j, k: (i, j)))
        operands.append(add)
    if out_block is None:
        out_spec = pl.BlockSpec((tm, tn), lambda i, j, k: (i, j))
        out_dims = (m_dim, n_dim)
    else:
        assert out_block % tn == 0
        out_spec = pl.BlockSpec((None, tm, tn), lambda i, j, k: ((j * tn) // out_block, i, ((j * tn) % out_block) // tn))
        out_dims = (n_dim // out_block, m_dim, out_block)
    res = _call(
        body,
        name=name,
        grid=(m_dim // tm, n_dim // tn, nk),
        in_specs=in_specs,
        out_specs=[out_spec] * len(out_dtypes),
        out_shape=[jax.ShapeDtypeStruct(out_dims, dt) for dt in out_dtypes],
        scratch_shapes=[pltpu.VMEM((tm, tn), F32)] if nk > 1 else [],
        semantics=("parallel", "parallel", "arbitrary"),
        operands=tuple(operands),
        side=side,
    )
    return res if isinstance(out_dtype, tuple) else res[0]


def _attn_bias(h):
    row = lax.broadcasted_iota(jnp.int32, (GROUP * BLOCK, 2 * BLOCK), 0)
    col = lax.broadcasted_iota(jnp.int32, (GROUP * BLOCK, 2 * BLOCK), 1)
    dist = BLOCK + (row & (BLOCK - 1)) - col
    head = h * GROUP + (row >> 7) + 1
    slope = jnp.exp(head.astype(F32) * (-0.25 * math.log(2.0)))
    return jnp.where((dist >= 0) & (dist < BLOCK), -slope * dist.astype(F32), NEG)


def _attn_probs(qn, kb, bias, sink, first_block):
    s = lax.dot_general(qn, kb, (((1,), (1,)), ((), ())), preferred_element_type=F32) * (HEAD_DIM ** -0.5) + bias
    col = lax.broadcasted_iota(jnp.int32, s.shape, 1)
    s = jnp.where(first_block & (col < BLOCK), NEG, s)
    m = jnp.maximum(jnp.max(s, axis=-1, keepdims=True), sink)
    e = jnp.exp(s - m)
    e_sink = jnp.exp(sink - m)
    inv = 1.0 / (jnp.sum(e, axis=-1, keepdims=True) + e_sink)
    return e * inv, e_sink * inv


def _attn_fwd(q, kp, vp, sink_rows, side=None):
    rows = GROUP * BLOCK

    def body(q_ref, k_ref, v_ref, sink_ref, o_ref, bias_ref):
        bias_ref[...] = _attn_bias(pl.program_id(0))
        sink = sink_ref[...]

        def step(n, carry):
            r0 = pl.multiple_of(n * BLOCK, BLOCK)
            qn = q_ref[:, pl.ds(r0, BLOCK), :].reshape(rows, HEAD_DIM)
            kb = k_ref[pl.ds(r0, 2 * BLOCK), :]
            vb = v_ref[pl.ds(r0, 2 * BLOCK), :]
            p, _ = _attn_probs(qn, kb, bias_ref[...], sink, n == 0)
            o = jnp.dot(p.astype(MXU_DTYPE), vb, preferred_element_type=F32)
            o_ref[:, pl.ds(r0, BLOCK), :] = o.reshape(GROUP, BLOCK, HEAD_DIM).astype(o_ref.dtype)
            return carry

        lax.fori_loop(0, S // BLOCK, step, 0)

    return _call(
        body,
        name="attn_fwd",
        grid=(N_KV,),
        in_specs=[
            pl.BlockSpec((None, GROUP, S, HEAD_DIM), lambda h: (h, 0, 0, 0)),
            pl.BlockSpec((None, BLOCK + S, HEAD_DIM), lambda h: (h, 0, 0)),
            pl.BlockSpec((None, BLOCK + S, HEAD_DIM), lambda h: (h, 0, 0)),
            pl.BlockSpec((None, rows, 1), lambda h: (h, 0, 0)),
        ],
        out_specs=pl.BlockSpec((None, GROUP, S, HEAD_DIM), lambda h: (h, 0, 0, 0)),
        out_shape=jax.ShapeDtypeStruct((N_KV, GROUP, S, HEAD_DIM), MXU_DTYPE),
        scratch_shapes=[pltpu.VMEM((rows, 2 * BLOCK), F32)],
        semantics=("parallel",),
        operands=(q, kp, vp, sink_rows),
        side=side,
    )


def _attn_bwd(q, kp, vp, sink_rows, do, side=None):
    rows = GROUP * BLOCK

    def body(q_ref, k_ref, v_ref, sink_ref, do_ref, dq_ref, dk_ref, dv_ref, ds_ref, bias_ref):
        bias_ref[...] = _attn_bias(pl.program_id(0))
        sink = sink_ref[...]
        dk_ref[...] = jnp.zeros_like(dk_ref)
        dv_ref[...] = jnp.zeros_like(dv_ref)

        def step(n, dsink):
            r0 = pl.multiple_of(n * BLOCK, BLOCK)
            qn = q_ref[:, pl.ds(r0, BLOCK), :].reshape(rows, HEAD_DIM)
            don = do_ref[:, pl.ds(r0, BLOCK), :].reshape(rows, HEAD_DIM)
            kb = k_ref[pl.ds(r0, 2 * BLOCK), :]
            vb = v_ref[pl.ds(r0, 2 * BLOCK), :]
            p, p_sink = _attn_probs(qn, kb, bias_ref[...], sink, n == 0)
            dp = lax.dot_general(don, vb, (((1,), (1,)), ((), ())), preferred_element_type=F32)
            delta = jnp.sum(p * dp, axis=-1, keepdims=True)
            ds = (p * (dp - delta) * (HEAD_DIM ** -0.5)).astype(MXU_DTYPE)
            dq = jnp.dot(ds, kb, preferred_element_type=F32)
            dq_ref[:, pl.ds(r0, BLOCK), :] = dq.reshape(GROUP, BLOCK, HEAD_DIM).astype(dq_ref.dtype)
            dk_ref[pl.ds(r0, 2 * BLOCK), :] += lax.dot_general(ds, qn, (((0,), (0,)), ((), ())),
                                                              preferred_element_type=F32)
            dv_ref[pl.ds(r0, 2 * BLOCK), :] += lax.dot_general(p.astype(MXU_DTYPE), don, (((0,), (0,)), ((), ())),
                                                              preferred_element_type=F32)
            return dsink - p_sink * delta

        dsink = lax.fori_loop(0, S // BLOCK, step, jnp.zeros((rows, 1), F32))
        for g in range(GROUP):
            ds_ref[g:g + 1, :] = _colsum(dsink[g * BLOCK:(g + 1) * BLOCK, :])

    hm = pl.BlockSpec((None, GROUP, S, HEAD_DIM), lambda h: (h, 0, 0, 0))
    kv = pl.BlockSpec((None, BLOCK + S, HEAD_DIM), lambda h: (h, 0, 0))
    return _call(
        body,
        name="attn_bwd",
        grid=(N_KV,),
        in_specs=[hm, kv, kv, pl.BlockSpec((None, rows, 1), lambda h: (h, 0, 0)), hm],
        out_specs=[hm, kv, kv, pl.BlockSpec((None, GROUP, 1), lambda h: (h, 0, 0))],
        out_shape=[
            jax.ShapeDtypeStruct((N_KV, GROUP, S, HEAD_DIM), MXU_DTYPE),
            jax.ShapeDtypeStruct((N_KV, BLOCK + S, HEAD_DIM), F32),
            jax.ShapeDtypeStruct((N_KV, BLOCK + S, HEAD_DIM), F32),
            jax.ShapeDtypeStruct((N_KV, GROUP, 1), F32),
        ],
        scratch_shapes=[pltpu.VMEM((rows, 2 * BLOCK), F32)],
        semantics=("parallel",),
        operands=(q, kp, vp, sink_rows, do),
        side=side,
    )


PAD = 8
CHUNK = 256


def _past_taps(xpad_ref, r0, width):
    ext = xpad_ref[pl.ds(r0, CHUNK + PAD), :]
    taps = []
    for k in range(width):
        back = width - 1 - k
        taps.append((ext if back == 0 else pltpu.roll(ext, back, 0))[PAD:, :])
    return taps


def _future_taps(xpad_ref, r0, width):
    ext = xpad_ref[pl.ds(r0, CHUNK + PAD), :]
    taps = []
    for ahead in range(width):
        taps.append((ext if ahead == 0 else pltpu.roll(ext, CHUNK + PAD - ahead, 0))[:CHUNK, :])
    return taps


def _conv_fwd(src, col0, w, b, *, tc, name, side=None):
    width, c_dim = w.shape

    def body(x_ref, w_ref, b_ref, o_ref, xpad_ref):
        xpad_ref[pl.ds(0, PAD), :] = jnp.zeros((PAD, tc), F32)
        xpad_ref[pl.ds(PAD, S), :] = x_ref[...]
        wv = w_ref[...]
        bv = b_ref[...]

        def step(ci, carry):
            r0 = pl.multiple_of(ci * CHUNK, CHUNK)
            taps = _past_taps(xpad_ref, r0, width)
            y = bv + taps[0] * wv[0:1, :]
            for k in range(1, width):
                y = y + taps[k] * wv[k:k + 1, :]
            o_ref[pl.ds(r0, CHUNK), :] = y
            return carry

        lax.fori_loop(0, S // CHUNK, step, 0)

    return _call(
        body,
        name=name,
        grid=(c_dim // tc,),
        in_specs=[
            pl.BlockSpec((S, tc), lambda j: (0, col0 // tc + j)),
            pl.BlockSpec((width, tc), lambda j: (0, j)),
            pl.BlockSpec((1, tc), lambda j: (0, j)),
        ],
        out_specs=pl.BlockSpec((S, tc), lambda j: (0, j)),
        out_shape=jax.ShapeDtypeStruct((S, c_dim), F32),
        scratch_shapes=[pltpu.VMEM((S + PAD, tc), F32)],
        semantics=("parallel",),
        operands=(src, w, b),
        side=side,
    )


def _conv_bwd(dy, src, col0, w, *, tc, name, side=None):
    width, c_dim = w.shape

    def body(dy_ref, x_ref, w_ref, dx_ref, dw_ref, db_ref, xpad_ref, dpad_ref):
        xpad_ref[pl.ds(0, PAD), :] = jnp.zeros((PAD, tc), F32)
        xpad_ref[pl.ds(PAD, S), :] = x_ref[...]
        dpad_ref[pl.ds(0, S), :] = dy_ref[...]
        dpad_ref[pl.ds(S, PAD), :] = jnp.zeros((PAD, tc), F32)
        wv = w_ref[...]

        def step(ci, acc):
            r0 = pl.multiple_of(ci * CHUNK, CHUNK)
            past = _past_taps(xpad_ref, r0, width)
            ahead = _future_taps(dpad_ref, r0, width)
            d = ahead[0]
            dx = d * wv[width - 1:width, :]
            for j in range(1, width):
                dx = dx + ahead[j] * wv[width - 1 - j:width - j, :]
            dx_ref[pl.ds(r0, CHUNK), :] = dx.astype(dx_ref.dtype)
            return tuple(acc[k] + _colsum(past[k] * d) for k in range(width)) + (acc[width] + _colsum(d),)

        zero = jnp.zeros((1, tc), F32)
        acc = lax.fori_loop(0, S // CHUNK, step, (zero,) * (width + 1))
        for k in range(width):
            dw_ref[k:k + 1, :] = acc[k]
        db_ref[...] = acc[width]

    return _call(
        body,
        name=name,
        grid=(c_dim // tc,),
        in_specs=[
            pl.BlockSpec((S, tc), lambda j: (0, j)),
            pl.BlockSpec((S, tc), lambda j: (0, col0 // tc + j)),
            pl.BlockSpec((width, tc), lambda j: (0, j)),
        ],
        out_specs=[
            pl.BlockSpec((S, tc), lambda j: (0, j)),
            pl.BlockSpec((width, tc), lambda j: (0, j)),
            pl.BlockSpec((1, tc), lambda j: (0, j)),
        ],
        out_shape=[
            jax.ShapeDtypeStruct((S, c_dim), MXU_DTYPE),
            jax.ShapeDtypeStruct((width, c_dim), F32),
            jax.ShapeDtypeStruct((1, c_dim), F32),
        ],
        scratch_shapes=[pltpu.VMEM((S + PAD, tc), F32), pltpu.VMEM((S + PAD, tc), F32)],
        semantics=("parallel",),
        operands=(dy, src, w),
        side=side,
    )


SCAN_TC = 256


def _lru_gates(rxc, wa, wi, ba, bi, side=None):
    tm = 512

    def body(x_ref, wa_ref, wi_ref, ba_ref, bi_ref, r_ref, i_ref):
        xv = x_ref[...].astype(MXU_DTYPE)
        r_ref[...] = _sigmoid(jnp.dot(xv, wa_ref[...].astype(MXU_DTYPE), preferred_element_type=F32) + ba_ref[...])
        i_ref[...] = _sigmoid(jnp.dot(xv, wi_ref[...].astype(MXU_DTYPE), preferred_element_type=F32) + bi_ref[...])

    x_spec = pl.BlockSpec((tm, RNN_GROUP), lambda g, i: (i, g))
    w_spec = pl.BlockSpec((None, RNN_GROUP, RNN_GROUP), lambda g, i: (g, 0, 0))
    b_spec = pl.BlockSpec((1, RNN_GROUP), lambda g, i: (0, g))
    return _call(
        body,
        name="lru_gates",
        grid=(N_RNN_GROUPS, S // tm),
        in_specs=[x_spec, w_spec, w_spec, b_spec, b_spec],
        out_specs=[x_spec, x_spec],
        out_shape=[jax.ShapeDtypeStruct((S, D_RNN), F32)] * 2,
        semantics=("parallel", "parallel"),
        operands=(rxc, wa, wi, ba, bi),
        side=side,
    )


def _scan_down(a, u, row):
    for d in (1, 2, 4):
        a_s = jnp.where(row >= d, pltpu.roll(a, d, 0), 1.0)
        u_s = jnp.where(row >= d, pltpu.roll(u, d, 0), 0.0)
        u = a * u_s + u
        a = a * a_s
    return a, u


def _scan_up(a, u, row):
    for d in (1, 2, 4):
        a_s = jnp.where(row < 8 - d, pltpu.roll(a, 8 - d, 0), 1.0)
        u_s = jnp.where(row < 8 - d, pltpu.roll(u, 8 - d, 0), 0.0)
        u = a * u_s + u
        a = a * a_s
    return a, u


def _lru_scan_fwd(r, i, rxc, proj, lam, side=None):
    tc = SCAN_TC

    def body(r_ref, i_ref, x_ref, ry_ref, lam_ref, h_ref, y_ref):
        rate = LRU_C * _softplus(-lam_ref[...])
        row = lax.broadcasted_iota(jnp.int32, (8, tc), 0)

        def step(ci, carry):
            r0 = pl.multiple_of(ci * 16, 16)
            log_a = -rate * r_ref[pl.ds(r0, 16), :]
            a16 = jnp.exp(log_a)
            u16 = jnp.sqrt(_one_minus_exp(2.0 * log_a)) * (i_ref[pl.ds(r0, 16), :] * x_ref[pl.ds(r0, 16), :])
            hs = []
            for half in range(2):
                a_cum, h0 = _scan_down(a16[8 * half:8 * half + 8, :], u16[8 * half:8 * half + 8, :], row)
                h = a_cum * carry + h0
                carry = jnp.broadcast_to(h[7:8, :], (8, tc))
                hs.append(h)
            h16 = jnp.concatenate(hs, axis=0)
            h_ref[pl.ds(r0, 16), :] = h16
            y_ref[pl.ds(r0, 16), :] = (h16 * _gelu(ry_ref[pl.ds(r0, 16), :])[0]).astype(y_ref.dtype)
            return carry

        lax.fori_loop(0, S // 16, step, jnp.zeros((8, tc), F32))

    col = pl.BlockSpec((S, tc), lambda j: (0, j))
    return _call(
        body,
        name="lru_scan_fwd",
        grid=(D_RNN // tc,),
        in_specs=[col, col, col, pl.BlockSpec((S, tc), lambda j: (0, OFF_RY // tc + j)),
                  pl.BlockSpec((1, tc), lambda j: (0, j))],
        out_specs=[col, col],
        out_shape=[jax.ShapeDtypeStruct((S, D_RNN), F32), jax.ShapeDtypeStruct((S, D_RNN), MXU_DTYPE)],
        semantics=("parallel",),
        operands=(r, i, rxc, proj, lam),
        side=side,
    )


def _lru_scan_bwd(dy, proj, h, r, i, rxc, lam, side=None):
    tc = SCAN_TC

    def body(dy_ref, ry_ref, h_ref, r_ref, i_ref, x_ref, lam_ref,
             dry_ref, dzr_ref, dzi_ref, dx_ref, dba_ref, dbi_ref, dlam_ref, a_ref, dh_ref, hp_ref):
        lam_v = lam_ref[...]
        rate = LRU_C * _softplus(-lam_v)
        dlam_scale = LRU_C * _sigmoid(-lam_v)
        row = lax.broadcasted_iota(jnp.int32, (8, tc), 0)
        hp_ref[pl.ds(0, PAD), :] = jnp.zeros((PAD, tc), F32)
        hp_ref[pl.ds(PAD, S), :] = h_ref[...]
        a_ref[pl.ds(S, PAD), :] = jnp.zeros((PAD, tc), F32)

        def prep(ci, carry):
            r0 = pl.multiple_of(ci * CHUNK, CHUNK)
            a_ref[pl.ds(r0, CHUNK), :] = jnp.exp(-rate * r_ref[pl.ds(r0, CHUNK), :])
            ge, dge = _gelu(ry_ref[pl.ds(r0, CHUNK), :])
            dyv = dy_ref[pl.ds(r0, CHUNK), :]
            dh_ref[pl.ds(r0, CHUNK), :] = dyv * ge
            dry_ref[pl.ds(r0, CHUNK), :] = (dyv * h_ref[pl.ds(r0, CHUNK), :] * dge).astype(dry_ref.dtype)
            return carry

        lax.fori_loop(0, S // CHUNK, prep, 0)

        def step(ci, state):
            carry, dba, dbi, dlam = state
            r0 = pl.multiple_of(S - 16 - ci * 16, 16)
            a_ext = a_ref[pl.ds(r0, 24), :]
            a_next = pltpu.roll(a_ext, 23, 0)
            h_prev = pltpu.roll(hp_ref[pl.ds(r0, 24), :], 1, 0)
            dh16 = dh_ref[pl.ds(r0, 16), :]
            gs = [None, None]
            for half in (1, 0):
                lo = 8 * half
                c_cum, g0 = _scan_up(a_next[lo:lo + 8, :], dh16[lo:lo + 8, :], row)
                g = c_cum * carry + g0
                carry = jnp.broadcast_to(g[0:1, :], (8, tc))
                gs[half] = g
            g16 = jnp.concatenate(gs, axis=0)
            a16 = a_ext[0:16, :]
            r16 = r_ref[pl.ds(r0, 16), :]
            i16 = i_ref[pl.ds(r0, 16), :]
            x16 = x_ref[pl.ds(r0, 16), :]
            a2 = a16 * a16
            sq = jnp.sqrt(_one_minus_exp(-2.0 * rate * r16))
            dx_ref[pl.ds(r0, 16), :] = g16 * sq * i16
            dzi = g16 * sq * x16 * i16 * (1.0 - i16)
            dlog_a = g16 * h_prev[8:24, :] * a16 - g16 * i16 * x16 * a2 / sq
            dzr = -rate * dlog_a * r16 * (1.0 - r16)
            dzr_ref[pl.ds(r0, 16), :] = dzr.astype(dzr_ref.dtype)
            dzi_ref[pl.ds(r0, 16), :] = dzi.astype(dzi_ref.dtype)
            return carry, dba + _colsum(dzr), dbi + _colsum(dzi), dlam + _colsum(dlog_a * r16)

        zero = jnp.zeros((1, tc), F32)
        _, dba, dbi, dlam = lax.fori_loop(0, S // 16, step, (jnp.zeros((8, tc), F32), zero, zero, zero))
        dba_ref[...] = dba
        dbi_ref[...] = dbi
        dlam_ref[...] = dlam * dlam_scale

    col = pl.BlockSpec((S, tc), lambda j: (0, j))
    vec = pl.BlockSpec((1, tc), lambda j: (0, j))
    return _call(
        body,
        name="lru_scan_bwd",
        grid=(D_RNN // tc,),
        in_specs=[col, pl.BlockSpec((S, tc), lambda j: (0, OFF_RY // tc + j)), col, col, col, col, vec],
        out_specs=[col, col, col, col, vec, vec, vec],
        out_shape=[jax.ShapeDtypeStruct((S, D_RNN), MXU_DTYPE)] * 3 + [jax.ShapeDtypeStruct((S, D_RNN), F32)]
        + [jax.ShapeDtypeStruct((1, D_RNN), F32)] * 3,
        scratch_shapes=[pltpu.VMEM((S + PAD, tc), F32), pltpu.VMEM((S, tc), F32), pltpu.VMEM((S + PAD, tc), F32)],
        semantics=("parallel",),
        operands=(dy, proj, h, r, i, rxc, lam),
        side=side,
    )


def _lru_gate_wgrad(rxc, dzr, dzi, side=None):
    def body(x_ref, dzr_ref, dzi_ref, dwa_ref, dwi_ref):
        xv = x_ref[...].astype(MXU_DTYPE)
        dims = (((0,), (0,)), ((), ()))
        dwa_ref[...] = lax.dot_general(xv, dzr_ref[...], dims, preferred_element_type=F32)
        dwi_ref[...] = lax.dot_general(xv, dzi_ref[...], dims, preferred_element_type=F32)

    col = pl.BlockSpec((S, RNN_GROUP), lambda g: (0, g))
    w_spec = pl.BlockSpec((None, RNN_GROUP, RNN_GROUP), lambda g: (g, 0, 0))
    return _call(
        body,
        name="lru_gate_wgrad",
        grid=(N_RNN_GROUPS,),
        in_specs=[col, col, col],
        out_specs=[w_spec, w_spec],
        out_shape=[jax.ShapeDtypeStruct((N_RNN_GROUPS, RNN_GROUP, RNN_GROUP), F32)] * 2,
        semantics=("parallel",),
        operands=(rxc, dzr, dzi),
        side=side,
    )


def _lru_gate_xgrad(dzr, dzi, wa, wi, dx_in, side=None):
    tm = 512

    def body(dzr_ref, dzi_ref, wa_ref, wi_ref, dx_ref, o_ref):
        dims = (((1,), (1,)), ((), ()))
        o_ref[...] = (dx_ref[...]
                      + lax.dot_general(dzr_ref[...], wa_ref[...].astype(MXU_DTYPE), dims, preferred_element_type=F32)
                      + lax.dot_general(dzi_ref[...], wi_ref[...].astype(MXU_DTYPE), dims, preferred_element_type=F32))

    x_spec = pl.BlockSpec((tm, RNN_GROUP), lambda g, i: (i, g))
    w_spec = pl.BlockSpec((None, RNN_GROUP, RNN_GROUP), lambda g, i: (g, 0, 0))
    return _call(
        body,
        name="lru_gate_xgrad",
        grid=(N_RNN_GROUPS, S // tm),
        in_specs=[x_spec, x_spec, w_spec, w_spec, x_spec],
        out_specs=x_spec,
        out_shape=jax.ShapeDtypeStruct((S, D_RNN), F32),
        semantics=("parallel", "parallel"),
        operands=(dzr, dzi, wa, wi, dx_in),
        side=side,
    )


def _gate_fwd(y_attn, y_rnn, proj, b_gate, side=None):
    t = 512

    def body(ya_ref, yr_ref, ga_ref, gr_ref, ba_ref, br_ref, o_ref):
        o_ref[...] = (_sigmoid(ga_ref[...] + ba_ref[...]) * ya_ref[...]
                      + _sigmoid(gr_ref[...] + br_ref[...]) * yr_ref[...]).astype(o_ref.dtype)

    tile = pl.BlockSpec((t, t), lambda i, j: (i, j))
    return _call(
        body,
        name="gate_fwd",
        grid=(S // t, D // t),
        in_specs=[tile, tile,
                  pl.BlockSpec((t, t), lambda i, j: (i, OFF_GA // t + j)),
                  pl.BlockSpec((t, t), lambda i, j: (i, OFF_GR // t + j)),
                  pl.BlockSpec((1, t), lambda i, j: (0, j)),
                  pl.BlockSpec((1, t), lambda i, j: (0, D // t + j))],
        out_specs=tile,
        out_shape=jax.ShapeDtypeStruct((S, D), MXU_DTYPE),
        semantics=("parallel", "parallel"),
        operands=(y_attn, y_rnn, proj, proj, b_gate, b_gate),
        side=side,
    )


def _gate_bwd(dmix, y_attn, y_rnn, proj, b_gate, side=None):
    t = 512

    def body(dm_ref, ya_ref, yr_ref, ga_ref, gr_ref, ba_ref, br_ref,
             dya_ref, dyr_ref, dga_ref, dgr_ref, dba_ref, dbr_ref):
        @pl.when(pl.program_id(1) == 0)
        def _():
            dba_ref[...] = jnp.zeros_like(dba_ref)
            dbr_ref[...] = jnp.zeros_like(dbr_ref)

        dm = dm_ref[...]
        ga = _sigmoid(ga_ref[...] + ba_ref[...])
        gr = _sigmoid(gr_ref[...] + br_ref[...])
        dya_ref[...] = (dm * ga).astype(dya_ref.dtype)
        dyr_ref[...] = (dm * gr).astype(dyr_ref.dtype)
        dga = dm * ya_ref[...] * ga * (1.0 - ga)
        dgr = dm * yr_ref[...] * gr * (1.0 - gr)
        dga_ref[...] = dga.astype(dga_ref.dtype)
        dgr_ref[...] = dgr.astype(dgr_ref.dtype)
        dba_ref[...] += _colsum(dga)
        dbr_ref[...] += _colsum(dgr)

    tile = pl.BlockSpec((t, t), lambda j, i: (i, j))
    vec = pl.BlockSpec((1, t), lambda j, i: (0, j))
    return _call(
        body,
        name="gate_bwd",
        grid=(D // t, S // t),
        in_specs=[tile, tile, tile,
                  pl.BlockSpec((t, t), lambda j, i: (i, OFF_GA // t + j)),
                  pl.BlockSpec((t, t), lambda j, i: (i, OFF_GR // t + j)),
                  vec,
                  pl.BlockSpec((1, t), lambda j, i: (0, D // t + j))],
        out_specs=[tile, tile, tile, tile, vec, vec],
        out_shape=[jax.ShapeDtypeStruct((S, D), MXU_DTYPE)] * 4 + [jax.ShapeDtypeStruct((1, D), F32)] * 2,
        semantics=("parallel", "arbitrary"),
        operands=(dmix, y_attn, y_rnn, proj, proj, b_gate, b_gate),
        side=side,
    )


LN_TM = 256


def _ln_stats(pre):
    mu = jnp.mean(pre, axis=-1, keepdims=True)
    xc = pre - mu
    rstd = lax.rsqrt(jnp.mean(xc * xc, axis=-1, keepdims=True) + LN_EPS)
    return xc * rstd, rstd


def _ln_input_grad(dy, xhat, rstd, g):
    dyg = dy * g
    return rstd * (dyg - jnp.mean(dyg, axis=-1, keepdims=True)
                   - xhat * jnp.mean(dyg * xhat, axis=-1, keepdims=True))


def _ln_fwd(res, branch, g, b, side=None):
    def body(res_ref, br_ref, g_ref, b_ref, y_ref, yb_ref, xhat_ref, rstd_ref):
        xhat, rstd = _ln_stats(ALPHA * res_ref[...] + br_ref[...])
        y = xhat * g_ref[...] + b_ref[...]
        y_ref[...] = y
        yb_ref[...] = y.astype(yb_ref.dtype)
        xhat_ref[...] = xhat
        rstd_ref[...] = rstd

    tile = pl.BlockSpec((LN_TM, D), lambda i: (i, 0))
    vec = pl.BlockSpec((1, D), lambda i: (0, 0))
    return _call(
        body,
        name="ln_fwd",
        grid=(S // LN_TM,),
        in_specs=[tile, tile, vec, vec],
        out_specs=[tile, tile, tile, pl.BlockSpec((LN_TM, 1), lambda i: (i, 0))],
        out_shape=[jax.ShapeDtypeStruct((S, D), F32), jax.ShapeDtypeStruct((S, D), MXU_DTYPE),
                   jax.ShapeDtypeStruct((S, D), F32), jax.ShapeDtypeStruct((S, 1), F32)],
        semantics=("parallel",),
        operands=(res, branch, g, b),
        side=side,
    )


def _ln_bwd(dy_a, dy_b, xhat, rstd, g, side=None):
    def body(da_ref, db_in_ref, xhat_ref, rstd_ref, g_ref, dp_ref, dpb_ref, dg_ref, db_ref):
        @pl.when(pl.program_id(0) == 0)
        def _():
            dg_ref[...] = jnp.zeros_like(dg_ref)
            db_ref[...] = jnp.zeros_like(db_ref)

        dy = da_ref[...] + ALPHA * db_in_ref[...]
        xhat = xhat_ref[...]
        dp = _ln_input_grad(dy, xhat, rstd_ref[...], g_ref[...])
        dp_ref[...] = dp
        dpb_ref[...] = dp.astype(dpb_ref.dtype)
        dg_ref[...] += _colsum(dy * xhat)
        db_ref[...] += _colsum(dy)

    tile = pl.BlockSpec((LN_TM, D), lambda i: (i, 0))
    vec = pl.BlockSpec((1, D), lambda i: (0, 0))
    return _call(
        body,
        name="ln_bwd",
        grid=(S // LN_TM,),
        in_specs=[tile, tile, tile, pl.BlockSpec((LN_TM, 1), lambda i: (i, 0)), vec],
        out_specs=[tile, tile, vec, vec],
        out_shape=[jax.ShapeDtypeStruct((S, D), F32), jax.ShapeDtypeStruct((S, D), MXU_DTYPE),
                   jax.ShapeDtypeStruct((1, D), F32), jax.ShapeDtypeStruct((1, D), F32)],
        semantics=("arbitrary",),
        operands=(dy_a, dy_b, xhat, rstd, g),
        side=side,
    )


def _ln_loss_bwd(res, branch, g, b, target, side=None):
    def body(res_ref, br_ref, g_ref, b_ref, t_ref, loss_ref, dp_ref, dpb_ref, dg_ref, db_ref):
        @pl.when(pl.program_id(0) == 0)
        def _():
            loss_ref[...] = jnp.zeros_like(loss_ref)
            dg_ref[...] = jnp.zeros_like(dg_ref)
            db_ref[...] = jnp.zeros_like(db_ref)

        xhat, rstd = _ln_stats(ALPHA * res_ref[...] + br_ref[...])
        gv = g_ref[...]
        err = xhat * gv + b_ref[...] - t_ref[...]
        loss_ref[...] += (0.5 / D) * jnp.sum(_colsum(err * err), axis=1, keepdims=True)
        dy = err * (1.0 / D)
        dp = _ln_input_grad(dy, xhat, rstd, gv)
        dp_ref[...] = dp
        dpb_ref[...] = dp.astype(dpb_ref.dtype)
        dg_ref[...] += _colsum(dy * xhat)
        db_ref[...] += _colsum(dy)

    tile = pl.BlockSpec((LN_TM, D), lambda i: (i, 0))
    vec = pl.BlockSpec((1, D), lambda i: (0, 0))
    return _call(
        body,
        name="ln_loss_bwd",
        grid=(S // LN_TM,),
        in_specs=[tile, tile, vec, vec, tile],
        out_specs=[pl.BlockSpec((1, 1), lambda i: (0, 0)), tile, tile, vec, vec],
        out_shape=[jax.ShapeDtypeStruct((1, 1), F32), jax.ShapeDtypeStruct((S, D), F32),
                   jax.ShapeDtypeStruct((S, D), MXU_DTYPE),
                   jax.ShapeDtypeStruct((1, D), F32), jax.ShapeDtypeStruct((1, D), F32)],
        semantics=("arbitrary",),
        operands=(res, branch, g, b, target),
        side=side,
    )


FFN_TC = 256


def _ffn_act_fwd(up, gpre, w, b, side=None):
    tc = FFN_TC

    def body(up_ref, x_ref, w_ref, b_ref, o_ref, xpad_ref):
        xpad_ref[pl.ds(0, PAD), :] = jnp.zeros((PAD, tc), F32)
        xpad_ref[pl.ds(PAD, S), :] = x_ref[...]
        wv = w_ref[...]
        bv = b_ref[...]

        def step(ci, carry):
            r0 = pl.multiple_of(ci * CHUNK, CHUNK)
            taps = _past_taps(xpad_ref, r0, FFN_CONV_W)
            gate = bv + taps[0] * wv[0:1, :] + taps[1] * wv[1:2, :] + taps[2] * wv[2:3, :]
            o_ref[pl.ds(r0, CHUNK), :] = (_gelu(gate)[0] * up_ref[pl.ds(r0, CHUNK), :]).astype(o_ref.dtype)
            return carry

        lax.fori_loop(0, S // CHUNK, step, 0)

    col = pl.BlockSpec((S, tc), lambda j: (0, j))
    return _call(
        body,
        name="ffn_act_fwd",
        grid=(D_FF // tc,),
        in_specs=[col, col, pl.BlockSpec((FFN_CONV_W, tc), lambda j: (0, j)), pl.BlockSpec((1, tc), lambda j: (0, j))],
        out_specs=col,
        out_shape=jax.ShapeDtypeStruct((S, D_FF), MXU_DTYPE),
        scratch_shapes=[pltpu.VMEM((S + PAD, tc), F32)],
        semantics=("parallel",),
        operands=(up, gpre, w, b),
        side=side,
    )


def _ffn_act_bwd(dfin, up, gpre, w, b, side=None):
    tc = FFN_TC
    width = FFN_CONV_W

    def body(df_ref, up_ref, x_ref, w_ref, b_ref, dup_ref, dx_ref, dw_ref, db_ref, xpad_ref, dpad_ref):
        xpad_ref[pl.ds(0, PAD), :] = jnp.zeros((PAD, tc), F32)
        xpad_ref[pl.ds(PAD, S), :] = x_ref[...]
        dpad_ref[pl.ds(S, PAD), :] = jnp.zeros((PAD, tc), F32)
        wv = w_ref[...]
        bv = b_ref[...]

        def gate_grad(ci, acc):
            r0 = pl.multiple_of(ci * CHUNK, CHUNK)
            taps = _past_taps(xpad_ref, r0, width)
            gate = bv + taps[0] * wv[0:1, :] + taps[1] * wv[1:2, :] + taps[2] * wv[2:3, :]
            ge, dge = _gelu(gate)
            df = df_ref[pl.ds(r0, CHUNK), :]
            dup_ref[pl.ds(r0, CHUNK), :] = (df * ge).astype(dup_ref.dtype)
            d = df * up_ref[pl.ds(r0, CHUNK), :] * dge
            dpad_ref[pl.ds(r0, CHUNK), :] = d
            return tuple(acc[k] + _colsum(taps[k] * d) for k in range(width)) + (acc[width] + _colsum(d),)

        zero = jnp.zeros((1, tc), F32)
        acc = lax.fori_loop(0, S // CHUNK, gate_grad, (zero,) * (width + 1))
        for k in range(width):
            dw_ref[k:k + 1, :] = acc[k]
        db_ref[...] = acc[width]

        def input_grad(ci, carry):
            r0 = pl.multiple_of(ci * CHUNK, CHUNK)
            ahead = _future_taps(dpad_ref, r0, width)
            dx = ahead[0] * wv[2:3, :] + ahead[1] * wv[1:2, :] + ahead[2] * wv[0:1, :]
            dx_ref[pl.ds(r0, CHUNK), :] = dx.astype(dx_ref.dtype)
            return carry

        lax.fori_loop(0, S // CHUNK, input_grad, 0)

    col = pl.BlockSpec((S, tc), lambda j: (0, j))
    w_spec = pl.BlockSpec((width, tc), lambda j: (0, j))
    vec = pl.BlockSpec((1, tc), lambda j: (0, j))
    return _call(
        body,
        name="ffn_act_bwd",
        grid=(D_FF // tc,),
        in_specs=[col, col, col, w_spec, vec],
        out_specs=[col, col, w_spec, vec],
        out_shape=[jax.ShapeDtypeStruct((S, D_FF), MXU_DTYPE)] * 2
        + [jax.ShapeDtypeStruct((width, D_FF), F32), jax.ShapeDtypeStruct((1, D_FF), F32)],
        scratch_shapes=[pltpu.VMEM((S + PAD, tc), F32), pltpu.VMEM((S + PAD, tc), F32)],
        semantics=("parallel",),
        operands=(dfin, up, gpre, w, b),
        side=side,
    )


def _adamw_update(w, g, m, v):
    m = ADAM_B1 * m + (1.0 - ADAM_B1) * g
    v = ADAM_B2 * v + (1.0 - ADAM_B2) * (g * g)
    m_hat = m / (1.0 - ADAM_B1 ** ADAM_STEP)
    v_hat = v / (1.0 - ADAM_B2 ** ADAM_STEP)
    delta = -ADAM_LR * (m_hat / (jnp.sqrt(v_hat) + ADAM_EPS) + ADAM_WD * w)
    return delta, m, v


def _reduce_adamw(w, m, v, g_own, recv, *, tr, name, side=None):
    r_dim, c_dim = w.shape

    def body(w_ref, m_ref, v_ref, g_ref, recv_ref, grad_ref, delta_ref, nm_ref, nv_ref):
        g = g_ref[...]
        for k in range(1, N_DEV):
            g = g + recv_ref[k].astype(F32)
        delta, nm, nv = _adamw_update(w_ref[...], g, m_ref[...], v_ref[...])
        grad_ref[...] = g
        delta_ref[...] = delta
        nm_ref[...] = nm
        nv_ref[...] = nv

    tile = pl.BlockSpec((tr, c_dim), lambda i: (i, 0))
    return _call(
        body,
        name=name,
        grid=(r_dim // tr,),
        in_specs=[tile, tile, tile, tile, pl.BlockSpec((N_DEV, tr, c_dim), lambda i: (0, i, 0))],
        out_specs=[tile] * 4,
        out_shape=[jax.ShapeDtypeStruct((r_dim, c_dim), F32)] * 4,
        semantics=("parallel",),
        operands=(w, m, v, g_own, recv),
        side=side,
    )


def _adamw_many(ws, ms, vs, gs):
    n = len(ws)

    def body(*refs):
        for i in range(n):
            delta, nm, nv = _adamw_update(refs[i][...], refs[3 * n + i][...], refs[n + i][...], refs[2 * n + i][...])
            refs[4 * n + i][...] = delta
            refs[5 * n + i][...] = nm
            refs[6 * n + i][...] = nv

    vmem = pl.BlockSpec(memory_space=pltpu.VMEM)
    res = pl.pallas_call(
        body,
        name="adamw_small",
        in_specs=[vmem] * (4 * n),
        out_specs=[vmem] * (3 * n),
        out_shape=[jax.ShapeDtypeStruct(w.shape, F32) for w in ws] * 3,
        compiler_params=pltpu.CompilerParams(vmem_limit_bytes=VMEM_LIMIT),
    )(*ws, *ms, *vs, *gs)
    return res[:n], res[n:2 * n], res[2 * n:]


def _adamw_blocks(w, m, v, g, *, name, side=None):
    per = 2

    def body(w_ref, m_ref, v_ref, g_ref, delta_ref, nm_ref, nv_ref):
        delta, nm, nv = _adamw_update(w_ref[...], g_ref[...], m_ref[...], v_ref[...])
        delta_ref[...] = delta
        nm_ref[...] = nm
        nv_ref[...] = nv

    tile = pl.BlockSpec((1, per) + w.shape[2:], lambda i: (0, i, 0, 0))
    return _call(
        body,
        name=name,
        grid=(w.shape[1] // per,),
        in_specs=[tile] * 4,
        out_specs=[tile] * 3,
        out_shape=[jax.ShapeDtypeStruct(w.shape, F32)] * 3,
        semantics=("parallel",),
        operands=(w, m, v, g),
        side=side,
    )


def _coords():
    return lax.axis_index("x"), lax.axis_index("y"), lax.axis_index("c")


def _flip(coord, bit):
    return 1 - coord if bit else coord


def _relative(k):
    x, y, c = _coords()
    return _flip(x, k & 4), _flip(y, k & 2), _flip(c, k & 1)


def _index(pos):
    return 4 * pos[0] + 2 * pos[1] + pos[2]


FAR = (4, 2, 6)
AG_US_PER_MB = 44.0
RS_US_PER_MB = 87.0
ROW_ALIGN = 16


def _chunks(items, cursor, us, us_per_mb, through=None):
    budget = float("inf") if us is None else us / us_per_mb * 2 ** 20
    names = list(items)
    if through is not None:
        names = names[:names.index(through) + 1]
    chunks = []
    for name in names:
        arr = items[name]
        r_dim, c_dim = arr.shape[-2:]
        row_bytes = c_dim * arr.dtype.itemsize
        while cursor[name] < r_dim and budget > 0:
            rows = r_dim - cursor[name]
            if r_dim > ROW_ALIGN and budget < rows * row_bytes:
                rows = min(rows, max(ROW_ALIGN, int(budget // row_bytes) // ROW_ALIGN * ROW_ALIGN))
            chunks.append((name, cursor[name], rows))
            cursor[name] += rows
            budget -= rows * row_bytes
    return chunks


class _Gather:
    def __init__(self, shards):
        self.shards = dict(shards)
        self.bufs = {n: None for n in self.shards}
        self.cursor = {n: 0 for n in self.shards}

    def take(self, us=None, through=None):
        chunks = _chunks(self.shards, self.cursor, us, AG_US_PER_MB, through)
        return _GatherSide(self, chunks) if chunks else None

    def get(self, name):
        chunks = _chunks(self.shards, self.cursor, None, AG_US_PER_MB, through=name)
        if chunks:
            _run_side(_GatherSide(self, chunks), "gather_" + name)
        return self.bufs[name]


class _GatherSide:
    def __init__(self, owner, chunks):
        self.owner, self.chunks = owner, chunks
        self.names = list(dict.fromkeys(n for n, _, _ in chunks))
        old = [n for n in self.names if owner.bufs[n] is not None]
        self.operands = [owner.shards[n] for n in self.names] + [owner.bufs[n] for n in old]
        self.out_shape = [jax.ShapeDtypeStruct((N_DEV,) + owner.shards[n].shape, owner.shards[n].dtype)
                          for n in self.names]
        self.aliases = {len(self.names) + i: self.names.index(n) for i, n in enumerate(old)}
        self.sems = [pltpu.SemaphoreType.DMA((7 * len(chunks),)), pltpu.SemaphoreType.DMA((7 * len(chunks),)),
                     pltpu.SemaphoreType.DMA((len(chunks),))]

    def _copy(self, ins, outs, sems, ci, k, block, to, from_shard=False):
        name, r0, rows = self.chunks[ci]
        w = self.names.index(name)
        slot = outs[w].at[_index(block), pl.ds(r0, rows)]
        return pltpu.make_async_remote_copy(
            src_ref=ins[w].at[pl.ds(r0, rows)] if from_shard else slot, dst_ref=slot,
            send_sem=sems[0].at[7 * ci + k], recv_sem=sems[1].at[7 * ci + k], device_id=to, device_id_type=MESH)

    def _own(self, ins, outs, sems, ci):
        name, r0, rows = self.chunks[ci]
        w = self.names.index(name)
        return pltpu.make_async_copy(ins[w].at[pl.ds(r0, rows)], outs[w].at[_index(_relative(0)), pl.ds(r0, rows)],
                                     sems[2].at[ci])

    def start(self, ins, outs, sems):
        me, sibling = _relative(0), _relative(1)
        for ci in range(len(self.chunks)):
            self._own(ins, outs, sems, ci).start()
        for j, k in enumerate(FAR):
            for ci in range(len(self.chunks)):
                self._copy(ins, outs, sems, ci, 1 + j, me, _relative(k), from_shard=True).start()
        for ci in range(len(self.chunks)):
            self._copy(ins, outs, sems, ci, 0, me, sibling, from_shard=True).start()

    def finish(self, ins, outs, sems):
        me, sibling = _relative(0), _relative(1)
        n = len(self.chunks)
        for j, k in enumerate(FAR):
            for ci in range(n):
                self._copy(ins, outs, sems, ci, 1 + j, _relative(k), me).wait_recv()
                self._copy(ins, outs, sems, ci, 4 + j, _relative(k), sibling).start()
        for ci in range(n):
            self._copy(ins, outs, sems, ci, 0, sibling, me).wait_recv()
        for j, k in enumerate(FAR):
            for ci in range(n):
                self._copy(ins, outs, sems, ci, 4 + j, _relative(k | 1), me).wait_recv()
        for ci in range(n):
            self._copy(ins, outs, sems, ci, 0, me, sibling, from_shard=True).wait_send()
            for j, k in enumerate(FAR):
                self._copy(ins, outs, sems, ci, 1 + j, me, _relative(k), from_shard=True).wait_send()
                self._copy(ins, outs, sems, ci, 4 + j, _relative(k), sibling).wait_send()
            self._own(ins, outs, sems, ci).wait()

    def done(self, results):
        for n, buf in zip(self.names, results):
            self.owner.bufs[n] = buf


class _Scatter:
    def __init__(self, me):
        self.me = me
        self.sends, self.owns, self.bufs, self.cursor = {}, {}, {}, {}

    def add(self, name, send, own):
        self.sends[name] = send
        self.owns[name] = own
        self.bufs[name] = None
        self.cursor[name] = 0

    def add_blocks(self, name, blocks32, blocks16):
        self.add(name, blocks16, lax.dynamic_index_in_dim(blocks32, self.me, axis=0, keepdims=False))

    def add_cols(self, name, full32, full16):
        width = full32.shape[1] // N_DEV
        self.add(name, _blocks(full16, "cols"), lax.dynamic_slice_in_dim(full32, self.me * width, width, axis=1))

    def take(self, us=None):
        chunks = _chunks(self.sends, self.cursor, us, RS_US_PER_MB)
        return _ScatterSide(self, chunks) if chunks else None

    def get(self, name):
        chunks = _chunks(self.sends, self.cursor, None, RS_US_PER_MB, through=name)
        if chunks:
            _run_side(_ScatterSide(self, chunks), "scatter_" + name)
        return self.owns[name], self.bufs[name]


class _ScatterSide:
    ORDER = (4, 2, 6, 1, 5, 3, 7)

    def __init__(self, owner, chunks):
        self.owner, self.chunks = owner, chunks
        self.names = list(dict.fromkeys(n for n, _, _ in chunks))
        old = [n for n in self.names if owner.bufs[n] is not None]
        self.operands = [owner.sends[n] for n in self.names] + [owner.bufs[n] for n in old]
        self.out_shape = [jax.ShapeDtypeStruct(owner.sends[n].shape, BF16) for n in self.names]
        self.aliases = {len(self.names) + i: self.names.index(n) for i, n in enumerate(old)}
        self.sems = [pltpu.SemaphoreType.DMA((7 * len(chunks),)), pltpu.SemaphoreType.DMA((7 * len(chunks),))]

    def _copy(self, ins, outs, sems, ci, k):
        name, r0, rows = self.chunks[ci]
        w = self.names.index(name)
        return pltpu.make_async_remote_copy(
            src_ref=ins[w].at[_index(_relative(k)), pl.ds(r0, rows)], dst_ref=outs[w].at[k, pl.ds(r0, rows)],
            send_sem=sems[0].at[7 * ci + k - 1], recv_sem=sems[1].at[7 * ci + k - 1],
            device_id=_relative(k), device_id_type=MESH)

    def start(self, ins, outs, sems):
        for ci in range(len(self.chunks)):
            for k in self.ORDER:
                self._copy(ins, outs, sems, ci, k).start()

    def finish(self, ins, outs, sems):
        for ci in range(len(self.chunks)):
            for k in self.ORDER:
                self._copy(ins, outs, sems, ci, k).wait()

    def done(self, results):
        for n, buf in zip(self.names, results):
            self.owner.bufs[n] = buf


def _all_reduce_small(vecs, mats):
    nv, nm = len(vecs), len(mats)
    row0, rows = [], 0
    for v in vecs:
        row0.append(rows)
        rows += v.shape[0]
    rows = -(-rows // 8) * 8
    width = max(v.shape[1] for v in vecs)
    per = mats[0].shape[0] // N_DEV

    def body(*refs):
        v_in, m_in = refs[:nv], refs[nv:nv + nm]
        v_out, m_out = refs[nv + nm:2 * nv + nm], refs[2 * nv + nm:2 * (nv + nm)]
        mine, inbox, total, stage, v_send, v_recv, a_send, a_recv, b_send, b_recv = refs[2 * (nv + nm):]
        me = _index(_relative(0))

        def vec_copy(k, sender):
            return pltpu.make_async_remote_copy(
                src_ref=mine, dst_ref=inbox.at[_index(sender)],
                send_sem=v_send.at[k - 1], recv_sem=v_recv.at[k - 1], device_id=_relative(k), device_id_type=MESH)

        def part(ref, pos):
            return ref.at[pl.ds(per * _index(pos), per)]

        def to_owner(k, j):
            return pltpu.make_async_remote_copy(
                src_ref=part(m_in[j], _relative(k)), dst_ref=stage.at[k, j],
                send_sem=a_send.at[nm * (k - 1) + j], recv_sem=a_recv.at[nm * (k - 1) + j],
                device_id=_relative(k), device_id_type=MESH)

        def from_owner(k, j, owner):
            return pltpu.make_async_remote_copy(
                src_ref=part(m_out[j], owner), dst_ref=part(m_out[j], owner),
                send_sem=b_send.at[nm * (k - 1) + j], recv_sem=b_recv.at[nm * (k - 1) + j],
                device_id=_relative(k), device_id_type=MESH)

        mine[...] = jnp.zeros_like(mine)
        for v, r0 in zip(v_in, row0):
            mine[r0:r0 + v.shape[0], 0:v.shape[1]] = v[...]
        for k in range(1, N_DEV):
            vec_copy(k, _relative(0)).start()
            for j in range(nm):
                to_owner(k, j).start()
        inbox[me] = mine[...]

        for j in range(nm):
            acc = m_in[j][pl.ds(per * me, per)]
            for k in range(1, N_DEV):
                to_owner(k, j).wait_recv()
                acc = acc + stage[k, j]
            m_out[j][pl.ds(per * me, per)] = acc
            for k in range(1, N_DEV):
                from_owner(k, j, _relative(0)).start()

        for k in range(1, N_DEV):
            vec_copy(k, _relative(k)).wait_recv()
        acc = inbox[0]
        for d in range(1, N_DEV):
            acc = acc + inbox[d]
        total[...] = acc
        for v, r0 in zip(v_out, row0):
            v[...] = total[r0:r0 + v.shape[0], 0:v.shape[1]]

        for k in range(1, N_DEV):
            for j in range(nm):
                from_owner(k, j, _relative(k)).wait_recv()
        for k in range(1, N_DEV):
            vec_copy(k, _relative(0)).wait_send()
            for j in range(nm):
                to_owner(k, j).wait_send()
                from_owner(k, j, _relative(0)).wait_send()

    vmem = pl.BlockSpec(memory_space=pltpu.VMEM)
    res = pl.pallas_call(
        body,
        name="all_reduce_small",
        in_specs=[vmem] * (nv + nm),
        out_specs=[vmem] * (nv + nm),
        out_shape=[jax.ShapeDtypeStruct(a.shape, F32) for a in (*vecs, *mats)],
        scratch_shapes=[pltpu.VMEM((rows, width), F32), pltpu.VMEM((N_DEV, rows, width), F32),
                        pltpu.VMEM((rows, width), F32), pltpu.VMEM((N_DEV, nm, per) + mats[0].shape[1:], F32),
                        pltpu.SemaphoreType.DMA((N_DEV - 1,)), pltpu.SemaphoreType.DMA((N_DEV - 1,))]
        + [pltpu.SemaphoreType.DMA((nm * (N_DEV - 1),))] * 4,
        compiler_params=pltpu.CompilerParams(vmem_limit_bytes=VMEM_LIMIT),
    )(*vecs, *mats)
    return res[:nv], res[nv:]


def _block_diag(w):
    w4 = w.reshape(N_RNN_GROUPS, 4, RNN_BLOCK_W, RNN_BLOCK_W)
    eye = jnp.eye(4, dtype=w.dtype)
    return (w4[:, :, :, None, :] * eye[None, :, None, :, None]).reshape(N_RNN_GROUPS, RNN_GROUP, RNN_GROUP)


def _diag_blocks(wg):
    w5 = wg.reshape(N_RNN_GROUPS, 4, RNN_BLOCK_W, 4, RNN_BLOCK_W)
    return jnp.stack([w5[:, b, :, b, :] for b in range(4)], axis=1).reshape(16, RNN_BLOCK_W, RNN_BLOCK_W)


def _heads_major(t, n_heads):
    return t.reshape(S, n_heads, HEAD_DIM).transpose(1, 0, 2)


def _heads_minor(t):
    return t.transpose(1, 0, 2).reshape(S, t.shape[0] * HEAD_DIM)


def _natural(gathered, how):
    n, r, c = gathered.shape
    if how == "rows":
        return gathered.reshape(n * r, c)
    return gathered.transpose(1, 0, 2).reshape(r, n * c)


def _blocks(full, how):
    if how == "rows":
        return full.reshape(N_DEV, full.shape[0] // N_DEV, full.shape[1])
    return full.reshape(full.shape[0], N_DEV, full.shape[1] // N_DEV).transpose(1, 0, 2)


def _forward_backward(x2, target, small, gather, scatter):
    xb = x2.astype(MXU_DTYPE)
    w_in = _natural(gather.get("w_in"), "cols")
    proj, projb = _mm(xb, w_in, tm=1024, tn=512, tk=D, out_dtype=(F32, MXU_DTYPE), name="proj", side=gather.take(110))

    q = _heads_major(projb[:, :OFF_K], N_KV * GROUP).reshape(N_KV, GROUP, S, HEAD_DIM)
    front = ((0, 0), (BLOCK, 0), (0, 0))
    kp = jnp.pad(_heads_major(projb[:, OFF_K:OFF_V], N_KV), front)
    vp = jnp.pad(_heads_major(projb[:, OFF_V:OFF_RX], N_KV), front)
    sink_rows = jnp.repeat(small["attn_sinks"].reshape(N_KV, GROUP, 1), BLOCK, axis=1).reshape(N_KV, GROUP * BLOCK, 1)
    o_hm = _attn_fwd(q, kp, vp, sink_rows, side=gather.take(120))
    o = _heads_minor(o_hm.reshape(N_KV * GROUP, S, HEAD_DIM))

    rconv_w = _natural(gather.get("rnn_conv_w"), "cols")
    rxc = _conv_fwd(proj, OFF_RX, rconv_w, small["rnn_conv_b"], tc=512, name="rnn_conv_fwd", side=gather.take(18))
    r, i = _lru_gates(rxc, small["lru_wa"], small["lru_wi"], small["lru_ba"], small["lru_bi"], side=gather.take(33))
    h, yrin = _lru_scan_fwd(r, i, rxc, proj, small["lru_lambda"], side=gather.take(53))

    w_ap = _natural(gather.get("w_attn_proj"), "rows")
    w_rp = _natural(gather.get("w_rnn_proj"), "rows")
    y_attn = _mm(o, w_ap, tm=1024, tn=1024, tk=D, name="attn_proj", side=gather.take(22))
    y_rnn = _mm(yrin, w_rp, tm=1024, tn=1024, tk=D_RNN, name="rnn_proj", side=gather.take(27))
    mixin = _gate_fwd(y_attn, y_rnn, proj, small["b_gate"], side=gather.take(25))
    w_out = _natural(gather.get("w_out"), "rows")
    mix = _mm(mixin, w_out, tm=1024, tn=1024, tk=D, name="mix_out", side=gather.take(22))
    x1, x1b, xhat1, rstd1 = _ln_fwd(x2, mix, small["ln1_g"], small["ln1_b"], side=gather.take(23))

    w_up = gather.get("ffn_w_up")
    up = _mm(x1b, w_up, tm=1024, tn=768, tk=D, b_block=768, name="ffn_up", side=gather.take(58))
    w_gate = gather.get("ffn_w_gate")
    gpre = _mm(x1b, w_gate, tm=1024, tn=768, tk=D, b_block=768, name="ffn_gate", side=gather.take(58))
    fconv_w = _natural(gather.get("ffn_conv_w"), "cols")
    fin = _ffn_act_fwd(up, gpre, fconv_w, small["ffn_conv_b"], side=gather.take())
    w_down = _natural(gather.get("ffn_w_down"), "rows")
    f = _mm(fin, w_down, tm=1024, tn=1024, tk=2048, name="ffn_down")
    loss, dpre2, dpre2b, d_ln2_g, d_ln2_b = _ln_loss_bwd(x1, f, small["ln2_g"], small["ln2_b"], target)

    grads = {"ln2_g": d_ln2_g, "ln2_b": d_ln2_b}
    both = (F32, BF16)
    g32, g16 = _mm(fin, dpre2b, ta=True, tm=1024, tn=1024, tk=S, out_dtype=both, name="d_ffn_w_down")
    scatter.add_blocks("ffn_w_down", _blocks(g32, "rows"), _blocks(g16, "rows"))
    dfin = _mm(dpre2b, w_down, tb=True, tm=1024, tn=1024, tk=D, name="d_fin", side=scatter.take(57))
    dup, dgpre, grads["ffn_conv_w"], grads["ffn_conv_b"] = _ffn_act_bwd(
        dfin, up, gpre, fconv_w, small["ffn_conv_b"], side=scatter.take(85))
    g32, g16 = _mm(x1b, dup, ta=True, tm=1024, tn=768, tk=S, out_dtype=both, out_block=768, name="d_ffn_w_up",
                   side=scatter.take(57))
    scatter.add_blocks("ffn_w_up", g32, g16)
    g32, g16 = _mm(x1b, dgpre, ta=True, tm=1024, tn=768, tk=S, out_dtype=both, out_block=768, name="d_ffn_w_gate",
                   side=scatter.take(56))
    scatter.add_blocks("ffn_w_gate", g32, g16)
    dx1 = _mm(dup, w_up, tb=True, tm=1024, tn=1024, tk=768, b_block=768, name="d_x1_up", side=scatter.take(68))
    dx1 = _mm(dgpre, w_gate, tb=True, tm=1024, tn=1024, tk=768, b_block=768, add=dx1, name="d_x1_gate",
              side=scatter.take(70))
    dpre1, dpre1b, grads["ln1_g"], grads["ln1_b"] = _ln_bwd(dx1, dpre2, xhat1, rstd1, small["ln1_g"],
                                                            side=scatter.take(24))

    g32, g16 = _mm(mixin, dpre1b, ta=True, tm=1024, tn=1024, tk=S, out_dtype=both, name="d_w_out",
                   side=scatter.take(26))
    scatter.add_blocks("w_out", _blocks(g32, "rows"), _blocks(g16, "rows"))
    dmix = _mm(dpre1b, w_out, tb=True, tm=1024, tn=1024, tk=D, name="d_mixin", side=scatter.take(22))
    dya, dyr, dgl_a, dgl_r, db_a, db_r = _gate_bwd(dmix, y_attn, y_rnn, proj, small["b_gate"], side=scatter.take(36))
    grads["b_gate"] = jnp.concatenate([db_a, db_r], axis=1)
    g32, g16 = _mm(o, dya, ta=True, tm=1024, tn=1024, tk=S, out_dtype=both, name="d_w_attn_proj",
                   side=scatter.take(38))
    scatter.add_blocks("w_attn_proj", _blocks(g32, "rows"), _blocks(g16, "rows"))
    g32, g16 = _mm(yrin, dyr, ta=True, tm=1280, tn=1024, tk=S, out_dtype=both, name="d_w_rnn_proj",
                   side=scatter.take(27))
    scatter.add_blocks("w_rnn_proj", _blocks(g32, "rows"), _blocks(g16, "rows"))
    do = _mm(dya, w_ap, tb=True, tm=1024, tn=1024, tk=D, out_dtype=MXU_DTYPE, name="d_o", side=scatter.take(22))
    dyrin = _mm(dyr, w_rp, tb=True, tm=1024, tn=1280, tk=D, name="d_yrin", side=scatter.take(27))

    dry, dzr, dzi, drxc_in, grads["lru_ba"], grads["lru_bi"], grads["lru_lambda"] = _lru_scan_bwd(
        dyrin, proj, h, r, i, rxc, small["lru_lambda"], side=scatter.take(94))
    grads["lru_wa"], grads["lru_wi"] = _lru_gate_wgrad(rxc, dzr, dzi, side=scatter.take(22))
    drxc = _lru_gate_xgrad(dzr, dzi, small["lru_wa"], small["lru_wi"], drxc_in, side=scatter.take(33))
    drx, grads["rnn_conv_w"], grads["rnn_conv_b"] = _conv_bwd(drxc, proj, OFF_RX, rconv_w, tc=512,
                                                             name="rnn_conv_bwd", side=scatter.take(29))

    do_hm = _heads_major(do, N_KV * GROUP).reshape(N_KV, GROUP, S, HEAD_DIM)
    dq, dk, dv, dsink = _attn_bwd(q, kp, vp, sink_rows, do_hm, side=scatter.take(220))
    grads["attn_sinks"] = dsink.reshape(1, N_KV * GROUP)
    dproj = jnp.concatenate([
        _heads_minor(dq.reshape(N_KV * GROUP, S, HEAD_DIM)),
        _heads_minor(dk[:, BLOCK:, :]).astype(MXU_DTYPE),
        _heads_minor(dv[:, BLOCK:, :]).astype(MXU_DTYPE),
        drx, dry, dgl_a, dgl_r], axis=1)
    g32, g16 = _mm(xb, dproj, ta=True, tm=1024, tn=512, tk=S, out_dtype=both, name="d_w_in", side=scatter.take(110))
    scatter.add_cols("w_in", g32, g16)
    dx = _mm(dproj, w_in, tb=True, tm=1024, tn=1024, tk=512, add=dpre1, add_scale=ALPHA, name="d_x",
             side=scatter.take(150))
    return loss, dx, grads


SHARDED = (
    ("w_in", "cols", 128), ("w_attn_proj", "rows", 128), ("w_rnn_proj", "rows", 160), ("w_out", "rows", 128),
    ("ffn_w_up", "cols", 256), ("ffn_w_gate", "cols", 256), ("ffn_w_down", "rows", 256),
)
SMALL_REPLICATED = ("b_gate", "rnn_conv_b", "lru_wa", "lru_ba", "lru_wi", "lru_bi", "lru_lambda", "attn_sinks",
                    "ln1_g", "ln1_b", "ffn_conv_b", "ln2_g", "ln2_b")
SMALL_SHARDED = ("rnn_conv_w", "ffn_conv_w")
SMALL_MATS = ("lru_wa", "lru_wi")
WEIGHTS = ("w_in", "b_gate", "rnn_conv_w", "rnn_conv_b", "lru_wa", "lru_ba", "lru_wi", "lru_bi", "lru_lambda",
           "attn_sinks", "w_attn_proj", "w_rnn_proj", "w_out", "ln1_g", "ln1_b", "ffn_w_up", "ffn_w_gate",
           "ffn_conv_w", "ffn_conv_b", "ffn_w_down", "ln2_g", "ln2_b")
ADAMW_US = {"w_attn_proj": 9, "w_rnn_proj": 11, "w_out": 9, "ffn_w_up": 26, "ffn_w_gate": 26, "ffn_w_down": 28}


def kernel(x, w_in, b_gate, rnn_conv_w, rnn_conv_b, lru_wa, lru_ba, lru_wi, lru_bi, lru_lambda, attn_sinks, w_attn_proj, w_rnn_proj, w_out, ln1_g, ln1_b, ffn_w_up, ffn_w_gate, ffn_conv_w, ffn_conv_b, ffn_w_down, ln2_g, ln2_b, loss_target, m_w_in, m_b_gate, m_rnn_conv_w, m_rnn_conv_b, m_lru_wa, m_lru_ba, m_lru_wi, m_lru_bi, m_lru_lambda, m_attn_sinks, m_w_attn_proj, m_w_rnn_proj, m_w_out, m_ln1_g, m_ln1_b, m_ffn_w_up, m_ffn_w_gate, m_ffn_conv_w, m_ffn_conv_b, m_ffn_w_down, m_ln2_g, m_ln2_b, v_w_in, v_b_gate, v_rnn_conv_w, v_rnn_conv_b, v_lru_wa, v_lru_ba, v_lru_wi, v_lru_bi, v_lru_lambda, v_attn_sinks, v_w_attn_proj, v_w_rnn_proj, v_w_out, v_ln1_g, v_ln1_b, v_ffn_w_up, v_ffn_w_gate, v_ffn_conv_w, v_ffn_conv_b, v_ffn_w_down, v_ln2_g, v_ln2_b):
    given = dict(locals())
    wsh = {n: given[n][0] for n in WEIGHTS}
    msh = {n: given["m_" + n][0] for n in WEIGHTS}
    vsh = {n: given["v_" + n][0] for n in WEIGHTS}
    m_given = {n: given["m_" + n] for n in WEIGHTS}
    v_given = {n: given["v_" + n] for n in WEIGHTS}
    me = 4 * lax.axis_index("x") + 2 * lax.axis_index("y") + lax.axis_index("c")

    order = ("w_in", "rnn_conv_w", "ffn_conv_w", "w_attn_proj", "w_rnn_proj", "w_out", "ffn_w_up", "ffn_w_gate",
             "ffn_w_down")
    gather = _Gather({n: wsh[n] if n in SMALL_SHARDED else wsh[n].astype(MXU_DTYPE) for n in order})
    _run_side(gather.take(through="ffn_conv_w"), "gather_first")
    small = {n: given[n] for n in SMALL_REPLICATED}
    small["lru_wa"] = _block_diag(wsh["lru_wa"])
    small["lru_wi"] = _block_diag(wsh["lru_wi"])
    scatter = _Scatter(me)

    loss, dx, grads = _forward_backward(x[0], loss_target[0], small, gather, scatter)
    grads["lru_wa"] = _diag_blocks(grads["lru_wa"])
    grads["lru_wi"] = _diag_blocks(grads["lru_wi"])

    vec_names = tuple(n for n in SMALL_REPLICATED if n not in SMALL_MATS) + SMALL_SHARDED
    sums, mat_sums = _all_reduce_small([loss] + [grads[n] for n in vec_names], [grads[n] for n in SMALL_MATS])
    loss_total = sums[0].reshape(())
    g_small = dict(zip(vec_names, sums[1:]))
    for n in SMALL_SHARDED:
        width = wsh[n].shape[1]
        g_small[n] = lax.dynamic_slice_in_dim(g_small[n], me * width, width, axis=1)
    g_small = {n: g_small[n].reshape(given[n].shape) for n in vec_names}
    out = {}
    results = _adamw_many(*[[d[n] for n in vec_names] for d in (given, m_given, v_given, g_small)])
    for n, delta, nm, nv in zip(vec_names, *results):
        out[n] = (g_small[n], delta, nm, nv)
    for n, g in zip(SMALL_MATS, mat_sums):
        g = g.reshape(given[n].shape)
        out[n] = (g, *_adamw_blocks(given[n], m_given[n], v_given[n], g, name="adamw_" + n, side=scatter.take(4)))

    tile_rows = {n: tr for n, _, tr in SHARDED}
    for n in list(scatter.sends):
        own, recv = scatter.get(n)
        res = _reduce_adamw(wsh[n], msh[n], vsh[n], own, recv, tr=tile_rows[n], name="adamw_" + n,
                            side=scatter.take(0.9 * ADAMW_US[n]) if n in ADAMW_US else None)
        out[n] = tuple(r[None] for r in res)

    outputs = [loss_total, dx[None]]
    for kind in range(4):
        outputs += [out[n][kind] for n in WEIGHTS]
    return tuple(outputs)
```

```python
import math

import jax
import jax.numpy as jnp
from jax import lax
from jax.experimental import pallas as pl
from jax.experimental.pallas import tpu as pltpu

F32 = jnp.float32
BF16 = jnp.bfloat16
MXU_DTYPE = jnp.bfloat16

N_DEV = 8
S = 2048
D = 2048
HEAD_DIM = 64
N_KV = 4
GROUP = 8
BLOCK = 128
D_KV = N_KV * HEAD_DIM
D_RNN = 2560
RNN_GROUP = 640
N_RNN_GROUPS = D_RNN // RNN_GROUP
RNN_BLOCK_W = 160
RNN_CONV_W = 4
LRU_C = 8.0
D_FF = 6144
FFN_CONV_W = 3
D_IN = 11776
OFF_K = 2048
OFF_V = 2304
OFF_RX = 2560
OFF_RY = 5120
OFF_GA = 7680
OFF_GR = 9728
LN_EPS = 1e-5
ALPHA = 2.0 ** 0.25
ADAM_LR = 0.001
ADAM_B1 = 0.9
ADAM_B2 = 0.999
ADAM_EPS = 1e-08
ADAM_WD = 0.01
ADAM_STEP = 10
NEG = -1e30
VMEM_LIMIT = 56 * 1024 * 1024
MESH = pl.DeviceIdType.MESH
GELU_C = math.sqrt(2.0 / math.pi)


def _cparams(*sem):
    return pltpu.CompilerParams(dimension_semantics=sem or None, vmem_limit_bytes=VMEM_LIMIT)


def _call(body, *, name, grid, in_specs, out_specs, out_shape, operands, semantics, scratch_shapes=(), side=None):
    single = not isinstance(out_shape, (list, tuple))
    out_shape = [out_shape] if single else list(out_shape)
    out_specs = [out_specs] if single else list(out_specs)
    in_specs = list(in_specs)
    scratch_shapes = list(scratch_shapes)
    if side is None:
        res = pl.pallas_call(
            body, name=name, grid=grid, in_specs=in_specs, out_specs=out_specs, out_shape=out_shape,
            scratch_shapes=scratch_shapes, compiler_params=_cparams(*semantics))(*operands)
        return res[0] if single else res
    n_in, n_out, n_scr = len(in_specs), len(out_shape), len(scratch_shapes)
    s_in, s_out = len(side.operands), len(side.out_shape)
    hbm = pl.BlockSpec(memory_space=pltpu.HBM)

    def with_copies(*refs):
        core_in, side_in = refs[:n_in], refs[n_in:n_in + s_in]
        o0 = n_in + s_in
        core_out, side_out = refs[o0:o0 + n_out], refs[o0 + n_out:o0 + n_out + s_out]
        c0 = o0 + n_out + s_out
        core_scr, sems = refs[c0:c0 + n_scr], refs[c0 + n_scr:]
        first, last = None, None
        for d, size in enumerate(grid):
            at_start, at_end = pl.program_id(d) == 0, pl.program_id(d) == size - 1
            first = at_start if first is None else first & at_start
            last = at_end if last is None else last & at_end

        @pl.when(first)
        def _():
            side.start(side_in, side_out, sems)

        body(*core_in, *core_out, *core_scr)

        @pl.when(last)
        def _():
            side.finish(side_in, side_out, sems)

    res = pl.pallas_call(
        with_copies, name=name, grid=grid,
        in_specs=in_specs + [hbm] * s_in, out_specs=out_specs + [hbm] * s_out,
        out_shape=out_shape + list(side.out_shape),
        scratch_shapes=scratch_shapes + list(side.sems),
        input_output_aliases={n_in + i: n_out + o for i, o in side.aliases.items()},
        compiler_params=_cparams(*(("arbitrary",) * len(grid))))(*operands, *side.operands)
    side.done(res[n_out:])
    return res[0] if single else res[:n_out]


def _run_side(side, name):
    def body(*refs):
        s_in, s_out = len(side.operands), len(side.out_shape)
        side.start(refs[:s_in], refs[s_in:s_in + s_out], refs[s_in + s_out:])
        side.finish(refs[:s_in], refs[s_in:s_in + s_out], refs[s_in + s_out:])

    hbm = pl.BlockSpec(memory_space=pltpu.HBM)
    res = pl.pallas_call(
        body, name=name, in_specs=[hbm] * len(side.operands), out_specs=[hbm] * len(side.out_shape),
        out_shape=list(side.out_shape), scratch_shapes=list(side.sems),
        input_output_aliases=dict(side.aliases))(*side.operands)
    side.done(res)


def _gelu(x):
    x2 = x * x
    t = jnp.tanh(GELU_C * (x + 0.044715 * x * x2))
    g = 0.5 * x * (1.0 + t)
    dg = 0.5 * (1.0 + t) + 0.5 * x * (1.0 - t * t) * (GELU_C * (1.0 + 3.0 * 0.044715 * x2))
    return g, dg


def _sigmoid(x):
    return 1.0 / (1.0 + jnp.exp(-x))


def _softplus(x):
    z = jnp.exp(-jnp.abs(x))
    small = z * (1.0 - z * (0.5 - z * (1.0 / 3.0 - 0.25 * z)))
    return jnp.maximum(x, 0.0) + jnp.where(z < 0.02, small, jnp.log(1.0 + z))


def _one_minus_exp(x):
    series = -x * (1.0 + x * (0.5 + x * (1.0 / 6.0 + x * (1.0 / 24.0))))
    return jnp.where(x > -0.03, series, 1.0 - jnp.exp(x))


def _colsum(v):
    return jnp.sum(v, axis=0, keepdims=True)


def _mm(a, b, *, tm, tn, tk, name, ta=False, tb=False, out_dtype=F32, b_block=None, out_block=None, add=None,
        add_scale=1.0, side=None):
    out_dtypes = out_dtype if isinstance(out_dtype, tuple) else (out_dtype,)
    if ta:
        k_dim, m_dim = a.shape
    else:
        m_dim, k_dim = a.shape
    if b_block is None:
        n_dim = b.shape[0] if tb else b.shape[1]
    else:
        n_dim = b.shape[1] if tb else b.shape[0] * b_block
    assert m_dim % tm == 0 and n_dim % tn == 0 and k_dim % tk == 0, (name, m_dim, n_dim, k_dim)
    nk = k_dim // tk
    dims = (((0 if ta else 1,), (1 if tb else 0,)), ((), ()))
    has_add = add is not None

    def body(*refs):
        a_ref, b_ref = refs[0], refs[1]
        add_ref = refs[2] if has_add else None
        first_out = 3 if has_add else 2
        o_refs = refs[first_out:first_out + len(out_dtypes)]

        def product():
            return lax.dot_general(a_ref[...].astype(MXU_DTYPE), b_ref[...].astype(MXU_DTYPE), dims,
                                   preferred_element_type=F32)

        def finish(acc):
            if has_add:
                acc = acc + add_scale * add_ref[...]
            for o_ref in o_refs:
                o_ref[...] = acc.astype(o_ref.dtype)

        if nk == 1:
            finish(product())
        else:
            acc_ref = refs[-1]
            k = pl.program_id(2)

            @pl.when(k == 0)
            def _():
                acc_ref[...] = jnp.zeros_like(acc_ref)

            acc_ref[...] += product()

            @pl.when(k == nk - 1)
            def _():
                finish(acc_ref[...])

    if ta:
        a_spec = pl.BlockSpec((tk, tm), lambda i, j, k: (k, i))
    else:
        a_spec = pl.BlockSpec((tm, tk), lambda i, j, k: (i, k))
    if b_block is None:
        if tb:
            b_spec = pl.BlockSpec((tn, tk), lambda i, j, k: (j, k))
        else:
            b_spec = pl.BlockSpec((tk, tn), lambda i, j, k: (k, j))
    elif tb:
        assert b_block % tk == 0
        b_spec = pl.BlockSpec((None, tn, tk), lambda i, j, k: ((k * tk) // b_block, j, ((k * tk) % b_block) // tk))
    else:
        assert b_block % tn == 0
        b_spec = pl.BlockSpec((None, tk, tn), lambda i, j, k: ((j * tn) // b_block, k, ((j * tn) % b_block) // tn))
    in_specs = [a_spec, b_spec]
    operands = [a, b]
    if has_add:
        in_specs.append(pl.BlockSpec((tm, tn), lambda i, j, k: (i, j)))
        operands.append(add)
    if out_block is None:
        out_spec = pl.BlockSpec((tm, tn), lambda i, j, k: (i, j))
        out_dims = (m_dim, n_dim)
    else:
        assert out_block % tn == 0
        out_spec = pl.BlockSpec((None, tm, tn), lambda i, j, k: ((j * tn) // out_block, i, ((j * tn) % out_block) // tn))
        out_dims = (n_dim // out_block, m_dim, out_block)
    res = _call(
        body,
        name=name,
        grid=(m_dim // tm, n_dim // tn, nk),
        in_specs=in_specs,
        out_specs=[out_spec] * len(out_dtypes),
        out_shape=[jax.ShapeDtypeStruct(out_dims, dt) for dt in out_dtypes],
        scratch_shapes=[pltpu.VMEM((tm, tn), F32)] if nk > 1 else [],
        semantics=("parallel", "parallel", "arbitrary"),
        operands=tuple(operands),
        side=side,
    )
    return res if isinstance(out_dtype, tuple) else res[0]


def _attn_bias(bias_ref, h):
    key = lax.broadcasted_iota(jnp.int32, (2 * BLOCK, GROUP * BLOCK), 0)
    col = lax.broadcasted_iota(jnp.int32, (2 * BLOCK, GROUP * BLOCK), 1)
    dist = BLOCK + (col & (BLOCK - 1)) - key
    head = h * GROUP + (col >> 7) + 1
    slope = jnp.exp(head.astype(F32) * (-0.25 * math.log(2.0)))
    bias = jnp.where((dist >= 0) & (dist < BLOCK), -slope * dist.astype(F32), NEG)
    bias_ref[1] = bias
    bias_ref[0] = jnp.where(key < BLOCK, NEG, bias)


def _attn_probs(kb, qt, bias, sink):
    s = jnp.dot(kb, qt, preferred_element_type=F32) * (HEAD_DIM ** -0.5) + bias
    m = jnp.maximum(jnp.max(s, axis=0, keepdims=True), sink)
    e = jnp.exp(s - m)
    e_sink = jnp.exp(sink - m)
    inv = 1.0 / (jnp.sum(e, axis=0, keepdims=True) + e_sink)
    return e * inv, e_sink * inv


def _heads_on_lanes(ref, r0):
    return jnp.concatenate([ref[g, :, pl.ds(r0, BLOCK)] for g in range(GROUP)], axis=1)


def _attn_fwd(qt, kp, vt, sink_row, side=None):
    cols = GROUP * BLOCK

    def body(q_ref, k_ref, vt_ref, sink_ref, o_ref, bias_ref):
        _attn_bias(bias_ref, pl.program_id(0))
        sink = sink_ref[...]

        def step(n, carry):
            r0 = pl.multiple_of(n * BLOCK, BLOCK)
            p, _ = _attn_probs(k_ref[pl.ds(r0, 2 * BLOCK), :], _heads_on_lanes(q_ref, r0),
                               bias_ref[jnp.minimum(n, 1)], sink)
            o = jnp.dot(vt_ref[:, pl.ds(r0, 2 * BLOCK)], p.astype(MXU_DTYPE), preferred_element_type=F32)
            for g in range(GROUP):
                o_ref[g, :, pl.ds(r0, BLOCK)] = o[:, g * BLOCK:(g + 1) * BLOCK].astype(o_ref.dtype)
            return carry

        lax.fori_loop(0, S // BLOCK, step, 0)

    hm = pl.BlockSpec((None, GROUP, HEAD_DIM, S), lambda h: (h, 0, 0, 0))
    return _call(
        body,
        name="attn_fwd",
        grid=(N_KV,),
        in_specs=[
            hm,
            pl.BlockSpec((None, BLOCK + S, HEAD_DIM), lambda h: (h, 0, 0)),
            pl.BlockSpec((None, HEAD_DIM, BLOCK + S), lambda h: (h, 0, 0)),
            pl.BlockSpec((None, 1, cols), lambda h: (h, 0, 0)),
        ],
        out_specs=hm,
        out_shape=jax.ShapeDtypeStruct((N_KV, GROUP, HEAD_DIM, S), MXU_DTYPE),
        scratch_shapes=[pltpu.VMEM((2, 2 * BLOCK, cols), F32)],
        semantics=("parallel",),
        operands=(qt, kp, vt, sink_row),
        side=side,
    )


def _attn_bwd(qt, kp, kt, vp, sink_row, dot_, side=None):
    cols = GROUP * BLOCK

    def body(q_ref, k_ref, kt_ref, v_ref, sink_ref, do_ref, dq_ref, dk_ref, dv_ref, dsink_ref, bias_ref):
        _attn_bias(bias_ref, pl.program_id(0))
        sink = sink_ref[...]
        dk_ref[...] = jnp.zeros_like(dk_ref)
        dv_ref[...] = jnp.zeros_like(dv_ref)
        nt = (((1,), (1,)), ((), ()))

        def step(n, sink_acc):
            r0 = pl.multiple_of(n * BLOCK, BLOCK)
            band = pl.ds(r0, 2 * BLOCK)
            qn = _heads_on_lanes(q_ref, r0)
            don = _heads_on_lanes(do_ref, r0)
            p, p_sink = _attn_probs(k_ref[band, :], qn, bias_ref[jnp.minimum(n, 1)], sink)
            dp = jnp.dot(v_ref[band, :], don, preferred_element_type=F32)
            delta = jnp.sum(p * dp, axis=0, keepdims=True)
            ds = (p * (dp - delta) * (HEAD_DIM ** -0.5)).astype(MXU_DTYPE)
            dq = jnp.dot(kt_ref[:, band], ds, preferred_element_type=F32)
            for g in range(GROUP):
                dq_ref[g, :, pl.ds(r0, BLOCK)] = dq[:, g * BLOCK:(g + 1) * BLOCK].astype(dq_ref.dtype)
            dk_ref[band, :] += lax.dot_general(ds, qn, nt, preferred_element_type=F32)
            dv_ref[band, :] += lax.dot_general(p.astype(MXU_DTYPE), don, nt, preferred_element_type=F32)
            return sink_acc - p_sink * delta

        sink_acc = lax.fori_loop(0, S // BLOCK, step, jnp.zeros((1, cols), F32))
        for g in range(GROUP):
            dsink_ref[g:g + 1, :] = jnp.sum(sink_acc[:, g * BLOCK:(g + 1) * BLOCK], axis=1, keepdims=True)

    hm = pl.BlockSpec((None, GROUP, HEAD_DIM, S), lambda h: (h, 0, 0, 0))
    kv = pl.BlockSpec((None, BLOCK + S, HEAD_DIM), lambda h: (h, 0, 0))
    return _call(
        body,
        name="attn_bwd",
        grid=(N_KV,),
        in_specs=[hm, kv, pl.BlockSpec((None, HEAD_DIM, BLOCK + S), lambda h: (h, 0, 0)), kv,
                  pl.BlockSpec((None, 1, cols), lambda h: (h, 0, 0)), hm],
        out_specs=[hm, kv, kv, pl.BlockSpec((None, GROUP, 1), lambda h: (h, 0, 0))],
        out_shape=[
            jax.ShapeDtypeStruct((N_KV, GROUP, HEAD_DIM, S), MXU_DTYPE),
            jax.ShapeDtypeStruct((N_KV, BLOCK + S, HEAD_DIM), F32),
            jax.ShapeDtypeStruct((N_KV, BLOCK + S, HEAD_DIM), F32),
            jax.ShapeDtypeStruct((N_KV, GROUP, 1), F32),
        ],
        scratch_shapes=[pltpu.VMEM((2, 2 * BLOCK, cols), F32)],
        semantics=("parallel",),
        operands=(qt, kp, kt, vp, sink_row, dot_),
        side=side,
    )


PAD = 8
CHUNK = 256


def _past_taps(xpad_ref, r0, width):
    ext = xpad_ref[pl.ds(r0, CHUNK + PAD), :]
    taps = []
    for k in range(width):
        back = width - 1 - k
        taps.append((ext if back == 0 else pltpu.roll(ext, back, 0))[PAD:, :])
    return taps


def _future_taps(xpad_ref, r0, width):
    ext = xpad_ref[pl.ds(r0, CHUNK + PAD), :]
    taps = []
    for ahead in range(width):
        taps.append((ext if ahead == 0 else pltpu.roll(ext, CHUNK + PAD - ahead, 0))[:CHUNK, :])
    return taps


def _conv_fwd(src, col0, w, b, *, tc, name, side=None):
    width, c_dim = w.shape

    def body(x_ref, w_ref, b_ref, o_ref, xpad_ref):
        xpad_ref[pl.ds(0, PAD), :] = jnp.zeros((PAD, tc), F32)
        xpad_ref[pl.ds(PAD, S), :] = x_ref[...]
        wv = w_ref[...]
        bv = b_ref[...]

        def step(ci, carry):
            r0 = pl.multiple_of(ci * CHUNK, CHUNK)
            taps = _past_taps(xpad_ref, r0, width)
            y = bv + taps[0] * wv[0:1, :]
            for k in range(1, width):
                y = y + taps[k] * wv[k:k + 1, :]
            o_ref[pl.ds(r0, CHUNK), :] = y
            return carry

        lax.fori_loop(0, S // CHUNK, step, 0)

    return _call(
        body,
        name=name,
        grid=(c_dim // tc,),
        in_specs=[
            pl.BlockSpec((S, tc), lambda j: (0, col0 // tc + j)),
            pl.BlockSpec((width, tc), lambda j: (0, j)),
            pl.BlockSpec((1, tc), lambda j: (0, j)),
        ],
        out_specs=pl.BlockSpec((S, tc), lambda j: (0, j)),
        out_shape=jax.ShapeDtypeStruct((S, c_dim), F32),
        scratch_shapes=[pltpu.VMEM((S + PAD, tc), F32)],
        semantics=("parallel",),
        operands=(src, w, b),
        side=side,
    )


def _conv_bwd(dy, src, col0, w, *, tc, name, side=None):
    width, c_dim = w.shape

    def body(dy_ref, x_ref, w_ref, dx_ref, dw_ref, db_ref, xpad_ref, dpad_ref):
        xpad_ref[pl.ds(0, PAD), :] = jnp.zeros((PAD, tc), F32)
        xpad_ref[pl.ds(PAD, S), :] = x_ref[...]
        dpad_ref[pl.ds(0, S), :] = dy_ref[...]
        dpad_ref[pl.ds(S, PAD), :] = jnp.zeros((PAD, tc), F32)
        wv = w_ref[...]

        def step(ci, acc):
            r0 = pl.multiple_of(ci * CHUNK, CHUNK)
            past = _past_taps(xpad_ref, r0, width)
            ahead = _future_taps(dpad_ref, r0, width)
            d = ahead[0]
            dx = d * wv[width - 1:width, :]
            for j in range(1, width):
                dx = dx + ahead[j] * wv[width - 1 - j:width - j, :]
            dx_ref[pl.ds(r0, CHUNK), :] = dx.astype(dx_ref.dtype)
            return tuple(acc[k] + _colsum(past[k] * d) for k in range(width)) + (acc[width] + _colsum(d),)

        zero = jnp.zeros((1, tc), F32)
        acc = lax.fori_loop(0, S // CHUNK, step, (zero,) * (width + 1))
        for k in range(width):
            dw_ref[k:k + 1, :] = acc[k]
        db_ref[...] = acc[width]

    return _call(
        body,
        name=name,
        grid=(c_dim // tc,),
        in_specs=[
            pl.BlockSpec((S, tc), lambda j: (0, j)),
            pl.BlockSpec((S, tc), lambda j: (0, col0 // tc + j)),
            pl.BlockSpec((width, tc), lambda j: (0, j)),
        ],
        out_specs=[
            pl.BlockSpec((S, tc), lambda j: (0, j)),
            pl.BlockSpec((width, tc), lambda j: (0, j)),
            pl.BlockSpec((1, tc), lambda j: (0, j)),
        ],
        out_shape=[
            jax.ShapeDtypeStruct((S, c_dim), MXU_DTYPE),
            jax.ShapeDtypeStruct((width, c_dim), F32),
            jax.ShapeDtypeStruct((1, c_dim), F32),
        ],
        scratch_shapes=[pltpu.VMEM((S + PAD, tc), F32), pltpu.VMEM((S + PAD, tc), F32)],
        semantics=("parallel",),
        operands=(dy, src, w),
        side=side,
    )


SCAN_TC = 256


def _lru_gates(rxc, wa, wi, ba, bi, side=None):
    tm = 512

    def body(x_ref, wa_ref, wi_ref, ba_ref, bi_ref, r_ref, i_ref):
        xv = x_ref[...].astype(MXU_DTYPE)
        r_ref[...] = _sigmoid(jnp.dot(xv, wa_ref[...].astype(MXU_DTYPE), preferred_element_type=F32) + ba_ref[...])
        i_ref[...] = _sigmoid(jnp.dot(xv, wi_ref[...].astype(MXU_DTYPE), preferred_element_type=F32) + bi_ref[...])

    x_spec = pl.BlockSpec((tm, RNN_GROUP), lambda g, i: (i, g))
    w_spec = pl.BlockSpec((None, RNN_GROUP, RNN_GROUP), lambda g, i: (g, 0, 0))
    b_spec = pl.BlockSpec((1, RNN_GROUP), lambda g, i: (0, g))
    return _call(
        body,
        name="lru_gates",
        grid=(N_RNN_GROUPS, S // tm),
        in_specs=[x_spec, w_spec, w_spec, b_spec, b_spec],
        out_specs=[x_spec, x_spec],
        out_shape=[jax.ShapeDtypeStruct((S, D_RNN), F32)] * 2,
        semantics=("parallel", "parallel"),
        operands=(rxc, wa, wi, ba, bi),
        side=side,
    )


def _scan_down(a, u, row):
    for d in (1, 2, 4):
        a_s = jnp.where(row >= d, pltpu.roll(a, d, 0), 1.0)
        u_s = jnp.where(row >= d, pltpu.roll(u, d, 0), 0.0)
        u = a * u_s + u
        a = a * a_s
    return a, u


def _scan_up(a, u, row):
    for d in (1, 2, 4):
        a_s = jnp.where(row < 8 - d, pltpu.roll(a, 8 - d, 0), 1.0)
        u_s = jnp.where(row < 8 - d, pltpu.roll(u, 8 - d, 0), 0.0)
        u = a * u_s + u
        a = a * a_s
    return a, u


def _lru_scan_fwd(r, i, rxc, proj, lam, side=None):
    tc = SCAN_TC

    def body(r_ref, i_ref, x_ref, ry_ref, lam_ref, h_ref, y_ref):
        rate = LRU_C * _softplus(-lam_ref[...])
        row = lax.broadcasted_iota(jnp.int32, (8, tc), 0)

        def step(ci, carry):
            r0 = pl.multiple_of(ci * 16, 16)
            log_a = -rate * r_ref[pl.ds(r0, 16), :]
            a16 = jnp.exp(log_a)
            u16 = jnp.sqrt(_one_minus_exp(2.0 * log_a)) * (i_ref[pl.ds(r0, 16), :] * x_ref[pl.ds(r0, 16), :])
            hs = []
            for half in range(2):
                a_cum, h0 = _scan_down(a16[8 * half:8 * half + 8, :], u16[8 * half:8 * half + 8, :], row)
                h = a_cum * carry + h0
                carry = jnp.broadcast_to(h[7:8, :], (8, tc))
                hs.append(h)
            h16 = jnp.concatenate(hs, axis=0)
            h_ref[pl.ds(r0, 16), :] = h16
            y_ref[pl.ds(r0, 16), :] = (h16 * _gelu(ry_ref[pl.ds(r0, 16), :])[0]).astype(y_ref.dtype)
            return carry

        lax.fori_loop(0, S // 16, step, jnp.zeros((8, tc), F32))

    col = pl.BlockSpec((S, tc), lambda j: (0, j))
    return _call(
        body,
        name="lru_scan_fwd",
        grid=(D_RNN // tc,),
        in_specs=[col, col, col, pl.BlockSpec((S, tc), lambda j: (0, OFF_RY // tc + j)),
                  pl.BlockSpec((1, tc), lambda j: (0, j))],
        out_specs=[col, col],
        out_shape=[jax.ShapeDtypeStruct((S, D_RNN), F32), jax.ShapeDtypeStruct((S, D_RNN), MXU_DTYPE)],
        semantics=("parallel",),
        operands=(r, i, rxc, proj, lam),
        side=side,
    )


def _lru_scan_bwd(dy, proj, h, r, i, rxc, lam, side=None):
    tc = SCAN_TC

    def body(dy_ref, ry_ref, h_ref, r_ref, i_ref, x_ref, lam_ref,
             dry_ref, dzr_ref, dzi_ref, dx_ref, dba_ref, dbi_ref, dlam_ref, a_ref, dh_ref, hp_ref):
        lam_v = lam_ref[...]
        rate = LRU_C * _softplus(-lam_v)
        dlam_scale = LRU_C * _sigmoid(-lam_v)
        row = lax.broadcasted_iota(jnp.int32, (8, tc), 0)
        hp_ref[pl.ds(0, PAD), :] = jnp.zeros((PAD, tc), F32)
        hp_ref[pl.ds(PAD, S), :] = h_ref[...]
        a_ref[pl.ds(S, PAD), :] = jnp.zeros((PAD, tc), F32)

        def prep(ci, carry):
            r0 = pl.multiple_of(ci * CHUNK, CHUNK)
            a_ref[pl.ds(r0, CHUNK), :] = jnp.exp(-rate * r_ref[pl.ds(r0, CHUNK), :])
            ge, dge = _gelu(ry_ref[pl.ds(r0, CHUNK), :])
            dyv = dy_ref[pl.ds(r0, CHUNK), :]
            dh_ref[pl.ds(r0, CHUNK), :] = dyv * ge
            dry_ref[pl.ds(r0, CHUNK), :] = (dyv * h_ref[pl.ds(r0, CHUNK), :] * dge).astype(dry_ref.dtype)
            return carry

        lax.fori_loop(0, S // CHUNK, prep, 0)

        def step(ci, state):
            carry, dba, dbi, dlam = state
            r0 = pl.multiple_of(S - 16 - ci * 16, 16)
            a_ext = a_ref[pl.ds(r0, 24), :]
            a_next = pltpu.roll(a_ext, 23, 0)
            h_prev = pltpu.roll(hp_ref[pl.ds(r0, 24), :], 1, 0)
            dh16 = dh_ref[pl.ds(r0, 16), :]
            gs = [None, None]
            for half in (1, 0):
                lo = 8 * half
                c_cum, g0 = _scan_up(a_next[lo:lo + 8, :], dh16[lo:lo + 8, :], row)
                g = c_cum * carry + g0
                carry = jnp.broadcast_to(g[0:1, :], (8, tc))
                gs[half] = g
            g16 = jnp.concatenate(gs, axis=0)
            a16 = a_ext[0:16, :]
            r16 = r_ref[pl.ds(r0, 16), :]
            i16 = i_ref[pl.ds(r0, 16), :]
            x16 = x_ref[pl.ds(r0, 16), :]
            a2 = a16 * a16
            sq = jnp.sqrt(_one_minus_exp(-2.0 * rate * r16))
            dx_ref[pl.ds(r0, 16), :] = g16 * sq * i16
            dzi = g16 * sq * x16 * i16 * (1.0 - i16)
            dlog_a = g16 * h_prev[8:24, :] * a16 - g16 * i16 * x16 * a2 / sq
            dzr = -rate * dlog_a * r16 * (1.0 - r16)
            dzr_ref[pl.ds(r0, 16), :] = dzr.astype(dzr_ref.dtype)
            dzi_ref[pl.ds(r0, 16), :] = dzi.astype(dzi_ref.dtype)
            return carry, dba + _colsum(dzr), dbi + _colsum(dzi), dlam + _colsum(dlog_a * r16)

        zero = jnp.zeros((1, tc), F32)
        _, dba, dbi, dlam = lax.fori_loop(0, S // 16, step, (jnp.zeros((8, tc), F32), zero, zero, zero))
        dba_ref[...] = dba
        dbi_ref[...] = dbi
        dlam_ref[...] = dlam * dlam_scale

    col = pl.BlockSpec((S, tc), lambda j: (0, j))
    vec = pl.BlockSpec((1, tc), lambda j: (0, j))
    return _call(
        body,
        name="lru_scan_bwd",
        grid=(D_RNN // tc,),
        in_specs=[col, pl.BlockSpec((S, tc), lambda j: (0, OFF_RY // tc + j)), col, col, col, col, vec],
        out_specs=[col, col, col, col, vec, vec, vec],
        out_shape=[jax.ShapeDtypeStruct((S, D_RNN), MXU_DTYPE)] * 3 + [jax.ShapeDtypeStruct((S, D_RNN), F32)]
        + [jax.ShapeDtypeStruct((1, D_RNN), F32)] * 3,
        scratch_shapes=[pltpu.VMEM((S + PAD, tc), F32), pltpu.VMEM((S, tc), F32), pltpu.VMEM((S + PAD, tc), F32)],
        semantics=("parallel",),
        operands=(dy, proj, h, r, i, rxc, lam),
        side=side,
    )


def _lru_gate_wgrad(rxc, dzr, dzi, side=None):
    def body(x_ref, dzr_ref, dzi_ref, dwa_ref, dwi_ref):
        xv = x_ref[...].astype(MXU_DTYPE)
        dims = (((0,), (0,)), ((), ()))
        dwa_ref[...] = lax.dot_general(xv, dzr_ref[...], dims, preferred_element_type=F32)
        dwi_ref[...] = lax.dot_general(xv, dzi_ref[...], dims, preferred_element_type=F32)

    col = pl.BlockSpec((S, RNN_GROUP), lambda g: (0, g))
    w_spec = pl.BlockSpec((None, RNN_GROUP, RNN_GROUP), lambda g: (g, 0, 0))
    return _call(
        body,
        name="lru_gate_wgrad",
        grid=(N_RNN_GROUPS,),
        in_specs=[col, col, col],
        out_specs=[w_spec, w_spec],
        out_shape=[jax.ShapeDtypeStruct((N_RNN_GROUPS, RNN_GROUP, RNN_GROUP), F32)] * 2,
        semantics=("parallel",),
        operands=(rxc, dzr, dzi),
        side=side,
    )


def _lru_gate_xgrad(dzr, dzi, wa, wi, dx_in, side=None):
    tm = 512

    def body(dzr_ref, dzi_ref, wa_ref, wi_ref, dx_ref, o_ref):
        dims = (((1,), (1,)), ((), ()))
        o_ref[...] = (dx_ref[...]
                      + lax.dot_general(dzr_ref[...], wa_ref[...].astype(MXU_DTYPE), dims, preferred_element_type=F32)
                      + lax.dot_general(dzi_ref[...], wi_ref[...].astype(MXU_DTYPE), dims, preferred_element_type=F32))

    x_spec = pl.BlockSpec((tm, RNN_GROUP), lambda g, i: (i, g))
    w_spec = pl.BlockSpec((None, RNN_GROUP, RNN_GROUP), lambda g, i: (g, 0, 0))
    return _call(
        body,
        name="lru_gate_xgrad",
        grid=(N_RNN_GROUPS, S // tm),
        in_specs=[x_spec, x_spec, w_spec, w_spec, x_spec],
        out_specs=x_spec,
        out_shape=jax.ShapeDtypeStruct((S, D_RNN), F32),
        semantics=("parallel", "parallel"),
        operands=(dzr, dzi, wa, wi, dx_in),
        side=side,
    )


def _gate_fwd(y_attn, y_rnn, proj, b_gate, side=None):
    t = 512

    def body(ya_ref, yr_ref, ga_ref, gr_ref, ba_ref, br_ref, o_ref):
        o_ref[...] = (_sigmoid(ga_ref[...] + ba_ref[...]) * ya_ref[...]
                      + _sigmoid(gr_ref[...] + br_ref[...]) * yr_ref[...]).astype(o_ref.dtype)

    tile = pl.BlockSpec((t, t), lambda i, j: (i, j))
    return _call(
        body,
        name="gate_fwd",
        grid=(S // t, D // t),
        in_specs=[tile, tile,
                  pl.BlockSpec((t, t), lambda i, j: (i, OFF_GA // t + j)),
                  pl.BlockSpec((t, t), lambda i, j: (i, OFF_GR // t + j)),
                  pl.BlockSpec((1, t), lambda i, j: (0, j)),
                  pl.BlockSpec((1, t), lambda i, j: (0, D // t + j))],
        out_specs=tile,
        out_shape=jax.ShapeDtypeStruct((S, D), MXU_DTYPE),
        semantics=("parallel", "parallel"),
        operands=(y_attn, y_rnn, proj, proj, b_gate, b_gate),
        side=side,
    )


def _gate_bwd(dmix, y_attn, y_rnn, proj, b_gate, side=None):
    t = 512

    def body(dm_ref, ya_ref, yr_ref, ga_ref, gr_ref, ba_ref, br_ref,
             dya_ref, dyr_ref, dga_ref, dgr_ref, dba_ref, dbr_ref):
        @pl.when(pl.program_id(1) == 0)
        def _():
            dba_ref[...] = jnp.zeros_like(dba_ref)
            dbr_ref[...] = jnp.zeros_like(dbr_ref)

        dm = dm_ref[...]
        ga = _sigmoid(ga_ref[...] + ba_ref[...])
        gr = _sigmoid(gr_ref[...] + br_ref[...])
        dya_ref[...] = (dm * ga).astype(dya_ref.dtype)
        dyr_ref[...] = (dm * gr).astype(dyr_ref.dtype)
        dga = dm * ya_ref[...] * ga * (1.0 - ga)
        dgr = dm * yr_ref[...] * gr * (1.0 - gr)
        dga_ref[...] = dga.astype(dga_ref.dtype)
        dgr_ref[...] = dgr.astype(dgr_ref.dtype)
        dba_ref[...] += _colsum(dga)
        dbr_ref[...] += _colsum(dgr)

    tile = pl.BlockSpec((t, t), lambda j, i: (i, j))
    vec = pl.BlockSpec((1, t), lambda j, i: (0, j))
    return _call(
        body,
        name="gate_bwd",
        grid=(D // t, S // t),
        in_specs=[tile, tile, tile,
                  pl.BlockSpec((t, t), lambda j, i: (i, OFF_GA // t + j)),
                  pl.BlockSpec((t, t), lambda j, i: (i, OFF_GR // t + j)),
                  vec,
                  pl.BlockSpec((1, t), lambda j, i: (0, D // t + j))],
        out_specs=[tile, tile, tile, tile, vec, vec],
        out_shape=[jax.ShapeDtypeStruct((S, D), MXU_DTYPE)] * 4 + [jax.ShapeDtypeStruct((1, D), F32)] * 2,
        semantics=("parallel", "arbitrary"),
        operands=(dmix, y_attn, y_rnn, proj, proj, b_gate, b_gate),
        side=side,
    )


LN_TM = 256


def _ln_stats(pre):
    mu = jnp.mean(pre, axis=-1, keepdims=True)
    xc = pre - mu
    rstd = lax.rsqrt(jnp.mean(xc * xc, axis=-1, keepdims=True) + LN_EPS)
    return xc * rstd, rstd


def _ln_input_grad(dy, xhat, rstd, g):
    dyg = dy * g
    return rstd * (dyg - jnp.mean(dyg, axis=-1, keepdims=True)
                   - xhat * jnp.mean(dyg * xhat, axis=-1, keepdims=True))


def _ln_fwd(res, branch, g, b, side=None):
    def body(res_ref, br_ref, g_ref, b_ref, y_ref, yb_ref, xhat_ref, rstd_ref):
        xhat, rstd = _ln_stats(ALPHA * res_ref[...] + br_ref[...])
        y = xhat * g_ref[...] + b_ref[...]
        y_ref[...] = y
        yb_ref[...] = y.astype(yb_ref.dtype)
        xhat_ref[...] = xhat
        rstd_ref[...] = rstd

    tile = pl.BlockSpec((LN_TM, D), lambda i: (i, 0))
    vec = pl.BlockSpec((1, D), lambda i: (0, 0))
    return _call(
        body,
        name="ln_fwd",
        grid=(S // LN_TM,),
        in_specs=[tile, tile, vec, vec],
        out_specs=[tile, tile, tile, pl.BlockSpec((LN_TM, 1), lambda i: (i, 0))],
        out_shape=[jax.ShapeDtypeStruct((S, D), F32), jax.ShapeDtypeStruct((S, D), MXU_DTYPE),
                   jax.ShapeDtypeStruct((S, D), F32), jax.ShapeDtypeStruct((S, 1), F32)],
        semantics=("parallel",),
        operands=(res, branch, g, b),
        side=side,
    )


def _ln_bwd(dy_a, dy_b, xhat, rstd, g, side=None):
    def body(da_ref, db_in_ref, xhat_ref, rstd_ref, g_ref, dp_ref, dpb_ref, dg_ref, db_ref):
        @pl.when(pl.program_id(0) == 0)
        def _():
            dg_ref[...] = jnp.zeros_like(dg_ref)
            db_ref[...] = jnp.zeros_like(db_ref)

        dy = da_ref[...] + ALPHA * db_in_ref[...]
        xhat = xhat_ref[...]
        dp = _ln_input_grad(dy, xhat, rstd_ref[...], g_ref[...])
        dp_ref[...] = dp
        dpb_ref[...] = dp.astype(dpb_ref.dtype)
        dg_ref[...] += _colsum(dy * xhat)
        db_ref[...] += _colsum(dy)

    tile = pl.BlockSpec((LN_TM, D), lambda i: (i, 0))
    vec = pl.BlockSpec((1, D), lambda i: (0, 0))
    return _call(
        body,
        name="ln_bwd",
        grid=(S // LN_TM,),
        in_specs=[tile, tile, tile, pl.BlockSpec((LN_TM, 1), lambda i: (i, 0)), vec],
        out_specs=[tile, tile, vec, vec],
        out_shape=[jax.ShapeDtypeStruct((S, D), F32), jax.ShapeDtypeStruct((S, D), MXU_DTYPE),
                   jax.ShapeDtypeStruct((1, D), F32), jax.ShapeDtypeStruct((1, D), F32)],
        semantics=("arbitrary",),
        operands=(dy_a, dy_b, xhat, rstd, g),
        side=side,
    )


def _ln_loss_bwd(res, branch, g, b, target, side=None):
    def body(res_ref, br_ref, g_ref, b_ref, t_ref, loss_ref, dp_ref, dpb_ref, dg_ref, db_ref):
        @pl.when(pl.program_id(0) == 0)
        def _():
            loss_ref[...] = jnp.zeros_like(loss_ref)
            dg_ref[...] = jnp.zeros_like(dg_ref)
            db_ref[...] = jnp.zeros_like(db_ref)

        xhat, rstd = _ln_stats(ALPHA * res_ref[...] + br_ref[...])
        gv = g_ref[...]
        err = xhat * gv + b_ref[...] - t_ref[...]
        loss_ref[...] += (0.5 / D) * jnp.sum(_colsum(err * err), axis=1, keepdims=True)
        dy = err * (1.0 / D)
        dp = _ln_input_grad(dy, xhat, rstd, gv)
        dp_ref[...] = dp
        dpb_ref[...] = dp.astype(dpb_ref.dtype)
        dg_ref[...] += _colsum(dy * xhat)
        db_ref[...] += _colsum(dy)

    tile = pl.BlockSpec((LN_TM, D), lambda i: (i, 0))
    vec = pl.BlockSpec((1, D), lambda i: (0, 0))
    return _call(
        body,
        name="ln_loss_bwd",
        grid=(S // LN_TM,),
        in_specs=[tile, tile, vec, vec, tile],
        out_specs=[pl.BlockSpec((1, 1), lambda i: (0, 0)), tile, tile, vec, vec],
        out_shape=[jax.ShapeDtypeStruct((1, 1), F32), jax.ShapeDtypeStruct((S, D), F32),
                   jax.ShapeDtypeStruct((S, D), MXU_DTYPE),
                   jax.ShapeDtypeStruct((1, D), F32), jax.ShapeDtypeStruct((1, D), F32)],
        semantics=("arbitrary",),
        operands=(res, branch, g, b, target),
        side=side,
    )


FFN_TC = 256


def _ffn_act_fwd(up, gpre, w, b, side=None):
    tc = FFN_TC

    def body(up_ref, x_ref, w_ref, b_ref, o_ref, xpad_ref):
        xpad_ref[pl.ds(0, PAD), :] = jnp.zeros((PAD, tc), F32)
        xpad_ref[pl.ds(PAD, S), :] = x_ref[...]
        wv = w_ref[...]
        bv = b_ref[...]

        def step(ci, carry):
            r0 = pl.multiple_of(ci * CHUNK, CHUNK)
            taps = _past_taps(xpad_ref, r0, FFN_CONV_W)
            gate = bv + taps[0] * wv[0:1, :] + taps[1] * wv[1:2, :] + taps[2] * wv[2:3, :]
            o_ref[pl.ds(r0, CHUNK), :] = (_gelu(gate)[0] * up_ref[pl.ds(r0, CHUNK), :]).astype(o_ref.dtype)
            return carry

        lax.fori_loop(0, S // CHUNK, step, 0)

    col = pl.BlockSpec((S, tc), lambda j: (0, j))
    return _call(
        body,
        name="ffn_act_fwd",
        grid=(D_FF // tc,),
        in_specs=[col, col, pl.BlockSpec((FFN_CONV_W, tc), lambda j: (0, j)), pl.BlockSpec((1, tc), lambda j: (0, j))],
        out_specs=col,
        out_shape=jax.ShapeDtypeStruct((S, D_FF), MXU_DTYPE),
        scratch_shapes=[pltpu.VMEM((S + PAD, tc), F32)],
        semantics=("parallel",),
        operands=(up, gpre, w, b),
        side=side,
    )


def _ffn_act_bwd(dfin, up, gpre, w, b, side=None):
    tc = FFN_TC
    width = FFN_CONV_W

    def body(df_ref, up_ref, x_ref, w_ref, b_ref, dup_ref, dx_ref, dw_ref, db_ref, xpad_ref, dpad_ref):
        xpad_ref[pl.ds(0, PAD), :] = jnp.zeros((PAD, tc), F32)
        xpad_ref[pl.ds(PAD, S), :] = x_ref[...]
        dpad_ref[pl.ds(S, PAD), :] = jnp.zeros((PAD, tc), F32)
        wv = w_ref[...]
        bv = b_ref[...]

        def gate_grad(ci, acc):
            r0 = pl.multiple_of(ci * CHUNK, CHUNK)
            taps = _past_taps(xpad_ref, r0, width)
            gate = bv + taps[0] * wv[0:1, :] + taps[1] * wv[1:2, :] + taps[2] * wv[2:3, :]
            ge, dge = _gelu(gate)
            df = df_ref[pl.ds(r0, CHUNK), :]
            dup_ref[pl.ds(r0, CHUNK), :] = (df * ge).astype(dup_ref.dtype)
            d = df * up_ref[pl.ds(r0, CHUNK), :] * dge
            dpad_ref[pl.ds(r0, CHUNK), :] = d
            return tuple(acc[k] + _colsum(taps[k] * d) for k in range(width)) + (acc[width] + _colsum(d),)

        zero = jnp.zeros((1, tc), F32)
        acc = lax.fori_loop(0, S // CHUNK, gate_grad, (zero,) * (width + 1))
        for k in range(width):
            dw_ref[k:k + 1, :] = acc[k]
        db_ref[...] = acc[width]

        def input_grad(ci, carry):
            r0 = pl.multiple_of(ci * CHUNK, CHUNK)
            ahead = _future_taps(dpad_ref, r0, width)
            dx = ahead[0] * wv[2:3, :] + ahead[1] * wv[1:2, :] + ahead[2] * wv[0:1, :]
            dx_ref[pl.ds(r0, CHUNK), :] = dx.astype(dx_ref.dtype)
            return carry

        lax.fori_loop(0, S // CHUNK, input_grad, 0)

    col = pl.BlockSpec((S, tc), lambda j: (0, j))
    w_spec = pl.BlockSpec((width, tc), lambda j: (0, j))
    vec = pl.BlockSpec((1, tc), lambda j: (0, j))
    return _call(
        body,
        name="ffn_act_bwd",
        grid=(D_FF // tc,),
        in_specs=[col, col, col, w_spec, vec],
        out_specs=[col, col, w_spec, vec],
        out_shape=[jax.ShapeDtypeStruct((S, D_FF), MXU_DTYPE)] * 2
        + [jax.ShapeDtypeStruct((width, D_FF), F32), jax.ShapeDtypeStruct((1, D_FF), F32)],
        scratch_shapes=[pltpu.VMEM((S + PAD, tc), F32), pltpu.VMEM((S + PAD, tc), F32)],
        semantics=("parallel",),
        operands=(dfin, up, gpre, w, b),
        side=side,
    )


def _adamw_update(w, g, m, v):
    m = ADAM_B1 * m + (1.0 - ADAM_B1) * g
    v = ADAM_B2 * v + (1.0 - ADAM_B2) * (g * g)
    m_hat = m / (1.0 - ADAM_B1 ** ADAM_STEP)
    v_hat = v / (1.0 - ADAM_B2 ** ADAM_STEP)
    delta = -ADAM_LR * (m_hat / (jnp.sqrt(v_hat) + ADAM_EPS) + ADAM_WD * w)
    return delta, m, v


def _reduce_adamw(w, m, v, g_own, recv, *, tr, name, side=None):
    r_dim, c_dim = w.shape

    def body(w_ref, m_ref, v_ref, g_ref, recv_ref, grad_ref, delta_ref, nm_ref, nv_ref):
        g = g_ref[...]
        for k in range(1, N_DEV):
            g = g + recv_ref[k].astype(F32)
        delta, nm, nv = _adamw_update(w_ref[...], g, m_ref[...], v_ref[...])
        grad_ref[...] = g
        delta_ref[...] = delta
        nm_ref[...] = nm
        nv_ref[...] = nv

    tile = pl.BlockSpec((tr, c_dim), lambda i: (i, 0))
    return _call(
        body,
        name=name,
        grid=(r_dim // tr,),
        in_specs=[tile, tile, tile, tile, pl.BlockSpec((N_DEV, tr, c_dim), lambda i: (0, i, 0))],
        out_specs=[tile] * 4,
        out_shape=[jax.ShapeDtypeStruct((r_dim, c_dim), F32)] * 4,
        semantics=("parallel",),
        operands=(w, m, v, g_own, recv),
        side=side,
    )


def _adamw_many(ws, ms, vs, gs):
    n = len(ws)

    def body(*refs):
        for i in range(n):
            delta, nm, nv = _adamw_update(refs[i][...], refs[3 * n + i][...], refs[n + i][...], refs[2 * n + i][...])
            refs[4 * n + i][...] = delta
            refs[5 * n + i][...] = nm
            refs[6 * n + i][...] = nv

    vmem = pl.BlockSpec(memory_space=pltpu.VMEM)
    res = pl.pallas_call(
        body,
        name="adamw_small",
        in_specs=[vmem] * (4 * n),
        out_specs=[vmem] * (3 * n),
        out_shape=[jax.ShapeDtypeStruct(w.shape, F32) for w in ws] * 3,
        compiler_params=pltpu.CompilerParams(vmem_limit_bytes=VMEM_LIMIT),
    )(*ws, *ms, *vs, *gs)
    return res[:n], res[n:2 * n], res[2 * n:]


def _adamw_blocks(w, m, v, g, *, name, side=None):
    per = 2

    def body(w_ref, m_ref, v_ref, g_ref, delta_ref, nm_ref, nv_ref):
        delta, nm, nv = _adamw_update(w_ref[...], g_ref[...], m_ref[...], v_ref[...])
        delta_ref[...] = delta
        nm_ref[...] = nm
        nv_ref[...] = nv

    tile = pl.BlockSpec((1, per) + w.shape[2:], lambda i: (0, i, 0, 0))
    return _call(
        body,
        name=name,
        grid=(w.shape[1] // per,),
        in_specs=[tile] * 4,
        out_specs=[tile] * 3,
        out_shape=[jax.ShapeDtypeStruct(w.shape, F32)] * 3,
        semantics=("parallel",),
        operands=(w, m, v, g),
        side=side,
    )


def _coords():
    return lax.axis_index("x"), lax.axis_index("y"), lax.axis_index("c")


def _flip(coord, bit):
    return 1 - coord if bit else coord


def _relative(k):
    x, y, c = _coords()
    return _flip(x, k & 4), _flip(y, k & 2), _flip(c, k & 1)


def _index(pos):
    return 4 * pos[0] + 2 * pos[1] + pos[2]


FAR = (4, 2, 6)
AG_US_PER_MB = 44.0
RS_US_PER_MB = 87.0
ROW_ALIGN = 16


def _chunks(items, cursor, us, us_per_mb, through=None):
    budget = float("inf") if us is None else us / us_per_mb * 2 ** 20
    names = list(items)
    if through is not None:
        names = names[:names.index(through) + 1]
    chunks = []
    for name in names:
        arr = items[name]
        r_dim, c_dim = arr.shape[-2:]
        row_bytes = c_dim * arr.dtype.itemsize
        while cursor[name] < r_dim and budget > 0:
            rows = r_dim - cursor[name]
            if r_dim > ROW_ALIGN and budget < rows * row_bytes:
                rows = min(rows, max(ROW_ALIGN, int(budget // row_bytes) // ROW_ALIGN * ROW_ALIGN))
            chunks.append((name, cursor[name], rows))
            cursor[name] += rows
            budget -= rows * row_bytes
    return chunks


class _Gather:
    def __init__(self, shards):
        self.shards = dict(shards)
        self.bufs = {n: None for n in self.shards}
        self.cursor = {n: 0 for n in self.shards}

    def take(self, us=None, through=None):
        chunks = _chunks(self.shards, self.cursor, us, AG_US_PER_MB, through)
        return _GatherSide(self, chunks) if chunks else None

    def get(self, name):
        chunks = _chunks(self.shards, self.cursor, None, AG_US_PER_MB, through=name)
        if chunks:
            _run_side(_GatherSide(self, chunks), "gather_" + name)
        return self.bufs[name]


class _GatherSide:
    def __init__(self, owner, chunks):
        self.owner, self.chunks = owner, chunks
        self.names = list(dict.fromkeys(n for n, _, _ in chunks))
        old = [n for n in self.names if owner.bufs[n] is not None]
        self.operands = [owner.shards[n] for n in self.names] + [owner.bufs[n] for n in old]
        self.out_shape = [jax.ShapeDtypeStruct((N_DEV,) + owner.shards[n].shape, owner.shards[n].dtype)
                          for n in self.names]
        self.aliases = {len(self.names) + i: self.names.index(n) for i, n in enumerate(old)}
        self.sems = [pltpu.SemaphoreType.DMA((7 * len(chunks),)), pltpu.SemaphoreType.DMA((7 * len(chunks),)),
                     pltpu.SemaphoreType.DMA((len(chunks),))]

    def _copy(self, ins, outs, sems, ci, k, block, to, from_shard=False):
        name, r0, rows = self.chunks[ci]
        w = self.names.index(name)
        slot = outs[w].at[_index(block), pl.ds(r0, rows)]
        return pltpu.make_async_remote_copy(
            src_ref=ins[w].at[pl.ds(r0, rows)] if from_shard else slot, dst_ref=slot,
            send_sem=sems[0].at[7 * ci + k], recv_sem=sems[1].at[7 * ci + k], device_id=to, device_id_type=MESH)

    def _own(self, ins, outs, sems, ci):
        name, r0, rows = self.chunks[ci]
        w = self.names.index(name)
        return pltpu.make_async_copy(ins[w].at[pl.ds(r0, rows)], outs[w].at[_index(_relative(0)), pl.ds(r0, rows)],
                                     sems[2].at[ci])

    def start(self, ins, outs, sems):
        me, sibling = _relative(0), _relative(1)
        for ci in range(len(self.chunks)):
            self._own(ins, outs, sems, ci).start()
        for j, k in enumerate(FAR):
            for ci in range(len(self.chunks)):
                self._copy(ins, outs, sems, ci, 1 + j, me, _relative(k), from_shard=True).start()
        for ci in range(len(self.chunks)):
            self._copy(ins, outs, sems, ci, 0, me, sibling, from_shard=True).start()

    def finish(self, ins, outs, sems):
        me, sibling = _relative(0), _relative(1)
        n = len(self.chunks)
        for j, k in enumerate(FAR):
            for ci in range(n):
                self._copy(ins, outs, sems, ci, 1 + j, _relative(k), me).wait_recv()
                self._copy(ins, outs, sems, ci, 4 + j, _relative(k), sibling).start()
        for ci in range(n):
            self._copy(ins, outs, sems, ci, 0, sibling, me).wait_recv()
        for j, k in enumerate(FAR):
            for ci in range(n):
                self._copy(ins, outs, sems, ci, 4 + j, _relative(k | 1), me).wait_recv()
        for ci in range(n):
            self._copy(ins, outs, sems, ci, 0, me, sibling, from_shard=True).wait_send()
            for j, k in enumerate(FAR):
                self._copy(ins, outs, sems, ci, 1 + j, me, _relative(k), from_shard=True).wait_send()
                self._copy(ins, outs, sems, ci, 4 + j, _relative(k), sibling).wait_send()
            self._own(ins, outs, sems, ci).wait()

    def done(self, results):
        for n, buf in zip(self.names, results):
            self.owner.bufs[n] = buf


class _Scatter:
    def __init__(self, me):
        self.me = me
        self.sends, self.owns, self.bufs, self.cursor = {}, {}, {}, {}

    def add(self, name, send, own):
        self.sends[name] = send
        self.owns[name] = own
        self.bufs[name] = None
        self.cursor[name] = 0

    def add_blocks(self, name, blocks32, blocks16):
        self.add(name, blocks16, lax.dynamic_index_in_dim(blocks32, self.me, axis=0, keepdims=False))

    def add_cols(self, name, full32, full16):
        width = full32.shape[1] // N_DEV
        self.add(name, _blocks(full16, "cols"), lax.dynamic_slice_in_dim(full32, self.me * width, width, axis=1))

    def take(self, us=None):
        chunks = _chunks(self.sends, self.cursor, us, RS_US_PER_MB)
        return _ScatterSide(self, chunks) if chunks else None

    def get(self, name):
        chunks = _chunks(self.sends, self.cursor, None, RS_US_PER_MB, through=name)
        if chunks:
            _run_side(_ScatterSide(self, chunks), "scatter_" + name)
        return self.owns[name], self.bufs[name]


class _ScatterSide:
    ORDER = (4, 2, 6, 1, 5, 3, 7)

    def __init__(self, owner, chunks):
        self.owner, self.chunks = owner, chunks
        self.names = list(dict.fromkeys(n for n, _, _ in chunks))
        old = [n for n in self.names if owner.bufs[n] is not None]
        self.operands = [owner.sends[n] for n in self.names] + [owner.bufs[n] for n in old]
        self.out_shape = [jax.ShapeDtypeStruct(owner.sends[n].shape, BF16) for n in self.names]
        self.aliases = {len(self.names) + i: self.names.index(n) for i, n in enumerate(old)}
        self.sems = [pltpu.SemaphoreType.DMA((7 * len(chunks),)), pltpu.SemaphoreType.DMA((7 * len(chunks),))]

    def _copy(self, ins, outs, sems, ci, k):
        name, r0, rows = self.chunks[ci]
        w = self.names.index(name)
        return pltpu.make_async_remote_copy(
            src_ref=ins[w].at[_index(_relative(k)), pl.ds(r0, rows)], dst_ref=outs[w].at[k, pl.ds(r0, rows)],
            send_sem=sems[0].at[7 * ci + k - 1], recv_sem=sems[1].at[7 * ci + k - 1],
            device_id=_relative(k), device_id_type=MESH)

    def start(self, ins, outs, sems):
        for ci in range(len(self.chunks)):
            for k in self.ORDER:
                self._copy(ins, outs, sems, ci, k).start()

    def finish(self, ins, outs, sems):
        for ci in range(len(self.chunks)):
            for k in self.ORDER:
                self._copy(ins, outs, sems, ci, k).wait()

    def done(self, results):
        for n, buf in zip(self.names, results):
            self.owner.bufs[n] = buf


def _all_reduce_small(vecs, mats):
    nv, nm = len(vecs), len(mats)
    row0, rows = [], 0
    for v in vecs:
        row0.append(rows)
        rows += v.shape[0]
    rows = -(-rows // 8) * 8
    width = max(v.shape[1] for v in vecs)
    per = mats[0].shape[0] // N_DEV

    def body(*refs):
        v_in, m_in = refs[:nv], refs[nv:nv + nm]
        v_out, m_out = refs[nv + nm:2 * nv + nm], refs[2 * nv + nm:2 * (nv + nm)]
        mine, inbox, total, stage, v_send, v_recv, a_send, a_recv, b_send, b_recv = refs[2 * (nv + nm):]
        me = _index(_relative(0))

        def vec_copy(k, sender):
            return pltpu.make_async_remote_copy(
                src_ref=mine, dst_ref=inbox.at[_index(sender)],
                send_sem=v_send.at[k - 1], recv_sem=v_recv.at[k - 1], device_id=_relative(k), device_id_type=MESH)

        def part(ref, pos):
            return ref.at[pl.ds(per * _index(pos), per)]

        def to_owner(k, j):
            return pltpu.make_async_remote_copy(
                src_ref=part(m_in[j], _relative(k)), dst_ref=stage.at[k, j],
                send_sem=a_send.at[nm * (k - 1) + j], recv_sem=a_recv.at[nm * (k - 1) + j],
                device_id=_relative(k), device_id_type=MESH)

        def from_owner(k, j, owner):
            return pltpu.make_async_remote_copy(
                src_ref=part(m_out[j], owner), dst_ref=part(m_out[j], owner),
                send_sem=b_send.at[nm * (k - 1) + j], recv_sem=b_recv.at[nm * (k - 1) + j],
                device_id=_relative(k), device_id_type=MESH)

        mine[...] = jnp.zeros_like(mine)
        for v, r0 in zip(v_in, row0):
            mine[r0:r0 + v.shape[0], 0:v.shape[1]] = v[...]
        for k in range(1, N_DEV):
            vec_copy(k, _relative(0)).start()
            for j in range(nm):
                to_owner(k, j).start()
        inbox[me] = mine[...]

        for j in range(nm):
            acc = m_in[j][pl.ds(per * me, per)]
            for k in range(1, N_DEV):
                to_owner(k, j).wait_recv()
                acc = acc + stage[k, j]
            m_out[j][pl.ds(per * me, per)] = acc
            for k in range(1, N_DEV):
                from_owner(k, j, _relative(0)).start()

        for k in range(1, N_DEV):
            vec_copy(k, _relative(k)).wait_recv()
        acc = inbox[0]
        for d in range(1, N_DEV):
            acc = acc + inbox[d]
        total[...] = acc
        for v, r0 in zip(v_out, row0):
            v[...] = total[r0:r0 + v.shape[0], 0:v.shape[1]]

        for k in range(1, N_DEV):
            for j in range(nm):
                from_owner(k, j, _relative(k)).wait_recv()
        for k in range(1, N_DEV):
            vec_copy(k, _relative(0)).wait_send()
            for j in range(nm):
                to_owner(k, j).wait_send()
                from_owner(k, j, _relative(0)).wait_send()

    vmem = pl.BlockSpec(memory_space=pltpu.VMEM)
    res = pl.pallas_call(
        body,
        name="all_reduce_small",
        in_specs=[vmem] * (nv + nm),
        out_specs=[vmem] * (nv + nm),
        out_shape=[jax.ShapeDtypeStruct(a.shape, F32) for a in (*vecs, *mats)],
        scratch_shapes=[pltpu.VMEM((rows, width), F32), pltpu.VMEM((N_DEV, rows, width), F32),
                        pltpu.VMEM((rows, width), F32), pltpu.VMEM((N_DEV, nm, per) + mats[0].shape[1:], F32),
                        pltpu.SemaphoreType.DMA((N_DEV - 1,)), pltpu.SemaphoreType.DMA((N_DEV - 1,))]
        + [pltpu.SemaphoreType.DMA((nm * (N_DEV - 1),))] * 4,
        compiler_params=pltpu.CompilerParams(vmem_limit_bytes=VMEM_LIMIT),
    )(*vecs, *mats)
    return res[:nv], res[nv:]


def _block_diag(w):
    w4 = w.reshape(N_RNN_GROUPS, 4, RNN_BLOCK_W, RNN_BLOCK_W)
    eye = jnp.eye(4, dtype=w.dtype)
    return (w4[:, :, :, None, :] * eye[None, :, None, :, None]).reshape(N_RNN_GROUPS, RNN_GROUP, RNN_GROUP)


def _diag_blocks(wg):
    w5 = wg.reshape(N_RNN_GROUPS, 4, RNN_BLOCK_W, 4, RNN_BLOCK_W)
    return jnp.stack([w5[:, b, :, b, :] for b in range(4)], axis=1).reshape(16, RNN_BLOCK_W, RNN_BLOCK_W)


def _heads_major(t, n_heads):
    return t.reshape(S, n_heads, HEAD_DIM).transpose(1, 0, 2)


def _heads_minor(t):
    return t.transpose(1, 0, 2).reshape(S, t.shape[0] * HEAD_DIM)


def _natural(gathered, how):
    n, r, c = gathered.shape
    if how == "rows":
        return gathered.reshape(n * r, c)
    return gathered.transpose(1, 0, 2).reshape(r, n * c)


def _blocks(full, how):
    if how == "rows":
        return full.reshape(N_DEV, full.shape[0] // N_DEV, full.shape[1])
    return full.reshape(full.shape[0], N_DEV, full.shape[1] // N_DEV).transpose(1, 0, 2)


def _forward_backward(x2, target, small, gather, scatter):
    xb = x2.astype(MXU_DTYPE)
    w_in = _natural(gather.get("w_in"), "cols")
    proj, projb = _mm(xb, w_in, tm=1024, tn=512, tk=D, out_dtype=(F32, MXU_DTYPE), name="proj", side=gather.take(110))

    qt = projb[:, :OFF_K].T.reshape(N_KV, GROUP, HEAD_DIM, S)
    k2, v2 = projb[:, OFF_K:OFF_V], projb[:, OFF_V:OFF_RX]
    kp = jnp.pad(_heads_major(k2, N_KV), ((0, 0), (BLOCK, 0), (0, 0)))
    vp = jnp.pad(_heads_major(v2, N_KV), ((0, 0), (BLOCK, 0), (0, 0)))
    kt = jnp.pad(k2.T.reshape(N_KV, HEAD_DIM, S), ((0, 0), (0, 0), (BLOCK, 0)))
    vt = jnp.pad(v2.T.reshape(N_KV, HEAD_DIM, S), ((0, 0), (0, 0), (BLOCK, 0)))
    sink_row = jnp.repeat(small["attn_sinks"].reshape(N_KV, 1, GROUP), BLOCK, axis=2)
    ot = _attn_fwd(qt, kp, vt, sink_row, side=gather.take(120)).reshape(D, S)

    rconv_w = _natural(gather.get("rnn_conv_w"), "cols")
    rxc = _conv_fwd(proj, OFF_RX, rconv_w, small["rnn_conv_b"], tc=512, name="rnn_conv_fwd", side=gather.take(18))
    r, i = _lru_gates(rxc, small["lru_wa"], small["lru_wi"], small["lru_ba"], small["lru_bi"], side=gather.take(33))
    h, yrin = _lru_scan_fwd(r, i, rxc, proj, small["lru_lambda"], side=gather.take(53))

    w_ap = _natural(gather.get("w_attn_proj"), "rows")
    w_rp = _natural(gather.get("w_rnn_proj"), "rows")
    y_attn = _mm(ot, w_ap, ta=True, tm=1024, tn=1024, tk=D, name="attn_proj", side=gather.take(22))
    y_rnn = _mm(yrin, w_rp, tm=1024, tn=1024, tk=D_RNN, name="rnn_proj", side=gather.take(27))
    mixin = _gate_fwd(y_attn, y_rnn, proj, small["b_gate"], side=gather.take(25))
    w_out = _natural(gather.get("w_out"), "rows")
    mix = _mm(mixin, w_out, tm=1024, tn=1024, tk=D, name="mix_out", side=gather.take(22))
    x1, x1b, xhat1, rstd1 = _ln_fwd(x2, mix, small["ln1_g"], small["ln1_b"], side=gather.take(23))

    w_up = gather.get("ffn_w_up")
    up = _mm(x1b, w_up, tm=1024, tn=768, tk=D, b_block=768, name="ffn_up", side=gather.take(58))
    w_gate = gather.get("ffn_w_gate")
    gpre = _mm(x1b, w_gate, tm=1024, tn=768, tk=D, b_block=768, name="ffn_gate", side=gather.take(58))
    fconv_w = _natural(gather.get("ffn_conv_w"), "cols")
    fin = _ffn_act_fwd(up, gpre, fconv_w, small["ffn_conv_b"], side=gather.take())
    w_down = _natural(gather.get("ffn_w_down"), "rows")
    f = _mm(fin, w_down, tm=1024, tn=1024, tk=2048, name="ffn_down")
    loss, dpre2, dpre2b, d_ln2_g, d_ln2_b = _ln_loss_bwd(x1, f, small["ln2_g"], small["ln2_b"], target)

    grads = {"ln2_g": d_ln2_g, "ln2_b": d_ln2_b}
    both = (F32, BF16)
    g32, g16 = _mm(fin, dpre2b, ta=True, tm=1024, tn=1024, tk=S, out_dtype=both, name="d_ffn_w_down")
    scatter.add_blocks("ffn_w_down", _blocks(g32, "rows"), _blocks(g16, "rows"))
    dfin = _mm(dpre2b, w_down, tb=True, tm=1024, tn=1024, tk=D, name="d_fin", side=scatter.take(57))
    dup, dgpre, grads["ffn_conv_w"], grads["ffn_conv_b"] = _ffn_act_bwd(
        dfin, up, gpre, fconv_w, small["ffn_conv_b"], side=scatter.take(85))
    g32, g16 = _mm(x1b, dup, ta=True, tm=1024, tn=768, tk=S, out_dtype=both, out_block=768, name="d_ffn_w_up",
                   side=scatter.take(57))
    scatter.add_blocks("ffn_w_up", g32, g16)
    g32, g16 = _mm(x1b, dgpre, ta=True, tm=1024, tn=768, tk=S, out_dtype=both, out_block=768, name="d_ffn_w_gate",
                   side=scatter.take(56))
    scatter.add_blocks("ffn_w_gate", g32, g16)
    dx1 = _mm(dup, w_up, tb=True, tm=1024, tn=1024, tk=768, b_block=768, name="d_x1_up", side=scatter.take(68))
    dx1 = _mm(dgpre, w_gate, tb=True, tm=1024, tn=1024, tk=768, b_block=768, add=dx1, name="d_x1_gate",
              side=scatter.take(70))
    dpre1, dpre1b, grads["ln1_g"], grads["ln1_b"] = _ln_bwd(dx1, dpre2, xhat1, rstd1, small["ln1_g"],
                                                            side=scatter.take(24))

    g32, g16 = _mm(mixin, dpre1b, ta=True, tm=1024, tn=1024, tk=S, out_dtype=both, name="d_w_out",
                   side=scatter.take(26))
    scatter.add_blocks("w_out", _blocks(g32, "rows"), _blocks(g16, "rows"))
    dmix = _mm(dpre1b, w_out, tb=True, tm=1024, tn=1024, tk=D, name="d_mixin", side=scatter.take(22))
    dya, dyr, dgl_a, dgl_r, db_a, db_r = _gate_bwd(dmix, y_attn, y_rnn, proj, small["b_gate"], side=scatter.take(36))
    grads["b_gate"] = jnp.concatenate([db_a, db_r], axis=1)
    g32, g16 = _mm(ot, dya, tm=1024, tn=1024, tk=S, out_dtype=both, name="d_w_attn_proj", side=scatter.take(38))
    scatter.add_blocks("w_attn_proj", _blocks(g32, "rows"), _blocks(g16, "rows"))
    g32, g16 = _mm(yrin, dyr, ta=True, tm=1280, tn=1024, tk=S, out_dtype=both, name="d_w_rnn_proj",
                   side=scatter.take(27))
    scatter.add_blocks("w_rnn_proj", _blocks(g32, "rows"), _blocks(g16, "rows"))
    dot_ = _mm(w_ap, dya, tb=True, tm=1024, tn=1024, tk=D, out_dtype=MXU_DTYPE, name="d_o", side=scatter.take(22))
    dyrin = _mm(dyr, w_rp, tb=True, tm=1024, tn=1280, tk=D, name="d_yrin", side=scatter.take(27))

    dry, dzr, dzi, drxc_in, grads["lru_ba"], grads["lru_bi"], grads["lru_lambda"] = _lru_scan_bwd(
        dyrin, proj, h, r, i, rxc, small["lru_lambda"], side=scatter.take(94))
    grads["lru_wa"], grads["lru_wi"] = _lru_gate_wgrad(rxc, dzr, dzi, side=scatter.take(22))
    drxc = _lru_gate_xgrad(dzr, dzi, small["lru_wa"], small["lru_wi"], drxc_in, side=scatter.take(33))
    drx, grads["rnn_conv_w"], grads["rnn_conv_b"] = _conv_bwd(drxc, proj, OFF_RX, rconv_w, tc=512,
                                                             name="rnn_conv_bwd", side=scatter.take(29))

    dqt, dk, dv, dsink = _attn_bwd(qt, kp, kt, vp, sink_row, dot_.reshape(N_KV, GROUP, HEAD_DIM, S),
                                   side=scatter.take(100))
    grads["attn_sinks"] = dsink.reshape(1, N_KV * GROUP)
    dproj = jnp.concatenate([
        dqt.reshape(D, S).T,
        _heads_minor(dk[:, BLOCK:, :]).astype(MXU_DTYPE),
        _heads_minor(dv[:, BLOCK:, :]).astype(MXU_DTYPE),
        drx, dry, dgl_a, dgl_r], axis=1)
    g32, g16 = _mm(xb, dproj, ta=True, tm=1024, tn=512, tk=S, out_dtype=both, name="d_w_in", side=scatter.take(110))
    scatter.add_cols("w_in", g32, g16)
    dx = _mm(dproj, w_in, tb=True, tm=1024, tn=1024, tk=512, add=dpre1, add_scale=ALPHA, name="d_x",
             side=scatter.take(150))
    return loss, dx, grads


SHARDED = (
    ("w_in", "cols", 128), ("w_attn_proj", "rows", 128), ("w_rnn_proj", "rows", 160), ("w_out", "rows", 128),
    ("ffn_w_up", "cols", 256), ("ffn_w_gate", "cols", 256), ("ffn_w_down", "rows", 256),
)
SMALL_REPLICATED = ("b_gate", "rnn_conv_b", "lru_wa", "lru_ba", "lru_wi", "lru_bi", "lru_lambda", "attn_sinks",
                    "ln1_g", "ln1_b", "ffn_conv_b", "ln2_g", "ln2_b")
SMALL_SHARDED = ("rnn_conv_w", "ffn_conv_w")
SMALL_MATS = ("lru_wa", "lru_wi")
WEIGHTS = ("w_in", "b_gate", "rnn_conv_w", "rnn_conv_b", "lru_wa", "lru_ba", "lru_wi", "lru_bi", "lru_lambda",
           "attn_sinks", "w_attn_proj", "w_rnn_proj", "w_out", "ln1_g", "ln1_b", "ffn_w_up", "ffn_w_gate",
           "ffn_conv_w", "ffn_conv_b", "ffn_w_down", "ln2_g", "ln2_b")
ADAMW_US = {"w_attn_proj": 9, "w_rnn_proj": 11, "w_out": 9, "ffn_w_up": 26, "ffn_w_gate": 26, "ffn_w_down": 28}


def kernel(x, w_in, b_gate, rnn_conv_w, rnn_conv_b, lru_wa, lru_ba, lru_wi, lru_bi, lru_lambda, attn_sinks, w_attn_proj, w_rnn_proj, w_out, ln1_g, ln1_b, ffn_w_up, ffn_w_gate, ffn_conv_w, ffn_conv_b, ffn_w_down, ln2_g, ln2_b, loss_target, m_w_in, m_b_gate, m_rnn_conv_w, m_rnn_conv_b, m_lru_wa, m_lru_ba, m_lru_wi, m_lru_bi, m_lru_lambda, m_attn_sinks, m_w_attn_proj, m_w_rnn_proj, m_w_out, m_ln1_g, m_ln1_b, m_ffn_w_up, m_ffn_w_gate, m_ffn_conv_w, m_ffn_conv_b, m_ffn_w_down, m_ln2_g, m_ln2_b, v_w_in, v_b_gate, v_rnn_conv_w, v_rnn_conv_b, v_lru_wa, v_lru_ba, v_lru_wi, v_lru_bi, v_lru_lambda, v_attn_sinks, v_w_attn_proj, v_w_rnn_proj, v_w_out, v_ln1_g, v_ln1_b, v_ffn_w_up, v_ffn_w_gate, v_ffn_conv_w, v_ffn_conv_b, v_ffn_w_down, v_ln2_g, v_ln2_b):
    given = dict(locals())
    wsh = {n: given[n][0] for n in WEIGHTS}
    msh = {n: given["m_" + n][0] for n in WEIGHTS}
    vsh = {n: given["v_" + n][0] for n in WEIGHTS}
    m_given = {n: given["m_" + n] for n in WEIGHTS}
    v_given = {n: given["v_" + n] for n in WEIGHTS}
    me = 4 * lax.axis_index("x") + 2 * lax.axis_index("y") + lax.axis_index("c")

    order = ("w_in", "rnn_conv_w", "ffn_conv_w", "w_attn_proj", "w_rnn_proj", "w_out", "ffn_w_up", "ffn_w_gate",
             "ffn_w_down")
    gather = _Gather({n: wsh[n] if n in SMALL_SHARDED else wsh[n].astype(MXU_DTYPE) for n in order})
    _run_side(gather.take(through="ffn_conv_w"), "gather_first")
    small = {n: given[n] for n in SMALL_REPLICATED}
    small["lru_wa"] = _block_diag(wsh["lru_wa"])
    small["lru_wi"] = _block_diag(wsh["lru_wi"])
    scatter = _Scatter(me)

    loss, dx, grads = _forward_backward(x[0], loss_target[0], small, gather, scatter)
    grads["lru_wa"] = _diag_blocks(grads["lru_wa"])
    grads["lru_wi"] = _diag_blocks(grads["lru_wi"])

    vec_names = tuple(n for n in SMALL_REPLICATED if n not in SMALL_MATS) + SMALL_SHARDED
    sums, mat_sums = _all_reduce_small([loss] + [grads[n] for n in vec_names], [grads[n] for n in SMALL_MATS])
    loss_total = sums[0].reshape(())
    g_small = dict(zip(vec_names, sums[1:]))
    for n in SMALL_SHARDED:
        width = wsh[n].shape[1]
        g_small[n] = lax.dynamic_slice_in_dim(g_small[n], me * width, width, axis=1)
    g_small = {n: g_small[n].reshape(given[n].shape) for n in vec_names}
    out = {}
    results = _adamw_many(*[[d[n] for n in vec_names] for d in (given, m_given, v_given, g_small)])
    for n, delta, nm, nv in zip(vec_names, *results):
        out[n] = (g_small[n], delta, nm, nv)
    for n, g in zip(SMALL_MATS, mat_sums):
        g = g.reshape(given[n].shape)
        out[n] = (g, *_adamw_blocks(given[n], m_given[n], v_given[n], g, name="adamw_" + n, side=scatter.take(4)))

    tile_rows = {n: tr for n, _, tr in SHARDED}
    for n in list(scatter.sends):
        own, recv = scatter.get(n)
        res = _reduce_adamw(wsh[n], msh[n], vsh[n], own, recv, tr=tile_rows[n], name="adamw_" + n,
                            side=scatter.take(0.9 * ADAMW_US[n]) if n in ADAMW_US else None)
        out[n] = tuple(r[None] for r in res)

    outputs = [loss_total, dx[None]]
    for kind in range(4):
        outputs += [out[n][kind] for n in WEIGHTS]
    return tuple(outputs)
```

```python
import math

import jax
import jax.numpy as jnp
from jax import lax
from jax.experimental import pallas as pl
from jax.experimental.pallas import tpu as pltpu

F32 = jnp.float32
BF16 = jnp.bfloat16
MXU_DTYPE = jnp.bfloat16

N_DEV = 8
S = 2048
D = 2048
HEAD_DIM = 64
N_KV = 4
GROUP = 8
BLOCK = 128
D_KV = N_KV * HEAD_DIM
D_RNN = 2560
RNN_GROUP = 640
N_RNN_GROUPS = D_RNN // RNN_GROUP
RNN_BLOCK_W = 160
RNN_CONV_W = 4
LRU_C = 8.0
D_FF = 6144
FFN_CONV_W = 3
D_IN = 11776
OFF_K = 2048
OFF_V = 2304
OFF_RX = 2560
OFF_RY = 5120
OFF_GA = 7680
OFF_GR = 9728
LN_EPS = 1e-5
ALPHA = 2.0 ** 0.25
ADAM_LR = 0.001
ADAM_B1 = 0.9
ADAM_B2 = 0.999
ADAM_EPS = 1e-08
ADAM_WD = 0.01
ADAM_STEP = 10
NEG = -1e30
VMEM_LIMIT = 56 * 1024 * 1024
MESH = pl.DeviceIdType.MESH
GELU_C = math.sqrt(2.0 / math.pi)


def _cparams(*sem):
    return pltpu.CompilerParams(dimension_semantics=sem or None, vmem_limit_bytes=VMEM_LIMIT)


def _call(body, *, name, grid, in_specs, out_specs, out_shape, operands, semantics, scratch_shapes=(), side=None):
    single = not isinstance(out_shape, (list, tuple))
    out_shape = [out_shape] if single else list(out_shape)
    out_specs = [out_specs] if single else list(out_specs)
    in_specs = list(in_specs)
    scratch_shapes = list(scratch_shapes)
    if side is None:
        res = pl.pallas_call(
            body, name=name, grid=grid, in_specs=in_specs, out_specs=out_specs, out_shape=out_shape,
            scratch_shapes=scratch_shapes, compiler_params=_cparams(*semantics))(*operands)
        return res[0] if single else res
    n_in, n_out, n_scr = len(in_specs), len(out_shape), len(scratch_shapes)
    s_in, s_out = len(side.operands), len(side.out_shape)
    hbm = pl.BlockSpec(memory_space=pltpu.HBM)

    def with_copies(*refs):
        core_in, side_in = refs[:n_in], refs[n_in:n_in + s_in]
        o0 = n_in + s_in
        core_out, side_out = refs[o0:o0 + n_out], refs[o0 + n_out:o0 + n_out + s_out]
        c0 = o0 + n_out + s_out
        core_scr, sems = refs[c0:c0 + n_scr], refs[c0 + n_scr:]
        first, last = None, None
        for d, size in enumerate(grid):
            at_start, at_end = pl.program_id(d) == 0, pl.program_id(d) == size - 1
            first = at_start if first is None else first & at_start
            last = at_end if last is None else last & at_end

        @pl.when(first)
        def _():
            side.start(side_in, side_out, sems)

        body(*core_in, *core_out, *core_scr)

        @pl.when(last)
        def _():
            side.finish(side_in, side_out, sems)

    res = pl.pallas_call(
        with_copies, name=name, grid=grid,
        in_specs=in_specs + [hbm] * s_in, out_specs=out_specs + [hbm] * s_out,
        out_shape=out_shape + list(side.out_shape),
        scratch_shapes=scratch_shapes + list(side.sems),
        input_output_aliases={n_in + i: n_out + o for i, o in side.aliases.items()},
        compiler_params=_cparams(*(("arbitrary",) * len(grid))))(*operands, *side.operands)
    side.done(res[n_out:])
    return res[0] if single else res[:n_out]


def _run_side(side, name):
    def body(*refs):
        s_in, s_out = len(side.operands), len(side.out_shape)
        side.start(refs[:s_in], refs[s_in:s_in + s_out], refs[s_in + s_out:])
        side.finish(refs[:s_in], refs[s_in:s_in + s_out], refs[s_in + s_out:])

    hbm = pl.BlockSpec(memory_space=pltpu.HBM)
    res = pl.pallas_call(
        body, name=name, in_specs=[hbm] * len(side.operands), out_specs=[hbm] * len(side.out_shape),
        out_shape=list(side.out_shape), scratch_shapes=list(side.sems),
        input_output_aliases=dict(side.aliases))(*side.operands)
    side.done(res)


def _gelu(x):
    x2 = x * x
    t = jnp.tanh(GELU_C * (x + 0.044715 * x * x2))
    g = 0.5 * x * (1.0 + t)
    dg = 0.5 * (1.0 + t) + 0.5 * x * (1.0 - t * t) * (GELU_C * (1.0 + 3.0 * 0.044715 * x2))
    return g, dg


def _sigmoid(x):
    return 1.0 / (1.0 + jnp.exp(-x))


def _softplus(x):
    z = jnp.exp(-jnp.abs(x))
    small = z * (1.0 - z * (0.5 - z * (1.0 / 3.0 - 0.25 * z)))
    return jnp.maximum(x, 0.0) + jnp.where(z < 0.02, small, jnp.log(1.0 + z))


def _one_minus_exp(x):
    series = -x * (1.0 + x * (0.5 + x * (1.0 / 6.0 + x * (1.0 / 24.0))))
    return jnp.where(x > -0.03, series, 1.0 - jnp.exp(x))


def _colsum(v):
    return jnp.sum(v, axis=0, keepdims=True)


def _mm(a, b, *, tm, tn, tk, name, ta=False, tb=False, out_dtype=F32, b_block=None, out_block=None, add=None,
        add_scale=1.0, side=None):
    out_dtypes = out_dtype if isinstance(out_dtype, tuple) else (out_dtype,)
    if ta:
        k_dim, m_dim = a.shape
    else:
        m_dim, k_dim = a.shape
    if b_block is None:
        n_dim = b.shape[0] if tb else b.shape[1]
    else:
        n_dim = b.shape[1] if tb else b.shape[0] * b_block
    assert m_dim % tm == 0 and n_dim % tn == 0 and k_dim % tk == 0, (name, m_dim, n_dim, k_dim)
    nk = k_dim // tk
    dims = (((0 if ta else 1,), (1 if tb else 0,)), ((), ()))
    has_add = add is not None

    def body(*refs):
        a_ref, b_ref = refs[0], refs[1]
        add_ref = refs[2] if has_add else None
        first_out = 3 if has_add else 2
        o_refs = refs[first_out:first_out + len(out_dtypes)]

        def product():
            return lax.dot_general(a_ref[...].astype(MXU_DTYPE), b_ref[...].astype(MXU_DTYPE), dims,
                                   preferred_element_type=F32)

        def finish(acc):
            if has_add:
                acc = acc + add_scale * add_ref[...]
            for o_ref in o_refs:
                o_ref[...] = acc.astype(o_ref.dtype)

        if nk == 1:
            finish(product())
        else:
            acc_ref = refs[-1]
            k = pl.program_id(2)

            @pl.when(k == 0)
            def _():
                acc_ref[...] = jnp.zeros_like(acc_ref)

            acc_ref[...] += product()

            @pl.when(k == nk - 1)
            def _():
                finish(acc_ref[...])

    if ta:
        a_spec = pl.BlockSpec((tk, tm), lambda i, j, k: (k, i))
    else:
        a_spec = pl.BlockSpec((tm, tk), lambda i, j, k: (i, k))
    if b_block is None:
        if tb:
            b_spec = pl.BlockSpec((tn, tk), lambda i, j, k: (j, k))
        else:
            b_spec = pl.BlockSpec((tk, tn), lambda i, j, k: (k, j))
    elif tb:
        assert b_block % tk == 0
        b_spec = pl.BlockSpec((None, tn, tk), lambda i, j, k: ((k * tk) // b_block, j, ((k * tk) % b_block) // tk))
    else:
        assert b_block % tn == 0
        b_spec = pl.BlockSpec((None, tk, tn), lambda i, j, k: ((j * tn) // b_block, k, ((j * tn) % b_block) // tn))
    in_specs = [a_spec, b_spec]
    operands = [a, b]
    if has_add:
        in_specs.append(pl.BlockSpec((tm, tn), lambda i, j, k: (i, j)))
        operands.append(add)
    if out_block is None:
        out_spec = pl.BlockSpec((tm, tn), lambda i, j, k: (i, j))
        out_dims = (m_dim, n_dim)
    else:
        assert out_block % tn == 0
        out_spec = pl.BlockSpec((None, tm, tn), lambda i, j, k: ((j * tn) // out_block, i, ((j * tn) % out_block) // tn))
        out_dims = (n_dim // out_block, m_dim, out_block)
    res = _call(
        body,
        name=name,
        grid=(m_dim // tm, n_dim // tn, nk),
        in_specs=in_specs,
        out_specs=[out_spec] * len(out_dtypes),
        out_shape=[jax.ShapeDtypeStruct(out_dims, dt) for dt in out_dtypes],
        scratch_shapes=[pltpu.VMEM((tm, tn), F32)] if nk > 1 else [],
        semantics=("parallel", "parallel", "arbitrary"),
        operands=tuple(operands),
        side=side,
    )
    return res if isinstance(out_dtype, tuple) else res[0]


def _attn_bias(bias_ref, h):
    key = lax.broadcasted_iota(jnp.int32, (2 * BLOCK, GROUP * BLOCK), 0)
    col = lax.broadcasted_iota(jnp.int32, (2 * BLOCK, GROUP * BLOCK), 1)
    dist = BLOCK + (col & (BLOCK - 1)) - key
    head = h * GROUP + (col >> 7) + 1
    slope = jnp.exp(head.astype(F32) * (-0.25 * math.log(2.0)))
    bias = jnp.where((dist >= 0) & (dist < BLOCK), -slope * dist.astype(F32), NEG)
    bias_ref[1] = bias
    bias_ref[0] = jnp.where(key < BLOCK, NEG, bias)


def _attn_probs(kb, qt, bias, sink):
    s = jnp.dot(kb, qt, preferred_element_type=F32) * (HEAD_DIM ** -0.5) + bias
    m = jnp.maximum(jnp.max(s, axis=0, keepdims=True), sink)
    e = jnp.exp(s - m)
    e_sink = jnp.exp(sink - m)
    inv = 1.0 / (jnp.sum(e, axis=0, keepdims=True) + e_sink)
    return e * inv, e_sink * inv


def _heads_on_lanes(ref, r0):
    return jnp.concatenate([ref[g, :, pl.ds(r0, BLOCK)] for g in range(GROUP)], axis=1)


def _attn_fwd(qt, kp, vt, sink_row, side=None):
    cols = GROUP * BLOCK

    def body(q_ref, k_ref, vt_ref, sink_ref, o_ref, bias_ref):
        _attn_bias(bias_ref, pl.program_id(0))
        sink = sink_ref[...]

        def step(n, carry):
            r0 = pl.multiple_of(n * BLOCK, BLOCK)
            p, _ = _attn_probs(k_ref[pl.ds(r0, 2 * BLOCK), :], _heads_on_lanes(q_ref, r0),
                               bias_ref[jnp.minimum(n, 1)], sink)
            o = jnp.dot(vt_ref[:, pl.ds(r0, 2 * BLOCK)], p.astype(MXU_DTYPE), preferred_element_type=F32)
            for g in range(GROUP):
                o_ref[g, :, pl.ds(r0, BLOCK)] = o[:, g * BLOCK:(g + 1) * BLOCK].astype(o_ref.dtype)
            return carry

        lax.fori_loop(0, S // BLOCK, step, 0)

    hm = pl.BlockSpec((None, GROUP, HEAD_DIM, S), lambda h: (h, 0, 0, 0))
    return _call(
        body,
        name="attn_fwd",
        grid=(N_KV,),
        in_specs=[
            hm,
            pl.BlockSpec((None, BLOCK + S, HEAD_DIM), lambda h: (h, 0, 0)),
            pl.BlockSpec((None, HEAD_DIM, BLOCK + S), lambda h: (h, 0, 0)),
            pl.BlockSpec((None, 1, cols), lambda h: (h, 0, 0)),
        ],
        out_specs=hm,
        out_shape=jax.ShapeDtypeStruct((N_KV, GROUP, HEAD_DIM, S), MXU_DTYPE),
        scratch_shapes=[pltpu.VMEM((2, 2 * BLOCK, cols), F32)],
        semantics=("parallel",),
        operands=(qt, kp, vt, sink_row),
        side=side,
    )


def _attn_bwd(qt, kp, kt, vp, sink_row, dot_, side=None):
    cols = GROUP * BLOCK

    def body(q_ref, k_ref, kt_ref, v_ref, sink_ref, do_ref, dq_ref, dk_ref, dv_ref, dsink_ref, bias_ref):
        _attn_bias(bias_ref, pl.program_id(0))
        sink = sink_ref[...]
        dk_ref[...] = jnp.zeros_like(dk_ref)
        dv_ref[...] = jnp.zeros_like(dv_ref)
        nt = (((1,), (1,)), ((), ()))

        def step(n, sink_acc):
            r0 = pl.multiple_of(n * BLOCK, BLOCK)
            band = pl.ds(r0, 2 * BLOCK)
            qn = _heads_on_lanes(q_ref, r0)
            don = _heads_on_lanes(do_ref, r0)
            p, p_sink = _attn_probs(k_ref[band, :], qn, bias_ref[jnp.minimum(n, 1)], sink)
            dp = jnp.dot(v_ref[band, :], don, preferred_element_type=F32)
            delta = jnp.sum(p * dp, axis=0, keepdims=True)
            ds = (p * (dp - delta) * (HEAD_DIM ** -0.5)).astype(MXU_DTYPE)
            dq = jnp.dot(kt_ref[:, band], ds, preferred_element_type=F32)
            for g in range(GROUP):
                dq_ref[g, :, pl.ds(r0, BLOCK)] = dq[:, g * BLOCK:(g + 1) * BLOCK].astype(dq_ref.dtype)
            dk_ref[band, :] += lax.dot_general(ds, qn, nt, preferred_element_type=F32)
            dv_ref[band, :] += lax.dot_general(p.astype(MXU_DTYPE), don, nt, preferred_element_type=F32)
            return sink_acc - p_sink * delta

        sink_acc = lax.fori_loop(0, S // BLOCK, step, jnp.zeros((1, cols), F32))
        for g in range(GROUP):
            dsink_ref[g:g + 1, :] = jnp.sum(sink_acc[:, g * BLOCK:(g + 1) * BLOCK], axis=1, keepdims=True)

    hm = pl.BlockSpec((None, GROUP, HEAD_DIM, S), lambda h: (h, 0, 0, 0))
    kv = pl.BlockSpec((None, BLOCK + S, HEAD_DIM), lambda h: (h, 0, 0))
    return _call(
        body,
        name="attn_bwd",
        grid=(N_KV,),
        in_specs=[hm, kv, pl.BlockSpec((None, HEAD_DIM, BLOCK + S), lambda h: (h, 0, 0)), kv,
                  pl.BlockSpec((None, 1, cols), lambda h: (h, 0, 0)), hm],
        out_specs=[hm, kv, kv, pl.BlockSpec((None, GROUP, 1), lambda h: (h, 0, 0))],
        out_shape=[
            jax.ShapeDtypeStruct((N_KV, GROUP, HEAD_DIM, S), MXU_DTYPE),
            jax.ShapeDtypeStruct((N_KV, BLOCK + S, HEAD_DIM), F32),
            jax.ShapeDtypeStruct((N_KV, BLOCK + S, HEAD_DIM), F32),
            jax.ShapeDtypeStruct((N_KV, GROUP, 1), F32),
        ],
        scratch_shapes=[pltpu.VMEM((2, 2 * BLOCK, cols), F32)],
        semantics=("parallel",),
        operands=(qt, kp, kt, vp, sink_row, dot_),
        side=side,
    )


PAD = 8
CHUNK = 256


def _past_taps(xpad_ref, r0, width):
    ext = xpad_ref[pl.ds(r0, CHUNK + PAD), :]
    taps = []
    for k in range(width):
        back = width - 1 - k
        taps.append((ext if back == 0 else pltpu.roll(ext, back, 0))[PAD:, :])
    return taps


def _future_taps(xpad_ref, r0, width):
    ext = xpad_ref[pl.ds(r0, CHUNK + PAD), :]
    taps = []
    for ahead in range(width):
        taps.append((ext if ahead == 0 else pltpu.roll(ext, CHUNK + PAD - ahead, 0))[:CHUNK, :])
    return taps


def _conv_fwd(src, col0, w, b, *, tc, name, side=None):
    width, c_dim = w.shape

    def body(x_ref, w_ref, b_ref, o_ref, xpad_ref):
        xpad_ref[pl.ds(0, PAD), :] = jnp.zeros((PAD, tc), F32)
        xpad_ref[pl.ds(PAD, S), :] = x_ref[...]
        wv = w_ref[...]
        bv = b_ref[...]

        def step(ci, carry):
            r0 = pl.multiple_of(ci * CHUNK, CHUNK)
            taps = _past_taps(xpad_ref, r0, width)
            y = bv + taps[0] * wv[0:1, :]
            for k in range(1, width):
                y = y + taps[k] * wv[k:k + 1, :]
            o_ref[pl.ds(r0, CHUNK), :] = y
            return carry

        lax.fori_loop(0, S // CHUNK, step, 0)

    return _call(
        body,
        name=name,
        grid=(c_dim // tc,),
        in_specs=[
            pl.BlockSpec((S, tc), lambda j: (0, col0 // tc + j)),
            pl.BlockSpec((width, tc), lambda j: (0, j)),
            pl.BlockSpec((1, tc), lambda j: (0, j)),
        ],
        out_specs=pl.BlockSpec((S, tc), lambda j: (0, j)),
        out_shape=jax.ShapeDtypeStruct((S, c_dim), F32),
        scratch_shapes=[pltpu.VMEM((S + PAD, tc), F32)],
        semantics=("parallel",),
        operands=(src, w, b),
        side=side,
    )


def _conv_bwd(dy, src, col0, w, *, tc, name, side=None):
    width, c_dim = w.shape

    def body(dy_ref, x_ref, w_ref, dx_ref, dw_ref, db_ref, xpad_ref, dpad_ref):
        xpad_ref[pl.ds(0, PAD), :] = jnp.zeros((PAD, tc), F32)
        xpad_ref[pl.ds(PAD, S), :] = x_ref[...]
        dpad_ref[pl.ds(0, S), :] = dy_ref[...]
        dpad_ref[pl.ds(S, PAD), :] = jnp.zeros((PAD, tc), F32)
        wv = w_ref[...]

        def step(ci, acc):
            r0 = pl.multiple_of(ci * CHUNK, CHUNK)
            past = _past_taps(xpad_ref, r0, width)
            ahead = _future_taps(dpad_ref, r0, width)
            d = ahead[0]
            dx = d * wv[width - 1:width, :]
            for j in range(1, width):
                dx = dx + ahead[j] * wv[width - 1 - j:width - j, :]
            dx_ref[pl.ds(r0, CHUNK), :] = dx.astype(dx_ref.dtype)
            return tuple(acc[k] + _colsum(past[k] * d) for k in range(width)) + (acc[width] + _colsum(d),)

        zero = jnp.zeros((1, tc), F32)
        acc = lax.fori_loop(0, S // CHUNK, step, (zero,) * (width + 1))
        for k in range(width):
            dw_ref[k:k + 1, :] = acc[k]
        db_ref[...] = acc[width]

    return _call(
        body,
        name=name,
        grid=(c_dim // tc,),
        in_specs=[
            pl.BlockSpec((S, tc), lambda j: (0, j)),
            pl.BlockSpec((S, tc), lambda j: (0, col0 // tc + j)),
            pl.BlockSpec((width, tc), lambda j: (0, j)),
        ],
        out_specs=[
            pl.BlockSpec((S, tc), lambda j: (0, j)),
            pl.BlockSpec((width, tc), lambda j: (0, j)),
            pl.BlockSpec((1, tc), lambda j: (0, j)),
        ],
        out_shape=[
            jax.ShapeDtypeStruct((S, c_dim), MXU_DTYPE),
            jax.ShapeDtypeStruct((width, c_dim), F32),
            jax.ShapeDtypeStruct((1, c_dim), F32),
        ],
        scratch_shapes=[pltpu.VMEM((S + PAD, tc), F32), pltpu.VMEM((S + PAD, tc), F32)],
        semantics=("parallel",),
        operands=(dy, src, w),
        side=side,
    )


SCAN_TC = 256


def _lru_gates(rxc, wa, wi, ba, bi, side=None):
    tm = 512

    def body(x_ref, wa_ref, wi_ref, ba_ref, bi_ref, r_ref, i_ref):
        xv = x_ref[...].astype(MXU_DTYPE)
        r_ref[...] = _sigmoid(jnp.dot(xv, wa_ref[...].astype(MXU_DTYPE), preferred_element_type=F32) + ba_ref[...])
        i_ref[...] = _sigmoid(jnp.dot(xv, wi_ref[...].astype(MXU_DTYPE), preferred_element_type=F32) + bi_ref[...])

    x_spec = pl.BlockSpec((tm, RNN_GROUP), lambda g, i: (i, g))
    w_spec = pl.BlockSpec((None, RNN_GROUP, RNN_GROUP), lambda g, i: (g, 0, 0))
    b_spec = pl.BlockSpec((1, RNN_GROUP), lambda g, i: (0, g))
    return _call(
        body,
        name="lru_gates",
        grid=(N_RNN_GROUPS, S // tm),
        in_specs=[x_spec, w_spec, w_spec, b_spec, b_spec],
        out_specs=[x_spec, x_spec],
        out_shape=[jax.ShapeDtypeStruct((S, D_RNN), F32)] * 2,
        semantics=("parallel", "parallel"),
        operands=(rxc, wa, wi, ba, bi),
        side=side,
    )


def _scan_down(a, u, row):
    for d in (1, 2, 4):
        a_s = jnp.where(row >= d, pltpu.roll(a, d, 0), 1.0)
        u_s = jnp.where(row >= d, pltpu.roll(u, d, 0), 0.0)
        u = a * u_s + u
        a = a * a_s
    return a, u


def _scan_up(a, u, row):
    for d in (1, 2, 4):
        a_s = jnp.where(row < 8 - d, pltpu.roll(a, 8 - d, 0), 1.0)
        u_s = jnp.where(row < 8 - d, pltpu.roll(u, 8 - d, 0), 0.0)
        u = a * u_s + u
        a = a * a_s
    return a, u


def _lru_scan_fwd(r, i, rxc, proj, lam, side=None):
    tc = SCAN_TC

    def body(r_ref, i_ref, x_ref, ry_ref, lam_ref, h_ref, y_ref):
        rate = LRU_C * _softplus(-lam_ref[...])
        row = lax.broadcasted_iota(jnp.int32, (8, tc), 0)

        def step(ci, carry):
            r0 = pl.multiple_of(ci * 16, 16)
            log_a = -rate * r_ref[pl.ds(r0, 16), :]
            a16 = jnp.exp(log_a)
            u16 = jnp.sqrt(_one_minus_exp(2.0 * log_a)) * (i_ref[pl.ds(r0, 16), :] * x_ref[pl.ds(r0, 16), :])
            hs = []
            for half in range(2):
                a_cum, h0 = _scan_down(a16[8 * half:8 * half + 8, :], u16[8 * half:8 * half + 8, :], row)
                h = a_cum * carry + h0
                carry = jnp.broadcast_to(h[7:8, :], (8, tc))
                hs.append(h)
            h16 = jnp.concatenate(hs, axis=0)
            h_ref[pl.ds(r0, 16), :] = h16
            y_ref[pl.ds(r0, 16), :] = (h16 * _gelu(ry_ref[pl.ds(r0, 16), :])[0]).astype(y_ref.dtype)
            return carry

        lax.fori_loop(0, S // 16, step, jnp.zeros((8, tc), F32))

    col = pl.BlockSpec((S, tc), lambda j: (0, j))
    return _call(
        body,
        name="lru_scan_fwd",
        grid=(D_RNN // tc,),
        in_specs=[col, col, col, pl.BlockSpec((S, tc), lambda j: (0, OFF_RY // tc + j)),
                  pl.BlockSpec((1, tc), lambda j: (0, j))],
        out_specs=[col, col],
        out_shape=[jax.ShapeDtypeStruct((S, D_RNN), F32), jax.ShapeDtypeStruct((S, D_RNN), MXU_DTYPE)],
        semantics=("parallel",),
        operands=(r, i, rxc, proj, lam),
        side=side,
    )


def _lru_scan_bwd(dy, proj, h, r, i, rxc, lam, side=None):
    tc = SCAN_TC

    def body(dy_ref, ry_ref, h_ref, r_ref, i_ref, x_ref, lam_ref,
             dry_ref, dzr_ref, dzi_ref, dx_ref, dba_ref, dbi_ref, dlam_ref, a_ref, dh_ref, hp_ref):
        lam_v = lam_ref[...]
        rate = LRU_C * _softplus(-lam_v)
        dlam_scale = LRU_C * _sigmoid(-lam_v)
        row = lax.broadcasted_iota(jnp.int32, (8, tc), 0)
        hp_ref[pl.ds(0, PAD), :] = jnp.zeros((PAD, tc), F32)
        hp_ref[pl.ds(PAD, S), :] = h_ref[...]
        a_ref[pl.ds(S, PAD), :] = jnp.zeros((PAD, tc), F32)

        def prep(ci, carry):
            r0 = pl.multiple_of(ci * CHUNK, CHUNK)
            a_ref[pl.ds(r0, CHUNK), :] = jnp.exp(-rate * r_ref[pl.ds(r0, CHUNK), :])
            ge, dge = _gelu(ry_ref[pl.ds(r0, CHUNK), :])
            dyv = dy_ref[pl.ds(r0, CHUNK), :]
            dh_ref[pl.ds(r0, CHUNK), :] = dyv * ge
            dry_ref[pl.ds(r0, CHUNK), :] = (dyv * h_ref[pl.ds(r0, CHUNK), :] * dge).astype(dry_ref.dtype)
            return carry

        lax.fori_loop(0, S // CHUNK, prep, 0)

        def step(ci, state):
            carry, dba, dbi, dlam = state
            r0 = pl.multiple_of(S - 16 - ci * 16, 16)
            a_ext = a_ref[pl.ds(r0, 24), :]
            a_next = pltpu.roll(a_ext, 23, 0)
            h_prev = pltpu.roll(hp_ref[pl.ds(r0, 24), :], 1, 0)
            dh16 = dh_ref[pl.ds(r0, 16), :]
            gs = [None, None]
            for half in (1, 0):
                lo = 8 * half
                c_cum, g0 = _scan_up(a_next[lo:lo + 8, :], dh16[lo:lo + 8, :], row)
                g = c_cum * carry + g0
                carry = jnp.broadcast_to(g[0:1, :], (8, tc))
                gs[half] = g
            g16 = jnp.concatenate(gs, axis=0)
            a16 = a_ext[0:16, :]
            r16 = r_ref[pl.ds(r0, 16), :]
            i16 = i_ref[pl.ds(r0, 16), :]
            x16 = x_ref[pl.ds(r0, 16), :]
            a2 = a16 * a16
            sq = jnp.sqrt(_one_minus_exp(-2.0 * rate * r16))
            dx_ref[pl.ds(r0, 16), :] = g16 * sq * i16
            dzi = g16 * sq * x16 * i16 * (1.0 - i16)
            dlog_a = g16 * h_prev[8:24, :] * a16 - g16 * i16 * x16 * a2 / sq
            dzr = -rate * dlog_a * r16 * (1.0 - r16)
            dzr_ref[pl.ds(r0, 16), :] = dzr.astype(dzr_ref.dtype)
            dzi_ref[pl.ds(r0, 16), :] = dzi.astype(dzi_ref.dtype)
            return carry, dba + _colsum(dzr), dbi + _colsum(dzi), dlam + _colsum(dlog_a * r16)

        zero = jnp.zeros((1, tc), F32)
        _, dba, dbi, dlam = lax.fori_loop(0, S // 16, step, (jnp.zeros((8, tc), F32), zero, zero, zero))
        dba_ref[...] = dba
        dbi_ref[...] = dbi
        dlam_ref[...] = dlam * dlam_scale

    col = pl.BlockSpec((S, tc), lambda j: (0, j))
    vec = pl.BlockSpec((1, tc), lambda j: (0, j))
    return _call(
        body,
        name="lru_scan_bwd",
        grid=(D_RNN // tc,),
        in_specs=[col, pl.BlockSpec((S, tc), lambda j: (0, OFF_RY // tc + j)), col, col, col, col, vec],
        out_specs=[col, col, col, col, vec, vec, vec],
        out_shape=[jax.ShapeDtypeStruct((S, D_RNN), MXU_DTYPE)] * 3 + [jax.ShapeDtypeStruct((S, D_RNN), F32)]
        + [jax.ShapeDtypeStruct((1, D_RNN), F32)] * 3,
        scratch_shapes=[pltpu.VMEM((S + PAD, tc), F32), pltpu.VMEM((S, tc), F32), pltpu.VMEM((S + PAD, tc), F32)],
        semantics=("parallel",),
        operands=(dy, proj, h, r, i, rxc, lam),
        side=side,
    )


def _lru_gate_wgrad(rxc, dzr, dzi, side=None):
    def body(x_ref, dzr_ref, dzi_ref, dwa_ref, dwi_ref):
        xv = x_ref[...].astype(MXU_DTYPE)
        dims = (((0,), (0,)), ((), ()))
        dwa_ref[...] = lax.dot_general(xv, dzr_ref[...], dims, preferred_element_type=F32)
        dwi_ref[...] = lax.dot_general(xv, dzi_ref[...], dims, preferred_element_type=F32)

    col = pl.BlockSpec((S, RNN_GROUP), lambda g: (0, g))
    w_spec = pl.BlockSpec((None, RNN_GROUP, RNN_GROUP), lambda g: (g, 0, 0))
    return _call(
        body,
        name="lru_gate_wgrad",
        grid=(N_RNN_GROUPS,),
        in_specs=[col, col, col],
        out_specs=[w_spec, w_spec],
        out_shape=[jax.ShapeDtypeStruct((N_RNN_GROUPS, RNN_GROUP, RNN_GROUP), F32)] * 2,
        semantics=("parallel",),
        operands=(rxc, dzr, dzi),
        side=side,
    )


def _lru_gate_xgrad(dzr, dzi, wa, wi, dx_in, side=None):
    tm = 512

    def body(dzr_ref, dzi_ref, wa_ref, wi_ref, dx_ref, o_ref):
        dims = (((1,), (1,)), ((), ()))
        o_ref[...] = (dx_ref[...]
                      + lax.dot_general(dzr_ref[...], wa_ref[...].astype(MXU_DTYPE), dims, preferred_element_type=F32)
                      + lax.dot_general(dzi_ref[...], wi_ref[...].astype(MXU_DTYPE), dims, preferred_element_type=F32))

    x_spec = pl.BlockSpec((tm, RNN_GROUP), lambda g, i: (i, g))
    w_spec = pl.BlockSpec((None, RNN_GROUP, RNN_GROUP), lambda g, i: (g, 0, 0))
    return _call(
        body,
        name="lru_gate_xgrad",
        grid=(N_RNN_GROUPS, S // tm),
        in_specs=[x_spec, x_spec, w_spec, w_spec, x_spec],
        out_specs=x_spec,
        out_shape=jax.ShapeDtypeStruct((S, D_RNN), F32),
        semantics=("parallel", "parallel"),
        operands=(dzr, dzi, wa, wi, dx_in),
        side=side,
    )


def _gate_fwd(y_attn, y_rnn, proj, b_gate, side=None):
    t = 512

    def body(ya_ref, yr_ref, ga_ref, gr_ref, ba_ref, br_ref, o_ref):
        o_ref[...] = (_sigmoid(ga_ref[...] + ba_ref[...]) * ya_ref[...]
                      + _sigmoid(gr_ref[...] + br_ref[...]) * yr_ref[...]).astype(o_ref.dtype)

    tile = pl.BlockSpec((t, t), lambda i, j: (i, j))
    return _call(
        body,
        name="gate_fwd",
        grid=(S // t, D // t),
        in_specs=[tile, tile,
                  pl.BlockSpec((t, t), lambda i, j: (i, OFF_GA // t + j)),
                  pl.BlockSpec((t, t), lambda i, j: (i, OFF_GR // t + j)),
                  pl.BlockSpec((1, t), lambda i, j: (0, j)),
                  pl.BlockSpec((1, t), lambda i, j: (0, D // t + j))],
        out_specs=tile,
        out_shape=jax.ShapeDtypeStruct((S, D), MXU_DTYPE),
        semantics=("parallel", "parallel"),
        operands=(y_attn, y_rnn, proj, proj, b_gate, b_gate),
        side=side,
    )


def _gate_bwd(dmix, y_attn, y_rnn, proj, b_gate, side=None):
    t = 512

    def body(dm_ref, ya_ref, yr_ref, ga_ref, gr_ref, ba_ref, br_ref,
             dya_ref, dyr_ref, dga_ref, dgr_ref, dba_ref, dbr_ref):
        @pl.when(pl.program_id(1) == 0)
        def _():
            dba_ref[...] = jnp.zeros_like(dba_ref)
            dbr_ref[...] = jnp.zeros_like(dbr_ref)

        dm = dm_ref[...]
        ga = _sigmoid(ga_ref[...] + ba_ref[...])
        gr = _sigmoid(gr_ref[...] + br_ref[...])
        dya_ref[...] = (dm * ga).astype(dya_ref.dtype)
        dyr_ref[...] = (dm * gr).astype(dyr_ref.dtype)
        dga = dm * ya_ref[...] * ga * (1.0 - ga)
        dgr = dm * yr_ref[...] * gr * (1.0 - gr)
        dga_ref[...] = dga.astype(dga_ref.dtype)
        dgr_ref[...] = dgr.astype(dgr_ref.dtype)
        dba_ref[...] += _colsum(dga)
        dbr_ref[...] += _colsum(dgr)

    tile = pl.BlockSpec((t, t), lambda j, i: (i, j))
    vec = pl.BlockSpec((1, t), lambda j, i: (0, j))
    return _call(
        body,
        name="gate_bwd",
        grid=(D // t, S // t),
        in_specs=[tile, tile, tile,
                  pl.BlockSpec((t, t), lambda j, i: (i, OFF_GA // t + j)),
                  pl.BlockSpec((t, t), lambda j, i: (i, OFF_GR // t + j)),
                  vec,
                  pl.BlockSpec((1, t), lambda j, i: (0, D // t + j))],
        out_specs=[tile, tile, tile, tile, vec, vec],
        out_shape=[jax.ShapeDtypeStruct((S, D), MXU_DTYPE)] * 4 + [jax.ShapeDtypeStruct((1, D), F32)] * 2,
        semantics=("parallel", "arbitrary"),
        operands=(dmix, y_attn, y_rnn, proj, proj, b_gate, b_gate),
        side=side,
    )


LN_TM = 256


def _ln_stats(pre):
    mu = jnp.mean(pre, axis=-1, keepdims=True)
    xc = pre - mu
    rstd = lax.rsqrt(jnp.mean(xc * xc, axis=-1, keepdims=True) + LN_EPS)
    return xc * rstd, rstd


def _ln_input_grad(dy, xhat, rstd, g):
    dyg = dy * g
    return rstd * (dyg - jnp.mean(dyg, axis=-1, keepdims=True)
                   - xhat * jnp.mean(dyg * xhat, axis=-1, keepdims=True))


def _ln_fwd(res, branch, g, b, side=None):
    def body(res_ref, br_ref, g_ref, b_ref, y_ref, yb_ref, xhat_ref, rstd_ref):
        xhat, rstd = _ln_stats(ALPHA * res_ref[...] + br_ref[...])
        y = xhat * g_ref[...] + b_ref[...]
        y_ref[...] = y
        yb_ref[...] = y.astype(yb_ref.dtype)
        xhat_ref[...] = xhat
        rstd_ref[...] = rstd

    tile = pl.BlockSpec((LN_TM, D), lambda i: (i, 0))
    vec = pl.BlockSpec((1, D), lambda i: (0, 0))
    return _call(
        body,
        name="ln_fwd",
        grid=(S // LN_TM,),
        in_specs=[tile, tile, vec, vec],
        out_specs=[tile, tile, tile, pl.BlockSpec((LN_TM, 1), lambda i: (i, 0))],
        out_shape=[jax.ShapeDtypeStruct((S, D), F32), jax.ShapeDtypeStruct((S, D), MXU_DTYPE),
                   jax.ShapeDtypeStruct((S, D), F32), jax.ShapeDtypeStruct((S, 1), F32)],
        semantics=("parallel",),
        operands=(res, branch, g, b),
        side=side,
    )


def _ln_bwd(dy_a, dy_b, xhat, rstd, g, side=None):
    def body(da_ref, db_in_ref, xhat_ref, rstd_ref, g_ref, dp_ref, dpb_ref, dg_ref, db_ref):
        @pl.when(pl.program_id(0) == 0)
        def _():
            dg_ref[...] = jnp.zeros_like(dg_ref)
            db_ref[...] = jnp.zeros_like(db_ref)

        dy = da_ref[...] + ALPHA * db_in_ref[...]
        xhat = xhat_ref[...]
        dp = _ln_input_grad(dy, xhat, rstd_ref[...], g_ref[...])
        dp_ref[...] = dp
        dpb_ref[...] = dp.astype(dpb_ref.dtype)
        dg_ref[...] += _colsum(dy * xhat)
        db_ref[...] += _colsum(dy)

    tile = pl.BlockSpec((LN_TM, D), lambda i: (i, 0))
    vec = pl.BlockSpec((1, D), lambda i: (0, 0))
    return _call(
        body,
        name="ln_bwd",
        grid=(S // LN_TM,),
        in_specs=[tile, tile, tile, pl.BlockSpec((LN_TM, 1), lambda i: (i, 0)), vec],
        out_specs=[tile, tile, vec, vec],
        out_shape=[jax.ShapeDtypeStruct((S, D), F32), jax.ShapeDtypeStruct((S, D), MXU_DTYPE),
                   jax.ShapeDtypeStruct((1, D), F32), jax.ShapeDtypeStruct((1, D), F32)],
        semantics=("arbitrary",),
        operands=(dy_a, dy_b, xhat, rstd, g),
        side=side,
    )


def _ln_loss_bwd(res, branch, g, b, target, side=None):
    def body(res_ref, br_ref, g_ref, b_ref, t_ref, loss_ref, dp_ref, dpb_ref, dg_ref, db_ref):
        @pl.when(pl.program_id(0) == 0)
        def _():
            loss_ref[...] = jnp.zeros_like(loss_ref)
            dg_ref[...] = jnp.zeros_like(dg_ref)
            db_ref[...] = jnp.zeros_like(db_ref)

        xhat, rstd = _ln_stats(ALPHA * res_ref[...] + br_ref[...])
        gv = g_ref[...]
        err = xhat * gv + b_ref[...] - t_ref[...]
        loss_ref[...] += (0.5 / D) * jnp.sum(_colsum(err * err), axis=1, keepdims=True)
        dy = err * (1.0 / D)
        dp = _ln_input_grad(dy, xhat, rstd, gv)
        dp_ref[...] = dp
        dpb_ref[...] = dp.astype(dpb_ref.dtype)
        dg_ref[...] += _colsum(dy * xhat)
        db_ref[...] += _colsum(dy)

    tile = pl.BlockSpec((LN_TM, D), lambda i: (i, 0))
    vec = pl.BlockSpec((1, D), lambda i: (0, 0))
    return _call(
        body,
        name="ln_loss_bwd",
        grid=(S // LN_TM,),
        in_specs=[tile, tile, vec, vec, tile],
        out_specs=[pl.BlockSpec((1, 1), lambda i: (0, 0)), tile, tile, vec, vec],
        out_shape=[jax.ShapeDtypeStruct((1, 1), F32), jax.ShapeDtypeStruct((S, D), F32),
                   jax.ShapeDtypeStruct((S, D), MXU_DTYPE),
                   jax.ShapeDtypeStruct((1, D), F32), jax.ShapeDtypeStruct((1, D), F32)],
        semantics=("arbitrary",),
        operands=(res, branch, g, b, target),
        side=side,
    )


FFN_TC = 256


def _ffn_act_fwd(up, gpre, w, b, side=None):
    tc = FFN_TC

    def body(up_ref, x_ref, w_ref, b_ref, o_ref, xpad_ref):
        xpad_ref[pl.ds(0, PAD), :] = jnp.zeros((PAD, tc), F32)
        xpad_ref[pl.ds(PAD, S), :] = x_ref[...]
        wv = w_ref[...]
        bv = b_ref[...]

        def step(ci, carry):
            r0 = pl.multiple_of(ci * CHUNK, CHUNK)
            taps = _past_taps(xpad_ref, r0, FFN_CONV_W)
            gate = bv + taps[0] * wv[0:1, :] + taps[1] * wv[1:2, :] + taps[2] * wv[2:3, :]
            o_ref[pl.ds(r0, CHUNK), :] = (_gelu(gate)[0] * up_ref[pl.ds(r0, CHUNK), :]).astype(o_ref.dtype)
            return carry

        lax.fori_loop(0, S // CHUNK, step, 0)

    col = pl.BlockSpec((S, tc), lambda j: (0, j))
    return _call(
        body,
        name="ffn_act_fwd",
        grid=(D_FF // tc,),
        in_specs=[col, col, pl.BlockSpec((FFN_CONV_W, tc), lambda j: (0, j)), pl.BlockSpec((1, tc), lambda j: (0, j))],
        out_specs=col,
        out_shape=jax.ShapeDtypeStruct((S, D_FF), MXU_DTYPE),
        scratch_shapes=[pltpu.VMEM((S + PAD, tc), F32)],
        semantics=("parallel",),
        operands=(up, gpre, w, b),
        side=side,
    )


def _ffn_act_bwd(dfin, up, gpre, w, b, side=None):
    tc = FFN_TC
    width = FFN_CONV_W

    def body(df_ref, up_ref, x_ref, w_ref, b_ref, dup_ref, dx_ref, dw_ref, db_ref, xpad_ref, dpad_ref):
        xpad_ref[pl.ds(0, PAD), :] = jnp.zeros((PAD, tc), F32)
        xpad_ref[pl.ds(PAD, S), :] = x_ref[...]
        dpad_ref[pl.ds(S, PAD), :] = jnp.zeros((PAD, tc), F32)
        wv = w_ref[...]
        bv = b_ref[...]

        def gate_grad(ci, acc):
            r0 = pl.multiple_of(ci * CHUNK, CHUNK)
            taps = _past_taps(xpad_ref, r0, width)
            gate = bv + taps[0] * wv[0:1, :] + taps[1] * wv[1:2, :] + taps[2] * wv[2:3, :]
            ge, dge = _gelu(gate)
            df = df_ref[pl.ds(r0, CHUNK), :]
            dup_ref[pl.ds(r0, CHUNK), :] = (df * ge).astype(dup_ref.dtype)
            d = df * up_ref[pl.ds(r0, CHUNK), :] * dge
            dpad_ref[pl.ds(r0, CHUNK), :] = d
            return tuple(acc[k] + _colsum(taps[k] * d) for k in range(width)) + (acc[width] + _colsum(d),)

        zero = jnp.zeros((1, tc), F32)
        acc = lax.fori_loop(0, S // CHUNK, gate_grad, (zero,) * (width + 1))
        for k in range(width):
            dw_ref[k:k + 1, :] = acc[k]
        db_ref[...] = acc[width]

        def input_grad(ci, carry):
            r0 = pl.multiple_of(ci * CHUNK, CHUNK)
            ahead = _future_taps(dpad_ref, r0, width)
            dx = ahead[0] * wv[2:3, :] + ahead[1] * wv[1:2, :] + ahead[2] * wv[0:1, :]
            dx_ref[pl.ds(r0, CHUNK), :] = dx.astype(dx_ref.dtype)
            return carry

        lax.fori_loop(0, S // CHUNK, input_grad, 0)

    col = pl.BlockSpec((S, tc), lambda j: (0, j))
    w_spec = pl.BlockSpec((width, tc), lambda j: (0, j))
    vec = pl.BlockSpec((1, tc), lambda j: (0, j))
    return _call(
        body,
        name="ffn_act_bwd",
        grid=(D_FF // tc,),
        in_specs=[col, col, col, w_spec, vec],
        out_specs=[col, col, w_spec, vec],
        out_shape=[jax.ShapeDtypeStruct((S, D_FF), MXU_DTYPE)] * 2
        + [jax.ShapeDtypeStruct((width, D_FF), F32), jax.ShapeDtypeStruct((1, D_FF), F32)],
        scratch_shapes=[pltpu.VMEM((S + PAD, tc), F32), pltpu.VMEM((S + PAD, tc), F32)],
        semantics=("parallel",),
        operands=(dfin, up, gpre, w, b),
        side=side,
    )


def _adamw_update(w, g, m, v):
    m = ADAM_B1 * m + (1.0 - ADAM_B1) * g
    v = ADAM_B2 * v + (1.0 - ADAM_B2) * (g * g)
    m_hat = m / (1.0 - ADAM_B1 ** ADAM_STEP)
    v_hat = v / (1.0 - ADAM_B2 ** ADAM_STEP)
    delta = -ADAM_LR * (m_hat / (jnp.sqrt(v_hat) + ADAM_EPS) + ADAM_WD * w)
    return delta, m, v


def _add_pairs(send, pair, far_index, *, name):
    _, r_dim, c_dim = send.shape
    tr = r_dim // 4

    def body(far_ref, mine_ref, theirs_ref, o_ref):
        o_ref[...] = (mine_ref[...].astype(F32) + theirs_ref[...].astype(F32)).astype(o_ref.dtype)

    return pl.pallas_call(
        body,
        name=name,
        grid_spec=pltpu.PrefetchScalarGridSpec(
            num_scalar_prefetch=1,
            grid=(3, r_dim // tr),
            in_specs=[pl.BlockSpec((None, tr, c_dim), lambda j, i, far: (far[j], i, 0)),
                      pl.BlockSpec((None, tr, c_dim), lambda j, i, far: (1 + j, i, 0))],
            out_specs=pl.BlockSpec((None, tr, c_dim), lambda j, i, far: (j, i, 0)),
        ),
        out_shape=jax.ShapeDtypeStruct((3, r_dim, c_dim), BF16),
        compiler_params=_cparams("parallel", "parallel"),
    )(far_index, send, pair)


def _reduce_adamw(w, m, v, g_own, pair, far, *, tr, name):
    r_dim, c_dim = w.shape

    def body(w_ref, m_ref, v_ref, g_ref, pair_ref, far_ref, grad_ref, delta_ref, nm_ref, nv_ref):
        g = g_ref[...] + pair_ref[...].astype(F32)
        for j in range(3):
            g = g + far_ref[j].astype(F32)
        delta, nm, nv = _adamw_update(w_ref[...], g, m_ref[...], v_ref[...])
        grad_ref[...] = g
        delta_ref[...] = delta
        nm_ref[...] = nm
        nv_ref[...] = nv

    tile = pl.BlockSpec((tr, c_dim), lambda i: (i, 0))
    return _call(
        body,
        name=name,
        grid=(r_dim // tr,),
        in_specs=[tile, tile, tile, tile, pl.BlockSpec((None, tr, c_dim), lambda i: (0, i, 0)),
                  pl.BlockSpec((3, tr, c_dim), lambda i: (0, i, 0))],
        out_specs=[tile] * 4,
        out_shape=[jax.ShapeDtypeStruct((r_dim, c_dim), F32)] * 4,
        semantics=("parallel",),
        operands=(w, m, v, g_own, pair, far),
    )


def _adamw_many(ws, ms, vs, gs):
    n = len(ws)

    def body(*refs):
        for i in range(n):
            delta, nm, nv = _adamw_update(refs[i][...], refs[3 * n + i][...], refs[n + i][...], refs[2 * n + i][...])
            refs[4 * n + i][...] = delta
            refs[5 * n + i][...] = nm
            refs[6 * n + i][...] = nv

    vmem = pl.BlockSpec(memory_space=pltpu.VMEM)
    res = pl.pallas_call(
        body,
        name="adamw_small",
        in_specs=[vmem] * (4 * n),
        out_specs=[vmem] * (3 * n),
        out_shape=[jax.ShapeDtypeStruct(w.shape, F32) for w in ws] * 3,
        compiler_params=pltpu.CompilerParams(vmem_limit_bytes=VMEM_LIMIT),
    )(*ws, *ms, *vs, *gs)
    return res[:n], res[n:2 * n], res[2 * n:]


def _adamw_blocks(w, m, v, g, *, name, side=None):
    per = 2

    def body(w_ref, m_ref, v_ref, g_ref, delta_ref, nm_ref, nv_ref):
        delta, nm, nv = _adamw_update(w_ref[...], g_ref[...], m_ref[...], v_ref[...])
        delta_ref[...] = delta
        nm_ref[...] = nm
        nv_ref[...] = nv

    tile = pl.BlockSpec((1, per) + w.shape[2:], lambda i: (0, i, 0, 0))
    return _call(
        body,
        name=name,
        grid=(w.shape[1] // per,),
        in_specs=[tile] * 4,
        out_specs=[tile] * 3,
        out_shape=[jax.ShapeDtypeStruct(w.shape, F32)] * 3,
        semantics=("parallel",),
        operands=(w, m, v, g),
        side=side,
    )


def _coords():
    return lax.axis_index("x"), lax.axis_index("y"), lax.axis_index("c")


def _flip(coord, bit):
    return 1 - coord if bit else coord


def _relative(k):
    x, y, c = _coords()
    return _flip(x, k & 4), _flip(y, k & 2), _flip(c, k & 1)


def _index(pos):
    return 4 * pos[0] + 2 * pos[1] + pos[2]


FAR = (4, 2, 6)
AG_US_PER_MB = 44.0
RS_US_PER_MB = 45.0
ROW_ALIGN = 16


def _chunks(items, cursor, us, us_per_mb, through=None):
    budget = float("inf") if us is None else us / us_per_mb * 2 ** 20
    names = list(items)
    if through is not None:
        names = names[:names.index(through) + 1]
    chunks = []
    for name in names:
        arr = items[name]
        r_dim, c_dim = arr.shape[-2:]
        row_bytes = c_dim * arr.dtype.itemsize
        while cursor[name] < r_dim and budget > 0:
            rows = r_dim - cursor[name]
            if r_dim > ROW_ALIGN and budget < rows * row_bytes:
                rows = min(rows, max(ROW_ALIGN, int(budget // row_bytes) // ROW_ALIGN * ROW_ALIGN))
            chunks.append((name, cursor[name], rows))
            cursor[name] += rows
            budget -= rows * row_bytes
    return chunks


class _Gather:
    def __init__(self, shards):
        self.shards = dict(shards)
        self.bufs = {n: None for n in self.shards}
        self.cursor = {n: 0 for n in self.shards}

    def take(self, us=None, through=None):
        chunks = _chunks(self.shards, self.cursor, us, AG_US_PER_MB, through)
        return _GatherSide(self, chunks) if chunks else None

    def get(self, name):
        chunks = _chunks(self.shards, self.cursor, None, AG_US_PER_MB, through=name)
        if chunks:
            _run_side(_GatherSide(self, chunks), "gather_" + name)
        return self.bufs[name]


class _GatherSide:
    def __init__(self, owner, chunks):
        self.owner, self.chunks = owner, chunks
        self.names = list(dict.fromkeys(n for n, _, _ in chunks))
        old = [n for n in self.names if owner.bufs[n] is not None]
        self.operands = [owner.shards[n] for n in self.names] + [owner.bufs[n] for n in old]
        self.out_shape = [jax.ShapeDtypeStruct((N_DEV,) + owner.shards[n].shape, owner.shards[n].dtype)
                          for n in self.names]
        self.aliases = {len(self.names) + i: self.names.index(n) for i, n in enumerate(old)}
        self.sems = [pltpu.SemaphoreType.DMA((7 * len(chunks),)), pltpu.SemaphoreType.DMA((7 * len(chunks),)),
                     pltpu.SemaphoreType.DMA((len(chunks),))]

    def _copy(self, ins, outs, sems, ci, k, block, to, from_shard=False):
        name, r0, rows = self.chunks[ci]
        w = self.names.index(name)
        slot = outs[w].at[_index(block), pl.ds(r0, rows)]
        return pltpu.make_async_remote_copy(
            src_ref=ins[w].at[pl.ds(r0, rows)] if from_shard else slot, dst_ref=slot,
            send_sem=sems[0].at[7 * ci + k], recv_sem=sems[1].at[7 * ci + k], device_id=to, device_id_type=MESH)

    def _own(self, ins, outs, sems, ci):
        name, r0, rows = self.chunks[ci]
        w = self.names.index(name)
        return pltpu.make_async_copy(ins[w].at[pl.ds(r0, rows)], outs[w].at[_index(_relative(0)), pl.ds(r0, rows)],
                                     sems[2].at[ci])

    def start(self, ins, outs, sems):
        me, sibling = _relative(0), _relative(1)
        for ci in range(len(self.chunks)):
            self._own(ins, outs, sems, ci).start()
        for j, k in enumerate(FAR):
            for ci in range(len(self.chunks)):
                self._copy(ins, outs, sems, ci, 1 + j, me, _relative(k), from_shard=True).start()
        for ci in range(len(self.chunks)):
            self._copy(ins, outs, sems, ci, 0, me, sibling, from_shard=True).start()

    def finish(self, ins, outs, sems):
        me, sibling = _relative(0), _relative(1)
        n = len(self.chunks)
        for j, k in enumerate(FAR):
            for ci in range(n):
                self._copy(ins, outs, sems, ci, 1 + j, _relative(k), me).wait_recv()
                self._copy(ins, outs, sems, ci, 4 + j, _relative(k), sibling).start()
        for ci in range(n):
            self._copy(ins, outs, sems, ci, 0, sibling, me).wait_recv()
        for j, k in enumerate(FAR):
            for ci in range(n):
                self._copy(ins, outs, sems, ci, 4 + j, _relative(k | 1), me).wait_recv()
        for ci in range(n):
            self._copy(ins, outs, sems, ci, 0, me, sibling, from_shard=True).wait_send()
            for j, k in enumerate(FAR):
                self._copy(ins, outs, sems, ci, 1 + j, me, _relative(k), from_shard=True).wait_send()
                self._copy(ins, outs, sems, ci, 4 + j, _relative(k), sibling).wait_send()
            self._own(ins, outs, sems, ci).wait()

    def done(self, results):
        for n, buf in zip(self.names, results):
            self.owner.bufs[n] = buf


class _Scatter:
    def __init__(self, me, far_index):
        self.me, self.far_index = me, far_index
        self.sends, self.owns, self.pairs, self.sums, self.fars = {}, {}, {}, {}, {}
        self.pair_cursor, self.far_cursor = {}, {}

    def add(self, name, send, own):
        self.sends[name] = send
        self.owns[name] = own
        self.pairs[name] = self.fars[name] = None
        self.pair_cursor[name] = 0

    def _rows(self, name):
        return self.sends[name].shape[1]

    def _add_ready_pairs(self):
        for name in self.sends:
            if name not in self.sums and self.pair_cursor[name] == self._rows(name):
                self.sums[name] = _add_pairs(self.sends[name], self.pairs[name], self.far_index, name="pair_" + name)
                self.far_cursor[name] = 0

    def _side(self, us, through=None):
        self._add_ready_pairs()
        names = list(self.sends)
        if through is not None:
            names = names[:names.index(through) + 1]
        pair_chunks = [(n, self.pair_cursor[n], self._rows(n) - self.pair_cursor[n]) for n in names
                       if self.pair_cursor[n] < self._rows(n)]
        for n, _, _ in pair_chunks:
            self.pair_cursor[n] = self._rows(n)
        far_chunks = _chunks(self.sums, self.far_cursor, us, RS_US_PER_MB,
                             through if through in self.sums else None) if self.sums else []
        return _ScatterSide(self, pair_chunks, far_chunks) if pair_chunks or far_chunks else None

    def add_blocks(self, name, blocks32, blocks16):
        self.add(name, blocks16, lax.dynamic_index_in_dim(blocks32, self.me, axis=0, keepdims=False))

    def add_cols(self, name, full32, full16):
        width = full32.shape[1] // N_DEV
        self.add(name, _blocks(full16, "cols"), lax.dynamic_slice_in_dim(full32, self.me * width, width, axis=1))

    def take(self, us):
        return self._side(us)

    def get(self, name):
        step = 0
        while name not in self.sums or self.far_cursor[name] < self._rows(name):
            _run_side(self._side(None, through=name), "scatter_%s_%d" % (name, step))
            step += 1
        return self.owns[name], self.pairs[name], self.fars[name]


class _ScatterSide:
    TO_SIBLING = (1, 5, 3, 7)

    def __init__(self, owner, pair_chunks, far_chunks):
        self.owner, self.pair_chunks, self.far_chunks = owner, pair_chunks, far_chunks
        self.pair_names = list(dict.fromkeys(n for n, _, _ in pair_chunks))
        self.far_names = list(dict.fromkeys(n for n, _, _ in far_chunks))
        ins = [(owner.sends[n], owner.pairs[n], (4,)) for n in self.pair_names]
        ins += [(owner.sums[n], owner.fars[n], (3,)) for n in self.far_names]
        old = [i for i, (_, buf, _) in enumerate(ins) if buf is not None]
        self.operands = [src for src, _, _ in ins] + [ins[i][1] for i in old]
        self.out_shape = [jax.ShapeDtypeStruct(slots + src.shape[1:], BF16) for src, _, slots in ins]
        self.aliases = {len(ins) + j: i for j, i in enumerate(old)}
        n_pair, n_far = 4 * len(pair_chunks), 3 * len(far_chunks)
        self.sems = [pltpu.SemaphoreType.DMA((max(n_pair, 1),)), pltpu.SemaphoreType.DMA((max(n_pair, 1),)),
                     pltpu.SemaphoreType.DMA((max(n_far, 1),)), pltpu.SemaphoreType.DMA((max(n_far, 1),))]

    def _copies(self, ins, outs, sems):
        copies = []
        for ci, (name, r0, rows) in enumerate(self.pair_chunks):
            w = self.pair_names.index(name)
            for j, k in enumerate(self.TO_SIBLING):
                copies.append(pltpu.make_async_remote_copy(
                    src_ref=ins[w].at[_index(_relative(k)), pl.ds(r0, rows)], dst_ref=outs[w].at[j, pl.ds(r0, rows)],
                    send_sem=sems[0].at[4 * ci + j], recv_sem=sems[1].at[4 * ci + j],
                    device_id=_relative(1), device_id_type=MESH))
        for ci, (name, r0, rows) in enumerate(self.far_chunks):
            w = len(self.pair_names) + self.far_names.index(name)
            for j, k in enumerate(FAR):
                copies.append(pltpu.make_async_remote_copy(
                    src_ref=ins[w].at[j, pl.ds(r0, rows)], dst_ref=outs[w].at[j, pl.ds(r0, rows)],
                    send_sem=sems[2].at[3 * ci + j], recv_sem=sems[3].at[3 * ci + j],
                    device_id=_relative(k), device_id_type=MESH))
        return copies

    def start(self, ins, outs, sems):
        for cp in self._copies(ins, outs, sems):
            cp.start()

    def finish(self, ins, outs, sems):
        for cp in self._copies(ins, outs, sems):
            cp.wait()

    def done(self, results):
        for n, buf in zip(self.pair_names, results):
            self.owner.pairs[n] = buf
        for n, buf in zip(self.far_names, results[len(self.pair_names):]):
            self.owner.fars[n] = buf


def _all_reduce_small(vecs, mats):
    nv, nm = len(vecs), len(mats)
    row0, rows = [], 0
    for v in vecs:
        row0.append(rows)
        rows += v.shape[0]
    rows = -(-rows // 8) * 8
    width = max(v.shape[1] for v in vecs)
    per = mats[0].shape[0] // N_DEV

    def body(*refs):
        v_in, m_in = refs[:nv], refs[nv:nv + nm]
        v_out, m_out = refs[nv + nm:2 * nv + nm], refs[2 * nv + nm:2 * (nv + nm)]
        mine, inbox, total, stage, v_send, v_recv, a_send, a_recv, b_send, b_recv = refs[2 * (nv + nm):]
        me = _index(_relative(0))

        def vec_copy(k, sender):
            return pltpu.make_async_remote_copy(
                src_ref=mine, dst_ref=inbox.at[_index(sender)],
                send_sem=v_send.at[k - 1], recv_sem=v_recv.at[k - 1], device_id=_relative(k), device_id_type=MESH)

        def part(ref, pos):
            return ref.at[pl.ds(per * _index(pos), per)]

        def to_owner(k, j):
            return pltpu.make_async_remote_copy(
                src_ref=part(m_in[j], _relative(k)), dst_ref=stage.at[k, j],
                send_sem=a_send.at[nm * (k - 1) + j], recv_sem=a_recv.at[nm * (k - 1) + j],
                device_id=_relative(k), device_id_type=MESH)

        def from_owner(k, j, owner):
            return pltpu.make_async_remote_copy(
                src_ref=part(m_out[j], owner), dst_ref=part(m_out[j], owner),
                send_sem=b_send.at[nm * (k - 1) + j], recv_sem=b_recv.at[nm * (k - 1) + j],
                device_id=_relative(k), device_id_type=MESH)

        mine[...] = jnp.zeros_like(mine)
        for v, r0 in zip(v_in, row0):
            mine[r0:r0 + v.shape[0], 0:v.shape[1]] = v[...]
        for k in range(1, N_DEV):
            vec_copy(k, _relative(0)).start()
            for j in range(nm):
                to_owner(k, j).start()
        inbox[me] = mine[...]

        for j in range(nm):
            acc = m_in[j][pl.ds(per * me, per)]
            for k in range(1, N_DEV):
                to_owner(k, j).wait_recv()
                acc = acc + stage[k, j]
            m_out[j][pl.ds(per * me, per)] = acc
            for k in range(1, N_DEV):
                from_owner(k, j, _relative(0)).start()

        for k in range(1, N_DEV):
            vec_copy(k, _relative(k)).wait_recv()
        acc = inbox[0]
        for d in range(1, N_DEV):
            acc = acc + inbox[d]
        total[...] = acc
        for v, r0 in zip(v_out, row0):
            v[...] = total[r0:r0 + v.shape[0], 0:v.shape[1]]

        for k in range(1, N_DEV):
            for j in range(nm):
                from_owner(k, j, _relative(k)).wait_recv()
        for k in range(1, N_DEV):
            vec_copy(k, _relative(0)).wait_send()
            for j in range(nm):
                to_owner(k, j).wait_send()
                from_owner(k, j, _relative(0)).wait_send()

    vmem = pl.BlockSpec(memory_space=pltpu.VMEM)
    res = pl.pallas_call(
        body,
        name="all_reduce_small",
        in_specs=[vmem] * (nv + nm),
        out_specs=[vmem] * (nv + nm),
        out_shape=[jax.ShapeDtypeStruct(a.shape, F32) for a in (*vecs, *mats)],
        scratch_shapes=[pltpu.VMEM((rows, width), F32), pltpu.VMEM((N_DEV, rows, width), F32),
                        pltpu.VMEM((rows, width), F32), pltpu.VMEM((N_DEV, nm, per) + mats[0].shape[1:], F32),
                        pltpu.SemaphoreType.DMA((N_DEV - 1,)), pltpu.SemaphoreType.DMA((N_DEV - 1,))]
        + [pltpu.SemaphoreType.DMA((nm * (N_DEV - 1),))] * 4,
        compiler_params=pltpu.CompilerParams(vmem_limit_bytes=VMEM_LIMIT),
    )(*vecs, *mats)
    return res[:nv], res[nv:]


def _block_diag(w):
    w4 = w.reshape(N_RNN_GROUPS, 4, RNN_BLOCK_W, RNN_BLOCK_W)
    eye = jnp.eye(4, dtype=w.dtype)
    return (w4[:, :, :, None, :] * eye[None, :, None, :, None]).reshape(N_RNN_GROUPS, RNN_GROUP, RNN_GROUP)


def _diag_blocks(wg):
    w5 = wg.reshape(N_RNN_GROUPS, 4, RNN_BLOCK_W, 4, RNN_BLOCK_W)
    return jnp.stack([w5[:, b, :, b, :] for b in range(4)], axis=1).reshape(16, RNN_BLOCK_W, RNN_BLOCK_W)


def _heads_major(t, n_heads):
    return t.reshape(S, n_heads, HEAD_DIM).transpose(1, 0, 2)


def _heads_minor(t):
    return t.transpose(1, 0, 2).reshape(S, t.shape[0] * HEAD_DIM)


def _natural(gathered, how):
    n, r, c = gathered.shape
    if how == "rows":
        return gathered.reshape(n * r, c)
    return gathered.transpose(1, 0, 2).reshape(r, n * c)


def _blocks(full, how):
    if how == "rows":
        return full.reshape(N_DEV, full.shape[0] // N_DEV, full.shape[1])
    return full.reshape(full.shape[0], N_DEV, full.shape[1] // N_DEV).transpose(1, 0, 2)


def _forward_backward(x2, target, small, gather, scatter):
    xb = x2.astype(MXU_DTYPE)
    w_in = _natural(gather.get("w_in"), "cols")
    proj, projb = _mm(xb, w_in, tm=1024, tn=512, tk=D, out_dtype=(F32, MXU_DTYPE), name="proj", side=gather.take(110))

    qt = projb[:, :OFF_K].T.reshape(N_KV, GROUP, HEAD_DIM, S)
    k2, v2 = projb[:, OFF_K:OFF_V], projb[:, OFF_V:OFF_RX]
    kp = jnp.pad(_heads_major(k2, N_KV), ((0, 0), (BLOCK, 0), (0, 0)))
    vp = jnp.pad(_heads_major(v2, N_KV), ((0, 0), (BLOCK, 0), (0, 0)))
    kt = jnp.pad(k2.T.reshape(N_KV, HEAD_DIM, S), ((0, 0), (0, 0), (BLOCK, 0)))
    vt = jnp.pad(v2.T.reshape(N_KV, HEAD_DIM, S), ((0, 0), (0, 0), (BLOCK, 0)))
    sink_row = jnp.repeat(small["attn_sinks"].reshape(N_KV, 1, GROUP), BLOCK, axis=2)
    ot = _attn_fwd(qt, kp, vt, sink_row, side=gather.take(120)).reshape(D, S)

    rconv_w = _natural(gather.get("rnn_conv_w"), "cols")
    rxc = _conv_fwd(proj, OFF_RX, rconv_w, small["rnn_conv_b"], tc=512, name="rnn_conv_fwd", side=gather.take(18))
    r, i = _lru_gates(rxc, small["lru_wa"], small["lru_wi"], small["lru_ba"], small["lru_bi"], side=gather.take(33))
    h, yrin = _lru_scan_fwd(r, i, rxc, proj, small["lru_lambda"], side=gather.take(53))

    w_ap = _natural(gather.get("w_attn_proj"), "rows")
    w_rp = _natural(gather.get("w_rnn_proj"), "rows")
    y_attn = _mm(ot, w_ap, ta=True, tm=1024, tn=1024, tk=D, name="attn_proj", side=gather.take(22))
    y_rnn = _mm(yrin, w_rp, tm=1024, tn=1024, tk=D_RNN, name="rnn_proj", side=gather.take(27))
    mixin = _gate_fwd(y_attn, y_rnn, proj, small["b_gate"], side=gather.take(25))
    w_out = _natural(gather.get("w_out"), "rows")
    mix = _mm(mixin, w_out, tm=1024, tn=1024, tk=D, name="mix_out", side=gather.take(22))
    x1, x1b, xhat1, rstd1 = _ln_fwd(x2, mix, small["ln1_g"], small["ln1_b"], side=gather.take(23))

    w_up = gather.get("ffn_w_up")
    up = _mm(x1b, w_up, tm=1024, tn=768, tk=D, b_block=768, name="ffn_up", side=gather.take(58))
    w_gate = gather.get("ffn_w_gate")
    gpre = _mm(x1b, w_gate, tm=1024, tn=768, tk=D, b_block=768, name="ffn_gate", side=gather.take(58))
    fconv_w = _natural(gather.get("ffn_conv_w"), "cols")
    fin = _ffn_act_fwd(up, gpre, fconv_w, small["ffn_conv_b"], side=gather.take())
    w_down = _natural(gather.get("ffn_w_down"), "rows")
    f = _mm(fin, w_down, tm=1024, tn=1024, tk=2048, name="ffn_down")
    loss, dpre2, dpre2b, d_ln2_g, d_ln2_b = _ln_loss_bwd(x1, f, small["ln2_g"], small["ln2_b"], target)

    grads = {"ln2_g": d_ln2_g, "ln2_b": d_ln2_b}
    both = (F32, BF16)
    g32, g16 = _mm(fin, dpre2b, ta=True, tm=1024, tn=1024, tk=S, out_dtype=both, name="d_ffn_w_down")
    scatter.add_blocks("ffn_w_down", _blocks(g32, "rows"), _blocks(g16, "rows"))
    dfin = _mm(dpre2b, w_down, tb=True, tm=1024, tn=1024, tk=D, name="d_fin", side=scatter.take(57))
    dup, dgpre, grads["ffn_conv_w"], grads["ffn_conv_b"] = _ffn_act_bwd(
        dfin, up, gpre, fconv_w, small["ffn_conv_b"], side=scatter.take(85))
    g32, g16 = _mm(x1b, dup, ta=True, tm=1024, tn=768, tk=S, out_dtype=both, out_block=768, name="d_ffn_w_up",
                   side=scatter.take(57))
    scatter.add_blocks("ffn_w_up", g32, g16)
    g32, g16 = _mm(x1b, dgpre, ta=True, tm=1024, tn=768, tk=S, out_dtype=both, out_block=768, name="d_ffn_w_gate",
                   side=scatter.take(56))
    scatter.add_blocks("ffn_w_gate", g32, g16)
    dx1 = _mm(dup, w_up, tb=True, tm=1024, tn=1024, tk=768, b_block=768, name="d_x1_up", side=scatter.take(68))
    dx1 = _mm(dgpre, w_gate, tb=True, tm=1024, tn=1024, tk=768, b_block=768, add=dx1, name="d_x1_gate",
              side=scatter.take(70))
    dpre1, dpre1b, grads["ln1_g"], grads["ln1_b"] = _ln_bwd(dx1, dpre2, xhat1, rstd1, small["ln1_g"],
                                                            side=scatter.take(24))

    g32, g16 = _mm(mixin, dpre1b, ta=True, tm=1024, tn=1024, tk=S, out_dtype=both, name="d_w_out",
                   side=scatter.take(26))
    scatter.add_blocks("w_out", _blocks(g32, "rows"), _blocks(g16, "rows"))
    dmix = _mm(dpre1b, w_out, tb=True, tm=1024, tn=1024, tk=D, name="d_mixin", side=scatter.take(22))
    dya, dyr, dgl_a, dgl_r, db_a, db_r = _gate_bwd(dmix, y_attn, y_rnn, proj, small["b_gate"], side=scatter.take(36))
    grads["b_gate"] = jnp.concatenate([db_a, db_r], axis=1)
    g32, g16 = _mm(ot, dya, tm=1024, tn=1024, tk=S, out_dtype=both, name="d_w_attn_proj", side=scatter.take(38))
    scatter.add_blocks("w_attn_proj", _blocks(g32, "rows"), _blocks(g16, "rows"))
    g32, g16 = _mm(yrin, dyr, ta=True, tm=1280, tn=1024, tk=S, out_dtype=both, name="d_w_rnn_proj",
                   side=scatter.take(27))
    scatter.add_blocks("w_rnn_proj", _blocks(g32, "rows"), _blocks(g16, "rows"))
    dot_ = _mm(w_ap, dya, tb=True, tm=1024, tn=1024, tk=D, out_dtype=MXU_DTYPE, name="d_o", side=scatter.take(22))
    dyrin = _mm(dyr, w_rp, tb=True, tm=1024, tn=1280, tk=D, name="d_yrin", side=scatter.take(27))

    dry, dzr, dzi, drxc_in, grads["lru_ba"], grads["lru_bi"], grads["lru_lambda"] = _lru_scan_bwd(
        dyrin, proj, h, r, i, rxc, small["lru_lambda"], side=scatter.take(94))
    grads["lru_wa"], grads["lru_wi"] = _lru_gate_wgrad(rxc, dzr, dzi, side=scatter.take(22))
    drxc = _lru_gate_xgrad(dzr, dzi, small["lru_wa"], small["lru_wi"], drxc_in, side=scatter.take(33))
    drx, grads["rnn_conv_w"], grads["rnn_conv_b"] = _conv_bwd(drxc, proj, OFF_RX, rconv_w, tc=512,
                                                             name="rnn_conv_bwd", side=scatter.take(29))

    dqt, dk, dv, dsink = _attn_bwd(qt, kp, kt, vp, sink_row, dot_.reshape(N_KV, GROUP, HEAD_DIM, S),
                                   side=scatter.take(100))
    grads["attn_sinks"] = dsink.reshape(1, N_KV * GROUP)
    dproj = jnp.concatenate([
        dqt.reshape(D, S).T,
        _heads_minor(dk[:, BLOCK:, :]).astype(MXU_DTYPE),
        _heads_minor(dv[:, BLOCK:, :]).astype(MXU_DTYPE),
        drx, dry, dgl_a, dgl_r], axis=1)
    g32, g16 = _mm(xb, dproj, ta=True, tm=1024, tn=512, tk=S, out_dtype=both, name="d_w_in", side=scatter.take(110))
    scatter.add_cols("w_in", g32, g16)
    dx = _mm(dproj, w_in, tb=True, tm=1024, tn=1024, tk=512, add=dpre1, add_scale=ALPHA, name="d_x",
             side=scatter.take(150))
    return loss, dx, grads


SHARDED = (
    ("w_in", "cols", 128), ("w_attn_proj", "rows", 128), ("w_rnn_proj", "rows", 160), ("w_out", "rows", 128),
    ("ffn_w_up", "cols", 256), ("ffn_w_gate", "cols", 256), ("ffn_w_down", "rows", 256),
)
SMALL_REPLICATED = ("b_gate", "rnn_conv_b", "lru_wa", "lru_ba", "lru_wi", "lru_bi", "lru_lambda", "attn_sinks",
                    "ln1_g", "ln1_b", "ffn_conv_b", "ln2_g", "ln2_b")
SMALL_SHARDED = ("rnn_conv_w", "ffn_conv_w")
SMALL_MATS = ("lru_wa", "lru_wi")
WEIGHTS = ("w_in", "b_gate", "rnn_conv_w", "rnn_conv_b", "lru_wa", "lru_ba", "lru_wi", "lru_bi", "lru_lambda",
           "attn_sinks", "w_attn_proj", "w_rnn_proj", "w_out", "ln1_g", "ln1_b", "ffn_w_up", "ffn_w_gate",
           "ffn_conv_w", "ffn_conv_b", "ffn_w_down", "ln2_g", "ln2_b")


def kernel(x, w_in, b_gate, rnn_conv_w, rnn_conv_b, lru_wa, lru_ba, lru_wi, lru_bi, lru_lambda, attn_sinks, w_attn_proj, w_rnn_proj, w_out, ln1_g, ln1_b, ffn_w_up, ffn_w_gate, ffn_conv_w, ffn_conv_b, ffn_w_down, ln2_g, ln2_b, loss_target, m_w_in, m_b_gate, m_rnn_conv_w, m_rnn_conv_b, m_lru_wa, m_lru_ba, m_lru_wi, m_lru_bi, m_lru_lambda, m_attn_sinks, m_w_attn_proj, m_w_rnn_proj, m_w_out, m_ln1_g, m_ln1_b, m_ffn_w_up, m_ffn_w_gate, m_ffn_conv_w, m_ffn_conv_b, m_ffn_w_down, m_ln2_g, m_ln2_b, v_w_in, v_b_gate, v_rnn_conv_w, v_rnn_conv_b, v_lru_wa, v_lru_ba, v_lru_wi, v_lru_bi, v_lru_lambda, v_attn_sinks, v_w_attn_proj, v_w_rnn_proj, v_w_out, v_ln1_g, v_ln1_b, v_ffn_w_up, v_ffn_w_gate, v_ffn_conv_w, v_ffn_conv_b, v_ffn_w_down, v_ln2_g, v_ln2_b):
    given = dict(locals())
    wsh = {n: given[n][0] for n in WEIGHTS}
    msh = {n: given["m_" + n][0] for n in WEIGHTS}
    vsh = {n: given["v_" + n][0] for n in WEIGHTS}
    m_given = {n: given["m_" + n] for n in WEIGHTS}
    v_given = {n: given["v_" + n] for n in WEIGHTS}
    me = 4 * lax.axis_index("x") + 2 * lax.axis_index("y") + lax.axis_index("c")

    order = ("w_in", "rnn_conv_w", "ffn_conv_w", "w_attn_proj", "w_rnn_proj", "w_out", "ffn_w_up", "ffn_w_gate",
             "ffn_w_down")
    gather = _Gather({n: wsh[n] if n in SMALL_SHARDED else wsh[n].astype(MXU_DTYPE) for n in order})
    _run_side(gather.take(through="ffn_conv_w"), "gather_first")
    small = {n: given[n] for n in SMALL_REPLICATED}
    small["lru_wa"] = _block_diag(wsh["lru_wa"])
    small["lru_wi"] = _block_diag(wsh["lru_wi"])
    scatter = _Scatter(me, jnp.stack([_index(_relative(k)) for k in FAR]).astype(jnp.int32))

    loss, dx, grads = _forward_backward(x[0], loss_target[0], small, gather, scatter)
    grads["lru_wa"] = _diag_blocks(grads["lru_wa"])
    grads["lru_wi"] = _diag_blocks(grads["lru_wi"])

    vec_names = tuple(n for n in SMALL_REPLICATED if n not in SMALL_MATS) + SMALL_SHARDED
    sums, mat_sums = _all_reduce_small([loss] + [grads[n] for n in vec_names], [grads[n] for n in SMALL_MATS])
    loss_total = sums[0].reshape(())
    g_small = dict(zip(vec_names, sums[1:]))
    for n in SMALL_SHARDED:
        width = wsh[n].shape[1]
        g_small[n] = lax.dynamic_slice_in_dim(g_small[n], me * width, width, axis=1)
    g_small = {n: g_small[n].reshape(given[n].shape) for n in vec_names}
    out = {}
    results = _adamw_many(*[[d[n] for n in vec_names] for d in (given, m_given, v_given, g_small)])
    for n, delta, nm, nv in zip(vec_names, *results):
        out[n] = (g_small[n], delta, nm, nv)
    for n, g in zip(SMALL_MATS, mat_sums):
        g = g.reshape(given[n].shape)
        out[n] = (g, *_adamw_blocks(given[n], m_given[n], v_given[n], g, name="adamw_" + n))

    tile_rows = {n: tr for n, _, tr in SHARDED}
    for n in list(scatter.sends):
        own, pair, far = scatter.get(n)
        res = _reduce_adamw(wsh[n], msh[n], vsh[n], own, pair, far, tr=tile_rows[n], name="adamw_" + n)
        out[n] = tuple(r[None] for r in res)

    outputs = [loss_total, dx[None]]
    for kind in range(4):
        outputs += [out[n][kind] for n in WEIGHTS]
    return tuple(outputs)
```

```python
import math

import jax
import jax.numpy as jnp
from jax import lax
from jax.experimental import pallas as pl
from jax.experimental.pallas import tpu as pltpu

F32 = jnp.float32
BF16 = jnp.bfloat16
MXU_DTYPE = jnp.bfloat16

N_DEV = 8
S = 2048
D = 2048
HEAD_DIM = 64
N_KV = 4
GROUP = 8
BLOCK = 128
D_KV = N_KV * HEAD_DIM
D_RNN = 2560
RNN_GROUP = 640
N_RNN_GROUPS = D_RNN // RNN_GROUP
RNN_BLOCK_W = 160
RNN_CONV_W = 4
LRU_C = 8.0
D_FF = 6144
FFN_CONV_W = 3
D_IN = 11776
OFF_K = 2048
OFF_V = 2304
OFF_RX = 2560
OFF_RY = 5120
OFF_GA = 7680
OFF_GR = 9728
LN_EPS = 1e-5
ALPHA = 2.0 ** 0.25
ADAM_LR = 0.001
ADAM_B1 = 0.9
ADAM_B2 = 0.999
ADAM_EPS = 1e-08
ADAM_WD = 0.01
ADAM_STEP = 10
NEG = -1e30
VMEM_LIMIT = 56 * 1024 * 1024
MESH = pl.DeviceIdType.MESH
GELU_C = math.sqrt(2.0 / math.pi)


def _cparams(*sem):
    return pltpu.CompilerParams(dimension_semantics=sem or None, vmem_limit_bytes=VMEM_LIMIT)


def _call(body, *, name, grid, in_specs, out_specs, out_shape, operands, semantics, scratch_shapes=(), side=None):
    single = not isinstance(out_shape, (list, tuple))
    out_shape = [out_shape] if single else list(out_shape)
    out_specs = [out_specs] if single else list(out_specs)
    in_specs = list(in_specs)
    scratch_shapes = list(scratch_shapes)
    if side is None:
        res = pl.pallas_call(
            body, name=name, grid=grid, in_specs=in_specs, out_specs=out_specs, out_shape=out_shape,
            scratch_shapes=scratch_shapes, compiler_params=_cparams(*semantics))(*operands)
        return res[0] if single else res
    n_in, n_out, n_scr = len(in_specs), len(out_shape), len(scratch_shapes)
    s_in, s_out = len(side.operands), len(side.out_shape)
    hbm = pl.BlockSpec(memory_space=pltpu.HBM)

    def with_copies(*refs):
        core_in, side_in = refs[:n_in], refs[n_in:n_in + s_in]
        o0 = n_in + s_in
        core_out, side_out = refs[o0:o0 + n_out], refs[o0 + n_out:o0 + n_out + s_out]
        c0 = o0 + n_out + s_out
        core_scr, sems = refs[c0:c0 + n_scr], refs[c0 + n_scr:]
        first, last = None, None
        for d, size in enumerate(grid):
            at_start, at_end = pl.program_id(d) == 0, pl.program_id(d) == size - 1
            first = at_start if first is None else first & at_start
            last = at_end if last is None else last & at_end

        @pl.when(first)
        def _():
            side.start(side_in, side_out, sems)

        body(*core_in, *core_out, *core_scr)

        @pl.when(last)
        def _():
            side.finish(side_in, side_out, sems)

    res = pl.pallas_call(
        with_copies, name=name, grid=grid,
        in_specs=in_specs + [hbm] * s_in, out_specs=out_specs + [hbm] * s_out,
        out_shape=out_shape + list(side.out_shape),
        scratch_shapes=scratch_shapes + list(side.sems),
        input_output_aliases={n_in + i: n_out + o for i, o in side.aliases.items()},
        compiler_params=_cparams(*(("arbitrary",) * len(grid))))(*operands, *side.operands)
    side.done(res[n_out:])
    return res[0] if single else res[:n_out]


def _run_side(side, name):
    def body(*refs):
        s_in, s_out = len(side.operands), len(side.out_shape)
        side.start(refs[:s_in], refs[s_in:s_in + s_out], refs[s_in + s_out:])
        side.finish(refs[:s_in], refs[s_in:s_in + s_out], refs[s_in + s_out:])

    hbm = pl.BlockSpec(memory_space=pltpu.HBM)
    res = pl.pallas_call(
        body, name=name, in_specs=[hbm] * len(side.operands), out_specs=[hbm] * len(side.out_shape),
        out_shape=list(side.out_shape), scratch_shapes=list(side.sems),
        input_output_aliases=dict(side.aliases))(*side.operands)
    side.done(res)


def _gelu(x):
    x2 = x * x
    t = jnp.tanh(GELU_C * (x + 0.044715 * x * x2))
    g = 0.5 * x * (1.0 + t)
    dg = 0.5 * (1.0 + t) + 0.5 * x * (1.0 - t * t) * (GELU_C * (1.0 + 3.0 * 0.044715 * x2))
    return g, dg


def _sigmoid(x):
    return 1.0 / (1.0 + jnp.exp(-x))


def _softplus(x):
    z = jnp.exp(-jnp.abs(x))
    small = z * (1.0 - z * (0.5 - z * (1.0 / 3.0 - 0.25 * z)))
    return jnp.maximum(x, 0.0) + jnp.where(z < 0.02, small, jnp.log(1.0 + z))


def _one_minus_exp(x):
    series = -x * (1.0 + x * (0.5 + x * (1.0 / 6.0 + x * (1.0 / 24.0))))
    return jnp.where(x > -0.03, series, 1.0 - jnp.exp(x))


def _colsum(v):
    return jnp.sum(v, axis=0, keepdims=True)


def _mm(a, b, *, tm, tn, tk, name, ta=False, tb=False, out_dtype=F32, b_block=None, out_block=None, add=None,
        add_scale=1.0, side=None):
    out_dtypes = out_dtype if isinstance(out_dtype, tuple) else (out_dtype,)
    if ta:
        k_dim, m_dim = a.shape
    else:
        m_dim, k_dim = a.shape
    if b_block is None:
        n_dim = b.shape[0] if tb else b.shape[1]
    else:
        n_dim = b.shape[1] if tb else b.shape[0] * b_block
    assert m_dim % tm == 0 and n_dim % tn == 0 and k_dim % tk == 0, (name, m_dim, n_dim, k_dim)
    nk = k_dim // tk
    dims = (((0 if ta else 1,), (1 if tb else 0,)), ((), ()))
    has_add = add is not None

    def body(*refs):
        a_ref, b_ref = refs[0], refs[1]
        add_ref = refs[2] if has_add else None
        first_out = 3 if has_add else 2
        o_refs = refs[first_out:first_out + len(out_dtypes)]

        def product():
            return lax.dot_general(a_ref[...].astype(MXU_DTYPE), b_ref[...].astype(MXU_DTYPE), dims,
                                   preferred_element_type=F32)

        def finish(acc):
            if has_add:
                acc = acc + add_scale * add_ref[...]
            for o_ref in o_refs:
                o_ref[...] = acc.astype(o_ref.dtype)

        if nk == 1:
            finish(product())
        else:
            acc_ref = refs[-1]
            k = pl.program_id(2)

            @pl.when(k == 0)
            def _():
                acc_ref[...] = jnp.zeros_like(acc_ref)

            acc_ref[...] += product()

            @pl.when(k == nk - 1)
            def _():
                finish(acc_ref[...])

    if ta:
        a_spec = pl.BlockSpec((tk, tm), lambda i, j, k: (k, i))
    else:
        a_spec = pl.BlockSpec((tm, tk), lambda i, j, k: (i, k))
    if b_block is None:
        if tb:
            b_spec = pl.BlockSpec((tn, tk), lambda i, j, k: (j, k))
        else:
            b_spec = pl.BlockSpec((tk, tn), lambda i, j, k: (k, j))
    elif tb:
        assert b_block % tk == 0
        b_spec = pl.BlockSpec((None, tn, tk), lambda i, j, k: ((k * tk) // b_block, j, ((k * tk) % b_block) // tk))
    else:
        assert b_block % tn == 0
        b_spec = pl.BlockSpec((None, tk, tn), lambda i, j, k: ((j * tn) // b_block, k, ((j * tn) % b_block) // tn))
    in_specs = [a_spec, b_spec]
    operands = [a, b]
    if has_add:
        in_specs.append(pl.BlockSpec((tm, tn), lambda i, j, k: (i, j)))
        operands.append(add)
    if out_block is None:
        out_spec = pl.BlockSpec((tm, tn), lambda i, j, k: (i, j))
        out_dims = (m_dim, n_dim)
    else:
        assert out_block % tn == 0
        out_spec = pl.BlockSpec((None, tm, tn), lambda i, j, k: ((j * tn) // out_block, i, ((j * tn) % out_block) // tn))
        out_dims = (n_dim // out_block, m_dim, out_block)
    res = _call(
        body,
        name=name,
        grid=(m_dim // tm, n_dim // tn, nk),
        in_specs=in_specs,
        out_specs=[out_spec] * len(out_dtypes),
        out_shape=[jax.ShapeDtypeStruct(out_dims, dt) for dt in out_dtypes],
        scratch_shapes=[pltpu.VMEM((tm, tn), F32)] if nk > 1 else [],
        semantics=("parallel", "parallel", "arbitrary"),
        operands=tuple(operands),
        side=side,
    )
    return res if isinstance(out_dtype, tuple) else res[0]


def _attn_bias(bias_ref, h):
    key = lax.broadcasted_iota(jnp.int32, (2 * BLOCK, GROUP * BLOCK), 0)
    col = lax.broadcasted_iota(jnp.int32, (2 * BLOCK, GROUP * BLOCK), 1)
    dist = BLOCK + (col & (BLOCK - 1)) - key
    head = h * GROUP + (col >> 7) + 1
    slope = jnp.exp(head.astype(F32) * (-0.25 * math.log(2.0)))
    bias = jnp.where((dist >= 0) & (dist < BLOCK), -slope * dist.astype(F32), NEG)
    bias_ref[1] = bias
    bias_ref[0] = jnp.where(key < BLOCK, NEG, bias)


def _attn_probs(kb, qt, bias, sink):
    s = jnp.dot(kb, qt, preferred_element_type=F32) * (HEAD_DIM ** -0.5) + bias
    m = jnp.maximum(jnp.max(s, axis=0, keepdims=True), sink)
    e = jnp.exp(s - m)
    e_sink = jnp.exp(sink - m)
    inv = 1.0 / (jnp.sum(e, axis=0, keepdims=True) + e_sink)
    return e * inv, e_sink * inv


def _heads_on_lanes(ref, r0):
    return jnp.concatenate([ref[g, :, pl.ds(r0, BLOCK)] for g in range(GROUP)], axis=1)


def _attn_fwd(qt, kp, vt, sink_row, side=None):
    cols = GROUP * BLOCK

    def body(q_ref, k_ref, vt_ref, sink_ref, o_ref, bias_ref):
        _attn_bias(bias_ref, pl.program_id(0))
        sink = sink_ref[...]

        def step(n, carry):
            r0 = pl.multiple_of(n * BLOCK, BLOCK)
            p, _ = _attn_probs(k_ref[pl.ds(r0, 2 * BLOCK), :], _heads_on_lanes(q_ref, r0),
                               bias_ref[jnp.minimum(n, 1)], sink)
            o = jnp.dot(vt_ref[:, pl.ds(r0, 2 * BLOCK)], p.astype(MXU_DTYPE), preferred_element_type=F32)
            for g in range(GROUP):
                o_ref[g, :, pl.ds(r0, BLOCK)] = o[:, g * BLOCK:(g + 1) * BLOCK].astype(o_ref.dtype)
            return carry

        lax.fori_loop(0, S // BLOCK, step, 0)

    hm = pl.BlockSpec((None, GROUP, HEAD_DIM, S), lambda h: (h, 0, 0, 0))
    return _call(
        body,
        name="attn_fwd",
        grid=(N_KV,),
        in_specs=[
            hm,
            pl.BlockSpec((None, BLOCK + S, HEAD_DIM), lambda h: (h, 0, 0)),
            pl.BlockSpec((None, HEAD_DIM, BLOCK + S), lambda h: (h, 0, 0)),
            pl.BlockSpec((None, 1, cols), lambda h: (h, 0, 0)),
        ],
        out_specs=hm,
        out_shape=jax.ShapeDtypeStruct((N_KV, GROUP, HEAD_DIM, S), MXU_DTYPE),
        scratch_shapes=[pltpu.VMEM((2, 2 * BLOCK, cols), F32)],
        semantics=("parallel",),
        operands=(qt, kp, vt, sink_row),
        side=side,
    )


def _attn_bwd(qt, kp, kt, vp, sink_row, dot_, side=None):
    cols = GROUP * BLOCK

    def body(q_ref, k_ref, kt_ref, v_ref, sink_ref, do_ref, dq_ref, dk_ref, dv_ref, dsink_ref, bias_ref):
        _attn_bias(bias_ref, pl.program_id(0))
        sink = sink_ref[...]
        dk_ref[...] = jnp.zeros_like(dk_ref)
        dv_ref[...] = jnp.zeros_like(dv_ref)
        nt = (((1,), (1,)), ((), ()))

        def step(n, sink_acc):
            r0 = pl.multiple_of(n * BLOCK, BLOCK)
            band = pl.ds(r0, 2 * BLOCK)
            qn = _heads_on_lanes(q_ref, r0)
            don = _heads_on_lanes(do_ref, r0)
            p, p_sink = _attn_probs(k_ref[band, :], qn, bias_ref[jnp.minimum(n, 1)], sink)
            dp = jnp.dot(v_ref[band, :], don, preferred_element_type=F32)
            delta = jnp.sum(p * dp, axis=0, keepdims=True)
            ds = (p * (dp - delta) * (HEAD_DIM ** -0.5)).astype(MXU_DTYPE)
            dq = jnp.dot(kt_ref[:, band], ds, preferred_element_type=F32)
            for g in range(GROUP):
                dq_ref[g, :, pl.ds(r0, BLOCK)] = dq[:, g * BLOCK:(g + 1) * BLOCK].astype(dq_ref.dtype)
            dk_ref[band, :] += lax.dot_general(ds, qn, nt, preferred_element_type=F32)
            dv_ref[band, :] += lax.dot_general(p.astype(MXU_DTYPE), don, nt, preferred_element_type=F32)
            return sink_acc - p_sink * delta

        sink_acc = lax.fori_loop(0, S // BLOCK, step, jnp.zeros((1, cols), F32))
        for g in range(GROUP):
            dsink_ref[g:g + 1, :] = jnp.sum(sink_acc[:, g * BLOCK:(g + 1) * BLOCK], axis=1, keepdims=True)

    hm = pl.BlockSpec((None, GROUP, HEAD_DIM, S), lambda h: (h, 0, 0, 0))
    kv = pl.BlockSpec((None, BLOCK + S, HEAD_DIM), lambda h: (h, 0, 0))
    return _call(
        body,
        name="attn_bwd",
        grid=(N_KV,),
        in_specs=[hm, kv, pl.BlockSpec((None, HEAD_DIM, BLOCK + S), lambda h: (h, 0, 0)), kv,
                  pl.BlockSpec((None, 1, cols), lambda h: (h, 0, 0)), hm],
        out_specs=[hm, kv, kv, pl.BlockSpec((None, GROUP, 1), lambda h: (h, 0, 0))],
        out_shape=[
            jax.ShapeDtypeStruct((N_KV, GROUP, HEAD_DIM, S), MXU_DTYPE),
            jax.ShapeDtypeStruct((N_KV, BLOCK + S, HEAD_DIM), F32),
            jax.ShapeDtypeStruct((N_KV, BLOCK + S, HEAD_DIM), F32),
            jax.ShapeDtypeStruct((N_KV, GROUP, 1), F32),
        ],
        scratch_shapes=[pltpu.VMEM((2, 2 * BLOCK, cols), F32)],
        semantics=("parallel",),
        operands=(qt, kp, kt, vp, sink_row, dot_),
        side=side,
    )


PAD = 8
CHUNK = 256


def _past_taps(xpad_ref, r0, width):
    ext = xpad_ref[pl.ds(r0, CHUNK + PAD), :]
    taps = []
    for k in range(width):
        back = width - 1 - k
        taps.append((ext if back == 0 else pltpu.roll(ext, back, 0))[PAD:, :])
    return taps


def _future_taps(xpad_ref, r0, width):
    ext = xpad_ref[pl.ds(r0, CHUNK + PAD), :]
    taps = []
    for ahead in range(width):
        taps.append((ext if ahead == 0 else pltpu.roll(ext, CHUNK + PAD - ahead, 0))[:CHUNK, :])
    return taps


def _conv_fwd(src, col0, w, b, *, tc, name, side=None):
    width, c_dim = w.shape

    def body(x_ref, w_ref, b_ref, o_ref, xpad_ref):
        xpad_ref[pl.ds(0, PAD), :] = jnp.zeros((PAD, tc), F32)
        xpad_ref[pl.ds(PAD, S), :] = x_ref[...]
        wv = w_ref[...]
        bv = b_ref[...]

        def step(ci, carry):
            r0 = pl.multiple_of(ci * CHUNK, CHUNK)
            taps = _past_taps(xpad_ref, r0, width)
            y = bv + taps[0] * wv[0:1, :]
            for k in range(1, width):
                y = y + taps[k] * wv[k:k + 1, :]
            o_ref[pl.ds(r0, CHUNK), :] = y
            return carry

        lax.fori_loop(0, S // CHUNK, step, 0)

    return _call(
        body,
        name=name,
        grid=(c_dim // tc,),
        in_specs=[
            pl.BlockSpec((S, tc), lambda j: (0, col0 // tc + j)),
            pl.BlockSpec((width, tc), lambda j: (0, j)),
            pl.BlockSpec((1, tc), lambda j: (0, j)),
        ],
        out_specs=pl.BlockSpec((S, tc), lambda j: (0, j)),
        out_shape=jax.ShapeDtypeStruct((S, c_dim), F32),
        scratch_shapes=[pltpu.VMEM((S + PAD, tc), F32)],
        semantics=("parallel",),
        operands=(src, w, b),
        side=side,
    )


def _conv_bwd(dy, src, col0, w, *, tc, name, side=None):
    width, c_dim = w.shape

    def body(dy_ref, x_ref, w_ref, dx_ref, dw_ref, db_ref, xpad_ref, dpad_ref):
        xpad_ref[pl.ds(0, PAD), :] = jnp.zeros((PAD, tc), F32)
        xpad_ref[pl.ds(PAD, S), :] = x_ref[...]
        dpad_ref[pl.ds(0, S), :] = dy_ref[...]
        dpad_ref[pl.ds(S, PAD), :] = jnp.zeros((PAD, tc), F32)
        wv = w_ref[...]

        def step(ci, acc):
            r0 = pl.multiple_of(ci * CHUNK, CHUNK)
            past = _past_taps(xpad_ref, r0, width)
            ahead = _future_taps(dpad_ref, r0, width)
            d = ahead[0]
            dx = d * wv[width - 1:width, :]
            for j in range(1, width):
                dx = dx + ahead[j] * wv[width - 1 - j:width - j, :]
            dx_ref[pl.ds(r0, CHUNK), :] = dx.astype(dx_ref.dtype)
            return tuple(acc[k] + _colsum(past[k] * d) for k in range(width)) + (acc[width] + _colsum(d),)

        zero = jnp.zeros((1, tc), F32)
        acc = lax.fori_loop(0, S // CHUNK, step, (zero,) * (width + 1))
        for k in range(width):
            dw_ref[k:k + 1, :] = acc[k]
        db_ref[...] = acc[width]

    return _call(
        body,
        name=name,
        grid=(c_dim // tc,),
        in_specs=[
            pl.BlockSpec((S, tc), lambda j: (0, j)),
            pl.BlockSpec((S, tc), lambda j: (0, col0 // tc + j)),
            pl.BlockSpec((width, tc), lambda j: (0, j)),
        ],
        out_specs=[
            pl.BlockSpec((S, tc), lambda j: (0, j)),
            pl.BlockSpec((width, tc), lambda j: (0, j)),
            pl.BlockSpec((1, tc), lambda j: (0, j)),
        ],
        out_shape=[
            jax.ShapeDtypeStruct((S, c_dim), MXU_DTYPE),
            jax.ShapeDtypeStruct((width, c_dim), F32),
            jax.ShapeDtypeStruct((1, c_dim), F32),
        ],
        scratch_shapes=[pltpu.VMEM((S + PAD, tc), F32), pltpu.VMEM((S + PAD, tc), F32)],
        semantics=("parallel",),
        operands=(dy, src, w),
        side=side,
    )


SCAN_TC = 256


def _lru_gates(rxc, wa, wi, ba, bi, side=None):
    tm = 512

    def body(x_ref, wa_ref, wi_ref, ba_ref, bi_ref, r_ref, i_ref):
        xv = x_ref[...].astype(MXU_DTYPE)
        r_ref[...] = _sigmoid(jnp.dot(xv, wa_ref[...].astype(MXU_DTYPE), preferred_element_type=F32) + ba_ref[...])
        i_ref[...] = _sigmoid(jnp.dot(xv, wi_ref[...].astype(MXU_DTYPE), preferred_element_type=F32) + bi_ref[...])

    x_spec = pl.BlockSpec((tm, RNN_GROUP), lambda g, i: (i, g))
    w_spec = pl.BlockSpec((None, RNN_GROUP, RNN_GROUP), lambda g, i: (g, 0, 0))
    b_spec = pl.BlockSpec((1, RNN_GROUP), lambda g, i: (0, g))
    return _call(
        body,
        name="lru_gates",
        grid=(N_RNN_GROUPS, S // tm),
        in_specs=[x_spec, w_spec, w_spec, b_spec, b_spec],
        out_specs=[x_spec, x_spec],
        out_shape=[jax.ShapeDtypeStruct((S, D_RNN), F32)] * 2,
        semantics=("parallel", "parallel"),
        operands=(rxc, wa, wi, ba, bi),
        side=side,
    )


def _scan_down(a, u, row):
    for d in (1, 2, 4):
        a_s = jnp.where(row >= d, pltpu.roll(a, d, 0), 1.0)
        u_s = jnp.where(row >= d, pltpu.roll(u, d, 0), 0.0)
        u = a * u_s + u
        a = a * a_s
    return a, u


def _scan_up(a, u, row):
    for d in (1, 2, 4):
        a_s = jnp.where(row < 8 - d, pltpu.roll(a, 8 - d, 0), 1.0)
        u_s = jnp.where(row < 8 - d, pltpu.roll(u, 8 - d, 0), 0.0)
        u = a * u_s + u
        a = a * a_s
    return a, u


def _lru_scan_fwd(r, i, rxc, proj, lam, side=None):
    tc = SCAN_TC

    def body(r_ref, i_ref, x_ref, ry_ref, lam_ref, h_ref, y_ref):
        rate = LRU_C * _softplus(-lam_ref[...])
        row = lax.broadcasted_iota(jnp.int32, (8, tc), 0)

        def step(ci, carry):
            r0 = pl.multiple_of(ci * 16, 16)
            log_a = -rate * r_ref[pl.ds(r0, 16), :]
            a16 = jnp.exp(log_a)
            u16 = jnp.sqrt(_one_minus_exp(2.0 * log_a)) * (i_ref[pl.ds(r0, 16), :] * x_ref[pl.ds(r0, 16), :])
            hs = []
            for half in range(2):
                a_cum, h0 = _scan_down(a16[8 * half:8 * half + 8, :], u16[8 * half:8 * half + 8, :], row)
                h = a_cum * carry + h0
                carry = jnp.broadcast_to(h[7:8, :], (8, tc))
                hs.append(h)
            h16 = jnp.concatenate(hs, axis=0)
            h_ref[pl.ds(r0, 16), :] = h16
            y_ref[pl.ds(r0, 16), :] = (h16 * _gelu(ry_ref[pl.ds(r0, 16), :])[0]).astype(y_ref.dtype)
            return carry

        lax.fori_loop(0, S // 16, step, jnp.zeros((8, tc), F32))

    col = pl.BlockSpec((S, tc), lambda j: (0, j))
    return _call(
        body,
        name="lru_scan_fwd",
        grid=(D_RNN // tc,),
        in_specs=[col, col, col, pl.BlockSpec((S, tc), lambda j: (0, OFF_RY // tc + j)),
                  pl.BlockSpec((1, tc), lambda j: (0, j))],
        out_specs=[col, col],
        out_shape=[jax.ShapeDtypeStruct((S, D_RNN), F32), jax.ShapeDtypeStruct((S, D_RNN), MXU_DTYPE)],
        semantics=("parallel",),
        operands=(r, i, rxc, proj, lam),
        side=side,
    )


def _lru_scan_bwd(dy, proj, h, r, i, rxc, lam, side=None):
    tc = SCAN_TC

    def body(dy_ref, ry_ref, h_ref, r_ref, i_ref, x_ref, lam_ref,
             dry_ref, dzr_ref, dzi_ref, dx_ref, dba_ref, dbi_ref, dlam_ref, a_ref, dh_ref, hp_ref):
        lam_v = lam_ref[...]
        rate = LRU_C * _softplus(-lam_v)
        dlam_scale = LRU_C * _sigmoid(-lam_v)
        row = lax.broadcasted_iota(jnp.int32, (8, tc), 0)
        hp_ref[pl.ds(0, PAD), :] = jnp.zeros((PAD, tc), F32)
        hp_ref[pl.ds(PAD, S), :] = h_ref[...]
        a_ref[pl.ds(S, PAD), :] = jnp.zeros((PAD, tc), F32)

        def prep(ci, carry):
            r0 = pl.multiple_of(ci * CHUNK, CHUNK)
            a_ref[pl.ds(r0, CHUNK), :] = jnp.exp(-rate * r_ref[pl.ds(r0, CHUNK), :])
            ge, dge = _gelu(ry_ref[pl.ds(r0, CHUNK), :])
            dyv = dy_ref[pl.ds(r0, CHUNK), :]
            dh_ref[pl.ds(r0, CHUNK), :] = dyv * ge
            dry_ref[pl.ds(r0, CHUNK), :] = (dyv * h_ref[pl.ds(r0, CHUNK), :] * dge).astype(dry_ref.dtype)
            return carry

        lax.fori_loop(0, S // CHUNK, prep, 0)

        def step(ci, state):
            carry, dba, dbi, dlam = state
            r0 = pl.multiple_of(S - 16 - ci * 16, 16)
            a_ext = a_ref[pl.ds(r0, 24), :]
            a_next = pltpu.roll(a_ext, 23, 0)
            h_prev = pltpu.roll(hp_ref[pl.ds(r0, 24), :], 1, 0)
            dh16 = dh_ref[pl.ds(r0, 16), :]
            gs = [None, None]
            for half in (1, 0):
                lo = 8 * half
                c_cum, g0 = _scan_up(a_next[lo:lo + 8, :], dh16[lo:lo + 8, :], row)
                g = c_cum * carry + g0
                carry = jnp.broadcast_to(g[0:1, :], (8, tc))
                gs[half] = g
            g16 = jnp.concatenate(gs, axis=0)
            a16 = a_ext[0:16, :]
            r16 = r_ref[pl.ds(r0, 16), :]
            i16 = i_ref[pl.ds(r0, 16), :]
            x16 = x_ref[pl.ds(r0, 16), :]
            a2 = a16 * a16
            sq = jnp.sqrt(_one_minus_exp(-2.0 * rate * r16))
            dx_ref[pl.ds(r0, 16), :] = g16 * sq * i16
            dzi = g16 * sq * x16 * i16 * (1.0 - i16)
            dlog_a = g16 * h_prev[8:24, :] * a16 - g16 * i16 * x16 * a2 / sq
            dzr = -rate * dlog_a * r16 * (1.0 - r16)
            dzr_ref[pl.ds(r0, 16), :] = dzr.astype(dzr_ref.dtype)
            dzi_ref[pl.ds(r0, 16), :] = dzi.astype(dzi_ref.dtype)
            return carry, dba + _colsum(dzr), dbi + _colsum(dzi), dlam + _colsum(dlog_a * r16)

        zero = jnp.zeros((1, tc), F32)
        _, dba, dbi, dlam = lax.fori_loop(0, S // 16, step, (jnp.zeros((8, tc), F32), zero, zero, zero))
        dba_ref[...] = dba
        dbi_ref[...] = dbi
        dlam_ref[...] = dlam * dlam_scale

    col = pl.BlockSpec((S, tc), lambda j: (0, j))
    vec = pl.BlockSpec((1, tc), lambda j: (0, j))
    return _call(
        body,
        name="lru_scan_bwd",
        grid=(D_RNN // tc,),
        in_specs=[col, pl.BlockSpec((S, tc), lambda j: (0, OFF_RY // tc + j)), col, col, col, col, vec],
        out_specs=[col, col, col, col, vec, vec, vec],
        out_shape=[jax.ShapeDtypeStruct((S, D_RNN), MXU_DTYPE)] * 3 + [jax.ShapeDtypeStruct((S, D_RNN), F32)]
        + [jax.ShapeDtypeStruct((1, D_RNN), F32)] * 3,
        scratch_shapes=[pltpu.VMEM((S + PAD, tc), F32), pltpu.VMEM((S, tc), F32), pltpu.VMEM((S + PAD, tc), F32)],
        semantics=("parallel",),
        operands=(dy, proj, h, r, i, rxc, lam),
        side=side,
    )


def _lru_gate_wgrad(rxc, dzr, dzi, side=None):
    def body(x_ref, dzr_ref, dzi_ref, dwa_ref, dwi_ref):
        xv = x_ref[...].astype(MXU_DTYPE)
        dims = (((0,), (0,)), ((), ()))
        dwa_ref[...] = lax.dot_general(xv, dzr_ref[...], dims, preferred_element_type=F32)
        dwi_ref[...] = lax.dot_general(xv, dzi_ref[...], dims, preferred_element_type=F32)

    col = pl.BlockSpec((S, RNN_GROUP), lambda g: (0, g))
    w_spec = pl.BlockSpec((None, RNN_GROUP, RNN_GROUP), lambda g: (g, 0, 0))
    return _call(
        body,
        name="lru_gate_wgrad",
        grid=(N_RNN_GROUPS,),
        in_specs=[col, col, col],
        out_specs=[w_spec, w_spec],
        out_shape=[jax.ShapeDtypeStruct((N_RNN_GROUPS, RNN_GROUP, RNN_GROUP), F32)] * 2,
        semantics=("parallel",),
        operands=(rxc, dzr, dzi),
        side=side,
    )


def _lru_gate_xgrad(dzr, dzi, wa, wi, dx_in, side=None):
    tm = 512

    def body(dzr_ref, dzi_ref, wa_ref, wi_ref, dx_ref, o_ref):
        dims = (((1,), (1,)), ((), ()))
        o_ref[...] = (dx_ref[...]
                      + lax.dot_general(dzr_ref[...], wa_ref[...].astype(MXU_DTYPE), dims, preferred_element_type=F32)
                      + lax.dot_general(dzi_ref[...], wi_ref[...].astype(MXU_DTYPE), dims, preferred_element_type=F32))

    x_spec = pl.BlockSpec((tm, RNN_GROUP), lambda g, i: (i, g))
    w_spec = pl.BlockSpec((None, RNN_GROUP, RNN_GROUP), lambda g, i: (g, 0, 0))
    return _call(
        body,
        name="lru_gate_xgrad",
        grid=(N_RNN_GROUPS, S // tm),
        in_specs=[x_spec, x_spec, w_spec, w_spec, x_spec],
        out_specs=x_spec,
        out_shape=jax.ShapeDtypeStruct((S, D_RNN), F32),
        semantics=("parallel", "parallel"),
        operands=(dzr, dzi, wa, wi, dx_in),
        side=side,
    )


def _gate_fwd(y_attn, y_rnn, proj, b_gate, side=None):
    t = 512

    def body(ya_ref, yr_ref, ga_ref, gr_ref, ba_ref, br_ref, o_ref):
        o_ref[...] = (_sigmoid(ga_ref[...] + ba_ref[...]) * ya_ref[...]
                      + _sigmoid(gr_ref[...] + br_ref[...]) * yr_ref[...]).astype(o_ref.dtype)

    tile = pl.BlockSpec((t, t), lambda i, j: (i, j))
    return _call(
        body,
        name="gate_fwd",
        grid=(S // t, D // t),
        in_specs=[tile, tile,
                  pl.BlockSpec((t, t), lambda i, j: (i, OFF_GA // t + j)),
                  pl.BlockSpec((t, t), lambda i, j: (i, OFF_GR // t + j)),
                  pl.BlockSpec((1, t), lambda i, j: (0, j)),
                  pl.BlockSpec((1, t), lambda i, j: (0, D // t + j))],
        out_specs=tile,
        out_shape=jax.ShapeDtypeStruct((S, D), MXU_DTYPE),
        semantics=("parallel", "parallel"),
        operands=(y_attn, y_rnn, proj, proj, b_gate, b_gate),
        side=side,
    )


def _gate_bwd(dmix, y_attn, y_rnn, proj, b_gate, side=None):
    t = 512

    def body(dm_ref, ya_ref, yr_ref, ga_ref, gr_ref, ba_ref, br_ref,
             dya_ref, dyr_ref, dga_ref, dgr_ref, dba_ref, dbr_ref):
        @pl.when(pl.program_id(1) == 0)
        def _():
            dba_ref[...] = jnp.zeros_like(dba_ref)
            dbr_ref[...] = jnp.zeros_like(dbr_ref)

        dm = dm_ref[...]
        ga = _sigmoid(ga_ref[...] + ba_ref[...])
        gr = _sigmoid(gr_ref[...] + br_ref[...])
        dya_ref[...] = (dm * ga).astype(dya_ref.dtype)
        dyr_ref[...] = (dm * gr).astype(dyr_ref.dtype)
        dga = dm * ya_ref[...] * ga * (1.0 - ga)
        dgr = dm * yr_ref[...] * gr * (1.0 - gr)
        dga_ref[...] = dga.astype(dga_ref.dtype)
        dgr_ref[...] = dgr.astype(dgr_ref.dtype)
        dba_ref[...] += _colsum(dga)
        dbr_ref[...] += _colsum(dgr)

    tile = pl.BlockSpec((t, t), lambda j, i: (i, j))
    vec = pl.BlockSpec((1, t), lambda j, i: (0, j))
    return _call(
        body,
        name="gate_bwd",
        grid=(D // t, S // t),
        in_specs=[tile, tile, tile,
                  pl.BlockSpec((t, t), lambda j, i: (i, OFF_GA // t + j)),
                  pl.BlockSpec((t, t), lambda j, i: (i, OFF_GR // t + j)),
                  vec,
                  pl.BlockSpec((1, t), lambda j, i: (0, D // t + j))],
        out_specs=[tile, tile, tile, tile, vec, vec],
        out_shape=[jax.ShapeDtypeStruct((S, D), MXU_DTYPE)] * 4 + [jax.ShapeDtypeStruct((1, D), F32)] * 2,
        semantics=("parallel", "arbitrary"),
        operands=(dmix, y_attn, y_rnn, proj, proj, b_gate, b_gate),
        side=side,
    )


LN_TM = 256


def _ln_stats(pre):
    mu = jnp.mean(pre, axis=-1, keepdims=True)
    xc = pre - mu
    rstd = lax.rsqrt(jnp.mean(xc * xc, axis=-1, keepdims=True) + LN_EPS)
    return xc * rstd, rstd


def _ln_input_grad(dy, xhat, rstd, g):
    dyg = dy * g
    return rstd * (dyg - jnp.mean(dyg, axis=-1, keepdims=True)
                   - xhat * jnp.mean(dyg * xhat, axis=-1, keepdims=True))


def _ln_fwd(res, branch, g, b, side=None):
    def body(res_ref, br_ref, g_ref, b_ref, y_ref, yb_ref, xhat_ref, rstd_ref):
        xhat, rstd = _ln_stats(ALPHA * res_ref[...] + br_ref[...])
        y = xhat * g_ref[...] + b_ref[...]
        y_ref[...] = y
        yb_ref[...] = y.astype(yb_ref.dtype)
        xhat_ref[...] = xhat
        rstd_ref[...] = rstd

    tile = pl.BlockSpec((LN_TM, D), lambda i: (i, 0))
    vec = pl.BlockSpec((1, D), lambda i: (0, 0))
    return _call(
        body,
        name="ln_fwd",
        grid=(S // LN_TM,),
        in_specs=[tile, tile, vec, vec],
        out_specs=[tile, tile, tile, pl.BlockSpec((LN_TM, 1), lambda i: (i, 0))],
        out_shape=[jax.ShapeDtypeStruct((S, D), F32), jax.ShapeDtypeStruct((S, D), MXU_DTYPE),
                   jax.ShapeDtypeStruct((S, D), F32), jax.ShapeDtypeStruct((S, 1), F32)],
        semantics=("parallel",),
        operands=(res, branch, g, b),
        side=side,
    )


def _ln_bwd(dy_a, dy_b, xhat, rstd, g, side=None):
    def body(da_ref, db_in_ref, xhat_ref, rstd_ref, g_ref, dp_ref, dpb_ref, dg_ref, db_ref):
        @pl.when(pl.program_id(0) == 0)
        def _():
            dg_ref[...] = jnp.zeros_like(dg_ref)
            db_ref[...] = jnp.zeros_like(db_ref)

        dy = da_ref[...] + ALPHA * db_in_ref[...]
        xhat = xhat_ref[...]
        dp = _ln_input_grad(dy, xhat, rstd_ref[...], g_ref[...])
        dp_ref[...] = dp
        dpb_ref[...] = dp.astype(dpb_ref.dtype)
        dg_ref[...] += _colsum(dy * xhat)
        db_ref[...] += _colsum(dy)

    tile = pl.BlockSpec((LN_TM, D), lambda i: (i, 0))
    vec = pl.BlockSpec((1, D), lambda i: (0, 0))
    return _call(
        body,
        name="ln_bwd",
        grid=(S // LN_TM,),
        in_specs=[tile, tile, tile, pl.BlockSpec((LN_TM, 1), lambda i: (i, 0)), vec],
        out_specs=[tile, tile, vec, vec],
        out_shape=[jax.ShapeDtypeStruct((S, D), F32), jax.ShapeDtypeStruct((S, D), MXU_DTYPE),
                   jax.ShapeDtypeStruct((1, D), F32), jax.ShapeDtypeStruct((1, D), F32)],
        semantics=("arbitrary",),
        operands=(dy_a, dy_b, xhat, rstd, g),
        side=side,
    )


def _ln_loss_bwd(res, branch, g, b, target, side=None):
    def body(res_ref, br_ref, g_ref, b_ref, t_ref, loss_ref, dp_ref, dpb_ref, dg_ref, db_ref):
        @pl.when(pl.program_id(0) == 0)
        def _():
            loss_ref[...] = jnp.zeros_like(loss_ref)
            dg_ref[...] = jnp.zeros_like(dg_ref)
            db_ref[...] = jnp.zeros_like(db_ref)

        xhat, rstd = _ln_stats(ALPHA * res_ref[...] + br_ref[...])
        gv = g_ref[...]
        err = xhat * gv + b_ref[...] - t_ref[...]
        loss_ref[...] += (0.5 / D) * jnp.sum(_colsum(err * err), axis=1, keepdims=True)
        dy = err * (1.0 / D)
        dp = _ln_input_grad(dy, xhat, rstd, gv)
        dp_ref[...] = dp
        dpb_ref[...] = dp.astype(dpb_ref.dtype)
        dg_ref[...] += _colsum(dy * xhat)
        db_ref[...] += _colsum(dy)

    tile = pl.BlockSpec((LN_TM, D), lambda i: (i, 0))
    vec = pl.BlockSpec((1, D), lambda i: (0, 0))
    return _call(
        body,
        name="ln_loss_bwd",
        grid=(S // LN_TM,),
        in_specs=[tile, tile, vec, vec, tile],
        out_specs=[pl.BlockSpec((1, 1), lambda i: (0, 0)), tile, tile, vec, vec],
        out_shape=[jax.ShapeDtypeStruct((1, 1), F32), jax.ShapeDtypeStruct((S, D), F32),
                   jax.ShapeDtypeStruct((S, D), MXU_DTYPE),
                   jax.ShapeDtypeStruct((1, D), F32), jax.ShapeDtypeStruct((1, D), F32)],
        semantics=("arbitrary",),
        operands=(res, branch, g, b, target),
        side=side,
    )


FFN_TC = 256


def _ffn_act_fwd(up, gpre, w, b, side=None):
    tc = FFN_TC

    def body(up_ref, x_ref, w_ref, b_ref, o_ref, xpad_ref):
        xpad_ref[pl.ds(0, PAD), :] = jnp.zeros((PAD, tc), F32)
        xpad_ref[pl.ds(PAD, S), :] = x_ref[...]
        wv = w_ref[...]
        bv = b_ref[...]

        def step(ci, carry):
            r0 = pl.multiple_of(ci * CHUNK, CHUNK)
            taps = _past_taps(xpad_ref, r0, FFN_CONV_W)
            gate = bv + taps[0] * wv[0:1, :] + taps[1] * wv[1:2, :] + taps[2] * wv[2:3, :]
            o_ref[pl.ds(r0, CHUNK), :] = (_gelu(gate)[0] * up_ref[pl.ds(r0, CHUNK), :]).astype(o_ref.dtype)
            return carry

        lax.fori_loop(0, S // CHUNK, step, 0)

    col = pl.BlockSpec((S, tc), lambda j: (0, j))
    return _call(
        body,
        name="ffn_act_fwd",
        grid=(D_FF // tc,),
        in_specs=[col, col, pl.BlockSpec((FFN_CONV_W, tc), lambda j: (0, j)), pl.BlockSpec((1, tc), lambda j: (0, j))],
        out_specs=col,
        out_shape=jax.ShapeDtypeStruct((S, D_FF), MXU_DTYPE),
        scratch_shapes=[pltpu.VMEM((S + PAD, tc), F32)],
        semantics=("parallel",),
        operands=(up, gpre, w, b),
        side=side,
    )


def _ffn_act_bwd(dfin, up, gpre, w, b, side=None):
    tc = FFN_TC
    width = FFN_CONV_W

    def body(df_ref, up_ref, x_ref, w_ref, b_ref, dup_ref, dx_ref, dw_ref, db_ref, xpad_ref, dpad_ref):
        xpad_ref[pl.ds(0, PAD), :] = jnp.zeros((PAD, tc), F32)
        xpad_ref[pl.ds(PAD, S), :] = x_ref[...]
        dpad_ref[pl.ds(S, PAD), :] = jnp.zeros((PAD, tc), F32)
        wv = w_ref[...]
        bv = b_ref[...]

        def gate_grad(ci, acc):
            r0 = pl.multiple_of(ci * CHUNK, CHUNK)
            taps = _past_taps(xpad_ref, r0, width)
            gate = bv + taps[0] * wv[0:1, :] + taps[1] * wv[1:2, :] + taps[2] * wv[2:3, :]
            ge, dge = _gelu(gate)
            df = df_ref[pl.ds(r0, CHUNK), :]
            dup_ref[pl.ds(r0, CHUNK), :] = (df * ge).astype(dup_ref.dtype)
            d = df * up_ref[pl.ds(r0, CHUNK), :] * dge
            dpad_ref[pl.ds(r0, CHUNK), :] = d
            return tuple(acc[k] + _colsum(taps[k] * d) for k in range(width)) + (acc[width] + _colsum(d),)

        zero = jnp.zeros((1, tc), F32)
        acc = lax.fori_loop(0, S // CHUNK, gate_grad, (zero,) * (width + 1))
        for k in range(width):
            dw_ref[k:k + 1, :] = acc[k]
        db_ref[...] = acc[width]

        def input_grad(ci, carry):
            r0 = pl.multiple_of(ci * CHUNK, CHUNK)
            ahead = _future_taps(dpad_ref, r0, width)
            dx = ahead[0] * wv[2:3, :] + ahead[1] * wv[1:2, :] + ahead[2] * wv[0:1, :]
            dx_ref[pl.ds(r0, CHUNK), :] = dx.astype(dx_ref.dtype)
            return carry

        lax.fori_loop(0, S // CHUNK, input_grad, 0)

    col = pl.BlockSpec((S, tc), lambda j: (0, j))
    w_spec = pl.BlockSpec((width, tc), lambda j: (0, j))
    vec = pl.BlockSpec((1, tc), lambda j: (0, j))
    return _call(
        body,
        name="ffn_act_bwd",
        grid=(D_FF // tc,),
        in_specs=[col, col, col, w_spec, vec],
        out_specs=[col, col, w_spec, vec],
        out_shape=[jax.ShapeDtypeStruct((S, D_FF), MXU_DTYPE)] * 2
        + [jax.ShapeDtypeStruct((width, D_FF), F32), jax.ShapeDtypeStruct((1, D_FF), F32)],
        scratch_shapes=[pltpu.VMEM((S + PAD, tc), F32), pltpu.VMEM((S + PAD, tc), F32)],
        semantics=("parallel",),
        operands=(dfin, up, gpre, w, b),
        side=side,
    )


def _adamw_update(w, g, m, v):
    m = ADAM_B1 * m + (1.0 - ADAM_B1) * g
    v = ADAM_B2 * v + (1.0 - ADAM_B2) * (g * g)
    m_hat = m / (1.0 - ADAM_B1 ** ADAM_STEP)
    v_hat = v / (1.0 - ADAM_B2 ** ADAM_STEP)
    delta = -ADAM_LR * (m_hat / (jnp.sqrt(v_hat) + ADAM_EPS) + ADAM_WD * w)
    return delta, m, v


def _add_pairs(send, pair, far_index, *, name):
    _, r_dim, c_dim = send.shape
    tr = r_dim // 4

    def body(far_ref, mine_ref, theirs_ref, o_ref):
        o_ref[...] = (mine_ref[...].astype(F32) + theirs_ref[...].astype(F32)).astype(o_ref.dtype)

    return pl.pallas_call(
        body,
        name=name,
        grid_spec=pltpu.PrefetchScalarGridSpec(
            num_scalar_prefetch=1,
            grid=(3, r_dim // tr),
            in_specs=[pl.BlockSpec((None, tr, c_dim), lambda j, i, far: (far[j], i, 0)),
                      pl.BlockSpec((None, tr, c_dim), lambda j, i, far: (1 + j, i, 0))],
            out_specs=pl.BlockSpec((None, tr, c_dim), lambda j, i, far: (j, i, 0)),
        ),
        out_shape=jax.ShapeDtypeStruct((3, r_dim, c_dim), BF16),
        compiler_params=_cparams("parallel", "parallel"),
    )(far_index, send, pair)


def _reduce_adamw(w, m, v, g_own, pair, far, *, tr, name):
    r_dim, c_dim = w.shape

    def body(w_ref, m_ref, v_ref, g_ref, pair_ref, far_ref, grad_ref, delta_ref, nm_ref, nv_ref):
        g = g_ref[...] + pair_ref[...].astype(F32)
        for j in range(3):
            g = g + far_ref[j].astype(F32)
        delta, nm, nv = _adamw_update(w_ref[...], g, m_ref[...], v_ref[...])
        grad_ref[...] = g
        delta_ref[...] = delta
        nm_ref[...] = nm
        nv_ref[...] = nv

    tile = pl.BlockSpec((tr, c_dim), lambda i: (i, 0))
    return _call(
        body,
        name=name,
        grid=(r_dim // tr,),
        in_specs=[tile, tile, tile, tile, pl.BlockSpec((None, tr, c_dim), lambda i: (0, i, 0)),
                  pl.BlockSpec((3, tr, c_dim), lambda i: (0, i, 0))],
        out_specs=[tile] * 4,
        out_shape=[jax.ShapeDtypeStruct((r_dim, c_dim), F32)] * 4,
        semantics=("parallel",),
        operands=(w, m, v, g_own, pair, far),
    )


def _adamw_many(ws, ms, vs, gs):
    n = len(ws)

    def body(*refs):
        for i in range(n):
            delta, nm, nv = _adamw_update(refs[i][...], refs[3 * n + i][...], refs[n + i][...], refs[2 * n + i][...])
            refs[4 * n + i][...] = delta
            refs[5 * n + i][...] = nm
            refs[6 * n + i][...] = nv

    vmem = pl.BlockSpec(memory_space=pltpu.VMEM)
    res = pl.pallas_call(
        body,
        name="adamw_small",
        in_specs=[vmem] * (4 * n),
        out_specs=[vmem] * (3 * n),
        out_shape=[jax.ShapeDtypeStruct(w.shape, F32) for w in ws] * 3,
        compiler_params=pltpu.CompilerParams(vmem_limit_bytes=VMEM_LIMIT),
    )(*ws, *ms, *vs, *gs)
    return res[:n], res[n:2 * n], res[2 * n:]


def _adamw_blocks(w, m, v, g, *, name, side=None):
    per = 2

    def body(w_ref, m_ref, v_ref, g_ref, delta_ref, nm_ref, nv_ref):
        delta, nm, nv = _adamw_update(w_ref[...], g_ref[...], m_ref[...], v_ref[...])
        delta_ref[...] = delta
        nm_ref[...] = nm
        nv_ref[...] = nv

    tile = pl.BlockSpec((1, per) + w.shape[2:], lambda i: (0, i, 0, 0))
    return _call(
        body,
        name=name,
        grid=(w.shape[1] // per,),
        in_specs=[tile] * 4,
        out_specs=[tile] * 3,
        out_shape=[jax.ShapeDtypeStruct(w.shape, F32)] * 3,
        semantics=("parallel",),
        operands=(w, m, v, g),
        side=side,
    )


def _coords():
    return lax.axis_index("x"), lax.axis_index("y"), lax.axis_index("c")


def _flip(coord, bit):
    return 1 - coord if bit else coord


def _relative(k):
    x, y, c = _coords()
    return _flip(x, k & 4), _flip(y, k & 2), _flip(c, k & 1)


def _index(pos):
    return 4 * pos[0] + 2 * pos[1] + pos[2]


FAR = (4, 2, 6)
AG_US_PER_MB = 44.0
RS_US_PER_MB = 45.0
ROW_ALIGN = 16


def _chunks(items, cursor, us, us_per_mb, through=None):
    budget = float("inf") if us is None else us / us_per_mb * 2 ** 20
    names = list(items)
    if through is not None:
        names = names[:names.index(through) + 1]
    chunks = []
    for name in names:
        arr = items[name]
        r_dim, c_dim = arr.shape[-2:]
        row_bytes = c_dim * arr.dtype.itemsize
        while cursor[name] < r_dim and budget > 0:
            rows = r_dim - cursor[name]
            if r_dim > ROW_ALIGN and budget < rows * row_bytes:
                rows = min(rows, max(ROW_ALIGN, int(budget // row_bytes) // ROW_ALIGN * ROW_ALIGN))
            chunks.append((name, cursor[name], rows))
            cursor[name] += rows
            budget -= rows * row_bytes
    return chunks


class _Gather:
    def __init__(self, shards):
        self.shards = dict(shards)
        self.bufs = {n: None for n in self.shards}
        self.cursor = {n: 0 for n in self.shards}

    def take(self, us=None, through=None):
        chunks = _chunks(self.shards, self.cursor, us, AG_US_PER_MB, through)
        return _GatherSide(self, chunks) if chunks else None

    def get(self, name):
        chunks = _chunks(self.shards, self.cursor, None, AG_US_PER_MB, through=name)
        if chunks:
            _run_side(_GatherSide(self, chunks), "gather_" + name)
        return self.bufs[name]


class _GatherSide:
    def __init__(self, owner, chunks):
        self.owner, self.chunks = owner, chunks
        self.names = list(dict.fromkeys(n for n, _, _ in chunks))
        old = [n for n in self.names if owner.bufs[n] is not None]
        self.operands = [owner.shards[n] for n in self.names] + [owner.bufs[n] for n in old]
        self.out_shape = [jax.ShapeDtypeStruct((N_DEV,) + owner.shards[n].shape, owner.shards[n].dtype)
                          for n in self.names]
        self.aliases = {len(self.names) + i: self.names.index(n) for i, n in enumerate(old)}
        self.sems = [pltpu.SemaphoreType.DMA((7 * len(chunks),)), pltpu.SemaphoreType.DMA((7 * len(chunks),)),
                     pltpu.SemaphoreType.DMA((len(chunks),))]

    def _copy(self, ins, outs, sems, ci, k, block, to, from_shard=False):
        name, r0, rows = self.chunks[ci]
        w = self.names.index(name)
        slot = outs[w].at[_index(block), pl.ds(r0, rows)]
        return pltpu.make_async_remote_copy(
            src_ref=ins[w].at[pl.ds(r0, rows)] if from_shard else slot, dst_ref=slot,
            send_sem=sems[0].at[7 * ci + k], recv_sem=sems[1].at[7 * ci + k], device_id=to, device_id_type=MESH)

    def _own(self, ins, outs, sems, ci):
        name, r0, rows = self.chunks[ci]
        w = self.names.index(name)
        return pltpu.make_async_copy(ins[w].at[pl.ds(r0, rows)], outs[w].at[_index(_relative(0)), pl.ds(r0, rows)],
                                     sems[2].at[ci])

    def start(self, ins, outs, sems):
        me, sibling = _relative(0), _relative(1)
        for ci in range(len(self.chunks)):
            self._own(ins, outs, sems, ci).start()
        for j, k in enumerate(FAR):
            for ci in range(len(self.chunks)):
                self._copy(ins, outs, sems, ci, 1 + j, me, _relative(k), from_shard=True).start()
        for ci in range(len(self.chunks)):
            self._copy(ins, outs, sems, ci, 0, me, sibling, from_shard=True).start()

    def finish(self, ins, outs, sems):
        me, sibling = _relative(0), _relative(1)
        n = len(self.chunks)
        for j, k in enumerate(FAR):
            for ci in range(n):
                self._copy(ins, outs, sems, ci, 1 + j, _relative(k), me).wait_recv()
                self._copy(ins, outs, sems, ci, 4 + j, _relative(k), sibling).start()
        for ci in range(n):
            self._copy(ins, outs, sems, ci, 0, sibling, me).wait_recv()
        for j, k in enumerate(FAR):
            for ci in range(n):
                self._copy(ins, outs, sems, ci, 4 + j, _relative(k | 1), me).wait_recv()
        for ci in range(n):
            self._copy(ins, outs, sems, ci, 0, me, sibling, from_shard=True).wait_send()
            for j, k in enumerate(FAR):
                self._copy(ins, outs, sems, ci, 1 + j, me, _relative(k), from_shard=True).wait_send()
                self._copy(ins, outs, sems, ci, 4 + j, _relative(k), sibling).wait_send()
            self._own(ins, outs, sems, ci).wait()

    def done(self, results):
        for n, buf in zip(self.names, results):
            self.owner.bufs[n] = buf


class _Scatter:
    def __init__(self, me, far_index):
        self.me, self.far_index = me, far_index
        self.sends, self.owns, self.pairs, self.sums, self.fars = {}, {}, {}, {}, {}
        self.pair_cursor, self.far_cursor = {}, {}

    def add(self, name, send, own):
        self.sends[name] = send
        self.owns[name] = own
        self.pairs[name] = self.fars[name] = None
        self.pair_cursor[name] = 0

    def _rows(self, name):
        return self.sends[name].shape[1]

    def _add_ready_pairs(self):
        for name in self.sends:
            if name not in self.sums and self.pair_cursor[name] == self._rows(name):
                self.sums[name] = _add_pairs(self.sends[name], self.pairs[name], self.far_index, name="pair_" + name)
                self.far_cursor[name] = 0

    def _side(self, us, through=None):
        self._add_ready_pairs()
        names = list(self.sends)
        if through is not None:
            names = names[:names.index(through) + 1]
        pair_chunks = [(n, self.pair_cursor[n], self._rows(n) - self.pair_cursor[n]) for n in names
                       if self.pair_cursor[n] < self._rows(n)]
        for n, _, _ in pair_chunks:
            self.pair_cursor[n] = self._rows(n)
        far_chunks = _chunks(self.sums, self.far_cursor, us, RS_US_PER_MB,
                             through if through in self.sums else None) if self.sums else []
        return _ScatterSide(self, pair_chunks, far_chunks) if pair_chunks or far_chunks else None

    def add_blocks(self, name, blocks32, blocks16):
        self.add(name, blocks16, lax.dynamic_index_in_dim(blocks32, self.me, axis=0, keepdims=False))

    def add_cols(self, name, full32, full16):
        width = full32.shape[1] // N_DEV
        self.add(name, _blocks(full16, "cols"), lax.dynamic_slice_in_dim(full32, self.me * width, width, axis=1))

    def take(self, us):
        return self._side(us)

    def flush_pairs(self, name):
        side = self._side(0.0)
        if side is not None:
            _run_side(side, name)
        self._add_ready_pairs()

    def get(self, name):
        step = 0
        while name not in self.sums or self.far_cursor[name] < self._rows(name):
            _run_side(self._side(None, through=name), "scatter_%s_%d" % (name, step))
            step += 1
        return self.owns[name], self.pairs[name], self.fars[name]


class _ScatterSide:
    TO_SIBLING = (1, 5, 3, 7)

    def __init__(self, owner, pair_chunks, far_chunks):
        self.owner, self.pair_chunks, self.far_chunks = owner, pair_chunks, far_chunks
        self.pair_names = list(dict.fromkeys(n for n, _, _ in pair_chunks))
        self.far_names = list(dict.fromkeys(n for n, _, _ in far_chunks))
        ins = [(owner.sends[n], owner.pairs[n], (4,)) for n in self.pair_names]
        ins += [(owner.sums[n], owner.fars[n], (3,)) for n in self.far_names]
        old = [i for i, (_, buf, _) in enumerate(ins) if buf is not None]
        self.operands = [src for src, _, _ in ins] + [ins[i][1] for i in old]
        self.out_shape = [jax.ShapeDtypeStruct(slots + src.shape[1:], BF16) for src, _, slots in ins]
        self.aliases = {len(ins) + j: i for j, i in enumerate(old)}
        n_pair, n_far = 4 * len(pair_chunks), 3 * len(far_chunks)
        self.sems = [pltpu.SemaphoreType.DMA((max(n_pair, 1),)), pltpu.SemaphoreType.DMA((max(n_pair, 1),)),
                     pltpu.SemaphoreType.DMA((max(n_far, 1),)), pltpu.SemaphoreType.DMA((max(n_far, 1),))]

    def _copies(self, ins, outs, sems):
        copies = []
        for ci, (name, r0, rows) in enumerate(self.pair_chunks):
            w = self.pair_names.index(name)
            for j, k in enumerate(self.TO_SIBLING):
                copies.append(pltpu.make_async_remote_copy(
                    src_ref=ins[w].at[_index(_relative(k)), pl.ds(r0, rows)], dst_ref=outs[w].at[j, pl.ds(r0, rows)],
                    send_sem=sems[0].at[4 * ci + j], recv_sem=sems[1].at[4 * ci + j],
                    device_id=_relative(1), device_id_type=MESH))
        for ci, (name, r0, rows) in enumerate(self.far_chunks):
            w = len(self.pair_names) + self.far_names.index(name)
            for j, k in enumerate(FAR):
                copies.append(pltpu.make_async_remote_copy(
                    src_ref=ins[w].at[j, pl.ds(r0, rows)], dst_ref=outs[w].at[j, pl.ds(r0, rows)],
                    send_sem=sems[2].at[3 * ci + j], recv_sem=sems[3].at[3 * ci + j],
                    device_id=_relative(k), device_id_type=MESH))
        return copies

    def start(self, ins, outs, sems):
        for cp in self._copies(ins, outs, sems):
            cp.start()

    def finish(self, ins, outs, sems):
        for cp in self._copies(ins, outs, sems):
            cp.wait()

    def done(self, results):
        for n, buf in zip(self.pair_names, results):
            self.owner.pairs[n] = buf
        for n, buf in zip(self.far_names, results[len(self.pair_names):]):
            self.owner.fars[n] = buf


class _Joined:
    def __init__(self, sides):
        self.sides = sides
        self.operands, self.out_shape, self.sems, self.aliases, self.spans = [], [], [], {}, []
        for s in sides:
            i0, o0, s0 = len(self.operands), len(self.out_shape), len(self.sems)
            self.operands += list(s.operands)
            self.out_shape += list(s.out_shape)
            self.sems += list(s.sems)
            self.aliases.update({i0 + i: o0 + o for i, o in s.aliases.items()})
            self.spans.append((slice(i0, len(self.operands)), slice(o0, len(self.out_shape)),
                               slice(s0, len(self.sems))))

    def start(self, ins, outs, sems):
        for s, (i, o, m) in zip(self.sides, self.spans):
            s.start(ins[i], outs[o], sems[m])

    def finish(self, ins, outs, sems):
        for s, (i, o, m) in zip(self.sides, self.spans):
            s.finish(ins[i], outs[o], sems[m])

    def done(self, results):
        for s, (_, o, _) in zip(self.sides, self.spans):
            s.done(results[o])


def _join(*sides):
    sides = [s for s in sides if s is not None]
    if len(sides) <= 1:
        return sides[0] if sides else None
    return _Joined(sides)


def _pack_rows(vecs):
    rows = -(-sum(v.shape[0] for v in vecs) // 8) * 8
    width = max(v.shape[1] for v in vecs)

    def body(*refs):
        out = refs[-1]
        out[...] = jnp.zeros_like(out)
        r0 = 0
        for v in refs[:-1]:
            out[r0:r0 + v.shape[0], 0:v.shape[1]] = v[...]
            r0 += v.shape[0]

    vmem = pl.BlockSpec(memory_space=pltpu.VMEM)
    return pl.pallas_call(body, name="pack_small", in_specs=[vmem] * len(vecs), out_specs=vmem,
                          out_shape=jax.ShapeDtypeStruct((rows, width), F32))(*vecs)


def _sum_rows(inbox, shapes):
    def body(inbox_ref, *refs):
        outs, total = refs[:-1], refs[-1]
        acc = inbox_ref[0]
        for d in range(1, N_DEV):
            acc = acc + inbox_ref[d]
        total[...] = acc
        r0 = 0
        for o in outs:
            o[...] = total[r0:r0 + o.shape[0], 0:o.shape[1]]
            r0 += o.shape[0]

    vmem = pl.BlockSpec(memory_space=pltpu.VMEM)
    return pl.pallas_call(body, name="sum_small", in_specs=[vmem], out_specs=[vmem] * len(shapes),
                          out_shape=[jax.ShapeDtypeStruct(s, F32) for s in shapes],
                          scratch_shapes=[pltpu.VMEM(inbox.shape[1:], F32)])(inbox)


class _ShareRows:
    def __init__(self, mine):
        self.operands, self.aliases = [mine], {}
        self.out_shape = [jax.ShapeDtypeStruct((N_DEV,) + mine.shape, F32)]
        self.sems = [pltpu.SemaphoreType.DMA((N_DEV - 1,)), pltpu.SemaphoreType.DMA((N_DEV - 1,)),
                     pltpu.SemaphoreType.DMA(())]

    def _copy(self, ins, outs, sems, k, sender):
        return pltpu.make_async_remote_copy(
            src_ref=ins[0], dst_ref=outs[0].at[_index(sender)], send_sem=sems[0].at[k - 1], recv_sem=sems[1].at[k - 1],
            device_id=_relative(k), device_id_type=MESH)

    def _own(self, ins, outs, sems):
        return pltpu.make_async_copy(ins[0], outs[0].at[_index(_relative(0))], sems[2])

    def start(self, ins, outs, sems):
        self._own(ins, outs, sems).start()
        for k in range(1, N_DEV):
            self._copy(ins, outs, sems, k, _relative(0)).start()

    def finish(self, ins, outs, sems):
        for k in range(1, N_DEV):
            self._copy(ins, outs, sems, k, _relative(k)).wait_recv()
            self._copy(ins, outs, sems, k, _relative(0)).wait_send()
        self._own(ins, outs, sems).wait()

    def done(self, results):
        self.inbox = results[0]


class _PartsToOwners:
    def __init__(self, mats):
        self.n, self.per = len(mats), mats[0].shape[0] // N_DEV
        self.operands, self.aliases = list(mats), {}
        self.out_shape = [jax.ShapeDtypeStruct((N_DEV, self.n, self.per) + mats[0].shape[1:], F32)]
        self.sems = [pltpu.SemaphoreType.DMA((self.n * (N_DEV - 1),))] * 2

    def _copies(self, ins, outs, sems):
        return [pltpu.make_async_remote_copy(
            src_ref=ins[j].at[pl.ds(self.per * _index(_relative(k)), self.per)], dst_ref=outs[0].at[k, j],
            send_sem=sems[0].at[self.n * (k - 1) + j], recv_sem=sems[1].at[self.n * (k - 1) + j],
            device_id=_relative(k), device_id_type=MESH) for k in range(1, N_DEV) for j in range(self.n)]

    def start(self, ins, outs, sems):
        for cp in self._copies(ins, outs, sems):
            cp.start()

    def finish(self, ins, outs, sems):
        for cp in self._copies(ins, outs, sems):
            cp.wait()

    def done(self, results):
        self.stage = results[0]


def _sum_parts(mats, stage):
    n, per = len(mats), mats[0].shape[0] // N_DEV

    def body(*refs):
        stage_ref, out = refs[n], refs[n + 1]
        me = _index(_relative(0))
        for j in range(n):
            acc = refs[j][pl.ds(per * me, per)]
            for k in range(1, N_DEV):
                acc = acc + stage_ref[k, j]
            out[j] = acc

    vmem = pl.BlockSpec(memory_space=pltpu.VMEM)
    return pl.pallas_call(body, name="sum_small_parts", in_specs=[vmem] * (n + 1), out_specs=vmem,
                          out_shape=jax.ShapeDtypeStruct((n, per) + mats[0].shape[1:], F32),
                          compiler_params=pltpu.CompilerParams(vmem_limit_bytes=VMEM_LIMIT))(*mats, stage)


class _PartsToAll:
    def __init__(self, parts, rows):
        self.n, self.per = parts.shape[0], parts.shape[1]
        self.operands, self.aliases = [parts], {}
        self.out_shape = [jax.ShapeDtypeStruct((rows,) + parts.shape[2:], F32)] * self.n
        self.sems = [pltpu.SemaphoreType.DMA((self.n * (N_DEV - 1),))] * 2 + [pltpu.SemaphoreType.DMA((self.n,))]

    def _rows(self, ref, pos):
        return ref.at[pl.ds(self.per * _index(pos), self.per)]

    def _copy(self, ins, outs, sems, k, j, owner):
        return pltpu.make_async_remote_copy(
            src_ref=ins[0].at[j], dst_ref=self._rows(outs[j], owner),
            send_sem=sems[0].at[self.n * (k - 1) + j], recv_sem=sems[1].at[self.n * (k - 1) + j],
            device_id=_relative(k), device_id_type=MESH)

    def _own(self, ins, outs, sems, j):
        return pltpu.make_async_copy(ins[0].at[j], self._rows(outs[j], _relative(0)), sems[2].at[j])

    def start(self, ins, outs, sems):
        for j in range(self.n):
            self._own(ins, outs, sems, j).start()
            for k in range(1, N_DEV):
                self._copy(ins, outs, sems, k, j, _relative(0)).start()

    def finish(self, ins, outs, sems):
        for j in range(self.n):
            for k in range(1, N_DEV):
                self._copy(ins, outs, sems, k, j, _relative(k)).wait_recv()
                self._copy(ins, outs, sems, k, j, _relative(0)).wait_send()
            self._own(ins, outs, sems, j).wait()

    def done(self, results):
        self.totals = list(results)


class _SmallSync:
    def __init__(self, vec_names, mat_names):
        self.vec_names, self.mat_names = vec_names, mat_names

    def begin(self, loss, grads):
        vecs = [loss] + [grads[n] for n in self.vec_names]
        self.shapes = [v.shape for v in vecs]
        self.mats = [_diag_blocks(grads[n]) for n in self.mat_names]
        self.share = _ShareRows(_pack_rows(vecs))
        self.to_owners = _PartsToOwners(self.mats)
        return _join(self.share, self.to_owners)

    def middle(self):
        self.sums = _sum_rows(self.share.inbox, self.shapes)
        self.to_all = _PartsToAll(_sum_parts(self.mats, self.to_owners.stage), self.mats[0].shape[0])
        return self.to_all

    def end(self):
        return self.sums[0], dict(zip(self.vec_names, self.sums[1:])), dict(zip(self.mat_names, self.to_all.totals))


def _block_diag(w):
    out = jnp.zeros((N_RNN_GROUPS, RNN_GROUP, RNN_GROUP), w.dtype)
    for n in range(w.shape[0]):
        g, at = n // 4, RNN_BLOCK_W * (n % 4)
        out = out.at[g, at:at + RNN_BLOCK_W, at:at + RNN_BLOCK_W].set(w[n])
    return out


def _diag_blocks(wg):
    blocks = []
    for n in range(4 * N_RNN_GROUPS):
        g, at = n // 4, RNN_BLOCK_W * (n % 4)
        blocks.append(wg[g, at:at + RNN_BLOCK_W, at:at + RNN_BLOCK_W])
    return jnp.stack(blocks)


def _heads_major(t, n_heads):
    return t.reshape(S, n_heads, HEAD_DIM).transpose(1, 0, 2)


def _heads_minor(t):
    return t.transpose(1, 0, 2).reshape(S, t.shape[0] * HEAD_DIM)


def _natural(gathered, how):
    n, r, c = gathered.shape
    if how == "rows":
        return gathered.reshape(n * r, c)
    return gathered.transpose(1, 0, 2).reshape(r, n * c)


def _blocks(full, how):
    if how == "rows":
        return full.reshape(N_DEV, full.shape[0] // N_DEV, full.shape[1])
    return full.reshape(full.shape[0], N_DEV, full.shape[1] // N_DEV).transpose(1, 0, 2)


def _forward_backward(x2, target, small, gather, scatter, sync):
    xb = x2.astype(MXU_DTYPE)
    w_in = _natural(gather.get("w_in"), "cols")
    proj, projb = _mm(xb, w_in, tm=1024, tn=512, tk=D, out_dtype=(F32, MXU_DTYPE), name="proj", side=gather.take(110))

    qt = projb[:, :OFF_K].T.reshape(N_KV, GROUP, HEAD_DIM, S)
    k2, v2 = projb[:, OFF_K:OFF_V], projb[:, OFF_V:OFF_RX]
    kp = jnp.pad(_heads_major(k2, N_KV), ((0, 0), (BLOCK, 0), (0, 0)))
    vp = jnp.pad(_heads_major(v2, N_KV), ((0, 0), (BLOCK, 0), (0, 0)))
    kt = jnp.pad(k2.T.reshape(N_KV, HEAD_DIM, S), ((0, 0), (0, 0), (BLOCK, 0)))
    vt = jnp.pad(v2.T.reshape(N_KV, HEAD_DIM, S), ((0, 0), (0, 0), (BLOCK, 0)))
    sink_row = jnp.repeat(small["attn_sinks"].reshape(N_KV, 1, GROUP), BLOCK, axis=2)
    ot = _attn_fwd(qt, kp, vt, sink_row, side=gather.take(120)).reshape(D, S)

    rconv_w = _natural(gather.get("rnn_conv_w"), "cols")
    rxc = _conv_fwd(proj, OFF_RX, rconv_w, small["rnn_conv_b"], tc=512, name="rnn_conv_fwd", side=gather.take(18))
    r, i = _lru_gates(rxc, small["lru_wa"], small["lru_wi"], small["lru_ba"], small["lru_bi"], side=gather.take(33))
    h, yrin = _lru_scan_fwd(r, i, rxc, proj, small["lru_lambda"], side=gather.take(53))

    w_ap = _natural(gather.get("w_attn_proj"), "rows")
    w_rp = _natural(gather.get("w_rnn_proj"), "rows")
    y_attn = _mm(ot, w_ap, ta=True, tm=1024, tn=1024, tk=D, name="attn_proj", side=gather.take(22))
    y_rnn = _mm(yrin, w_rp, tm=1024, tn=1024, tk=D_RNN, name="rnn_proj", side=gather.take(27))
    mixin = _gate_fwd(y_attn, y_rnn, proj, small["b_gate"], side=gather.take(25))
    w_out = _natural(gather.get("w_out"), "rows")
    mix = _mm(mixin, w_out, tm=1024, tn=1024, tk=D, name="mix_out", side=gather.take(22))
    x1, x1b, xhat1, rstd1 = _ln_fwd(x2, mix, small["ln1_g"], small["ln1_b"], side=gather.take(23))

    w_up = gather.get("ffn_w_up")
    up = _mm(x1b, w_up, tm=1024, tn=768, tk=D, b_block=768, name="ffn_up", side=gather.take(58))
    w_gate = gather.get("ffn_w_gate")
    gpre = _mm(x1b, w_gate, tm=1024, tn=768, tk=D, b_block=768, name="ffn_gate", side=gather.take(58))
    fconv_w = _natural(gather.get("ffn_conv_w"), "cols")
    fin = _ffn_act_fwd(up, gpre, fconv_w, small["ffn_conv_b"], side=gather.take())
    w_down = _natural(gather.get("ffn_w_down"), "rows")
    f = _mm(fin, w_down, tm=1024, tn=1024, tk=2048, name="ffn_down")
    loss, dpre2, dpre2b, d_ln2_g, d_ln2_b = _ln_loss_bwd(x1, f, small["ln2_g"], small["ln2_b"], target)

    grads = {"ln2_g": d_ln2_g, "ln2_b": d_ln2_b}
    both = (F32, BF16)
    g32, g16 = _mm(fin, dpre2b, ta=True, tm=1024, tn=1024, tk=S, out_dtype=both, name="d_ffn_w_down")
    scatter.add_blocks("ffn_w_down", _blocks(g32, "rows"), _blocks(g16, "rows"))
    dfin = _mm(dpre2b, w_down, tb=True, tm=1024, tn=1024, tk=D, name="d_fin", side=scatter.take(57))
    dup, dgpre, grads["ffn_conv_w"], grads["ffn_conv_b"] = _ffn_act_bwd(
        dfin, up, gpre, fconv_w, small["ffn_conv_b"], side=scatter.take(85))
    g32, g16 = _mm(x1b, dup, ta=True, tm=1024, tn=768, tk=S, out_dtype=both, out_block=768, name="d_ffn_w_up",
                   side=scatter.take(57))
    scatter.add_blocks("ffn_w_up", g32, g16)
    g32, g16 = _mm(x1b, dgpre, ta=True, tm=1024, tn=768, tk=S, out_dtype=both, out_block=768, name="d_ffn_w_gate",
                   side=scatter.take(56))
    scatter.add_blocks("ffn_w_gate", g32, g16)
    dx1 = _mm(dup, w_up, tb=True, tm=1024, tn=1024, tk=768, b_block=768, name="d_x1_up", side=scatter.take(68))
    dx1 = _mm(dgpre, w_gate, tb=True, tm=1024, tn=1024, tk=768, b_block=768, add=dx1, name="d_x1_gate",
              side=scatter.take(70))
    dpre1, dpre1b, grads["ln1_g"], grads["ln1_b"] = _ln_bwd(dx1, dpre2, xhat1, rstd1, small["ln1_g"],
                                                            side=scatter.take(24))

    g32, g16 = _mm(mixin, dpre1b, ta=True, tm=1024, tn=1024, tk=S, out_dtype=both, name="d_w_out",
                   side=scatter.take(26))
    scatter.add_blocks("w_out", _blocks(g32, "rows"), _blocks(g16, "rows"))
    dmix = _mm(dpre1b, w_out, tb=True, tm=1024, tn=1024, tk=D, name="d_mixin", side=scatter.take(22))
    dya, dyr, dgl_a, dgl_r, db_a, db_r = _gate_bwd(dmix, y_attn, y_rnn, proj, small["b_gate"], side=scatter.take(36))
    grads["b_gate"] = jnp.concatenate([db_a, db_r], axis=1)
    g32, g16 = _mm(ot, dya, tm=1024, tn=1024, tk=S, out_dtype=both, name="d_w_attn_proj", side=scatter.take(38))
    scatter.add_blocks("w_attn_proj", _blocks(g32, "rows"), _blocks(g16, "rows"))
    g32, g16 = _mm(yrin, dyr, ta=True, tm=1280, tn=1024, tk=S, out_dtype=both, name="d_w_rnn_proj",
                   side=scatter.take(27))
    scatter.add_blocks("w_rnn_proj", _blocks(g32, "rows"), _blocks(g16, "rows"))
    dot_ = _mm(w_ap, dya, tb=True, tm=1024, tn=1024, tk=D, out_dtype=MXU_DTYPE, name="d_o", side=scatter.take(22))
    dyrin = _mm(dyr, w_rp, tb=True, tm=1024, tn=1280, tk=D, name="d_yrin", side=scatter.take(27))

    dry, dzr, dzi, drxc_in, grads["lru_ba"], grads["lru_bi"], grads["lru_lambda"] = _lru_scan_bwd(
        dyrin, proj, h, r, i, rxc, small["lru_lambda"], side=scatter.take(94))
    grads["lru_wa"], grads["lru_wi"] = _lru_gate_wgrad(rxc, dzr, dzi, side=scatter.take(22))
    drxc = _lru_gate_xgrad(dzr, dzi, small["lru_wa"], small["lru_wi"], drxc_in, side=scatter.take(33))
    drx, grads["rnn_conv_w"], grads["rnn_conv_b"] = _conv_bwd(drxc, proj, OFF_RX, rconv_w, tc=512,
                                                             name="rnn_conv_bwd", side=scatter.take(29))

    dqt, dk, dv, dsink = _attn_bwd(qt, kp, kt, vp, sink_row, dot_.reshape(N_KV, GROUP, HEAD_DIM, S),
                                   side=scatter.take(100))
    grads["attn_sinks"] = dsink.reshape(1, N_KV * GROUP)
    dproj = jnp.concatenate([
        dqt.reshape(D, S).T,
        _heads_minor(dk[:, BLOCK:, :]).astype(MXU_DTYPE),
        _heads_minor(dv[:, BLOCK:, :]).astype(MXU_DTYPE),
        drx, dry, dgl_a, dgl_r], axis=1)
    g32, g16 = _mm(xb, dproj, ta=True, tm=1024, tn=512, tk=S, out_dtype=both, name="d_w_in",
                   side=_join(scatter.take(110), sync.begin(loss, grads)))
    scatter.add_cols("w_in", g32, g16)
    scatter.flush_pairs("pairs_w_in")
    dx = _mm(dproj, w_in, tb=True, tm=1024, tn=1024, tk=512, add=dpre1, add_scale=ALPHA, name="d_x",
             side=_join(scatter.take(300), sync.middle()))
    return dx


SHARDED = (
    ("w_in", "cols", 128), ("w_attn_proj", "rows", 128), ("w_rnn_proj", "rows", 160), ("w_out", "rows", 128),
    ("ffn_w_up", "cols", 256), ("ffn_w_gate", "cols", 256), ("ffn_w_down", "rows", 256),
)
SMALL_REPLICATED = ("b_gate", "rnn_conv_b", "lru_wa", "lru_ba", "lru_wi", "lru_bi", "lru_lambda", "attn_sinks",
                    "ln1_g", "ln1_b", "ffn_conv_b", "ln2_g", "ln2_b")
SMALL_SHARDED = ("rnn_conv_w", "ffn_conv_w")
SMALL_MATS = ("lru_wa", "lru_wi")
WEIGHTS = ("w_in", "b_gate", "rnn_conv_w", "rnn_conv_b", "lru_wa", "lru_ba", "lru_wi", "lru_bi", "lru_lambda",
           "attn_sinks", "w_attn_proj", "w_rnn_proj", "w_out", "ln1_g", "ln1_b", "ffn_w_up", "ffn_w_gate",
           "ffn_conv_w", "ffn_conv_b", "ffn_w_down", "ln2_g", "ln2_b")


def kernel(x, w_in, b_gate, rnn_conv_w, rnn_conv_b, lru_wa, lru_ba, lru_wi, lru_bi, lru_lambda, attn_sinks, w_attn_proj, w_rnn_proj, w_out, ln1_g, ln1_b, ffn_w_up, ffn_w_gate, ffn_conv_w, ffn_conv_b, ffn_w_down, ln2_g, ln2_b, loss_target, m_w_in, m_b_gate, m_rnn_conv_w, m_rnn_conv_b, m_lru_wa, m_lru_ba, m_lru_wi, m_lru_bi, m_lru_lambda, m_attn_sinks, m_w_attn_proj, m_w_rnn_proj, m_w_out, m_ln1_g, m_ln1_b, m_ffn_w_up, m_ffn_w_gate, m_ffn_conv_w, m_ffn_conv_b, m_ffn_w_down, m_ln2_g, m_ln2_b, v_w_in, v_b_gate, v_rnn_conv_w, v_rnn_conv_b, v_lru_wa, v_lru_ba, v_lru_wi, v_lru_bi, v_lru_lambda, v_attn_sinks, v_w_attn_proj, v_w_rnn_proj, v_w_out, v_ln1_g, v_ln1_b, v_ffn_w_up, v_ffn_w_gate, v_ffn_conv_w, v_ffn_conv_b, v_ffn_w_down, v_ln2_g, v_ln2_b):
    given = dict(locals())
    wsh = {n: given[n][0] for n in WEIGHTS}
    msh = {n: given["m_" + n][0] for n in WEIGHTS}
    vsh = {n: given["v_" + n][0] for n in WEIGHTS}
    m_given = {n: given["m_" + n] for n in WEIGHTS}
    v_given = {n: given["v_" + n] for n in WEIGHTS}
    me = 4 * lax.axis_index("x") + 2 * lax.axis_index("y") + lax.axis_index("c")

    order = ("w_in", "rnn_conv_w", "ffn_conv_w", "w_attn_proj", "w_rnn_proj", "w_out", "ffn_w_up", "ffn_w_gate",
             "ffn_w_down")
    gather = _Gather({n: wsh[n] if n in SMALL_SHARDED else wsh[n].astype(MXU_DTYPE) for n in order})
    _run_side(gather.take(through="ffn_conv_w"), "gather_first")
    small = {n: given[n] for n in SMALL_REPLICATED}
    small["lru_wa"] = _block_diag(wsh["lru_wa"])
    small["lru_wi"] = _block_diag(wsh["lru_wi"])
    scatter = _Scatter(me, jnp.stack([_index(_relative(k)) for k in FAR]).astype(jnp.int32))

    vec_names = tuple(n for n in SMALL_REPLICATED if n not in SMALL_MATS) + SMALL_SHARDED
    sync = _SmallSync(vec_names, SMALL_MATS)
    dx = _forward_backward(x[0], loss_target[0], small, gather, scatter, sync)

    loss_total, g_small, mat_sums = sync.end()
    loss_total = loss_total.reshape(())
    for n in SMALL_SHARDED:
        width = wsh[n].shape[1]
        g_small[n] = lax.dynamic_slice_in_dim(g_small[n], me * width, width, axis=1)
    g_small = {n: g_small[n].reshape(given[n].shape) for n in vec_names}
    out = {}
    results = _adamw_many(*[[d[n] for n in vec_names] for d in (given, m_given, v_given, g_small)])
    for n, delta, nm, nv in zip(vec_names, *results):
        out[n] = (g_small[n], delta, nm, nv)
    for n in SMALL_MATS:
        g = mat_sums[n].reshape(given[n].shape)
        out[n] = (g, *_adamw_blocks(given[n], m_given[n], v_given[n], g, name="adamw_" + n))

    tile_rows = {n: tr for n, _, tr in SHARDED}
    for n in list(scatter.sends):
        own, pair, far = scatter.get(n)
        res = _reduce_adamw(wsh[n], msh[n], vsh[n], own, pair, far, tr=tile_rows[n], name="adamw_" + n)
        out[n] = tuple(r[None] for r in res)

    outputs = [loss_total, dx[None]]
    for kind in range(4):
        outputs += [out[n][kind] for n in WEIGHTS]
    return tuple(outputs)
```

```python
import math

import jax
import jax.numpy as jnp
from jax import lax
from jax.experimental import pallas as pl
from jax.experimental.pallas import tpu as pltpu

F32 = jnp.float32
BF16 = jnp.bfloat16
MXU_DTYPE = jnp.bfloat16

N_DEV = 8
S = 2048
D = 2048
HEAD_DIM = 64
N_KV = 4
GROUP = 8
BLOCK = 128
D_KV = N_KV * HEAD_DIM
D_RNN = 2560
RNN_GROUP = 640
N_RNN_GROUPS = D_RNN // RNN_GROUP
RNN_BLOCK_W = 160
RNN_CONV_W = 4
LRU_C = 8.0
D_FF = 6144
FFN_CONV_W = 3
D_IN = 11776
OFF_K = 2048
OFF_V = 2304
OFF_RX = 2560
OFF_RY = 5120
OFF_GA = 7680
OFF_GR = 9728
LN_EPS = 1e-5
ALPHA = 2.0 ** 0.25
ADAM_LR = 0.001
ADAM_B1 = 0.9
ADAM_B2 = 0.999
ADAM_EPS = 1e-08
ADAM_WD = 0.01
ADAM_STEP = 10
NEG = -1e30
VMEM_LIMIT = 56 * 1024 * 1024
MESH = pl.DeviceIdType.MESH
GELU_C = math.sqrt(2.0 / math.pi)


def _cparams(*sem):
    return pltpu.CompilerParams(dimension_semantics=sem or None, vmem_limit_bytes=VMEM_LIMIT)


def _call(body, *, name, grid, in_specs, out_specs, out_shape, operands, semantics, scratch_shapes=(), side=None):
    single = not isinstance(out_shape, (list, tuple))
    out_shape = [out_shape] if single else list(out_shape)
    out_specs = [out_specs] if single else list(out_specs)
    in_specs = list(in_specs)
    scratch_shapes = list(scratch_shapes)
    if side is None:
        res = pl.pallas_call(
            body, name=name, grid=grid, in_specs=in_specs, out_specs=out_specs, out_shape=out_shape,
            scratch_shapes=scratch_shapes, compiler_params=_cparams(*semantics))(*operands)
        return res[0] if single else res
    n_in, n_out, n_scr = len(in_specs), len(out_shape), len(scratch_shapes)
    s_in, s_out = len(side.operands), len(side.out_shape)
    hbm = pl.BlockSpec(memory_space=pltpu.HBM)

    def with_copies(*refs):
        core_in, side_in = refs[:n_in], refs[n_in:n_in + s_in]
        o0 = n_in + s_in
        core_out, side_out = refs[o0:o0 + n_out], refs[o0 + n_out:o0 + n_out + s_out]
        c0 = o0 + n_out + s_out
        core_scr, sems = refs[c0:c0 + n_scr], refs[c0 + n_scr:]
        first, last = None, None
        for d, size in enumerate(grid):
            at_start, at_end = pl.program_id(d) == 0, pl.program_id(d) == size - 1
            first = at_start if first is None else first & at_start
            last = at_end if last is None else last & at_end

        @pl.when(first)
        def _():
            side.start(side_in, side_out, sems)

        body(*core_in, *core_out, *core_scr)

        @pl.when(last)
        def _():
            side.finish(side_in, side_out, sems)

    res = pl.pallas_call(
        with_copies, name=name, grid=grid,
        in_specs=in_specs + [hbm] * s_in, out_specs=out_specs + [hbm] * s_out,
        out_shape=out_shape + list(side.out_shape),
        scratch_shapes=scratch_shapes + list(side.sems),
        input_output_aliases={n_in + i: n_out + o for i, o in side.aliases.items()},
        compiler_params=_cparams(*(("arbitrary",) * len(grid))))(*operands, *side.operands)
    side.done(res[n_out:])
    return res[0] if single else res[:n_out]


def _run_side(side, name):
    def body(*refs):
        s_in, s_out = len(side.operands), len(side.out_shape)
        side.start(refs[:s_in], refs[s_in:s_in + s_out], refs[s_in + s_out:])
        side.finish(refs[:s_in], refs[s_in:s_in + s_out], refs[s_in + s_out:])

    hbm = pl.BlockSpec(memory_space=pltpu.HBM)
    res = pl.pallas_call(
        body, name=name, in_specs=[hbm] * len(side.operands), out_specs=[hbm] * len(side.out_shape),
        out_shape=list(side.out_shape), scratch_shapes=list(side.sems),
        input_output_aliases=dict(side.aliases))(*side.operands)
    side.done(res)


def _gelu(x):
    x2 = x * x
    t = jnp.tanh(GELU_C * (x + 0.044715 * x * x2))
    g = 0.5 * x * (1.0 + t)
    dg = 0.5 * (1.0 + t) + 0.5 * x * (1.0 - t * t) * (GELU_C * (1.0 + 3.0 * 0.044715 * x2))
    return g, dg


def _sigmoid(x):
    return 1.0 / (1.0 + jnp.exp(-x))


def _softplus(x):
    z = jnp.exp(-jnp.abs(x))
    small = z * (1.0 - z * (0.5 - z * (1.0 / 3.0 - 0.25 * z)))
    return jnp.maximum(x, 0.0) + jnp.where(z < 0.02, small, jnp.log(1.0 + z))


def _one_minus_exp(x):
    series = -x * (1.0 + x * (0.5 + x * (1.0 / 6.0 + x * (1.0 / 24.0))))
    return jnp.where(x > -0.03, series, 1.0 - jnp.exp(x))


def _colsum(v):
    return jnp.sum(v, axis=0, keepdims=True)


def _mm(a, b, *, tm, tn, tk, name, ta=False, tb=False, out_dtype=F32, b_block=None, out_block=None, add=None,
        add_scale=1.0, side=None):
    out_dtypes = out_dtype if isinstance(out_dtype, tuple) else (out_dtype,)
    if ta:
        k_dim, m_dim = a.shape
    else:
        m_dim, k_dim = a.shape
    if b_block is None:
        n_dim = b.shape[0] if tb else b.shape[1]
    else:
        n_dim = b.shape[1] if tb else b.shape[0] * b_block
    assert m_dim % tm == 0 and n_dim % tn == 0 and k_dim % tk == 0, (name, m_dim, n_dim, k_dim)
    nk = k_dim // tk
    dims = (((0 if ta else 1,), (1 if tb else 0,)), ((), ()))
    has_add = add is not None

    def body(*refs):
        a_ref, b_ref = refs[0], refs[1]
        add_ref = refs[2] if has_add else None
        first_out = 3 if has_add else 2
        o_refs = refs[first_out:first_out + len(out_dtypes)]

        def product():
            return lax.dot_general(a_ref[...].astype(MXU_DTYPE), b_ref[...].astype(MXU_DTYPE), dims,
                                   preferred_element_type=F32)

        def finish(acc):
            if has_add:
                acc = acc + add_scale * add_ref[...]
            for o_ref in o_refs:
                o_ref[...] = acc.astype(o_ref.dtype)

        if nk == 1:
            finish(product())
        else:
            acc_ref = refs[-1]
            k = pl.program_id(2)

            @pl.when(k == 0)
            def _():
                acc_ref[...] = jnp.zeros_like(acc_ref)

            acc_ref[...] += product()

            @pl.when(k == nk - 1)
            def _():
                finish(acc_ref[...])

    if ta:
        a_spec = pl.BlockSpec((tk, tm), lambda i, j, k: (k, i))
    else:
        a_spec = pl.BlockSpec((tm, tk), lambda i, j, k: (i, k))
    if b_block is None:
        if tb:
            b_spec = pl.BlockSpec((tn, tk), lambda i, j, k: (j, k))
        else:
            b_spec = pl.BlockSpec((tk, tn), lambda i, j, k: (k, j))
    elif tb:
        assert b_block % tk == 0
        b_spec = pl.BlockSpec((None, tn, tk), lambda i, j, k: ((k * tk) // b_block, j, ((k * tk) % b_block) // tk))
    else:
        assert b_block % tn == 0
        b_spec = pl.BlockSpec((None, tk, tn), lambda i, j, k: ((j * tn) // b_block, k, ((j * tn) % b_block) // tn))
    in_specs = [a_spec, b_spec]
    operands = [a, b]
    if has_add:
        in_specs.append(pl.BlockSpec((tm, tn), lambda i, j, k: (i, j)))
        operands.append(add)
    if out_block is None:
        out_spec = pl.BlockSpec((tm, tn), lambda i, j, k: (i, j))
        out_dims = (m_dim, n_dim)
    else:
        assert out_block % tn == 0
        out_spec = pl.BlockSpec((None, tm, tn), lambda i, j, k: ((j * tn) // out_block, i, ((j * tn) % out_block) // tn))
        out_dims = (n_dim // out_block, m_dim, out_block)
    res = _call(
        body,
        name=name,
        grid=(m_dim // tm, n_dim // tn, nk),
        in_specs=in_specs,
        out_specs=[out_spec] * len(out_dtypes),
        out_shape=[jax.ShapeDtypeStruct(out_dims, dt) for dt in out_dtypes],
        scratch_shapes=[pltpu.VMEM((tm, tn), F32)] if nk > 1 else [],
        semantics=("parallel", "parallel", "arbitrary"),
        operands=tuple(operands),
        side=side,
    )
    return res if isinstance(out_dtype, tuple) else res[0]


def _attn_bias(bias_ref, h):
    key = lax.broadcasted_iota(jnp.int32, (2 * BLOCK, GROUP * BLOCK), 0)
    col = lax.broadcasted_iota(jnp.int32, (2 * BLOCK, GROUP * BLOCK), 1)
    dist = BLOCK + (col & (BLOCK - 1)) - key
    head = h * GROUP + (col >> 7) + 1
    slope = jnp.exp(head.astype(F32) * (-0.25 * math.log(2.0)))
    bias = jnp.where((dist >= 0) & (dist < BLOCK), -slope * dist.astype(F32), NEG)
    bias_ref[1] = bias
    bias_ref[0] = jnp.where(key < BLOCK, NEG, bias)


def _attn_probs(kb, qt, bias, sink):
    s = jnp.dot(kb, qt, preferred_element_type=F32) * (HEAD_DIM ** -0.5) + bias
    m = jnp.maximum(jnp.max(s, axis=0, keepdims=True), sink)
    e = jnp.exp(s - m)
    e_sink = jnp.exp(sink - m)
    inv = 1.0 / (jnp.sum(e, axis=0, keepdims=True) + e_sink)
    return e * inv, e_sink * inv


def _heads_on_lanes(ref, r0):
    return jnp.concatenate([ref[g, :, pl.ds(r0, BLOCK)] for g in range(GROUP)], axis=1)


def _attn_fwd(qt, kp, vt, sink_row, side=None):
    cols = GROUP * BLOCK

    def body(q_ref, k_ref, vt_ref, sink_ref, o_ref, bias_ref):
        _attn_bias(bias_ref, pl.program_id(0))
        sink = sink_ref[...]

        def step(n, carry):
            r0 = pl.multiple_of(n * BLOCK, BLOCK)
            p, _ = _attn_probs(k_ref[pl.ds(r0, 2 * BLOCK), :], _heads_on_lanes(q_ref, r0),
                               bias_ref[jnp.minimum(n, 1)], sink)
            o = jnp.dot(vt_ref[:, pl.ds(r0, 2 * BLOCK)], p.astype(MXU_DTYPE), preferred_element_type=F32)
            for g in range(GROUP):
                o_ref[g, :, pl.ds(r0, BLOCK)] = o[:, g * BLOCK:(g + 1) * BLOCK].astype(o_ref.dtype)
            return carry

        lax.fori_loop(0, S // BLOCK, step, 0)

    hm = pl.BlockSpec((None, GROUP, HEAD_DIM, S), lambda h: (h, 0, 0, 0))
    return _call(
        body,
        name="attn_fwd",
        grid=(N_KV,),
        in_specs=[
            hm,
            pl.BlockSpec((None, BLOCK + S, HEAD_DIM), lambda h: (h, 0, 0)),
            pl.BlockSpec((None, HEAD_DIM, BLOCK + S), lambda h: (h, 0, 0)),
            pl.BlockSpec((None, 1, cols), lambda h: (h, 0, 0)),
        ],
        out_specs=hm,
        out_shape=jax.ShapeDtypeStruct((N_KV, GROUP, HEAD_DIM, S), MXU_DTYPE),
        scratch_shapes=[pltpu.VMEM((2, 2 * BLOCK, cols), F32)],
        semantics=("parallel",),
        operands=(qt, kp, vt, sink_row),
        side=side,
    )


def _attn_bwd(qt, kp, kt, vp, sink_row, dot_, side=None):
    cols = GROUP * BLOCK

    def body(q_ref, k_ref, kt_ref, v_ref, sink_ref, do_ref, dq_ref, dk_ref, dv_ref, dsink_ref, bias_ref):
        _attn_bias(bias_ref, pl.program_id(0))
        sink = sink_ref[...]
        dk_ref[...] = jnp.zeros_like(dk_ref)
        dv_ref[...] = jnp.zeros_like(dv_ref)
        nt = (((1,), (1,)), ((), ()))

        def step(n, sink_acc):
            r0 = pl.multiple_of(n * BLOCK, BLOCK)
            band = pl.ds(r0, 2 * BLOCK)
            qn = _heads_on_lanes(q_ref, r0)
            don = _heads_on_lanes(do_ref, r0)
            p, p_sink = _attn_probs(k_ref[band, :], qn, bias_ref[jnp.minimum(n, 1)], sink)
            dp = jnp.dot(v_ref[band, :], don, preferred_element_type=F32)
            delta = jnp.sum(p * dp, axis=0, keepdims=True)
            ds = (p * (dp - delta) * (HEAD_DIM ** -0.5)).astype(MXU_DTYPE)
            dq = jnp.dot(kt_ref[:, band], ds, preferred_element_type=F32)
            for g in range(GROUP):
                dq_ref[g, :, pl.ds(r0, BLOCK)] = dq[:, g * BLOCK:(g + 1) * BLOCK].astype(dq_ref.dtype)
            dk_ref[band, :] += lax.dot_general(ds, qn, nt, preferred_element_type=F32)
            dv_ref[band, :] += lax.dot_general(p.astype(MXU_DTYPE), don, nt, preferred_element_type=F32)
            return sink_acc - p_sink * delta

        sink_acc = lax.fori_loop(0, S // BLOCK, step, jnp.zeros((1, cols), F32))
        for g in range(GROUP):
            dsink_ref[g:g + 1, :] = jnp.sum(sink_acc[:, g * BLOCK:(g + 1) * BLOCK], axis=1, keepdims=True)

    hm = pl.BlockSpec((None, GROUP, HEAD_DIM, S), lambda h: (h, 0, 0, 0))
    kv = pl.BlockSpec((None, BLOCK + S, HEAD_DIM), lambda h: (h, 0, 0))
    return _call(
        body,
        name="attn_bwd",
        grid=(N_KV,),
        in_specs=[hm, kv, pl.BlockSpec((None, HEAD_DIM, BLOCK + S), lambda h: (h, 0, 0)), kv,
                  pl.BlockSpec((None, 1, cols), lambda h: (h, 0, 0)), hm],
        out_specs=[hm, kv, kv, pl.BlockSpec((None, GROUP, 1), lambda h: (h, 0, 0))],
        out_shape=[
            jax.ShapeDtypeStruct((N_KV, GROUP, HEAD_DIM, S), MXU_DTYPE),
            jax.ShapeDtypeStruct((N_KV, BLOCK + S, HEAD_DIM), F32),
            jax.ShapeDtypeStruct((N_KV, BLOCK + S, HEAD_DIM), F32),
            jax.ShapeDtypeStruct((N_KV, GROUP, 1), F32),
        ],
        scratch_shapes=[pltpu.VMEM((2, 2 * BLOCK, cols), F32)],
        semantics=("parallel",),
        operands=(qt, kp, kt, vp, sink_row, dot_),
        side=side,
    )


PAD = 8
CHUNK = 256


def _past_taps(xpad_ref, r0, width):
    ext = xpad_ref[pl.ds(r0, CHUNK + PAD), :]
    taps = []
    for k in range(width):
        back = width - 1 - k
        taps.append((ext if back == 0 else pltpu.roll(ext, back, 0))[PAD:, :])
    return taps


def _future_taps(xpad_ref, r0, width):
    ext = xpad_ref[pl.ds(r0, CHUNK + PAD), :]
    taps = []
    for ahead in range(width):
        taps.append((ext if ahead == 0 else pltpu.roll(ext, CHUNK + PAD - ahead, 0))[:CHUNK, :])
    return taps


def _conv_fwd(src, col0, w, b, *, tc, name, side=None):
    width, c_dim = w.shape

    def body(x_ref, w_ref, b_ref, o_ref, xpad_ref):
        xpad_ref[pl.ds(0, PAD), :] = jnp.zeros((PAD, tc), F32)
        xpad_ref[pl.ds(PAD, S), :] = x_ref[...]
        wv = w_ref[...]
        bv = b_ref[...]

        def step(ci, carry):
            r0 = pl.multiple_of(ci * CHUNK, CHUNK)
            taps = _past_taps(xpad_ref, r0, width)
            y = bv + taps[0] * wv[0:1, :]
            for k in range(1, width):
                y = y + taps[k] * wv[k:k + 1, :]
            o_ref[pl.ds(r0, CHUNK), :] = y
            return carry

        lax.fori_loop(0, S // CHUNK, step, 0)

    return _call(
        body,
        name=name,
        grid=(c_dim // tc,),
        in_specs=[
            pl.BlockSpec((S, tc), lambda j: (0, col0 // tc + j)),
            pl.BlockSpec((width, tc), lambda j: (0, j)),
            pl.BlockSpec((1, tc), lambda j: (0, j)),
        ],
        out_specs=pl.BlockSpec((S, tc), lambda j: (0, j)),
        out_shape=jax.ShapeDtypeStruct((S, c_dim), F32),
        scratch_shapes=[pltpu.VMEM((S + PAD, tc), F32)],
        semantics=("parallel",),
        operands=(src, w, b),
        side=side,
    )


def _conv_bwd(dy, src, col0, w, *, tc, name, side=None):
    width, c_dim = w.shape

    def body(dy_ref, x_ref, w_ref, dx_ref, dw_ref, db_ref, xpad_ref, dpad_ref):
        xpad_ref[pl.ds(0, PAD), :] = jnp.zeros((PAD, tc), F32)
        xpad_ref[pl.ds(PAD, S), :] = x_ref[...]
        dpad_ref[pl.ds(0, S), :] = dy_ref[...]
        dpad_ref[pl.ds(S, PAD), :] = jnp.zeros((PAD, tc), F32)
        wv = w_ref[...]

        def step(ci, acc):
            r0 = pl.multiple_of(ci * CHUNK, CHUNK)
            past = _past_taps(xpad_ref, r0, width)
            ahead = _future_taps(dpad_ref, r0, width)
            d = ahead[0]
            dx = d * wv[width - 1:width, :]
            for j in range(1, width):
                dx = dx + ahead[j] * wv[width - 1 - j:width - j, :]
            dx_ref[pl.ds(r0, CHUNK), :] = dx.astype(dx_ref.dtype)
            return tuple(acc[k] + _colsum(past[k] * d) for k in range(width)) + (acc[width] + _colsum(d),)

        zero = jnp.zeros((1, tc), F32)
        acc = lax.fori_loop(0, S // CHUNK, step, (zero,) * (width + 1))
        for k in range(width):
            dw_ref[k:k + 1, :] = acc[k]
        db_ref[...] = acc[width]

    return _call(
        body,
        name=name,
        grid=(c_dim // tc,),
        in_specs=[
            pl.BlockSpec((S, tc), lambda j: (0, j)),
            pl.BlockSpec((S, tc), lambda j: (0, col0 // tc + j)),
            pl.BlockSpec((width, tc), lambda j: (0, j)),
        ],
        out_specs=[
            pl.BlockSpec((S, tc), lambda j: (0, j)),
            pl.BlockSpec((width, tc), lambda j: (0, j)),
            pl.BlockSpec((1, tc), lambda j: (0, j)),
        ],
        out_shape=[
            jax.ShapeDtypeStruct((S, c_dim), MXU_DTYPE),
            jax.ShapeDtypeStruct((width, c_dim), F32),
            jax.ShapeDtypeStruct((1, c_dim), F32),
        ],
        scratch_shapes=[pltpu.VMEM((S + PAD, tc), F32), pltpu.VMEM((S + PAD, tc), F32)],
        semantics=("parallel",),
        operands=(dy, src, w),
        side=side,
    )


SCAN_TC = 256


def _lru_gates(rxc, wa, wi, ba, bi, side=None):
    tm = 512

    def body(x_ref, wa_ref, wi_ref, ba_ref, bi_ref, r_ref, i_ref):
        xv = x_ref[...].astype(MXU_DTYPE)
        r_ref[...] = _sigmoid(jnp.dot(xv, wa_ref[...].astype(MXU_DTYPE), preferred_element_type=F32) + ba_ref[...])
        i_ref[...] = _sigmoid(jnp.dot(xv, wi_ref[...].astype(MXU_DTYPE), preferred_element_type=F32) + bi_ref[...])

    x_spec = pl.BlockSpec((tm, RNN_GROUP), lambda g, i: (i, g))
    w_spec = pl.BlockSpec((None, RNN_GROUP, RNN_GROUP), lambda g, i: (g, 0, 0))
    b_spec = pl.BlockSpec((1, RNN_GROUP), lambda g, i: (0, g))
    return _call(
        body,
        name="lru_gates",
        grid=(N_RNN_GROUPS, S // tm),
        in_specs=[x_spec, w_spec, w_spec, b_spec, b_spec],
        out_specs=[x_spec, x_spec],
        out_shape=[jax.ShapeDtypeStruct((S, D_RNN), F32)] * 2,
        semantics=("parallel", "parallel"),
        operands=(rxc, wa, wi, ba, bi),
        side=side,
    )


def _scan_down(a, u, row):
    for d in (1, 2, 4):
        a_s = jnp.where(row >= d, pltpu.roll(a, d, 0), 1.0)
        u_s = jnp.where(row >= d, pltpu.roll(u, d, 0), 0.0)
        u = a * u_s + u
        a = a * a_s
    return a, u


def _scan_up(a, u, row):
    for d in (1, 2, 4):
        a_s = jnp.where(row < 8 - d, pltpu.roll(a, 8 - d, 0), 1.0)
        u_s = jnp.where(row < 8 - d, pltpu.roll(u, 8 - d, 0), 0.0)
        u = a * u_s + u
        a = a * a_s
    return a, u


def _lru_scan_fwd(r, i, rxc, proj, lam, side=None):
    tc = SCAN_TC

    def body(r_ref, i_ref, x_ref, ry_ref, lam_ref, h_ref, y_ref):
        rate = LRU_C * _softplus(-lam_ref[...])
        row = lax.broadcasted_iota(jnp.int32, (8, tc), 0)

        def step(ci, carry):
            r0 = pl.multiple_of(ci * 16, 16)
            log_a = -rate * r_ref[pl.ds(r0, 16), :]
            a16 = jnp.exp(log_a)
            u16 = jnp.sqrt(_one_minus_exp(2.0 * log_a)) * (i_ref[pl.ds(r0, 16), :] * x_ref[pl.ds(r0, 16), :])
            hs = []
            for half in range(2):
                a_cum, h0 = _scan_down(a16[8 * half:8 * half + 8, :], u16[8 * half:8 * half + 8, :], row)
                h = a_cum * carry + h0
                carry = jnp.broadcast_to(h[7:8, :], (8, tc))
                hs.append(h)
            h16 = jnp.concatenate(hs, axis=0)
            h_ref[pl.ds(r0, 16), :] = h16
            y_ref[pl.ds(r0, 16), :] = (h16 * _gelu(ry_ref[pl.ds(r0, 16), :])[0]).astype(y_ref.dtype)
            return carry

        lax.fori_loop(0, S // 16, step, jnp.zeros((8, tc), F32))

    col = pl.BlockSpec((S, tc), lambda j: (0, j))
    return _call(
        body,
        name="lru_scan_fwd",
        grid=(D_RNN // tc,),
        in_specs=[col, col, col, pl.BlockSpec((S, tc), lambda j: (0, OFF_RY // tc + j)),
                  pl.BlockSpec((1, tc), lambda j: (0, j))],
        out_specs=[col, col],
        out_shape=[jax.ShapeDtypeStruct((S, D_RNN), F32), jax.ShapeDtypeStruct((S, D_RNN), MXU_DTYPE)],
        semantics=("parallel",),
        operands=(r, i, rxc, proj, lam),
        side=side,
    )


def _lru_scan_bwd(dy, proj, h, r, i, rxc, lam, side=None):
    tc = SCAN_TC

    def body(dy_ref, ry_ref, h_ref, r_ref, i_ref, x_ref, lam_ref,
             dry_ref, dzr_ref, dzi_ref, dx_ref, dba_ref, dbi_ref, dlam_ref, a_ref, dh_ref, hp_ref):
        lam_v = lam_ref[...]
        rate = LRU_C * _softplus(-lam_v)
        dlam_scale = LRU_C * _sigmoid(-lam_v)
        row = lax.broadcasted_iota(jnp.int32, (8, tc), 0)
        hp_ref[pl.ds(0, PAD), :] = jnp.zeros((PAD, tc), F32)
        hp_ref[pl.ds(PAD, S), :] = h_ref[...]
        a_ref[pl.ds(S, PAD), :] = jnp.zeros((PAD, tc), F32)

        def prep(ci, carry):
            r0 = pl.multiple_of(ci * CHUNK, CHUNK)
            a_ref[pl.ds(r0, CHUNK), :] = jnp.exp(-rate * r_ref[pl.ds(r0, CHUNK), :])
            ge, dge = _gelu(ry_ref[pl.ds(r0, CHUNK), :])
            dyv = dy_ref[pl.ds(r0, CHUNK), :]
            dh_ref[pl.ds(r0, CHUNK), :] = dyv * ge
            dry_ref[pl.ds(r0, CHUNK), :] = (dyv * h_ref[pl.ds(r0, CHUNK), :] * dge).astype(dry_ref.dtype)
            return carry

        lax.fori_loop(0, S // CHUNK, prep, 0)

        def step(ci, state):
            carry, dba, dbi, dlam = state
            r0 = pl.multiple_of(S - 16 - ci * 16, 16)
            a_ext = a_ref[pl.ds(r0, 24), :]
            a_next = pltpu.roll(a_ext, 23, 0)
            h_prev = pltpu.roll(hp_ref[pl.ds(r0, 24), :], 1, 0)
            dh16 = dh_ref[pl.ds(r0, 16), :]
            gs = [None, None]
            for half in (1, 0):
                lo = 8 * half
                c_cum, g0 = _scan_up(a_next[lo:lo + 8, :], dh16[lo:lo + 8, :], row)
                g = c_cum * carry + g0
                carry = jnp.broadcast_to(g[0:1, :], (8, tc))
                gs[half] = g
            g16 = jnp.concatenate(gs, axis=0)
            a16 = a_ext[0:16, :]
            r16 = r_ref[pl.ds(r0, 16), :]
            i16 = i_ref[pl.ds(r0, 16), :]
            x16 = x_ref[pl.ds(r0, 16), :]
            a2 = a16 * a16
            sq = jnp.sqrt(_one_minus_exp(-2.0 * rate * r16))
            dx_ref[pl.ds(r0, 16), :] = g16 * sq * i16
            dzi = g16 * sq * x16 * i16 * (1.0 - i16)
            dlog_a = g16 * h_prev[8:24, :] * a16 - g16 * i16 * x16 * a2 / sq
            dzr = -rate * dlog_a * r16 * (1.0 - r16)
            dzr_ref[pl.ds(r0, 16), :] = dzr.astype(dzr_ref.dtype)
            dzi_ref[pl.ds(r0, 16), :] = dzi.astype(dzi_ref.dtype)
            return carry, dba + _colsum(dzr), dbi + _colsum(dzi), dlam + _colsum(dlog_a * r16)

        zero = jnp.zeros((1, tc), F32)
        _, dba, dbi, dlam = lax.fori_loop(0, S // 16, step, (jnp.zeros((8, tc), F32), zero, zero, zero))
        dba_ref[...] = dba
        dbi_ref[...] = dbi
        dlam_ref[...] = dlam * dlam_scale

    col = pl.BlockSpec((S, tc), lambda j: (0, j))
    vec = pl.BlockSpec((1, tc), lambda j: (0, j))
    return _call(
        body,
        name="lru_scan_bwd",
        grid=(D_RNN // tc,),
        in_specs=[col, pl.BlockSpec((S, tc), lambda j: (0, OFF_RY // tc + j)), col, col, col, col, vec],
        out_specs=[col, col, col, col, vec, vec, vec],
        out_shape=[jax.ShapeDtypeStruct((S, D_RNN), MXU_DTYPE)] * 3 + [jax.ShapeDtypeStruct((S, D_RNN), F32)]
        + [jax.ShapeDtypeStruct((1, D_RNN), F32)] * 3,
        scratch_shapes=[pltpu.VMEM((S + PAD, tc), F32), pltpu.VMEM((S, tc), F32), pltpu.VMEM((S + PAD, tc), F32)],
        semantics=("parallel",),
        operands=(dy, proj, h, r, i, rxc, lam),
        side=side,
    )


def _lru_gate_wgrad(rxc, dzr, dzi, side=None):
    def body(x_ref, dzr_ref, dzi_ref, dwa_ref, dwi_ref):
        xv = x_ref[...].astype(MXU_DTYPE)
        dims = (((0,), (0,)), ((), ()))
        dwa_ref[...] = lax.dot_general(xv, dzr_ref[...], dims, preferred_element_type=F32)
        dwi_ref[...] = lax.dot_general(xv, dzi_ref[...], dims, preferred_element_type=F32)

    col = pl.BlockSpec((S, RNN_GROUP), lambda g: (0, g))
    w_spec = pl.BlockSpec((None, RNN_GROUP, RNN_GROUP), lambda g: (g, 0, 0))
    return _call(
        body,
        name="lru_gate_wgrad",
        grid=(N_RNN_GROUPS,),
        in_specs=[col, col, col],
        out_specs=[w_spec, w_spec],
        out_shape=[jax.ShapeDtypeStruct((N_RNN_GROUPS, RNN_GROUP, RNN_GROUP), F32)] * 2,
        semantics=("parallel",),
        operands=(rxc, dzr, dzi),
        side=side,
    )


def _lru_gate_xgrad(dzr, dzi, wa, wi, dx_in, side=None):
    tm = 512

    def body(dzr_ref, dzi_ref, wa_ref, wi_ref, dx_ref, o_ref):
        dims = (((1,), (1,)), ((), ()))
        o_ref[...] = (dx_ref[...]
                      + lax.dot_general(dzr_ref[...], wa_ref[...].astype(MXU_DTYPE), dims, preferred_element_type=F32)
                      + lax.dot_general(dzi_ref[...], wi_ref[...].astype(MXU_DTYPE), dims, preferred_element_type=F32))

    x_spec = pl.BlockSpec((tm, RNN_GROUP), lambda g, i: (i, g))
    w_spec = pl.BlockSpec((None, RNN_GROUP, RNN_GROUP), lambda g, i: (g, 0, 0))
    return _call(
        body,
        name="lru_gate_xgrad",
        grid=(N_RNN_GROUPS, S // tm),
        in_specs=[x_spec, x_spec, w_spec, w_spec, x_spec],
        out_specs=x_spec,
        out_shape=jax.ShapeDtypeStruct((S, D_RNN), F32),
        semantics=("parallel", "parallel"),
        operands=(dzr, dzi, wa, wi, dx_in),
        side=side,
    )


def _gate_fwd(y_attn, y_rnn, proj, b_gate, side=None):
    t = 512

    def body(ya_ref, yr_ref, ga_ref, gr_ref, ba_ref, br_ref, o_ref):
        o_ref[...] = (_sigmoid(ga_ref[...] + ba_ref[...]) * ya_ref[...]
                      + _sigmoid(gr_ref[...] + br_ref[...]) * yr_ref[...]).astype(o_ref.dtype)

    tile = pl.BlockSpec((t, t), lambda i, j: (i, j))
    return _call(
        body,
        name="gate_fwd",
        grid=(S // t, D // t),
        in_specs=[tile, tile,
                  pl.BlockSpec((t, t), lambda i, j: (i, OFF_GA // t + j)),
                  pl.BlockSpec((t, t), lambda i, j: (i, OFF_GR // t + j)),
                  pl.BlockSpec((1, t), lambda i, j: (0, j)),
                  pl.BlockSpec((1, t), lambda i, j: (0, D // t + j))],
        out_specs=tile,
        out_shape=jax.ShapeDtypeStruct((S, D), MXU_DTYPE),
        semantics=("parallel", "parallel"),
        operands=(y_attn, y_rnn, proj, proj, b_gate, b_gate),
        side=side,
    )


def _gate_bwd(dmix, y_attn, y_rnn, proj, b_gate, side=None):
    t = 512

    def body(dm_ref, ya_ref, yr_ref, ga_ref, gr_ref, ba_ref, br_ref,
             dya_ref, dyr_ref, dga_ref, dgr_ref, dba_ref, dbr_ref):
        @pl.when(pl.program_id(1) == 0)
        def _():
            dba_ref[...] = jnp.zeros_like(dba_ref)
            dbr_ref[...] = jnp.zeros_like(dbr_ref)

        dm = dm_ref[...]
        ga = _sigmoid(ga_ref[...] + ba_ref[...])
        gr = _sigmoid(gr_ref[...] + br_ref[...])
        dya_ref[...] = (dm * ga).astype(dya_ref.dtype)
        dyr_ref[...] = (dm * gr).astype(dyr_ref.dtype)
        dga = dm * ya_ref[...] * ga * (1.0 - ga)
        dgr = dm * yr_ref[...] * gr * (1.0 - gr)
        dga_ref[...] = dga.astype(dga_ref.dtype)
        dgr_ref[...] = dgr.astype(dgr_ref.dtype)
        dba_ref[...] += _colsum(dga)
        dbr_ref[...] += _colsum(dgr)

    tile = pl.BlockSpec((t, t), lambda j, i: (i, j))
    vec = pl.BlockSpec((1, t), lambda j, i: (0, j))
    return _call(
        body,
        name="gate_bwd",
        grid=(D // t, S // t),
        in_specs=[tile, tile, tile,
                  pl.BlockSpec((t, t), lambda j, i: (i, OFF_GA // t + j)),
                  pl.BlockSpec((t, t), lambda j, i: (i, OFF_GR // t + j)),
                  vec,
                  pl.BlockSpec((1, t), lambda j, i: (0, D // t + j))],
        out_specs=[tile, tile, tile, tile, vec, vec],
        out_shape=[jax.ShapeDtypeStruct((S, D), MXU_DTYPE)] * 4 + [jax.ShapeDtypeStruct((1, D), F32)] * 2,
        semantics=("parallel", "arbitrary"),
        operands=(dmix, y_attn, y_rnn, proj, proj, b_gate, b_gate),
        side=side,
    )


LN_TM = 256


def _ln_stats(pre):
    mu = jnp.mean(pre, axis=-1, keepdims=True)
    xc = pre - mu
    rstd = lax.rsqrt(jnp.mean(xc * xc, axis=-1, keepdims=True) + LN_EPS)
    return xc * rstd, rstd


def _ln_input_grad(dy, xhat, rstd, g):
    dyg = dy * g
    return rstd * (dyg - jnp.mean(dyg, axis=-1, keepdims=True)
                   - xhat * jnp.mean(dyg * xhat, axis=-1, keepdims=True))


def _ln_fwd(res, branch, g, b, side=None):
    def body(res_ref, br_ref, g_ref, b_ref, y_ref, yb_ref, xhat_ref, rstd_ref):
        xhat, rstd = _ln_stats(ALPHA * res_ref[...] + br_ref[...])
        y = xhat * g_ref[...] + b_ref[...]
        y_ref[...] = y
        yb_ref[...] = y.astype(yb_ref.dtype)
        xhat_ref[...] = xhat
        rstd_ref[...] = rstd

    tile = pl.BlockSpec((LN_TM, D), lambda i: (i, 0))
    vec = pl.BlockSpec((1, D), lambda i: (0, 0))
    return _call(
        body,
        name="ln_fwd",
        grid=(S // LN_TM,),
        in_specs=[tile, tile, vec, vec],
        out_specs=[tile, tile, tile, pl.BlockSpec((LN_TM, 1), lambda i: (i, 0))],
        out_shape=[jax.ShapeDtypeStruct((S, D), F32), jax.ShapeDtypeStruct((S, D), MXU_DTYPE),
                   jax.ShapeDtypeStruct((S, D), F32), jax.ShapeDtypeStruct((S, 1), F32)],
        semantics=("parallel",),
        operands=(res, branch, g, b),
        side=side,
    )


def _ln_bwd(dy_a, dy_b, xhat, rstd, g, side=None):
    def body(da_ref, db_in_ref, xhat_ref, rstd_ref, g_ref, dp_ref, dpb_ref, dg_ref, db_ref):
        @pl.when(pl.program_id(0) == 0)
        def _():
            dg_ref[...] = jnp.zeros_like(dg_ref)
            db_ref[...] = jnp.zeros_like(db_ref)

        dy = da_ref[...] + ALPHA * db_in_ref[...]
        xhat = xhat_ref[...]
        dp = _ln_input_grad(dy, xhat, rstd_ref[...], g_ref[...])
        dp_ref[...] = dp
        dpb_ref[...] = dp.astype(dpb_ref.dtype)
        dg_ref[...] += _colsum(dy * xhat)
        db_ref[...] += _colsum(dy)

    tile = pl.BlockSpec((LN_TM, D), lambda i: (i, 0))
    vec = pl.BlockSpec((1, D), lambda i: (0, 0))
    return _call(
        body,
        name="ln_bwd",
        grid=(S // LN_TM,),
        in_specs=[tile, tile, tile, pl.BlockSpec((LN_TM, 1), lambda i: (i, 0)), vec],
        out_specs=[tile, tile, vec, vec],
        out_shape=[jax.ShapeDtypeStruct((S, D), F32), jax.ShapeDtypeStruct((S, D), MXU_DTYPE),
                   jax.ShapeDtypeStruct((1, D), F32), jax.ShapeDtypeStruct((1, D), F32)],
        semantics=("arbitrary",),
        operands=(dy_a, dy_b, xhat, rstd, g),
        side=side,
    )


def _ln_loss_bwd(res, branch, g, b, target, side=None):
    def body(res_ref, br_ref, g_ref, b_ref, t_ref, loss_ref, dp_ref, dpb_ref, dg_ref, db_ref):
        @pl.when(pl.program_id(0) == 0)
        def _():
            loss_ref[...] = jnp.zeros_like(loss_ref)
            dg_ref[...] = jnp.zeros_like(dg_ref)
            db_ref[...] = jnp.zeros_like(db_ref)

        xhat, rstd = _ln_stats(ALPHA * res_ref[...] + br_ref[...])
        gv = g_ref[...]
        err = xhat * gv + b_ref[...] - t_ref[...]
        loss_ref[...] += (0.5 / D) * jnp.sum(_colsum(err * err), axis=1, keepdims=True)
        dy = err * (1.0 / D)
        dp = _ln_input_grad(dy, xhat, rstd, gv)
        dp_ref[...] = dp
        dpb_ref[...] = dp.astype(dpb_ref.dtype)
        dg_ref[...] += _colsum(dy * xhat)
        db_ref[...] += _colsum(dy)

    tile = pl.BlockSpec((LN_TM, D), lambda i: (i, 0))
    vec = pl.BlockSpec((1, D), lambda i: (0, 0))
    return _call(
        body,
        name="ln_loss_bwd",
        grid=(S // LN_TM,),
        in_specs=[tile, tile, vec, vec, tile],
        out_specs=[pl.BlockSpec((1, 1), lambda i: (0, 0)), tile, tile, vec, vec],
        out_shape=[jax.ShapeDtypeStruct((1, 1), F32), jax.ShapeDtypeStruct((S, D), F32),
                   jax.ShapeDtypeStruct((S, D), MXU_DTYPE),
                   jax.ShapeDtypeStruct((1, D), F32), jax.ShapeDtypeStruct((1, D), F32)],
        semantics=("arbitrary",),
        operands=(res, branch, g, b, target),
        side=side,
    )


FFN_TC = 256


def _ffn_act_fwd(up, gpre, w, b, side=None):
    tc = FFN_TC

    def body(up_ref, x_ref, w_ref, b_ref, o_ref, xpad_ref):
        xpad_ref[pl.ds(0, PAD), :] = jnp.zeros((PAD, tc), F32)
        xpad_ref[pl.ds(PAD, S), :] = x_ref[...]
        wv = w_ref[...]
        bv = b_ref[...]

        def step(ci, carry):
            r0 = pl.multiple_of(ci * CHUNK, CHUNK)
            taps = _past_taps(xpad_ref, r0, FFN_CONV_W)
            gate = bv + taps[0] * wv[0:1, :] + taps[1] * wv[1:2, :] + taps[2] * wv[2:3, :]
            o_ref[pl.ds(r0, CHUNK), :] = (_gelu(gate)[0] * up_ref[pl.ds(r0, CHUNK), :]).astype(o_ref.dtype)
            return carry

        lax.fori_loop(0, S // CHUNK, step, 0)

    col = pl.BlockSpec((S, tc), lambda j: (0, j))
    return _call(
        body,
        name="ffn_act_fwd",
        grid=(D_FF // tc,),
        in_specs=[col, col, pl.BlockSpec((FFN_CONV_W, tc), lambda j: (0, j)), pl.BlockSpec((1, tc), lambda j: (0, j))],
        out_specs=col,
        out_shape=jax.ShapeDtypeStruct((S, D_FF), MXU_DTYPE),
        scratch_shapes=[pltpu.VMEM((S + PAD, tc), F32)],
        semantics=("parallel",),
        operands=(up, gpre, w, b),
        side=side,
    )


def _ffn_act_bwd(dfin, up, gpre, w, b, side=None):
    tc = FFN_TC
    width = FFN_CONV_W

    def body(df_ref, up_ref, x_ref, w_ref, b_ref, dup_ref, dx_ref, dw_ref, db_ref, xpad_ref, dpad_ref):
        xpad_ref[pl.ds(0, PAD), :] = jnp.zeros((PAD, tc), F32)
        xpad_ref[pl.ds(PAD, S), :] = x_ref[...]
        dpad_ref[pl.ds(S, PAD), :] = jnp.zeros((PAD, tc), F32)
        wv = w_ref[...]
        bv = b_ref[...]

        def gate_grad(ci, acc):
            r0 = pl.multiple_of(ci * CHUNK, CHUNK)
            taps = _past_taps(xpad_ref, r0, width)
            gate = bv + taps[0] * wv[0:1, :] + taps[1] * wv[1:2, :] + taps[2] * wv[2:3, :]
            ge, dge = _gelu(gate)
            df = df_ref[pl.ds(r0, CHUNK), :]
            dup_ref[pl.ds(r0, CHUNK), :] = (df * ge).astype(dup_ref.dtype)
            d = df * up_ref[pl.ds(r0, CHUNK), :] * dge
            dpad_ref[pl.ds(r0, CHUNK), :] = d
            return tuple(acc[k] + _colsum(taps[k] * d) for k in range(width)) + (acc[width] + _colsum(d),)

        zero = jnp.zeros((1, tc), F32)
        acc = lax.fori_loop(0, S // CHUNK, gate_grad, (zero,) * (width + 1))
        for k in range(width):
            dw_ref[k:k + 1, :] = acc[k]
        db_ref[...] = acc[width]

        def input_grad(ci, carry):
            r0 = pl.multiple_of(ci * CHUNK, CHUNK)
            ahead = _future_taps(dpad_ref, r0, width)
            dx = ahead[0] * wv[2:3, :] + ahead[1] * wv[1:2, :] + ahead[2] * wv[0:1, :]
            dx_ref[pl.ds(r0, CHUNK), :] = dx.astype(dx_ref.dtype)
            return carry

        lax.fori_loop(0, S // CHUNK, input_grad, 0)

    col = pl.BlockSpec((S, tc), lambda j: (0, j))
    w_spec = pl.BlockSpec((width, tc), lambda j: (0, j))
    vec = pl.BlockSpec((1, tc), lambda j: (0, j))
    return _call(
        body,
        name="ffn_act_bwd",
        grid=(D_FF // tc,),
        in_specs=[col, col, col, w_spec, vec],
        out_specs=[col, col, w_spec, vec],
        out_shape=[jax.ShapeDtypeStruct((S, D_FF), MXU_DTYPE)] * 2
        + [jax.ShapeDtypeStruct((width, D_FF), F32), jax.ShapeDtypeStruct((1, D_FF), F32)],
        scratch_shapes=[pltpu.VMEM((S + PAD, tc), F32), pltpu.VMEM((S + PAD, tc), F32)],
        semantics=("parallel",),
        operands=(dfin, up, gpre, w, b),
        side=side,
    )


def _adamw_update(w, g, m, v):
    m = ADAM_B1 * m + (1.0 - ADAM_B1) * g
    v = ADAM_B2 * v + (1.0 - ADAM_B2) * (g * g)
    m_hat = m / (1.0 - ADAM_B1 ** ADAM_STEP)
    v_hat = v / (1.0 - ADAM_B2 ** ADAM_STEP)
    delta = -ADAM_LR * (m_hat / (jnp.sqrt(v_hat) + ADAM_EPS) + ADAM_WD * w)
    return delta, m, v


def _add_pairs(send, pair, far_index, *, name):
    _, r_dim, c_dim = send.shape
    tr = r_dim // 4

    def body(far_ref, mine_ref, theirs_ref, o_ref):
        o_ref[...] = (mine_ref[...].astype(F32) + theirs_ref[...].astype(F32)).astype(o_ref.dtype)

    return pl.pallas_call(
        body,
        name=name,
        grid_spec=pltpu.PrefetchScalarGridSpec(
            num_scalar_prefetch=1,
            grid=(3, r_dim // tr),
            in_specs=[pl.BlockSpec((None, tr, c_dim), lambda j, i, far: (far[j], i, 0)),
                      pl.BlockSpec((None, tr, c_dim), lambda j, i, far: (1 + j, i, 0))],
            out_specs=pl.BlockSpec((None, tr, c_dim), lambda j, i, far: (j, i, 0)),
        ),
        out_shape=jax.ShapeDtypeStruct((3, r_dim, c_dim), BF16),
        compiler_params=_cparams("parallel", "parallel"),
    )(far_index, send, pair)


def _reduce_adamw(w, m, v, g_own, pair, far, *, tr, name):
    r_dim, c_dim = w.shape

    def body(w_ref, m_ref, v_ref, g_ref, pair_ref, far_ref, grad_ref, delta_ref, nm_ref, nv_ref):
        g = g_ref[...] + pair_ref[...].astype(F32)
        for j in range(3):
            g = g + far_ref[j].astype(F32)
        delta, nm, nv = _adamw_update(w_ref[...], g, m_ref[...], v_ref[...])
        grad_ref[...] = g
        delta_ref[...] = delta
        nm_ref[...] = nm
        nv_ref[...] = nv

    tile = pl.BlockSpec((tr, c_dim), lambda i: (i, 0))
    return _call(
        body,
        name=name,
        grid=(r_dim // tr,),
        in_specs=[tile, tile, tile, tile, pl.BlockSpec((None, tr, c_dim), lambda i: (0, i, 0)),
                  pl.BlockSpec((3, tr, c_dim), lambda i: (0, i, 0))],
        out_specs=[tile] * 4,
        out_shape=[jax.ShapeDtypeStruct((r_dim, c_dim), F32)] * 4,
        semantics=("parallel",),
        operands=(w, m, v, g_own, pair, far),
    )


def _adamw_many(ws, ms, vs, gs):
    n = len(ws)

    def body(*refs):
        for i in range(n):
            delta, nm, nv = _adamw_update(refs[i][...], refs[3 * n + i][...], refs[n + i][...], refs[2 * n + i][...])
            refs[4 * n + i][...] = delta
            refs[5 * n + i][...] = nm
            refs[6 * n + i][...] = nv

    vmem = pl.BlockSpec(memory_space=pltpu.VMEM)
    res = pl.pallas_call(
        body,
        name="adamw_small",
        in_specs=[vmem] * (4 * n),
        out_specs=[vmem] * (3 * n),
        out_shape=[jax.ShapeDtypeStruct(w.shape, F32) for w in ws] * 3,
        compiler_params=pltpu.CompilerParams(vmem_limit_bytes=VMEM_LIMIT),
    )(*ws, *ms, *vs, *gs)
    return res[:n], res[n:2 * n], res[2 * n:]


def _adamw_blocks(w, m, v, g, *, name, side=None):
    per = 2

    def body(w_ref, m_ref, v_ref, g_ref, delta_ref, nm_ref, nv_ref):
        delta, nm, nv = _adamw_update(w_ref[...], g_ref[...], m_ref[...], v_ref[...])
        delta_ref[...] = delta
        nm_ref[...] = nm
        nv_ref[...] = nv

    tile = pl.BlockSpec((1, per) + w.shape[2:], lambda i: (0, i, 0, 0))
    return _call(
        body,
        name=name,
        grid=(w.shape[1] // per,),
        in_specs=[tile] * 4,
        out_specs=[tile] * 3,
        out_shape=[jax.ShapeDtypeStruct(w.shape, F32)] * 3,
        semantics=("parallel",),
        operands=(w, m, v, g),
        side=side,
    )


def _coords():
    return lax.axis_index("x"), lax.axis_index("y"), lax.axis_index("c")


def _flip(coord, bit):
    return 1 - coord if bit else coord


def _relative(k):
    x, y, c = _coords()
    return _flip(x, k & 4), _flip(y, k & 2), _flip(c, k & 1)


def _index(pos):
    return 4 * pos[0] + 2 * pos[1] + pos[2]


FAR = (4, 2, 6)
AG_US_PER_MB = 44.0
RS_US_PER_MB = 36.0
MIN_RIDE_US = 30.0
ROW_ALIGN = 16


def _chunks(items, cursor, us, us_per_mb, through=None):
    budget = float("inf") if us is None else us / us_per_mb * 2 ** 20
    names = list(items)
    if through is not None:
        names = names[:names.index(through) + 1]
    chunks = []
    for name in names:
        arr = items[name]
        r_dim, c_dim = arr.shape[-2:]
        row_bytes = c_dim * arr.dtype.itemsize
        while cursor[name] < r_dim and budget > 0:
            rows = r_dim - cursor[name]
            if r_dim > ROW_ALIGN and budget < rows * row_bytes:
                rows = min(rows, max(ROW_ALIGN, int(budget // row_bytes) // ROW_ALIGN * ROW_ALIGN))
            chunks.append((name, cursor[name], rows))
            cursor[name] += rows
            budget -= rows * row_bytes
    return chunks


class _Gather:
    def __init__(self, shards):
        self.shards = dict(shards)
        self.bufs = {n: None for n in self.shards}
        self.cursor = {n: 0 for n in self.shards}

    def take(self, us=None, through=None):
        chunks = _chunks(self.shards, self.cursor, us, AG_US_PER_MB, through)
        return _GatherSide(self, chunks) if chunks else None

    def get(self, name):
        chunks = _chunks(self.shards, self.cursor, None, AG_US_PER_MB, through=name)
        if chunks:
            _run_side(_GatherSide(self, chunks), "gather_" + name)
        return self.bufs[name]


class _GatherSide:
    def __init__(self, owner, chunks):
        self.owner, self.chunks = owner, chunks
        self.names = list(dict.fromkeys(n for n, _, _ in chunks))
        old = [n for n in self.names if owner.bufs[n] is not None]
        self.operands = [owner.shards[n] for n in self.names] + [owner.bufs[n] for n in old]
        self.out_shape = [jax.ShapeDtypeStruct((N_DEV,) + owner.shards[n].shape, owner.shards[n].dtype)
                          for n in self.names]
        self.aliases = {len(self.names) + i: self.names.index(n) for i, n in enumerate(old)}
        self.sems = [pltpu.SemaphoreType.DMA((7 * len(chunks),)), pltpu.SemaphoreType.DMA((7 * len(chunks),)),
                     pltpu.SemaphoreType.DMA((len(chunks),))]

    def _copy(self, ins, outs, sems, ci, k, block, to, from_shard=False):
        name, r0, rows = self.chunks[ci]
        w = self.names.index(name)
        slot = outs[w].at[_index(block), pl.ds(r0, rows)]
        return pltpu.make_async_remote_copy(
            src_ref=ins[w].at[pl.ds(r0, rows)] if from_shard else slot, dst_ref=slot,
            send_sem=sems[0].at[7 * ci + k], recv_sem=sems[1].at[7 * ci + k], device_id=to, device_id_type=MESH)

    def _own(self, ins, outs, sems, ci):
        name, r0, rows = self.chunks[ci]
        w = self.names.index(name)
        return pltpu.make_async_copy(ins[w].at[pl.ds(r0, rows)], outs[w].at[_index(_relative(0)), pl.ds(r0, rows)],
                                     sems[2].at[ci])

    def start(self, ins, outs, sems):
        me, sibling = _relative(0), _relative(1)
        for ci in range(len(self.chunks)):
            self._own(ins, outs, sems, ci).start()
        for j, k in enumerate(FAR):
            for ci in range(len(self.chunks)):
                self._copy(ins, outs, sems, ci, 1 + j, me, _relative(k), from_shard=True).start()
        for ci in range(len(self.chunks)):
            self._copy(ins, outs, sems, ci, 0, me, sibling, from_shard=True).start()

    def finish(self, ins, outs, sems):
        me, sibling = _relative(0), _relative(1)
        n = len(self.chunks)
        for j, k in enumerate(FAR):
            for ci in range(n):
                self._copy(ins, outs, sems, ci, 1 + j, _relative(k), me).wait_recv()
                self._copy(ins, outs, sems, ci, 4 + j, _relative(k), sibling).start()
        for ci in range(n):
            self._copy(ins, outs, sems, ci, 0, sibling, me).wait_recv()
        for j, k in enumerate(FAR):
            for ci in range(n):
                self._copy(ins, outs, sems, ci, 4 + j, _relative(k | 1), me).wait_recv()
        for ci in range(n):
            self._copy(ins, outs, sems, ci, 0, me, sibling, from_shard=True).wait_send()
            for j, k in enumerate(FAR):
                self._copy(ins, outs, sems, ci, 1 + j, me, _relative(k), from_shard=True).wait_send()
                self._copy(ins, outs, sems, ci, 4 + j, _relative(k), sibling).wait_send()
            self._own(ins, outs, sems, ci).wait()

    def done(self, results):
        for n, buf in zip(self.names, results):
            self.owner.bufs[n] = buf


class _Scatter:
    def __init__(self, me, far_index):
        self.me, self.far_index = me, far_index
        self.sends, self.owns, self.pairs, self.sums, self.fars = {}, {}, {}, {}, {}
        self.pair_cursor, self.far_cursor = {}, {}

    def add(self, name, send, own):
        self.sends[name] = send
        self.owns[name] = own
        self.pairs[name] = self.fars[name] = None
        self.pair_cursor[name] = 0

    def _rows(self, name):
        return self.sends[name].shape[1]

    def _add_ready_pairs(self):
        for name in self.sends:
            if name not in self.sums and self.pair_cursor[name] == self._rows(name):
                self.sums[name] = _add_pairs(self.sends[name], self.pairs[name], self.far_index, name="pair_" + name)
                self.far_cursor[name] = 0

    def _side(self, us, through=None):
        self._add_ready_pairs()
        names = list(self.sends)
        if through is not None:
            names = names[:names.index(through) + 1]
        pair_chunks = [(n, self.pair_cursor[n], self._rows(n) - self.pair_cursor[n]) for n in names
                       if self.pair_cursor[n] < self._rows(n)]
        for n, _, _ in pair_chunks:
            self.pair_cursor[n] = self._rows(n)
        far_chunks = _chunks(self.sums, self.far_cursor, us, RS_US_PER_MB,
                             through if through in self.sums else None) if self.sums else []
        return _ScatterSide(self, pair_chunks, far_chunks) if pair_chunks or far_chunks else None

    def add_blocks(self, name, blocks32, blocks16):
        self.add(name, blocks16, lax.dynamic_index_in_dim(blocks32, self.me, axis=0, keepdims=False))

    def add_cols(self, name, full32, full16):
        width = full32.shape[1] // N_DEV
        self.add(name, _blocks(full16, "cols"), lax.dynamic_slice_in_dim(full32, self.me * width, width, axis=1))

    def take(self, us):
        return self._side(us) if us >= MIN_RIDE_US else None

    def flush_pairs(self, name):
        side = self._side(0.0)
        if side is not None:
            _run_side(side, name)
        self._add_ready_pairs()

    def get(self, name):
        step = 0
        while name not in self.sums or self.far_cursor[name] < self._rows(name):
            _run_side(self._side(None, through=name), "scatter_%s_%d" % (name, step))
            step += 1
        return self.owns[name], self.pairs[name], self.fars[name]


class _ScatterSide:
    TO_SIBLING = (1, 5, 3, 7)

    def __init__(self, owner, pair_chunks, far_chunks):
        self.owner, self.pair_chunks, self.far_chunks = owner, pair_chunks, far_chunks
        self.pair_names = list(dict.fromkeys(n for n, _, _ in pair_chunks))
        self.far_names = list(dict.fromkeys(n for n, _, _ in far_chunks))
        ins = [(owner.sends[n], owner.pairs[n], (4,)) for n in self.pair_names]
        ins += [(owner.sums[n], owner.fars[n], (3,)) for n in self.far_names]
        old = [i for i, (_, buf, _) in enumerate(ins) if buf is not None]
        self.operands = [src for src, _, _ in ins] + [ins[i][1] for i in old]
        self.out_shape = [jax.ShapeDtypeStruct(slots + src.shape[1:], BF16) for src, _, slots in ins]
        self.aliases = {len(ins) + j: i for j, i in enumerate(old)}
        n_pair, n_far = 4 * len(pair_chunks), 3 * len(far_chunks)
        self.sems = [pltpu.SemaphoreType.DMA((max(n_pair, 1),)), pltpu.SemaphoreType.DMA((max(n_pair, 1),)),
                     pltpu.SemaphoreType.DMA((max(n_far, 1),)), pltpu.SemaphoreType.DMA((max(n_far, 1),))]

    def _copies(self, ins, outs, sems):
        copies = []
        for ci, (name, r0, rows) in enumerate(self.pair_chunks):
            w = self.pair_names.index(name)
            for j, k in enumerate(self.TO_SIBLING):
                copies.append(pltpu.make_async_remote_copy(
                    src_ref=ins[w].at[_index(_relative(k)), pl.ds(r0, rows)], dst_ref=outs[w].at[j, pl.ds(r0, rows)],
                    send_sem=sems[0].at[4 * ci + j], recv_sem=sems[1].at[4 * ci + j],
                    device_id=_relative(1), device_id_type=MESH))
        for ci, (name, r0, rows) in enumerate(self.far_chunks):
            w = len(self.pair_names) + self.far_names.index(name)
            for j, k in enumerate(FAR):
                copies.append(pltpu.make_async_remote_copy(
                    src_ref=ins[w].at[j, pl.ds(r0, rows)], dst_ref=outs[w].at[j, pl.ds(r0, rows)],
                    send_sem=sems[2].at[3 * ci + j], recv_sem=sems[3].at[3 * ci + j],
                    device_id=_relative(k), device_id_type=MESH))
        return copies

    def start(self, ins, outs, sems):
        for cp in self._copies(ins, outs, sems):
            cp.start()

    def finish(self, ins, outs, sems):
        for cp in self._copies(ins, outs, sems):
            cp.wait()

    def done(self, results):
        for n, buf in zip(self.pair_names, results):
            self.owner.pairs[n] = buf
        for n, buf in zip(self.far_names, results[len(self.pair_names):]):
            self.owner.fars[n] = buf


class _Joined:
    def __init__(self, sides):
        self.sides = sides
        self.operands, self.out_shape, self.sems, self.aliases, self.spans = [], [], [], {}, []
        for s in sides:
            i0, o0, s0 = len(self.operands), len(self.out_shape), len(self.sems)
            self.operands += list(s.operands)
            self.out_shape += list(s.out_shape)
            self.sems += list(s.sems)
            self.aliases.update({i0 + i: o0 + o for i, o in s.aliases.items()})
            self.spans.append((slice(i0, len(self.operands)), slice(o0, len(self.out_shape)),
                               slice(s0, len(self.sems))))

    def start(self, ins, outs, sems):
        for s, (i, o, m) in zip(self.sides, self.spans):
            s.start(ins[i], outs[o], sems[m])

    def finish(self, ins, outs, sems):
        for s, (i, o, m) in zip(self.sides, self.spans):
            s.finish(ins[i], outs[o], sems[m])

    def done(self, results):
        for s, (_, o, _) in zip(self.sides, self.spans):
            s.done(results[o])


def _join(*sides):
    sides = [s for s in sides if s is not None]
    if len(sides) <= 1:
        return sides[0] if sides else None
    return _Joined(sides)


def _pack_rows(vecs):
    rows = -(-sum(v.shape[0] for v in vecs) // 8) * 8
    width = max(v.shape[1] for v in vecs)

    def body(*refs):
        out = refs[-1]
        out[...] = jnp.zeros_like(out)
        r0 = 0
        for v in refs[:-1]:
            out[r0:r0 + v.shape[0], 0:v.shape[1]] = v[...]
            r0 += v.shape[0]

    vmem = pl.BlockSpec(memory_space=pltpu.VMEM)
    return pl.pallas_call(body, name="pack_small", in_specs=[vmem] * len(vecs), out_specs=vmem,
                          out_shape=jax.ShapeDtypeStruct((rows, width), F32))(*vecs)


def _sum_rows(inbox, shapes):
    def body(inbox_ref, *refs):
        outs, total = refs[:-1], refs[-1]
        acc = inbox_ref[0]
        for d in range(1, N_DEV):
            acc = acc + inbox_ref[d]
        total[...] = acc
        r0 = 0
        for o in outs:
            o[...] = total[r0:r0 + o.shape[0], 0:o.shape[1]]
            r0 += o.shape[0]

    vmem = pl.BlockSpec(memory_space=pltpu.VMEM)
    return pl.pallas_call(body, name="sum_small", in_specs=[vmem], out_specs=[vmem] * len(shapes),
                          out_shape=[jax.ShapeDtypeStruct(s, F32) for s in shapes],
                          scratch_shapes=[pltpu.VMEM(inbox.shape[1:], F32)])(inbox)


class _ShareRows:
    def __init__(self, mine):
        self.operands, self.aliases = [mine], {}
        self.out_shape = [jax.ShapeDtypeStruct((N_DEV,) + mine.shape, F32)]
        self.sems = [pltpu.SemaphoreType.DMA((N_DEV - 1,)), pltpu.SemaphoreType.DMA((N_DEV - 1,)),
                     pltpu.SemaphoreType.DMA(())]

    def _copy(self, ins, outs, sems, k, sender):
        return pltpu.make_async_remote_copy(
            src_ref=ins[0], dst_ref=outs[0].at[_index(sender)], send_sem=sems[0].at[k - 1], recv_sem=sems[1].at[k - 1],
            device_id=_relative(k), device_id_type=MESH)

    def _own(self, ins, outs, sems):
        return pltpu.make_async_copy(ins[0], outs[0].at[_index(_relative(0))], sems[2])

    def start(self, ins, outs, sems):
        self._own(ins, outs, sems).start()
        for k in range(1, N_DEV):
            self._copy(ins, outs, sems, k, _relative(0)).start()

    def finish(self, ins, outs, sems):
        for k in range(1, N_DEV):
            self._copy(ins, outs, sems, k, _relative(k)).wait_recv()
            self._copy(ins, outs, sems, k, _relative(0)).wait_send()
        self._own(ins, outs, sems).wait()

    def done(self, results):
        self.inbox = results[0]


class _PartsToOwners:
    def __init__(self, mats):
        self.n, self.per = len(mats), mats[0].shape[0] // N_DEV
        self.operands, self.aliases = list(mats), {}
        self.out_shape = [jax.ShapeDtypeStruct((N_DEV, self.n, self.per) + mats[0].shape[1:], F32)]
        self.sems = [pltpu.SemaphoreType.DMA((self.n * (N_DEV - 1),))] * 2

    def _copies(self, ins, outs, sems):
        return [pltpu.make_async_remote_copy(
            src_ref=ins[j].at[pl.ds(self.per * _index(_relative(k)), self.per)], dst_ref=outs[0].at[k, j],
            send_sem=sems[0].at[self.n * (k - 1) + j], recv_sem=sems[1].at[self.n * (k - 1) + j],
            device_id=_relative(k), device_id_type=MESH) for k in range(1, N_DEV) for j in range(self.n)]

    def start(self, ins, outs, sems):
        for cp in self._copies(ins, outs, sems):
            cp.start()

    def finish(self, ins, outs, sems):
        for cp in self._copies(ins, outs, sems):
            cp.wait()

    def done(self, results):
        self.stage = results[0]


def _sum_parts(mats, stage):
    n, per = len(mats), mats[0].shape[0] // N_DEV

    def body(*refs):
        stage_ref, out = refs[n], refs[n + 1]
        me = _index(_relative(0))
        for j in range(n):
            acc = refs[j][pl.ds(per * me, per)]
            for k in range(1, N_DEV):
                acc = acc + stage_ref[k, j]
            out[j] = acc

    vmem = pl.BlockSpec(memory_space=pltpu.VMEM)
    return pl.pallas_call(body, name="sum_small_parts", in_specs=[vmem] * (n + 1), out_specs=vmem,
                          out_shape=jax.ShapeDtypeStruct((n, per) + mats[0].shape[1:], F32),
                          compiler_params=pltpu.CompilerParams(vmem_limit_bytes=VMEM_LIMIT))(*mats, stage)


class _PartsToAll:
    def __init__(self, parts, rows):
        self.n, self.per = parts.shape[0], parts.shape[1]
        self.operands, self.aliases = [parts], {}
        self.out_shape = [jax.ShapeDtypeStruct((rows,) + parts.shape[2:], F32)] * self.n
        self.sems = [pltpu.SemaphoreType.DMA((self.n * (N_DEV - 1),))] * 2 + [pltpu.SemaphoreType.DMA((self.n,))]

    def _rows(self, ref, pos):
        return ref.at[pl.ds(self.per * _index(pos), self.per)]

    def _copy(self, ins, outs, sems, k, j, owner):
        return pltpu.make_async_remote_copy(
            src_ref=ins[0].at[j], dst_ref=self._rows(outs[j], owner),
            send_sem=sems[0].at[self.n * (k - 1) + j], recv_sem=sems[1].at[self.n * (k - 1) + j],
            device_id=_relative(k), device_id_type=MESH)

    def _own(self, ins, outs, sems, j):
        return pltpu.make_async_copy(ins[0].at[j], self._rows(outs[j], _relative(0)), sems[2].at[j])

    def start(self, ins, outs, sems):
        for j in range(self.n):
            self._own(ins, outs, sems, j).start()
            for k in range(1, N_DEV):
                self._copy(ins, outs, sems, k, j, _relative(0)).start()

    def finish(self, ins, outs, sems):
        for j in range(self.n):
            for k in range(1, N_DEV):
                self._copy(ins, outs, sems, k, j, _relative(k)).wait_recv()
                self._copy(ins, outs, sems, k, j, _relative(0)).wait_send()
            self._own(ins, outs, sems, j).wait()

    def done(self, results):
        self.totals = list(results)


class _SmallSync:
    def __init__(self, vec_names, mat_names):
        self.vec_names, self.mat_names = vec_names, mat_names

    def begin(self, loss, grads):
        vecs = [loss] + [grads[n] for n in self.vec_names]
        self.shapes = [v.shape for v in vecs]
        self.mats = [_diag_blocks(grads[n]) for n in self.mat_names]
        self.share = _ShareRows(_pack_rows(vecs))
        self.to_owners = _PartsToOwners(self.mats)
        return _join(self.share, self.to_owners)

    def middle(self):
        self.sums = _sum_rows(self.share.inbox, self.shapes)
        self.to_all = _PartsToAll(_sum_parts(self.mats, self.to_owners.stage), self.mats[0].shape[0])
        return self.to_all

    def end(self):
        return self.sums[0], dict(zip(self.vec_names, self.sums[1:])), dict(zip(self.mat_names, self.to_all.totals))


def _block_diag(w):
    groups = []
    for g in range(N_RNN_GROUPS):
        placed = [jnp.pad(w[4 * g + b], ((RNN_BLOCK_W * b, RNN_BLOCK_W * (3 - b)),) * 2) for b in range(4)]
        groups.append(placed[0] + placed[1] + placed[2] + placed[3])
    return jnp.stack(groups)


def _diag_blocks(wg):
    blocks = []
    for n in range(4 * N_RNN_GROUPS):
        g, at = n // 4, RNN_BLOCK_W * (n % 4)
        blocks.append(wg[g, at:at + RNN_BLOCK_W, at:at + RNN_BLOCK_W])
    return jnp.stack(blocks)


def _heads_major(t, n_heads):
    return t.reshape(S, n_heads, HEAD_DIM).transpose(1, 0, 2)


def _heads_minor(t):
    return t.transpose(1, 0, 2).reshape(S, t.shape[0] * HEAD_DIM)


def _natural(gathered, how):
    n, r, c = gathered.shape
    if how == "rows":
        return gathered.reshape(n * r, c)
    return gathered.transpose(1, 0, 2).reshape(r, n * c)


def _blocks(full, how):
    if how == "rows":
        return full.reshape(N_DEV, full.shape[0] // N_DEV, full.shape[1])
    return full.reshape(full.shape[0], N_DEV, full.shape[1] // N_DEV).transpose(1, 0, 2)


def _forward_backward(x2, target, small, gather, scatter, sync):
    xb = x2.astype(MXU_DTYPE)
    w_in = _natural(gather.get("w_in"), "cols")
    proj, projb = _mm(xb, w_in, tm=1024, tn=512, tk=D, out_dtype=(F32, MXU_DTYPE), name="proj", side=gather.take(110))

    qt = projb[:, :OFF_K].T.reshape(N_KV, GROUP, HEAD_DIM, S)
    k2, v2 = projb[:, OFF_K:OFF_V], projb[:, OFF_V:OFF_RX]
    kp = jnp.pad(_heads_major(k2, N_KV), ((0, 0), (BLOCK, 0), (0, 0)))
    vp = jnp.pad(_heads_major(v2, N_KV), ((0, 0), (BLOCK, 0), (0, 0)))
    kt = jnp.pad(k2.T.reshape(N_KV, HEAD_DIM, S), ((0, 0), (0, 0), (BLOCK, 0)))
    vt = jnp.pad(v2.T.reshape(N_KV, HEAD_DIM, S), ((0, 0), (0, 0), (BLOCK, 0)))
    sink_row = jnp.repeat(small["attn_sinks"].reshape(N_KV, 1, GROUP), BLOCK, axis=2)
    ot = _attn_fwd(qt, kp, vt, sink_row, side=gather.take(120)).reshape(D, S)

    rconv_w = _natural(gather.get("rnn_conv_w"), "cols")
    rxc = _conv_fwd(proj, OFF_RX, rconv_w, small["rnn_conv_b"], tc=512, name="rnn_conv_fwd", side=gather.take(18))
    r, i = _lru_gates(rxc, small["lru_wa"], small["lru_wi"], small["lru_ba"], small["lru_bi"], side=gather.take(33))
    h, yrin = _lru_scan_fwd(r, i, rxc, proj, small["lru_lambda"], side=gather.take(53))

    w_ap = _natural(gather.get("w_attn_proj"), "rows")
    w_rp = _natural(gather.get("w_rnn_proj"), "rows")
    y_attn = _mm(ot, w_ap, ta=True, tm=1024, tn=1024, tk=D, name="attn_proj", side=gather.take(22))
    y_rnn = _mm(yrin, w_rp, tm=1024, tn=1024, tk=D_RNN, name="rnn_proj", side=gather.take(27))
    mixin = _gate_fwd(y_attn, y_rnn, proj, small["b_gate"], side=gather.take(25))
    w_out = _natural(gather.get("w_out"), "rows")
    mix = _mm(mixin, w_out, tm=1024, tn=1024, tk=D, name="mix_out", side=gather.take(22))
    x1, x1b, xhat1, rstd1 = _ln_fwd(x2, mix, small["ln1_g"], small["ln1_b"], side=gather.take(23))

    w_up = gather.get("ffn_w_up")
    up = _mm(x1b, w_up, tm=1024, tn=768, tk=D, b_block=768, name="ffn_up", side=gather.take(58))
    w_gate = gather.get("ffn_w_gate")
    gpre = _mm(x1b, w_gate, tm=1024, tn=768, tk=D, b_block=768, name="ffn_gate", side=gather.take(58))
    fconv_w = _natural(gather.get("ffn_conv_w"), "cols")
    fin = _ffn_act_fwd(up, gpre, fconv_w, small["ffn_conv_b"], side=gather.take())
    w_down = _natural(gather.get("ffn_w_down"), "rows")
    f = _mm(fin, w_down, tm=1024, tn=1024, tk=2048, name="ffn_down")
    loss, dpre2, dpre2b, d_ln2_g, d_ln2_b = _ln_loss_bwd(x1, f, small["ln2_g"], small["ln2_b"], target)

    grads = {"ln2_g": d_ln2_g, "ln2_b": d_ln2_b}
    both = (F32, BF16)
    g32, g16 = _mm(fin, dpre2b, ta=True, tm=1024, tn=1024, tk=S, out_dtype=both, name="d_ffn_w_down")
    scatter.add_blocks("ffn_w_down", _blocks(g32, "rows"), _blocks(g16, "rows"))
    dfin = _mm(dpre2b, w_down, tb=True, tm=1024, tn=1024, tk=D, name="d_fin", side=scatter.take(57))
    dup, dgpre, grads["ffn_conv_w"], grads["ffn_conv_b"] = _ffn_act_bwd(
        dfin, up, gpre, fconv_w, small["ffn_conv_b"], side=scatter.take(85))
    g32, g16 = _mm(x1b, dup, ta=True, tm=1024, tn=768, tk=S, out_dtype=both, out_block=768, name="d_ffn_w_up",
                   side=scatter.take(57))
    scatter.add_blocks("ffn_w_up", g32, g16)
    g32, g16 = _mm(x1b, dgpre, ta=True, tm=1024, tn=768, tk=S, out_dtype=both, out_block=768, name="d_ffn_w_gate",
                   side=scatter.take(56))
    scatter.add_blocks("ffn_w_gate", g32, g16)
    dx1 = _mm(dup, w_up, tb=True, tm=1024, tn=1024, tk=768, b_block=768, name="d_x1_up", side=scatter.take(68))
    dx1 = _mm(dgpre, w_gate, tb=True, tm=1024, tn=1024, tk=768, b_block=768, add=dx1, name="d_x1_gate",
              side=scatter.take(70))
    dpre1, dpre1b, grads["ln1_g"], grads["ln1_b"] = _ln_bwd(dx1, dpre2, xhat1, rstd1, small["ln1_g"],
                                                            side=scatter.take(24))

    g32, g16 = _mm(mixin, dpre1b, ta=True, tm=1024, tn=1024, tk=S, out_dtype=both, name="d_w_out",
                   side=scatter.take(26))
    scatter.add_blocks("w_out", _blocks(g32, "rows"), _blocks(g16, "rows"))
    dmix = _mm(dpre1b, w_out, tb=True, tm=1024, tn=1024, tk=D, name="d_mixin", side=scatter.take(22))
    dya, dyr, dgl_a, dgl_r, db_a, db_r = _gate_bwd(dmix, y_attn, y_rnn, proj, small["b_gate"], side=scatter.take(36))
    grads["b_gate"] = jnp.concatenate([db_a, db_r], axis=1)
    g32, g16 = _mm(ot, dya, tm=1024, tn=1024, tk=S, out_dtype=both, name="d_w_attn_proj", side=scatter.take(38))
    scatter.add_blocks("w_attn_proj", _blocks(g32, "rows"), _blocks(g16, "rows"))
    g32, g16 = _mm(yrin, dyr, ta=True, tm=1280, tn=1024, tk=S, out_dtype=both, name="d_w_rnn_proj",
                   side=scatter.take(27))
    scatter.add_blocks("w_rnn_proj", _blocks(g32, "rows"), _blocks(g16, "rows"))
    dot_ = _mm(w_ap, dya, tb=True, tm=1024, tn=1024, tk=D, out_dtype=MXU_DTYPE, name="d_o", side=scatter.take(22))
    dyrin = _mm(dyr, w_rp, tb=True, tm=1024, tn=1280, tk=D, name="d_yrin", side=scatter.take(27))

    dry, dzr, dzi, drxc_in, grads["lru_ba"], grads["lru_bi"], grads["lru_lambda"] = _lru_scan_bwd(
        dyrin, proj, h, r, i, rxc, small["lru_lambda"], side=scatter.take(94))
    grads["lru_wa"], grads["lru_wi"] = _lru_gate_wgrad(rxc, dzr, dzi, side=scatter.take(22))
    drxc = _lru_gate_xgrad(dzr, dzi, small["lru_wa"], small["lru_wi"], drxc_in, side=scatter.take(33))
    drx, grads["rnn_conv_w"], grads["rnn_conv_b"] = _conv_bwd(drxc, proj, OFF_RX, rconv_w, tc=512,
                                                             name="rnn_conv_bwd", side=scatter.take(29))

    dqt, dk, dv, dsink = _attn_bwd(qt, kp, kt, vp, sink_row, dot_.reshape(N_KV, GROUP, HEAD_DIM, S),
                                   side=scatter.take(100))
    grads["attn_sinks"] = dsink.reshape(1, N_KV * GROUP)
    dproj = jnp.concatenate([
        dqt.reshape(D, S).T,
        _heads_minor(dk[:, BLOCK:, :]).astype(MXU_DTYPE),
        _heads_minor(dv[:, BLOCK:, :]).astype(MXU_DTYPE),
        drx, dry, dgl_a, dgl_r], axis=1)
    g32, g16 = _mm(xb, dproj, ta=True, tm=1024, tn=512, tk=S, out_dtype=both, name="d_w_in",
                   side=_join(scatter.take(110), sync.begin(loss, grads)))
    scatter.add_cols("w_in", g32, g16)
    scatter.flush_pairs("pairs_w_in")
    dx = _mm(dproj, w_in, tb=True, tm=1024, tn=1024, tk=512, add=dpre1, add_scale=ALPHA, name="d_x",
             side=_join(scatter.take(300), sync.middle()))
    return dx


SHARDED = (
    ("w_in", "cols", 128), ("w_attn_proj", "rows", 32), ("w_rnn_proj", "rows", 32), ("w_out", "rows", 32),
    ("ffn_w_up", "cols", 128), ("ffn_w_gate", "cols", 128), ("ffn_w_down", "rows", 64),
)
SMALL_REPLICATED = ("b_gate", "rnn_conv_b", "lru_wa", "lru_ba", "lru_wi", "lru_bi", "lru_lambda", "attn_sinks",
                    "ln1_g", "ln1_b", "ffn_conv_b", "ln2_g", "ln2_b")
SMALL_SHARDED = ("rnn_conv_w", "ffn_conv_w")
SMALL_MATS = ("lru_wa", "lru_wi")
WEIGHTS = ("w_in", "b_gate", "rnn_conv_w", "rnn_conv_b", "lru_wa", "lru_ba", "lru_wi", "lru_bi", "lru_lambda",
           "attn_sinks", "w_attn_proj", "w_rnn_proj", "w_out", "ln1_g", "ln1_b", "ffn_w_up", "ffn_w_gate",
           "ffn_conv_w", "ffn_conv_b", "ffn_w_down", "ln2_g", "ln2_b")


def kernel(x, w_in, b_gate, rnn_conv_w, rnn_conv_b, lru_wa, lru_ba, lru_wi, lru_bi, lru_lambda, attn_sinks, w_attn_proj, w_rnn_proj, w_out, ln1_g, ln1_b, ffn_w_up, ffn_w_gate, ffn_conv_w, ffn_conv_b, ffn_w_down, ln2_g, ln2_b, loss_target, m_w_in, m_b_gate, m_rnn_conv_w, m_rnn_conv_b, m_lru_wa, m_lru_ba, m_lru_wi, m_lru_bi, m_lru_lambda, m_attn_sinks, m_w_attn_proj, m_w_rnn_proj, m_w_out, m_ln1_g, m_ln1_b, m_ffn_w_up, m_ffn_w_gate, m_ffn_conv_w, m_ffn_conv_b, m_ffn_w_down, m_ln2_g, m_ln2_b, v_w_in, v_b_gate, v_rnn_conv_w, v_rnn_conv_b, v_lru_wa, v_lru_ba, v_lru_wi, v_lru_bi, v_lru_lambda, v_attn_sinks, v_w_attn_proj, v_w_rnn_proj, v_w_out, v_ln1_g, v_ln1_b, v_ffn_w_up, v_ffn_w_gate, v_ffn_conv_w, v_ffn_conv_b, v_ffn_w_down, v_ln2_g, v_ln2_b):
    given = dict(locals())
    wsh = {n: given[n][0] for n in WEIGHTS}
    msh = {n: given["m_" + n][0] for n in WEIGHTS}
    vsh = {n: given["v_" + n][0] for n in WEIGHTS}
    m_given = {n: given["m_" + n] for n in WEIGHTS}
    v_given = {n: given["v_" + n] for n in WEIGHTS}
    me = 4 * lax.axis_index("x") + 2 * lax.axis_index("y") + lax.axis_index("c")

    order = ("w_in", "rnn_conv_w", "ffn_conv_w", "w_attn_proj", "w_rnn_proj", "w_out", "ffn_w_up", "ffn_w_gate",
             "ffn_w_down")
    gather = _Gather({n: wsh[n] if n in SMALL_SHARDED else wsh[n].astype(MXU_DTYPE) for n in order})
    _run_side(gather.take(through="ffn_conv_w"), "gather_first")
    small = {n: given[n] for n in SMALL_REPLICATED}
    small["lru_wa"] = _block_diag(wsh["lru_wa"])
    small["lru_wi"] = _block_diag(wsh["lru_wi"])
    scatter = _Scatter(me, jnp.stack([_index(_relative(k)) for k in FAR]).astype(jnp.int32))

    vec_names = tuple(n for n in SMALL_REPLICATED if n not in SMALL_MATS) + SMALL_SHARDED
    sync = _SmallSync(vec_names, SMALL_MATS)
    dx = _forward_backward(x[0], loss_target[0], small, gather, scatter, sync)

    loss_total, g_small, mat_sums = sync.end()
    loss_total = loss_total.reshape(())
    for n in SMALL_SHARDED:
        width = wsh[n].shape[1]
        g_small[n] = lax.dynamic_slice_in_dim(g_small[n], me * width, width, axis=1)
    g_small = {n: g_small[n].reshape(given[n].shape) for n in vec_names}
    out = {}
    results = _adamw_many(*[[d[n] for n in vec_names] for d in (given, m_given, v_given, g_small)])
    for n, delta, nm, nv in zip(vec_names, *results):
        out[n] = (g_small[n], delta, nm, nv)
    for n in SMALL_MATS:
        g = mat_sums[n].reshape(given[n].shape)
        out[n] = (g, *_adamw_blocks(given[n], m_given[n], v_given[n], g, name="adamw_" + n))

    tile_rows = {n: tr for n, _, tr in SHARDED}
    for n in list(scatter.sends):
        own, pair, far = scatter.get(n)
        res = _reduce_adamw(wsh[n], msh[n], vsh[n], own, pair, far, tr=tile_rows[n], name="adamw_" + n)
        out[n] = tuple(r[None] for r in res)

    outputs = [loss_total, dx[None]]
    for kind in range(4):
        outputs += [out[n][kind] for n in WEIGHTS]
    return tuple(outputs)
```

```python
import math

import jax
import jax.numpy as jnp
from jax import lax
from jax.experimental import pallas as pl
from jax.experimental.pallas import tpu as pltpu

F32 = jnp.float32
BF16 = jnp.bfloat16
MXU_DTYPE = jnp.bfloat16

N_DEV = 8
S = 2048
D = 2048
HEAD_DIM = 64
N_KV = 4
GROUP = 8
BLOCK = 128
D_KV = N_KV * HEAD_DIM
D_RNN = 2560
RNN_GROUP = 640
N_RNN_GROUPS = D_RNN // RNN_GROUP
RNN_BLOCK_W = 160
RNN_CONV_W = 4
LRU_C = 8.0
D_FF = 6144
FFN_CONV_W = 3
D_IN = 11776
OFF_K = 2048
OFF_V = 2304
OFF_RX = 2560
OFF_RY = 5120
OFF_GA = 7680
OFF_GR = 9728
LN_EPS = 1e-5
ALPHA = 2.0 ** 0.25
ADAM_LR = 0.001
ADAM_B1 = 0.9
ADAM_B2 = 0.999
ADAM_EPS = 1e-08
ADAM_WD = 0.01
ADAM_STEP = 10
NEG = -1e30
VMEM_LIMIT = 56 * 1024 * 1024
MID_RIDE_TENTHS = 6
MESH = pl.DeviceIdType.MESH
GELU_C = math.sqrt(2.0 / math.pi)


def _cparams(*sem):
    return pltpu.CompilerParams(dimension_semantics=sem or None, vmem_limit_bytes=VMEM_LIMIT)


def _call(body, *, name, grid, in_specs, out_specs, out_shape, operands, semantics, scratch_shapes=(), side=None):
    single = not isinstance(out_shape, (list, tuple))
    out_shape = [out_shape] if single else list(out_shape)
    out_specs = [out_specs] if single else list(out_specs)
    in_specs = list(in_specs)
    scratch_shapes = list(scratch_shapes)
    if side is None:
        res = pl.pallas_call(
            body, name=name, grid=grid, in_specs=in_specs, out_specs=out_specs, out_shape=out_shape,
            scratch_shapes=scratch_shapes, compiler_params=_cparams(*semantics))(*operands)
        return res[0] if single else res
    n_in, n_out, n_scr = len(in_specs), len(out_shape), len(scratch_shapes)
    s_in, s_out = len(side.operands), len(side.out_shape)
    hbm = pl.BlockSpec(memory_space=pltpu.HBM)
    steps = math.prod(grid)
    mid_step = (steps * MID_RIDE_TENTHS) // 10

    def with_copies(*refs):
        core_in, side_in = refs[:n_in], refs[n_in:n_in + s_in]
        o0 = n_in + s_in
        core_out, side_out = refs[o0:o0 + n_out], refs[o0 + n_out:o0 + n_out + s_out]
        c0 = o0 + n_out + s_out
        core_scr, sems = refs[c0:c0 + n_scr], refs[c0 + n_scr:]
        step = 0
        for d, size in enumerate(grid):
            step = step * size + pl.program_id(d)

        @pl.when(step == 0)
        def _():
            side.start(side_in, side_out, sems)

        body(*core_in, *core_out, *core_scr)

        @pl.when(step == mid_step)
        def _():
            side.mid(side_in, side_out, sems)

        @pl.when(step == steps - 1)
        def _():
            side.finish(side_in, side_out, sems)

    res = pl.pallas_call(
        with_copies, name=name, grid=grid,
        in_specs=in_specs + [hbm] * s_in, out_specs=out_specs + [hbm] * s_out,
        out_shape=out_shape + list(side.out_shape),
        scratch_shapes=scratch_shapes + list(side.sems),
        input_output_aliases={n_in + i: n_out + o for i, o in side.aliases.items()},
        compiler_params=_cparams(*(("arbitrary",) * len(grid))))(*operands, *side.operands)
    side.done(res[n_out:])
    return res[0] if single else res[:n_out]


def _run_side(side, name):
    def body(*refs):
        s_in, s_out = len(side.operands), len(side.out_shape)
        side.start(refs[:s_in], refs[s_in:s_in + s_out], refs[s_in + s_out:])
        side.mid(refs[:s_in], refs[s_in:s_in + s_out], refs[s_in + s_out:])
        side.finish(refs[:s_in], refs[s_in:s_in + s_out], refs[s_in + s_out:])

    hbm = pl.BlockSpec(memory_space=pltpu.HBM)
    res = pl.pallas_call(
        body, name=name, in_specs=[hbm] * len(side.operands), out_specs=[hbm] * len(side.out_shape),
        out_shape=list(side.out_shape), scratch_shapes=list(side.sems),
        input_output_aliases=dict(side.aliases))(*side.operands)
    side.done(res)


def _gelu(x):
    x2 = x * x
    t = jnp.tanh(GELU_C * (x + 0.044715 * x * x2))
    g = 0.5 * x * (1.0 + t)
    dg = 0.5 * (1.0 + t) + 0.5 * x * (1.0 - t * t) * (GELU_C * (1.0 + 3.0 * 0.044715 * x2))
    return g, dg


def _sigmoid(x):
    return 1.0 / (1.0 + jnp.exp(-x))


def _softplus(x):
    z = jnp.exp(-jnp.abs(x))
    small = z * (1.0 - z * (0.5 - z * (1.0 / 3.0 - 0.25 * z)))
    return jnp.maximum(x, 0.0) + jnp.where(z < 0.02, small, jnp.log(1.0 + z))


def _one_minus_exp(x):
    series = -x * (1.0 + x * (0.5 + x * (1.0 / 6.0 + x * (1.0 / 24.0))))
    return jnp.where(x > -0.03, series, 1.0 - jnp.exp(x))


def _colsum(v):
    return jnp.sum(v, axis=0, keepdims=True)


def _mm(a, b, *, tm, tn, tk, name, ta=False, tb=False, out_dtype=F32, b_block=None, out_block=None, add=None,
        add_scale=1.0, side=None):
    out_dtypes = out_dtype if isinstance(out_dtype, tuple) else (out_dtype,)
    if ta:
        k_dim, m_dim = a.shape
    else:
        m_dim, k_dim = a.shape
    if b_block is None:
        n_dim = b.shape[0] if tb else b.shape[1]
    else:
        n_dim = b.shape[1] if tb else b.shape[0] * b_block
    assert m_dim % tm == 0 and n_dim % tn == 0 and k_dim % tk == 0, (name, m_dim, n_dim, k_dim)
    nk = k_dim // tk
    dims = (((0 if ta else 1,), (1 if tb else 0,)), ((), ()))
    has_add = add is not None

    def body(*refs):
        a_ref, b_ref = refs[0], refs[1]
        add_ref = refs[2] if has_add else None
        first_out = 3 if has_add else 2
        o_refs = refs[first_out:first_out + len(out_dtypes)]

        def product():
            return lax.dot_general(a_ref[...].astype(MXU_DTYPE), b_ref[...].astype(MXU_DTYPE), dims,
                                   preferred_element_type=F32)

        def finish(acc):
            if has_add:
                acc = acc + add_scale * add_ref[...]
            for o_ref in o_refs:
                o_ref[...] = acc.astype(o_ref.dtype)

        if nk == 1:
            finish(product())
        else:
            acc_ref = refs[-1]
            k = pl.program_id(2)

            @pl.when(k == 0)
            def _():
                acc_ref[...] = jnp.zeros_like(acc_ref)

            acc_ref[...] += product()

            @pl.when(k == nk - 1)
            def _():
                finish(acc_ref[...])

    if ta:
        a_spec = pl.BlockSpec((tk, tm), lambda i, j, k: (k, i))
    else:
        a_spec = pl.BlockSpec((tm, tk), lambda i, j, k: (i, k))
    if b_block is None:
        if tb:
            b_spec = pl.BlockSpec((tn, tk), lambda i, j, k: (j, k))
        else:
            b_spec = pl.BlockSpec((tk, tn), lambda i, j, k: (k, j))
    elif tb:
        assert b_block % tk == 0
        b_spec = pl.BlockSpec((None, tn, tk), lambda i, j, k: ((k * tk) // b_block, j, ((k * tk) % b_block) // tk))
    else:
        assert b_block % tn == 0
        b_spec = pl.BlockSpec((None, tk, tn), lambda i, j, k: ((j * tn) // b_block, k, ((j * tn) % b_block) // tn))
    in_specs = [a_spec, b_spec]
    operands = [a, b]
    if has_add:
        in_specs.append(pl.BlockSpec((tm, tn), lambda i, j, k: (i, j)))
        operands.append(add)
    if out_block is None:
        out_spec = pl.BlockSpec((tm, tn), lambda i, j, k: (i, j))
        out_dims = (m_dim, n_dim)
    else:
        assert out_block % tn == 0
        out_spec = pl.BlockSpec((None, tm, tn), lambda i, j, k: ((j * tn) // out_block, i, ((j * tn) % out_block) // tn))
        out_dims = (n_dim // out_block, m_dim, out_block)
    res = _call(
        body,
        name=name,
        grid=(m_dim // tm, n_dim // tn, nk),
        in_specs=in_specs,
        out_specs=[out_spec] * len(out_dtypes),
        out_shape=[jax.ShapeDtypeStruct(out_dims, dt) for dt in out_dtypes],
        scratch_shapes=[pltpu.VMEM((tm, tn), F32)] if nk > 1 else [],
        semantics=("parallel", "parallel", "arbitrary"),
        operands=tuple(operands),
        side=side,
    )
    return res if isinstance(out_dtype, tuple) else res[0]


def _attn_bias(bias_ref, h):
    key = lax.broadcasted_iota(jnp.int32, (2 * BLOCK, GROUP * BLOCK), 0)
    col = lax.broadcasted_iota(jnp.int32, (2 * BLOCK, GROUP * BLOCK), 1)
    dist = BLOCK + (col & (BLOCK - 1)) - key
    head = h * GROUP + (col >> 7) + 1
    slope = jnp.exp(head.astype(F32) * (-0.25 * math.log(2.0)))
    bias = jnp.where((dist >= 0) & (dist < BLOCK), -slope * dist.astype(F32), NEG)
    bias_ref[1] = bias
    bias_ref[0] = jnp.where(key < BLOCK, NEG, bias)


def _attn_probs(kb, qt, bias, sink):
    s = jnp.dot(kb, qt, preferred_element_type=F32) * (HEAD_DIM ** -0.5) + bias
    m = jnp.maximum(jnp.max(s, axis=0, keepdims=True), sink)
    e = jnp.exp(s - m)
    e_sink = jnp.exp(sink - m)
    inv = 1.0 / (jnp.sum(e, axis=0, keepdims=True) + e_sink)
    return e * inv, e_sink * inv


def _heads_on_lanes(ref, r0):
    return jnp.concatenate([ref[g, :, pl.ds(r0, BLOCK)] for g in range(GROUP)], axis=1)


def _attn_fwd(qt, kp, vt, sink_row, side=None):
    cols = GROUP * BLOCK

    def body(q_ref, k_ref, vt_ref, sink_ref, o_ref, bias_ref):
        _attn_bias(bias_ref, pl.program_id(0))
        sink = sink_ref[...]

        def step(n, carry):
            r0 = pl.multiple_of(n * BLOCK, BLOCK)
            p, _ = _attn_probs(k_ref[pl.ds(r0, 2 * BLOCK), :], _heads_on_lanes(q_ref, r0),
                               bias_ref[jnp.minimum(n, 1)], sink)
            o = jnp.dot(vt_ref[:, pl.ds(r0, 2 * BLOCK)], p.astype(MXU_DTYPE), preferred_element_type=F32)
            for g in range(GROUP):
                o_ref[g, :, pl.ds(r0, BLOCK)] = o[:, g * BLOCK:(g + 1) * BLOCK].astype(o_ref.dtype)
            return carry

        lax.fori_loop(0, S // BLOCK, step, 0)

    hm = pl.BlockSpec((None, GROUP, HEAD_DIM, S), lambda h: (h, 0, 0, 0))
    return _call(
        body,
        name="attn_fwd",
        grid=(N_KV,),
        in_specs=[
            hm,
            pl.BlockSpec((None, BLOCK + S, HEAD_DIM), lambda h: (h, 0, 0)),
            pl.BlockSpec((None, HEAD_DIM, BLOCK + S), lambda h: (h, 0, 0)),
            pl.BlockSpec((None, 1, cols), lambda h: (h, 0, 0)),
        ],
        out_specs=hm,
        out_shape=jax.ShapeDtypeStruct((N_KV, GROUP, HEAD_DIM, S), MXU_DTYPE),
        scratch_shapes=[pltpu.VMEM((2, 2 * BLOCK, cols), F32)],
        semantics=("parallel",),
        operands=(qt, kp, vt, sink_row),
        side=side,
    )


def _attn_bwd(qt, kp, kt, vp, sink_row, dot_, side=None):
    cols = GROUP * BLOCK

    def body(q_ref, k_ref, kt_ref, v_ref, sink_ref, do_ref, dq_ref, dk_ref, dv_ref, dsink_ref, bias_ref):
        _attn_bias(bias_ref, pl.program_id(0))
        sink = sink_ref[...]
        dk_ref[...] = jnp.zeros_like(dk_ref)
        dv_ref[...] = jnp.zeros_like(dv_ref)
        nt = (((1,), (1,)), ((), ()))

        def step(n, sink_acc):
            r0 = pl.multiple_of(n * BLOCK, BLOCK)
            band = pl.ds(r0, 2 * BLOCK)
            qn = _heads_on_lanes(q_ref, r0)
            don = _heads_on_lanes(do_ref, r0)
            p, p_sink = _attn_probs(k_ref[band, :], qn, bias_ref[jnp.minimum(n, 1)], sink)
            dp = jnp.dot(v_ref[band, :], don, preferred_element_type=F32)
            delta = jnp.sum(p * dp, axis=0, keepdims=True)
            ds = (p * (dp - delta) * (HEAD_DIM ** -0.5)).astype(MXU_DTYPE)
            dq = jnp.dot(kt_ref[:, band], ds, preferred_element_type=F32)
            for g in range(GROUP):
                dq_ref[g, :, pl.ds(r0, BLOCK)] = dq[:, g * BLOCK:(g + 1) * BLOCK].astype(dq_ref.dtype)
            dk_ref[band, :] += lax.dot_general(ds, qn, nt, preferred_element_type=F32)
            dv_ref[band, :] += lax.dot_general(p.astype(MXU_DTYPE), don, nt, preferred_element_type=F32)
            return sink_acc - p_sink * delta

        sink_acc = lax.fori_loop(0, S // BLOCK, step, jnp.zeros((1, cols), F32))
        for g in range(GROUP):
            dsink_ref[g:g + 1, :] = jnp.sum(sink_acc[:, g * BLOCK:(g + 1) * BLOCK], axis=1, keepdims=True)

    hm = pl.BlockSpec((None, GROUP, HEAD_DIM, S), lambda h: (h, 0, 0, 0))
    kv = pl.BlockSpec((None, BLOCK + S, HEAD_DIM), lambda h: (h, 0, 0))
    return _call(
        body,
        name="attn_bwd",
        grid=(N_KV,),
        in_specs=[hm, kv, pl.BlockSpec((None, HEAD_DIM, BLOCK + S), lambda h: (h, 0, 0)), kv,
                  pl.BlockSpec((None, 1, cols), lambda h: (h, 0, 0)), hm],
        out_specs=[hm, kv, kv, pl.BlockSpec((None, GROUP, 1), lambda h: (h, 0, 0))],
        out_shape=[
            jax.ShapeDtypeStruct((N_KV, GROUP, HEAD_DIM, S), MXU_DTYPE),
            jax.ShapeDtypeStruct((N_KV, BLOCK + S, HEAD_DIM), F32),
            jax.ShapeDtypeStruct((N_KV, BLOCK + S, HEAD_DIM), F32),
            jax.ShapeDtypeStruct((N_KV, GROUP, 1), F32),
        ],
        scratch_shapes=[pltpu.VMEM((2, 2 * BLOCK, cols), F32)],
        semantics=("parallel",),
        operands=(qt, kp, kt, vp, sink_row, dot_),
        side=side,
    )


PAD = 8
CHUNK = 256


def _past_taps(xpad_ref, r0, width):
    ext = xpad_ref[pl.ds(r0, CHUNK + PAD), :]
    taps = []
    for k in range(width):
        back = width - 1 - k
        taps.append((ext if back == 0 else pltpu.roll(ext, back, 0))[PAD:, :])
    return taps


def _future_taps(xpad_ref, r0, width):
    ext = xpad_ref[pl.ds(r0, CHUNK + PAD), :]
    taps = []
    for ahead in range(width):
        taps.append((ext if ahead == 0 else pltpu.roll(ext, CHUNK + PAD - ahead, 0))[:CHUNK, :])
    return taps


def _conv_fwd(src, col0, w, b, *, tc, name, side=None):
    width, c_dim = w.shape

    def body(x_ref, w_ref, b_ref, o_ref, xpad_ref):
        xpad_ref[pl.ds(0, PAD), :] = jnp.zeros((PAD, tc), F32)
        xpad_ref[pl.ds(PAD, S), :] = x_ref[...]
        wv = w_ref[...]
        bv = b_ref[...]

        def step(ci, carry):
            r0 = pl.multiple_of(ci * CHUNK, CHUNK)
            taps = _past_taps(xpad_ref, r0, width)
            y = bv + taps[0] * wv[0:1, :]
            for k in range(1, width):
                y = y + taps[k] * wv[k:k + 1, :]
            o_ref[pl.ds(r0, CHUNK), :] = y
            return carry

        lax.fori_loop(0, S // CHUNK, step, 0)

    return _call(
        body,
        name=name,
        grid=(c_dim // tc,),
        in_specs=[
            pl.BlockSpec((S, tc), lambda j: (0, col0 // tc + j)),
            pl.BlockSpec((width, tc), lambda j: (0, j)),
            pl.BlockSpec((1, tc), lambda j: (0, j)),
        ],
        out_specs=pl.BlockSpec((S, tc), lambda j: (0, j)),
        out_shape=jax.ShapeDtypeStruct((S, c_dim), F32),
        scratch_shapes=[pltpu.VMEM((S + PAD, tc), F32)],
        semantics=("parallel",),
        operands=(src, w, b),
        side=side,
    )


def _conv_bwd(dy, src, col0, w, *, tc, name, side=None):
    width, c_dim = w.shape

    def body(dy_ref, x_ref, w_ref, dx_ref, dw_ref, db_ref, xpad_ref, dpad_ref):
        xpad_ref[pl.ds(0, PAD), :] = jnp.zeros((PAD, tc), F32)
        xpad_ref[pl.ds(PAD, S), :] = x_ref[...]
        dpad_ref[pl.ds(0, S), :] = dy_ref[...]
        dpad_ref[pl.ds(S, PAD), :] = jnp.zeros((PAD, tc), F32)
        wv = w_ref[...]

        def step(ci, acc):
            r0 = pl.multiple_of(ci * CHUNK, CHUNK)
            past = _past_taps(xpad_ref, r0, width)
            ahead = _future_taps(dpad_ref, r0, width)
            d = ahead[0]
            dx = d * wv[width - 1:width, :]
            for j in range(1, width):
                dx = dx + ahead[j] * wv[width - 1 - j:width - j, :]
            dx_ref[pl.ds(r0, CHUNK), :] = dx.astype(dx_ref.dtype)
            return tuple(acc[k] + _colsum(past[k] * d) for k in range(width)) + (acc[width] + _colsum(d),)

        zero = jnp.zeros((1, tc), F32)
        acc = lax.fori_loop(0, S // CHUNK, step, (zero,) * (width + 1))
        for k in range(width):
            dw_ref[k:k + 1, :] = acc[k]
        db_ref[...] = acc[width]

    return _call(
        body,
        name=name,
        grid=(c_dim // tc,),
        in_specs=[
            pl.BlockSpec((S, tc), lambda j: (0, j)),
            pl.BlockSpec((S, tc), lambda j: (0, col0 // tc + j)),
            pl.BlockSpec((width, tc), lambda j: (0, j)),
        ],
        out_specs=[
            pl.BlockSpec((S, tc), lambda j: (0, j)),
            pl.BlockSpec((width, tc), lambda j: (0, j)),
            pl.BlockSpec((1, tc), lambda j: (0, j)),
        ],
        out_shape=[
            jax.ShapeDtypeStruct((S, c_dim), MXU_DTYPE),
            jax.ShapeDtypeStruct((width, c_dim), F32),
            jax.ShapeDtypeStruct((1, c_dim), F32),
        ],
        scratch_shapes=[pltpu.VMEM((S + PAD, tc), F32), pltpu.VMEM((S + PAD, tc), F32)],
        semantics=("parallel",),
        operands=(dy, src, w),
        side=side,
    )


SCAN_TC = 256


def _lru_gates(rxc, wa, wi, ba, bi, side=None):
    tm = 512

    def body(x_ref, wa_ref, wi_ref, ba_ref, bi_ref, r_ref, i_ref):
        xv = x_ref[...].astype(MXU_DTYPE)
        r_ref[...] = _sigmoid(jnp.dot(xv, wa_ref[...].astype(MXU_DTYPE), preferred_element_type=F32) + ba_ref[...])
        i_ref[...] = _sigmoid(jnp.dot(xv, wi_ref[...].astype(MXU_DTYPE), preferred_element_type=F32) + bi_ref[...])

    x_spec = pl.BlockSpec((tm, RNN_GROUP), lambda g, i: (i, g))
    w_spec = pl.BlockSpec((None, RNN_GROUP, RNN_GROUP), lambda g, i: (g, 0, 0))
    b_spec = pl.BlockSpec((1, RNN_GROUP), lambda g, i: (0, g))
    return _call(
        body,
        name="lru_gates",
        grid=(N_RNN_GROUPS, S // tm),
        in_specs=[x_spec, w_spec, w_spec, b_spec, b_spec],
        out_specs=[x_spec, x_spec],
        out_shape=[jax.ShapeDtypeStruct((S, D_RNN), F32)] * 2,
        semantics=("parallel", "parallel"),
        operands=(rxc, wa, wi, ba, bi),
        side=side,
    )


def _scan_down(a, u, row):
    for d in (1, 2, 4):
        a_s = jnp.where(row >= d, pltpu.roll(a, d, 0), 1.0)
        u_s = jnp.where(row >= d, pltpu.roll(u, d, 0), 0.0)
        u = a * u_s + u
        a = a * a_s
    return a, u


def _scan_up(a, u, row):
    for d in (1, 2, 4):
        a_s = jnp.where(row < 8 - d, pltpu.roll(a, 8 - d, 0), 1.0)
        u_s = jnp.where(row < 8 - d, pltpu.roll(u, 8 - d, 0), 0.0)
        u = a * u_s + u
        a = a * a_s
    return a, u


def _lru_scan_fwd(r, i, rxc, proj, lam, side=None):
    tc = SCAN_TC

    def body(r_ref, i_ref, x_ref, ry_ref, lam_ref, h_ref, y_ref):
        rate = LRU_C * _softplus(-lam_ref[...])
        row = lax.broadcasted_iota(jnp.int32, (8, tc), 0)

        def step(ci, carry):
            r0 = pl.multiple_of(ci * 16, 16)
            log_a = -rate * r_ref[pl.ds(r0, 16), :]
            a16 = jnp.exp(log_a)
            u16 = jnp.sqrt(_one_minus_exp(2.0 * log_a)) * (i_ref[pl.ds(r0, 16), :] * x_ref[pl.ds(r0, 16), :])
            hs = []
            for half in range(2):
                a_cum, h0 = _scan_down(a16[8 * half:8 * half + 8, :], u16[8 * half:8 * half + 8, :], row)
                h = a_cum * carry + h0
                carry = jnp.broadcast_to(h[7:8, :], (8, tc))
                hs.append(h)
            h16 = jnp.concatenate(hs, axis=0)
            h_ref[pl.ds(r0, 16), :] = h16
            y_ref[pl.ds(r0, 16), :] = (h16 * _gelu(ry_ref[pl.ds(r0, 16), :])[0]).astype(y_ref.dtype)
            return carry

        lax.fori_loop(0, S // 16, step, jnp.zeros((8, tc), F32))

    col = pl.BlockSpec((S, tc), lambda j: (0, j))
    return _call(
        body,
        name="lru_scan_fwd",
        grid=(D_RNN // tc,),
        in_specs=[col, col, col, pl.BlockSpec((S, tc), lambda j: (0, OFF_RY // tc + j)),
                  pl.BlockSpec((1, tc), lambda j: (0, j))],
        out_specs=[col, col],
        out_shape=[jax.ShapeDtypeStruct((S, D_RNN), F32), jax.ShapeDtypeStruct((S, D_RNN), MXU_DTYPE)],
        semantics=("parallel",),
        operands=(r, i, rxc, proj, lam),
        side=side,
    )


def _lru_scan_bwd(dy, proj, h, r, i, rxc, lam, side=None):
    tc = SCAN_TC

    def body(dy_ref, ry_ref, h_ref, r_ref, i_ref, x_ref, lam_ref,
             dry_ref, dzr_ref, dzi_ref, dx_ref, dba_ref, dbi_ref, dlam_ref, a_ref, dh_ref, hp_ref):
        lam_v = lam_ref[...]
        rate = LRU_C * _softplus(-lam_v)
        dlam_scale = LRU_C * _sigmoid(-lam_v)
        row = lax.broadcasted_iota(jnp.int32, (8, tc), 0)
        hp_ref[pl.ds(0, PAD), :] = jnp.zeros((PAD, tc), F32)
        hp_ref[pl.ds(PAD, S), :] = h_ref[...]
        a_ref[pl.ds(S, PAD), :] = jnp.zeros((PAD, tc), F32)

        def prep(ci, carry):
            r0 = pl.multiple_of(ci * CHUNK, CHUNK)
            a_ref[pl.ds(r0, CHUNK), :] = jnp.exp(-rate * r_ref[pl.ds(r0, CHUNK), :])
            ge, dge = _gelu(ry_ref[pl.ds(r0, CHUNK), :])
            dyv = dy_ref[pl.ds(r0, CHUNK), :]
            dh_ref[pl.ds(r0, CHUNK), :] = dyv * ge
            dry_ref[pl.ds(r0, CHUNK), :] = (dyv * h_ref[pl.ds(r0, CHUNK), :] * dge).astype(dry_ref.dtype)
            return carry

        lax.fori_loop(0, S // CHUNK, prep, 0)

        def step(ci, state):
            carry, dba, dbi, dlam = state
            r0 = pl.multiple_of(S - 16 - ci * 16, 16)
            a_ext = a_ref[pl.ds(r0, 24), :]
            a_next = pltpu.roll(a_ext, 23, 0)
            h_prev = pltpu.roll(hp_ref[pl.ds(r0, 24), :], 1, 0)
            dh16 = dh_ref[pl.ds(r0, 16), :]
            gs = [None, None]
            for half in (1, 0):
                lo = 8 * half
                c_cum, g0 = _scan_up(a_next[lo:lo + 8, :], dh16[lo:lo + 8, :], row)
                g = c_cum * carry + g0
                carry = jnp.broadcast_to(g[0:1, :], (8, tc))
                gs[half] = g
            g16 = jnp.concatenate(gs, axis=0)
            a16 = a_ext[0:16, :]
            r16 = r_ref[pl.ds(r0, 16), :]
            i16 = i_ref[pl.ds(r0, 16), :]
            x16 = x_ref[pl.ds(r0, 16), :]
            a2 = a16 * a16
            sq = jnp.sqrt(_one_minus_exp(-2.0 * rate * r16))
            dx_ref[pl.ds(r0, 16), :] = g16 * sq * i16
            dzi = g16 * sq * x16 * i16 * (1.0 - i16)
            dlog_a = g16 * h_prev[8:24, :] * a16 - g16 * i16 * x16 * a2 / sq
            dzr = -rate * dlog_a * r16 * (1.0 - r16)
            dzr_ref[pl.ds(r0, 16), :] = dzr.astype(dzr_ref.dtype)
            dzi_ref[pl.ds(r0, 16), :] = dzi.astype(dzi_ref.dtype)
            return carry, dba + _colsum(dzr), dbi + _colsum(dzi), dlam + _colsum(dlog_a * r16)

        zero = jnp.zeros((1, tc), F32)
        _, dba, dbi, dlam = lax.fori_loop(0, S // 16, step, (jnp.zeros((8, tc), F32), zero, zero, zero))
        dba_ref[...] = dba
        dbi_ref[...] = dbi
        dlam_ref[...] = dlam * dlam_scale

    col = pl.BlockSpec((S, tc), lambda j: (0, j))
    vec = pl.BlockSpec((1, tc), lambda j: (0, j))
    return _call(
        body,
        name="lru_scan_bwd",
        grid=(D_RNN // tc,),
        in_specs=[col, pl.BlockSpec((S, tc), lambda j: (0, OFF_RY // tc + j)), col, col, col, col, vec],
        out_specs=[col, col, col, col, vec, vec, vec],
        out_shape=[jax.ShapeDtypeStruct((S, D_RNN), MXU_DTYPE)] * 3 + [jax.ShapeDtypeStruct((S, D_RNN), F32)]
        + [jax.ShapeDtypeStruct((1, D_RNN), F32)] * 3,
        scratch_shapes=[pltpu.VMEM((S + PAD, tc), F32), pltpu.VMEM((S, tc), F32), pltpu.VMEM((S + PAD, tc), F32)],
        semantics=("parallel",),
        operands=(dy, proj, h, r, i, rxc, lam),
        side=side,
    )


def _lru_gate_wgrad(rxc, dzr, dzi, side=None):
    def body(x_ref, dzr_ref, dzi_ref, dwa_ref, dwi_ref):
        xv = x_ref[...].astype(MXU_DTYPE)
        dims = (((0,), (0,)), ((), ()))
        dwa_ref[...] = lax.dot_general(xv, dzr_ref[...], dims, preferred_element_type=F32)
        dwi_ref[...] = lax.dot_general(xv, dzi_ref[...], dims, preferred_element_type=F32)

    col = pl.BlockSpec((S, RNN_GROUP), lambda g: (0, g))
    w_spec = pl.BlockSpec((None, RNN_GROUP, RNN_GROUP), lambda g: (g, 0, 0))
    return _call(
        body,
        name="lru_gate_wgrad",
        grid=(N_RNN_GROUPS,),
        in_specs=[col, col, col],
        out_specs=[w_spec, w_spec],
        out_shape=[jax.ShapeDtypeStruct((N_RNN_GROUPS, RNN_GROUP, RNN_GROUP), F32)] * 2,
        semantics=("parallel",),
        operands=(rxc, dzr, dzi),
        side=side,
    )


def _lru_gate_xgrad(dzr, dzi, wa, wi, dx_in, side=None):
    tm = 512

    def body(dzr_ref, dzi_ref, wa_ref, wi_ref, dx_ref, o_ref):
        dims = (((1,), (1,)), ((), ()))
        o_ref[...] = (dx_ref[...]
                      + lax.dot_general(dzr_ref[...], wa_ref[...].astype(MXU_DTYPE), dims, preferred_element_type=F32)
                      + lax.dot_general(dzi_ref[...], wi_ref[...].astype(MXU_DTYPE), dims, preferred_element_type=F32))

    x_spec = pl.BlockSpec((tm, RNN_GROUP), lambda g, i: (i, g))
    w_spec = pl.BlockSpec((None, RNN_GROUP, RNN_GROUP), lambda g, i: (g, 0, 0))
    return _call(
        body,
        name="lru_gate_xgrad",
        grid=(N_RNN_GROUPS, S // tm),
        in_specs=[x_spec, x_spec, w_spec, w_spec, x_spec],
        out_specs=x_spec,
        out_shape=jax.ShapeDtypeStruct((S, D_RNN), F32),
        semantics=("parallel", "parallel"),
        operands=(dzr, dzi, wa, wi, dx_in),
        side=side,
    )


def _gate_fwd(y_attn, y_rnn, proj, b_gate, side=None):
    t = 512

    def body(ya_ref, yr_ref, ga_ref, gr_ref, ba_ref, br_ref, o_ref):
        o_ref[...] = (_sigmoid(ga_ref[...] + ba_ref[...]) * ya_ref[...]
                      + _sigmoid(gr_ref[...] + br_ref[...]) * yr_ref[...]).astype(o_ref.dtype)

    tile = pl.BlockSpec((t, t), lambda i, j: (i, j))
    return _call(
        body,
        name="gate_fwd",
        grid=(S // t, D // t),
        in_specs=[tile, tile,
                  pl.BlockSpec((t, t), lambda i, j: (i, OFF_GA // t + j)),
                  pl.BlockSpec((t, t), lambda i, j: (i, OFF_GR // t + j)),
                  pl.BlockSpec((1, t), lambda i, j: (0, j)),
                  pl.BlockSpec((1, t), lambda i, j: (0, D // t + j))],
        out_specs=tile,
        out_shape=jax.ShapeDtypeStruct((S, D), MXU_DTYPE),
        semantics=("parallel", "parallel"),
        operands=(y_attn, y_rnn, proj, proj, b_gate, b_gate),
        side=side,
    )


def _gate_bwd(dmix, y_attn, y_rnn, proj, b_gate, side=None):
    t = 512

    def body(dm_ref, ya_ref, yr_ref, ga_ref, gr_ref, ba_ref, br_ref,
             dya_ref, dyr_ref, dga_ref, dgr_ref, dba_ref, dbr_ref):
        @pl.when(pl.program_id(1) == 0)
        def _():
            dba_ref[...] = jnp.zeros_like(dba_ref)
            dbr_ref[...] = jnp.zeros_like(dbr_ref)

        dm = dm_ref[...]
        ga = _sigmoid(ga_ref[...] + ba_ref[...])
        gr = _sigmoid(gr_ref[...] + br_ref[...])
        dya_ref[...] = (dm * ga).astype(dya_ref.dtype)
        dyr_ref[...] = (dm * gr).astype(dyr_ref.dtype)
        dga = dm * ya_ref[...] * ga * (1.0 - ga)
        dgr = dm * yr_ref[...] * gr * (1.0 - gr)
        dga_ref[...] = dga.astype(dga_ref.dtype)
        dgr_ref[...] = dgr.astype(dgr_ref.dtype)
        dba_ref[...] += _colsum(dga)
        dbr_ref[...] += _colsum(dgr)

    tile = pl.BlockSpec((t, t), lambda j, i: (i, j))
    vec = pl.BlockSpec((1, t), lambda j, i: (0, j))
    return _call(
        body,
        name="gate_bwd",
        grid=(D // t, S // t),
        in_specs=[tile, tile, tile,
                  pl.BlockSpec((t, t), lambda j, i: (i, OFF_GA // t + j)),
                  pl.BlockSpec((t, t), lambda j, i: (i, OFF_GR // t + j)),
                  vec,
                  pl.BlockSpec((1, t), lambda j, i: (0, D // t + j))],
        out_specs=[tile, tile, tile, tile, vec, vec],
        out_shape=[jax.ShapeDtypeStruct((S, D), MXU_DTYPE)] * 4 + [jax.ShapeDtypeStruct((1, D), F32)] * 2,
        semantics=("parallel", "arbitrary"),
        operands=(dmix, y_attn, y_rnn, proj, proj, b_gate, b_gate),
        side=side,
    )


LN_TM = 256


def _ln_stats(pre):
    mu = jnp.mean(pre, axis=-1, keepdims=True)
    xc = pre - mu
    rstd = lax.rsqrt(jnp.mean(xc * xc, axis=-1, keepdims=True) + LN_EPS)
    return xc * rstd, rstd


def _ln_input_grad(dy, xhat, rstd, g):
    dyg = dy * g
    return rstd * (dyg - jnp.mean(dyg, axis=-1, keepdims=True)
                   - xhat * jnp.mean(dyg * xhat, axis=-1, keepdims=True))


def _ln_fwd(res, branch, g, b, side=None):
    def body(res_ref, br_ref, g_ref, b_ref, y_ref, yb_ref, xhat_ref, rstd_ref):
        xhat, rstd = _ln_stats(ALPHA * res_ref[...] + br_ref[...])
        y = xhat * g_ref[...] + b_ref[...]
        y_ref[...] = y
        yb_ref[...] = y.astype(yb_ref.dtype)
        xhat_ref[...] = xhat
        rstd_ref[...] = rstd

    tile = pl.BlockSpec((LN_TM, D), lambda i: (i, 0))
    vec = pl.BlockSpec((1, D), lambda i: (0, 0))
    return _call(
        body,
        name="ln_fwd",
        grid=(S // LN_TM,),
        in_specs=[tile, tile, vec, vec],
        out_specs=[tile, tile, tile, pl.BlockSpec((LN_TM, 1), lambda i: (i, 0))],
        out_shape=[jax.ShapeDtypeStruct((S, D), F32), jax.ShapeDtypeStruct((S, D), MXU_DTYPE),
                   jax.ShapeDtypeStruct((S, D), F32), jax.ShapeDtypeStruct((S, 1), F32)],
        semantics=("parallel",),
        operands=(res, branch, g, b),
        side=side,
    )


def _ln_bwd(dy_a, dy_b, xhat, rstd, g, side=None):
    def body(da_ref, db_in_ref, xhat_ref, rstd_ref, g_ref, dp_ref, dpb_ref, dg_ref, db_ref):
        @pl.when(pl.program_id(0) == 0)
        def _():
            dg_ref[...] = jnp.zeros_like(dg_ref)
            db_ref[...] = jnp.zeros_like(db_ref)

        dy = da_ref[...] + ALPHA * db_in_ref[...]
        xhat = xhat_ref[...]
        dp = _ln_input_grad(dy, xhat, rstd_ref[...], g_ref[...])
        dp_ref[...] = dp
        dpb_ref[...] = dp.astype(dpb_ref.dtype)
        dg_ref[...] += _colsum(dy * xhat)
        db_ref[...] += _colsum(dy)

    tile = pl.BlockSpec((LN_TM, D), lambda i: (i, 0))
    vec = pl.BlockSpec((1, D), lambda i: (0, 0))
    return _call(
        body,
        name="ln_bwd",
        grid=(S // LN_TM,),
        in_specs=[tile, tile, tile, pl.BlockSpec((LN_TM, 1), lambda i: (i, 0)), vec],
        out_specs=[tile, tile, vec, vec],
        out_shape=[jax.ShapeDtypeStruct((S, D), F32), jax.ShapeDtypeStruct((S, D), MXU_DTYPE),
                   jax.ShapeDtypeStruct((1, D), F32), jax.ShapeDtypeStruct((1, D), F32)],
        semantics=("arbitrary",),
        operands=(dy_a, dy_b, xhat, rstd, g),
        side=side,
    )


def _ln_loss_bwd(res, branch, g, b, target, side=None):
    def body(res_ref, br_ref, g_ref, b_ref, t_ref, loss_ref, dp_ref, dpb_ref, dg_ref, db_ref):
        @pl.when(pl.program_id(0) == 0)
        def _():
            loss_ref[...] = jnp.zeros_like(loss_ref)
            dg_ref[...] = jnp.zeros_like(dg_ref)
            db_ref[...] = jnp.zeros_like(db_ref)

        xhat, rstd = _ln_stats(ALPHA * res_ref[...] + br_ref[...])
        gv = g_ref[...]
        err = xhat * gv + b_ref[...] - t_ref[...]
        loss_ref[...] += (0.5 / D) * jnp.sum(_colsum(err * err), axis=1, keepdims=True)
        dy = err * (1.0 / D)
        dp = _ln_input_grad(dy, xhat, rstd, gv)
        dp_ref[...] = dp
        dpb_ref[...] = dp.astype(dpb_ref.dtype)
        dg_ref[...] += _colsum(dy * xhat)
        db_ref[...] += _colsum(dy)

    tile = pl.BlockSpec((LN_TM, D), lambda i: (i, 0))
    vec = pl.BlockSpec((1, D), lambda i: (0, 0))
    return _call(
        body,
        name="ln_loss_bwd",
        grid=(S // LN_TM,),
        in_specs=[tile, tile, vec, vec, tile],
        out_specs=[pl.BlockSpec((1, 1), lambda i: (0, 0)), tile, tile, vec, vec],
        out_shape=[jax.ShapeDtypeStruct((1, 1), F32), jax.ShapeDtypeStruct((S, D), F32),
                   jax.ShapeDtypeStruct((S, D), MXU_DTYPE),
                   jax.ShapeDtypeStruct((1, D), F32), jax.ShapeDtypeStruct((1, D), F32)],
        semantics=("arbitrary",),
        operands=(res, branch, g, b, target),
        side=side,
    )


FFN_TC = 256


def _ffn_act_fwd(up, gpre, w, b, side=None):
    tc = FFN_TC

    def body(up_ref, x_ref, w_ref, b_ref, o_ref, xpad_ref):
        xpad_ref[pl.ds(0, PAD), :] = jnp.zeros((PAD, tc), F32)
        xpad_ref[pl.ds(PAD, S), :] = x_ref[...]
        wv = w_ref[...]
        bv = b_ref[...]

        def step(ci, carry):
            r0 = pl.multiple_of(ci * CHUNK, CHUNK)
            taps = _past_taps(xpad_ref, r0, FFN_CONV_W)
            gate = bv + taps[0] * wv[0:1, :] + taps[1] * wv[1:2, :] + taps[2] * wv[2:3, :]
            o_ref[pl.ds(r0, CHUNK), :] = (_gelu(gate)[0] * up_ref[pl.ds(r0, CHUNK), :]).astype(o_ref.dtype)
            return carry

        lax.fori_loop(0, S // CHUNK, step, 0)

    col = pl.BlockSpec((S, tc), lambda j: (0, j))
    return _call(
        body,
        name="ffn_act_fwd",
        grid=(D_FF // tc,),
        in_specs=[col, col, pl.BlockSpec((FFN_CONV_W, tc), lambda j: (0, j)), pl.BlockSpec((1, tc), lambda j: (0, j))],
        out_specs=col,
        out_shape=jax.ShapeDtypeStruct((S, D_FF), MXU_DTYPE),
        scratch_shapes=[pltpu.VMEM((S + PAD, tc), F32)],
        semantics=("parallel",),
        operands=(up, gpre, w, b),
        side=side,
    )


def _ffn_act_bwd(dfin, up, gpre, w, b, side=None):
    tc = FFN_TC
    width = FFN_CONV_W

    def body(df_ref, up_ref, x_ref, w_ref, b_ref, dup_ref, dx_ref, dw_ref, db_ref, xpad_ref, dpad_ref):
        xpad_ref[pl.ds(0, PAD), :] = jnp.zeros((PAD, tc), F32)
        xpad_ref[pl.ds(PAD, S), :] = x_ref[...]
        dpad_ref[pl.ds(S, PAD), :] = jnp.zeros((PAD, tc), F32)
        wv = w_ref[...]
        bv = b_ref[...]

        def gate_grad(ci, acc):
            r0 = pl.multiple_of(ci * CHUNK, CHUNK)
            taps = _past_taps(xpad_ref, r0, width)
            gate = bv + taps[0] * wv[0:1, :] + taps[1] * wv[1:2, :] + taps[2] * wv[2:3, :]
            ge, dge = _gelu(gate)
            df = df_ref[pl.ds(r0, CHUNK), :]
            dup_ref[pl.ds(r0, CHUNK), :] = (df * ge).astype(dup_ref.dtype)
            d = df * up_ref[pl.ds(r0, CHUNK), :] * dge
            dpad_ref[pl.ds(r0, CHUNK), :] = d
            return tuple(acc[k] + _colsum(taps[k] * d) for k in range(width)) + (acc[width] + _colsum(d),)

        zero = jnp.zeros((1, tc), F32)
        acc = lax.fori_loop(0, S // CHUNK, gate_grad, (zero,) * (width + 1))
        for k in range(width):
            dw_ref[k:k + 1, :] = acc[k]
        db_ref[...] = acc[width]

        def input_grad(ci, carry):
            r0 = pl.multiple_of(ci * CHUNK, CHUNK)
            ahead = _future_taps(dpad_ref, r0, width)
            dx = ahead[0] * wv[2:3, :] + ahead[1] * wv[1:2, :] + ahead[2] * wv[0:1, :]
            dx_ref[pl.ds(r0, CHUNK), :] = dx.astype(dx_ref.dtype)
            return carry

        lax.fori_loop(0, S // CHUNK, input_grad, 0)

    col = pl.BlockSpec((S, tc), lambda j: (0, j))
    w_spec = pl.BlockSpec((width, tc), lambda j: (0, j))
    vec = pl.BlockSpec((1, tc), lambda j: (0, j))
    return _call(
        body,
        name="ffn_act_bwd",
        grid=(D_FF // tc,),
        in_specs=[col, col, col, w_spec, vec],
        out_specs=[col, col, w_spec, vec],
        out_shape=[jax.ShapeDtypeStruct((S, D_FF), MXU_DTYPE)] * 2
        + [jax.ShapeDtypeStruct((width, D_FF), F32), jax.ShapeDtypeStruct((1, D_FF), F32)],
        scratch_shapes=[pltpu.VMEM((S + PAD, tc), F32), pltpu.VMEM((S + PAD, tc), F32)],
        semantics=("parallel",),
        operands=(dfin, up, gpre, w, b),
        side=side,
    )


def _adamw_update(w, g, m, v):
    m = ADAM_B1 * m + (1.0 - ADAM_B1) * g
    v = ADAM_B2 * v + (1.0 - ADAM_B2) * (g * g)
    m_hat = m / (1.0 - ADAM_B1 ** ADAM_STEP)
    v_hat = v / (1.0 - ADAM_B2 ** ADAM_STEP)
    delta = -ADAM_LR * (m_hat / (jnp.sqrt(v_hat) + ADAM_EPS) + ADAM_WD * w)
    return delta, m, v


def _add_pairs(send, pair, far_index, *, name):
    _, r_dim, c_dim = send.shape
    tr = r_dim // 4

    def body(far_ref, mine_ref, theirs_ref, o_ref):
        o_ref[...] = (mine_ref[...].astype(F32) + theirs_ref[...].astype(F32)).astype(o_ref.dtype)

    return pl.pallas_call(
        body,
        name=name,
        grid_spec=pltpu.PrefetchScalarGridSpec(
            num_scalar_prefetch=1,
            grid=(3, r_dim // tr),
            in_specs=[pl.BlockSpec((None, tr, c_dim), lambda j, i, far: (far[j], i, 0)),
                      pl.BlockSpec((None, tr, c_dim), lambda j, i, far: (1 + j, i, 0))],
            out_specs=pl.BlockSpec((None, tr, c_dim), lambda j, i, far: (j, i, 0)),
        ),
        out_shape=jax.ShapeDtypeStruct((3, r_dim, c_dim), BF16),
        compiler_params=_cparams("parallel", "parallel"),
    )(far_index, send, pair)


def _reduce_adamw(w, m, v, g_own, pair, far, *, tr, name):
    r_dim, c_dim = w.shape

    def body(w_ref, m_ref, v_ref, g_ref, pair_ref, far_ref, grad_ref, delta_ref, nm_ref, nv_ref):
        g = g_ref[...] + pair_ref[...].astype(F32)
        for j in range(3):
            g = g + far_ref[j].astype(F32)
        delta, nm, nv = _adamw_update(w_ref[...], g, m_ref[...], v_ref[...])
        grad_ref[...] = g
        delta_ref[...] = delta
        nm_ref[...] = nm
        nv_ref[...] = nv

    tile = pl.BlockSpec((tr, c_dim), lambda i: (i, 0))
    return _call(
        body,
        name=name,
        grid=(r_dim // tr,),
        in_specs=[tile, tile, tile, tile, pl.BlockSpec((None, tr, c_dim), lambda i: (0, i, 0)),
                  pl.BlockSpec((3, tr, c_dim), lambda i: (0, i, 0))],
        out_specs=[tile] * 4,
        out_shape=[jax.ShapeDtypeStruct((r_dim, c_dim), F32)] * 4,
        semantics=("parallel",),
        operands=(w, m, v, g_own, pair, far),
    )


def _adamw_many(ws, ms, vs, gs):
    n = len(ws)

    def body(*refs):
        for i in range(n):
            delta, nm, nv = _adamw_update(refs[i][...], refs[3 * n + i][...], refs[n + i][...], refs[2 * n + i][...])
            refs[4 * n + i][...] = delta
            refs[5 * n + i][...] = nm
            refs[6 * n + i][...] = nv

    vmem = pl.BlockSpec(memory_space=pltpu.VMEM)
    res = pl.pallas_call(
        body,
        name="adamw_small",
        in_specs=[vmem] * (4 * n),
        out_specs=[vmem] * (3 * n),
        out_shape=[jax.ShapeDtypeStruct(w.shape, F32) for w in ws] * 3,
        compiler_params=pltpu.CompilerParams(vmem_limit_bytes=VMEM_LIMIT),
    )(*ws, *ms, *vs, *gs)
    return res[:n], res[n:2 * n], res[2 * n:]


def _adamw_blocks(w, m, v, g, *, name, side=None):
    per = 2

    def body(w_ref, m_ref, v_ref, g_ref, delta_ref, nm_ref, nv_ref):
        delta, nm, nv = _adamw_update(w_ref[...], g_ref[...], m_ref[...], v_ref[...])
        delta_ref[...] = delta
        nm_ref[...] = nm
        nv_ref[...] = nv

    tile = pl.BlockSpec((1, per) + w.shape[2:], lambda i: (0, i, 0, 0))
    return _call(
        body,
        name=name,
        grid=(w.shape[1] // per,),
        in_specs=[tile] * 4,
        out_specs=[tile] * 3,
        out_shape=[jax.ShapeDtypeStruct(w.shape, F32)] * 3,
        semantics=("parallel",),
        operands=(w, m, v, g),
        side=side,
    )


def _coords():
    return lax.axis_index("x"), lax.axis_index("y"), lax.axis_index("c")


def _flip(coord, bit):
    return 1 - coord if bit else coord


def _relative(k):
    x, y, c = _coords()
    return _flip(x, k & 4), _flip(y, k & 2), _flip(c, k & 1)


def _index(pos):
    return 4 * pos[0] + 2 * pos[1] + pos[2]


FAR = (4, 2, 6)
AG_US_PER_MB = 34.0
RS_US_PER_MB = 36.0
MIN_RIDE_US = 30.0
ROW_ALIGN = 32


def _chunks(items, cursor, us, us_per_mb, through=None):
    budget = float("inf") if us is None else us / us_per_mb * 2 ** 20
    names = list(items)
    if through is not None:
        names = names[:names.index(through) + 1]
    chunks = []
    for name in names:
        arr = items[name]
        r_dim, c_dim = arr.shape[-2:]
        row_bytes = c_dim * arr.dtype.itemsize
        while cursor[name] < r_dim and budget > 0:
            rows = r_dim - cursor[name]
            if r_dim > ROW_ALIGN and budget < rows * row_bytes:
                rows = min(rows, max(ROW_ALIGN, int(budget // row_bytes) // ROW_ALIGN * ROW_ALIGN))
            chunks.append((name, cursor[name], rows))
            cursor[name] += rows
            budget -= rows * row_bytes
    return chunks


class _Gather:
    def __init__(self, shards):
        self.shards = dict(shards)
        self.bufs = {n: None for n in self.shards}
        self.cursor = {n: 0 for n in self.shards}

    def take(self, us=None, through=None):
        chunks = _chunks(self.shards, self.cursor, us, AG_US_PER_MB, through)
        return _GatherSide(self, chunks) if chunks else None

    def get(self, name):
        chunks = _chunks(self.shards, self.cursor, None, AG_US_PER_MB, through=name)
        if chunks:
            _run_side(_GatherSide(self, chunks), "gather_" + name)
        return self.bufs[name]


class _GatherSide:
    SEMS = 8

    def __init__(self, owner, chunks):
        self.owner, self.chunks = owner, chunks
        self.names = list(dict.fromkeys(n for n, _, _ in chunks))
        old = [n for n in self.names if owner.bufs[n] is not None]
        self.operands = [owner.shards[n] for n in self.names] + [owner.bufs[n] for n in old]
        self.out_shape = [jax.ShapeDtypeStruct((N_DEV,) + owner.shards[n].shape, owner.shards[n].dtype)
                          for n in self.names]
        self.aliases = {len(self.names) + i: self.names.index(n) for i, n in enumerate(old)}
        self.sems = [pltpu.SemaphoreType.DMA((self.SEMS * len(chunks),)),
                     pltpu.SemaphoreType.DMA((self.SEMS * len(chunks),)), pltpu.SemaphoreType.DMA((len(chunks),))]

    def _halves(self, ci):
        _, r0, rows = self.chunks[ci]
        if rows % ROW_ALIGN:
            return None
        return (r0, rows // 2), (r0 + rows // 2, rows // 2)

    def _copy(self, ins, outs, sems, ci, s, block, to, rows=None, from_shard=False):
        name, r0, n = self.chunks[ci]
        if rows is not None:
            r0, n = rows
        w = self.names.index(name)
        slot = outs[w].at[_index(block), pl.ds(r0, n)]
        return pltpu.make_async_remote_copy(
            src_ref=ins[w].at[pl.ds(r0, n)] if from_shard else slot, dst_ref=slot,
            send_sem=sems[0].at[self.SEMS * ci + s], recv_sem=sems[1].at[self.SEMS * ci + s],
            device_id=to, device_id_type=MESH)

    def _own(self, ins, outs, sems, ci):
        name, r0, rows = self.chunks[ci]
        w = self.names.index(name)
        return pltpu.make_async_copy(ins[w].at[pl.ds(r0, rows)], outs[w].at[_index(_relative(0)), pl.ds(r0, rows)],
                                     sems[2].at[ci])

    def _pass(self, ins, outs, sems, ci, which):
        source, target = ((4, 2), (2, 4))[which]
        return self._copy(ins, outs, sems, ci, 3 + which, _relative(source), _relative(target),
                          rows=self._halves(ci)[which])

    def start(self, ins, outs, sems):
        me = _relative(0)
        for ci in range(len(self.chunks)):
            self._own(ins, outs, sems, ci).start()
        for ci in range(len(self.chunks)):
            self._copy(ins, outs, sems, ci, 1, me, _relative(4), from_shard=True).start()
            self._copy(ins, outs, sems, ci, 2, me, _relative(2), from_shard=True).start()
            if self._halves(ci) is None:
                self._copy(ins, outs, sems, ci, 3, me, _relative(6), from_shard=True).start()
        for ci in range(len(self.chunks)):
            self._copy(ins, outs, sems, ci, 0, me, _relative(1), from_shard=True).start()

    def mid(self, ins, outs, sems):
        me = _relative(0)
        cut = [ci for ci in range(len(self.chunks)) if self._halves(ci) is not None]
        for ci in cut:
            self._copy(ins, outs, sems, ci, 1, _relative(4), me).wait_recv()
            self._pass(ins, outs, sems, ci, 0).start()
        for ci in cut:
            self._copy(ins, outs, sems, ci, 2, _relative(2), me).wait_recv()
            self._pass(ins, outs, sems, ci, 1).start()

    def finish(self, ins, outs, sems):
        me, sibling = _relative(0), _relative(1)
        n = len(self.chunks)
        for ci in range(n):
            if self._halves(ci) is None:
                for s, k in ((1, 4), (2, 2), (3, 6)):
                    self._copy(ins, outs, sems, ci, s, _relative(k), me).wait_recv()
            else:
                h0, h1 = self._halves(ci)
                self._copy(ins, outs, sems, ci, 3, _relative(6), me, rows=h0).wait_recv()
                self._copy(ins, outs, sems, ci, 4, _relative(6), me, rows=h1).wait_recv()
            for j, k in enumerate(FAR):
                self._copy(ins, outs, sems, ci, 5 + j, _relative(k), sibling).start()
        for ci in range(n):
            self._copy(ins, outs, sems, ci, 0, sibling, me).wait_recv()
            for j, k in enumerate(FAR):
                self._copy(ins, outs, sems, ci, 5 + j, _relative(k | 1), me).wait_recv()
        for ci in range(n):
            self._copy(ins, outs, sems, ci, 0, me, sibling, from_shard=True).wait_send()
            self._copy(ins, outs, sems, ci, 1, me, _relative(4), from_shard=True).wait_send()
            self._copy(ins, outs, sems, ci, 2, me, _relative(2), from_shard=True).wait_send()
            if self._halves(ci) is None:
                self._copy(ins, outs, sems, ci, 3, me, _relative(6), from_shard=True).wait_send()
            else:
                self._pass(ins, outs, sems, ci, 0).wait_send()
                self._pass(ins, outs, sems, ci, 1).wait_send()
            for j, k in enumerate(FAR):
                self._copy(ins, outs, sems, ci, 5 + j, _relative(k), sibling).wait_send()
            self._own(ins, outs, sems, ci).wait()

    def done(self, results):
        for n, buf in zip(self.names, results):
            self.owner.bufs[n] = buf


class _Scatter:
    def __init__(self, me, far_index):
        self.me, self.far_index = me, far_index
        self.sends, self.owns, self.pairs, self.sums, self.fars = {}, {}, {}, {}, {}
        self.pair_cursor, self.far_cursor = {}, {}

    def add(self, name, send, own):
        self.sends[name] = send
        self.owns[name] = own
        self.pairs[name] = self.fars[name] = None
        self.pair_cursor[name] = 0

    def _rows(self, name):
        return self.sends[name].shape[1]

    def _add_ready_pairs(self):
        for name in self.sends:
            if name not in self.sums and self.pair_cursor[name] == self._rows(name):
                self.sums[name] = _add_pairs(self.sends[name], self.pairs[name], self.far_index, name="pair_" + name)
                self.far_cursor[name] = 0

    def _side(self, us, through=None):
        self._add_ready_pairs()
        names = list(self.sends)
        if through is not None:
            names = names[:names.index(through) + 1]
        pair_chunks = [(n, self.pair_cursor[n], self._rows(n) - self.pair_cursor[n]) for n in names
                       if self.pair_cursor[n] < self._rows(n)]
        for n, _, _ in pair_chunks:
            self.pair_cursor[n] = self._rows(n)
        far_chunks = _chunks(self.sums, self.far_cursor, us, RS_US_PER_MB,
                             through if through in self.sums else None) if self.sums else []
        return _ScatterSide(self, pair_chunks, far_chunks) if pair_chunks or far_chunks else None

    def add_blocks(self, name, blocks32, blocks16):
        self.add(name, blocks16, lax.dynamic_index_in_dim(blocks32, self.me, axis=0, keepdims=False))

    def add_cols(self, name, full32, full16):
        width = full32.shape[1] // N_DEV
        self.add(name, _blocks(full16, "cols"), lax.dynamic_slice_in_dim(full32, self.me * width, width, axis=1))

    def take(self, us):
        return self._side(us) if us >= MIN_RIDE_US else None

    def flush_pairs(self, name):
        side = self._side(0.0)
        if side is not None:
            _run_side(side, name)
        self._add_ready_pairs()

    def get(self, name):
        step = 0
        while name not in self.sums or self.far_cursor[name] < self._rows(name):
            _run_side(self._side(None, through=name), "scatter_%s_%d" % (name, step))
            step += 1
        return self.owns[name], self.pairs[name], self.fars[name]


class _ScatterSide:
    TO_SIBLING = (1, 5, 3, 7)

    def __init__(self, owner, pair_chunks, far_chunks):
        self.owner, self.pair_chunks, self.far_chunks = owner, pair_chunks, far_chunks
        self.pair_names = list(dict.fromkeys(n for n, _, _ in pair_chunks))
        self.far_names = list(dict.fromkeys(n for n, _, _ in far_chunks))
        ins = [(owner.sends[n], owner.pairs[n], (4,)) for n in self.pair_names]
        ins += [(owner.sums[n], owner.fars[n], (3,)) for n in self.far_names]
        old = [i for i, (_, buf, _) in enumerate(ins) if buf is not None]
        self.operands = [src for src, _, _ in ins] + [ins[i][1] for i in old]
        self.out_shape = [jax.ShapeDtypeStruct(slots + src.shape[1:], BF16) for src, _, slots in ins]
        self.aliases = {len(ins) + j: i for j, i in enumerate(old)}
        n_pair, n_far = 4 * len(pair_chunks), 3 * len(far_chunks)
        self.sems = [pltpu.SemaphoreType.DMA((max(n_pair, 1),)), pltpu.SemaphoreType.DMA((max(n_pair, 1),)),
                     pltpu.SemaphoreType.DMA((max(n_far, 1),)), pltpu.SemaphoreType.DMA((max(n_far, 1),))]

    def _copies(self, ins, outs, sems):
        copies = []
        for ci, (name, r0, rows) in enumerate(self.pair_chunks):
            w = self.pair_names.index(name)
            for j, k in enumerate(self.TO_SIBLING):
                copies.append(pltpu.make_async_remote_copy(
                    src_ref=ins[w].at[_index(_relative(k)), pl.ds(r0, rows)], dst_ref=outs[w].at[j, pl.ds(r0, rows)],
                    send_sem=sems[0].at[4 * ci + j], recv_sem=sems[1].at[4 * ci + j],
                    device_id=_relative(1), device_id_type=MESH))
        for ci, (name, r0, rows) in enumerate(self.far_chunks):
            w = len(self.pair_names) + self.far_names.index(name)
            for j, k in enumerate(FAR):
                copies.append(pltpu.make_async_remote_copy(
                    src_ref=ins[w].at[j, pl.ds(r0, rows)], dst_ref=outs[w].at[j, pl.ds(r0, rows)],
                    send_sem=sems[2].at[3 * ci + j], recv_sem=sems[3].at[3 * ci + j],
                    device_id=_relative(k), device_id_type=MESH))
        return copies

    def start(self, ins, outs, sems):
        for cp in self._copies(ins, outs, sems):
            cp.start()

    def mid(self, ins, outs, sems):
        pass

    def finish(self, ins, outs, sems):
        for cp in self._copies(ins, outs, sems):
            cp.wait()

    def done(self, results):
        for n, buf in zip(self.pair_names, results):
            self.owner.pairs[n] = buf
        for n, buf in zip(self.far_names, results[len(self.pair_names):]):
            self.owner.fars[n] = buf


class _Joined:
    def __init__(self, sides):
        self.sides = sides
        self.operands, self.out_shape, self.sems, self.aliases, self.spans = [], [], [], {}, []
        for s in sides:
            i0, o0, s0 = len(self.operands), len(self.out_shape), len(self.sems)
            self.operands += list(s.operands)
            self.out_shape += list(s.out_shape)
            self.sems += list(s.sems)
            self.aliases.update({i0 + i: o0 + o for i, o in s.aliases.items()})
            self.spans.append((slice(i0, len(self.operands)), slice(o0, len(self.out_shape)),
                               slice(s0, len(self.sems))))

    def start(self, ins, outs, sems):
        for s, (i, o, m) in zip(self.sides, self.spans):
            s.start(ins[i], outs[o], sems[m])

    def mid(self, ins, outs, sems):
        for s, (i, o, m) in zip(self.sides, self.spans):
            s.mid(ins[i], outs[o], sems[m])

    def finish(self, ins, outs, sems):
        for s, (i, o, m) in zip(self.sides, self.spans):
            s.finish(ins[i], outs[o], sems[m])

    def done(self, results):
        for s, (_, o, _) in zip(self.sides, self.spans):
            s.done(results[o])


def _join(*sides):
    sides = [s for s in sides if s is not None]
    if len(sides) <= 1:
        return sides[0] if sides else None
    return _Joined(sides)


def _pack_rows(vecs):
    rows = -(-sum(v.shape[0] for v in vecs) // 8) * 8
    width = max(v.shape[1] for v in vecs)

    def body(*refs):
        out = refs[-1]
        out[...] = jnp.zeros_like(out)
        r0 = 0
        for v in refs[:-1]:
            out[r0:r0 + v.shape[0], 0:v.shape[1]] = v[...]
            r0 += v.shape[0]

    vmem = pl.BlockSpec(memory_space=pltpu.VMEM)
    return pl.pallas_call(body, name="pack_small", in_specs=[vmem] * len(vecs), out_specs=vmem,
                          out_shape=jax.ShapeDtypeStruct((rows, width), F32))(*vecs)


def _sum_rows(inbox, shapes):
    def body(inbox_ref, *refs):
        outs, total = refs[:-1], refs[-1]
        acc = inbox_ref[0]
        for d in range(1, N_DEV):
            acc = acc + inbox_ref[d]
        total[...] = acc
        r0 = 0
        for o in outs:
            o[...] = total[r0:r0 + o.shape[0], 0:o.shape[1]]
            r0 += o.shape[0]

    vmem = pl.BlockSpec(memory_space=pltpu.VMEM)
    return pl.pallas_call(body, name="sum_small", in_specs=[vmem], out_specs=[vmem] * len(shapes),
                          out_shape=[jax.ShapeDtypeStruct(s, F32) for s in shapes],
                          scratch_shapes=[pltpu.VMEM(inbox.shape[1:], F32)])(inbox)


class _ShareRows:
    def __init__(self, mine):
        self.operands, self.aliases = [mine], {}
        self.out_shape = [jax.ShapeDtypeStruct((N_DEV,) + mine.shape, F32)]
        self.sems = [pltpu.SemaphoreType.DMA((N_DEV - 1,)), pltpu.SemaphoreType.DMA((N_DEV - 1,)),
                     pltpu.SemaphoreType.DMA(())]

    def _copy(self, ins, outs, sems, k, sender):
        return pltpu.make_async_remote_copy(
            src_ref=ins[0], dst_ref=outs[0].at[_index(sender)], send_sem=sems[0].at[k - 1], recv_sem=sems[1].at[k - 1],
            device_id=_relative(k), device_id_type=MESH)

    def _own(self, ins, outs, sems):
        return pltpu.make_async_copy(ins[0], outs[0].at[_index(_relative(0))], sems[2])

    def start(self, ins, outs, sems):
        self._own(ins, outs, sems).start()
        for k in range(1, N_DEV):
            self._copy(ins, outs, sems, k, _relative(0)).start()

    def mid(self, ins, outs, sems):
        pass

    def finish(self, ins, outs, sems):
        for k in range(1, N_DEV):
            self._copy(ins, outs, sems, k, _relative(k)).wait_recv()
            self._copy(ins, outs, sems, k, _relative(0)).wait_send()
        self._own(ins, outs, sems).wait()

    def done(self, results):
        self.inbox = results[0]


class _PartsToOwners:
    def __init__(self, mats):
        self.n, self.per = len(mats), mats[0].shape[0] // N_DEV
        self.operands, self.aliases = list(mats), {}
        self.out_shape = [jax.ShapeDtypeStruct((N_DEV, self.n, self.per) + mats[0].shape[1:], F32)]
        self.sems = [pltpu.SemaphoreType.DMA((self.n * (N_DEV - 1),))] * 2

    def _copies(self, ins, outs, sems):
        return [pltpu.make_async_remote_copy(
            src_ref=ins[j].at[pl.ds(self.per * _index(_relative(k)), self.per)], dst_ref=outs[0].at[k, j],
            send_sem=sems[0].at[self.n * (k - 1) + j], recv_sem=sems[1].at[self.n * (k - 1) + j],
            device_id=_relative(k), device_id_type=MESH) for k in range(1, N_DEV) for j in range(self.n)]

    def start(self, ins, outs, sems):
        for cp in self._copies(ins, outs, sems):
            cp.start()

    def mid(self, ins, outs, sems):
        pass

    def finish(self, ins, outs, sems):
        for cp in self._copies(ins, outs, sems):
            cp.wait()

    def done(self, results):
        self.stage = results[0]


def _sum_parts(mats, stage):
    n, per = len(mats), mats[0].shape[0] // N_DEV

    def body(*refs):
        stage_ref, out = refs[n], refs[n + 1]
        me = _index(_relative(0))
        for j in range(n):
            acc = refs[j][pl.ds(per * me, per)]
            for k in range(1, N_DEV):
                acc = acc + stage_ref[k, j]
            out[j] = acc

    vmem = pl.BlockSpec(memory_space=pltpu.VMEM)
    return pl.pallas_call(body, name="sum_small_parts", in_specs=[vmem] * (n + 1), out_specs=vmem,
                          out_shape=jax.ShapeDtypeStruct((n, per) + mats[0].shape[1:], F32),
                          compiler_params=pltpu.CompilerParams(vmem_limit_bytes=VMEM_LIMIT))(*mats, stage)


class _PartsToAll:
    def __init__(self, parts, rows):
        self.n, self.per = parts.shape[0], parts.shape[1]
        self.operands, self.aliases = [parts], {}
        self.out_shape = [jax.ShapeDtypeStruct((rows,) + parts.shape[2:], F32)] * self.n
        self.sems = [pltpu.SemaphoreType.DMA((self.n * (N_DEV - 1),))] * 2 + [pltpu.SemaphoreType.DMA((self.n,))]

    def _rows(self, ref, pos):
        return ref.at[pl.ds(self.per * _index(pos), self.per)]

    def _copy(self, ins, outs, sems, k, j, owner):
        return pltpu.make_async_remote_copy(
            src_ref=ins[0].at[j], dst_ref=self._rows(outs[j], owner),
            send_sem=sems[0].at[self.n * (k - 1) + j], recv_sem=sems[1].at[self.n * (k - 1) + j],
            device_id=_relative(k), device_id_type=MESH)

    def _own(self, ins, outs, sems, j):
        return pltpu.make_async_copy(ins[0].at[j], self._rows(outs[j], _relative(0)), sems[2].at[j])

    def start(self, ins, outs, sems):
        for j in range(self.n):
            self._own(ins, outs, sems, j).start()
            for k in range(1, N_DEV):
                self._copy(ins, outs, sems, k, j, _relative(0)).start()

    def mid(self, ins, outs, sems):
        pass

    def finish(self, ins, outs, sems):
        for j in range(self.n):
            for k in range(1, N_DEV):
                self._copy(ins, outs, sems, k, j, _relative(k)).wait_recv()
                self._copy(ins, outs, sems, k, j, _relative(0)).wait_send()
            self._own(ins, outs, sems, j).wait()

    def done(self, results):
        self.totals = list(results)


class _SmallSync:
    def __init__(self, vec_names, mat_names):
        self.vec_names, self.mat_names = vec_names, mat_names

    def begin(self, loss, grads):
        vecs = [loss] + [grads[n] for n in self.vec_names]
        self.shapes = [v.shape for v in vecs]
        self.mats = [_diag_blocks(grads[n]) for n in self.mat_names]
        self.share = _ShareRows(_pack_rows(vecs))
        self.to_owners = _PartsToOwners(self.mats)
        return _join(self.share, self.to_owners)

    def middle(self):
        self.sums = _sum_rows(self.share.inbox, self.shapes)
        self.to_all = _PartsToAll(_sum_parts(self.mats, self.to_owners.stage), self.mats[0].shape[0])
        return self.to_all

    def end(self):
        return self.sums[0], dict(zip(self.vec_names, self.sums[1:])), dict(zip(self.mat_names, self.to_all.totals))


def _block_diag(w):
    groups = []
    for g in range(N_RNN_GROUPS):
        placed = [jnp.pad(w[4 * g + b], ((RNN_BLOCK_W * b, RNN_BLOCK_W * (3 - b)),) * 2) for b in range(4)]
        groups.append(placed[0] + placed[1] + placed[2] + placed[3])
    return jnp.stack(groups)


def _diag_blocks(wg):
    blocks = []
    for n in range(4 * N_RNN_GROUPS):
        g, at = n // 4, RNN_BLOCK_W * (n % 4)
        blocks.append(wg[g, at:at + RNN_BLOCK_W, at:at + RNN_BLOCK_W])
    return jnp.stack(blocks)


def _heads_major(t, n_heads):
    return t.reshape(S, n_heads, HEAD_DIM).transpose(1, 0, 2)


def _heads_minor(t):
    return t.transpose(1, 0, 2).reshape(S, t.shape[0] * HEAD_DIM)


def _natural(gathered, how):
    n, r, c = gathered.shape
    if how == "rows":
        return gathered.reshape(n * r, c)
    return gathered.transpose(1, 0, 2).reshape(r, n * c)


def _blocks(full, how):
    if how == "rows":
        return full.reshape(N_DEV, full.shape[0] // N_DEV, full.shape[1])
    return full.reshape(full.shape[0], N_DEV, full.shape[1] // N_DEV).transpose(1, 0, 2)


def _forward_backward(x2, target, small, gather, scatter, sync):
    xb = x2.astype(MXU_DTYPE)
    w_in = _natural(gather.get("w_in"), "cols")
    proj, projb = _mm(xb, w_in, tm=1024, tn=512, tk=D, out_dtype=(F32, MXU_DTYPE), name="proj", side=gather.take(110))

    qt = projb[:, :OFF_K].T.reshape(N_KV, GROUP, HEAD_DIM, S)
    k2, v2 = projb[:, OFF_K:OFF_V], projb[:, OFF_V:OFF_RX]
    kp = jnp.pad(_heads_major(k2, N_KV), ((0, 0), (BLOCK, 0), (0, 0)))
    vp = jnp.pad(_heads_major(v2, N_KV), ((0, 0), (BLOCK, 0), (0, 0)))
    kt = jnp.pad(k2.T.reshape(N_KV, HEAD_DIM, S), ((0, 0), (0, 0), (BLOCK, 0)))
    vt = jnp.pad(v2.T.reshape(N_KV, HEAD_DIM, S), ((0, 0), (0, 0), (BLOCK, 0)))
    sink_row = jnp.repeat(small["attn_sinks"].reshape(N_KV, 1, GROUP), BLOCK, axis=2)
    ot = _attn_fwd(qt, kp, vt, sink_row, side=gather.take(120)).reshape(D, S)

    rconv_w = _natural(gather.get("rnn_conv_w"), "cols")
    rxc = _conv_fwd(proj, OFF_RX, rconv_w, small["rnn_conv_b"], tc=512, name="rnn_conv_fwd", side=gather.take(18))
    r, i = _lru_gates(rxc, small["lru_wa"], small["lru_wi"], small["lru_ba"], small["lru_bi"], side=gather.take(33))
    h, yrin = _lru_scan_fwd(r, i, rxc, proj, small["lru_lambda"], side=gather.take(53))

    w_ap = _natural(gather.get("w_attn_proj"), "rows")
    w_rp = _natural(gather.get("w_rnn_proj"), "rows")
    y_attn = _mm(ot, w_ap, ta=True, tm=1024, tn=1024, tk=D, name="attn_proj", side=gather.take(22))
    y_rnn = _mm(yrin, w_rp, tm=1024, tn=1024, tk=D_RNN, name="rnn_proj", side=gather.take(27))
    mixin = _gate_fwd(y_attn, y_rnn, proj, small["b_gate"], side=gather.take(25))
    w_out = _natural(gather.get("w_out"), "rows")
    mix = _mm(mixin, w_out, tm=1024, tn=1024, tk=D, name="mix_out", side=gather.take(22))
    x1, x1b, xhat1, rstd1 = _ln_fwd(x2, mix, small["ln1_g"], small["ln1_b"], side=gather.take(23))

    w_up = gather.get("ffn_w_up")
    up = _mm(x1b, w_up, tm=1024, tn=768, tk=D, b_block=768, name="ffn_up", side=gather.take(58))
    w_gate = gather.get("ffn_w_gate")
    gpre = _mm(x1b, w_gate, tm=1024, tn=768, tk=D, b_block=768, name="ffn_gate", side=gather.take(58))
    fconv_w = _natural(gather.get("ffn_conv_w"), "cols")
    fin = _ffn_act_fwd(up, gpre, fconv_w, small["ffn_conv_b"], side=gather.take())
    w_down = _natural(gather.get("ffn_w_down"), "rows")
    f = _mm(fin, w_down, tm=1024, tn=1024, tk=2048, name="ffn_down")
    loss, dpre2, dpre2b, d_ln2_g, d_ln2_b = _ln_loss_bwd(x1, f, small["ln2_g"], small["ln2_b"], target)

    grads = {"ln2_g": d_ln2_g, "ln2_b": d_ln2_b}
    both = (F32, BF16)
    g32, g16 = _mm(fin, dpre2b, ta=True, tm=1024, tn=1024, tk=S, out_dtype=both, name="d_ffn_w_down")
    scatter.add_blocks("ffn_w_down", _blocks(g32, "rows"), _blocks(g16, "rows"))
    dfin = _mm(dpre2b, w_down, tb=True, tm=1024, tn=1024, tk=D, name="d_fin", side=scatter.take(57))
    dup, dgpre, grads["ffn_conv_w"], grads["ffn_conv_b"] = _ffn_act_bwd(
        dfin, up, gpre, fconv_w, small["ffn_conv_b"], side=scatter.take(85))
    g32, g16 = _mm(x1b, dup, ta=True, tm=1024, tn=768, tk=S, out_dtype=both, out_block=768, name="d_ffn_w_up",
                   side=scatter.take(57))
    scatter.add_blocks("ffn_w_up", g32, g16)
    g32, g16 = _mm(x1b, dgpre, ta=True, tm=1024, tn=768, tk=S, out_dtype=both, out_block=768, name="d_ffn_w_gate",
                   side=scatter.take(56))
    scatter.add_blocks("ffn_w_gate", g32, g16)
    dx1 = _mm(dup, w_up, tb=True, tm=1024, tn=1024, tk=768, b_block=768, name="d_x1_up", side=scatter.take(68))
    dx1 = _mm(dgpre, w_gate, tb=True, tm=1024, tn=1024, tk=768, b_block=768, add=dx1, name="d_x1_gate",
              side=scatter.take(70))
    dpre1, dpre1b, grads["ln1_g"], grads["ln1_b"] = _ln_bwd(dx1, dpre2, xhat1, rstd1, small["ln1_g"],
                                                            side=scatter.take(24))

    g32, g16 = _mm(mixin, dpre1b, ta=True, tm=1024, tn=1024, tk=S, out_dtype=both, name="d_w_out",
                   side=scatter.take(26))
    scatter.add_blocks("w_out", _blocks(g32, "rows"), _blocks(g16, "rows"))
    dmix = _mm(dpre1b, w_out, tb=True, tm=1024, tn=1024, tk=D, name="d_mixin", side=scatter.take(22))
    dya, dyr, dgl_a, dgl_r, db_a, db_r = _gate_bwd(dmix, y_attn, y_rnn, proj, small["b_gate"], side=scatter.take(36))
    grads["b_gate"] = jnp.concatenate([db_a, db_r], axis=1)
    g32, g16 = _mm(ot, dya, tm=1024, tn=1024, tk=S, out_dtype=both, name="d_w_attn_proj", side=scatter.take(38))
    scatter.add_blocks("w_attn_proj", _blocks(g32, "rows"), _blocks(g16, "rows"))
    g32, g16 = _mm(yrin, dyr, ta=True, tm=1280, tn=1024, tk=S, out_dtype=both, name="d_w_rnn_proj",
                   side=scatter.take(27))
    scatter.add_blocks("w_rnn_proj", _blocks(g32, "rows"), _blocks(g16, "rows"))
    dot_ = _mm(w_ap, dya, tb=True, tm=1024, tn=1024, tk=D, out_dtype=MXU_DTYPE, name="d_o", side=scatter.take(22))
    dyrin = _mm(dyr, w_rp, tb=True, tm=1024, tn=1280, tk=D, name="d_yrin", side=scatter.take(27))

    dry, dzr, dzi, drxc_in, grads["lru_ba"], grads["lru_bi"], grads["lru_lambda"] = _lru_scan_bwd(
        dyrin, proj, h, r, i, rxc, small["lru_lambda"], side=scatter.take(94))
    grads["lru_wa"], grads["lru_wi"] = _lru_gate_wgrad(rxc, dzr, dzi, side=scatter.take(22))
    drxc = _lru_gate_xgrad(dzr, dzi, small["lru_wa"], small["lru_wi"], drxc_in, side=scatter.take(33))
    drx, grads["rnn_conv_w"], grads["rnn_conv_b"] = _conv_bwd(drxc, proj, OFF_RX, rconv_w, tc=512,
                                                             name="rnn_conv_bwd", side=scatter.take(29))

    dqt, dk, dv, dsink = _attn_bwd(qt, kp, kt, vp, sink_row, dot_.reshape(N_KV, GROUP, HEAD_DIM, S),
                                   side=scatter.take(100))
    grads["attn_sinks"] = dsink.reshape(1, N_KV * GROUP)
    dproj = jnp.concatenate([
        dqt.reshape(D, S).T,
        _heads_minor(dk[:, BLOCK:, :]).astype(MXU_DTYPE),
        _heads_minor(dv[:, BLOCK:, :]).astype(MXU_DTYPE),
        drx, dry, dgl_a, dgl_r], axis=1)
    g32, g16 = _mm(xb, dproj, ta=True, tm=1024, tn=512, tk=S, out_dtype=both, name="d_w_in",
                   side=_join(scatter.take(110), sync.begin(loss, grads)))
    scatter.add_cols("w_in", g32, g16)
    scatter.flush_pairs("pairs_w_in")
    dx = _mm(dproj, w_in, tb=True, tm=1024, tn=1024, tk=512, add=dpre1, add_scale=ALPHA, name="d_x",
             side=_join(scatter.take(300), sync.middle()))
    return dx


SHARDED = (
    ("w_in", "cols", 128), ("w_attn_proj", "rows", 32), ("w_rnn_proj", "rows", 32), ("w_out", "rows", 32),
    ("ffn_w_up", "cols", 128), ("ffn_w_gate", "cols", 128), ("ffn_w_down", "rows", 64),
)
SMALL_REPLICATED = ("b_gate", "rnn_conv_b", "lru_wa", "lru_ba", "lru_wi", "lru_bi", "lru_lambda", "attn_sinks",
                    "ln1_g", "ln1_b", "ffn_conv_b", "ln2_g", "ln2_b")
SMALL_SHARDED = ("rnn_conv_w", "ffn_conv_w")
SMALL_MATS = ("lru_wa", "lru_wi")
WEIGHTS = ("w_in", "b_gate", "rnn_conv_w", "rnn_conv_b", "lru_wa", "lru_ba", "lru_wi", "lru_bi", "lru_lambda",
           "attn_sinks", "w_attn_proj", "w_rnn_proj", "w_out", "ln1_g", "ln1_b", "ffn_w_up", "ffn_w_gate",
           "ffn_conv_w", "ffn_conv_b", "ffn_w_down", "ln2_g", "ln2_b")


def kernel(x, w_in, b_gate, rnn_conv_w, rnn_conv_b, lru_wa, lru_ba, lru_wi, lru_bi, lru_lambda, attn_sinks, w_attn_proj, w_rnn_proj, w_out, ln1_g, ln1_b, ffn_w_up, ffn_w_gate, ffn_conv_w, ffn_conv_b, ffn_w_down, ln2_g, ln2_b, loss_target, m_w_in, m_b_gate, m_rnn_conv_w, m_rnn_conv_b, m_lru_wa, m_lru_ba, m_lru_wi, m_lru_bi, m_lru_lambda, m_attn_sinks, m_w_attn_proj, m_w_rnn_proj, m_w_out, m_ln1_g, m_ln1_b, m_ffn_w_up, m_ffn_w_gate, m_ffn_conv_w, m_ffn_conv_b, m_ffn_w_down, m_ln2_g, m_ln2_b, v_w_in, v_b_gate, v_rnn_conv_w, v_rnn_conv_b, v_lru_wa, v_lru_ba, v_lru_wi, v_lru_bi, v_lru_lambda, v_attn_sinks, v_w_attn_proj, v_w_rnn_proj, v_w_out, v_ln1_g, v_ln1_b, v_ffn_w_up, v_ffn_w_gate, v_ffn_conv_w, v_ffn_conv_b, v_ffn_w_down, v_ln2_g, v_ln2_b):
    given = dict(locals())
    wsh = {n: given[n][0] for n in WEIGHTS}
    msh = {n: given["m_" + n][0] for n in WEIGHTS}
    vsh = {n: given["v_" + n][0] for n in WEIGHTS}
    m_given = {n: given["m_" + n] for n in WEIGHTS}
    v_given = {n: given["v_" + n] for n in WEIGHTS}
    me = 4 * lax.axis_index("x") + 2 * lax.axis_index("y") + lax.axis_index("c")

    order = ("w_in", "rnn_conv_w", "ffn_conv_w", "w_attn_proj", "w_rnn_proj", "w_out", "ffn_w_up", "ffn_w_gate",
             "ffn_w_down")
    gather = _Gather({n: wsh[n] if n in SMALL_SHARDED else wsh[n].astype(MXU_DTYPE) for n in order})
    _run_side(gather.take(through="ffn_conv_w"), "gather_first")
    small = {n: given[n] for n in SMALL_REPLICATED}
    small["lru_wa"] = _block_diag(wsh["lru_wa"])
    small["lru_wi"] = _block_diag(wsh["lru_wi"])
    scatter = _Scatter(me, jnp.stack([_index(_relative(k)) for k in FAR]).astype(jnp.int32))

    vec_names = tuple(n for n in SMALL_REPLICATED if n not in SMALL_MATS) + SMALL_SHARDED
    sync = _SmallSync(vec_names, SMALL_MATS)
    dx = _forward_backward(x[0], loss_target[0], small, gather, scatter, sync)

    loss_total, g_small, mat_sums = sync.end()
    loss_total = loss_total.reshape(())
    for n in SMALL_SHARDED:
        width = wsh[n].shape[1]
        g_small[n] = lax.dynamic_slice_in_dim(g_small[n], me * width, width, axis=1)
    g_small = {n: g_small[n].reshape(given[n].shape) for n in vec_names}
    out = {}
    results = _adamw_many(*[[d[n] for n in vec_names] for d in (given, m_given, v_given, g_small)])
    for n, delta, nm, nv in zip(vec_names, *results):
        out[n] = (g_small[n], delta, nm, nv)
    for n in SMALL_MATS:
        g = mat_sums[n].reshape(given[n].shape)
        out[n] = (g, *_adamw_blocks(given[n], m_given[n], v_given[n], g, name="adamw_" + n))

    tile_rows = {n: tr for n, _, tr in SHARDED}
    for n in list(scatter.sends):
        own, pair, far = scatter.get(n)
        res = _reduce_adamw(wsh[n], msh[n], vsh[n], own, pair, far, tr=tile_rows[n], name="adamw_" + n)
        out[n] = tuple(r[None] for r in res)

    outputs = [loss_total, dx[None]]
    for kind in range(4):
        outputs += [out[n][kind] for n in WEIGHTS]
    return tuple(outputs)
```

```python
import math

import jax
import jax.numpy as jnp
from jax import lax
from jax.experimental import pallas as pl
from jax.experimental.pallas import tpu as pltpu

F32 = jnp.float32
BF16 = jnp.bfloat16
MXU_DTYPE = jnp.bfloat16

N_DEV = 8
S = 2048
D = 2048
HEAD_DIM = 64
N_KV = 4
GROUP = 8
BLOCK = 128
D_KV = N_KV * HEAD_DIM
D_RNN = 2560
RNN_GROUP = 640
N_RNN_GROUPS = D_RNN // RNN_GROUP
RNN_BLOCK_W = 160
RNN_CONV_W = 4
LRU_C = 8.0
D_FF = 6144
FFN_CONV_W = 3
D_IN = 11776
OFF_K = 2048
OFF_V = 2304
OFF_RX = 2560
OFF_RY = 5120
OFF_GA = 7680
OFF_GR = 9728
LN_EPS = 1e-5
ALPHA = 2.0 ** 0.25
ADAM_LR = 0.001
ADAM_B1 = 0.9
ADAM_B2 = 0.999
ADAM_EPS = 1e-08
ADAM_WD = 0.01
ADAM_STEP = 10
NEG = -1e30
VMEM_LIMIT = 56 * 1024 * 1024
MID_RIDE_TENTHS = 6
MESH = pl.DeviceIdType.MESH
GELU_C = math.sqrt(2.0 / math.pi)


def _cparams(*sem):
    return pltpu.CompilerParams(dimension_semantics=sem or None, vmem_limit_bytes=VMEM_LIMIT)


def _call(body, *, name, grid, in_specs, out_specs, out_shape, operands, semantics, scratch_shapes=(), side=None):
    single = not isinstance(out_shape, (list, tuple))
    out_shape = [out_shape] if single else list(out_shape)
    out_specs = [out_specs] if single else list(out_specs)
    in_specs = list(in_specs)
    scratch_shapes = list(scratch_shapes)
    if side is None:
        res = pl.pallas_call(
            body, name=name, grid=grid, in_specs=in_specs, out_specs=out_specs, out_shape=out_shape,
            scratch_shapes=scratch_shapes, compiler_params=_cparams(*semantics))(*operands)
        return res[0] if single else res
    n_in, n_out, n_scr = len(in_specs), len(out_shape), len(scratch_shapes)
    s_in, s_out = len(side.operands), len(side.out_shape)
    hbm = pl.BlockSpec(memory_space=pltpu.HBM)
    steps = math.prod(grid)
    mid_step = (steps * MID_RIDE_TENTHS) // 10

    def with_copies(*refs):
        core_in, side_in = refs[:n_in], refs[n_in:n_in + s_in]
        o0 = n_in + s_in
        core_out, side_out = refs[o0:o0 + n_out], refs[o0 + n_out:o0 + n_out + s_out]
        c0 = o0 + n_out + s_out
        core_scr, sems = refs[c0:c0 + n_scr], refs[c0 + n_scr:]
        step = 0
        for d, size in enumerate(grid):
            step = step * size + pl.program_id(d)

        @pl.when(step == 0)
        def _():
            side.start(side_in, side_out, sems)

        body(*core_in, *core_out, *core_scr)

        @pl.when(step == mid_step)
        def _():
            side.mid(side_in, side_out, sems)

        @pl.when(step == steps - 1)
        def _():
            side.finish(side_in, side_out, sems)

    res = pl.pallas_call(
        with_copies, name=name, grid=grid,
        in_specs=in_specs + [hbm] * s_in, out_specs=out_specs + [hbm] * s_out,
        out_shape=out_shape + list(side.out_shape),
        scratch_shapes=scratch_shapes + list(side.sems),
        input_output_aliases={n_in + i: n_out + o for i, o in side.aliases.items()},
        compiler_params=_cparams(*(("arbitrary",) * len(grid))))(*operands, *side.operands)
    side.done(res[n_out:])
    return res[0] if single else res[:n_out]


def _run_side(side, name):
    def body(*refs):
        s_in, s_out = len(side.operands), len(side.out_shape)
        side.start(refs[:s_in], refs[s_in:s_in + s_out], refs[s_in + s_out:])
        side.mid(refs[:s_in], refs[s_in:s_in + s_out], refs[s_in + s_out:])
        side.finish(refs[:s_in], refs[s_in:s_in + s_out], refs[s_in + s_out:])

    hbm = pl.BlockSpec(memory_space=pltpu.HBM)
    res = pl.pallas_call(
        body, name=name, in_specs=[hbm] * len(side.operands), out_specs=[hbm] * len(side.out_shape),
        out_shape=list(side.out_shape), scratch_shapes=list(side.sems),
        input_output_aliases=dict(side.aliases))(*side.operands)
    side.done(res)


def _gelu(x):
    x2 = x * x
    t = jnp.tanh(GELU_C * (x + 0.044715 * x * x2))
    g = 0.5 * x * (1.0 + t)
    dg = 0.5 * (1.0 + t) + 0.5 * x * (1.0 - t * t) * (GELU_C * (1.0 + 3.0 * 0.044715 * x2))
    return g, dg


def _sigmoid(x):
    return 1.0 / (1.0 + jnp.exp(-x))


def _softplus(x):
    z = jnp.exp(-jnp.abs(x))
    small = z * (1.0 - z * (0.5 - z * (1.0 / 3.0 - 0.25 * z)))
    return jnp.maximum(x, 0.0) + jnp.where(z < 0.02, small, jnp.log(1.0 + z))


def _one_minus_exp(x):
    series = -x * (1.0 + x * (0.5 + x * (1.0 / 6.0 + x * (1.0 / 24.0))))
    return jnp.where(x > -0.03, series, 1.0 - jnp.exp(x))


def _colsum(v):
    return jnp.sum(v, axis=0, keepdims=True)


def _mm(a, b, *, tm, tn, tk, name, ta=False, tb=False, out_dtype=F32, b_block=None, out_block=None, add=None,
        add_scale=1.0, side=None):
    out_dtypes = out_dtype if isinstance(out_dtype, tuple) else (out_dtype,)
    if ta:
        k_dim, m_dim = a.shape
    else:
        m_dim, k_dim = a.shape
    if b_block is None:
        n_dim = b.shape[0] if tb else b.shape[1]
    else:
        n_dim = b.shape[1] if tb else b.shape[0] * b_block
    assert m_dim % tm == 0 and n_dim % tn == 0 and k_dim % tk == 0, (name, m_dim, n_dim, k_dim)
    nk = k_dim // tk
    dims = (((0 if ta else 1,), (1 if tb else 0,)), ((), ()))
    has_add = add is not None

    def body(*refs):
        a_ref, b_ref = refs[0], refs[1]
        add_ref = refs[2] if has_add else None
        first_out = 3 if has_add else 2
        o_refs = refs[first_out:first_out + len(out_dtypes)]

        def product():
            return lax.dot_general(a_ref[...].astype(MXU_DTYPE), b_ref[...].astype(MXU_DTYPE), dims,
                                   preferred_element_type=F32)

        def finish(acc):
            if has_add:
                acc = acc + add_scale * add_ref[...]
            for o_ref in o_refs:
                o_ref[...] = acc.astype(o_ref.dtype)

        if nk == 1:
            finish(product())
        else:
            acc_ref = refs[-1]
            k = pl.program_id(2)

            @pl.when(k == 0)
            def _():
                acc_ref[...] = jnp.zeros_like(acc_ref)

            acc_ref[...] += product()

            @pl.when(k == nk - 1)
            def _():
                finish(acc_ref[...])

    if ta:
        a_spec = pl.BlockSpec((tk, tm), lambda i, j, k: (k, i))
    else:
        a_spec = pl.BlockSpec((tm, tk), lambda i, j, k: (i, k))
    if b_block is None:
        if tb:
            b_spec = pl.BlockSpec((tn, tk), lambda i, j, k: (j, k))
        else:
            b_spec = pl.BlockSpec((tk, tn), lambda i, j, k: (k, j))
    elif tb:
        assert b_block % tk == 0
        b_spec = pl.BlockSpec((None, tn, tk), lambda i, j, k: ((k * tk) // b_block, j, ((k * tk) % b_block) // tk))
    else:
        assert b_block % tn == 0
        b_spec = pl.BlockSpec((None, tk, tn), lambda i, j, k: ((j * tn) // b_block, k, ((j * tn) % b_block) // tn))
    in_specs = [a_spec, b_spec]
    operands = [a, b]
    if has_add:
        in_specs.append(pl.BlockSpec((tm, tn), lambda i, j, k: (i, j)))
        operands.append(add)
    if out_block is None:
        out_spec = pl.BlockSpec((tm, tn), lambda i, j, k: (i, j))
        out_dims = (m_dim, n_dim)
    else:
        assert out_block % tn == 0
        out_spec = pl.BlockSpec((None, tm, tn), lambda i, j, k: ((j * tn) // out_block, i, ((j * tn) % out_block) // tn))
        out_dims = (n_dim // out_block, m_dim, out_block)
    res = _call(
        body,
        name=name,
        grid=(m_dim // tm, n_dim // tn, nk),
        in_specs=in_specs,
        out_specs=[out_spec] * len(out_dtypes),
        out_shape=[jax.ShapeDtypeStruct(out_dims, dt) for dt in out_dtypes],
        scratch_shapes=[pltpu.VMEM((tm, tn), F32)] if nk > 1 else [],
        semantics=("parallel", "parallel", "arbitrary"),
        operands=tuple(operands),
        side=side,
    )
    return res if isinstance(out_dtype, tuple) else res[0]


def _attn_bias(bias_ref, h):
    key = lax.broadcasted_iota(jnp.int32, (2 * BLOCK, GROUP * BLOCK), 0)
    col = lax.broadcasted_iota(jnp.int32, (2 * BLOCK, GROUP * BLOCK), 1)
    dist = BLOCK + (col & (BLOCK - 1)) - key
    head = h * GROUP + (col >> 7) + 1
    slope = jnp.exp(head.astype(F32) * (-0.25 * math.log(2.0)))
    bias = jnp.where((dist >= 0) & (dist < BLOCK), -slope * dist.astype(F32), NEG)
    bias_ref[1] = bias
    bias_ref[0] = jnp.where(key < BLOCK, NEG, bias)


def _attn_probs(kb, qt, bias, sink):
    s = jnp.dot(kb, qt, preferred_element_type=F32) * (HEAD_DIM ** -0.5) + bias
    m = jnp.maximum(jnp.max(s, axis=0, keepdims=True), sink)
    e = jnp.exp(s - m)
    e_sink = jnp.exp(sink - m)
    inv = 1.0 / (jnp.sum(e, axis=0, keepdims=True) + e_sink)
    return e * inv, e_sink * inv


def _heads_on_lanes(ref, r0):
    return jnp.concatenate([ref[g, :, pl.ds(r0, BLOCK)] for g in range(GROUP)], axis=1)


def _attn_fwd(qt, kp, vt, sink_row, side=None):
    cols = GROUP * BLOCK

    def body(q_ref, k_ref, vt_ref, sink_ref, o_ref, bias_ref):
        _attn_bias(bias_ref, pl.program_id(0))
        sink = sink_ref[...]

        def step(n, carry):
            r0 = pl.multiple_of(n * BLOCK, BLOCK)
            p, _ = _attn_probs(k_ref[pl.ds(r0, 2 * BLOCK), :], _heads_on_lanes(q_ref, r0),
                               bias_ref[jnp.minimum(n, 1)], sink)
            o = jnp.dot(vt_ref[:, pl.ds(r0, 2 * BLOCK)], p.astype(MXU_DTYPE), preferred_element_type=F32)
            for g in range(GROUP):
                o_ref[g, :, pl.ds(r0, BLOCK)] = o[:, g * BLOCK:(g + 1) * BLOCK].astype(o_ref.dtype)
            return carry

        lax.fori_loop(0, S // BLOCK, step, 0)

    hm = pl.BlockSpec((None, GROUP, HEAD_DIM, S), lambda h: (h, 0, 0, 0))
    return _call(
        body,
        name="attn_fwd",
        grid=(N_KV,),
        in_specs=[
            hm,
            pl.BlockSpec((None, BLOCK + S, HEAD_DIM), lambda h: (h, 0, 0)),
            pl.BlockSpec((None, HEAD_DIM, BLOCK + S), lambda h: (h, 0, 0)),
            pl.BlockSpec((None, 1, cols), lambda h: (h, 0, 0)),
        ],
        out_specs=hm,
        out_shape=jax.ShapeDtypeStruct((N_KV, GROUP, HEAD_DIM, S), MXU_DTYPE),
        scratch_shapes=[pltpu.VMEM((2, 2 * BLOCK, cols), F32)],
        semantics=("parallel",),
        operands=(qt, kp, vt, sink_row),
        side=side,
    )


def _attn_bwd(qt, kp, kt, vp, sink_row, dot_, side=None):
    cols = GROUP * BLOCK

    def body(q_ref, k_ref, kt_ref, v_ref, sink_ref, do_ref, dq_ref, dk_ref, dv_ref, dsink_ref, bias_ref):
        _attn_bias(bias_ref, pl.program_id(0))
        sink = sink_ref[...]
        dk_ref[...] = jnp.zeros_like(dk_ref)
        dv_ref[...] = jnp.zeros_like(dv_ref)
        nt = (((1,), (1,)), ((), ()))

        def step(n, sink_acc):
            r0 = pl.multiple_of(n * BLOCK, BLOCK)
            band = pl.ds(r0, 2 * BLOCK)
            qn = _heads_on_lanes(q_ref, r0)
            don = _heads_on_lanes(do_ref, r0)
            p, p_sink = _attn_probs(k_ref[band, :], qn, bias_ref[jnp.minimum(n, 1)], sink)
            dp = jnp.dot(v_ref[band, :], don, preferred_element_type=F32)
            delta = jnp.sum(p * dp, axis=0, keepdims=True)
            ds = (p * (dp - delta) * (HEAD_DIM ** -0.5)).astype(MXU_DTYPE)
            dq = jnp.dot(kt_ref[:, band], ds, preferred_element_type=F32)
            for g in range(GROUP):
                dq_ref[g, :, pl.ds(r0, BLOCK)] = dq[:, g * BLOCK:(g + 1) * BLOCK].astype(dq_ref.dtype)
            dk_ref[band, :] += lax.dot_general(ds, qn, nt, preferred_element_type=F32)
            dv_ref[band, :] += lax.dot_general(p.astype(MXU_DTYPE), don, nt, preferred_element_type=F32)
            return sink_acc - p_sink * delta

        sink_acc = lax.fori_loop(0, S // BLOCK, step, jnp.zeros((1, cols), F32))
        for g in range(GROUP):
            dsink_ref[g:g + 1, :] = jnp.sum(sink_acc[:, g * BLOCK:(g + 1) * BLOCK], axis=1, keepdims=True)

    hm = pl.BlockSpec((None, GROUP, HEAD_DIM, S), lambda h: (h, 0, 0, 0))
    kv = pl.BlockSpec((None, BLOCK + S, HEAD_DIM), lambda h: (h, 0, 0))
    return _call(
        body,
        name="attn_bwd",
        grid=(N_KV,),
        in_specs=[hm, kv, pl.BlockSpec((None, HEAD_DIM, BLOCK + S), lambda h: (h, 0, 0)), kv,
                  pl.BlockSpec((None, 1, cols), lambda h: (h, 0, 0)), hm],
        out_specs=[hm, kv, kv, pl.BlockSpec((None, GROUP, 1), lambda h: (h, 0, 0))],
        out_shape=[
            jax.ShapeDtypeStruct((N_KV, GROUP, HEAD_DIM, S), MXU_DTYPE),
            jax.ShapeDtypeStruct((N_KV, BLOCK + S, HEAD_DIM), F32),
            jax.ShapeDtypeStruct((N_KV, BLOCK + S, HEAD_DIM), F32),
            jax.ShapeDtypeStruct((N_KV, GROUP, 1), F32),
        ],
        scratch_shapes=[pltpu.VMEM((2, 2 * BLOCK, cols), F32)],
        semantics=("parallel",),
        operands=(qt, kp, kt, vp, sink_row, dot_),
        side=side,
    )


PAD = 8
CHUNK = 256


def _past_taps(xpad_ref, r0, width):
    ext = xpad_ref[pl.ds(r0, CHUNK + PAD), :]
    taps = []
    for k in range(width):
        back = width - 1 - k
        taps.append((ext if back == 0 else pltpu.roll(ext, back, 0))[PAD:, :])
    return taps


def _future_taps(xpad_ref, r0, width):
    ext = xpad_ref[pl.ds(r0, CHUNK + PAD), :]
    taps = []
    for ahead in range(width):
        taps.append((ext if ahead == 0 else pltpu.roll(ext, CHUNK + PAD - ahead, 0))[:CHUNK, :])
    return taps


def _conv_fwd(src, col0, w, b, *, tc, name, side=None):
    width, c_dim = w.shape

    def body(x_ref, w_ref, b_ref, o_ref, xpad_ref):
        xpad_ref[pl.ds(0, PAD), :] = jnp.zeros((PAD, tc), F32)
        xpad_ref[pl.ds(PAD, S), :] = x_ref[...]
        wv = w_ref[...]
        bv = b_ref[...]

        def step(ci, carry):
            r0 = pl.multiple_of(ci * CHUNK, CHUNK)
            taps = _past_taps(xpad_ref, r0, width)
            y = bv + taps[0] * wv[0:1, :]
            for k in range(1, width):
                y = y + taps[k] * wv[k:k + 1, :]
            o_ref[pl.ds(r0, CHUNK), :] = y
            return carry

        lax.fori_loop(0, S // CHUNK, step, 0)

    return _call(
        body,
        name=name,
        grid=(c_dim // tc,),
        in_specs=[
            pl.BlockSpec((S, tc), lambda j: (0, col0 // tc + j)),
            pl.BlockSpec((width, tc), lambda j: (0, j)),
            pl.BlockSpec((1, tc), lambda j: (0, j)),
        ],
        out_specs=pl.BlockSpec((S, tc), lambda j: (0, j)),
        out_shape=jax.ShapeDtypeStruct((S, c_dim), F32),
        scratch_shapes=[pltpu.VMEM((S + PAD, tc), F32)],
        semantics=("parallel",),
        operands=(src, w, b),
        side=side,
    )


def _conv_bwd(dy, src, col0, w, *, tc, name, side=None):
    width, c_dim = w.shape

    def body(dy_ref, x_ref, w_ref, dx_ref, dw_ref, db_ref, xpad_ref, dpad_ref):
        xpad_ref[pl.ds(0, PAD), :] = jnp.zeros((PAD, tc), F32)
        xpad_ref[pl.ds(PAD, S), :] = x_ref[...]
        dpad_ref[pl.ds(0, S), :] = dy_ref[...]
        dpad_ref[pl.ds(S, PAD), :] = jnp.zeros((PAD, tc), F32)
        wv = w_ref[...]

        def step(ci, acc):
            r0 = pl.multiple_of(ci * CHUNK, CHUNK)
            past = _past_taps(xpad_ref, r0, width)
            ahead = _future_taps(dpad_ref, r0, width)
            d = ahead[0]
            dx = d * wv[width - 1:width, :]
            for j in range(1, width):
                dx = dx + ahead[j] * wv[width - 1 - j:width - j, :]
            dx_ref[pl.ds(r0, CHUNK), :] = dx.astype(dx_ref.dtype)
            return tuple(acc[k] + _colsum(past[k] * d) for k in range(width)) + (acc[width] + _colsum(d),)

        zero = jnp.zeros((1, tc), F32)
        acc = lax.fori_loop(0, S // CHUNK, step, (zero,) * (width + 1))
        for k in range(width):
            dw_ref[k:k + 1, :] = acc[k]
        db_ref[...] = acc[width]

    return _call(
        body,
        name=name,
        grid=(c_dim // tc,),
        in_specs=[
            pl.BlockSpec((S, tc), lambda j: (0, j)),
            pl.BlockSpec((S, tc), lambda j: (0, col0 // tc + j)),
            pl.BlockSpec((width, tc), lambda j: (0, j)),
        ],
        out_specs=[
            pl.BlockSpec((S, tc), lambda j: (0, j)),
            pl.BlockSpec((width, tc), lambda j: (0, j)),
            pl.BlockSpec((1, tc), lambda j: (0, j)),
        ],
        out_shape=[
            jax.ShapeDtypeStruct((S, c_dim), MXU_DTYPE),
            jax.ShapeDtypeStruct((width, c_dim), F32),
            jax.ShapeDtypeStruct((1, c_dim), F32),
        ],
        scratch_shapes=[pltpu.VMEM((S + PAD, tc), F32), pltpu.VMEM((S + PAD, tc), F32)],
        semantics=("parallel",),
        operands=(dy, src, w),
        side=side,
    )


SCAN_TC = 256


def _lru_gates(rxc, wa, wi, ba, bi, side=None):
    tm = 512

    def body(x_ref, wa_ref, wi_ref, ba_ref, bi_ref, r_ref, i_ref):
        xv = x_ref[...].astype(MXU_DTYPE)
        r_ref[...] = _sigmoid(jnp.dot(xv, wa_ref[...].astype(MXU_DTYPE), preferred_element_type=F32) + ba_ref[...])
        i_ref[...] = _sigmoid(jnp.dot(xv, wi_ref[...].astype(MXU_DTYPE), preferred_element_type=F32) + bi_ref[...])

    x_spec = pl.BlockSpec((tm, RNN_GROUP), lambda g, i: (i, g))
    w_spec = pl.BlockSpec((None, RNN_GROUP, RNN_GROUP), lambda g, i: (g, 0, 0))
    b_spec = pl.BlockSpec((1, RNN_GROUP), lambda g, i: (0, g))
    return _call(
        body,
        name="lru_gates",
        grid=(N_RNN_GROUPS, S // tm),
        in_specs=[x_spec, w_spec, w_spec, b_spec, b_spec],
        out_specs=[x_spec, x_spec],
        out_shape=[jax.ShapeDtypeStruct((S, D_RNN), F32)] * 2,
        semantics=("parallel", "parallel"),
        operands=(rxc, wa, wi, ba, bi),
        side=side,
    )


def _scan_down(a, u, row):
    for d in (1, 2, 4):
        a_s = jnp.where(row >= d, pltpu.roll(a, d, 0), 1.0)
        u_s = jnp.where(row >= d, pltpu.roll(u, d, 0), 0.0)
        u = a * u_s + u
        a = a * a_s
    return a, u


def _scan_up(a, u, row):
    for d in (1, 2, 4):
        a_s = jnp.where(row < 8 - d, pltpu.roll(a, 8 - d, 0), 1.0)
        u_s = jnp.where(row < 8 - d, pltpu.roll(u, 8 - d, 0), 0.0)
        u = a * u_s + u
        a = a * a_s
    return a, u


def _lru_scan_fwd(r, i, rxc, proj, lam, side=None):
    tc = SCAN_TC

    def body(r_ref, i_ref, x_ref, ry_ref, lam_ref, h_ref, y_ref):
        rate = LRU_C * _softplus(-lam_ref[...])
        row = lax.broadcasted_iota(jnp.int32, (8, tc), 0)

        def step(ci, carry):
            r0 = pl.multiple_of(ci * 16, 16)
            log_a = -rate * r_ref[pl.ds(r0, 16), :]
            a16 = jnp.exp(log_a)
            u16 = jnp.sqrt(_one_minus_exp(2.0 * log_a)) * (i_ref[pl.ds(r0, 16), :] * x_ref[pl.ds(r0, 16), :])
            hs = []
            for half in range(2):
                a_cum, h0 = _scan_down(a16[8 * half:8 * half + 8, :], u16[8 * half:8 * half + 8, :], row)
                h = a_cum * carry + h0
                carry = jnp.broadcast_to(h[7:8, :], (8, tc))
                hs.append(h)
            h16 = jnp.concatenate(hs, axis=0)
            h_ref[pl.ds(r0, 16), :] = h16
            y_ref[pl.ds(r0, 16), :] = (h16 * _gelu(ry_ref[pl.ds(r0, 16), :])[0]).astype(y_ref.dtype)
            return carry

        lax.fori_loop(0, S // 16, step, jnp.zeros((8, tc), F32))

    col = pl.BlockSpec((S, tc), lambda j: (0, j))
    return _call(
        body,
        name="lru_scan_fwd",
        grid=(D_RNN // tc,),
        in_specs=[col, col, col, pl.BlockSpec((S, tc), lambda j: (0, OFF_RY // tc + j)),
                  pl.BlockSpec((1, tc), lambda j: (0, j))],
        out_specs=[col, col],
        out_shape=[jax.ShapeDtypeStruct((S, D_RNN), F32), jax.ShapeDtypeStruct((S, D_RNN), MXU_DTYPE)],
        semantics=("parallel",),
        operands=(r, i, rxc, proj, lam),
        side=side,
    )


def _lru_scan_bwd(dy, proj, h, r, i, rxc, lam, side=None):
    tc = SCAN_TC

    def body(dy_ref, ry_ref, h_ref, r_ref, i_ref, x_ref, lam_ref,
             dry_ref, dzr_ref, dzi_ref, dx_ref, dba_ref, dbi_ref, dlam_ref, a_ref, dh_ref, hp_ref):
        lam_v = lam_ref[...]
        rate = LRU_C * _softplus(-lam_v)
        dlam_scale = LRU_C * _sigmoid(-lam_v)
        row = lax.broadcasted_iota(jnp.int32, (8, tc), 0)
        hp_ref[pl.ds(0, PAD), :] = jnp.zeros((PAD, tc), F32)
        hp_ref[pl.ds(PAD, S), :] = h_ref[...]
        a_ref[pl.ds(S, PAD), :] = jnp.zeros((PAD, tc), F32)

        def prep(ci, carry):
            r0 = pl.multiple_of(ci * CHUNK, CHUNK)
            a_ref[pl.ds(r0, CHUNK), :] = jnp.exp(-rate * r_ref[pl.ds(r0, CHUNK), :])
            ge, dge = _gelu(ry_ref[pl.ds(r0, CHUNK), :])
            dyv = dy_ref[pl.ds(r0, CHUNK), :]
            dh_ref[pl.ds(r0, CHUNK), :] = dyv * ge
            dry_ref[pl.ds(r0, CHUNK), :] = (dyv * h_ref[pl.ds(r0, CHUNK), :] * dge).astype(dry_ref.dtype)
            return carry

        lax.fori_loop(0, S // CHUNK, prep, 0)

        def step(ci, state):
            carry, dba, dbi, dlam = state
            r0 = pl.multiple_of(S - 16 - ci * 16, 16)
            a_ext = a_ref[pl.ds(r0, 24), :]
            a_next = pltpu.roll(a_ext, 23, 0)
            h_prev = pltpu.roll(hp_ref[pl.ds(r0, 24), :], 1, 0)
            dh16 = dh_ref[pl.ds(r0, 16), :]
            gs = [None, None]
            for half in (1, 0):
                lo = 8 * half
                c_cum, g0 = _scan_up(a_next[lo:lo + 8, :], dh16[lo:lo + 8, :], row)
                g = c_cum * carry + g0
                carry = jnp.broadcast_to(g[0:1, :], (8, tc))
                gs[half] = g
            g16 = jnp.concatenate(gs, axis=0)
            a16 = a_ext[0:16, :]
            r16 = r_ref[pl.ds(r0, 16), :]
            i16 = i_ref[pl.ds(r0, 16), :]
            x16 = x_ref[pl.ds(r0, 16), :]
            a2 = a16 * a16
            sq = jnp.sqrt(_one_minus_exp(-2.0 * rate * r16))
            dx_ref[pl.ds(r0, 16), :] = g16 * sq * i16
            dzi = g16 * sq * x16 * i16 * (1.0 - i16)
            dlog_a = g16 * h_prev[8:24, :] * a16 - g16 * i16 * x16 * a2 / sq
            dzr = -rate * dlog_a * r16 * (1.0 - r16)
            dzr_ref[pl.ds(r0, 16), :] = dzr.astype(dzr_ref.dtype)
            dzi_ref[pl.ds(r0, 16), :] = dzi.astype(dzi_ref.dtype)
            return carry, dba + _colsum(dzr), dbi + _colsum(dzi), dlam + _colsum(dlog_a * r16)

        zero = jnp.zeros((1, tc), F32)
        _, dba, dbi, dlam = lax.fori_loop(0, S // 16, step, (jnp.zeros((8, tc), F32), zero, zero, zero))
        dba_ref[...] = dba
        dbi_ref[...] = dbi
        dlam_ref[...] = dlam * dlam_scale

    col = pl.BlockSpec((S, tc), lambda j: (0, j))
    vec = pl.BlockSpec((1, tc), lambda j: (0, j))
    return _call(
        body,
        name="lru_scan_bwd",
        grid=(D_RNN // tc,),
        in_specs=[col, pl.BlockSpec((S, tc), lambda j: (0, OFF_RY // tc + j)), col, col, col, col, vec],
        out_specs=[col, col, col, col, vec, vec, vec],
        out_shape=[jax.ShapeDtypeStruct((S, D_RNN), MXU_DTYPE)] * 3 + [jax.ShapeDtypeStruct((S, D_RNN), F32)]
        + [jax.ShapeDtypeStruct((1, D_RNN), F32)] * 3,
        scratch_shapes=[pltpu.VMEM((S + PAD, tc), F32), pltpu.VMEM((S, tc), F32), pltpu.VMEM((S + PAD, tc), F32)],
        semantics=("parallel",),
        operands=(dy, proj, h, r, i, rxc, lam),
        side=side,
    )


def _lru_gate_wgrad(rxc, dzr, dzi, side=None):
    def body(x_ref, dzr_ref, dzi_ref, dwa_ref, dwi_ref):
        xv = x_ref[...].astype(MXU_DTYPE)
        dims = (((0,), (0,)), ((), ()))
        dwa_ref[...] = lax.dot_general(xv, dzr_ref[...], dims, preferred_element_type=F32)
        dwi_ref[...] = lax.dot_general(xv, dzi_ref[...], dims, preferred_element_type=F32)

    col = pl.BlockSpec((S, RNN_GROUP), lambda g: (0, g))
    w_spec = pl.BlockSpec((None, RNN_GROUP, RNN_GROUP), lambda g: (g, 0, 0))
    return _call(
        body,
        name="lru_gate_wgrad",
        grid=(N_RNN_GROUPS,),
        in_specs=[col, col, col],
        out_specs=[w_spec, w_spec],
        out_shape=[jax.ShapeDtypeStruct((N_RNN_GROUPS, RNN_GROUP, RNN_GROUP), F32)] * 2,
        semantics=("parallel",),
        operands=(rxc, dzr, dzi),
        side=side,
    )


def _lru_gate_xgrad(dzr, dzi, wa, wi, dx_in, side=None):
    tm = 512

    def body(dzr_ref, dzi_ref, wa_ref, wi_ref, dx_ref, o_ref):
        dims = (((1,), (1,)), ((), ()))
        o_ref[...] = (dx_ref[...]
                      + lax.dot_general(dzr_ref[...], wa_ref[...].astype(MXU_DTYPE), dims, preferred_element_type=F32)
                      + lax.dot_general(dzi_ref[...], wi_ref[...].astype(MXU_DTYPE), dims, preferred_element_type=F32))

    x_spec = pl.BlockSpec((tm, RNN_GROUP), lambda g, i: (i, g))
    w_spec = pl.BlockSpec((None, RNN_GROUP, RNN_GROUP), lambda g, i: (g, 0, 0))
    return _call(
        body,
        name="lru_gate_xgrad",
        grid=(N_RNN_GROUPS, S // tm),
        in_specs=[x_spec, x_spec, w_spec, w_spec, x_spec],
        out_specs=x_spec,
        out_shape=jax.ShapeDtypeStruct((S, D_RNN), F32),
        semantics=("parallel", "parallel"),
        operands=(dzr, dzi, wa, wi, dx_in),
        side=side,
    )


def _gate_fwd(y_attn, y_rnn, proj, b_gate, side=None):
    t = 512

    def body(ya_ref, yr_ref, ga_ref, gr_ref, ba_ref, br_ref, o_ref):
        o_ref[...] = (_sigmoid(ga_ref[...] + ba_ref[...]) * ya_ref[...]
                      + _sigmoid(gr_ref[...] + br_ref[...]) * yr_ref[...]).astype(o_ref.dtype)

    tile = pl.BlockSpec((t, t), lambda i, j: (i, j))
    return _call(
        body,
        name="gate_fwd",
        grid=(S // t, D // t),
        in_specs=[tile, tile,
                  pl.BlockSpec((t, t), lambda i, j: (i, OFF_GA // t + j)),
                  pl.BlockSpec((t, t), lambda i, j: (i, OFF_GR // t + j)),
                  pl.BlockSpec((1, t), lambda i, j: (0, j)),
                  pl.BlockSpec((1, t), lambda i, j: (0, D // t + j))],
        out_specs=tile,
        out_shape=jax.ShapeDtypeStruct((S, D), MXU_DTYPE),
        semantics=("parallel", "parallel"),
        operands=(y_attn, y_rnn, proj, proj, b_gate, b_gate),
        side=side,
    )


def _gate_bwd(dmix, y_attn, y_rnn, proj, b_gate, side=None):
    t = 512

    def body(dm_ref, ya_ref, yr_ref, ga_ref, gr_ref, ba_ref, br_ref,
             dya_ref, dyr_ref, dga_ref, dgr_ref, dba_ref, dbr_ref):
        @pl.when(pl.program_id(1) == 0)
        def _():
            dba_ref[...] = jnp.zeros_like(dba_ref)
            dbr_ref[...] = jnp.zeros_like(dbr_ref)

        dm = dm_ref[...]
        ga = _sigmoid(ga_ref[...] + ba_ref[...])
        gr = _sigmoid(gr_ref[...] + br_ref[...])
        dya_ref[...] = (dm * ga).astype(dya_ref.dtype)
        dyr_ref[...] = (dm * gr).astype(dyr_ref.dtype)
        dga = dm * ya_ref[...] * ga * (1.0 - ga)
        dgr = dm * yr_ref[...] * gr * (1.0 - gr)
        dga_ref[...] = dga.astype(dga_ref.dtype)
        dgr_ref[...] = dgr.astype(dgr_ref.dtype)
        dba_ref[...] += _colsum(dga)
        dbr_ref[...] += _colsum(dgr)

    tile = pl.BlockSpec((t, t), lambda j, i: (i, j))
    vec = pl.BlockSpec((1, t), lambda j, i: (0, j))
    return _call(
        body,
        name="gate_bwd",
        grid=(D // t, S // t),
        in_specs=[tile, tile, tile,
                  pl.BlockSpec((t, t), lambda j, i: (i, OFF_GA // t + j)),
                  pl.BlockSpec((t, t), lambda j, i: (i, OFF_GR // t + j)),
                  vec,
                  pl.BlockSpec((1, t), lambda j, i: (0, D // t + j))],
        out_specs=[tile, tile, tile, tile, vec, vec],
        out_shape=[jax.ShapeDtypeStruct((S, D), MXU_DTYPE)] * 4 + [jax.ShapeDtypeStruct((1, D), F32)] * 2,
        semantics=("parallel", "arbitrary"),
        operands=(dmix, y_attn, y_rnn, proj, proj, b_gate, b_gate),
        side=side,
    )


LN_TM = 256


def _ln_stats(pre):
    mu = jnp.mean(pre, axis=-1, keepdims=True)
    xc = pre - mu
    rstd = lax.rsqrt(jnp.mean(xc * xc, axis=-1, keepdims=True) + LN_EPS)
    return xc * rstd, rstd


def _ln_input_grad(dy, xhat, rstd, g):
    dyg = dy * g
    return rstd * (dyg - jnp.mean(dyg, axis=-1, keepdims=True)
                   - xhat * jnp.mean(dyg * xhat, axis=-1, keepdims=True))


def _ln_fwd(res, branch, g, b, side=None):
    def body(res_ref, br_ref, g_ref, b_ref, y_ref, yb_ref, xhat_ref, rstd_ref):
        xhat, rstd = _ln_stats(ALPHA * res_ref[...] + br_ref[...])
        y = xhat * g_ref[...] + b_ref[...]
        y_ref[...] = y
        yb_ref[...] = y.astype(yb_ref.dtype)
        xhat_ref[...] = xhat
        rstd_ref[...] = rstd

    tile = pl.BlockSpec((LN_TM, D), lambda i: (i, 0))
    vec = pl.BlockSpec((1, D), lambda i: (0, 0))
    return _call(
        body,
        name="ln_fwd",
        grid=(S // LN_TM,),
        in_specs=[tile, tile, vec, vec],
        out_specs=[tile, tile, tile, pl.BlockSpec((LN_TM, 1), lambda i: (i, 0))],
        out_shape=[jax.ShapeDtypeStruct((S, D), F32), jax.ShapeDtypeStruct((S, D), MXU_DTYPE),
                   jax.ShapeDtypeStruct((S, D), F32), jax.ShapeDtypeStruct((S, 1), F32)],
        semantics=("parallel",),
        operands=(res, branch, g, b),
        side=side,
    )


def _ln_bwd(dy_a, dy_b, xhat, rstd, g, side=None):
    def body(da_ref, db_in_ref, xhat_ref, rstd_ref, g_ref, dp_ref, dpb_ref, dg_ref, db_ref):
        @pl.when(pl.program_id(0) == 0)
        def _():
            dg_ref[...] = jnp.zeros_like(dg_ref)
            db_ref[...] = jnp.zeros_like(db_ref)

        dy = da_ref[...] + ALPHA * db_in_ref[...]
        xhat = xhat_ref[...]
        dp = _ln_input_grad(dy, xhat, rstd_ref[...], g_ref[...])
        dp_ref[...] = dp
        dpb_ref[...] = dp.astype(dpb_ref.dtype)
        dg_ref[...] += _colsum(dy * xhat)
        db_ref[...] += _colsum(dy)

    tile = pl.BlockSpec((LN_TM, D), lambda i: (i, 0))
    vec = pl.BlockSpec((1, D), lambda i: (0, 0))
    return _call(
        body,
        name="ln_bwd",
        grid=(S // LN_TM,),
        in_specs=[tile, tile, tile, pl.BlockSpec((LN_TM, 1), lambda i: (i, 0)), vec],
        out_specs=[tile, tile, vec, vec],
        out_shape=[jax.ShapeDtypeStruct((S, D), F32), jax.ShapeDtypeStruct((S, D), MXU_DTYPE),
                   jax.ShapeDtypeStruct((1, D), F32), jax.ShapeDtypeStruct((1, D), F32)],
        semantics=("arbitrary",),
        operands=(dy_a, dy_b, xhat, rstd, g),
        side=side,
    )


def _ln_loss_bwd(res, branch, g, b, target, side=None):
    def body(res_ref, br_ref, g_ref, b_ref, t_ref, loss_ref, dp_ref, dpb_ref, dg_ref, db_ref):
        @pl.when(pl.program_id(0) == 0)
        def _():
            loss_ref[...] = jnp.zeros_like(loss_ref)
            dg_ref[...] = jnp.zeros_like(dg_ref)
            db_ref[...] = jnp.zeros_like(db_ref)

        xhat, rstd = _ln_stats(ALPHA * res_ref[...] + br_ref[...])
        gv = g_ref[...]
        err = xhat * gv + b_ref[...] - t_ref[...]
        loss_ref[...] += (0.5 / D) * jnp.sum(_colsum(err * err), axis=1, keepdims=True)
        dy = err * (1.0 / D)
        dp = _ln_input_grad(dy, xhat, rstd, gv)
        dp_ref[...] = dp
        dpb_ref[...] = dp.astype(dpb_ref.dtype)
        dg_ref[...] += _colsum(dy * xhat)
        db_ref[...] += _colsum(dy)

    tile = pl.BlockSpec((LN_TM, D), lambda i: (i, 0))
    vec = pl.BlockSpec((1, D), lambda i: (0, 0))
    return _call(
        body,
        name="ln_loss_bwd",
        grid=(S // LN_TM,),
        in_specs=[tile, tile, vec, vec, tile],
        out_specs=[pl.BlockSpec((1, 1), lambda i: (0, 0)), tile, tile, vec, vec],
        out_shape=[jax.ShapeDtypeStruct((1, 1), F32), jax.ShapeDtypeStruct((S, D), F32),
                   jax.ShapeDtypeStruct((S, D), MXU_DTYPE),
                   jax.ShapeDtypeStruct((1, D), F32), jax.ShapeDtypeStruct((1, D), F32)],
        semantics=("arbitrary",),
        operands=(res, branch, g, b, target),
        side=side,
    )


FFN_TC = 256


def _ffn_act_fwd(up, gpre, w, b, side=None):
    tc = FFN_TC

    def body(up_ref, x_ref, w_ref, b_ref, o_ref, xpad_ref):
        xpad_ref[pl.ds(0, PAD), :] = jnp.zeros((PAD, tc), F32)
        xpad_ref[pl.ds(PAD, S), :] = x_ref[...]
        wv = w_ref[...]
        bv = b_ref[...]

        def step(ci, carry):
            r0 = pl.multiple_of(ci * CHUNK, CHUNK)
            taps = _past_taps(xpad_ref, r0, FFN_CONV_W)
            gate = bv + taps[0] * wv[0:1, :] + taps[1] * wv[1:2, :] + taps[2] * wv[2:3, :]
            o_ref[pl.ds(r0, CHUNK), :] = (_gelu(gate)[0] * up_ref[pl.ds(r0, CHUNK), :]).astype(o_ref.dtype)
            return carry

        lax.fori_loop(0, S // CHUNK, step, 0)

    col = pl.BlockSpec((S, tc), lambda j: (0, j))
    return _call(
        body,
        name="ffn_act_fwd",
        grid=(D_FF // tc,),
        in_specs=[col, col, pl.BlockSpec((FFN_CONV_W, tc), lambda j: (0, j)), pl.BlockSpec((1, tc), lambda j: (0, j))],
        out_specs=col,
        out_shape=jax.ShapeDtypeStruct((S, D_FF), MXU_DTYPE),
        scratch_shapes=[pltpu.VMEM((S + PAD, tc), F32)],
        semantics=("parallel",),
        operands=(up, gpre, w, b),
        side=side,
    )


def _ffn_act_bwd(dfin, up, gpre, w, b, side=None):
    tc = FFN_TC
    width = FFN_CONV_W

    def body(df_ref, up_ref, x_ref, w_ref, b_ref, dup_ref, dx_ref, dw_ref, db_ref, xpad_ref, dpad_ref):
        xpad_ref[pl.ds(0, PAD), :] = jnp.zeros((PAD, tc), F32)
        xpad_ref[pl.ds(PAD, S), :] = x_ref[...]
        dpad_ref[pl.ds(S, PAD), :] = jnp.zeros((PAD, tc), F32)
        wv = w_ref[...]
        bv = b_ref[...]

        def gate_grad(ci, acc):
            r0 = pl.multiple_of(ci * CHUNK, CHUNK)
            taps = _past_taps(xpad_ref, r0, width)
            gate = bv + taps[0] * wv[0:1, :] + taps[1] * wv[1:2, :] + taps[2] * wv[2:3, :]
            ge, dge = _gelu(gate)
            df = df_ref[pl.ds(r0, CHUNK), :]
            dup_ref[pl.ds(r0, CHUNK), :] = (df * ge).astype(dup_ref.dtype)
            d = df * up_ref[pl.ds(r0, CHUNK), :] * dge
            dpad_ref[pl.ds(r0, CHUNK), :] = d
            return tuple(acc[k] + _colsum(taps[k] * d) for k in range(width)) + (acc[width] + _colsum(d),)

        zero = jnp.zeros((1, tc), F32)
        acc = lax.fori_loop(0, S // CHUNK, gate_grad, (zero,) * (width + 1))
        for k in range(width):
            dw_ref[k:k + 1, :] = acc[k]
        db_ref[...] = acc[width]

        def input_grad(ci, carry):
            r0 = pl.multiple_of(ci * CHUNK, CHUNK)
            ahead = _future_taps(dpad_ref, r0, width)
            dx = ahead[0] * wv[2:3, :] + ahead[1] * wv[1:2, :] + ahead[2] * wv[0:1, :]
            dx_ref[pl.ds(r0, CHUNK), :] = dx.astype(dx_ref.dtype)
            return carry

        lax.fori_loop(0, S // CHUNK, input_grad, 0)

    col = pl.BlockSpec((S, tc), lambda j: (0, j))
    w_spec = pl.BlockSpec((width, tc), lambda j: (0, j))
    vec = pl.BlockSpec((1, tc), lambda j: (0, j))
    return _call(
        body,
        name="ffn_act_bwd",
        grid=(D_FF // tc,),
        in_specs=[col, col, col, w_spec, vec],
        out_specs=[col, col, w_spec, vec],
        out_shape=[jax.ShapeDtypeStruct((S, D_FF), MXU_DTYPE)] * 2
        + [jax.ShapeDtypeStruct((width, D_FF), F32), jax.ShapeDtypeStruct((1, D_FF), F32)],
        scratch_shapes=[pltpu.VMEM((S + PAD, tc), F32), pltpu.VMEM((S + PAD, tc), F32)],
        semantics=("parallel",),
        operands=(dfin, up, gpre, w, b),
        side=side,
    )


def _adamw_update(w, g, m, v):
    m = ADAM_B1 * m + (1.0 - ADAM_B1) * g
    v = ADAM_B2 * v + (1.0 - ADAM_B2) * (g * g)
    m_hat = m / (1.0 - ADAM_B1 ** ADAM_STEP)
    v_hat = v / (1.0 - ADAM_B2 ** ADAM_STEP)
    delta = -ADAM_LR * (m_hat / (jnp.sqrt(v_hat) + ADAM_EPS) + ADAM_WD * w)
    return delta, m, v


def _add_pairs(send, pair, far_index, *, name):
    _, r_dim, c_dim = send.shape
    tr = r_dim // 4

    def body(far_ref, mine_ref, theirs_ref, o_ref):
        o_ref[...] = (mine_ref[...].astype(F32) + theirs_ref[...].astype(F32)).astype(o_ref.dtype)

    return pl.pallas_call(
        body,
        name=name,
        grid_spec=pltpu.PrefetchScalarGridSpec(
            num_scalar_prefetch=1,
            grid=(3, r_dim // tr),
            in_specs=[pl.BlockSpec((None, tr, c_dim), lambda j, i, far: (far[j], i, 0)),
                      pl.BlockSpec((None, tr, c_dim), lambda j, i, far: (1 + j, i, 0))],
            out_specs=pl.BlockSpec((None, tr, c_dim), lambda j, i, far: (j, i, 0)),
        ),
        out_shape=jax.ShapeDtypeStruct((3, r_dim, c_dim), BF16),
        compiler_params=_cparams("parallel", "parallel"),
    )(far_index, send, pair)


def _reduce_adamw(w, m, v, g_own, pair, far, me, *, tr, name):
    r_dim, c_dim = w.shape

    def body(me_ref, w_ref, m_ref, v_ref, g_ref, pair_ref, far_ref, grad_ref, delta_ref, nm_ref, nv_ref):
        g = g_ref[...] + pair_ref[...].astype(F32)
        for j in range(3):
            g = g + far_ref[j].astype(F32)
        delta, nm, nv = _adamw_update(w_ref[...], g, m_ref[...], v_ref[...])
        grad_ref[...] = g
        delta_ref[...] = delta
        nm_ref[...] = nm
        nv_ref[...] = nv

    tile = pl.BlockSpec((tr, c_dim), lambda i, me: (i, 0))
    if g_own.ndim == 3:
        own_spec = pl.BlockSpec((None, tr, c_dim), lambda i, me: (me[0], i, 0))
    else:
        own_spec = tile
    return pl.pallas_call(
        body,
        name=name,
        grid_spec=pltpu.PrefetchScalarGridSpec(
            num_scalar_prefetch=1,
            grid=(r_dim // tr,),
            in_specs=[tile, tile, tile, own_spec, pl.BlockSpec((None, tr, c_dim), lambda i, me: (0, i, 0)),
                      pl.BlockSpec((3, tr, c_dim), lambda i, me: (0, i, 0))],
            out_specs=[tile] * 4,
        ),
        out_shape=[jax.ShapeDtypeStruct((r_dim, c_dim), F32)] * 4,
        compiler_params=_cparams("parallel"),
    )(me, w, m, v, g_own, pair, far)


def _adamw_many(ws, ms, vs, gs):
    n = len(ws)

    def body(*refs):
        for i in range(n):
            delta, nm, nv = _adamw_update(refs[i][...], refs[3 * n + i][...], refs[n + i][...], refs[2 * n + i][...])
            refs[4 * n + i][...] = delta
            refs[5 * n + i][...] = nm
            refs[6 * n + i][...] = nv

    vmem = pl.BlockSpec(memory_space=pltpu.VMEM)
    res = pl.pallas_call(
        body,
        name="adamw_small",
        in_specs=[vmem] * (4 * n),
        out_specs=[vmem] * (3 * n),
        out_shape=[jax.ShapeDtypeStruct(w.shape, F32) for w in ws] * 3,
        compiler_params=pltpu.CompilerParams(vmem_limit_bytes=VMEM_LIMIT),
    )(*ws, *ms, *vs, *gs)
    return res[:n], res[n:2 * n], res[2 * n:]


def _adamw_blocks(w, m, v, g, *, name, side=None):
    per = 2

    def body(w_ref, m_ref, v_ref, g_ref, delta_ref, nm_ref, nv_ref):
        delta, nm, nv = _adamw_update(w_ref[...], g_ref[...], m_ref[...], v_ref[...])
        delta_ref[...] = delta
        nm_ref[...] = nm
        nv_ref[...] = nv

    tile = pl.BlockSpec((1, per) + w.shape[2:], lambda i: (0, i, 0, 0))
    return _call(
        body,
        name=name,
        grid=(w.shape[1] // per,),
        in_specs=[tile] * 4,
        out_specs=[tile] * 3,
        out_shape=[jax.ShapeDtypeStruct(w.shape, F32)] * 3,
        semantics=("parallel",),
        operands=(w, m, v, g),
        side=side,
    )


def _coords():
    return lax.axis_index("x"), lax.axis_index("y"), lax.axis_index("c")


def _flip(coord, bit):
    return 1 - coord if bit else coord


def _relative(k):
    x, y, c = _coords()
    return _flip(x, k & 4), _flip(y, k & 2), _flip(c, k & 1)


def _index(pos):
    return 4 * pos[0] + 2 * pos[1] + pos[2]


FAR = (4, 2, 6)
AG_US_PER_MB = 34.0
RS_US_PER_MB = 36.0
MIN_RIDE_US = 30.0
MIN_GATHER_RIDE_US = 26.0
ROW_ALIGN = 32


def _chunks(items, cursor, us, us_per_mb, through=None):
    budget = float("inf") if us is None else us / us_per_mb * 2 ** 20
    names = list(items)
    if through is not None:
        names = names[:names.index(through) + 1]
    chunks = []
    for name in names:
        arr = items[name]
        r_dim, c_dim = arr.shape[-2:]
        row_bytes = c_dim * arr.dtype.itemsize
        while cursor[name] < r_dim and budget > 0:
            rows = r_dim - cursor[name]
            if r_dim > ROW_ALIGN and budget < rows * row_bytes:
                rows = min(rows, max(ROW_ALIGN, int(budget // row_bytes) // ROW_ALIGN * ROW_ALIGN))
            chunks.append((name, cursor[name], rows))
            cursor[name] += rows
            budget -= rows * row_bytes
    return chunks


class _Gather:
    def __init__(self, shards):
        self.shards = dict(shards)
        self.bufs = {n: None for n in self.shards}
        self.cursor = {n: 0 for n in self.shards}

    def take(self, us=None, through=None):
        if us is not None and us < MIN_GATHER_RIDE_US:
            return None
        chunks = _chunks(self.shards, self.cursor, us, AG_US_PER_MB, through)
        return _GatherSide(self, chunks) if chunks else None

    def get(self, name):
        chunks = _chunks(self.shards, self.cursor, None, AG_US_PER_MB, through=name)
        if chunks:
            _run_side(_GatherSide(self, chunks), "gather_" + name)
        return self.bufs[name]


class _GatherSide:
    SEMS = 8

    def __init__(self, owner, chunks):
        self.owner, self.chunks = owner, chunks
        self.names = list(dict.fromkeys(n for n, _, _ in chunks))
        old = [n for n in self.names if owner.bufs[n] is not None]
        self.operands = [owner.shards[n] for n in self.names] + [owner.bufs[n] for n in old]
        self.out_shape = [jax.ShapeDtypeStruct((N_DEV,) + owner.shards[n].shape, owner.shards[n].dtype)
                          for n in self.names]
        self.aliases = {len(self.names) + i: self.names.index(n) for i, n in enumerate(old)}
        self.sems = [pltpu.SemaphoreType.DMA((self.SEMS * len(chunks),)),
                     pltpu.SemaphoreType.DMA((self.SEMS * len(chunks),)), pltpu.SemaphoreType.DMA((len(chunks),))]

    def _halves(self, ci):
        _, r0, rows = self.chunks[ci]
        if rows % ROW_ALIGN:
            return None
        return (r0, rows // 2), (r0 + rows // 2, rows // 2)

    def _copy(self, ins, outs, sems, ci, s, block, to, rows=None, from_shard=False):
        name, r0, n = self.chunks[ci]
        if rows is not None:
            r0, n = rows
        w = self.names.index(name)
        slot = outs[w].at[_index(block), pl.ds(r0, n)]
        return pltpu.make_async_remote_copy(
            src_ref=ins[w].at[pl.ds(r0, n)] if from_shard else slot, dst_ref=slot,
            send_sem=sems[0].at[self.SEMS * ci + s], recv_sem=sems[1].at[self.SEMS * ci + s],
            device_id=to, device_id_type=MESH)

    def _own(self, ins, outs, sems, ci):
        name, r0, rows = self.chunks[ci]
        w = self.names.index(name)
        return pltpu.make_async_copy(ins[w].at[pl.ds(r0, rows)], outs[w].at[_index(_relative(0)), pl.ds(r0, rows)],
                                     sems[2].at[ci])

    def _pass(self, ins, outs, sems, ci, which):
        source, target = ((4, 2), (2, 4))[which]
        return self._copy(ins, outs, sems, ci, 3 + which, _relative(source), _relative(target),
                          rows=self._halves(ci)[which])

    def start(self, ins, outs, sems):
        me = _relative(0)
        for ci in range(len(self.chunks)):
            self._own(ins, outs, sems, ci).start()
        for ci in range(len(self.chunks)):
            self._copy(ins, outs, sems, ci, 1, me, _relative(4), from_shard=True).start()
            self._copy(ins, outs, sems, ci, 2, me, _relative(2), from_shard=True).start()
            if self._halves(ci) is None:
                self._copy(ins, outs, sems, ci, 3, me, _relative(6), from_shard=True).start()
        for ci in range(len(self.chunks)):
            self._copy(ins, outs, sems, ci, 0, me, _relative(1), from_shard=True).start()

    def mid(self, ins, outs, sems):
        me = _relative(0)
        cut = [ci for ci in range(len(self.chunks)) if self._halves(ci) is not None]
        for ci in cut:
            self._copy(ins, outs, sems, ci, 1, _relative(4), me).wait_recv()
            self._pass(ins, outs, sems, ci, 0).start()
        for ci in cut:
            self._copy(ins, outs, sems, ci, 2, _relative(2), me).wait_recv()
            self._pass(ins, outs, sems, ci, 1).start()

    def finish(self, ins, outs, sems):
        me, sibling = _relative(0), _relative(1)
        n = len(self.chunks)
        for ci in range(n):
            if self._halves(ci) is None:
                for s, k in ((1, 4), (2, 2), (3, 6)):
                    self._copy(ins, outs, sems, ci, s, _relative(k), me).wait_recv()
            else:
                h0, h1 = self._halves(ci)
                self._copy(ins, outs, sems, ci, 3, _relative(6), me, rows=h0).wait_recv()
                self._copy(ins, outs, sems, ci, 4, _relative(6), me, rows=h1).wait_recv()
            for j, k in enumerate(FAR):
                self._copy(ins, outs, sems, ci, 5 + j, _relative(k), sibling).start()
        for ci in range(n):
            self._copy(ins, outs, sems, ci, 0, sibling, me).wait_recv()
            for j, k in enumerate(FAR):
                self._copy(ins, outs, sems, ci, 5 + j, _relative(k | 1), me).wait_recv()
        for ci in range(n):
            self._copy(ins, outs, sems, ci, 0, me, sibling, from_shard=True).wait_send()
            self._copy(ins, outs, sems, ci, 1, me, _relative(4), from_shard=True).wait_send()
            self._copy(ins, outs, sems, ci, 2, me, _relative(2), from_shard=True).wait_send()
            if self._halves(ci) is None:
                self._copy(ins, outs, sems, ci, 3, me, _relative(6), from_shard=True).wait_send()
            else:
                self._pass(ins, outs, sems, ci, 0).wait_send()
                self._pass(ins, outs, sems, ci, 1).wait_send()
            for j, k in enumerate(FAR):
                self._copy(ins, outs, sems, ci, 5 + j, _relative(k), sibling).wait_send()
            self._own(ins, outs, sems, ci).wait()

    def done(self, results):
        for n, buf in zip(self.names, results):
            self.owner.bufs[n] = buf


class _Scatter:
    def __init__(self, me, far_index):
        self.me, self.far_index = me, far_index
        self.sends, self.owns, self.pairs, self.sums, self.fars = {}, {}, {}, {}, {}
        self.pair_cursor, self.far_cursor = {}, {}

    def add(self, name, send, own):
        self.sends[name] = send
        self.owns[name] = own
        self.pairs[name] = self.fars[name] = None
        self.pair_cursor[name] = 0

    def _rows(self, name):
        return self.sends[name].shape[1]

    def _add_ready_pairs(self):
        for name in self.sends:
            if name not in self.sums and self.pair_cursor[name] == self._rows(name):
                self.sums[name] = _add_pairs(self.sends[name], self.pairs[name], self.far_index, name="pair_" + name)
                self.far_cursor[name] = 0

    def _side(self, us, through=None):
        self._add_ready_pairs()
        names = list(self.sends)
        if through is not None:
            names = names[:names.index(through) + 1]
        pair_chunks = [(n, self.pair_cursor[n], self._rows(n) - self.pair_cursor[n]) for n in names
                       if self.pair_cursor[n] < self._rows(n)]
        for n, _, _ in pair_chunks:
            self.pair_cursor[n] = self._rows(n)
        far_chunks = _chunks(self.sums, self.far_cursor, us, RS_US_PER_MB,
                             through if through in self.sums else None) if self.sums else []
        return _ScatterSide(self, pair_chunks, far_chunks) if pair_chunks or far_chunks else None

    def add_blocks(self, name, blocks32, blocks16):
        self.add(name, blocks16, blocks32)

    def add_cols(self, name, full32, full16):
        width = full32.shape[1] // N_DEV
        self.add(name, _blocks(full16, "cols"), lax.dynamic_slice_in_dim(full32, self.me * width, width, axis=1))

    def take(self, us):
        return self._side(us) if us >= MIN_RIDE_US else None

    def flush_pairs(self, name):
        side = self._side(0.0)
        if side is not None:
            _run_side(side, name)
        self._add_ready_pairs()

    def get(self, name):
        step = 0
        while name not in self.sums or self.far_cursor[name] < self._rows(name):
            _run_side(self._side(None, through=name), "scatter_%s_%d" % (name, step))
            step += 1
        return self.owns[name], self.pairs[name], self.fars[name]


class _ScatterSide:
    TO_SIBLING = (1, 5, 3, 7)

    def __init__(self, owner, pair_chunks, far_chunks):
        self.owner, self.pair_chunks, self.far_chunks = owner, pair_chunks, far_chunks
        self.pair_names = list(dict.fromkeys(n for n, _, _ in pair_chunks))
        self.far_names = list(dict.fromkeys(n for n, _, _ in far_chunks))
        ins = [(owner.sends[n], owner.pairs[n], (4,)) for n in self.pair_names]
        ins += [(owner.sums[n], owner.fars[n], (3,)) for n in self.far_names]
        old = [i for i, (_, buf, _) in enumerate(ins) if buf is not None]
        self.operands = [src for src, _, _ in ins] + [ins[i][1] for i in old]
        self.out_shape = [jax.ShapeDtypeStruct(slots + src.shape[1:], BF16) for src, _, slots in ins]
        self.aliases = {len(ins) + j: i for j, i in enumerate(old)}
        n_pair, n_far = 4 * len(pair_chunks), 3 * len(far_chunks)
        self.sems = [pltpu.SemaphoreType.DMA((max(n_pair, 1),)), pltpu.SemaphoreType.DMA((max(n_pair, 1),)),
                     pltpu.SemaphoreType.DMA((max(n_far, 1),)), pltpu.SemaphoreType.DMA((max(n_far, 1),))]

    def _copies(self, ins, outs, sems):
        copies = []
        for ci, (name, r0, rows) in enumerate(self.pair_chunks):
            w = self.pair_names.index(name)
            for j, k in enumerate(self.TO_SIBLING):
                copies.append(pltpu.make_async_remote_copy(
                    src_ref=ins[w].at[_index(_relative(k)), pl.ds(r0, rows)], dst_ref=outs[w].at[j, pl.ds(r0, rows)],
                    send_sem=sems[0].at[4 * ci + j], recv_sem=sems[1].at[4 * ci + j],
                    device_id=_relative(1), device_id_type=MESH))
        for ci, (name, r0, rows) in enumerate(self.far_chunks):
            w = len(self.pair_names) + self.far_names.index(name)
            for j, k in enumerate(FAR):
                copies.append(pltpu.make_async_remote_copy(
                    src_ref=ins[w].at[j, pl.ds(r0, rows)], dst_ref=outs[w].at[j, pl.ds(r0, rows)],
                    send_sem=sems[2].at[3 * ci + j], recv_sem=sems[3].at[3 * ci + j],
                    device_id=_relative(k), device_id_type=MESH))
        return copies

    def start(self, ins, outs, sems):
        for cp in self._copies(ins, outs, sems):
            cp.start()

    def mid(self, ins, outs, sems):
        pass

    def finish(self, ins, outs, sems):
        for cp in self._copies(ins, outs, sems):
            cp.wait()

    def done(self, results):
        for n, buf in zip(self.pair_names, results):
            self.owner.pairs[n] = buf
        for n, buf in zip(self.far_names, results[len(self.pair_names):]):
            self.owner.fars[n] = buf


class _Joined:
    def __init__(self, sides):
        self.sides = sides
        self.operands, self.out_shape, self.sems, self.aliases, self.spans = [], [], [], {}, []
        for s in sides:
            i0, o0, s0 = len(self.operands), len(self.out_shape), len(self.sems)
            self.operands += list(s.operands)
            self.out_shape += list(s.out_shape)
            self.sems += list(s.sems)
            self.aliases.update({i0 + i: o0 + o for i, o in s.aliases.items()})
            self.spans.append((slice(i0, len(self.operands)), slice(o0, len(self.out_shape)),
                               slice(s0, len(self.sems))))

    def start(self, ins, outs, sems):
        for s, (i, o, m) in zip(self.sides, self.spans):
            s.start(ins[i], outs[o], sems[m])

    def mid(self, ins, outs, sems):
        for s, (i, o, m) in zip(self.sides, self.spans):
            s.mid(ins[i], outs[o], sems[m])

    def finish(self, ins, outs, sems):
        for s, (i, o, m) in zip(self.sides, self.spans):
            s.finish(ins[i], outs[o], sems[m])

    def done(self, results):
        for s, (_, o, _) in zip(self.sides, self.spans):
            s.done(results[o])


def _join(*sides):
    sides = [s for s in sides if s is not None]
    if len(sides) <= 1:
        return sides[0] if sides else None
    return _Joined(sides)


def _pack_rows(vecs):
    rows = -(-sum(v.shape[0] for v in vecs) // 8) * 8
    width = max(v.shape[1] for v in vecs)

    def body(*refs):
        out = refs[-1]
        out[...] = jnp.zeros_like(out)
        r0 = 0
        for v in refs[:-1]:
            out[r0:r0 + v.shape[0], 0:v.shape[1]] = v[...]
            r0 += v.shape[0]

    vmem = pl.BlockSpec(memory_space=pltpu.VMEM)
    return pl.pallas_call(body, name="pack_small", in_specs=[vmem] * len(vecs), out_specs=vmem,
                          out_shape=jax.ShapeDtypeStruct((rows, width), F32))(*vecs)


def _sum_rows(inbox, shapes):
    def body(inbox_ref, *refs):
        outs, total = refs[:-1], refs[-1]
        acc = inbox_ref[0]
        for d in range(1, N_DEV):
            acc = acc + inbox_ref[d]
        total[...] = acc
        r0 = 0
        for o in outs:
            o[...] = total[r0:r0 + o.shape[0], 0:o.shape[1]]
            r0 += o.shape[0]

    vmem = pl.BlockSpec(memory_space=pltpu.VMEM)
    return pl.pallas_call(body, name="sum_small", in_specs=[vmem], out_specs=[vmem] * len(shapes),
                          out_shape=[jax.ShapeDtypeStruct(s, F32) for s in shapes],
                          scratch_shapes=[pltpu.VMEM(inbox.shape[1:], F32)])(inbox)


class _ShareRows:
    def __init__(self, mine):
        self.operands, self.aliases = [mine], {}
        self.out_shape = [jax.ShapeDtypeStruct((N_DEV,) + mine.shape, F32)]
        self.sems = [pltpu.SemaphoreType.DMA((N_DEV - 1,)), pltpu.SemaphoreType.DMA((N_DEV - 1,)),
                     pltpu.SemaphoreType.DMA(())]

    def _copy(self, ins, outs, sems, k, sender):
        return pltpu.make_async_remote_copy(
            src_ref=ins[0], dst_ref=outs[0].at[_index(sender)], send_sem=sems[0].at[k - 1], recv_sem=sems[1].at[k - 1],
            device_id=_relative(k), device_id_type=MESH)

    def _own(self, ins, outs, sems):
        return pltpu.make_async_copy(ins[0], outs[0].at[_index(_relative(0))], sems[2])

    def start(self, ins, outs, sems):
        self._own(ins, outs, sems).start()
        for k in range(1, N_DEV):
            self._copy(ins, outs, sems, k, _relative(0)).start()

    def mid(self, ins, outs, sems):
        pass

    def finish(self, ins, outs, sems):
        for k in range(1, N_DEV):
            self._copy(ins, outs, sems, k, _relative(k)).wait_recv()
            self._copy(ins, outs, sems, k, _relative(0)).wait_send()
        self._own(ins, outs, sems).wait()

    def done(self, results):
        self.inbox = results[0]


class _PartsToOwners:
    def __init__(self, mats):
        self.n, self.per = len(mats), mats[0].shape[0] // N_DEV
        self.operands, self.aliases = list(mats), {}
        self.out_shape = [jax.ShapeDtypeStruct((N_DEV, self.n, self.per) + mats[0].shape[1:], F32)]
        self.sems = [pltpu.SemaphoreType.DMA((self.n * (N_DEV - 1),))] * 2

    def _copies(self, ins, outs, sems):
        return [pltpu.make_async_remote_copy(
            src_ref=ins[j].at[pl.ds(self.per * _index(_relative(k)), self.per)], dst_ref=outs[0].at[k, j],
            send_sem=sems[0].at[self.n * (k - 1) + j], recv_sem=sems[1].at[self.n * (k - 1) + j],
            device_id=_relative(k), device_id_type=MESH) for k in range(1, N_DEV) for j in range(self.n)]

    def start(self, ins, outs, sems):
        for cp in self._copies(ins, outs, sems):
            cp.start()

    def mid(self, ins, outs, sems):
        pass

    def finish(self, ins, outs, sems):
        for cp in self._copies(ins, outs, sems):
            cp.wait()

    def done(self, results):
        self.stage = results[0]


def _sum_parts(mats, stage):
    n, per = len(mats), mats[0].shape[0] // N_DEV

    def body(*refs):
        stage_ref, out = refs[n], refs[n + 1]
        me = _index(_relative(0))
        for j in range(n):
            acc = refs[j][pl.ds(per * me, per)]
            for k in range(1, N_DEV):
                acc = acc + stage_ref[k, j]
            out[j] = acc

    vmem = pl.BlockSpec(memory_space=pltpu.VMEM)
    return pl.pallas_call(body, name="sum_small_parts", in_specs=[vmem] * (n + 1), out_specs=vmem,
                          out_shape=jax.ShapeDtypeStruct((n, per) + mats[0].shape[1:], F32),
                          compiler_params=pltpu.CompilerParams(vmem_limit_bytes=VMEM_LIMIT))(*mats, stage)


class _PartsToAll:
    def __init__(self, parts, rows):
        self.n, self.per = parts.shape[0], parts.shape[1]
        self.operands, self.aliases = [parts], {}
        self.out_shape = [jax.ShapeDtypeStruct((rows,) + parts.shape[2:], F32)] * self.n
        self.sems = [pltpu.SemaphoreType.DMA((self.n * (N_DEV - 1),))] * 2 + [pltpu.SemaphoreType.DMA((self.n,))]

    def _rows(self, ref, pos):
        return ref.at[pl.ds(self.per * _index(pos), self.per)]

    def _copy(self, ins, outs, sems, k, j, owner):
        return pltpu.make_async_remote_copy(
            src_ref=ins[0].at[j], dst_ref=self._rows(outs[j], owner),
            send_sem=sems[0].at[self.n * (k - 1) + j], recv_sem=sems[1].at[self.n * (k - 1) + j],
            device_id=_relative(k), device_id_type=MESH)

    def _own(self, ins, outs, sems, j):
        return pltpu.make_async_copy(ins[0].at[j], self._rows(outs[j], _relative(0)), sems[2].at[j])

    def start(self, ins, outs, sems):
        for j in range(self.n):
            self._own(ins, outs, sems, j).start()
            for k in range(1, N_DEV):
                self._copy(ins, outs, sems, k, j, _relative(0)).start()

    def mid(self, ins, outs, sems):
        pass

    def finish(self, ins, outs, sems):
        for j in range(self.n):
            for k in range(1, N_DEV):
                self._copy(ins, outs, sems, k, j, _relative(k)).wait_recv()
                self._copy(ins, outs, sems, k, j, _relative(0)).wait_send()
            self._own(ins, outs, sems, j).wait()

    def done(self, results):
        self.totals = list(results)


class _SmallSync:
    def __init__(self, vec_names, mat_names):
        self.vec_names, self.mat_names = vec_names, mat_names

    def begin(self, loss, grads):
        vecs = [loss] + [grads[n] for n in self.vec_names]
        self.shapes = [v.shape for v in vecs]
        self.mats = [_diag_blocks(grads[n]) for n in self.mat_names]
        self.share = _ShareRows(_pack_rows(vecs))
        self.to_owners = _PartsToOwners(self.mats)
        return _join(self.share, self.to_owners)

    def middle(self):
        self.sums = _sum_rows(self.share.inbox, self.shapes)
        self.to_all = _PartsToAll(_sum_parts(self.mats, self.to_owners.stage), self.mats[0].shape[0])
        return self.to_all

    def end(self):
        return self.sums[0], dict(zip(self.vec_names, self.sums[1:])), dict(zip(self.mat_names, self.to_all.totals))


def _block_diag(w):
    groups = []
    for g in range(N_RNN_GROUPS):
        placed = [jnp.pad(w[4 * g + b], ((RNN_BLOCK_W * b, RNN_BLOCK_W * (3 - b)),) * 2) for b in range(4)]
        groups.append(placed[0] + placed[1] + placed[2] + placed[3])
    return jnp.stack(groups)


def _diag_blocks(wg):
    blocks = []
    for n in range(4 * N_RNN_GROUPS):
        g, at = n // 4, RNN_BLOCK_W * (n % 4)
        blocks.append(wg[g, at:at + RNN_BLOCK_W, at:at + RNN_BLOCK_W])
    return jnp.stack(blocks)


def _heads_major(t, n_heads):
    return t.reshape(S, n_heads, HEAD_DIM).transpose(1, 0, 2)


def _heads_minor(t):
    return t.transpose(1, 0, 2).reshape(S, t.shape[0] * HEAD_DIM)


def _natural(gathered, how):
    n, r, c = gathered.shape
    if how == "rows":
        return gathered.reshape(n * r, c)
    return gathered.transpose(1, 0, 2).reshape(r, n * c)


def _blocks(full, how):
    if how == "rows":
        return full.reshape(N_DEV, full.shape[0] // N_DEV, full.shape[1])
    return full.reshape(full.shape[0], N_DEV, full.shape[1] // N_DEV).transpose(1, 0, 2)


def _forward_backward(x2, target, small, gather, scatter, sync):
    xb = x2.astype(MXU_DTYPE)
    w_in = _natural(gather.get("w_in"), "cols")
    proj, projb = _mm(xb, w_in, tm=1024, tn=512, tk=D, out_dtype=(F32, MXU_DTYPE), name="proj", side=gather.take(110))

    qt = projb[:, :OFF_K].T.reshape(N_KV, GROUP, HEAD_DIM, S)
    k2, v2 = projb[:, OFF_K:OFF_V], projb[:, OFF_V:OFF_RX]
    kp = jnp.pad(_heads_major(k2, N_KV), ((0, 0), (BLOCK, 0), (0, 0)))
    vp = jnp.pad(_heads_major(v2, N_KV), ((0, 0), (BLOCK, 0), (0, 0)))
    kt = jnp.pad(k2.T.reshape(N_KV, HEAD_DIM, S), ((0, 0), (0, 0), (BLOCK, 0)))
    vt = jnp.pad(v2.T.reshape(N_KV, HEAD_DIM, S), ((0, 0), (0, 0), (BLOCK, 0)))
    sink_row = jnp.repeat(small["attn_sinks"].reshape(N_KV, 1, GROUP), BLOCK, axis=2)
    ot = _attn_fwd(qt, kp, vt, sink_row, side=gather.take(36)).reshape(D, S)

    rconv_w = _natural(gather.get("rnn_conv_w"), "cols")
    rxc = _conv_fwd(proj, OFF_RX, rconv_w, small["rnn_conv_b"], tc=512, name="rnn_conv_fwd", side=gather.take(18))
    r, i = _lru_gates(rxc, small["lru_wa"], small["lru_wi"], small["lru_ba"], small["lru_bi"], side=gather.take(33))
    h, yrin = _lru_scan_fwd(r, i, rxc, proj, small["lru_lambda"], side=gather.take(53))

    w_ap = _natural(gather.get("w_attn_proj"), "rows")
    w_rp = _natural(gather.get("w_rnn_proj"), "rows")
    y_attn = _mm(ot, w_ap, ta=True, tm=1024, tn=1024, tk=D, name="attn_proj", side=gather.take(22))
    y_rnn = _mm(yrin, w_rp, tm=1024, tn=1024, tk=D_RNN, name="rnn_proj", side=gather.take(27))
    mixin = _gate_fwd(y_attn, y_rnn, proj, small["b_gate"], side=gather.take(25))
    w_out = _natural(gather.get("w_out"), "rows")
    mix = _mm(mixin, w_out, tm=1024, tn=1024, tk=D, name="mix_out", side=gather.take(22))
    x1, x1b, xhat1, rstd1 = _ln_fwd(x2, mix, small["ln1_g"], small["ln1_b"], side=gather.take(23))

    w_up = gather.get("ffn_w_up")
    up = _mm(x1b, w_up, tm=1024, tn=768, tk=D, b_block=768, name="ffn_up", side=gather.take(58))
    w_gate = gather.get("ffn_w_gate")
    gpre = _mm(x1b, w_gate, tm=1024, tn=768, tk=D, b_block=768, name="ffn_gate", side=gather.take(58))
    fconv_w = _natural(gather.get("ffn_conv_w"), "cols")
    fin = _ffn_act_fwd(up, gpre, fconv_w, small["ffn_conv_b"], side=gather.take())
    w_down = _natural(gather.get("ffn_w_down"), "rows")
    f = _mm(fin, w_down, tm=1024, tn=1024, tk=2048, name="ffn_down")
    loss, dpre2, dpre2b, d_ln2_g, d_ln2_b = _ln_loss_bwd(x1, f, small["ln2_g"], small["ln2_b"], target)

    grads = {"ln2_g": d_ln2_g, "ln2_b": d_ln2_b}
    both = (F32, BF16)
    g32, g16 = _mm(fin, dpre2b, ta=True, tm=1024, tn=1024, tk=S, out_dtype=both, name="d_ffn_w_down")
    scatter.add_blocks("ffn_w_down", _blocks(g32, "rows"), _blocks(g16, "rows"))
    dfin = _mm(dpre2b, w_down, tb=True, tm=1024, tn=1024, tk=D, name="d_fin", side=scatter.take(57))
    dup, dgpre, grads["ffn_conv_w"], grads["ffn_conv_b"] = _ffn_act_bwd(
        dfin, up, gpre, fconv_w, small["ffn_conv_b"], side=scatter.take(85))
    g32, g16 = _mm(x1b, dup, ta=True, tm=1024, tn=768, tk=S, out_dtype=both, out_block=768, name="d_ffn_w_up",
                   side=scatter.take(57))
    scatter.add_blocks("ffn_w_up", g32, g16)
    g32, g16 = _mm(x1b, dgpre, ta=True, tm=1024, tn=768, tk=S, out_dtype=both, out_block=768, name="d_ffn_w_gate",
                   side=scatter.take(56))
    scatter.add_blocks("ffn_w_gate", g32, g16)
    dx1 = _mm(dup, w_up, tb=True, tm=1024, tn=1024, tk=768, b_block=768, name="d_x1_up", side=scatter.take(68))
    dx1 = _mm(dgpre, w_gate, tb=True, tm=1024, tn=1024, tk=768, b_block=768, add=dx1, name="d_x1_gate",
              side=scatter.take(70))
    dpre1, dpre1b, grads["ln1_g"], grads["ln1_b"] = _ln_bwd(dx1, dpre2, xhat1, rstd1, small["ln1_g"],
                                                            side=scatter.take(24))

    g32, g16 = _mm(mixin, dpre1b, ta=True, tm=1024, tn=1024, tk=S, out_dtype=both, name="d_w_out",
                   side=scatter.take(26))
    scatter.add_blocks("w_out", _blocks(g32, "rows"), _blocks(g16, "rows"))
    dmix = _mm(dpre1b, w_out, tb=True, tm=1024, tn=1024, tk=D, name="d_mixin", side=scatter.take(22))
    dya, dyr, dgl_a, dgl_r, db_a, db_r = _gate_bwd(dmix, y_attn, y_rnn, proj, small["b_gate"], side=scatter.take(36))
    grads["b_gate"] = jnp.concatenate([db_a, db_r], axis=1)
    g32, g16 = _mm(ot, dya, tm=1024, tn=1024, tk=S, out_dtype=both, name="d_w_attn_proj", side=scatter.take(38))
    scatter.add_blocks("w_attn_proj", _blocks(g32, "rows"), _blocks(g16, "rows"))
    g32, g16 = _mm(yrin, dyr, ta=True, tm=1280, tn=1024, tk=S, out_dtype=both, name="d_w_rnn_proj",
                   side=scatter.take(27))
    scatter.add_blocks("w_rnn_proj", _blocks(g32, "rows"), _blocks(g16, "rows"))
    dot_ = _mm(w_ap, dya, tb=True, tm=1024, tn=1024, tk=D, out_dtype=MXU_DTYPE, name="d_o", side=scatter.take(22))
    dyrin = _mm(dyr, w_rp, tb=True, tm=1024, tn=1280, tk=D, name="d_yrin", side=scatter.take(27))

    dry, dzr, dzi, drxc_in, grads["lru_ba"], grads["lru_bi"], grads["lru_lambda"] = _lru_scan_bwd(
        dyrin, proj, h, r, i, rxc, small["lru_lambda"], side=scatter.take(94))
    grads["lru_wa"], grads["lru_wi"] = _lru_gate_wgrad(rxc, dzr, dzi, side=scatter.take(22))
    drxc = _lru_gate_xgrad(dzr, dzi, small["lru_wa"], small["lru_wi"], drxc_in, side=scatter.take(33))
    drx, grads["rnn_conv_w"], grads["rnn_conv_b"] = _conv_bwd(drxc, proj, OFF_RX, rconv_w, tc=512,
                                                             name="rnn_conv_bwd", side=scatter.take(29))

    dqt, dk, dv, dsink = _attn_bwd(qt, kp, kt, vp, sink_row, dot_.reshape(N_KV, GROUP, HEAD_DIM, S),
                                   side=scatter.take(65))
    grads["attn_sinks"] = dsink.reshape(1, N_KV * GROUP)
    dproj = jnp.concatenate([
        dqt.reshape(D, S).T,
        _heads_minor(dk[:, BLOCK:, :]).astype(MXU_DTYPE),
        _heads_minor(dv[:, BLOCK:, :]).astype(MXU_DTYPE),
        drx, dry, dgl_a, dgl_r], axis=1)
    g32, g16 = _mm(xb, dproj, ta=True, tm=1024, tn=512, tk=S, out_dtype=both, name="d_w_in",
                   side=_join(scatter.take(110), sync.begin(loss, grads)))
    scatter.add_cols("w_in", g32, g16)
    scatter.flush_pairs("pairs_w_in")
    dx = _mm(dproj, w_in, tb=True, tm=1024, tn=1024, tk=512, add=dpre1, add_scale=ALPHA, name="d_x",
             side=_join(scatter.take(300), sync.middle()))
    return dx


SHARDED = (
    ("w_in", "cols", 128), ("w_attn_proj", "rows", 32), ("w_rnn_proj", "rows", 32), ("w_out", "rows", 32),
    ("ffn_w_up", "cols", 128), ("ffn_w_gate", "cols", 128), ("ffn_w_down", "rows", 64),
)
SMALL_REPLICATED = ("b_gate", "rnn_conv_b", "lru_wa", "lru_ba", "lru_wi", "lru_bi", "lru_lambda", "attn_sinks",
                    "ln1_g", "ln1_b", "ffn_conv_b", "ln2_g", "ln2_b")
SMALL_SHARDED = ("rnn_conv_w", "ffn_conv_w")
SMALL_MATS = ("lru_wa", "lru_wi")
WEIGHTS = ("w_in", "b_gate", "rnn_conv_w", "rnn_conv_b", "lru_wa", "lru_ba", "lru_wi", "lru_bi", "lru_lambda",
           "attn_sinks", "w_attn_proj", "w_rnn_proj", "w_out", "ln1_g", "ln1_b", "ffn_w_up", "ffn_w_gate",
           "ffn_conv_w", "ffn_conv_b", "ffn_w_down", "ln2_g", "ln2_b")


def kernel(x, w_in, b_gate, rnn_conv_w, rnn_conv_b, lru_wa, lru_ba, lru_wi, lru_bi, lru_lambda, attn_sinks, w_attn_proj, w_rnn_proj, w_out, ln1_g, ln1_b, ffn_w_up, ffn_w_gate, ffn_conv_w, ffn_conv_b, ffn_w_down, ln2_g, ln2_b, loss_target, m_w_in, m_b_gate, m_rnn_conv_w, m_rnn_conv_b, m_lru_wa, m_lru_ba, m_lru_wi, m_lru_bi, m_lru_lambda, m_attn_sinks, m_w_attn_proj, m_w_rnn_proj, m_w_out, m_ln1_g, m_ln1_b, m_ffn_w_up, m_ffn_w_gate, m_ffn_conv_w, m_ffn_conv_b, m_ffn_w_down, m_ln2_g, m_ln2_b, v_w_in, v_b_gate, v_rnn_conv_w, v_rnn_conv_b, v_lru_wa, v_lru_ba, v_lru_wi, v_lru_bi, v_lru_lambda, v_attn_sinks, v_w_attn_proj, v_w_rnn_proj, v_w_out, v_ln1_g, v_ln1_b, v_ffn_w_up, v_ffn_w_gate, v_ffn_conv_w, v_ffn_conv_b, v_ffn_w_down, v_ln2_g, v_ln2_b):
    given = dict(locals())
    wsh = {n: given[n][0] for n in WEIGHTS}
    msh = {n: given["m_" + n][0] for n in WEIGHTS}
    vsh = {n: given["v_" + n][0] for n in WEIGHTS}
    m_given = {n: given["m_" + n] for n in WEIGHTS}
    v_given = {n: given["v_" + n] for n in WEIGHTS}
    me = 4 * lax.axis_index("x") + 2 * lax.axis_index("y") + lax.axis_index("c")

    order = ("w_in", "rnn_conv_w", "ffn_conv_w", "w_attn_proj", "w_rnn_proj", "w_out", "ffn_w_up", "ffn_w_gate",
             "ffn_w_down")
    gather = _Gather({n: wsh[n] if n in SMALL_SHARDED else wsh[n].astype(MXU_DTYPE) for n in order})
    _run_side(gather.take(through="ffn_conv_w"), "gather_first")
    small = {n: given[n] for n in SMALL_REPLICATED}
    small["lru_wa"] = _block_diag(wsh["lru_wa"])
    small["lru_wi"] = _block_diag(wsh["lru_wi"])
    scatter = _Scatter(me, jnp.stack([_index(_relative(k)) for k in FAR]).astype(jnp.int32))

    vec_names = tuple(n for n in SMALL_REPLICATED if n not in SMALL_MATS) + SMALL_SHARDED
    sync = _SmallSync(vec_names, SMALL_MATS)
    dx = _forward_backward(x[0], loss_target[0], small, gather, scatter, sync)

    loss_total, g_small, mat_sums = sync.end()
    loss_total = loss_total.reshape(())
    for n in SMALL_SHARDED:
        width = wsh[n].shape[1]
        g_small[n] = lax.dynamic_slice_in_dim(g_small[n], me * width, width, axis=1)
    g_small = {n: g_small[n].reshape(given[n].shape) for n in vec_names}
    out = {}
    results = _adamw_many(*[[d[n] for n in vec_names] for d in (given, m_given, v_given, g_small)])
    for n, delta, nm, nv in zip(vec_names, *results):
        out[n] = (g_small[n], delta, nm, nv)
    for n in SMALL_MATS:
        g = mat_sums[n].reshape(given[n].shape)
        out[n] = (g, *_adamw_blocks(given[n], m_given[n], v_given[n], g, name="adamw_" + n))

    tile_rows = {n: tr for n, _, tr in SHARDED}
    for n in list(scatter.sends):
        own, pair, far = scatter.get(n)
        res = _reduce_adamw(wsh[n], msh[n], vsh[n], own, pair, far, me.reshape(1).astype(jnp.int32),
                            tr=tile_rows[n], name="adamw_" + n)
        out[n] = tuple(r[None] for r in res)

    outputs = [loss_total, dx[None]]
    for kind in range(4):
        outputs += [out[n][kind] for n in WEIGHTS]
    return tuple(outputs)
```

```python
import math

import jax
import jax.numpy as jnp
from jax import lax
from jax.experimental import pallas as pl
from jax.experimental.pallas import tpu as pltpu

F32 = jnp.float32
BF16 = jnp.bfloat16
MXU_DTYPE = jnp.bfloat16

N_DEV = 8
S = 2048
D = 2048
HEAD_DIM = 64
N_KV = 4
GROUP = 8
BLOCK = 128
D_KV = N_KV * HEAD_DIM
D_RNN = 2560
RNN_GROUP = 640
N_RNN_GROUPS = D_RNN // RNN_GROUP
RNN_BLOCK_W = 160
RNN_CONV_W = 4
LRU_C = 8.0
D_FF = 6144
FFN_CONV_W = 3
D_IN = 11776
OFF_K = 2048
OFF_V = 2304
OFF_RX = 2560
OFF_RY = 5120
OFF_GA = 7680
OFF_GR = 9728
LN_EPS = 1e-5
ALPHA = 2.0 ** 0.25
ADAM_LR = 0.001
ADAM_B1 = 0.9
ADAM_B2 = 0.999
ADAM_EPS = 1e-08
ADAM_WD = 0.01
ADAM_STEP = 10
NEG = -1e30
VMEM_LIMIT = 56 * 1024 * 1024
MID_RIDE_TENTHS = 6
MESH = pl.DeviceIdType.MESH
GELU_C = math.sqrt(2.0 / math.pi)


def _cparams(*sem):
    return pltpu.CompilerParams(dimension_semantics=sem or None, vmem_limit_bytes=VMEM_LIMIT)


def _call(body, *, name, grid, in_specs, out_specs, out_shape, operands, semantics, scratch_shapes=(), side=None):
    single = not isinstance(out_shape, (list, tuple))
    out_shape = [out_shape] if single else list(out_shape)
    out_specs = [out_specs] if single else list(out_specs)
    in_specs = list(in_specs)
    scratch_shapes = list(scratch_shapes)
    if side is None:
        res = pl.pallas_call(
            body, name=name, grid=grid, in_specs=in_specs, out_specs=out_specs, out_shape=out_shape,
            scratch_shapes=scratch_shapes, compiler_params=_cparams(*semantics))(*operands)
        return res[0] if single else res
    n_in, n_out, n_scr = len(in_specs), len(out_shape), len(scratch_shapes)
    s_in, s_out = len(side.operands), len(side.out_shape)
    hbm = pl.BlockSpec(memory_space=pltpu.HBM)
    steps = math.prod(grid)
    mid_step = (steps * MID_RIDE_TENTHS) // 10

    def with_copies(*refs):
        core_in, side_in = refs[:n_in], refs[n_in:n_in + s_in]
        o0 = n_in + s_in
        core_out, side_out = refs[o0:o0 + n_out], refs[o0 + n_out:o0 + n_out + s_out]
        c0 = o0 + n_out + s_out
        core_scr, sems = refs[c0:c0 + n_scr], refs[c0 + n_scr:]
        step = 0
        for d, size in enumerate(grid):
            step = step * size + pl.program_id(d)

        @pl.when(step == 0)
        def _():
            side.start(side_in, side_out, sems)

        body(*core_in, *core_out, *core_scr)

        @pl.when(step == mid_step)
        def _():
            side.mid(side_in, side_out, sems)

        @pl.when(step == steps - 1)
        def _():
            side.finish(side_in, side_out, sems)

    res = pl.pallas_call(
        with_copies, name=name, grid=grid,
        in_specs=in_specs + [hbm] * s_in, out_specs=out_specs + [hbm] * s_out,
        out_shape=out_shape + list(side.out_shape),
        scratch_shapes=scratch_shapes + list(side.sems),
        input_output_aliases={n_in + i: n_out + o for i, o in side.aliases.items()},
        compiler_params=_cparams(*(("arbitrary",) * len(grid))))(*operands, *side.operands)
    side.done(res[n_out:])
    return res[0] if single else res[:n_out]


def _run_side(side, name):
    def body(*refs):
        s_in, s_out = len(side.operands), len(side.out_shape)
        side.start(refs[:s_in], refs[s_in:s_in + s_out], refs[s_in + s_out:])
        side.mid(refs[:s_in], refs[s_in:s_in + s_out], refs[s_in + s_out:])
        side.finish(refs[:s_in], refs[s_in:s_in + s_out], refs[s_in + s_out:])

    hbm = pl.BlockSpec(memory_space=pltpu.HBM)
    res = pl.pallas_call(
        body, name=name, in_specs=[hbm] * len(side.operands), out_specs=[hbm] * len(side.out_shape),
        out_shape=list(side.out_shape), scratch_shapes=list(side.sems),
        input_output_aliases=dict(side.aliases))(*side.operands)
    side.done(res)


def _gelu(x):
    x2 = x * x
    t = jnp.tanh(GELU_C * (x + 0.044715 * x * x2))
    g = 0.5 * x * (1.0 + t)
    dg = 0.5 * (1.0 + t) + 0.5 * x * (1.0 - t * t) * (GELU_C * (1.0 + 3.0 * 0.044715 * x2))
    return g, dg


def _sigmoid(x):
    return 1.0 / (1.0 + jnp.exp(-x))


def _softplus(x):
    z = jnp.exp(-jnp.abs(x))
    small = z * (1.0 - z * (0.5 - z * (1.0 / 3.0 - 0.25 * z)))
    return jnp.maximum(x, 0.0) + jnp.where(z < 0.02, small, jnp.log(1.0 + z))


def _one_minus_exp(x):
    series = -x * (1.0 + x * (0.5 + x * (1.0 / 6.0 + x * (1.0 / 24.0))))
    return jnp.where(x > -0.03, series, 1.0 - jnp.exp(x))


def _colsum(v):
    return jnp.sum(v, axis=0, keepdims=True)


def _mm(a, b, *, tm, tn, tk, name, ta=False, tb=False, out_dtype=F32, b_block=None, out_block=None, add=None,
        add_scale=1.0, side=None):
    out_dtypes = out_dtype if isinstance(out_dtype, tuple) else (out_dtype,)
    if ta:
        k_dim, m_dim = a.shape
    else:
        m_dim, k_dim = a.shape
    if b_block is None:
        n_dim = b.shape[0] if tb else b.shape[1]
    else:
        n_dim = b.shape[1] if tb else b.shape[0] * b_block
    assert m_dim % tm == 0 and n_dim % tn == 0 and k_dim % tk == 0, (name, m_dim, n_dim, k_dim)
    nk = k_dim // tk
    dims = (((0 if ta else 1,), (1 if tb else 0,)), ((), ()))
    has_add = add is not None

    def body(*refs):
        a_ref, b_ref = refs[0], refs[1]
        add_ref = refs[2] if has_add else None
        first_out = 3 if has_add else 2
        o_refs = refs[first_out:first_out + len(out_dtypes)]

        def product():
            return lax.dot_general(a_ref[...].astype(MXU_DTYPE), b_ref[...].astype(MXU_DTYPE), dims,
                                   preferred_element_type=F32)

        def finish(acc):
            if has_add:
                acc = acc + add_scale * add_ref[...]
            for o_ref in o_refs:
                o_ref[...] = acc.astype(o_ref.dtype)

        if nk == 1:
            finish(product())
        else:
            acc_ref = refs[-1]
            k = pl.program_id(2)

            @pl.when(k == 0)
            def _():
                acc_ref[...] = jnp.zeros_like(acc_ref)

            acc_ref[...] += product()

            @pl.when(k == nk - 1)
            def _():
                finish(acc_ref[...])

    if ta:
        a_spec = pl.BlockSpec((tk, tm), lambda i, j, k: (k, i))
    else:
        a_spec = pl.BlockSpec((tm, tk), lambda i, j, k: (i, k))
    if b_block is None:
        if tb:
            b_spec = pl.BlockSpec((tn, tk), lambda i, j, k: (j, k))
        else:
            b_spec = pl.BlockSpec((tk, tn), lambda i, j, k: (k, j))
    elif tb:
        assert b_block % tk == 0
        b_spec = pl.BlockSpec((None, tn, tk), lambda i, j, k: ((k * tk) // b_block, j, ((k * tk) % b_block) // tk))
    else:
        assert b_block % tn == 0
        b_spec = pl.BlockSpec((None, tk, tn), lambda i, j, k: ((j * tn) // b_block, k, ((j * tn) % b_block) // tn))
    in_specs = [a_spec, b_spec]
    operands = [a, b]
    if has_add:
        in_specs.append(pl.BlockSpec((tm, tn), lambda i, j, k: (i, j)))
        operands.append(add)
    if out_block is None:
        out_spec = pl.BlockSpec((tm, tn), lambda i, j, k: (i, j))
        out_dims = (m_dim, n_dim)
    else:
        assert out_block % tn == 0
        out_spec = pl.BlockSpec((None, tm, tn), lambda i, j, k: ((j * tn) // out_block, i, ((j * tn) % out_block) // tn))
        out_dims = (n_dim // out_block, m_dim, out_block)
    res = _call(
        body,
        name=name,
        grid=(m_dim // tm, n_dim // tn, nk),
        in_specs=in_specs,
        out_specs=[out_spec] * len(out_dtypes),
        out_shape=[jax.ShapeDtypeStruct(out_dims, dt) for dt in out_dtypes],
        scratch_shapes=[pltpu.VMEM((tm, tn), F32)] if nk > 1 else [],
        semantics=("parallel", "parallel", "arbitrary"),
        operands=tuple(operands),
        side=side,
    )
    return res if isinstance(out_dtype, tuple) else res[0]


def _attn_bias(bias_ref, h):
    key = lax.broadcasted_iota(jnp.int32, (2 * BLOCK, GROUP * BLOCK), 0)
    col = lax.broadcasted_iota(jnp.int32, (2 * BLOCK, GROUP * BLOCK), 1)
    dist = BLOCK + (col & (BLOCK - 1)) - key
    head = h * GROUP + (col >> 7) + 1
    slope = jnp.exp(head.astype(F32) * (-0.25 * math.log(2.0)))
    bias = jnp.where((dist >= 0) & (dist < BLOCK), -slope * dist.astype(F32), NEG)
    bias_ref[1] = bias
    bias_ref[0] = jnp.where(key < BLOCK, NEG, bias)


def _attn_probs(kb, qt, bias, sink):
    s = jnp.dot(kb, qt, preferred_element_type=F32) * (HEAD_DIM ** -0.5) + bias
    m = jnp.maximum(jnp.max(s, axis=0, keepdims=True), sink)
    e = jnp.exp(s - m)
    e_sink = jnp.exp(sink - m)
    inv = 1.0 / (jnp.sum(e, axis=0, keepdims=True) + e_sink)
    return e * inv, e_sink * inv


def _heads_on_lanes(ref, r0):
    return jnp.concatenate([ref[g, :, pl.ds(r0, BLOCK)] for g in range(GROUP)], axis=1)


def _attn_fwd(qt, kp, vt, sink_row, side=None):
    cols = GROUP * BLOCK

    def body(q_ref, k_ref, vt_ref, sink_ref, o_ref, bias_ref):
        _attn_bias(bias_ref, pl.program_id(0))
        sink = sink_ref[...]

        def step(n, carry):
            r0 = pl.multiple_of(n * BLOCK, BLOCK)
            p, _ = _attn_probs(k_ref[pl.ds(r0, 2 * BLOCK), :], _heads_on_lanes(q_ref, r0),
                               bias_ref[jnp.minimum(n, 1)], sink)
            o = jnp.dot(vt_ref[:, pl.ds(r0, 2 * BLOCK)], p.astype(MXU_DTYPE), preferred_element_type=F32)
            for g in range(GROUP):
                o_ref[g, :, pl.ds(r0, BLOCK)] = o[:, g * BLOCK:(g + 1) * BLOCK].astype(o_ref.dtype)
            return carry

        lax.fori_loop(0, S // BLOCK, step, 0)

    hm = pl.BlockSpec((None, GROUP, HEAD_DIM, S), lambda h: (h, 0, 0, 0))
    return _call(
        body,
        name="attn_fwd",
        grid=(N_KV,),
        in_specs=[
            hm,
            pl.BlockSpec((None, BLOCK + S, HEAD_DIM), lambda h: (h, 0, 0)),
            pl.BlockSpec((None, HEAD_DIM, BLOCK + S), lambda h: (h, 0, 0)),
            pl.BlockSpec((None, 1, cols), lambda h: (h, 0, 0)),
        ],
        out_specs=hm,
        out_shape=jax.ShapeDtypeStruct((N_KV, GROUP, HEAD_DIM, S), MXU_DTYPE),
        scratch_shapes=[pltpu.VMEM((2, 2 * BLOCK, cols), F32)],
        semantics=("parallel",),
        operands=(qt, kp, vt, sink_row),
        side=side,
    )


def _attn_bwd(qt, kp, kt, vp, sink_row, dot_, side=None):
    cols = GROUP * BLOCK

    def body(q_ref, k_ref, kt_ref, v_ref, sink_ref, do_ref, dq_ref, dk_ref, dv_ref, dsink_ref, bias_ref):
        _attn_bias(bias_ref, pl.program_id(0))
        sink = sink_ref[...]
        dk_ref[...] = jnp.zeros_like(dk_ref)
        dv_ref[...] = jnp.zeros_like(dv_ref)
        nt = (((1,), (1,)), ((), ()))

        def step(n, sink_acc):
            r0 = pl.multiple_of(n * BLOCK, BLOCK)
            band = pl.ds(r0, 2 * BLOCK)
            qn = _heads_on_lanes(q_ref, r0)
            don = _heads_on_lanes(do_ref, r0)
            p, p_sink = _attn_probs(k_ref[band, :], qn, bias_ref[jnp.minimum(n, 1)], sink)
            dp = jnp.dot(v_ref[band, :], don, preferred_element_type=F32)
            delta = jnp.sum(p * dp, axis=0, keepdims=True)
            ds = (p * (dp - delta) * (HEAD_DIM ** -0.5)).astype(MXU_DTYPE)
            dq = jnp.dot(kt_ref[:, band], ds, preferred_element_type=F32)
            for g in range(GROUP):
                dq_ref[g, :, pl.ds(r0, BLOCK)] = dq[:, g * BLOCK:(g + 1) * BLOCK].astype(dq_ref.dtype)
            dk_ref[band, :] += lax.dot_general(ds, qn, nt, preferred_element_type=F32)
            dv_ref[band, :] += lax.dot_general(p.astype(MXU_DTYPE), don, nt, preferred_element_type=F32)
            return sink_acc - p_sink * delta

        sink_acc = lax.fori_loop(0, S // BLOCK, step, jnp.zeros((1, cols), F32))
        for g in range(GROUP):
            dsink_ref[g:g + 1, :] = jnp.sum(sink_acc[:, g * BLOCK:(g + 1) * BLOCK], axis=1, keepdims=True)

    hm = pl.BlockSpec((None, GROUP, HEAD_DIM, S), lambda h: (h, 0, 0, 0))
    kv = pl.BlockSpec((None, BLOCK + S, HEAD_DIM), lambda h: (h, 0, 0))
    return _call(
        body,
        name="attn_bwd",
        grid=(N_KV,),
        in_specs=[hm, kv, pl.BlockSpec((None, HEAD_DIM, BLOCK + S), lambda h: (h, 0, 0)), kv,
                  pl.BlockSpec((None, 1, cols), lambda h: (h, 0, 0)), hm],
        out_specs=[hm, kv, kv, pl.BlockSpec((None, GROUP, 1), lambda h: (h, 0, 0))],
        out_shape=[
            jax.ShapeDtypeStruct((N_KV, GROUP, HEAD_DIM, S), MXU_DTYPE),
            jax.ShapeDtypeStruct((N_KV, BLOCK + S, HEAD_DIM), F32),
            jax.ShapeDtypeStruct((N_KV, BLOCK + S, HEAD_DIM), F32),
            jax.ShapeDtypeStruct((N_KV, GROUP, 1), F32),
        ],
        scratch_shapes=[pltpu.VMEM((2, 2 * BLOCK, cols), F32)],
        semantics=("parallel",),
        operands=(qt, kp, kt, vp, sink_row, dot_),
        side=side,
    )


PAD = 8
CHUNK = 256


def _past_taps(xpad_ref, r0, width):
    ext = xpad_ref[pl.ds(r0, CHUNK + PAD), :]
    taps = []
    for k in range(width):
        back = width - 1 - k
        taps.append((ext if back == 0 else pltpu.roll(ext, back, 0))[PAD:, :])
    return taps


def _future_taps(xpad_ref, r0, width):
    ext = xpad_ref[pl.ds(r0, CHUNK + PAD), :]
    taps = []
    for ahead in range(width):
        taps.append((ext if ahead == 0 else pltpu.roll(ext, CHUNK + PAD - ahead, 0))[:CHUNK, :])
    return taps


def _conv_fwd(src, col0, w, b, *, tc, name, side=None):
    width, c_dim = w.shape

    def body(x_ref, w_ref, b_ref, o_ref, xpad_ref):
        xpad_ref[pl.ds(0, PAD), :] = jnp.zeros((PAD, tc), F32)
        xpad_ref[pl.ds(PAD, S), :] = x_ref[...]
        wv = w_ref[...]
        bv = b_ref[...]

        def step(ci, carry):
            r0 = pl.multiple_of(ci * CHUNK, CHUNK)
            taps = _past_taps(xpad_ref, r0, width)
            y = bv + taps[0] * wv[0:1, :]
            for k in range(1, width):
                y = y + taps[k] * wv[k:k + 1, :]
            o_ref[pl.ds(r0, CHUNK), :] = y
            return carry

        lax.fori_loop(0, S // CHUNK, step, 0)

    return _call(
        body,
        name=name,
        grid=(c_dim // tc,),
        in_specs=[
            pl.BlockSpec((S, tc), lambda j: (0, col0 // tc + j)),
            pl.BlockSpec((width, tc), lambda j: (0, j)),
            pl.BlockSpec((1, tc), lambda j: (0, j)),
        ],
        out_specs=pl.BlockSpec((S, tc), lambda j: (0, j)),
        out_shape=jax.ShapeDtypeStruct((S, c_dim), F32),
        scratch_shapes=[pltpu.VMEM((S + PAD, tc), F32)],
        semantics=("parallel",),
        operands=(src, w, b),
        side=side,
    )


def _conv_bwd(dy, src, col0, w, *, tc, name, side=None):
    width, c_dim = w.shape

    def body(dy_ref, x_ref, w_ref, dx_ref, dw_ref, db_ref, xpad_ref, dpad_ref):
        xpad_ref[pl.ds(0, PAD), :] = jnp.zeros((PAD, tc), F32)
        xpad_ref[pl.ds(PAD, S), :] = x_ref[...]
        dpad_ref[pl.ds(0, S), :] = dy_ref[...]
        dpad_ref[pl.ds(S, PAD), :] = jnp.zeros((PAD, tc), F32)
        wv = w_ref[...]

        def step(ci, acc):
            r0 = pl.multiple_of(ci * CHUNK, CHUNK)
            past = _past_taps(xpad_ref, r0, width)
            ahead = _future_taps(dpad_ref, r0, width)
            d = ahead[0]
            dx = d * wv[width - 1:width, :]
            for j in range(1, width):
                dx = dx + ahead[j] * wv[width - 1 - j:width - j, :]
            dx_ref[pl.ds(r0, CHUNK), :] = dx.astype(dx_ref.dtype)
            return tuple(acc[k] + _colsum(past[k] * d) for k in range(width)) + (acc[width] + _colsum(d),)

        zero = jnp.zeros((1, tc), F32)
        acc = lax.fori_loop(0, S // CHUNK, step, (zero,) * (width + 1))
        for k in range(width):
            dw_ref[k:k + 1, :] = acc[k]
        db_ref[...] = acc[width]

    return _call(
        body,
        name=name,
        grid=(c_dim // tc,),
        in_specs=[
            pl.BlockSpec((S, tc), lambda j: (0, j)),
            pl.BlockSpec((S, tc), lambda j: (0, col0 // tc + j)),
            pl.BlockSpec((width, tc), lambda j: (0, j)),
        ],
        out_specs=[
            pl.BlockSpec((S, tc), lambda j: (0, j)),
            pl.BlockSpec((width, tc), lambda j: (0, j)),
            pl.BlockSpec((1, tc), lambda j: (0, j)),
        ],
        out_shape=[
            jax.ShapeDtypeStruct((S, c_dim), MXU_DTYPE),
            jax.ShapeDtypeStruct((width, c_dim), F32),
            jax.ShapeDtypeStruct((1, c_dim), F32),
        ],
        scratch_shapes=[pltpu.VMEM((S + PAD, tc), F32), pltpu.VMEM((S + PAD, tc), F32)],
        semantics=("parallel",),
        operands=(dy, src, w),
        side=side,
    )


SCAN_TC = 256


def _lru_gates(rxc, wa, wi, ba, bi, side=None):
    tm = 512

    def body(x_ref, wa_ref, wi_ref, ba_ref, bi_ref, r_ref, i_ref):
        xv = x_ref[...].astype(MXU_DTYPE)
        r_ref[...] = _sigmoid(jnp.dot(xv, wa_ref[...].astype(MXU_DTYPE), preferred_element_type=F32) + ba_ref[...])
        i_ref[...] = _sigmoid(jnp.dot(xv, wi_ref[...].astype(MXU_DTYPE), preferred_element_type=F32) + bi_ref[...])

    x_spec = pl.BlockSpec((tm, RNN_GROUP), lambda g, i: (i, g))
    w_spec = pl.BlockSpec((None, RNN_GROUP, RNN_GROUP), lambda g, i: (g, 0, 0))
    b_spec = pl.BlockSpec((1, RNN_GROUP), lambda g, i: (0, g))
    return _call(
        body,
        name="lru_gates",
        grid=(N_RNN_GROUPS, S // tm),
        in_specs=[x_spec, w_spec, w_spec, b_spec, b_spec],
        out_specs=[x_spec, x_spec],
        out_shape=[jax.ShapeDtypeStruct((S, D_RNN), F32)] * 2,
        semantics=("parallel", "parallel"),
        operands=(rxc, wa, wi, ba, bi),
        side=side,
    )


def _scan_down(a, u, row):
    for d in (1, 2, 4):
        a_s = jnp.where(row >= d, pltpu.roll(a, d, 0), 1.0)
        u_s = jnp.where(row >= d, pltpu.roll(u, d, 0), 0.0)
        u = a * u_s + u
        a = a * a_s
    return a, u


def _scan_up(a, u, row):
    for d in (1, 2, 4):
        a_s = jnp.where(row < 8 - d, pltpu.roll(a, 8 - d, 0), 1.0)
        u_s = jnp.where(row < 8 - d, pltpu.roll(u, 8 - d, 0), 0.0)
        u = a * u_s + u
        a = a * a_s
    return a, u


def _lru_scan_fwd(r, i, rxc, proj, lam, side=None):
    tc = SCAN_TC

    def body(r_ref, i_ref, x_ref, ry_ref, lam_ref, h_ref, y_ref):
        rate = LRU_C * _softplus(-lam_ref[...])
        row = lax.broadcasted_iota(jnp.int32, (8, tc), 0)

        def step(ci, carry):
            r0 = pl.multiple_of(ci * 16, 16)
            log_a = -rate * r_ref[pl.ds(r0, 16), :]
            a16 = jnp.exp(log_a)
            u16 = jnp.sqrt(_one_minus_exp(2.0 * log_a)) * (i_ref[pl.ds(r0, 16), :] * x_ref[pl.ds(r0, 16), :])
            hs = []
            for half in range(2):
                a_cum, h0 = _scan_down(a16[8 * half:8 * half + 8, :], u16[8 * half:8 * half + 8, :], row)
                h = a_cum * carry + h0
                carry = jnp.broadcast_to(h[7:8, :], (8, tc))
                hs.append(h)
            h16 = jnp.concatenate(hs, axis=0)
            h_ref[pl.ds(r0, 16), :] = h16
            y_ref[pl.ds(r0, 16), :] = (h16 * _gelu(ry_ref[pl.ds(r0, 16), :])[0]).astype(y_ref.dtype)
            return carry

        lax.fori_loop(0, S // 16, step, jnp.zeros((8, tc), F32))

    col = pl.BlockSpec((S, tc), lambda j: (0, j))
    return _call(
        body,
        name="lru_scan_fwd",
        grid=(D_RNN // tc,),
        in_specs=[col, col, col, pl.BlockSpec((S, tc), lambda j: (0, OFF_RY // tc + j)),
                  pl.BlockSpec((1, tc), lambda j: (0, j))],
        out_specs=[col, col],
        out_shape=[jax.ShapeDtypeStruct((S, D_RNN), F32), jax.ShapeDtypeStruct((S, D_RNN), MXU_DTYPE)],
        semantics=("parallel",),
        operands=(r, i, rxc, proj, lam),
        side=side,
    )


def _lru_scan_bwd(dy, proj, h, r, i, rxc, lam, side=None):
    tc = SCAN_TC

    def body(dy_ref, ry_ref, h_ref, r_ref, i_ref, x_ref, lam_ref,
             dry_ref, dzr_ref, dzi_ref, dx_ref, dba_ref, dbi_ref, dlam_ref, a_ref, dh_ref, hp_ref):
        lam_v = lam_ref[...]
        rate = LRU_C * _softplus(-lam_v)
        dlam_scale = LRU_C * _sigmoid(-lam_v)
        row = lax.broadcasted_iota(jnp.int32, (8, tc), 0)
        hp_ref[pl.ds(0, PAD), :] = jnp.zeros((PAD, tc), F32)
        hp_ref[pl.ds(PAD, S), :] = h_ref[...]
        a_ref[pl.ds(S, PAD), :] = jnp.zeros((PAD, tc), F32)

        def prep(ci, carry):
            r0 = pl.multiple_of(ci * CHUNK, CHUNK)
            a_ref[pl.ds(r0, CHUNK), :] = jnp.exp(-rate * r_ref[pl.ds(r0, CHUNK), :])
            ge, dge = _gelu(ry_ref[pl.ds(r0, CHUNK), :])
            dyv = dy_ref[pl.ds(r0, CHUNK), :]
            dh_ref[pl.ds(r0, CHUNK), :] = dyv * ge
            dry_ref[pl.ds(r0, CHUNK), :] = (dyv * h_ref[pl.ds(r0, CHUNK), :] * dge).astype(dry_ref.dtype)
            return carry

        lax.fori_loop(0, S // CHUNK, prep, 0)

        def step(ci, state):
            carry, dba, dbi, dlam = state
            r0 = pl.multiple_of(S - 16 - ci * 16, 16)
            a_ext = a_ref[pl.ds(r0, 24), :]
            a_next = pltpu.roll(a_ext, 23, 0)
            h_prev = pltpu.roll(hp_ref[pl.ds(r0, 24), :], 1, 0)
            dh16 = dh_ref[pl.ds(r0, 16), :]
            gs = [None, None]
            for half in (1, 0):
                lo = 8 * half
                c_cum, g0 = _scan_up(a_next[lo:lo + 8, :], dh16[lo:lo + 8, :], row)
                g = c_cum * carry + g0
                carry = jnp.broadcast_to(g[0:1, :], (8, tc))
                gs[half] = g
            g16 = jnp.concatenate(gs, axis=0)
            a16 = a_ext[0:16, :]
            r16 = r_ref[pl.ds(r0, 16), :]
            i16 = i_ref[pl.ds(r0, 16), :]
            x16 = x_ref[pl.ds(r0, 16), :]
            a2 = a16 * a16
            sq = jnp.sqrt(_one_minus_exp(-2.0 * rate * r16))
            dx_ref[pl.ds(r0, 16), :] = g16 * sq * i16
            dzi = g16 * sq * x16 * i16 * (1.0 - i16)
            dlog_a = g16 * h_prev[8:24, :] * a16 - g16 * i16 * x16 * a2 / sq
            dzr = -rate * dlog_a * r16 * (1.0 - r16)
            dzr_ref[pl.ds(r0, 16), :] = dzr.astype(dzr_ref.dtype)
            dzi_ref[pl.ds(r0, 16), :] = dzi.astype(dzi_ref.dtype)
            return carry, dba + _colsum(dzr), dbi + _colsum(dzi), dlam + _colsum(dlog_a * r16)

        zero = jnp.zeros((1, tc), F32)
        _, dba, dbi, dlam = lax.fori_loop(0, S // 16, step, (jnp.zeros((8, tc), F32), zero, zero, zero))
        dba_ref[...] = dba
        dbi_ref[...] = dbi
        dlam_ref[...] = dlam * dlam_scale

    col = pl.BlockSpec((S, tc), lambda j: (0, j))
    vec = pl.BlockSpec((1, tc), lambda j: (0, j))
    return _call(
        body,
        name="lru_scan_bwd",
        grid=(D_RNN // tc,),
        in_specs=[col, pl.BlockSpec((S, tc), lambda j: (0, OFF_RY // tc + j)), col, col, col, col, vec],
        out_specs=[col, col, col, col, vec, vec, vec],
        out_shape=[jax.ShapeDtypeStruct((S, D_RNN), MXU_DTYPE)] * 3 + [jax.ShapeDtypeStruct((S, D_RNN), F32)]
        + [jax.ShapeDtypeStruct((1, D_RNN), F32)] * 3,
        scratch_shapes=[pltpu.VMEM((S + PAD, tc), F32), pltpu.VMEM((S, tc), F32), pltpu.VMEM((S + PAD, tc), F32)],
        semantics=("parallel",),
        operands=(dy, proj, h, r, i, rxc, lam),
        side=side,
    )


def _lru_gate_wgrad(rxc, dzr, dzi, side=None):
    def body(x_ref, dzr_ref, dzi_ref, dwa_ref, dwi_ref):
        xv = x_ref[...].astype(MXU_DTYPE)
        dims = (((0,), (0,)), ((), ()))
        dwa_ref[...] = lax.dot_general(xv, dzr_ref[...], dims, preferred_element_type=F32)
        dwi_ref[...] = lax.dot_general(xv, dzi_ref[...], dims, preferred_element_type=F32)

    col = pl.BlockSpec((S, RNN_GROUP), lambda g: (0, g))
    w_spec = pl.BlockSpec((None, RNN_GROUP, RNN_GROUP), lambda g: (g, 0, 0))
    return _call(
        body,
        name="lru_gate_wgrad",
        grid=(N_RNN_GROUPS,),
        in_specs=[col, col, col],
        out_specs=[w_spec, w_spec],
        out_shape=[jax.ShapeDtypeStruct((N_RNN_GROUPS, RNN_GROUP, RNN_GROUP), F32)] * 2,
        semantics=("parallel",),
        operands=(rxc, dzr, dzi),
        side=side,
    )


def _lru_gate_xgrad(dzr, dzi, wa, wi, dx_in, side=None):
    tm = 512

    def body(dzr_ref, dzi_ref, wa_ref, wi_ref, dx_ref, o_ref):
        dims = (((1,), (1,)), ((), ()))
        o_ref[...] = (dx_ref[...]
                      + lax.dot_general(dzr_ref[...], wa_ref[...].astype(MXU_DTYPE), dims, preferred_element_type=F32)
                      + lax.dot_general(dzi_ref[...], wi_ref[...].astype(MXU_DTYPE), dims, preferred_element_type=F32))

    x_spec = pl.BlockSpec((tm, RNN_GROUP), lambda g, i: (i, g))
    w_spec = pl.BlockSpec((None, RNN_GROUP, RNN_GROUP), lambda g, i: (g, 0, 0))
    return _call(
        body,
        name="lru_gate_xgrad",
        grid=(N_RNN_GROUPS, S // tm),
        in_specs=[x_spec, x_spec, w_spec, w_spec, x_spec],
        out_specs=x_spec,
        out_shape=jax.ShapeDtypeStruct((S, D_RNN), F32),
        semantics=("parallel", "parallel"),
        operands=(dzr, dzi, wa, wi, dx_in),
        side=side,
    )


def _gate_fwd(y_attn, y_rnn, proj, b_gate, side=None):
    t = 512

    def body(ya_ref, yr_ref, ga_ref, gr_ref, ba_ref, br_ref, o_ref):
        o_ref[...] = (_sigmoid(ga_ref[...] + ba_ref[...]) * ya_ref[...]
                      + _sigmoid(gr_ref[...] + br_ref[...]) * yr_ref[...]).astype(o_ref.dtype)

    tile = pl.BlockSpec((t, t), lambda i, j: (i, j))
    return _call(
        body,
        name="gate_fwd",
        grid=(S // t, D // t),
        in_specs=[tile, tile,
                  pl.BlockSpec((t, t), lambda i, j: (i, OFF_GA // t + j)),
                  pl.BlockSpec((t, t), lambda i, j: (i, OFF_GR // t + j)),
                  pl.BlockSpec((1, t), lambda i, j: (0, j)),
                  pl.BlockSpec((1, t), lambda i, j: (0, D // t + j))],
        out_specs=tile,
        out_shape=jax.ShapeDtypeStruct((S, D), MXU_DTYPE),
        semantics=("parallel", "parallel"),
        operands=(y_attn, y_rnn, proj, proj, b_gate, b_gate),
        side=side,
    )


def _gate_bwd(dmix, y_attn, y_rnn, proj, b_gate, side=None):
    t = 512

    def body(dm_ref, ya_ref, yr_ref, ga_ref, gr_ref, ba_ref, br_ref,
             dya_ref, dyr_ref, dga_ref, dgr_ref, dba_ref, dbr_ref):
        @pl.when(pl.program_id(1) == 0)
        def _():
            dba_ref[...] = jnp.zeros_like(dba_ref)
            dbr_ref[...] = jnp.zeros_like(dbr_ref)

        dm = dm_ref[...]
        ga = _sigmoid(ga_ref[...] + ba_ref[...])
        gr = _sigmoid(gr_ref[...] + br_ref[...])
        dya_ref[...] = (dm * ga).astype(dya_ref.dtype)
        dyr_ref[...] = (dm * gr).astype(dyr_ref.dtype)
        dga = dm * ya_ref[...] * ga * (1.0 - ga)
        dgr = dm * yr_ref[...] * gr * (1.0 - gr)
        dga_ref[...] = dga.astype(dga_ref.dtype)
        dgr_ref[...] = dgr.astype(dgr_ref.dtype)
        dba_ref[...] += _colsum(dga)
        dbr_ref[...] += _colsum(dgr)

    tile = pl.BlockSpec((t, t), lambda j, i: (i, j))
    vec = pl.BlockSpec((1, t), lambda j, i: (0, j))
    return _call(
        body,
        name="gate_bwd",
        grid=(D // t, S // t),
        in_specs=[tile, tile, tile,
                  pl.BlockSpec((t, t), lambda j, i: (i, OFF_GA // t + j)),
                  pl.BlockSpec((t, t), lambda j, i: (i, OFF_GR // t + j)),
                  vec,
                  pl.BlockSpec((1, t), lambda j, i: (0, D // t + j))],
        out_specs=[tile, tile, tile, tile, vec, vec],
        out_shape=[jax.ShapeDtypeStruct((S, D), MXU_DTYPE)] * 4 + [jax.ShapeDtypeStruct((1, D), F32)] * 2,
        semantics=("parallel", "arbitrary"),
        operands=(dmix, y_attn, y_rnn, proj, proj, b_gate, b_gate),
        side=side,
    )


LN_TM = 256


def _ln_stats(pre):
    mu = jnp.mean(pre, axis=-1, keepdims=True)
    xc = pre - mu
    rstd = lax.rsqrt(jnp.mean(xc * xc, axis=-1, keepdims=True) + LN_EPS)
    return xc * rstd, rstd


def _ln_input_grad(dy, xhat, rstd, g):
    dyg = dy * g
    return rstd * (dyg - jnp.mean(dyg, axis=-1, keepdims=True)
                   - xhat * jnp.mean(dyg * xhat, axis=-1, keepdims=True))


def _ln_fwd(res, branch, g, b, side=None):
    def body(res_ref, br_ref, g_ref, b_ref, y_ref, yb_ref, xhat_ref, rstd_ref):
        xhat, rstd = _ln_stats(ALPHA * res_ref[...] + br_ref[...])
        y = xhat * g_ref[...] + b_ref[...]
        y_ref[...] = y
        yb_ref[...] = y.astype(yb_ref.dtype)
        xhat_ref[...] = xhat
        rstd_ref[...] = rstd

    tile = pl.BlockSpec((LN_TM, D), lambda i: (i, 0))
    vec = pl.BlockSpec((1, D), lambda i: (0, 0))
    return _call(
        body,
        name="ln_fwd",
        grid=(S // LN_TM,),
        in_specs=[tile, tile, vec, vec],
        out_specs=[tile, tile, tile, pl.BlockSpec((LN_TM, 1), lambda i: (i, 0))],
        out_shape=[jax.ShapeDtypeStruct((S, D), F32), jax.ShapeDtypeStruct((S, D), MXU_DTYPE),
                   jax.ShapeDtypeStruct((S, D), F32), jax.ShapeDtypeStruct((S, 1), F32)],
        semantics=("parallel",),
        operands=(res, branch, g, b),
        side=side,
    )


def _ln_bwd(dy_a, dy_b, xhat, rstd, g, side=None):
    def body(da_ref, db_in_ref, xhat_ref, rstd_ref, g_ref, dp_ref, dpb_ref, dg_ref, db_ref):
        @pl.when(pl.program_id(0) == 0)
        def _():
            dg_ref[...] = jnp.zeros_like(dg_ref)
            db_ref[...] = jnp.zeros_like(db_ref)

        dy = da_ref[...] + ALPHA * db_in_ref[...]
        xhat = xhat_ref[...]
        dp = _ln_input_grad(dy, xhat, rstd_ref[...], g_ref[...])
        dp_ref[...] = dp
        dpb_ref[...] = dp.astype(dpb_ref.dtype)
        dg_ref[...] += _colsum(dy * xhat)
        db_ref[...] += _colsum(dy)

    tile = pl.BlockSpec((LN_TM, D), lambda i: (i, 0))
    vec = pl.BlockSpec((1, D), lambda i: (0, 0))
    return _call(
        body,
        name="ln_bwd",
        grid=(S // LN_TM,),
        in_specs=[tile, tile, tile, pl.BlockSpec((LN_TM, 1), lambda i: (i, 0)), vec],
        out_specs=[tile, tile, vec, vec],
        out_shape=[jax.ShapeDtypeStruct((S, D), F32), jax.ShapeDtypeStruct((S, D), MXU_DTYPE),
                   jax.ShapeDtypeStruct((1, D), F32), jax.ShapeDtypeStruct((1, D), F32)],
        semantics=("arbitrary",),
        operands=(dy_a, dy_b, xhat, rstd, g),
        side=side,
    )


def _ln_loss_bwd(res, branch, g, b, target, side=None):
    def body(res_ref, br_ref, g_ref, b_ref, t_ref, loss_ref, dp_ref, dpb_ref, dg_ref, db_ref):
        @pl.when(pl.program_id(0) == 0)
        def _():
            loss_ref[...] = jnp.zeros_like(loss_ref)
            dg_ref[...] = jnp.zeros_like(dg_ref)
            db_ref[...] = jnp.zeros_like(db_ref)

        xhat, rstd = _ln_stats(ALPHA * res_ref[...] + br_ref[...])
        gv = g_ref[...]
        err = xhat * gv + b_ref[...] - t_ref[...]
        loss_ref[...] += (0.5 / D) * jnp.sum(_colsum(err * err), axis=1, keepdims=True)
        dy = err * (1.0 / D)
        dp = _ln_input_grad(dy, xhat, rstd, gv)
        dp_ref[...] = dp
        dpb_ref[...] = dp.astype(dpb_ref.dtype)
        dg_ref[...] += _colsum(dy * xhat)
        db_ref[...] += _colsum(dy)

    tile = pl.BlockSpec((LN_TM, D), lambda i: (i, 0))
    vec = pl.BlockSpec((1, D), lambda i: (0, 0))
    return _call(
        body,
        name="ln_loss_bwd",
        grid=(S // LN_TM,),
        in_specs=[tile, tile, vec, vec, tile],
        out_specs=[pl.BlockSpec((1, 1), lambda i: (0, 0)), tile, tile, vec, vec],
        out_shape=[jax.ShapeDtypeStruct((1, 1), F32), jax.ShapeDtypeStruct((S, D), F32),
                   jax.ShapeDtypeStruct((S, D), MXU_DTYPE),
                   jax.ShapeDtypeStruct((1, D), F32), jax.ShapeDtypeStruct((1, D), F32)],
        semantics=("arbitrary",),
        operands=(res, branch, g, b, target),
        side=side,
    )


FFN_TC = 256


def _ffn_act_fwd(up, gpre, w, b, side=None):
    tc = FFN_TC

    def body(up_ref, x_ref, w_ref, b_ref, o_ref, xpad_ref):
        xpad_ref[pl.ds(0, PAD), :] = jnp.zeros((PAD, tc), F32)
        xpad_ref[pl.ds(PAD, S), :] = x_ref[...]
        wv = w_ref[...]
        bv = b_ref[...]

        def step(ci, carry):
            r0 = pl.multiple_of(ci * CHUNK, CHUNK)
            taps = _past_taps(xpad_ref, r0, FFN_CONV_W)
            gate = bv + taps[0] * wv[0:1, :] + taps[1] * wv[1:2, :] + taps[2] * wv[2:3, :]
            o_ref[pl.ds(r0, CHUNK), :] = (_gelu(gate)[0] * up_ref[pl.ds(r0, CHUNK), :]).astype(o_ref.dtype)
            return carry

        lax.fori_loop(0, S // CHUNK, step, 0)

    col = pl.BlockSpec((S, tc), lambda j: (0, j))
    return _call(
        body,
        name="ffn_act_fwd",
        grid=(D_FF // tc,),
        in_specs=[col, col, pl.BlockSpec((FFN_CONV_W, tc), lambda j: (0, j)), pl.BlockSpec((1, tc), lambda j: (0, j))],
        out_specs=col,
        out_shape=jax.ShapeDtypeStruct((S, D_FF), MXU_DTYPE),
        scratch_shapes=[pltpu.VMEM((S + PAD, tc), F32)],
        semantics=("parallel",),
        operands=(up, gpre, w, b),
        side=side,
    )


def _ffn_act_bwd(dfin, up, gpre, w, b, side=None):
    tc = FFN_TC
    width = FFN_CONV_W

    def body(df_ref, up_ref, x_ref, w_ref, b_ref, dup_ref, dx_ref, dw_ref, db_ref, xpad_ref, dpad_ref):
        xpad_ref[pl.ds(0, PAD), :] = jnp.zeros((PAD, tc), F32)
        xpad_ref[pl.ds(PAD, S), :] = x_ref[...]
        dpad_ref[pl.ds(S, PAD), :] = jnp.zeros((PAD, tc), F32)
        wv = w_ref[...]
        bv = b_ref[...]

        def gate_grad(ci, acc):
            r0 = pl.multiple_of(ci * CHUNK, CHUNK)
            taps = _past_taps(xpad_ref, r0, width)
            gate = bv + taps[0] * wv[0:1, :] + taps[1] * wv[1:2, :] + taps[2] * wv[2:3, :]
            ge, dge = _gelu(gate)
            df = df_ref[pl.ds(r0, CHUNK), :]
            dup_ref[pl.ds(r0, CHUNK), :] = (df * ge).astype(dup_ref.dtype)
            d = df * up_ref[pl.ds(r0, CHUNK), :] * dge
            dpad_ref[pl.ds(r0, CHUNK), :] = d
            return tuple(acc[k] + _colsum(taps[k] * d) for k in range(width)) + (acc[width] + _colsum(d),)

        zero = jnp.zeros((1, tc), F32)
        acc = lax.fori_loop(0, S // CHUNK, gate_grad, (zero,) * (width + 1))
        for k in range(width):
            dw_ref[k:k + 1, :] = acc[k]
        db_ref[...] = acc[width]

        def input_grad(ci, carry):
            r0 = pl.multiple_of(ci * CHUNK, CHUNK)
            ahead = _future_taps(dpad_ref, r0, width)
            dx = ahead[0] * wv[2:3, :] + ahead[1] * wv[1:2, :] + ahead[2] * wv[0:1, :]
            dx_ref[pl.ds(r0, CHUNK), :] = dx.astype(dx_ref.dtype)
            return carry

        lax.fori_loop(0, S // CHUNK, input_grad, 0)

    col = pl.BlockSpec((S, tc), lambda j: (0, j))
    w_spec = pl.BlockSpec((width, tc), lambda j: (0, j))
    vec = pl.BlockSpec((1, tc), lambda j: (0, j))
    return _call(
        body,
        name="ffn_act_bwd",
        grid=(D_FF // tc,),
        in_specs=[col, col, col, w_spec, vec],
        out_specs=[col, col, w_spec, vec],
        out_shape=[jax.ShapeDtypeStruct((S, D_FF), MXU_DTYPE)] * 2
        + [jax.ShapeDtypeStruct((width, D_FF), F32), jax.ShapeDtypeStruct((1, D_FF), F32)],
        scratch_shapes=[pltpu.VMEM((S + PAD, tc), F32), pltpu.VMEM((S + PAD, tc), F32)],
        semantics=("parallel",),
        operands=(dfin, up, gpre, w, b),
        side=side,
    )


def _adamw_update(w, g, m, v):
    m = ADAM_B1 * m + (1.0 - ADAM_B1) * g
    v = ADAM_B2 * v + (1.0 - ADAM_B2) * (g * g)
    m_hat = m / (1.0 - ADAM_B1 ** ADAM_STEP)
    v_hat = v / (1.0 - ADAM_B2 ** ADAM_STEP)
    delta = -ADAM_LR * (m_hat / (jnp.sqrt(v_hat) + ADAM_EPS) + ADAM_WD * w)
    return delta, m, v


def _add_pairs(send, pair, far_index, *, name):
    _, r_dim, c_dim = send.shape
    tr = r_dim // 4

    def body(far_ref, mine_ref, theirs_ref, o_ref):
        o_ref[...] = (mine_ref[...].astype(F32) + theirs_ref[...].astype(F32)).astype(o_ref.dtype)

    return pl.pallas_call(
        body,
        name=name,
        grid_spec=pltpu.PrefetchScalarGridSpec(
            num_scalar_prefetch=1,
            grid=(3, r_dim // tr),
            in_specs=[pl.BlockSpec((None, tr, c_dim), lambda j, i, far: (far[j], i, 0)),
                      pl.BlockSpec((None, tr, c_dim), lambda j, i, far: (1 + j, i, 0))],
            out_specs=pl.BlockSpec((None, tr, c_dim), lambda j, i, far: (j, i, 0)),
        ),
        out_shape=jax.ShapeDtypeStruct((3, r_dim, c_dim), BF16),
        compiler_params=_cparams("parallel", "parallel"),
    )(far_index, send, pair)


def _reduce_adamw(w, m, v, g_own, pair, far, me, *, tr, name):
    r_dim, c_dim = w.shape

    def body(me_ref, w_ref, m_ref, v_ref, g_ref, pair_ref, far_ref, grad_ref, delta_ref, nm_ref, nv_ref):
        g = g_ref[...] + pair_ref[...].astype(F32)
        for j in range(3):
            g = g + far_ref[j].astype(F32)
        delta, nm, nv = _adamw_update(w_ref[...], g, m_ref[...], v_ref[...])
        grad_ref[...] = g
        delta_ref[...] = delta
        nm_ref[...] = nm
        nv_ref[...] = nv

    tile = pl.BlockSpec((tr, c_dim), lambda i, me: (i, 0))
    if g_own.ndim == 3:
        own_spec = pl.BlockSpec((None, tr, c_dim), lambda i, me: (me[0], i, 0))
    else:
        own_spec = tile
    return pl.pallas_call(
        body,
        name=name,
        grid_spec=pltpu.PrefetchScalarGridSpec(
            num_scalar_prefetch=1,
            grid=(r_dim // tr,),
            in_specs=[tile, tile, tile, own_spec, pl.BlockSpec((None, tr, c_dim), lambda i, me: (0, i, 0)),
                      pl.BlockSpec((3, tr, c_dim), lambda i, me: (0, i, 0))],
            out_specs=[tile] * 4,
        ),
        out_shape=[jax.ShapeDtypeStruct((r_dim, c_dim), F32)] * 4,
        compiler_params=_cparams("parallel"),
    )(me, w, m, v, g_own, pair, far)


def _adamw_many(ws, ms, vs, gs):
    n = len(ws)

    def body(*refs):
        for i in range(n):
            delta, nm, nv = _adamw_update(refs[i][...], refs[3 * n + i][...], refs[n + i][...], refs[2 * n + i][...])
            refs[4 * n + i][...] = delta
            refs[5 * n + i][...] = nm
            refs[6 * n + i][...] = nv

    vmem = pl.BlockSpec(memory_space=pltpu.VMEM)
    res = pl.pallas_call(
        body,
        name="adamw_small",
        in_specs=[vmem] * (4 * n),
        out_specs=[vmem] * (3 * n),
        out_shape=[jax.ShapeDtypeStruct(w.shape, F32) for w in ws] * 3,
        compiler_params=pltpu.CompilerParams(vmem_limit_bytes=VMEM_LIMIT),
    )(*ws, *ms, *vs, *gs)
    return res[:n], res[n:2 * n], res[2 * n:]


def _adamw_blocks(w, m, v, g, *, name, side=None):
    per = 2

    def body(w_ref, m_ref, v_ref, g_ref, delta_ref, nm_ref, nv_ref):
        delta, nm, nv = _adamw_update(w_ref[...], g_ref[...], m_ref[...], v_ref[...])
        delta_ref[...] = delta
        nm_ref[...] = nm
        nv_ref[...] = nv

    tile = pl.BlockSpec((1, per) + w.shape[2:], lambda i: (0, i, 0, 0))
    return _call(
        body,
        name=name,
        grid=(w.shape[1] // per,),
        in_specs=[tile] * 4,
        out_specs=[tile] * 3,
        out_shape=[jax.ShapeDtypeStruct(w.shape, F32)] * 3,
        semantics=("parallel",),
        operands=(w, m, v, g),
        side=side,
    )


def _coords():
    return lax.axis_index("x"), lax.axis_index("y"), lax.axis_index("c")


def _flip(coord, bit):
    return 1 - coord if bit else coord


def _relative(k):
    x, y, c = _coords()
    return _flip(x, k & 4), _flip(y, k & 2), _flip(c, k & 1)


def _index(pos):
    return 4 * pos[0] + 2 * pos[1] + pos[2]


FAR = (4, 2, 6)
AG_US_PER_MB = 38.0
RS_US_PER_MB = 46.0
MIN_RIDE_US = 30.0
MIN_GATHER_RIDE_US = 22.0
ROW_ALIGN = 32


def _chunks(items, cursor, us, us_per_mb, through=None):
    budget = float("inf") if us is None else us / us_per_mb * 2 ** 20
    names = list(items)
    if through is not None:
        names = names[:names.index(through) + 1]
    chunks = []
    for name in names:
        arr = items[name]
        r_dim, c_dim = arr.shape[-2:]
        row_bytes = c_dim * arr.dtype.itemsize
        while cursor[name] < r_dim and budget > 0:
            rows = r_dim - cursor[name]
            if r_dim > ROW_ALIGN and budget < rows * row_bytes:
                rows = min(rows, max(ROW_ALIGN, int(budget // row_bytes) // ROW_ALIGN * ROW_ALIGN))
            chunks.append((name, cursor[name], rows))
            cursor[name] += rows
            budget -= rows * row_bytes
    return chunks


class _Gather:
    def __init__(self, shards):
        self.shards = dict(shards)
        self.bufs = {n: None for n in self.shards}
        self.cursor = {n: 0 for n in self.shards}

    def take(self, us=None, through=None):
        if us is not None and us < MIN_GATHER_RIDE_US:
            return None
        chunks = _chunks(self.shards, self.cursor, us, AG_US_PER_MB, through)
        return _GatherSide(self, chunks) if chunks else None

    def get(self, name):
        chunks = _chunks(self.shards, self.cursor, None, AG_US_PER_MB, through=name)
        if chunks:
            _run_side(_GatherSide(self, chunks), "gather_" + name)
        return self.bufs[name]


class _GatherSide:
    SEMS = 8

    def __init__(self, owner, chunks):
        self.owner, self.chunks = owner, chunks
        self.names = list(dict.fromkeys(n for n, _, _ in chunks))
        old = [n for n in self.names if owner.bufs[n] is not None]
        self.operands = [owner.shards[n] for n in self.names] + [owner.bufs[n] for n in old]
        self.out_shape = [jax.ShapeDtypeStruct((N_DEV,) + owner.shards[n].shape, owner.shards[n].dtype)
                          for n in self.names]
        self.aliases = {len(self.names) + i: self.names.index(n) for i, n in enumerate(old)}
        self.sems = [pltpu.SemaphoreType.DMA((self.SEMS * len(chunks),)),
                     pltpu.SemaphoreType.DMA((self.SEMS * len(chunks),)), pltpu.SemaphoreType.DMA((len(chunks),))]

    def _halves(self, ci):
        _, r0, rows = self.chunks[ci]
        if rows % ROW_ALIGN:
            return None
        return (r0, rows // 2), (r0 + rows // 2, rows // 2)

    def _copy(self, ins, outs, sems, ci, s, block, to, rows=None, from_shard=False):
        name, r0, n = self.chunks[ci]
        if rows is not None:
            r0, n = rows
        w = self.names.index(name)
        slot = outs[w].at[_index(block), pl.ds(r0, n)]
        return pltpu.make_async_remote_copy(
            src_ref=ins[w].at[pl.ds(r0, n)] if from_shard else slot, dst_ref=slot,
            send_sem=sems[0].at[self.SEMS * ci + s], recv_sem=sems[1].at[self.SEMS * ci + s],
            device_id=to, device_id_type=MESH)

    def _own(self, ins, outs, sems, ci):
        name, r0, rows = self.chunks[ci]
        w = self.names.index(name)
        return pltpu.make_async_copy(ins[w].at[pl.ds(r0, rows)], outs[w].at[_index(_relative(0)), pl.ds(r0, rows)],
                                     sems[2].at[ci])

    def _pass(self, ins, outs, sems, ci, which):
        source, target = ((4, 2), (2, 4))[which]
        return self._copy(ins, outs, sems, ci, 3 + which, _relative(source), _relative(target),
                          rows=self._halves(ci)[which])

    def start(self, ins, outs, sems):
        me = _relative(0)
        for ci in range(len(self.chunks)):
            self._own(ins, outs, sems, ci).start()
        for ci in range(len(self.chunks)):
            self._copy(ins, outs, sems, ci, 1, me, _relative(4), from_shard=True).start()
            self._copy(ins, outs, sems, ci, 2, me, _relative(2), from_shard=True).start()
            if self._halves(ci) is None:
                self._copy(ins, outs, sems, ci, 3, me, _relative(6), from_shard=True).start()
        for ci in range(len(self.chunks)):
            self._copy(ins, outs, sems, ci, 0, me, _relative(1), from_shard=True).start()

    def mid(self, ins, outs, sems):
        me = _relative(0)
        cut = [ci for ci in range(len(self.chunks)) if self._halves(ci) is not None]
        for ci in cut:
            self._copy(ins, outs, sems, ci, 1, _relative(4), me).wait_recv()
            self._pass(ins, outs, sems, ci, 0).start()
        for ci in cut:
            self._copy(ins, outs, sems, ci, 2, _relative(2), me).wait_recv()
            self._pass(ins, outs, sems, ci, 1).start()

    def finish(self, ins, outs, sems):
        me, sibling = _relative(0), _relative(1)
        n = len(self.chunks)
        for ci in range(n):
            if self._halves(ci) is None:
                for s, k in ((1, 4), (2, 2), (3, 6)):
                    self._copy(ins, outs, sems, ci, s, _relative(k), me).wait_recv()
            else:
                h0, h1 = self._halves(ci)
                self._copy(ins, outs, sems, ci, 3, _relative(6), me, rows=h0).wait_recv()
                self._copy(ins, outs, sems, ci, 4, _relative(6), me, rows=h1).wait_recv()
            for j, k in enumerate(FAR):
                self._copy(ins, outs, sems, ci, 5 + j, _relative(k), sibling).start()
        for ci in range(n):
            self._copy(ins, outs, sems, ci, 0, sibling, me).wait_recv()
            for j, k in enumerate(FAR):
                self._copy(ins, outs, sems, ci, 5 + j, _relative(k | 1), me).wait_recv()
        for ci in range(n):
            self._copy(ins, outs, sems, ci, 0, me, sibling, from_shard=True).wait_send()
            self._copy(ins, outs, sems, ci, 1, me, _relative(4), from_shard=True).wait_send()
            self._copy(ins, outs, sems, ci, 2, me, _relative(2), from_shard=True).wait_send()
            if self._halves(ci) is None:
                self._copy(ins, outs, sems, ci, 3, me, _relative(6), from_shard=True).wait_send()
            else:
                self._pass(ins, outs, sems, ci, 0).wait_send()
                self._pass(ins, outs, sems, ci, 1).wait_send()
            for j, k in enumerate(FAR):
                self._copy(ins, outs, sems, ci, 5 + j, _relative(k), sibling).wait_send()
            self._own(ins, outs, sems, ci).wait()

    def done(self, results):
        for n, buf in zip(self.names, results):
            self.owner.bufs[n] = buf


class _Scatter:
    def __init__(self, me, far_index):
        self.me, self.far_index = me, far_index
        self.sends, self.owns, self.pairs, self.sums, self.fars = {}, {}, {}, {}, {}
        self.pair_cursor, self.far_cursor = {}, {}

    def add(self, name, send, own):
        self.sends[name] = send
        self.owns[name] = own
        self.pairs[name] = self.fars[name] = None
        self.pair_cursor[name] = 0

    def _rows(self, name):
        return self.sends[name].shape[1]

    def _add_ready_pairs(self):
        for name in self.sends:
            if name not in self.sums and self.pair_cursor[name] == self._rows(name):
                self.sums[name] = _add_pairs(self.sends[name], self.pairs[name], self.far_index, name="pair_" + name)
                self.far_cursor[name] = 0

    def _side(self, us, through=None):
        self._add_ready_pairs()
        names = list(self.sends)
        if through is not None:
            names = names[:names.index(through) + 1]
        pair_chunks = [(n, self.pair_cursor[n], self._rows(n) - self.pair_cursor[n]) for n in names
                       if self.pair_cursor[n] < self._rows(n)]
        for n, _, _ in pair_chunks:
            self.pair_cursor[n] = self._rows(n)
        far_chunks = _chunks(self.sums, self.far_cursor, us, RS_US_PER_MB,
                             through if through in self.sums else None) if self.sums else []
        return _ScatterSide(self, pair_chunks, far_chunks) if pair_chunks or far_chunks else None

    def add_blocks(self, name, blocks32, blocks16):
        self.add(name, blocks16, blocks32)

    def add_cols(self, name, full32, full16):
        width = full32.shape[1] // N_DEV
        self.add(name, _blocks(full16, "cols"), lax.dynamic_slice_in_dim(full32, self.me * width, width, axis=1))

    def take(self, us):
        return self._side(us) if us >= MIN_RIDE_US else None

    def flush_pairs(self, name):
        side = self._side(0.0)
        if side is not None:
            _run_side(side, name)
        self._add_ready_pairs()

    def get(self, name):
        step = 0
        while name not in self.sums or self.far_cursor[name] < self._rows(name):
            _run_side(self._side(None, through=name), "scatter_%s_%d" % (name, step))
            step += 1
        return self.owns[name], self.pairs[name], self.fars[name]


class _ScatterSide:
    TO_SIBLING = (1, 5, 3, 7)

    def __init__(self, owner, pair_chunks, far_chunks):
        self.owner, self.pair_chunks, self.far_chunks = owner, pair_chunks, far_chunks
        self.pair_names = list(dict.fromkeys(n for n, _, _ in pair_chunks))
        self.far_names = list(dict.fromkeys(n for n, _, _ in far_chunks))
        ins = [(owner.sends[n], owner.pairs[n], (4,)) for n in self.pair_names]
        ins += [(owner.sums[n], owner.fars[n], (3,)) for n in self.far_names]
        old = [i for i, (_, buf, _) in enumerate(ins) if buf is not None]
        self.operands = [src for src, _, _ in ins] + [ins[i][1] for i in old]
        self.out_shape = [jax.ShapeDtypeStruct(slots + src.shape[1:], BF16) for src, _, slots in ins]
        self.aliases = {len(ins) + j: i for j, i in enumerate(old)}
        n_pair, n_far = 4 * len(pair_chunks), 3 * len(far_chunks)
        self.sems = [pltpu.SemaphoreType.DMA((max(n_pair, 1),)), pltpu.SemaphoreType.DMA((max(n_pair, 1),)),
                     pltpu.SemaphoreType.DMA((max(n_far, 1),)), pltpu.SemaphoreType.DMA((max(n_far, 1),))]

    def _copies(self, ins, outs, sems):
        copies = []
        for ci, (name, r0, rows) in enumerate(self.pair_chunks):
            w = self.pair_names.index(name)
            for j, k in enumerate(self.TO_SIBLING):
                copies.append(pltpu.make_async_remote_copy(
                    src_ref=ins[w].at[_index(_relative(k)), pl.ds(r0, rows)], dst_ref=outs[w].at[j, pl.ds(r0, rows)],
                    send_sem=sems[0].at[4 * ci + j], recv_sem=sems[1].at[4 * ci + j],
                    device_id=_relative(1), device_id_type=MESH))
        for ci, (name, r0, rows) in enumerate(self.far_chunks):
            w = len(self.pair_names) + self.far_names.index(name)
            for j, k in enumerate(FAR):
                copies.append(pltpu.make_async_remote_copy(
                    src_ref=ins[w].at[j, pl.ds(r0, rows)], dst_ref=outs[w].at[j, pl.ds(r0, rows)],
                    send_sem=sems[2].at[3 * ci + j], recv_sem=sems[3].at[3 * ci + j],
                    device_id=_relative(k), device_id_type=MESH))
        return copies

    def start(self, ins, outs, sems):
        for cp in self._copies(ins, outs, sems):
            cp.start()

    def mid(self, ins, outs, sems):
        pass

    def finish(self, ins, outs, sems):
        for cp in self._copies(ins, outs, sems):
            cp.wait()

    def done(self, results):
        for n, buf in zip(self.pair_names, results):
            self.owner.pairs[n] = buf
        for n, buf in zip(self.far_names, results[len(self.pair_names):]):
            self.owner.fars[n] = buf


class _Joined:
    def __init__(self, sides):
        self.sides = sides
        self.operands, self.out_shape, self.sems, self.aliases, self.spans = [], [], [], {}, []
        for s in sides:
            i0, o0, s0 = len(self.operands), len(self.out_shape), len(self.sems)
            self.operands += list(s.operands)
            self.out_shape += list(s.out_shape)
            self.sems += list(s.sems)
            self.aliases.update({i0 + i: o0 + o for i, o in s.aliases.items()})
            self.spans.append((slice(i0, len(self.operands)), slice(o0, len(self.out_shape)),
                               slice(s0, len(self.sems))))

    def start(self, ins, outs, sems):
        for s, (i, o, m) in zip(self.sides, self.spans):
            s.start(ins[i], outs[o], sems[m])

    def mid(self, ins, outs, sems):
        for s, (i, o, m) in zip(self.sides, self.spans):
            s.mid(ins[i], outs[o], sems[m])

    def finish(self, ins, outs, sems):
        for s, (i, o, m) in zip(self.sides, self.spans):
            s.finish(ins[i], outs[o], sems[m])

    def done(self, results):
        for s, (_, o, _) in zip(self.sides, self.spans):
            s.done(results[o])


def _join(*sides):
    sides = [s for s in sides if s is not None]
    if len(sides) <= 1:
        return sides[0] if sides else None
    return _Joined(sides)


def _pack_rows(vecs):
    rows = -(-sum(v.shape[0] for v in vecs) // 8) * 8
    width = max(v.shape[1] for v in vecs)

    def body(*refs):
        out = refs[-1]
        out[...] = jnp.zeros_like(out)
        r0 = 0
        for v in refs[:-1]:
            out[r0:r0 + v.shape[0], 0:v.shape[1]] = v[...]
            r0 += v.shape[0]

    vmem = pl.BlockSpec(memory_space=pltpu.VMEM)
    return pl.pallas_call(body, name="pack_small", in_specs=[vmem] * len(vecs), out_specs=vmem,
                          out_shape=jax.ShapeDtypeStruct((rows, width), F32))(*vecs)


def _sum_rows(inbox, shapes):
    def body(inbox_ref, *refs):
        outs, total = refs[:-1], refs[-1]
        acc = inbox_ref[0]
        for d in range(1, N_DEV):
            acc = acc + inbox_ref[d]
        total[...] = acc
        r0 = 0
        for o in outs:
            o[...] = total[r0:r0 + o.shape[0], 0:o.shape[1]]
            r0 += o.shape[0]

    vmem = pl.BlockSpec(memory_space=pltpu.VMEM)
    return pl.pallas_call(body, name="sum_small", in_specs=[vmem], out_specs=[vmem] * len(shapes),
                          out_shape=[jax.ShapeDtypeStruct(s, F32) for s in shapes],
                          scratch_shapes=[pltpu.VMEM(inbox.shape[1:], F32)])(inbox)


class _ShareRows:
    def __init__(self, mine):
        self.operands, self.aliases = [mine], {}
        self.out_shape = [jax.ShapeDtypeStruct((N_DEV,) + mine.shape, F32)]
        self.sems = [pltpu.SemaphoreType.DMA((N_DEV - 1,)), pltpu.SemaphoreType.DMA((N_DEV - 1,)),
                     pltpu.SemaphoreType.DMA(())]

    def _copy(self, ins, outs, sems, k, sender):
        return pltpu.make_async_remote_copy(
            src_ref=ins[0], dst_ref=outs[0].at[_index(sender)], send_sem=sems[0].at[k - 1], recv_sem=sems[1].at[k - 1],
            device_id=_relative(k), device_id_type=MESH)

    def _own(self, ins, outs, sems):
        return pltpu.make_async_copy(ins[0], outs[0].at[_index(_relative(0))], sems[2])

    def start(self, ins, outs, sems):
        self._own(ins, outs, sems).start()
        for k in range(1, N_DEV):
            self._copy(ins, outs, sems, k, _relative(0)).start()

    def mid(self, ins, outs, sems):
        pass

    def finish(self, ins, outs, sems):
        for k in range(1, N_DEV):
            self._copy(ins, outs, sems, k, _relative(k)).wait_recv()
            self._copy(ins, outs, sems, k, _relative(0)).wait_send()
        self._own(ins, outs, sems).wait()

    def done(self, results):
        self.inbox = results[0]


class _PartsToOwners:
    def __init__(self, mats):
        self.n, self.per = len(mats), mats[0].shape[0] // N_DEV
        self.operands, self.aliases = list(mats), {}
        self.out_shape = [jax.ShapeDtypeStruct((N_DEV, self.n, self.per) + mats[0].shape[1:], F32)]
        self.sems = [pltpu.SemaphoreType.DMA((self.n * (N_DEV - 1),))] * 2

    def _copies(self, ins, outs, sems):
        return [pltpu.make_async_remote_copy(
            src_ref=ins[j].at[pl.ds(self.per * _index(_relative(k)), self.per)], dst_ref=outs[0].at[k, j],
            send_sem=sems[0].at[self.n * (k - 1) + j], recv_sem=sems[1].at[self.n * (k - 1) + j],
            device_id=_relative(k), device_id_type=MESH) for k in range(1, N_DEV) for j in range(self.n)]

    def start(self, ins, outs, sems):
        for cp in self._copies(ins, outs, sems):
            cp.start()

    def mid(self, ins, outs, sems):
        pass

    def finish(self, ins, outs, sems):
        for cp in self._copies(ins, outs, sems):
            cp.wait()

    def done(self, results):
        self.stage = results[0]


def _sum_parts(mats, stage):
    n, per = len(mats), mats[0].shape[0] // N_DEV

    def body(*refs):
        stage_ref, out = refs[n], refs[n + 1]
        me = _index(_relative(0))
        for j in range(n):
            acc = refs[j][pl.ds(per * me, per)]
            for k in range(1, N_DEV):
                acc = acc + stage_ref[k, j]
            out[j] = acc

    vmem = pl.BlockSpec(memory_space=pltpu.VMEM)
    return pl.pallas_call(body, name="sum_small_parts", in_specs=[vmem] * (n + 1), out_specs=vmem,
                          out_shape=jax.ShapeDtypeStruct((n, per) + mats[0].shape[1:], F32),
                          compiler_params=pltpu.CompilerParams(vmem_limit_bytes=VMEM_LIMIT))(*mats, stage)


class _PartsToAll:
    def __init__(self, parts, rows):
        self.n, self.per = parts.shape[0], parts.shape[1]
        self.operands, self.aliases = [parts], {}
        self.out_shape = [jax.ShapeDtypeStruct((rows,) + parts.shape[2:], F32)] * self.n
        self.sems = [pltpu.SemaphoreType.DMA((self.n * (N_DEV - 1),))] * 2 + [pltpu.SemaphoreType.DMA((self.n,))]

    def _rows(self, ref, pos):
        return ref.at[pl.ds(self.per * _index(pos), self.per)]

    def _copy(self, ins, outs, sems, k, j, owner):
        return pltpu.make_async_remote_copy(
            src_ref=ins[0].at[j], dst_ref=self._rows(outs[j], owner),
            send_sem=sems[0].at[self.n * (k - 1) + j], recv_sem=sems[1].at[self.n * (k - 1) + j],
            device_id=_relative(k), device_id_type=MESH)

    def _own(self, ins, outs, sems, j):
        return pltpu.make_async_copy(ins[0].at[j], self._rows(outs[j], _relative(0)), sems[2].at[j])

    def start(self, ins, outs, sems):
        for j in range(self.n):
            self._own(ins, outs, sems, j).start()
            for k in range(1, N_DEV):
                self._copy(ins, outs, sems, k, j, _relative(0)).start()

    def mid(self, ins, outs, sems):
        pass

    def finish(self, ins, outs, sems):
        for j in range(self.n):
            for k in range(1, N_DEV):
                self._copy(ins, outs, sems, k, j, _relative(k)).wait_recv()
                self._copy(ins, outs, sems, k, j, _relative(0)).wait_send()
            self._own(ins, outs, sems, j).wait()

    def done(self, results):
        self.totals = list(results)


class _SmallSync:
    def __init__(self, vec_names, mat_names):
        self.vec_names, self.mat_names = vec_names, mat_names

    def begin(self, loss, grads):
        vecs = [loss] + [grads[n] for n in self.vec_names]
        self.shapes = [v.shape for v in vecs]
        self.mats = [_diag_blocks(grads[n]) for n in self.mat_names]
        self.share = _ShareRows(_pack_rows(vecs))
        self.to_owners = _PartsToOwners(self.mats)
        return _join(self.share, self.to_owners)

    def middle(self):
        self.sums = _sum_rows(self.share.inbox, self.shapes)
        self.to_all = _PartsToAll(_sum_parts(self.mats, self.to_owners.stage), self.mats[0].shape[0])
        return self.to_all

    def end(self):
        return self.sums[0], dict(zip(self.vec_names, self.sums[1:])), dict(zip(self.mat_names, self.to_all.totals))


def _block_diag(w):
    groups = []
    for g in range(N_RNN_GROUPS):
        placed = [jnp.pad(w[4 * g + b], ((RNN_BLOCK_W * b, RNN_BLOCK_W * (3 - b)),) * 2) for b in range(4)]
        groups.append(placed[0] + placed[1] + placed[2] + placed[3])
    return jnp.stack(groups)


def _diag_blocks(wg):
    blocks = []
    for n in range(4 * N_RNN_GROUPS):
        g, at = n // 4, RNN_BLOCK_W * (n % 4)
        blocks.append(wg[g, at:at + RNN_BLOCK_W, at:at + RNN_BLOCK_W])
    return jnp.stack(blocks)


def _heads_major(t, n_heads):
    return t.reshape(S, n_heads, HEAD_DIM).transpose(1, 0, 2)


def _heads_minor(t):
    return t.transpose(1, 0, 2).reshape(S, t.shape[0] * HEAD_DIM)


def _natural(gathered, how):
    n, r, c = gathered.shape
    if how == "rows":
        return gathered.reshape(n * r, c)
    return gathered.transpose(1, 0, 2).reshape(r, n * c)


def _blocks(full, how):
    if how == "rows":
        return full.reshape(N_DEV, full.shape[0] // N_DEV, full.shape[1])
    return full.reshape(full.shape[0], N_DEV, full.shape[1] // N_DEV).transpose(1, 0, 2)


def _forward_backward(x2, target, small, gather, scatter, sync):
    xb = x2.astype(MXU_DTYPE)
    w_in = _natural(gather.get("w_in"), "cols")
    proj, projb = _mm(xb, w_in, tm=1024, tn=512, tk=D, out_dtype=(F32, MXU_DTYPE), name="proj", side=gather.take(110))

    qt = projb[:, :OFF_K].T.reshape(N_KV, GROUP, HEAD_DIM, S)
    k2, v2 = projb[:, OFF_K:OFF_V], projb[:, OFF_V:OFF_RX]
    kp = jnp.pad(_heads_major(k2, N_KV), ((0, 0), (BLOCK, 0), (0, 0)))
    vp = jnp.pad(_heads_major(v2, N_KV), ((0, 0), (BLOCK, 0), (0, 0)))
    kt = jnp.pad(k2.T.reshape(N_KV, HEAD_DIM, S), ((0, 0), (0, 0), (BLOCK, 0)))
    vt = jnp.pad(v2.T.reshape(N_KV, HEAD_DIM, S), ((0, 0), (0, 0), (BLOCK, 0)))
    sink_row = jnp.repeat(small["attn_sinks"].reshape(N_KV, 1, GROUP), BLOCK, axis=2)
    ot = _attn_fwd(qt, kp, vt, sink_row, side=gather.take(36)).reshape(D, S)

    rconv_w = _natural(gather.get("rnn_conv_w"), "cols")
    rxc = _conv_fwd(proj, OFF_RX, rconv_w, small["rnn_conv_b"], tc=512, name="rnn_conv_fwd", side=gather.take(18))
    r, i = _lru_gates(rxc, small["lru_wa"], small["lru_wi"], small["lru_ba"], small["lru_bi"], side=gather.take(33))
    h, yrin = _lru_scan_fwd(r, i, rxc, proj, small["lru_lambda"], side=gather.take(53))

    w_ap = _natural(gather.get("w_attn_proj"), "rows")
    w_rp = _natural(gather.get("w_rnn_proj"), "rows")
    y_attn = _mm(ot, w_ap, ta=True, tm=1024, tn=1024, tk=D, name="attn_proj", side=gather.take(22))
    y_rnn = _mm(yrin, w_rp, tm=1024, tn=1024, tk=D_RNN, name="rnn_proj", side=gather.take(27))
    mixin = _gate_fwd(y_attn, y_rnn, proj, small["b_gate"], side=gather.take(25))
    w_out = _natural(gather.get("w_out"), "rows")
    mix = _mm(mixin, w_out, tm=1024, tn=1024, tk=D, name="mix_out", side=gather.take(22))
    x1, x1b, xhat1, rstd1 = _ln_fwd(x2, mix, small["ln1_g"], small["ln1_b"], side=gather.take(23))

    w_up = gather.get("ffn_w_up")
    up = _mm(x1b, w_up, tm=1024, tn=768, tk=D, b_block=768, name="ffn_up", side=gather.take(58))
    w_gate = gather.get("ffn_w_gate")
    gpre = _mm(x1b, w_gate, tm=1024, tn=768, tk=D, b_block=768, name="ffn_gate", side=gather.take(58))
    fconv_w = _natural(gather.get("ffn_conv_w"), "cols")
    fin = _ffn_act_fwd(up, gpre, fconv_w, small["ffn_conv_b"], side=gather.take())
    w_down = _natural(gather.get("ffn_w_down"), "rows")
    f = _mm(fin, w_down, tm=1024, tn=1024, tk=2048, name="ffn_down")
    loss, dpre2, dpre2b, d_ln2_g, d_ln2_b = _ln_loss_bwd(x1, f, small["ln2_g"], small["ln2_b"], target)

    grads = {"ln2_g": d_ln2_g, "ln2_b": d_ln2_b}
    both = (F32, BF16)
    g32, g16 = _mm(fin, dpre2b, ta=True, tm=1024, tn=1024, tk=S, out_dtype=both, name="d_ffn_w_down")
    scatter.add_blocks("ffn_w_down", _blocks(g32, "rows"), _blocks(g16, "rows"))
    dfin = _mm(dpre2b, w_down, tb=True, tm=1024, tn=1024, tk=D, name="d_fin", side=scatter.take(57))
    dup, dgpre, grads["ffn_conv_w"], grads["ffn_conv_b"] = _ffn_act_bwd(
        dfin, up, gpre, fconv_w, small["ffn_conv_b"], side=scatter.take(85))
    g32, g16 = _mm(x1b, dup, ta=True, tm=1024, tn=768, tk=S, out_dtype=both, out_block=768, name="d_ffn_w_up",
                   side=scatter.take(57))
    scatter.add_blocks("ffn_w_up", g32, g16)
    g32, g16 = _mm(x1b, dgpre, ta=True, tm=1024, tn=768, tk=S, out_dtype=both, out_block=768, name="d_ffn_w_gate",
                   side=scatter.take(56))
    scatter.add_blocks("ffn_w_gate", g32, g16)
    dx1 = _mm(dup, w_up, tb=True, tm=1024, tn=1024, tk=768, b_block=768, name="d_x1_up", side=scatter.take(68))
    dx1 = _mm(dgpre, w_gate, tb=True, tm=1024, tn=1024, tk=768, b_block=768, add=dx1, name="d_x1_gate",
              side=scatter.take(70))
    dpre1, dpre1b, grads["ln1_g"], grads["ln1_b"] = _ln_bwd(dx1, dpre2, xhat1, rstd1, small["ln1_g"],
                                                            side=scatter.take(24))

    g32, g16 = _mm(mixin, dpre1b, ta=True, tm=1024, tn=1024, tk=S, out_dtype=both, name="d_w_out",
                   side=scatter.take(26))
    scatter.add_blocks("w_out", _blocks(g32, "rows"), _blocks(g16, "rows"))
    dmix = _mm(dpre1b, w_out, tb=True, tm=1024, tn=1024, tk=D, name="d_mixin", side=scatter.take(22))
    dya, dyr, dgl_a, dgl_r, db_a, db_r = _gate_bwd(dmix, y_attn, y_rnn, proj, small["b_gate"], side=scatter.take(36))
    grads["b_gate"] = jnp.concatenate([db_a, db_r], axis=1)
    g32, g16 = _mm(ot, dya, tm=1024, tn=1024, tk=S, out_dtype=both, name="d_w_attn_proj", side=scatter.take(38))
    scatter.add_blocks("w_attn_proj", _blocks(g32, "rows"), _blocks(g16, "rows"))
    g32, g16 = _mm(yrin, dyr, ta=True, tm=1280, tn=1024, tk=S, out_dtype=both, name="d_w_rnn_proj",
                   side=scatter.take(27))
    scatter.add_blocks("w_rnn_proj", _blocks(g32, "rows"), _blocks(g16, "rows"))
    dot_ = _mm(w_ap, dya, tb=True, tm=1024, tn=1024, tk=D, out_dtype=MXU_DTYPE, name="d_o", side=scatter.take(22))
    dyrin = _mm(dyr, w_rp, tb=True, tm=1024, tn=1280, tk=D, name="d_yrin", side=scatter.take(27))

    dry, dzr, dzi, drxc_in, grads["lru_ba"], grads["lru_bi"], grads["lru_lambda"] = _lru_scan_bwd(
        dyrin, proj, h, r, i, rxc, small["lru_lambda"], side=scatter.take(94))
    grads["lru_wa"], grads["lru_wi"] = _lru_gate_wgrad(rxc, dzr, dzi, side=scatter.take(22))
    drxc = _lru_gate_xgrad(dzr, dzi, small["lru_wa"], small["lru_wi"], drxc_in, side=scatter.take(33))
    drx, grads["rnn_conv_w"], grads["rnn_conv_b"] = _conv_bwd(drxc, proj, OFF_RX, rconv_w, tc=512,
                                                             name="rnn_conv_bwd", side=scatter.take(29))

    dqt, dk, dv, dsink = _attn_bwd(qt, kp, kt, vp, sink_row, dot_.reshape(N_KV, GROUP, HEAD_DIM, S),
                                   side=scatter.take(65))
    grads["attn_sinks"] = dsink.reshape(1, N_KV * GROUP)
    dproj = jnp.concatenate([
        dqt.reshape(D, S).T,
        _heads_minor(dk[:, BLOCK:, :]).astype(MXU_DTYPE),
        _heads_minor(dv[:, BLOCK:, :]).astype(MXU_DTYPE),
        drx, dry, dgl_a, dgl_r], axis=1)
    g32, g16 = _mm(xb, dproj, ta=True, tm=1024, tn=512, tk=S, out_dtype=both, name="d_w_in",
                   side=_join(scatter.take(110), sync.begin(loss, grads)))
    scatter.add_cols("w_in", g32, g16)
    scatter.flush_pairs("pairs_w_in")
    dx = _mm(dproj, w_in, tb=True, tm=1024, tn=1024, tk=512, add=dpre1, add_scale=ALPHA, name="d_x",
             side=_join(scatter.take(300), sync.middle()))
    return dx


SHARDED = (
    ("w_in", "cols", 128), ("w_attn_proj", "rows", 32), ("w_rnn_proj", "rows", 32), ("w_out", "rows", 32),
    ("ffn_w_up", "cols", 128), ("ffn_w_gate", "cols", 128), ("ffn_w_down", "rows", 64),
)
SMALL_REPLICATED = ("b_gate", "rnn_conv_b", "lru_wa", "lru_ba", "lru_wi", "lru_bi", "lru_lambda", "attn_sinks",
                    "ln1_g", "ln1_b", "ffn_conv_b", "ln2_g", "ln2_b")
SMALL_SHARDED = ("rnn_conv_w", "ffn_conv_w")
SMALL_MATS = ("lru_wa", "lru_wi")
WEIGHTS = ("w_in", "b_gate", "rnn_conv_w", "rnn_conv_b", "lru_wa", "lru_ba", "lru_wi", "lru_bi", "lru_lambda",
           "attn_sinks", "w_attn_proj", "w_rnn_proj", "w_out", "ln1_g", "ln1_b", "ffn_w_up", "ffn_w_gate",
           "ffn_conv_w", "ffn_conv_b", "ffn_w_down", "ln2_g", "ln2_b")


def kernel(x, w_in, b_gate, rnn_conv_w, rnn_conv_b, lru_wa, lru_ba, lru_wi, lru_bi, lru_lambda, attn_sinks, w_attn_proj, w_rnn_proj, w_out, ln1_g, ln1_b, ffn_w_up, ffn_w_gate, ffn_conv_w, ffn_conv_b, ffn_w_down, ln2_g, ln2_b, loss_target, m_w_in, m_b_gate, m_rnn_conv_w, m_rnn_conv_b, m_lru_wa, m_lru_ba, m_lru_wi, m_lru_bi, m_lru_lambda, m_attn_sinks, m_w_attn_proj, m_w_rnn_proj, m_w_out, m_ln1_g, m_ln1_b, m_ffn_w_up, m_ffn_w_gate, m_ffn_conv_w, m_ffn_conv_b, m_ffn_w_down, m_ln2_g, m_ln2_b, v_w_in, v_b_gate, v_rnn_conv_w, v_rnn_conv_b, v_lru_wa, v_lru_ba, v_lru_wi, v_lru_bi, v_lru_lambda, v_attn_sinks, v_w_attn_proj, v_w_rnn_proj, v_w_out, v_ln1_g, v_ln1_b, v_ffn_w_up, v_ffn_w_gate, v_ffn_conv_w, v_ffn_conv_b, v_ffn_w_down, v_ln2_g, v_ln2_b):
    given = dict(locals())
    wsh = {n: given[n][0] for n in WEIGHTS}
    msh = {n: given["m_" + n][0] for n in WEIGHTS}
    vsh = {n: given["v_" + n][0] for n in WEIGHTS}
    m_given = {n: given["m_" + n] for n in WEIGHTS}
    v_given = {n: given["v_" + n] for n in WEIGHTS}
    me = 4 * lax.axis_index("x") + 2 * lax.axis_index("y") + lax.axis_index("c")

    order = ("w_in", "rnn_conv_w", "ffn_conv_w", "w_attn_proj", "w_rnn_proj", "w_out", "ffn_w_up", "ffn_w_gate",
             "ffn_w_down")
    gather = _Gather({n: wsh[n] if n in SMALL_SHARDED else wsh[n].astype(MXU_DTYPE) for n in order})
    _run_side(gather.take(through="ffn_conv_w"), "gather_first")
    small = {n: given[n] for n in SMALL_REPLICATED}
    small["lru_wa"] = _block_diag(wsh["lru_wa"])
    small["lru_wi"] = _block_diag(wsh["lru_wi"])
    scatter = _Scatter(me, jnp.stack([_index(_relative(k)) for k in FAR]).astype(jnp.int32))

    vec_names = tuple(n for n in SMALL_REPLICATED if n not in SMALL_MATS) + SMALL_SHARDED
    sync = _SmallSync(vec_names, SMALL_MATS)
    dx = _forward_backward(x[0], loss_target[0], small, gather, scatter, sync)

    loss_total, g_small, mat_sums = sync.end()
    loss_total = loss_total.reshape(())
    for n in SMALL_SHARDED:
        width = wsh[n].shape[1]
        g_small[n] = lax.dynamic_slice_in_dim(g_small[n], me * width, width, axis=1)
    g_small = {n: g_small[n].reshape(given[n].shape) for n in vec_names}
    out = {}
    results = _adamw_many(*[[d[n] for n in vec_names] for d in (given, m_given, v_given, g_small)])
    for n, delta, nm, nv in zip(vec_names, *results):
        out[n] = (g_small[n], delta, nm, nv)
    for n in SMALL_MATS:
        g = mat_sums[n].reshape(given[n].shape)
        out[n] = (g, *_adamw_blocks(given[n], m_given[n], v_given[n], g, name="adamw_" + n))

    tile_rows = {n: tr for n, _, tr in SHARDED}
    for n in list(scatter.sends):
        own, pair, far = scatter.get(n)
        res = _reduce_adamw(wsh[n], msh[n], vsh[n], own, pair, far, me.reshape(1).astype(jnp.int32),
                            tr=tile_rows[n], name="adamw_" + n)
        out[n] = tuple(r[None] for r in res)

    outputs = [loss_total, dx[None]]
    for kind in range(4):
        outputs += [out[n][kind] for n in WEIGHTS]
    return tuple(outputs)
```

```python
import math

import jax
import jax.numpy as jnp
from jax import lax
from jax.experimental import pallas as pl
from jax.experimental.pallas import tpu as pltpu

F32 = jnp.float32
BF16 = jnp.bfloat16
MXU_DTYPE = jnp.bfloat16

N_DEV = 8
S = 2048
D = 2048
HEAD_DIM = 64
N_KV = 4
GROUP = 8
BLOCK = 128
D_KV = N_KV * HEAD_DIM
D_RNN = 2560
RNN_GROUP = 640
N_RNN_GROUPS = D_RNN // RNN_GROUP
RNN_BLOCK_W = 160
RNN_CONV_W = 4
LRU_C = 8.0
D_FF = 6144
FFN_CONV_W = 3
D_IN = 11776
OFF_K = 2048
OFF_V = 2304
OFF_RX = 2560
OFF_RY = 5120
OFF_GA = 7680
OFF_GR = 9728
LN_EPS = 1e-5
ALPHA = 2.0 ** 0.25
ADAM_LR = 0.001
ADAM_B1 = 0.9
ADAM_B2 = 0.999
ADAM_EPS = 1e-08
ADAM_WD = 0.01
ADAM_STEP = 10
NEG = -1e30
VMEM_LIMIT = 56 * 1024 * 1024
MID_RIDE_TENTHS = 6
MESH = pl.DeviceIdType.MESH
GELU_C = math.sqrt(2.0 / math.pi)


def _cparams(*sem):
    return pltpu.CompilerParams(dimension_semantics=sem or None, vmem_limit_bytes=VMEM_LIMIT)


def _call(body, *, name, grid, in_specs, out_specs, out_shape, operands, semantics, scratch_shapes=(), side=None):
    single = not isinstance(out_shape, (list, tuple))
    out_shape = [out_shape] if single else list(out_shape)
    out_specs = [out_specs] if single else list(out_specs)
    in_specs = list(in_specs)
    scratch_shapes = list(scratch_shapes)
    if side is None:
        res = pl.pallas_call(
            body, name=name, grid=grid, in_specs=in_specs, out_specs=out_specs, out_shape=out_shape,
            scratch_shapes=scratch_shapes, compiler_params=_cparams(*semantics))(*operands)
        return res[0] if single else res
    n_in, n_out, n_scr = len(in_specs), len(out_shape), len(scratch_shapes)
    s_in, s_out = len(side.operands), len(side.out_shape)
    hbm = pl.BlockSpec(memory_space=pltpu.HBM)
    steps = math.prod(grid)
    mid_step = (steps * MID_RIDE_TENTHS) // 10

    def with_copies(*refs):
        core_in, side_in = refs[:n_in], refs[n_in:n_in + s_in]
        o0 = n_in + s_in
        core_out, side_out = refs[o0:o0 + n_out], refs[o0 + n_out:o0 + n_out + s_out]
        c0 = o0 + n_out + s_out
        core_scr, sems = refs[c0:c0 + n_scr], refs[c0 + n_scr:]
        step = 0
        for d, size in enumerate(grid):
            step = step * size + pl.program_id(d)

        @pl.when(step == 0)
        def _():
            side.start(side_in, side_out, sems)

        body(*core_in, *core_out, *core_scr)

        @pl.when(step == mid_step)
        def _():
            side.mid(side_in, side_out, sems)

        @pl.when(step == steps - 1)
        def _():
            side.finish(side_in, side_out, sems)

    res = pl.pallas_call(
        with_copies, name=name, grid=grid,
        in_specs=in_specs + [hbm] * s_in, out_specs=out_specs + [hbm] * s_out,
        out_shape=out_shape + list(side.out_shape),
        scratch_shapes=scratch_shapes + list(side.sems),
        input_output_aliases={n_in + i: n_out + o for i, o in side.aliases.items()},
        compiler_params=_cparams(*(("arbitrary",) * len(grid))))(*operands, *side.operands)
    side.done(res[n_out:])
    return res[0] if single else res[:n_out]


def _run_side(side, name):
    def body(*refs):
        s_in, s_out = len(side.operands), len(side.out_shape)
        side.start(refs[:s_in], refs[s_in:s_in + s_out], refs[s_in + s_out:])
        side.mid(refs[:s_in], refs[s_in:s_in + s_out], refs[s_in + s_out:])
        side.finish(refs[:s_in], refs[s_in:s_in + s_out], refs[s_in + s_out:])

    hbm = pl.BlockSpec(memory_space=pltpu.HBM)
    res = pl.pallas_call(
        body, name=name, in_specs=[hbm] * len(side.operands), out_specs=[hbm] * len(side.out_shape),
        out_shape=list(side.out_shape), scratch_shapes=list(side.sems),
        input_output_aliases=dict(side.aliases))(*side.operands)
    side.done(res)


def _gelu(x):
    x2 = x * x
    t = jnp.tanh(GELU_C * (x + 0.044715 * x * x2))
    g = 0.5 * x * (1.0 + t)
    dg = 0.5 * (1.0 + t) + 0.5 * x * (1.0 - t * t) * (GELU_C * (1.0 + 3.0 * 0.044715 * x2))
    return g, dg


def _sigmoid(x):
    return 1.0 / (1.0 + jnp.exp(-x))


def _softplus(x):
    z = jnp.exp(-jnp.abs(x))
    small = z * (1.0 - z * (0.5 - z * (1.0 / 3.0 - 0.25 * z)))
    return jnp.maximum(x, 0.0) + jnp.where(z < 0.02, small, jnp.log(1.0 + z))


def _one_minus_exp(x):
    series = -x * (1.0 + x * (0.5 + x * (1.0 / 6.0 + x * (1.0 / 24.0))))
    return jnp.where(x > -0.03, series, 1.0 - jnp.exp(x))


def _colsum(v):
    return jnp.sum(v, axis=0, keepdims=True)


def _mm(a, b, *, tm, tn, tk, name, ta=False, tb=False, out_dtype=F32, b_block=None, out_block=None, add=None,
        add_scale=1.0, side=None):
    out_dtypes = out_dtype if isinstance(out_dtype, tuple) else (out_dtype,)
    if ta:
        k_dim, m_dim = a.shape
    else:
        m_dim, k_dim = a.shape
    if b_block is None:
        n_dim = b.shape[0] if tb else b.shape[1]
    else:
        n_dim = b.shape[1] if tb else b.shape[0] * b_block
    assert m_dim % tm == 0 and n_dim % tn == 0 and k_dim % tk == 0, (name, m_dim, n_dim, k_dim)
    nk = k_dim // tk
    dims = (((0 if ta else 1,), (1 if tb else 0,)), ((), ()))
    has_add = add is not None

    def body(*refs):
        a_ref, b_ref = refs[0], refs[1]
        add_ref = refs[2] if has_add else None
        first_out = 3 if has_add else 2
        o_refs = refs[first_out:first_out + len(out_dtypes)]

        def product():
            return lax.dot_general(a_ref[...].astype(MXU_DTYPE), b_ref[...].astype(MXU_DTYPE), dims,
                                   preferred_element_type=F32)

        def finish(acc):
            if has_add:
                acc = acc + add_scale * add_ref[...]
            for o_ref in o_refs:
                o_ref[...] = acc.astype(o_ref.dtype)

        if nk == 1:
            finish(product())
        else:
            acc_ref = refs[-1]
            k = pl.program_id(2)

            @pl.when(k == 0)
            def _():
                acc_ref[...] = jnp.zeros_like(acc_ref)

            acc_ref[...] += product()

            @pl.when(k == nk - 1)
            def _():
                finish(acc_ref[...])

    if ta:
        a_spec = pl.BlockSpec((tk, tm), lambda i, j, k: (k, i))
    else:
        a_spec = pl.BlockSpec((tm, tk), lambda i, j, k: (i, k))
    if b_block is None:
        if tb:
            b_spec = pl.BlockSpec((tn, tk), lambda i, j, k: (j, k))
        else:
            b_spec = pl.BlockSpec((tk, tn), lambda i, j, k: (k, j))
    elif tb:
        assert b_block % tk == 0
        b_spec = pl.BlockSpec((None, tn, tk), lambda i, j, k: ((k * tk) // b_block, j, ((k * tk) % b_block) // tk))
    else:
        assert b_block % tn == 0
        b_spec = pl.BlockSpec((None, tk, tn), lambda i, j, k: ((j * tn) // b_block, k, ((j * tn) % b_block) // tn))
    in_specs = [a_spec, b_spec]
    operands = [a, b]
    if has_add:
        in_specs.append(pl.BlockSpec((tm, tn), lambda i, j, k: (i, j)))
        operands.append(add)
    if out_block is None:
        out_spec = pl.BlockSpec((tm, tn), lambda i, j, k: (i, j))
        out_dims = (m_dim, n_dim)
    else:
        assert out_block % tn == 0
        out_spec = pl.BlockSpec((None, tm, tn), lambda i, j, k: ((j * tn) // out_block, i, ((j * tn) % out_block) // tn))
        out_dims = (n_dim // out_block, m_dim, out_block)
    res = _call(
        body,
        name=name,
        grid=(m_dim // tm, n_dim // tn, nk),
        in_specs=in_specs,
        out_specs=[out_spec] * len(out_dtypes),
        out_shape=[jax.ShapeDtypeStruct(out_dims, dt) for dt in out_dtypes],
        scratch_shapes=[pltpu.VMEM((tm, tn), F32)] if nk > 1 else [],
        semantics=("parallel", "parallel", "arbitrary"),
        operands=tuple(operands),
        side=side,
    )
    return res if isinstance(out_dtype, tuple) else res[0]


def _attn_bias(bias_ref, h):
    key = lax.broadcasted_iota(jnp.int32, (2 * BLOCK, GROUP * BLOCK), 0)
    col = lax.broadcasted_iota(jnp.int32, (2 * BLOCK, GROUP * BLOCK), 1)
    dist = BLOCK + (col & (BLOCK - 1)) - key
    head = h * GROUP + (col >> 7) + 1
    slope = jnp.exp(head.astype(F32) * (-0.25 * math.log(2.0)))
    bias = jnp.where((dist >= 0) & (dist < BLOCK), -slope * dist.astype(F32), NEG)
    bias_ref[1] = bias
    bias_ref[0] = jnp.where(key < BLOCK, NEG, bias)


def _attn_probs(kb, qt, bias, sink):
    s = jnp.dot(kb, qt, preferred_element_type=F32) * (HEAD_DIM ** -0.5) + bias
    m = jnp.maximum(jnp.max(s, axis=0, keepdims=True), sink)
    e = jnp.exp(s - m)
    e_sink = jnp.exp(sink - m)
    inv = 1.0 / (jnp.sum(e, axis=0, keepdims=True) + e_sink)
    return e * inv, e_sink * inv


def _heads_on_lanes(ref, r0):
    return jnp.concatenate([ref[g, :, pl.ds(r0, BLOCK)] for g in range(GROUP)], axis=1)


def _attn_fwd(qt, kp, vt, sink_row, side=None):
    cols = GROUP * BLOCK

    def body(q_ref, k_ref, vt_ref, sink_ref, o_ref, bias_ref):
        _attn_bias(bias_ref, pl.program_id(0))
        sink = sink_ref[...]

        def step(n, carry):
            r0 = pl.multiple_of(n * BLOCK, BLOCK)
            p, _ = _attn_probs(k_ref[pl.ds(r0, 2 * BLOCK), :], _heads_on_lanes(q_ref, r0),
                               bias_ref[jnp.minimum(n, 1)], sink)
            o = jnp.dot(vt_ref[:, pl.ds(r0, 2 * BLOCK)], p.astype(MXU_DTYPE), preferred_element_type=F32)
            for g in range(GROUP):
                o_ref[g, :, pl.ds(r0, BLOCK)] = o[:, g * BLOCK:(g + 1) * BLOCK].astype(o_ref.dtype)
            return carry

        lax.fori_loop(0, S // BLOCK, step, 0)

    hm = pl.BlockSpec((None, GROUP, HEAD_DIM, S), lambda h: (h, 0, 0, 0))
    return _call(
        body,
        name="attn_fwd",
        grid=(N_KV,),
        in_specs=[
            hm,
            pl.BlockSpec((None, BLOCK + S, HEAD_DIM), lambda h: (h, 0, 0)),
            pl.BlockSpec((None, HEAD_DIM, BLOCK + S), lambda h: (h, 0, 0)),
            pl.BlockSpec((None, 1, cols), lambda h: (h, 0, 0)),
        ],
        out_specs=hm,
        out_shape=jax.ShapeDtypeStruct((N_KV, GROUP, HEAD_DIM, S), MXU_DTYPE),
        scratch_shapes=[pltpu.VMEM((2, 2 * BLOCK, cols), F32)],
        semantics=("parallel",),
        operands=(qt, kp, vt, sink_row),
        side=side,
    )


def _attn_bwd(qt, kp, kt, vp, sink_row, dot_, side=None):
    cols = GROUP * BLOCK

    def body(q_ref, k_ref, kt_ref, v_ref, sink_ref, do_ref, dq_ref, dk_ref, dv_ref, dsink_ref, bias_ref):
        _attn_bias(bias_ref, pl.program_id(0))
        sink = sink_ref[...]
        dk_ref[...] = jnp.zeros_like(dk_ref)
        dv_ref[...] = jnp.zeros_like(dv_ref)
        nt = (((1,), (1,)), ((), ()))

        def step(n, sink_acc):
            r0 = pl.multiple_of(n * BLOCK, BLOCK)
            band = pl.ds(r0, 2 * BLOCK)
            qn = _heads_on_lanes(q_ref, r0)
            don = _heads_on_lanes(do_ref, r0)
            p, p_sink = _attn_probs(k_ref[band, :], qn, bias_ref[jnp.minimum(n, 1)], sink)
            dp = jnp.dot(v_ref[band, :], don, preferred_element_type=F32)
            delta = jnp.sum(p * dp, axis=0, keepdims=True)
            ds = (p * (dp - delta) * (HEAD_DIM ** -0.5)).astype(MXU_DTYPE)
            dq = jnp.dot(kt_ref[:, band], ds, preferred_element_type=F32)
            for g in range(GROUP):
                dq_ref[g, :, pl.ds(r0, BLOCK)] = dq[:, g * BLOCK:(g + 1) * BLOCK].astype(dq_ref.dtype)
            dk_ref[band, :] += lax.dot_general(ds, qn, nt, preferred_element_type=F32)
            dv_ref[band, :] += lax.dot_general(p.astype(MXU_DTYPE), don, nt, preferred_element_type=F32)
            return sink_acc - p_sink * delta

        sink_acc = lax.fori_loop(0, S // BLOCK, step, jnp.zeros((1, cols), F32))
        for g in range(GROUP):
            dsink_ref[g:g + 1, :] = jnp.sum(sink_acc[:, g * BLOCK:(g + 1) * BLOCK], axis=1, keepdims=True)

    hm = pl.BlockSpec((None, GROUP, HEAD_DIM, S), lambda h: (h, 0, 0, 0))
    kv = pl.BlockSpec((None, BLOCK + S, HEAD_DIM), lambda h: (h, 0, 0))
    return _call(
        body,
        name="attn_bwd",
        grid=(N_KV,),
        in_specs=[hm, kv, pl.BlockSpec((None, HEAD_DIM, BLOCK + S), lambda h: (h, 0, 0)), kv,
                  pl.BlockSpec((None, 1, cols), lambda h: (h, 0, 0)), hm],
        out_specs=[hm, kv, kv, pl.BlockSpec((None, GROUP, 1), lambda h: (h, 0, 0))],
        out_shape=[
            jax.ShapeDtypeStruct((N_KV, GROUP, HEAD_DIM, S), MXU_DTYPE),
            jax.ShapeDtypeStruct((N_KV, BLOCK + S, HEAD_DIM), F32),
            jax.ShapeDtypeStruct((N_KV, BLOCK + S, HEAD_DIM), F32),
            jax.ShapeDtypeStruct((N_KV, GROUP, 1), F32),
        ],
        scratch_shapes=[pltpu.VMEM((2, 2 * BLOCK, cols), F32)],
        semantics=("parallel",),
        operands=(qt, kp, kt, vp, sink_row, dot_),
        side=side,
    )


PAD = 8
CHUNK = 256


def _past_taps(xpad_ref, r0, width):
    ext = xpad_ref[pl.ds(r0, CHUNK + PAD), :]
    taps = []
    for k in range(width):
        back = width - 1 - k
        taps.append((ext if back == 0 else pltpu.roll(ext, back, 0))[PAD:, :])
    return taps


def _future_taps(xpad_ref, r0, width):
    ext = xpad_ref[pl.ds(r0, CHUNK + PAD), :]
    taps = []
    for ahead in range(width):
        taps.append((ext if ahead == 0 else pltpu.roll(ext, CHUNK + PAD - ahead, 0))[:CHUNK, :])
    return taps


def _conv_fwd(src, col0, w, b, *, tc, name, side=None):
    width, c_dim = w.shape

    def body(x_ref, w_ref, b_ref, o_ref, xpad_ref):
        xpad_ref[pl.ds(0, PAD), :] = jnp.zeros((PAD, tc), F32)
        xpad_ref[pl.ds(PAD, S), :] = x_ref[...]
        wv = w_ref[...]
        bv = b_ref[...]

        def step(ci, carry):
            r0 = pl.multiple_of(ci * CHUNK, CHUNK)
            taps = _past_taps(xpad_ref, r0, width)
            y = bv + taps[0] * wv[0:1, :]
            for k in range(1, width):
                y = y + taps[k] * wv[k:k + 1, :]
            o_ref[pl.ds(r0, CHUNK), :] = y
            return carry

        lax.fori_loop(0, S // CHUNK, step, 0)

    return _call(
        body,
        name=name,
        grid=(c_dim // tc,),
        in_specs=[
            pl.BlockSpec((S, tc), lambda j: (0, col0 // tc + j)),
            pl.BlockSpec((width, tc), lambda j: (0, j)),
            pl.BlockSpec((1, tc), lambda j: (0, j)),
        ],
        out_specs=pl.BlockSpec((S, tc), lambda j: (0, j)),
        out_shape=jax.ShapeDtypeStruct((S, c_dim), F32),
        scratch_shapes=[pltpu.VMEM((S + PAD, tc), F32)],
        semantics=("parallel",),
        operands=(src, w, b),
        side=side,
    )


def _conv_bwd(dy, src, col0, w, *, tc, name, side=None):
    width, c_dim = w.shape

    def body(dy_ref, x_ref, w_ref, dx_ref, dw_ref, db_ref, xpad_ref, dpad_ref):
        xpad_ref[pl.ds(0, PAD), :] = jnp.zeros((PAD, tc), F32)
        xpad_ref[pl.ds(PAD, S), :] = x_ref[...]
        dpad_ref[pl.ds(0, S), :] = dy_ref[...]
        dpad_ref[pl.ds(S, PAD), :] = jnp.zeros((PAD, tc), F32)
        wv = w_ref[...]

        def step(ci, acc):
            r0 = pl.multiple_of(ci * CHUNK, CHUNK)
            past = _past_taps(xpad_ref, r0, width)
            ahead = _future_taps(dpad_ref, r0, width)
            d = ahead[0]
            dx = d * wv[width - 1:width, :]
            for j in range(1, width):
                dx = dx + ahead[j] * wv[width - 1 - j:width - j, :]
            dx_ref[pl.ds(r0, CHUNK), :] = dx.astype(dx_ref.dtype)
            return tuple(acc[k] + _colsum(past[k] * d) for k in range(width)) + (acc[width] + _colsum(d),)

        zero = jnp.zeros((1, tc), F32)
        acc = lax.fori_loop(0, S // CHUNK, step, (zero,) * (width + 1))
        for k in range(width):
            dw_ref[k:k + 1, :] = acc[k]
        db_ref[...] = acc[width]

    return _call(
        body,
        name=name,
        grid=(c_dim // tc,),
        in_specs=[
            pl.BlockSpec((S, tc), lambda j: (0, j)),
            pl.BlockSpec((S, tc), lambda j: (0, col0 // tc + j)),
            pl.BlockSpec((width, tc), lambda j: (0, j)),
        ],
        out_specs=[
            pl.BlockSpec((S, tc), lambda j: (0, j)),
            pl.BlockSpec((width, tc), lambda j: (0, j)),
            pl.BlockSpec((1, tc), lambda j: (0, j)),
        ],
        out_shape=[
            jax.ShapeDtypeStruct((S, c_dim), MXU_DTYPE),
            jax.ShapeDtypeStruct((width, c_dim), F32),
            jax.ShapeDtypeStruct((1, c_dim), F32),
        ],
        scratch_shapes=[pltpu.VMEM((S + PAD, tc), F32), pltpu.VMEM((S + PAD, tc), F32)],
        semantics=("parallel",),
        operands=(dy, src, w),
        side=side,
    )


SCAN_TC = 256


def _lru_gates(rxc, wa, wi, ba, bi, side=None):
    tm = 512

    def body(x_ref, wa_ref, wi_ref, ba_ref, bi_ref, r_ref, i_ref):
        xv = x_ref[...].astype(MXU_DTYPE)
        r_ref[...] = _sigmoid(jnp.dot(xv, wa_ref[...].astype(MXU_DTYPE), preferred_element_type=F32) + ba_ref[...])
        i_ref[...] = _sigmoid(jnp.dot(xv, wi_ref[...].astype(MXU_DTYPE), preferred_element_type=F32) + bi_ref[...])

    x_spec = pl.BlockSpec((tm, RNN_GROUP), lambda g, i: (i, g))
    w_spec = pl.BlockSpec((None, RNN_GROUP, RNN_GROUP), lambda g, i: (g, 0, 0))
    b_spec = pl.BlockSpec((1, RNN_GROUP), lambda g, i: (0, g))
    return _call(
        body,
        name="lru_gates",
        grid=(N_RNN_GROUPS, S // tm),
        in_specs=[x_spec, w_spec, w_spec, b_spec, b_spec],
        out_specs=[x_spec, x_spec],
        out_shape=[jax.ShapeDtypeStruct((S, D_RNN), F32)] * 2,
        semantics=("parallel", "parallel"),
        operands=(rxc, wa, wi, ba, bi),
        side=side,
    )


def _scan_down(a, u, row):
    for d in (1, 2, 4):
        a_s = jnp.where(row >= d, pltpu.roll(a, d, 0), 1.0)
        u_s = jnp.where(row >= d, pltpu.roll(u, d, 0), 0.0)
        u = a * u_s + u
        a = a * a_s
    return a, u


def _scan_up(a, u, row):
    for d in (1, 2, 4):
        a_s = jnp.where(row < 8 - d, pltpu.roll(a, 8 - d, 0), 1.0)
        u_s = jnp.where(row < 8 - d, pltpu.roll(u, 8 - d, 0), 0.0)
        u = a * u_s + u
        a = a * a_s
    return a, u


def _lru_scan_fwd(r, i, rxc, proj, lam, side=None):
    tc = SCAN_TC

    def body(r_ref, i_ref, x_ref, ry_ref, lam_ref, h_ref, y_ref):
        rate = LRU_C * _softplus(-lam_ref[...])
        row = lax.broadcasted_iota(jnp.int32, (8, tc), 0)

        def step(ci, carry):
            r0 = pl.multiple_of(ci * 16, 16)
            log_a = -rate * r_ref[pl.ds(r0, 16), :]
            a16 = jnp.exp(log_a)
            u16 = jnp.sqrt(_one_minus_exp(2.0 * log_a)) * (i_ref[pl.ds(r0, 16), :] * x_ref[pl.ds(r0, 16), :])
            hs = []
            for half in range(2):
                a_cum, h0 = _scan_down(a16[8 * half:8 * half + 8, :], u16[8 * half:8 * half + 8, :], row)
                h = a_cum * carry + h0
                carry = jnp.broadcast_to(h[7:8, :], (8, tc))
                hs.append(h)
            h16 = jnp.concatenate(hs, axis=0)
            h_ref[pl.ds(r0, 16), :] = h16
            y_ref[pl.ds(r0, 16), :] = (h16 * _gelu(ry_ref[pl.ds(r0, 16), :])[0]).astype(y_ref.dtype)
            return carry

        lax.fori_loop(0, S // 16, step, jnp.zeros((8, tc), F32))

    col = pl.BlockSpec((S, tc), lambda j: (0, j))
    return _call(
        body,
        name="lru_scan_fwd",
        grid=(D_RNN // tc,),
        in_specs=[col, col, col, pl.BlockSpec((S, tc), lambda j: (0, OFF_RY // tc + j)),
                  pl.BlockSpec((1, tc), lambda j: (0, j))],
        out_specs=[col, col],
        out_shape=[jax.ShapeDtypeStruct((S, D_RNN), F32), jax.ShapeDtypeStruct((S, D_RNN), MXU_DTYPE)],
        semantics=("parallel",),
        operands=(r, i, rxc, proj, lam),
        side=side,
    )


def _lru_scan_bwd(dy, proj, h, r, i, rxc, lam, side=None):
    tc = SCAN_TC

    def body(dy_ref, ry_ref, h_ref, r_ref, i_ref, x_ref, lam_ref,
             dry_ref, dzr_ref, dzi_ref, dx_ref, dba_ref, dbi_ref, dlam_ref, a_ref, dh_ref, hp_ref):
        lam_v = lam_ref[...]
        rate = LRU_C * _softplus(-lam_v)
        dlam_scale = LRU_C * _sigmoid(-lam_v)
        row = lax.broadcasted_iota(jnp.int32, (8, tc), 0)
        hp_ref[pl.ds(0, PAD), :] = jnp.zeros((PAD, tc), F32)
        hp_ref[pl.ds(PAD, S), :] = h_ref[...]
        a_ref[pl.ds(S, PAD), :] = jnp.zeros((PAD, tc), F32)

        def prep(ci, carry):
            r0 = pl.multiple_of(ci * CHUNK, CHUNK)
            a_ref[pl.ds(r0, CHUNK), :] = jnp.exp(-rate * r_ref[pl.ds(r0, CHUNK), :])
            ge, dge = _gelu(ry_ref[pl.ds(r0, CHUNK), :])
            dyv = dy_ref[pl.ds(r0, CHUNK), :]
            dh_ref[pl.ds(r0, CHUNK), :] = dyv * ge
            dry_ref[pl.ds(r0, CHUNK), :] = (dyv * h_ref[pl.ds(r0, CHUNK), :] * dge).astype(dry_ref.dtype)
            return carry

        lax.fori_loop(0, S // CHUNK, prep, 0)

        def step(ci, state):
            carry, dba, dbi, dlam = state
            r0 = pl.multiple_of(S - 16 - ci * 16, 16)
            a_ext = a_ref[pl.ds(r0, 24), :]
            a_next = pltpu.roll(a_ext, 23, 0)
            h_prev = pltpu.roll(hp_ref[pl.ds(r0, 24), :], 1, 0)
            dh16 = dh_ref[pl.ds(r0, 16), :]
            gs = [None, None]
            for half in (1, 0):
                lo = 8 * half
                c_cum, g0 = _scan_up(a_next[lo:lo + 8, :], dh16[lo:lo + 8, :], row)
                g = c_cum * carry + g0
                carry = jnp.broadcast_to(g[0:1, :], (8, tc))
                gs[half] = g
            g16 = jnp.concatenate(gs, axis=0)
            a16 = a_ext[0:16, :]
            r16 = r_ref[pl.ds(r0, 16), :]
            i16 = i_ref[pl.ds(r0, 16), :]
            x16 = x_ref[pl.ds(r0, 16), :]
            a2 = a16 * a16
            sq = jnp.sqrt(_one_minus_exp(-2.0 * rate * r16))
            dx_ref[pl.ds(r0, 16), :] = g16 * sq * i16
            dzi = g16 * sq * x16 * i16 * (1.0 - i16)
            dlog_a = g16 * h_prev[8:24, :] * a16 - g16 * i16 * x16 * a2 / sq
            dzr = -rate * dlog_a * r16 * (1.0 - r16)
            dzr_ref[pl.ds(r0, 16), :] = dzr.astype(dzr_ref.dtype)
            dzi_ref[pl.ds(r0, 16), :] = dzi.astype(dzi_ref.dtype)
            return carry, dba + _colsum(dzr), dbi + _colsum(dzi), dlam + _colsum(dlog_a * r16)

        zero = jnp.zeros((1, tc), F32)
        _, dba, dbi, dlam = lax.fori_loop(0, S // 16, step, (jnp.zeros((8, tc), F32), zero, zero, zero))
        dba_ref[...] = dba
        dbi_ref[...] = dbi
        dlam_ref[...] = dlam * dlam_scale

    col = pl.BlockSpec((S, tc), lambda j: (0, j))
    vec = pl.BlockSpec((1, tc), lambda j: (0, j))
    return _call(
        body,
        name="lru_scan_bwd",
        grid=(D_RNN // tc,),
        in_specs=[col, pl.BlockSpec((S, tc), lambda j: (0, OFF_RY // tc + j)), col, col, col, col, vec],
        out_specs=[col, col, col, col, vec, vec, vec],
        out_shape=[jax.ShapeDtypeStruct((S, D_RNN), MXU_DTYPE)] * 3 + [jax.ShapeDtypeStruct((S, D_RNN), F32)]
        + [jax.ShapeDtypeStruct((1, D_RNN), F32)] * 3,
        scratch_shapes=[pltpu.VMEM((S + PAD, tc), F32), pltpu.VMEM((S, tc), F32), pltpu.VMEM((S + PAD, tc), F32)],
        semantics=("parallel",),
        operands=(dy, proj, h, r, i, rxc, lam),
        side=side,
    )


def _lru_gate_wgrad(rxc, dzr, dzi, side=None):
    def body(x_ref, dzr_ref, dzi_ref, dwa_ref, dwi_ref):
        xv = x_ref[...].astype(MXU_DTYPE)
        dims = (((0,), (0,)), ((), ()))
        dwa_ref[...] = lax.dot_general(xv, dzr_ref[...], dims, preferred_element_type=F32)
        dwi_ref[...] = lax.dot_general(xv, dzi_ref[...], dims, preferred_element_type=F32)

    col = pl.BlockSpec((S, RNN_GROUP), lambda g: (0, g))
    w_spec = pl.BlockSpec((None, RNN_GROUP, RNN_GROUP), lambda g: (g, 0, 0))
    return _call(
        body,
        name="lru_gate_wgrad",
        grid=(N_RNN_GROUPS,),
        in_specs=[col, col, col],
        out_specs=[w_spec, w_spec],
        out_shape=[jax.ShapeDtypeStruct((N_RNN_GROUPS, RNN_GROUP, RNN_GROUP), F32)] * 2,
        semantics=("parallel",),
        operands=(rxc, dzr, dzi),
        side=side,
    )


def _lru_gate_xgrad(dzr, dzi, wa, wi, dx_in, side=None):
    tm = 512

    def body(dzr_ref, dzi_ref, wa_ref, wi_ref, dx_ref, o_ref):
        dims = (((1,), (1,)), ((), ()))
        o_ref[...] = (dx_ref[...]
                      + lax.dot_general(dzr_ref[...], wa_ref[...].astype(MXU_DTYPE), dims, preferred_element_type=F32)
                      + lax.dot_general(dzi_ref[...], wi_ref[...].astype(MXU_DTYPE), dims, preferred_element_type=F32))

    x_spec = pl.BlockSpec((tm, RNN_GROUP), lambda g, i: (i, g))
    w_spec = pl.BlockSpec((None, RNN_GROUP, RNN_GROUP), lambda g, i: (g, 0, 0))
    return _call(
        body,
        name="lru_gate_xgrad",
        grid=(N_RNN_GROUPS, S // tm),
        in_specs=[x_spec, x_spec, w_spec, w_spec, x_spec],
        out_specs=x_spec,
        out_shape=jax.ShapeDtypeStruct((S, D_RNN), F32),
        semantics=("parallel", "parallel"),
        operands=(dzr, dzi, wa, wi, dx_in),
        side=side,
    )


def _gate_fwd(y_attn, y_rnn, proj, b_gate, side=None):
    t = 512

    def body(ya_ref, yr_ref, ga_ref, gr_ref, ba_ref, br_ref, o_ref):
        o_ref[...] = (_sigmoid(ga_ref[...] + ba_ref[...]) * ya_ref[...]
                      + _sigmoid(gr_ref[...] + br_ref[...]) * yr_ref[...]).astype(o_ref.dtype)

    tile = pl.BlockSpec((t, t), lambda i, j: (i, j))
    return _call(
        body,
        name="gate_fwd",
        grid=(S // t, D // t),
        in_specs=[tile, tile,
                  pl.BlockSpec((t, t), lambda i, j: (i, OFF_GA // t + j)),
                  pl.BlockSpec((t, t), lambda i, j: (i, OFF_GR // t + j)),
                  pl.BlockSpec((1, t), lambda i, j: (0, j)),
                  pl.BlockSpec((1, t), lambda i, j: (0, D // t + j))],
        out_specs=tile,
        out_shape=jax.ShapeDtypeStruct((S, D), MXU_DTYPE),
        semantics=("parallel", "parallel"),
        operands=(y_attn, y_rnn, proj, proj, b_gate, b_gate),
        side=side,
    )


def _gate_bwd(dmix, y_attn, y_rnn, proj, b_gate, side=None):
    t = 512

    def body(dm_ref, ya_ref, yr_ref, ga_ref, gr_ref, ba_ref, br_ref,
             dya_ref, dyr_ref, dga_ref, dgr_ref, dba_ref, dbr_ref):
        @pl.when(pl.program_id(1) == 0)
        def _():
            dba_ref[...] = jnp.zeros_like(dba_ref)
            dbr_ref[...] = jnp.zeros_like(dbr_ref)

        dm = dm_ref[...]
        ga = _sigmoid(ga_ref[...] + ba_ref[...])
        gr = _sigmoid(gr_ref[...] + br_ref[...])
        dya_ref[...] = (dm * ga).astype(dya_ref.dtype)
        dyr_ref[...] = (dm * gr).astype(dyr_ref.dtype)
        dga = dm * ya_ref[...] * ga * (1.0 - ga)
        dgr = dm * yr_ref[...] * gr * (1.0 - gr)
        dga_ref[...] = dga.astype(dga_ref.dtype)
        dgr_ref[...] = dgr.astype(dgr_ref.dtype)
        dba_ref[...] += _colsum(dga)
        dbr_ref[...] += _colsum(dgr)

    tile = pl.BlockSpec((t, t), lambda j, i: (i, j))
    vec = pl.BlockSpec((1, t), lambda j, i: (0, j))
    return _call(
        body,
        name="gate_bwd",
        grid=(D // t, S // t),
        in_specs=[tile, tile, tile,
                  pl.BlockSpec((t, t), lambda j, i: (i, OFF_GA // t + j)),
                  pl.BlockSpec((t, t), lambda j, i: (i, OFF_GR // t + j)),
                  vec,
                  pl.BlockSpec((1, t), lambda j, i: (0, D // t + j))],
        out_specs=[tile, tile, tile, tile, vec, vec],
        out_shape=[jax.ShapeDtypeStruct((S, D), MXU_DTYPE)] * 4 + [jax.ShapeDtypeStruct((1, D), F32)] * 2,
        semantics=("parallel", "arbitrary"),
        operands=(dmix, y_attn, y_rnn, proj, proj, b_gate, b_gate),
        side=side,
    )


LN_TM = 256


def _ln_stats(pre):
    mu = jnp.mean(pre, axis=-1, keepdims=True)
    xc = pre - mu
    rstd = lax.rsqrt(jnp.mean(xc * xc, axis=-1, keepdims=True) + LN_EPS)
    return xc * rstd, rstd


def _ln_input_grad(dy, xhat, rstd, g):
    dyg = dy * g
    return rstd * (dyg - jnp.mean(dyg, axis=-1, keepdims=True)
                   - xhat * jnp.mean(dyg * xhat, axis=-1, keepdims=True))


def _ln_fwd(res, branch, g, b, side=None):
    def body(res_ref, br_ref, g_ref, b_ref, y_ref, yb_ref, xhat_ref, rstd_ref):
        xhat, rstd = _ln_stats(ALPHA * res_ref[...] + br_ref[...])
        y = xhat * g_ref[...] + b_ref[...]
        y_ref[...] = y
        yb_ref[...] = y.astype(yb_ref.dtype)
        xhat_ref[...] = xhat
        rstd_ref[...] = rstd

    tile = pl.BlockSpec((LN_TM, D), lambda i: (i, 0))
    vec = pl.BlockSpec((1, D), lambda i: (0, 0))
    return _call(
        body,
        name="ln_fwd",
        grid=(S // LN_TM,),
        in_specs=[tile, tile, vec, vec],
        out_specs=[tile, tile, tile, pl.BlockSpec((LN_TM, 1), lambda i: (i, 0))],
        out_shape=[jax.ShapeDtypeStruct((S, D), F32), jax.ShapeDtypeStruct((S, D), MXU_DTYPE),
                   jax.ShapeDtypeStruct((S, D), F32), jax.ShapeDtypeStruct((S, 1), F32)],
        semantics=("parallel",),
        operands=(res, branch, g, b),
        side=side,
    )


def _ln_bwd(dy_a, dy_b, xhat, rstd, g, side=None):
    def body(da_ref, db_in_ref, xhat_ref, rstd_ref, g_ref, dp_ref, dpb_ref, dg_ref, db_ref):
        @pl.when(pl.program_id(0) == 0)
        def _():
            dg_ref[...] = jnp.zeros_like(dg_ref)
            db_ref[...] = jnp.zeros_like(db_ref)

        dy = da_ref[...] + ALPHA * db_in_ref[...]
        xhat = xhat_ref[...]
        dp = _ln_input_grad(dy, xhat, rstd_ref[...], g_ref[...])
        dp_ref[...] = dp
        dpb_ref[...] = dp.astype(dpb_ref.dtype)
        dg_ref[...] += _colsum(dy * xhat)
        db_ref[...] += _colsum(dy)

    tile = pl.BlockSpec((LN_TM, D), lambda i: (i, 0))
    vec = pl.BlockSpec((1, D), lambda i: (0, 0))
    return _call(
        body,
        name="ln_bwd",
        grid=(S // LN_TM,),
        in_specs=[tile, tile, tile, pl.BlockSpec((LN_TM, 1), lambda i: (i, 0)), vec],
        out_specs=[tile, tile, vec, vec],
        out_shape=[jax.ShapeDtypeStruct((S, D), F32), jax.ShapeDtypeStruct((S, D), MXU_DTYPE),
                   jax.ShapeDtypeStruct((1, D), F32), jax.ShapeDtypeStruct((1, D), F32)],
        semantics=("arbitrary",),
        operands=(dy_a, dy_b, xhat, rstd, g),
        side=side,
    )


def _ln_loss_bwd(res, branch, g, b, target, side=None):
    def body(res_ref, br_ref, g_ref, b_ref, t_ref, loss_ref, dp_ref, dpb_ref, dg_ref, db_ref):
        @pl.when(pl.program_id(0) == 0)
        def _():
            loss_ref[...] = jnp.zeros_like(loss_ref)
            dg_ref[...] = jnp.zeros_like(dg_ref)
            db_ref[...] = jnp.zeros_like(db_ref)

        xhat, rstd = _ln_stats(ALPHA * res_ref[...] + br_ref[...])
        gv = g_ref[...]
        err = xhat * gv + b_ref[...] - t_ref[...]
        loss_ref[...] += (0.5 / D) * jnp.sum(_colsum(err * err), axis=1, keepdims=True)
        dy = err * (1.0 / D)
        dp = _ln_input_grad(dy, xhat, rstd, gv)
        dp_ref[...] = dp
        dpb_ref[...] = dp.astype(dpb_ref.dtype)
        dg_ref[...] += _colsum(dy * xhat)
        db_ref[...] += _colsum(dy)

    tile = pl.BlockSpec((LN_TM, D), lambda i: (i, 0))
    vec = pl.BlockSpec((1, D), lambda i: (0, 0))
    return _call(
        body,
        name="ln_loss_bwd",
        grid=(S // LN_TM,),
        in_specs=[tile, tile, vec, vec, tile],
        out_specs=[pl.BlockSpec((1, 1), lambda i: (0, 0)), tile, tile, vec, vec],
        out_shape=[jax.ShapeDtypeStruct((1, 1), F32), jax.ShapeDtypeStruct((S, D), F32),
                   jax.ShapeDtypeStruct((S, D), MXU_DTYPE),
                   jax.ShapeDtypeStruct((1, D), F32), jax.ShapeDtypeStruct((1, D), F32)],
        semantics=("arbitrary",),
        operands=(res, branch, g, b, target),
        side=side,
    )


FFN_TC = 256


def _ffn_act_fwd(up, gpre, w, b, side=None):
    tc = FFN_TC

    def body(up_ref, x_ref, w_ref, b_ref, o_ref, xpad_ref):
        xpad_ref[pl.ds(0, PAD), :] = jnp.zeros((PAD, tc), F32)
        xpad_ref[pl.ds(PAD, S), :] = x_ref[...]
        wv = w_ref[...]
        bv = b_ref[...]

        def step(ci, carry):
            r0 = pl.multiple_of(ci * CHUNK, CHUNK)
            taps = _past_taps(xpad_ref, r0, FFN_CONV_W)
            gate = bv + taps[0] * wv[0:1, :] + taps[1] * wv[1:2, :] + taps[2] * wv[2:3, :]
            o_ref[pl.ds(r0, CHUNK), :] = (_gelu(gate)[0] * up_ref[pl.ds(r0, CHUNK), :]).astype(o_ref.dtype)
            return carry

        lax.fori_loop(0, S // CHUNK, step, 0)

    col = pl.BlockSpec((S, tc), lambda j: (0, j))
    return _call(
        body,
        name="ffn_act_fwd",
        grid=(D_FF // tc,),
        in_specs=[col, col, pl.BlockSpec((FFN_CONV_W, tc), lambda j: (0, j)), pl.BlockSpec((1, tc), lambda j: (0, j))],
        out_specs=col,
        out_shape=jax.ShapeDtypeStruct((S, D_FF), MXU_DTYPE),
        scratch_shapes=[pltpu.VMEM((S + PAD, tc), F32)],
        semantics=("parallel",),
        operands=(up, gpre, w, b),
        side=side,
    )


def _ffn_act_bwd(dfin, up, gpre, w, b, side=None):
    tc = FFN_TC
    width = FFN_CONV_W

    def body(df_ref, up_ref, x_ref, w_ref, b_ref, dup_ref, dx_ref, dw_ref, db_ref, xpad_ref, dpad_ref):
        xpad_ref[pl.ds(0, PAD), :] = jnp.zeros((PAD, tc), F32)
        xpad_ref[pl.ds(PAD, S), :] = x_ref[...]
        dpad_ref[pl.ds(S, PAD), :] = jnp.zeros((PAD, tc), F32)
        wv = w_ref[...]
        bv = b_ref[...]

        def gate_grad(ci, acc):
            r0 = pl.multiple_of(ci * CHUNK, CHUNK)
            taps = _past_taps(xpad_ref, r0, width)
            gate = bv + taps[0] * wv[0:1, :] + taps[1] * wv[1:2, :] + taps[2] * wv[2:3, :]
            ge, dge = _gelu(gate)
            df = df_ref[pl.ds(r0, CHUNK), :]
            dup_ref[pl.ds(r0, CHUNK), :] = (df * ge).astype(dup_ref.dtype)
            d = df * up_ref[pl.ds(r0, CHUNK), :] * dge
            dpad_ref[pl.ds(r0, CHUNK), :] = d
            return tuple(acc[k] + _colsum(taps[k] * d) for k in range(width)) + (acc[width] + _colsum(d),)

        zero = jnp.zeros((1, tc), F32)
        acc = lax.fori_loop(0, S // CHUNK, gate_grad, (zero,) * (width + 1))
        for k in range(width):
            dw_ref[k:k + 1, :] = acc[k]
        db_ref[...] = acc[width]

        def input_grad(ci, carry):
            r0 = pl.multiple_of(ci * CHUNK, CHUNK)
            ahead = _future_taps(dpad_ref, r0, width)
            dx = ahead[0] * wv[2:3, :] + ahead[1] * wv[1:2, :] + ahead[2] * wv[0:1, :]
            dx_ref[pl.ds(r0, CHUNK), :] = dx.astype(dx_ref.dtype)
            return carry

        lax.fori_loop(0, S // CHUNK, input_grad, 0)

    col = pl.BlockSpec((S, tc), lambda j: (0, j))
    w_spec = pl.BlockSpec((width, tc), lambda j: (0, j))
    vec = pl.BlockSpec((1, tc), lambda j: (0, j))
    return _call(
        body,
        name="ffn_act_bwd",
        grid=(D_FF // tc,),
        in_specs=[col, col, col, w_spec, vec],
        out_specs=[col, col, w_spec, vec],
        out_shape=[jax.ShapeDtypeStruct((S, D_FF), MXU_DTYPE)] * 2
        + [jax.ShapeDtypeStruct((width, D_FF), F32), jax.ShapeDtypeStruct((1, D_FF), F32)],
        scratch_shapes=[pltpu.VMEM((S + PAD, tc), F32), pltpu.VMEM((S + PAD, tc), F32)],
        semantics=("parallel",),
        operands=(dfin, up, gpre, w, b),
        side=side,
    )


def _adamw_update(w, g, m, v):
    m = ADAM_B1 * m + (1.0 - ADAM_B1) * g
    v = ADAM_B2 * v + (1.0 - ADAM_B2) * (g * g)
    m_hat = m / (1.0 - ADAM_B1 ** ADAM_STEP)
    v_hat = v / (1.0 - ADAM_B2 ** ADAM_STEP)
    delta = -ADAM_LR * (m_hat / (jnp.sqrt(v_hat) + ADAM_EPS) + ADAM_WD * w)
    return delta, m, v


def _add_pairs(send, pair, far_index, *, name):
    _, r_dim, c_dim = send.shape
    tr = r_dim // 4

    def body(far_ref, mine_ref, theirs_ref, o_ref):
        o_ref[...] = (mine_ref[...].astype(F32) + theirs_ref[...].astype(F32)).astype(o_ref.dtype)

    return pl.pallas_call(
        body,
        name=name,
        grid_spec=pltpu.PrefetchScalarGridSpec(
            num_scalar_prefetch=1,
            grid=(3, r_dim // tr),
            in_specs=[pl.BlockSpec((None, tr, c_dim), lambda j, i, far: (far[j], i, 0)),
                      pl.BlockSpec((None, tr, c_dim), lambda j, i, far: (1 + j, i, 0))],
            out_specs=pl.BlockSpec((None, tr, c_dim), lambda j, i, far: (j, i, 0)),
        ),
        out_shape=jax.ShapeDtypeStruct((3, r_dim, c_dim), BF16),
        compiler_params=_cparams("parallel", "parallel"),
    )(far_index, send, pair)


def _reduce_adamw(w, m, v, g_own, pair, far, me, *, tr, name):
    r_dim, c_dim = w.shape

    def body(me_ref, w_ref, m_ref, v_ref, g_ref, pair_ref, far_ref, grad_ref, delta_ref, nm_ref, nv_ref):
        g = g_ref[...] + pair_ref[...].astype(F32)
        for j in range(3):
            g = g + far_ref[j].astype(F32)
        delta, nm, nv = _adamw_update(w_ref[...], g, m_ref[...], v_ref[...])
        grad_ref[...] = g
        delta_ref[...] = delta
        nm_ref[...] = nm
        nv_ref[...] = nv

    tile = pl.BlockSpec((tr, c_dim), lambda i, me: (i, 0))
    if g_own.ndim == 3:
        own_spec = pl.BlockSpec((None, tr, c_dim), lambda i, me: (me[0], i, 0))
    else:
        own_spec = tile
    return pl.pallas_call(
        body,
        name=name,
        grid_spec=pltpu.PrefetchScalarGridSpec(
            num_scalar_prefetch=1,
            grid=(r_dim // tr,),
            in_specs=[tile, tile, tile, own_spec, pl.BlockSpec((None, tr, c_dim), lambda i, me: (0, i, 0)),
                      pl.BlockSpec((3, tr, c_dim), lambda i, me: (0, i, 0))],
            out_specs=[tile] * 4,
        ),
        out_shape=[jax.ShapeDtypeStruct((r_dim, c_dim), F32)] * 4,
        compiler_params=_cparams("parallel"),
    )(me, w, m, v, g_own, pair, far)


def _adamw_many(ws, ms, vs, gs):
    n = len(ws)

    def body(*refs):
        for i in range(n):
            delta, nm, nv = _adamw_update(refs[i][...], refs[3 * n + i][...], refs[n + i][...], refs[2 * n + i][...])
            refs[4 * n + i][...] = delta
            refs[5 * n + i][...] = nm
            refs[6 * n + i][...] = nv

    vmem = pl.BlockSpec(memory_space=pltpu.VMEM)
    res = pl.pallas_call(
        body,
        name="adamw_small",
        in_specs=[vmem] * (4 * n),
        out_specs=[vmem] * (3 * n),
        out_shape=[jax.ShapeDtypeStruct(w.shape, F32) for w in ws] * 3,
        compiler_params=pltpu.CompilerParams(vmem_limit_bytes=VMEM_LIMIT),
    )(*ws, *ms, *vs, *gs)
    return res[:n], res[n:2 * n], res[2 * n:]


def _adamw_blocks(w, m, v, g, *, name, side=None):
    per = 2

    def body(w_ref, m_ref, v_ref, g_ref, delta_ref, nm_ref, nv_ref):
        delta, nm, nv = _adamw_update(w_ref[...], g_ref[...], m_ref[...], v_ref[...])
        delta_ref[...] = delta
        nm_ref[...] = nm
        nv_ref[...] = nv

    tile = pl.BlockSpec((1, per) + w.shape[2:], lambda i: (0, i, 0, 0))
    return _call(
        body,
        name=name,
        grid=(w.shape[1] // per,),
        in_specs=[tile] * 4,
        out_specs=[tile] * 3,
        out_shape=[jax.ShapeDtypeStruct(w.shape, F32)] * 3,
        semantics=("parallel",),
        operands=(w, m, v, g),
        side=side,
    )


def _coords():
    return lax.axis_index("x"), lax.axis_index("y"), lax.axis_index("c")


def _flip(coord, bit):
    return 1 - coord if bit else coord


def _relative(k):
    x, y, c = _coords()
    return _flip(x, k & 4), _flip(y, k & 2), _flip(c, k & 1)


def _index(pos):
    return 4 * pos[0] + 2 * pos[1] + pos[2]


FAR = (4, 2, 6)
AG_US_PER_MB = 38.0
RS_US_PER_MB = 46.0
MIN_RIDE_US = 30.0
MIN_GATHER_RIDE_US = 22.0
ROW_ALIGN = 32


def _chunks(items, cursor, us, us_per_mb, through=None):
    budget = float("inf") if us is None else us / us_per_mb * 2 ** 20
    names = list(items)
    if through is not None:
        names = names[:names.index(through) + 1]
    chunks = []
    for name in names:
        arr = items[name]
        r_dim, c_dim = arr.shape[-2:]
        row_bytes = c_dim * arr.dtype.itemsize
        while cursor[name] < r_dim and budget > 0:
            rows = r_dim - cursor[name]
            if r_dim > ROW_ALIGN and budget < rows * row_bytes:
                rows = min(rows, max(ROW_ALIGN, int(budget // row_bytes) // ROW_ALIGN * ROW_ALIGN))
            chunks.append((name, cursor[name], rows))
            cursor[name] += rows
            budget -= rows * row_bytes
    return chunks


class _Gather:
    def __init__(self, shards):
        self.shards, self.bufs, self.cursor = {}, {}, {}
        self.add_shards(shards)

    def add_shards(self, shards):
        for n, shard in shards.items():
            self.shards[n], self.bufs[n], self.cursor[n] = shard, None, 0

    def take(self, us=None, through=None):
        if us is not None and us < MIN_GATHER_RIDE_US:
            return None
        chunks = _chunks(self.shards, self.cursor, us, AG_US_PER_MB, through)
        return _GatherSide(self, chunks) if chunks else None

    def get(self, name):
        chunks = _chunks(self.shards, self.cursor, None, AG_US_PER_MB, through=name)
        if chunks:
            _run_side(_GatherSide(self, chunks), "gather_" + name)
        return self.bufs[name]


class _GatherSide:
    SEMS = 8

    def __init__(self, owner, chunks):
        self.owner, self.chunks = owner, chunks
        self.names = list(dict.fromkeys(n for n, _, _ in chunks))
        old = [n for n in self.names if owner.bufs[n] is not None]
        self.operands = [owner.shards[n] for n in self.names] + [owner.bufs[n] for n in old]
        self.out_shape = [jax.ShapeDtypeStruct((N_DEV,) + owner.shards[n].shape, owner.shards[n].dtype)
                          for n in self.names]
        self.aliases = {len(self.names) + i: self.names.index(n) for i, n in enumerate(old)}
        self.sems = [pltpu.SemaphoreType.DMA((self.SEMS * len(chunks),)),
                     pltpu.SemaphoreType.DMA((self.SEMS * len(chunks),)), pltpu.SemaphoreType.DMA((len(chunks),))]

    def _halves(self, ci):
        _, r0, rows = self.chunks[ci]
        if rows % ROW_ALIGN:
            return None
        return (r0, rows // 2), (r0 + rows // 2, rows // 2)

    def _copy(self, ins, outs, sems, ci, s, block, to, rows=None, from_shard=False):
        name, r0, n = self.chunks[ci]
        if rows is not None:
            r0, n = rows
        w = self.names.index(name)
        slot = outs[w].at[_index(block), pl.ds(r0, n)]
        return pltpu.make_async_remote_copy(
            src_ref=ins[w].at[pl.ds(r0, n)] if from_shard else slot, dst_ref=slot,
            send_sem=sems[0].at[self.SEMS * ci + s], recv_sem=sems[1].at[self.SEMS * ci + s],
            device_id=to, device_id_type=MESH)

    def _own(self, ins, outs, sems, ci):
        name, r0, rows = self.chunks[ci]
        w = self.names.index(name)
        return pltpu.make_async_copy(ins[w].at[pl.ds(r0, rows)], outs[w].at[_index(_relative(0)), pl.ds(r0, rows)],
                                     sems[2].at[ci])

    def _pass(self, ins, outs, sems, ci, which):
        source, target = ((4, 2), (2, 4))[which]
        return self._copy(ins, outs, sems, ci, 3 + which, _relative(source), _relative(target),
                          rows=self._halves(ci)[which])

    def start(self, ins, outs, sems):
        me = _relative(0)
        for ci in range(len(self.chunks)):
            self._own(ins, outs, sems, ci).start()
        for ci in range(len(self.chunks)):
            self._copy(ins, outs, sems, ci, 1, me, _relative(4), from_shard=True).start()
            self._copy(ins, outs, sems, ci, 2, me, _relative(2), from_shard=True).start()
            if self._halves(ci) is None:
                self._copy(ins, outs, sems, ci, 3, me, _relative(6), from_shard=True).start()
        for ci in range(len(self.chunks)):
            self._copy(ins, outs, sems, ci, 0, me, _relative(1), from_shard=True).start()

    def mid(self, ins, outs, sems):
        me = _relative(0)
        cut = [ci for ci in range(len(self.chunks)) if self._halves(ci) is not None]
        for ci in cut:
            self._copy(ins, outs, sems, ci, 1, _relative(4), me).wait_recv()
            self._pass(ins, outs, sems, ci, 0).start()
        for ci in cut:
            self._copy(ins, outs, sems, ci, 2, _relative(2), me).wait_recv()
            self._pass(ins, outs, sems, ci, 1).start()

    def finish(self, ins, outs, sems):
        me, sibling = _relative(0), _relative(1)
        n = len(self.chunks)
        for ci in range(n):
            if self._halves(ci) is None:
                for s, k in ((1, 4), (2, 2), (3, 6)):
                    self._copy(ins, outs, sems, ci, s, _relative(k), me).wait_recv()
            else:
                h0, h1 = self._halves(ci)
                self._copy(ins, outs, sems, ci, 3, _relative(6), me, rows=h0).wait_recv()
                self._copy(ins, outs, sems, ci, 4, _relative(6), me, rows=h1).wait_recv()
            for j, k in enumerate(FAR):
                self._copy(ins, outs, sems, ci, 5 + j, _relative(k), sibling).start()
        for ci in range(n):
            self._copy(ins, outs, sems, ci, 0, sibling, me).wait_recv()
            for j, k in enumerate(FAR):
                self._copy(ins, outs, sems, ci, 5 + j, _relative(k | 1), me).wait_recv()
        for ci in range(n):
            self._copy(ins, outs, sems, ci, 0, me, sibling, from_shard=True).wait_send()
            self._copy(ins, outs, sems, ci, 1, me, _relative(4), from_shard=True).wait_send()
            self._copy(ins, outs, sems, ci, 2, me, _relative(2), from_shard=True).wait_send()
            if self._halves(ci) is None:
                self._copy(ins, outs, sems, ci, 3, me, _relative(6), from_shard=True).wait_send()
            else:
                self._pass(ins, outs, sems, ci, 0).wait_send()
                self._pass(ins, outs, sems, ci, 1).wait_send()
            for j, k in enumerate(FAR):
                self._copy(ins, outs, sems, ci, 5 + j, _relative(k), sibling).wait_send()
            self._own(ins, outs, sems, ci).wait()

    def done(self, results):
        for n, buf in zip(self.names, results):
            self.owner.bufs[n] = buf


class _Scatter:
    def __init__(self, me, far_index):
        self.me, self.far_index = me, far_index
        self.sends, self.owns, self.pairs, self.sums, self.fars = {}, {}, {}, {}, {}
        self.pair_cursor, self.far_cursor = {}, {}

    def add(self, name, send, own):
        self.sends[name] = send
        self.owns[name] = own
        self.pairs[name] = self.fars[name] = None
        self.pair_cursor[name] = 0

    def _rows(self, name):
        return self.sends[name].shape[1]

    def _add_ready_pairs(self):
        for name in self.sends:
            if name not in self.sums and self.pair_cursor[name] == self._rows(name):
                self.sums[name] = _add_pairs(self.sends[name], self.pairs[name], self.far_index, name="pair_" + name)
                self.far_cursor[name] = 0

    def _side(self, us, through=None):
        self._add_ready_pairs()
        names = list(self.sends)
        if through is not None:
            names = names[:names.index(through) + 1]
        pair_chunks = [(n, self.pair_cursor[n], self._rows(n) - self.pair_cursor[n]) for n in names
                       if self.pair_cursor[n] < self._rows(n)]
        for n, _, _ in pair_chunks:
            self.pair_cursor[n] = self._rows(n)
        far_chunks = _chunks(self.sums, self.far_cursor, us, RS_US_PER_MB,
                             through if through in self.sums else None) if self.sums else []
        return _ScatterSide(self, pair_chunks, far_chunks) if pair_chunks or far_chunks else None

    def add_blocks(self, name, blocks32, blocks16):
        self.add(name, blocks16, blocks32)

    def add_cols(self, name, full32, full16):
        width = full32.shape[1] // N_DEV
        self.add(name, _blocks(full16, "cols"), lax.dynamic_slice_in_dim(full32, self.me * width, width, axis=1))

    def take(self, us):
        return self._side(us) if us >= MIN_RIDE_US else None

    def flush_pairs(self, name):
        side = self._side(0.0)
        if side is not None:
            _run_side(side, name)
        self._add_ready_pairs()

    def get(self, name):
        step = 0
        while name not in self.sums or self.far_cursor[name] < self._rows(name):
            _run_side(self._side(None, through=name), "scatter_%s_%d" % (name, step))
            step += 1
        return self.owns[name], self.pairs[name], self.fars[name]


class _ScatterSide:
    TO_SIBLING = (1, 5, 3, 7)

    def __init__(self, owner, pair_chunks, far_chunks):
        self.owner, self.pair_chunks, self.far_chunks = owner, pair_chunks, far_chunks
        self.pair_names = list(dict.fromkeys(n for n, _, _ in pair_chunks))
        self.far_names = list(dict.fromkeys(n for n, _, _ in far_chunks))
        ins = [(owner.sends[n], owner.pairs[n], (4,)) for n in self.pair_names]
        ins += [(owner.sums[n], owner.fars[n], (3,)) for n in self.far_names]
        old = [i for i, (_, buf, _) in enumerate(ins) if buf is not None]
        self.operands = [src for src, _, _ in ins] + [ins[i][1] for i in old]
        self.out_shape = [jax.ShapeDtypeStruct(slots + src.shape[1:], BF16) for src, _, slots in ins]
        self.aliases = {len(ins) + j: i for j, i in enumerate(old)}
        n_pair, n_far = 4 * len(pair_chunks), 3 * len(far_chunks)
        self.sems = [pltpu.SemaphoreType.DMA((max(n_pair, 1),)), pltpu.SemaphoreType.DMA((max(n_pair, 1),)),
                     pltpu.SemaphoreType.DMA((max(n_far, 1),)), pltpu.SemaphoreType.DMA((max(n_far, 1),))]

    def _copies(self, ins, outs, sems):
        copies = []
        for ci, (name, r0, rows) in enumerate(self.pair_chunks):
            w = self.pair_names.index(name)
            for j, k in enumerate(self.TO_SIBLING):
                copies.append(pltpu.make_async_remote_copy(
                    src_ref=ins[w].at[_index(_relative(k)), pl.ds(r0, rows)], dst_ref=outs[w].at[j, pl.ds(r0, rows)],
                    send_sem=sems[0].at[4 * ci + j], recv_sem=sems[1].at[4 * ci + j],
                    device_id=_relative(1), device_id_type=MESH))
        for ci, (name, r0, rows) in enumerate(self.far_chunks):
            w = len(self.pair_names) + self.far_names.index(name)
            for j, k in enumerate(FAR):
                copies.append(pltpu.make_async_remote_copy(
                    src_ref=ins[w].at[j, pl.ds(r0, rows)], dst_ref=outs[w].at[j, pl.ds(r0, rows)],
                    send_sem=sems[2].at[3 * ci + j], recv_sem=sems[3].at[3 * ci + j],
                    device_id=_relative(k), device_id_type=MESH))
        return copies

    def start(self, ins, outs, sems):
        for cp in self._copies(ins, outs, sems):
            cp.start()

    def mid(self, ins, outs, sems):
        pass

    def finish(self, ins, outs, sems):
        for cp in self._copies(ins, outs, sems):
            cp.wait()

    def done(self, results):
        for n, buf in zip(self.pair_names, results):
            self.owner.pairs[n] = buf
        for n, buf in zip(self.far_names, results[len(self.pair_names):]):
            self.owner.fars[n] = buf


class _Joined:
    def __init__(self, sides):
        self.sides = sides
        self.operands, self.out_shape, self.sems, self.aliases, self.spans = [], [], [], {}, []
        for s in sides:
            i0, o0, s0 = len(self.operands), len(self.out_shape), len(self.sems)
            self.operands += list(s.operands)
            self.out_shape += list(s.out_shape)
            self.sems += list(s.sems)
            self.aliases.update({i0 + i: o0 + o for i, o in s.aliases.items()})
            self.spans.append((slice(i0, len(self.operands)), slice(o0, len(self.out_shape)),
                               slice(s0, len(self.sems))))

    def start(self, ins, outs, sems):
        for s, (i, o, m) in zip(self.sides, self.spans):
            s.start(ins[i], outs[o], sems[m])

    def mid(self, ins, outs, sems):
        for s, (i, o, m) in zip(self.sides, self.spans):
            s.mid(ins[i], outs[o], sems[m])

    def finish(self, ins, outs, sems):
        for s, (i, o, m) in zip(self.sides, self.spans):
            s.finish(ins[i], outs[o], sems[m])

    def done(self, results):
        for s, (_, o, _) in zip(self.sides, self.spans):
            s.done(results[o])


def _join(*sides):
    sides = [s for s in sides if s is not None]
    if len(sides) <= 1:
        return sides[0] if sides else None
    return _Joined(sides)


def _pack_rows(vecs):
    rows = -(-sum(v.shape[0] for v in vecs) // 8) * 8
    width = max(v.shape[1] for v in vecs)

    def body(*refs):
        out = refs[-1]
        out[...] = jnp.zeros_like(out)
        r0 = 0
        for v in refs[:-1]:
            out[r0:r0 + v.shape[0], 0:v.shape[1]] = v[...]
            r0 += v.shape[0]

    vmem = pl.BlockSpec(memory_space=pltpu.VMEM)
    return pl.pallas_call(body, name="pack_small", in_specs=[vmem] * len(vecs), out_specs=vmem,
                          out_shape=jax.ShapeDtypeStruct((rows, width), F32))(*vecs)


def _sum_rows(inbox, shapes):
    def body(inbox_ref, *refs):
        outs, total = refs[:-1], refs[-1]
        acc = inbox_ref[0]
        for d in range(1, N_DEV):
            acc = acc + inbox_ref[d]
        total[...] = acc
        r0 = 0
        for o in outs:
            o[...] = total[r0:r0 + o.shape[0], 0:o.shape[1]]
            r0 += o.shape[0]

    vmem = pl.BlockSpec(memory_space=pltpu.VMEM)
    return pl.pallas_call(body, name="sum_small", in_specs=[vmem], out_specs=[vmem] * len(shapes),
                          out_shape=[jax.ShapeDtypeStruct(s, F32) for s in shapes],
                          scratch_shapes=[pltpu.VMEM(inbox.shape[1:], F32)])(inbox)


class _ShareRows:
    def __init__(self, mine):
        self.operands, self.aliases = [mine], {}
        self.out_shape = [jax.ShapeDtypeStruct((N_DEV,) + mine.shape, F32)]
        self.sems = [pltpu.SemaphoreType.DMA((N_DEV - 1,)), pltpu.SemaphoreType.DMA((N_DEV - 1,)),
                     pltpu.SemaphoreType.DMA(())]

    def _copy(self, ins, outs, sems, k, sender):
        return pltpu.make_async_remote_copy(
            src_ref=ins[0], dst_ref=outs[0].at[_index(sender)], send_sem=sems[0].at[k - 1], recv_sem=sems[1].at[k - 1],
            device_id=_relative(k), device_id_type=MESH)

    def _own(self, ins, outs, sems):
        return pltpu.make_async_copy(ins[0], outs[0].at[_index(_relative(0))], sems[2])

    def start(self, ins, outs, sems):
        self._own(ins, outs, sems).start()
        for k in range(1, N_DEV):
            self._copy(ins, outs, sems, k, _relative(0)).start()

    def mid(self, ins, outs, sems):
        pass

    def finish(self, ins, outs, sems):
        for k in range(1, N_DEV):
            self._copy(ins, outs, sems, k, _relative(k)).wait_recv()
            self._copy(ins, outs, sems, k, _relative(0)).wait_send()
        self._own(ins, outs, sems).wait()

    def done(self, results):
        self.inbox = results[0]


class _PartsToOwners:
    def __init__(self, mats):
        self.n, self.per = len(mats), mats[0].shape[0] // N_DEV
        self.operands, self.aliases = list(mats), {}
        self.out_shape = [jax.ShapeDtypeStruct((N_DEV, self.n, self.per) + mats[0].shape[1:], F32)]
        self.sems = [pltpu.SemaphoreType.DMA((self.n * (N_DEV - 1),))] * 2

    def _copies(self, ins, outs, sems):
        return [pltpu.make_async_remote_copy(
            src_ref=ins[j].at[pl.ds(self.per * _index(_relative(k)), self.per)], dst_ref=outs[0].at[k, j],
            send_sem=sems[0].at[self.n * (k - 1) + j], recv_sem=sems[1].at[self.n * (k - 1) + j],
            device_id=_relative(k), device_id_type=MESH) for k in range(1, N_DEV) for j in range(self.n)]

    def start(self, ins, outs, sems):
        for cp in self._copies(ins, outs, sems):
            cp.start()

    def mid(self, ins, outs, sems):
        pass

    def finish(self, ins, outs, sems):
        for cp in self._copies(ins, outs, sems):
            cp.wait()

    def done(self, results):
        self.stage = results[0]


def _sum_parts(mats, stage):
    n, per = len(mats), mats[0].shape[0] // N_DEV

    def body(*refs):
        stage_ref, out = refs[n], refs[n + 1]
        me = _index(_relative(0))
        for j in range(n):
            acc = refs[j][pl.ds(per * me, per)]
            for k in range(1, N_DEV):
                acc = acc + stage_ref[k, j]
            out[j] = acc

    vmem = pl.BlockSpec(memory_space=pltpu.VMEM)
    return pl.pallas_call(body, name="sum_small_parts", in_specs=[vmem] * (n + 1), out_specs=vmem,
                          out_shape=jax.ShapeDtypeStruct((n, per) + mats[0].shape[1:], F32),
                          compiler_params=pltpu.CompilerParams(vmem_limit_bytes=VMEM_LIMIT))(*mats, stage)


class _PartsToAll:
    def __init__(self, parts, rows):
        self.n, self.per = parts.shape[0], parts.shape[1]
        self.operands, self.aliases = [parts], {}
        self.out_shape = [jax.ShapeDtypeStruct((rows,) + parts.shape[2:], F32)] * self.n
        self.sems = [pltpu.SemaphoreType.DMA((self.n * (N_DEV - 1),))] * 2 + [pltpu.SemaphoreType.DMA((self.n,))]

    def _rows(self, ref, pos):
        return ref.at[pl.ds(self.per * _index(pos), self.per)]

    def _copy(self, ins, outs, sems, k, j, owner):
        return pltpu.make_async_remote_copy(
            src_ref=ins[0].at[j], dst_ref=self._rows(outs[j], owner),
            send_sem=sems[0].at[self.n * (k - 1) + j], recv_sem=sems[1].at[self.n * (k - 1) + j],
            device_id=_relative(k), device_id_type=MESH)

    def _own(self, ins, outs, sems, j):
        return pltpu.make_async_copy(ins[0].at[j], self._rows(outs[j], _relative(0)), sems[2].at[j])

    def start(self, ins, outs, sems):
        for j in range(self.n):
            self._own(ins, outs, sems, j).start()
            for k in range(1, N_DEV):
                self._copy(ins, outs, sems, k, j, _relative(0)).start()

    def mid(self, ins, outs, sems):
        pass

    def finish(self, ins, outs, sems):
        for j in range(self.n):
            for k in range(1, N_DEV):
                self._copy(ins, outs, sems, k, j, _relative(k)).wait_recv()
                self._copy(ins, outs, sems, k, j, _relative(0)).wait_send()
            self._own(ins, outs, sems, j).wait()

    def done(self, results):
        self.totals = list(results)


class _SmallSync:
    def __init__(self, vec_names, mat_names):
        self.vec_names, self.mat_names = vec_names, mat_names

    def begin(self, loss, grads):
        vecs = [loss] + [grads[n] for n in self.vec_names]
        self.shapes = [v.shape for v in vecs]
        self.mats = [_diag_blocks(grads[n]) for n in self.mat_names]
        self.share = _ShareRows(_pack_rows(vecs))
        self.to_owners = _PartsToOwners(self.mats)
        return _join(self.share, self.to_owners)

    def middle(self):
        self.sums = _sum_rows(self.share.inbox, self.shapes)
        self.to_all = _PartsToAll(_sum_parts(self.mats, self.to_owners.stage), self.mats[0].shape[0])
        return self.to_all

    def end(self):
        return self.sums[0], dict(zip(self.vec_names, self.sums[1:])), dict(zip(self.mat_names, self.to_all.totals))


def _block_diag(w):
    groups = []
    for g in range(N_RNN_GROUPS):
        placed = [jnp.pad(w[4 * g + b], ((RNN_BLOCK_W * b, RNN_BLOCK_W * (3 - b)),) * 2) for b in range(4)]
        groups.append(placed[0] + placed[1] + placed[2] + placed[3])
    return jnp.stack(groups)


def _diag_blocks(wg):
    blocks = []
    for n in range(4 * N_RNN_GROUPS):
        g, at = n // 4, RNN_BLOCK_W * (n % 4)
        blocks.append(wg[g, at:at + RNN_BLOCK_W, at:at + RNN_BLOCK_W])
    return jnp.stack(blocks)


def _heads_major(t, n_heads):
    return t.reshape(S, n_heads, HEAD_DIM).transpose(1, 0, 2)


def _heads_minor(t):
    return t.transpose(1, 0, 2).reshape(S, t.shape[0] * HEAD_DIM)


def _natural(gathered, how):
    n, r, c = gathered.shape
    if how == "rows":
        return gathered.reshape(n * r, c)
    return gathered.transpose(1, 0, 2).reshape(r, n * c)


def _blocks(full, how):
    if how == "rows":
        return full.reshape(N_DEV, full.shape[0] // N_DEV, full.shape[1])
    return full.reshape(full.shape[0], N_DEV, full.shape[1] // N_DEV).transpose(1, 0, 2)


def _cast_many(arrays, side=None):
    steps = 4

    def body(*refs):
        n = len(refs) // 2
        for src, dst in zip(refs[:n], refs[n:]):
            dst[...] = src[...].astype(dst.dtype)

    specs = [pl.BlockSpec((a.shape[0] // steps, a.shape[1]), lambda i: (i, 0)) for a in arrays]
    return _call(
        body,
        name="cast_weights",
        grid=(steps,),
        in_specs=specs,
        out_specs=specs,
        out_shape=[jax.ShapeDtypeStruct(a.shape, MXU_DTYPE) for a in arrays],
        semantics=("parallel",),
        operands=tuple(arrays),
        side=side,
    )


def _forward_backward(x2, xb, target, small, gather, scatter, sync):
    w_in = _natural(gather.get("w_in"), "cols")
    proj, projb = _mm(xb, w_in, tm=1024, tn=512, tk=D, out_dtype=(F32, MXU_DTYPE), name="proj", side=gather.take(110))

    qt = projb[:, :OFF_K].T.reshape(N_KV, GROUP, HEAD_DIM, S)
    k2, v2 = projb[:, OFF_K:OFF_V], projb[:, OFF_V:OFF_RX]
    kp = jnp.pad(_heads_major(k2, N_KV), ((0, 0), (BLOCK, 0), (0, 0)))
    vp = jnp.pad(_heads_major(v2, N_KV), ((0, 0), (BLOCK, 0), (0, 0)))
    kt = jnp.pad(k2.T.reshape(N_KV, HEAD_DIM, S), ((0, 0), (0, 0), (BLOCK, 0)))
    vt = jnp.pad(v2.T.reshape(N_KV, HEAD_DIM, S), ((0, 0), (0, 0), (BLOCK, 0)))
    sink_row = jnp.repeat(small["attn_sinks"].reshape(N_KV, 1, GROUP), BLOCK, axis=2)
    ot = _attn_fwd(qt, kp, vt, sink_row, side=gather.take(36)).reshape(D, S)

    rconv_w = _natural(gather.get("rnn_conv_w"), "cols")
    rxc = _conv_fwd(proj, OFF_RX, rconv_w, small["rnn_conv_b"], tc=512, name="rnn_conv_fwd", side=gather.take(18))
    r, i = _lru_gates(rxc, small["lru_wa"], small["lru_wi"], small["lru_ba"], small["lru_bi"], side=gather.take(33))
    h, yrin = _lru_scan_fwd(r, i, rxc, proj, small["lru_lambda"], side=gather.take(53))

    w_ap = _natural(gather.get("w_attn_proj"), "rows")
    w_rp = _natural(gather.get("w_rnn_proj"), "rows")
    y_attn = _mm(ot, w_ap, ta=True, tm=1024, tn=1024, tk=D, name="attn_proj", side=gather.take(22))
    y_rnn = _mm(yrin, w_rp, tm=1024, tn=1024, tk=D_RNN, name="rnn_proj", side=gather.take(27))
    mixin = _gate_fwd(y_attn, y_rnn, proj, small["b_gate"], side=gather.take(25))
    w_out = _natural(gather.get("w_out"), "rows")
    mix = _mm(mixin, w_out, tm=1024, tn=1024, tk=D, name="mix_out", side=gather.take(22))
    x1, x1b, xhat1, rstd1 = _ln_fwd(x2, mix, small["ln1_g"], small["ln1_b"], side=gather.take(23))

    w_up = gather.get("ffn_w_up")
    up = _mm(x1b, w_up, tm=1024, tn=768, tk=D, b_block=768, name="ffn_up", side=gather.take(58))
    w_gate = gather.get("ffn_w_gate")
    gpre = _mm(x1b, w_gate, tm=1024, tn=768, tk=D, b_block=768, name="ffn_gate", side=gather.take(58))
    fconv_w = _natural(gather.get("ffn_conv_w"), "cols")
    fin = _ffn_act_fwd(up, gpre, fconv_w, small["ffn_conv_b"], side=gather.take())
    w_down = _natural(gather.get("ffn_w_down"), "rows")
    f = _mm(fin, w_down, tm=1024, tn=1024, tk=2048, name="ffn_down")
    loss, dpre2, dpre2b, d_ln2_g, d_ln2_b = _ln_loss_bwd(x1, f, small["ln2_g"], small["ln2_b"], target)

    grads = {"ln2_g": d_ln2_g, "ln2_b": d_ln2_b}
    both = (F32, BF16)
    g32, g16 = _mm(fin, dpre2b, ta=True, tm=1024, tn=1024, tk=S, out_dtype=both, name="d_ffn_w_down")
    scatter.add_blocks("ffn_w_down", _blocks(g32, "rows"), _blocks(g16, "rows"))
    dfin = _mm(dpre2b, w_down, tb=True, tm=1024, tn=1024, tk=D, name="d_fin", side=scatter.take(57))
    dup, dgpre, grads["ffn_conv_w"], grads["ffn_conv_b"] = _ffn_act_bwd(
        dfin, up, gpre, fconv_w, small["ffn_conv_b"], side=scatter.take(85))
    g32, g16 = _mm(x1b, dup, ta=True, tm=1024, tn=768, tk=S, out_dtype=both, out_block=768, name="d_ffn_w_up",
                   side=scatter.take(57))
    scatter.add_blocks("ffn_w_up", g32, g16)
    g32, g16 = _mm(x1b, dgpre, ta=True, tm=1024, tn=768, tk=S, out_dtype=both, out_block=768, name="d_ffn_w_gate",
                   side=scatter.take(56))
    scatter.add_blocks("ffn_w_gate", g32, g16)
    dx1 = _mm(dup, w_up, tb=True, tm=1024, tn=1024, tk=768, b_block=768, name="d_x1_up", side=scatter.take(68))
    dx1 = _mm(dgpre, w_gate, tb=True, tm=1024, tn=1024, tk=768, b_block=768, add=dx1, name="d_x1_gate",
              side=scatter.take(70))
    dpre1, dpre1b, grads["ln1_g"], grads["ln1_b"] = _ln_bwd(dx1, dpre2, xhat1, rstd1, small["ln1_g"],
                                                            side=scatter.take(24))

    g32, g16 = _mm(mixin, dpre1b, ta=True, tm=1024, tn=1024, tk=S, out_dtype=both, name="d_w_out",
                   side=scatter.take(26))
    scatter.add_blocks("w_out", _blocks(g32, "rows"), _blocks(g16, "rows"))
    dmix = _mm(dpre1b, w_out, tb=True, tm=1024, tn=1024, tk=D, name="d_mixin", side=scatter.take(22))
    dya, dyr, dgl_a, dgl_r, db_a, db_r = _gate_bwd(dmix, y_attn, y_rnn, proj, small["b_gate"], side=scatter.take(36))
    grads["b_gate"] = jnp.concatenate([db_a, db_r], axis=1)
    g32, g16 = _mm(ot, dya, tm=1024, tn=1024, tk=S, out_dtype=both, name="d_w_attn_proj", side=scatter.take(38))
    scatter.add_blocks("w_attn_proj", _blocks(g32, "rows"), _blocks(g16, "rows"))
    g32, g16 = _mm(yrin, dyr, ta=True, tm=1280, tn=1024, tk=S, out_dtype=both, name="d_w_rnn_proj",
                   side=scatter.take(27))
    scatter.add_blocks("w_rnn_proj", _blocks(g32, "rows"), _blocks(g16, "rows"))
    dot_ = _mm(w_ap, dya, tb=True, tm=1024, tn=1024, tk=D, out_dtype=MXU_DTYPE, name="d_o", side=scatter.take(22))
    dyrin = _mm(dyr, w_rp, tb=True, tm=1024, tn=1280, tk=D, name="d_yrin", side=scatter.take(27))

    dry, dzr, dzi, drxc_in, grads["lru_ba"], grads["lru_bi"], grads["lru_lambda"] = _lru_scan_bwd(
        dyrin, proj, h, r, i, rxc, small["lru_lambda"], side=scatter.take(94))
    grads["lru_wa"], grads["lru_wi"] = _lru_gate_wgrad(rxc, dzr, dzi, side=scatter.take(22))
    drxc = _lru_gate_xgrad(dzr, dzi, small["lru_wa"], small["lru_wi"], drxc_in, side=scatter.take(33))
    drx, grads["rnn_conv_w"], grads["rnn_conv_b"] = _conv_bwd(drxc, proj, OFF_RX, rconv_w, tc=512,
                                                             name="rnn_conv_bwd", side=scatter.take(29))

    dqt, dk, dv, dsink = _attn_bwd(qt, kp, kt, vp, sink_row, dot_.reshape(N_KV, GROUP, HEAD_DIM, S),
                                   side=scatter.take(65))
    grads["attn_sinks"] = dsink.reshape(1, N_KV * GROUP)
    dproj = jnp.concatenate([
        dqt.reshape(D, S).T,
        _heads_minor(dk[:, BLOCK:, :]).astype(MXU_DTYPE),
        _heads_minor(dv[:, BLOCK:, :]).astype(MXU_DTYPE),
        drx, dry, dgl_a, dgl_r], axis=1)
    g32, g16 = _mm(xb, dproj, ta=True, tm=1024, tn=512, tk=S, out_dtype=both, name="d_w_in",
                   side=_join(scatter.take(110), sync.begin(loss, grads)))
    scatter.add_cols("w_in", g32, g16)
    scatter.flush_pairs("pairs_w_in")
    dx = _mm(dproj, w_in, tb=True, tm=1024, tn=1024, tk=512, add=dpre1, add_scale=ALPHA, name="d_x",
             side=_join(scatter.take(300), sync.middle()))
    return dx


SHARDED = (
    ("w_in", "cols", 128), ("w_attn_proj", "rows", 32), ("w_rnn_proj", "rows", 32), ("w_out", "rows", 32),
    ("ffn_w_up", "cols", 128), ("ffn_w_gate", "cols", 128), ("ffn_w_down", "rows", 64),
)
SMALL_REPLICATED = ("b_gate", "rnn_conv_b", "lru_wa", "lru_ba", "lru_wi", "lru_bi", "lru_lambda", "attn_sinks",
                    "ln1_g", "ln1_b", "ffn_conv_b", "ln2_g", "ln2_b")
SMALL_SHARDED = ("rnn_conv_w", "ffn_conv_w")
SMALL_MATS = ("lru_wa", "lru_wi")
WEIGHTS = ("w_in", "b_gate", "rnn_conv_w", "rnn_conv_b", "lru_wa", "lru_ba", "lru_wi", "lru_bi", "lru_lambda",
           "attn_sinks", "w_attn_proj", "w_rnn_proj", "w_out", "ln1_g", "ln1_b", "ffn_w_up", "ffn_w_gate",
           "ffn_conv_w", "ffn_conv_b", "ffn_w_down", "ln2_g", "ln2_b")


def kernel(x, w_in, b_gate, rnn_conv_w, rnn_conv_b, lru_wa, lru_ba, lru_wi, lru_bi, lru_lambda, attn_sinks, w_attn_proj, w_rnn_proj, w_out, ln1_g, ln1_b, ffn_w_up, ffn_w_gate, ffn_conv_w, ffn_conv_b, ffn_w_down, ln2_g, ln2_b, loss_target, m_w_in, m_b_gate, m_rnn_conv_w, m_rnn_conv_b, m_lru_wa, m_lru_ba, m_lru_wi, m_lru_bi, m_lru_lambda, m_attn_sinks, m_w_attn_proj, m_w_rnn_proj, m_w_out, m_ln1_g, m_ln1_b, m_ffn_w_up, m_ffn_w_gate, m_ffn_conv_w, m_ffn_conv_b, m_ffn_w_down, m_ln2_g, m_ln2_b, v_w_in, v_b_gate, v_rnn_conv_w, v_rnn_conv_b, v_lru_wa, v_lru_ba, v_lru_wi, v_lru_bi, v_lru_lambda, v_attn_sinks, v_w_attn_proj, v_w_rnn_proj, v_w_out, v_ln1_g, v_ln1_b, v_ffn_w_up, v_ffn_w_gate, v_ffn_conv_w, v_ffn_conv_b, v_ffn_w_down, v_ln2_g, v_ln2_b):
    given = dict(locals())
    wsh = {n: given[n][0] for n in WEIGHTS}
    msh = {n: given["m_" + n][0] for n in WEIGHTS}
    vsh = {n: given["v_" + n][0] for n in WEIGHTS}
    m_given = {n: given["m_" + n] for n in WEIGHTS}
    v_given = {n: given["v_" + n] for n in WEIGHTS}
    me = 4 * lax.axis_index("x") + 2 * lax.axis_index("y") + lax.axis_index("c")

    order = ("w_in", "rnn_conv_w", "ffn_conv_w", "w_attn_proj", "w_rnn_proj", "w_out", "ffn_w_up", "ffn_w_gate",
             "ffn_w_down")
    gather = _Gather({n: wsh[n] if n in SMALL_SHARDED else wsh[n].astype(MXU_DTYPE) for n in order[:3]})
    *casts, xb = _cast_many([wsh[n] for n in order[3:]] + [x[0]], side=gather.take(through="ffn_conv_w"))
    gather.add_shards(dict(zip(order[3:], casts)))
    small = {n: given[n] for n in SMALL_REPLICATED}
    small["lru_wa"] = _block_diag(wsh["lru_wa"])
    small["lru_wi"] = _block_diag(wsh["lru_wi"])
    scatter = _Scatter(me, jnp.stack([_index(_relative(k)) for k in FAR]).astype(jnp.int32))

    vec_names = tuple(n for n in SMALL_REPLICATED if n not in SMALL_MATS) + SMALL_SHARDED
    sync = _SmallSync(vec_names, SMALL_MATS)
    dx = _forward_backward(x[0], xb, loss_target[0], small, gather, scatter, sync)

    loss_total, g_small, mat_sums = sync.end()
    loss_total = loss_total.reshape(())
    for n in SMALL_SHARDED:
        width = wsh[n].shape[1]
        g_small[n] = lax.dynamic_slice_in_dim(g_small[n], me * width, width, axis=1)
    g_small = {n: g_small[n].reshape(given[n].shape) for n in vec_names}
    out = {}
    results = _adamw_many(*[[d[n] for n in vec_names] for d in (given, m_given, v_given, g_small)])
    for n, delta, nm, nv in zip(vec_names, *results):
        out[n] = (g_small[n], delta, nm, nv)
    for n in SMALL_MATS:
        g = mat_sums[n].reshape(given[n].shape)
        out[n] = (g, *_adamw_blocks(given[n], m_given[n], v_given[n], g, name="adamw_" + n))

    tile_rows = {n: tr for n, _, tr in SHARDED}
    for n in list(scatter.sends):
        own, pair, far = scatter.get(n)
        res = _reduce_adamw(wsh[n], msh[n], vsh[n], own, pair, far, me.reshape(1).astype(jnp.int32),
                            tr=tile_rows[n], name="adamw_" + n)
        out[n] = tuple(r[None] for r in res)

    outputs = [loss_total, dx[None]]
    for kind in range(4):
        outputs += [out[n][kind] for n in WEIGHTS]
    return tuple(outputs)
```

```python
import math

import jax
import jax.numpy as jnp
from jax import lax
from jax.experimental import pallas as pl
from jax.experimental.pallas import tpu as pltpu

F32 = jnp.float32
BF16 = jnp.bfloat16
MXU_DTYPE = jnp.bfloat16

N_DEV = 8
S = 2048
D = 2048
HEAD_DIM = 64
N_KV = 4
GROUP = 8
BLOCK = 128
D_KV = N_KV * HEAD_DIM
D_RNN = 2560
RNN_GROUP = 640
N_RNN_GROUPS = D_RNN // RNN_GROUP
RNN_BLOCK_W = 160
RNN_CONV_W = 4
LRU_C = 8.0
D_FF = 6144
FFN_CONV_W = 3
D_IN = 11776
OFF_K = 2048
OFF_V = 2304
OFF_RX = 2560
OFF_RY = 5120
OFF_GA = 7680
OFF_GR = 9728
LN_EPS = 1e-5
ALPHA = 2.0 ** 0.25
ADAM_LR = 0.001
ADAM_B1 = 0.9
ADAM_B2 = 0.999
ADAM_EPS = 1e-08
ADAM_WD = 0.01
ADAM_STEP = 10
NEG = -1e30
VMEM_LIMIT = 56 * 1024 * 1024
MID_RIDE_TENTHS = 6
MESH = pl.DeviceIdType.MESH
GELU_C = math.sqrt(2.0 / math.pi)


def _cparams(*sem):
    return pltpu.CompilerParams(dimension_semantics=sem or None, vmem_limit_bytes=VMEM_LIMIT)


def _call(body, *, name, grid, in_specs, out_specs, out_shape, operands, semantics, scratch_shapes=(), side=None):
    single = not isinstance(out_shape, (list, tuple))
    out_shape = [out_shape] if single else list(out_shape)
    out_specs = [out_specs] if single else list(out_specs)
    in_specs = list(in_specs)
    scratch_shapes = list(scratch_shapes)
    if side is None:
        res = pl.pallas_call(
            body, name=name, grid=grid, in_specs=in_specs, out_specs=out_specs, out_shape=out_shape,
            scratch_shapes=scratch_shapes, compiler_params=_cparams(*semantics))(*operands)
        return res[0] if single else res
    n_in, n_out, n_scr = len(in_specs), len(out_shape), len(scratch_shapes)
    s_in, s_out = len(side.operands), len(side.out_shape)
    hbm = pl.BlockSpec(memory_space=pltpu.HBM)
    steps = math.prod(grid)
    mid_step = (steps * MID_RIDE_TENTHS) // 10

    def with_copies(*refs):
        core_in, side_in = refs[:n_in], refs[n_in:n_in + s_in]
        o0 = n_in + s_in
        core_out, side_out = refs[o0:o0 + n_out], refs[o0 + n_out:o0 + n_out + s_out]
        c0 = o0 + n_out + s_out
        core_scr, sems = refs[c0:c0 + n_scr], refs[c0 + n_scr:]
        step = 0
        for d, size in enumerate(grid):
            step = step * size + pl.program_id(d)

        @pl.when(step == 0)
        def _():
            side.start(side_in, side_out, sems)

        body(*core_in, *core_out, *core_scr)

        @pl.when(step == mid_step)
        def _():
            side.mid(side_in, side_out, sems)

        @pl.when(step == steps - 1)
        def _():
            side.finish(side_in, side_out, sems)

    res = pl.pallas_call(
        with_copies, name=name, grid=grid,
        in_specs=in_specs + [hbm] * s_in, out_specs=out_specs + [hbm] * s_out,
        out_shape=out_shape + list(side.out_shape),
        scratch_shapes=scratch_shapes + list(side.sems),
        input_output_aliases={n_in + i: n_out + o for i, o in side.aliases.items()},
        compiler_params=_cparams(*(("arbitrary",) * len(grid))))(*operands, *side.operands)
    side.done(res[n_out:])
    return res[0] if single else res[:n_out]


def _run_side(side, name):
    def body(*refs):
        s_in, s_out = len(side.operands), len(side.out_shape)
        side.start(refs[:s_in], refs[s_in:s_in + s_out], refs[s_in + s_out:])
        side.mid(refs[:s_in], refs[s_in:s_in + s_out], refs[s_in + s_out:])
        side.finish(refs[:s_in], refs[s_in:s_in + s_out], refs[s_in + s_out:])

    hbm = pl.BlockSpec(memory_space=pltpu.HBM)
    res = pl.pallas_call(
        body, name=name, in_specs=[hbm] * len(side.operands), out_specs=[hbm] * len(side.out_shape),
        out_shape=list(side.out_shape), scratch_shapes=list(side.sems),
        input_output_aliases=dict(side.aliases))(*side.operands)
    side.done(res)


def _gelu(x):
    x2 = x * x
    t = jnp.tanh(GELU_C * (x + 0.044715 * x * x2))
    g = 0.5 * x * (1.0 + t)
    dg = 0.5 * (1.0 + t) + 0.5 * x * (1.0 - t * t) * (GELU_C * (1.0 + 3.0 * 0.044715 * x2))
    return g, dg


def _sigmoid(x):
    return 1.0 / (1.0 + jnp.exp(-x))


def _softplus(x):
    z = jnp.exp(-jnp.abs(x))
    small = z * (1.0 - z * (0.5 - z * (1.0 / 3.0 - 0.25 * z)))
    return jnp.maximum(x, 0.0) + jnp.where(z < 0.02, small, jnp.log(1.0 + z))


def _one_minus_exp(x):
    series = -x * (1.0 + x * (0.5 + x * (1.0 / 6.0 + x * (1.0 / 24.0))))
    return jnp.where(x > -0.03, series, 1.0 - jnp.exp(x))


def _colsum(v):
    return jnp.sum(v, axis=0, keepdims=True)


def _mm(a, b, *, tm, tn, tk, name, ta=False, tb=False, out_dtype=F32, b_block=None, out_block=None, add=None,
        add_scale=1.0, side=None):
    out_dtypes = out_dtype if isinstance(out_dtype, tuple) else (out_dtype,)
    if ta:
        k_dim, m_dim = a.shape
    else:
        m_dim, k_dim = a.shape
    if b_block is None:
        n_dim = b.shape[0] if tb else b.shape[1]
    else:
        n_dim = b.shape[1] if tb else b.shape[0] * b_block
    assert m_dim % tm == 0 and n_dim % tn == 0 and k_dim % tk == 0, (name, m_dim, n_dim, k_dim)
    nk = k_dim // tk
    dims = (((0 if ta else 1,), (1 if tb else 0,)), ((), ()))
    has_add = add is not None

    def body(*refs):
        a_ref, b_ref = refs[0], refs[1]
        add_ref = refs[2] if has_add else None
        first_out = 3 if has_add else 2
        o_refs = refs[first_out:first_out + len(out_dtypes)]

        def product():
            return lax.dot_general(a_ref[...].astype(MXU_DTYPE), b_ref[...].astype(MXU_DTYPE), dims,
                                   preferred_element_type=F32)

        def finish(acc):
            if has_add:
                acc = acc + add_scale * add_ref[...]
            for o_ref in o_refs:
                o_ref[...] = acc.astype(o_ref.dtype)

        if nk == 1:
            finish(product())
        else:
            acc_ref = refs[-1]
            k = pl.program_id(2)

            @pl.when(k == 0)
            def _():
                acc_ref[...] = jnp.zeros_like(acc_ref)

            acc_ref[...] += product()

            @pl.when(k == nk - 1)
            def _():
                finish(acc_ref[...])

    if ta:
        a_spec = pl.BlockSpec((tk, tm), lambda i, j, k: (k, i))
    else:
        a_spec = pl.BlockSpec((tm, tk), lambda i, j, k: (i, k))
    if b_block is None:
        if tb:
            b_spec = pl.BlockSpec((tn, tk), lambda i, j, k: (j, k))
        else:
            b_spec = pl.BlockSpec((tk, tn), lambda i, j, k: (k, j))
    elif tb:
        assert b_block % tk == 0
        b_spec = pl.BlockSpec((None, tn, tk), lambda i, j, k: ((k * tk) // b_block, j, ((k * tk) % b_block) // tk))
    else:
        assert b_block % tn == 0
        b_spec = pl.BlockSpec((None, tk, tn), lambda i, j, k: ((j * tn) // b_block, k, ((j * tn) % b_block) // tn))
    in_specs = [a_spec, b_spec]
    operands = [a, b]
    if has_add:
        in_specs.append(pl.BlockSpec((tm, tn), lambda i, j, k: (i, j)))
        operands.append(add)
    if out_block is None:
        out_spec = pl.BlockSpec((tm, tn), lambda i, j, k: (i, j))
        out_dims = (m_dim, n_dim)
    else:
        assert out_block % tn == 0
        out_spec = pl.BlockSpec((None, tm, tn), lambda i, j, k: ((j * tn) // out_block, i, ((j * tn) % out_block) // tn))
        out_dims = (n_dim // out_block, m_dim, out_block)
    res = _call(
        body,
        name=name,
        grid=(m_dim // tm, n_dim // tn, nk),
        in_specs=in_specs,
        out_specs=[out_spec] * len(out_dtypes),
        out_shape=[jax.ShapeDtypeStruct(out_dims, dt) for dt in out_dtypes],
        scratch_shapes=[pltpu.VMEM((tm, tn), F32)] if nk > 1 else [],
        semantics=("parallel", "parallel", "arbitrary"),
        operands=tuple(operands),
        side=side,
    )
    return res if isinstance(out_dtype, tuple) else res[0]


def _attn_bias(bias_ref, h):
    key = lax.broadcasted_iota(jnp.int32, (2 * BLOCK, GROUP * BLOCK), 0)
    col = lax.broadcasted_iota(jnp.int32, (2 * BLOCK, GROUP * BLOCK), 1)
    dist = BLOCK + (col & (BLOCK - 1)) - key
    head = h * GROUP + (col >> 7) + 1
    slope = jnp.exp(head.astype(F32) * (-0.25 * math.log(2.0)))
    bias = jnp.where((dist >= 0) & (dist < BLOCK), -slope * dist.astype(F32), NEG)
    bias_ref[1] = bias
    bias_ref[0] = jnp.where(key < BLOCK, NEG, bias)


def _attn_probs(kb, qt, bias, sink):
    s = jnp.dot(kb, qt, preferred_element_type=F32) * (HEAD_DIM ** -0.5) + bias
    m = jnp.maximum(jnp.max(s, axis=0, keepdims=True), sink)
    e = jnp.exp(s - m)
    e_sink = jnp.exp(sink - m)
    inv = 1.0 / (jnp.sum(e, axis=0, keepdims=True) + e_sink)
    return e * inv, e_sink * inv


def _heads_on_lanes(ref, r0):
    return jnp.concatenate([ref[g, :, pl.ds(r0, BLOCK)] for g in range(GROUP)], axis=1)


def _attn_fwd(qt, kp, vt, sink_row, side=None):
    cols = GROUP * BLOCK

    def body(q_ref, k_ref, vt_ref, sink_ref, o_ref, bias_ref):
        _attn_bias(bias_ref, pl.program_id(0))
        sink = sink_ref[...]

        def step(n, carry):
            r0 = pl.multiple_of(n * BLOCK, BLOCK)
            p, _ = _attn_probs(k_ref[pl.ds(r0, 2 * BLOCK), :], _heads_on_lanes(q_ref, r0),
                               bias_ref[jnp.minimum(n, 1)], sink)
            o = jnp.dot(vt_ref[:, pl.ds(r0, 2 * BLOCK)], p.astype(MXU_DTYPE), preferred_element_type=F32)
            for g in range(GROUP):
                o_ref[g, :, pl.ds(r0, BLOCK)] = o[:, g * BLOCK:(g + 1) * BLOCK].astype(o_ref.dtype)
            return carry

        lax.fori_loop(0, S // BLOCK, step, 0)

    hm = pl.BlockSpec((None, GROUP, HEAD_DIM, S), lambda h: (h, 0, 0, 0))
    return _call(
        body,
        name="attn_fwd",
        grid=(N_KV,),
        in_specs=[
            hm,
            pl.BlockSpec((None, BLOCK + S, HEAD_DIM), lambda h: (h, 0, 0)),
            pl.BlockSpec((None, HEAD_DIM, BLOCK + S), lambda h: (h, 0, 0)),
            pl.BlockSpec((None, 1, cols), lambda h: (h, 0, 0)),
        ],
        out_specs=hm,
        out_shape=jax.ShapeDtypeStruct((N_KV, GROUP, HEAD_DIM, S), MXU_DTYPE),
        scratch_shapes=[pltpu.VMEM((2, 2 * BLOCK, cols), F32)],
        semantics=("parallel",),
        operands=(qt, kp, vt, sink_row),
        side=side,
    )


def _attn_bwd(qt, kp, kt, vp, sink_row, dot_, side=None):
    cols = GROUP * BLOCK

    def body(q_ref, k_ref, kt_ref, v_ref, sink_ref, do_ref, dq_ref, dk_ref, dv_ref, dsink_ref, bias_ref):
        _attn_bias(bias_ref, pl.program_id(0))
        sink = sink_ref[...]
        dk_ref[...] = jnp.zeros_like(dk_ref)
        dv_ref[...] = jnp.zeros_like(dv_ref)
        nt = (((1,), (1,)), ((), ()))

        def step(n, sink_acc):
            r0 = pl.multiple_of(n * BLOCK, BLOCK)
            band = pl.ds(r0, 2 * BLOCK)
            qn = _heads_on_lanes(q_ref, r0)
            don = _heads_on_lanes(do_ref, r0)
            p, p_sink = _attn_probs(k_ref[band, :], qn, bias_ref[jnp.minimum(n, 1)], sink)
            dp = jnp.dot(v_ref[band, :], don, preferred_element_type=F32)
            delta = jnp.sum(p * dp, axis=0, keepdims=True)
            ds = (p * (dp - delta) * (HEAD_DIM ** -0.5)).astype(MXU_DTYPE)
            dq = jnp.dot(kt_ref[:, band], ds, preferred_element_type=F32)
            for g in range(GROUP):
                dq_ref[g, :, pl.ds(r0, BLOCK)] = dq[:, g * BLOCK:(g + 1) * BLOCK].astype(dq_ref.dtype)
            dk_ref[band, :] += lax.dot_general(ds, qn, nt, preferred_element_type=F32)
            dv_ref[band, :] += lax.dot_general(p.astype(MXU_DTYPE), don, nt, preferred_element_type=F32)
            return sink_acc - p_sink * delta

        sink_acc = lax.fori_loop(0, S // BLOCK, step, jnp.zeros((1, cols), F32))
        for g in range(GROUP):
            dsink_ref[g:g + 1, :] = jnp.sum(sink_acc[:, g * BLOCK:(g + 1) * BLOCK], axis=1, keepdims=True)

    hm = pl.BlockSpec((None, GROUP, HEAD_DIM, S), lambda h: (h, 0, 0, 0))
    kv = pl.BlockSpec((None, BLOCK + S, HEAD_DIM), lambda h: (h, 0, 0))
    return _call(
        body,
        name="attn_bwd",
        grid=(N_KV,),
        in_specs=[hm, kv, pl.BlockSpec((None, HEAD_DIM, BLOCK + S), lambda h: (h, 0, 0)), kv,
                  pl.BlockSpec((None, 1, cols), lambda h: (h, 0, 0)), hm],
        out_specs=[hm, kv, kv, pl.BlockSpec((None, GROUP, 1), lambda h: (h, 0, 0))],
        out_shape=[
            jax.ShapeDtypeStruct((N_KV, GROUP, HEAD_DIM, S), MXU_DTYPE),
            jax.ShapeDtypeStruct((N_KV, BLOCK + S, HEAD_DIM), F32),
            jax.ShapeDtypeStruct((N_KV, BLOCK + S, HEAD_DIM), F32),
            jax.ShapeDtypeStruct((N_KV, GROUP, 1), F32),
        ],
        scratch_shapes=[pltpu.VMEM((2, 2 * BLOCK, cols), F32)],
        semantics=("parallel",),
        operands=(qt, kp, kt, vp, sink_row, dot_),
        side=side,
    )


PAD = 8
CHUNK = 256


def _past_taps(xpad_ref, r0, width):
    ext = xpad_ref[pl.ds(r0, CHUNK + PAD), :]
    taps = []
    for k in range(width):
        back = width - 1 - k
        taps.append((ext if back == 0 else pltpu.roll(ext, back, 0))[PAD:, :])
    return taps


def _future_taps(xpad_ref, r0, width):
    ext = xpad_ref[pl.ds(r0, CHUNK + PAD), :]
    taps = []
    for ahead in range(width):
        taps.append((ext if ahead == 0 else pltpu.roll(ext, CHUNK + PAD - ahead, 0))[:CHUNK, :])
    return taps


def _conv_fwd(src, col0, w, b, *, tc, name, side=None):
    width, c_dim = w.shape

    def body(x_ref, w_ref, b_ref, o_ref, xpad_ref):
        xpad_ref[pl.ds(0, PAD), :] = jnp.zeros((PAD, tc), F32)
        xpad_ref[pl.ds(PAD, S), :] = x_ref[...]
        wv = w_ref[...]
        bv = b_ref[...]

        def step(ci, carry):
            r0 = pl.multiple_of(ci * CHUNK, CHUNK)
            taps = _past_taps(xpad_ref, r0, width)
            y = bv + taps[0] * wv[0:1, :]
            for k in range(1, width):
                y = y + taps[k] * wv[k:k + 1, :]
            o_ref[pl.ds(r0, CHUNK), :] = y
            return carry

        lax.fori_loop(0, S // CHUNK, step, 0)

    return _call(
        body,
        name=name,
        grid=(c_dim // tc,),
        in_specs=[
            pl.BlockSpec((S, tc), lambda j: (0, col0 // tc + j)),
            pl.BlockSpec((width, tc), lambda j: (0, j)),
            pl.BlockSpec((1, tc), lambda j: (0, j)),
        ],
        out_specs=pl.BlockSpec((S, tc), lambda j: (0, j)),
        out_shape=jax.ShapeDtypeStruct((S, c_dim), F32),
        scratch_shapes=[pltpu.VMEM((S + PAD, tc), F32)],
        semantics=("parallel",),
        operands=(src, w, b),
        side=side,
    )


def _conv_bwd(dy, src, col0, w, *, tc, name, side=None):
    width, c_dim = w.shape

    def body(dy_ref, x_ref, w_ref, dx_ref, dw_ref, db_ref, xpad_ref, dpad_ref):
        xpad_ref[pl.ds(0, PAD), :] = jnp.zeros((PAD, tc), F32)
        xpad_ref[pl.ds(PAD, S), :] = x_ref[...]
        dpad_ref[pl.ds(0, S), :] = dy_ref[...]
        dpad_ref[pl.ds(S, PAD), :] = jnp.zeros((PAD, tc), F32)
        wv = w_ref[...]

        def step(ci, acc):
            r0 = pl.multiple_of(ci * CHUNK, CHUNK)
            past = _past_taps(xpad_ref, r0, width)
            ahead = _future_taps(dpad_ref, r0, width)
            d = ahead[0]
            dx = d * wv[width - 1:width, :]
            for j in range(1, width):
                dx = dx + ahead[j] * wv[width - 1 - j:width - j, :]
            dx_ref[pl.ds(r0, CHUNK), :] = dx.astype(dx_ref.dtype)
            return tuple(acc[k] + _colsum(past[k] * d) for k in range(width)) + (acc[width] + _colsum(d),)

        zero = jnp.zeros((1, tc), F32)
        acc = lax.fori_loop(0, S // CHUNK, step, (zero,) * (width + 1))
        for k in range(width):
            dw_ref[k:k + 1, :] = acc[k]
        db_ref[...] = acc[width]

    return _call(
        body,
        name=name,
        grid=(c_dim // tc,),
        in_specs=[
            pl.BlockSpec((S, tc), lambda j: (0, j)),
            pl.BlockSpec((S, tc), lambda j: (0, col0 // tc + j)),
            pl.BlockSpec((width, tc), lambda j: (0, j)),
        ],
        out_specs=[
            pl.BlockSpec((S, tc), lambda j: (0, j)),
            pl.BlockSpec((width, tc), lambda j: (0, j)),
            pl.BlockSpec((1, tc), lambda j: (0, j)),
        ],
        out_shape=[
            jax.ShapeDtypeStruct((S, c_dim), MXU_DTYPE),
            jax.ShapeDtypeStruct((width, c_dim), F32),
            jax.ShapeDtypeStruct((1, c_dim), F32),
        ],
        scratch_shapes=[pltpu.VMEM((S + PAD, tc), F32), pltpu.VMEM((S + PAD, tc), F32)],
        semantics=("parallel",),
        operands=(dy, src, w),
        side=side,
    )


SCAN_TC = 256


def _lru_gates(rxc, wa, wi, ba, bi, side=None):
    tm = 512

    def body(x_ref, wa_ref, wi_ref, ba_ref, bi_ref, r_ref, i_ref):
        xv = x_ref[...].astype(MXU_DTYPE)
        r_ref[...] = _sigmoid(jnp.dot(xv, wa_ref[...].astype(MXU_DTYPE), preferred_element_type=F32) + ba_ref[...])
        i_ref[...] = _sigmoid(jnp.dot(xv, wi_ref[...].astype(MXU_DTYPE), preferred_element_type=F32) + bi_ref[...])

    x_spec = pl.BlockSpec((tm, RNN_GROUP), lambda g, i: (i, g))
    w_spec = pl.BlockSpec((None, RNN_GROUP, RNN_GROUP), lambda g, i: (g, 0, 0))
    b_spec = pl.BlockSpec((1, RNN_GROUP), lambda g, i: (0, g))
    return _call(
        body,
        name="lru_gates",
        grid=(N_RNN_GROUPS, S // tm),
        in_specs=[x_spec, w_spec, w_spec, b_spec, b_spec],
        out_specs=[x_spec, x_spec],
        out_shape=[jax.ShapeDtypeStruct((S, D_RNN), F32)] * 2,
        semantics=("parallel", "parallel"),
        operands=(rxc, wa, wi, ba, bi),
        side=side,
    )


def _scan_down(a, u, row):
    for d in (1, 2, 4):
        a_s = jnp.where(row >= d, pltpu.roll(a, d, 0), 1.0)
        u_s = jnp.where(row >= d, pltpu.roll(u, d, 0), 0.0)
        u = a * u_s + u
        a = a * a_s
    return a, u


def _scan_up(a, u, row):
    for d in (1, 2, 4):
        a_s = jnp.where(row < 8 - d, pltpu.roll(a, 8 - d, 0), 1.0)
        u_s = jnp.where(row < 8 - d, pltpu.roll(u, 8 - d, 0), 0.0)
        u = a * u_s + u
        a = a * a_s
    return a, u


def _lru_scan_fwd(r, i, rxc, proj, lam, side=None):
    tc = SCAN_TC

    def body(r_ref, i_ref, x_ref, ry_ref, lam_ref, h_ref, y_ref):
        rate = LRU_C * _softplus(-lam_ref[...])
        row = lax.broadcasted_iota(jnp.int32, (8, tc), 0)

        def step(ci, carry):
            r0 = pl.multiple_of(ci * 16, 16)
            log_a = -rate * r_ref[pl.ds(r0, 16), :]
            a16 = jnp.exp(log_a)
            u16 = jnp.sqrt(_one_minus_exp(2.0 * log_a)) * (i_ref[pl.ds(r0, 16), :] * x_ref[pl.ds(r0, 16), :])
            hs = []
            for half in range(2):
                a_cum, h0 = _scan_down(a16[8 * half:8 * half + 8, :], u16[8 * half:8 * half + 8, :], row)
                h = a_cum * carry + h0
                carry = jnp.broadcast_to(h[7:8, :], (8, tc))
                hs.append(h)
            h16 = jnp.concatenate(hs, axis=0)
            h_ref[pl.ds(r0, 16), :] = h16
            y_ref[pl.ds(r0, 16), :] = (h16 * _gelu(ry_ref[pl.ds(r0, 16), :])[0]).astype(y_ref.dtype)
            return carry

        lax.fori_loop(0, S // 16, step, jnp.zeros((8, tc), F32))

    col = pl.BlockSpec((S, tc), lambda j: (0, j))
    return _call(
        body,
        name="lru_scan_fwd",
        grid=(D_RNN // tc,),
        in_specs=[col, col, col, pl.BlockSpec((S, tc), lambda j: (0, OFF_RY // tc + j)),
                  pl.BlockSpec((1, tc), lambda j: (0, j))],
        out_specs=[col, col],
        out_shape=[jax.ShapeDtypeStruct((S, D_RNN), F32), jax.ShapeDtypeStruct((S, D_RNN), MXU_DTYPE)],
        semantics=("parallel",),
        operands=(r, i, rxc, proj, lam),
        side=side,
    )


def _lru_scan_bwd(dy, proj, h, r, i, rxc, lam, side=None):
    tc = SCAN_TC

    def body(dy_ref, ry_ref, h_ref, r_ref, i_ref, x_ref, lam_ref,
             dry_ref, dzr_ref, dzi_ref, dx_ref, dba_ref, dbi_ref, dlam_ref, a_ref, dh_ref, hp_ref):
        lam_v = lam_ref[...]
        rate = LRU_C * _softplus(-lam_v)
        dlam_scale = LRU_C * _sigmoid(-lam_v)
        row = lax.broadcasted_iota(jnp.int32, (8, tc), 0)
        hp_ref[pl.ds(0, PAD), :] = jnp.zeros((PAD, tc), F32)
        hp_ref[pl.ds(PAD, S), :] = h_ref[...]
        a_ref[pl.ds(S, PAD), :] = jnp.zeros((PAD, tc), F32)

        def prep(ci, carry):
            r0 = pl.multiple_of(ci * CHUNK, CHUNK)
            a_ref[pl.ds(r0, CHUNK), :] = jnp.exp(-rate * r_ref[pl.ds(r0, CHUNK), :])
            ge, dge = _gelu(ry_ref[pl.ds(r0, CHUNK), :])
            dyv = dy_ref[pl.ds(r0, CHUNK), :]
            dh_ref[pl.ds(r0, CHUNK), :] = dyv * ge
            dry_ref[pl.ds(r0, CHUNK), :] = (dyv * h_ref[pl.ds(r0, CHUNK), :] * dge).astype(dry_ref.dtype)
            return carry

        lax.fori_loop(0, S // CHUNK, prep, 0)

        def step(ci, state):
            carry, dba, dbi, dlam = state
            r0 = pl.multiple_of(S - 16 - ci * 16, 16)
            a_ext = a_ref[pl.ds(r0, 24), :]
            a_next = pltpu.roll(a_ext, 23, 0)
            h_prev = pltpu.roll(hp_ref[pl.ds(r0, 24), :], 1, 0)
            dh16 = dh_ref[pl.ds(r0, 16), :]
            gs = [None, None]
            for half in (1, 0):
                lo = 8 * half
                c_cum, g0 = _scan_up(a_next[lo:lo + 8, :], dh16[lo:lo + 8, :], row)
                g = c_cum * carry + g0
                carry = jnp.broadcast_to(g[0:1, :], (8, tc))
                gs[half] = g
            g16 = jnp.concatenate(gs, axis=0)
            a16 = a_ext[0:16, :]
            r16 = r_ref[pl.ds(r0, 16), :]
            i16 = i_ref[pl.ds(r0, 16), :]
            x16 = x_ref[pl.ds(r0, 16), :]
            a2 = a16 * a16
            sq = jnp.sqrt(_one_minus_exp(-2.0 * rate * r16))
            dx_ref[pl.ds(r0, 16), :] = g16 * sq * i16
            dzi = g16 * sq * x16 * i16 * (1.0 - i16)
            dlog_a = g16 * h_prev[8:24, :] * a16 - g16 * i16 * x16 * a2 / sq
            dzr = -rate * dlog_a * r16 * (1.0 - r16)
            dzr_ref[pl.ds(r0, 16), :] = dzr.astype(dzr_ref.dtype)
            dzi_ref[pl.ds(r0, 16), :] = dzi.astype(dzi_ref.dtype)
            return carry, dba + _colsum(dzr), dbi + _colsum(dzi), dlam + _colsum(dlog_a * r16)

        zero = jnp.zeros((1, tc), F32)
        _, dba, dbi, dlam = lax.fori_loop(0, S // 16, step, (jnp.zeros((8, tc), F32), zero, zero, zero))
        dba_ref[...] = dba
        dbi_ref[...] = dbi
        dlam_ref[...] = dlam * dlam_scale

    col = pl.BlockSpec((S, tc), lambda j: (0, j))
    vec = pl.BlockSpec((1, tc), lambda j: (0, j))
    return _call(
        body,
        name="lru_scan_bwd",
        grid=(D_RNN // tc,),
        in_specs=[col, pl.BlockSpec((S, tc), lambda j: (0, OFF_RY // tc + j)), col, col, col, col, vec],
        out_specs=[col, col, col, col, vec, vec, vec],
        out_shape=[jax.ShapeDtypeStruct((S, D_RNN), MXU_DTYPE)] * 3 + [jax.ShapeDtypeStruct((S, D_RNN), F32)]
        + [jax.ShapeDtypeStruct((1, D_RNN), F32)] * 3,
        scratch_shapes=[pltpu.VMEM((S + PAD, tc), F32), pltpu.VMEM((S, tc), F32), pltpu.VMEM((S + PAD, tc), F32)],
        semantics=("parallel",),
        operands=(dy, proj, h, r, i, rxc, lam),
        side=side,
    )


def _lru_gate_wgrad(rxc, dzr, dzi, side=None):
    def body(x_ref, dzr_ref, dzi_ref, dwa_ref, dwi_ref):
        xv = x_ref[...].astype(MXU_DTYPE)
        dims = (((0,), (0,)), ((), ()))
        dwa_ref[...] = lax.dot_general(xv, dzr_ref[...], dims, preferred_element_type=F32)
        dwi_ref[...] = lax.dot_general(xv, dzi_ref[...], dims, preferred_element_type=F32)

    col = pl.BlockSpec((S, RNN_GROUP), lambda g: (0, g))
    w_spec = pl.BlockSpec((None, RNN_GROUP, RNN_GROUP), lambda g: (g, 0, 0))
    return _call(
        body,
        name="lru_gate_wgrad",
        grid=(N_RNN_GROUPS,),
        in_specs=[col, col, col],
        out_specs=[w_spec, w_spec],
        out_shape=[jax.ShapeDtypeStruct((N_RNN_GROUPS, RNN_GROUP, RNN_GROUP), F32)] * 2,
        semantics=("parallel",),
        operands=(rxc, dzr, dzi),
        side=side,
    )


def _lru_gate_xgrad(dzr, dzi, wa, wi, dx_in, side=None):
    tm = 512

    def body(dzr_ref, dzi_ref, wa_ref, wi_ref, dx_ref, o_ref):
        dims = (((1,), (1,)), ((), ()))
        o_ref[...] = (dx_ref[...]
                      + lax.dot_general(dzr_ref[...], wa_ref[...].astype(MXU_DTYPE), dims, preferred_element_type=F32)
                      + lax.dot_general(dzi_ref[...], wi_ref[...].astype(MXU_DTYPE), dims, preferred_element_type=F32))

    x_spec = pl.BlockSpec((tm, RNN_GROUP), lambda g, i: (i, g))
    w_spec = pl.BlockSpec((None, RNN_GROUP, RNN_GROUP), lambda g, i: (g, 0, 0))
    return _call(
        body,
        name="lru_gate_xgrad",
        grid=(N_RNN_GROUPS, S // tm),
        in_specs=[x_spec, x_spec, w_spec, w_spec, x_spec],
        out_specs=x_spec,
        out_shape=jax.ShapeDtypeStruct((S, D_RNN), F32),
        semantics=("parallel", "parallel"),
        operands=(dzr, dzi, wa, wi, dx_in),
        side=side,
    )


def _gate_fwd(y_attn, y_rnn, proj, b_gate, side=None):
    t = 512

    def body(ya_ref, yr_ref, ga_ref, gr_ref, ba_ref, br_ref, o_ref):
        o_ref[...] = (_sigmoid(ga_ref[...] + ba_ref[...]) * ya_ref[...]
                      + _sigmoid(gr_ref[...] + br_ref[...]) * yr_ref[...]).astype(o_ref.dtype)

    tile = pl.BlockSpec((t, t), lambda i, j: (i, j))
    return _call(
        body,
        name="gate_fwd",
        grid=(S // t, D // t),
        in_specs=[tile, tile,
                  pl.BlockSpec((t, t), lambda i, j: (i, OFF_GA // t + j)),
                  pl.BlockSpec((t, t), lambda i, j: (i, OFF_GR // t + j)),
                  pl.BlockSpec((1, t), lambda i, j: (0, j)),
                  pl.BlockSpec((1, t), lambda i, j: (0, D // t + j))],
        out_specs=tile,
        out_shape=jax.ShapeDtypeStruct((S, D), MXU_DTYPE),
        semantics=("parallel", "parallel"),
        operands=(y_attn, y_rnn, proj, proj, b_gate, b_gate),
        side=side,
    )


def _gate_bwd(dmix, y_attn, y_rnn, proj, b_gate, side=None):
    t = 512

    def body(dm_ref, ya_ref, yr_ref, ga_ref, gr_ref, ba_ref, br_ref,
             dya_ref, dyr_ref, dga_ref, dgr_ref, dba_ref, dbr_ref):
        @pl.when(pl.program_id(1) == 0)
        def _():
            dba_ref[...] = jnp.zeros_like(dba_ref)
            dbr_ref[...] = jnp.zeros_like(dbr_ref)

        dm = dm_ref[...]
        ga = _sigmoid(ga_ref[...] + ba_ref[...])
        gr = _sigmoid(gr_ref[...] + br_ref[...])
        dya_ref[...] = (dm * ga).astype(dya_ref.dtype)
        dyr_ref[...] = (dm * gr).astype(dyr_ref.dtype)
        dga = dm * ya_ref[...] * ga * (1.0 - ga)
        dgr = dm * yr_ref[...] * gr * (1.0 - gr)
        dga_ref[...] = dga.astype(dga_ref.dtype)
        dgr_ref[...] = dgr.astype(dgr_ref.dtype)
        dba_ref[...] += _colsum(dga)
        dbr_ref[...] += _colsum(dgr)

    tile = pl.BlockSpec((t, t), lambda j, i: (i, j))
    vec = pl.BlockSpec((1, t), lambda j, i: (0, j))
    return _call(
        body,
        name="gate_bwd",
        grid=(D // t, S // t),
        in_specs=[tile, tile, tile,
                  pl.BlockSpec((t, t), lambda j, i: (i, OFF_GA // t + j)),
                  pl.BlockSpec((t, t), lambda j, i: (i, OFF_GR // t + j)),
                  vec,
                  pl.BlockSpec((1, t), lambda j, i: (0, D // t + j))],
        out_specs=[tile, tile, tile, tile, vec, vec],
        out_shape=[jax.ShapeDtypeStruct((S, D), MXU_DTYPE)] * 4 + [jax.ShapeDtypeStruct((1, D), F32)] * 2,
        semantics=("parallel", "arbitrary"),
        operands=(dmix, y_attn, y_rnn, proj, proj, b_gate, b_gate),
        side=side,
    )


LN_TM = 256


def _ln_stats(pre):
    mu = jnp.mean(pre, axis=-1, keepdims=True)
    xc = pre - mu
    rstd = lax.rsqrt(jnp.mean(xc * xc, axis=-1, keepdims=True) + LN_EPS)
    return xc * rstd, rstd


def _ln_input_grad(dy, xhat, rstd, g):
    dyg = dy * g
    return rstd * (dyg - jnp.mean(dyg, axis=-1, keepdims=True)
                   - xhat * jnp.mean(dyg * xhat, axis=-1, keepdims=True))


def _ln_fwd(res, branch, g, b, side=None):
    def body(res_ref, br_ref, g_ref, b_ref, y_ref, yb_ref, xhat_ref, rstd_ref):
        xhat, rstd = _ln_stats(ALPHA * res_ref[...] + br_ref[...])
        y = xhat * g_ref[...] + b_ref[...]
        y_ref[...] = y
        yb_ref[...] = y.astype(yb_ref.dtype)
        xhat_ref[...] = xhat
        rstd_ref[...] = rstd

    tile = pl.BlockSpec((LN_TM, D), lambda i: (i, 0))
    vec = pl.BlockSpec((1, D), lambda i: (0, 0))
    return _call(
        body,
        name="ln_fwd",
        grid=(S // LN_TM,),
        in_specs=[tile, tile, vec, vec],
        out_specs=[tile, tile, tile, pl.BlockSpec((LN_TM, 1), lambda i: (i, 0))],
        out_shape=[jax.ShapeDtypeStruct((S, D), F32), jax.ShapeDtypeStruct((S, D), MXU_DTYPE),
                   jax.ShapeDtypeStruct((S, D), F32), jax.ShapeDtypeStruct((S, 1), F32)],
        semantics=("parallel",),
        operands=(res, branch, g, b),
        side=side,
    )


def _ln_bwd(dy_a, dy_b, xhat, rstd, g, side=None):
    def body(da_ref, db_in_ref, xhat_ref, rstd_ref, g_ref, dp_ref, dpb_ref, dg_ref, db_ref):
        @pl.when(pl.program_id(0) == 0)
        def _():
            dg_ref[...] = jnp.zeros_like(dg_ref)
            db_ref[...] = jnp.zeros_like(db_ref)

        dy = da_ref[...] + ALPHA * db_in_ref[...]
        xhat = xhat_ref[...]
        dp = _ln_input_grad(dy, xhat, rstd_ref[...], g_ref[...])
        dp_ref[...] = dp
        dpb_ref[...] = dp.astype(dpb_ref.dtype)
        dg_ref[...] += _colsum(dy * xhat)
        db_ref[...] += _colsum(dy)

    tile = pl.BlockSpec((LN_TM, D), lambda i: (i, 0))
    vec = pl.BlockSpec((1, D), lambda i: (0, 0))
    return _call(
        body,
        name="ln_bwd",
        grid=(S // LN_TM,),
        in_specs=[tile, tile, tile, pl.BlockSpec((LN_TM, 1), lambda i: (i, 0)), vec],
        out_specs=[tile, tile, vec, vec],
        out_shape=[jax.ShapeDtypeStruct((S, D), F32), jax.ShapeDtypeStruct((S, D), MXU_DTYPE),
                   jax.ShapeDtypeStruct((1, D), F32), jax.ShapeDtypeStruct((1, D), F32)],
        semantics=("arbitrary",),
        operands=(dy_a, dy_b, xhat, rstd, g),
        side=side,
    )


def _ln_loss_bwd(res, branch, g, b, target, side=None):
    def body(res_ref, br_ref, g_ref, b_ref, t_ref, loss_ref, dp_ref, dpb_ref, dg_ref, db_ref):
        @pl.when(pl.program_id(0) == 0)
        def _():
            loss_ref[...] = jnp.zeros_like(loss_ref)
            dg_ref[...] = jnp.zeros_like(dg_ref)
            db_ref[...] = jnp.zeros_like(db_ref)

        xhat, rstd = _ln_stats(ALPHA * res_ref[...] + br_ref[...])
        gv = g_ref[...]
        err = xhat * gv + b_ref[...] - t_ref[...]
        loss_ref[...] += (0.5 / D) * jnp.sum(_colsum(err * err), axis=1, keepdims=True)
        dy = err * (1.0 / D)
        dp = _ln_input_grad(dy, xhat, rstd, gv)
        dp_ref[...] = dp
        dpb_ref[...] = dp.astype(dpb_ref.dtype)
        dg_ref[...] += _colsum(dy * xhat)
        db_ref[...] += _colsum(dy)

    tile = pl.BlockSpec((LN_TM, D), lambda i: (i, 0))
    vec = pl.BlockSpec((1, D), lambda i: (0, 0))
    return _call(
        body,
        name="ln_loss_bwd",
        grid=(S // LN_TM,),
        in_specs=[tile, tile, vec, vec, tile],
        out_specs=[pl.BlockSpec((1, 1), lambda i: (0, 0)), tile, tile, vec, vec],
        out_shape=[jax.ShapeDtypeStruct((1, 1), F32), jax.ShapeDtypeStruct((S, D), F32),
                   jax.ShapeDtypeStruct((S, D), MXU_DTYPE),
                   jax.ShapeDtypeStruct((1, D), F32), jax.ShapeDtypeStruct((1, D), F32)],
        semantics=("arbitrary",),
        operands=(res, branch, g, b, target),
        side=side,
    )


FFN_TC = 256


def _ffn_act_fwd(up, gpre, w, b, side=None):
    tc = FFN_TC

    def body(up_ref, x_ref, w_ref, b_ref, o_ref, xpad_ref):
        xpad_ref[pl.ds(0, PAD), :] = jnp.zeros((PAD, tc), F32)
        xpad_ref[pl.ds(PAD, S), :] = x_ref[...]
        wv = w_ref[...]
        bv = b_ref[...]

        def step(ci, carry):
            r0 = pl.multiple_of(ci * CHUNK, CHUNK)
            taps = _past_taps(xpad_ref, r0, FFN_CONV_W)
            gate = bv + taps[0] * wv[0:1, :] + taps[1] * wv[1:2, :] + taps[2] * wv[2:3, :]
            o_ref[pl.ds(r0, CHUNK), :] = (_gelu(gate)[0] * up_ref[pl.ds(r0, CHUNK), :]).astype(o_ref.dtype)
            return carry

        lax.fori_loop(0, S // CHUNK, step, 0)

    col = pl.BlockSpec((S, tc), lambda j: (0, j))
    return _call(
        body,
        name="ffn_act_fwd",
        grid=(D_FF // tc,),
        in_specs=[col, col, pl.BlockSpec((FFN_CONV_W, tc), lambda j: (0, j)), pl.BlockSpec((1, tc), lambda j: (0, j))],
        out_specs=col,
        out_shape=jax.ShapeDtypeStruct((S, D_FF), MXU_DTYPE),
        scratch_shapes=[pltpu.VMEM((S + PAD, tc), F32)],
        semantics=("parallel",),
        operands=(up, gpre, w, b),
        side=side,
    )


def _ffn_act_bwd(dfin, up, gpre, w, b, side=None):
    tc = FFN_TC
    width = FFN_CONV_W

    def body(df_ref, up_ref, x_ref, w_ref, b_ref, dup_ref, dx_ref, dw_ref, db_ref, xpad_ref, dpad_ref):
        xpad_ref[pl.ds(0, PAD), :] = jnp.zeros((PAD, tc), F32)
        xpad_ref[pl.ds(PAD, S), :] = x_ref[...]
        dpad_ref[pl.ds(S, PAD), :] = jnp.zeros((PAD, tc), F32)
        wv = w_ref[...]
        bv = b_ref[...]

        def gate_grad(ci, acc):
            r0 = pl.multiple_of(ci * CHUNK, CHUNK)
            taps = _past_taps(xpad_ref, r0, width)
            gate = bv + taps[0] * wv[0:1, :] + taps[1] * wv[1:2, :] + taps[2] * wv[2:3, :]
            ge, dge = _gelu(gate)
            df = df_ref[pl.ds(r0, CHUNK), :]
            dup_ref[pl.ds(r0, CHUNK), :] = (df * ge).astype(dup_ref.dtype)
            d = df * up_ref[pl.ds(r0, CHUNK), :] * dge
            dpad_ref[pl.ds(r0, CHUNK), :] = d
            return tuple(acc[k] + _colsum(taps[k] * d) for k in range(width)) + (acc[width] + _colsum(d),)

        zero = jnp.zeros((1, tc), F32)
        acc = lax.fori_loop(0, S // CHUNK, gate_grad, (zero,) * (width + 1))
        for k in range(width):
            dw_ref[k:k + 1, :] = acc[k]
        db_ref[...] = acc[width]

        def input_grad(ci, carry):
            r0 = pl.multiple_of(ci * CHUNK, CHUNK)
            ahead = _future_taps(dpad_ref, r0, width)
            dx = ahead[0] * wv[2:3, :] + ahead[1] * wv[1:2, :] + ahead[2] * wv[0:1, :]
            dx_ref[pl.ds(r0, CHUNK), :] = dx.astype(dx_ref.dtype)
            return carry

        lax.fori_loop(0, S // CHUNK, input_grad, 0)

    col = pl.BlockSpec((S, tc), lambda j: (0, j))
    w_spec = pl.BlockSpec((width, tc), lambda j: (0, j))
    vec = pl.BlockSpec((1, tc), lambda j: (0, j))
    return _call(
        body,
        name="ffn_act_bwd",
        grid=(D_FF // tc,),
        in_specs=[col, col, col, w_spec, vec],
        out_specs=[col, col, w_spec, vec],
        out_shape=[jax.ShapeDtypeStruct((S, D_FF), MXU_DTYPE)] * 2
        + [jax.ShapeDtypeStruct((width, D_FF), F32), jax.ShapeDtypeStruct((1, D_FF), F32)],
        scratch_shapes=[pltpu.VMEM((S + PAD, tc), F32), pltpu.VMEM((S + PAD, tc), F32)],
        semantics=("parallel",),
        operands=(dfin, up, gpre, w, b),
        side=side,
    )


def _adamw_update(w, g, m, v):
    m = ADAM_B1 * m + (1.0 - ADAM_B1) * g
    v = ADAM_B2 * v + (1.0 - ADAM_B2) * (g * g)
    m_hat = m / (1.0 - ADAM_B1 ** ADAM_STEP)
    v_hat = v / (1.0 - ADAM_B2 ** ADAM_STEP)
    delta = -ADAM_LR * (m_hat / (jnp.sqrt(v_hat) + ADAM_EPS) + ADAM_WD * w)
    return delta, m, v


def _add_pairs(send, pair, far_index, *, name):
    _, r_dim, c_dim = send.shape
    tr = r_dim // 4

    def body(far_ref, mine_ref, theirs_ref, o_ref):
        o_ref[...] = (mine_ref[...].astype(F32) + theirs_ref[...].astype(F32)).astype(o_ref.dtype)

    return pl.pallas_call(
        body,
        name=name,
        grid_spec=pltpu.PrefetchScalarGridSpec(
            num_scalar_prefetch=1,
            grid=(3, r_dim // tr),
            in_specs=[pl.BlockSpec((None, tr, c_dim), lambda j, i, far: (far[j], i, 0)),
                      pl.BlockSpec((None, tr, c_dim), lambda j, i, far: (1 + j, i, 0))],
            out_specs=pl.BlockSpec((None, tr, c_dim), lambda j, i, far: (j, i, 0)),
        ),
        out_shape=jax.ShapeDtypeStruct((3, r_dim, c_dim), BF16),
        compiler_params=_cparams("parallel", "parallel"),
    )(far_index, send, pair)


def _reduce_adamw(w, m, v, g_own, pair, far, me, *, tr, name, row0=0, earlier=None):
    r_dim, c_dim = w.shape
    rows = pair.shape[1]
    first = row0 // tr

    def body(me_ref, w_ref, m_ref, v_ref, g_ref, pair_ref, far_ref, *refs):
        grad_ref, delta_ref, nm_ref, nv_ref = refs[-4:]
        g = g_ref[...] + pair_ref[...].astype(F32)
        for j in range(3):
            g = g + far_ref[j].astype(F32)
        delta, nm, nv = _adamw_update(w_ref[...], g, m_ref[...], v_ref[...])
        grad_ref[...] = g
        delta_ref[...] = delta
        nm_ref[...] = nm
        nv_ref[...] = nv

    tile = pl.BlockSpec((tr, c_dim), lambda i, me: (first + i, 0))
    if g_own.ndim == 3:
        own_spec = pl.BlockSpec((None, tr, c_dim), lambda i, me: (me[0], i, 0))
    else:
        own_spec = pl.BlockSpec((tr, c_dim), lambda i, me: (i, 0))
    earlier = list(earlier or ())
    return pl.pallas_call(
        body,
        name=name,
        grid_spec=pltpu.PrefetchScalarGridSpec(
            num_scalar_prefetch=1,
            grid=(rows // tr,),
            in_specs=[tile, tile, tile, own_spec, pl.BlockSpec((None, tr, c_dim), lambda i, me: (0, i, 0)),
                      pl.BlockSpec((3, tr, c_dim), lambda i, me: (0, i, 0))]
            + [pl.BlockSpec(memory_space=pl.ANY)] * len(earlier),
            out_specs=[tile] * 4,
        ),
        out_shape=[jax.ShapeDtypeStruct((r_dim, c_dim), F32)] * 4,
        input_output_aliases={7 + k: k for k in range(len(earlier))},
        compiler_params=_cparams("parallel"),
    )(me, w, m, v, g_own, pair, far, *earlier)


def _adamw_many(ws, ms, vs, gs):
    n = len(ws)

    def body(*refs):
        for i in range(n):
            delta, nm, nv = _adamw_update(refs[i][...], refs[3 * n + i][...], refs[n + i][...], refs[2 * n + i][...])
            refs[4 * n + i][...] = delta
            refs[5 * n + i][...] = nm
            refs[6 * n + i][...] = nv

    vmem = pl.BlockSpec(memory_space=pltpu.VMEM)
    res = pl.pallas_call(
        body,
        name="adamw_small",
        in_specs=[vmem] * (4 * n),
        out_specs=[vmem] * (3 * n),
        out_shape=[jax.ShapeDtypeStruct(w.shape, F32) for w in ws] * 3,
        compiler_params=pltpu.CompilerParams(vmem_limit_bytes=VMEM_LIMIT),
    )(*ws, *ms, *vs, *gs)
    return res[:n], res[n:2 * n], res[2 * n:]


def _adamw_blocks(w, m, v, g, *, name, side=None):
    per = 2

    def body(w_ref, m_ref, v_ref, g_ref, delta_ref, nm_ref, nv_ref):
        delta, nm, nv = _adamw_update(w_ref[...], g_ref[...], m_ref[...], v_ref[...])
        delta_ref[...] = delta
        nm_ref[...] = nm
        nv_ref[...] = nv

    tile = pl.BlockSpec((1, per) + w.shape[2:], lambda i: (0, i, 0, 0))
    return _call(
        body,
        name=name,
        grid=(w.shape[1] // per,),
        in_specs=[tile] * 4,
        out_specs=[tile] * 3,
        out_shape=[jax.ShapeDtypeStruct(w.shape, F32)] * 3,
        semantics=("parallel",),
        operands=(w, m, v, g),
        side=side,
    )


def _coords():
    return lax.axis_index("x"), lax.axis_index("y"), lax.axis_index("c")


def _flip(coord, bit):
    return 1 - coord if bit else coord


def _relative(k):
    x, y, c = _coords()
    return _flip(x, k & 4), _flip(y, k & 2), _flip(c, k & 1)


def _index(pos):
    return 4 * pos[0] + 2 * pos[1] + pos[2]


FAR = (4, 2, 6)
AG_US_PER_MB = 38.0
RS_US_PER_MB = 46.0
MIN_RIDE_US = 30.0
MIN_GATHER_RIDE_US = 22.0
ROW_ALIGN = 32


def _chunks(items, cursor, us, us_per_mb, through=None):
    budget = float("inf") if us is None else us / us_per_mb * 2 ** 20
    names = list(items)
    if through is not None:
        names = names[:names.index(through) + 1]
    chunks = []
    for name in names:
        arr = items[name]
        r_dim, c_dim = arr.shape[-2:]
        row_bytes = c_dim * arr.dtype.itemsize
        while cursor[name] < r_dim and budget > 0:
            rows = r_dim - cursor[name]
            if r_dim > ROW_ALIGN and budget < rows * row_bytes:
                rows = min(rows, max(ROW_ALIGN, int(budget // row_bytes) // ROW_ALIGN * ROW_ALIGN))
            chunks.append((name, cursor[name], rows))
            cursor[name] += rows
            budget -= rows * row_bytes
    return chunks


class _Gather:
    def __init__(self, shards):
        self.shards, self.bufs, self.cursor = {}, {}, {}
        self.add_shards(shards)

    def add_shards(self, shards):
        for n, shard in shards.items():
            self.shards[n], self.bufs[n], self.cursor[n] = shard, None, 0

    def take(self, us=None, through=None):
        if us is not None and us < MIN_GATHER_RIDE_US:
            return None
        chunks = _chunks(self.shards, self.cursor, us, AG_US_PER_MB, through)
        return _GatherSide(self, chunks) if chunks else None

    def get(self, name):
        chunks = _chunks(self.shards, self.cursor, None, AG_US_PER_MB, through=name)
        if chunks:
            _run_side(_GatherSide(self, chunks), "gather_" + name)
        return self.bufs[name]


class _GatherSide:
    SEMS = 8

    def __init__(self, owner, chunks):
        self.owner, self.chunks = owner, chunks
        self.names = list(dict.fromkeys(n for n, _, _ in chunks))
        old = [n for n in self.names if owner.bufs[n] is not None]
        self.operands = [owner.shards[n] for n in self.names] + [owner.bufs[n] for n in old]
        self.out_shape = [jax.ShapeDtypeStruct((N_DEV,) + owner.shards[n].shape, owner.shards[n].dtype)
                          for n in self.names]
        self.aliases = {len(self.names) + i: self.names.index(n) for i, n in enumerate(old)}
        self.sems = [pltpu.SemaphoreType.DMA((self.SEMS * len(chunks),)),
                     pltpu.SemaphoreType.DMA((self.SEMS * len(chunks),)), pltpu.SemaphoreType.DMA((len(chunks),))]

    def _halves(self, ci):
        _, r0, rows = self.chunks[ci]
        if rows % ROW_ALIGN:
            return None
        return (r0, rows // 2), (r0 + rows // 2, rows // 2)

    def _copy(self, ins, outs, sems, ci, s, block, to, rows=None, from_shard=False):
        name, r0, n = self.chunks[ci]
        if rows is not None:
            r0, n = rows
        w = self.names.index(name)
        slot = outs[w].at[_index(block), pl.ds(r0, n)]
        return pltpu.make_async_remote_copy(
            src_ref=ins[w].at[pl.ds(r0, n)] if from_shard else slot, dst_ref=slot,
            send_sem=sems[0].at[self.SEMS * ci + s], recv_sem=sems[1].at[self.SEMS * ci + s],
            device_id=to, device_id_type=MESH)

    def _own(self, ins, outs, sems, ci):
        name, r0, rows = self.chunks[ci]
        w = self.names.index(name)
        return pltpu.make_async_copy(ins[w].at[pl.ds(r0, rows)], outs[w].at[_index(_relative(0)), pl.ds(r0, rows)],
                                     sems[2].at[ci])

    def _pass(self, ins, outs, sems, ci, which):
        source, target = ((4, 2), (2, 4))[which]
        return self._copy(ins, outs, sems, ci, 3 + which, _relative(source), _relative(target),
                          rows=self._halves(ci)[which])

    def start(self, ins, outs, sems):
        me = _relative(0)
        for ci in range(len(self.chunks)):
            self._own(ins, outs, sems, ci).start()
        for ci in range(len(self.chunks)):
            self._copy(ins, outs, sems, ci, 1, me, _relative(4), from_shard=True).start()
            self._copy(ins, outs, sems, ci, 2, me, _relative(2), from_shard=True).start()
            if self._halves(ci) is None:
                self._copy(ins, outs, sems, ci, 3, me, _relative(6), from_shard=True).start()
        for ci in range(len(self.chunks)):
            self._copy(ins, outs, sems, ci, 0, me, _relative(1), from_shard=True).start()

    def mid(self, ins, outs, sems):
        me = _relative(0)
        cut = [ci for ci in range(len(self.chunks)) if self._halves(ci) is not None]
        for ci in cut:
            self._copy(ins, outs, sems, ci, 1, _relative(4), me).wait_recv()
            self._pass(ins, outs, sems, ci, 0).start()
        for ci in cut:
            self._copy(ins, outs, sems, ci, 2, _relative(2), me).wait_recv()
            self._pass(ins, outs, sems, ci, 1).start()

    def finish(self, ins, outs, sems):
        me, sibling = _relative(0), _relative(1)
        n = len(self.chunks)
        for ci in range(n):
            if self._halves(ci) is None:
                for s, k in ((1, 4), (2, 2), (3, 6)):
                    self._copy(ins, outs, sems, ci, s, _relative(k), me).wait_recv()
            else:
                h0, h1 = self._halves(ci)
                self._copy(ins, outs, sems, ci, 3, _relative(6), me, rows=h0).wait_recv()
                self._copy(ins, outs, sems, ci, 4, _relative(6), me, rows=h1).wait_recv()
            for j, k in enumerate(FAR):
                self._copy(ins, outs, sems, ci, 5 + j, _relative(k), sibling).start()
        for ci in range(n):
            self._copy(ins, outs, sems, ci, 0, sibling, me).wait_recv()
            for j, k in enumerate(FAR):
                self._copy(ins, outs, sems, ci, 5 + j, _relative(k | 1), me).wait_recv()
        for ci in range(n):
            self._copy(ins, outs, sems, ci, 0, me, sibling, from_shard=True).wait_send()
            self._copy(ins, outs, sems, ci, 1, me, _relative(4), from_shard=True).wait_send()
            self._copy(ins, outs, sems, ci, 2, me, _relative(2), from_shard=True).wait_send()
            if self._halves(ci) is None:
                self._copy(ins, outs, sems, ci, 3, me, _relative(6), from_shard=True).wait_send()
            else:
                self._pass(ins, outs, sems, ci, 0).wait_send()
                self._pass(ins, outs, sems, ci, 1).wait_send()
            for j, k in enumerate(FAR):
                self._copy(ins, outs, sems, ci, 5 + j, _relative(k), sibling).wait_send()
            self._own(ins, outs, sems, ci).wait()

    def done(self, results):
        for n, buf in zip(self.names, results):
            self.owner.bufs[n] = buf


class _Scatter:
    def __init__(self, me, far_index):
        self.me, self.far_index = me, far_index
        self.sends, self.owns, self.pairs, self.sums, self.fars = {}, {}, {}, {}, {}
        self.pair_cursor, self.far_cursor = {}, {}

    def add(self, name, send, own):
        self.sends[name] = send
        self.owns[name] = own
        self.pairs[name] = self.fars[name] = None
        self.pair_cursor[name] = 0

    def _rows(self, name):
        return self.sends[name].shape[1]

    def _add_ready_pairs(self):
        for name in self.sends:
            if name not in self.sums and self.pair_cursor[name] == self._rows(name):
                self.sums[name] = _add_pairs(self.sends[name], self.pairs[name], self.far_index, name="pair_" + name)
                self.far_cursor[name] = 0

    def _side(self, us, through=None):
        self._add_ready_pairs()
        names = list(self.sends)
        if through is not None:
            names = names[:names.index(through) + 1]
        pair_chunks = [(n, self.pair_cursor[n], self._rows(n) - self.pair_cursor[n]) for n in names
                       if self.pair_cursor[n] < self._rows(n)]
        for n, _, _ in pair_chunks:
            self.pair_cursor[n] = self._rows(n)
        far_chunks = _chunks(self.sums, self.far_cursor, us, RS_US_PER_MB,
                             through if through in self.sums else None) if self.sums else []
        return _ScatterSide(self, pair_chunks, far_chunks) if pair_chunks or far_chunks else None

    def add_blocks(self, name, blocks32, blocks16):
        self.add(name, blocks16, blocks32)

    def add_cols(self, name, full32, full16):
        width = full32.shape[1] // N_DEV
        self.add(name, _blocks(full16, "cols"), lax.dynamic_slice_in_dim(full32, self.me * width, width, axis=1))

    def take(self, us):
        return self._side(us) if us >= MIN_RIDE_US else None

    def flush_pairs(self, name):
        side = self._side(0.0)
        if side is not None:
            _run_side(side, name)
        self._add_ready_pairs()

    def get(self, name):
        step = 0
        while name not in self.sums or self.far_cursor[name] < self._rows(name):
            _run_side(self._side(None, through=name), "scatter_%s_%d" % (name, step))
            step += 1
        return self.owns[name], self.pairs[name], self.fars[name]


class _ScatterSide:
    TO_SIBLING = (1, 5, 3, 7)

    def __init__(self, owner, pair_chunks, far_chunks):
        self.owner, self.pair_chunks, self.far_chunks = owner, pair_chunks, far_chunks
        self.pair_names = list(dict.fromkeys(n for n, _, _ in pair_chunks))
        self.far_names = list(dict.fromkeys(n for n, _, _ in far_chunks))
        ins = [(owner.sends[n], owner.pairs[n], (4,)) for n in self.pair_names]
        ins += [(owner.sums[n], owner.fars[n], (3,)) for n in self.far_names]
        old = [i for i, (_, buf, _) in enumerate(ins) if buf is not None]
        self.operands = [src for src, _, _ in ins] + [ins[i][1] for i in old]
        self.out_shape = [jax.ShapeDtypeStruct(slots + src.shape[1:], BF16) for src, _, slots in ins]
        self.aliases = {len(ins) + j: i for j, i in enumerate(old)}
        n_pair, n_far = 4 * len(pair_chunks), 3 * len(far_chunks)
        self.sems = [pltpu.SemaphoreType.DMA((max(n_pair, 1),)), pltpu.SemaphoreType.DMA((max(n_pair, 1),)),
                     pltpu.SemaphoreType.DMA((max(n_far, 1),)), pltpu.SemaphoreType.DMA((max(n_far, 1),))]

    def _copies(self, ins, outs, sems):
        copies = []
        for ci, (name, r0, rows) in enumerate(self.pair_chunks):
            w = self.pair_names.index(name)
            for j, k in enumerate(self.TO_SIBLING):
                copies.append(pltpu.make_async_remote_copy(
                    src_ref=ins[w].at[_index(_relative(k)), pl.ds(r0, rows)], dst_ref=outs[w].at[j, pl.ds(r0, rows)],
                    send_sem=sems[0].at[4 * ci + j], recv_sem=sems[1].at[4 * ci + j],
                    device_id=_relative(1), device_id_type=MESH))
        for ci, (name, r0, rows) in enumerate(self.far_chunks):
            w = len(self.pair_names) + self.far_names.index(name)
            for j, k in enumerate(FAR):
                copies.append(pltpu.make_async_remote_copy(
                    src_ref=ins[w].at[j, pl.ds(r0, rows)], dst_ref=outs[w].at[j, pl.ds(r0, rows)],
                    send_sem=sems[2].at[3 * ci + j], recv_sem=sems[3].at[3 * ci + j],
                    device_id=_relative(k), device_id_type=MESH))
        return copies

    def start(self, ins, outs, sems):
        for cp in self._copies(ins, outs, sems):
            cp.start()

    def mid(self, ins, outs, sems):
        pass

    def finish(self, ins, outs, sems):
        for cp in self._copies(ins, outs, sems):
            cp.wait()

    def done(self, results):
        for n, buf in zip(self.pair_names, results):
            self.owner.pairs[n] = buf
        for n, buf in zip(self.far_names, results[len(self.pair_names):]):
            self.owner.fars[n] = buf


class _Joined:
    def __init__(self, sides):
        self.sides = sides
        self.operands, self.out_shape, self.sems, self.aliases, self.spans = [], [], [], {}, []
        for s in sides:
            i0, o0, s0 = len(self.operands), len(self.out_shape), len(self.sems)
            self.operands += list(s.operands)
            self.out_shape += list(s.out_shape)
            self.sems += list(s.sems)
            self.aliases.update({i0 + i: o0 + o for i, o in s.aliases.items()})
            self.spans.append((slice(i0, len(self.operands)), slice(o0, len(self.out_shape)),
                               slice(s0, len(self.sems))))

    def start(self, ins, outs, sems):
        for s, (i, o, m) in zip(self.sides, self.spans):
            s.start(ins[i], outs[o], sems[m])

    def mid(self, ins, outs, sems):
        for s, (i, o, m) in zip(self.sides, self.spans):
            s.mid(ins[i], outs[o], sems[m])

    def finish(self, ins, outs, sems):
        for s, (i, o, m) in zip(self.sides, self.spans):
            s.finish(ins[i], outs[o], sems[m])

    def done(self, results):
        for s, (_, o, _) in zip(self.sides, self.spans):
            s.done(results[o])


def _join(*sides):
    sides = [s for s in sides if s is not None]
    if len(sides) <= 1:
        return sides[0] if sides else None
    return _Joined(sides)


def _pack_rows(vecs):
    rows = -(-sum(v.shape[0] for v in vecs) // 8) * 8
    width = max(v.shape[1] for v in vecs)

    def body(*refs):
        out = refs[-1]
        out[...] = jnp.zeros_like(out)
        r0 = 0
        for v in refs[:-1]:
            out[r0:r0 + v.shape[0], 0:v.shape[1]] = v[...]
            r0 += v.shape[0]

    vmem = pl.BlockSpec(memory_space=pltpu.VMEM)
    return pl.pallas_call(body, name="pack_small", in_specs=[vmem] * len(vecs), out_specs=vmem,
                          out_shape=jax.ShapeDtypeStruct((rows, width), F32))(*vecs)


def _sum_rows(inbox, shapes):
    def body(inbox_ref, *refs):
        outs, total = refs[:-1], refs[-1]
        acc = inbox_ref[0]
        for d in range(1, N_DEV):
            acc = acc + inbox_ref[d]
        total[...] = acc
        r0 = 0
        for o in outs:
            o[...] = total[r0:r0 + o.shape[0], 0:o.shape[1]]
            r0 += o.shape[0]

    vmem = pl.BlockSpec(memory_space=pltpu.VMEM)
    return pl.pallas_call(body, name="sum_small", in_specs=[vmem], out_specs=[vmem] * len(shapes),
                          out_shape=[jax.ShapeDtypeStruct(s, F32) for s in shapes],
                          scratch_shapes=[pltpu.VMEM(inbox.shape[1:], F32)])(inbox)


class _ShareRows:
    def __init__(self, mine):
        self.operands, self.aliases = [mine], {}
        self.out_shape = [jax.ShapeDtypeStruct((N_DEV,) + mine.shape, F32)]
        self.sems = [pltpu.SemaphoreType.DMA((N_DEV - 1,)), pltpu.SemaphoreType.DMA((N_DEV - 1,)),
                     pltpu.SemaphoreType.DMA(())]

    def _copy(self, ins, outs, sems, k, sender):
        return pltpu.make_async_remote_copy(
            src_ref=ins[0], dst_ref=outs[0].at[_index(sender)], send_sem=sems[0].at[k - 1], recv_sem=sems[1].at[k - 1],
            device_id=_relative(k), device_id_type=MESH)

    def _own(self, ins, outs, sems):
        return pltpu.make_async_copy(ins[0], outs[0].at[_index(_relative(0))], sems[2])

    def start(self, ins, outs, sems):
        self._own(ins, outs, sems).start()
        for k in range(1, N_DEV):
            self._copy(ins, outs, sems, k, _relative(0)).start()

    def mid(self, ins, outs, sems):
        pass

    def finish(self, ins, outs, sems):
        for k in range(1, N_DEV):
            self._copy(ins, outs, sems, k, _relative(k)).wait_recv()
            self._copy(ins, outs, sems, k, _relative(0)).wait_send()
        self._own(ins, outs, sems).wait()

    def done(self, results):
        self.inbox = results[0]


class _PartsToOwners:
    def __init__(self, mats):
        self.n, self.per = len(mats), mats[0].shape[0] // N_DEV
        self.operands, self.aliases = list(mats), {}
        self.out_shape = [jax.ShapeDtypeStruct((N_DEV, self.n, self.per) + mats[0].shape[1:], F32)]
        self.sems = [pltpu.SemaphoreType.DMA((self.n * (N_DEV - 1),))] * 2

    def _copies(self, ins, outs, sems):
        return [pltpu.make_async_remote_copy(
            src_ref=ins[j].at[pl.ds(self.per * _index(_relative(k)), self.per)], dst_ref=outs[0].at[k, j],
            send_sem=sems[0].at[self.n * (k - 1) + j], recv_sem=sems[1].at[self.n * (k - 1) + j],
            device_id=_relative(k), device_id_type=MESH) for k in range(1, N_DEV) for j in range(self.n)]

    def start(self, ins, outs, sems):
        for cp in self._copies(ins, outs, sems):
            cp.start()

    def mid(self, ins, outs, sems):
        pass

    def finish(self, ins, outs, sems):
        for cp in self._copies(ins, outs, sems):
            cp.wait()

    def done(self, results):
        self.stage = results[0]


def _sum_parts(mats, stage):
    n, per = len(mats), mats[0].shape[0] // N_DEV

    def body(*refs):
        stage_ref, out = refs[n], refs[n + 1]
        me = _index(_relative(0))
        for j in range(n):
            acc = refs[j][pl.ds(per * me, per)]
            for k in range(1, N_DEV):
                acc = acc + stage_ref[k, j]
            out[j] = acc

    vmem = pl.BlockSpec(memory_space=pltpu.VMEM)
    return pl.pallas_call(body, name="sum_small_parts", in_specs=[vmem] * (n + 1), out_specs=vmem,
                          out_shape=jax.ShapeDtypeStruct((n, per) + mats[0].shape[1:], F32),
                          compiler_params=pltpu.CompilerParams(vmem_limit_bytes=VMEM_LIMIT))(*mats, stage)


class _PartsToAll:
    def __init__(self, parts, rows):
        self.n, self.per = parts.shape[0], parts.shape[1]
        self.operands, self.aliases = [parts], {}
        self.out_shape = [jax.ShapeDtypeStruct((rows,) + parts.shape[2:], F32)] * self.n
        self.sems = [pltpu.SemaphoreType.DMA((self.n * (N_DEV - 1),))] * 2 + [pltpu.SemaphoreType.DMA((self.n,))]

    def _rows(self, ref, pos):
        return ref.at[pl.ds(self.per * _index(pos), self.per)]

    def _copy(self, ins, outs, sems, k, j, owner):
        return pltpu.make_async_remote_copy(
            src_ref=ins[0].at[j], dst_ref=self._rows(outs[j], owner),
            send_sem=sems[0].at[self.n * (k - 1) + j], recv_sem=sems[1].at[self.n * (k - 1) + j],
            device_id=_relative(k), device_id_type=MESH)

    def _own(self, ins, outs, sems, j):
        return pltpu.make_async_copy(ins[0].at[j], self._rows(outs[j], _relative(0)), sems[2].at[j])

    def start(self, ins, outs, sems):
        for j in range(self.n):
            self._own(ins, outs, sems, j).start()
            for k in range(1, N_DEV):
                self._copy(ins, outs, sems, k, j, _relative(0)).start()

    def mid(self, ins, outs, sems):
        pass

    def finish(self, ins, outs, sems):
        for j in range(self.n):
            for k in range(1, N_DEV):
                self._copy(ins, outs, sems, k, j, _relative(k)).wait_recv()
                self._copy(ins, outs, sems, k, j, _relative(0)).wait_send()
            self._own(ins, outs, sems, j).wait()

    def done(self, results):
        self.totals = list(results)


class _SmallSync:
    def __init__(self, vec_names, mat_names):
        self.vec_names, self.mat_names = vec_names, mat_names

    def begin(self, loss, grads):
        vecs = [loss] + [grads[n] for n in self.vec_names]
        self.shapes = [v.shape for v in vecs]
        self.mats = [_diag_blocks(grads[n]) for n in self.mat_names]
        self.share = _ShareRows(_pack_rows(vecs))
        self.to_owners = _PartsToOwners(self.mats)
        return _join(self.share, self.to_owners)

    def middle(self):
        self.sums = _sum_rows(self.share.inbox, self.shapes)
        self.to_all = _PartsToAll(_sum_parts(self.mats, self.to_owners.stage), self.mats[0].shape[0])
        return self.to_all

    def end(self):
        return self.sums[0], dict(zip(self.vec_names, self.sums[1:])), dict(zip(self.mat_names, self.to_all.totals))


def _block_diag(w):
    groups = []
    for g in range(N_RNN_GROUPS):
        placed = [jnp.pad(w[4 * g + b], ((RNN_BLOCK_W * b, RNN_BLOCK_W * (3 - b)),) * 2) for b in range(4)]
        groups.append(placed[0] + placed[1] + placed[2] + placed[3])
    return jnp.stack(groups)


def _diag_blocks(wg):
    blocks = []
    for n in range(4 * N_RNN_GROUPS):
        g, at = n // 4, RNN_BLOCK_W * (n % 4)
        blocks.append(wg[g, at:at + RNN_BLOCK_W, at:at + RNN_BLOCK_W])
    return jnp.stack(blocks)


def _heads_major(t, n_heads):
    return t.reshape(S, n_heads, HEAD_DIM).transpose(1, 0, 2)


def _heads_minor(t):
    return t.transpose(1, 0, 2).reshape(S, t.shape[0] * HEAD_DIM)


def _natural(gathered, how):
    n, r, c = gathered.shape
    if how == "rows":
        return gathered.reshape(n * r, c)
    return gathered.transpose(1, 0, 2).reshape(r, n * c)


def _blocks(full, how):
    if how == "rows":
        return full.reshape(N_DEV, full.shape[0] // N_DEV, full.shape[1])
    return full.reshape(full.shape[0], N_DEV, full.shape[1] // N_DEV).transpose(1, 0, 2)


def _cast_many(arrays, side=None):
    steps = 4

    def body(*refs):
        n = len(refs) // 2
        for src, dst in zip(refs[:n], refs[n:]):
            dst[...] = src[...].astype(dst.dtype)

    specs = [pl.BlockSpec((a.shape[0] // steps, a.shape[1]), lambda i: (i, 0)) for a in arrays]
    return _call(
        body,
        name="cast_weights",
        grid=(steps,),
        in_specs=specs,
        out_specs=specs,
        out_shape=[jax.ShapeDtypeStruct(a.shape, MXU_DTYPE) for a in arrays],
        semantics=("parallel",),
        operands=tuple(arrays),
        side=side,
    )


def _forward_backward(x2, xb, target, small, gather, scatter, sync):
    w_in = _natural(gather.get("w_in"), "cols")
    proj, projb = _mm(xb, w_in, tm=1024, tn=512, tk=D, out_dtype=(F32, MXU_DTYPE), name="proj", side=gather.take(110))

    qt = projb[:, :OFF_K].T.reshape(N_KV, GROUP, HEAD_DIM, S)
    k2, v2 = projb[:, OFF_K:OFF_V], projb[:, OFF_V:OFF_RX]
    kp = jnp.pad(_heads_major(k2, N_KV), ((0, 0), (BLOCK, 0), (0, 0)))
    vp = jnp.pad(_heads_major(v2, N_KV), ((0, 0), (BLOCK, 0), (0, 0)))
    kt = jnp.pad(k2.T.reshape(N_KV, HEAD_DIM, S), ((0, 0), (0, 0), (BLOCK, 0)))
    vt = jnp.pad(v2.T.reshape(N_KV, HEAD_DIM, S), ((0, 0), (0, 0), (BLOCK, 0)))
    sink_row = jnp.repeat(small["attn_sinks"].reshape(N_KV, 1, GROUP), BLOCK, axis=2)
    ot = _attn_fwd(qt, kp, vt, sink_row, side=gather.take(36)).reshape(D, S)

    rconv_w = _natural(gather.get("rnn_conv_w"), "cols")
    rxc = _conv_fwd(proj, OFF_RX, rconv_w, small["rnn_conv_b"], tc=512, name="rnn_conv_fwd", side=gather.take(18))
    r, i = _lru_gates(rxc, small["lru_wa"], small["lru_wi"], small["lru_ba"], small["lru_bi"], side=gather.take(33))
    h, yrin = _lru_scan_fwd(r, i, rxc, proj, small["lru_lambda"], side=gather.take(53))

    w_ap = _natural(gather.get("w_attn_proj"), "rows")
    w_rp = _natural(gather.get("w_rnn_proj"), "rows")
    y_attn = _mm(ot, w_ap, ta=True, tm=1024, tn=1024, tk=D, name="attn_proj", side=gather.take(22))
    y_rnn = _mm(yrin, w_rp, tm=1024, tn=1024, tk=D_RNN, name="rnn_proj", side=gather.take(27))
    mixin = _gate_fwd(y_attn, y_rnn, proj, small["b_gate"], side=gather.take(25))
    w_out = _natural(gather.get("w_out"), "rows")
    mix = _mm(mixin, w_out, tm=1024, tn=1024, tk=D, name="mix_out", side=gather.take(22))
    x1, x1b, xhat1, rstd1 = _ln_fwd(x2, mix, small["ln1_g"], small["ln1_b"], side=gather.take(23))

    w_up = gather.get("ffn_w_up")
    up = _mm(x1b, w_up, tm=1024, tn=768, tk=D, b_block=768, name="ffn_up", side=gather.take(58))
    w_gate = gather.get("ffn_w_gate")
    gpre = _mm(x1b, w_gate, tm=1024, tn=768, tk=D, b_block=768, name="ffn_gate", side=gather.take(58))
    fconv_w = _natural(gather.get("ffn_conv_w"), "cols")
    fin = _ffn_act_fwd(up, gpre, fconv_w, small["ffn_conv_b"], side=gather.take())
    w_down = _natural(gather.get("ffn_w_down"), "rows")
    f = _mm(fin, w_down, tm=1024, tn=1024, tk=2048, name="ffn_down")
    loss, dpre2, dpre2b, d_ln2_g, d_ln2_b = _ln_loss_bwd(x1, f, small["ln2_g"], small["ln2_b"], target)

    grads = {"ln2_g": d_ln2_g, "ln2_b": d_ln2_b}
    both = (F32, BF16)
    g32, g16 = _mm(fin, dpre2b, ta=True, tm=1024, tn=1024, tk=S, out_dtype=both, name="d_ffn_w_down")
    scatter.add_blocks("ffn_w_down", _blocks(g32, "rows"), _blocks(g16, "rows"))
    dfin = _mm(dpre2b, w_down, tb=True, tm=1024, tn=1024, tk=D, name="d_fin", side=scatter.take(57))
    dup, dgpre, grads["ffn_conv_w"], grads["ffn_conv_b"] = _ffn_act_bwd(
        dfin, up, gpre, fconv_w, small["ffn_conv_b"], side=scatter.take(85))
    g32, g16 = _mm(x1b, dup, ta=True, tm=1024, tn=768, tk=S, out_dtype=both, out_block=768, name="d_ffn_w_up",
                   side=scatter.take(57))
    scatter.add_blocks("ffn_w_up", g32, g16)
    g32, g16 = _mm(x1b, dgpre, ta=True, tm=1024, tn=768, tk=S, out_dtype=both, out_block=768, name="d_ffn_w_gate",
                   side=scatter.take(56))
    scatter.add_blocks("ffn_w_gate", g32, g16)
    dx1 = _mm(dup, w_up, tb=True, tm=1024, tn=1024, tk=768, b_block=768, name="d_x1_up", side=scatter.take(68))
    dx1 = _mm(dgpre, w_gate, tb=True, tm=1024, tn=1024, tk=768, b_block=768, add=dx1, name="d_x1_gate",
              side=scatter.take(70))
    dpre1, dpre1b, grads["ln1_g"], grads["ln1_b"] = _ln_bwd(dx1, dpre2, xhat1, rstd1, small["ln1_g"],
                                                            side=scatter.take(24))

    g32, g16 = _mm(mixin, dpre1b, ta=True, tm=1024, tn=1024, tk=S, out_dtype=both, name="d_w_out",
                   side=scatter.take(26))
    scatter.add_blocks("w_out", _blocks(g32, "rows"), _blocks(g16, "rows"))
    dmix = _mm(dpre1b, w_out, tb=True, tm=1024, tn=1024, tk=D, name="d_mixin", side=scatter.take(22))
    dya, dyr, dgl_a, dgl_r, db_a, db_r = _gate_bwd(dmix, y_attn, y_rnn, proj, small["b_gate"], side=scatter.take(36))
    grads["b_gate"] = jnp.concatenate([db_a, db_r], axis=1)
    g32, g16 = _mm(ot, dya, tm=1024, tn=1024, tk=S, out_dtype=both, name="d_w_attn_proj", side=scatter.take(38))
    scatter.add_blocks("w_attn_proj", _blocks(g32, "rows"), _blocks(g16, "rows"))
    g32, g16 = _mm(yrin, dyr, ta=True, tm=1280, tn=1024, tk=S, out_dtype=both, name="d_w_rnn_proj",
                   side=scatter.take(27))
    scatter.add_blocks("w_rnn_proj", _blocks(g32, "rows"), _blocks(g16, "rows"))
    dot_ = _mm(w_ap, dya, tb=True, tm=1024, tn=1024, tk=D, out_dtype=MXU_DTYPE, name="d_o", side=scatter.take(22))
    dyrin = _mm(dyr, w_rp, tb=True, tm=1024, tn=1280, tk=D, name="d_yrin", side=scatter.take(27))

    dry, dzr, dzi, drxc_in, grads["lru_ba"], grads["lru_bi"], grads["lru_lambda"] = _lru_scan_bwd(
        dyrin, proj, h, r, i, rxc, small["lru_lambda"], side=scatter.take(94))
    grads["lru_wa"], grads["lru_wi"] = _lru_gate_wgrad(rxc, dzr, dzi, side=scatter.take(22))
    drxc = _lru_gate_xgrad(dzr, dzi, small["lru_wa"], small["lru_wi"], drxc_in, side=scatter.take(33))
    drx, grads["rnn_conv_w"], grads["rnn_conv_b"] = _conv_bwd(drxc, proj, OFF_RX, rconv_w, tc=512,
                                                             name="rnn_conv_bwd", side=scatter.take(29))

    dqt, dk, dv, dsink = _attn_bwd(qt, kp, kt, vp, sink_row, dot_.reshape(N_KV, GROUP, HEAD_DIM, S),
                                   side=scatter.take(65))
    grads["attn_sinks"] = dsink.reshape(1, N_KV * GROUP)
    dproj = jnp.concatenate([
        dqt.reshape(D, S).T,
        _heads_minor(dk[:, BLOCK:, :]).astype(MXU_DTYPE),
        _heads_minor(dv[:, BLOCK:, :]).astype(MXU_DTYPE),
        drx, dry, dgl_a, dgl_r], axis=1)
    for part in range(W_IN_PARTS):
        rows = slice(part * (D // W_IN_PARTS), (part + 1) * (D // W_IN_PARTS))
        side = _join(scatter.take(55), sync.begin(loss, grads)) if part == 0 else scatter.take(68)
        g32, g16 = _mm(xb[:, rows], dproj, ta=True, tm=D // W_IN_PARTS, tn=512, tk=S, out_dtype=both,
                       name="d_w_in_%d" % part, side=side)
        scatter.add_cols("w_in_%d" % part, g32, g16)
        scatter.flush_pairs("pairs_w_in_%d" % part)
    dx = _mm(dproj, w_in, tb=True, tm=1024, tn=1024, tk=512, add=dpre1, add_scale=ALPHA, name="d_x",
             side=_join(scatter.take(400), sync.middle()))
    return dx


SHARDED = (
    ("w_in", "cols", 128), ("w_attn_proj", "rows", 32), ("w_rnn_proj", "rows", 32), ("w_out", "rows", 32),
    ("ffn_w_up", "cols", 128), ("ffn_w_gate", "cols", 128), ("ffn_w_down", "rows", 64),
)
SMALL_REPLICATED = ("b_gate", "rnn_conv_b", "lru_wa", "lru_ba", "lru_wi", "lru_bi", "lru_lambda", "attn_sinks",
                    "ln1_g", "ln1_b", "ffn_conv_b", "ln2_g", "ln2_b")
SMALL_SHARDED = ("rnn_conv_w", "ffn_conv_w")
SMALL_MATS = ("lru_wa", "lru_wi")
W_IN_PARTS = 2
WEIGHTS = ("w_in", "b_gate", "rnn_conv_w", "rnn_conv_b", "lru_wa", "lru_ba", "lru_wi", "lru_bi", "lru_lambda",
           "attn_sinks", "w_attn_proj", "w_rnn_proj", "w_out", "ln1_g", "ln1_b", "ffn_w_up", "ffn_w_gate",
           "ffn_conv_w", "ffn_conv_b", "ffn_w_down", "ln2_g", "ln2_b")


def kernel(x, w_in, b_gate, rnn_conv_w, rnn_conv_b, lru_wa, lru_ba, lru_wi, lru_bi, lru_lambda, attn_sinks, w_attn_proj, w_rnn_proj, w_out, ln1_g, ln1_b, ffn_w_up, ffn_w_gate, ffn_conv_w, ffn_conv_b, ffn_w_down, ln2_g, ln2_b, loss_target, m_w_in, m_b_gate, m_rnn_conv_w, m_rnn_conv_b, m_lru_wa, m_lru_ba, m_lru_wi, m_lru_bi, m_lru_lambda, m_attn_sinks, m_w_attn_proj, m_w_rnn_proj, m_w_out, m_ln1_g, m_ln1_b, m_ffn_w_up, m_ffn_w_gate, m_ffn_conv_w, m_ffn_conv_b, m_ffn_w_down, m_ln2_g, m_ln2_b, v_w_in, v_b_gate, v_rnn_conv_w, v_rnn_conv_b, v_lru_wa, v_lru_ba, v_lru_wi, v_lru_bi, v_lru_lambda, v_attn_sinks, v_w_attn_proj, v_w_rnn_proj, v_w_out, v_ln1_g, v_ln1_b, v_ffn_w_up, v_ffn_w_gate, v_ffn_conv_w, v_ffn_conv_b, v_ffn_w_down, v_ln2_g, v_ln2_b):
    given = dict(locals())
    wsh = {n: given[n][0] for n in WEIGHTS}
    msh = {n: given["m_" + n][0] for n in WEIGHTS}
    vsh = {n: given["v_" + n][0] for n in WEIGHTS}
    m_given = {n: given["m_" + n] for n in WEIGHTS}
    v_given = {n: given["v_" + n] for n in WEIGHTS}
    me = 4 * lax.axis_index("x") + 2 * lax.axis_index("y") + lax.axis_index("c")

    order = ("w_in", "rnn_conv_w", "ffn_conv_w", "w_attn_proj", "w_rnn_proj", "w_out", "ffn_w_up", "ffn_w_gate",
             "ffn_w_down")
    gather = _Gather({n: wsh[n] if n in SMALL_SHARDED else wsh[n].astype(MXU_DTYPE) for n in order[:3]})
    *casts, xb = _cast_many([wsh[n] for n in order[3:]] + [x[0]], side=gather.take(through="ffn_conv_w"))
    gather.add_shards(dict(zip(order[3:], casts)))
    small = {n: given[n] for n in SMALL_REPLICATED}
    small["lru_wa"] = _block_diag(wsh["lru_wa"])
    small["lru_wi"] = _block_diag(wsh["lru_wi"])
    scatter = _Scatter(me, jnp.stack([_index(_relative(k)) for k in FAR]).astype(jnp.int32))

    vec_names = tuple(n for n in SMALL_REPLICATED if n not in SMALL_MATS) + SMALL_SHARDED
    sync = _SmallSync(vec_names, SMALL_MATS)
    dx = _forward_backward(x[0], xb, loss_target[0], small, gather, scatter, sync)

    loss_total, g_small, mat_sums = sync.end()
    loss_total = loss_total.reshape(())
    for n in SMALL_SHARDED:
        width = wsh[n].shape[1]
        g_small[n] = lax.dynamic_slice_in_dim(g_small[n], me * width, width, axis=1)
    g_small = {n: g_small[n].reshape(given[n].shape) for n in vec_names}
    out = {}
    results = _adamw_many(*[[d[n] for n in vec_names] for d in (given, m_given, v_given, g_small)])
    for n, delta, nm, nv in zip(vec_names, *results):
        out[n] = (g_small[n], delta, nm, nv)
    for n in SMALL_MATS:
        g = mat_sums[n].reshape(given[n].shape)
        out[n] = (g, *_adamw_blocks(given[n], m_given[n], v_given[n], g, name="adamw_" + n))

    tile_rows = {n: tr for n, _, tr in SHARDED}
    me1 = me.reshape(1).astype(jnp.int32)
    res = None
    for n in list(scatter.sends):
        own, pair, far = scatter.get(n)
        if n.startswith("w_in_"):
            row0 = int(n[len("w_in_"):]) * (D // W_IN_PARTS)
            res = _reduce_adamw(wsh["w_in"], msh["w_in"], vsh["w_in"], own, pair, far, me1, tr=tile_rows["w_in"],
                                name="adamw_" + n, row0=row0, earlier=res if row0 else None)
            out["w_in"] = tuple(r[None] for r in res)
        else:
            res_n = _reduce_adamw(wsh[n], msh[n], vsh[n], own, pair, far, me1, tr=tile_rows[n], name="adamw_" + n)
            out[n] = tuple(r[None] for r in res_n)

    outputs = [loss_total, dx[None]]
    for kind in range(4):
        outputs += [out[n][kind] for n in WEIGHTS]
    return tuple(outputs)
```

```python
import math

import jax
import jax.numpy as jnp
from jax import lax
from jax.experimental import pallas as pl
from jax.experimental.pallas import tpu as pltpu

F32 = jnp.float32
BF16 = jnp.bfloat16
MXU_DTYPE = jnp.bfloat16

N_DEV = 8
S = 2048
D = 2048
HEAD_DIM = 64
N_KV = 4
GROUP = 8
BLOCK = 128
D_KV = N_KV * HEAD_DIM
D_RNN = 2560
RNN_GROUP = 640
N_RNN_GROUPS = D_RNN // RNN_GROUP
RNN_BLOCK_W = 160
RNN_CONV_W = 4
LRU_C = 8.0
D_FF = 6144
FFN_CONV_W = 3
D_IN = 11776
OFF_K = 2048
OFF_V = 2304
OFF_RX = 2560
OFF_RY = 5120
OFF_GA = 7680
OFF_GR = 9728
LN_EPS = 1e-5
ALPHA = 2.0 ** 0.25
ADAM_LR = 0.001
ADAM_B1 = 0.9
ADAM_B2 = 0.999
ADAM_EPS = 1e-08
ADAM_WD = 0.01
ADAM_STEP = 10
NEG = -1e30
VMEM_LIMIT = 56 * 1024 * 1024
MID_RIDE_TENTHS = 6
MESH = pl.DeviceIdType.MESH
GELU_C = math.sqrt(2.0 / math.pi)


def _cparams(*sem):
    return pltpu.CompilerParams(dimension_semantics=sem or None, vmem_limit_bytes=VMEM_LIMIT)


def _call(body, *, name, grid, in_specs, out_specs, out_shape, operands, semantics, scratch_shapes=(), side=None):
    single = not isinstance(out_shape, (list, tuple))
    out_shape = [out_shape] if single else list(out_shape)
    out_specs = [out_specs] if single else list(out_specs)
    in_specs = list(in_specs)
    scratch_shapes = list(scratch_shapes)
    if side is None:
        res = pl.pallas_call(
            body, name=name, grid=grid, in_specs=in_specs, out_specs=out_specs, out_shape=out_shape,
            scratch_shapes=scratch_shapes, compiler_params=_cparams(*semantics))(*operands)
        return res[0] if single else res
    n_in, n_out, n_scr = len(in_specs), len(out_shape), len(scratch_shapes)
    s_in, s_out = len(side.operands), len(side.out_shape)
    hbm = pl.BlockSpec(memory_space=pltpu.HBM)
    steps = math.prod(grid)
    mid_step = (steps * MID_RIDE_TENTHS) // 10

    def with_copies(*refs):
        core_in, side_in = refs[:n_in], refs[n_in:n_in + s_in]
        o0 = n_in + s_in
        core_out, side_out = refs[o0:o0 + n_out], refs[o0 + n_out:o0 + n_out + s_out]
        c0 = o0 + n_out + s_out
        core_scr, sems = refs[c0:c0 + n_scr], refs[c0 + n_scr:]
        step = 0
        for d, size in enumerate(grid):
            step = step * size + pl.program_id(d)

        @pl.when(step == 0)
        def _():
            side.start(side_in, side_out, sems)

        body(*core_in, *core_out, *core_scr)

        @pl.when(step == mid_step)
        def _():
            side.mid(side_in, side_out, sems)

        @pl.when(step == steps - 1)
        def _():
            side.finish(side_in, side_out, sems)

    res = pl.pallas_call(
        with_copies, name=name, grid=grid,
        in_specs=in_specs + [hbm] * s_in, out_specs=out_specs + [hbm] * s_out,
        out_shape=out_shape + list(side.out_shape),
        scratch_shapes=scratch_shapes + list(side.sems),
        input_output_aliases={n_in + i: n_out + o for i, o in side.aliases.items()},
        compiler_params=_cparams(*(("arbitrary",) * len(grid))))(*operands, *side.operands)
    side.done(res[n_out:])
    return res[0] if single else res[:n_out]


def _run_side(side, name):
    def body(*refs):
        s_in, s_out = len(side.operands), len(side.out_shape)
        side.start(refs[:s_in], refs[s_in:s_in + s_out], refs[s_in + s_out:])
        side.mid(refs[:s_in], refs[s_in:s_in + s_out], refs[s_in + s_out:])
        side.finish(refs[:s_in], refs[s_in:s_in + s_out], refs[s_in + s_out:])

    hbm = pl.BlockSpec(memory_space=pltpu.HBM)
    res = pl.pallas_call(
        body, name=name, in_specs=[hbm] * len(side.operands), out_specs=[hbm] * len(side.out_shape),
        out_shape=list(side.out_shape), scratch_shapes=list(side.sems),
        input_output_aliases=dict(side.aliases))(*side.operands)
    side.done(res)


def _gelu(x):
    x2 = x * x
    t = jnp.tanh(GELU_C * (x + 0.044715 * x * x2))
    g = 0.5 * x * (1.0 + t)
    dg = 0.5 * (1.0 + t) + 0.5 * x * (1.0 - t * t) * (GELU_C * (1.0 + 3.0 * 0.044715 * x2))
    return g, dg


def _sigmoid(x):
    return 1.0 / (1.0 + jnp.exp(-x))


def _softplus(x):
    z = jnp.exp(-jnp.abs(x))
    small = z * (1.0 - z * (0.5 - z * (1.0 / 3.0 - 0.25 * z)))
    return jnp.maximum(x, 0.0) + jnp.where(z < 0.02, small, jnp.log(1.0 + z))


def _one_minus_exp(x):
    series = -x * (1.0 + x * (0.5 + x * (1.0 / 6.0 + x * (1.0 / 24.0))))
    return jnp.where(x > -0.03, series, 1.0 - jnp.exp(x))


def _colsum(v):
    return jnp.sum(v, axis=0, keepdims=True)


def _mm(a, b, *, tm, tn, tk, name, ta=False, tb=False, out_dtype=F32, b_block=None, out_block=None, add=None,
        add_scale=1.0, side=None):
    out_dtypes = out_dtype if isinstance(out_dtype, tuple) else (out_dtype,)
    if ta:
        k_dim, m_dim = a.shape
    else:
        m_dim, k_dim = a.shape
    if b_block is None:
        n_dim = b.shape[0] if tb else b.shape[1]
    else:
        n_dim = b.shape[1] if tb else b.shape[0] * b_block
    assert m_dim % tm == 0 and n_dim % tn == 0 and k_dim % tk == 0, (name, m_dim, n_dim, k_dim)
    nk = k_dim // tk
    dims = (((0 if ta else 1,), (1 if tb else 0,)), ((), ()))
    has_add = add is not None

    def body(*refs):
        a_ref, b_ref = refs[0], refs[1]
        add_ref = refs[2] if has_add else None
        first_out = 3 if has_add else 2
        o_refs = refs[first_out:first_out + len(out_dtypes)]

        def product():
            return lax.dot_general(a_ref[...].astype(MXU_DTYPE), b_ref[...].astype(MXU_DTYPE), dims,
                                   preferred_element_type=F32)

        def finish(acc):
            if has_add:
                acc = acc + add_scale * add_ref[...]
            for o_ref in o_refs:
                o_ref[...] = acc.astype(o_ref.dtype)

        if nk == 1:
            finish(product())
        else:
            acc_ref = refs[-1]
            k = pl.program_id(2)

            @pl.when(k == 0)
            def _():
                acc_ref[...] = jnp.zeros_like(acc_ref)

            acc_ref[...] += product()

            @pl.when(k == nk - 1)
            def _():
                finish(acc_ref[...])

    if ta:
        a_spec = pl.BlockSpec((tk, tm), lambda i, j, k: (k, i))
    else:
        a_spec = pl.BlockSpec((tm, tk), lambda i, j, k: (i, k))
    if b_block is None:
        if tb:
            b_spec = pl.BlockSpec((tn, tk), lambda i, j, k: (j, k))
        else:
            b_spec = pl.BlockSpec((tk, tn), lambda i, j, k: (k, j))
    elif tb:
        assert b_block % tk == 0
        b_spec = pl.BlockSpec((None, tn, tk), lambda i, j, k: ((k * tk) // b_block, j, ((k * tk) % b_block) // tk))
    else:
        assert b_block % tn == 0
        b_spec = pl.BlockSpec((None, tk, tn), lambda i, j, k: ((j * tn) // b_block, k, ((j * tn) % b_block) // tn))
    in_specs = [a_spec, b_spec]
    operands = [a, b]
    if has_add:
        in_specs.append(pl.BlockSpec((tm, tn), lambda i, j, k: (i, j)))
        operands.append(add)
    if out_block is None:
        out_spec = pl.BlockSpec((tm, tn), lambda i, j, k: (i, j))
        out_dims = (m_dim, n_dim)
    else:
        assert out_block % tn == 0
        out_spec = pl.BlockSpec((None, tm, tn), lambda i, j, k: ((j * tn) // out_block, i, ((j * tn) % out_block) // tn))
        out_dims = (n_dim // out_block, m_dim, out_block)
    res = _call(
        body,
        name=name,
        grid=(m_dim // tm, n_dim // tn, nk),
        in_specs=in_specs,
        out_specs=[out_spec] * len(out_dtypes),
        out_shape=[jax.ShapeDtypeStruct(out_dims, dt) for dt in out_dtypes],
        scratch_shapes=[pltpu.VMEM((tm, tn), F32)] if nk > 1 else [],
        semantics=("parallel", "parallel", "arbitrary"),
        operands=tuple(operands),
        side=side,
    )
    return res if isinstance(out_dtype, tuple) else res[0]


def _attn_bias(bias_ref, h):
    key = lax.broadcasted_iota(jnp.int32, (2 * BLOCK, GROUP * BLOCK), 0)
    col = lax.broadcasted_iota(jnp.int32, (2 * BLOCK, GROUP * BLOCK), 1)
    dist = BLOCK + (col & (BLOCK - 1)) - key
    head = h * GROUP + (col >> 7) + 1
    slope = jnp.exp(head.astype(F32) * (-0.25 * math.log(2.0)))
    bias = jnp.where((dist >= 0) & (dist < BLOCK), -slope * dist.astype(F32), NEG)
    bias_ref[1] = bias
    bias_ref[0] = jnp.where(key < BLOCK, NEG, bias)


def _attn_probs(kb, qt, bias, sink):
    s = jnp.dot(kb, qt, preferred_element_type=F32) * (HEAD_DIM ** -0.5) + bias
    m = jnp.maximum(jnp.max(s, axis=0, keepdims=True), sink)
    e = jnp.exp(s - m)
    e_sink = jnp.exp(sink - m)
    inv = 1.0 / (jnp.sum(e, axis=0, keepdims=True) + e_sink)
    return e * inv, e_sink * inv


def _heads_on_lanes(ref, r0):
    return jnp.concatenate([ref[g, :, pl.ds(r0, BLOCK)] for g in range(GROUP)], axis=1)


def _attn_fwd(qt, kp, vt, sink_row, side=None):
    cols = GROUP * BLOCK

    def body(q_ref, k_ref, vt_ref, sink_ref, o_ref, bias_ref):
        _attn_bias(bias_ref, pl.program_id(0))
        sink = sink_ref[...]

        def step(n, carry):
            r0 = pl.multiple_of(n * BLOCK, BLOCK)
            p, _ = _attn_probs(k_ref[pl.ds(r0, 2 * BLOCK), :], _heads_on_lanes(q_ref, r0),
                               bias_ref[jnp.minimum(n, 1)], sink)
            o = jnp.dot(vt_ref[:, pl.ds(r0, 2 * BLOCK)], p.astype(MXU_DTYPE), preferred_element_type=F32)
            for g in range(GROUP):
                o_ref[g, :, pl.ds(r0, BLOCK)] = o[:, g * BLOCK:(g + 1) * BLOCK].astype(o_ref.dtype)
            return carry

        lax.fori_loop(0, S // BLOCK, step, 0)

    hm = pl.BlockSpec((None, GROUP, HEAD_DIM, S), lambda h: (h, 0, 0, 0))
    return _call(
        body,
        name="attn_fwd",
        grid=(N_KV,),
        in_specs=[
            hm,
            pl.BlockSpec((None, BLOCK + S, HEAD_DIM), lambda h: (h, 0, 0)),
            pl.BlockSpec((None, HEAD_DIM, BLOCK + S), lambda h: (h, 0, 0)),
            pl.BlockSpec((None, 1, cols), lambda h: (h, 0, 0)),
        ],
        out_specs=hm,
        out_shape=jax.ShapeDtypeStruct((N_KV, GROUP, HEAD_DIM, S), MXU_DTYPE),
        scratch_shapes=[pltpu.VMEM((2, 2 * BLOCK, cols), F32)],
        semantics=("parallel",),
        operands=(qt, kp, vt, sink_row),
        side=side,
    )


def _attn_bwd(qt, kp, kt, vp, sink_row, dot_, side=None):
    cols = GROUP * BLOCK

    def body(q_ref, k_ref, kt_ref, v_ref, sink_ref, do_ref, dq_ref, dk_ref, dv_ref, dsink_ref, bias_ref):
        _attn_bias(bias_ref, pl.program_id(0))
        sink = sink_ref[...]
        dk_ref[...] = jnp.zeros_like(dk_ref)
        dv_ref[...] = jnp.zeros_like(dv_ref)
        nt = (((1,), (1,)), ((), ()))

        def step(n, sink_acc):
            r0 = pl.multiple_of(n * BLOCK, BLOCK)
            band = pl.ds(r0, 2 * BLOCK)
            qn = _heads_on_lanes(q_ref, r0)
            don = _heads_on_lanes(do_ref, r0)
            p, p_sink = _attn_probs(k_ref[band, :], qn, bias_ref[jnp.minimum(n, 1)], sink)
            dp = jnp.dot(v_ref[band, :], don, preferred_element_type=F32)
            delta = jnp.sum(p * dp, axis=0, keepdims=True)
            ds = (p * (dp - delta) * (HEAD_DIM ** -0.5)).astype(MXU_DTYPE)
            dq = jnp.dot(kt_ref[:, band], ds, preferred_element_type=F32)
            for g in range(GROUP):
                dq_ref[g, :, pl.ds(r0, BLOCK)] = dq[:, g * BLOCK:(g + 1) * BLOCK].astype(dq_ref.dtype)
            dk_ref[band, :] += lax.dot_general(ds, qn, nt, preferred_element_type=F32)
            dv_ref[band, :] += lax.dot_general(p.astype(MXU_DTYPE), don, nt, preferred_element_type=F32)
            return sink_acc - p_sink * delta

        sink_acc = lax.fori_loop(0, S // BLOCK, step, jnp.zeros((1, cols), F32))
        for g in range(GROUP):
            dsink_ref[g:g + 1, :] = jnp.sum(sink_acc[:, g * BLOCK:(g + 1) * BLOCK], axis=1, keepdims=True)

    hm = pl.BlockSpec((None, GROUP, HEAD_DIM, S), lambda h: (h, 0, 0, 0))
    kv = pl.BlockSpec((None, BLOCK + S, HEAD_DIM), lambda h: (h, 0, 0))
    return _call(
        body,
        name="attn_bwd",
        grid=(N_KV,),
        in_specs=[hm, kv, pl.BlockSpec((None, HEAD_DIM, BLOCK + S), lambda h: (h, 0, 0)), kv,
                  pl.BlockSpec((None, 1, cols), lambda h: (h, 0, 0)), hm],
        out_specs=[hm, kv, kv, pl.BlockSpec((None, GROUP, 1), lambda h: (h, 0, 0))],
        out_shape=[
            jax.ShapeDtypeStruct((N_KV, GROUP, HEAD_DIM, S), MXU_DTYPE),
            jax.ShapeDtypeStruct((N_KV, BLOCK + S, HEAD_DIM), F32),
            jax.ShapeDtypeStruct((N_KV, BLOCK + S, HEAD_DIM), F32),
            jax.ShapeDtypeStruct((N_KV, GROUP, 1), F32),
        ],
        scratch_shapes=[pltpu.VMEM((2, 2 * BLOCK, cols), F32)],
        semantics=("parallel",),
        operands=(qt, kp, kt, vp, sink_row, dot_),
        side=side,
    )


PAD = 8
CHUNK = 256


def _past_taps(xpad_ref, r0, width):
    ext = xpad_ref[pl.ds(r0, CHUNK + PAD), :]
    taps = []
    for k in range(width):
        back = width - 1 - k
        taps.append((ext if back == 0 else pltpu.roll(ext, back, 0))[PAD:, :])
    return taps


def _future_taps(xpad_ref, r0, width):
    ext = xpad_ref[pl.ds(r0, CHUNK + PAD), :]
    taps = []
    for ahead in range(width):
        taps.append((ext if ahead == 0 else pltpu.roll(ext, CHUNK + PAD - ahead, 0))[:CHUNK, :])
    return taps


def _conv_fwd(src, col0, w, b, *, tc, name, side=None):
    width, c_dim = w.shape

    def body(x_ref, w_ref, b_ref, o_ref, xpad_ref):
        xpad_ref[pl.ds(0, PAD), :] = jnp.zeros((PAD, tc), F32)
        xpad_ref[pl.ds(PAD, S), :] = x_ref[...]
        wv = w_ref[...]
        bv = b_ref[...]

        def step(ci, carry):
            r0 = pl.multiple_of(ci * CHUNK, CHUNK)
            taps = _past_taps(xpad_ref, r0, width)
            y = bv + taps[0] * wv[0:1, :]
            for k in range(1, width):
                y = y + taps[k] * wv[k:k + 1, :]
            o_ref[pl.ds(r0, CHUNK), :] = y
            return carry

        lax.fori_loop(0, S // CHUNK, step, 0)

    return _call(
        body,
        name=name,
        grid=(c_dim // tc,),
        in_specs=[
            pl.BlockSpec((S, tc), lambda j: (0, col0 // tc + j)),
            pl.BlockSpec((width, tc), lambda j: (0, j)),
            pl.BlockSpec((1, tc), lambda j: (0, j)),
        ],
        out_specs=pl.BlockSpec((S, tc), lambda j: (0, j)),
        out_shape=jax.ShapeDtypeStruct((S, c_dim), F32),
        scratch_shapes=[pltpu.VMEM((S + PAD, tc), F32)],
        semantics=("parallel",),
        operands=(src, w, b),
        side=side,
    )


def _conv_bwd(dy, src, col0, w, *, tc, name, side=None):
    width, c_dim = w.shape

    def body(dy_ref, x_ref, w_ref, dx_ref, dw_ref, db_ref, xpad_ref, dpad_ref):
        xpad_ref[pl.ds(0, PAD), :] = jnp.zeros((PAD, tc), F32)
        xpad_ref[pl.ds(PAD, S), :] = x_ref[...]
        dpad_ref[pl.ds(0, S), :] = dy_ref[...]
        dpad_ref[pl.ds(S, PAD), :] = jnp.zeros((PAD, tc), F32)
        wv = w_ref[...]

        def step(ci, acc):
            r0 = pl.multiple_of(ci * CHUNK, CHUNK)
            past = _past_taps(xpad_ref, r0, width)
            ahead = _future_taps(dpad_ref, r0, width)
            d = ahead[0]
            dx = d * wv[width - 1:width, :]
            for j in range(1, width):
                dx = dx + ahead[j] * wv[width - 1 - j:width - j, :]
            dx_ref[pl.ds(r0, CHUNK), :] = dx.astype(dx_ref.dtype)
            return tuple(acc[k] + _colsum(past[k] * d) for k in range(width)) + (acc[width] + _colsum(d),)

        zero = jnp.zeros((1, tc), F32)
        acc = lax.fori_loop(0, S // CHUNK, step, (zero,) * (width + 1))
        for k in range(width):
            dw_ref[k:k + 1, :] = acc[k]
        db_ref[...] = acc[width]

    return _call(
        body,
        name=name,
        grid=(c_dim // tc,),
        in_specs=[
            pl.BlockSpec((S, tc), lambda j: (0, j)),
            pl.BlockSpec((S, tc), lambda j: (0, col0 // tc + j)),
            pl.BlockSpec((width, tc), lambda j: (0, j)),
        ],
        out_specs=[
            pl.BlockSpec((S, tc), lambda j: (0, j)),
            pl.BlockSpec((width, tc), lambda j: (0, j)),
            pl.BlockSpec((1, tc), lambda j: (0, j)),
        ],
        out_shape=[
            jax.ShapeDtypeStruct((S, c_dim), MXU_DTYPE),
            jax.ShapeDtypeStruct((width, c_dim), F32),
            jax.ShapeDtypeStruct((1, c_dim), F32),
        ],
        scratch_shapes=[pltpu.VMEM((S + PAD, tc), F32), pltpu.VMEM((S + PAD, tc), F32)],
        semantics=("parallel",),
        operands=(dy, src, w),
        side=side,
    )


SCAN_TC = 256


def _lru_gates(rxc, wa, wi, ba, bi, side=None):
    tm = 512

    def body(x_ref, wa_ref, wi_ref, ba_ref, bi_ref, r_ref, i_ref):
        xv = x_ref[...].astype(MXU_DTYPE)
        r_ref[...] = _sigmoid(jnp.dot(xv, wa_ref[...].astype(MXU_DTYPE), preferred_element_type=F32) + ba_ref[...])
        i_ref[...] = _sigmoid(jnp.dot(xv, wi_ref[...].astype(MXU_DTYPE), preferred_element_type=F32) + bi_ref[...])

    x_spec = pl.BlockSpec((tm, RNN_GROUP), lambda g, i: (i, g))
    w_spec = pl.BlockSpec((None, RNN_GROUP, RNN_GROUP), lambda g, i: (g, 0, 0))
    b_spec = pl.BlockSpec((1, RNN_GROUP), lambda g, i: (0, g))
    return _call(
        body,
        name="lru_gates",
        grid=(N_RNN_GROUPS, S // tm),
        in_specs=[x_spec, w_spec, w_spec, b_spec, b_spec],
        out_specs=[x_spec, x_spec],
        out_shape=[jax.ShapeDtypeStruct((S, D_RNN), F32)] * 2,
        semantics=("parallel", "parallel"),
        operands=(rxc, wa, wi, ba, bi),
        side=side,
    )


def _scan_down(a, u, row):
    for d in (1, 2, 4):
        a_s = jnp.where(row >= d, pltpu.roll(a, d, 0), 1.0)
        u_s = jnp.where(row >= d, pltpu.roll(u, d, 0), 0.0)
        u = a * u_s + u
        a = a * a_s
    return a, u


def _scan_up(a, u, row):
    for d in (1, 2, 4):
        a_s = jnp.where(row < 8 - d, pltpu.roll(a, 8 - d, 0), 1.0)
        u_s = jnp.where(row < 8 - d, pltpu.roll(u, 8 - d, 0), 0.0)
        u = a * u_s + u
        a = a * a_s
    return a, u


def _lru_scan_fwd(r, i, rxc, proj, lam, side=None):
    tc = SCAN_TC

    def body(r_ref, i_ref, x_ref, ry_ref, lam_ref, h_ref, y_ref):
        rate = LRU_C * _softplus(-lam_ref[...])
        row = lax.broadcasted_iota(jnp.int32, (8, tc), 0)

        def step(ci, carry):
            r0 = pl.multiple_of(ci * 16, 16)
            log_a = -rate * r_ref[pl.ds(r0, 16), :]
            a16 = jnp.exp(log_a)
            u16 = jnp.sqrt(_one_minus_exp(2.0 * log_a)) * (i_ref[pl.ds(r0, 16), :] * x_ref[pl.ds(r0, 16), :])
            hs = []
            for half in range(2):
                a_cum, h0 = _scan_down(a16[8 * half:8 * half + 8, :], u16[8 * half:8 * half + 8, :], row)
                h = a_cum * carry + h0
                carry = jnp.broadcast_to(h[7:8, :], (8, tc))
                hs.append(h)
            h16 = jnp.concatenate(hs, axis=0)
            h_ref[pl.ds(r0, 16), :] = h16
            y_ref[pl.ds(r0, 16), :] = (h16 * _gelu(ry_ref[pl.ds(r0, 16), :])[0]).astype(y_ref.dtype)
            return carry

        lax.fori_loop(0, S // 16, step, jnp.zeros((8, tc), F32))

    col = pl.BlockSpec((S, tc), lambda j: (0, j))
    return _call(
        body,
        name="lru_scan_fwd",
        grid=(D_RNN // tc,),
        in_specs=[col, col, col, pl.BlockSpec((S, tc), lambda j: (0, OFF_RY // tc + j)),
                  pl.BlockSpec((1, tc), lambda j: (0, j))],
        out_specs=[col, col],
        out_shape=[jax.ShapeDtypeStruct((S, D_RNN), F32), jax.ShapeDtypeStruct((S, D_RNN), MXU_DTYPE)],
        semantics=("parallel",),
        operands=(r, i, rxc, proj, lam),
        side=side,
    )


def _lru_scan_bwd(dy, proj, h, r, i, rxc, lam, side=None):
    tc = SCAN_TC

    def body(dy_ref, ry_ref, h_ref, r_ref, i_ref, x_ref, lam_ref,
             dry_ref, dzr_ref, dzi_ref, dx_ref, dba_ref, dbi_ref, dlam_ref, a_ref, dh_ref, hp_ref):
        lam_v = lam_ref[...]
        rate = LRU_C * _softplus(-lam_v)
        dlam_scale = LRU_C * _sigmoid(-lam_v)
        row = lax.broadcasted_iota(jnp.int32, (8, tc), 0)
        hp_ref[pl.ds(0, PAD), :] = jnp.zeros((PAD, tc), F32)
        hp_ref[pl.ds(PAD, S), :] = h_ref[...]
        a_ref[pl.ds(S, PAD), :] = jnp.zeros((PAD, tc), F32)

        def prep(ci, carry):
            r0 = pl.multiple_of(ci * CHUNK, CHUNK)
            a_ref[pl.ds(r0, CHUNK), :] = jnp.exp(-rate * r_ref[pl.ds(r0, CHUNK), :])
            ge, dge = _gelu(ry_ref[pl.ds(r0, CHUNK), :])
            dyv = dy_ref[pl.ds(r0, CHUNK), :]
            dh_ref[pl.ds(r0, CHUNK), :] = dyv * ge
            dry_ref[pl.ds(r0, CHUNK), :] = (dyv * h_ref[pl.ds(r0, CHUNK), :] * dge).astype(dry_ref.dtype)
            return carry

        lax.fori_loop(0, S // CHUNK, prep, 0)

        def step(ci, state):
            carry, dba, dbi, dlam = state
            r0 = pl.multiple_of(S - 16 - ci * 16, 16)
            a_ext = a_ref[pl.ds(r0, 24), :]
            a_next = pltpu.roll(a_ext, 23, 0)
            h_prev = pltpu.roll(hp_ref[pl.ds(r0, 24), :], 1, 0)
            dh16 = dh_ref[pl.ds(r0, 16), :]
            gs = [None, None]
            for half in (1, 0):
                lo = 8 * half
                c_cum, g0 = _scan_up(a_next[lo:lo + 8, :], dh16[lo:lo + 8, :], row)
                g = c_cum * carry + g0
                carry = jnp.broadcast_to(g[0:1, :], (8, tc))
                gs[half] = g
            g16 = jnp.concatenate(gs, axis=0)
            a16 = a_ext[0:16, :]
            r16 = r_ref[pl.ds(r0, 16), :]
            i16 = i_ref[pl.ds(r0, 16), :]
            x16 = x_ref[pl.ds(r0, 16), :]
            a2 = a16 * a16
            sq = jnp.sqrt(_one_minus_exp(-2.0 * rate * r16))
            dx_ref[pl.ds(r0, 16), :] = g16 * sq * i16
            dzi = g16 * sq * x16 * i16 * (1.0 - i16)
            dlog_a = g16 * h_prev[8:24, :] * a16 - g16 * i16 * x16 * a2 / sq
            dzr = -rate * dlog_a * r16 * (1.0 - r16)
            dzr_ref[pl.ds(r0, 16), :] = dzr.astype(dzr_ref.dtype)
            dzi_ref[pl.ds(r0, 16), :] = dzi.astype(dzi_ref.dtype)
            return carry, dba + _colsum(dzr), dbi + _colsum(dzi), dlam + _colsum(dlog_a * r16)

        zero = jnp.zeros((1, tc), F32)
        _, dba, dbi, dlam = lax.fori_loop(0, S // 16, step, (jnp.zeros((8, tc), F32), zero, zero, zero))
        dba_ref[...] = dba
        dbi_ref[...] = dbi
        dlam_ref[...] = dlam * dlam_scale

    col = pl.BlockSpec((S, tc), lambda j: (0, j))
    vec = pl.BlockSpec((1, tc), lambda j: (0, j))
    return _call(
        body,
        name="lru_scan_bwd",
        grid=(D_RNN // tc,),
        in_specs=[col, pl.BlockSpec((S, tc), lambda j: (0, OFF_RY // tc + j)), col, col, col, col, vec],
        out_specs=[col, col, col, col, vec, vec, vec],
        out_shape=[jax.ShapeDtypeStruct((S, D_RNN), MXU_DTYPE)] * 3 + [jax.ShapeDtypeStruct((S, D_RNN), F32)]
        + [jax.ShapeDtypeStruct((1, D_RNN), F32)] * 3,
        scratch_shapes=[pltpu.VMEM((S + PAD, tc), F32), pltpu.VMEM((S, tc), F32), pltpu.VMEM((S + PAD, tc), F32)],
        semantics=("parallel",),
        operands=(dy, proj, h, r, i, rxc, lam),
        side=side,
    )


def _lru_gate_wgrad(rxc, dzr, dzi, side=None):
    def body(x_ref, dzr_ref, dzi_ref, dwa_ref, dwi_ref):
        xv = x_ref[...].astype(MXU_DTYPE)
        dims = (((0,), (0,)), ((), ()))
        dwa_ref[...] = lax.dot_general(xv, dzr_ref[...], dims, preferred_element_type=F32)
        dwi_ref[...] = lax.dot_general(xv, dzi_ref[...], dims, preferred_element_type=F32)

    col = pl.BlockSpec((S, RNN_GROUP), lambda g: (0, g))
    w_spec = pl.BlockSpec((None, RNN_GROUP, RNN_GROUP), lambda g: (g, 0, 0))
    return _call(
        body,
        name="lru_gate_wgrad",
        grid=(N_RNN_GROUPS,),
        in_specs=[col, col, col],
        out_specs=[w_spec, w_spec],
        out_shape=[jax.ShapeDtypeStruct((N_RNN_GROUPS, RNN_GROUP, RNN_GROUP), F32)] * 2,
        semantics=("parallel",),
        operands=(rxc, dzr, dzi),
        side=side,
    )


def _lru_gate_xgrad(dzr, dzi, wa, wi, dx_in, side=None):
    tm = 512

    def body(dzr_ref, dzi_ref, wa_ref, wi_ref, dx_ref, o_ref):
        dims = (((1,), (1,)), ((), ()))
        o_ref[...] = (dx_ref[...]
                      + lax.dot_general(dzr_ref[...], wa_ref[...].astype(MXU_DTYPE), dims, preferred_element_type=F32)
                      + lax.dot_general(dzi_ref[...], wi_ref[...].astype(MXU_DTYPE), dims, preferred_element_type=F32))

    x_spec = pl.BlockSpec((tm, RNN_GROUP), lambda g, i: (i, g))
    w_spec = pl.BlockSpec((None, RNN_GROUP, RNN_GROUP), lambda g, i: (g, 0, 0))
    return _call(
        body,
        name="lru_gate_xgrad",
        grid=(N_RNN_GROUPS, S // tm),
        in_specs=[x_spec, x_spec, w_spec, w_spec, x_spec],
        out_specs=x_spec,
        out_shape=jax.ShapeDtypeStruct((S, D_RNN), F32),
        semantics=("parallel", "parallel"),
        operands=(dzr, dzi, wa, wi, dx_in),
        side=side,
    )


def _gate_fwd(y_attn, y_rnn, proj, b_gate, side=None):
    t = 512

    def body(ya_ref, yr_ref, ga_ref, gr_ref, ba_ref, br_ref, o_ref):
        o_ref[...] = (_sigmoid(ga_ref[...] + ba_ref[...]) * ya_ref[...]
                      + _sigmoid(gr_ref[...] + br_ref[...]) * yr_ref[...]).astype(o_ref.dtype)

    tile = pl.BlockSpec((t, t), lambda i, j: (i, j))
    return _call(
        body,
        name="gate_fwd",
        grid=(S // t, D // t),
        in_specs=[tile, tile,
                  pl.BlockSpec((t, t), lambda i, j: (i, OFF_GA // t + j)),
                  pl.BlockSpec((t, t), lambda i, j: (i, OFF_GR // t + j)),
                  pl.BlockSpec((1, t), lambda i, j: (0, j)),
                  pl.BlockSpec((1, t), lambda i, j: (0, D // t + j))],
        out_specs=tile,
        out_shape=jax.ShapeDtypeStruct((S, D), MXU_DTYPE),
        semantics=("parallel", "parallel"),
        operands=(y_attn, y_rnn, proj, proj, b_gate, b_gate),
        side=side,
    )


def _gate_bwd(dmix, y_attn, y_rnn, proj, b_gate, side=None):
    t = 512

    def body(dm_ref, ya_ref, yr_ref, ga_ref, gr_ref, ba_ref, br_ref,
             dya_ref, dyr_ref, dga_ref, dgr_ref, dba_ref, dbr_ref):
        @pl.when(pl.program_id(1) == 0)
        def _():
            dba_ref[...] = jnp.zeros_like(dba_ref)
            dbr_ref[...] = jnp.zeros_like(dbr_ref)

        dm = dm_ref[...]
        ga = _sigmoid(ga_ref[...] + ba_ref[...])
        gr = _sigmoid(gr_ref[...] + br_ref[...])
        dya_ref[...] = (dm * ga).astype(dya_ref.dtype)
        dyr_ref[...] = (dm * gr).astype(dyr_ref.dtype)
        dga = dm * ya_ref[...] * ga * (1.0 - ga)
        dgr = dm * yr_ref[...] * gr * (1.0 - gr)
        dga_ref[...] = dga.astype(dga_ref.dtype)
        dgr_ref[...] = dgr.astype(dgr_ref.dtype)
        dba_ref[...] += _colsum(dga)
        dbr_ref[...] += _colsum(dgr)

    tile = pl.BlockSpec((t, t), lambda j, i: (i, j))
    vec = pl.BlockSpec((1, t), lambda j, i: (0, j))
    return _call(
        body,
        name="gate_bwd",
        grid=(D // t, S // t),
        in_specs=[tile, tile, tile,
                  pl.BlockSpec((t, t), lambda j, i: (i, OFF_GA // t + j)),
                  pl.BlockSpec((t, t), lambda j, i: (i, OFF_GR // t + j)),
                  vec,
                  pl.BlockSpec((1, t), lambda j, i: (0, D // t + j))],
        out_specs=[tile, tile, tile, tile, vec, vec],
        out_shape=[jax.ShapeDtypeStruct((S, D), MXU_DTYPE)] * 4 + [jax.ShapeDtypeStruct((1, D), F32)] * 2,
        semantics=("parallel", "arbitrary"),
        operands=(dmix, y_attn, y_rnn, proj, proj, b_gate, b_gate),
        side=side,
    )


LN_TM = 256


def _ln_stats(pre):
    mu = jnp.mean(pre, axis=-1, keepdims=True)
    xc = pre - mu
    rstd = lax.rsqrt(jnp.mean(xc * xc, axis=-1, keepdims=True) + LN_EPS)
    return xc * rstd, rstd


def _ln_input_grad(dy, xhat, rstd, g):
    dyg = dy * g
    return rstd * (dyg - jnp.mean(dyg, axis=-1, keepdims=True)
                   - xhat * jnp.mean(dyg * xhat, axis=-1, keepdims=True))


def _ln_fwd(res, branch, g, b, side=None):
    def body(res_ref, br_ref, g_ref, b_ref, y_ref, yb_ref, xhat_ref, rstd_ref):
        xhat, rstd = _ln_stats(ALPHA * res_ref[...] + br_ref[...])
        y = xhat * g_ref[...] + b_ref[...]
        y_ref[...] = y
        yb_ref[...] = y.astype(yb_ref.dtype)
        xhat_ref[...] = xhat
        rstd_ref[...] = rstd

    tile = pl.BlockSpec((LN_TM, D), lambda i: (i, 0))
    vec = pl.BlockSpec((1, D), lambda i: (0, 0))
    return _call(
        body,
        name="ln_fwd",
        grid=(S // LN_TM,),
        in_specs=[tile, tile, vec, vec],
        out_specs=[tile, tile, tile, pl.BlockSpec((LN_TM, 1), lambda i: (i, 0))],
        out_shape=[jax.ShapeDtypeStruct((S, D), F32), jax.ShapeDtypeStruct((S, D), MXU_DTYPE),
                   jax.ShapeDtypeStruct((S, D), F32), jax.ShapeDtypeStruct((S, 1), F32)],
        semantics=("parallel",),
        operands=(res, branch, g, b),
        side=side,
    )


def _ln_bwd(dy_a, dy_b, xhat, rstd, g, side=None):
    def body(da_ref, db_in_ref, xhat_ref, rstd_ref, g_ref, dp_ref, dpb_ref, dg_ref, db_ref):
        @pl.when(pl.program_id(0) == 0)
        def _():
            dg_ref[...] = jnp.zeros_like(dg_ref)
            db_ref[...] = jnp.zeros_like(db_ref)

        dy = da_ref[...] + ALPHA * db_in_ref[...]
        xhat = xhat_ref[...]
        dp = _ln_input_grad(dy, xhat, rstd_ref[...], g_ref[...])
        dp_ref[...] = dp
        dpb_ref[...] = dp.astype(dpb_ref.dtype)
        dg_ref[...] += _colsum(dy * xhat)
        db_ref[...] += _colsum(dy)

    tile = pl.BlockSpec((LN_TM, D), lambda i: (i, 0))
    vec = pl.BlockSpec((1, D), lambda i: (0, 0))
    return _call(
        body,
        name="ln_bwd",
        grid=(S // LN_TM,),
        in_specs=[tile, tile, tile, pl.BlockSpec((LN_TM, 1), lambda i: (i, 0)), vec],
        out_specs=[tile, tile, vec, vec],
        out_shape=[jax.ShapeDtypeStruct((S, D), F32), jax.ShapeDtypeStruct((S, D), MXU_DTYPE),
                   jax.ShapeDtypeStruct((1, D), F32), jax.ShapeDtypeStruct((1, D), F32)],
        semantics=("arbitrary",),
        operands=(dy_a, dy_b, xhat, rstd, g),
        side=side,
    )


def _ln_loss_bwd(res, branch, g, b, target, side=None):
    def body(res_ref, br_ref, g_ref, b_ref, t_ref, loss_ref, dp_ref, dpb_ref, dg_ref, db_ref):
        @pl.when(pl.program_id(0) == 0)
        def _():
            loss_ref[...] = jnp.zeros_like(loss_ref)
            dg_ref[...] = jnp.zeros_like(dg_ref)
            db_ref[...] = jnp.zeros_like(db_ref)

        xhat, rstd = _ln_stats(ALPHA * res_ref[...] + br_ref[...])
        gv = g_ref[...]
        err = xhat * gv + b_ref[...] - t_ref[...]
        loss_ref[...] += (0.5 / D) * jnp.sum(_colsum(err * err), axis=1, keepdims=True)
        dy = err * (1.0 / D)
        dp = _ln_input_grad(dy, xhat, rstd, gv)
        dp_ref[...] = dp
        dpb_ref[...] = dp.astype(dpb_ref.dtype)
        dg_ref[...] += _colsum(dy * xhat)
        db_ref[...] += _colsum(dy)

    tile = pl.BlockSpec((LN_TM, D), lambda i: (i, 0))
    vec = pl.BlockSpec((1, D), lambda i: (0, 0))
    return _call(
        body,
        name="ln_loss_bwd",
        grid=(S // LN_TM,),
        in_specs=[tile, tile, vec, vec, tile],
        out_specs=[pl.BlockSpec((1, 1), lambda i: (0, 0)), tile, tile, vec, vec],
        out_shape=[jax.ShapeDtypeStruct((1, 1), F32), jax.ShapeDtypeStruct((S, D), F32),
                   jax.ShapeDtypeStruct((S, D), MXU_DTYPE),
                   jax.ShapeDtypeStruct((1, D), F32), jax.ShapeDtypeStruct((1, D), F32)],
        semantics=("arbitrary",),
        operands=(res, branch, g, b, target),
        side=side,
    )


FFN_TC = 256


def _ffn_act_fwd(up, gpre, w, b, side=None):
    tc = FFN_TC

    def body(up_ref, x_ref, w_ref, b_ref, o_ref, xpad_ref):
        xpad_ref[pl.ds(0, PAD), :] = jnp.zeros((PAD, tc), F32)
        xpad_ref[pl.ds(PAD, S), :] = x_ref[...]
        wv = w_ref[...]
        bv = b_ref[...]

        def step(ci, carry):
            r0 = pl.multiple_of(ci * CHUNK, CHUNK)
            taps = _past_taps(xpad_ref, r0, FFN_CONV_W)
            gate = bv + taps[0] * wv[0:1, :] + taps[1] * wv[1:2, :] + taps[2] * wv[2:3, :]
            o_ref[pl.ds(r0, CHUNK), :] = (_gelu(gate)[0] * up_ref[pl.ds(r0, CHUNK), :]).astype(o_ref.dtype)
            return carry

        lax.fori_loop(0, S // CHUNK, step, 0)

    col = pl.BlockSpec((S, tc), lambda j: (0, j))
    return _call(
        body,
        name="ffn_act_fwd",
        grid=(D_FF // tc,),
        in_specs=[col, col, pl.BlockSpec((FFN_CONV_W, tc), lambda j: (0, j)), pl.BlockSpec((1, tc), lambda j: (0, j))],
        out_specs=col,
        out_shape=jax.ShapeDtypeStruct((S, D_FF), MXU_DTYPE),
        scratch_shapes=[pltpu.VMEM((S + PAD, tc), F32)],
        semantics=("parallel",),
        operands=(up, gpre, w, b),
        side=side,
    )


def _ffn_act_bwd(dfin, up, gpre, w, b, side=None):
    tc = FFN_TC
    width = FFN_CONV_W

    def body(df_ref, up_ref, x_ref, w_ref, b_ref, dup_ref, dx_ref, dw_ref, db_ref, xpad_ref, dpad_ref):
        xpad_ref[pl.ds(0, PAD), :] = jnp.zeros((PAD, tc), F32)
        xpad_ref[pl.ds(PAD, S), :] = x_ref[...]
        dpad_ref[pl.ds(S, PAD), :] = jnp.zeros((PAD, tc), F32)
        wv = w_ref[...]
        bv = b_ref[...]

        def gate_grad(ci, acc):
            r0 = pl.multiple_of(ci * CHUNK, CHUNK)
            taps = _past_taps(xpad_ref, r0, width)
            gate = bv + taps[0] * wv[0:1, :] + taps[1] * wv[1:2, :] + taps[2] * wv[2:3, :]
            ge, dge = _gelu(gate)
            df = df_ref[pl.ds(r0, CHUNK), :]
            dup_ref[pl.ds(r0, CHUNK), :] = (df * ge).astype(dup_ref.dtype)
            d = df * up_ref[pl.ds(r0, CHUNK), :] * dge
            dpad_ref[pl.ds(r0, CHUNK), :] = d
            return tuple(acc[k] + _colsum(taps[k] * d) for k in range(width)) + (acc[width] + _colsum(d),)

        zero = jnp.zeros((1, tc), F32)
        acc = lax.fori_loop(0, S // CHUNK, gate_grad, (zero,) * (width + 1))
        for k in range(width):
            dw_ref[k:k + 1, :] = acc[k]
        db_ref[...] = acc[width]

        def input_grad(ci, carry):
            r0 = pl.multiple_of(ci * CHUNK, CHUNK)
            ahead = _future_taps(dpad_ref, r0, width)
            dx = ahead[0] * wv[2:3, :] + ahead[1] * wv[1:2, :] + ahead[2] * wv[0:1, :]
            dx_ref[pl.ds(r0, CHUNK), :] = dx.astype(dx_ref.dtype)
            return carry

        lax.fori_loop(0, S // CHUNK, input_grad, 0)

    col = pl.BlockSpec((S, tc), lambda j: (0, j))
    w_spec = pl.BlockSpec((width, tc), lambda j: (0, j))
    vec = pl.BlockSpec((1, tc), lambda j: (0, j))
    return _call(
        body,
        name="ffn_act_bwd",
        grid=(D_FF // tc,),
        in_specs=[col, col, col, w_spec, vec],
        out_specs=[col, col, w_spec, vec],
        out_shape=[jax.ShapeDtypeStruct((S, D_FF), MXU_DTYPE)] * 2
        + [jax.ShapeDtypeStruct((width, D_FF), F32), jax.ShapeDtypeStruct((1, D_FF), F32)],
        scratch_shapes=[pltpu.VMEM((S + PAD, tc), F32), pltpu.VMEM((S + PAD, tc), F32)],
        semantics=("parallel",),
        operands=(dfin, up, gpre, w, b),
        side=side,
    )


def _adamw_update(w, g, m, v):
    m = ADAM_B1 * m + (1.0 - ADAM_B1) * g
    v = ADAM_B2 * v + (1.0 - ADAM_B2) * (g * g)
    m_hat = m / (1.0 - ADAM_B1 ** ADAM_STEP)
    v_hat = v / (1.0 - ADAM_B2 ** ADAM_STEP)
    delta = -ADAM_LR * (m_hat / (jnp.sqrt(v_hat) + ADAM_EPS) + ADAM_WD * w)
    return delta, m, v


def _add_pairs(send, pair, far_index, *, name):
    _, r_dim, c_dim = send.shape
    tr = r_dim // 4

    def body(far_ref, mine_ref, theirs_ref, o_ref):
        o_ref[...] = (mine_ref[...].astype(F32) + theirs_ref[...].astype(F32)).astype(o_ref.dtype)

    return pl.pallas_call(
        body,
        name=name,
        grid_spec=pltpu.PrefetchScalarGridSpec(
            num_scalar_prefetch=1,
            grid=(3, r_dim // tr),
            in_specs=[pl.BlockSpec((None, tr, c_dim), lambda j, i, far: (far[j], i, 0)),
                      pl.BlockSpec((None, tr, c_dim), lambda j, i, far: (1 + j, i, 0))],
            out_specs=pl.BlockSpec((None, tr, c_dim), lambda j, i, far: (j, i, 0)),
        ),
        out_shape=jax.ShapeDtypeStruct((3, r_dim, c_dim), BF16),
        compiler_params=_cparams("parallel", "parallel"),
    )(far_index, send, pair)


def _reduce_adamw(w, m, v, g_own, pair, far, me, *, tr, name, row0=0, earlier=None):
    r_dim, c_dim = w.shape
    rows = pair.shape[1]
    first = row0 // tr

    def body(me_ref, w_ref, m_ref, v_ref, g_ref, pair_ref, far_ref, *refs):
        grad_ref, delta_ref, nm_ref, nv_ref = refs[-4:]
        g = g_ref[...] + pair_ref[...].astype(F32)
        for j in range(3):
            g = g + far_ref[j].astype(F32)
        delta, nm, nv = _adamw_update(w_ref[...], g, m_ref[...], v_ref[...])
        grad_ref[...] = g
        delta_ref[...] = delta
        nm_ref[...] = nm
        nv_ref[...] = nv

    tile = pl.BlockSpec((tr, c_dim), lambda i, me: (first + i, 0))
    if g_own.ndim == 3:
        own_spec = pl.BlockSpec((None, tr, c_dim), lambda i, me: (me[0], i, 0))
    else:
        own_spec = pl.BlockSpec((tr, c_dim), lambda i, me: (i, 0))
    earlier = list(earlier or ())
    return pl.pallas_call(
        body,
        name=name,
        grid_spec=pltpu.PrefetchScalarGridSpec(
            num_scalar_prefetch=1,
            grid=(rows // tr,),
            in_specs=[tile, tile, tile, own_spec, pl.BlockSpec((None, tr, c_dim), lambda i, me: (0, i, 0)),
                      pl.BlockSpec((3, tr, c_dim), lambda i, me: (0, i, 0))]
            + [pl.BlockSpec(memory_space=pl.ANY)] * len(earlier),
            out_specs=[tile] * 4,
        ),
        out_shape=[jax.ShapeDtypeStruct((r_dim, c_dim), F32)] * 4,
        input_output_aliases={7 + k: k for k in range(len(earlier))},
        compiler_params=_cparams("parallel"),
    )(me, w, m, v, g_own, pair, far, *earlier)


def _adamw_many(ws, ms, vs, gs):
    n = len(ws)

    def body(*refs):
        for i in range(n):
            delta, nm, nv = _adamw_update(refs[i][...], refs[3 * n + i][...], refs[n + i][...], refs[2 * n + i][...])
            refs[4 * n + i][...] = delta
            refs[5 * n + i][...] = nm
            refs[6 * n + i][...] = nv

    vmem = pl.BlockSpec(memory_space=pltpu.VMEM)
    res = pl.pallas_call(
        body,
        name="adamw_small",
        in_specs=[vmem] * (4 * n),
        out_specs=[vmem] * (3 * n),
        out_shape=[jax.ShapeDtypeStruct(w.shape, F32) for w in ws] * 3,
        compiler_params=pltpu.CompilerParams(vmem_limit_bytes=VMEM_LIMIT),
    )(*ws, *ms, *vs, *gs)
    return res[:n], res[n:2 * n], res[2 * n:]


def _adamw_blocks(w, m, v, g, *, name, side=None):
    per = 2

    def body(w_ref, m_ref, v_ref, g_ref, delta_ref, nm_ref, nv_ref):
        delta, nm, nv = _adamw_update(w_ref[...], g_ref[...], m_ref[...], v_ref[...])
        delta_ref[...] = delta
        nm_ref[...] = nm
        nv_ref[...] = nv

    tile = pl.BlockSpec((1, per) + w.shape[2:], lambda i: (0, i, 0, 0))
    return _call(
        body,
        name=name,
        grid=(w.shape[1] // per,),
        in_specs=[tile] * 4,
        out_specs=[tile] * 3,
        out_shape=[jax.ShapeDtypeStruct(w.shape, F32)] * 3,
        semantics=("parallel",),
        operands=(w, m, v, g),
        side=side,
    )


def _coords():
    return lax.axis_index("x"), lax.axis_index("y"), lax.axis_index("c")


def _flip(coord, bit):
    return 1 - coord if bit else coord


def _relative(k):
    x, y, c = _coords()
    return _flip(x, k & 4), _flip(y, k & 2), _flip(c, k & 1)


def _index(pos):
    return 4 * pos[0] + 2 * pos[1] + pos[2]


FAR = (4, 2, 6)
AG_US_PER_MB = 38.0
RS_US_PER_MB = 46.0
MIN_RIDE_US = 30.0
MIN_GATHER_RIDE_US = 22.0
ROW_ALIGN = 32


def _chunks(items, cursor, us, us_per_mb, through=None):
    budget = float("inf") if us is None else us / us_per_mb * 2 ** 20
    names = list(items)
    if through is not None:
        names = names[:names.index(through) + 1]
    chunks = []
    for name in names:
        arr = items[name]
        r_dim, c_dim = arr.shape[-2:]
        row_bytes = c_dim * arr.dtype.itemsize
        while cursor[name] < r_dim and budget > 0:
            rows = r_dim - cursor[name]
            if r_dim > ROW_ALIGN and budget < rows * row_bytes:
                rows = min(rows, max(ROW_ALIGN, int(budget // row_bytes) // ROW_ALIGN * ROW_ALIGN))
            chunks.append((name, cursor[name], rows))
            cursor[name] += rows
            budget -= rows * row_bytes
    return chunks


class _Gather:
    def __init__(self, shards):
        self.shards, self.bufs, self.cursor = {}, {}, {}
        self.add_shards(shards)

    def add_shards(self, shards):
        for n, shard in shards.items():
            self.shards[n], self.bufs[n], self.cursor[n] = shard, None, 0

    def take(self, us=None, through=None):
        if us is not None and us < MIN_GATHER_RIDE_US:
            return None
        chunks = _chunks(self.shards, self.cursor, us, AG_US_PER_MB, through)
        return _GatherSide(self, chunks) if chunks else None

    def get(self, name):
        chunks = _chunks(self.shards, self.cursor, None, AG_US_PER_MB, through=name)
        if chunks:
            _run_side(_GatherSide(self, chunks), "gather_" + name)
        return self.bufs[name]


class _GatherSide:
    SEMS = 8

    def __init__(self, owner, chunks):
        self.owner, self.chunks = owner, chunks
        self.names = list(dict.fromkeys(n for n, _, _ in chunks))
        old = [n for n in self.names if owner.bufs[n] is not None]
        self.operands = [owner.shards[n] for n in self.names] + [owner.bufs[n] for n in old]
        self.out_shape = [jax.ShapeDtypeStruct((N_DEV,) + owner.shards[n].shape, owner.shards[n].dtype)
                          for n in self.names]
        self.aliases = {len(self.names) + i: self.names.index(n) for i, n in enumerate(old)}
        self.sems = [pltpu.SemaphoreType.DMA((self.SEMS * len(chunks),)),
                     pltpu.SemaphoreType.DMA((self.SEMS * len(chunks),)), pltpu.SemaphoreType.DMA((len(chunks),))]

    def _halves(self, ci):
        _, r0, rows = self.chunks[ci]
        if rows % ROW_ALIGN:
            return None
        return (r0, rows // 2), (r0 + rows // 2, rows // 2)

    def _copy(self, ins, outs, sems, ci, s, block, to, rows=None, from_shard=False):
        name, r0, n = self.chunks[ci]
        if rows is not None:
            r0, n = rows
        w = self.names.index(name)
        slot = outs[w].at[_index(block), pl.ds(r0, n)]
        return pltpu.make_async_remote_copy(
            src_ref=ins[w].at[pl.ds(r0, n)] if from_shard else slot, dst_ref=slot,
            send_sem=sems[0].at[self.SEMS * ci + s], recv_sem=sems[1].at[self.SEMS * ci + s],
            device_id=to, device_id_type=MESH)

    def _own(self, ins, outs, sems, ci):
        name, r0, rows = self.chunks[ci]
        w = self.names.index(name)
        return pltpu.make_async_copy(ins[w].at[pl.ds(r0, rows)], outs[w].at[_index(_relative(0)), pl.ds(r0, rows)],
                                     sems[2].at[ci])

    def _pass(self, ins, outs, sems, ci, which):
        source, target = ((4, 2), (2, 4))[which]
        return self._copy(ins, outs, sems, ci, 3 + which, _relative(source), _relative(target),
                          rows=self._halves(ci)[which])

    def start(self, ins, outs, sems):
        me = _relative(0)
        for ci in range(len(self.chunks)):
            self._own(ins, outs, sems, ci).start()
        for ci in range(len(self.chunks)):
            self._copy(ins, outs, sems, ci, 1, me, _relative(4), from_shard=True).start()
            self._copy(ins, outs, sems, ci, 2, me, _relative(2), from_shard=True).start()
            if self._halves(ci) is None:
                self._copy(ins, outs, sems, ci, 3, me, _relative(6), from_shard=True).start()
        for ci in range(len(self.chunks)):
            self._copy(ins, outs, sems, ci, 0, me, _relative(1), from_shard=True).start()

    def mid(self, ins, outs, sems):
        me = _relative(0)
        cut = [ci for ci in range(len(self.chunks)) if self._halves(ci) is not None]
        for ci in cut:
            self._copy(ins, outs, sems, ci, 1, _relative(4), me).wait_recv()
            self._pass(ins, outs, sems, ci, 0).start()
        for ci in cut:
            self._copy(ins, outs, sems, ci, 2, _relative(2), me).wait_recv()
            self._pass(ins, outs, sems, ci, 1).start()

    def finish(self, ins, outs, sems):
        me, sibling = _relative(0), _relative(1)
        n = len(self.chunks)
        for ci in range(n):
            if self._halves(ci) is None:
                for s, k in ((1, 4), (2, 2), (3, 6)):
                    self._copy(ins, outs, sems, ci, s, _relative(k), me).wait_recv()
            else:
                h0, h1 = self._halves(ci)
                self._copy(ins, outs, sems, ci, 3, _relative(6), me, rows=h0).wait_recv()
                self._copy(ins, outs, sems, ci, 4, _relative(6), me, rows=h1).wait_recv()
            for j, k in enumerate(FAR):
                self._copy(ins, outs, sems, ci, 5 + j, _relative(k), sibling).start()
        for ci in range(n):
            self._copy(ins, outs, sems, ci, 0, sibling, me).wait_recv()
            for j, k in enumerate(FAR):
                self._copy(ins, outs, sems, ci, 5 + j, _relative(k | 1), me).wait_recv()
        for ci in range(n):
            self._copy(ins, outs, sems, ci, 0, me, sibling, from_shard=True).wait_send()
            self._copy(ins, outs, sems, ci, 1, me, _relative(4), from_shard=True).wait_send()
            self._copy(ins, outs, sems, ci, 2, me, _relative(2), from_shard=True).wait_send()
            if self._halves(ci) is None:
                self._copy(ins, outs, sems, ci, 3, me, _relative(6), from_shard=True).wait_send()
            else:
                self._pass(ins, outs, sems, ci, 0).wait_send()
                self._pass(ins, outs, sems, ci, 1).wait_send()
            for j, k in enumerate(FAR):
                self._copy(ins, outs, sems, ci, 5 + j, _relative(k), sibling).wait_send()
            self._own(ins, outs, sems, ci).wait()

    def done(self, results):
        for n, buf in zip(self.names, results):
            self.owner.bufs[n] = buf


class _Scatter:
    def __init__(self, me, far_index):
        self.me, self.far_index = me, far_index
        self.sends, self.owns, self.pairs, self.sums, self.fars = {}, {}, {}, {}, {}
        self.pair_cursor, self.far_cursor = {}, {}

    def add(self, name, send, own):
        self.sends[name] = send
        self.owns[name] = own
        self.pairs[name] = self.fars[name] = None
        self.pair_cursor[name] = 0

    def _rows(self, name):
        return self.sends[name].shape[1]

    def _add_ready_pairs(self):
        for name in self.sends:
            if name not in self.sums and self.pair_cursor[name] == self._rows(name):
                self.sums[name] = _add_pairs(self.sends[name], self.pairs[name], self.far_index, name="pair_" + name)
                self.far_cursor[name] = 0

    def _side(self, us, through=None):
        self._add_ready_pairs()
        names = list(self.sends)
        if through is not None:
            names = names[:names.index(through) + 1]
        pair_chunks = [(n, self.pair_cursor[n], self._rows(n) - self.pair_cursor[n]) for n in names
                       if self.pair_cursor[n] < self._rows(n)]
        for n, _, _ in pair_chunks:
            self.pair_cursor[n] = self._rows(n)
        far_chunks = _chunks(self.sums, self.far_cursor, us, RS_US_PER_MB,
                             through if through in self.sums else None) if self.sums else []
        return _ScatterSide(self, pair_chunks, far_chunks) if pair_chunks or far_chunks else None

    def add_blocks(self, name, blocks32, blocks16):
        self.add(name, blocks16, blocks32)

    def add_cols(self, name, full32, full16):
        width = full32.shape[1] // N_DEV
        self.add(name, _blocks(full16, "cols"), lax.dynamic_slice_in_dim(full32, self.me * width, width, axis=1))

    def take(self, us):
        return self._side(us) if us >= MIN_RIDE_US else None

    def flush_pairs(self, name):
        side = self._side(0.0)
        if side is not None:
            _run_side(side, name)
        self._add_ready_pairs()

    def get(self, name):
        step = 0
        while name not in self.sums or self.far_cursor[name] < self._rows(name):
            _run_side(self._side(None, through=name), "scatter_%s_%d" % (name, step))
            step += 1
        return self.owns[name], self.pairs[name], self.fars[name]


class _ScatterSide:
    TO_SIBLING = (1, 5, 3, 7)

    def __init__(self, owner, pair_chunks, far_chunks):
        self.owner, self.pair_chunks, self.far_chunks = owner, pair_chunks, far_chunks
        self.pair_names = list(dict.fromkeys(n for n, _, _ in pair_chunks))
        self.far_names = list(dict.fromkeys(n for n, _, _ in far_chunks))
        ins = [(owner.sends[n], owner.pairs[n], (4,)) for n in self.pair_names]
        ins += [(owner.sums[n], owner.fars[n], (3,)) for n in self.far_names]
        old = [i for i, (_, buf, _) in enumerate(ins) if buf is not None]
        self.operands = [src for src, _, _ in ins] + [ins[i][1] for i in old]
        self.out_shape = [jax.ShapeDtypeStruct(slots + src.shape[1:], BF16) for src, _, slots in ins]
        self.aliases = {len(ins) + j: i for j, i in enumerate(old)}
        n_pair, n_far = 4 * len(pair_chunks), 3 * len(far_chunks)
        self.sems = [pltpu.SemaphoreType.DMA((max(n_pair, 1),)), pltpu.SemaphoreType.DMA((max(n_pair, 1),)),
                     pltpu.SemaphoreType.DMA((max(n_far, 1),)), pltpu.SemaphoreType.DMA((max(n_far, 1),))]

    def _copies(self, ins, outs, sems):
        copies = []
        for ci, (name, r0, rows) in enumerate(self.pair_chunks):
            w = self.pair_names.index(name)
            for j, k in enumerate(self.TO_SIBLING):
                copies.append(pltpu.make_async_remote_copy(
                    src_ref=ins[w].at[_index(_relative(k)), pl.ds(r0, rows)], dst_ref=outs[w].at[j, pl.ds(r0, rows)],
                    send_sem=sems[0].at[4 * ci + j], recv_sem=sems[1].at[4 * ci + j],
                    device_id=_relative(1), device_id_type=MESH))
        for ci, (name, r0, rows) in enumerate(self.far_chunks):
            w = len(self.pair_names) + self.far_names.index(name)
            for j, k in enumerate(FAR):
                copies.append(pltpu.make_async_remote_copy(
                    src_ref=ins[w].at[j, pl.ds(r0, rows)], dst_ref=outs[w].at[j, pl.ds(r0, rows)],
                    send_sem=sems[2].at[3 * ci + j], recv_sem=sems[3].at[3 * ci + j],
                    device_id=_relative(k), device_id_type=MESH))
        return copies

    def start(self, ins, outs, sems):
        for cp in self._copies(ins, outs, sems):
            cp.start()

    def mid(self, ins, outs, sems):
        pass

    def finish(self, ins, outs, sems):
        for cp in self._copies(ins, outs, sems):
            cp.wait()

    def done(self, results):
        for n, buf in zip(self.pair_names, results):
            self.owner.pairs[n] = buf
        for n, buf in zip(self.far_names, results[len(self.pair_names):]):
            self.owner.fars[n] = buf


class _Joined:
    def __init__(self, sides):
        self.sides = sides
        self.operands, self.out_shape, self.sems, self.aliases, self.spans = [], [], [], {}, []
        for s in sides:
            i0, o0, s0 = len(self.operands), len(self.out_shape), len(self.sems)
            self.operands += list(s.operands)
            self.out_shape += list(s.out_shape)
            self.sems += list(s.sems)
            self.aliases.update({i0 + i: o0 + o for i, o in s.aliases.items()})
            self.spans.append((slice(i0, len(self.operands)), slice(o0, len(self.out_shape)),
                               slice(s0, len(self.sems))))

    def start(self, ins, outs, sems):
        for s, (i, o, m) in zip(self.sides, self.spans):
            s.start(ins[i], outs[o], sems[m])

    def mid(self, ins, outs, sems):
        for s, (i, o, m) in zip(self.sides, self.spans):
            s.mid(ins[i], outs[o], sems[m])

    def finish(self, ins, outs, sems):
        for s, (i, o, m) in zip(self.sides, self.spans):
            s.finish(ins[i], outs[o], sems[m])

    def done(self, results):
        for s, (_, o, _) in zip(self.sides, self.spans):
            s.done(results[o])


def _join(*sides):
    sides = [s for s in sides if s is not None]
    if len(sides) <= 1:
        return sides[0] if sides else None
    return _Joined(sides)


PART_W = 768


def _pack_rows(vecs):
    rows = -(-sum(v.shape[0] for v in vecs) // 8) * 8

    def body(*refs):
        out = refs[-1]
        out[...] = jnp.zeros_like(out)
        r0 = 0
        for v in refs[:-1]:
            k, n = v.shape
            for p in range(-(-n // PART_W)):
                w = min(PART_W, n - PART_W * p)
                out[p, r0:r0 + k, 0:w] = v[:, PART_W * p:PART_W * p + w]
            r0 += k

    vmem = pl.BlockSpec(memory_space=pltpu.VMEM)
    return pl.pallas_call(body, name="pack_small", in_specs=[vmem] * len(vecs), out_specs=vmem,
                          out_shape=jax.ShapeDtypeStruct((N_DEV, rows, PART_W), F32))(*vecs)


def _unpack_rows(packed, shapes):
    def body(packed_ref, *outs):
        r0 = 0
        for o in outs:
            k, n = o.shape
            for p in range(-(-n // PART_W)):
                w = min(PART_W, n - PART_W * p)
                o[:, PART_W * p:PART_W * p + w] = packed_ref[p, r0:r0 + k, 0:w]
            r0 += k

    vmem = pl.BlockSpec(memory_space=pltpu.VMEM)
    return pl.pallas_call(body, name="unpack_small", in_specs=[vmem], out_specs=[vmem] * len(shapes),
                          out_shape=[jax.ShapeDtypeStruct(s, F32) for s in shapes])(packed)


class _PartsToOwners:
    def __init__(self, arrays):
        self.n = len(arrays)
        self.pers = [a.shape[0] // N_DEV for a in arrays]
        self.operands, self.aliases = list(arrays), {}
        self.out_shape = [jax.ShapeDtypeStruct((N_DEV, per) + a.shape[1:], a.dtype) for a, per in zip(arrays, self.pers)]
        self.sems = [pltpu.SemaphoreType.DMA((self.n * (N_DEV - 1),))] * 2

    def _copies(self, ins, outs, sems):
        return [pltpu.make_async_remote_copy(
            src_ref=ins[j].at[pl.ds(self.pers[j] * _index(_relative(k)), self.pers[j])], dst_ref=outs[j].at[k],
            send_sem=sems[0].at[self.n * (k - 1) + j], recv_sem=sems[1].at[self.n * (k - 1) + j],
            device_id=_relative(k), device_id_type=MESH) for k in range(1, N_DEV) for j in range(self.n)]

    def start(self, ins, outs, sems):
        for cp in self._copies(ins, outs, sems):
            cp.start()

    def mid(self, ins, outs, sems):
        pass

    def finish(self, ins, outs, sems):
        for cp in self._copies(ins, outs, sems):
            cp.wait()

    def done(self, results):
        self.stages = list(results)


def _sum_parts(arrays, stages):
    n = len(arrays)
    pers = [a.shape[0] // N_DEV for a in arrays]

    def body(*refs):
        me = _index(_relative(0))
        for j in range(n):
            acc = refs[j][pl.ds(pers[j] * me, pers[j])]
            for k in range(1, N_DEV):
                acc = acc + refs[n + j][k].astype(F32)
            refs[2 * n + j][...] = acc

    vmem = pl.BlockSpec(memory_space=pltpu.VMEM)
    return pl.pallas_call(body, name="sum_small_parts", in_specs=[vmem] * (2 * n), out_specs=[vmem] * n,
                          out_shape=[jax.ShapeDtypeStruct((per,) + a.shape[1:], F32) for a, per in zip(arrays, pers)],
                          compiler_params=pltpu.CompilerParams(vmem_limit_bytes=VMEM_LIMIT))(*arrays, *stages)


class _PartsToAll:
    def __init__(self, parts):
        self.n = len(parts)
        self.pers = [p.shape[0] for p in parts]
        self.operands, self.aliases = list(parts), {}
        self.out_shape = [jax.ShapeDtypeStruct((N_DEV * p.shape[0],) + p.shape[1:], F32) for p in parts]
        self.sems = [pltpu.SemaphoreType.DMA((self.n * (N_DEV - 1),))] * 2 + [pltpu.SemaphoreType.DMA((self.n,))]

    def _rows(self, outs, j, pos):
        return outs[j].at[pl.ds(self.pers[j] * _index(pos), self.pers[j])]

    def _copy(self, ins, outs, sems, k, j, owner):
        return pltpu.make_async_remote_copy(
            src_ref=ins[j], dst_ref=self._rows(outs, j, owner),
            send_sem=sems[0].at[self.n * (k - 1) + j], recv_sem=sems[1].at[self.n * (k - 1) + j],
            device_id=_relative(k), device_id_type=MESH)

    def _own(self, ins, outs, sems, j):
        return pltpu.make_async_copy(ins[j], self._rows(outs, j, _relative(0)), sems[2].at[j])

    def start(self, ins, outs, sems):
        for j in range(self.n):
            self._own(ins, outs, sems, j).start()
            for k in range(1, N_DEV):
                self._copy(ins, outs, sems, k, j, _relative(0)).start()

    def mid(self, ins, outs, sems):
        pass

    def finish(self, ins, outs, sems):
        for j in range(self.n):
            for k in range(1, N_DEV):
                self._copy(ins, outs, sems, k, j, _relative(k)).wait_recv()
                self._copy(ins, outs, sems, k, j, _relative(0)).wait_send()
            self._own(ins, outs, sems, j).wait()

    def done(self, results):
        self.totals = list(results)


class _SmallSync:
    def __init__(self, vec_names, mat_names):
        self.vec_names, self.mat_names = vec_names, mat_names

    def begin(self, loss, grads):
        vecs = [loss] + [grads[n] for n in self.vec_names]
        self.shapes = [v.shape for v in vecs]
        self.own = [_diag_blocks(grads[n]) for n in self.mat_names] + [_pack_rows(vecs)]
        self.to_owners = _PartsToOwners([a.astype(BF16) for a in self.own[:-1]] + self.own[-1:])
        return self.to_owners

    def middle(self):
        self.to_all = _PartsToAll(_sum_parts(self.own, self.to_owners.stages))
        return self.to_all

    def end(self):
        *mats, packed = self.to_all.totals
        sums = _unpack_rows(packed, self.shapes)
        return sums[0], dict(zip(self.vec_names, sums[1:])), dict(zip(self.mat_names, mats))


def _block_diag(w):
    groups = []
    for g in range(N_RNN_GROUPS):
        placed = [jnp.pad(w[4 * g + b], ((RNN_BLOCK_W * b, RNN_BLOCK_W * (3 - b)),) * 2) for b in range(4)]
        groups.append(placed[0] + placed[1] + placed[2] + placed[3])
    return jnp.stack(groups)


def _diag_blocks(wg):
    blocks = []
    for n in range(4 * N_RNN_GROUPS):
        g, at = n // 4, RNN_BLOCK_W * (n % 4)
        blocks.append(wg[g, at:at + RNN_BLOCK_W, at:at + RNN_BLOCK_W])
    return jnp.stack(blocks)


def _heads_major(t, n_heads):
    return t.reshape(S, n_heads, HEAD_DIM).transpose(1, 0, 2)


def _heads_minor(t):
    return t.transpose(1, 0, 2).reshape(S, t.shape[0] * HEAD_DIM)


def _natural(gathered, how):
    n, r, c = gathered.shape
    if how == "rows":
        return gathered.reshape(n * r, c)
    return gathered.transpose(1, 0, 2).reshape(r, n * c)


def _blocks(full, how):
    if how == "rows":
        return full.reshape(N_DEV, full.shape[0] // N_DEV, full.shape[1])
    return full.reshape(full.shape[0], N_DEV, full.shape[1] // N_DEV).transpose(1, 0, 2)


def _cast_many(arrays, side=None):
    steps = 4

    def body(*refs):
        n = len(refs) // 2
        for src, dst in zip(refs[:n], refs[n:]):
            dst[...] = src[...].astype(dst.dtype)

    specs = [pl.BlockSpec((a.shape[0] // steps, a.shape[1]), lambda i: (i, 0)) for a in arrays]
    return _call(
        body,
        name="cast_weights",
        grid=(steps,),
        in_specs=specs,
        out_specs=specs,
        out_shape=[jax.ShapeDtypeStruct(a.shape, MXU_DTYPE) for a in arrays],
        semantics=("parallel",),
        operands=tuple(arrays),
        side=side,
    )


def _forward_backward(x2, xb, target, small, gather, scatter, sync):
    w_in = _natural(gather.get("w_in"), "cols")
    proj, projb = _mm(xb, w_in, tm=1024, tn=512, tk=D, out_dtype=(F32, MXU_DTYPE), name="proj", side=gather.take(110))

    qt = projb[:, :OFF_K].T.reshape(N_KV, GROUP, HEAD_DIM, S)
    k2, v2 = projb[:, OFF_K:OFF_V], projb[:, OFF_V:OFF_RX]
    kp = jnp.pad(_heads_major(k2, N_KV), ((0, 0), (BLOCK, 0), (0, 0)))
    vp = jnp.pad(_heads_major(v2, N_KV), ((0, 0), (BLOCK, 0), (0, 0)))
    kt = jnp.pad(k2.T.reshape(N_KV, HEAD_DIM, S), ((0, 0), (0, 0), (BLOCK, 0)))
    vt = jnp.pad(v2.T.reshape(N_KV, HEAD_DIM, S), ((0, 0), (0, 0), (BLOCK, 0)))
    sink_row = jnp.repeat(small["attn_sinks"].reshape(N_KV, 1, GROUP), BLOCK, axis=2)
    ot = _attn_fwd(qt, kp, vt, sink_row, side=gather.take(36)).reshape(D, S)

    rconv_w = _natural(gather.get("rnn_conv_w"), "cols")
    rxc = _conv_fwd(proj, OFF_RX, rconv_w, small["rnn_conv_b"], tc=512, name="rnn_conv_fwd", side=gather.take(18))
    r, i = _lru_gates(rxc, small["lru_wa"], small["lru_wi"], small["lru_ba"], small["lru_bi"], side=gather.take(33))
    h, yrin = _lru_scan_fwd(r, i, rxc, proj, small["lru_lambda"], side=gather.take(53))

    w_ap = _natural(gather.get("w_attn_proj"), "rows")
    w_rp = _natural(gather.get("w_rnn_proj"), "rows")
    y_attn = _mm(ot, w_ap, ta=True, tm=1024, tn=1024, tk=D, name="attn_proj", side=gather.take(22))
    y_rnn = _mm(yrin, w_rp, tm=1024, tn=1024, tk=D_RNN, name="rnn_proj", side=gather.take(27))
    mixin = _gate_fwd(y_attn, y_rnn, proj, small["b_gate"], side=gather.take(25))
    w_out = _natural(gather.get("w_out"), "rows")
    mix = _mm(mixin, w_out, tm=1024, tn=1024, tk=D, name="mix_out", side=gather.take(22))
    x1, x1b, xhat1, rstd1 = _ln_fwd(x2, mix, small["ln1_g"], small["ln1_b"], side=gather.take(23))

    w_up = gather.get("ffn_w_up")
    up = _mm(x1b, w_up, tm=1024, tn=768, tk=D, b_block=768, name="ffn_up", side=gather.take(58))
    w_gate = gather.get("ffn_w_gate")
    gpre = _mm(x1b, w_gate, tm=1024, tn=768, tk=D, b_block=768, name="ffn_gate", side=gather.take(58))
    fconv_w = _natural(gather.get("ffn_conv_w"), "cols")
    fin = _ffn_act_fwd(up, gpre, fconv_w, small["ffn_conv_b"], side=gather.take())
    w_down = _natural(gather.get("ffn_w_down"), "rows")
    f = _mm(fin, w_down, tm=1024, tn=1024, tk=2048, name="ffn_down")
    loss, dpre2, dpre2b, d_ln2_g, d_ln2_b = _ln_loss_bwd(x1, f, small["ln2_g"], small["ln2_b"], target)

    grads = {"ln2_g": d_ln2_g, "ln2_b": d_ln2_b}
    both = (F32, BF16)
    g32, g16 = _mm(fin, dpre2b, ta=True, tm=1024, tn=1024, tk=S, out_dtype=both, name="d_ffn_w_down")
    scatter.add_blocks("ffn_w_down", _blocks(g32, "rows"), _blocks(g16, "rows"))
    dfin = _mm(dpre2b, w_down, tb=True, tm=1024, tn=1024, tk=D, name="d_fin", side=scatter.take(57))
    dup, dgpre, grads["ffn_conv_w"], grads["ffn_conv_b"] = _ffn_act_bwd(
        dfin, up, gpre, fconv_w, small["ffn_conv_b"], side=scatter.take(85))
    g32, g16 = _mm(x1b, dup, ta=True, tm=1024, tn=768, tk=S, out_dtype=both, out_block=768, name="d_ffn_w_up",
                   side=scatter.take(57))
    scatter.add_blocks("ffn_w_up", g32, g16)
    g32, g16 = _mm(x1b, dgpre, ta=True, tm=1024, tn=768, tk=S, out_dtype=both, out_block=768, name="d_ffn_w_gate",
                   side=scatter.take(56))
    scatter.add_blocks("ffn_w_gate", g32, g16)
    dx1 = _mm(dup, w_up, tb=True, tm=1024, tn=1024, tk=768, b_block=768, name="d_x1_up", side=scatter.take(68))
    dx1 = _mm(dgpre, w_gate, tb=True, tm=1024, tn=1024, tk=768, b_block=768, add=dx1, name="d_x1_gate",
              side=scatter.take(70))
    dpre1, dpre1b, grads["ln1_g"], grads["ln1_b"] = _ln_bwd(dx1, dpre2, xhat1, rstd1, small["ln1_g"],
                                                            side=scatter.take(24))

    g32, g16 = _mm(mixin, dpre1b, ta=True, tm=1024, tn=1024, tk=S, out_dtype=both, name="d_w_out",
                   side=scatter.take(26))
    scatter.add_blocks("w_out", _blocks(g32, "rows"), _blocks(g16, "rows"))
    dmix = _mm(dpre1b, w_out, tb=True, tm=1024, tn=1024, tk=D, name="d_mixin", side=scatter.take(22))
    dya, dyr, dgl_a, dgl_r, db_a, db_r = _gate_bwd(dmix, y_attn, y_rnn, proj, small["b_gate"], side=scatter.take(36))
    grads["b_gate"] = jnp.concatenate([db_a, db_r], axis=1)
    g32, g16 = _mm(ot, dya, tm=1024, tn=1024, tk=S, out_dtype=both, name="d_w_attn_proj", side=scatter.take(38))
    scatter.add_blocks("w_attn_proj", _blocks(g32, "rows"), _blocks(g16, "rows"))
    g32, g16 = _mm(yrin, dyr, ta=True, tm=1280, tn=1024, tk=S, out_dtype=both, name="d_w_rnn_proj",
                   side=scatter.take(27))
    scatter.add_blocks("w_rnn_proj", _blocks(g32, "rows"), _blocks(g16, "rows"))
    dot_ = _mm(w_ap, dya, tb=True, tm=1024, tn=1024, tk=D, out_dtype=MXU_DTYPE, name="d_o", side=scatter.take(22))
    dyrin = _mm(dyr, w_rp, tb=True, tm=1024, tn=1280, tk=D, name="d_yrin", side=scatter.take(27))

    dry, dzr, dzi, drxc_in, grads["lru_ba"], grads["lru_bi"], grads["lru_lambda"] = _lru_scan_bwd(
        dyrin, proj, h, r, i, rxc, small["lru_lambda"], side=scatter.take(94))
    grads["lru_wa"], grads["lru_wi"] = _lru_gate_wgrad(rxc, dzr, dzi, side=scatter.take(22))
    drxc = _lru_gate_xgrad(dzr, dzi, small["lru_wa"], small["lru_wi"], drxc_in, side=scatter.take(33))
    drx, grads["rnn_conv_w"], grads["rnn_conv_b"] = _conv_bwd(drxc, proj, OFF_RX, rconv_w, tc=512,
                                                             name="rnn_conv_bwd", side=scatter.take(29))

    dqt, dk, dv, dsink = _attn_bwd(qt, kp, kt, vp, sink_row, dot_.reshape(N_KV, GROUP, HEAD_DIM, S),
                                   side=scatter.take(65))
    grads["attn_sinks"] = dsink.reshape(1, N_KV * GROUP)
    dproj = jnp.concatenate([
        dqt.reshape(D, S).T,
        _heads_minor(dk[:, BLOCK:, :]).astype(MXU_DTYPE),
        _heads_minor(dv[:, BLOCK:, :]).astype(MXU_DTYPE),
        drx, dry, dgl_a, dgl_r], axis=1)
    for part in range(W_IN_PARTS):
        rows = slice(part * (D // W_IN_PARTS), (part + 1) * (D // W_IN_PARTS))
        side = _join(scatter.take(55), sync.begin(loss, grads)) if part == 0 else scatter.take(68)
        g32, g16 = _mm(xb[:, rows], dproj, ta=True, tm=D // W_IN_PARTS, tn=512, tk=S, out_dtype=both,
                       name="d_w_in_%d" % part, side=side)
        scatter.add_cols("w_in_%d" % part, g32, g16)
        scatter.flush_pairs("pairs_w_in_%d" % part)
    dx = _mm(dproj, w_in, tb=True, tm=1024, tn=1024, tk=512, add=dpre1, add_scale=ALPHA, name="d_x",
             side=_join(scatter.take(400), sync.middle()))
    return dx


SHARDED = (
    ("w_in", "cols", 128), ("w_attn_proj", "rows", 32), ("w_rnn_proj", "rows", 32), ("w_out", "rows", 32),
    ("ffn_w_up", "cols", 128), ("ffn_w_gate", "cols", 128), ("ffn_w_down", "rows", 64),
)
SMALL_REPLICATED = ("b_gate", "rnn_conv_b", "lru_wa", "lru_ba", "lru_wi", "lru_bi", "lru_lambda", "attn_sinks",
                    "ln1_g", "ln1_b", "ffn_conv_b", "ln2_g", "ln2_b")
SMALL_SHARDED = ("rnn_conv_w", "ffn_conv_w")
SMALL_MATS = ("lru_wa", "lru_wi")
W_IN_PARTS = 2
WEIGHTS = ("w_in", "b_gate", "rnn_conv_w", "rnn_conv_b", "lru_wa", "lru_ba", "lru_wi", "lru_bi", "lru_lambda",
           "attn_sinks", "w_attn_proj", "w_rnn_proj", "w_out", "ln1_g", "ln1_b", "ffn_w_up", "ffn_w_gate",
           "ffn_conv_w", "ffn_conv_b", "ffn_w_down", "ln2_g", "ln2_b")


def kernel(x, w_in, b_gate, rnn_conv_w, rnn_conv_b, lru_wa, lru_ba, lru_wi, lru_bi, lru_lambda, attn_sinks, w_attn_proj, w_rnn_proj, w_out, ln1_g, ln1_b, ffn_w_up, ffn_w_gate, ffn_conv_w, ffn_conv_b, ffn_w_down, ln2_g, ln2_b, loss_target, m_w_in, m_b_gate, m_rnn_conv_w, m_rnn_conv_b, m_lru_wa, m_lru_ba, m_lru_wi, m_lru_bi, m_lru_lambda, m_attn_sinks, m_w_attn_proj, m_w_rnn_proj, m_w_out, m_ln1_g, m_ln1_b, m_ffn_w_up, m_ffn_w_gate, m_ffn_conv_w, m_ffn_conv_b, m_ffn_w_down, m_ln2_g, m_ln2_b, v_w_in, v_b_gate, v_rnn_conv_w, v_rnn_conv_b, v_lru_wa, v_lru_ba, v_lru_wi, v_lru_bi, v_lru_lambda, v_attn_sinks, v_w_attn_proj, v_w_rnn_proj, v_w_out, v_ln1_g, v_ln1_b, v_ffn_w_up, v_ffn_w_gate, v_ffn_conv_w, v_ffn_conv_b, v_ffn_w_down, v_ln2_g, v_ln2_b):
    given = dict(locals())
    wsh = {n: given[n][0] for n in WEIGHTS}
    msh = {n: given["m_" + n][0] for n in WEIGHTS}
    vsh = {n: given["v_" + n][0] for n in WEIGHTS}
    m_given = {n: given["m_" + n] for n in WEIGHTS}
    v_given = {n: given["v_" + n] for n in WEIGHTS}
    me = 4 * lax.axis_index("x") + 2 * lax.axis_index("y") + lax.axis_index("c")

    order = ("w_in", "rnn_conv_w", "ffn_conv_w", "w_attn_proj", "w_rnn_proj", "w_out", "ffn_w_up", "ffn_w_gate",
             "ffn_w_down")
    gather = _Gather({n: wsh[n] if n in SMALL_SHARDED else wsh[n].astype(MXU_DTYPE) for n in order[:3]})
    *casts, xb = _cast_many([wsh[n] for n in order[3:]] + [x[0]], side=gather.take(through="ffn_conv_w"))
    gather.add_shards(dict(zip(order[3:], casts)))
    small = {n: given[n] for n in SMALL_REPLICATED}
    small["lru_wa"] = _block_diag(wsh["lru_wa"])
    small["lru_wi"] = _block_diag(wsh["lru_wi"])
    scatter = _Scatter(me, jnp.stack([_index(_relative(k)) for k in FAR]).astype(jnp.int32))

    vec_names = tuple(n for n in SMALL_REPLICATED if n not in SMALL_MATS) + SMALL_SHARDED
    sync = _SmallSync(vec_names, SMALL_MATS)
    dx = _forward_backward(x[0], xb, loss_target[0], small, gather, scatter, sync)

    loss_total, g_small, mat_sums = sync.end()
    loss_total = loss_total.reshape(())
    for n in SMALL_SHARDED:
        width = wsh[n].shape[1]
        g_small[n] = lax.dynamic_slice_in_dim(g_small[n], me * width, width, axis=1)
    g_small = {n: g_small[n].reshape(given[n].shape) for n in vec_names}
    out = {}
    results = _adamw_many(*[[d[n] for n in vec_names] for d in (given, m_given, v_given, g_small)])
    for n, delta, nm, nv in zip(vec_names, *results):
        out[n] = (g_small[n], delta, nm, nv)
    for n in SMALL_MATS:
        g = mat_sums[n].reshape(given[n].shape)
        out[n] = (g, *_adamw_blocks(given[n], m_given[n], v_given[n], g, name="adamw_" + n))

    tile_rows = {n: tr for n, _, tr in SHARDED}
    me1 = me.reshape(1).astype(jnp.int32)
    res = None
    for n in list(scatter.sends):
        own, pair, far = scatter.get(n)
        if n.startswith("w_in_"):
            row0 = int(n[len("w_in_"):]) * (D // W_IN_PARTS)
            res = _reduce_adamw(wsh["w_in"], msh["w_in"], vsh["w_in"], own, pair, far, me1, tr=tile_rows["w_in"],
                                name="adamw_" + n, row0=row0, earlier=res if row0 else None)
            out["w_in"] = tuple(r[None] for r in res)
        else:
            res_n = _reduce_adamw(wsh[n], msh[n], vsh[n], own, pair, far, me1, tr=tile_rows[n], name="adamw_" + n)
            out[n] = tuple(r[None] for r in res_n)

    outputs = [loss_total, dx[None]]
    for kind in range(4):
        outputs += [out[n][kind] for n in WEIGHTS]
    return tuple(outputs)
```

```python
import math

import jax
import jax.numpy as jnp
from jax import lax
from jax.experimental import pallas as pl
from jax.experimental.pallas import tpu as pltpu

F32 = jnp.float32
BF16 = jnp.bfloat16
MXU_DTYPE = jnp.bfloat16

N_DEV = 8
S = 2048
D = 2048
HEAD_DIM = 64
N_KV = 4
GROUP = 8
BLOCK = 128
D_KV = N_KV * HEAD_DIM
D_RNN = 2560
RNN_GROUP = 640
N_RNN_GROUPS = D_RNN // RNN_GROUP
RNN_BLOCK_W = 160
RNN_CONV_W = 4
LRU_C = 8.0
D_FF = 6144
FFN_CONV_W = 3
D_IN = 11776
OFF_K = 2048
OFF_V = 2304
OFF_RX = 2560
OFF_RY = 5120
OFF_GA = 7680
OFF_GR = 9728
LN_EPS = 1e-5
ALPHA = 2.0 ** 0.25
ADAM_LR = 0.001
ADAM_B1 = 0.9
ADAM_B2 = 0.999
ADAM_EPS = 1e-08
ADAM_WD = 0.01
ADAM_STEP = 10
NEG = -1e30
VMEM_LIMIT = 56 * 1024 * 1024
MID_RIDE_TENTHS = 6
MESH = pl.DeviceIdType.MESH
GELU_C = math.sqrt(2.0 / math.pi)


def _cparams(*sem):
    return pltpu.CompilerParams(dimension_semantics=sem or None, vmem_limit_bytes=VMEM_LIMIT)


def _call(body, *, name, grid, in_specs, out_specs, out_shape, operands, semantics, scratch_shapes=(), side=None):
    single = not isinstance(out_shape, (list, tuple))
    out_shape = [out_shape] if single else list(out_shape)
    out_specs = [out_specs] if single else list(out_specs)
    in_specs = list(in_specs)
    scratch_shapes = list(scratch_shapes)
    if side is None:
        res = pl.pallas_call(
            body, name=name, grid=grid, in_specs=in_specs, out_specs=out_specs, out_shape=out_shape,
            scratch_shapes=scratch_shapes, compiler_params=_cparams(*semantics))(*operands)
        return res[0] if single else res
    n_in, n_out, n_scr = len(in_specs), len(out_shape), len(scratch_shapes)
    s_in, s_out = len(side.operands), len(side.out_shape)
    hbm = pl.BlockSpec(memory_space=pltpu.HBM)
    steps = math.prod(grid)
    mid_step = (steps * MID_RIDE_TENTHS) // 10

    def with_copies(*refs):
        core_in, side_in = refs[:n_in], refs[n_in:n_in + s_in]
        o0 = n_in + s_in
        core_out, side_out = refs[o0:o0 + n_out], refs[o0 + n_out:o0 + n_out + s_out]
        c0 = o0 + n_out + s_out
        core_scr, sems = refs[c0:c0 + n_scr], refs[c0 + n_scr:]
        step = 0
        for d, size in enumerate(grid):
            step = step * size + pl.program_id(d)

        @pl.when(step == 0)
        def _():
            side.start(side_in, side_out, sems)

        body(*core_in, *core_out, *core_scr)

        @pl.when(step == mid_step)
        def _():
            side.mid(side_in, side_out, sems)

        @pl.when(step == steps - 1)
        def _():
            side.finish(side_in, side_out, sems)

    res = pl.pallas_call(
        with_copies, name=name, grid=grid,
        in_specs=in_specs + [hbm] * s_in, out_specs=out_specs + [hbm] * s_out,
        out_shape=out_shape + list(side.out_shape),
        scratch_shapes=scratch_shapes + list(side.sems),
        input_output_aliases={n_in + i: n_out + o for i, o in side.aliases.items()},
        compiler_params=_cparams(*(("arbitrary",) * len(grid))))(*operands, *side.operands)
    side.done(res[n_out:])
    return res[0] if single else res[:n_out]


def _run_side(side, name):
    def body(*refs):
        s_in, s_out = len(side.operands), len(side.out_shape)
        side.start(refs[:s_in], refs[s_in:s_in + s_out], refs[s_in + s_out:])
        side.mid(refs[:s_in], refs[s_in:s_in + s_out], refs[s_in + s_out:])
        side.finish(refs[:s_in], refs[s_in:s_in + s_out], refs[s_in + s_out:])

    hbm = pl.BlockSpec(memory_space=pltpu.HBM)
    res = pl.pallas_call(
        body, name=name, in_specs=[hbm] * len(side.operands), out_specs=[hbm] * len(side.out_shape),
        out_shape=list(side.out_shape), scratch_shapes=list(side.sems),
        input_output_aliases=dict(side.aliases))(*side.operands)
    side.done(res)


def _gelu(x):
    x2 = x * x
    t = jnp.tanh(GELU_C * (x + 0.044715 * x * x2))
    g = 0.5 * x * (1.0 + t)
    dg = 0.5 * (1.0 + t) + 0.5 * x * (1.0 - t * t) * (GELU_C * (1.0 + 3.0 * 0.044715 * x2))
    return g, dg


def _sigmoid(x):
    return 1.0 / (1.0 + jnp.exp(-x))


def _softplus(x):
    z = jnp.exp(-jnp.abs(x))
    small = z * (1.0 - z * (0.5 - z * (1.0 / 3.0 - 0.25 * z)))
    return jnp.maximum(x, 0.0) + jnp.where(z < 0.02, small, jnp.log(1.0 + z))


def _one_minus_exp(x):
    series = -x * (1.0 + x * (0.5 + x * (1.0 / 6.0 + x * (1.0 / 24.0))))
    return jnp.where(x > -0.03, series, 1.0 - jnp.exp(x))


def _colsum(v):
    return jnp.sum(v, axis=0, keepdims=True)


def _mm(a, b, *, tm, tn, tk, name, ta=False, tb=False, out_dtype=F32, b_block=None, out_block=None, add=None,
        add_scale=1.0, side=None):
    out_dtypes = out_dtype if isinstance(out_dtype, tuple) else (out_dtype,)
    if ta:
        k_dim, m_dim = a.shape
    else:
        m_dim, k_dim = a.shape
    if b_block is None:
        n_dim = b.shape[0] if tb else b.shape[1]
    else:
        n_dim = b.shape[1] if tb else b.shape[0] * b_block
    assert m_dim % tm == 0 and n_dim % tn == 0 and k_dim % tk == 0, (name, m_dim, n_dim, k_dim)
    nk = k_dim // tk
    dims = (((0 if ta else 1,), (1 if tb else 0,)), ((), ()))
    has_add = add is not None

    def body(*refs):
        a_ref, b_ref = refs[0], refs[1]
        add_ref = refs[2] if has_add else None
        first_out = 3 if has_add else 2
        o_refs = refs[first_out:first_out + len(out_dtypes)]

        def product():
            return lax.dot_general(a_ref[...].astype(MXU_DTYPE), b_ref[...].astype(MXU_DTYPE), dims,
                                   preferred_element_type=F32)

        def finish(acc):
            if has_add:
                acc = acc + add_scale * add_ref[...]
            for o_ref in o_refs:
                o_ref[...] = acc.astype(o_ref.dtype)

        if nk == 1:
            finish(product())
        else:
            acc_ref = refs[-1]
            k = pl.program_id(2)

            @pl.when(k == 0)
            def _():
                acc_ref[...] = jnp.zeros_like(acc_ref)

            acc_ref[...] += product()

            @pl.when(k == nk - 1)
            def _():
                finish(acc_ref[...])

    if ta:
        a_spec = pl.BlockSpec((tk, tm), lambda i, j, k: (k, i))
    else:
        a_spec = pl.BlockSpec((tm, tk), lambda i, j, k: (i, k))
    if b_block is None:
        if tb:
            b_spec = pl.BlockSpec((tn, tk), lambda i, j, k: (j, k))
        else:
            b_spec = pl.BlockSpec((tk, tn), lambda i, j, k: (k, j))
    elif tb:
        assert b_block % tk == 0
        b_spec = pl.BlockSpec((None, tn, tk), lambda i, j, k: ((k * tk) // b_block, j, ((k * tk) % b_block) // tk))
    else:
        assert b_block % tn == 0
        b_spec = pl.BlockSpec((None, tk, tn), lambda i, j, k: ((j * tn) // b_block, k, ((j * tn) % b_block) // tn))
    in_specs = [a_spec, b_spec]
    operands = [a, b]
    if has_add:
        in_specs.append(pl.BlockSpec((tm, tn), lambda i, j, k: (i, j)))
        operands.append(add)
    if out_block is None:
        out_spec = pl.BlockSpec((tm, tn), lambda i, j, k: (i, j))
        out_dims = (m_dim, n_dim)
    else:
        assert out_block % tn == 0
        out_spec = pl.BlockSpec((None, tm, tn), lambda i, j, k: ((j * tn) // out_block, i, ((j * tn) % out_block) // tn))
        out_dims = (n_dim // out_block, m_dim, out_block)
    res = _call(
        body,
        name=name,
        grid=(m_dim // tm, n_dim // tn, nk),
        in_specs=in_specs,
        out_specs=[out_spec] * len(out_dtypes),
        out_shape=[jax.ShapeDtypeStruct(out_dims, dt) for dt in out_dtypes],
        scratch_shapes=[pltpu.VMEM((tm, tn), F32)] if nk > 1 else [],
        semantics=("parallel", "parallel", "arbitrary"),
        operands=tuple(operands),
        side=side,
    )
    return res if isinstance(out_dtype, tuple) else res[0]


def _attn_bias(bias_ref, h):
    key = lax.broadcasted_iota(jnp.int32, (2 * BLOCK, GROUP * BLOCK), 0)
    col = lax.broadcasted_iota(jnp.int32, (2 * BLOCK, GROUP * BLOCK), 1)
    dist = BLOCK + (col & (BLOCK - 1)) - key
    head = h * GROUP + (col >> 7) + 1
    slope = jnp.exp(head.astype(F32) * (-0.25 * math.log(2.0)))
    bias = jnp.where((dist >= 0) & (dist < BLOCK), -slope * dist.astype(F32), NEG)
    bias_ref[1] = bias
    bias_ref[0] = jnp.where(key < BLOCK, NEG, bias)


def _attn_probs(kb, qt, bias, sink):
    s = jnp.dot(kb, qt, preferred_element_type=F32) * (HEAD_DIM ** -0.5) + bias
    m = jnp.maximum(jnp.max(s, axis=0, keepdims=True), sink)
    e = jnp.exp(s - m)
    e_sink = jnp.exp(sink - m)
    inv = 1.0 / (jnp.sum(e, axis=0, keepdims=True) + e_sink)
    return e * inv, e_sink * inv


def _heads_on_lanes(ref, r0):
    return jnp.concatenate([ref[g, :, pl.ds(r0, BLOCK)] for g in range(GROUP)], axis=1)


def _attn_fwd(qt, kp, vt, sink_row, side=None):
    cols = GROUP * BLOCK

    def body(q_ref, k_ref, vt_ref, sink_ref, o_ref, bias_ref):
        _attn_bias(bias_ref, pl.program_id(0))
        sink = sink_ref[...]

        def step(n, carry):
            r0 = pl.multiple_of(n * BLOCK, BLOCK)
            p, _ = _attn_probs(k_ref[pl.ds(r0, 2 * BLOCK), :], _heads_on_lanes(q_ref, r0),
                               bias_ref[jnp.minimum(n, 1)], sink)
            o = jnp.dot(vt_ref[:, pl.ds(r0, 2 * BLOCK)], p.astype(MXU_DTYPE), preferred_element_type=F32)
            for g in range(GROUP):
                o_ref[g, :, pl.ds(r0, BLOCK)] = o[:, g * BLOCK:(g + 1) * BLOCK].astype(o_ref.dtype)
            return carry

        lax.fori_loop(0, S // BLOCK, step, 0)

    hm = pl.BlockSpec((None, GROUP, HEAD_DIM, S), lambda h: (h, 0, 0, 0))
    return _call(
        body,
        name="attn_fwd",
        grid=(N_KV,),
        in_specs=[
            hm,
            pl.BlockSpec((None, BLOCK + S, HEAD_DIM), lambda h: (h, 0, 0)),
            pl.BlockSpec((None, HEAD_DIM, BLOCK + S), lambda h: (h, 0, 0)),
            pl.BlockSpec((None, 1, cols), lambda h: (h, 0, 0)),
        ],
        out_specs=hm,
        out_shape=jax.ShapeDtypeStruct((N_KV, GROUP, HEAD_DIM, S), MXU_DTYPE),
        scratch_shapes=[pltpu.VMEM((2, 2 * BLOCK, cols), F32)],
        semantics=("parallel",),
        operands=(qt, kp, vt, sink_row),
        side=side,
    )


def _attn_bwd(qt, kp, kt, vp, sink_row, dot_, side=None):
    cols = GROUP * BLOCK

    def body(q_ref, k_ref, kt_ref, v_ref, sink_ref, do_ref, dq_ref, dk_ref, dv_ref, dsink_ref, bias_ref):
        _attn_bias(bias_ref, pl.program_id(0))
        sink = sink_ref[...]
        dk_ref[...] = jnp.zeros_like(dk_ref)
        dv_ref[...] = jnp.zeros_like(dv_ref)
        nt = (((1,), (1,)), ((), ()))

        def step(n, sink_acc):
            r0 = pl.multiple_of(n * BLOCK, BLOCK)
            band = pl.ds(r0, 2 * BLOCK)
            qn = _heads_on_lanes(q_ref, r0)
            don = _heads_on_lanes(do_ref, r0)
            p, p_sink = _attn_probs(k_ref[band, :], qn, bias_ref[jnp.minimum(n, 1)], sink)
            dp = jnp.dot(v_ref[band, :], don, preferred_element_type=F32)
            delta = jnp.sum(p * dp, axis=0, keepdims=True)
            ds = (p * (dp - delta) * (HEAD_DIM ** -0.5)).astype(MXU_DTYPE)
            dq = jnp.dot(kt_ref[:, band], ds, preferred_element_type=F32)
            for g in range(GROUP):
                dq_ref[g, :, pl.ds(r0, BLOCK)] = dq[:, g * BLOCK:(g + 1) * BLOCK].astype(dq_ref.dtype)
            dk_ref[band, :] += lax.dot_general(ds, qn, nt, preferred_element_type=F32)
            dv_ref[band, :] += lax.dot_general(p.astype(MXU_DTYPE), don, nt, preferred_element_type=F32)
            return sink_acc - p_sink * delta

        sink_acc = lax.fori_loop(0, S // BLOCK, step, jnp.zeros((1, cols), F32))
        for g in range(GROUP):
            dsink_ref[g:g + 1, :] = jnp.sum(sink_acc[:, g * BLOCK:(g + 1) * BLOCK], axis=1, keepdims=True)

    hm = pl.BlockSpec((None, GROUP, HEAD_DIM, S), lambda h: (h, 0, 0, 0))
    kv = pl.BlockSpec((None, BLOCK + S, HEAD_DIM), lambda h: (h, 0, 0))
    return _call(
        body,
        name="attn_bwd",
        grid=(N_KV,),
        in_specs=[hm, kv, pl.BlockSpec((None, HEAD_DIM, BLOCK + S), lambda h: (h, 0, 0)), kv,
                  pl.BlockSpec((None, 1, cols), lambda h: (h, 0, 0)), hm],
        out_specs=[hm, kv, kv, pl.BlockSpec((None, GROUP, 1), lambda h: (h, 0, 0))],
        out_shape=[
            jax.ShapeDtypeStruct((N_KV, GROUP, HEAD_DIM, S), MXU_DTYPE),
            jax.ShapeDtypeStruct((N_KV, BLOCK + S, HEAD_DIM), F32),
            jax.ShapeDtypeStruct((N_KV, BLOCK + S, HEAD_DIM), F32),
            jax.ShapeDtypeStruct((N_KV, GROUP, 1), F32),
        ],
        scratch_shapes=[pltpu.VMEM((2, 2 * BLOCK, cols), F32)],
        semantics=("parallel",),
        operands=(qt, kp, kt, vp, sink_row, dot_),
        side=side,
    )


PAD = 8
CHUNK = 256


def _past_taps(xpad_ref, r0, width):
    ext = xpad_ref[pl.ds(r0, CHUNK + PAD), :]
    taps = []
    for k in range(width):
        back = width - 1 - k
        taps.append((ext if back == 0 else pltpu.roll(ext, back, 0))[PAD:, :])
    return taps


def _future_taps(xpad_ref, r0, width):
    ext = xpad_ref[pl.ds(r0, CHUNK + PAD), :]
    taps = []
    for ahead in range(width):
        taps.append((ext if ahead == 0 else pltpu.roll(ext, CHUNK + PAD - ahead, 0))[:CHUNK, :])
    return taps


def _conv_fwd(src, col0, w, b, *, tc, name, side=None):
    width, c_dim = w.shape

    def body(x_ref, w_ref, b_ref, o_ref, xpad_ref):
        xpad_ref[pl.ds(0, PAD), :] = jnp.zeros((PAD, tc), F32)
        xpad_ref[pl.ds(PAD, S), :] = x_ref[...]
        wv = w_ref[...]
        bv = b_ref[...]

        def step(ci, carry):
            r0 = pl.multiple_of(ci * CHUNK, CHUNK)
            taps = _past_taps(xpad_ref, r0, width)
            y = bv + taps[0] * wv[0:1, :]
            for k in range(1, width):
                y = y + taps[k] * wv[k:k + 1, :]
            o_ref[pl.ds(r0, CHUNK), :] = y
            return carry

        lax.fori_loop(0, S // CHUNK, step, 0)

    return _call(
        body,
        name=name,
        grid=(c_dim // tc,),
        in_specs=[
            pl.BlockSpec((S, tc), lambda j: (0, col0 // tc + j)),
            pl.BlockSpec((width, tc), lambda j: (0, j)),
            pl.BlockSpec((1, tc), lambda j: (0, j)),
        ],
        out_specs=pl.BlockSpec((S, tc), lambda j: (0, j)),
        out_shape=jax.ShapeDtypeStruct((S, c_dim), F32),
        scratch_shapes=[pltpu.VMEM((S + PAD, tc), F32)],
        semantics=("parallel",),
        operands=(src, w, b),
        side=side,
    )


def _conv_bwd(dy, src, col0, w, *, tc, name, side=None):
    width, c_dim = w.shape

    def body(dy_ref, x_ref, w_ref, dx_ref, dw_ref, db_ref, xpad_ref, dpad_ref):
        xpad_ref[pl.ds(0, PAD), :] = jnp.zeros((PAD, tc), F32)
        xpad_ref[pl.ds(PAD, S), :] = x_ref[...]
        dpad_ref[pl.ds(0, S), :] = dy_ref[...]
        dpad_ref[pl.ds(S, PAD), :] = jnp.zeros((PAD, tc), F32)
        wv = w_ref[...]

        def step(ci, acc):
            r0 = pl.multiple_of(ci * CHUNK, CHUNK)
            past = _past_taps(xpad_ref, r0, width)
            ahead = _future_taps(dpad_ref, r0, width)
            d = ahead[0]
            dx = d * wv[width - 1:width, :]
            for j in range(1, width):
                dx = dx + ahead[j] * wv[width - 1 - j:width - j, :]
            dx_ref[pl.ds(r0, CHUNK), :] = dx.astype(dx_ref.dtype)
            return tuple(acc[k] + _colsum(past[k] * d) for k in range(width)) + (acc[width] + _colsum(d),)

        zero = jnp.zeros((1, tc), F32)
        acc = lax.fori_loop(0, S // CHUNK, step, (zero,) * (width + 1))
        for k in range(width):
            dw_ref[k:k + 1, :] = acc[k]
        db_ref[...] = acc[width]

    return _call(
        body,
        name=name,
        grid=(c_dim // tc,),
        in_specs=[
            pl.BlockSpec((S, tc), lambda j: (0, j)),
            pl.BlockSpec((S, tc), lambda j: (0, col0 // tc + j)),
            pl.BlockSpec((width, tc), lambda j: (0, j)),
        ],
        out_specs=[
            pl.BlockSpec((S, tc), lambda j: (0, j)),
            pl.BlockSpec((width, tc), lambda j: (0, j)),
            pl.BlockSpec((1, tc), lambda j: (0, j)),
        ],
        out_shape=[
            jax.ShapeDtypeStruct((S, c_dim), MXU_DTYPE),
            jax.ShapeDtypeStruct((width, c_dim), F32),
            jax.ShapeDtypeStruct((1, c_dim), F32),
        ],
        scratch_shapes=[pltpu.VMEM((S + PAD, tc), F32), pltpu.VMEM((S + PAD, tc), F32)],
        semantics=("parallel",),
        operands=(dy, src, w),
        side=side,
    )


SCAN_TC = 256


def _lru_gates(rxc, wa, wi, ba, bi, side=None):
    tm = 512

    def body(x_ref, wa_ref, wi_ref, ba_ref, bi_ref, r_ref, i_ref):
        xv = x_ref[...].astype(MXU_DTYPE)
        r_ref[...] = _sigmoid(jnp.dot(xv, wa_ref[...].astype(MXU_DTYPE), preferred_element_type=F32) + ba_ref[...])
        i_ref[...] = _sigmoid(jnp.dot(xv, wi_ref[...].astype(MXU_DTYPE), preferred_element_type=F32) + bi_ref[...])

    x_spec = pl.BlockSpec((tm, RNN_GROUP), lambda g, i: (i, g))
    w_spec = pl.BlockSpec((None, RNN_GROUP, RNN_GROUP), lambda g, i: (g, 0, 0))
    b_spec = pl.BlockSpec((1, RNN_GROUP), lambda g, i: (0, g))
    return _call(
        body,
        name="lru_gates",
        grid=(N_RNN_GROUPS, S // tm),
        in_specs=[x_spec, w_spec, w_spec, b_spec, b_spec],
        out_specs=[x_spec, x_spec],
        out_shape=[jax.ShapeDtypeStruct((S, D_RNN), F32)] * 2,
        semantics=("parallel", "parallel"),
        operands=(rxc, wa, wi, ba, bi),
        side=side,
    )


def _scan_down(a, u, row):
    for d in (1, 2, 4):
        a_s = jnp.where(row >= d, pltpu.roll(a, d, 0), 1.0)
        u_s = jnp.where(row >= d, pltpu.roll(u, d, 0), 0.0)
        u = a * u_s + u
        a = a * a_s
    return a, u


def _scan_up(a, u, row):
    for d in (1, 2, 4):
        a_s = jnp.where(row < 8 - d, pltpu.roll(a, 8 - d, 0), 1.0)
        u_s = jnp.where(row < 8 - d, pltpu.roll(u, 8 - d, 0), 0.0)
        u = a * u_s + u
        a = a * a_s
    return a, u


def _lru_scan_fwd(r, i, rxc, proj, lam, side=None):
    tc = SCAN_TC

    def body(r_ref, i_ref, x_ref, ry_ref, lam_ref, h_ref, y_ref):
        rate = LRU_C * _softplus(-lam_ref[...])
        row = lax.broadcasted_iota(jnp.int32, (8, tc), 0)

        def step(ci, carry):
            r0 = pl.multiple_of(ci * 16, 16)
            log_a = -rate * r_ref[pl.ds(r0, 16), :]
            a16 = jnp.exp(log_a)
            u16 = jnp.sqrt(_one_minus_exp(2.0 * log_a)) * (i_ref[pl.ds(r0, 16), :] * x_ref[pl.ds(r0, 16), :])
            hs = []
            for half in range(2):
                a_cum, h0 = _scan_down(a16[8 * half:8 * half + 8, :], u16[8 * half:8 * half + 8, :], row)
                h = a_cum * carry + h0
                carry = jnp.broadcast_to(h[7:8, :], (8, tc))
                hs.append(h)
            h16 = jnp.concatenate(hs, axis=0)
            h_ref[pl.ds(r0, 16), :] = h16
            y_ref[pl.ds(r0, 16), :] = (h16 * _gelu(ry_ref[pl.ds(r0, 16), :])[0]).astype(y_ref.dtype)
            return carry

        lax.fori_loop(0, S // 16, step, jnp.zeros((8, tc), F32))

    col = pl.BlockSpec((S, tc), lambda j: (0, j))
    return _call(
        body,
        name="lru_scan_fwd",
        grid=(D_RNN // tc,),
        in_specs=[col, col, col, pl.BlockSpec((S, tc), lambda j: (0, OFF_RY // tc + j)),
                  pl.BlockSpec((1, tc), lambda j: (0, j))],
        out_specs=[col, col],
        out_shape=[jax.ShapeDtypeStruct((S, D_RNN), F32), jax.ShapeDtypeStruct((S, D_RNN), MXU_DTYPE)],
        semantics=("parallel",),
        operands=(r, i, rxc, proj, lam),
        side=side,
    )


def _lru_scan_bwd(dy, proj, h, r, i, rxc, lam, side=None):
    tc = SCAN_TC

    def body(dy_ref, ry_ref, h_ref, r_ref, i_ref, x_ref, lam_ref,
             dry_ref, dzr_ref, dzi_ref, dx_ref, dba_ref, dbi_ref, dlam_ref, a_ref, dh_ref, hp_ref):
        lam_v = lam_ref[...]
        rate = LRU_C * _softplus(-lam_v)
        dlam_scale = LRU_C * _sigmoid(-lam_v)
        row = lax.broadcasted_iota(jnp.int32, (8, tc), 0)
        hp_ref[pl.ds(0, PAD), :] = jnp.zeros((PAD, tc), F32)
        hp_ref[pl.ds(PAD, S), :] = h_ref[...]
        a_ref[pl.ds(S, PAD), :] = jnp.zeros((PAD, tc), F32)

        def prep(ci, carry):
            r0 = pl.multiple_of(ci * CHUNK, CHUNK)
            a_ref[pl.ds(r0, CHUNK), :] = jnp.exp(-rate * r_ref[pl.ds(r0, CHUNK), :])
            ge, dge = _gelu(ry_ref[pl.ds(r0, CHUNK), :])
            dyv = dy_ref[pl.ds(r0, CHUNK), :]
            dh_ref[pl.ds(r0, CHUNK), :] = dyv * ge
            dry_ref[pl.ds(r0, CHUNK), :] = (dyv * h_ref[pl.ds(r0, CHUNK), :] * dge).astype(dry_ref.dtype)
            return carry

        lax.fori_loop(0, S // CHUNK, prep, 0)

        def step(ci, state):
            carry, dba, dbi, dlam = state
            r0 = pl.multiple_of(S - 16 - ci * 16, 16)
            a_ext = a_ref[pl.ds(r0, 24), :]
            a_next = pltpu.roll(a_ext, 23, 0)
            h_prev = pltpu.roll(hp_ref[pl.ds(r0, 24), :], 1, 0)
            dh16 = dh_ref[pl.ds(r0, 16), :]
            gs = [None, None]
            for half in (1, 0):
                lo = 8 * half
                c_cum, g0 = _scan_up(a_next[lo:lo + 8, :], dh16[lo:lo + 8, :], row)
                g = c_cum * carry + g0
                carry = jnp.broadcast_to(g[0:1, :], (8, tc))
                gs[half] = g
            g16 = jnp.concatenate(gs, axis=0)
            a16 = a_ext[0:16, :]
            r16 = r_ref[pl.ds(r0, 16), :]
            i16 = i_ref[pl.ds(r0, 16), :]
            x16 = x_ref[pl.ds(r0, 16), :]
            a2 = a16 * a16
            sq = jnp.sqrt(_one_minus_exp(-2.0 * rate * r16))
            dx_ref[pl.ds(r0, 16), :] = g16 * sq * i16
            dzi = g16 * sq * x16 * i16 * (1.0 - i16)
            dlog_a = g16 * h_prev[8:24, :] * a16 - g16 * i16 * x16 * a2 / sq
            dzr = -rate * dlog_a * r16 * (1.0 - r16)
            dzr_ref[pl.ds(r0, 16), :] = dzr.astype(dzr_ref.dtype)
            dzi_ref[pl.ds(r0, 16), :] = dzi.astype(dzi_ref.dtype)
            return carry, dba + _colsum(dzr), dbi + _colsum(dzi), dlam + _colsum(dlog_a * r16)

        zero = jnp.zeros((1, tc), F32)
        _, dba, dbi, dlam = lax.fori_loop(0, S // 16, step, (jnp.zeros((8, tc), F32), zero, zero, zero))
        dba_ref[...] = dba
        dbi_ref[...] = dbi
        dlam_ref[...] = dlam * dlam_scale

    col = pl.BlockSpec((S, tc), lambda j: (0, j))
    vec = pl.BlockSpec((1, tc), lambda j: (0, j))
    return _call(
        body,
        name="lru_scan_bwd",
        grid=(D_RNN // tc,),
        in_specs=[col, pl.BlockSpec((S, tc), lambda j: (0, OFF_RY // tc + j)), col, col, col, col, vec],
        out_specs=[col, col, col, col, vec, vec, vec],
        out_shape=[jax.ShapeDtypeStruct((S, D_RNN), MXU_DTYPE)] * 3 + [jax.ShapeDtypeStruct((S, D_RNN), F32)]
        + [jax.ShapeDtypeStruct((1, D_RNN), F32)] * 3,
        scratch_shapes=[pltpu.VMEM((S + PAD, tc), F32), pltpu.VMEM((S, tc), F32), pltpu.VMEM((S + PAD, tc), F32)],
        semantics=("parallel",),
        operands=(dy, proj, h, r, i, rxc, lam),
        side=side,
    )


def _lru_gate_wgrad(rxc, dzr, dzi, side=None):
    def body(x_ref, dzr_ref, dzi_ref, dwa_ref, dwi_ref):
        xv = x_ref[...].astype(MXU_DTYPE)
        dims = (((0,), (0,)), ((), ()))
        dwa_ref[...] = lax.dot_general(xv, dzr_ref[...], dims, preferred_element_type=F32)
        dwi_ref[...] = lax.dot_general(xv, dzi_ref[...], dims, preferred_element_type=F32)

    col = pl.BlockSpec((S, RNN_GROUP), lambda g: (0, g))
    w_spec = pl.BlockSpec((None, RNN_GROUP, RNN_GROUP), lambda g: (g, 0, 0))
    return _call(
        body,
        name="lru_gate_wgrad",
        grid=(N_RNN_GROUPS,),
        in_specs=[col, col, col],
        out_specs=[w_spec, w_spec],
        out_shape=[jax.ShapeDtypeStruct((N_RNN_GROUPS, RNN_GROUP, RNN_GROUP), F32)] * 2,
        semantics=("parallel",),
        operands=(rxc, dzr, dzi),
        side=side,
    )


def _lru_gate_xgrad(dzr, dzi, wa, wi, dx_in, side=None):
    tm = 512

    def body(dzr_ref, dzi_ref, wa_ref, wi_ref, dx_ref, o_ref):
        dims = (((1,), (1,)), ((), ()))
        o_ref[...] = (dx_ref[...]
                      + lax.dot_general(dzr_ref[...], wa_ref[...].astype(MXU_DTYPE), dims, preferred_element_type=F32)
                      + lax.dot_general(dzi_ref[...], wi_ref[...].astype(MXU_DTYPE), dims, preferred_element_type=F32))

    x_spec = pl.BlockSpec((tm, RNN_GROUP), lambda g, i: (i, g))
    w_spec = pl.BlockSpec((None, RNN_GROUP, RNN_GROUP), lambda g, i: (g, 0, 0))
    return _call(
        body,
        name="lru_gate_xgrad",
        grid=(N_RNN_GROUPS, S // tm),
        in_specs=[x_spec, x_spec, w_spec, w_spec, x_spec],
        out_specs=x_spec,
        out_shape=jax.ShapeDtypeStruct((S, D_RNN), F32),
        semantics=("parallel", "parallel"),
        operands=(dzr, dzi, wa, wi, dx_in),
        side=side,
    )


def _gate_fwd(y_attn, y_rnn, proj, b_gate, side=None):
    t = 512

    def body(ya_ref, yr_ref, ga_ref, gr_ref, ba_ref, br_ref, o_ref):
        o_ref[...] = (_sigmoid(ga_ref[...] + ba_ref[...]) * ya_ref[...]
                      + _sigmoid(gr_ref[...] + br_ref[...]) * yr_ref[...]).astype(o_ref.dtype)

    tile = pl.BlockSpec((t, t), lambda i, j: (i, j))
    return _call(
        body,
        name="gate_fwd",
        grid=(S // t, D // t),
        in_specs=[tile, tile,
                  pl.BlockSpec((t, t), lambda i, j: (i, OFF_GA // t + j)),
                  pl.BlockSpec((t, t), lambda i, j: (i, OFF_GR // t + j)),
                  pl.BlockSpec((1, t), lambda i, j: (0, j)),
                  pl.BlockSpec((1, t), lambda i, j: (0, D // t + j))],
        out_specs=tile,
        out_shape=jax.ShapeDtypeStruct((S, D), MXU_DTYPE),
        semantics=("parallel", "parallel"),
        operands=(y_attn, y_rnn, proj, proj, b_gate, b_gate),
        side=side,
    )


def _gate_bwd(dmix, y_attn, y_rnn, proj, b_gate, side=None):
    t = 512

    def body(dm_ref, ya_ref, yr_ref, ga_ref, gr_ref, ba_ref, br_ref,
             dya_ref, dyr_ref, dga_ref, dgr_ref, dba_ref, dbr_ref):
        @pl.when(pl.program_id(1) == 0)
        def _():
            dba_ref[...] = jnp.zeros_like(dba_ref)
            dbr_ref[...] = jnp.zeros_like(dbr_ref)

        dm = dm_ref[...]
        ga = _sigmoid(ga_ref[...] + ba_ref[...])
        gr = _sigmoid(gr_ref[...] + br_ref[...])
        dya_ref[...] = (dm * ga).astype(dya_ref.dtype)
        dyr_ref[...] = (dm * gr).astype(dyr_ref.dtype)
        dga = dm * ya_ref[...] * ga * (1.0 - ga)
        dgr = dm * yr_ref[...] * gr * (1.0 - gr)
        dga_ref[...] = dga.astype(dga_ref.dtype)
        dgr_ref[...] = dgr.astype(dgr_ref.dtype)
        dba_ref[...] += _colsum(dga)
        dbr_ref[...] += _colsum(dgr)

    tile = pl.BlockSpec((t, t), lambda j, i: (i, j))
    vec = pl.BlockSpec((1, t), lambda j, i: (0, j))
    return _call(
        body,
        name="gate_bwd",
        grid=(D // t, S // t),
        in_specs=[tile, tile, tile,
                  pl.BlockSpec((t, t), lambda j, i: (i, OFF_GA // t + j)),
                  pl.BlockSpec((t, t), lambda j, i: (i, OFF_GR // t + j)),
                  vec,
                  pl.BlockSpec((1, t), lambda j, i: (0, D // t + j))],
        out_specs=[tile, tile, tile, tile, vec, vec],
        out_shape=[jax.ShapeDtypeStruct((S, D), MXU_DTYPE)] * 4 + [jax.ShapeDtypeStruct((1, D), F32)] * 2,
        semantics=("parallel", "arbitrary"),
        operands=(dmix, y_attn, y_rnn, proj, proj, b_gate, b_gate),
        side=side,
    )


LN_TM = 256


def _ln_stats(pre):
    mu = jnp.mean(pre, axis=-1, keepdims=True)
    xc = pre - mu
    rstd = lax.rsqrt(jnp.mean(xc * xc, axis=-1, keepdims=True) + LN_EPS)
    return xc * rstd, rstd


def _ln_input_grad(dy, xhat, rstd, g):
    dyg = dy * g
    return rstd * (dyg - jnp.mean(dyg, axis=-1, keepdims=True)
                   - xhat * jnp.mean(dyg * xhat, axis=-1, keepdims=True))


def _ln_fwd(res, branch, g, b, side=None):
    def body(res_ref, br_ref, g_ref, b_ref, y_ref, yb_ref, xhat_ref, rstd_ref):
        xhat, rstd = _ln_stats(ALPHA * res_ref[...] + br_ref[...])
        y = xhat * g_ref[...] + b_ref[...]
        y_ref[...] = y
        yb_ref[...] = y.astype(yb_ref.dtype)
        xhat_ref[...] = xhat
        rstd_ref[...] = rstd

    tile = pl.BlockSpec((LN_TM, D), lambda i: (i, 0))
    vec = pl.BlockSpec((1, D), lambda i: (0, 0))
    return _call(
        body,
        name="ln_fwd",
        grid=(S // LN_TM,),
        in_specs=[tile, tile, vec, vec],
        out_specs=[tile, tile, tile, pl.BlockSpec((LN_TM, 1), lambda i: (i, 0))],
        out_shape=[jax.ShapeDtypeStruct((S, D), F32), jax.ShapeDtypeStruct((S, D), MXU_DTYPE),
                   jax.ShapeDtypeStruct((S, D), F32), jax.ShapeDtypeStruct((S, 1), F32)],
        semantics=("parallel",),
        operands=(res, branch, g, b),
        side=side,
    )


def _ln_bwd(dy_a, dy_b, xhat, rstd, g, side=None):
    def body(da_ref, db_in_ref, xhat_ref, rstd_ref, g_ref, dp_ref, dpb_ref, dg_ref, db_ref):
        @pl.when(pl.program_id(0) == 0)
        def _():
            dg_ref[...] = jnp.zeros_like(dg_ref)
            db_ref[...] = jnp.zeros_like(db_ref)

        dy = da_ref[...] + ALPHA * db_in_ref[...]
        xhat = xhat_ref[...]
        dp = _ln_input_grad(dy, xhat, rstd_ref[...], g_ref[...])
        dp_ref[...] = dp
        dpb_ref[...] = dp.astype(dpb_ref.dtype)
        dg_ref[...] += _colsum(dy * xhat)
        db_ref[...] += _colsum(dy)

    tile = pl.BlockSpec((LN_TM, D), lambda i: (i, 0))
    vec = pl.BlockSpec((1, D), lambda i: (0, 0))
    return _call(
        body,
        name="ln_bwd",
        grid=(S // LN_TM,),
        in_specs=[tile, tile, tile, pl.BlockSpec((LN_TM, 1), lambda i: (i, 0)), vec],
        out_specs=[tile, tile, vec, vec],
        out_shape=[jax.ShapeDtypeStruct((S, D), F32), jax.ShapeDtypeStruct((S, D), MXU_DTYPE),
                   jax.ShapeDtypeStruct((1, D), F32), jax.ShapeDtypeStruct((1, D), F32)],
        semantics=("arbitrary",),
        operands=(dy_a, dy_b, xhat, rstd, g),
        side=side,
    )


def _ln_loss_bwd(res, branch, g, b, target, side=None):
    def body(res_ref, br_ref, g_ref, b_ref, t_ref, loss_ref, dp_ref, dpb_ref, dg_ref, db_ref):
        @pl.when(pl.program_id(0) == 0)
        def _():
            loss_ref[...] = jnp.zeros_like(loss_ref)
            dg_ref[...] = jnp.zeros_like(dg_ref)
            db_ref[...] = jnp.zeros_like(db_ref)

        xhat, rstd = _ln_stats(ALPHA * res_ref[...] + br_ref[...])
        gv = g_ref[...]
        err = xhat * gv + b_ref[...] - t_ref[...]
        loss_ref[...] += (0.5 / D) * jnp.sum(_colsum(err * err), axis=1, keepdims=True)
        dy = err * (1.0 / D)
        dp = _ln_input_grad(dy, xhat, rstd, gv)
        dp_ref[...] = dp
        dpb_ref[...] = dp.astype(dpb_ref.dtype)
        dg_ref[...] += _colsum(dy * xhat)
        db_ref[...] += _colsum(dy)

    tile = pl.BlockSpec((LN_TM, D), lambda i: (i, 0))
    vec = pl.BlockSpec((1, D), lambda i: (0, 0))
    return _call(
        body,
        name="ln_loss_bwd",
        grid=(S // LN_TM,),
        in_specs=[tile, tile, vec, vec, tile],
        out_specs=[pl.BlockSpec((1, 1), lambda i: (0, 0)), tile, tile, vec, vec],
        out_shape=[jax.ShapeDtypeStruct((1, 1), F32), jax.ShapeDtypeStruct((S, D), F32),
                   jax.ShapeDtypeStruct((S, D), MXU_DTYPE),
                   jax.ShapeDtypeStruct((1, D), F32), jax.ShapeDtypeStruct((1, D), F32)],
        semantics=("arbitrary",),
        operands=(res, branch, g, b, target),
        side=side,
    )


FFN_TC = 256


def _ffn_act_fwd(up, gpre, w, b, side=None):
    tc = FFN_TC

    def body(up_ref, x_ref, w_ref, b_ref, o_ref, xpad_ref):
        xpad_ref[pl.ds(0, PAD), :] = jnp.zeros((PAD, tc), F32)
        xpad_ref[pl.ds(PAD, S), :] = x_ref[...]
        wv = w_ref[...]
        bv = b_ref[...]

        def step(ci, carry):
            r0 = pl.multiple_of(ci * CHUNK, CHUNK)
            taps = _past_taps(xpad_ref, r0, FFN_CONV_W)
            gate = bv + taps[0] * wv[0:1, :] + taps[1] * wv[1:2, :] + taps[2] * wv[2:3, :]
            o_ref[pl.ds(r0, CHUNK), :] = (_gelu(gate)[0] * up_ref[pl.ds(r0, CHUNK), :]).astype(o_ref.dtype)
            return carry

        lax.fori_loop(0, S // CHUNK, step, 0)

    col = pl.BlockSpec((S, tc), lambda j: (0, j))
    return _call(
        body,
        name="ffn_act_fwd",
        grid=(D_FF // tc,),
        in_specs=[col, col, pl.BlockSpec((FFN_CONV_W, tc), lambda j: (0, j)), pl.BlockSpec((1, tc), lambda j: (0, j))],
        out_specs=col,
        out_shape=jax.ShapeDtypeStruct((S, D_FF), MXU_DTYPE),
        scratch_shapes=[pltpu.VMEM((S + PAD, tc), F32)],
        semantics=("parallel",),
        operands=(up, gpre, w, b),
        side=side,
    )


def _ffn_act_bwd(dfin, up, gpre, w, b, side=None):
    tc = FFN_TC
    width = FFN_CONV_W

    def body(df_ref, up_ref, x_ref, w_ref, b_ref, dup_ref, dx_ref, dw_ref, db_ref, xpad_ref, dpad_ref):
        xpad_ref[pl.ds(0, PAD), :] = jnp.zeros((PAD, tc), F32)
        xpad_ref[pl.ds(PAD, S), :] = x_ref[...]
        dpad_ref[pl.ds(S, PAD), :] = jnp.zeros((PAD, tc), F32)
        wv = w_ref[...]
        bv = b_ref[...]

        def gate_grad(ci, acc):
            r0 = pl.multiple_of(ci * CHUNK, CHUNK)
            taps = _past_taps(xpad_ref, r0, width)
            gate = bv + taps[0] * wv[0:1, :] + taps[1] * wv[1:2, :] + taps[2] * wv[2:3, :]
            ge, dge = _gelu(gate)
            df = df_ref[pl.ds(r0, CHUNK), :]
            dup_ref[pl.ds(r0, CHUNK), :] = (df * ge).astype(dup_ref.dtype)
            d = df * up_ref[pl.ds(r0, CHUNK), :] * dge
            dpad_ref[pl.ds(r0, CHUNK), :] = d
            return tuple(acc[k] + _colsum(taps[k] * d) for k in range(width)) + (acc[width] + _colsum(d),)

        zero = jnp.zeros((1, tc), F32)
        acc = lax.fori_loop(0, S // CHUNK, gate_grad, (zero,) * (width + 1))
        for k in range(width):
            dw_ref[k:k + 1, :] = acc[k]
        db_ref[...] = acc[width]

        def input_grad(ci, carry):
            r0 = pl.multiple_of(ci * CHUNK, CHUNK)
            ahead = _future_taps(dpad_ref, r0, width)
            dx = ahead[0] * wv[2:3, :] + ahead[1] * wv[1:2, :] + ahead[2] * wv[0:1, :]
            dx_ref[pl.ds(r0, CHUNK), :] = dx.astype(dx_ref.dtype)
            return carry

        lax.fori_loop(0, S // CHUNK, input_grad, 0)

    col = pl.BlockSpec((S, tc), lambda j: (0, j))
    w_spec = pl.BlockSpec((width, tc), lambda j: (0, j))
    vec = pl.BlockSpec((1, tc), lambda j: (0, j))
    return _call(
        body,
        name="ffn_act_bwd",
        grid=(D_FF // tc,),
        in_specs=[col, col, col, w_spec, vec],
        out_specs=[col, col, w_spec, vec],
        out_shape=[jax.ShapeDtypeStruct((S, D_FF), MXU_DTYPE)] * 2
        + [jax.ShapeDtypeStruct((width, D_FF), F32), jax.ShapeDtypeStruct((1, D_FF), F32)],
        scratch_shapes=[pltpu.VMEM((S + PAD, tc), F32), pltpu.VMEM((S + PAD, tc), F32)],
        semantics=("parallel",),
        operands=(dfin, up, gpre, w, b),
        side=side,
    )


def _adamw_update(w, g, m, v):
    m = ADAM_B1 * m + (1.0 - ADAM_B1) * g
    v = ADAM_B2 * v + (1.0 - ADAM_B2) * (g * g)
    m_hat = m / (1.0 - ADAM_B1 ** ADAM_STEP)
    v_hat = v / (1.0 - ADAM_B2 ** ADAM_STEP)
    delta = -ADAM_LR * (m_hat / (jnp.sqrt(v_hat) + ADAM_EPS) + ADAM_WD * w)
    return delta, m, v


def _add_pairs(send, pair, far_index, *, name):
    _, r_dim, c_dim = send.shape
    tr = r_dim // 4

    def body(far_ref, mine_ref, theirs_ref, o_ref):
        o_ref[...] = (mine_ref[...].astype(F32) + theirs_ref[...].astype(F32)).astype(o_ref.dtype)

    return pl.pallas_call(
        body,
        name=name,
        grid_spec=pltpu.PrefetchScalarGridSpec(
            num_scalar_prefetch=1,
            grid=(3, r_dim // tr),
            in_specs=[pl.BlockSpec((None, tr, c_dim), lambda j, i, far: (far[j], i, 0)),
                      pl.BlockSpec((None, tr, c_dim), lambda j, i, far: (1 + j, i, 0))],
            out_specs=pl.BlockSpec((None, tr, c_dim), lambda j, i, far: (j, i, 0)),
        ),
        out_shape=jax.ShapeDtypeStruct((3, r_dim, c_dim), BF16),
        compiler_params=_cparams("parallel", "parallel"),
    )(far_index, send, pair)


def _reduce_adamw(w, m, v, g_own, pair, far, me, *, tr, name, row0=0, earlier=None):
    r_dim, c_dim = w.shape
    rows = pair.shape[1]
    first = row0 // tr

    def body(me_ref, w_ref, m_ref, v_ref, g_ref, pair_ref, far_ref, *refs):
        grad_ref, delta_ref, nm_ref, nv_ref = refs[-4:]
        g = g_ref[...] + pair_ref[...].astype(F32)
        for j in range(3):
            g = g + far_ref[j].astype(F32)
        delta, nm, nv = _adamw_update(w_ref[...], g, m_ref[...], v_ref[...])
        grad_ref[...] = g
        delta_ref[...] = delta
        nm_ref[...] = nm
        nv_ref[...] = nv

    tile = pl.BlockSpec((tr, c_dim), lambda i, me: (first + i, 0))
    if g_own.ndim == 3:
        own_spec = pl.BlockSpec((None, tr, c_dim), lambda i, me: (me[0], i, 0))
    else:
        own_spec = pl.BlockSpec((tr, c_dim), lambda i, me: (i, 0))
    earlier = list(earlier or ())
    return pl.pallas_call(
        body,
        name=name,
        grid_spec=pltpu.PrefetchScalarGridSpec(
            num_scalar_prefetch=1,
            grid=(rows // tr,),
            in_specs=[tile, tile, tile, own_spec, pl.BlockSpec((None, tr, c_dim), lambda i, me: (0, i, 0)),
                      pl.BlockSpec((3, tr, c_dim), lambda i, me: (0, i, 0))]
            + [pl.BlockSpec(memory_space=pl.ANY)] * len(earlier),
            out_specs=[tile] * 4,
        ),
        out_shape=[jax.ShapeDtypeStruct((r_dim, c_dim), F32)] * 4,
        input_output_aliases={7 + k: k for k in range(len(earlier))},
        compiler_params=_cparams("parallel"),
    )(me, w, m, v, g_own, pair, far, *earlier)


def _adamw_many(ws, ms, vs, gs):
    n = len(ws)

    def body(*refs):
        for i in range(n):
            delta, nm, nv = _adamw_update(refs[i][...], refs[3 * n + i][...], refs[n + i][...], refs[2 * n + i][...])
            refs[4 * n + i][...] = delta
            refs[5 * n + i][...] = nm
            refs[6 * n + i][...] = nv

    vmem = pl.BlockSpec(memory_space=pltpu.VMEM)
    res = pl.pallas_call(
        body,
        name="adamw_small",
        in_specs=[vmem] * (4 * n),
        out_specs=[vmem] * (3 * n),
        out_shape=[jax.ShapeDtypeStruct(w.shape, F32) for w in ws] * 3,
        compiler_params=pltpu.CompilerParams(vmem_limit_bytes=VMEM_LIMIT),
    )(*ws, *ms, *vs, *gs)
    return res[:n], res[n:2 * n], res[2 * n:]


def _adamw_blocks(w, m, v, g, *, name, side=None):
    per = 2

    def body(w_ref, m_ref, v_ref, g_ref, delta_ref, nm_ref, nv_ref):
        delta, nm, nv = _adamw_update(w_ref[...], g_ref[...], m_ref[...], v_ref[...])
        delta_ref[...] = delta
        nm_ref[...] = nm
        nv_ref[...] = nv

    tile = pl.BlockSpec((1, per) + w.shape[2:], lambda i: (0, i, 0, 0))
    return _call(
        body,
        name=name,
        grid=(w.shape[1] // per,),
        in_specs=[tile] * 4,
        out_specs=[tile] * 3,
        out_shape=[jax.ShapeDtypeStruct(w.shape, F32)] * 3,
        semantics=("parallel",),
        operands=(w, m, v, g),
        side=side,
    )


def _coords():
    return lax.axis_index("x"), lax.axis_index("y"), lax.axis_index("c")


def _flip(coord, bit):
    return 1 - coord if bit else coord


def _relative(k):
    x, y, c = _coords()
    return _flip(x, k & 4), _flip(y, k & 2), _flip(c, k & 1)


def _index(pos):
    return 4 * pos[0] + 2 * pos[1] + pos[2]


FAR = (4, 2, 6)
AG_US_PER_MB = 38.0
RS_US_PER_MB = 46.0
MIN_RIDE_US = 30.0
MIN_GATHER_RIDE_US = 22.0
ROW_ALIGN = 32


def _chunks(items, cursor, us, us_per_mb, through=None):
    budget = float("inf") if us is None else us / us_per_mb * 2 ** 20
    names = list(items)
    if through is not None:
        names = names[:names.index(through) + 1]
    chunks = []
    for name in names:
        arr = items[name]
        r_dim, c_dim = arr.shape[-2:]
        row_bytes = c_dim * arr.dtype.itemsize
        while cursor[name] < r_dim and budget > 0:
            rows = r_dim - cursor[name]
            if r_dim > ROW_ALIGN and budget < rows * row_bytes:
                rows = min(rows, max(ROW_ALIGN, int(budget // row_bytes) // ROW_ALIGN * ROW_ALIGN))
            chunks.append((name, cursor[name], rows))
            cursor[name] += rows
            budget -= rows * row_bytes
    return chunks


class _Gather:
    def __init__(self, shards):
        self.shards, self.bufs, self.cursor = {}, {}, {}
        self.add_shards(shards)

    def add_shards(self, shards):
        for n, shard in shards.items():
            self.shards[n], self.bufs[n], self.cursor[n] = shard, None, 0

    def take(self, us=None, through=None):
        if us is not None and us < MIN_GATHER_RIDE_US:
            return None
        chunks = _chunks(self.shards, self.cursor, us, AG_US_PER_MB, through)
        return _GatherSide(self, chunks) if chunks else None

    def get(self, name):
        chunks = _chunks(self.shards, self.cursor, None, AG_US_PER_MB, through=name)
        if chunks:
            _run_side(_GatherSide(self, chunks), "gather_" + name)
        return self.bufs[name]


class _GatherSide:
    SEMS = 8

    def __init__(self, owner, chunks):
        self.owner, self.chunks = owner, chunks
        self.names = list(dict.fromkeys(n for n, _, _ in chunks))
        old = [n for n in self.names if owner.bufs[n] is not None]
        self.operands = [owner.shards[n] for n in self.names] + [owner.bufs[n] for n in old]
        self.out_shape = [jax.ShapeDtypeStruct((N_DEV,) + owner.shards[n].shape, owner.shards[n].dtype)
                          for n in self.names]
        self.aliases = {len(self.names) + i: self.names.index(n) for i, n in enumerate(old)}
        self.sems = [pltpu.SemaphoreType.DMA((self.SEMS * len(chunks),)),
                     pltpu.SemaphoreType.DMA((self.SEMS * len(chunks),)), pltpu.SemaphoreType.DMA((len(chunks),))]

    def _halves(self, ci):
        _, r0, rows = self.chunks[ci]
        if rows % ROW_ALIGN:
            return None
        return (r0, rows // 2), (r0 + rows // 2, rows // 2)

    def _copy(self, ins, outs, sems, ci, s, block, to, rows=None, from_shard=False):
        name, r0, n = self.chunks[ci]
        if rows is not None:
            r0, n = rows
        w = self.names.index(name)
        slot = outs[w].at[_index(block), pl.ds(r0, n)]
        return pltpu.make_async_remote_copy(
            src_ref=ins[w].at[pl.ds(r0, n)] if from_shard else slot, dst_ref=slot,
            send_sem=sems[0].at[self.SEMS * ci + s], recv_sem=sems[1].at[self.SEMS * ci + s],
            device_id=to, device_id_type=MESH)

    def _own(self, ins, outs, sems, ci):
        name, r0, rows = self.chunks[ci]
        w = self.names.index(name)
        return pltpu.make_async_copy(ins[w].at[pl.ds(r0, rows)], outs[w].at[_index(_relative(0)), pl.ds(r0, rows)],
                                     sems[2].at[ci])

    def _pass(self, ins, outs, sems, ci, which):
        source, target = ((4, 2), (2, 4))[which]
        return self._copy(ins, outs, sems, ci, 3 + which, _relative(source), _relative(target),
                          rows=self._halves(ci)[which])

    def start(self, ins, outs, sems):
        me = _relative(0)
        for ci in range(len(self.chunks)):
            self._own(ins, outs, sems, ci).start()
        for ci in range(len(self.chunks)):
            self._copy(ins, outs, sems, ci, 1, me, _relative(4), from_shard=True).start()
            self._copy(ins, outs, sems, ci, 2, me, _relative(2), from_shard=True).start()
            if self._halves(ci) is None:
                self._copy(ins, outs, sems, ci, 3, me, _relative(6), from_shard=True).start()
        for ci in range(len(self.chunks)):
            self._copy(ins, outs, sems, ci, 0, me, _relative(1), from_shard=True).start()

    def mid(self, ins, outs, sems):
        me = _relative(0)
        cut = [ci for ci in range(len(self.chunks)) if self._halves(ci) is not None]
        for ci in cut:
            self._copy(ins, outs, sems, ci, 1, _relative(4), me).wait_recv()
            self._pass(ins, outs, sems, ci, 0).start()
            self._copy(ins, outs, sems, ci, 5, _relative(4), _relative(1)).start()
        for ci in cut:
            self._copy(ins, outs, sems, ci, 2, _relative(2), me).wait_recv()
            self._pass(ins, outs, sems, ci, 1).start()
            self._copy(ins, outs, sems, ci, 6, _relative(2), _relative(1)).start()

    def finish(self, ins, outs, sems):
        me, sibling = _relative(0), _relative(1)
        n = len(self.chunks)
        for ci in range(n):
            if self._halves(ci) is None:
                for s, k in ((1, 4), (2, 2), (3, 6)):
                    self._copy(ins, outs, sems, ci, s, _relative(k), me).wait_recv()
                for j, k in enumerate(FAR):
                    self._copy(ins, outs, sems, ci, 5 + j, _relative(k), sibling).start()
            else:
                h0, h1 = self._halves(ci)
                self._copy(ins, outs, sems, ci, 3, _relative(6), me, rows=h0).wait_recv()
                self._copy(ins, outs, sems, ci, 4, _relative(6), me, rows=h1).wait_recv()
                self._copy(ins, outs, sems, ci, 7, _relative(6), sibling).start()
        for ci in range(n):
            self._copy(ins, outs, sems, ci, 0, sibling, me).wait_recv()
            for j, k in enumerate(FAR):
                self._copy(ins, outs, sems, ci, 5 + j, _relative(k | 1), me).wait_recv()
        for ci in range(n):
            self._copy(ins, outs, sems, ci, 0, me, sibling, from_shard=True).wait_send()
            self._copy(ins, outs, sems, ci, 1, me, _relative(4), from_shard=True).wait_send()
            self._copy(ins, outs, sems, ci, 2, me, _relative(2), from_shard=True).wait_send()
            if self._halves(ci) is None:
                self._copy(ins, outs, sems, ci, 3, me, _relative(6), from_shard=True).wait_send()
            else:
                self._pass(ins, outs, sems, ci, 0).wait_send()
                self._pass(ins, outs, sems, ci, 1).wait_send()
            for j, k in enumerate(FAR):
                self._copy(ins, outs, sems, ci, 5 + j, _relative(k), sibling).wait_send()
            self._own(ins, outs, sems, ci).wait()

    def done(self, results):
        for n, buf in zip(self.names, results):
            self.owner.bufs[n] = buf


class _Scatter:
    def __init__(self, me, far_index):
        self.me, self.far_index = me, far_index
        self.sends, self.owns, self.pairs, self.sums, self.fars = {}, {}, {}, {}, {}
        self.pair_cursor, self.far_cursor = {}, {}

    def add(self, name, send, own):
        self.sends[name] = send
        self.owns[name] = own
        self.pairs[name] = self.fars[name] = None
        self.pair_cursor[name] = 0

    def _rows(self, name):
        return self.sends[name].shape[1]

    def _add_ready_pairs(self):
        for name in self.sends:
            if name not in self.sums and self.pair_cursor[name] == self._rows(name):
                self.sums[name] = _add_pairs(self.sends[name], self.pairs[name], self.far_index, name="pair_" + name)
                self.far_cursor[name] = 0

    def _side(self, us, through=None):
        self._add_ready_pairs()
        names = list(self.sends)
        if through is not None:
            names = names[:names.index(through) + 1]
        pair_chunks = [(n, self.pair_cursor[n], self._rows(n) - self.pair_cursor[n]) for n in names
                       if self.pair_cursor[n] < self._rows(n)]
        for n, _, _ in pair_chunks:
            self.pair_cursor[n] = self._rows(n)
        far_chunks = _chunks(self.sums, self.far_cursor, us, RS_US_PER_MB,
                             through if through in self.sums else None) if self.sums else []
        return _ScatterSide(self, pair_chunks, far_chunks) if pair_chunks or far_chunks else None

    def add_blocks(self, name, blocks32, blocks16):
        self.add(name, blocks16, blocks32)

    def add_cols(self, name, full32, full16):
        width = full32.shape[1] // N_DEV
        self.add(name, _blocks(full16, "cols"), lax.dynamic_slice_in_dim(full32, self.me * width, width, axis=1))

    def take(self, us):
        return self._side(us) if us >= MIN_RIDE_US else None

    def flush_pairs(self, name):
        side = self._side(0.0)
        if side is not None:
            _run_side(side, name)
        self._add_ready_pairs()

    def get(self, name):
        step = 0
        while name not in self.sums or self.far_cursor[name] < self._rows(name):
            _run_side(self._side(None, through=name), "scatter_%s_%d" % (name, step))
            step += 1
        return self.owns[name], self.pairs[name], self.fars[name]


class _ScatterSide:
    TO_SIBLING = (1, 5, 3, 7)

    def __init__(self, owner, pair_chunks, far_chunks):
        self.owner, self.pair_chunks, self.far_chunks = owner, pair_chunks, far_chunks
        self.pair_names = list(dict.fromkeys(n for n, _, _ in pair_chunks))
        self.far_names = list(dict.fromkeys(n for n, _, _ in far_chunks))
        ins = [(owner.sends[n], owner.pairs[n], (4,)) for n in self.pair_names]
        ins += [(owner.sums[n], owner.fars[n], (3,)) for n in self.far_names]
        old = [i for i, (_, buf, _) in enumerate(ins) if buf is not None]
        self.operands = [src for src, _, _ in ins] + [ins[i][1] for i in old]
        self.out_shape = [jax.ShapeDtypeStruct(slots + src.shape[1:], BF16) for src, _, slots in ins]
        self.aliases = {len(ins) + j: i for j, i in enumerate(old)}
        n_pair, n_far = 4 * len(pair_chunks), 3 * len(far_chunks)
        self.sems = [pltpu.SemaphoreType.DMA((max(n_pair, 1),)), pltpu.SemaphoreType.DMA((max(n_pair, 1),)),
                     pltpu.SemaphoreType.DMA((max(n_far, 1),)), pltpu.SemaphoreType.DMA((max(n_far, 1),))]

    def _copies(self, ins, outs, sems):
        copies = []
        for ci, (name, r0, rows) in enumerate(self.pair_chunks):
            w = self.pair_names.index(name)
            for j, k in enumerate(self.TO_SIBLING):
                copies.append(pltpu.make_async_remote_copy(
                    src_ref=ins[w].at[_index(_relative(k)), pl.ds(r0, rows)], dst_ref=outs[w].at[j, pl.ds(r0, rows)],
                    send_sem=sems[0].at[4 * ci + j], recv_sem=sems[1].at[4 * ci + j],
                    device_id=_relative(1), device_id_type=MESH))
        for ci, (name, r0, rows) in enumerate(self.far_chunks):
            w = len(self.pair_names) + self.far_names.index(name)
            for j, k in enumerate(FAR):
                copies.append(pltpu.make_async_remote_copy(
                    src_ref=ins[w].at[j, pl.ds(r0, rows)], dst_ref=outs[w].at[j, pl.ds(r0, rows)],
                    send_sem=sems[2].at[3 * ci + j], recv_sem=sems[3].at[3 * ci + j],
                    device_id=_relative(k), device_id_type=MESH))
        return copies

    def start(self, ins, outs, sems):
        for cp in self._copies(ins, outs, sems):
            cp.start()

    def mid(self, ins, outs, sems):
        pass

    def finish(self, ins, outs, sems):
        for cp in self._copies(ins, outs, sems):
            cp.wait()

    def done(self, results):
        for n, buf in zip(self.pair_names, results):
            self.owner.pairs[n] = buf
        for n, buf in zip(self.far_names, results[len(self.pair_names):]):
            self.owner.fars[n] = buf


class _Joined:
    def __init__(self, sides):
        self.sides = sides
        self.operands, self.out_shape, self.sems, self.aliases, self.spans = [], [], [], {}, []
        for s in sides:
            i0, o0, s0 = len(self.operands), len(self.out_shape), len(self.sems)
            self.operands += list(s.operands)
            self.out_shape += list(s.out_shape)
            self.sems += list(s.sems)
            self.aliases.update({i0 + i: o0 + o for i, o in s.aliases.items()})
            self.spans.append((slice(i0, len(self.operands)), slice(o0, len(self.out_shape)),
                               slice(s0, len(self.sems))))

    def start(self, ins, outs, sems):
        for s, (i, o, m) in zip(self.sides, self.spans):
            s.start(ins[i], outs[o], sems[m])

    def mid(self, ins, outs, sems):
        for s, (i, o, m) in zip(self.sides, self.spans):
            s.mid(ins[i], outs[o], sems[m])

    def finish(self, ins, outs, sems):
        for s, (i, o, m) in zip(self.sides, self.spans):
            s.finish(ins[i], outs[o], sems[m])

    def done(self, results):
        for s, (_, o, _) in zip(self.sides, self.spans):
            s.done(results[o])


def _join(*sides):
    sides = [s for s in sides if s is not None]
    if len(sides) <= 1:
        return sides[0] if sides else None
    return _Joined(sides)


PART_W = 768


def _pack_rows(vecs):
    rows = -(-sum(v.shape[0] for v in vecs) // 8) * 8

    def body(*refs):
        out = refs[-1]
        out[...] = jnp.zeros_like(out)
        r0 = 0
        for v in refs[:-1]:
            k, n = v.shape
            for p in range(-(-n // PART_W)):
                w = min(PART_W, n - PART_W * p)
                out[p, r0:r0 + k, 0:w] = v[:, PART_W * p:PART_W * p + w]
            r0 += k

    vmem = pl.BlockSpec(memory_space=pltpu.VMEM)
    return pl.pallas_call(body, name="pack_small", in_specs=[vmem] * len(vecs), out_specs=vmem,
                          out_shape=jax.ShapeDtypeStruct((N_DEV, rows, PART_W), F32))(*vecs)


def _unpack_rows(packed, shapes):
    def body(packed_ref, *outs):
        r0 = 0
        for o in outs:
            k, n = o.shape
            for p in range(-(-n // PART_W)):
                w = min(PART_W, n - PART_W * p)
                o[:, PART_W * p:PART_W * p + w] = packed_ref[p, r0:r0 + k, 0:w]
            r0 += k

    vmem = pl.BlockSpec(memory_space=pltpu.VMEM)
    return pl.pallas_call(body, name="unpack_small", in_specs=[vmem], out_specs=[vmem] * len(shapes),
                          out_shape=[jax.ShapeDtypeStruct(s, F32) for s in shapes])(packed)


class _PartsToOwners:
    def __init__(self, arrays):
        self.n = len(arrays)
        self.pers = [a.shape[0] // N_DEV for a in arrays]
        self.operands, self.aliases = list(arrays), {}
        self.out_shape = [jax.ShapeDtypeStruct((N_DEV, per) + a.shape[1:], a.dtype) for a, per in zip(arrays, self.pers)]
        self.sems = [pltpu.SemaphoreType.DMA((self.n * (N_DEV - 1),))] * 2

    def _copies(self, ins, outs, sems):
        return [pltpu.make_async_remote_copy(
            src_ref=ins[j].at[pl.ds(self.pers[j] * _index(_relative(k)), self.pers[j])], dst_ref=outs[j].at[k],
            send_sem=sems[0].at[self.n * (k - 1) + j], recv_sem=sems[1].at[self.n * (k - 1) + j],
            device_id=_relative(k), device_id_type=MESH) for k in range(1, N_DEV) for j in range(self.n)]

    def start(self, ins, outs, sems):
        for cp in self._copies(ins, outs, sems):
            cp.start()

    def mid(self, ins, outs, sems):
        pass

    def finish(self, ins, outs, sems):
        for cp in self._copies(ins, outs, sems):
            cp.wait()

    def done(self, results):
        self.stages = list(results)


def _sum_parts(arrays, stages):
    n = len(arrays)
    pers = [a.shape[0] // N_DEV for a in arrays]

    def body(*refs):
        me = _index(_relative(0))
        for j in range(n):
            acc = refs[j][pl.ds(pers[j] * me, pers[j])]
            for k in range(1, N_DEV):
                acc = acc + refs[n + j][k].astype(F32)
            refs[2 * n + j][...] = acc

    vmem = pl.BlockSpec(memory_space=pltpu.VMEM)
    return pl.pallas_call(body, name="sum_small_parts", in_specs=[vmem] * (2 * n), out_specs=[vmem] * n,
                          out_shape=[jax.ShapeDtypeStruct((per,) + a.shape[1:], F32) for a, per in zip(arrays, pers)],
                          compiler_params=pltpu.CompilerParams(vmem_limit_bytes=VMEM_LIMIT))(*arrays, *stages)


class _PartsToAll:
    def __init__(self, parts):
        self.n = len(parts)
        self.pers = [p.shape[0] for p in parts]
        self.operands, self.aliases = list(parts), {}
        self.out_shape = [jax.ShapeDtypeStruct((N_DEV * p.shape[0],) + p.shape[1:], F32) for p in parts]
        self.sems = [pltpu.SemaphoreType.DMA((self.n * (N_DEV - 1),))] * 2 + [pltpu.SemaphoreType.DMA((self.n,))]

    def _rows(self, outs, j, pos):
        return outs[j].at[pl.ds(self.pers[j] * _index(pos), self.pers[j])]

    def _copy(self, ins, outs, sems, k, j, owner):
        return pltpu.make_async_remote_copy(
            src_ref=ins[j], dst_ref=self._rows(outs, j, owner),
            send_sem=sems[0].at[self.n * (k - 1) + j], recv_sem=sems[1].at[self.n * (k - 1) + j],
            device_id=_relative(k), device_id_type=MESH)

    def _own(self, ins, outs, sems, j):
        return pltpu.make_async_copy(ins[j], self._rows(outs, j, _relative(0)), sems[2].at[j])

    def start(self, ins, outs, sems):
        for j in range(self.n):
            self._own(ins, outs, sems, j).start()
            for k in range(1, N_DEV):
                self._copy(ins, outs, sems, k, j, _relative(0)).start()

    def mid(self, ins, outs, sems):
        pass

    def finish(self, ins, outs, sems):
        for j in range(self.n):
            for k in range(1, N_DEV):
                self._copy(ins, outs, sems, k, j, _relative(k)).wait_recv()
                self._copy(ins, outs, sems, k, j, _relative(0)).wait_send()
            self._own(ins, outs, sems, j).wait()

    def done(self, results):
        self.totals = list(results)


class _SmallSync:
    def __init__(self, vec_names, mat_names):
        self.vec_names, self.mat_names = vec_names, mat_names

    def begin(self, loss, grads):
        vecs = [loss] + [grads[n] for n in self.vec_names]
        self.shapes = [v.shape for v in vecs]
        self.own = [_diag_blocks(grads[n]) for n in self.mat_names] + [_pack_rows(vecs)]
        self.to_owners = _PartsToOwners([a.astype(BF16) for a in self.own[:-1]] + self.own[-1:])
        return self.to_owners

    def middle(self):
        self.to_all = _PartsToAll(_sum_parts(self.own, self.to_owners.stages))
        return self.to_all

    def end(self):
        *mats, packed = self.to_all.totals
        sums = _unpack_rows(packed, self.shapes)
        return sums[0], dict(zip(self.vec_names, sums[1:])), dict(zip(self.mat_names, mats))


def _block_diag(w):
    groups = []
    for g in range(N_RNN_GROUPS):
        placed = [jnp.pad(w[4 * g + b], ((RNN_BLOCK_W * b, RNN_BLOCK_W * (3 - b)),) * 2) for b in range(4)]
        groups.append(placed[0] + placed[1] + placed[2] + placed[3])
    return jnp.stack(groups)


def _diag_blocks(wg):
    blocks = []
    for n in range(4 * N_RNN_GROUPS):
        g, at = n // 4, RNN_BLOCK_W * (n % 4)
        blocks.append(wg[g, at:at + RNN_BLOCK_W, at:at + RNN_BLOCK_W])
    return jnp.stack(blocks)


def _heads_major(t, n_heads):
    return t.reshape(S, n_heads, HEAD_DIM).transpose(1, 0, 2)


def _heads_minor(t):
    return t.transpose(1, 0, 2).reshape(S, t.shape[0] * HEAD_DIM)


def _natural(gathered, how):
    n, r, c = gathered.shape
    if how == "rows":
        return gathered.reshape(n * r, c)
    return gathered.transpose(1, 0, 2).reshape(r, n * c)


def _blocks(full, how):
    if how == "rows":
        return full.reshape(N_DEV, full.shape[0] // N_DEV, full.shape[1])
    return full.reshape(full.shape[0], N_DEV, full.shape[1] // N_DEV).transpose(1, 0, 2)


def _cast_many(arrays, side=None):
    steps = 4

    def body(*refs):
        n = len(refs) // 2
        for src, dst in zip(refs[:n], refs[n:]):
            dst[...] = src[...].astype(dst.dtype)

    specs = [pl.BlockSpec((a.shape[0] // steps, a.shape[1]), lambda i: (i, 0)) for a in arrays]
    return _call(
        body,
        name="cast_weights",
        grid=(steps,),
        in_specs=specs,
        out_specs=specs,
        out_shape=[jax.ShapeDtypeStruct(a.shape, MXU_DTYPE) for a in arrays],
        semantics=("parallel",),
        operands=tuple(arrays),
        side=side,
    )


def _forward_backward(x2, xb, target, small, gather, scatter, sync):
    w_in = _natural(gather.get("w_in"), "cols")
    proj, projb = _mm(xb, w_in, tm=1024, tn=512, tk=D, out_dtype=(F32, MXU_DTYPE), name="proj", side=gather.take(110))

    qt = projb[:, :OFF_K].T.reshape(N_KV, GROUP, HEAD_DIM, S)
    k2, v2 = projb[:, OFF_K:OFF_V], projb[:, OFF_V:OFF_RX]
    kp = jnp.pad(_heads_major(k2, N_KV), ((0, 0), (BLOCK, 0), (0, 0)))
    vp = jnp.pad(_heads_major(v2, N_KV), ((0, 0), (BLOCK, 0), (0, 0)))
    kt = jnp.pad(k2.T.reshape(N_KV, HEAD_DIM, S), ((0, 0), (0, 0), (BLOCK, 0)))
    vt = jnp.pad(v2.T.reshape(N_KV, HEAD_DIM, S), ((0, 0), (0, 0), (BLOCK, 0)))
    sink_row = jnp.repeat(small["attn_sinks"].reshape(N_KV, 1, GROUP), BLOCK, axis=2)
    ot = _attn_fwd(qt, kp, vt, sink_row, side=gather.take(36)).reshape(D, S)

    rconv_w = _natural(gather.get("rnn_conv_w"), "cols")
    rxc = _conv_fwd(proj, OFF_RX, rconv_w, small["rnn_conv_b"], tc=512, name="rnn_conv_fwd", side=gather.take(18))
    r, i = _lru_gates(rxc, small["lru_wa"], small["lru_wi"], small["lru_ba"], small["lru_bi"], side=gather.take(33))
    h, yrin = _lru_scan_fwd(r, i, rxc, proj, small["lru_lambda"], side=gather.take(53))

    w_ap = _natural(gather.get("w_attn_proj"), "rows")
    w_rp = _natural(gather.get("w_rnn_proj"), "rows")
    y_attn = _mm(ot, w_ap, ta=True, tm=1024, tn=1024, tk=D, name="attn_proj", side=gather.take(22))
    y_rnn = _mm(yrin, w_rp, tm=1024, tn=1024, tk=D_RNN, name="rnn_proj", side=gather.take(27))
    mixin = _gate_fwd(y_attn, y_rnn, proj, small["b_gate"], side=gather.take(25))
    w_out = _natural(gather.get("w_out"), "rows")
    mix = _mm(mixin, w_out, tm=1024, tn=1024, tk=D, name="mix_out", side=gather.take(22))
    x1, x1b, xhat1, rstd1 = _ln_fwd(x2, mix, small["ln1_g"], small["ln1_b"], side=gather.take(23))

    w_up = gather.get("ffn_w_up")
    up = _mm(x1b, w_up, tm=1024, tn=768, tk=D, b_block=768, name="ffn_up", side=gather.take(58))
    w_gate = gather.get("ffn_w_gate")
    gpre = _mm(x1b, w_gate, tm=1024, tn=768, tk=D, b_block=768, name="ffn_gate", side=gather.take(58))
    fconv_w = _natural(gather.get("ffn_conv_w"), "cols")
    fin = _ffn_act_fwd(up, gpre, fconv_w, small["ffn_conv_b"], side=gather.take())
    w_down = _natural(gather.get("ffn_w_down"), "rows")
    f = _mm(fin, w_down, tm=1024, tn=1024, tk=2048, name="ffn_down")
    loss, dpre2, dpre2b, d_ln2_g, d_ln2_b = _ln_loss_bwd(x1, f, small["ln2_g"], small["ln2_b"], target)

    grads = {"ln2_g": d_ln2_g, "ln2_b": d_ln2_b}
    both = (F32, BF16)
    g32, g16 = _mm(fin, dpre2b, ta=True, tm=1024, tn=1024, tk=S, out_dtype=both, name="d_ffn_w_down")
    scatter.add_blocks("ffn_w_down", _blocks(g32, "rows"), _blocks(g16, "rows"))
    dfin = _mm(dpre2b, w_down, tb=True, tm=1024, tn=1024, tk=D, name="d_fin", side=scatter.take(57))
    dup, dgpre, grads["ffn_conv_w"], grads["ffn_conv_b"] = _ffn_act_bwd(
        dfin, up, gpre, fconv_w, small["ffn_conv_b"], side=scatter.take(85))
    g32, g16 = _mm(x1b, dup, ta=True, tm=1024, tn=768, tk=S, out_dtype=both, out_block=768, name="d_ffn_w_up",
                   side=scatter.take(57))
    scatter.add_blocks("ffn_w_up", g32, g16)
    g32, g16 = _mm(x1b, dgpre, ta=True, tm=1024, tn=768, tk=S, out_dtype=both, out_block=768, name="d_ffn_w_gate",
                   side=scatter.take(56))
    scatter.add_blocks("ffn_w_gate", g32, g16)
    dx1 = _mm(dup, w_up, tb=True, tm=1024, tn=1024, tk=768, b_block=768, name="d_x1_up", side=scatter.take(68))
    dx1 = _mm(dgpre, w_gate, tb=True, tm=1024, tn=1024, tk=768, b_block=768, add=dx1, name="d_x1_gate",
              side=scatter.take(70))
    dpre1, dpre1b, grads["ln1_g"], grads["ln1_b"] = _ln_bwd(dx1, dpre2, xhat1, rstd1, small["ln1_g"],
                                                            side=scatter.take(24))

    g32, g16 = _mm(mixin, dpre1b, ta=True, tm=1024, tn=1024, tk=S, out_dtype=both, name="d_w_out",
                   side=scatter.take(26))
    scatter.add_blocks("w_out", _blocks(g32, "rows"), _blocks(g16, "rows"))
    dmix = _mm(dpre1b, w_out, tb=True, tm=1024, tn=1024, tk=D, name="d_mixin", side=scatter.take(22))
    dya, dyr, dgl_a, dgl_r, db_a, db_r = _gate_bwd(dmix, y_attn, y_rnn, proj, small["b_gate"], side=scatter.take(36))
    grads["b_gate"] = jnp.concatenate([db_a, db_r], axis=1)
    g32, g16 = _mm(ot, dya, tm=1024, tn=1024, tk=S, out_dtype=both, name="d_w_attn_proj", side=scatter.take(38))
    scatter.add_blocks("w_attn_proj", _blocks(g32, "rows"), _blocks(g16, "rows"))
    g32, g16 = _mm(yrin, dyr, ta=True, tm=1280, tn=1024, tk=S, out_dtype=both, name="d_w_rnn_proj",
                   side=scatter.take(27))
    scatter.add_blocks("w_rnn_proj", _blocks(g32, "rows"), _blocks(g16, "rows"))
    dot_ = _mm(w_ap, dya, tb=True, tm=1024, tn=1024, tk=D, out_dtype=MXU_DTYPE, name="d_o", side=scatter.take(22))
    dyrin = _mm(dyr, w_rp, tb=True, tm=1024, tn=1280, tk=D, name="d_yrin", side=scatter.take(27))

    dry, dzr, dzi, drxc_in, grads["lru_ba"], grads["lru_bi"], grads["lru_lambda"] = _lru_scan_bwd(
        dyrin, proj, h, r, i, rxc, small["lru_lambda"], side=scatter.take(94))
    grads["lru_wa"], grads["lru_wi"] = _lru_gate_wgrad(rxc, dzr, dzi, side=scatter.take(22))
    drxc = _lru_gate_xgrad(dzr, dzi, small["lru_wa"], small["lru_wi"], drxc_in, side=scatter.take(33))
    drx, grads["rnn_conv_w"], grads["rnn_conv_b"] = _conv_bwd(drxc, proj, OFF_RX, rconv_w, tc=512,
                                                             name="rnn_conv_bwd", side=scatter.take(29))

    dqt, dk, dv, dsink = _attn_bwd(qt, kp, kt, vp, sink_row, dot_.reshape(N_KV, GROUP, HEAD_DIM, S),
                                   side=scatter.take(65))
    grads["attn_sinks"] = dsink.reshape(1, N_KV * GROUP)
    dproj = jnp.concatenate([
        dqt.reshape(D, S).T,
        _heads_minor(dk[:, BLOCK:, :]).astype(MXU_DTYPE),
        _heads_minor(dv[:, BLOCK:, :]).astype(MXU_DTYPE),
        drx, dry, dgl_a, dgl_r], axis=1)
    for part in range(W_IN_PARTS):
        rows = slice(part * (D // W_IN_PARTS), (part + 1) * (D // W_IN_PARTS))
        side = _join(scatter.take(55), sync.begin(loss, grads)) if part == 0 else scatter.take(68)
        g32, g16 = _mm(xb[:, rows], dproj, ta=True, tm=D // W_IN_PARTS, tn=512, tk=S, out_dtype=both,
                       name="d_w_in_%d" % part, side=side)
        scatter.add_cols("w_in_%d" % part, g32, g16)
        scatter.flush_pairs("pairs_w_in_%d" % part)
    dx = _mm(dproj, w_in, tb=True, tm=1024, tn=1024, tk=512, add=dpre1, add_scale=ALPHA, name="d_x",
             side=_join(scatter.take(400), sync.middle()))
    return dx


SHARDED = (
    ("w_in", "cols", 128), ("w_attn_proj", "rows", 32), ("w_rnn_proj", "rows", 32), ("w_out", "rows", 32),
    ("ffn_w_up", "cols", 128), ("ffn_w_gate", "cols", 128), ("ffn_w_down", "rows", 64),
)
SMALL_REPLICATED = ("b_gate", "rnn_conv_b", "lru_wa", "lru_ba", "lru_wi", "lru_bi", "lru_lambda", "attn_sinks",
                    "ln1_g", "ln1_b", "ffn_conv_b", "ln2_g", "ln2_b")
SMALL_SHARDED = ("rnn_conv_w", "ffn_conv_w")
SMALL_MATS = ("lru_wa", "lru_wi")
W_IN_PARTS = 2
WEIGHTS = ("w_in", "b_gate", "rnn_conv_w", "rnn_conv_b", "lru_wa", "lru_ba", "lru_wi", "lru_bi", "lru_lambda",
           "attn_sinks", "w_attn_proj", "w_rnn_proj", "w_out", "ln1_g", "ln1_b", "ffn_w_up", "ffn_w_gate",
           "ffn_conv_w", "ffn_conv_b", "ffn_w_down", "ln2_g", "ln2_b")


def kernel(x, w_in, b_gate, rnn_conv_w, rnn_conv_b, lru_wa, lru_ba, lru_wi, lru_bi, lru_lambda, attn_sinks, w_attn_proj, w_rnn_proj, w_out, ln1_g, ln1_b, ffn_w_up, ffn_w_gate, ffn_conv_w, ffn_conv_b, ffn_w_down, ln2_g, ln2_b, loss_target, m_w_in, m_b_gate, m_rnn_conv_w, m_rnn_conv_b, m_lru_wa, m_lru_ba, m_lru_wi, m_lru_bi, m_lru_lambda, m_attn_sinks, m_w_attn_proj, m_w_rnn_proj, m_w_out, m_ln1_g, m_ln1_b, m_ffn_w_up, m_ffn_w_gate, m_ffn_conv_w, m_ffn_conv_b, m_ffn_w_down, m_ln2_g, m_ln2_b, v_w_in, v_b_gate, v_rnn_conv_w, v_rnn_conv_b, v_lru_wa, v_lru_ba, v_lru_wi, v_lru_bi, v_lru_lambda, v_attn_sinks, v_w_attn_proj, v_w_rnn_proj, v_w_out, v_ln1_g, v_ln1_b, v_ffn_w_up, v_ffn_w_gate, v_ffn_conv_w, v_ffn_conv_b, v_ffn_w_down, v_ln2_g, v_ln2_b):
    given = dict(locals())
    wsh = {n: given[n][0] for n in WEIGHTS}
    msh = {n: given["m_" + n][0] for n in WEIGHTS}
    vsh = {n: given["v_" + n][0] for n in WEIGHTS}
    m_given = {n: given["m_" + n] for n in WEIGHTS}
    v_given = {n: given["v_" + n] for n in WEIGHTS}
    me = 4 * lax.axis_index("x") + 2 * lax.axis_index("y") + lax.axis_index("c")

    order = ("w_in", "rnn_conv_w", "ffn_conv_w", "w_attn_proj", "w_rnn_proj", "w_out", "ffn_w_up", "ffn_w_gate",
             "ffn_w_down")
    gather = _Gather({n: wsh[n] if n in SMALL_SHARDED else wsh[n].astype(MXU_DTYPE) for n in order[:3]})
    *casts, xb = _cast_many([wsh[n] for n in order[3:]] + [x[0]], side=gather.take(through="ffn_conv_w"))
    gather.add_shards(dict(zip(order[3:], casts)))
    small = {n: given[n] for n in SMALL_REPLICATED}
    small["lru_wa"] = _block_diag(wsh["lru_wa"])
    small["lru_wi"] = _block_diag(wsh["lru_wi"])
    scatter = _Scatter(me, jnp.stack([_index(_relative(k)) for k in FAR]).astype(jnp.int32))

    vec_names = tuple(n for n in SMALL_REPLICATED if n not in SMALL_MATS) + SMALL_SHARDED
    sync = _SmallSync(vec_names, SMALL_MATS)
    dx = _forward_backward(x[0], xb, loss_target[0], small, gather, scatter, sync)

    loss_total, g_small, mat_sums = sync.end()
    loss_total = loss_total.reshape(())
    for n in SMALL_SHARDED:
        width = wsh[n].shape[1]
        g_small[n] = lax.dynamic_slice_in_dim(g_small[n], me * width, width, axis=1)
    g_small = {n: g_small[n].reshape(given[n].shape) for n in vec_names}
    out = {}
    results = _adamw_many(*[[d[n] for n in vec_names] for d in (given, m_given, v_given, g_small)])
    for n, delta, nm, nv in zip(vec_names, *results):
        out[n] = (g_small[n], delta, nm, nv)
    for n in SMALL_MATS:
        g = mat_sums[n].reshape(given[n].shape)
        out[n] = (g, *_adamw_blocks(given[n], m_given[n], v_given[n], g, name="adamw_" + n))

    tile_rows = {n: tr for n, _, tr in SHARDED}
    me1 = me.reshape(1).astype(jnp.int32)
    res = None
    for n in list(scatter.sends):
        own, pair, far = scatter.get(n)
        if n.startswith("w_in_"):
            row0 = int(n[len("w_in_"):]) * (D // W_IN_PARTS)
            res = _reduce_adamw(wsh["w_in"], msh["w_in"], vsh["w_in"], own, pair, far, me1, tr=tile_rows["w_in"],
                                name="adamw_" + n, row0=row0, earlier=res if row0 else None)
            out["w_in"] = tuple(r[None] for r in res)
        else:
            res_n = _reduce_adamw(wsh[n], msh[n], vsh[n], own, pair, far, me1, tr=tile_rows[n], name="adamw_" + n)
            out[n] = tuple(r[None] for r in res_n)

    outputs = [loss_total, dx[None]]
    for kind in range(4):
        outputs += [out[n][kind] for n in WEIGHTS]
    return tuple(outputs)
```

```python
import math

import jax
import jax.numpy as jnp
from jax import lax
from jax.experimental import pallas as pl
from jax.experimental.pallas import tpu as pltpu

F32 = jnp.float32
BF16 = jnp.bfloat16
MXU_DTYPE = jnp.bfloat16

N_DEV = 8
S = 2048
D = 2048
HEAD_DIM = 64
N_KV = 4
GROUP = 8
BLOCK = 128
D_KV = N_KV * HEAD_DIM
D_RNN = 2560
RNN_GROUP = 640
N_RNN_GROUPS = D_RNN // RNN_GROUP
RNN_BLOCK_W = 160
RNN_CONV_W = 4
LRU_C = 8.0
D_FF = 6144
FFN_CONV_W = 3
D_IN = 11776
OFF_K = 2048
OFF_V = 2304
OFF_RX = 2560
OFF_RY = 5120
OFF_GA = 7680
OFF_GR = 9728
LN_EPS = 1e-5
ALPHA = 2.0 ** 0.25
ADAM_LR = 0.001
ADAM_B1 = 0.9
ADAM_B2 = 0.999
ADAM_EPS = 1e-08
ADAM_WD = 0.01
ADAM_STEP = 10
NEG = -1e30
VMEM_LIMIT = 56 * 1024 * 1024
MID_RIDE_TENTHS = 6
MESH = pl.DeviceIdType.MESH
GELU_C = math.sqrt(2.0 / math.pi)


def _cparams(*sem):
    return pltpu.CompilerParams(dimension_semantics=sem or None, vmem_limit_bytes=VMEM_LIMIT)


def _call(body, *, name, grid, in_specs, out_specs, out_shape, operands, semantics, scratch_shapes=(), side=None):
    single = not isinstance(out_shape, (list, tuple))
    out_shape = [out_shape] if single else list(out_shape)
    out_specs = [out_specs] if single else list(out_specs)
    in_specs = list(in_specs)
    scratch_shapes = list(scratch_shapes)
    if side is None:
        res = pl.pallas_call(
            body, name=name, grid=grid, in_specs=in_specs, out_specs=out_specs, out_shape=out_shape,
            scratch_shapes=scratch_shapes, compiler_params=_cparams(*semantics))(*operands)
        return res[0] if single else res
    n_in, n_out, n_scr = len(in_specs), len(out_shape), len(scratch_shapes)
    s_in, s_out = len(side.operands), len(side.out_shape)
    hbm = pl.BlockSpec(memory_space=pltpu.HBM)
    steps = math.prod(grid)
    mid_step = (steps * MID_RIDE_TENTHS) // 10

    def with_copies(*refs):
        core_in, side_in = refs[:n_in], refs[n_in:n_in + s_in]
        o0 = n_in + s_in
        core_out, side_out = refs[o0:o0 + n_out], refs[o0 + n_out:o0 + n_out + s_out]
        c0 = o0 + n_out + s_out
        core_scr, sems = refs[c0:c0 + n_scr], refs[c0 + n_scr:]
        step = 0
        for d, size in enumerate(grid):
            step = step * size + pl.program_id(d)

        @pl.when(step == 0)
        def _():
            side.start(side_in, side_out, sems)

        body(*core_in, *core_out, *core_scr)

        @pl.when(step == mid_step)
        def _():
            side.mid(side_in, side_out, sems)

        @pl.when(step == steps - 1)
        def _():
            side.finish(side_in, side_out, sems)

    res = pl.pallas_call(
        with_copies, name=name, grid=grid,
        in_specs=in_specs + [hbm] * s_in, out_specs=out_specs + [hbm] * s_out,
        out_shape=out_shape + list(side.out_shape),
        scratch_shapes=scratch_shapes + list(side.sems),
        input_output_aliases={n_in + i: n_out + o for i, o in side.aliases.items()},
        compiler_params=_cparams(*(("arbitrary",) * len(grid))))(*operands, *side.operands)
    side.done(res[n_out:])
    return res[0] if single else res[:n_out]


def _run_side(side, name):
    def body(*refs):
        s_in, s_out = len(side.operands), len(side.out_shape)
        side.start(refs[:s_in], refs[s_in:s_in + s_out], refs[s_in + s_out:])
        side.mid(refs[:s_in], refs[s_in:s_in + s_out], refs[s_in + s_out:])
        side.finish(refs[:s_in], refs[s_in:s_in + s_out], refs[s_in + s_out:])

    hbm = pl.BlockSpec(memory_space=pltpu.HBM)
    res = pl.pallas_call(
        body, name=name, in_specs=[hbm] * len(side.operands), out_specs=[hbm] * len(side.out_shape),
        out_shape=list(side.out_shape), scratch_shapes=list(side.sems),
        input_output_aliases=dict(side.aliases))(*side.operands)
    side.done(res)


def _gelu(x):
    x2 = x * x
    t = jnp.tanh(GELU_C * (x + 0.044715 * x * x2))
    g = 0.5 * x * (1.0 + t)
    dg = 0.5 * (1.0 + t) + 0.5 * x * (1.0 - t * t) * (GELU_C * (1.0 + 3.0 * 0.044715 * x2))
    return g, dg


def _sigmoid(x):
    return 1.0 / (1.0 + jnp.exp(-x))


def _softplus(x):
    z = jnp.exp(-jnp.abs(x))
    small = z * (1.0 - z * (0.5 - z * (1.0 / 3.0 - 0.25 * z)))
    return jnp.maximum(x, 0.0) + jnp.where(z < 0.02, small, jnp.log(1.0 + z))


def _one_minus_exp(x):
    series = -x * (1.0 + x * (0.5 + x * (1.0 / 6.0 + x * (1.0 / 24.0))))
    return jnp.where(x > -0.03, series, 1.0 - jnp.exp(x))


def _colsum(v):
    return jnp.sum(v, axis=0, keepdims=True)


def _mm(a, b, *, tm, tn, tk, name, ta=False, tb=False, out_dtype=F32, b_block=None, out_block=None, add=None,
        add_scale=1.0, side=None):
    out_dtypes = out_dtype if isinstance(out_dtype, tuple) else (out_dtype,)
    if ta:
        k_dim, m_dim = a.shape
    else:
        m_dim, k_dim = a.shape
    if b_block is None:
        n_dim = b.shape[0] if tb else b.shape[1]
    else:
        n_dim = b.shape[1] if tb else b.shape[0] * b_block
    assert m_dim % tm == 0 and n_dim % tn == 0 and k_dim % tk == 0, (name, m_dim, n_dim, k_dim)
    nk = k_dim // tk
    dims = (((0 if ta else 1,), (1 if tb else 0,)), ((), ()))
    has_add = add is not None

    def body(*refs):
        a_ref, b_ref = refs[0], refs[1]
        add_ref = refs[2] if has_add else None
        first_out = 3 if has_add else 2
        o_refs = refs[first_out:first_out + len(out_dtypes)]

        def product():
            return lax.dot_general(a_ref[...].astype(MXU_DTYPE), b_ref[...].astype(MXU_DTYPE), dims,
                                   preferred_element_type=F32)

        def finish(acc):
            if has_add:
                acc = acc + add_scale * add_ref[...]
            for o_ref in o_refs:
                o_ref[...] = acc.astype(o_ref.dtype)

        if nk == 1:
            finish(product())
        else:
            acc_ref = refs[-1]
            k = pl.program_id(2)

            @pl.when(k == 0)
            def _():
                acc_ref[...] = jnp.zeros_like(acc_ref)

            acc_ref[...] += product()

            @pl.when(k == nk - 1)
            def _():
                finish(acc_ref[...])

    if ta:
        a_spec = pl.BlockSpec((tk, tm), lambda i, j, k: (k, i))
    else:
        a_spec = pl.BlockSpec((tm, tk), lambda i, j, k: (i, k))
    if b_block is None:
        if tb:
            b_spec = pl.BlockSpec((tn, tk), lambda i, j, k: (j, k))
        else:
            b_spec = pl.BlockSpec((tk, tn), lambda i, j, k: (k, j))
    elif tb:
        assert b_block % tk == 0
        b_spec = pl.BlockSpec((None, tn, tk), lambda i, j, k: ((k * tk) // b_block, j, ((k * tk) % b_block) // tk))
    else:
        assert b_block % tn == 0
        b_spec = pl.BlockSpec((None, tk, tn), lambda i, j, k: ((j * tn) // b_block, k, ((j * tn) % b_block) // tn))
    in_specs = [a_spec, b_spec]
    operands = [a, b]
    if has_add:
        in_specs.append(pl.BlockSpec((tm, tn), lambda i, j, k: (i, j)))
        operands.append(add)
    if out_block is None:
        out_spec = pl.BlockSpec((tm, tn), lambda i, j, k: (i, j))
        out_dims = (m_dim, n_dim)
    else:
        assert out_block % tn == 0
        out_spec = pl.BlockSpec((None, tm, tn), lambda i, j, k: ((j * tn) // out_block, i, ((j * tn) % out_block) // tn))
        out_dims = (n_dim // out_block, m_dim, out_block)
    res = _call(
        body,
        name=name,
        grid=(m_dim // tm, n_dim // tn, nk),
        in_specs=in_specs,
        out_specs=[out_spec] * len(out_dtypes),
        out_shape=[jax.ShapeDtypeStruct(out_dims, dt) for dt in out_dtypes],
        scratch_shapes=[pltpu.VMEM((tm, tn), F32)] if nk > 1 else [],
        semantics=("parallel", "parallel", "arbitrary"),
        operands=tuple(operands),
        side=side,
    )
    return res if isinstance(out_dtype, tuple) else res[0]


def _attn_bias(bias_ref, h):
    key = lax.broadcasted_iota(jnp.int32, (2 * BLOCK, GROUP * BLOCK), 0)
    col = lax.broadcasted_iota(jnp.int32, (2 * BLOCK, GROUP * BLOCK), 1)
    dist = BLOCK + (col & (BLOCK - 1)) - key
    head = h * GROUP + (col >> 7) + 1
    slope = jnp.exp(head.astype(F32) * (-0.25 * math.log(2.0)))
    bias = jnp.where((dist >= 0) & (dist < BLOCK), -slope * dist.astype(F32), NEG)
    bias_ref[1] = bias
    bias_ref[0] = jnp.where(key < BLOCK, NEG, bias)


def _attn_probs(kb, qt, bias, sink):
    s = jnp.dot(kb, qt, preferred_element_type=F32) * (HEAD_DIM ** -0.5) + bias
    m = jnp.maximum(jnp.max(s, axis=0, keepdims=True), sink)
    e = jnp.exp(s - m)
    e_sink = jnp.exp(sink - m)
    inv = 1.0 / (jnp.sum(e, axis=0, keepdims=True) + e_sink)
    return e * inv, e_sink * inv


def _heads_on_lanes(ref, r0):
    return jnp.concatenate([ref[g, :, pl.ds(r0, BLOCK)] for g in range(GROUP)], axis=1)


def _attn_fwd(qt, kp, vt, sink_row, side=None):
    cols = GROUP * BLOCK

    def body(q_ref, k_ref, vt_ref, sink_ref, o_ref, bias_ref):
        _attn_bias(bias_ref, pl.program_id(0))
        sink = sink_ref[...]

        def step(n, carry):
            r0 = pl.multiple_of(n * BLOCK, BLOCK)
            p, _ = _attn_probs(k_ref[pl.ds(r0, 2 * BLOCK), :], _heads_on_lanes(q_ref, r0),
                               bias_ref[jnp.minimum(n, 1)], sink)
            o = jnp.dot(vt_ref[:, pl.ds(r0, 2 * BLOCK)], p.astype(MXU_DTYPE), preferred_element_type=F32)
            for g in range(GROUP):
                o_ref[g, :, pl.ds(r0, BLOCK)] = o[:, g * BLOCK:(g + 1) * BLOCK].astype(o_ref.dtype)
            return carry

        lax.fori_loop(0, S // BLOCK, step, 0)

    hm = pl.BlockSpec((None, GROUP, HEAD_DIM, S), lambda h: (h, 0, 0, 0))
    return _call(
        body,
        name="attn_fwd",
        grid=(N_KV,),
        in_specs=[
            hm,
            pl.BlockSpec((None, BLOCK + S, HEAD_DIM), lambda h: (h, 0, 0)),
            pl.BlockSpec((None, HEAD_DIM, BLOCK + S), lambda h: (h, 0, 0)),
            pl.BlockSpec((None, 1, cols), lambda h: (h, 0, 0)),
        ],
        out_specs=hm,
        out_shape=jax.ShapeDtypeStruct((N_KV, GROUP, HEAD_DIM, S), MXU_DTYPE),
        scratch_shapes=[pltpu.VMEM((2, 2 * BLOCK, cols), F32)],
        semantics=("parallel",),
        operands=(qt, kp, vt, sink_row),
        side=side,
    )


def _attn_bwd(qt, kp, kt, vp, sink_row, dot_, side=None):
    cols = GROUP * BLOCK

    def body(q_ref, k_ref, kt_ref, v_ref, sink_ref, do_ref, dq_ref, dk_ref, dv_ref, dsink_ref, bias_ref):
        _attn_bias(bias_ref, pl.program_id(0))
        sink = sink_ref[...]
        dk_ref[...] = jnp.zeros_like(dk_ref)
        dv_ref[...] = jnp.zeros_like(dv_ref)
        nt = (((1,), (1,)), ((), ()))

        def step(n, sink_acc):
            r0 = pl.multiple_of(n * BLOCK, BLOCK)
            band = pl.ds(r0, 2 * BLOCK)
            qn = _heads_on_lanes(q_ref, r0)
            don = _heads_on_lanes(do_ref, r0)
            p, p_sink = _attn_probs(k_ref[band, :], qn, bias_ref[jnp.minimum(n, 1)], sink)
            dp = jnp.dot(v_ref[band, :], don, preferred_element_type=F32)
            delta = jnp.sum(p * dp, axis=0, keepdims=True)
            ds = (p * (dp - delta) * (HEAD_DIM ** -0.5)).astype(MXU_DTYPE)
            dq = jnp.dot(kt_ref[:, band], ds, preferred_element_type=F32)
            for g in range(GROUP):
                dq_ref[g, :, pl.ds(r0, BLOCK)] = dq[:, g * BLOCK:(g + 1) * BLOCK].astype(dq_ref.dtype)
            dk_ref[band, :] += lax.dot_general(ds, qn, nt, preferred_element_type=F32)
            dv_ref[band, :] += lax.dot_general(p.astype(MXU_DTYPE), don, nt, preferred_element_type=F32)
            return sink_acc - p_sink * delta

        sink_acc = lax.fori_loop(0, S // BLOCK, step, jnp.zeros((1, cols), F32))
        for g in range(GROUP):
            dsink_ref[g:g + 1, :] = jnp.sum(sink_acc[:, g * BLOCK:(g + 1) * BLOCK], axis=1, keepdims=True)

    hm = pl.BlockSpec((None, GROUP, HEAD_DIM, S), lambda h: (h, 0, 0, 0))
    kv = pl.BlockSpec((None, BLOCK + S, HEAD_DIM), lambda h: (h, 0, 0))
    return _call(
        body,
        name="attn_bwd",
        grid=(N_KV,),
        in_specs=[hm, kv, pl.BlockSpec((None, HEAD_DIM, BLOCK + S), lambda h: (h, 0, 0)), kv,
                  pl.BlockSpec((None, 1, cols), lambda h: (h, 0, 0)), hm],
        out_specs=[hm, kv, kv, pl.BlockSpec((None, GROUP, 1), lambda h: (h, 0, 0))],
        out_shape=[
            jax.ShapeDtypeStruct((N_KV, GROUP, HEAD_DIM, S), MXU_DTYPE),
            jax.ShapeDtypeStruct((N_KV, BLOCK + S, HEAD_DIM), F32),
            jax.ShapeDtypeStruct((N_KV, BLOCK + S, HEAD_DIM), F32),
            jax.ShapeDtypeStruct((N_KV, GROUP, 1), F32),
        ],
        scratch_shapes=[pltpu.VMEM((2, 2 * BLOCK, cols), F32)],
        semantics=("parallel",),
        operands=(qt, kp, kt, vp, sink_row, dot_),
        side=side,
    )


PAD = 8
CHUNK = 256


def _past_taps(xpad_ref, r0, width):
    ext = xpad_ref[pl.ds(r0, CHUNK + PAD), :]
    taps = []
    for k in range(width):
        back = width - 1 - k
        taps.append((ext if back == 0 else pltpu.roll(ext, back, 0))[PAD:, :])
    return taps


def _future_taps(xpad_ref, r0, width):
    ext = xpad_ref[pl.ds(r0, CHUNK + PAD), :]
    taps = []
    for ahead in range(width):
        taps.append((ext if ahead == 0 else pltpu.roll(ext, CHUNK + PAD - ahead, 0))[:CHUNK, :])
    return taps


def _conv_fwd(src, col0, w, b, *, tc, name, side=None):
    width, c_dim = w.shape

    def body(x_ref, w_ref, b_ref, o_ref, xpad_ref):
        xpad_ref[pl.ds(0, PAD), :] = jnp.zeros((PAD, tc), F32)
        xpad_ref[pl.ds(PAD, S), :] = x_ref[...]
        wv = w_ref[...]
        bv = b_ref[...]

        def step(ci, carry):
            r0 = pl.multiple_of(ci * CHUNK, CHUNK)
            taps = _past_taps(xpad_ref, r0, width)
            y = bv + taps[0] * wv[0:1, :]
            for k in range(1, width):
                y = y + taps[k] * wv[k:k + 1, :]
            o_ref[pl.ds(r0, CHUNK), :] = y
            return carry

        lax.fori_loop(0, S // CHUNK, step, 0)

    return _call(
        body,
        name=name,
        grid=(c_dim // tc,),
        in_specs=[
            pl.BlockSpec((S, tc), lambda j: (0, col0 // tc + j)),
            pl.BlockSpec((width, tc), lambda j: (0, j)),
            pl.BlockSpec((1, tc), lambda j: (0, j)),
        ],
        out_specs=pl.BlockSpec((S, tc), lambda j: (0, j)),
        out_shape=jax.ShapeDtypeStruct((S, c_dim), F32),
        scratch_shapes=[pltpu.VMEM((S + PAD, tc), F32)],
        semantics=("parallel",),
        operands=(src, w, b),
        side=side,
    )


def _conv_bwd(dy, src, col0, w, *, tc, name, side=None):
    width, c_dim = w.shape

    def body(dy_ref, x_ref, w_ref, dx_ref, dw_ref, db_ref, xpad_ref, dpad_ref):
        xpad_ref[pl.ds(0, PAD), :] = jnp.zeros((PAD, tc), F32)
        xpad_ref[pl.ds(PAD, S), :] = x_ref[...]
        dpad_ref[pl.ds(0, S), :] = dy_ref[...]
        dpad_ref[pl.ds(S, PAD), :] = jnp.zeros((PAD, tc), F32)
        wv = w_ref[...]

        def step(ci, acc):
            r0 = pl.multiple_of(ci * CHUNK, CHUNK)
            past = _past_taps(xpad_ref, r0, width)
            ahead = _future_taps(dpad_ref, r0, width)
            d = ahead[0]
            dx = d * wv[width - 1:width, :]
            for j in range(1, width):
                dx = dx + ahead[j] * wv[width - 1 - j:width - j, :]
            dx_ref[pl.ds(r0, CHUNK), :] = dx.astype(dx_ref.dtype)
            return tuple(acc[k] + _colsum(past[k] * d) for k in range(width)) + (acc[width] + _colsum(d),)

        zero = jnp.zeros((1, tc), F32)
        acc = lax.fori_loop(0, S // CHUNK, step, (zero,) * (width + 1))
        for k in range(width):
            dw_ref[k:k + 1, :] = acc[k]
        db_ref[...] = acc[width]

    return _call(
        body,
        name=name,
        grid=(c_dim // tc,),
        in_specs=[
            pl.BlockSpec((S, tc), lambda j: (0, j)),
            pl.BlockSpec((S, tc), lambda j: (0, col0 // tc + j)),
            pl.BlockSpec((width, tc), lambda j: (0, j)),
        ],
        out_specs=[
            pl.BlockSpec((S, tc), lambda j: (0, j)),
            pl.BlockSpec((width, tc), lambda j: (0, j)),
            pl.BlockSpec((1, tc), lambda j: (0, j)),
        ],
        out_shape=[
            jax.ShapeDtypeStruct((S, c_dim), MXU_DTYPE),
            jax.ShapeDtypeStruct((width, c_dim), F32),
            jax.ShapeDtypeStruct((1, c_dim), F32),
        ],
        scratch_shapes=[pltpu.VMEM((S + PAD, tc), F32), pltpu.VMEM((S + PAD, tc), F32)],
        semantics=("parallel",),
        operands=(dy, src, w),
        side=side,
    )


SCAN_TC = 256


def _lru_gates(rxc, wa, wi, ba, bi, side=None):
    tm = 512

    def body(x_ref, wa_ref, wi_ref, ba_ref, bi_ref, r_ref, i_ref):
        xv = x_ref[...].astype(MXU_DTYPE)
        r_ref[...] = _sigmoid(jnp.dot(xv, wa_ref[...].astype(MXU_DTYPE), preferred_element_type=F32) + ba_ref[...])
        i_ref[...] = _sigmoid(jnp.dot(xv, wi_ref[...].astype(MXU_DTYPE), preferred_element_type=F32) + bi_ref[...])

    x_spec = pl.BlockSpec((tm, RNN_GROUP), lambda g, i: (i, g))
    w_spec = pl.BlockSpec((None, RNN_GROUP, RNN_GROUP), lambda g, i: (g, 0, 0))
    b_spec = pl.BlockSpec((1, RNN_GROUP), lambda g, i: (0, g))
    return _call(
        body,
        name="lru_gates",
        grid=(N_RNN_GROUPS, S // tm),
        in_specs=[x_spec, w_spec, w_spec, b_spec, b_spec],
        out_specs=[x_spec, x_spec],
        out_shape=[jax.ShapeDtypeStruct((S, D_RNN), F32)] * 2,
        semantics=("parallel", "parallel"),
        operands=(rxc, wa, wi, ba, bi),
        side=side,
    )


def _scan_down(a, u, row):
    for d in (1, 2, 4):
        a_s = jnp.where(row >= d, pltpu.roll(a, d, 0), 1.0)
        u_s = jnp.where(row >= d, pltpu.roll(u, d, 0), 0.0)
        u = a * u_s + u
        a = a * a_s
    return a, u


def _scan_up(a, u, row):
    for d in (1, 2, 4):
        a_s = jnp.where(row < 8 - d, pltpu.roll(a, 8 - d, 0), 1.0)
        u_s = jnp.where(row < 8 - d, pltpu.roll(u, 8 - d, 0), 0.0)
        u = a * u_s + u
        a = a * a_s
    return a, u


def _lru_scan_fwd(r, i, rxc, proj, lam, side=None):
    tc = SCAN_TC

    def body(r_ref, i_ref, x_ref, ry_ref, lam_ref, h_ref, y_ref):
        rate = LRU_C * _softplus(-lam_ref[...])
        row = lax.broadcasted_iota(jnp.int32, (8, tc), 0)

        def step(ci, carry):
            r0 = pl.multiple_of(ci * 16, 16)
            log_a = -rate * r_ref[pl.ds(r0, 16), :]
            a16 = jnp.exp(log_a)
            u16 = jnp.sqrt(_one_minus_exp(2.0 * log_a)) * (i_ref[pl.ds(r0, 16), :] * x_ref[pl.ds(r0, 16), :])
            hs = []
            for half in range(2):
                a_cum, h0 = _scan_down(a16[8 * half:8 * half + 8, :], u16[8 * half:8 * half + 8, :], row)
                h = a_cum * carry + h0
                carry = jnp.broadcast_to(h[7:8, :], (8, tc))
                hs.append(h)
            h16 = jnp.concatenate(hs, axis=0)
            h_ref[pl.ds(r0, 16), :] = h16
            y_ref[pl.ds(r0, 16), :] = (h16 * _gelu(ry_ref[pl.ds(r0, 16), :])[0]).astype(y_ref.dtype)
            return carry

        lax.fori_loop(0, S // 16, step, jnp.zeros((8, tc), F32))

    col = pl.BlockSpec((S, tc), lambda j: (0, j))
    return _call(
        body,
        name="lru_scan_fwd",
        grid=(D_RNN // tc,),
        in_specs=[col, col, col, pl.BlockSpec((S, tc), lambda j: (0, OFF_RY // tc + j)),
                  pl.BlockSpec((1, tc), lambda j: (0, j))],
        out_specs=[col, col],
        out_shape=[jax.ShapeDtypeStruct((S, D_RNN), F32), jax.ShapeDtypeStruct((S, D_RNN), MXU_DTYPE)],
        semantics=("parallel",),
        operands=(r, i, rxc, proj, lam),
        side=side,
    )


def _lru_scan_bwd(dy, proj, h, r, i, rxc, lam, side=None):
    tc = SCAN_TC

    def body(dy_ref, ry_ref, h_ref, r_ref, i_ref, x_ref, lam_ref,
             dry_ref, dzr_ref, dzi_ref, dx_ref, dba_ref, dbi_ref, dlam_ref, a_ref, dh_ref, hp_ref):
        lam_v = lam_ref[...]
        rate = LRU_C * _softplus(-lam_v)
        dlam_scale = LRU_C * _sigmoid(-lam_v)
        row = lax.broadcasted_iota(jnp.int32, (8, tc), 0)
        hp_ref[pl.ds(0, PAD), :] = jnp.zeros((PAD, tc), F32)
        hp_ref[pl.ds(PAD, S), :] = h_ref[...]
        a_ref[pl.ds(S, PAD), :] = jnp.zeros((PAD, tc), F32)

        def prep(ci, carry):
            r0 = pl.multiple_of(ci * CHUNK, CHUNK)
            a_ref[pl.ds(r0, CHUNK), :] = jnp.exp(-rate * r_ref[pl.ds(r0, CHUNK), :])
            ge, dge = _gelu(ry_ref[pl.ds(r0, CHUNK), :])
            dyv = dy_ref[pl.ds(r0, CHUNK), :]
            dh_ref[pl.ds(r0, CHUNK), :] = dyv * ge
            dry_ref[pl.ds(r0, CHUNK), :] = (dyv * h_ref[pl.ds(r0, CHUNK), :] * dge).astype(dry_ref.dtype)
            return carry

        lax.fori_loop(0, S // CHUNK, prep, 0)

        def step(ci, state):
            carry, dba, dbi, dlam = state
            r0 = pl.multiple_of(S - 16 - ci * 16, 16)
            a_ext = a_ref[pl.ds(r0, 24), :]
            a_next = pltpu.roll(a_ext, 23, 0)
            h_prev = pltpu.roll(hp_ref[pl.ds(r0, 24), :], 1, 0)
            dh16 = dh_ref[pl.ds(r0, 16), :]
            gs = [None, None]
            for half in (1, 0):
                lo = 8 * half
                c_cum, g0 = _scan_up(a_next[lo:lo + 8, :], dh16[lo:lo + 8, :], row)
                g = c_cum * carry + g0
                carry = jnp.broadcast_to(g[0:1, :], (8, tc))
                gs[half] = g
            g16 = jnp.concatenate(gs, axis=0)
            a16 = a_ext[0:16, :]
            r16 = r_ref[pl.ds(r0, 16), :]
            i16 = i_ref[pl.ds(r0, 16), :]
            x16 = x_ref[pl.ds(r0, 16), :]
            a2 = a16 * a16
            sq = jnp.sqrt(_one_minus_exp(-2.0 * rate * r16))
            dx_ref[pl.ds(r0, 16), :] = g16 * sq * i16
            dzi = g16 * sq * x16 * i16 * (1.0 - i16)
            dlog_a = g16 * h_prev[8:24, :] * a16 - g16 * i16 * x16 * a2 / sq
            dzr = -rate * dlog_a * r16 * (1.0 - r16)
            dzr_ref[pl.ds(r0, 16), :] = dzr.astype(dzr_ref.dtype)
            dzi_ref[pl.ds(r0, 16), :] = dzi.astype(dzi_ref.dtype)
            return carry, dba + _colsum(dzr), dbi + _colsum(dzi), dlam + _colsum(dlog_a * r16)

        zero = jnp.zeros((1, tc), F32)
        _, dba, dbi, dlam = lax.fori_loop(0, S // 16, step, (jnp.zeros((8, tc), F32), zero, zero, zero))
        dba_ref[...] = dba
        dbi_ref[...] = dbi
        dlam_ref[...] = dlam * dlam_scale

    col = pl.BlockSpec((S, tc), lambda j: (0, j))
    vec = pl.BlockSpec((1, tc), lambda j: (0, j))
    return _call(
        body,
        name="lru_scan_bwd",
        grid=(D_RNN // tc,),
        in_specs=[col, pl.BlockSpec((S, tc), lambda j: (0, OFF_RY // tc + j)), col, col, col, col, vec],
        out_specs=[col, col, col, col, vec, vec, vec],
        out_shape=[jax.ShapeDtypeStruct((S, D_RNN), MXU_DTYPE)] * 3 + [jax.ShapeDtypeStruct((S, D_RNN), F32)]
        + [jax.ShapeDtypeStruct((1, D_RNN), F32)] * 3,
        scratch_shapes=[pltpu.VMEM((S + PAD, tc), F32), pltpu.VMEM((S, tc), F32), pltpu.VMEM((S + PAD, tc), F32)],
        semantics=("parallel",),
        operands=(dy, proj, h, r, i, rxc, lam),
        side=side,
    )


def _lru_gate_wgrad(rxc, dzr, dzi, side=None):
    def body(x_ref, dzr_ref, dzi_ref, dwa_ref, dwi_ref):
        xv = x_ref[...].astype(MXU_DTYPE)
        dims = (((0,), (0,)), ((), ()))
        dwa_ref[...] = lax.dot_general(xv, dzr_ref[...], dims, preferred_element_type=F32)
        dwi_ref[...] = lax.dot_general(xv, dzi_ref[...], dims, preferred_element_type=F32)

    col = pl.BlockSpec((S, RNN_GROUP), lambda g: (0, g))
    w_spec = pl.BlockSpec((None, RNN_GROUP, RNN_GROUP), lambda g: (g, 0, 0))
    return _call(
        body,
        name="lru_gate_wgrad",
        grid=(N_RNN_GROUPS,),
        in_specs=[col, col, col],
        out_specs=[w_spec, w_spec],
        out_shape=[jax.ShapeDtypeStruct((N_RNN_GROUPS, RNN_GROUP, RNN_GROUP), F32)] * 2,
        semantics=("parallel",),
        operands=(rxc, dzr, dzi),
        side=side,
    )


def _lru_gate_xgrad(dzr, dzi, wa, wi, dx_in, side=None):
    tm = 512

    def body(dzr_ref, dzi_ref, wa_ref, wi_ref, dx_ref, o_ref):
        dims = (((1,), (1,)), ((), ()))
        o_ref[...] = (dx_ref[...]
                      + lax.dot_general(dzr_ref[...], wa_ref[...].astype(MXU_DTYPE), dims, preferred_element_type=F32)
                      + lax.dot_general(dzi_ref[...], wi_ref[...].astype(MXU_DTYPE), dims, preferred_element_type=F32))

    x_spec = pl.BlockSpec((tm, RNN_GROUP), lambda g, i: (i, g))
    w_spec = pl.BlockSpec((None, RNN_GROUP, RNN_GROUP), lambda g, i: (g, 0, 0))
    return _call(
        body,
        name="lru_gate_xgrad",
        grid=(N_RNN_GROUPS, S // tm),
        in_specs=[x_spec, x_spec, w_spec, w_spec, x_spec],
        out_specs=x_spec,
        out_shape=jax.ShapeDtypeStruct((S, D_RNN), F32),
        semantics=("parallel", "parallel"),
        operands=(dzr, dzi, wa, wi, dx_in),
        side=side,
    )


def _gate_fwd(y_attn, y_rnn, proj, b_gate, side=None):
    t = 512

    def body(ya_ref, yr_ref, ga_ref, gr_ref, ba_ref, br_ref, o_ref):
        o_ref[...] = (_sigmoid(ga_ref[...] + ba_ref[...]) * ya_ref[...]
                      + _sigmoid(gr_ref[...] + br_ref[...]) * yr_ref[...]).astype(o_ref.dtype)

    tile = pl.BlockSpec((t, t), lambda i, j: (i, j))
    return _call(
        body,
        name="gate_fwd",
        grid=(S // t, D // t),
        in_specs=[tile, tile,
                  pl.BlockSpec((t, t), lambda i, j: (i, OFF_GA // t + j)),
                  pl.BlockSpec((t, t), lambda i, j: (i, OFF_GR // t + j)),
                  pl.BlockSpec((1, t), lambda i, j: (0, j)),
                  pl.BlockSpec((1, t), lambda i, j: (0, D // t + j))],
        out_specs=tile,
        out_shape=jax.ShapeDtypeStruct((S, D), MXU_DTYPE),
        semantics=("parallel", "parallel"),
        operands=(y_attn, y_rnn, proj, proj, b_gate, b_gate),
        side=side,
    )


def _gate_bwd(dmix, y_attn, y_rnn, proj, b_gate, side=None):
    t = 512

    def body(dm_ref, ya_ref, yr_ref, ga_ref, gr_ref, ba_ref, br_ref,
             dya_ref, dyr_ref, dga_ref, dgr_ref, dba_ref, dbr_ref):
        @pl.when(pl.program_id(1) == 0)
        def _():
            dba_ref[...] = jnp.zeros_like(dba_ref)
            dbr_ref[...] = jnp.zeros_like(dbr_ref)

        dm = dm_ref[...]
        ga = _sigmoid(ga_ref[...] + ba_ref[...])
        gr = _sigmoid(gr_ref[...] + br_ref[...])
        dya_ref[...] = (dm * ga).astype(dya_ref.dtype)
        dyr_ref[...] = (dm * gr).astype(dyr_ref.dtype)
        dga = dm * ya_ref[...] * ga * (1.0 - ga)
        dgr = dm * yr_ref[...] * gr * (1.0 - gr)
        dga_ref[...] = dga.astype(dga_ref.dtype)
        dgr_ref[...] = dgr.astype(dgr_ref.dtype)
        dba_ref[...] += _colsum(dga)
        dbr_ref[...] += _colsum(dgr)

    tile = pl.BlockSpec((t, t), lambda j, i: (i, j))
    vec = pl.BlockSpec((1, t), lambda j, i: (0, j))
    return _call(
        body,
        name="gate_bwd",
        grid=(D // t, S // t),
        in_specs=[tile, tile, tile,
                  pl.BlockSpec((t, t), lambda j, i: (i, OFF_GA // t + j)),
                  pl.BlockSpec((t, t), lambda j, i: (i, OFF_GR // t + j)),
                  vec,
                  pl.BlockSpec((1, t), lambda j, i: (0, D // t + j))],
        out_specs=[tile, tile, tile, tile, vec, vec],
        out_shape=[jax.ShapeDtypeStruct((S, D), MXU_DTYPE)] * 4 + [jax.ShapeDtypeStruct((1, D), F32)] * 2,
        semantics=("parallel", "arbitrary"),
        operands=(dmix, y_attn, y_rnn, proj, proj, b_gate, b_gate),
        side=side,
    )


LN_TM = 256


def _ln_stats(pre):
    mu = jnp.mean(pre, axis=-1, keepdims=True)
    xc = pre - mu
    rstd = lax.rsqrt(jnp.mean(xc * xc, axis=-1, keepdims=True) + LN_EPS)
    return xc * rstd, rstd


def _ln_input_grad(dy, xhat, rstd, g):
    dyg = dy * g
    return rstd * (dyg - jnp.mean(dyg, axis=-1, keepdims=True)
                   - xhat * jnp.mean(dyg * xhat, axis=-1, keepdims=True))


def _ln_fwd(res, branch, g, b, side=None):
    def body(res_ref, br_ref, g_ref, b_ref, y_ref, yb_ref, xhat_ref, rstd_ref):
        xhat, rstd = _ln_stats(ALPHA * res_ref[...] + br_ref[...])
        y = xhat * g_ref[...] + b_ref[...]
        y_ref[...] = y
        yb_ref[...] = y.astype(yb_ref.dtype)
        xhat_ref[...] = xhat
        rstd_ref[...] = rstd

    tile = pl.BlockSpec((LN_TM, D), lambda i: (i, 0))
    vec = pl.BlockSpec((1, D), lambda i: (0, 0))
    return _call(
        body,
        name="ln_fwd",
        grid=(S // LN_TM,),
        in_specs=[tile, tile, vec, vec],
        out_specs=[tile, tile, tile, pl.BlockSpec((LN_TM, 1), lambda i: (i, 0))],
        out_shape=[jax.ShapeDtypeStruct((S, D), F32), jax.ShapeDtypeStruct((S, D), MXU_DTYPE),
                   jax.ShapeDtypeStruct((S, D), F32), jax.ShapeDtypeStruct((S, 1), F32)],
        semantics=("parallel",),
        operands=(res, branch, g, b),
        side=side,
    )


def _ln_bwd(dy_a, dy_b, xhat, rstd, g, side=None):
    def body(da_ref, db_in_ref, xhat_ref, rstd_ref, g_ref, dp_ref, dpb_ref, dg_ref, db_ref):
        @pl.when(pl.program_id(0) == 0)
        def _():
            dg_ref[...] = jnp.zeros_like(dg_ref)
            db_ref[...] = jnp.zeros_like(db_ref)

        dy = da_ref[...] + ALPHA * db_in_ref[...]
        xhat = xhat_ref[...]
        dp = _ln_input_grad(dy, xhat, rstd_ref[...], g_ref[...])
        dp_ref[...] = dp
        dpb_ref[...] = dp.astype(dpb_ref.dtype)
        dg_ref[...] += _colsum(dy * xhat)
        db_ref[...] += _colsum(dy)

    tile = pl.BlockSpec((LN_TM, D), lambda i: (i, 0))
    vec = pl.BlockSpec((1, D), lambda i: (0, 0))
    return _call(
        body,
        name="ln_bwd",
        grid=(S // LN_TM,),
        in_specs=[tile, tile, tile, pl.BlockSpec((LN_TM, 1), lambda i: (i, 0)), vec],
        out_specs=[tile, tile, vec, vec],
        out_shape=[jax.ShapeDtypeStruct((S, D), F32), jax.ShapeDtypeStruct((S, D), MXU_DTYPE),
                   jax.ShapeDtypeStruct((1, D), F32), jax.ShapeDtypeStruct((1, D), F32)],
        semantics=("arbitrary",),
        operands=(dy_a, dy_b, xhat, rstd, g),
        side=side,
    )


def _ln_loss_bwd(res, branch, g, b, target, side=None):
    def body(res_ref, br_ref, g_ref, b_ref, t_ref, loss_ref, dp_ref, dpb_ref, dg_ref, db_ref):
        @pl.when(pl.program_id(0) == 0)
        def _():
            loss_ref[...] = jnp.zeros_like(loss_ref)
            dg_ref[...] = jnp.zeros_like(dg_ref)
            db_ref[...] = jnp.zeros_like(db_ref)

        xhat, rstd = _ln_stats(ALPHA * res_ref[...] + br_ref[...])
        gv = g_ref[...]
        err = xhat * gv + b_ref[...] - t_ref[...]
        loss_ref[...] += (0.5 / D) * jnp.sum(_colsum(err * err), axis=1, keepdims=True)
        dy = err * (1.0 / D)
        dp = _ln_input_grad(dy, xhat, rstd, gv)
        dp_ref[...] = dp
        dpb_ref[...] = dp.astype(dpb_ref.dtype)
        dg_ref[...] += _colsum(dy * xhat)
        db_ref[...] += _colsum(dy)

    tile = pl.BlockSpec((LN_TM, D), lambda i: (i, 0))
    vec = pl.BlockSpec((1, D), lambda i: (0, 0))
    return _call(
        body,
        name="ln_loss_bwd",
        grid=(S // LN_TM,),
        in_specs=[tile, tile, vec, vec, tile],
        out_specs=[pl.BlockSpec((1, 1), lambda i: (0, 0)), tile, tile, vec, vec],
        out_shape=[jax.ShapeDtypeStruct((1, 1), F32), jax.ShapeDtypeStruct((S, D), F32),
                   jax.ShapeDtypeStruct((S, D), MXU_DTYPE),
                   jax.ShapeDtypeStruct((1, D), F32), jax.ShapeDtypeStruct((1, D), F32)],
        semantics=("arbitrary",),
        operands=(res, branch, g, b, target),
        side=side,
    )


FFN_TC = 256


def _ffn_act_fwd(up, gpre, w, b, side=None):
    tc = FFN_TC

    def body(up_ref, x_ref, w_ref, b_ref, o_ref, xpad_ref):
        xpad_ref[pl.ds(0, PAD), :] = jnp.zeros((PAD, tc), F32)
        xpad_ref[pl.ds(PAD, S), :] = x_ref[...]
        wv = w_ref[...]
        bv = b_ref[...]

        def step(ci, carry):
            r0 = pl.multiple_of(ci * CHUNK, CHUNK)
            taps = _past_taps(xpad_ref, r0, FFN_CONV_W)
            gate = bv + taps[0] * wv[0:1, :] + taps[1] * wv[1:2, :] + taps[2] * wv[2:3, :]
            o_ref[pl.ds(r0, CHUNK), :] = (_gelu(gate)[0] * up_ref[pl.ds(r0, CHUNK), :]).astype(o_ref.dtype)
            return carry

        lax.fori_loop(0, S // CHUNK, step, 0)

    col = pl.BlockSpec((S, tc), lambda j: (0, j))
    return _call(
        body,
        name="ffn_act_fwd",
        grid=(D_FF // tc,),
        in_specs=[col, col, pl.BlockSpec((FFN_CONV_W, tc), lambda j: (0, j)), pl.BlockSpec((1, tc), lambda j: (0, j))],
        out_specs=col,
        out_shape=jax.ShapeDtypeStruct((S, D_FF), MXU_DTYPE),
        scratch_shapes=[pltpu.VMEM((S + PAD, tc), F32)],
        semantics=("parallel",),
        operands=(up, gpre, w, b),
        side=side,
    )


def _ffn_act_bwd(dfin, up, gpre, w, b, side=None):
    tc = FFN_TC
    width = FFN_CONV_W

    def body(df_ref, up_ref, x_ref, w_ref, b_ref, dup_ref, dx_ref, dw_ref, db_ref, xpad_ref, dpad_ref):
        xpad_ref[pl.ds(0, PAD), :] = jnp.zeros((PAD, tc), F32)
        xpad_ref[pl.ds(PAD, S), :] = x_ref[...]
        dpad_ref[pl.ds(S, PAD), :] = jnp.zeros((PAD, tc), F32)
        wv = w_ref[...]
        bv = b_ref[...]

        def gate_grad(ci, acc):
            r0 = pl.multiple_of(ci * CHUNK, CHUNK)
            taps = _past_taps(xpad_ref, r0, width)
            gate = bv + taps[0] * wv[0:1, :] + taps[1] * wv[1:2, :] + taps[2] * wv[2:3, :]
            ge, dge = _gelu(gate)
            df = df_ref[pl.ds(r0, CHUNK), :]
            dup_ref[pl.ds(r0, CHUNK), :] = (df * ge).astype(dup_ref.dtype)
            d = df * up_ref[pl.ds(r0, CHUNK), :] * dge
            dpad_ref[pl.ds(r0, CHUNK), :] = d
            return tuple(acc[k] + _colsum(taps[k] * d) for k in range(width)) + (acc[width] + _colsum(d),)

        zero = jnp.zeros((1, tc), F32)
        acc = lax.fori_loop(0, S // CHUNK, gate_grad, (zero,) * (width + 1))
        for k in range(width):
            dw_ref[k:k + 1, :] = acc[k]
        db_ref[...] = acc[width]

        def input_grad(ci, carry):
            r0 = pl.multiple_of(ci * CHUNK, CHUNK)
            ahead = _future_taps(dpad_ref, r0, width)
            dx = ahead[0] * wv[2:3, :] + ahead[1] * wv[1:2, :] + ahead[2] * wv[0:1, :]
            dx_ref[pl.ds(r0, CHUNK), :] = dx.astype(dx_ref.dtype)
            return carry

        lax.fori_loop(0, S // CHUNK, input_grad, 0)

    col = pl.BlockSpec((S, tc), lambda j: (0, j))
    w_spec = pl.BlockSpec((width, tc), lambda j: (0, j))
    vec = pl.BlockSpec((1, tc), lambda j: (0, j))
    return _call(
        body,
        name="ffn_act_bwd",
        grid=(D_FF // tc,),
        in_specs=[col, col, col, w_spec, vec],
        out_specs=[col, col, w_spec, vec],
        out_shape=[jax.ShapeDtypeStruct((S, D_FF), MXU_DTYPE)] * 2
        + [jax.ShapeDtypeStruct((width, D_FF), F32), jax.ShapeDtypeStruct((1, D_FF), F32)],
        scratch_shapes=[pltpu.VMEM((S + PAD, tc), F32), pltpu.VMEM((S + PAD, tc), F32)],
        semantics=("parallel",),
        operands=(dfin, up, gpre, w, b),
        side=side,
    )


def _adamw_update(w, g, m, v):
    m = ADAM_B1 * m + (1.0 - ADAM_B1) * g
    v = ADAM_B2 * v + (1.0 - ADAM_B2) * (g * g)
    m_hat = m / (1.0 - ADAM_B1 ** ADAM_STEP)
    v_hat = v / (1.0 - ADAM_B2 ** ADAM_STEP)
    delta = -ADAM_LR * (m_hat / (jnp.sqrt(v_hat) + ADAM_EPS) + ADAM_WD * w)
    return delta, m, v


def _add_pairs(send, pair, far_index, *, name):
    _, r_dim, c_dim = send.shape
    tr = r_dim // 4

    def body(far_ref, mine_ref, theirs_ref, o_ref):
        o_ref[...] = (mine_ref[...].astype(F32) + theirs_ref[...].astype(F32)).astype(o_ref.dtype)

    return pl.pallas_call(
        body,
        name=name,
        grid_spec=pltpu.PrefetchScalarGridSpec(
            num_scalar_prefetch=1,
            grid=(3, r_dim // tr),
            in_specs=[pl.BlockSpec((None, tr, c_dim), lambda j, i, far: (far[j], i, 0)),
                      pl.BlockSpec((None, tr, c_dim), lambda j, i, far: (1 + j, i, 0))],
            out_specs=pl.BlockSpec((None, tr, c_dim), lambda j, i, far: (j, i, 0)),
        ),
        out_shape=jax.ShapeDtypeStruct((3, r_dim, c_dim), BF16),
        compiler_params=_cparams("parallel", "parallel"),
    )(far_index, send, pair)


def _reduce_adamw(w, m, v, g_own, pair, far, me, *, tr, name, row0=0, earlier=None):
    r_dim, c_dim = w.shape
    rows = pair.shape[1]
    first = row0 // tr

    def body(me_ref, w_ref, m_ref, v_ref, g_ref, pair_ref, far_ref, *refs):
        grad_ref, delta_ref, nm_ref, nv_ref = refs[-4:]
        g = g_ref[...] + pair_ref[...].astype(F32)
        for j in range(3):
            g = g + far_ref[j].astype(F32)
        delta, nm, nv = _adamw_update(w_ref[...], g, m_ref[...], v_ref[...])
        grad_ref[...] = g
        delta_ref[...] = delta
        nm_ref[...] = nm
        nv_ref[...] = nv

    tile = pl.BlockSpec((tr, c_dim), lambda i, me: (first + i, 0))
    if g_own.ndim == 3:
        own_spec = pl.BlockSpec((None, tr, c_dim), lambda i, me: (me[0], i, 0))
    else:
        own_spec = pl.BlockSpec((tr, c_dim), lambda i, me: (i, 0))
    earlier = list(earlier or ())
    return pl.pallas_call(
        body,
        name=name,
        grid_spec=pltpu.PrefetchScalarGridSpec(
            num_scalar_prefetch=1,
            grid=(rows // tr,),
            in_specs=[tile, tile, tile, own_spec, pl.BlockSpec((None, tr, c_dim), lambda i, me: (0, i, 0)),
                      pl.BlockSpec((3, tr, c_dim), lambda i, me: (0, i, 0))]
            + [pl.BlockSpec(memory_space=pl.ANY)] * len(earlier),
            out_specs=[tile] * 4,
        ),
        out_shape=[jax.ShapeDtypeStruct((r_dim, c_dim), F32)] * 4,
        input_output_aliases={7 + k: k for k in range(len(earlier))},
        compiler_params=_cparams("parallel"),
    )(me, w, m, v, g_own, pair, far, *earlier)


def _adamw_many(ws, ms, vs, gs):
    n = len(ws)

    def body(*refs):
        for i in range(n):
            delta, nm, nv = _adamw_update(refs[i][...], refs[3 * n + i][...], refs[n + i][...], refs[2 * n + i][...])
            refs[4 * n + i][...] = delta
            refs[5 * n + i][...] = nm
            refs[6 * n + i][...] = nv

    vmem = pl.BlockSpec(memory_space=pltpu.VMEM)
    res = pl.pallas_call(
        body,
        name="adamw_small",
        in_specs=[vmem] * (4 * n),
        out_specs=[vmem] * (3 * n),
        out_shape=[jax.ShapeDtypeStruct(w.shape, F32) for w in ws] * 3,
        compiler_params=pltpu.CompilerParams(vmem_limit_bytes=VMEM_LIMIT),
    )(*ws, *ms, *vs, *gs)
    return res[:n], res[n:2 * n], res[2 * n:]


def _adamw_blocks(w, m, v, g, *, name, side=None):
    per = 2

    def body(w_ref, m_ref, v_ref, g_ref, delta_ref, nm_ref, nv_ref):
        delta, nm, nv = _adamw_update(w_ref[...], g_ref[...], m_ref[...], v_ref[...])
        delta_ref[...] = delta
        nm_ref[...] = nm
        nv_ref[...] = nv

    tile = pl.BlockSpec((1, per) + w.shape[2:], lambda i: (0, i, 0, 0))
    return _call(
        body,
        name=name,
        grid=(w.shape[1] // per,),
        in_specs=[tile] * 4,
        out_specs=[tile] * 3,
        out_shape=[jax.ShapeDtypeStruct(w.shape, F32)] * 3,
        semantics=("parallel",),
        operands=(w, m, v, g),
        side=side,
    )


def _coords():
    return lax.axis_index("x"), lax.axis_index("y"), lax.axis_index("c")


def _flip(coord, bit):
    return 1 - coord if bit else coord


def _relative(k):
    x, y, c = _coords()
    return _flip(x, k & 4), _flip(y, k & 2), _flip(c, k & 1)


def _index(pos):
    return 4 * pos[0] + 2 * pos[1] + pos[2]


FAR = (4, 2, 6)
AG_US_PER_MB = 38.0
RS_US_PER_MB = 46.0
MIN_RIDE_US = 30.0
MIN_GATHER_RIDE_US = 22.0
ROW_ALIGN = 32


def _chunks(items, cursor, us, us_per_mb, through=None):
    budget = float("inf") if us is None else us / us_per_mb * 2 ** 20
    names = list(items)
    if through is not None:
        names = names[:names.index(through) + 1]
    chunks = []
    for name in names:
        arr = items[name]
        r_dim, c_dim = arr.shape[-2:]
        row_bytes = c_dim * arr.dtype.itemsize
        while cursor[name] < r_dim and budget > 0:
            rows = r_dim - cursor[name]
            if r_dim > ROW_ALIGN and budget < rows * row_bytes:
                rows = min(rows, max(ROW_ALIGN, int(budget // row_bytes) // ROW_ALIGN * ROW_ALIGN))
            chunks.append((name, cursor[name], rows))
            cursor[name] += rows
            budget -= rows * row_bytes
    return chunks


class _Gather:
    def __init__(self, shards):
        self.shards, self.bufs, self.cursor = {}, {}, {}
        self.add_shards(shards)

    def add_shards(self, shards):
        for n, shard in shards.items():
            self.shards[n], self.bufs[n], self.cursor[n] = shard, None, 0

    def take(self, us=None, through=None):
        if us is not None and us < MIN_GATHER_RIDE_US:
            return None
        chunks = _chunks(self.shards, self.cursor, us, AG_US_PER_MB, through)
        return _GatherSide(self, chunks) if chunks else None

    def get(self, name):
        chunks = _chunks(self.shards, self.cursor, None, AG_US_PER_MB, through=name)
        if chunks:
            _run_side(_GatherSide(self, chunks), "gather_" + name)
        return self.bufs[name]


class _GatherSide:
    SEMS = 8

    def __init__(self, owner, chunks):
        self.owner, self.chunks = owner, chunks
        self.names = list(dict.fromkeys(n for n, _, _ in chunks))
        old = [n for n in self.names if owner.bufs[n] is not None]
        self.operands = [owner.shards[n] for n in self.names] + [owner.bufs[n] for n in old]
        self.out_shape = [jax.ShapeDtypeStruct((N_DEV,) + owner.shards[n].shape, owner.shards[n].dtype)
                          for n in self.names]
        self.aliases = {len(self.names) + i: self.names.index(n) for i, n in enumerate(old)}
        self.sems = [pltpu.SemaphoreType.DMA((self.SEMS * len(chunks),)),
                     pltpu.SemaphoreType.DMA((self.SEMS * len(chunks),)), pltpu.SemaphoreType.DMA((len(chunks),))]

    def _halves(self, ci):
        _, r0, rows = self.chunks[ci]
        if rows % ROW_ALIGN:
            return None
        return (r0, rows // 2), (r0 + rows // 2, rows // 2)

    def _copy(self, ins, outs, sems, ci, s, block, to, rows=None, from_shard=False):
        name, r0, n = self.chunks[ci]
        if rows is not None:
            r0, n = rows
        w = self.names.index(name)
        slot = outs[w].at[_index(block), pl.ds(r0, n)]
        return pltpu.make_async_remote_copy(
            src_ref=ins[w].at[pl.ds(r0, n)] if from_shard else slot, dst_ref=slot,
            send_sem=sems[0].at[self.SEMS * ci + s], recv_sem=sems[1].at[self.SEMS * ci + s],
            device_id=to, device_id_type=MESH)

    def _own(self, ins, outs, sems, ci):
        name, r0, rows = self.chunks[ci]
        w = self.names.index(name)
        return pltpu.make_async_copy(ins[w].at[pl.ds(r0, rows)], outs[w].at[_index(_relative(0)), pl.ds(r0, rows)],
                                     sems[2].at[ci])

    def _pass(self, ins, outs, sems, ci, which):
        source, target = ((4, 2), (2, 4))[which]
        return self._copy(ins, outs, sems, ci, 3 + which, _relative(source), _relative(target),
                          rows=self._halves(ci)[which])

    def start(self, ins, outs, sems):
        me = _relative(0)
        for ci in range(len(self.chunks)):
            self._own(ins, outs, sems, ci).start()
        for ci in range(len(self.chunks)):
            self._copy(ins, outs, sems, ci, 1, me, _relative(4), from_shard=True).start()
            self._copy(ins, outs, sems, ci, 2, me, _relative(2), from_shard=True).start()
            if self._halves(ci) is None:
                self._copy(ins, outs, sems, ci, 3, me, _relative(6), from_shard=True).start()
        for ci in range(len(self.chunks)):
            self._copy(ins, outs, sems, ci, 0, me, _relative(1), from_shard=True).start()

    def mid(self, ins, outs, sems):
        me = _relative(0)
        cut = [ci for ci in range(len(self.chunks)) if self._halves(ci) is not None]
        for ci in cut:
            self._copy(ins, outs, sems, ci, 1, _relative(4), me).wait_recv()
            self._pass(ins, outs, sems, ci, 0).start()
            self._copy(ins, outs, sems, ci, 5, _relative(4), _relative(1)).start()
        for ci in cut:
            self._copy(ins, outs, sems, ci, 2, _relative(2), me).wait_recv()
            self._pass(ins, outs, sems, ci, 1).start()
            self._copy(ins, outs, sems, ci, 6, _relative(2), _relative(1)).start()

    def finish(self, ins, outs, sems):
        me, sibling = _relative(0), _relative(1)
        n = len(self.chunks)
        for ci in range(n):
            if self._halves(ci) is None:
                for s, k in ((1, 4), (2, 2), (3, 6)):
                    self._copy(ins, outs, sems, ci, s, _relative(k), me).wait_recv()
                for j, k in enumerate(FAR):
                    self._copy(ins, outs, sems, ci, 5 + j, _relative(k), sibling).start()
            else:
                h0, h1 = self._halves(ci)
                self._copy(ins, outs, sems, ci, 3, _relative(6), me, rows=h0).wait_recv()
                self._copy(ins, outs, sems, ci, 4, _relative(6), me, rows=h1).wait_recv()
                self._copy(ins, outs, sems, ci, 7, _relative(6), sibling).start()
        for ci in range(n):
            self._copy(ins, outs, sems, ci, 0, sibling, me).wait_recv()
            for j, k in enumerate(FAR):
                self._copy(ins, outs, sems, ci, 5 + j, _relative(k | 1), me).wait_recv()
        for ci in range(n):
            self._copy(ins, outs, sems, ci, 0, me, sibling, from_shard=True).wait_send()
            self._copy(ins, outs, sems, ci, 1, me, _relative(4), from_shard=True).wait_send()
            self._copy(ins, outs, sems, ci, 2, me, _relative(2), from_shard=True).wait_send()
            if self._halves(ci) is None:
                self._copy(ins, outs, sems, ci, 3, me, _relative(6), from_shard=True).wait_send()
            else:
                self._pass(ins, outs, sems, ci, 0).wait_send()
                self._pass(ins, outs, sems, ci, 1).wait_send()
            for j, k in enumerate(FAR):
                self._copy(ins, outs, sems, ci, 5 + j, _relative(k), sibling).wait_send()
            self._own(ins, outs, sems, ci).wait()

    def done(self, results):
        for n, buf in zip(self.names, results):
            self.owner.bufs[n] = buf


class _Scatter:
    def __init__(self, me, far_index):
        self.me, self.far_index = me, far_index
        self.sends, self.owns, self.pairs, self.sums, self.fars = {}, {}, {}, {}, {}
        self.pair_cursor, self.far_cursor = {}, {}

    def add(self, name, send, own):
        self.sends[name] = send
        self.owns[name] = own
        self.pairs[name] = self.fars[name] = None
        self.pair_cursor[name] = 0

    def _rows(self, name):
        return self.sends[name].shape[1]

    def _add_ready_pairs(self):
        for name in self.sends:
            if name not in self.sums and self.pair_cursor[name] == self._rows(name):
                self.sums[name] = _add_pairs(self.sends[name], self.pairs[name], self.far_index, name="pair_" + name)
                self.far_cursor[name] = 0

    def _side(self, us, through=None):
        self._add_ready_pairs()
        names = list(self.sends)
        if through is not None:
            names = names[:names.index(through) + 1]
        pair_chunks = [(n, self.pair_cursor[n], self._rows(n) - self.pair_cursor[n]) for n in names
                       if self.pair_cursor[n] < self._rows(n)]
        for n, _, _ in pair_chunks:
            self.pair_cursor[n] = self._rows(n)
        far_chunks = _chunks(self.sums, self.far_cursor, us, RS_US_PER_MB,
                             through if through in self.sums else None) if self.sums else []
        return _ScatterSide(self, pair_chunks, far_chunks) if pair_chunks or far_chunks else None

    def add_blocks(self, name, blocks32, blocks16):
        self.add(name, blocks16, blocks32)

    def add_cols(self, name, full32, full16):
        width = full32.shape[1] // N_DEV
        self.add(name, _blocks_cols(full16), lax.dynamic_slice_in_dim(full32, self.me * width, width, axis=1))

    def take(self, us):
        return self._side(us) if us >= MIN_RIDE_US else None

    def flush_pairs(self, name):
        side = self._side(0.0)
        if side is not None:
            _run_side(side, name)
        self._add_ready_pairs()

    def get(self, name):
        step = 0
        while name not in self.sums or self.far_cursor[name] < self._rows(name):
            _run_side(self._side(None, through=name), "scatter_%s_%d" % (name, step))
            step += 1
        return self.owns[name], self.pairs[name], self.fars[name]


class _ScatterSide:
    TO_SIBLING = (1, 5, 3, 7)

    def __init__(self, owner, pair_chunks, far_chunks):
        self.owner, self.pair_chunks, self.far_chunks = owner, pair_chunks, far_chunks
        self.pair_names = list(dict.fromkeys(n for n, _, _ in pair_chunks))
        self.far_names = list(dict.fromkeys(n for n, _, _ in far_chunks))
        ins = [(owner.sends[n], owner.pairs[n], (4,)) for n in self.pair_names]
        ins += [(owner.sums[n], owner.fars[n], (3,)) for n in self.far_names]
        old = [i for i, (_, buf, _) in enumerate(ins) if buf is not None]
        self.operands = [src for src, _, _ in ins] + [ins[i][1] for i in old]
        self.out_shape = [jax.ShapeDtypeStruct(slots + src.shape[1:], BF16) for src, _, slots in ins]
        self.aliases = {len(ins) + j: i for j, i in enumerate(old)}
        n_pair, n_far = 4 * len(pair_chunks), 3 * len(far_chunks)
        self.sems = [pltpu.SemaphoreType.DMA((max(n_pair, 1),)), pltpu.SemaphoreType.DMA((max(n_pair, 1),)),
                     pltpu.SemaphoreType.DMA((max(n_far, 1),)), pltpu.SemaphoreType.DMA((max(n_far, 1),))]

    def _copies(self, ins, outs, sems):
        copies = []
        for ci, (name, r0, rows) in enumerate(self.pair_chunks):
            w = self.pair_names.index(name)
            for j, k in enumerate(self.TO_SIBLING):
                copies.append(pltpu.make_async_remote_copy(
                    src_ref=ins[w].at[_index(_relative(k)), pl.ds(r0, rows)], dst_ref=outs[w].at[j, pl.ds(r0, rows)],
                    send_sem=sems[0].at[4 * ci + j], recv_sem=sems[1].at[4 * ci + j],
                    device_id=_relative(1), device_id_type=MESH))
        for ci, (name, r0, rows) in enumerate(self.far_chunks):
            w = len(self.pair_names) + self.far_names.index(name)
            for j, k in enumerate(FAR):
                copies.append(pltpu.make_async_remote_copy(
                    src_ref=ins[w].at[j, pl.ds(r0, rows)], dst_ref=outs[w].at[j, pl.ds(r0, rows)],
                    send_sem=sems[2].at[3 * ci + j], recv_sem=sems[3].at[3 * ci + j],
                    device_id=_relative(k), device_id_type=MESH))
        return copies

    def start(self, ins, outs, sems):
        for cp in self._copies(ins, outs, sems):
            cp.start()

    def mid(self, ins, outs, sems):
        pass

    def finish(self, ins, outs, sems):
        for cp in self._copies(ins, outs, sems):
            cp.wait()

    def done(self, results):
        for n, buf in zip(self.pair_names, results):
            self.owner.pairs[n] = buf
        for n, buf in zip(self.far_names, results[len(self.pair_names):]):
            self.owner.fars[n] = buf


class _Joined:
    def __init__(self, sides):
        self.sides = sides
        self.operands, self.out_shape, self.sems, self.aliases, self.spans = [], [], [], {}, []
        for s in sides:
            i0, o0, s0 = len(self.operands), len(self.out_shape), len(self.sems)
            self.operands += list(s.operands)
            self.out_shape += list(s.out_shape)
            self.sems += list(s.sems)
            self.aliases.update({i0 + i: o0 + o for i, o in s.aliases.items()})
            self.spans.append((slice(i0, len(self.operands)), slice(o0, len(self.out_shape)),
                               slice(s0, len(self.sems))))

    def start(self, ins, outs, sems):
        for s, (i, o, m) in zip(self.sides, self.spans):
            s.start(ins[i], outs[o], sems[m])

    def mid(self, ins, outs, sems):
        for s, (i, o, m) in zip(self.sides, self.spans):
            s.mid(ins[i], outs[o], sems[m])

    def finish(self, ins, outs, sems):
        for s, (i, o, m) in zip(self.sides, self.spans):
            s.finish(ins[i], outs[o], sems[m])

    def done(self, results):
        for s, (_, o, _) in zip(self.sides, self.spans):
            s.done(results[o])


def _join(*sides):
    sides = [s for s in sides if s is not None]
    if len(sides) <= 1:
        return sides[0] if sides else None
    return _Joined(sides)


PART_W = 768


def _pack_rows(vecs):
    rows = -(-sum(v.shape[0] for v in vecs) // 8) * 8

    def body(*refs):
        out = refs[-1]
        out[...] = jnp.zeros_like(out)
        r0 = 0
        for v in refs[:-1]:
            k, n = v.shape
            for p in range(-(-n // PART_W)):
                w = min(PART_W, n - PART_W * p)
                out[p, r0:r0 + k, 0:w] = v[:, PART_W * p:PART_W * p + w]
            r0 += k

    vmem = pl.BlockSpec(memory_space=pltpu.VMEM)
    return pl.pallas_call(body, name="pack_small", in_specs=[vmem] * len(vecs), out_specs=vmem,
                          out_shape=jax.ShapeDtypeStruct((N_DEV, rows, PART_W), F32))(*vecs)


def _unpack_rows(packed, shapes):
    def body(packed_ref, *outs):
        r0 = 0
        for o in outs:
            k, n = o.shape
            for p in range(-(-n // PART_W)):
                w = min(PART_W, n - PART_W * p)
                o[:, PART_W * p:PART_W * p + w] = packed_ref[p, r0:r0 + k, 0:w]
            r0 += k

    vmem = pl.BlockSpec(memory_space=pltpu.VMEM)
    return pl.pallas_call(body, name="unpack_small", in_specs=[vmem], out_specs=[vmem] * len(shapes),
                          out_shape=[jax.ShapeDtypeStruct(s, F32) for s in shapes])(packed)


class _PartsToOwners:
    def __init__(self, arrays):
        self.n = len(arrays)
        self.pers = [a.shape[0] // N_DEV for a in arrays]
        self.operands, self.aliases = list(arrays), {}
        self.out_shape = [jax.ShapeDtypeStruct((N_DEV, per) + a.shape[1:], a.dtype) for a, per in zip(arrays, self.pers)]
        self.sems = [pltpu.SemaphoreType.DMA((self.n * (N_DEV - 1),))] * 2

    def _copies(self, ins, outs, sems):
        return [pltpu.make_async_remote_copy(
            src_ref=ins[j].at[pl.ds(self.pers[j] * _index(_relative(k)), self.pers[j])], dst_ref=outs[j].at[k],
            send_sem=sems[0].at[self.n * (k - 1) + j], recv_sem=sems[1].at[self.n * (k - 1) + j],
            device_id=_relative(k), device_id_type=MESH) for k in range(1, N_DEV) for j in range(self.n)]

    def start(self, ins, outs, sems):
        for cp in self._copies(ins, outs, sems):
            cp.start()

    def mid(self, ins, outs, sems):
        pass

    def finish(self, ins, outs, sems):
        for cp in self._copies(ins, outs, sems):
            cp.wait()

    def done(self, results):
        self.stages = list(results)


def _sum_parts(arrays, stages):
    n = len(arrays)
    pers = [a.shape[0] // N_DEV for a in arrays]

    def body(*refs):
        me = _index(_relative(0))
        for j in range(n):
            acc = refs[j][pl.ds(pers[j] * me, pers[j])]
            for k in range(1, N_DEV):
                acc = acc + refs[n + j][k].astype(F32)
            refs[2 * n + j][...] = acc

    vmem = pl.BlockSpec(memory_space=pltpu.VMEM)
    return pl.pallas_call(body, name="sum_small_parts", in_specs=[vmem] * (2 * n), out_specs=[vmem] * n,
                          out_shape=[jax.ShapeDtypeStruct((per,) + a.shape[1:], F32) for a, per in zip(arrays, pers)],
                          compiler_params=pltpu.CompilerParams(vmem_limit_bytes=VMEM_LIMIT))(*arrays, *stages)


class _PartsToAll:
    def __init__(self, parts):
        self.n = len(parts)
        self.pers = [p.shape[0] for p in parts]
        self.operands, self.aliases = list(parts), {}
        self.out_shape = [jax.ShapeDtypeStruct((N_DEV * p.shape[0],) + p.shape[1:], F32) for p in parts]
        self.sems = [pltpu.SemaphoreType.DMA((self.n * (N_DEV - 1),))] * 2 + [pltpu.SemaphoreType.DMA((self.n,))]

    def _rows(self, outs, j, pos):
        return outs[j].at[pl.ds(self.pers[j] * _index(pos), self.pers[j])]

    def _copy(self, ins, outs, sems, k, j, owner):
        return pltpu.make_async_remote_copy(
            src_ref=ins[j], dst_ref=self._rows(outs, j, owner),
            send_sem=sems[0].at[self.n * (k - 1) + j], recv_sem=sems[1].at[self.n * (k - 1) + j],
            device_id=_relative(k), device_id_type=MESH)

    def _own(self, ins, outs, sems, j):
        return pltpu.make_async_copy(ins[j], self._rows(outs, j, _relative(0)), sems[2].at[j])

    def start(self, ins, outs, sems):
        for j in range(self.n):
            self._own(ins, outs, sems, j).start()
            for k in range(1, N_DEV):
                self._copy(ins, outs, sems, k, j, _relative(0)).start()

    def mid(self, ins, outs, sems):
        pass

    def finish(self, ins, outs, sems):
        for j in range(self.n):
            for k in range(1, N_DEV):
                self._copy(ins, outs, sems, k, j, _relative(k)).wait_recv()
                self._copy(ins, outs, sems, k, j, _relative(0)).wait_send()
            self._own(ins, outs, sems, j).wait()

    def done(self, results):
        self.totals = list(results)


class _SmallSync:
    def __init__(self, vec_names, mat_names):
        self.vec_names, self.mat_names = vec_names, mat_names

    def begin(self, loss, grads):
        vecs = [loss] + [grads[n] for n in self.vec_names]
        self.shapes = [v.shape for v in vecs]
        self.own = [_diag_blocks(grads[n]) for n in self.mat_names] + [_pack_rows(vecs)]
        self.to_owners = _PartsToOwners([a.astype(BF16) for a in self.own[:-1]] + self.own[-1:])
        return self.to_owners

    def middle(self):
        self.to_all = _PartsToAll(_sum_parts(self.own, self.to_owners.stages))
        return self.to_all

    def end(self):
        *mats, packed = self.to_all.totals
        sums = _unpack_rows(packed, self.shapes)
        return sums[0], dict(zip(self.vec_names, sums[1:])), dict(zip(self.mat_names, mats))


def _block_diag(w):
    groups = []
    for g in range(N_RNN_GROUPS):
        placed = [jnp.pad(w[4 * g + b], ((RNN_BLOCK_W * b, RNN_BLOCK_W * (3 - b)),) * 2) for b in range(4)]
        groups.append(placed[0] + placed[1] + placed[2] + placed[3])
    return jnp.stack(groups)


def _diag_blocks(wg):
    blocks = []
    for n in range(4 * N_RNN_GROUPS):
        g, at = n // 4, RNN_BLOCK_W * (n % 4)
        blocks.append(wg[g, at:at + RNN_BLOCK_W, at:at + RNN_BLOCK_W])
    return jnp.stack(blocks)


def _heads_major(t, n_heads):
    return t.reshape(S, n_heads, HEAD_DIM).transpose(1, 0, 2)


def _heads_minor(t):
    return t.transpose(1, 0, 2).reshape(S, t.shape[0] * HEAD_DIM)


def _natural_cols(gathered, side=None):
    n, r_dim, c_dim = gathered.shape
    tr = 256

    def body(g_ref, o_ref):
        for d in range(n):
            o_ref[:, d * c_dim:(d + 1) * c_dim] = g_ref[d]

    return _call(
        body,
        name="natural_cols",
        grid=(r_dim // tr,),
        in_specs=[pl.BlockSpec((n, tr, c_dim), lambda i: (0, i, 0))],
        out_specs=pl.BlockSpec((tr, n * c_dim), lambda i: (i, 0)),
        out_shape=jax.ShapeDtypeStruct((r_dim, n * c_dim), gathered.dtype),
        semantics=("parallel",),
        operands=(gathered,),
        side=side,
    )


def _blocks_cols(full):
    r_dim, c_dim = full.shape[0], full.shape[1] // N_DEV
    tr = 256

    def body(x_ref, o_ref):
        for d in range(N_DEV):
            o_ref[d] = x_ref[:, d * c_dim:(d + 1) * c_dim]

    return _call(
        body,
        name="blocks_cols",
        grid=(r_dim // tr,),
        in_specs=[pl.BlockSpec((tr, N_DEV * c_dim), lambda i: (i, 0))],
        out_specs=pl.BlockSpec((N_DEV, tr, c_dim), lambda i: (0, i, 0)),
        out_shape=jax.ShapeDtypeStruct((N_DEV, r_dim, c_dim), full.dtype),
        semantics=("parallel",),
        operands=(full,),
    )


def _natural(gathered, how):
    n, r, c = gathered.shape
    if how == "rows":
        return gathered.reshape(n * r, c)
    return gathered.transpose(1, 0, 2).reshape(r, n * c)


def _blocks(full, how):
    if how == "rows":
        return full.reshape(N_DEV, full.shape[0] // N_DEV, full.shape[1])
    return full.reshape(full.shape[0], N_DEV, full.shape[1] // N_DEV).transpose(1, 0, 2)


def _cast_many(arrays, side=None):
    steps = 4

    def body(*refs):
        n = len(refs) // 2
        for src, dst in zip(refs[:n], refs[n:]):
            dst[...] = src[...].astype(dst.dtype)

    specs = [pl.BlockSpec((a.shape[0] // steps, a.shape[1]), lambda i: (i, 0)) for a in arrays]
    return _call(
        body,
        name="cast_weights",
        grid=(steps,),
        in_specs=specs,
        out_specs=specs,
        out_shape=[jax.ShapeDtypeStruct(a.shape, MXU_DTYPE) for a in arrays],
        semantics=("parallel",),
        operands=tuple(arrays),
        side=side,
    )


def _forward_backward(x2, xb, target, small, gather, scatter, sync):
    w_in = _natural_cols(gather.get("w_in"))
    proj, projb = _mm(xb, w_in, tm=1024, tn=512, tk=D, out_dtype=(F32, MXU_DTYPE), name="proj", side=gather.take(110))

    qt = projb[:, :OFF_K].T.reshape(N_KV, GROUP, HEAD_DIM, S)
    k2, v2 = projb[:, OFF_K:OFF_V], projb[:, OFF_V:OFF_RX]
    kp = jnp.pad(_heads_major(k2, N_KV), ((0, 0), (BLOCK, 0), (0, 0)))
    vp = jnp.pad(_heads_major(v2, N_KV), ((0, 0), (BLOCK, 0), (0, 0)))
    kt = jnp.pad(k2.T.reshape(N_KV, HEAD_DIM, S), ((0, 0), (0, 0), (BLOCK, 0)))
    vt = jnp.pad(v2.T.reshape(N_KV, HEAD_DIM, S), ((0, 0), (0, 0), (BLOCK, 0)))
    sink_row = jnp.repeat(small["attn_sinks"].reshape(N_KV, 1, GROUP), BLOCK, axis=2)
    ot = _attn_fwd(qt, kp, vt, sink_row, side=gather.take(36)).reshape(D, S)

    rconv_w = _natural(gather.get("rnn_conv_w"), "cols")
    rxc = _conv_fwd(proj, OFF_RX, rconv_w, small["rnn_conv_b"], tc=512, name="rnn_conv_fwd", side=gather.take(18))
    r, i = _lru_gates(rxc, small["lru_wa"], small["lru_wi"], small["lru_ba"], small["lru_bi"], side=gather.take(33))
    h, yrin = _lru_scan_fwd(r, i, rxc, proj, small["lru_lambda"], side=gather.take(53))

    w_ap = _natural(gather.get("w_attn_proj"), "rows")
    w_rp = _natural(gather.get("w_rnn_proj"), "rows")
    y_attn = _mm(ot, w_ap, ta=True, tm=1024, tn=1024, tk=D, name="attn_proj", side=gather.take(22))
    y_rnn = _mm(yrin, w_rp, tm=1024, tn=1024, tk=D_RNN, name="rnn_proj", side=gather.take(27))
    mixin = _gate_fwd(y_attn, y_rnn, proj, small["b_gate"], side=gather.take(25))
    w_out = _natural(gather.get("w_out"), "rows")
    mix = _mm(mixin, w_out, tm=1024, tn=1024, tk=D, name="mix_out", side=gather.take(22))
    x1, x1b, xhat1, rstd1 = _ln_fwd(x2, mix, small["ln1_g"], small["ln1_b"], side=gather.take(23))

    w_up = gather.get("ffn_w_up")
    up = _mm(x1b, w_up, tm=1024, tn=768, tk=D, b_block=768, name="ffn_up", side=gather.take(58))
    w_gate = gather.get("ffn_w_gate")
    gpre = _mm(x1b, w_gate, tm=1024, tn=768, tk=D, b_block=768, name="ffn_gate", side=gather.take(58))
    fconv_w = _natural(gather.get("ffn_conv_w"), "cols")
    fin = _ffn_act_fwd(up, gpre, fconv_w, small["ffn_conv_b"], side=gather.take())
    w_down = _natural(gather.get("ffn_w_down"), "rows")
    f = _mm(fin, w_down, tm=1024, tn=1024, tk=2048, name="ffn_down")
    loss, dpre2, dpre2b, d_ln2_g, d_ln2_b = _ln_loss_bwd(x1, f, small["ln2_g"], small["ln2_b"], target)

    grads = {"ln2_g": d_ln2_g, "ln2_b": d_ln2_b}
    both = (F32, BF16)
    g32, g16 = _mm(fin, dpre2b, ta=True, tm=1024, tn=1024, tk=S, out_dtype=both, name="d_ffn_w_down")
    scatter.add_blocks("ffn_w_down", _blocks(g32, "rows"), _blocks(g16, "rows"))
    dfin = _mm(dpre2b, w_down, tb=True, tm=1024, tn=1024, tk=D, name="d_fin", side=scatter.take(57))
    dup, dgpre, grads["ffn_conv_w"], grads["ffn_conv_b"] = _ffn_act_bwd(
        dfin, up, gpre, fconv_w, small["ffn_conv_b"], side=scatter.take(85))
    g32, g16 = _mm(x1b, dup, ta=True, tm=1024, tn=768, tk=S, out_dtype=both, out_block=768, name="d_ffn_w_up",
                   side=scatter.take(57))
    scatter.add_blocks("ffn_w_up", g32, g16)
    g32, g16 = _mm(x1b, dgpre, ta=True, tm=1024, tn=768, tk=S, out_dtype=both, out_block=768, name="d_ffn_w_gate",
                   side=scatter.take(56))
    scatter.add_blocks("ffn_w_gate", g32, g16)
    dx1 = _mm(dup, w_up, tb=True, tm=1024, tn=1024, tk=768, b_block=768, name="d_x1_up", side=scatter.take(68))
    dx1 = _mm(dgpre, w_gate, tb=True, tm=1024, tn=1024, tk=768, b_block=768, add=dx1, name="d_x1_gate",
              side=scatter.take(70))
    dpre1, dpre1b, grads["ln1_g"], grads["ln1_b"] = _ln_bwd(dx1, dpre2, xhat1, rstd1, small["ln1_g"],
                                                            side=scatter.take(24))

    g32, g16 = _mm(mixin, dpre1b, ta=True, tm=1024, tn=1024, tk=S, out_dtype=both, name="d_w_out",
                   side=scatter.take(26))
    scatter.add_blocks("w_out", _blocks(g32, "rows"), _blocks(g16, "rows"))
    dmix = _mm(dpre1b, w_out, tb=True, tm=1024, tn=1024, tk=D, name="d_mixin", side=scatter.take(22))
    dya, dyr, dgl_a, dgl_r, db_a, db_r = _gate_bwd(dmix, y_attn, y_rnn, proj, small["b_gate"], side=scatter.take(36))
    grads["b_gate"] = jnp.concatenate([db_a, db_r], axis=1)
    g32, g16 = _mm(ot, dya, tm=1024, tn=1024, tk=S, out_dtype=both, name="d_w_attn_proj", side=scatter.take(38))
    scatter.add_blocks("w_attn_proj", _blocks(g32, "rows"), _blocks(g16, "rows"))
    g32, g16 = _mm(yrin, dyr, ta=True, tm=1280, tn=1024, tk=S, out_dtype=both, name="d_w_rnn_proj",
                   side=scatter.take(27))
    scatter.add_blocks("w_rnn_proj", _blocks(g32, "rows"), _blocks(g16, "rows"))
    dot_ = _mm(w_ap, dya, tb=True, tm=1024, tn=1024, tk=D, out_dtype=MXU_DTYPE, name="d_o", side=scatter.take(22))
    dyrin = _mm(dyr, w_rp, tb=True, tm=1024, tn=1280, tk=D, name="d_yrin", side=scatter.take(27))

    dry, dzr, dzi, drxc_in, grads["lru_ba"], grads["lru_bi"], grads["lru_lambda"] = _lru_scan_bwd(
        dyrin, proj, h, r, i, rxc, small["lru_lambda"], side=scatter.take(94))
    grads["lru_wa"], grads["lru_wi"] = _lru_gate_wgrad(rxc, dzr, dzi, side=scatter.take(22))
    drxc = _lru_gate_xgrad(dzr, dzi, small["lru_wa"], small["lru_wi"], drxc_in, side=scatter.take(33))
    drx, grads["rnn_conv_w"], grads["rnn_conv_b"] = _conv_bwd(drxc, proj, OFF_RX, rconv_w, tc=512,
                                                             name="rnn_conv_bwd", side=scatter.take(29))

    dqt, dk, dv, dsink = _attn_bwd(qt, kp, kt, vp, sink_row, dot_.reshape(N_KV, GROUP, HEAD_DIM, S),
                                   side=scatter.take(65))
    grads["attn_sinks"] = dsink.reshape(1, N_KV * GROUP)
    dproj = jnp.concatenate([
        dqt.reshape(D, S).T,
        _heads_minor(dk[:, BLOCK:, :]).astype(MXU_DTYPE),
        _heads_minor(dv[:, BLOCK:, :]).astype(MXU_DTYPE),
        drx, dry, dgl_a, dgl_r], axis=1)
    for part in range(W_IN_PARTS):
        rows = slice(part * (D // W_IN_PARTS), (part + 1) * (D // W_IN_PARTS))
        side = _join(scatter.take(55), sync.begin(loss, grads)) if part == 0 else scatter.take(68)
        g32, g16 = _mm(xb[:, rows], dproj, ta=True, tm=D // W_IN_PARTS, tn=512, tk=S, out_dtype=both,
                       name="d_w_in_%d" % part, side=side)
        scatter.add_cols("w_in_%d" % part, g32, g16)
        scatter.flush_pairs("pairs_w_in_%d" % part)
    dx = _mm(dproj, w_in, tb=True, tm=1024, tn=1024, tk=512, add=dpre1, add_scale=ALPHA, name="d_x",
             side=_join(scatter.take(400), sync.middle()))
    return dx


SHARDED = (
    ("w_in", "cols", 128), ("w_attn_proj", "rows", 32), ("w_rnn_proj", "rows", 32), ("w_out", "rows", 32),
    ("ffn_w_up", "cols", 128), ("ffn_w_gate", "cols", 128), ("ffn_w_down", "rows", 64),
)
SMALL_REPLICATED = ("b_gate", "rnn_conv_b", "lru_wa", "lru_ba", "lru_wi", "lru_bi", "lru_lambda", "attn_sinks",
                    "ln1_g", "ln1_b", "ffn_conv_b", "ln2_g", "ln2_b")
SMALL_SHARDED = ("rnn_conv_w", "ffn_conv_w")
SMALL_MATS = ("lru_wa", "lru_wi")
W_IN_PARTS = 2
WEIGHTS = ("w_in", "b_gate", "rnn_conv_w", "rnn_conv_b", "lru_wa", "lru_ba", "lru_wi", "lru_bi", "lru_lambda",
           "attn_sinks", "w_attn_proj", "w_rnn_proj", "w_out", "ln1_g", "ln1_b", "ffn_w_up", "ffn_w_gate",
           "ffn_conv_w", "ffn_conv_b", "ffn_w_down", "ln2_g", "ln2_b")


def kernel(x, w_in, b_gate, rnn_conv_w, rnn_conv_b, lru_wa, lru_ba, lru_wi, lru_bi, lru_lambda, attn_sinks, w_attn_proj, w_rnn_proj, w_out, ln1_g, ln1_b, ffn_w_up, ffn_w_gate, ffn_conv_w, ffn_conv_b, ffn_w_down, ln2_g, ln2_b, loss_target, m_w_in, m_b_gate, m_rnn_conv_w, m_rnn_conv_b, m_lru_wa, m_lru_ba, m_lru_wi, m_lru_bi, m_lru_lambda, m_attn_sinks, m_w_attn_proj, m_w_rnn_proj, m_w_out, m_ln1_g, m_ln1_b, m_ffn_w_up, m_ffn_w_gate, m_ffn_conv_w, m_ffn_conv_b, m_ffn_w_down, m_ln2_g, m_ln2_b, v_w_in, v_b_gate, v_rnn_conv_w, v_rnn_conv_b, v_lru_wa, v_lru_ba, v_lru_wi, v_lru_bi, v_lru_lambda, v_attn_sinks, v_w_attn_proj, v_w_rnn_proj, v_w_out, v_ln1_g, v_ln1_b, v_ffn_w_up, v_ffn_w_gate, v_ffn_conv_w, v_ffn_conv_b, v_ffn_w_down, v_ln2_g, v_ln2_b):
    given = dict(locals())
    wsh = {n: given[n][0] for n in WEIGHTS}
    msh = {n: given["m_" + n][0] for n in WEIGHTS}
    vsh = {n: given["v_" + n][0] for n in WEIGHTS}
    m_given = {n: given["m_" + n] for n in WEIGHTS}
    v_given = {n: given["v_" + n] for n in WEIGHTS}
    me = 4 * lax.axis_index("x") + 2 * lax.axis_index("y") + lax.axis_index("c")

    order = ("w_in", "rnn_conv_w", "ffn_conv_w", "w_attn_proj", "w_rnn_proj", "w_out", "ffn_w_up", "ffn_w_gate",
             "ffn_w_down")
    gather = _Gather({n: wsh[n] if n in SMALL_SHARDED else wsh[n].astype(MXU_DTYPE) for n in order[:3]})
    *casts, xb = _cast_many([wsh[n] for n in order[3:]] + [x[0]], side=gather.take(through="ffn_conv_w"))
    gather.add_shards(dict(zip(order[3:], casts)))
    small = {n: given[n] for n in SMALL_REPLICATED}
    small["lru_wa"] = _block_diag(wsh["lru_wa"])
    small["lru_wi"] = _block_diag(wsh["lru_wi"])
    scatter = _Scatter(me, jnp.stack([_index(_relative(k)) for k in FAR]).astype(jnp.int32))

    vec_names = tuple(n for n in SMALL_REPLICATED if n not in SMALL_MATS) + SMALL_SHARDED
    sync = _SmallSync(vec_names, SMALL_MATS)
    dx = _forward_backward(x[0], xb, loss_target[0], small, gather, scatter, sync)

    loss_total, g_small, mat_sums = sync.end()
    loss_total = loss_total.reshape(())
    for n in SMALL_SHARDED:
        width = wsh[n].shape[1]
        g_small[n] = lax.dynamic_slice_in_dim(g_small[n], me * width, width, axis=1)
    g_small = {n: g_small[n].reshape(given[n].shape) for n in vec_names}
    out = {}
    results = _adamw_many(*[[d[n] for n in vec_names] for d in (given, m_given, v_given, g_small)])
    for n, delta, nm, nv in zip(vec_names, *results):
        out[n] = (g_small[n], delta, nm, nv)
    for n in SMALL_MATS:
        g = mat_sums[n].reshape(given[n].shape)
        out[n] = (g, *_adamw_blocks(given[n], m_given[n], v_given[n], g, name="adamw_" + n))

    tile_rows = {n: tr for n, _, tr in SHARDED}
    me1 = me.reshape(1).astype(jnp.int32)
    res = None
    for n in list(scatter.sends):
        own, pair, far = scatter.get(n)
        if n.startswith("w_in_"):
            row0 = int(n[len("w_in_"):]) * (D // W_IN_PARTS)
            res = _reduce_adamw(wsh["w_in"], msh["w_in"], vsh["w_in"], own, pair, far, me1, tr=tile_rows["w_in"],
                                name="adamw_" + n, row0=row0, earlier=res if row0 else None)
            out["w_in"] = tuple(r[None] for r in res)
        else:
            res_n = _reduce_adamw(wsh[n], msh[n], vsh[n], own, pair, far, me1, tr=tile_rows[n], name="adamw_" + n)
            out[n] = tuple(r[None] for r in res_n)

    outputs = [loss_total, dx[None]]
    for kind in range(4):
        outputs += [out[n][kind] for n in WEIGHTS]
    return tuple(outputs)
```

```python
import math

import jax
import jax.numpy as jnp
from jax import lax
from jax.experimental import pallas as pl
from jax.experimental.pallas import tpu as pltpu

F32 = jnp.float32
BF16 = jnp.bfloat16
MXU_DTYPE = jnp.bfloat16

N_DEV = 8
S = 2048
D = 2048
HEAD_DIM = 64
N_KV = 4
GROUP = 8
BLOCK = 128
D_KV = N_KV * HEAD_DIM
D_RNN = 2560
RNN_GROUP = 640
N_RNN_GROUPS = D_RNN // RNN_GROUP
RNN_BLOCK_W = 160
RNN_CONV_W = 4
LRU_C = 8.0
D_FF = 6144
FFN_CONV_W = 3
D_IN = 11776
OFF_K = 2048
OFF_V = 2304
OFF_RX = 2560
OFF_RY = 5120
OFF_GA = 7680
OFF_GR = 9728
LN_EPS = 1e-5
ALPHA = 2.0 ** 0.25
ADAM_LR = 0.001
ADAM_B1 = 0.9
ADAM_B2 = 0.999
ADAM_EPS = 1e-08
ADAM_WD = 0.01
ADAM_STEP = 10
NEG = -1e30
VMEM_LIMIT = 56 * 1024 * 1024
MID_RIDE_TENTHS = 6
MESH = pl.DeviceIdType.MESH
GELU_C = math.sqrt(2.0 / math.pi)


def _cparams(*sem):
    return pltpu.CompilerParams(dimension_semantics=sem or None, vmem_limit_bytes=VMEM_LIMIT)


def _call(body, *, name, grid, in_specs, out_specs, out_shape, operands, semantics, scratch_shapes=(), side=None):
    single = not isinstance(out_shape, (list, tuple))
    out_shape = [out_shape] if single else list(out_shape)
    out_specs = [out_specs] if single else list(out_specs)
    in_specs = list(in_specs)
    scratch_shapes = list(scratch_shapes)
    if side is None:
        res = pl.pallas_call(
            body, name=name, grid=grid, in_specs=in_specs, out_specs=out_specs, out_shape=out_shape,
            scratch_shapes=scratch_shapes, compiler_params=_cparams(*semantics))(*operands)
        return res[0] if single else res
    n_in, n_out, n_scr = len(in_specs), len(out_shape), len(scratch_shapes)
    s_in, s_out = len(side.operands), len(side.out_shape)
    hbm = pl.BlockSpec(memory_space=pltpu.HBM)
    steps = math.prod(grid)
    mid_step = (steps * MID_RIDE_TENTHS) // 10

    def with_copies(*refs):
        core_in, side_in = refs[:n_in], refs[n_in:n_in + s_in]
        o0 = n_in + s_in
        core_out, side_out = refs[o0:o0 + n_out], refs[o0 + n_out:o0 + n_out + s_out]
        c0 = o0 + n_out + s_out
        core_scr, sems = refs[c0:c0 + n_scr], refs[c0 + n_scr:]
        step = 0
        for d, size in enumerate(grid):
            step = step * size + pl.program_id(d)

        @pl.when(step == 0)
        def _():
            side.start(side_in, side_out, sems)

        body(*core_in, *core_out, *core_scr)

        @pl.when(step == mid_step)
        def _():
            side.mid(side_in, side_out, sems)

        @pl.when(step == steps - 1)
        def _():
            side.finish(side_in, side_out, sems)

    res = pl.pallas_call(
        with_copies, name=name, grid=grid,
        in_specs=in_specs + [hbm] * s_in, out_specs=out_specs + [hbm] * s_out,
        out_shape=out_shape + list(side.out_shape),
        scratch_shapes=scratch_shapes + list(side.sems),
        input_output_aliases={n_in + i: n_out + o for i, o in side.aliases.items()},
        compiler_params=_cparams(*(("arbitrary",) * len(grid))))(*operands, *side.operands)
    side.done(res[n_out:])
    return res[0] if single else res[:n_out]


def _run_side(side, name):
    def body(*refs):
        s_in, s_out = len(side.operands), len(side.out_shape)
        side.start(refs[:s_in], refs[s_in:s_in + s_out], refs[s_in + s_out:])
        side.mid(refs[:s_in], refs[s_in:s_in + s_out], refs[s_in + s_out:])
        side.finish(refs[:s_in], refs[s_in:s_in + s_out], refs[s_in + s_out:])

    hbm = pl.BlockSpec(memory_space=pltpu.HBM)
    res = pl.pallas_call(
        body, name=name, in_specs=[hbm] * len(side.operands), out_specs=[hbm] * len(side.out_shape),
        out_shape=list(side.out_shape), scratch_shapes=list(side.sems),
        input_output_aliases=dict(side.aliases))(*side.operands)
    side.done(res)


def _gelu(x):
    x2 = x * x
    t = jnp.tanh(GELU_C * (x + 0.044715 * x * x2))
    g = 0.5 * x * (1.0 + t)
    dg = 0.5 * (1.0 + t) + 0.5 * x * (1.0 - t * t) * (GELU_C * (1.0 + 3.0 * 0.044715 * x2))
    return g, dg


def _sigmoid(x):
    return 1.0 / (1.0 + jnp.exp(-x))


def _softplus(x):
    z = jnp.exp(-jnp.abs(x))
    small = z * (1.0 - z * (0.5 - z * (1.0 / 3.0 - 0.25 * z)))
    return jnp.maximum(x, 0.0) + jnp.where(z < 0.02, small, jnp.log(1.0 + z))


def _one_minus_exp(x):
    series = -x * (1.0 + x * (0.5 + x * (1.0 / 6.0 + x * (1.0 / 24.0))))
    return jnp.where(x > -0.03, series, 1.0 - jnp.exp(x))


def _colsum(v):
    return jnp.sum(v, axis=0, keepdims=True)


def _mm(a, b, *, tm, tn, tk, name, ta=False, tb=False, out_dtype=F32, b_block=None, out_block=None, add=None,
        add_scale=1.0, side=None):
    out_dtypes = out_dtype if isinstance(out_dtype, tuple) else (out_dtype,)
    if ta:
        k_dim, m_dim = a.shape
    else:
        m_dim, k_dim = a.shape
    if b_block is None:
        n_dim = b.shape[0] if tb else b.shape[1]
    else:
        n_dim = b.shape[1] if tb else b.shape[0] * b_block
    assert m_dim % tm == 0 and n_dim % tn == 0 and k_dim % tk == 0, (name, m_dim, n_dim, k_dim)
    nk = k_dim // tk
    dims = (((0 if ta else 1,), (1 if tb else 0,)), ((), ()))
    has_add = add is not None

    def body(*refs):
        a_ref, b_ref = refs[0], refs[1]
        add_ref = refs[2] if has_add else None
        first_out = 3 if has_add else 2
        o_refs = refs[first_out:first_out + len(out_dtypes)]

        def product():
            return lax.dot_general(a_ref[...].astype(MXU_DTYPE), b_ref[...].astype(MXU_DTYPE), dims,
                                   preferred_element_type=F32)

        def finish(acc):
            if has_add:
                acc = acc + add_scale * add_ref[...]
            for o_ref in o_refs:
                o_ref[...] = acc.astype(o_ref.dtype)

        if nk == 1:
            finish(product())
        else:
            acc_ref = refs[-1]
            k = pl.program_id(2)

            @pl.when(k == 0)
            def _():
                acc_ref[...] = jnp.zeros_like(acc_ref)

            acc_ref[...] += product()

            @pl.when(k == nk - 1)
            def _():
                finish(acc_ref[...])

    if ta:
        a_spec = pl.BlockSpec((tk, tm), lambda i, j, k: (k, i))
    else:
        a_spec = pl.BlockSpec((tm, tk), lambda i, j, k: (i, k))
    if b_block is None:
        if tb:
            b_spec = pl.BlockSpec((tn, tk), lambda i, j, k: (j, k))
        else:
            b_spec = pl.BlockSpec((tk, tn), lambda i, j, k: (k, j))
    elif tb:
        assert b_block % tk == 0
        b_spec = pl.BlockSpec((None, tn, tk), lambda i, j, k: ((k * tk) // b_block, j, ((k * tk) % b_block) // tk))
    else:
        assert b_block % tn == 0
        b_spec = pl.BlockSpec((None, tk, tn), lambda i, j, k: ((j * tn) // b_block, k, ((j * tn) % b_block) // tn))
    in_specs = [a_spec, b_spec]
    operands = [a, b]
    if has_add:
        in_specs.append(pl.BlockSpec((tm, tn), lambda i, j, k: (i, j)))
        operands.append(add)
    if out_block is None:
        out_spec = pl.BlockSpec((tm, tn), lambda i, j, k: (i, j))
        out_dims = (m_dim, n_dim)
    else:
        assert out_block % tn == 0
        out_spec = pl.BlockSpec((None, tm, tn), lambda i, j, k: ((j * tn) // out_block, i, ((j * tn) % out_block) // tn))
        out_dims = (n_dim // out_block, m_dim, out_block)
    res = _call(
        body,
        name=name,
        grid=(m_dim // tm, n_dim // tn, nk),
        in_specs=in_specs,
        out_specs=[out_spec] * len(out_dtypes),
        out_shape=[jax.ShapeDtypeStruct(out_dims, dt) for dt in out_dtypes],
        scratch_shapes=[pltpu.VMEM((tm, tn), F32)] if nk > 1 else [],
        semantics=("parallel", "parallel", "arbitrary"),
        operands=tuple(operands),
        side=side,
    )
    return res if isinstance(out_dtype, tuple) else res[0]


def _attn_bias(bias_ref, h):
    key = lax.broadcasted_iota(jnp.int32, (2 * BLOCK, GROUP * BLOCK), 0)
    col = lax.broadcasted_iota(jnp.int32, (2 * BLOCK, GROUP * BLOCK), 1)
    dist = BLOCK + (col & (BLOCK - 1)) - key
    head = h * GROUP + (col >> 7) + 1
    slope = jnp.exp(head.astype(F32) * (-0.25 * math.log(2.0)))
    bias = jnp.where((dist >= 0) & (dist < BLOCK), -slope * dist.astype(F32), NEG)
    bias_ref[1] = bias
    bias_ref[0] = jnp.where(key < BLOCK, NEG, bias)


def _attn_probs(kb, qt, bias, sink):
    s = jnp.dot(kb, qt, preferred_element_type=F32) * (HEAD_DIM ** -0.5) + bias
    m = jnp.maximum(jnp.max(s, axis=0, keepdims=True), sink)
    e = jnp.exp(s - m)
    e_sink = jnp.exp(sink - m)
    inv = 1.0 / (jnp.sum(e, axis=0, keepdims=True) + e_sink)
    return e * inv, e_sink * inv


def _heads_on_lanes(ref, r0):
    return jnp.concatenate([ref[g, :, pl.ds(r0, BLOCK)] for g in range(GROUP)], axis=1)


def _attn_fwd(qt, kp, vt, sink_row, side=None):
    cols = GROUP * BLOCK

    def body(q_ref, k_ref, vt_ref, sink_ref, o_ref, bias_ref):
        _attn_bias(bias_ref, pl.program_id(0))
        sink = sink_ref[...]

        def step(n, carry):
            r0 = pl.multiple_of(n * BLOCK, BLOCK)
            p, _ = _attn_probs(k_ref[pl.ds(r0, 2 * BLOCK), :], _heads_on_lanes(q_ref, r0),
                               bias_ref[jnp.minimum(n, 1)], sink)
            o = jnp.dot(vt_ref[:, pl.ds(r0, 2 * BLOCK)], p.astype(MXU_DTYPE), preferred_element_type=F32)
            for g in range(GROUP):
                o_ref[g, :, pl.ds(r0, BLOCK)] = o[:, g * BLOCK:(g + 1) * BLOCK].astype(o_ref.dtype)
            return carry

        lax.fori_loop(0, S // BLOCK, step, 0)

    hm = pl.BlockSpec((None, GROUP, HEAD_DIM, S), lambda h: (h, 0, 0, 0))
    return _call(
        body,
        name="attn_fwd",
        grid=(N_KV,),
        in_specs=[
            hm,
            pl.BlockSpec((None, BLOCK + S, HEAD_DIM), lambda h: (h, 0, 0)),
            pl.BlockSpec((None, HEAD_DIM, BLOCK + S), lambda h: (h, 0, 0)),
            pl.BlockSpec((None, 1, cols), lambda h: (h, 0, 0)),
        ],
        out_specs=hm,
        out_shape=jax.ShapeDtypeStruct((N_KV, GROUP, HEAD_DIM, S), MXU_DTYPE),
        scratch_shapes=[pltpu.VMEM((2, 2 * BLOCK, cols), F32)],
        semantics=("parallel",),
        operands=(qt, kp, vt, sink_row),
        side=side,
    )


def _attn_bwd(qt, kp, kt, vp, sink_row, dot_, side=None):
    cols = GROUP * BLOCK

    def body(q_ref, k_ref, kt_ref, v_ref, sink_ref, do_ref, dq_ref, dk_ref, dv_ref, dsink_ref, bias_ref):
        _attn_bias(bias_ref, pl.program_id(0))
        sink = sink_ref[...]
        dk_ref[...] = jnp.zeros_like(dk_ref)
        dv_ref[...] = jnp.zeros_like(dv_ref)
        nt = (((1,), (1,)), ((), ()))

        def step(n, sink_acc):
            r0 = pl.multiple_of(n * BLOCK, BLOCK)
            band = pl.ds(r0, 2 * BLOCK)
            qn = _heads_on_lanes(q_ref, r0)
            don = _heads_on_lanes(do_ref, r0)
            p, p_sink = _attn_probs(k_ref[band, :], qn, bias_ref[jnp.minimum(n, 1)], sink)
            dp = jnp.dot(v_ref[band, :], don, preferred_element_type=F32)
            delta = jnp.sum(p * dp, axis=0, keepdims=True)
            ds = (p * (dp - delta) * (HEAD_DIM ** -0.5)).astype(MXU_DTYPE)
            dq = jnp.dot(kt_ref[:, band], ds, preferred_element_type=F32)
            for g in range(GROUP):
                dq_ref[g, :, pl.ds(r0, BLOCK)] = dq[:, g * BLOCK:(g + 1) * BLOCK].astype(dq_ref.dtype)
            dk_ref[band, :] += lax.dot_general(ds, qn, nt, preferred_element_type=F32)
            dv_ref[band, :] += lax.dot_general(p.astype(MXU_DTYPE), don, nt, preferred_element_type=F32)
            return sink_acc - p_sink * delta

        sink_acc = lax.fori_loop(0, S // BLOCK, step, jnp.zeros((1, cols), F32))
        for g in range(GROUP):
            dsink_ref[g:g + 1, :] = jnp.sum(sink_acc[:, g * BLOCK:(g + 1) * BLOCK], axis=1, keepdims=True)

    hm = pl.BlockSpec((None, GROUP, HEAD_DIM, S), lambda h: (h, 0, 0, 0))
    kv = pl.BlockSpec((None, BLOCK + S, HEAD_DIM), lambda h: (h, 0, 0))
    return _call(
        body,
        name="attn_bwd",
        grid=(N_KV,),
        in_specs=[hm, kv, pl.BlockSpec((None, HEAD_DIM, BLOCK + S), lambda h: (h, 0, 0)), kv,
                  pl.BlockSpec((None, 1, cols), lambda h: (h, 0, 0)), hm],
        out_specs=[hm, kv, kv, pl.BlockSpec((None, GROUP, 1), lambda h: (h, 0, 0))],
        out_shape=[
            jax.ShapeDtypeStruct((N_KV, GROUP, HEAD_DIM, S), MXU_DTYPE),
            jax.ShapeDtypeStruct((N_KV, BLOCK + S, HEAD_DIM), F32),
            jax.ShapeDtypeStruct((N_KV, BLOCK + S, HEAD_DIM), F32),
            jax.ShapeDtypeStruct((N_KV, GROUP, 1), F32),
        ],
        scratch_shapes=[pltpu.VMEM((2, 2 * BLOCK, cols), F32)],
        semantics=("parallel",),
        operands=(qt, kp, kt, vp, sink_row, dot_),
        side=side,
    )


PAD = 8
CHUNK = 256


def _past_taps(xpad_ref, r0, width):
    ext = xpad_ref[pl.ds(r0, CHUNK + PAD), :]
    taps = []
    for k in range(width):
        back = width - 1 - k
        taps.append((ext if back == 0 else pltpu.roll(ext, back, 0))[PAD:, :])
    return taps


def _future_taps(xpad_ref, r0, width):
    ext = xpad_ref[pl.ds(r0, CHUNK + PAD), :]
    taps = []
    for ahead in range(width):
        taps.append((ext if ahead == 0 else pltpu.roll(ext, CHUNK + PAD - ahead, 0))[:CHUNK, :])
    return taps


def _conv_fwd(src, col0, w, b, *, tc, name, side=None):
    width, c_dim = w.shape

    def body(x_ref, w_ref, b_ref, o_ref, xpad_ref):
        xpad_ref[pl.ds(0, PAD), :] = jnp.zeros((PAD, tc), F32)
        xpad_ref[pl.ds(PAD, S), :] = x_ref[...]
        wv = w_ref[...]
        bv = b_ref[...]

        def step(ci, carry):
            r0 = pl.multiple_of(ci * CHUNK, CHUNK)
            taps = _past_taps(xpad_ref, r0, width)
            y = bv + taps[0] * wv[0:1, :]
            for k in range(1, width):
                y = y + taps[k] * wv[k:k + 1, :]
            o_ref[pl.ds(r0, CHUNK), :] = y
            return carry

        lax.fori_loop(0, S // CHUNK, step, 0)

    return _call(
        body,
        name=name,
        grid=(c_dim // tc,),
        in_specs=[
            pl.BlockSpec((S, tc), lambda j: (0, col0 // tc + j)),
            pl.BlockSpec((width, tc), lambda j: (0, j)),
            pl.BlockSpec((1, tc), lambda j: (0, j)),
        ],
        out_specs=pl.BlockSpec((S, tc), lambda j: (0, j)),
        out_shape=jax.ShapeDtypeStruct((S, c_dim), F32),
        scratch_shapes=[pltpu.VMEM((S + PAD, tc), F32)],
        semantics=("parallel",),
        operands=(src, w, b),
        side=side,
    )


def _conv_bwd(dy, src, col0, w, *, tc, name, side=None):
    width, c_dim = w.shape

    def body(dy_ref, x_ref, w_ref, dx_ref, dw_ref, db_ref, xpad_ref, dpad_ref):
        xpad_ref[pl.ds(0, PAD), :] = jnp.zeros((PAD, tc), F32)
        xpad_ref[pl.ds(PAD, S), :] = x_ref[...]
        dpad_ref[pl.ds(0, S), :] = dy_ref[...]
        dpad_ref[pl.ds(S, PAD), :] = jnp.zeros((PAD, tc), F32)
        wv = w_ref[...]

        def step(ci, acc):
            r0 = pl.multiple_of(ci * CHUNK, CHUNK)
            past = _past_taps(xpad_ref, r0, width)
            ahead = _future_taps(dpad_ref, r0, width)
            d = ahead[0]
            dx = d * wv[width - 1:width, :]
            for j in range(1, width):
                dx = dx + ahead[j] * wv[width - 1 - j:width - j, :]
            dx_ref[pl.ds(r0, CHUNK), :] = dx.astype(dx_ref.dtype)
            return tuple(acc[k] + _colsum(past[k] * d) for k in range(width)) + (acc[width] + _colsum(d),)

        zero = jnp.zeros((1, tc), F32)
        acc = lax.fori_loop(0, S // CHUNK, step, (zero,) * (width + 1))
        for k in range(width):
            dw_ref[k:k + 1, :] = acc[k]
        db_ref[...] = acc[width]

    return _call(
        body,
        name=name,
        grid=(c_dim // tc,),
        in_specs=[
            pl.BlockSpec((S, tc), lambda j: (0, j)),
            pl.BlockSpec((S, tc), lambda j: (0, col0 // tc + j)),
            pl.BlockSpec((width, tc), lambda j: (0, j)),
        ],
        out_specs=[
            pl.BlockSpec((S, tc), lambda j: (0, j)),
            pl.BlockSpec((width, tc), lambda j: (0, j)),
            pl.BlockSpec((1, tc), lambda j: (0, j)),
        ],
        out_shape=[
            jax.ShapeDtypeStruct((S, c_dim), MXU_DTYPE),
            jax.ShapeDtypeStruct((width, c_dim), F32),
            jax.ShapeDtypeStruct((1, c_dim), F32),
        ],
        scratch_shapes=[pltpu.VMEM((S + PAD, tc), F32), pltpu.VMEM((S + PAD, tc), F32)],
        semantics=("parallel",),
        operands=(dy, src, w),
        side=side,
    )


SCAN_TC = 256


def _lru_gates(rxc, wa, wi, ba, bi, side=None):
    tm = 512

    def body(x_ref, wa_ref, wi_ref, ba_ref, bi_ref, r_ref, i_ref):
        xv = x_ref[...].astype(MXU_DTYPE)
        r_ref[...] = _sigmoid(jnp.dot(xv, wa_ref[...].astype(MXU_DTYPE), preferred_element_type=F32) + ba_ref[...])
        i_ref[...] = _sigmoid(jnp.dot(xv, wi_ref[...].astype(MXU_DTYPE), preferred_element_type=F32) + bi_ref[...])

    x_spec = pl.BlockSpec((tm, RNN_GROUP), lambda g, i: (i, g))
    w_spec = pl.BlockSpec((None, RNN_GROUP, RNN_GROUP), lambda g, i: (g, 0, 0))
    b_spec = pl.BlockSpec((1, RNN_GROUP), lambda g, i: (0, g))
    return _call(
        body,
        name="lru_gates",
        grid=(N_RNN_GROUPS, S // tm),
        in_specs=[x_spec, w_spec, w_spec, b_spec, b_spec],
        out_specs=[x_spec, x_spec],
        out_shape=[jax.ShapeDtypeStruct((S, D_RNN), F32)] * 2,
        semantics=("parallel", "parallel"),
        operands=(rxc, wa, wi, ba, bi),
        side=side,
    )


def _scan_down(a, u, row):
    for d in (1, 2, 4):
        a_s = jnp.where(row >= d, pltpu.roll(a, d, 0), 1.0)
        u_s = jnp.where(row >= d, pltpu.roll(u, d, 0), 0.0)
        u = a * u_s + u
        a = a * a_s
    return a, u


def _scan_up(a, u, row):
    for d in (1, 2, 4):
        a_s = jnp.where(row < 8 - d, pltpu.roll(a, 8 - d, 0), 1.0)
        u_s = jnp.where(row < 8 - d, pltpu.roll(u, 8 - d, 0), 0.0)
        u = a * u_s + u
        a = a * a_s
    return a, u


def _lru_scan_fwd(r, i, rxc, proj, lam, side=None):
    tc = SCAN_TC

    def body(r_ref, i_ref, x_ref, ry_ref, lam_ref, h_ref, y_ref):
        rate = LRU_C * _softplus(-lam_ref[...])
        row = lax.broadcasted_iota(jnp.int32, (8, tc), 0)

        def step(ci, carry):
            r0 = pl.multiple_of(ci * 16, 16)
            log_a = -rate * r_ref[pl.ds(r0, 16), :]
            a16 = jnp.exp(log_a)
            u16 = jnp.sqrt(_one_minus_exp(2.0 * log_a)) * (i_ref[pl.ds(r0, 16), :] * x_ref[pl.ds(r0, 16), :])
            hs = []
            for half in range(2):
                a_cum, h0 = _scan_down(a16[8 * half:8 * half + 8, :], u16[8 * half:8 * half + 8, :], row)
                h = a_cum * carry + h0
                carry = jnp.broadcast_to(h[7:8, :], (8, tc))
                hs.append(h)
            h16 = jnp.concatenate(hs, axis=0)
            h_ref[pl.ds(r0, 16), :] = h16
            y_ref[pl.ds(r0, 16), :] = (h16 * _gelu(ry_ref[pl.ds(r0, 16), :])[0]).astype(y_ref.dtype)
            return carry

        lax.fori_loop(0, S // 16, step, jnp.zeros((8, tc), F32))

    col = pl.BlockSpec((S, tc), lambda j: (0, j))
    return _call(
        body,
        name="lru_scan_fwd",
        grid=(D_RNN // tc,),
        in_specs=[col, col, col, pl.BlockSpec((S, tc), lambda j: (0, OFF_RY // tc + j)),
                  pl.BlockSpec((1, tc), lambda j: (0, j))],
        out_specs=[col, col],
        out_shape=[jax.ShapeDtypeStruct((S, D_RNN), F32), jax.ShapeDtypeStruct((S, D_RNN), MXU_DTYPE)],
        semantics=("parallel",),
        operands=(r, i, rxc, proj, lam),
        side=side,
    )


def _lru_scan_bwd(dy, proj, h, r, i, rxc, lam, side=None):
    tc = SCAN_TC

    def body(dy_ref, ry_ref, h_ref, r_ref, i_ref, x_ref, lam_ref,
             dry_ref, dzr_ref, dzi_ref, dx_ref, dba_ref, dbi_ref, dlam_ref, a_ref, dh_ref, hp_ref):
        lam_v = lam_ref[...]
        rate = LRU_C * _softplus(-lam_v)
        dlam_scale = LRU_C * _sigmoid(-lam_v)
        row = lax.broadcasted_iota(jnp.int32, (8, tc), 0)
        hp_ref[pl.ds(0, PAD), :] = jnp.zeros((PAD, tc), F32)
        hp_ref[pl.ds(PAD, S), :] = h_ref[...]
        a_ref[pl.ds(S, PAD), :] = jnp.zeros((PAD, tc), F32)

        def prep(ci, carry):
            r0 = pl.multiple_of(ci * CHUNK, CHUNK)
            a_ref[pl.ds(r0, CHUNK), :] = jnp.exp(-rate * r_ref[pl.ds(r0, CHUNK), :])
            ge, dge = _gelu(ry_ref[pl.ds(r0, CHUNK), :])
            dyv = dy_ref[pl.ds(r0, CHUNK), :]
            dh_ref[pl.ds(r0, CHUNK), :] = dyv * ge
            dry_ref[pl.ds(r0, CHUNK), :] = (dyv * h_ref[pl.ds(r0, CHUNK), :] * dge).astype(dry_ref.dtype)
            return carry

        lax.fori_loop(0, S // CHUNK, prep, 0)

        def step(ci, state):
            carry, dba, dbi, dlam = state
            r0 = pl.multiple_of(S - 16 - ci * 16, 16)
            a_ext = a_ref[pl.ds(r0, 24), :]
            a_next = pltpu.roll(a_ext, 23, 0)
            h_prev = pltpu.roll(hp_ref[pl.ds(r0, 24), :], 1, 0)
            dh16 = dh_ref[pl.ds(r0, 16), :]
            gs = [None, None]
            for half in (1, 0):
                lo = 8 * half
                c_cum, g0 = _scan_up(a_next[lo:lo + 8, :], dh16[lo:lo + 8, :], row)
                g = c_cum * carry + g0
                carry = jnp.broadcast_to(g[0:1, :], (8, tc))
                gs[half] = g
            g16 = jnp.concatenate(gs, axis=0)
            a16 = a_ext[0:16, :]
            r16 = r_ref[pl.ds(r0, 16), :]
            i16 = i_ref[pl.ds(r0, 16), :]
            x16 = x_ref[pl.ds(r0, 16), :]
            a2 = a16 * a16
            sq = jnp.sqrt(_one_minus_exp(-2.0 * rate * r16))
            dx_ref[pl.ds(r0, 16), :] = g16 * sq * i16
            dzi = g16 * sq * x16 * i16 * (1.0 - i16)
            dlog_a = g16 * h_prev[8:24, :] * a16 - g16 * i16 * x16 * a2 / sq
            dzr = -rate * dlog_a * r16 * (1.0 - r16)
            dzr_ref[pl.ds(r0, 16), :] = dzr.astype(dzr_ref.dtype)
            dzi_ref[pl.ds(r0, 16), :] = dzi.astype(dzi_ref.dtype)
            return carry, dba + _colsum(dzr), dbi + _colsum(dzi), dlam + _colsum(dlog_a * r16)

        zero = jnp.zeros((1, tc), F32)
        _, dba, dbi, dlam = lax.fori_loop(0, S // 16, step, (jnp.zeros((8, tc), F32), zero, zero, zero))
        dba_ref[...] = dba
        dbi_ref[...] = dbi
        dlam_ref[...] = dlam * dlam_scale

    col = pl.BlockSpec((S, tc), lambda j: (0, j))
    vec = pl.BlockSpec((1, tc), lambda j: (0, j))
    return _call(
        body,
        name="lru_scan_bwd",
        grid=(D_RNN // tc,),
        in_specs=[col, pl.BlockSpec((S, tc), lambda j: (0, OFF_RY // tc + j)), col, col, col, col, vec],
        out_specs=[col, col, col, col, vec, vec, vec],
        out_shape=[jax.ShapeDtypeStruct((S, D_RNN), MXU_DTYPE)] * 3 + [jax.ShapeDtypeStruct((S, D_RNN), F32)]
        + [jax.ShapeDtypeStruct((1, D_RNN), F32)] * 3,
        scratch_shapes=[pltpu.VMEM((S + PAD, tc), F32), pltpu.VMEM((S, tc), F32), pltpu.VMEM((S + PAD, tc), F32)],
        semantics=("parallel",),
        operands=(dy, proj, h, r, i, rxc, lam),
        side=side,
    )


def _lru_gate_wgrad(rxc, dzr, dzi, side=None):
    def body(x_ref, dzr_ref, dzi_ref, dwa_ref, dwi_ref):
        xv = x_ref[...].astype(MXU_DTYPE)
        dims = (((0,), (0,)), ((), ()))
        dwa_ref[...] = lax.dot_general(xv, dzr_ref[...], dims, preferred_element_type=F32)
        dwi_ref[...] = lax.dot_general(xv, dzi_ref[...], dims, preferred_element_type=F32)

    col = pl.BlockSpec((S, RNN_GROUP), lambda g: (0, g))
    w_spec = pl.BlockSpec((None, RNN_GROUP, RNN_GROUP), lambda g: (g, 0, 0))
    return _call(
        body,
        name="lru_gate_wgrad",
        grid=(N_RNN_GROUPS,),
        in_specs=[col, col, col],
        out_specs=[w_spec, w_spec],
        out_shape=[jax.ShapeDtypeStruct((N_RNN_GROUPS, RNN_GROUP, RNN_GROUP), F32)] * 2,
        semantics=("parallel",),
        operands=(rxc, dzr, dzi),
        side=side,
    )


def _lru_gate_xgrad(dzr, dzi, wa, wi, dx_in, side=None):
    tm = 512

    def body(dzr_ref, dzi_ref, wa_ref, wi_ref, dx_ref, o_ref):
        dims = (((1,), (1,)), ((), ()))
        o_ref[...] = (dx_ref[...]
                      + lax.dot_general(dzr_ref[...], wa_ref[...].astype(MXU_DTYPE), dims, preferred_element_type=F32)
                      + lax.dot_general(dzi_ref[...], wi_ref[...].astype(MXU_DTYPE), dims, preferred_element_type=F32))

    x_spec = pl.BlockSpec((tm, RNN_GROUP), lambda g, i: (i, g))
    w_spec = pl.BlockSpec((None, RNN_GROUP, RNN_GROUP), lambda g, i: (g, 0, 0))
    return _call(
        body,
        name="lru_gate_xgrad",
        grid=(N_RNN_GROUPS, S // tm),
        in_specs=[x_spec, x_spec, w_spec, w_spec, x_spec],
        out_specs=x_spec,
        out_shape=jax.ShapeDtypeStruct((S, D_RNN), F32),
        semantics=("parallel", "parallel"),
        operands=(dzr, dzi, wa, wi, dx_in),
        side=side,
    )


def _gate_fwd(y_attn, y_rnn, proj, b_gate, side=None):
    t = 512

    def body(ya_ref, yr_ref, ga_ref, gr_ref, ba_ref, br_ref, o_ref):
        o_ref[...] = (_sigmoid(ga_ref[...] + ba_ref[...]) * ya_ref[...]
                      + _sigmoid(gr_ref[...] + br_ref[...]) * yr_ref[...]).astype(o_ref.dtype)

    tile = pl.BlockSpec((t, t), lambda i, j: (i, j))
    return _call(
        body,
        name="gate_fwd",
        grid=(S // t, D // t),
        in_specs=[tile, tile,
                  pl.BlockSpec((t, t), lambda i, j: (i, OFF_GA // t + j)),
                  pl.BlockSpec((t, t), lambda i, j: (i, OFF_GR // t + j)),
                  pl.BlockSpec((1, t), lambda i, j: (0, j)),
                  pl.BlockSpec((1, t), lambda i, j: (0, D // t + j))],
        out_specs=tile,
        out_shape=jax.ShapeDtypeStruct((S, D), MXU_DTYPE),
        semantics=("parallel", "parallel"),
        operands=(y_attn, y_rnn, proj, proj, b_gate, b_gate),
        side=side,
    )


def _gate_bwd(dmix, y_attn, y_rnn, proj, b_gate, side=None):
    t = 512

    def body(dm_ref, ya_ref, yr_ref, ga_ref, gr_ref, ba_ref, br_ref,
             dya_ref, dyr_ref, dga_ref, dgr_ref, dba_ref, dbr_ref):
        @pl.when(pl.program_id(1) == 0)
        def _():
            dba_ref[...] = jnp.zeros_like(dba_ref)
            dbr_ref[...] = jnp.zeros_like(dbr_ref)

        dm = dm_ref[...]
        ga = _sigmoid(ga_ref[...] + ba_ref[...])
        gr = _sigmoid(gr_ref[...] + br_ref[...])
        dya_ref[...] = (dm * ga).astype(dya_ref.dtype)
        dyr_ref[...] = (dm * gr).astype(dyr_ref.dtype)
        dga = dm * ya_ref[...] * ga * (1.0 - ga)
        dgr = dm * yr_ref[...] * gr * (1.0 - gr)
        dga_ref[...] = dga.astype(dga_ref.dtype)
        dgr_ref[...] = dgr.astype(dgr_ref.dtype)
        dba_ref[...] += _colsum(dga)
        dbr_ref[...] += _colsum(dgr)

    tile = pl.BlockSpec((t, t), lambda j, i: (i, j))
    vec = pl.BlockSpec((1, t), lambda j, i: (0, j))
    return _call(
        body,
        name="gate_bwd",
        grid=(D // t, S // t),
        in_specs=[tile, tile, tile,
                  pl.BlockSpec((t, t), lambda j, i: (i, OFF_GA // t + j)),
                  pl.BlockSpec((t, t), lambda j, i: (i, OFF_GR // t + j)),
                  vec,
                  pl.BlockSpec((1, t), lambda j, i: (0, D // t + j))],
        out_specs=[tile, tile, tile, tile, vec, vec],
        out_shape=[jax.ShapeDtypeStruct((S, D), MXU_DTYPE)] * 4 + [jax.ShapeDtypeStruct((1, D), F32)] * 2,
        semantics=("parallel", "arbitrary"),
        operands=(dmix, y_attn, y_rnn, proj, proj, b_gate, b_gate),
        side=side,
    )


LN_TM = 256


def _ln_stats(pre):
    mu = jnp.mean(pre, axis=-1, keepdims=True)
    xc = pre - mu
    rstd = lax.rsqrt(jnp.mean(xc * xc, axis=-1, keepdims=True) + LN_EPS)
    return xc * rstd, rstd


def _ln_input_grad(dy, xhat, rstd, g):
    dyg = dy * g
    return rstd * (dyg - jnp.mean(dyg, axis=-1, keepdims=True)
                   - xhat * jnp.mean(dyg * xhat, axis=-1, keepdims=True))


def _ln_fwd(res, branch, g, b, side=None):
    def body(res_ref, br_ref, g_ref, b_ref, y_ref, yb_ref, xhat_ref, rstd_ref):
        xhat, rstd = _ln_stats(ALPHA * res_ref[...] + br_ref[...])
        y = xhat * g_ref[...] + b_ref[...]
        y_ref[...] = y
        yb_ref[...] = y.astype(yb_ref.dtype)
        xhat_ref[...] = xhat
        rstd_ref[...] = rstd

    tile = pl.BlockSpec((LN_TM, D), lambda i: (i, 0))
    vec = pl.BlockSpec((1, D), lambda i: (0, 0))
    return _call(
        body,
        name="ln_fwd",
        grid=(S // LN_TM,),
        in_specs=[tile, tile, vec, vec],
        out_specs=[tile, tile, tile, pl.BlockSpec((LN_TM, 1), lambda i: (i, 0))],
        out_shape=[jax.ShapeDtypeStruct((S, D), F32), jax.ShapeDtypeStruct((S, D), MXU_DTYPE),
                   jax.ShapeDtypeStruct((S, D), F32), jax.ShapeDtypeStruct((S, 1), F32)],
        semantics=("parallel",),
        operands=(res, branch, g, b),
        side=side,
    )


def _ln_bwd(dy_a, dy_b, xhat, rstd, g, side=None):
    def body(da_ref, db_in_ref, xhat_ref, rstd_ref, g_ref, dp_ref, dpb_ref, dg_ref, db_ref):
        @pl.when(pl.program_id(0) == 0)
        def _():
            dg_ref[...] = jnp.zeros_like(dg_ref)
            db_ref[...] = jnp.zeros_like(db_ref)

        dy = da_ref[...] + ALPHA * db_in_ref[...]
        xhat = xhat_ref[...]
        dp = _ln_input_grad(dy, xhat, rstd_ref[...], g_ref[...])
        dp_ref[...] = dp
        dpb_ref[...] = dp.astype(dpb_ref.dtype)
        dg_ref[...] += _colsum(dy * xhat)
        db_ref[...] += _colsum(dy)

    tile = pl.BlockSpec((LN_TM, D), lambda i: (i, 0))
    vec = pl.BlockSpec((1, D), lambda i: (0, 0))
    return _call(
        body,
        name="ln_bwd",
        grid=(S // LN_TM,),
        in_specs=[tile, tile, tile, pl.BlockSpec((LN_TM, 1), lambda i: (i, 0)), vec],
        out_specs=[tile, tile, vec, vec],
        out_shape=[jax.ShapeDtypeStruct((S, D), F32), jax.ShapeDtypeStruct((S, D), MXU_DTYPE),
                   jax.ShapeDtypeStruct((1, D), F32), jax.ShapeDtypeStruct((1, D), F32)],
        semantics=("arbitrary",),
        operands=(dy_a, dy_b, xhat, rstd, g),
        side=side,
    )


def _ln_loss_bwd(res, branch, g, b, target, side=None):
    def body(res_ref, br_ref, g_ref, b_ref, t_ref, loss_ref, dp_ref, dpb_ref, dg_ref, db_ref):
        @pl.when(pl.program_id(0) == 0)
        def _():
            loss_ref[...] = jnp.zeros_like(loss_ref)
            dg_ref[...] = jnp.zeros_like(dg_ref)
            db_ref[...] = jnp.zeros_like(db_ref)

        xhat, rstd = _ln_stats(ALPHA * res_ref[...] + br_ref[...])
        gv = g_ref[...]
        err = xhat * gv + b_ref[...] - t_ref[...]
        loss_ref[...] += (0.5 / D) * jnp.sum(_colsum(err * err), axis=1, keepdims=True)
        dy = err * (1.0 / D)
        dp = _ln_input_grad(dy, xhat, rstd, gv)
        dp_ref[...] = dp
        dpb_ref[...] = dp.astype(dpb_ref.dtype)
        dg_ref[...] += _colsum(dy * xhat)
        db_ref[...] += _colsum(dy)

    tile = pl.BlockSpec((LN_TM, D), lambda i: (i, 0))
    vec = pl.BlockSpec((1, D), lambda i: (0, 0))
    return _call(
        body,
        name="ln_loss_bwd",
        grid=(S // LN_TM,),
        in_specs=[tile, tile, vec, vec, tile],
        out_specs=[pl.BlockSpec((1, 1), lambda i: (0, 0)), tile, tile, vec, vec],
        out_shape=[jax.ShapeDtypeStruct((1, 1), F32), jax.ShapeDtypeStruct((S, D), F32),
                   jax.ShapeDtypeStruct((S, D), MXU_DTYPE),
                   jax.ShapeDtypeStruct((1, D), F32), jax.ShapeDtypeStruct((1, D), F32)],
        semantics=("arbitrary",),
        operands=(res, branch, g, b, target),
        side=side,
    )


FFN_TC = 256


def _ffn_act_fwd(up, gpre, w, b, side=None):
    tc = FFN_TC

    def body(up_ref, x_ref, w_ref, b_ref, o_ref, xpad_ref):
        xpad_ref[pl.ds(0, PAD), :] = jnp.zeros((PAD, tc), F32)
        xpad_ref[pl.ds(PAD, S), :] = x_ref[...]
        wv = w_ref[...]
        bv = b_ref[...]

        def step(ci, carry):
            r0 = pl.multiple_of(ci * CHUNK, CHUNK)
            taps = _past_taps(xpad_ref, r0, FFN_CONV_W)
            gate = bv + taps[0] * wv[0:1, :] + taps[1] * wv[1:2, :] + taps[2] * wv[2:3, :]
            o_ref[pl.ds(r0, CHUNK), :] = (_gelu(gate)[0] * up_ref[pl.ds(r0, CHUNK), :]).astype(o_ref.dtype)
            return carry

        lax.fori_loop(0, S // CHUNK, step, 0)

    col = pl.BlockSpec((S, tc), lambda j: (0, j))
    return _call(
        body,
        name="ffn_act_fwd",
        grid=(D_FF // tc,),
        in_specs=[col, col, pl.BlockSpec((FFN_CONV_W, tc), lambda j: (0, j)), pl.BlockSpec((1, tc), lambda j: (0, j))],
        out_specs=col,
        out_shape=jax.ShapeDtypeStruct((S, D_FF), MXU_DTYPE),
        scratch_shapes=[pltpu.VMEM((S + PAD, tc), F32)],
        semantics=("parallel",),
        operands=(up, gpre, w, b),
        side=side,
    )


def _ffn_act_bwd(dfin, up, gpre, w, b, side=None):
    tc = FFN_TC
    width = FFN_CONV_W

    def body(df_ref, up_ref, x_ref, w_ref, b_ref, dup_ref, dx_ref, dw_ref, db_ref, xpad_ref, dpad_ref):
        xpad_ref[pl.ds(0, PAD), :] = jnp.zeros((PAD, tc), F32)
        xpad_ref[pl.ds(PAD, S), :] = x_ref[...]
        dpad_ref[pl.ds(S, PAD), :] = jnp.zeros((PAD, tc), F32)
        wv = w_ref[...]
        bv = b_ref[...]

        def gate_grad(ci, acc):
            r0 = pl.multiple_of(ci * CHUNK, CHUNK)
            taps = _past_taps(xpad_ref, r0, width)
            gate = bv + taps[0] * wv[0:1, :] + taps[1] * wv[1:2, :] + taps[2] * wv[2:3, :]
            ge, dge = _gelu(gate)
            df = df_ref[pl.ds(r0, CHUNK), :]
            dup_ref[pl.ds(r0, CHUNK), :] = (df * ge).astype(dup_ref.dtype)
            d = df * up_ref[pl.ds(r0, CHUNK), :] * dge
            dpad_ref[pl.ds(r0, CHUNK), :] = d
            return tuple(acc[k] + _colsum(taps[k] * d) for k in range(width)) + (acc[width] + _colsum(d),)

        zero = jnp.zeros((1, tc), F32)
        acc = lax.fori_loop(0, S // CHUNK, gate_grad, (zero,) * (width + 1))
        for k in range(width):
            dw_ref[k:k + 1, :] = acc[k]
        db_ref[...] = acc[width]

        def input_grad(ci, carry):
            r0 = pl.multiple_of(ci * CHUNK, CHUNK)
            ahead = _future_taps(dpad_ref, r0, width)
            dx = ahead[0] * wv[2:3, :] + ahead[1] * wv[1:2, :] + ahead[2] * wv[0:1, :]
            dx_ref[pl.ds(r0, CHUNK), :] = dx.astype(dx_ref.dtype)
            return carry

        lax.fori_loop(0, S // CHUNK, input_grad, 0)

    col = pl.BlockSpec((S, tc), lambda j: (0, j))
    w_spec = pl.BlockSpec((width, tc), lambda j: (0, j))
    vec = pl.BlockSpec((1, tc), lambda j: (0, j))
    return _call(
        body,
        name="ffn_act_bwd",
        grid=(D_FF // tc,),
        in_specs=[col, col, col, w_spec, vec],
        out_specs=[col, col, w_spec, vec],
        out_shape=[jax.ShapeDtypeStruct((S, D_FF), MXU_DTYPE)] * 2
        + [jax.ShapeDtypeStruct((width, D_FF), F32), jax.ShapeDtypeStruct((1, D_FF), F32)],
        scratch_shapes=[pltpu.VMEM((S + PAD, tc), F32), pltpu.VMEM((S + PAD, tc), F32)],
        semantics=("parallel",),
        operands=(dfin, up, gpre, w, b),
        side=side,
    )


def _adamw_update(w, g, m, v):
    m = ADAM_B1 * m + (1.0 - ADAM_B1) * g
    v = ADAM_B2 * v + (1.0 - ADAM_B2) * (g * g)
    m_hat = m / (1.0 - ADAM_B1 ** ADAM_STEP)
    v_hat = v / (1.0 - ADAM_B2 ** ADAM_STEP)
    delta = -ADAM_LR * (m_hat / (jnp.sqrt(v_hat) + ADAM_EPS) + ADAM_WD * w)
    return delta, m, v


def _add_pairs(send, pair, far_index, *, name):
    _, r_dim, c_dim = send.shape
    tr = r_dim // 4

    def body(far_ref, mine_ref, theirs_ref, o_ref):
        o_ref[...] = (mine_ref[...].astype(F32) + theirs_ref[...].astype(F32)).astype(o_ref.dtype)

    return pl.pallas_call(
        body,
        name=name,
        grid_spec=pltpu.PrefetchScalarGridSpec(
            num_scalar_prefetch=1,
            grid=(3, r_dim // tr),
            in_specs=[pl.BlockSpec((None, tr, c_dim), lambda j, i, far: (far[j], i, 0)),
                      pl.BlockSpec((None, tr, c_dim), lambda j, i, far: (1 + j, i, 0))],
            out_specs=pl.BlockSpec((None, tr, c_dim), lambda j, i, far: (j, i, 0)),
        ),
        out_shape=jax.ShapeDtypeStruct((3, r_dim, c_dim), BF16),
        compiler_params=_cparams("parallel", "parallel"),
    )(far_index, send, pair)


def _reduce_adamw(w, m, v, g_own, pair, far, me, *, tr, name, part=0, earlier=None):
    _, r_dim, c_dim = w.shape
    cp = pair.shape[2]

    def body(me_ref, w_ref, m_ref, v_ref, g_ref, pair_ref, far_ref, *refs):
        grad_ref, delta_ref, nm_ref, nv_ref = refs[-4:]
        g = g_ref[...] + pair_ref[...].astype(F32)
        for j in range(3):
            g = g + far_ref[j].astype(F32)
        delta, nm, nv = _adamw_update(w_ref[...], g, m_ref[...], v_ref[...])
        grad_ref[...] = g
        delta_ref[...] = delta
        nm_ref[...] = nm
        nv_ref[...] = nv

    tile = pl.BlockSpec((None, tr, cp), lambda i, me: (0, i, part))
    if g_own.ndim == 3:
        own_spec = pl.BlockSpec((None, tr, cp), lambda i, me: (me[0], i, 0))
    else:
        own_spec = pl.BlockSpec((tr, cp), lambda i, me: (i, 0))
    earlier = list(earlier or ())
    return pl.pallas_call(
        body,
        name=name,
        grid_spec=pltpu.PrefetchScalarGridSpec(
            num_scalar_prefetch=1,
            grid=(r_dim // tr,),
            in_specs=[tile, tile, tile, own_spec, pl.BlockSpec((None, tr, cp), lambda i, me: (0, i, 0)),
                      pl.BlockSpec((3, tr, cp), lambda i, me: (0, i, 0))]
            + [pl.BlockSpec(memory_space=pl.ANY)] * len(earlier),
            out_specs=[tile] * 4,
        ),
        out_shape=[jax.ShapeDtypeStruct((1, r_dim, c_dim), F32)] * 4,
        input_output_aliases={7 + k: k for k in range(len(earlier))},
        compiler_params=_cparams("parallel"),
    )(me, w, m, v, g_own, pair, far, *earlier)


def _adamw_many(ws, ms, vs, gs):
    n = len(ws)

    def body(*refs):
        for i in range(n):
            delta, nm, nv = _adamw_update(refs[i][...], refs[3 * n + i][...], refs[n + i][...], refs[2 * n + i][...])
            refs[4 * n + i][...] = delta
            refs[5 * n + i][...] = nm
            refs[6 * n + i][...] = nv

    vmem = pl.BlockSpec(memory_space=pltpu.VMEM)
    res = pl.pallas_call(
        body,
        name="adamw_small",
        in_specs=[vmem] * (4 * n),
        out_specs=[vmem] * (3 * n),
        out_shape=[jax.ShapeDtypeStruct(w.shape, F32) for w in ws] * 3,
        compiler_params=pltpu.CompilerParams(vmem_limit_bytes=VMEM_LIMIT),
    )(*ws, *ms, *vs, *gs)
    return res[:n], res[n:2 * n], res[2 * n:]


def _adamw_blocks(w, m, v, g, *, name, side=None):
    per = 2

    def body(w_ref, m_ref, v_ref, g_ref, delta_ref, nm_ref, nv_ref):
        delta, nm, nv = _adamw_update(w_ref[...], g_ref[...], m_ref[...], v_ref[...])
        delta_ref[...] = delta
        nm_ref[...] = nm
        nv_ref[...] = nv

    tile = pl.BlockSpec((1, per) + w.shape[2:], lambda i: (0, i, 0, 0))
    return _call(
        body,
        name=name,
        grid=(w.shape[1] // per,),
        in_specs=[tile] * 4,
        out_specs=[tile] * 3,
        out_shape=[jax.ShapeDtypeStruct(w.shape, F32)] * 3,
        semantics=("parallel",),
        operands=(w, m, v, g),
        side=side,
    )


def _coords():
    return lax.axis_index("x"), lax.axis_index("y"), lax.axis_index("c")


def _flip(coord, bit):
    return 1 - coord if bit else coord


def _relative(k):
    x, y, c = _coords()
    return _flip(x, k & 4), _flip(y, k & 2), _flip(c, k & 1)


def _index(pos):
    return 4 * pos[0] + 2 * pos[1] + pos[2]


FAR = (4, 2, 6)
AG_US_PER_MB = 38.0
RS_US_PER_MB = 46.0
MIN_RIDE_US = 30.0
MIN_GATHER_RIDE_US = 22.0
ROW_ALIGN = 32


def _chunks(items, cursor, us, us_per_mb, through=None):
    budget = float("inf") if us is None else us / us_per_mb * 2 ** 20
    names = list(items)
    if through is not None:
        names = names[:names.index(through) + 1]
    chunks = []
    for name in names:
        arr = items[name]
        r_dim, c_dim = arr.shape[-2:]
        row_bytes = c_dim * arr.dtype.itemsize
        while cursor[name] < r_dim and budget > 0:
            rows = r_dim - cursor[name]
            if r_dim > ROW_ALIGN and budget < rows * row_bytes:
                rows = min(rows, max(ROW_ALIGN, int(budget // row_bytes) // ROW_ALIGN * ROW_ALIGN))
            chunks.append((name, cursor[name], rows))
            cursor[name] += rows
            budget -= rows * row_bytes
    return chunks


class _Gather:
    def __init__(self, shards):
        self.shards, self.bufs, self.cursor = {}, {}, {}
        self.add_shards(shards)

    def add_shards(self, shards):
        for n, shard in shards.items():
            self.shards[n], self.bufs[n], self.cursor[n] = shard, None, 0

    def take(self, us=None, through=None):
        if us is not None and us < MIN_GATHER_RIDE_US:
            return None
        chunks = _chunks(self.shards, self.cursor, us, AG_US_PER_MB, through)
        return _GatherSide(self, chunks) if chunks else None

    def get(self, name):
        chunks = _chunks(self.shards, self.cursor, None, AG_US_PER_MB, through=name)
        if chunks:
            _run_side(_GatherSide(self, chunks), "gather_" + name)
        return self.bufs[name]


class _GatherSide:
    SEMS = 8

    def __init__(self, owner, chunks):
        self.owner, self.chunks = owner, chunks
        self.names = list(dict.fromkeys(n for n, _, _ in chunks))
        old = [n for n in self.names if owner.bufs[n] is not None]
        self.operands = [owner.shards[n] for n in self.names] + [owner.bufs[n] for n in old]
        self.out_shape = [jax.ShapeDtypeStruct((N_DEV,) + owner.shards[n].shape, owner.shards[n].dtype)
                          for n in self.names]
        self.aliases = {len(self.names) + i: self.names.index(n) for i, n in enumerate(old)}
        self.sems = [pltpu.SemaphoreType.DMA((self.SEMS * len(chunks),)),
                     pltpu.SemaphoreType.DMA((self.SEMS * len(chunks),)), pltpu.SemaphoreType.DMA((len(chunks),))]

    def _halves(self, ci):
        _, r0, rows = self.chunks[ci]
        if rows % ROW_ALIGN:
            return None
        return (r0, rows // 2), (r0 + rows // 2, rows // 2)

    def _copy(self, ins, outs, sems, ci, s, block, to, rows=None, from_shard=False):
        name, r0, n = self.chunks[ci]
        if rows is not None:
            r0, n = rows
        w = self.names.index(name)
        slot = outs[w].at[_index(block), pl.ds(r0, n)]
        return pltpu.make_async_remote_copy(
            src_ref=ins[w].at[pl.ds(r0, n)] if from_shard else slot, dst_ref=slot,
            send_sem=sems[0].at[self.SEMS * ci + s], recv_sem=sems[1].at[self.SEMS * ci + s],
            device_id=to, device_id_type=MESH)

    def _own(self, ins, outs, sems, ci):
        name, r0, rows = self.chunks[ci]
        w = self.names.index(name)
        return pltpu.make_async_copy(ins[w].at[pl.ds(r0, rows)], outs[w].at[_index(_relative(0)), pl.ds(r0, rows)],
                                     sems[2].at[ci])

    def _pass(self, ins, outs, sems, ci, which):
        source, target = ((4, 2), (2, 4))[which]
        return self._copy(ins, outs, sems, ci, 3 + which, _relative(source), _relative(target),
                          rows=self._halves(ci)[which])

    def start(self, ins, outs, sems):
        me = _relative(0)
        for ci in range(len(self.chunks)):
            self._own(ins, outs, sems, ci).start()
        for ci in range(len(self.chunks)):
            self._copy(ins, outs, sems, ci, 1, me, _relative(4), from_shard=True).start()
            self._copy(ins, outs, sems, ci, 2, me, _relative(2), from_shard=True).start()
            if self._halves(ci) is None:
                self._copy(ins, outs, sems, ci, 3, me, _relative(6), from_shard=True).start()
        for ci in range(len(self.chunks)):
            self._copy(ins, outs, sems, ci, 0, me, _relative(1), from_shard=True).start()

    def mid(self, ins, outs, sems):
        me = _relative(0)
        cut = [ci for ci in range(len(self.chunks)) if self._halves(ci) is not None]
        for ci in cut:
            self._copy(ins, outs, sems, ci, 1, _relative(4), me).wait_recv()
            self._pass(ins, outs, sems, ci, 0).start()
            self._copy(ins, outs, sems, ci, 5, _relative(4), _relative(1)).start()
        for ci in cut:
            self._copy(ins, outs, sems, ci, 2, _relative(2), me).wait_recv()
            self._pass(ins, outs, sems, ci, 1).start()
            self._copy(ins, outs, sems, ci, 6, _relative(2), _relative(1)).start()

    def finish(self, ins, outs, sems):
        me, sibling = _relative(0), _relative(1)
        n = len(self.chunks)
        for ci in range(n):
            if self._halves(ci) is None:
                for s, k in ((1, 4), (2, 2), (3, 6)):
                    self._copy(ins, outs, sems, ci, s, _relative(k), me).wait_recv()
                for j, k in enumerate(FAR):
                    self._copy(ins, outs, sems, ci, 5 + j, _relative(k), sibling).start()
            else:
                h0, h1 = self._halves(ci)
                self._copy(ins, outs, sems, ci, 3, _relative(6), me, rows=h0).wait_recv()
                self._copy(ins, outs, sems, ci, 4, _relative(6), me, rows=h1).wait_recv()
                self._copy(ins, outs, sems, ci, 7, _relative(6), sibling).start()
        for ci in range(n):
            self._copy(ins, outs, sems, ci, 0, sibling, me).wait_recv()
            for j, k in enumerate(FAR):
                self._copy(ins, outs, sems, ci, 5 + j, _relative(k | 1), me).wait_recv()
        for ci in range(n):
            self._copy(ins, outs, sems, ci, 0, me, sibling, from_shard=True).wait_send()
            self._copy(ins, outs, sems, ci, 1, me, _relative(4), from_shard=True).wait_send()
            self._copy(ins, outs, sems, ci, 2, me, _relative(2), from_shard=True).wait_send()
            if self._halves(ci) is None:
                self._copy(ins, outs, sems, ci, 3, me, _relative(6), from_shard=True).wait_send()
            else:
                self._pass(ins, outs, sems, ci, 0).wait_send()
                self._pass(ins, outs, sems, ci, 1).wait_send()
            for j, k in enumerate(FAR):
                self._copy(ins, outs, sems, ci, 5 + j, _relative(k), sibling).wait_send()
            self._own(ins, outs, sems, ci).wait()

    def done(self, results):
        for n, buf in zip(self.names, results):
            self.owner.bufs[n] = buf


class _Scatter:
    def __init__(self, me, far_index):
        self.me, self.far_index = me, far_index
        self.sends, self.owns, self.pairs, self.sums, self.fars = {}, {}, {}, {}, {}
        self.pair_cursor, self.far_cursor = {}, {}

    def add(self, name, send, own):
        self.sends[name] = send
        self.owns[name] = own
        self.pairs[name] = self.fars[name] = None
        self.pair_cursor[name] = 0

    def _rows(self, name):
        return self.sends[name].shape[1]

    def _add_ready_pairs(self):
        for name in self.sends:
            if name not in self.sums and self.pair_cursor[name] == self._rows(name):
                self.sums[name] = _add_pairs(self.sends[name], self.pairs[name], self.far_index, name="pair_" + name)
                self.far_cursor[name] = 0

    def _side(self, us, through=None):
        self._add_ready_pairs()
        names = list(self.sends)
        if through is not None:
            names = names[:names.index(through) + 1]
        pair_chunks = [(n, self.pair_cursor[n], self._rows(n) - self.pair_cursor[n]) for n in names
                       if self.pair_cursor[n] < self._rows(n)]
        for n, _, _ in pair_chunks:
            self.pair_cursor[n] = self._rows(n)
        far_chunks = _chunks(self.sums, self.far_cursor, us, RS_US_PER_MB,
                             through if through in self.sums else None) if self.sums else []
        return _ScatterSide(self, pair_chunks, far_chunks) if pair_chunks or far_chunks else None

    def add_blocks(self, name, blocks32, blocks16):
        self.add(name, blocks16, blocks32)

    def take(self, us):
        return self._side(us) if us >= MIN_RIDE_US else None

    def flush_pairs(self, name):
        side = self._side(0.0)
        if side is not None:
            _run_side(side, name)
        self._add_ready_pairs()

    def get(self, name):
        step = 0
        while name not in self.sums or self.far_cursor[name] < self._rows(name):
            _run_side(self._side(None, through=name), "scatter_%s_%d" % (name, step))
            step += 1
        return self.owns[name], self.pairs[name], self.fars[name]


class _ScatterSide:
    TO_SIBLING = (1, 5, 3, 7)

    def __init__(self, owner, pair_chunks, far_chunks):
        self.owner, self.pair_chunks, self.far_chunks = owner, pair_chunks, far_chunks
        self.pair_names = list(dict.fromkeys(n for n, _, _ in pair_chunks))
        self.far_names = list(dict.fromkeys(n for n, _, _ in far_chunks))
        ins = [(owner.sends[n], owner.pairs[n], (4,)) for n in self.pair_names]
        ins += [(owner.sums[n], owner.fars[n], (3,)) for n in self.far_names]
        old = [i for i, (_, buf, _) in enumerate(ins) if buf is not None]
        self.operands = [src for src, _, _ in ins] + [ins[i][1] for i in old]
        self.out_shape = [jax.ShapeDtypeStruct(slots + src.shape[1:], BF16) for src, _, slots in ins]
        self.aliases = {len(ins) + j: i for j, i in enumerate(old)}
        n_pair, n_far = 4 * len(pair_chunks), 3 * len(far_chunks)
        self.sems = [pltpu.SemaphoreType.DMA((max(n_pair, 1),)), pltpu.SemaphoreType.DMA((max(n_pair, 1),)),
                     pltpu.SemaphoreType.DMA((max(n_far, 1),)), pltpu.SemaphoreType.DMA((max(n_far, 1),))]

    def _copies(self, ins, outs, sems):
        copies = []
        for ci, (name, r0, rows) in enumerate(self.pair_chunks):
            w = self.pair_names.index(name)
            for j, k in enumerate(self.TO_SIBLING):
                copies.append(pltpu.make_async_remote_copy(
                    src_ref=ins[w].at[_index(_relative(k)), pl.ds(r0, rows)], dst_ref=outs[w].at[j, pl.ds(r0, rows)],
                    send_sem=sems[0].at[4 * ci + j], recv_sem=sems[1].at[4 * ci + j],
                    device_id=_relative(1), device_id_type=MESH))
        for ci, (name, r0, rows) in enumerate(self.far_chunks):
            w = len(self.pair_names) + self.far_names.index(name)
            for j, k in enumerate(FAR):
                copies.append(pltpu.make_async_remote_copy(
                    src_ref=ins[w].at[j, pl.ds(r0, rows)], dst_ref=outs[w].at[j, pl.ds(r0, rows)],
                    send_sem=sems[2].at[3 * ci + j], recv_sem=sems[3].at[3 * ci + j],
                    device_id=_relative(k), device_id_type=MESH))
        return copies

    def start(self, ins, outs, sems):
        for cp in self._copies(ins, outs, sems):
            cp.start()

    def mid(self, ins, outs, sems):
        pass

    def finish(self, ins, outs, sems):
        for cp in self._copies(ins, outs, sems):
            cp.wait()

    def done(self, results):
        for n, buf in zip(self.pair_names, results):
            self.owner.pairs[n] = buf
        for n, buf in zip(self.far_names, results[len(self.pair_names):]):
            self.owner.fars[n] = buf


class _Joined:
    def __init__(self, sides):
        self.sides = sides
        self.operands, self.out_shape, self.sems, self.aliases, self.spans = [], [], [], {}, []
        for s in sides:
            i0, o0, s0 = len(self.operands), len(self.out_shape), len(self.sems)
            self.operands += list(s.operands)
            self.out_shape += list(s.out_shape)
            self.sems += list(s.sems)
            self.aliases.update({i0 + i: o0 + o for i, o in s.aliases.items()})
            self.spans.append((slice(i0, len(self.operands)), slice(o0, len(self.out_shape)),
                               slice(s0, len(self.sems))))

    def start(self, ins, outs, sems):
        for s, (i, o, m) in zip(self.sides, self.spans):
            s.start(ins[i], outs[o], sems[m])

    def mid(self, ins, outs, sems):
        for s, (i, o, m) in zip(self.sides, self.spans):
            s.mid(ins[i], outs[o], sems[m])

    def finish(self, ins, outs, sems):
        for s, (i, o, m) in zip(self.sides, self.spans):
            s.finish(ins[i], outs[o], sems[m])

    def done(self, results):
        for s, (_, o, _) in zip(self.sides, self.spans):
            s.done(results[o])


def _join(*sides):
    sides = [s for s in sides if s is not None]
    if len(sides) <= 1:
        return sides[0] if sides else None
    return _Joined(sides)


PART_W = 768


def _pack_rows(vecs):
    rows = -(-sum(v.shape[0] for v in vecs) // 8) * 8

    def body(*refs):
        out = refs[-1]
        out[...] = jnp.zeros_like(out)
        r0 = 0
        for v in refs[:-1]:
            k, n = v.shape
            for p in range(-(-n // PART_W)):
                w = min(PART_W, n - PART_W * p)
                out[p, r0:r0 + k, 0:w] = v[:, PART_W * p:PART_W * p + w]
            r0 += k

    vmem = pl.BlockSpec(memory_space=pltpu.VMEM)
    return pl.pallas_call(body, name="pack_small", in_specs=[vmem] * len(vecs), out_specs=vmem,
                          out_shape=jax.ShapeDtypeStruct((N_DEV, rows, PART_W), F32))(*vecs)


def _unpack_rows(packed, shapes):
    def body(packed_ref, *outs):
        r0 = 0
        for o in outs:
            k, n = o.shape
            for p in range(-(-n // PART_W)):
                w = min(PART_W, n - PART_W * p)
                o[:, PART_W * p:PART_W * p + w] = packed_ref[p, r0:r0 + k, 0:w]
            r0 += k

    vmem = pl.BlockSpec(memory_space=pltpu.VMEM)
    return pl.pallas_call(body, name="unpack_small", in_specs=[vmem], out_specs=[vmem] * len(shapes),
                          out_shape=[jax.ShapeDtypeStruct(s, F32) for s in shapes])(packed)


class _PartsToOwners:
    def __init__(self, arrays):
        self.n = len(arrays)
        self.pers = [a.shape[0] // N_DEV for a in arrays]
        self.operands, self.aliases = list(arrays), {}
        self.out_shape = [jax.ShapeDtypeStruct((N_DEV, per) + a.shape[1:], a.dtype) for a, per in zip(arrays, self.pers)]
        self.sems = [pltpu.SemaphoreType.DMA((self.n * (N_DEV - 1),))] * 2

    def _copies(self, ins, outs, sems):
        return [pltpu.make_async_remote_copy(
            src_ref=ins[j].at[pl.ds(self.pers[j] * _index(_relative(k)), self.pers[j])], dst_ref=outs[j].at[k],
            send_sem=sems[0].at[self.n * (k - 1) + j], recv_sem=sems[1].at[self.n * (k - 1) + j],
            device_id=_relative(k), device_id_type=MESH) for k in range(1, N_DEV) for j in range(self.n)]

    def start(self, ins, outs, sems):
        for cp in self._copies(ins, outs, sems):
            cp.start()

    def mid(self, ins, outs, sems):
        pass

    def finish(self, ins, outs, sems):
        for cp in self._copies(ins, outs, sems):
            cp.wait()

    def done(self, results):
        self.stages = list(results)


def _sum_parts(arrays, stages):
    n = len(arrays)
    pers = [a.shape[0] // N_DEV for a in arrays]

    def body(*refs):
        me = _index(_relative(0))
        for j in range(n):
            acc = refs[j][pl.ds(pers[j] * me, pers[j])]
            for k in range(1, N_DEV):
                acc = acc + refs[n + j][k].astype(F32)
            refs[2 * n + j][...] = acc

    vmem = pl.BlockSpec(memory_space=pltpu.VMEM)
    return pl.pallas_call(body, name="sum_small_parts", in_specs=[vmem] * (2 * n), out_specs=[vmem] * n,
                          out_shape=[jax.ShapeDtypeStruct((per,) + a.shape[1:], F32) for a, per in zip(arrays, pers)],
                          compiler_params=pltpu.CompilerParams(vmem_limit_bytes=VMEM_LIMIT))(*arrays, *stages)


class _PartsToAll:
    def __init__(self, parts):
        self.n = len(parts)
        self.pers = [p.shape[0] for p in parts]
        self.operands, self.aliases = list(parts), {}
        self.out_shape = [jax.ShapeDtypeStruct((N_DEV * p.shape[0],) + p.shape[1:], F32) for p in parts]
        self.sems = [pltpu.SemaphoreType.DMA((self.n * (N_DEV - 1),))] * 2 + [pltpu.SemaphoreType.DMA((self.n,))]

    def _rows(self, outs, j, pos):
        return outs[j].at[pl.ds(self.pers[j] * _index(pos), self.pers[j])]

    def _copy(self, ins, outs, sems, k, j, owner):
        return pltpu.make_async_remote_copy(
            src_ref=ins[j], dst_ref=self._rows(outs, j, owner),
            send_sem=sems[0].at[self.n * (k - 1) + j], recv_sem=sems[1].at[self.n * (k - 1) + j],
            device_id=_relative(k), device_id_type=MESH)

    def _own(self, ins, outs, sems, j):
        return pltpu.make_async_copy(ins[j], self._rows(outs, j, _relative(0)), sems[2].at[j])

    def start(self, ins, outs, sems):
        for j in range(self.n):
            self._own(ins, outs, sems, j).start()
            for k in range(1, N_DEV):
                self._copy(ins, outs, sems, k, j, _relative(0)).start()

    def mid(self, ins, outs, sems):
        pass

    def finish(self, ins, outs, sems):
        for j in range(self.n):
            for k in range(1, N_DEV):
                self._copy(ins, outs, sems, k, j, _relative(k)).wait_recv()
                self._copy(ins, outs, sems, k, j, _relative(0)).wait_send()
            self._own(ins, outs, sems, j).wait()

    def done(self, results):
        self.totals = list(results)


class _SmallSync:
    def __init__(self, vec_names, mat_names):
        self.vec_names, self.mat_names = vec_names, mat_names

    def begin(self, loss, grads):
        vecs = [loss] + [grads[n] for n in self.vec_names]
        self.shapes = [v.shape for v in vecs]
        self.own = [_diag_blocks(grads[n]) for n in self.mat_names] + [_pack_rows(vecs)]
        self.to_owners = _PartsToOwners([a.astype(BF16) for a in self.own[:-1]] + self.own[-1:])
        return self.to_owners

    def middle(self):
        self.to_all = _PartsToAll(_sum_parts(self.own, self.to_owners.stages))
        return self.to_all

    def end(self):
        *mats, packed = self.to_all.totals
        sums = _unpack_rows(packed, self.shapes)
        return sums[0], dict(zip(self.vec_names, sums[1:])), dict(zip(self.mat_names, mats))


def _block_diag(w):
    groups = []
    for g in range(N_RNN_GROUPS):
        placed = [jnp.pad(w[4 * g + b], ((RNN_BLOCK_W * b, RNN_BLOCK_W * (3 - b)),) * 2) for b in range(4)]
        groups.append(placed[0] + placed[1] + placed[2] + placed[3])
    return jnp.stack(groups)


def _diag_blocks(wg):
    blocks = []
    for n in range(4 * N_RNN_GROUPS):
        g, at = n // 4, RNN_BLOCK_W * (n % 4)
        blocks.append(wg[g, at:at + RNN_BLOCK_W, at:at + RNN_BLOCK_W])
    return jnp.stack(blocks)


def _heads_major(t, n_heads):
    return t.reshape(S, n_heads, HEAD_DIM).transpose(1, 0, 2)


def _heads_minor(t):
    return t.transpose(1, 0, 2).reshape(S, t.shape[0] * HEAD_DIM)


def _natural(gathered, how):
    n, r, c = gathered.shape
    if how == "rows":
        return gathered.reshape(n * r, c)
    return gathered.transpose(1, 0, 2).reshape(r, n * c)


def _blocks(full, how):
    if how == "rows":
        return full.reshape(N_DEV, full.shape[0] // N_DEV, full.shape[1])
    return full.reshape(full.shape[0], N_DEV, full.shape[1] // N_DEV).transpose(1, 0, 2)


def _cast_many(arrays, side=None):
    steps = 4

    def body(*refs):
        n = len(refs) // 2
        for src, dst in zip(refs[:n], refs[n:]):
            dst[...] = src[...].astype(dst.dtype)

    specs = [pl.BlockSpec((a.shape[0] // steps, a.shape[1]), lambda i: (i, 0)) for a in arrays]
    return _call(
        body,
        name="cast_weights",
        grid=(steps,),
        in_specs=specs,
        out_specs=specs,
        out_shape=[jax.ShapeDtypeStruct(a.shape, MXU_DTYPE) for a in arrays],
        semantics=("parallel",),
        operands=tuple(arrays),
        side=side,
    )


def _forward_backward(x2, xb, target, small, gather, scatter, sync):
    w_in_t = _natural(gather.get("w_in"), "rows")
    proj, projb = _mm(xb, w_in_t, tb=True, tm=1024, tn=512, tk=D, out_dtype=(F32, MXU_DTYPE), name="proj",
                      side=gather.take(110))

    qt = projb[:, :OFF_K].T.reshape(N_KV, GROUP, HEAD_DIM, S)
    k2, v2 = projb[:, OFF_K:OFF_V], projb[:, OFF_V:OFF_RX]
    kp = jnp.pad(_heads_major(k2, N_KV), ((0, 0), (BLOCK, 0), (0, 0)))
    vp = jnp.pad(_heads_major(v2, N_KV), ((0, 0), (BLOCK, 0), (0, 0)))
    kt = jnp.pad(k2.T.reshape(N_KV, HEAD_DIM, S), ((0, 0), (0, 0), (BLOCK, 0)))
    vt = jnp.pad(v2.T.reshape(N_KV, HEAD_DIM, S), ((0, 0), (0, 0), (BLOCK, 0)))
    sink_row = jnp.repeat(small["attn_sinks"].reshape(N_KV, 1, GROUP), BLOCK, axis=2)
    ot = _attn_fwd(qt, kp, vt, sink_row, side=gather.take(36)).reshape(D, S)

    rconv_w = _natural(gather.get("rnn_conv_w"), "cols")
    rxc = _conv_fwd(proj, OFF_RX, rconv_w, small["rnn_conv_b"], tc=512, name="rnn_conv_fwd", side=gather.take(18))
    r, i = _lru_gates(rxc, small["lru_wa"], small["lru_wi"], small["lru_ba"], small["lru_bi"], side=gather.take(33))
    h, yrin = _lru_scan_fwd(r, i, rxc, proj, small["lru_lambda"], side=gather.take(53))

    w_ap = _natural(gather.get("w_attn_proj"), "rows")
    w_rp = _natural(gather.get("w_rnn_proj"), "rows")
    y_attn = _mm(ot, w_ap, ta=True, tm=1024, tn=1024, tk=D, name="attn_proj", side=gather.take(22))
    y_rnn = _mm(yrin, w_rp, tm=1024, tn=1024, tk=D_RNN, name="rnn_proj", side=gather.take(27))
    mixin = _gate_fwd(y_attn, y_rnn, proj, small["b_gate"], side=gather.take(25))
    w_out = _natural(gather.get("w_out"), "rows")
    mix = _mm(mixin, w_out, tm=1024, tn=1024, tk=D, name="mix_out", side=gather.take(22))
    x1, x1b, xhat1, rstd1 = _ln_fwd(x2, mix, small["ln1_g"], small["ln1_b"], side=gather.take(23))

    w_up = gather.get("ffn_w_up")
    up = _mm(x1b, w_up, tm=1024, tn=768, tk=D, b_block=768, name="ffn_up", side=gather.take(58))
    w_gate = gather.get("ffn_w_gate")
    gpre = _mm(x1b, w_gate, tm=1024, tn=768, tk=D, b_block=768, name="ffn_gate", side=gather.take(58))
    fconv_w = _natural(gather.get("ffn_conv_w"), "cols")
    fin = _ffn_act_fwd(up, gpre, fconv_w, small["ffn_conv_b"], side=gather.take())
    w_down = _natural(gather.get("ffn_w_down"), "rows")
    f = _mm(fin, w_down, tm=1024, tn=1024, tk=2048, name="ffn_down")
    loss, dpre2, dpre2b, d_ln2_g, d_ln2_b = _ln_loss_bwd(x1, f, small["ln2_g"], small["ln2_b"], target)

    grads = {"ln2_g": d_ln2_g, "ln2_b": d_ln2_b}
    both = (F32, BF16)
    g32, g16 = _mm(fin, dpre2b, ta=True, tm=1024, tn=1024, tk=S, out_dtype=both, name="d_ffn_w_down")
    scatter.add_blocks("ffn_w_down", _blocks(g32, "rows"), _blocks(g16, "rows"))
    dfin = _mm(dpre2b, w_down, tb=True, tm=1024, tn=1024, tk=D, name="d_fin", side=scatter.take(57))
    dup, dgpre, grads["ffn_conv_w"], grads["ffn_conv_b"] = _ffn_act_bwd(
        dfin, up, gpre, fconv_w, small["ffn_conv_b"], side=scatter.take(85))
    g32, g16 = _mm(x1b, dup, ta=True, tm=1024, tn=768, tk=S, out_dtype=both, out_block=768, name="d_ffn_w_up",
                   side=scatter.take(57))
    scatter.add_blocks("ffn_w_up", g32, g16)
    g32, g16 = _mm(x1b, dgpre, ta=True, tm=1024, tn=768, tk=S, out_dtype=both, out_block=768, name="d_ffn_w_gate",
                   side=scatter.take(56))
    scatter.add_blocks("ffn_w_gate", g32, g16)
    dx1 = _mm(dup, w_up, tb=True, tm=1024, tn=1024, tk=768, b_block=768, name="d_x1_up", side=scatter.take(68))
    dx1 = _mm(dgpre, w_gate, tb=True, tm=1024, tn=1024, tk=768, b_block=768, add=dx1, name="d_x1_gate",
              side=scatter.take(70))
    dpre1, dpre1b, grads["ln1_g"], grads["ln1_b"] = _ln_bwd(dx1, dpre2, xhat1, rstd1, small["ln1_g"],
                                                            side=scatter.take(24))

    g32, g16 = _mm(mixin, dpre1b, ta=True, tm=1024, tn=1024, tk=S, out_dtype=both, name="d_w_out",
                   side=scatter.take(26))
    scatter.add_blocks("w_out", _blocks(g32, "rows"), _blocks(g16, "rows"))
    dmix = _mm(dpre1b, w_out, tb=True, tm=1024, tn=1024, tk=D, name="d_mixin", side=scatter.take(22))
    dya, dyr, dgl_a, dgl_r, db_a, db_r = _gate_bwd(dmix, y_attn, y_rnn, proj, small["b_gate"], side=scatter.take(36))
    grads["b_gate"] = jnp.concatenate([db_a, db_r], axis=1)
    g32, g16 = _mm(ot, dya, tm=1024, tn=1024, tk=S, out_dtype=both, name="d_w_attn_proj", side=scatter.take(38))
    scatter.add_blocks("w_attn_proj", _blocks(g32, "rows"), _blocks(g16, "rows"))
    g32, g16 = _mm(yrin, dyr, ta=True, tm=1280, tn=1024, tk=S, out_dtype=both, name="d_w_rnn_proj",
                   side=scatter.take(27))
    scatter.add_blocks("w_rnn_proj", _blocks(g32, "rows"), _blocks(g16, "rows"))
    dot_ = _mm(w_ap, dya, tb=True, tm=1024, tn=1024, tk=D, out_dtype=MXU_DTYPE, name="d_o", side=scatter.take(22))
    dyrin = _mm(dyr, w_rp, tb=True, tm=1024, tn=1280, tk=D, name="d_yrin", side=scatter.take(27))

    dry, dzr, dzi, drxc_in, grads["lru_ba"], grads["lru_bi"], grads["lru_lambda"] = _lru_scan_bwd(
        dyrin, proj, h, r, i, rxc, small["lru_lambda"], side=scatter.take(94))
    grads["lru_wa"], grads["lru_wi"] = _lru_gate_wgrad(rxc, dzr, dzi, side=scatter.take(22))
    drxc = _lru_gate_xgrad(dzr, dzi, small["lru_wa"], small["lru_wi"], drxc_in, side=scatter.take(33))
    drx, grads["rnn_conv_w"], grads["rnn_conv_b"] = _conv_bwd(drxc, proj, OFF_RX, rconv_w, tc=512,
                                                             name="rnn_conv_bwd", side=scatter.take(29))

    dqt, dk, dv, dsink = _attn_bwd(qt, kp, kt, vp, sink_row, dot_.reshape(N_KV, GROUP, HEAD_DIM, S),
                                   side=scatter.take(65))
    grads["attn_sinks"] = dsink.reshape(1, N_KV * GROUP)
    dproj = jnp.concatenate([
        dqt.reshape(D, S).T,
        _heads_minor(dk[:, BLOCK:, :]).astype(MXU_DTYPE),
        _heads_minor(dv[:, BLOCK:, :]).astype(MXU_DTYPE),
        drx, dry, dgl_a, dgl_r], axis=1)
    for part in range(W_IN_PARTS):
        cols = slice(part * (D // W_IN_PARTS), (part + 1) * (D // W_IN_PARTS))
        side = _join(scatter.take(55), sync.begin(loss, grads)) if part == 0 else scatter.take(68)
        g32, g16 = _mm(dproj, xb[:, cols], ta=True, tm=512, tn=D // W_IN_PARTS, tk=S, out_dtype=both,
                       name="d_w_in_%d" % part, side=side)
        scatter.add_blocks("w_in_%d" % part, _blocks(g32, "rows"), _blocks(g16, "rows"))
        scatter.flush_pairs("pairs_w_in_%d" % part)
    dx = _mm(dproj, w_in_t, tm=1024, tn=1024, tk=512, add=dpre1, add_scale=ALPHA, name="d_x",
             side=_join(scatter.take(400), sync.middle()))
    return dx


SHARDED = (
    ("w_in", "cols", 368), ("w_attn_proj", "rows", 32), ("w_rnn_proj", "rows", 32), ("w_out", "rows", 32),
    ("ffn_w_up", "cols", 128), ("ffn_w_gate", "cols", 128), ("ffn_w_down", "rows", 64),
)
SMALL_REPLICATED = ("b_gate", "rnn_conv_b", "lru_wa", "lru_ba", "lru_wi", "lru_bi", "lru_lambda", "attn_sinks",
                    "ln1_g", "ln1_b", "ffn_conv_b", "ln2_g", "ln2_b")
SMALL_SHARDED = ("rnn_conv_w", "ffn_conv_w")
SMALL_MATS = ("lru_wa", "lru_wi")
W_IN_PARTS = 2
WEIGHTS = ("w_in", "b_gate", "rnn_conv_w", "rnn_conv_b", "lru_wa", "lru_ba", "lru_wi", "lru_bi", "lru_lambda",
           "attn_sinks", "w_attn_proj", "w_rnn_proj", "w_out", "ln1_g", "ln1_b", "ffn_w_up", "ffn_w_gate",
           "ffn_conv_w", "ffn_conv_b", "ffn_w_down", "ln2_g", "ln2_b")


def kernel(x, w_in, b_gate, rnn_conv_w, rnn_conv_b, lru_wa, lru_ba, lru_wi, lru_bi, lru_lambda, attn_sinks, w_attn_proj, w_rnn_proj, w_out, ln1_g, ln1_b, ffn_w_up, ffn_w_gate, ffn_conv_w, ffn_conv_b, ffn_w_down, ln2_g, ln2_b, loss_target, m_w_in, m_b_gate, m_rnn_conv_w, m_rnn_conv_b, m_lru_wa, m_lru_ba, m_lru_wi, m_lru_bi, m_lru_lambda, m_attn_sinks, m_w_attn_proj, m_w_rnn_proj, m_w_out, m_ln1_g, m_ln1_b, m_ffn_w_up, m_ffn_w_gate, m_ffn_conv_w, m_ffn_conv_b, m_ffn_w_down, m_ln2_g, m_ln2_b, v_w_in, v_b_gate, v_rnn_conv_w, v_rnn_conv_b, v_lru_wa, v_lru_ba, v_lru_wi, v_lru_bi, v_lru_lambda, v_attn_sinks, v_w_attn_proj, v_w_rnn_proj, v_w_out, v_ln1_g, v_ln1_b, v_ffn_w_up, v_ffn_w_gate, v_ffn_conv_w, v_ffn_conv_b, v_ffn_w_down, v_ln2_g, v_ln2_b):
    given = dict(locals())
    wsh = {n: given[n][0] for n in WEIGHTS}
    msh = {n: given["m_" + n][0] for n in WEIGHTS}
    vsh = {n: given["v_" + n][0] for n in WEIGHTS}
    m_given = {n: given["m_" + n] for n in WEIGHTS}
    v_given = {n: given["v_" + n] for n in WEIGHTS}
    me = 4 * lax.axis_index("x") + 2 * lax.axis_index("y") + lax.axis_index("c")

    order = ("w_in", "rnn_conv_w", "ffn_conv_w", "w_attn_proj", "w_rnn_proj", "w_out", "ffn_w_up", "ffn_w_gate",
             "ffn_w_down")
    gather = _Gather({"w_in": wsh["w_in"].T.astype(MXU_DTYPE), **{n: wsh[n] for n in order[1:3]}})
    *casts, xb = _cast_many([wsh[n] for n in order[3:]] + [x[0]], side=gather.take(through="ffn_conv_w"))
    gather.add_shards(dict(zip(order[3:], casts)))
    small = {n: given[n] for n in SMALL_REPLICATED}
    small["lru_wa"] = _block_diag(wsh["lru_wa"])
    small["lru_wi"] = _block_diag(wsh["lru_wi"])
    scatter = _Scatter(me, jnp.stack([_index(_relative(k)) for k in FAR]).astype(jnp.int32))

    vec_names = tuple(n for n in SMALL_REPLICATED if n not in SMALL_MATS) + SMALL_SHARDED
    sync = _SmallSync(vec_names, SMALL_MATS)
    dx = _forward_backward(x[0], xb, loss_target[0], small, gather, scatter, sync)

    loss_total, g_small, mat_sums = sync.end()
    loss_total = loss_total.reshape(())
    for n in SMALL_SHARDED:
        width = wsh[n].shape[1]
        g_small[n] = lax.dynamic_slice_in_dim(g_small[n], me * width, width, axis=1)
    g_small = {n: g_small[n].reshape(given[n].shape) for n in vec_names}
    out = {}
    results = _adamw_many(*[[d[n] for n in vec_names] for d in (given, m_given, v_given, g_small)])
    for n, delta, nm, nv in zip(vec_names, *results):
        out[n] = (g_small[n], delta, nm, nv)
    for n in SMALL_MATS:
        g = mat_sums[n].reshape(given[n].shape)
        out[n] = (g, *_adamw_blocks(given[n], m_given[n], v_given[n], g, name="adamw_" + n))

    tile_rows = {n: tr for n, _, tr in SHARDED}
    me1 = me.reshape(1).astype(jnp.int32)
    res = None
    for n in list(scatter.sends):
        own, pair, far = scatter.get(n)
        if n.startswith("w_in_"):
            part = int(n[len("w_in_"):])
            w_t, m_t, v_t = (a["w_in"].transpose(0, 2, 1) for a in (given, m_given, v_given))
            res = _reduce_adamw(w_t, m_t, v_t, own, pair, far, me1, tr=tile_rows["w_in"], name="adamw_" + n,
                                part=part, earlier=res if part else None)
            out["w_in"] = tuple(r.transpose(0, 2, 1) for r in res)
        else:
            out[n] = tuple(_reduce_adamw(given[n], m_given[n], v_given[n], own, pair, far, me1, tr=tile_rows[n],
                                         name="adamw_" + n))

    outputs = [loss_total, dx[None]]
    for kind in range(4):
        outputs += [out[n][kind] for n in WEIGHTS]
    return tuple(outputs)
```

```python
import math

import jax
import jax.numpy as jnp
from jax import lax
from jax.experimental import pallas as pl
from jax.experimental.pallas import tpu as pltpu

F32 = jnp.float32
BF16 = jnp.bfloat16
MXU_DTYPE = jnp.bfloat16

N_DEV = 8
S = 2048
D = 2048
HEAD_DIM = 64
N_KV = 4
GROUP = 8
BLOCK = 128
D_KV = N_KV * HEAD_DIM
D_RNN = 2560
RNN_GROUP = 640
N_RNN_GROUPS = D_RNN // RNN_GROUP
RNN_BLOCK_W = 160
RNN_CONV_W = 4
LRU_C = 8.0
D_FF = 6144
FFN_CONV_W = 3
D_IN = 11776
OFF_K = 2048
OFF_V = 2304
OFF_RX = 2560
OFF_RY = 5120
OFF_GA = 7680
OFF_GR = 9728
LN_EPS = 1e-5
ALPHA = 2.0 ** 0.25
ADAM_LR = 0.001
ADAM_B1 = 0.9
ADAM_B2 = 0.999
ADAM_EPS = 1e-08
ADAM_WD = 0.01
ADAM_STEP = 10
NEG = -1e30
VMEM_LIMIT = 56 * 1024 * 1024
MID_RIDE_TENTHS = 6
MESH = pl.DeviceIdType.MESH
GELU_C = math.sqrt(2.0 / math.pi)


def _cparams(*sem):
    return pltpu.CompilerParams(dimension_semantics=sem or None, vmem_limit_bytes=VMEM_LIMIT)


def _call(body, *, name, grid, in_specs, out_specs, out_shape, operands, semantics, scratch_shapes=(), side=None):
    single = not isinstance(out_shape, (list, tuple))
    out_shape = [out_shape] if single else list(out_shape)
    out_specs = [out_specs] if single else list(out_specs)
    in_specs = list(in_specs)
    scratch_shapes = list(scratch_shapes)
    if side is None:
        res = pl.pallas_call(
            body, name=name, grid=grid, in_specs=in_specs, out_specs=out_specs, out_shape=out_shape,
            scratch_shapes=scratch_shapes, compiler_params=_cparams(*semantics))(*operands)
        return res[0] if single else res
    n_in, n_out, n_scr = len(in_specs), len(out_shape), len(scratch_shapes)
    s_in, s_out = len(side.operands), len(side.out_shape)
    hbm = pl.BlockSpec(memory_space=pltpu.HBM)
    steps = math.prod(grid)
    mid_step = (steps * MID_RIDE_TENTHS) // 10

    def with_copies(*refs):
        core_in, side_in = refs[:n_in], refs[n_in:n_in + s_in]
        o0 = n_in + s_in
        core_out, side_out = refs[o0:o0 + n_out], refs[o0 + n_out:o0 + n_out + s_out]
        c0 = o0 + n_out + s_out
        core_scr, sems = refs[c0:c0 + n_scr], refs[c0 + n_scr:]
        step = 0
        for d, size in enumerate(grid):
            step = step * size + pl.program_id(d)

        @pl.when(step == 0)
        def _():
            side.start(side_in, side_out, sems)

        body(*core_in, *core_out, *core_scr)

        @pl.when(step == mid_step)
        def _():
            side.mid(side_in, side_out, sems)

        @pl.when(step == steps - 1)
        def _():
            side.finish(side_in, side_out, sems)

    res = pl.pallas_call(
        with_copies, name=name, grid=grid,
        in_specs=in_specs + [hbm] * s_in, out_specs=out_specs + [hbm] * s_out,
        out_shape=out_shape + list(side.out_shape),
        scratch_shapes=scratch_shapes + list(side.sems),
        input_output_aliases={n_in + i: n_out + o for i, o in side.aliases.items()},
        compiler_params=_cparams(*(("arbitrary",) * len(grid))))(*operands, *side.operands)
    side.done(res[n_out:])
    return res[0] if single else res[:n_out]


def _run_side(side, name):
    def body(*refs):
        s_in, s_out = len(side.operands), len(side.out_shape)
        side.start(refs[:s_in], refs[s_in:s_in + s_out], refs[s_in + s_out:])
        side.mid(refs[:s_in], refs[s_in:s_in + s_out], refs[s_in + s_out:])
        side.finish(refs[:s_in], refs[s_in:s_in + s_out], refs[s_in + s_out:])

    hbm = pl.BlockSpec(memory_space=pltpu.HBM)
    res = pl.pallas_call(
        body, name=name, in_specs=[hbm] * len(side.operands), out_specs=[hbm] * len(side.out_shape),
        out_shape=list(side.out_shape), scratch_shapes=list(side.sems),
        input_output_aliases=dict(side.aliases))(*side.operands)
    side.done(res)


def _gelu(x):
    x2 = x * x
    t = jnp.tanh(GELU_C * (x + 0.044715 * x * x2))
    g = 0.5 * x * (1.0 + t)
    dg = 0.5 * (1.0 + t) + 0.5 * x * (1.0 - t * t) * (GELU_C * (1.0 + 3.0 * 0.044715 * x2))
    return g, dg


def _sigmoid(x):
    return 1.0 / (1.0 + jnp.exp(-x))


def _softplus(x):
    z = jnp.exp(-jnp.abs(x))
    small = z * (1.0 - z * (0.5 - z * (1.0 / 3.0 - 0.25 * z)))
    return jnp.maximum(x, 0.0) + jnp.where(z < 0.02, small, jnp.log(1.0 + z))


def _one_minus_exp(x):
    series = -x * (1.0 + x * (0.5 + x * (1.0 / 6.0 + x * (1.0 / 24.0))))
    return jnp.where(x > -0.03, series, 1.0 - jnp.exp(x))


def _colsum(v):
    return jnp.sum(v, axis=0, keepdims=True)


def _mm(a, b, *, tm, tn, tk, name, ta=False, tb=False, out_dtype=F32, b_block=None, out_block=None, add=None,
        add_scale=1.0, side=None):
    out_dtypes = out_dtype if isinstance(out_dtype, tuple) else (out_dtype,)
    if ta:
        k_dim, m_dim = a.shape
    else:
        m_dim, k_dim = a.shape
    if b_block is None:
        n_dim = b.shape[0] if tb else b.shape[1]
    else:
        n_dim = b.shape[1] if tb else b.shape[0] * b_block
    assert m_dim % tm == 0 and n_dim % tn == 0 and k_dim % tk == 0, (name, m_dim, n_dim, k_dim)
    nk = k_dim // tk
    dims = (((0 if ta else 1,), (1 if tb else 0,)), ((), ()))
    has_add = add is not None

    def body(*refs):
        a_ref, b_ref = refs[0], refs[1]
        add_ref = refs[2] if has_add else None
        first_out = 3 if has_add else 2
        o_refs = refs[first_out:first_out + len(out_dtypes)]

        def product():
            return lax.dot_general(a_ref[...].astype(MXU_DTYPE), b_ref[...].astype(MXU_DTYPE), dims,
                                   preferred_element_type=F32)

        def finish(acc):
            if has_add:
                acc = acc + add_scale * add_ref[...]
            for o_ref in o_refs:
                o_ref[...] = acc.astype(o_ref.dtype)

        if nk == 1:
            finish(product())
        else:
            acc_ref = refs[-1]
            k = pl.program_id(2)

            @pl.when(k == 0)
            def _():
                acc_ref[...] = jnp.zeros_like(acc_ref)

            acc_ref[...] += product()

            @pl.when(k == nk - 1)
            def _():
                finish(acc_ref[...])

    if ta:
        a_spec = pl.BlockSpec((tk, tm), lambda i, j, k: (k, i))
    else:
        a_spec = pl.BlockSpec((tm, tk), lambda i, j, k: (i, k))
    if b_block is None:
        if tb:
            b_spec = pl.BlockSpec((tn, tk), lambda i, j, k: (j, k))
        else:
            b_spec = pl.BlockSpec((tk, tn), lambda i, j, k: (k, j))
    elif tb:
        assert b_block % tk == 0
        b_spec = pl.BlockSpec((None, tn, tk), lambda i, j, k: ((k * tk) // b_block, j, ((k * tk) % b_block) // tk))
    else:
        assert b_block % tn == 0
        b_spec = pl.BlockSpec((None, tk, tn), lambda i, j, k: ((j * tn) // b_block, k, ((j * tn) % b_block) // tn))
    in_specs = [a_spec, b_spec]
    operands = [a, b]
    if has_add:
        in_specs.append(pl.BlockSpec((tm, tn), lambda i, j, k: (i, j)))
        operands.append(add)
    if out_block is None:
        out_spec = pl.BlockSpec((tm, tn), lambda i, j, k: (i, j))
        out_dims = (m_dim, n_dim)
    else:
        assert out_block % tn == 0
        out_spec = pl.BlockSpec((None, tm, tn), lambda i, j, k: ((j * tn) // out_block, i, ((j * tn) % out_block) // tn))
        out_dims = (n_dim // out_block, m_dim, out_block)
    res = _call(
        body,
        name=name,
        grid=(m_dim // tm, n_dim // tn, nk),
        in_specs=in_specs,
        out_specs=[out_spec] * len(out_dtypes),
        out_shape=[jax.ShapeDtypeStruct(out_dims, dt) for dt in out_dtypes],
        scratch_shapes=[pltpu.VMEM((tm, tn), F32)] if nk > 1 else [],
        semantics=("parallel", "parallel", "arbitrary"),
        operands=tuple(operands),
        side=side,
    )
    return res if isinstance(out_dtype, tuple) else res[0]


def _attn_bias(bias_ref, h):
    key = lax.broadcasted_iota(jnp.int32, (2 * BLOCK, GROUP * BLOCK), 0)
    col = lax.broadcasted_iota(jnp.int32, (2 * BLOCK, GROUP * BLOCK), 1)
    dist = BLOCK + (col & (BLOCK - 1)) - key
    head = h * GROUP + (col >> 7) + 1
    slope = jnp.exp(head.astype(F32) * (-0.25 * math.log(2.0)))
    bias = jnp.where((dist >= 0) & (dist < BLOCK), -slope * dist.astype(F32), NEG)
    bias_ref[1] = bias
    bias_ref[0] = jnp.where(key < BLOCK, NEG, bias)


def _attn_probs(kb, qt, bias, sink):
    s = jnp.dot(kb, qt, preferred_element_type=F32) * (HEAD_DIM ** -0.5) + bias
    m = jnp.maximum(jnp.max(s, axis=0, keepdims=True), sink)
    e = jnp.exp(s - m)
    e_sink = jnp.exp(sink - m)
    inv = 1.0 / (jnp.sum(e, axis=0, keepdims=True) + e_sink)
    return e * inv, e_sink * inv


def _heads_on_lanes(ref, r0):
    return jnp.concatenate([ref[g, :, pl.ds(r0, BLOCK)] for g in range(GROUP)], axis=1)


def _attn_fwd(qt, kp, vt, sink_row, side=None):
    cols = GROUP * BLOCK

    def body(q_ref, k_ref, vt_ref, sink_ref, o_ref, bias_ref):
        _attn_bias(bias_ref, pl.program_id(0))
        sink = sink_ref[...]

        def step(n, carry):
            r0 = pl.multiple_of(n * BLOCK, BLOCK)
            p, _ = _attn_probs(k_ref[pl.ds(r0, 2 * BLOCK), :], _heads_on_lanes(q_ref, r0),
                               bias_ref[jnp.minimum(n, 1)], sink)
            o = jnp.dot(vt_ref[:, pl.ds(r0, 2 * BLOCK)], p.astype(MXU_DTYPE), preferred_element_type=F32)
            for g in range(GROUP):
                o_ref[g, :, pl.ds(r0, BLOCK)] = o[:, g * BLOCK:(g + 1) * BLOCK].astype(o_ref.dtype)
            return carry

        lax.fori_loop(0, S // BLOCK, step, 0)

    hm = pl.BlockSpec((None, GROUP, HEAD_DIM, S), lambda h: (h, 0, 0, 0))
    return _call(
        body,
        name="attn_fwd",
        grid=(N_KV,),
        in_specs=[
            hm,
            pl.BlockSpec((None, BLOCK + S, HEAD_DIM), lambda h: (h, 0, 0)),
            pl.BlockSpec((None, HEAD_DIM, BLOCK + S), lambda h: (h, 0, 0)),
            pl.BlockSpec((None, 1, cols), lambda h: (h, 0, 0)),
        ],
        out_specs=hm,
        out_shape=jax.ShapeDtypeStruct((N_KV, GROUP, HEAD_DIM, S), MXU_DTYPE),
        scratch_shapes=[pltpu.VMEM((2, 2 * BLOCK, cols), F32)],
        semantics=("parallel",),
        operands=(qt, kp, vt, sink_row),
        side=side,
    )


def _attn_bwd(qt, kp, kt, vp, sink_row, dot_, side=None):
    cols = GROUP * BLOCK

    def body(q_ref, k_ref, kt_ref, v_ref, sink_ref, do_ref, dq_ref, dk_ref, dv_ref, dsink_ref, bias_ref):
        _attn_bias(bias_ref, pl.program_id(0))
        sink = sink_ref[...]
        dk_ref[...] = jnp.zeros_like(dk_ref)
        dv_ref[...] = jnp.zeros_like(dv_ref)
        nt = (((1,), (1,)), ((), ()))

        def step(n, sink_acc):
            r0 = pl.multiple_of(n * BLOCK, BLOCK)
            band = pl.ds(r0, 2 * BLOCK)
            qn = _heads_on_lanes(q_ref, r0)
            don = _heads_on_lanes(do_ref, r0)
            p, p_sink = _attn_probs(k_ref[band, :], qn, bias_ref[jnp.minimum(n, 1)], sink)
            dp = jnp.dot(v_ref[band, :], don, preferred_element_type=F32)
            delta = jnp.sum(p * dp, axis=0, keepdims=True)
            ds = (p * (dp - delta) * (HEAD_DIM ** -0.5)).astype(MXU_DTYPE)
            dq = jnp.dot(kt_ref[:, band], ds, preferred_element_type=F32)
            for g in range(GROUP):
                dq_ref[g, :, pl.ds(r0, BLOCK)] = dq[:, g * BLOCK:(g + 1) * BLOCK].astype(dq_ref.dtype)
            dk_ref[band, :] += lax.dot_general(ds, qn, nt, preferred_element_type=F32)
            dv_ref[band, :] += lax.dot_general(p.astype(MXU_DTYPE), don, nt, preferred_element_type=F32)
            return sink_acc - p_sink * delta

        sink_acc = lax.fori_loop(0, S // BLOCK, step, jnp.zeros((1, cols), F32))
        for g in range(GROUP):
            dsink_ref[g:g + 1, :] = jnp.sum(sink_acc[:, g * BLOCK:(g + 1) * BLOCK], axis=1, keepdims=True)

    hm = pl.BlockSpec((None, GROUP, HEAD_DIM, S), lambda h: (h, 0, 0, 0))
    kv = pl.BlockSpec((None, BLOCK + S, HEAD_DIM), lambda h: (h, 0, 0))
    return _call(
        body,
        name="attn_bwd",
        grid=(N_KV,),
        in_specs=[hm, kv, pl.BlockSpec((None, HEAD_DIM, BLOCK + S), lambda h: (h, 0, 0)), kv,
                  pl.BlockSpec((None, 1, cols), lambda h: (h, 0, 0)), hm],
        out_specs=[hm, kv, kv, pl.BlockSpec((None, GROUP, 1), lambda h: (h, 0, 0))],
        out_shape=[
            jax.ShapeDtypeStruct((N_KV, GROUP, HEAD_DIM, S), MXU_DTYPE),
            jax.ShapeDtypeStruct((N_KV, BLOCK + S, HEAD_DIM), F32),
            jax.ShapeDtypeStruct((N_KV, BLOCK + S, HEAD_DIM), F32),
            jax.ShapeDtypeStruct((N_KV, GROUP, 1), F32),
        ],
        scratch_shapes=[pltpu.VMEM((2, 2 * BLOCK, cols), F32)],
        semantics=("parallel",),
        operands=(qt, kp, kt, vp, sink_row, dot_),
        side=side,
    )


PAD = 8
CHUNK = 256


def _past_taps(xpad_ref, r0, width):
    ext = xpad_ref[pl.ds(r0, CHUNK + PAD), :]
    taps = []
    for k in range(width):
        back = width - 1 - k
        taps.append((ext if back == 0 else pltpu.roll(ext, back, 0))[PAD:, :])
    return taps


def _future_taps(xpad_ref, r0, width):
    ext = xpad_ref[pl.ds(r0, CHUNK + PAD), :]
    taps = []
    for ahead in range(width):
        taps.append((ext if ahead == 0 else pltpu.roll(ext, CHUNK + PAD - ahead, 0))[:CHUNK, :])
    return taps


def _conv_fwd(src, col0, w, b, *, tc, name, side=None):
    width, c_dim = w.shape

    def body(x_ref, w_ref, b_ref, o_ref, xpad_ref):
        xpad_ref[pl.ds(0, PAD), :] = jnp.zeros((PAD, tc), F32)
        xpad_ref[pl.ds(PAD, S), :] = x_ref[...]
        wv = w_ref[...]
        bv = b_ref[...]

        def step(ci, carry):
            r0 = pl.multiple_of(ci * CHUNK, CHUNK)
            taps = _past_taps(xpad_ref, r0, width)
            y = bv + taps[0] * wv[0:1, :]
            for k in range(1, width):
                y = y + taps[k] * wv[k:k + 1, :]
            o_ref[pl.ds(r0, CHUNK), :] = y
            return carry

        lax.fori_loop(0, S // CHUNK, step, 0)

    return _call(
        body,
        name=name,
        grid=(c_dim // tc,),
        in_specs=[
            pl.BlockSpec((S, tc), lambda j: (0, col0 // tc + j)),
            pl.BlockSpec((width, tc), lambda j: (0, j)),
            pl.BlockSpec((1, tc), lambda j: (0, j)),
        ],
        out_specs=pl.BlockSpec((S, tc), lambda j: (0, j)),
        out_shape=jax.ShapeDtypeStruct((S, c_dim), F32),
        scratch_shapes=[pltpu.VMEM((S + PAD, tc), F32)],
        semantics=("parallel",),
        operands=(src, w, b),
        side=side,
    )


def _conv_bwd(dy, src, col0, w, *, tc, name, side=None):
    width, c_dim = w.shape

    def body(dy_ref, x_ref, w_ref, dx_ref, dw_ref, db_ref, xpad_ref, dpad_ref):
        xpad_ref[pl.ds(0, PAD), :] = jnp.zeros((PAD, tc), F32)
        xpad_ref[pl.ds(PAD, S), :] = x_ref[...]
        dpad_ref[pl.ds(0, S), :] = dy_ref[...]
        dpad_ref[pl.ds(S, PAD), :] = jnp.zeros((PAD, tc), F32)
        wv = w_ref[...]

        def step(ci, acc):
            r0 = pl.multiple_of(ci * CHUNK, CHUNK)
            past = _past_taps(xpad_ref, r0, width)
            ahead = _future_taps(dpad_ref, r0, width)
            d = ahead[0]
            dx = d * wv[width - 1:width, :]
            for j in range(1, width):
                dx = dx + ahead[j] * wv[width - 1 - j:width - j, :]
            dx_ref[pl.ds(r0, CHUNK), :] = dx.astype(dx_ref.dtype)
            return tuple(acc[k] + _colsum(past[k] * d) for k in range(width)) + (acc[width] + _colsum(d),)

        zero = jnp.zeros((1, tc), F32)
        acc = lax.fori_loop(0, S // CHUNK, step, (zero,) * (width + 1))
        for k in range(width):
            dw_ref[k:k + 1, :] = acc[k]
        db_ref[...] = acc[width]

    return _call(
        body,
        name=name,
        grid=(c_dim // tc,),
        in_specs=[
            pl.BlockSpec((S, tc), lambda j: (0, j)),
            pl.BlockSpec((S, tc), lambda j: (0, col0 // tc + j)),
            pl.BlockSpec((width, tc), lambda j: (0, j)),
        ],
        out_specs=[
            pl.BlockSpec((S, tc), lambda j: (0, j)),
            pl.BlockSpec((width, tc), lambda j: (0, j)),
            pl.BlockSpec((1, tc), lambda j: (0, j)),
        ],
        out_shape=[
            jax.ShapeDtypeStruct((S, c_dim), MXU_DTYPE),
            jax.ShapeDtypeStruct((width, c_dim), F32),
            jax.ShapeDtypeStruct((1, c_dim), F32),
        ],
        scratch_shapes=[pltpu.VMEM((S + PAD, tc), F32), pltpu.VMEM((S + PAD, tc), F32)],
        semantics=("parallel",),
        operands=(dy, src, w),
        side=side,
    )


SCAN_TC = 256


def _lru_gates(rxc, wa, wi, ba, bi, side=None):
    tm = 512

    def body(x_ref, wa_ref, wi_ref, ba_ref, bi_ref, r_ref, i_ref):
        xv = x_ref[...].astype(MXU_DTYPE)
        r_ref[...] = _sigmoid(jnp.dot(xv, wa_ref[...].astype(MXU_DTYPE), preferred_element_type=F32) + ba_ref[...])
        i_ref[...] = _sigmoid(jnp.dot(xv, wi_ref[...].astype(MXU_DTYPE), preferred_element_type=F32) + bi_ref[...])

    x_spec = pl.BlockSpec((tm, RNN_GROUP), lambda g, i: (i, g))
    w_spec = pl.BlockSpec((None, RNN_GROUP, RNN_GROUP), lambda g, i: (g, 0, 0))
    b_spec = pl.BlockSpec((1, RNN_GROUP), lambda g, i: (0, g))
    return _call(
        body,
        name="lru_gates",
        grid=(N_RNN_GROUPS, S // tm),
        in_specs=[x_spec, w_spec, w_spec, b_spec, b_spec],
        out_specs=[x_spec, x_spec],
        out_shape=[jax.ShapeDtypeStruct((S, D_RNN), F32)] * 2,
        semantics=("parallel", "parallel"),
        operands=(rxc, wa, wi, ba, bi),
        side=side,
    )


def _scan_down(a, u, row):
    for d in (1, 2, 4):
        a_s = jnp.where(row >= d, pltpu.roll(a, d, 0), 1.0)
        u_s = jnp.where(row >= d, pltpu.roll(u, d, 0), 0.0)
        u = a * u_s + u
        a = a * a_s
    return a, u


def _scan_up(a, u, row):
    for d in (1, 2, 4):
        a_s = jnp.where(row < 8 - d, pltpu.roll(a, 8 - d, 0), 1.0)
        u_s = jnp.where(row < 8 - d, pltpu.roll(u, 8 - d, 0), 0.0)
        u = a * u_s + u
        a = a * a_s
    return a, u


def _lru_scan_fwd(r, i, rxc, proj, lam, side=None):
    tc = SCAN_TC

    def body(r_ref, i_ref, x_ref, ry_ref, lam_ref, h_ref, y_ref):
        rate = LRU_C * _softplus(-lam_ref[...])
        row = lax.broadcasted_iota(jnp.int32, (8, tc), 0)

        def step(ci, carry):
            r0 = pl.multiple_of(ci * 16, 16)
            log_a = -rate * r_ref[pl.ds(r0, 16), :]
            a16 = jnp.exp(log_a)
            u16 = jnp.sqrt(_one_minus_exp(2.0 * log_a)) * (i_ref[pl.ds(r0, 16), :] * x_ref[pl.ds(r0, 16), :])
            hs = []
            for half in range(2):
                a_cum, h0 = _scan_down(a16[8 * half:8 * half + 8, :], u16[8 * half:8 * half + 8, :], row)
                h = a_cum * carry + h0
                carry = jnp.broadcast_to(h[7:8, :], (8, tc))
                hs.append(h)
            h16 = jnp.concatenate(hs, axis=0)
            h_ref[pl.ds(r0, 16), :] = h16
            y_ref[pl.ds(r0, 16), :] = (h16 * _gelu(ry_ref[pl.ds(r0, 16), :])[0]).astype(y_ref.dtype)
            return carry

        lax.fori_loop(0, S // 16, step, jnp.zeros((8, tc), F32))

    col = pl.BlockSpec((S, tc), lambda j: (0, j))
    return _call(
        body,
        name="lru_scan_fwd",
        grid=(D_RNN // tc,),
        in_specs=[col, col, col, pl.BlockSpec((S, tc), lambda j: (0, OFF_RY // tc + j)),
                  pl.BlockSpec((1, tc), lambda j: (0, j))],
        out_specs=[col, col],
        out_shape=[jax.ShapeDtypeStruct((S, D_RNN), F32), jax.ShapeDtypeStruct((S, D_RNN), MXU_DTYPE)],
        semantics=("parallel",),
        operands=(r, i, rxc, proj, lam),
        side=side,
    )


def _lru_scan_bwd(dy, proj, h, r, i, rxc, lam, side=None):
    tc = SCAN_TC

    def body(dy_ref, ry_ref, h_ref, r_ref, i_ref, x_ref, lam_ref,
             dry_ref, dzr_ref, dzi_ref, dx_ref, dba_ref, dbi_ref, dlam_ref, a_ref, dh_ref, hp_ref):
        lam_v = lam_ref[...]
        rate = LRU_C * _softplus(-lam_v)
        dlam_scale = LRU_C * _sigmoid(-lam_v)
        row = lax.broadcasted_iota(jnp.int32, (8, tc), 0)
        hp_ref[pl.ds(0, PAD), :] = jnp.zeros((PAD, tc), F32)
        hp_ref[pl.ds(PAD, S), :] = h_ref[...]
        a_ref[pl.ds(S, PAD), :] = jnp.zeros((PAD, tc), F32)

        def prep(ci, carry):
            r0 = pl.multiple_of(ci * CHUNK, CHUNK)
            a_ref[pl.ds(r0, CHUNK), :] = jnp.exp(-rate * r_ref[pl.ds(r0, CHUNK), :])
            ge, dge = _gelu(ry_ref[pl.ds(r0, CHUNK), :])
            dyv = dy_ref[pl.ds(r0, CHUNK), :]
            dh_ref[pl.ds(r0, CHUNK), :] = dyv * ge
            dry_ref[pl.ds(r0, CHUNK), :] = (dyv * h_ref[pl.ds(r0, CHUNK), :] * dge).astype(dry_ref.dtype)
            return carry

        lax.fori_loop(0, S // CHUNK, prep, 0)

        def step(ci, state):
            carry, dba, dbi, dlam = state
            r0 = pl.multiple_of(S - 16 - ci * 16, 16)
            a_ext = a_ref[pl.ds(r0, 24), :]
            a_next = pltpu.roll(a_ext, 23, 0)
            h_prev = pltpu.roll(hp_ref[pl.ds(r0, 24), :], 1, 0)
            dh16 = dh_ref[pl.ds(r0, 16), :]
            gs = [None, None]
            for half in (1, 0):
                lo = 8 * half
                c_cum, g0 = _scan_up(a_next[lo:lo + 8, :], dh16[lo:lo + 8, :], row)
                g = c_cum * carry + g0
                carry = jnp.broadcast_to(g[0:1, :], (8, tc))
                gs[half] = g
            g16 = jnp.concatenate(gs, axis=0)
            a16 = a_ext[0:16, :]
            r16 = r_ref[pl.ds(r0, 16), :]
            i16 = i_ref[pl.ds(r0, 16), :]
            x16 = x_ref[pl.ds(r0, 16), :]
            a2 = a16 * a16
            sq = jnp.sqrt(_one_minus_exp(-2.0 * rate * r16))
            dx_ref[pl.ds(r0, 16), :] = g16 * sq * i16
            dzi = g16 * sq * x16 * i16 * (1.0 - i16)
            dlog_a = g16 * h_prev[8:24, :] * a16 - g16 * i16 * x16 * a2 / sq
            dzr = -rate * dlog_a * r16 * (1.0 - r16)
            dzr_ref[pl.ds(r0, 16), :] = dzr.astype(dzr_ref.dtype)
            dzi_ref[pl.ds(r0, 16), :] = dzi.astype(dzi_ref.dtype)
            return carry, dba + _colsum(dzr), dbi + _colsum(dzi), dlam + _colsum(dlog_a * r16)

        zero = jnp.zeros((1, tc), F32)
        _, dba, dbi, dlam = lax.fori_loop(0, S // 16, step, (jnp.zeros((8, tc), F32), zero, zero, zero))
        dba_ref[...] = dba
        dbi_ref[...] = dbi
        dlam_ref[...] = dlam * dlam_scale

    col = pl.BlockSpec((S, tc), lambda j: (0, j))
    vec = pl.BlockSpec((1, tc), lambda j: (0, j))
    return _call(
        body,
        name="lru_scan_bwd",
        grid=(D_RNN // tc,),
        in_specs=[col, pl.BlockSpec((S, tc), lambda j: (0, OFF_RY // tc + j)), col, col, col, col, vec],
        out_specs=[col, col, col, col, vec, vec, vec],
        out_shape=[jax.ShapeDtypeStruct((S, D_RNN), MXU_DTYPE)] * 3 + [jax.ShapeDtypeStruct((S, D_RNN), F32)]
        + [jax.ShapeDtypeStruct((1, D_RNN), F32)] * 3,
        scratch_shapes=[pltpu.VMEM((S + PAD, tc), F32), pltpu.VMEM((S, tc), F32), pltpu.VMEM((S + PAD, tc), F32)],
        semantics=("parallel",),
        operands=(dy, proj, h, r, i, rxc, lam),
        side=side,
    )


def _lru_gate_wgrad(rxc, dzr, dzi, side=None):
    def body(x_ref, dzr_ref, dzi_ref, dwa_ref, dwi_ref):
        xv = x_ref[...].astype(MXU_DTYPE)
        dims = (((0,), (0,)), ((), ()))
        dwa_ref[...] = lax.dot_general(xv, dzr_ref[...], dims, preferred_element_type=F32)
        dwi_ref[...] = lax.dot_general(xv, dzi_ref[...], dims, preferred_element_type=F32)

    col = pl.BlockSpec((S, RNN_GROUP), lambda g: (0, g))
    w_spec = pl.BlockSpec((None, RNN_GROUP, RNN_GROUP), lambda g: (g, 0, 0))
    return _call(
        body,
        name="lru_gate_wgrad",
        grid=(N_RNN_GROUPS,),
        in_specs=[col, col, col],
        out_specs=[w_spec, w_spec],
        out_shape=[jax.ShapeDtypeStruct((N_RNN_GROUPS, RNN_GROUP, RNN_GROUP), F32)] * 2,
        semantics=("parallel",),
        operands=(rxc, dzr, dzi),
        side=side,
    )


def _lru_gate_xgrad(dzr, dzi, wa, wi, dx_in, side=None):
    tm = 512

    def body(dzr_ref, dzi_ref, wa_ref, wi_ref, dx_ref, o_ref):
        dims = (((1,), (1,)), ((), ()))
        o_ref[...] = (dx_ref[...]
                      + lax.dot_general(dzr_ref[...], wa_ref[...].astype(MXU_DTYPE), dims, preferred_element_type=F32)
                      + lax.dot_general(dzi_ref[...], wi_ref[...].astype(MXU_DTYPE), dims, preferred_element_type=F32))

    x_spec = pl.BlockSpec((tm, RNN_GROUP), lambda g, i: (i, g))
    w_spec = pl.BlockSpec((None, RNN_GROUP, RNN_GROUP), lambda g, i: (g, 0, 0))
    return _call(
        body,
        name="lru_gate_xgrad",
        grid=(N_RNN_GROUPS, S // tm),
        in_specs=[x_spec, x_spec, w_spec, w_spec, x_spec],
        out_specs=x_spec,
        out_shape=jax.ShapeDtypeStruct((S, D_RNN), F32),
        semantics=("parallel", "parallel"),
        operands=(dzr, dzi, wa, wi, dx_in),
        side=side,
    )


def _gate_fwd(y_attn, y_rnn, proj, b_gate, side=None):
    t = 512

    def body(ya_ref, yr_ref, ga_ref, gr_ref, ba_ref, br_ref, o_ref):
        o_ref[...] = (_sigmoid(ga_ref[...] + ba_ref[...]) * ya_ref[...]
                      + _sigmoid(gr_ref[...] + br_ref[...]) * yr_ref[...]).astype(o_ref.dtype)

    tile = pl.BlockSpec((t, t), lambda i, j: (i, j))
    return _call(
        body,
        name="gate_fwd",
        grid=(S // t, D // t),
        in_specs=[tile, tile,
                  pl.BlockSpec((t, t), lambda i, j: (i, OFF_GA // t + j)),
                  pl.BlockSpec((t, t), lambda i, j: (i, OFF_GR // t + j)),
                  pl.BlockSpec((1, t), lambda i, j: (0, j)),
                  pl.BlockSpec((1, t), lambda i, j: (0, D // t + j))],
        out_specs=tile,
        out_shape=jax.ShapeDtypeStruct((S, D), MXU_DTYPE),
        semantics=("parallel", "parallel"),
        operands=(y_attn, y_rnn, proj, proj, b_gate, b_gate),
        side=side,
    )


def _gate_bwd(dmix, y_attn, y_rnn, proj, b_gate, side=None):
    t = 512

    def body(dm_ref, ya_ref, yr_ref, ga_ref, gr_ref, ba_ref, br_ref,
             dya_ref, dyr_ref, dga_ref, dgr_ref, dba_ref, dbr_ref):
        @pl.when(pl.program_id(1) == 0)
        def _():
            dba_ref[...] = jnp.zeros_like(dba_ref)
            dbr_ref[...] = jnp.zeros_like(dbr_ref)

        dm = dm_ref[...]
        ga = _sigmoid(ga_ref[...] + ba_ref[...])
        gr = _sigmoid(gr_ref[...] + br_ref[...])
        dya_ref[...] = (dm * ga).astype(dya_ref.dtype)
        dyr_ref[...] = (dm * gr).astype(dyr_ref.dtype)
        dga = dm * ya_ref[...] * ga * (1.0 - ga)
        dgr = dm * yr_ref[...] * gr * (1.0 - gr)
        dga_ref[...] = dga.astype(dga_ref.dtype)
        dgr_ref[...] = dgr.astype(dgr_ref.dtype)
        dba_ref[...] += _colsum(dga)
        dbr_ref[...] += _colsum(dgr)

    tile = pl.BlockSpec((t, t), lambda j, i: (i, j))
    vec = pl.BlockSpec((1, t), lambda j, i: (0, j))
    return _call(
        body,
        name="gate_bwd",
        grid=(D // t, S // t),
        in_specs=[tile, tile, tile,
                  pl.BlockSpec((t, t), lambda j, i: (i, OFF_GA // t + j)),
                  pl.BlockSpec((t, t), lambda j, i: (i, OFF_GR // t + j)),
                  vec,
                  pl.BlockSpec((1, t), lambda j, i: (0, D // t + j))],
        out_specs=[tile, tile, tile, tile, vec, vec],
        out_shape=[jax.ShapeDtypeStruct((S, D), MXU_DTYPE)] * 4 + [jax.ShapeDtypeStruct((1, D), F32)] * 2,
        semantics=("parallel", "arbitrary"),
        operands=(dmix, y_attn, y_rnn, proj, proj, b_gate, b_gate),
        side=side,
    )


LN_TM = 256


def _ln_stats(pre):
    mu = jnp.mean(pre, axis=-1, keepdims=True)
    xc = pre - mu
    rstd = lax.rsqrt(jnp.mean(xc * xc, axis=-1, keepdims=True) + LN_EPS)
    return xc * rstd, rstd


def _ln_input_grad(dy, xhat, rstd, g):
    dyg = dy * g
    return rstd * (dyg - jnp.mean(dyg, axis=-1, keepdims=True)
                   - xhat * jnp.mean(dyg * xhat, axis=-1, keepdims=True))


def _ln_fwd(res, branch, g, b, side=None):
    def body(res_ref, br_ref, g_ref, b_ref, y_ref, yb_ref, xhat_ref, rstd_ref):
        xhat, rstd = _ln_stats(ALPHA * res_ref[...] + br_ref[...])
        y = xhat * g_ref[...] + b_ref[...]
        y_ref[...] = y
        yb_ref[...] = y.astype(yb_ref.dtype)
        xhat_ref[...] = xhat
        rstd_ref[...] = rstd

    tile = pl.BlockSpec((LN_TM, D), lambda i: (i, 0))
    vec = pl.BlockSpec((1, D), lambda i: (0, 0))
    return _call(
        body,
        name="ln_fwd",
        grid=(S // LN_TM,),
        in_specs=[tile, tile, vec, vec],
        out_specs=[tile, tile, tile, pl.BlockSpec((LN_TM, 1), lambda i: (i, 0))],
        out_shape=[jax.ShapeDtypeStruct((S, D), F32), jax.ShapeDtypeStruct((S, D), MXU_DTYPE),
                   jax.ShapeDtypeStruct((S, D), F32), jax.ShapeDtypeStruct((S, 1), F32)],
        semantics=("parallel",),
        operands=(res, branch, g, b),
        side=side,
    )


def _ln_bwd(dy_a, dy_b, xhat, rstd, g, side=None):
    def body(da_ref, db_in_ref, xhat_ref, rstd_ref, g_ref, dp_ref, dpb_ref, dg_ref, db_ref):
        @pl.when(pl.program_id(0) == 0)
        def _():
            dg_ref[...] = jnp.zeros_like(dg_ref)
            db_ref[...] = jnp.zeros_like(db_ref)

        dy = da_ref[...] + ALPHA * db_in_ref[...]
        xhat = xhat_ref[...]
        dp = _ln_input_grad(dy, xhat, rstd_ref[...], g_ref[...])
        dp_ref[...] = dp
        dpb_ref[...] = dp.astype(dpb_ref.dtype)
        dg_ref[...] += _colsum(dy * xhat)
        db_ref[...] += _colsum(dy)

    tile = pl.BlockSpec((LN_TM, D), lambda i: (i, 0))
    vec = pl.BlockSpec((1, D), lambda i: (0, 0))
    return _call(
        body,
        name="ln_bwd",
        grid=(S // LN_TM,),
        in_specs=[tile, tile, tile, pl.BlockSpec((LN_TM, 1), lambda i: (i, 0)), vec],
        out_specs=[tile, tile, vec, vec],
        out_shape=[jax.ShapeDtypeStruct((S, D), F32), jax.ShapeDtypeStruct((S, D), MXU_DTYPE),
                   jax.ShapeDtypeStruct((1, D), F32), jax.ShapeDtypeStruct((1, D), F32)],
        semantics=("arbitrary",),
        operands=(dy_a, dy_b, xhat, rstd, g),
        side=side,
    )


def _ln_loss_bwd(res, branch, g, b, target, side=None):
    def body(res_ref, br_ref, g_ref, b_ref, t_ref, loss_ref, dp_ref, dpb_ref, dg_ref, db_ref):
        @pl.when(pl.program_id(0) == 0)
        def _():
            loss_ref[...] = jnp.zeros_like(loss_ref)
            dg_ref[...] = jnp.zeros_like(dg_ref)
            db_ref[...] = jnp.zeros_like(db_ref)

        xhat, rstd = _ln_stats(ALPHA * res_ref[...] + br_ref[...])
        gv = g_ref[...]
        err = xhat * gv + b_ref[...] - t_ref[...]
        loss_ref[...] += (0.5 / D) * jnp.sum(_colsum(err * err), axis=1, keepdims=True)
        dy = err * (1.0 / D)
        dp = _ln_input_grad(dy, xhat, rstd, gv)
        dp_ref[...] = dp
        dpb_ref[...] = dp.astype(dpb_ref.dtype)
        dg_ref[...] += _colsum(dy * xhat)
        db_ref[...] += _colsum(dy)

    tile = pl.BlockSpec((LN_TM, D), lambda i: (i, 0))
    vec = pl.BlockSpec((1, D), lambda i: (0, 0))
    return _call(
        body,
        name="ln_loss_bwd",
        grid=(S // LN_TM,),
        in_specs=[tile, tile, vec, vec, tile],
        out_specs=[pl.BlockSpec((1, 1), lambda i: (0, 0)), tile, tile, vec, vec],
        out_shape=[jax.ShapeDtypeStruct((1, 1), F32), jax.ShapeDtypeStruct((S, D), F32),
                   jax.ShapeDtypeStruct((S, D), MXU_DTYPE),
                   jax.ShapeDtypeStruct((1, D), F32), jax.ShapeDtypeStruct((1, D), F32)],
        semantics=("arbitrary",),
        operands=(res, branch, g, b, target),
        side=side,
    )


FFN_TC = 256


def _ffn_act_fwd(up, gpre, w, b, side=None):
    tc = FFN_TC

    def body(up_ref, x_ref, w_ref, b_ref, o_ref, xpad_ref):
        xpad_ref[pl.ds(0, PAD), :] = jnp.zeros((PAD, tc), F32)
        xpad_ref[pl.ds(PAD, S), :] = x_ref[...]
        wv = w_ref[...]
        bv = b_ref[...]

        def step(ci, carry):
            r0 = pl.multiple_of(ci * CHUNK, CHUNK)
            taps = _past_taps(xpad_ref, r0, FFN_CONV_W)
            gate = bv + taps[0] * wv[0:1, :] + taps[1] * wv[1:2, :] + taps[2] * wv[2:3, :]
            o_ref[pl.ds(r0, CHUNK), :] = (_gelu(gate)[0] * up_ref[pl.ds(r0, CHUNK), :]).astype(o_ref.dtype)
            return carry

        lax.fori_loop(0, S // CHUNK, step, 0)

    col = pl.BlockSpec((S, tc), lambda j: (0, j))
    return _call(
        body,
        name="ffn_act_fwd",
        grid=(D_FF // tc,),
        in_specs=[col, col, pl.BlockSpec((FFN_CONV_W, tc), lambda j: (0, j)), pl.BlockSpec((1, tc), lambda j: (0, j))],
        out_specs=col,
        out_shape=jax.ShapeDtypeStruct((S, D_FF), MXU_DTYPE),
        scratch_shapes=[pltpu.VMEM((S + PAD, tc), F32)],
        semantics=("parallel",),
        operands=(up, gpre, w, b),
        side=side,
    )


def _ffn_act_bwd(dfin, up, gpre, w, b, side=None):
    tc = FFN_TC
    width = FFN_CONV_W

    def body(df_ref, up_ref, x_ref, w_ref, b_ref, dup_ref, dx_ref, dw_ref, db_ref, xpad_ref, dpad_ref):
        xpad_ref[pl.ds(0, PAD), :] = jnp.zeros((PAD, tc), F32)
        xpad_ref[pl.ds(PAD, S), :] = x_ref[...]
        dpad_ref[pl.ds(S, PAD), :] = jnp.zeros((PAD, tc), F32)
        wv = w_ref[...]
        bv = b_ref[...]

        def gate_grad(ci, acc):
            r0 = pl.multiple_of(ci * CHUNK, CHUNK)
            taps = _past_taps(xpad_ref, r0, width)
            gate = bv + taps[0] * wv[0:1, :] + taps[1] * wv[1:2, :] + taps[2] * wv[2:3, :]
            ge, dge = _gelu(gate)
            df = df_ref[pl.ds(r0, CHUNK), :]
            dup_ref[pl.ds(r0, CHUNK), :] = (df * ge).astype(dup_ref.dtype)
            d = df * up_ref[pl.ds(r0, CHUNK), :] * dge
            dpad_ref[pl.ds(r0, CHUNK), :] = d
            return tuple(acc[k] + _colsum(taps[k] * d) for k in range(width)) + (acc[width] + _colsum(d),)

        zero = jnp.zeros((1, tc), F32)
        acc = lax.fori_loop(0, S // CHUNK, gate_grad, (zero,) * (width + 1))
        for k in range(width):
            dw_ref[k:k + 1, :] = acc[k]
        db_ref[...] = acc[width]

        def input_grad(ci, carry):
            r0 = pl.multiple_of(ci * CHUNK, CHUNK)
            ahead = _future_taps(dpad_ref, r0, width)
            dx = ahead[0] * wv[2:3, :] + ahead[1] * wv[1:2, :] + ahead[2] * wv[0:1, :]
            dx_ref[pl.ds(r0, CHUNK), :] = dx.astype(dx_ref.dtype)
            return carry

        lax.fori_loop(0, S // CHUNK, input_grad, 0)

    col = pl.BlockSpec((S, tc), lambda j: (0, j))
    w_spec = pl.BlockSpec((width, tc), lambda j: (0, j))
    vec = pl.BlockSpec((1, tc), lambda j: (0, j))
    return _call(
        body,
        name="ffn_act_bwd",
        grid=(D_FF // tc,),
        in_specs=[col, col, col, w_spec, vec],
        out_specs=[col, col, w_spec, vec],
        out_shape=[jax.ShapeDtypeStruct((S, D_FF), MXU_DTYPE)] * 2
        + [jax.ShapeDtypeStruct((width, D_FF), F32), jax.ShapeDtypeStruct((1, D_FF), F32)],
        scratch_shapes=[pltpu.VMEM((S + PAD, tc), F32), pltpu.VMEM((S + PAD, tc), F32)],
        semantics=("parallel",),
        operands=(dfin, up, gpre, w, b),
        side=side,
    )


def _adamw_update(w, g, m, v):
    m = ADAM_B1 * m + (1.0 - ADAM_B1) * g
    v = ADAM_B2 * v + (1.0 - ADAM_B2) * (g * g)
    m_hat = m / (1.0 - ADAM_B1 ** ADAM_STEP)
    v_hat = v / (1.0 - ADAM_B2 ** ADAM_STEP)
    delta = -ADAM_LR * (m_hat / (jnp.sqrt(v_hat) + ADAM_EPS) + ADAM_WD * w)
    return delta, m, v


def _add_pairs(send, pair, far_index, *, name):
    _, r_dim, c_dim = send.shape
    tr = r_dim // 4

    def body(far_ref, mine_ref, theirs_ref, o_ref):
        o_ref[...] = (mine_ref[...].astype(F32) + theirs_ref[...].astype(F32)).astype(o_ref.dtype)

    return pl.pallas_call(
        body,
        name=name,
        grid_spec=pltpu.PrefetchScalarGridSpec(
            num_scalar_prefetch=1,
            grid=(3, r_dim // tr),
            in_specs=[pl.BlockSpec((None, tr, c_dim), lambda j, i, far: (far[j], i, 0)),
                      pl.BlockSpec((None, tr, c_dim), lambda j, i, far: (1 + j, i, 0))],
            out_specs=pl.BlockSpec((None, tr, c_dim), lambda j, i, far: (j, i, 0)),
        ),
        out_shape=jax.ShapeDtypeStruct((3, r_dim, c_dim), BF16),
        compiler_params=_cparams("parallel", "parallel"),
    )(far_index, send, pair)


def _reduce_adamw(w, m, v, g_own, pair, far, me, *, tr, name, part=0, earlier=None):
    _, r_dim, c_dim = w.shape
    cp = pair.shape[2]

    def body(me_ref, w_ref, m_ref, v_ref, g_ref, pair_ref, far_ref, *refs):
        grad_ref, delta_ref, nm_ref, nv_ref = refs[-4:]
        g = g_ref[...] + pair_ref[...].astype(F32)
        for j in range(3):
            g = g + far_ref[j].astype(F32)
        delta, nm, nv = _adamw_update(w_ref[...], g, m_ref[...], v_ref[...])
        grad_ref[...] = g
        delta_ref[...] = delta
        nm_ref[...] = nm
        nv_ref[...] = nv

    tile = pl.BlockSpec((None, tr, cp), lambda i, me: (0, i, part))
    if g_own.ndim == 3:
        own_spec = pl.BlockSpec((None, tr, cp), lambda i, me: (me[0], i, 0))
    else:
        own_spec = pl.BlockSpec((tr, cp), lambda i, me: (i, 0))
    earlier = list(earlier or ())
    return pl.pallas_call(
        body,
        name=name,
        grid_spec=pltpu.PrefetchScalarGridSpec(
            num_scalar_prefetch=1,
            grid=(r_dim // tr,),
            in_specs=[tile, tile, tile, own_spec, pl.BlockSpec((None, tr, cp), lambda i, me: (0, i, 0)),
                      pl.BlockSpec((3, tr, cp), lambda i, me: (0, i, 0))]
            + [pl.BlockSpec(memory_space=pl.ANY)] * len(earlier),
            out_specs=[tile] * 4,
        ),
        out_shape=[jax.ShapeDtypeStruct((1, r_dim, c_dim), F32)] * 4,
        input_output_aliases={7 + k: k for k in range(len(earlier))},
        compiler_params=_cparams("parallel"),
    )(me, w, m, v, g_own, pair, far, *earlier)


def _adamw_many(ws, ms, vs, gs):
    n = len(ws)

    def body(*refs):
        for i in range(n):
            delta, nm, nv = _adamw_update(refs[i][...], refs[3 * n + i][...], refs[n + i][...], refs[2 * n + i][...])
            refs[4 * n + i][...] = delta
            refs[5 * n + i][...] = nm
            refs[6 * n + i][...] = nv

    vmem = pl.BlockSpec(memory_space=pltpu.VMEM)
    res = pl.pallas_call(
        body,
        name="adamw_small",
        in_specs=[vmem] * (4 * n),
        out_specs=[vmem] * (3 * n),
        out_shape=[jax.ShapeDtypeStruct(w.shape, F32) for w in ws] * 3,
        compiler_params=pltpu.CompilerParams(vmem_limit_bytes=VMEM_LIMIT),
    )(*ws, *ms, *vs, *gs)
    return res[:n], res[n:2 * n], res[2 * n:]


def _adamw_blocks(w, m, v, g, *, name, side=None):
    per = 2

    def body(w_ref, m_ref, v_ref, g_ref, delta_ref, nm_ref, nv_ref):
        delta, nm, nv = _adamw_update(w_ref[...], g_ref[...], m_ref[...], v_ref[...])
        delta_ref[...] = delta
        nm_ref[...] = nm
        nv_ref[...] = nv

    tile = pl.BlockSpec((1, per) + w.shape[2:], lambda i: (0, i, 0, 0))
    return _call(
        body,
        name=name,
        grid=(w.shape[1] // per,),
        in_specs=[tile] * 4,
        out_specs=[tile] * 3,
        out_shape=[jax.ShapeDtypeStruct(w.shape, F32)] * 3,
        semantics=("parallel",),
        operands=(w, m, v, g),
        side=side,
    )


def _coords():
    return lax.axis_index("x"), lax.axis_index("y"), lax.axis_index("c")


def _flip(coord, bit):
    return 1 - coord if bit else coord


def _relative(k):
    x, y, c = _coords()
    return _flip(x, k & 4), _flip(y, k & 2), _flip(c, k & 1)


def _index(pos):
    return 4 * pos[0] + 2 * pos[1] + pos[2]


FAR = (4, 2, 6)
AG_US_PER_MB = 38.0
RS_US_PER_MB = 46.0
MIN_RIDE_US = 30.0
PAIR_EXCHANGE_US = 20.0
MIN_GATHER_RIDE_US = 22.0
ROW_ALIGN = 32


def _chunks(items, cursor, us, us_per_mb, through=None):
    budget = float("inf") if us is None else us / us_per_mb * 2 ** 20
    names = list(items)
    if through is not None:
        names = names[:names.index(through) + 1]
    chunks = []
    for name in names:
        arr = items[name]
        r_dim, c_dim = arr.shape[-2:]
        row_bytes = c_dim * arr.dtype.itemsize
        while cursor[name] < r_dim and budget > 0:
            rows = r_dim - cursor[name]
            if r_dim > ROW_ALIGN and budget < rows * row_bytes:
                rows = min(rows, max(ROW_ALIGN, int(budget // row_bytes) // ROW_ALIGN * ROW_ALIGN))
            chunks.append((name, cursor[name], rows))
            cursor[name] += rows
            budget -= rows * row_bytes
    return chunks


class _Gather:
    def __init__(self, shards):
        self.shards, self.bufs, self.cursor = {}, {}, {}
        self.add_shards(shards)

    def add_shards(self, shards):
        for n, shard in shards.items():
            self.shards[n], self.bufs[n], self.cursor[n] = shard, None, 0

    def take(self, us=None, through=None):
        if us is not None and us < MIN_GATHER_RIDE_US:
            return None
        chunks = _chunks(self.shards, self.cursor, us, AG_US_PER_MB, through)
        return _GatherSide(self, chunks) if chunks else None

    def get(self, name):
        chunks = _chunks(self.shards, self.cursor, None, AG_US_PER_MB, through=name)
        if chunks:
            _run_side(_GatherSide(self, chunks), "gather_" + name)
        return self.bufs[name]


class _GatherSide:
    SEMS = 8

    def __init__(self, owner, chunks):
        self.owner, self.chunks = owner, chunks
        self.names = list(dict.fromkeys(n for n, _, _ in chunks))
        old = [n for n in self.names if owner.bufs[n] is not None]
        self.operands = [owner.shards[n] for n in self.names] + [owner.bufs[n] for n in old]
        self.out_shape = [jax.ShapeDtypeStruct((N_DEV,) + owner.shards[n].shape, owner.shards[n].dtype)
                          for n in self.names]
        self.aliases = {len(self.names) + i: self.names.index(n) for i, n in enumerate(old)}
        self.sems = [pltpu.SemaphoreType.DMA((self.SEMS * len(chunks),)),
                     pltpu.SemaphoreType.DMA((self.SEMS * len(chunks),)), pltpu.SemaphoreType.DMA((len(chunks),))]

    def _halves(self, ci):
        _, r0, rows = self.chunks[ci]
        if rows % ROW_ALIGN:
            return None
        return (r0, rows // 2), (r0 + rows // 2, rows // 2)

    def _copy(self, ins, outs, sems, ci, s, block, to, rows=None, from_shard=False):
        name, r0, n = self.chunks[ci]
        if rows is not None:
            r0, n = rows
        w = self.names.index(name)
        slot = outs[w].at[_index(block), pl.ds(r0, n)]
        return pltpu.make_async_remote_copy(
            src_ref=ins[w].at[pl.ds(r0, n)] if from_shard else slot, dst_ref=slot,
            send_sem=sems[0].at[self.SEMS * ci + s], recv_sem=sems[1].at[self.SEMS * ci + s],
            device_id=to, device_id_type=MESH)

    def _own(self, ins, outs, sems, ci):
        name, r0, rows = self.chunks[ci]
        w = self.names.index(name)
        return pltpu.make_async_copy(ins[w].at[pl.ds(r0, rows)], outs[w].at[_index(_relative(0)), pl.ds(r0, rows)],
                                     sems[2].at[ci])

    def _pass(self, ins, outs, sems, ci, which):
        source, target = ((4, 2), (2, 4))[which]
        return self._copy(ins, outs, sems, ci, 3 + which, _relative(source), _relative(target),
                          rows=self._halves(ci)[which])

    def start(self, ins, outs, sems):
        me = _relative(0)
        for ci in range(len(self.chunks)):
            self._own(ins, outs, sems, ci).start()
        for ci in range(len(self.chunks)):
            self._copy(ins, outs, sems, ci, 1, me, _relative(4), from_shard=True).start()
            self._copy(ins, outs, sems, ci, 2, me, _relative(2), from_shard=True).start()
            if self._halves(ci) is None:
                self._copy(ins, outs, sems, ci, 3, me, _relative(6), from_shard=True).start()
        for ci in range(len(self.chunks)):
            self._copy(ins, outs, sems, ci, 0, me, _relative(1), from_shard=True).start()

    def mid(self, ins, outs, sems):
        me = _relative(0)
        cut = [ci for ci in range(len(self.chunks)) if self._halves(ci) is not None]
        for ci in cut:
            self._copy(ins, outs, sems, ci, 1, _relative(4), me).wait_recv()
            self._pass(ins, outs, sems, ci, 0).start()
            self._copy(ins, outs, sems, ci, 5, _relative(4), _relative(1)).start()
        for ci in cut:
            self._copy(ins, outs, sems, ci, 2, _relative(2), me).wait_recv()
            self._pass(ins, outs, sems, ci, 1).start()
            self._copy(ins, outs, sems, ci, 6, _relative(2), _relative(1)).start()

    def finish(self, ins, outs, sems):
        me, sibling = _relative(0), _relative(1)
        n = len(self.chunks)
        for ci in range(n):
            if self._halves(ci) is None:
                for s, k in ((1, 4), (2, 2), (3, 6)):
                    self._copy(ins, outs, sems, ci, s, _relative(k), me).wait_recv()
                for j, k in enumerate(FAR):
                    self._copy(ins, outs, sems, ci, 5 + j, _relative(k), sibling).start()
            else:
                h0, h1 = self._halves(ci)
                self._copy(ins, outs, sems, ci, 3, _relative(6), me, rows=h0).wait_recv()
                self._copy(ins, outs, sems, ci, 4, _relative(6), me, rows=h1).wait_recv()
                self._copy(ins, outs, sems, ci, 7, _relative(6), sibling).start()
        for ci in range(n):
            self._copy(ins, outs, sems, ci, 0, sibling, me).wait_recv()
            for j, k in enumerate(FAR):
                self._copy(ins, outs, sems, ci, 5 + j, _relative(k | 1), me).wait_recv()
        for ci in range(n):
            self._copy(ins, outs, sems, ci, 0, me, sibling, from_shard=True).wait_send()
            self._copy(ins, outs, sems, ci, 1, me, _relative(4), from_shard=True).wait_send()
            self._copy(ins, outs, sems, ci, 2, me, _relative(2), from_shard=True).wait_send()
            if self._halves(ci) is None:
                self._copy(ins, outs, sems, ci, 3, me, _relative(6), from_shard=True).wait_send()
            else:
                self._pass(ins, outs, sems, ci, 0).wait_send()
                self._pass(ins, outs, sems, ci, 1).wait_send()
            for j, k in enumerate(FAR):
                self._copy(ins, outs, sems, ci, 5 + j, _relative(k), sibling).wait_send()
            self._own(ins, outs, sems, ci).wait()

    def done(self, results):
        for n, buf in zip(self.names, results):
            self.owner.bufs[n] = buf


class _Scatter:
    def __init__(self, me, far_index):
        self.me, self.far_index = me, far_index
        self.sends, self.owns, self.pairs, self.sums, self.fars = {}, {}, {}, {}, {}
        self.pair_cursor, self.far_cursor = {}, {}

    def add(self, name, send, own):
        self.sends[name] = send
        self.owns[name] = own
        self.pairs[name] = self.fars[name] = None
        self.pair_cursor[name] = 0

    def _rows(self, name):
        return self.sends[name].shape[1]

    def _add_ready_pairs(self):
        for name in self.sends:
            if name not in self.sums and self.pair_cursor[name] == self._rows(name):
                self.sums[name] = _add_pairs(self.sends[name], self.pairs[name], self.far_index, name="pair_" + name)
                self.far_cursor[name] = 0

    def _side(self, us, through=None):
        self._add_ready_pairs()
        names = list(self.sends)
        if through is not None:
            names = names[:names.index(through) + 1]
        pair_chunks = [(n, self.pair_cursor[n], self._rows(n) - self.pair_cursor[n]) for n in names
                       if self.pair_cursor[n] < self._rows(n)]
        for n, _, _ in pair_chunks:
            self.pair_cursor[n] = self._rows(n)
        far_chunks = _chunks(self.sums, self.far_cursor, us, RS_US_PER_MB,
                             through if through in self.sums else None) if self.sums else []
        return _ScatterSide(self, pair_chunks, far_chunks) if pair_chunks or far_chunks else None

    def add_blocks(self, name, blocks32, blocks16):
        self.add(name, blocks16, blocks32)

    def take(self, us):
        return self._side(us) if us >= MIN_RIDE_US else None

    def flush_pairs(self, name):
        side = self._side(PAIR_EXCHANGE_US)
        if side is not None:
            _run_side(side, name)
        self._add_ready_pairs()

    def get(self, name):
        step = 0
        while name not in self.sums or self.far_cursor[name] < self._rows(name):
            _run_side(self._side(None, through=name), "scatter_%s_%d" % (name, step))
            step += 1
        return self.owns[name], self.pairs[name], self.fars[name]


class _ScatterSide:
    TO_SIBLING = (1, 5, 3, 7)

    def __init__(self, owner, pair_chunks, far_chunks):
        self.owner, self.pair_chunks, self.far_chunks = owner, pair_chunks, far_chunks
        self.pair_names = list(dict.fromkeys(n for n, _, _ in pair_chunks))
        self.far_names = list(dict.fromkeys(n for n, _, _ in far_chunks))
        ins = [(owner.sends[n], owner.pairs[n], (4,)) for n in self.pair_names]
        ins += [(owner.sums[n], owner.fars[n], (3,)) for n in self.far_names]
        old = [i for i, (_, buf, _) in enumerate(ins) if buf is not None]
        self.operands = [src for src, _, _ in ins] + [ins[i][1] for i in old]
        self.out_shape = [jax.ShapeDtypeStruct(slots + src.shape[1:], BF16) for src, _, slots in ins]
        self.aliases = {len(ins) + j: i for j, i in enumerate(old)}
        n_pair, n_far = 4 * len(pair_chunks), 3 * len(far_chunks)
        self.sems = [pltpu.SemaphoreType.DMA((max(n_pair, 1),)), pltpu.SemaphoreType.DMA((max(n_pair, 1),)),
                     pltpu.SemaphoreType.DMA((max(n_far, 1),)), pltpu.SemaphoreType.DMA((max(n_far, 1),))]

    def _copies(self, ins, outs, sems):
        copies = []
        for ci, (name, r0, rows) in enumerate(self.pair_chunks):
            w = self.pair_names.index(name)
            for j, k in enumerate(self.TO_SIBLING):
                copies.append(pltpu.make_async_remote_copy(
                    src_ref=ins[w].at[_index(_relative(k)), pl.ds(r0, rows)], dst_ref=outs[w].at[j, pl.ds(r0, rows)],
                    send_sem=sems[0].at[4 * ci + j], recv_sem=sems[1].at[4 * ci + j],
                    device_id=_relative(1), device_id_type=MESH))
        for ci, (name, r0, rows) in enumerate(self.far_chunks):
            w = len(self.pair_names) + self.far_names.index(name)
            for j, k in enumerate(FAR):
                copies.append(pltpu.make_async_remote_copy(
                    src_ref=ins[w].at[j, pl.ds(r0, rows)], dst_ref=outs[w].at[j, pl.ds(r0, rows)],
                    send_sem=sems[2].at[3 * ci + j], recv_sem=sems[3].at[3 * ci + j],
                    device_id=_relative(k), device_id_type=MESH))
        return copies

    def start(self, ins, outs, sems):
        for cp in self._copies(ins, outs, sems):
            cp.start()

    def mid(self, ins, outs, sems):
        pass

    def finish(self, ins, outs, sems):
        for cp in self._copies(ins, outs, sems):
            cp.wait()

    def done(self, results):
        for n, buf in zip(self.pair_names, results):
            self.owner.pairs[n] = buf
        for n, buf in zip(self.far_names, results[len(self.pair_names):]):
            self.owner.fars[n] = buf


class _Joined:
    def __init__(self, sides):
        self.sides = sides
        self.operands, self.out_shape, self.sems, self.aliases, self.spans = [], [], [], {}, []
        for s in sides:
            i0, o0, s0 = len(self.operands), len(self.out_shape), len(self.sems)
            self.operands += list(s.operands)
            self.out_shape += list(s.out_shape)
            self.sems += list(s.sems)
            self.aliases.update({i0 + i: o0 + o for i, o in s.aliases.items()})
            self.spans.append((slice(i0, len(self.operands)), slice(o0, len(self.out_shape)),
                               slice(s0, len(self.sems))))

    def start(self, ins, outs, sems):
        for s, (i, o, m) in zip(self.sides, self.spans):
            s.start(ins[i], outs[o], sems[m])

    def mid(self, ins, outs, sems):
        for s, (i, o, m) in zip(self.sides, self.spans):
            s.mid(ins[i], outs[o], sems[m])

    def finish(self, ins, outs, sems):
        for s, (i, o, m) in zip(self.sides, self.spans):
            s.finish(ins[i], outs[o], sems[m])

    def done(self, results):
        for s, (_, o, _) in zip(self.sides, self.spans):
            s.done(results[o])


def _join(*sides):
    sides = [s for s in sides if s is not None]
    if len(sides) <= 1:
        return sides[0] if sides else None
    return _Joined(sides)


PART_W = 768


def _pack_rows(vecs):
    rows = -(-sum(v.shape[0] for v in vecs) // 8) * 8

    def body(*refs):
        out = refs[-1]
        out[...] = jnp.zeros_like(out)
        r0 = 0
        for v in refs[:-1]:
            k, n = v.shape
            for p in range(-(-n // PART_W)):
                w = min(PART_W, n - PART_W * p)
                out[p, r0:r0 + k, 0:w] = v[:, PART_W * p:PART_W * p + w]
            r0 += k

    vmem = pl.BlockSpec(memory_space=pltpu.VMEM)
    return pl.pallas_call(body, name="pack_small", in_specs=[vmem] * len(vecs), out_specs=vmem,
                          out_shape=jax.ShapeDtypeStruct((N_DEV, rows, PART_W), F32))(*vecs)


def _unpack_rows(packed, shapes):
    def body(packed_ref, *outs):
        r0 = 0
        for o in outs:
            k, n = o.shape
            for p in range(-(-n // PART_W)):
                w = min(PART_W, n - PART_W * p)
                o[:, PART_W * p:PART_W * p + w] = packed_ref[p, r0:r0 + k, 0:w]
            r0 += k

    vmem = pl.BlockSpec(memory_space=pltpu.VMEM)
    return pl.pallas_call(body, name="unpack_small", in_specs=[vmem], out_specs=[vmem] * len(shapes),
                          out_shape=[jax.ShapeDtypeStruct(s, F32) for s in shapes])(packed)


class _PartsToOwners:
    def __init__(self, arrays):
        self.n = len(arrays)
        self.pers = [a.shape[0] // N_DEV for a in arrays]
        self.operands, self.aliases = list(arrays), {}
        self.out_shape = [jax.ShapeDtypeStruct((N_DEV, per) + a.shape[1:], a.dtype) for a, per in zip(arrays, self.pers)]
        self.sems = [pltpu.SemaphoreType.DMA((self.n * (N_DEV - 1),))] * 2

    def _copies(self, ins, outs, sems):
        return [pltpu.make_async_remote_copy(
            src_ref=ins[j].at[pl.ds(self.pers[j] * _index(_relative(k)), self.pers[j])], dst_ref=outs[j].at[k],
            send_sem=sems[0].at[self.n * (k - 1) + j], recv_sem=sems[1].at[self.n * (k - 1) + j],
            device_id=_relative(k), device_id_type=MESH) for k in range(1, N_DEV) for j in range(self.n)]

    def start(self, ins, outs, sems):
        for cp in self._copies(ins, outs, sems):
            cp.start()

    def mid(self, ins, outs, sems):
        pass

    def finish(self, ins, outs, sems):
        for cp in self._copies(ins, outs, sems):
            cp.wait()

    def done(self, results):
        self.stages = list(results)


def _sum_parts(arrays, stages):
    n = len(arrays)
    pers = [a.shape[0] // N_DEV for a in arrays]

    def body(*refs):
        me = _index(_relative(0))
        for j in range(n):
            acc = refs[j][pl.ds(pers[j] * me, pers[j])]
            for k in range(1, N_DEV):
                acc = acc + refs[n + j][k].astype(F32)
            refs[2 * n + j][...] = acc

    vmem = pl.BlockSpec(memory_space=pltpu.VMEM)
    return pl.pallas_call(body, name="sum_small_parts", in_specs=[vmem] * (2 * n), out_specs=[vmem] * n,
                          out_shape=[jax.ShapeDtypeStruct((per,) + a.shape[1:], F32) for a, per in zip(arrays, pers)],
                          compiler_params=pltpu.CompilerParams(vmem_limit_bytes=VMEM_LIMIT))(*arrays, *stages)


class _PartsToAll:
    def __init__(self, parts):
        self.n = len(parts)
        self.pers = [p.shape[0] for p in parts]
        self.operands, self.aliases = list(parts), {}
        self.out_shape = [jax.ShapeDtypeStruct((N_DEV * p.shape[0],) + p.shape[1:], F32) for p in parts]
        self.sems = [pltpu.SemaphoreType.DMA((self.n * (N_DEV - 1),))] * 2 + [pltpu.SemaphoreType.DMA((self.n,))]

    def _rows(self, outs, j, pos):
        return outs[j].at[pl.ds(self.pers[j] * _index(pos), self.pers[j])]

    def _copy(self, ins, outs, sems, k, j, owner):
        return pltpu.make_async_remote_copy(
            src_ref=ins[j], dst_ref=self._rows(outs, j, owner),
            send_sem=sems[0].at[self.n * (k - 1) + j], recv_sem=sems[1].at[self.n * (k - 1) + j],
            device_id=_relative(k), device_id_type=MESH)

    def _own(self, ins, outs, sems, j):
        return pltpu.make_async_copy(ins[j], self._rows(outs, j, _relative(0)), sems[2].at[j])

    def start(self, ins, outs, sems):
        for j in range(self.n):
            self._own(ins, outs, sems, j).start()
            for k in range(1, N_DEV):
                self._copy(ins, outs, sems, k, j, _relative(0)).start()

    def mid(self, ins, outs, sems):
        pass

    def finish(self, ins, outs, sems):
        for j in range(self.n):
            for k in range(1, N_DEV):
                self._copy(ins, outs, sems, k, j, _relative(k)).wait_recv()
                self._copy(ins, outs, sems, k, j, _relative(0)).wait_send()
            self._own(ins, outs, sems, j).wait()

    def done(self, results):
        self.totals = list(results)


class _SmallSync:
    def __init__(self, vec_names, mat_names):
        self.vec_names, self.mat_names = vec_names, mat_names

    def begin(self, loss, grads):
        vecs = [loss] + [grads[n] for n in self.vec_names]
        self.shapes = [v.shape for v in vecs]
        self.own = [_diag_blocks(grads[n]) for n in self.mat_names] + [_pack_rows(vecs)]
        self.to_owners = _PartsToOwners([a.astype(BF16) for a in self.own[:-1]] + self.own[-1:])
        return self.to_owners

    def middle(self):
        self.to_all = _PartsToAll(_sum_parts(self.own, self.to_owners.stages))
        return self.to_all

    def end(self):
        *mats, packed = self.to_all.totals
        sums = _unpack_rows(packed, self.shapes)
        return sums[0], dict(zip(self.vec_names, sums[1:])), dict(zip(self.mat_names, mats))


def _block_diag(w):
    groups = []
    for g in range(N_RNN_GROUPS):
        placed = [jnp.pad(w[4 * g + b], ((RNN_BLOCK_W * b, RNN_BLOCK_W * (3 - b)),) * 2) for b in range(4)]
        groups.append(placed[0] + placed[1] + placed[2] + placed[3])
    return jnp.stack(groups)


def _diag_blocks(wg):
    blocks = []
    for n in range(4 * N_RNN_GROUPS):
        g, at = n // 4, RNN_BLOCK_W * (n % 4)
        blocks.append(wg[g, at:at + RNN_BLOCK_W, at:at + RNN_BLOCK_W])
    return jnp.stack(blocks)


def _heads_major(t, n_heads):
    return t.reshape(S, n_heads, HEAD_DIM).transpose(1, 0, 2)


def _heads_minor(t):
    return t.transpose(1, 0, 2).reshape(S, t.shape[0] * HEAD_DIM)


def _natural(gathered, how):
    n, r, c = gathered.shape
    if how == "rows":
        return gathered.reshape(n * r, c)
    return gathered.transpose(1, 0, 2).reshape(r, n * c)


def _blocks(full, how):
    if how == "rows":
        return full.reshape(N_DEV, full.shape[0] // N_DEV, full.shape[1])
    return full.reshape(full.shape[0], N_DEV, full.shape[1] // N_DEV).transpose(1, 0, 2)


def _cast_many(arrays, side=None):
    steps = 4

    def body(*refs):
        n = len(refs) // 2
        for src, dst in zip(refs[:n], refs[n:]):
            dst[...] = src[...].astype(dst.dtype)

    specs = [pl.BlockSpec((a.shape[0] // steps, a.shape[1]), lambda i: (i, 0)) for a in arrays]
    return _call(
        body,
        name="cast_weights",
        grid=(steps,),
        in_specs=specs,
        out_specs=specs,
        out_shape=[jax.ShapeDtypeStruct(a.shape, MXU_DTYPE) for a in arrays],
        semantics=("parallel",),
        operands=tuple(arrays),
        side=side,
    )


def _forward_backward(x2, xb, target, small, gather, scatter, sync):
    w_in_t = _natural(gather.get("w_in"), "rows")
    proj, projb = _mm(xb, w_in_t, tb=True, tm=1024, tn=512, tk=D, out_dtype=(F32, MXU_DTYPE), name="proj",
                      side=gather.take(110))

    qt = projb[:, :OFF_K].T.reshape(N_KV, GROUP, HEAD_DIM, S)
    k2, v2 = projb[:, OFF_K:OFF_V], projb[:, OFF_V:OFF_RX]
    kp = jnp.pad(_heads_major(k2, N_KV), ((0, 0), (BLOCK, 0), (0, 0)))
    vp = jnp.pad(_heads_major(v2, N_KV), ((0, 0), (BLOCK, 0), (0, 0)))
    kt = jnp.pad(k2.T.reshape(N_KV, HEAD_DIM, S), ((0, 0), (0, 0), (BLOCK, 0)))
    vt = jnp.pad(v2.T.reshape(N_KV, HEAD_DIM, S), ((0, 0), (0, 0), (BLOCK, 0)))
    sink_row = jnp.repeat(small["attn_sinks"].reshape(N_KV, 1, GROUP), BLOCK, axis=2)
    ot = _attn_fwd(qt, kp, vt, sink_row, side=gather.take(36)).reshape(D, S)

    rconv_w = _natural(gather.get("rnn_conv_w"), "cols")
    rxc = _conv_fwd(proj, OFF_RX, rconv_w, small["rnn_conv_b"], tc=512, name="rnn_conv_fwd", side=gather.take(18))
    r, i = _lru_gates(rxc, small["lru_wa"], small["lru_wi"], small["lru_ba"], small["lru_bi"], side=gather.take(33))
    h, yrin = _lru_scan_fwd(r, i, rxc, proj, small["lru_lambda"], side=gather.take(53))

    w_ap = _natural(gather.get("w_attn_proj"), "rows")
    w_rp = _natural(gather.get("w_rnn_proj"), "rows")
    y_attn = _mm(ot, w_ap, ta=True, tm=1024, tn=1024, tk=D, name="attn_proj", side=gather.take(22))
    y_rnn = _mm(yrin, w_rp, tm=1024, tn=1024, tk=D_RNN, name="rnn_proj", side=gather.take(27))
    mixin = _gate_fwd(y_attn, y_rnn, proj, small["b_gate"], side=gather.take(25))
    w_out = _natural(gather.get("w_out"), "rows")
    mix = _mm(mixin, w_out, tm=1024, tn=1024, tk=D, name="mix_out", side=gather.take(22))
    x1, x1b, xhat1, rstd1 = _ln_fwd(x2, mix, small["ln1_g"], small["ln1_b"], side=gather.take(23))

    w_up = gather.get("ffn_w_up")
    up = _mm(x1b, w_up, tm=1024, tn=768, tk=D, b_block=768, name="ffn_up", side=gather.take(58))
    w_gate = gather.get("ffn_w_gate")
    gpre = _mm(x1b, w_gate, tm=1024, tn=768, tk=D, b_block=768, name="ffn_gate", side=gather.take(58))
    fconv_w = _natural(gather.get("ffn_conv_w"), "cols")
    fin = _ffn_act_fwd(up, gpre, fconv_w, small["ffn_conv_b"], side=gather.take())
    w_down = _natural(gather.get("ffn_w_down"), "rows")
    f = _mm(fin, w_down, tm=1024, tn=1024, tk=2048, name="ffn_down")
    loss, dpre2, dpre2b, d_ln2_g, d_ln2_b = _ln_loss_bwd(x1, f, small["ln2_g"], small["ln2_b"], target)

    grads = {"ln2_g": d_ln2_g, "ln2_b": d_ln2_b}
    both = (F32, BF16)
    g32, g16 = _mm(fin, dpre2b, ta=True, tm=1024, tn=1024, tk=S, out_dtype=both, name="d_ffn_w_down")
    scatter.add_blocks("ffn_w_down", _blocks(g32, "rows"), _blocks(g16, "rows"))
    dfin = _mm(dpre2b, w_down, tb=True, tm=1024, tn=1024, tk=D, name="d_fin", side=scatter.take(57))
    dup, dgpre, grads["ffn_conv_w"], grads["ffn_conv_b"] = _ffn_act_bwd(
        dfin, up, gpre, fconv_w, small["ffn_conv_b"], side=scatter.take(85))
    g32, g16 = _mm(x1b, dup, ta=True, tm=1024, tn=768, tk=S, out_dtype=both, out_block=768, name="d_ffn_w_up",
                   side=scatter.take(57))
    scatter.add_blocks("ffn_w_up", g32, g16)
    g32, g16 = _mm(x1b, dgpre, ta=True, tm=1024, tn=768, tk=S, out_dtype=both, out_block=768, name="d_ffn_w_gate",
                   side=scatter.take(56))
    scatter.add_blocks("ffn_w_gate", g32, g16)
    dx1 = _mm(dup, w_up, tb=True, tm=1024, tn=1024, tk=768, b_block=768, name="d_x1_up", side=scatter.take(68))
    dx1 = _mm(dgpre, w_gate, tb=True, tm=1024, tn=1024, tk=768, b_block=768, add=dx1, name="d_x1_gate",
              side=scatter.take(70))
    dpre1, dpre1b, grads["ln1_g"], grads["ln1_b"] = _ln_bwd(dx1, dpre2, xhat1, rstd1, small["ln1_g"],
                                                            side=scatter.take(24))

    g32, g16 = _mm(mixin, dpre1b, ta=True, tm=1024, tn=1024, tk=S, out_dtype=both, name="d_w_out",
                   side=scatter.take(26))
    scatter.add_blocks("w_out", _blocks(g32, "rows"), _blocks(g16, "rows"))
    dmix = _mm(dpre1b, w_out, tb=True, tm=1024, tn=1024, tk=D, name="d_mixin", side=scatter.take(22))
    dya, dyr, dgl_a, dgl_r, db_a, db_r = _gate_bwd(dmix, y_attn, y_rnn, proj, small["b_gate"], side=scatter.take(36))
    grads["b_gate"] = jnp.concatenate([db_a, db_r], axis=1)
    g32, g16 = _mm(ot, dya, tm=1024, tn=1024, tk=S, out_dtype=both, name="d_w_attn_proj", side=scatter.take(38))
    scatter.add_blocks("w_attn_proj", _blocks(g32, "rows"), _blocks(g16, "rows"))
    g32, g16 = _mm(yrin, dyr, ta=True, tm=1280, tn=1024, tk=S, out_dtype=both, name="d_w_rnn_proj",
                   side=scatter.take(27))
    scatter.add_blocks("w_rnn_proj", _blocks(g32, "rows"), _blocks(g16, "rows"))
    dot_ = _mm(w_ap, dya, tb=True, tm=1024, tn=1024, tk=D, out_dtype=MXU_DTYPE, name="d_o", side=scatter.take(22))
    dyrin = _mm(dyr, w_rp, tb=True, tm=1024, tn=1280, tk=D, name="d_yrin", side=scatter.take(27))

    dry, dzr, dzi, drxc_in, grads["lru_ba"], grads["lru_bi"], grads["lru_lambda"] = _lru_scan_bwd(
        dyrin, proj, h, r, i, rxc, small["lru_lambda"], side=scatter.take(94))
    grads["lru_wa"], grads["lru_wi"] = _lru_gate_wgrad(rxc, dzr, dzi, side=scatter.take(22))
    drxc = _lru_gate_xgrad(dzr, dzi, small["lru_wa"], small["lru_wi"], drxc_in, side=scatter.take(33))
    drx, grads["rnn_conv_w"], grads["rnn_conv_b"] = _conv_bwd(drxc, proj, OFF_RX, rconv_w, tc=512,
                                                             name="rnn_conv_bwd", side=scatter.take(29))

    dqt, dk, dv, dsink = _attn_bwd(qt, kp, kt, vp, sink_row, dot_.reshape(N_KV, GROUP, HEAD_DIM, S),
                                   side=scatter.take(65))
    grads["attn_sinks"] = dsink.reshape(1, N_KV * GROUP)
    dproj = jnp.concatenate([
        dqt.reshape(D, S).T,
        _heads_minor(dk[:, BLOCK:, :]).astype(MXU_DTYPE),
        _heads_minor(dv[:, BLOCK:, :]).astype(MXU_DTYPE),
        drx, dry, dgl_a, dgl_r], axis=1)
    for part in range(W_IN_PARTS):
        cols = slice(part * (D // W_IN_PARTS), (part + 1) * (D // W_IN_PARTS))
        side = _join(scatter.take(55), sync.begin(loss, grads)) if part == 0 else scatter.take(68)
        g32, g16 = _mm(dproj, xb[:, cols], ta=True, tm=512, tn=D // W_IN_PARTS, tk=S, out_dtype=both,
                       name="d_w_in_%d" % part, side=side)
        scatter.add_blocks("w_in_%d" % part, _blocks(g32, "rows"), _blocks(g16, "rows"))
        scatter.flush_pairs("pairs_w_in_%d" % part)
    dx = _mm(dproj, w_in_t, tm=1024, tn=1024, tk=512, add=dpre1, add_scale=ALPHA, name="d_x",
             side=_join(scatter.take(400), sync.middle()))
    return dx


SHARDED = (
    ("w_in", "cols", 368), ("w_attn_proj", "rows", 32), ("w_rnn_proj", "rows", 32), ("w_out", "rows", 32),
    ("ffn_w_up", "cols", 128), ("ffn_w_gate", "cols", 128), ("ffn_w_down", "rows", 64),
)
SMALL_REPLICATED = ("b_gate", "rnn_conv_b", "lru_wa", "lru_ba", "lru_wi", "lru_bi", "lru_lambda", "attn_sinks",
                    "ln1_g", "ln1_b", "ffn_conv_b", "ln2_g", "ln2_b")
SMALL_SHARDED = ("rnn_conv_w", "ffn_conv_w")
SMALL_MATS = ("lru_wa", "lru_wi")
W_IN_PARTS = 2
WEIGHTS = ("w_in", "b_gate", "rnn_conv_w", "rnn_conv_b", "lru_wa", "lru_ba", "lru_wi", "lru_bi", "lru_lambda",
           "attn_sinks", "w_attn_proj", "w_rnn_proj", "w_out", "ln1_g", "ln1_b", "ffn_w_up", "ffn_w_gate",
           "ffn_conv_w", "ffn_conv_b", "ffn_w_down", "ln2_g", "ln2_b")


def kernel(x, w_in, b_gate, rnn_conv_w, rnn_conv_b, lru_wa, lru_ba, lru_wi, lru_bi, lru_lambda, attn_sinks, w_attn_proj, w_rnn_proj, w_out, ln1_g, ln1_b, ffn_w_up, ffn_w_gate, ffn_conv_w, ffn_conv_b, ffn_w_down, ln2_g, ln2_b, loss_target, m_w_in, m_b_gate, m_rnn_conv_w, m_rnn_conv_b, m_lru_wa, m_lru_ba, m_lru_wi, m_lru_bi, m_lru_lambda, m_attn_sinks, m_w_attn_proj, m_w_rnn_proj, m_w_out, m_ln1_g, m_ln1_b, m_ffn_w_up, m_ffn_w_gate, m_ffn_conv_w, m_ffn_conv_b, m_ffn_w_down, m_ln2_g, m_ln2_b, v_w_in, v_b_gate, v_rnn_conv_w, v_rnn_conv_b, v_lru_wa, v_lru_ba, v_lru_wi, v_lru_bi, v_lru_lambda, v_attn_sinks, v_w_attn_proj, v_w_rnn_proj, v_w_out, v_ln1_g, v_ln1_b, v_ffn_w_up, v_ffn_w_gate, v_ffn_conv_w, v_ffn_conv_b, v_ffn_w_down, v_ln2_g, v_ln2_b):
    given = dict(locals())
    wsh = {n: given[n][0] for n in WEIGHTS}
    msh = {n: given["m_" + n][0] for n in WEIGHTS}
    vsh = {n: given["v_" + n][0] for n in WEIGHTS}
    m_given = {n: given["m_" + n] for n in WEIGHTS}
    v_given = {n: given["v_" + n] for n in WEIGHTS}
    me = 4 * lax.axis_index("x") + 2 * lax.axis_index("y") + lax.axis_index("c")

    order = ("w_in", "rnn_conv_w", "ffn_conv_w", "w_attn_proj", "w_rnn_proj", "w_out", "ffn_w_up", "ffn_w_gate",
             "ffn_w_down")
    gather = _Gather({"w_in": wsh["w_in"].T.astype(MXU_DTYPE), **{n: wsh[n] for n in order[1:3]}})
    *casts, xb = _cast_many([wsh[n] for n in order[3:]] + [x[0]], side=gather.take(through="ffn_conv_w"))
    gather.add_shards(dict(zip(order[3:], casts)))
    small = {n: given[n] for n in SMALL_REPLICATED}
    small["lru_wa"] = _block_diag(wsh["lru_wa"])
    small["lru_wi"] = _block_diag(wsh["lru_wi"])
    scatter = _Scatter(me, jnp.stack([_index(_relative(k)) for k in FAR]).astype(jnp.int32))

    vec_names = tuple(n for n in SMALL_REPLICATED if n not in SMALL_MATS) + SMALL_SHARDED
    sync = _SmallSync(vec_names, SMALL_MATS)
    dx = _forward_backward(x[0], xb, loss_target[0], small, gather, scatter, sync)

    loss_total, g_small, mat_sums = sync.end()
    loss_total = loss_total.reshape(())
    for n in SMALL_SHARDED:
        width = wsh[n].shape[1]
        g_small[n] = lax.dynamic_slice_in_dim(g_small[n], me * width, width, axis=1)
    g_small = {n: g_small[n].reshape(given[n].shape) for n in vec_names}
    out = {}
    results = _adamw_many(*[[d[n] for n in vec_names] for d in (given, m_given, v_given, g_small)])
    for n, delta, nm, nv in zip(vec_names, *results):
        out[n] = (g_small[n], delta, nm, nv)
    for n in SMALL_MATS:
        g = mat_sums[n].reshape(given[n].shape)
        out[n] = (g, *_adamw_blocks(given[n], m_given[n], v_given[n], g, name="adamw_" + n))

    tile_rows = {n: tr for n, _, tr in SHARDED}
    me1 = me.reshape(1).astype(jnp.int32)
    res = None
    for n in list(scatter.sends):
        own, pair, far = scatter.get(n)
        if n.startswith("w_in_"):
            part = int(n[len("w_in_"):])
            w_t, m_t, v_t = (a["w_in"].transpose(0, 2, 1) for a in (given, m_given, v_given))
            res = _reduce_adamw(w_t, m_t, v_t, own, pair, far, me1, tr=tile_rows["w_in"], name="adamw_" + n,
                                part=part, earlier=res if part else None)
            out["w_in"] = tuple(r.transpose(0, 2, 1) for r in res)
        else:
            out[n] = tuple(_reduce_adamw(given[n], m_given[n], v_given[n], own, pair, far, me1, tr=tile_rows[n],
                                         name="adamw_" + n))

    outputs = [loss_total, dx[None]]
    for kind in range(4):
        outputs += [out[n][kind] for n in WEIGHTS]
    return tuple(outputs)
```

```python
import math

import jax
import jax.numpy as jnp
from jax import lax
from jax.experimental import pallas as pl
from jax.experimental.pallas import tpu as pltpu

F32 = jnp.float32
BF16 = jnp.bfloat16
MXU_DTYPE = jnp.bfloat16

N_DEV = 8
S = 2048
D = 2048
HEAD_DIM = 64
N_KV = 4
GROUP = 8
BLOCK = 128
D_KV = N_KV * HEAD_DIM
D_RNN = 2560
RNN_GROUP = 640
N_RNN_GROUPS = D_RNN // RNN_GROUP
RNN_BLOCK_W = 160
RNN_CONV_W = 4
LRU_C = 8.0
D_FF = 6144
FFN_CONV_W = 3
D_IN = 11776
OFF_K = 2048
OFF_V = 2304
OFF_RX = 2560
OFF_RY = 5120
OFF_GA = 7680
OFF_GR = 9728
LN_EPS = 1e-5
ALPHA = 2.0 ** 0.25
ADAM_LR = 0.001
ADAM_B1 = 0.9
ADAM_B2 = 0.999
ADAM_EPS = 1e-08
ADAM_WD = 0.01
ADAM_STEP = 10
NEG = -1e30
VMEM_LIMIT = 56 * 1024 * 1024
MID_RIDE_TENTHS = 5
MESH = pl.DeviceIdType.MESH
GELU_C = math.sqrt(2.0 / math.pi)


def _cparams(*sem):
    return pltpu.CompilerParams(dimension_semantics=sem or None, vmem_limit_bytes=VMEM_LIMIT)


def _call(body, *, name, grid, in_specs, out_specs, out_shape, operands, semantics, scratch_shapes=(), side=None):
    single = not isinstance(out_shape, (list, tuple))
    out_shape = [out_shape] if single else list(out_shape)
    out_specs = [out_specs] if single else list(out_specs)
    in_specs = list(in_specs)
    scratch_shapes = list(scratch_shapes)
    if side is None:
        res = pl.pallas_call(
            body, name=name, grid=grid, in_specs=in_specs, out_specs=out_specs, out_shape=out_shape,
            scratch_shapes=scratch_shapes, compiler_params=_cparams(*semantics))(*operands)
        return res[0] if single else res
    n_in, n_out, n_scr = len(in_specs), len(out_shape), len(scratch_shapes)
    s_in, s_out = len(side.operands), len(side.out_shape)
    hbm = pl.BlockSpec(memory_space=pltpu.HBM)
    steps = math.prod(grid)
    mid_step = (steps * MID_RIDE_TENTHS) // 10

    def with_copies(*refs):
        core_in, side_in = refs[:n_in], refs[n_in:n_in + s_in]
        o0 = n_in + s_in
        core_out, side_out = refs[o0:o0 + n_out], refs[o0 + n_out:o0 + n_out + s_out]
        c0 = o0 + n_out + s_out
        core_scr, sems = refs[c0:c0 + n_scr], refs[c0 + n_scr:]
        step = 0
        for d, size in enumerate(grid):
            step = step * size + pl.program_id(d)

        @pl.when(step == 0)
        def _():
            side.start(side_in, side_out, sems)

        body(*core_in, *core_out, *core_scr)

        @pl.when(step == mid_step)
        def _():
            side.mid(side_in, side_out, sems)

        @pl.when(step == steps - 1)
        def _():
            side.finish(side_in, side_out, sems)

    res = pl.pallas_call(
        with_copies, name=name, grid=grid,
        in_specs=in_specs + [hbm] * s_in, out_specs=out_specs + [hbm] * s_out,
        out_shape=out_shape + list(side.out_shape),
        scratch_shapes=scratch_shapes + list(side.sems),
        input_output_aliases={n_in + i: n_out + o for i, o in side.aliases.items()},
        compiler_params=_cparams(*(("arbitrary",) * len(grid))))(*operands, *side.operands)
    side.done(res[n_out:])
    return res[0] if single else res[:n_out]


def _run_side(side, name):
    def body(*refs):
        s_in, s_out = len(side.operands), len(side.out_shape)
        side.start(refs[:s_in], refs[s_in:s_in + s_out], refs[s_in + s_out:])
        side.mid(refs[:s_in], refs[s_in:s_in + s_out], refs[s_in + s_out:])
        side.finish(refs[:s_in], refs[s_in:s_in + s_out], refs[s_in + s_out:])

    hbm = pl.BlockSpec(memory_space=pltpu.HBM)
    res = pl.pallas_call(
        body, name=name, in_specs=[hbm] * len(side.operands), out_specs=[hbm] * len(side.out_shape),
        out_shape=list(side.out_shape), scratch_shapes=list(side.sems),
        input_output_aliases=dict(side.aliases))(*side.operands)
    side.done(res)


def _gelu(x):
    x2 = x * x
    t = jnp.tanh(GELU_C * (x + 0.044715 * x * x2))
    g = 0.5 * x * (1.0 + t)
    dg = 0.5 * (1.0 + t) + 0.5 * x * (1.0 - t * t) * (GELU_C * (1.0 + 3.0 * 0.044715 * x2))
    return g, dg


def _sigmoid(x):
    return 1.0 / (1.0 + jnp.exp(-x))


def _softplus(x):
    z = jnp.exp(-jnp.abs(x))
    small = z * (1.0 - z * (0.5 - z * (1.0 / 3.0 - 0.25 * z)))
    return jnp.maximum(x, 0.0) + jnp.where(z < 0.02, small, jnp.log(1.0 + z))


def _one_minus_exp(x):
    series = -x * (1.0 + x * (0.5 + x * (1.0 / 6.0 + x * (1.0 / 24.0))))
    return jnp.where(x > -0.03, series, 1.0 - jnp.exp(x))


def _colsum(v):
    return jnp.sum(v, axis=0, keepdims=True)


def _mm(a, b, *, tm, tn, tk, name, ta=False, tb=False, out_dtype=F32, b_block=None, out_block=None, add=None,
        add_scale=1.0, side=None):
    out_dtypes = out_dtype if isinstance(out_dtype, tuple) else (out_dtype,)
    if ta:
        k_dim, m_dim = a.shape
    else:
        m_dim, k_dim = a.shape
    if b_block is None:
        n_dim = b.shape[0] if tb else b.shape[1]
    else:
        n_dim = b.shape[1] if tb else b.shape[0] * b_block
    assert m_dim % tm == 0 and n_dim % tn == 0 and k_dim % tk == 0, (name, m_dim, n_dim, k_dim)
    nk = k_dim // tk
    dims = (((0 if ta else 1,), (1 if tb else 0,)), ((), ()))
    has_add = add is not None

    def body(*refs):
        a_ref, b_ref = refs[0], refs[1]
        add_ref = refs[2] if has_add else None
        first_out = 3 if has_add else 2
        o_refs = refs[first_out:first_out + len(out_dtypes)]

        def product():
            return lax.dot_general(a_ref[...].astype(MXU_DTYPE), b_ref[...].astype(MXU_DTYPE), dims,
                                   preferred_element_type=F32)

        def finish(acc):
            if has_add:
                acc = acc + add_scale * add_ref[...]
            for o_ref in o_refs:
                o_ref[...] = acc.astype(o_ref.dtype)

        if nk == 1:
            finish(product())
        else:
            acc_ref = refs[-1]
            k = pl.program_id(2)

            @pl.when(k == 0)
            def _():
                acc_ref[...] = jnp.zeros_like(acc_ref)

            acc_ref[...] += product()

            @pl.when(k == nk - 1)
            def _():
                finish(acc_ref[...])

    if ta:
        a_spec = pl.BlockSpec((tk, tm), lambda i, j, k: (k, i))
    else:
        a_spec = pl.BlockSpec((tm, tk), lambda i, j, k: (i, k))
    if b_block is None:
        if tb:
            b_spec = pl.BlockSpec((tn, tk), lambda i, j, k: (j, k))
        else:
            b_spec = pl.BlockSpec((tk, tn), lambda i, j, k: (k, j))
    elif tb:
        assert b_block % tk == 0
        b_spec = pl.BlockSpec((None, tn, tk), lambda i, j, k: ((k * tk) // b_block, j, ((k * tk) % b_block) // tk))
    else:
        assert b_block % tn == 0
        b_spec = pl.BlockSpec((None, tk, tn), lambda i, j, k: ((j * tn) // b_block, k, ((j * tn) % b_block) // tn))
    in_specs = [a_spec, b_spec]
    operands = [a, b]
    if has_add:
        in_specs.append(pl.BlockSpec((tm, tn), lambda i, j, k: (i, j)))
        operands.append(add)
    if out_block is None:
        out_spec = pl.BlockSpec((tm, tn), lambda i, j, k: (i, j))
        out_dims = (m_dim, n_dim)
    else:
        assert out_block % tn == 0
        out_spec = pl.BlockSpec((None, tm, tn), lambda i, j, k: ((j * tn) // out_block, i, ((j * tn) % out_block) // tn))
        out_dims = (n_dim // out_block, m_dim, out_block)
    res = _call(
        body,
        name=name,
        grid=(m_dim // tm, n_dim // tn, nk),
        in_specs=in_specs,
        out_specs=[out_spec] * len(out_dtypes),
        out_shape=[jax.ShapeDtypeStruct(out_dims, dt) for dt in out_dtypes],
        scratch_shapes=[pltpu.VMEM((tm, tn), F32)] if nk > 1 else [],
        semantics=("parallel", "parallel", "arbitrary"),
        operands=tuple(operands),
        side=side,
    )
    return res if isinstance(out_dtype, tuple) else res[0]


def _attn_bias(bias_ref, h):
    key = lax.broadcasted_iota(jnp.int32, (2 * BLOCK, GROUP * BLOCK), 0)
    col = lax.broadcasted_iota(jnp.int32, (2 * BLOCK, GROUP * BLOCK), 1)
    dist = BLOCK + (col & (BLOCK - 1)) - key
    head = h * GROUP + (col >> 7) + 1
    slope = jnp.exp(head.astype(F32) * (-0.25 * math.log(2.0)))
    bias = jnp.where((dist >= 0) & (dist < BLOCK), -slope * dist.astype(F32), NEG)
    bias_ref[1] = bias
    bias_ref[0] = jnp.where(key < BLOCK, NEG, bias)


def _attn_probs(kb, qt, bias, sink):
    s = jnp.dot(kb, qt, preferred_element_type=F32) * (HEAD_DIM ** -0.5) + bias
    m = jnp.maximum(jnp.max(s, axis=0, keepdims=True), sink)
    e = jnp.exp(s - m)
    e_sink = jnp.exp(sink - m)
    inv = 1.0 / (jnp.sum(e, axis=0, keepdims=True) + e_sink)
    return e * inv, e_sink * inv


def _heads_on_lanes(ref, r0):
    return jnp.concatenate([ref[g, :, pl.ds(r0, BLOCK)] for g in range(GROUP)], axis=1)


def _attn_fwd(qt, kp, vt, sink_row, side=None):
    cols = GROUP * BLOCK

    def body(q_ref, k_ref, vt_ref, sink_ref, o_ref, bias_ref):
        _attn_bias(bias_ref, pl.program_id(0))
        sink = sink_ref[...]

        def step(n, carry):
            r0 = pl.multiple_of(n * BLOCK, BLOCK)
            p, _ = _attn_probs(k_ref[pl.ds(r0, 2 * BLOCK), :], _heads_on_lanes(q_ref, r0),
                               bias_ref[jnp.minimum(n, 1)], sink)
            o = jnp.dot(vt_ref[:, pl.ds(r0, 2 * BLOCK)], p.astype(MXU_DTYPE), preferred_element_type=F32)
            for g in range(GROUP):
                o_ref[g, :, pl.ds(r0, BLOCK)] = o[:, g * BLOCK:(g + 1) * BLOCK].astype(o_ref.dtype)
            return carry

        lax.fori_loop(0, S // BLOCK, step, 0)

    hm = pl.BlockSpec((None, GROUP, HEAD_DIM, S), lambda h: (h, 0, 0, 0))
    return _call(
        body,
        name="attn_fwd",
        grid=(N_KV,),
        in_specs=[
            hm,
            pl.BlockSpec((None, BLOCK + S, HEAD_DIM), lambda h: (h, 0, 0)),
            pl.BlockSpec((None, HEAD_DIM, BLOCK + S), lambda h: (h, 0, 0)),
            pl.BlockSpec((None, 1, cols), lambda h: (h, 0, 0)),
        ],
        out_specs=hm,
        out_shape=jax.ShapeDtypeStruct((N_KV, GROUP, HEAD_DIM, S), MXU_DTYPE),
        scratch_shapes=[pltpu.VMEM((2, 2 * BLOCK, cols), F32)],
        semantics=("parallel",),
        operands=(qt, kp, vt, sink_row),
        side=side,
    )


def _attn_bwd(qt, kp, kt, vp, sink_row, dot_, side=None):
    cols = GROUP * BLOCK

    def body(q_ref, k_ref, kt_ref, v_ref, sink_ref, do_ref, dq_ref, dk_ref, dv_ref, dsink_ref, bias_ref):
        _attn_bias(bias_ref, pl.program_id(0))
        sink = sink_ref[...]
        dk_ref[...] = jnp.zeros_like(dk_ref)
        dv_ref[...] = jnp.zeros_like(dv_ref)
        nt = (((1,), (1,)), ((), ()))

        def step(n, sink_acc):
            r0 = pl.multiple_of(n * BLOCK, BLOCK)
            band = pl.ds(r0, 2 * BLOCK)
            qn = _heads_on_lanes(q_ref, r0)
            don = _heads_on_lanes(do_ref, r0)
            p, p_sink = _attn_probs(k_ref[band, :], qn, bias_ref[jnp.minimum(n, 1)], sink)
            dp = jnp.dot(v_ref[band, :], don, preferred_element_type=F32)
            delta = jnp.sum(p * dp, axis=0, keepdims=True)
            ds = (p * (dp - delta) * (HEAD_DIM ** -0.5)).astype(MXU_DTYPE)
            dq = jnp.dot(kt_ref[:, band], ds, preferred_element_type=F32)
            for g in range(GROUP):
                dq_ref[g, :, pl.ds(r0, BLOCK)] = dq[:, g * BLOCK:(g + 1) * BLOCK].astype(dq_ref.dtype)
            dk_ref[band, :] += lax.dot_general(ds, qn, nt, preferred_element_type=F32)
            dv_ref[band, :] += lax.dot_general(p.astype(MXU_DTYPE), don, nt, preferred_element_type=F32)
            return sink_acc - p_sink * delta

        sink_acc = lax.fori_loop(0, S // BLOCK, step, jnp.zeros((1, cols), F32))
        for g in range(GROUP):
            dsink_ref[g:g + 1, :] = jnp.sum(sink_acc[:, g * BLOCK:(g + 1) * BLOCK], axis=1, keepdims=True)

    hm = pl.BlockSpec((None, GROUP, HEAD_DIM, S), lambda h: (h, 0, 0, 0))
    kv = pl.BlockSpec((None, BLOCK + S, HEAD_DIM), lambda h: (h, 0, 0))
    return _call(
        body,
        name="attn_bwd",
        grid=(N_KV,),
        in_specs=[hm, kv, pl.BlockSpec((None, HEAD_DIM, BLOCK + S), lambda h: (h, 0, 0)), kv,
                  pl.BlockSpec((None, 1, cols), lambda h: (h, 0, 0)), hm],
        out_specs=[hm, kv, kv, pl.BlockSpec((None, GROUP, 1), lambda h: (h, 0, 0))],
        out_shape=[
            jax.ShapeDtypeStruct((N_KV, GROUP, HEAD_DIM, S), MXU_DTYPE),
            jax.ShapeDtypeStruct((N_KV, BLOCK + S, HEAD_DIM), F32),
            jax.ShapeDtypeStruct((N_KV, BLOCK + S, HEAD_DIM), F32),
            jax.ShapeDtypeStruct((N_KV, GROUP, 1), F32),
        ],
        scratch_shapes=[pltpu.VMEM((2, 2 * BLOCK, cols), F32)],
        semantics=("parallel",),
        operands=(qt, kp, kt, vp, sink_row, dot_),
        side=side,
    )


PAD = 8
CHUNK = 256


def _past_taps(xpad_ref, r0, width):
    ext = xpad_ref[pl.ds(r0, CHUNK + PAD), :]
    taps = []
    for k in range(width):
        back = width - 1 - k
        taps.append((ext if back == 0 else pltpu.roll(ext, back, 0))[PAD:, :])
    return taps


def _future_taps(xpad_ref, r0, width):
    ext = xpad_ref[pl.ds(r0, CHUNK + PAD), :]
    taps = []
    for ahead in range(width):
        taps.append((ext if ahead == 0 else pltpu.roll(ext, CHUNK + PAD - ahead, 0))[:CHUNK, :])
    return taps


def _conv_fwd(src, col0, w, b, *, tc, name, side=None):
    width, c_dim = w.shape

    def body(x_ref, w_ref, b_ref, o_ref, xpad_ref):
        xpad_ref[pl.ds(0, PAD), :] = jnp.zeros((PAD, tc), F32)
        xpad_ref[pl.ds(PAD, S), :] = x_ref[...]
        wv = w_ref[...]
        bv = b_ref[...]

        def step(ci, carry):
            r0 = pl.multiple_of(ci * CHUNK, CHUNK)
            taps = _past_taps(xpad_ref, r0, width)
            y = bv + taps[0] * wv[0:1, :]
            for k in range(1, width):
                y = y + taps[k] * wv[k:k + 1, :]
            o_ref[pl.ds(r0, CHUNK), :] = y
            return carry

        lax.fori_loop(0, S // CHUNK, step, 0)

    return _call(
        body,
        name=name,
        grid=(c_dim // tc,),
        in_specs=[
            pl.BlockSpec((S, tc), lambda j: (0, col0 // tc + j)),
            pl.BlockSpec((width, tc), lambda j: (0, j)),
            pl.BlockSpec((1, tc), lambda j: (0, j)),
        ],
        out_specs=pl.BlockSpec((S, tc), lambda j: (0, j)),
        out_shape=jax.ShapeDtypeStruct((S, c_dim), F32),
        scratch_shapes=[pltpu.VMEM((S + PAD, tc), F32)],
        semantics=("parallel",),
        operands=(src, w, b),
        side=side,
    )


def _conv_bwd(dy, src, col0, w, *, tc, name, side=None):
    width, c_dim = w.shape

    def body(dy_ref, x_ref, w_ref, dx_ref, dw_ref, db_ref, xpad_ref, dpad_ref):
        xpad_ref[pl.ds(0, PAD), :] = jnp.zeros((PAD, tc), F32)
        xpad_ref[pl.ds(PAD, S), :] = x_ref[...]
        dpad_ref[pl.ds(0, S), :] = dy_ref[...]
        dpad_ref[pl.ds(S, PAD), :] = jnp.zeros((PAD, tc), F32)
        wv = w_ref[...]

        def step(ci, acc):
            r0 = pl.multiple_of(ci * CHUNK, CHUNK)
            past = _past_taps(xpad_ref, r0, width)
            ahead = _future_taps(dpad_ref, r0, width)
            d = ahead[0]
            dx = d * wv[width - 1:width, :]
            for j in range(1, width):
                dx = dx + ahead[j] * wv[width - 1 - j:width - j, :]
            dx_ref[pl.ds(r0, CHUNK), :] = dx.astype(dx_ref.dtype)
            return tuple(acc[k] + _colsum(past[k] * d) for k in range(width)) + (acc[width] + _colsum(d),)

        zero = jnp.zeros((1, tc), F32)
        acc = lax.fori_loop(0, S // CHUNK, step, (zero,) * (width + 1))
        for k in range(width):
            dw_ref[k:k + 1, :] = acc[k]
        db_ref[...] = acc[width]

    return _call(
        body,
        name=name,
        grid=(c_dim // tc,),
        in_specs=[
            pl.BlockSpec((S, tc), lambda j: (0, j)),
            pl.BlockSpec((S, tc), lambda j: (0, col0 // tc + j)),
            pl.BlockSpec((width, tc), lambda j: (0, j)),
        ],
        out_specs=[
            pl.BlockSpec((S, tc), lambda j: (0, j)),
            pl.BlockSpec((width, tc), lambda j: (0, j)),
            pl.BlockSpec((1, tc), lambda j: (0, j)),
        ],
        out_shape=[
            jax.ShapeDtypeStruct((S, c_dim), MXU_DTYPE),
            jax.ShapeDtypeStruct((width, c_dim), F32),
            jax.ShapeDtypeStruct((1, c_dim), F32),
        ],
        scratch_shapes=[pltpu.VMEM((S + PAD, tc), F32), pltpu.VMEM((S + PAD, tc), F32)],
        semantics=("parallel",),
        operands=(dy, src, w),
        side=side,
    )


SCAN_TC = 256


def _lru_gates(rxc, wa, wi, ba, bi, side=None):
    tm = 512

    def body(x_ref, wa_ref, wi_ref, ba_ref, bi_ref, r_ref, i_ref):
        xv = x_ref[...].astype(MXU_DTYPE)
        r_ref[...] = _sigmoid(jnp.dot(xv, wa_ref[...].astype(MXU_DTYPE), preferred_element_type=F32) + ba_ref[...])
        i_ref[...] = _sigmoid(jnp.dot(xv, wi_ref[...].astype(MXU_DTYPE), preferred_element_type=F32) + bi_ref[...])

    x_spec = pl.BlockSpec((tm, RNN_GROUP), lambda g, i: (i, g))
    w_spec = pl.BlockSpec((None, RNN_GROUP, RNN_GROUP), lambda g, i: (g, 0, 0))
    b_spec = pl.BlockSpec((1, RNN_GROUP), lambda g, i: (0, g))
    return _call(
        body,
        name="lru_gates",
        grid=(N_RNN_GROUPS, S // tm),
        in_specs=[x_spec, w_spec, w_spec, b_spec, b_spec],
        out_specs=[x_spec, x_spec],
        out_shape=[jax.ShapeDtypeStruct((S, D_RNN), F32)] * 2,
        semantics=("parallel", "parallel"),
        operands=(rxc, wa, wi, ba, bi),
        side=side,
    )


def _scan_down(a, u, row):
    for d in (1, 2, 4):
        a_s = jnp.where(row >= d, pltpu.roll(a, d, 0), 1.0)
        u_s = jnp.where(row >= d, pltpu.roll(u, d, 0), 0.0)
        u = a * u_s + u
        a = a * a_s
    return a, u


def _scan_up(a, u, row):
    for d in (1, 2, 4):
        a_s = jnp.where(row < 8 - d, pltpu.roll(a, 8 - d, 0), 1.0)
        u_s = jnp.where(row < 8 - d, pltpu.roll(u, 8 - d, 0), 0.0)
        u = a * u_s + u
        a = a * a_s
    return a, u


def _lru_scan_fwd(r, i, rxc, proj, lam, side=None):
    tc = SCAN_TC

    def body(r_ref, i_ref, x_ref, ry_ref, lam_ref, h_ref, y_ref):
        rate = LRU_C * _softplus(-lam_ref[...])
        row = lax.broadcasted_iota(jnp.int32, (8, tc), 0)

        def step(ci, carry):
            r0 = pl.multiple_of(ci * 16, 16)
            log_a = -rate * r_ref[pl.ds(r0, 16), :]
            a16 = jnp.exp(log_a)
            u16 = jnp.sqrt(_one_minus_exp(2.0 * log_a)) * (i_ref[pl.ds(r0, 16), :] * x_ref[pl.ds(r0, 16), :])
            hs = []
            for half in range(2):
                a_cum, h0 = _scan_down(a16[8 * half:8 * half + 8, :], u16[8 * half:8 * half + 8, :], row)
                h = a_cum * carry + h0
                carry = jnp.broadcast_to(h[7:8, :], (8, tc))
                hs.append(h)
            h16 = jnp.concatenate(hs, axis=0)
            h_ref[pl.ds(r0, 16), :] = h16
            y_ref[pl.ds(r0, 16), :] = (h16 * _gelu(ry_ref[pl.ds(r0, 16), :])[0]).astype(y_ref.dtype)
            return carry

        lax.fori_loop(0, S // 16, step, jnp.zeros((8, tc), F32))

    col = pl.BlockSpec((S, tc), lambda j: (0, j))
    return _call(
        body,
        name="lru_scan_fwd",
        grid=(D_RNN // tc,),
        in_specs=[col, col, col, pl.BlockSpec((S, tc), lambda j: (0, OFF_RY // tc + j)),
                  pl.BlockSpec((1, tc), lambda j: (0, j))],
        out_specs=[col, col],
        out_shape=[jax.ShapeDtypeStruct((S, D_RNN), F32), jax.ShapeDtypeStruct((S, D_RNN), MXU_DTYPE)],
        semantics=("parallel",),
        operands=(r, i, rxc, proj, lam),
        side=side,
    )


def _lru_scan_bwd(dy, proj, h, r, i, rxc, lam, side=None):
    tc = SCAN_TC

    def body(dy_ref, ry_ref, h_ref, r_ref, i_ref, x_ref, lam_ref,
             dry_ref, dzr_ref, dzi_ref, dx_ref, dba_ref, dbi_ref, dlam_ref, a_ref, dh_ref, hp_ref):
        lam_v = lam_ref[...]
        rate = LRU_C * _softplus(-lam_v)
        dlam_scale = LRU_C * _sigmoid(-lam_v)
        row = lax.broadcasted_iota(jnp.int32, (8, tc), 0)
        hp_ref[pl.ds(0, PAD), :] = jnp.zeros((PAD, tc), F32)
        hp_ref[pl.ds(PAD, S), :] = h_ref[...]
        a_ref[pl.ds(S, PAD), :] = jnp.zeros((PAD, tc), F32)

        def prep(ci, carry):
            r0 = pl.multiple_of(ci * CHUNK, CHUNK)
            a_ref[pl.ds(r0, CHUNK), :] = jnp.exp(-rate * r_ref[pl.ds(r0, CHUNK), :])
            ge, dge = _gelu(ry_ref[pl.ds(r0, CHUNK), :])
            dyv = dy_ref[pl.ds(r0, CHUNK), :]
            dh_ref[pl.ds(r0, CHUNK), :] = dyv * ge
            dry_ref[pl.ds(r0, CHUNK), :] = (dyv * h_ref[pl.ds(r0, CHUNK), :] * dge).astype(dry_ref.dtype)
            return carry

        lax.fori_loop(0, S // CHUNK, prep, 0)

        def step(ci, state):
            carry, dba, dbi, dlam = state
            r0 = pl.multiple_of(S - 16 - ci * 16, 16)
            a_ext = a_ref[pl.ds(r0, 24), :]
            a_next = pltpu.roll(a_ext, 23, 0)
            h_prev = pltpu.roll(hp_ref[pl.ds(r0, 24), :], 1, 0)
            dh16 = dh_ref[pl.ds(r0, 16), :]
            gs = [None, None]
            for half in (1, 0):
                lo = 8 * half
                c_cum, g0 = _scan_up(a_next[lo:lo + 8, :], dh16[lo:lo + 8, :], row)
                g = c_cum * carry + g0
                carry = jnp.broadcast_to(g[0:1, :], (8, tc))
                gs[half] = g
            g16 = jnp.concatenate(gs, axis=0)
            a16 = a_ext[0:16, :]
            r16 = r_ref[pl.ds(r0, 16), :]
            i16 = i_ref[pl.ds(r0, 16), :]
            x16 = x_ref[pl.ds(r0, 16), :]
            a2 = a16 * a16
            sq = jnp.sqrt(_one_minus_exp(-2.0 * rate * r16))
            dx_ref[pl.ds(r0, 16), :] = g16 * sq * i16
            dzi = g16 * sq * x16 * i16 * (1.0 - i16)
            dlog_a = g16 * h_prev[8:24, :] * a16 - g16 * i16 * x16 * a2 / sq
            dzr = -rate * dlog_a * r16 * (1.0 - r16)
            dzr_ref[pl.ds(r0, 16), :] = dzr.astype(dzr_ref.dtype)
            dzi_ref[pl.ds(r0, 16), :] = dzi.astype(dzi_ref.dtype)
            return carry, dba + _colsum(dzr), dbi + _colsum(dzi), dlam + _colsum(dlog_a * r16)

        zero = jnp.zeros((1, tc), F32)
        _, dba, dbi, dlam = lax.fori_loop(0, S // 16, step, (jnp.zeros((8, tc), F32), zero, zero, zero))
        dba_ref[...] = dba
        dbi_ref[...] = dbi
        dlam_ref[...] = dlam * dlam_scale

    col = pl.BlockSpec((S, tc), lambda j: (0, j))
    vec = pl.BlockSpec((1, tc), lambda j: (0, j))
    return _call(
        body,
        name="lru_scan_bwd",
        grid=(D_RNN // tc,),
        in_specs=[col, pl.BlockSpec((S, tc), lambda j: (0, OFF_RY // tc + j)), col, col, col, col, vec],
        out_specs=[col, col, col, col, vec, vec, vec],
        out_shape=[jax.ShapeDtypeStruct((S, D_RNN), MXU_DTYPE)] * 3 + [jax.ShapeDtypeStruct((S, D_RNN), F32)]
        + [jax.ShapeDtypeStruct((1, D_RNN), F32)] * 3,
        scratch_shapes=[pltpu.VMEM((S + PAD, tc), F32), pltpu.VMEM((S, tc), F32), pltpu.VMEM((S + PAD, tc), F32)],
        semantics=("parallel",),
        operands=(dy, proj, h, r, i, rxc, lam),
        side=side,
    )


def _lru_gate_wgrad(rxc, dzr, dzi, side=None):
    def body(x_ref, dzr_ref, dzi_ref, dwa_ref, dwi_ref):
        xv = x_ref[...].astype(MXU_DTYPE)
        dims = (((0,), (0,)), ((), ()))
        dwa_ref[...] = lax.dot_general(xv, dzr_ref[...], dims, preferred_element_type=F32)
        dwi_ref[...] = lax.dot_general(xv, dzi_ref[...], dims, preferred_element_type=F32)

    col = pl.BlockSpec((S, RNN_GROUP), lambda g: (0, g))
    w_spec = pl.BlockSpec((None, RNN_GROUP, RNN_GROUP), lambda g: (g, 0, 0))
    return _call(
        body,
        name="lru_gate_wgrad",
        grid=(N_RNN_GROUPS,),
        in_specs=[col, col, col],
        out_specs=[w_spec, w_spec],
        out_shape=[jax.ShapeDtypeStruct((N_RNN_GROUPS, RNN_GROUP, RNN_GROUP), F32)] * 2,
        semantics=("parallel",),
        operands=(rxc, dzr, dzi),
        side=side,
    )


def _lru_gate_xgrad(dzr, dzi, wa, wi, dx_in, side=None):
    tm = 512

    def body(dzr_ref, dzi_ref, wa_ref, wi_ref, dx_ref, o_ref):
        dims = (((1,), (1,)), ((), ()))
        o_ref[...] = (dx_ref[...]
                      + lax.dot_general(dzr_ref[...], wa_ref[...].astype(MXU_DTYPE), dims, preferred_element_type=F32)
                      + lax.dot_general(dzi_ref[...], wi_ref[...].astype(MXU_DTYPE), dims, preferred_element_type=F32))

    x_spec = pl.BlockSpec((tm, RNN_GROUP), lambda g, i: (i, g))
    w_spec = pl.BlockSpec((None, RNN_GROUP, RNN_GROUP), lambda g, i: (g, 0, 0))
    return _call(
        body,
        name="lru_gate_xgrad",
        grid=(N_RNN_GROUPS, S // tm),
        in_specs=[x_spec, x_spec, w_spec, w_spec, x_spec],
        out_specs=x_spec,
        out_shape=jax.ShapeDtypeStruct((S, D_RNN), F32),
        semantics=("parallel", "parallel"),
        operands=(dzr, dzi, wa, wi, dx_in),
        side=side,
    )


def _gate_fwd(y_attn, y_rnn, proj, b_gate, side=None):
    t = 512

    def body(ya_ref, yr_ref, ga_ref, gr_ref, ba_ref, br_ref, o_ref):
        o_ref[...] = (_sigmoid(ga_ref[...] + ba_ref[...]) * ya_ref[...]
                      + _sigmoid(gr_ref[...] + br_ref[...]) * yr_ref[...]).astype(o_ref.dtype)

    tile = pl.BlockSpec((t, t), lambda i, j: (i, j))
    return _call(
        body,
        name="gate_fwd",
        grid=(S // t, D // t),
        in_specs=[tile, tile,
                  pl.BlockSpec((t, t), lambda i, j: (i, OFF_GA // t + j)),
                  pl.BlockSpec((t, t), lambda i, j: (i, OFF_GR // t + j)),
                  pl.BlockSpec((1, t), lambda i, j: (0, j)),
                  pl.BlockSpec((1, t), lambda i, j: (0, D // t + j))],
        out_specs=tile,
        out_shape=jax.ShapeDtypeStruct((S, D), MXU_DTYPE),
        semantics=("parallel", "parallel"),
        operands=(y_attn, y_rnn, proj, proj, b_gate, b_gate),
        side=side,
    )


def _gate_bwd(dmix, y_attn, y_rnn, proj, b_gate, side=None):
    t = 512

    def body(dm_ref, ya_ref, yr_ref, ga_ref, gr_ref, ba_ref, br_ref,
             dya_ref, dyr_ref, dga_ref, dgr_ref, dba_ref, dbr_ref):
        @pl.when(pl.program_id(1) == 0)
        def _():
            dba_ref[...] = jnp.zeros_like(dba_ref)
            dbr_ref[...] = jnp.zeros_like(dbr_ref)

        dm = dm_ref[...]
        ga = _sigmoid(ga_ref[...] + ba_ref[...])
        gr = _sigmoid(gr_ref[...] + br_ref[...])
        dya_ref[...] = (dm * ga).astype(dya_ref.dtype)
        dyr_ref[...] = (dm * gr).astype(dyr_ref.dtype)
        dga = dm * ya_ref[...] * ga * (1.0 - ga)
        dgr = dm * yr_ref[...] * gr * (1.0 - gr)
        dga_ref[...] = dga.astype(dga_ref.dtype)
        dgr_ref[...] = dgr.astype(dgr_ref.dtype)
        dba_ref[...] += _colsum(dga)
        dbr_ref[...] += _colsum(dgr)

    tile = pl.BlockSpec((t, t), lambda j, i: (i, j))
    vec = pl.BlockSpec((1, t), lambda j, i: (0, j))
    return _call(
        body,
        name="gate_bwd",
        grid=(D // t, S // t),
        in_specs=[tile, tile, tile,
                  pl.BlockSpec((t, t), lambda j, i: (i, OFF_GA // t + j)),
                  pl.BlockSpec((t, t), lambda j, i: (i, OFF_GR // t + j)),
                  vec,
                  pl.BlockSpec((1, t), lambda j, i: (0, D // t + j))],
        out_specs=[tile, tile, tile, tile, vec, vec],
        out_shape=[jax.ShapeDtypeStruct((S, D), MXU_DTYPE)] * 4 + [jax.ShapeDtypeStruct((1, D), F32)] * 2,
        semantics=("parallel", "arbitrary"),
        operands=(dmix, y_attn, y_rnn, proj, proj, b_gate, b_gate),
        side=side,
    )


LN_TM = 256


def _ln_stats(pre):
    mu = jnp.mean(pre, axis=-1, keepdims=True)
    xc = pre - mu
    rstd = lax.rsqrt(jnp.mean(xc * xc, axis=-1, keepdims=True) + LN_EPS)
    return xc * rstd, rstd


def _ln_input_grad(dy, xhat, rstd, g):
    dyg = dy * g
    return rstd * (dyg - jnp.mean(dyg, axis=-1, keepdims=True)
                   - xhat * jnp.mean(dyg * xhat, axis=-1, keepdims=True))


def _ln_fwd(res, branch, g, b, side=None):
    def body(res_ref, br_ref, g_ref, b_ref, y_ref, yb_ref, xhat_ref, rstd_ref):
        xhat, rstd = _ln_stats(ALPHA * res_ref[...] + br_ref[...])
        y = xhat * g_ref[...] + b_ref[...]
        y_ref[...] = y
        yb_ref[...] = y.astype(yb_ref.dtype)
        xhat_ref[...] = xhat
        rstd_ref[...] = rstd

    tile = pl.BlockSpec((LN_TM, D), lambda i: (i, 0))
    vec = pl.BlockSpec((1, D), lambda i: (0, 0))
    return _call(
        body,
        name="ln_fwd",
        grid=(S // LN_TM,),
        in_specs=[tile, tile, vec, vec],
        out_specs=[tile, tile, tile, pl.BlockSpec((LN_TM, 1), lambda i: (i, 0))],
        out_shape=[jax.ShapeDtypeStruct((S, D), F32), jax.ShapeDtypeStruct((S, D), MXU_DTYPE),
                   jax.ShapeDtypeStruct((S, D), F32), jax.ShapeDtypeStruct((S, 1), F32)],
        semantics=("parallel",),
        operands=(res, branch, g, b),
        side=side,
    )


def _ln_bwd(dy_a, dy_b, xhat, rstd, g, side=None):
    def body(da_ref, db_in_ref, xhat_ref, rstd_ref, g_ref, dp_ref, dpb_ref, dg_ref, db_ref):
        @pl.when(pl.program_id(0) == 0)
        def _():
            dg_ref[...] = jnp.zeros_like(dg_ref)
            db_ref[...] = jnp.zeros_like(db_ref)

        dy = da_ref[...] + ALPHA * db_in_ref[...]
        xhat = xhat_ref[...]
        dp = _ln_input_grad(dy, xhat, rstd_ref[...], g_ref[...])
        dp_ref[...] = dp
        dpb_ref[...] = dp.astype(dpb_ref.dtype)
        dg_ref[...] += _colsum(dy * xhat)
        db_ref[...] += _colsum(dy)

    tile = pl.BlockSpec((LN_TM, D), lambda i: (i, 0))
    vec = pl.BlockSpec((1, D), lambda i: (0, 0))
    return _call(
        body,
        name="ln_bwd",
        grid=(S // LN_TM,),
        in_specs=[tile, tile, tile, pl.BlockSpec((LN_TM, 1), lambda i: (i, 0)), vec],
        out_specs=[tile, tile, vec, vec],
        out_shape=[jax.ShapeDtypeStruct((S, D), F32), jax.ShapeDtypeStruct((S, D), MXU_DTYPE),
                   jax.ShapeDtypeStruct((1, D), F32), jax.ShapeDtypeStruct((1, D), F32)],
        semantics=("arbitrary",),
        operands=(dy_a, dy_b, xhat, rstd, g),
        side=side,
    )


def _ln_loss_bwd(res, branch, g, b, target, side=None):
    def body(res_ref, br_ref, g_ref, b_ref, t_ref, loss_ref, dp_ref, dpb_ref, dg_ref, db_ref):
        @pl.when(pl.program_id(0) == 0)
        def _():
            loss_ref[...] = jnp.zeros_like(loss_ref)
            dg_ref[...] = jnp.zeros_like(dg_ref)
            db_ref[...] = jnp.zeros_like(db_ref)

        xhat, rstd = _ln_stats(ALPHA * res_ref[...] + br_ref[...])
        gv = g_ref[...]
        err = xhat * gv + b_ref[...] - t_ref[...]
        loss_ref[...] += (0.5 / D) * jnp.sum(_colsum(err * err), axis=1, keepdims=True)
        dy = err * (1.0 / D)
        dp = _ln_input_grad(dy, xhat, rstd, gv)
        dp_ref[...] = dp
        dpb_ref[...] = dp.astype(dpb_ref.dtype)
        dg_ref[...] += _colsum(dy * xhat)
        db_ref[...] += _colsum(dy)

    tile = pl.BlockSpec((LN_TM, D), lambda i: (i, 0))
    vec = pl.BlockSpec((1, D), lambda i: (0, 0))
    return _call(
        body,
        name="ln_loss_bwd",
        grid=(S // LN_TM,),
        in_specs=[tile, tile, vec, vec, tile],
        out_specs=[pl.BlockSpec((1, 1), lambda i: (0, 0)), tile, tile, vec, vec],
        out_shape=[jax.ShapeDtypeStruct((1, 1), F32), jax.ShapeDtypeStruct((S, D), F32),
                   jax.ShapeDtypeStruct((S, D), MXU_DTYPE),
                   jax.ShapeDtypeStruct((1, D), F32), jax.ShapeDtypeStruct((1, D), F32)],
        semantics=("arbitrary",),
        operands=(res, branch, g, b, target),
        side=side,
    )


FFN_TC = 256


def _ffn_act_fwd(up, gpre, w, b, side=None):
    tc = FFN_TC

    def body(up_ref, x_ref, w_ref, b_ref, o_ref, gate_ref, xpad_ref):
        xpad_ref[pl.ds(0, PAD), :] = jnp.zeros((PAD, tc), F32)
        xpad_ref[pl.ds(PAD, S), :] = x_ref[...]
        wv = w_ref[...]
        bv = b_ref[...]

        def step(ci, carry):
            r0 = pl.multiple_of(ci * CHUNK, CHUNK)
            taps = _past_taps(xpad_ref, r0, FFN_CONV_W)
            gate = bv + taps[0] * wv[0:1, :] + taps[1] * wv[1:2, :] + taps[2] * wv[2:3, :]
            gate_ref[pl.ds(r0, CHUNK), :] = gate
            o_ref[pl.ds(r0, CHUNK), :] = (_gelu(gate)[0] * up_ref[pl.ds(r0, CHUNK), :]).astype(o_ref.dtype)
            return carry

        lax.fori_loop(0, S // CHUNK, step, 0)

    col = pl.BlockSpec((S, tc), lambda j: (0, j))
    return _call(
        body,
        name="ffn_act_fwd",
        grid=(D_FF // tc,),
        in_specs=[col, col, pl.BlockSpec((FFN_CONV_W, tc), lambda j: (0, j)), pl.BlockSpec((1, tc), lambda j: (0, j))],
        out_specs=[col, col],
        out_shape=[jax.ShapeDtypeStruct((S, D_FF), MXU_DTYPE), jax.ShapeDtypeStruct((S, D_FF), F32)],
        scratch_shapes=[pltpu.VMEM((S + PAD, tc), F32)],
        semantics=("parallel",),
        operands=(up, gpre, w, b),
        side=side,
    )


def _ffn_act_bwd(dfin, up, gpre, gate, w, side=None):
    tc = FFN_TC
    width = FFN_CONV_W

    def body(df_ref, up_ref, x_ref, gate_ref, w_ref, dup_ref, dx_ref, dw_ref, db_ref, dpad_ref):
        dpad_ref[pl.ds(S, PAD), :] = jnp.zeros((PAD, tc), F32)
        wv = w_ref[...]

        def gate_grad(ci, db):
            r0 = pl.multiple_of(ci * CHUNK, CHUNK)
            ge, dge = _gelu(gate_ref[pl.ds(r0, CHUNK), :])
            df = df_ref[pl.ds(r0, CHUNK), :]
            dup_ref[pl.ds(r0, CHUNK), :] = (df * ge).astype(dup_ref.dtype)
            d = df * up_ref[pl.ds(r0, CHUNK), :] * dge
            dpad_ref[pl.ds(r0, CHUNK), :] = d
            return db + _colsum(d)

        db_ref[...] = lax.fori_loop(0, S // CHUNK, gate_grad, jnp.zeros((1, tc), F32))

        def input_grad(ci, acc):
            r0 = pl.multiple_of(ci * CHUNK, CHUNK)
            ahead = _future_taps(dpad_ref, r0, width)
            dx = ahead[0] * wv[2:3, :] + ahead[1] * wv[1:2, :] + ahead[2] * wv[0:1, :]
            dx_ref[pl.ds(r0, CHUNK), :] = dx.astype(dx_ref.dtype)
            x = x_ref[pl.ds(r0, CHUNK), :]
            return tuple(acc[k] + _colsum(x * ahead[width - 1 - k]) for k in range(width))

        zero = jnp.zeros((1, tc), F32)
        acc = lax.fori_loop(0, S // CHUNK, input_grad, (zero,) * width)
        for k in range(width):
            dw_ref[k:k + 1, :] = acc[k]

    col = pl.BlockSpec((S, tc), lambda j: (0, j))
    w_spec = pl.BlockSpec((width, tc), lambda j: (0, j))
    vec = pl.BlockSpec((1, tc), lambda j: (0, j))
    return _call(
        body,
        name="ffn_act_bwd",
        grid=(D_FF // tc,),
        in_specs=[col, col, col, col, w_spec],
        out_specs=[col, col, w_spec, vec],
        out_shape=[jax.ShapeDtypeStruct((S, D_FF), MXU_DTYPE)] * 2
        + [jax.ShapeDtypeStruct((width, D_FF), F32), jax.ShapeDtypeStruct((1, D_FF), F32)],
        scratch_shapes=[pltpu.VMEM((S + PAD, tc), F32)],
        semantics=("parallel",),
        operands=(dfin, up, gpre, gate, w),
        side=side,
    )


def _adamw_update(w, g, m, v):
    m = ADAM_B1 * m + (1.0 - ADAM_B1) * g
    v = ADAM_B2 * v + (1.0 - ADAM_B2) * (g * g)
    m_hat = m / (1.0 - ADAM_B1 ** ADAM_STEP)
    v_hat = v / (1.0 - ADAM_B2 ** ADAM_STEP)
    delta = -ADAM_LR * (m_hat / (jnp.sqrt(v_hat) + ADAM_EPS) + ADAM_WD * w)
    return delta, m, v


def _add_pairs(send, pair, far_index, *, name):
    _, r_dim, c_dim = send.shape
    tr = r_dim // 4

    def body(far_ref, mine_ref, theirs_ref, o_ref):
        o_ref[...] = (mine_ref[...].astype(F32) + theirs_ref[...].astype(F32)).astype(o_ref.dtype)

    return pl.pallas_call(
        body,
        name=name,
        grid_spec=pltpu.PrefetchScalarGridSpec(
            num_scalar_prefetch=1,
            grid=(3, r_dim // tr),
            in_specs=[pl.BlockSpec((None, tr, c_dim), lambda j, i, far: (far[j], i, 0)),
                      pl.BlockSpec((None, tr, c_dim), lambda j, i, far: (1 + j, i, 0))],
            out_specs=pl.BlockSpec((None, tr, c_dim), lambda j, i, far: (j, i, 0)),
        ),
        out_shape=jax.ShapeDtypeStruct((3, r_dim, c_dim), BF16),
        compiler_params=_cparams("parallel", "parallel"),
    )(far_index, send, pair)


def _reduce_adamw(w, m, v, g_own, pair, far, me, *, tr, name, part=0, earlier=None):
    _, r_dim, c_dim = w.shape
    cp = pair.shape[2]

    def body(me_ref, w_ref, m_ref, v_ref, g_ref, pair_ref, far_ref, *refs):
        grad_ref, delta_ref, nm_ref, nv_ref = refs[-4:]
        g = g_ref[...] + pair_ref[...].astype(F32)
        for j in range(3):
            g = g + far_ref[j].astype(F32)
        delta, nm, nv = _adamw_update(w_ref[...], g, m_ref[...], v_ref[...])
        grad_ref[...] = g
        delta_ref[...] = delta
        nm_ref[...] = nm
        nv_ref[...] = nv

    tile = pl.BlockSpec((None, tr, cp), lambda i, me: (0, i, part))
    if g_own.ndim == 3:
        own_spec = pl.BlockSpec((None, tr, cp), lambda i, me: (me[0], i, 0))
    else:
        own_spec = pl.BlockSpec((tr, cp), lambda i, me: (i, 0))
    earlier = list(earlier or ())
    return pl.pallas_call(
        body,
        name=name,
        grid_spec=pltpu.PrefetchScalarGridSpec(
            num_scalar_prefetch=1,
            grid=(r_dim // tr,),
            in_specs=[tile, tile, tile, own_spec, pl.BlockSpec((None, tr, cp), lambda i, me: (0, i, 0)),
                      pl.BlockSpec((3, tr, cp), lambda i, me: (0, i, 0))]
            + [pl.BlockSpec(memory_space=pl.ANY)] * len(earlier),
            out_specs=[tile] * 4,
        ),
        out_shape=[jax.ShapeDtypeStruct((1, r_dim, c_dim), F32)] * 4,
        input_output_aliases={7 + k: k for k in range(len(earlier))},
        compiler_params=_cparams("parallel"),
    )(me, w, m, v, g_own, pair, far, *earlier)


def _adamw_many(ws, ms, vs, gs):
    n = len(ws)

    def body(*refs):
        for i in range(n):
            delta, nm, nv = _adamw_update(refs[i][...], refs[3 * n + i][...], refs[n + i][...], refs[2 * n + i][...])
            refs[4 * n + i][...] = delta
            refs[5 * n + i][...] = nm
            refs[6 * n + i][...] = nv

    vmem = pl.BlockSpec(memory_space=pltpu.VMEM)
    res = pl.pallas_call(
        body,
        name="adamw_small",
        in_specs=[vmem] * (4 * n),
        out_specs=[vmem] * (3 * n),
        out_shape=[jax.ShapeDtypeStruct(w.shape, F32) for w in ws] * 3,
        compiler_params=pltpu.CompilerParams(vmem_limit_bytes=VMEM_LIMIT),
    )(*ws, *ms, *vs, *gs)
    return res[:n], res[n:2 * n], res[2 * n:]


def _adamw_blocks(w, m, v, g, *, name, side=None):
    per = 2

    def body(w_ref, m_ref, v_ref, g_ref, delta_ref, nm_ref, nv_ref):
        delta, nm, nv = _adamw_update(w_ref[...], g_ref[...], m_ref[...], v_ref[...])
        delta_ref[...] = delta
        nm_ref[...] = nm
        nv_ref[...] = nv

    tile = pl.BlockSpec((1, per) + w.shape[2:], lambda i: (0, i, 0, 0))
    return _call(
        body,
        name=name,
        grid=(w.shape[1] // per,),
        in_specs=[tile] * 4,
        out_specs=[tile] * 3,
        out_shape=[jax.ShapeDtypeStruct(w.shape, F32)] * 3,
        semantics=("parallel",),
        operands=(w, m, v, g),
        side=side,
    )


def _coords():
    return lax.axis_index("x"), lax.axis_index("y"), lax.axis_index("c")


def _flip(coord, bit):
    return 1 - coord if bit else coord


def _relative(k):
    x, y, c = _coords()
    return _flip(x, k & 4), _flip(y, k & 2), _flip(c, k & 1)


def _index(pos):
    return 4 * pos[0] + 2 * pos[1] + pos[2]


FAR = (4, 2, 6)
AG_US_PER_MB = 38.0
RS_US_PER_MB = 46.0
MIN_RIDE_US = 30.0
PAIR_EXCHANGE_US = 20.0
MIN_GATHER_RIDE_US = 22.0
ROW_ALIGN = 32


def _chunks(items, cursor, us, us_per_mb, through=None):
    budget = float("inf") if us is None else us / us_per_mb * 2 ** 20
    names = list(items)
    if through is not None:
        names = names[:names.index(through) + 1]
    chunks = []
    for name in names:
        arr = items[name]
        r_dim, c_dim = arr.shape[-2:]
        row_bytes = c_dim * arr.dtype.itemsize
        while cursor[name] < r_dim and budget > 0:
            rows = r_dim - cursor[name]
            if r_dim > ROW_ALIGN and budget < rows * row_bytes:
                rows = min(rows, max(ROW_ALIGN, int(budget // row_bytes) // ROW_ALIGN * ROW_ALIGN))
            chunks.append((name, cursor[name], rows))
            cursor[name] += rows
            budget -= rows * row_bytes
    return chunks


class _Gather:
    def __init__(self, shards):
        self.shards, self.bufs, self.cursor = {}, {}, {}
        self.add_shards(shards)

    def add_shards(self, shards):
        for n, shard in shards.items():
            self.shards[n], self.bufs[n], self.cursor[n] = shard, None, 0

    def take(self, us=None, through=None):
        if us is not None and us < MIN_GATHER_RIDE_US:
            return None
        chunks = _chunks(self.shards, self.cursor, us, AG_US_PER_MB, through)
        return _GatherSide(self, chunks) if chunks else None

    def get(self, name):
        chunks = _chunks(self.shards, self.cursor, None, AG_US_PER_MB, through=name)
        if chunks:
            _run_side(_GatherSide(self, chunks), "gather_" + name)
        return self.bufs[name]


class _GatherSide:
    SEMS = 8

    def __init__(self, owner, chunks):
        self.owner, self.chunks = owner, chunks
        self.names = list(dict.fromkeys(n for n, _, _ in chunks))
        old = [n for n in self.names if owner.bufs[n] is not None]
        self.operands = [owner.shards[n] for n in self.names] + [owner.bufs[n] for n in old]
        self.out_shape = [jax.ShapeDtypeStruct((N_DEV,) + owner.shards[n].shape, owner.shards[n].dtype)
                          for n in self.names]
        self.aliases = {len(self.names) + i: self.names.index(n) for i, n in enumerate(old)}
        self.sems = [pltpu.SemaphoreType.DMA((self.SEMS * len(chunks),)),
                     pltpu.SemaphoreType.DMA((self.SEMS * len(chunks),)), pltpu.SemaphoreType.DMA((len(chunks),))]

    def _halves(self, ci):
        _, r0, rows = self.chunks[ci]
        if rows % ROW_ALIGN:
            return None
        return (r0, rows // 2), (r0 + rows // 2, rows // 2)

    def _copy(self, ins, outs, sems, ci, s, block, to, rows=None, from_shard=False):
        name, r0, n = self.chunks[ci]
        if rows is not None:
            r0, n = rows
        w = self.names.index(name)
        slot = outs[w].at[_index(block), pl.ds(r0, n)]
        return pltpu.make_async_remote_copy(
            src_ref=ins[w].at[pl.ds(r0, n)] if from_shard else slot, dst_ref=slot,
            send_sem=sems[0].at[self.SEMS * ci + s], recv_sem=sems[1].at[self.SEMS * ci + s],
            device_id=to, device_id_type=MESH)

    def _own(self, ins, outs, sems, ci):
        name, r0, rows = self.chunks[ci]
        w = self.names.index(name)
        return pltpu.make_async_copy(ins[w].at[pl.ds(r0, rows)], outs[w].at[_index(_relative(0)), pl.ds(r0, rows)],
                                     sems[2].at[ci])

    def _pass(self, ins, outs, sems, ci, which):
        source, target = ((4, 2), (2, 4))[which]
        return self._copy(ins, outs, sems, ci, 3 + which, _relative(source), _relative(target),
                          rows=self._halves(ci)[which])

    def start(self, ins, outs, sems):
        me = _relative(0)
        for ci in range(len(self.chunks)):
            self._own(ins, outs, sems, ci).start()
        for ci in range(len(self.chunks)):
            self._copy(ins, outs, sems, ci, 1, me, _relative(4), from_shard=True).start()
            self._copy(ins, outs, sems, ci, 2, me, _relative(2), from_shard=True).start()
            if self._halves(ci) is None:
                self._copy(ins, outs, sems, ci, 3, me, _relative(6), from_shard=True).start()
        for ci in range(len(self.chunks)):
            self._copy(ins, outs, sems, ci, 0, me, _relative(1), from_shard=True).start()

    def mid(self, ins, outs, sems):
        me = _relative(0)
        cut = [ci for ci in range(len(self.chunks)) if self._halves(ci) is not None]
        for ci in cut:
            self._copy(ins, outs, sems, ci, 1, _relative(4), me).wait_recv()
            self._pass(ins, outs, sems, ci, 0).start()
            self._copy(ins, outs, sems, ci, 5, _relative(4), _relative(1)).start()
        for ci in cut:
            self._copy(ins, outs, sems, ci, 2, _relative(2), me).wait_recv()
            self._pass(ins, outs, sems, ci, 1).start()
            self._copy(ins, outs, sems, ci, 6, _relative(2), _relative(1)).start()

    def finish(self, ins, outs, sems):
        me, sibling = _relative(0), _relative(1)
        n = len(self.chunks)
        for ci in range(n):
            if self._halves(ci) is None:
                for s, k in ((1, 4), (2, 2), (3, 6)):
                    self._copy(ins, outs, sems, ci, s, _relative(k), me).wait_recv()
                for j, k in enumerate(FAR):
                    self._copy(ins, outs, sems, ci, 5 + j, _relative(k), sibling).start()
            else:
                h0, h1 = self._halves(ci)
                self._copy(ins, outs, sems, ci, 3, _relative(6), me, rows=h0).wait_recv()
                self._copy(ins, outs, sems, ci, 4, _relative(6), me, rows=h1).wait_recv()
                self._copy(ins, outs, sems, ci, 7, _relative(6), sibling).start()
        for ci in range(n):
            self._copy(ins, outs, sems, ci, 0, sibling, me).wait_recv()
            for j, k in enumerate(FAR):
                self._copy(ins, outs, sems, ci, 5 + j, _relative(k | 1), me).wait_recv()
        for ci in range(n):
            self._copy(ins, outs, sems, ci, 0, me, sibling, from_shard=True).wait_send()
            self._copy(ins, outs, sems, ci, 1, me, _relative(4), from_shard=True).wait_send()
            self._copy(ins, outs, sems, ci, 2, me, _relative(2), from_shard=True).wait_send()
            if self._halves(ci) is None:
                self._copy(ins, outs, sems, ci, 3, me, _relative(6), from_shard=True).wait_send()
            else:
                self._pass(ins, outs, sems, ci, 0).wait_send()
                self._pass(ins, outs, sems, ci, 1).wait_send()
            for j, k in enumerate(FAR):
                self._copy(ins, outs, sems, ci, 5 + j, _relative(k), sibling).wait_send()
            self._own(ins, outs, sems, ci).wait()

    def done(self, results):
        for n, buf in zip(self.names, results):
            self.owner.bufs[n] = buf


class _Scatter:
    def __init__(self, me, far_index):
        self.me, self.far_index = me, far_index
        self.sends, self.owns, self.pairs, self.sums, self.fars = {}, {}, {}, {}, {}
        self.pair_cursor, self.far_cursor = {}, {}

    def add(self, name, send, own):
        self.sends[name] = send
        self.owns[name] = own
        self.pairs[name] = self.fars[name] = None
        self.pair_cursor[name] = 0

    def _rows(self, name):
        return self.sends[name].shape[1]

    def _add_ready_pairs(self):
        for name in self.sends:
            if name not in self.sums and self.pair_cursor[name] == self._rows(name):
                self.sums[name] = _add_pairs(self.sends[name], self.pairs[name], self.far_index, name="pair_" + name)
                self.far_cursor[name] = 0

    def _side(self, us, through=None):
        self._add_ready_pairs()
        names = list(self.sends)
        if through is not None:
            names = names[:names.index(through) + 1]
        pair_chunks = [(n, self.pair_cursor[n], self._rows(n) - self.pair_cursor[n]) for n in names
                       if self.pair_cursor[n] < self._rows(n)]
        for n, _, _ in pair_chunks:
            self.pair_cursor[n] = self._rows(n)
        far_chunks = _chunks(self.sums, self.far_cursor, us, RS_US_PER_MB,
                             through if through in self.sums else None) if self.sums else []
        return _ScatterSide(self, pair_chunks, far_chunks) if pair_chunks or far_chunks else None

    def add_blocks(self, name, blocks32, blocks16):
        self.add(name, blocks16, blocks32)

    def take(self, us):
        return self._side(us) if us >= MIN_RIDE_US else None

    def flush_pairs(self, name):
        side = self._side(PAIR_EXCHANGE_US)
        if side is not None:
            _run_side(side, name)
        self._add_ready_pairs()

    def get(self, name):
        step = 0
        while name not in self.sums or self.far_cursor[name] < self._rows(name):
            _run_side(self._side(None, through=name), "scatter_%s_%d" % (name, step))
            step += 1
        return self.owns[name], self.pairs[name], self.fars[name]


class _ScatterSide:
    TO_SIBLING = (1, 5, 3, 7)

    def __init__(self, owner, pair_chunks, far_chunks):
        self.owner, self.pair_chunks, self.far_chunks = owner, pair_chunks, far_chunks
        self.pair_names = list(dict.fromkeys(n for n, _, _ in pair_chunks))
        self.far_names = list(dict.fromkeys(n for n, _, _ in far_chunks))
        ins = [(owner.sends[n], owner.pairs[n], (4,)) for n in self.pair_names]
        ins += [(owner.sums[n], owner.fars[n], (3,)) for n in self.far_names]
        old = [i for i, (_, buf, _) in enumerate(ins) if buf is not None]
        self.operands = [src for src, _, _ in ins] + [ins[i][1] for i in old]
        self.out_shape = [jax.ShapeDtypeStruct(slots + src.shape[1:], BF16) for src, _, slots in ins]
        self.aliases = {len(ins) + j: i for j, i in enumerate(old)}
        n_pair, n_far = 4 * len(pair_chunks), 3 * len(far_chunks)
        self.sems = [pltpu.SemaphoreType.DMA((max(n_pair, 1),)), pltpu.SemaphoreType.DMA((max(n_pair, 1),)),
                     pltpu.SemaphoreType.DMA((max(n_far, 1),)), pltpu.SemaphoreType.DMA((max(n_far, 1),))]

    def _copies(self, ins, outs, sems):
        copies = []
        for ci, (name, r0, rows) in enumerate(self.pair_chunks):
            w = self.pair_names.index(name)
            for j, k in enumerate(self.TO_SIBLING):
                copies.append(pltpu.make_async_remote_copy(
                    src_ref=ins[w].at[_index(_relative(k)), pl.ds(r0, rows)], dst_ref=outs[w].at[j, pl.ds(r0, rows)],
                    send_sem=sems[0].at[4 * ci + j], recv_sem=sems[1].at[4 * ci + j],
                    device_id=_relative(1), device_id_type=MESH))
        for ci, (name, r0, rows) in enumerate(self.far_chunks):
            w = len(self.pair_names) + self.far_names.index(name)
            for j, k in enumerate(FAR):
                copies.append(pltpu.make_async_remote_copy(
                    src_ref=ins[w].at[j, pl.ds(r0, rows)], dst_ref=outs[w].at[j, pl.ds(r0, rows)],
                    send_sem=sems[2].at[3 * ci + j], recv_sem=sems[3].at[3 * ci + j],
                    device_id=_relative(k), device_id_type=MESH))
        return copies

    def start(self, ins, outs, sems):
        for cp in self._copies(ins, outs, sems):
            cp.start()

    def mid(self, ins, outs, sems):
        pass

    def finish(self, ins, outs, sems):
        for cp in self._copies(ins, outs, sems):
            cp.wait()

    def done(self, results):
        for n, buf in zip(self.pair_names, results):
            self.owner.pairs[n] = buf
        for n, buf in zip(self.far_names, results[len(self.pair_names):]):
            self.owner.fars[n] = buf


class _Joined:
    def __init__(self, sides):
        self.sides = sides
        self.operands, self.out_shape, self.sems, self.aliases, self.spans = [], [], [], {}, []
        for s in sides:
            i0, o0, s0 = len(self.operands), len(self.out_shape), len(self.sems)
            self.operands += list(s.operands)
            self.out_shape += list(s.out_shape)
            self.sems += list(s.sems)
            self.aliases.update({i0 + i: o0 + o for i, o in s.aliases.items()})
            self.spans.append((slice(i0, len(self.operands)), slice(o0, len(self.out_shape)),
                               slice(s0, len(self.sems))))

    def start(self, ins, outs, sems):
        for s, (i, o, m) in zip(self.sides, self.spans):
            s.start(ins[i], outs[o], sems[m])

    def mid(self, ins, outs, sems):
        for s, (i, o, m) in zip(self.sides, self.spans):
            s.mid(ins[i], outs[o], sems[m])

    def finish(self, ins, outs, sems):
        for s, (i, o, m) in zip(self.sides, self.spans):
            s.finish(ins[i], outs[o], sems[m])

    def done(self, results):
        for s, (_, o, _) in zip(self.sides, self.spans):
            s.done(results[o])


def _join(*sides):
    sides = [s for s in sides if s is not None]
    if len(sides) <= 1:
        return sides[0] if sides else None
    return _Joined(sides)


PART_W = 768


def _pack_rows(vecs):
    rows = -(-sum(v.shape[0] for v in vecs) // 8) * 8

    def body(*refs):
        out = refs[-1]
        out[...] = jnp.zeros_like(out)
        r0 = 0
        for v in refs[:-1]:
            k, n = v.shape
            for p in range(-(-n // PART_W)):
                w = min(PART_W, n - PART_W * p)
                out[p, r0:r0 + k, 0:w] = v[:, PART_W * p:PART_W * p + w]
            r0 += k

    vmem = pl.BlockSpec(memory_space=pltpu.VMEM)
    return pl.pallas_call(body, name="pack_small", in_specs=[vmem] * len(vecs), out_specs=vmem,
                          out_shape=jax.ShapeDtypeStruct((N_DEV, rows, PART_W), F32))(*vecs)


def _unpack_rows(packed, shapes):
    def body(packed_ref, *outs):
        r0 = 0
        for o in outs:
            k, n = o.shape
            for p in range(-(-n // PART_W)):
                w = min(PART_W, n - PART_W * p)
                o[:, PART_W * p:PART_W * p + w] = packed_ref[p, r0:r0 + k, 0:w]
            r0 += k

    vmem = pl.BlockSpec(memory_space=pltpu.VMEM)
    return pl.pallas_call(body, name="unpack_small", in_specs=[vmem], out_specs=[vmem] * len(shapes),
                          out_shape=[jax.ShapeDtypeStruct(s, F32) for s in shapes])(packed)


class _PartsToOwners:
    def __init__(self, arrays):
        self.n = len(arrays)
        self.pers = [a.shape[0] // N_DEV for a in arrays]
        self.operands, self.aliases = list(arrays), {}
        self.out_shape = [jax.ShapeDtypeStruct((N_DEV, per) + a.shape[1:], a.dtype) for a, per in zip(arrays, self.pers)]
        self.sems = [pltpu.SemaphoreType.DMA((self.n * (N_DEV - 1),))] * 2

    def _copies(self, ins, outs, sems):
        return [pltpu.make_async_remote_copy(
            src_ref=ins[j].at[pl.ds(self.pers[j] * _index(_relative(k)), self.pers[j])], dst_ref=outs[j].at[k],
            send_sem=sems[0].at[self.n * (k - 1) + j], recv_sem=sems[1].at[self.n * (k - 1) + j],
            device_id=_relative(k), device_id_type=MESH) for k in range(1, N_DEV) for j in range(self.n)]

    def start(self, ins, outs, sems):
        for cp in self._copies(ins, outs, sems):
            cp.start()

    def mid(self, ins, outs, sems):
        pass

    def finish(self, ins, outs, sems):
        for cp in self._copies(ins, outs, sems):
            cp.wait()

    def done(self, results):
        self.stages = list(results)


def _sum_parts(arrays, stages):
    n = len(arrays)
    pers = [a.shape[0] // N_DEV for a in arrays]

    def body(*refs):
        me = _index(_relative(0))
        for j in range(n):
            acc = refs[j][pl.ds(pers[j] * me, pers[j])]
            for k in range(1, N_DEV):
                acc = acc + refs[n + j][k].astype(F32)
            refs[2 * n + j][...] = acc

    vmem = pl.BlockSpec(memory_space=pltpu.VMEM)
    return pl.pallas_call(body, name="sum_small_parts", in_specs=[vmem] * (2 * n), out_specs=[vmem] * n,
                          out_shape=[jax.ShapeDtypeStruct((per,) + a.shape[1:], F32) for a, per in zip(arrays, pers)],
                          compiler_params=pltpu.CompilerParams(vmem_limit_bytes=VMEM_LIMIT))(*arrays, *stages)


class _PartsToAll:
    def __init__(self, parts):
        self.n = len(parts)
        self.pers = [p.shape[0] for p in parts]
        self.operands, self.aliases = list(parts), {}
        self.out_shape = [jax.ShapeDtypeStruct((N_DEV * p.shape[0],) + p.shape[1:], F32) for p in parts]
        self.sems = [pltpu.SemaphoreType.DMA((self.n * (N_DEV - 1),))] * 2 + [pltpu.SemaphoreType.DMA((self.n,))]

    def _rows(self, outs, j, pos):
        return outs[j].at[pl.ds(self.pers[j] * _index(pos), self.pers[j])]

    def _copy(self, ins, outs, sems, k, j, owner):
        return pltpu.make_async_remote_copy(
            src_ref=ins[j], dst_ref=self._rows(outs, j, owner),
            send_sem=sems[0].at[self.n * (k - 1) + j], recv_sem=sems[1].at[self.n * (k - 1) + j],
            device_id=_relative(k), device_id_type=MESH)

    def _own(self, ins, outs, sems, j):
        return pltpu.make_async_copy(ins[j], self._rows(outs, j, _relative(0)), sems[2].at[j])

    def start(self, ins, outs, sems):
        for j in range(self.n):
            self._own(ins, outs, sems, j).start()
            for k in range(1, N_DEV):
                self._copy(ins, outs, sems, k, j, _relative(0)).start()

    def mid(self, ins, outs, sems):
        pass

    def finish(self, ins, outs, sems):
        for j in range(self.n):
            for k in range(1, N_DEV):
                self._copy(ins, outs, sems, k, j, _relative(k)).wait_recv()
                self._copy(ins, outs, sems, k, j, _relative(0)).wait_send()
            self._own(ins, outs, sems, j).wait()

    def done(self, results):
        self.totals = list(results)


class _SmallSync:
    def __init__(self, vec_names, mat_names):
        self.vec_names, self.mat_names = vec_names, mat_names

    def begin(self, loss, grads):
        vecs = [loss] + [grads[n] for n in self.vec_names]
        self.shapes = [v.shape for v in vecs]
        self.own = [_diag_blocks(grads[n]) for n in self.mat_names] + [_pack_rows(vecs)]
        self.to_owners = _PartsToOwners([a.astype(BF16) for a in self.own[:-1]] + self.own[-1:])
        return self.to_owners

    def middle(self):
        self.to_all = _PartsToAll(_sum_parts(self.own, self.to_owners.stages))
        return self.to_all

    def end(self):
        *mats, packed = self.to_all.totals
        sums = _unpack_rows(packed, self.shapes)
        return sums[0], dict(zip(self.vec_names, sums[1:])), dict(zip(self.mat_names, mats))


def _block_diag(w):
    groups = []
    for g in range(N_RNN_GROUPS):
        placed = [jnp.pad(w[4 * g + b], ((RNN_BLOCK_W * b, RNN_BLOCK_W * (3 - b)),) * 2) for b in range(4)]
        groups.append(placed[0] + placed[1] + placed[2] + placed[3])
    return jnp.stack(groups)


def _diag_blocks(wg):
    blocks = []
    for n in range(4 * N_RNN_GROUPS):
        g, at = n // 4, RNN_BLOCK_W * (n % 4)
        blocks.append(wg[g, at:at + RNN_BLOCK_W, at:at + RNN_BLOCK_W])
    return jnp.stack(blocks)


def _heads_major(t, n_heads):
    return t.reshape(S, n_heads, HEAD_DIM).transpose(1, 0, 2)


def _heads_minor(t):
    return t.transpose(1, 0, 2).reshape(S, t.shape[0] * HEAD_DIM)


def _natural(gathered, how):
    n, r, c = gathered.shape
    if how == "rows":
        return gathered.reshape(n * r, c)
    return gathered.transpose(1, 0, 2).reshape(r, n * c)


def _blocks(full, how):
    if how == "rows":
        return full.reshape(N_DEV, full.shape[0] // N_DEV, full.shape[1])
    return full.reshape(full.shape[0], N_DEV, full.shape[1] // N_DEV).transpose(1, 0, 2)


def _cast_many(arrays, side=None):
    steps = 4

    def body(*refs):
        n = len(refs) // 2
        for src, dst in zip(refs[:n], refs[n:]):
            dst[...] = src[...].astype(dst.dtype)

    specs = [pl.BlockSpec((a.shape[0] // steps, a.shape[1]), lambda i: (i, 0)) for a in arrays]
    return _call(
        body,
        name="cast_weights",
        grid=(steps,),
        in_specs=specs,
        out_specs=specs,
        out_shape=[jax.ShapeDtypeStruct(a.shape, MXU_DTYPE) for a in arrays],
        semantics=("parallel",),
        operands=tuple(arrays),
        side=side,
    )


def _forward_backward(x2, xb, target, small, gather, scatter, sync):
    w_in_t = _natural(gather.get("w_in"), "rows")
    proj, projb = _mm(xb, w_in_t, tb=True, tm=1024, tn=512, tk=D, out_dtype=(F32, MXU_DTYPE), name="proj",
                      side=gather.take(110))

    qt = projb[:, :OFF_K].T.reshape(N_KV, GROUP, HEAD_DIM, S)
    k2, v2 = projb[:, OFF_K:OFF_V], projb[:, OFF_V:OFF_RX]
    kp = jnp.pad(_heads_major(k2, N_KV), ((0, 0), (BLOCK, 0), (0, 0)))
    vp = jnp.pad(_heads_major(v2, N_KV), ((0, 0), (BLOCK, 0), (0, 0)))
    kt = jnp.pad(k2.T.reshape(N_KV, HEAD_DIM, S), ((0, 0), (0, 0), (BLOCK, 0)))
    vt = jnp.pad(v2.T.reshape(N_KV, HEAD_DIM, S), ((0, 0), (0, 0), (BLOCK, 0)))
    sink_row = jnp.repeat(small["attn_sinks"].reshape(N_KV, 1, GROUP), BLOCK, axis=2)
    ot = _attn_fwd(qt, kp, vt, sink_row, side=gather.take(36)).reshape(D, S)

    rconv_w = _natural(gather.get("rnn_conv_w"), "cols")
    rxc = _conv_fwd(proj, OFF_RX, rconv_w, small["rnn_conv_b"], tc=512, name="rnn_conv_fwd", side=gather.take(18))
    r, i = _lru_gates(rxc, small["lru_wa"], small["lru_wi"], small["lru_ba"], small["lru_bi"], side=gather.take(33))
    h, yrin = _lru_scan_fwd(r, i, rxc, proj, small["lru_lambda"], side=gather.take(53))

    w_ap = _natural(gather.get("w_attn_proj"), "rows")
    w_rp = _natural(gather.get("w_rnn_proj"), "rows")
    y_attn = _mm(ot, w_ap, ta=True, tm=1024, tn=1024, tk=D, name="attn_proj", side=gather.take(22))
    y_rnn = _mm(yrin, w_rp, tm=1024, tn=1024, tk=D_RNN, name="rnn_proj", side=gather.take(27))
    mixin = _gate_fwd(y_attn, y_rnn, proj, small["b_gate"], side=gather.take(25))
    w_out = _natural(gather.get("w_out"), "rows")
    mix = _mm(mixin, w_out, tm=1024, tn=1024, tk=D, name="mix_out", side=gather.take(22))
    x1, x1b, xhat1, rstd1 = _ln_fwd(x2, mix, small["ln1_g"], small["ln1_b"], side=gather.take(23))

    w_up = gather.get("ffn_w_up")
    up = _mm(x1b, w_up, tm=1024, tn=768, tk=D, b_block=768, name="ffn_up", side=gather.take(58))
    w_gate = gather.get("ffn_w_gate")
    gpre = _mm(x1b, w_gate, tm=1024, tn=768, tk=D, b_block=768, name="ffn_gate", side=gather.take(58))
    fconv_w = _natural(gather.get("ffn_conv_w"), "cols")
    fin, gate = _ffn_act_fwd(up, gpre, fconv_w, small["ffn_conv_b"], side=gather.take())
    w_down = _natural(gather.get("ffn_w_down"), "rows")
    f = _mm(fin, w_down, tm=1024, tn=1024, tk=2048, name="ffn_down")
    loss, dpre2, dpre2b, d_ln2_g, d_ln2_b = _ln_loss_bwd(x1, f, small["ln2_g"], small["ln2_b"], target)

    grads = {"ln2_g": d_ln2_g, "ln2_b": d_ln2_b}
    both = (F32, BF16)
    g32, g16 = _mm(fin, dpre2b, ta=True, tm=1024, tn=1024, tk=S, out_dtype=both, name="d_ffn_w_down")
    scatter.add_blocks("ffn_w_down", _blocks(g32, "rows"), _blocks(g16, "rows"))
    dfin = _mm(dpre2b, w_down, tb=True, tm=1024, tn=1024, tk=D, name="d_fin", side=scatter.take(57))
    dup, dgpre, grads["ffn_conv_w"], grads["ffn_conv_b"] = _ffn_act_bwd(
        dfin, up, gpre, gate, fconv_w, side=scatter.take(85))
    g32, g16 = _mm(x1b, dup, ta=True, tm=1024, tn=768, tk=S, out_dtype=both, out_block=768, name="d_ffn_w_up",
                   side=scatter.take(57))
    scatter.add_blocks("ffn_w_up", g32, g16)
    g32, g16 = _mm(x1b, dgpre, ta=True, tm=1024, tn=768, tk=S, out_dtype=both, out_block=768, name="d_ffn_w_gate",
                   side=scatter.take(56))
    scatter.add_blocks("ffn_w_gate", g32, g16)
    dx1 = _mm(dup, w_up, tb=True, tm=1024, tn=1024, tk=768, b_block=768, name="d_x1_up", side=scatter.take(68))
    dx1 = _mm(dgpre, w_gate, tb=True, tm=1024, tn=1024, tk=768, b_block=768, add=dx1, name="d_x1_gate",
              side=scatter.take(70))
    dpre1, dpre1b, grads["ln1_g"], grads["ln1_b"] = _ln_bwd(dx1, dpre2, xhat1, rstd1, small["ln1_g"],
                                                            side=scatter.take(24))

    g32, g16 = _mm(mixin, dpre1b, ta=True, tm=1024, tn=1024, tk=S, out_dtype=both, name="d_w_out",
                   side=scatter.take(26))
    scatter.add_blocks("w_out", _blocks(g32, "rows"), _blocks(g16, "rows"))
    dmix = _mm(dpre1b, w_out, tb=True, tm=1024, tn=1024, tk=D, name="d_mixin", side=scatter.take(22))
    dya, dyr, dgl_a, dgl_r, db_a, db_r = _gate_bwd(dmix, y_attn, y_rnn, proj, small["b_gate"], side=scatter.take(36))
    grads["b_gate"] = jnp.concatenate([db_a, db_r], axis=1)
    g32, g16 = _mm(ot, dya, tm=1024, tn=1024, tk=S, out_dtype=both, name="d_w_attn_proj", side=scatter.take(38))
    scatter.add_blocks("w_attn_proj", _blocks(g32, "rows"), _blocks(g16, "rows"))
    g32, g16 = _mm(yrin, dyr, ta=True, tm=1280, tn=1024, tk=S, out_dtype=both, name="d_w_rnn_proj",
                   side=scatter.take(27))
    scatter.add_blocks("w_rnn_proj", _blocks(g32, "rows"), _blocks(g16, "rows"))
    dot_ = _mm(w_ap, dya, tb=True, tm=1024, tn=1024, tk=D, out_dtype=MXU_DTYPE, name="d_o", side=scatter.take(22))
    dyrin = _mm(dyr, w_rp, tb=True, tm=1024, tn=1280, tk=D, name="d_yrin", side=scatter.take(27))

    dry, dzr, dzi, drxc_in, grads["lru_ba"], grads["lru_bi"], grads["lru_lambda"] = _lru_scan_bwd(
        dyrin, proj, h, r, i, rxc, small["lru_lambda"], side=scatter.take(94))
    grads["lru_wa"], grads["lru_wi"] = _lru_gate_wgrad(rxc, dzr, dzi, side=scatter.take(22))
    drxc = _lru_gate_xgrad(dzr, dzi, small["lru_wa"], small["lru_wi"], drxc_in, side=scatter.take(33))
    drx, grads["rnn_conv_w"], grads["rnn_conv_b"] = _conv_bwd(drxc, proj, OFF_RX, rconv_w, tc=512,
                                                             name="rnn_conv_bwd", side=scatter.take(29))

    dqt, dk, dv, dsink = _attn_bwd(qt, kp, kt, vp, sink_row, dot_.reshape(N_KV, GROUP, HEAD_DIM, S),
                                   side=scatter.take(65))
    grads["attn_sinks"] = dsink.reshape(1, N_KV * GROUP)
    dproj = jnp.concatenate([
        dqt.reshape(D, S).T,
        _heads_minor(dk[:, BLOCK:, :]).astype(MXU_DTYPE),
        _heads_minor(dv[:, BLOCK:, :]).astype(MXU_DTYPE),
        drx, dry, dgl_a, dgl_r], axis=1)
    for part in range(W_IN_PARTS):
        cols = slice(part * (D // W_IN_PARTS), (part + 1) * (D // W_IN_PARTS))
        side = _join(scatter.take(55), sync.begin(loss, grads)) if part == 0 else scatter.take(68)
        g32, g16 = _mm(dproj, xb[:, cols], ta=True, tm=512, tn=D // W_IN_PARTS, tk=S, out_dtype=both,
                       name="d_w_in_%d" % part, side=side)
        scatter.add_blocks("w_in_%d" % part, _blocks(g32, "rows"), _blocks(g16, "rows"))
        scatter.flush_pairs("pairs_w_in_%d" % part)
    dx = _mm(dproj, w_in_t, tm=1024, tn=1024, tk=512, add=dpre1, add_scale=ALPHA, name="d_x",
             side=_join(scatter.take(400), sync.middle()))
    return dx


SHARDED = (
    ("w_in", "cols", 368), ("w_attn_proj", "rows", 32), ("w_rnn_proj", "rows", 32), ("w_out", "rows", 32),
    ("ffn_w_up", "cols", 128), ("ffn_w_gate", "cols", 128), ("ffn_w_down", "rows", 64),
)
SMALL_REPLICATED = ("b_gate", "rnn_conv_b", "lru_wa", "lru_ba", "lru_wi", "lru_bi", "lru_lambda", "attn_sinks",
                    "ln1_g", "ln1_b", "ffn_conv_b", "ln2_g", "ln2_b")
SMALL_SHARDED = ("rnn_conv_w", "ffn_conv_w")
SMALL_MATS = ("lru_wa", "lru_wi")
W_IN_PARTS = 2
WEIGHTS = ("w_in", "b_gate", "rnn_conv_w", "rnn_conv_b", "lru_wa", "lru_ba", "lru_wi", "lru_bi", "lru_lambda",
           "attn_sinks", "w_attn_proj", "w_rnn_proj", "w_out", "ln1_g", "ln1_b", "ffn_w_up", "ffn_w_gate",
           "ffn_conv_w", "ffn_conv_b", "ffn_w_down", "ln2_g", "ln2_b")


def kernel(x, w_in, b_gate, rnn_conv_w, rnn_conv_b, lru_wa, lru_ba, lru_wi, lru_bi, lru_lambda, attn_sinks, w_attn_proj, w_rnn_proj, w_out, ln1_g, ln1_b, ffn_w_up, ffn_w_gate, ffn_conv_w, ffn_conv_b, ffn_w_down, ln2_g, ln2_b, loss_target, m_w_in, m_b_gate, m_rnn_conv_w, m_rnn_conv_b, m_lru_wa, m_lru_ba, m_lru_wi, m_lru_bi, m_lru_lambda, m_attn_sinks, m_w_attn_proj, m_w_rnn_proj, m_w_out, m_ln1_g, m_ln1_b, m_ffn_w_up, m_ffn_w_gate, m_ffn_conv_w, m_ffn_conv_b, m_ffn_w_down, m_ln2_g, m_ln2_b, v_w_in, v_b_gate, v_rnn_conv_w, v_rnn_conv_b, v_lru_wa, v_lru_ba, v_lru_wi, v_lru_bi, v_lru_lambda, v_attn_sinks, v_w_attn_proj, v_w_rnn_proj, v_w_out, v_ln1_g, v_ln1_b, v_ffn_w_up, v_ffn_w_gate, v_ffn_conv_w, v_ffn_conv_b, v_ffn_w_down, v_ln2_g, v_ln2_b):
    given = dict(locals())
    wsh = {n: given[n][0] for n in WEIGHTS}
    msh = {n: given["m_" + n][0] for n in WEIGHTS}
    vsh = {n: given["v_" + n][0] for n in WEIGHTS}
    m_given = {n: given["m_" + n] for n in WEIGHTS}
    v_given = {n: given["v_" + n] for n in WEIGHTS}
    me = 4 * lax.axis_index("x") + 2 * lax.axis_index("y") + lax.axis_index("c")

    order = ("w_in", "rnn_conv_w", "ffn_conv_w", "w_attn_proj", "w_rnn_proj", "w_out", "ffn_w_up", "ffn_w_gate",
             "ffn_w_down")
    gather = _Gather({"w_in": wsh["w_in"].T.astype(MXU_DTYPE), **{n: wsh[n] for n in order[1:3]}})
    *casts, xb = _cast_many([wsh[n] for n in order[3:]] + [x[0]], side=gather.take(through="ffn_conv_w"))
    gather.add_shards(dict(zip(order[3:], casts)))
    small = {n: given[n] for n in SMALL_REPLICATED}
    small["lru_wa"] = _block_diag(wsh["lru_wa"])
    small["lru_wi"] = _block_diag(wsh["lru_wi"])
    scatter = _Scatter(me, jnp.stack([_index(_relative(k)) for k in FAR]).astype(jnp.int32))

    vec_names = tuple(n for n in SMALL_REPLICATED if n not in SMALL_MATS) + SMALL_SHARDED
    sync = _SmallSync(vec_names, SMALL_MATS)
    dx = _forward_backward(x[0], xb, loss_target[0], small, gather, scatter, sync)

    loss_total, g_small, mat_sums = sync.end()
    loss_total = loss_total.reshape(())
    for n in SMALL_SHARDED:
        width = wsh[n].shape[1]
        g_small[n] = lax.dynamic_slice_in_dim(g_small[n], me * width, width, axis=1)
    g_small = {n: g_small[n].reshape(given[n].shape) for n in vec_names}
    out = {}
    results = _adamw_many(*[[d[n] for n in vec_names] for d in (given, m_given, v_given, g_small)])
    for n, delta, nm, nv in zip(vec_names, *results):
        out[n] = (g_small[n], delta, nm, nv)
    for n in SMALL_MATS:
        g = mat_sums[n].reshape(given[n].shape)
        out[n] = (g, *_adamw_blocks(given[n], m_given[n], v_given[n], g, name="adamw_" + n))

    tile_rows = {n: tr for n, _, tr in SHARDED}
    me1 = me.reshape(1).astype(jnp.int32)
    res = None
    for n in list(scatter.sends):
        own, pair, far = scatter.get(n)
        if n.startswith("w_in_"):
            part = int(n[len("w_in_"):])
            w_t, m_t, v_t = (a["w_in"].transpose(0, 2, 1) for a in (given, m_given, v_given))
            res = _reduce_adamw(w_t, m_t, v_t, own, pair, far, me1, tr=tile_rows["w_in"], name="adamw_" + n,
                                part=part, earlier=res if part else None)
            out["w_in"] = tuple(r.transpose(0, 2, 1) for r in res)
        else:
            out[n] = tuple(_reduce_adamw(given[n], m_given[n], v_given[n], own, pair, far, me1, tr=tile_rows[n],
                                         name="adamw_" + n))

    outputs = [loss_total, dx[None]]
    for kind in range(4):
        outputs += [out[n][kind] for n in WEIGHTS]
    return tuple(outputs)
```

```python
import math

import jax
import jax.numpy as jnp
from jax import lax
from jax.experimental import pallas as pl
from jax.experimental.pallas import tpu as pltpu

F32 = jnp.float32
BF16 = jnp.bfloat16
MXU_DTYPE = jnp.bfloat16

N_DEV = 8
S = 2048
D = 2048
HEAD_DIM = 64
N_KV = 4
GROUP = 8
BLOCK = 128
D_KV = N_KV * HEAD_DIM
D_RNN = 2560
RNN_GROUP = 640
N_RNN_GROUPS = D_RNN // RNN_GROUP
RNN_BLOCK_W = 160
RNN_CONV_W = 4
LRU_C = 8.0
D_FF = 6144
FFN_CONV_W = 3
D_IN = 11776
OFF_K = 2048
OFF_V = 2304
OFF_RX = 2560
OFF_RY = 5120
OFF_GA = 7680
OFF_GR = 9728
LN_EPS = 1e-5
ALPHA = 2.0 ** 0.25
ADAM_LR = 0.001
ADAM_B1 = 0.9
ADAM_B2 = 0.999
ADAM_EPS = 1e-08
ADAM_WD = 0.01
ADAM_STEP = 10
NEG = -1e30
VMEM_LIMIT = 56 * 1024 * 1024
MID_RIDE_TENTHS = 6
MESH = pl.DeviceIdType.MESH
GELU_C = math.sqrt(2.0 / math.pi)


def _cparams(*sem):
    return pltpu.CompilerParams(dimension_semantics=sem or None, vmem_limit_bytes=VMEM_LIMIT)


def _call(body, *, name, grid, in_specs, out_specs, out_shape, operands, semantics, scratch_shapes=(), side=None,
          window=None):
    single = not isinstance(out_shape, (list, tuple))
    out_shape = [out_shape] if single else list(out_shape)
    out_specs = [out_specs] if single else list(out_specs)
    in_specs = list(in_specs)
    operands = tuple(operands)
    scratch_shapes = list(scratch_shapes)
    hbm = pl.BlockSpec(memory_space=pltpu.HBM)
    aliases = {}
    if window is not None:
        whole, col0 = window
        block, index_map = out_specs[0].block_shape, out_specs[0].index_map
        assert col0 % block[1] == 0 and out_shape[0].dtype == whole.dtype
        out_specs[0] = pl.BlockSpec(block, lambda *g: (index_map(*g)[0], index_map(*g)[1] + col0 // block[1]))
        out_shape[0] = jax.ShapeDtypeStruct(whole.shape, whole.dtype)
        if not isinstance(whole, jax.ShapeDtypeStruct):
            aliases[len(in_specs)] = 0
            in_specs.append(hbm)
            operands += (whole,)
            compute, n_read = body, len(in_specs) - 1

            def body(*refs):
                compute(*refs[:n_read], *refs[n_read + 1:])

    if side is None:
        res = pl.pallas_call(
            body, name=name, grid=grid, in_specs=in_specs, out_specs=out_specs, out_shape=out_shape,
            scratch_shapes=scratch_shapes, input_output_aliases=aliases,
            compiler_params=_cparams(*semantics))(*operands)
        return res[0] if single else res
    n_in, n_out, n_scr = len(in_specs), len(out_shape), len(scratch_shapes)
    s_in, s_out = len(side.operands), len(side.out_shape)
    steps = math.prod(grid)
    mid_step = (steps * MID_RIDE_TENTHS) // 10

    def with_copies(*refs):
        core_in, side_in = refs[:n_in], refs[n_in:n_in + s_in]
        o0 = n_in + s_in
        core_out, side_out = refs[o0:o0 + n_out], refs[o0 + n_out:o0 + n_out + s_out]
        c0 = o0 + n_out + s_out
        core_scr, sems = refs[c0:c0 + n_scr], refs[c0 + n_scr:]
        step = 0
        for d, size in enumerate(grid):
            step = step * size + pl.program_id(d)

        @pl.when(step == 0)
        def _():
            side.start(side_in, side_out, sems)

        body(*core_in, *core_out, *core_scr)

        @pl.when(step == mid_step)
        def _():
            side.mid(side_in, side_out, sems)

        @pl.when(step == steps - 1)
        def _():
            side.finish(side_in, side_out, sems)

    res = pl.pallas_call(
        with_copies, name=name, grid=grid,
        in_specs=in_specs + [hbm] * s_in, out_specs=out_specs + [hbm] * s_out,
        out_shape=out_shape + list(side.out_shape),
        scratch_shapes=scratch_shapes + list(side.sems),
        input_output_aliases={**aliases, **{n_in + i: n_out + o for i, o in side.aliases.items()}},
        compiler_params=_cparams(*(("arbitrary",) * len(grid))))(*operands, *side.operands)
    side.done(res[n_out:])
    return res[0] if single else res[:n_out]


def _run_side(side, name):
    def body(*refs):
        s_in, s_out = len(side.operands), len(side.out_shape)
        side.start(refs[:s_in], refs[s_in:s_in + s_out], refs[s_in + s_out:])
        side.mid(refs[:s_in], refs[s_in:s_in + s_out], refs[s_in + s_out:])
        side.finish(refs[:s_in], refs[s_in:s_in + s_out], refs[s_in + s_out:])

    hbm = pl.BlockSpec(memory_space=pltpu.HBM)
    res = pl.pallas_call(
        body, name=name, in_specs=[hbm] * len(side.operands), out_specs=[hbm] * len(side.out_shape),
        out_shape=list(side.out_shape), scratch_shapes=list(side.sems),
        input_output_aliases=dict(side.aliases))(*side.operands)
    side.done(res)


def _gelu(x):
    x2 = x * x
    t = jnp.tanh(GELU_C * (x + 0.044715 * x * x2))
    g = 0.5 * x * (1.0 + t)
    dg = 0.5 * (1.0 + t) + 0.5 * x * (1.0 - t * t) * (GELU_C * (1.0 + 3.0 * 0.044715 * x2))
    return g, dg


def _sigmoid(x):
    return 1.0 / (1.0 + jnp.exp(-x))


def _softplus(x):
    z = jnp.exp(-jnp.abs(x))
    small = z * (1.0 - z * (0.5 - z * (1.0 / 3.0 - 0.25 * z)))
    return jnp.maximum(x, 0.0) + jnp.where(z < 0.02, small, jnp.log(1.0 + z))


def _one_minus_exp(x):
    series = -x * (1.0 + x * (0.5 + x * (1.0 / 6.0 + x * (1.0 / 24.0))))
    return jnp.where(x > -0.03, series, 1.0 - jnp.exp(x))


def _colsum(v):
    return jnp.sum(v, axis=0, keepdims=True)


def _mm(a, b, *, tm, tn, tk, name, ta=False, tb=False, out_dtype=F32, b_block=None, out_block=None, add=None,
        add_scale=1.0, side=None):
    out_dtypes = out_dtype if isinstance(out_dtype, tuple) else (out_dtype,)
    if ta:
        k_dim, m_dim = a.shape
    else:
        m_dim, k_dim = a.shape
    if b_block is None:
        n_dim = b.shape[0] if tb else b.shape[1]
    else:
        n_dim = b.shape[1] if tb else b.shape[0] * b_block
    assert m_dim % tm == 0 and n_dim % tn == 0 and k_dim % tk == 0, (name, m_dim, n_dim, k_dim)
    nk = k_dim // tk
    dims = (((0 if ta else 1,), (1 if tb else 0,)), ((), ()))
    has_add = add is not None

    def body(*refs):
        a_ref, b_ref = refs[0], refs[1]
        add_ref = refs[2] if has_add else None
        first_out = 3 if has_add else 2
        o_refs = refs[first_out:first_out + len(out_dtypes)]

        def product():
            return lax.dot_general(a_ref[...].astype(MXU_DTYPE), b_ref[...].astype(MXU_DTYPE), dims,
                                   preferred_element_type=F32)

        def finish(acc):
            if has_add:
                acc = acc + add_scale * add_ref[...]
            for o_ref in o_refs:
                o_ref[...] = acc.astype(o_ref.dtype)

        if nk == 1:
            finish(product())
        else:
            acc_ref = refs[-1]
            k = pl.program_id(2)

            @pl.when(k == 0)
            def _():
                acc_ref[...] = jnp.zeros_like(acc_ref)

            acc_ref[...] += product()

            @pl.when(k == nk - 1)
            def _():
                finish(acc_ref[...])

    if ta:
        a_spec = pl.BlockSpec((tk, tm), lambda i, j, k: (k, i))
    else:
        a_spec = pl.BlockSpec((tm, tk), lambda i, j, k: (i, k))
    if b_block is None:
        if tb:
            b_spec = pl.BlockSpec((tn, tk), lambda i, j, k: (j, k))
        else:
            b_spec = pl.BlockSpec((tk, tn), lambda i, j, k: (k, j))
    elif tb:
        assert b_block % tk == 0
        b_spec = pl.BlockSpec((None, tn, tk), lambda i, j, k: ((k * tk) // b_block, j, ((k * tk) % b_block) // tk))
    else:
        assert b_block % tn == 0
        b_spec = pl.BlockSpec((None, tk, tn), lambda i, j, k: ((j * tn) // b_block, k, ((j * tn) % b_block) // tn))
    in_specs = [a_spec, b_spec]
    operands = [a, b]
    if has_add:
        in_specs.append(pl.BlockSpec((tm, tn), lambda i, j, k: (i, j)))
        operands.append(add)
    if out_block is None:
        out_spec = pl.BlockSpec((tm, tn), lambda i, j, k: (i, j))
        out_dims = (m_dim, n_dim)
    else:
        assert out_block % tn == 0
        out_spec = pl.BlockSpec((None, tm, tn), lambda i, j, k: ((j * tn) // out_block, i, ((j * tn) % out_block) // tn))
        out_dims = (n_dim // out_block, m_dim, out_block)
    res = _call(
        body,
        name=name,
        grid=(m_dim // tm, n_dim // tn, nk),
        in_specs=in_specs,
        out_specs=[out_spec] * len(out_dtypes),
        out_shape=[jax.ShapeDtypeStruct(out_dims, dt) for dt in out_dtypes],
        scratch_shapes=[pltpu.VMEM((tm, tn), F32)] if nk > 1 else [],
        semantics=("parallel", "parallel", "arbitrary"),
        operands=tuple(operands),
        side=side,
    )
    return res if isinstance(out_dtype, tuple) else res[0]


def _attn_bias(bias_ref, h):
    key = lax.broadcasted_iota(jnp.int32, (2 * BLOCK, GROUP * BLOCK), 0)
    col = lax.broadcasted_iota(jnp.int32, (2 * BLOCK, GROUP * BLOCK), 1)
    dist = BLOCK + (col & (BLOCK - 1)) - key
    head = h * GROUP + (col >> 7) + 1
    slope = jnp.exp(head.astype(F32) * (-0.25 * math.log(2.0)))
    bias = jnp.where((dist >= 0) & (dist < BLOCK), -slope * dist.astype(F32), NEG)
    bias_ref[1] = bias
    bias_ref[0] = jnp.where(key < BLOCK, NEG, bias)


def _attn_probs(kb, qt, bias, sink):
    s = jnp.dot(kb, qt, preferred_element_type=F32) * (HEAD_DIM ** -0.5) + bias
    m = jnp.maximum(jnp.max(s, axis=0, keepdims=True), sink)
    e = jnp.exp(s - m)
    e_sink = jnp.exp(sink - m)
    inv = 1.0 / (jnp.sum(e, axis=0, keepdims=True) + e_sink)
    return e * inv, e_sink * inv


def _heads_on_lanes(ref, r0):
    return jnp.concatenate([ref[g, :, pl.ds(r0, BLOCK)] for g in range(GROUP)], axis=1)


def _attn_fwd(qt, kp, vt, sink_row, side=None):
    cols = GROUP * BLOCK

    def body(q_ref, k_ref, vt_ref, sink_ref, o_ref, bias_ref):
        _attn_bias(bias_ref, pl.program_id(0))
        sink = sink_ref[...]

        def step(n, carry):
            r0 = pl.multiple_of(n * BLOCK, BLOCK)
            p, _ = _attn_probs(k_ref[pl.ds(r0, 2 * BLOCK), :], _heads_on_lanes(q_ref, r0),
                               bias_ref[jnp.minimum(n, 1)], sink)
            o = jnp.dot(vt_ref[:, pl.ds(r0, 2 * BLOCK)], p.astype(MXU_DTYPE), preferred_element_type=F32)
            for g in range(GROUP):
                o_ref[g, :, pl.ds(r0, BLOCK)] = o[:, g * BLOCK:(g + 1) * BLOCK].astype(o_ref.dtype)
            return carry

        lax.fori_loop(0, S // BLOCK, step, 0)

    hm = pl.BlockSpec((None, GROUP, HEAD_DIM, S), lambda h: (h, 0, 0, 0))
    return _call(
        body,
        name="attn_fwd",
        grid=(N_KV,),
        in_specs=[
            hm,
            pl.BlockSpec((None, BLOCK + S, HEAD_DIM), lambda h: (h, 0, 0)),
            pl.BlockSpec((None, HEAD_DIM, BLOCK + S), lambda h: (h, 0, 0)),
            pl.BlockSpec((None, 1, cols), lambda h: (h, 0, 0)),
        ],
        out_specs=hm,
        out_shape=jax.ShapeDtypeStruct((N_KV, GROUP, HEAD_DIM, S), MXU_DTYPE),
        scratch_shapes=[pltpu.VMEM((2, 2 * BLOCK, cols), F32)],
        semantics=("parallel",),
        operands=(qt, kp, vt, sink_row),
        side=side,
    )


def _attn_bwd(qt, kp, kt, vp, sink_row, dot_, side=None):
    cols = GROUP * BLOCK

    def body(q_ref, k_ref, kt_ref, v_ref, sink_ref, do_ref, dq_ref, dk_ref, dv_ref, dsink_ref, bias_ref):
        _attn_bias(bias_ref, pl.program_id(0))
        sink = sink_ref[...]
        dk_ref[...] = jnp.zeros_like(dk_ref)
        dv_ref[...] = jnp.zeros_like(dv_ref)
        nt = (((1,), (1,)), ((), ()))

        def step(n, sink_acc):
            r0 = pl.multiple_of(n * BLOCK, BLOCK)
            band = pl.ds(r0, 2 * BLOCK)
            qn = _heads_on_lanes(q_ref, r0)
            don = _heads_on_lanes(do_ref, r0)
            p, p_sink = _attn_probs(k_ref[band, :], qn, bias_ref[jnp.minimum(n, 1)], sink)
            dp = jnp.dot(v_ref[band, :], don, preferred_element_type=F32)
            delta = jnp.sum(p * dp, axis=0, keepdims=True)
            ds = (p * (dp - delta) * (HEAD_DIM ** -0.5)).astype(MXU_DTYPE)
            dq = jnp.dot(kt_ref[:, band], ds, preferred_element_type=F32)
            for g in range(GROUP):
                dq_ref[g, :, pl.ds(r0, BLOCK)] = dq[:, g * BLOCK:(g + 1) * BLOCK].astype(dq_ref.dtype)
            dk_ref[band, :] += lax.dot_general(ds, qn, nt, preferred_element_type=F32)
            dv_ref[band, :] += lax.dot_general(p.astype(MXU_DTYPE), don, nt, preferred_element_type=F32)
            return sink_acc - p_sink * delta

        sink_acc = lax.fori_loop(0, S // BLOCK, step, jnp.zeros((1, cols), F32))
        for g in range(GROUP):
            dsink_ref[g:g + 1, :] = jnp.sum(sink_acc[:, g * BLOCK:(g + 1) * BLOCK], axis=1, keepdims=True)

    hm = pl.BlockSpec((None, GROUP, HEAD_DIM, S), lambda h: (h, 0, 0, 0))
    kv = pl.BlockSpec((None, BLOCK + S, HEAD_DIM), lambda h: (h, 0, 0))
    return _call(
        body,
        name="attn_bwd",
        grid=(N_KV,),
        in_specs=[hm, kv, pl.BlockSpec((None, HEAD_DIM, BLOCK + S), lambda h: (h, 0, 0)), kv,
                  pl.BlockSpec((None, 1, cols), lambda h: (h, 0, 0)), hm],
        out_specs=[hm, kv, kv, pl.BlockSpec((None, GROUP, 1), lambda h: (h, 0, 0))],
        out_shape=[
            jax.ShapeDtypeStruct((N_KV, GROUP, HEAD_DIM, S), MXU_DTYPE),
            jax.ShapeDtypeStruct((N_KV, BLOCK + S, HEAD_DIM), F32),
            jax.ShapeDtypeStruct((N_KV, BLOCK + S, HEAD_DIM), F32),
            jax.ShapeDtypeStruct((N_KV, GROUP, 1), F32),
        ],
        scratch_shapes=[pltpu.VMEM((2, 2 * BLOCK, cols), F32)],
        semantics=("parallel",),
        operands=(qt, kp, kt, vp, sink_row, dot_),
        side=side,
    )


PAD = 8
CHUNK = 256


def _past_taps(xpad_ref, r0, width):
    ext = xpad_ref[pl.ds(r0, CHUNK + PAD), :]
    taps = []
    for k in range(width):
        back = width - 1 - k
        taps.append((ext if back == 0 else pltpu.roll(ext, back, 0))[PAD:, :])
    return taps


def _future_taps(xpad_ref, r0, width):
    ext = xpad_ref[pl.ds(r0, CHUNK + PAD), :]
    taps = []
    for ahead in range(width):
        taps.append((ext if ahead == 0 else pltpu.roll(ext, CHUNK + PAD - ahead, 0))[:CHUNK, :])
    return taps


def _conv_fwd(src, col0, w, b, *, tc, name, side=None):
    width, c_dim = w.shape

    def body(x_ref, w_ref, b_ref, o_ref, xpad_ref):
        xpad_ref[pl.ds(0, PAD), :] = jnp.zeros((PAD, tc), F32)
        xpad_ref[pl.ds(PAD, S), :] = x_ref[...]
        wv = w_ref[...]
        bv = b_ref[...]

        def step(ci, carry):
            r0 = pl.multiple_of(ci * CHUNK, CHUNK)
            taps = _past_taps(xpad_ref, r0, width)
            y = bv + taps[0] * wv[0:1, :]
            for k in range(1, width):
                y = y + taps[k] * wv[k:k + 1, :]
            o_ref[pl.ds(r0, CHUNK), :] = y
            return carry

        lax.fori_loop(0, S // CHUNK, step, 0)

    return _call(
        body,
        name=name,
        grid=(c_dim // tc,),
        in_specs=[
            pl.BlockSpec((S, tc), lambda j: (0, col0 // tc + j)),
            pl.BlockSpec((width, tc), lambda j: (0, j)),
            pl.BlockSpec((1, tc), lambda j: (0, j)),
        ],
        out_specs=pl.BlockSpec((S, tc), lambda j: (0, j)),
        out_shape=jax.ShapeDtypeStruct((S, c_dim), F32),
        scratch_shapes=[pltpu.VMEM((S + PAD, tc), F32)],
        semantics=("parallel",),
        operands=(src, w, b),
        side=side,
    )


def _conv_bwd(dy, src, col0, w, *, tc, name, side=None, window=None):
    width, c_dim = w.shape

    def body(dy_ref, x_ref, w_ref, dx_ref, dw_ref, db_ref, xpad_ref, dpad_ref):
        xpad_ref[pl.ds(0, PAD), :] = jnp.zeros((PAD, tc), F32)
        xpad_ref[pl.ds(PAD, S), :] = x_ref[...]
        dpad_ref[pl.ds(0, S), :] = dy_ref[...]
        dpad_ref[pl.ds(S, PAD), :] = jnp.zeros((PAD, tc), F32)
        wv = w_ref[...]

        def step(ci, acc):
            r0 = pl.multiple_of(ci * CHUNK, CHUNK)
            past = _past_taps(xpad_ref, r0, width)
            ahead = _future_taps(dpad_ref, r0, width)
            d = ahead[0]
            dx = d * wv[width - 1:width, :]
            for j in range(1, width):
                dx = dx + ahead[j] * wv[width - 1 - j:width - j, :]
            dx_ref[pl.ds(r0, CHUNK), :] = dx.astype(dx_ref.dtype)
            return tuple(acc[k] + _colsum(past[k] * d) for k in range(width)) + (acc[width] + _colsum(d),)

        zero = jnp.zeros((1, tc), F32)
        acc = lax.fori_loop(0, S // CHUNK, step, (zero,) * (width + 1))
        for k in range(width):
            dw_ref[k:k + 1, :] = acc[k]
        db_ref[...] = acc[width]

    return _call(
        body,
        name=name,
        grid=(c_dim // tc,),
        in_specs=[
            pl.BlockSpec((S, tc), lambda j: (0, j)),
            pl.BlockSpec((S, tc), lambda j: (0, col0 // tc + j)),
            pl.BlockSpec((width, tc), lambda j: (0, j)),
        ],
        out_specs=[
            pl.BlockSpec((S, tc), lambda j: (0, j)),
            pl.BlockSpec((width, tc), lambda j: (0, j)),
            pl.BlockSpec((1, tc), lambda j: (0, j)),
        ],
        out_shape=[
            jax.ShapeDtypeStruct((S, c_dim), MXU_DTYPE),
            jax.ShapeDtypeStruct((width, c_dim), F32),
            jax.ShapeDtypeStruct((1, c_dim), F32),
        ],
        scratch_shapes=[pltpu.VMEM((S + PAD, tc), F32), pltpu.VMEM((S + PAD, tc), F32)],
        semantics=("parallel",),
        operands=(dy, src, w),
        side=side,
        window=window,
    )


SCAN_TC = 256


def _lru_gates(rxc, wa, wi, ba, bi, side=None):
    tm = 512

    def body(x_ref, wa_ref, wi_ref, ba_ref, bi_ref, r_ref, i_ref):
        xv = x_ref[...].astype(MXU_DTYPE)
        r_ref[...] = _sigmoid(jnp.dot(xv, wa_ref[...].astype(MXU_DTYPE), preferred_element_type=F32) + ba_ref[...])
        i_ref[...] = _sigmoid(jnp.dot(xv, wi_ref[...].astype(MXU_DTYPE), preferred_element_type=F32) + bi_ref[...])

    x_spec = pl.BlockSpec((tm, RNN_GROUP), lambda g, i: (i, g))
    w_spec = pl.BlockSpec((None, RNN_GROUP, RNN_GROUP), lambda g, i: (g, 0, 0))
    b_spec = pl.BlockSpec((1, RNN_GROUP), lambda g, i: (0, g))
    return _call(
        body,
        name="lru_gates",
        grid=(N_RNN_GROUPS, S // tm),
        in_specs=[x_spec, w_spec, w_spec, b_spec, b_spec],
        out_specs=[x_spec, x_spec],
        out_shape=[jax.ShapeDtypeStruct((S, D_RNN), F32)] * 2,
        semantics=("parallel", "parallel"),
        operands=(rxc, wa, wi, ba, bi),
        side=side,
    )


def _scan_down(a, u, row):
    for d in (1, 2, 4):
        a_s = jnp.where(row >= d, pltpu.roll(a, d, 0), 1.0)
        u_s = jnp.where(row >= d, pltpu.roll(u, d, 0), 0.0)
        u = a * u_s + u
        a = a * a_s
    return a, u


def _scan_up(a, u, row):
    for d in (1, 2, 4):
        a_s = jnp.where(row < 8 - d, pltpu.roll(a, 8 - d, 0), 1.0)
        u_s = jnp.where(row < 8 - d, pltpu.roll(u, 8 - d, 0), 0.0)
        u = a * u_s + u
        a = a * a_s
    return a, u


def _lru_scan_fwd(r, i, rxc, proj, lam, side=None):
    tc = SCAN_TC

    def body(r_ref, i_ref, x_ref, ry_ref, lam_ref, h_ref, y_ref):
        rate = LRU_C * _softplus(-lam_ref[...])
        row = lax.broadcasted_iota(jnp.int32, (8, tc), 0)

        def step(ci, carry):
            r0 = pl.multiple_of(ci * 16, 16)
            log_a = -rate * r_ref[pl.ds(r0, 16), :]
            a16 = jnp.exp(log_a)
            u16 = jnp.sqrt(_one_minus_exp(2.0 * log_a)) * (i_ref[pl.ds(r0, 16), :] * x_ref[pl.ds(r0, 16), :])
            hs = []
            for half in range(2):
                a_cum, h0 = _scan_down(a16[8 * half:8 * half + 8, :], u16[8 * half:8 * half + 8, :], row)
                h = a_cum * carry + h0
                carry = jnp.broadcast_to(h[7:8, :], (8, tc))
                hs.append(h)
            h16 = jnp.concatenate(hs, axis=0)
            h_ref[pl.ds(r0, 16), :] = h16
            y_ref[pl.ds(r0, 16), :] = (h16 * _gelu(ry_ref[pl.ds(r0, 16), :])[0]).astype(y_ref.dtype)
            return carry

        lax.fori_loop(0, S // 16, step, jnp.zeros((8, tc), F32))

    col = pl.BlockSpec((S, tc), lambda j: (0, j))
    return _call(
        body,
        name="lru_scan_fwd",
        grid=(D_RNN // tc,),
        in_specs=[col, col, col, pl.BlockSpec((S, tc), lambda j: (0, OFF_RY // tc + j)),
                  pl.BlockSpec((1, tc), lambda j: (0, j))],
        out_specs=[col, col],
        out_shape=[jax.ShapeDtypeStruct((S, D_RNN), F32), jax.ShapeDtypeStruct((S, D_RNN), MXU_DTYPE)],
        semantics=("parallel",),
        operands=(r, i, rxc, proj, lam),
        side=side,
    )


def _lru_scan_bwd(dy, proj, h, r, i, rxc, lam, side=None, window=None):
    tc = SCAN_TC

    def body(dy_ref, ry_ref, h_ref, r_ref, i_ref, x_ref, lam_ref,
             dry_ref, dzr_ref, dzi_ref, dx_ref, dba_ref, dbi_ref, dlam_ref, a_ref, dh_ref, hp_ref):
        lam_v = lam_ref[...]
        rate = LRU_C * _softplus(-lam_v)
        dlam_scale = LRU_C * _sigmoid(-lam_v)
        row = lax.broadcasted_iota(jnp.int32, (8, tc), 0)
        hp_ref[pl.ds(0, PAD), :] = jnp.zeros((PAD, tc), F32)
        hp_ref[pl.ds(PAD, S), :] = h_ref[...]
        a_ref[pl.ds(S, PAD), :] = jnp.zeros((PAD, tc), F32)

        def prep(ci, carry):
            r0 = pl.multiple_of(ci * CHUNK, CHUNK)
            a_ref[pl.ds(r0, CHUNK), :] = jnp.exp(-rate * r_ref[pl.ds(r0, CHUNK), :])
            ge, dge = _gelu(ry_ref[pl.ds(r0, CHUNK), :])
            dyv = dy_ref[pl.ds(r0, CHUNK), :]
            dh_ref[pl.ds(r0, CHUNK), :] = dyv * ge
            dry_ref[pl.ds(r0, CHUNK), :] = (dyv * h_ref[pl.ds(r0, CHUNK), :] * dge).astype(dry_ref.dtype)
            return carry

        lax.fori_loop(0, S // CHUNK, prep, 0)

        def step(ci, state):
            carry, dba, dbi, dlam = state
            r0 = pl.multiple_of(S - 16 - ci * 16, 16)
            a_ext = a_ref[pl.ds(r0, 24), :]
            a_next = pltpu.roll(a_ext, 23, 0)
            h_prev = pltpu.roll(hp_ref[pl.ds(r0, 24), :], 1, 0)
            dh16 = dh_ref[pl.ds(r0, 16), :]
            gs = [None, None]
            for half in (1, 0):
                lo = 8 * half
                c_cum, g0 = _scan_up(a_next[lo:lo + 8, :], dh16[lo:lo + 8, :], row)
                g = c_cum * carry + g0
                carry = jnp.broadcast_to(g[0:1, :], (8, tc))
                gs[half] = g
            g16 = jnp.concatenate(gs, axis=0)
            a16 = a_ext[0:16, :]
            r16 = r_ref[pl.ds(r0, 16), :]
            i16 = i_ref[pl.ds(r0, 16), :]
            x16 = x_ref[pl.ds(r0, 16), :]
            a2 = a16 * a16
            sq = jnp.sqrt(_one_minus_exp(-2.0 * rate * r16))
            dx_ref[pl.ds(r0, 16), :] = g16 * sq * i16
            dzi = g16 * sq * x16 * i16 * (1.0 - i16)
            dlog_a = g16 * h_prev[8:24, :] * a16 - g16 * i16 * x16 * a2 / sq
            dzr = -rate * dlog_a * r16 * (1.0 - r16)
            dzr_ref[pl.ds(r0, 16), :] = dzr.astype(dzr_ref.dtype)
            dzi_ref[pl.ds(r0, 16), :] = dzi.astype(dzi_ref.dtype)
            return carry, dba + _colsum(dzr), dbi + _colsum(dzi), dlam + _colsum(dlog_a * r16)

        zero = jnp.zeros((1, tc), F32)
        _, dba, dbi, dlam = lax.fori_loop(0, S // 16, step, (jnp.zeros((8, tc), F32), zero, zero, zero))
        dba_ref[...] = dba
        dbi_ref[...] = dbi
        dlam_ref[...] = dlam * dlam_scale

    col = pl.BlockSpec((S, tc), lambda j: (0, j))
    vec = pl.BlockSpec((1, tc), lambda j: (0, j))
    return _call(
        body,
        name="lru_scan_bwd",
        grid=(D_RNN // tc,),
        in_specs=[col, pl.BlockSpec((S, tc), lambda j: (0, OFF_RY // tc + j)), col, col, col, col, vec],
        out_specs=[col, col, col, col, vec, vec, vec],
        out_shape=[jax.ShapeDtypeStruct((S, D_RNN), MXU_DTYPE)] * 3 + [jax.ShapeDtypeStruct((S, D_RNN), F32)]
        + [jax.ShapeDtypeStruct((1, D_RNN), F32)] * 3,
        scratch_shapes=[pltpu.VMEM((S + PAD, tc), F32), pltpu.VMEM((S, tc), F32), pltpu.VMEM((S + PAD, tc), F32)],
        semantics=("parallel",),
        operands=(dy, proj, h, r, i, rxc, lam),
        side=side,
        window=window,
    )


def _lru_gate_wgrad(rxc, dzr, dzi, side=None):
    def body(x_ref, dzr_ref, dzi_ref, dwa_ref, dwi_ref):
        xv = x_ref[...].astype(MXU_DTYPE)
        dims = (((0,), (0,)), ((), ()))
        dwa_ref[...] = lax.dot_general(xv, dzr_ref[...], dims, preferred_element_type=F32)
        dwi_ref[...] = lax.dot_general(xv, dzi_ref[...], dims, preferred_element_type=F32)

    col = pl.BlockSpec((S, RNN_GROUP), lambda g: (0, g))
    w_spec = pl.BlockSpec((None, RNN_GROUP, RNN_GROUP), lambda g: (g, 0, 0))
    return _call(
        body,
        name="lru_gate_wgrad",
        grid=(N_RNN_GROUPS,),
        in_specs=[col, col, col],
        out_specs=[w_spec, w_spec],
        out_shape=[jax.ShapeDtypeStruct((N_RNN_GROUPS, RNN_GROUP, RNN_GROUP), F32)] * 2,
        semantics=("parallel",),
        operands=(rxc, dzr, dzi),
        side=side,
    )


def _lru_gate_xgrad(dzr, dzi, wa, wi, dx_in, side=None):
    tm = 512

    def body(dzr_ref, dzi_ref, wa_ref, wi_ref, dx_ref, o_ref):
        dims = (((1,), (1,)), ((), ()))
        o_ref[...] = (dx_ref[...]
                      + lax.dot_general(dzr_ref[...], wa_ref[...].astype(MXU_DTYPE), dims, preferred_element_type=F32)
                      + lax.dot_general(dzi_ref[...], wi_ref[...].astype(MXU_DTYPE), dims, preferred_element_type=F32))

    x_spec = pl.BlockSpec((tm, RNN_GROUP), lambda g, i: (i, g))
    w_spec = pl.BlockSpec((None, RNN_GROUP, RNN_GROUP), lambda g, i: (g, 0, 0))
    return _call(
        body,
        name="lru_gate_xgrad",
        grid=(N_RNN_GROUPS, S // tm),
        in_specs=[x_spec, x_spec, w_spec, w_spec, x_spec],
        out_specs=x_spec,
        out_shape=jax.ShapeDtypeStruct((S, D_RNN), F32),
        semantics=("parallel", "parallel"),
        operands=(dzr, dzi, wa, wi, dx_in),
        side=side,
    )


def _gate_fwd(y_attn, y_rnn, proj, b_gate, side=None):
    t = 512

    def body(ya_ref, yr_ref, ga_ref, gr_ref, ba_ref, br_ref, o_ref):
        o_ref[...] = (_sigmoid(ga_ref[...] + ba_ref[...]) * ya_ref[...]
                      + _sigmoid(gr_ref[...] + br_ref[...]) * yr_ref[...]).astype(o_ref.dtype)

    tile = pl.BlockSpec((t, t), lambda i, j: (i, j))
    return _call(
        body,
        name="gate_fwd",
        grid=(S // t, D // t),
        in_specs=[tile, tile,
                  pl.BlockSpec((t, t), lambda i, j: (i, OFF_GA // t + j)),
                  pl.BlockSpec((t, t), lambda i, j: (i, OFF_GR // t + j)),
                  pl.BlockSpec((1, t), lambda i, j: (0, j)),
                  pl.BlockSpec((1, t), lambda i, j: (0, D // t + j))],
        out_specs=tile,
        out_shape=jax.ShapeDtypeStruct((S, D), MXU_DTYPE),
        semantics=("parallel", "parallel"),
        operands=(y_attn, y_rnn, proj, proj, b_gate, b_gate),
        side=side,
    )


def _gate_bwd(dmix, y_attn, y_rnn, proj, b_gate, side=None, window=None):
    t = 512

    def body(dm_ref, ya_ref, yr_ref, ga_ref, gr_ref, ba_ref, br_ref,
             dga_ref, dya_ref, dyr_ref, dgr_ref, dba_ref, dbr_ref):
        @pl.when(pl.program_id(1) == 0)
        def _():
            dba_ref[...] = jnp.zeros_like(dba_ref)
            dbr_ref[...] = jnp.zeros_like(dbr_ref)

        dm = dm_ref[...]
        ga = _sigmoid(ga_ref[...] + ba_ref[...])
        gr = _sigmoid(gr_ref[...] + br_ref[...])
        dya_ref[...] = (dm * ga).astype(dya_ref.dtype)
        dyr_ref[...] = (dm * gr).astype(dyr_ref.dtype)
        dga = dm * ya_ref[...] * ga * (1.0 - ga)
        dgr = dm * yr_ref[...] * gr * (1.0 - gr)
        dga_ref[...] = dga.astype(dga_ref.dtype)
        dgr_ref[...] = dgr.astype(dgr_ref.dtype)
        dba_ref[...] += _colsum(dga)
        dbr_ref[...] += _colsum(dgr)

    tile = pl.BlockSpec((t, t), lambda j, i: (i, j))
    vec = pl.BlockSpec((1, t), lambda j, i: (0, j))
    return _call(
        body,
        name="gate_bwd",
        grid=(D // t, S // t),
        in_specs=[tile, tile, tile,
                  pl.BlockSpec((t, t), lambda j, i: (i, OFF_GA // t + j)),
                  pl.BlockSpec((t, t), lambda j, i: (i, OFF_GR // t + j)),
                  vec,
                  pl.BlockSpec((1, t), lambda j, i: (0, D // t + j))],
        out_specs=[tile, tile, tile, tile, vec, vec],
        out_shape=[jax.ShapeDtypeStruct((S, D), MXU_DTYPE)] * 4 + [jax.ShapeDtypeStruct((1, D), F32)] * 2,
        semantics=("parallel", "arbitrary"),
        operands=(dmix, y_attn, y_rnn, proj, proj, b_gate, b_gate),
        side=side,
        window=window,
    )


LN_TM = 256


def _ln_stats(pre):
    mu = jnp.mean(pre, axis=-1, keepdims=True)
    xc = pre - mu
    rstd = lax.rsqrt(jnp.mean(xc * xc, axis=-1, keepdims=True) + LN_EPS)
    return xc * rstd, rstd


def _ln_input_grad(dy, xhat, rstd, g):
    dyg = dy * g
    return rstd * (dyg - jnp.mean(dyg, axis=-1, keepdims=True)
                   - xhat * jnp.mean(dyg * xhat, axis=-1, keepdims=True))


def _ln_fwd(res, branch, g, b, side=None):
    def body(res_ref, br_ref, g_ref, b_ref, y_ref, yb_ref, xhat_ref, rstd_ref):
        xhat, rstd = _ln_stats(ALPHA * res_ref[...] + br_ref[...])
        y = xhat * g_ref[...] + b_ref[...]
        y_ref[...] = y
        yb_ref[...] = y.astype(yb_ref.dtype)
        xhat_ref[...] = xhat
        rstd_ref[...] = rstd

    tile = pl.BlockSpec((LN_TM, D), lambda i: (i, 0))
    vec = pl.BlockSpec((1, D), lambda i: (0, 0))
    return _call(
        body,
        name="ln_fwd",
        grid=(S // LN_TM,),
        in_specs=[tile, tile, vec, vec],
        out_specs=[tile, tile, tile, pl.BlockSpec((LN_TM, 1), lambda i: (i, 0))],
        out_shape=[jax.ShapeDtypeStruct((S, D), F32), jax.ShapeDtypeStruct((S, D), MXU_DTYPE),
                   jax.ShapeDtypeStruct((S, D), F32), jax.ShapeDtypeStruct((S, 1), F32)],
        semantics=("parallel",),
        operands=(res, branch, g, b),
        side=side,
    )


def _ln_bwd(dy_a, dy_b, xhat, rstd, g, side=None):
    def body(da_ref, db_in_ref, xhat_ref, rstd_ref, g_ref, dp_ref, dpb_ref, dg_ref, db_ref):
        @pl.when(pl.program_id(0) == 0)
        def _():
            dg_ref[...] = jnp.zeros_like(dg_ref)
            db_ref[...] = jnp.zeros_like(db_ref)

        dy = da_ref[...] + ALPHA * db_in_ref[...]
        xhat = xhat_ref[...]
        dp = _ln_input_grad(dy, xhat, rstd_ref[...], g_ref[...])
        dp_ref[...] = dp
        dpb_ref[...] = dp.astype(dpb_ref.dtype)
        dg_ref[...] += _colsum(dy * xhat)
        db_ref[...] += _colsum(dy)

    tile = pl.BlockSpec((LN_TM, D), lambda i: (i, 0))
    vec = pl.BlockSpec((1, D), lambda i: (0, 0))
    return _call(
        body,
        name="ln_bwd",
        grid=(S // LN_TM,),
        in_specs=[tile, tile, tile, pl.BlockSpec((LN_TM, 1), lambda i: (i, 0)), vec],
        out_specs=[tile, tile, vec, vec],
        out_shape=[jax.ShapeDtypeStruct((S, D), F32), jax.ShapeDtypeStruct((S, D), MXU_DTYPE),
                   jax.ShapeDtypeStruct((1, D), F32), jax.ShapeDtypeStruct((1, D), F32)],
        semantics=("arbitrary",),
        operands=(dy_a, dy_b, xhat, rstd, g),
        side=side,
    )


def _ln_loss_bwd(res, branch, g, b, target, side=None):
    def body(res_ref, br_ref, g_ref, b_ref, t_ref, loss_ref, dp_ref, dpb_ref, dg_ref, db_ref):
        @pl.when(pl.program_id(0) == 0)
        def _():
            loss_ref[...] = jnp.zeros_like(loss_ref)
            dg_ref[...] = jnp.zeros_like(dg_ref)
            db_ref[...] = jnp.zeros_like(db_ref)

        xhat, rstd = _ln_stats(ALPHA * res_ref[...] + br_ref[...])
        gv = g_ref[...]
        err = xhat * gv + b_ref[...] - t_ref[...]
        loss_ref[...] += (0.5 / D) * jnp.sum(_colsum(err * err), axis=1, keepdims=True)
        dy = err * (1.0 / D)
        dp = _ln_input_grad(dy, xhat, rstd, gv)
        dp_ref[...] = dp
        dpb_ref[...] = dp.astype(dpb_ref.dtype)
        dg_ref[...] += _colsum(dy * xhat)
        db_ref[...] += _colsum(dy)

    tile = pl.BlockSpec((LN_TM, D), lambda i: (i, 0))
    vec = pl.BlockSpec((1, D), lambda i: (0, 0))
    return _call(
        body,
        name="ln_loss_bwd",
        grid=(S // LN_TM,),
        in_specs=[tile, tile, vec, vec, tile],
        out_specs=[pl.BlockSpec((1, 1), lambda i: (0, 0)), tile, tile, vec, vec],
        out_shape=[jax.ShapeDtypeStruct((1, 1), F32), jax.ShapeDtypeStruct((S, D), F32),
                   jax.ShapeDtypeStruct((S, D), MXU_DTYPE),
                   jax.ShapeDtypeStruct((1, D), F32), jax.ShapeDtypeStruct((1, D), F32)],
        semantics=("arbitrary",),
        operands=(res, branch, g, b, target),
        side=side,
    )


FFN_TC = 256


def _ffn_act_fwd(up, gpre, w, b, side=None):
    tc = FFN_TC

    def body(up_ref, x_ref, w_ref, b_ref, o_ref, xpad_ref):
        xpad_ref[pl.ds(0, PAD), :] = jnp.zeros((PAD, tc), F32)
        xpad_ref[pl.ds(PAD, S), :] = x_ref[...]
        wv = w_ref[...]
        bv = b_ref[...]

        def step(ci, carry):
            r0 = pl.multiple_of(ci * CHUNK, CHUNK)
            taps = _past_taps(xpad_ref, r0, FFN_CONV_W)
            gate = bv + taps[0] * wv[0:1, :] + taps[1] * wv[1:2, :] + taps[2] * wv[2:3, :]
            o_ref[pl.ds(r0, CHUNK), :] = (_gelu(gate)[0] * up_ref[pl.ds(r0, CHUNK), :]).astype(o_ref.dtype)
            return carry

        lax.fori_loop(0, S // CHUNK, step, 0)

    col = pl.BlockSpec((S, tc), lambda j: (0, j))
    return _call(
        body,
        name="ffn_act_fwd",
        grid=(D_FF // tc,),
        in_specs=[col, col, pl.BlockSpec((FFN_CONV_W, tc), lambda j: (0, j)), pl.BlockSpec((1, tc), lambda j: (0, j))],
        out_specs=col,
        out_shape=jax.ShapeDtypeStruct((S, D_FF), MXU_DTYPE),
        scratch_shapes=[pltpu.VMEM((S + PAD, tc), F32)],
        semantics=("parallel",),
        operands=(up, gpre, w, b),
        side=side,
    )


def _ffn_act_bwd(dfin, up, gpre, w, b, side=None):
    tc = FFN_TC
    width = FFN_CONV_W

    def body(df_ref, up_ref, x_ref, w_ref, b_ref, dup_ref, dx_ref, dw_ref, db_ref, xpad_ref, dpad_ref):
        xpad_ref[pl.ds(0, PAD), :] = jnp.zeros((PAD, tc), F32)
        xpad_ref[pl.ds(PAD, S), :] = x_ref[...]
        dpad_ref[pl.ds(S, PAD), :] = jnp.zeros((PAD, tc), F32)
        wv = w_ref[...]
        bv = b_ref[...]

        def gate_grad(ci, acc):
            r0 = pl.multiple_of(ci * CHUNK, CHUNK)
            taps = _past_taps(xpad_ref, r0, width)
            gate = bv + taps[0] * wv[0:1, :] + taps[1] * wv[1:2, :] + taps[2] * wv[2:3, :]
            ge, dge = _gelu(gate)
            df = df_ref[pl.ds(r0, CHUNK), :]
            dup_ref[pl.ds(r0, CHUNK), :] = (df * ge).astype(dup_ref.dtype)
            d = df * up_ref[pl.ds(r0, CHUNK), :] * dge
            dpad_ref[pl.ds(r0, CHUNK), :] = d
            return tuple(acc[k] + _colsum(taps[k] * d) for k in range(width)) + (acc[width] + _colsum(d),)

        zero = jnp.zeros((1, tc), F32)
        acc = lax.fori_loop(0, S // CHUNK, gate_grad, (zero,) * (width + 1))
        for k in range(width):
            dw_ref[k:k + 1, :] = acc[k]
        db_ref[...] = acc[width]

        def input_grad(ci, carry):
            r0 = pl.multiple_of(ci * CHUNK, CHUNK)
            ahead = _future_taps(dpad_ref, r0, width)
            dx = ahead[0] * wv[2:3, :] + ahead[1] * wv[1:2, :] + ahead[2] * wv[0:1, :]
            dx_ref[pl.ds(r0, CHUNK), :] = dx.astype(dx_ref.dtype)
            return carry

        lax.fori_loop(0, S // CHUNK, input_grad, 0)

    col = pl.BlockSpec((S, tc), lambda j: (0, j))
    w_spec = pl.BlockSpec((width, tc), lambda j: (0, j))
    vec = pl.BlockSpec((1, tc), lambda j: (0, j))
    return _call(
        body,
        name="ffn_act_bwd",
        grid=(D_FF // tc,),
        in_specs=[col, col, col, w_spec, vec],
        out_specs=[col, col, w_spec, vec],
        out_shape=[jax.ShapeDtypeStruct((S, D_FF), MXU_DTYPE)] * 2
        + [jax.ShapeDtypeStruct((width, D_FF), F32), jax.ShapeDtypeStruct((1, D_FF), F32)],
        scratch_shapes=[pltpu.VMEM((S + PAD, tc), F32), pltpu.VMEM((S + PAD, tc), F32)],
        semantics=("parallel",),
        operands=(dfin, up, gpre, w, b),
        side=side,
    )


def _adamw_update(w, g, m, v):
    m = ADAM_B1 * m + (1.0 - ADAM_B1) * g
    v = ADAM_B2 * v + (1.0 - ADAM_B2) * (g * g)
    m_hat = m / (1.0 - ADAM_B1 ** ADAM_STEP)
    v_hat = v / (1.0 - ADAM_B2 ** ADAM_STEP)
    delta = -ADAM_LR * (m_hat / (jnp.sqrt(v_hat) + ADAM_EPS) + ADAM_WD * w)
    return delta, m, v


def _add_pairs(send, pair, far_index, *, name):
    _, r_dim, c_dim = send.shape
    tr = r_dim // 4

    def body(far_ref, mine_ref, theirs_ref, o_ref):
        o_ref[...] = (mine_ref[...].astype(F32) + theirs_ref[...].astype(F32)).astype(o_ref.dtype)

    return pl.pallas_call(
        body,
        name=name,
        grid_spec=pltpu.PrefetchScalarGridSpec(
            num_scalar_prefetch=1,
            grid=(3, r_dim // tr),
            in_specs=[pl.BlockSpec((None, tr, c_dim), lambda j, i, far: (far[j], i, 0)),
                      pl.BlockSpec((None, tr, c_dim), lambda j, i, far: (1 + j, i, 0))],
            out_specs=pl.BlockSpec((None, tr, c_dim), lambda j, i, far: (j, i, 0)),
        ),
        out_shape=jax.ShapeDtypeStruct((3, r_dim, c_dim), BF16),
        compiler_params=_cparams("parallel", "parallel"),
    )(far_index, send, pair)


def _reduce_adamw(w, m, v, g_own, pair, far, me, *, tr, name, part=0, earlier=None):
    _, r_dim, c_dim = w.shape
    cp = pair.shape[2]

    def body(me_ref, w_ref, m_ref, v_ref, g_ref, pair_ref, far_ref, *refs):
        grad_ref, delta_ref, nm_ref, nv_ref = refs[-4:]
        g = g_ref[...] + pair_ref[...].astype(F32)
        for j in range(3):
            g = g + far_ref[j].astype(F32)
        delta, nm, nv = _adamw_update(w_ref[...], g, m_ref[...], v_ref[...])
        grad_ref[...] = g
        delta_ref[...] = delta
        nm_ref[...] = nm
        nv_ref[...] = nv

    tile = pl.BlockSpec((None, tr, cp), lambda i, me: (0, i, part))
    if g_own.ndim == 3:
        own_spec = pl.BlockSpec((None, tr, cp), lambda i, me: (me[0], i, 0))
    else:
        own_spec = pl.BlockSpec((tr, cp), lambda i, me: (i, 0))
    earlier = list(earlier or ())
    return pl.pallas_call(
        body,
        name=name,
        grid_spec=pltpu.PrefetchScalarGridSpec(
            num_scalar_prefetch=1,
            grid=(r_dim // tr,),
            in_specs=[tile, tile, tile, own_spec, pl.BlockSpec((None, tr, cp), lambda i, me: (0, i, 0)),
                      pl.BlockSpec((3, tr, cp), lambda i, me: (0, i, 0))]
            + [pl.BlockSpec(memory_space=pl.ANY)] * len(earlier),
            out_specs=[tile] * 4,
        ),
        out_shape=[jax.ShapeDtypeStruct((1, r_dim, c_dim), F32)] * 4,
        input_output_aliases={7 + k: k for k in range(len(earlier))},
        compiler_params=_cparams("parallel"),
    )(me, w, m, v, g_own, pair, far, *earlier)


def _adamw_many(ws, ms, vs, gs):
    n = len(ws)

    def body(*refs):
        for i in range(n):
            delta, nm, nv = _adamw_update(refs[i][...], refs[3 * n + i][...], refs[n + i][...], refs[2 * n + i][...])
            refs[4 * n + i][...] = delta
            refs[5 * n + i][...] = nm
            refs[6 * n + i][...] = nv

    vmem = pl.BlockSpec(memory_space=pltpu.VMEM)
    res = pl.pallas_call(
        body,
        name="adamw_small",
        in_specs=[vmem] * (4 * n),
        out_specs=[vmem] * (3 * n),
        out_shape=[jax.ShapeDtypeStruct(w.shape, F32) for w in ws] * 3,
        compiler_params=pltpu.CompilerParams(vmem_limit_bytes=VMEM_LIMIT),
    )(*ws, *ms, *vs, *gs)
    return res[:n], res[n:2 * n], res[2 * n:]


def _adamw_blocks(w, m, v, g, *, name, side=None):
    per = 2

    def body(w_ref, m_ref, v_ref, g_ref, delta_ref, nm_ref, nv_ref):
        delta, nm, nv = _adamw_update(w_ref[...], g_ref[...], m_ref[...], v_ref[...])
        delta_ref[...] = delta
        nm_ref[...] = nm
        nv_ref[...] = nv

    tile = pl.BlockSpec((1, per) + w.shape[2:], lambda i: (0, i, 0, 0))
    return _call(
        body,
        name=name,
        grid=(w.shape[1] // per,),
        in_specs=[tile] * 4,
        out_specs=[tile] * 3,
        out_shape=[jax.ShapeDtypeStruct(w.shape, F32)] * 3,
        semantics=("parallel",),
        operands=(w, m, v, g),
        side=side,
    )


def _coords():
    return lax.axis_index("x"), lax.axis_index("y"), lax.axis_index("c")


def _flip(coord, bit):
    return 1 - coord if bit else coord


def _relative(k):
    x, y, c = _coords()
    return _flip(x, k & 4), _flip(y, k & 2), _flip(c, k & 1)


def _index(pos):
    return 4 * pos[0] + 2 * pos[1] + pos[2]


FAR = (4, 2, 6)
AG_US_PER_MB = 38.0
RS_US_PER_MB = 46.0
MIN_RIDE_US = 30.0
PAIR_EXCHANGE_US = 20.0
MIN_GATHER_RIDE_US = 22.0
ROW_ALIGN = 32


def _chunks(items, cursor, us, us_per_mb, through=None):
    budget = float("inf") if us is None else us / us_per_mb * 2 ** 20
    names = list(items)
    if through is not None:
        names = names[:names.index(through) + 1]
    chunks = []
    for name in names:
        arr = items[name]
        r_dim, c_dim = arr.shape[-2:]
        row_bytes = c_dim * arr.dtype.itemsize
        while cursor[name] < r_dim and budget > 0:
            rows = r_dim - cursor[name]
            if r_dim > ROW_ALIGN and budget < rows * row_bytes:
                rows = min(rows, max(ROW_ALIGN, int(budget // row_bytes) // ROW_ALIGN * ROW_ALIGN))
            chunks.append((name, cursor[name], rows))
            cursor[name] += rows
            budget -= rows * row_bytes
    return chunks


class _Gather:
    def __init__(self, shards):
        self.shards, self.bufs, self.cursor = {}, {}, {}
        self.add_shards(shards)

    def add_shards(self, shards):
        for n, shard in shards.items():
            self.shards[n], self.bufs[n], self.cursor[n] = shard, None, 0

    def take(self, us=None, through=None):
        if us is not None and us < MIN_GATHER_RIDE_US:
            return None
        chunks = _chunks(self.shards, self.cursor, us, AG_US_PER_MB, through)
        return _GatherSide(self, chunks) if chunks else None

    def get(self, name):
        chunks = _chunks(self.shards, self.cursor, None, AG_US_PER_MB, through=name)
        if chunks:
            _run_side(_GatherSide(self, chunks), "gather_" + name)
        return self.bufs[name]


class _GatherSide:
    SEMS = 8

    def __init__(self, owner, chunks):
        self.owner, self.chunks = owner, chunks
        self.names = list(dict.fromkeys(n for n, _, _ in chunks))
        old = [n for n in self.names if owner.bufs[n] is not None]
        self.operands = [owner.shards[n] for n in self.names] + [owner.bufs[n] for n in old]
        self.out_shape = [jax.ShapeDtypeStruct((N_DEV,) + owner.shards[n].shape, owner.shards[n].dtype)
                          for n in self.names]
        self.aliases = {len(self.names) + i: self.names.index(n) for i, n in enumerate(old)}
        self.sems = [pltpu.SemaphoreType.DMA((self.SEMS * len(chunks),)),
                     pltpu.SemaphoreType.DMA((self.SEMS * len(chunks),)), pltpu.SemaphoreType.DMA((len(chunks),))]

    def _halves(self, ci):
        _, r0, rows = self.chunks[ci]
        if rows % ROW_ALIGN:
            return None
        return (r0, rows // 2), (r0 + rows // 2, rows // 2)

    def _copy(self, ins, outs, sems, ci, s, block, to, rows=None, from_shard=False):
        name, r0, n = self.chunks[ci]
        if rows is not None:
            r0, n = rows
        w = self.names.index(name)
        slot = outs[w].at[_index(block), pl.ds(r0, n)]
        return pltpu.make_async_remote_copy(
            src_ref=ins[w].at[pl.ds(r0, n)] if from_shard else slot, dst_ref=slot,
            send_sem=sems[0].at[self.SEMS * ci + s], recv_sem=sems[1].at[self.SEMS * ci + s],
            device_id=to, device_id_type=MESH)

    def _own(self, ins, outs, sems, ci):
        name, r0, rows = self.chunks[ci]
        w = self.names.index(name)
        return pltpu.make_async_copy(ins[w].at[pl.ds(r0, rows)], outs[w].at[_index(_relative(0)), pl.ds(r0, rows)],
                                     sems[2].at[ci])

    def _pass(self, ins, outs, sems, ci, which):
        source, target = ((4, 2), (2, 4))[which]
        return self._copy(ins, outs, sems, ci, 3 + which, _relative(source), _relative(target),
                          rows=self._halves(ci)[which])

    def start(self, ins, outs, sems):
        me = _relative(0)
        for ci in range(len(self.chunks)):
            self._own(ins, outs, sems, ci).start()
        for ci in range(len(self.chunks)):
            self._copy(ins, outs, sems, ci, 1, me, _relative(4), from_shard=True).start()
            self._copy(ins, outs, sems, ci, 2, me, _relative(2), from_shard=True).start()
            if self._halves(ci) is None:
                self._copy(ins, outs, sems, ci, 3, me, _relative(6), from_shard=True).start()
        for ci in range(len(self.chunks)):
            self._copy(ins, outs, sems, ci, 0, me, _relative(1), from_shard=True).start()

    def mid(self, ins, outs, sems):
        me = _relative(0)
        cut = [ci for ci in range(len(self.chunks)) if self._halves(ci) is not None]
        for ci in cut:
            self._copy(ins, outs, sems, ci, 1, _relative(4), me).wait_recv()
            self._pass(ins, outs, sems, ci, 0).start()
            self._copy(ins, outs, sems, ci, 5, _relative(4), _relative(1)).start()
        for ci in cut:
            self._copy(ins, outs, sems, ci, 2, _relative(2), me).wait_recv()
            self._pass(ins, outs, sems, ci, 1).start()
            self._copy(ins, outs, sems, ci, 6, _relative(2), _relative(1)).start()

    def finish(self, ins, outs, sems):
        me, sibling = _relative(0), _relative(1)
        n = len(self.chunks)
        for ci in range(n):
            if self._halves(ci) is None:
                for s, k in ((1, 4), (2, 2), (3, 6)):
                    self._copy(ins, outs, sems, ci, s, _relative(k), me).wait_recv()
                for j, k in enumerate(FAR):
                    self._copy(ins, outs, sems, ci, 5 + j, _relative(k), sibling).start()
            else:
                h0, h1 = self._halves(ci)
                self._copy(ins, outs, sems, ci, 3, _relative(6), me, rows=h0).wait_recv()
                self._copy(ins, outs, sems, ci, 4, _relative(6), me, rows=h1).wait_recv()
                self._copy(ins, outs, sems, ci, 7, _relative(6), sibling).start()
        for ci in range(n):
            self._copy(ins, outs, sems, ci, 0, sibling, me).wait_recv()
            for j, k in enumerate(FAR):
                self._copy(ins, outs, sems, ci, 5 + j, _relative(k | 1), me).wait_recv()
        for ci in range(n):
            self._copy(ins, outs, sems, ci, 0, me, sibling, from_shard=True).wait_send()
            self._copy(ins, outs, sems, ci, 1, me, _relative(4), from_shard=True).wait_send()
            self._copy(ins, outs, sems, ci, 2, me, _relative(2), from_shard=True).wait_send()
            if self._halves(ci) is None:
                self._copy(ins, outs, sems, ci, 3, me, _relative(6), from_shard=True).wait_send()
            else:
                self._pass(ins, outs, sems, ci, 0).wait_send()
                self._pass(ins, outs, sems, ci, 1).wait_send()
            for j, k in enumerate(FAR):
                self._copy(ins, outs, sems, ci, 5 + j, _relative(k), sibling).wait_send()
            self._own(ins, outs, sems, ci).wait()

    def done(self, results):
        for n, buf in zip(self.names, results):
            self.owner.bufs[n] = buf


class _Scatter:
    def __init__(self, me, far_index):
        self.me, self.far_index = me, far_index
        self.sends, self.owns, self.pairs, self.sums, self.fars = {}, {}, {}, {}, {}
        self.pair_cursor, self.far_cursor = {}, {}

    def add(self, name, send, own):
        self.sends[name] = send
        self.owns[name] = own
        self.pairs[name] = self.fars[name] = None
        self.pair_cursor[name] = 0

    def _rows(self, name):
        return self.sends[name].shape[1]

    def _add_ready_pairs(self):
        for name in self.sends:
            if name not in self.sums and self.pair_cursor[name] == self._rows(name):
                self.sums[name] = _add_pairs(self.sends[name], self.pairs[name], self.far_index, name="pair_" + name)
                self.far_cursor[name] = 0

    def _side(self, us, through=None):
        self._add_ready_pairs()
        names = list(self.sends)
        if through is not None:
            names = names[:names.index(through) + 1]
        pair_chunks = [(n, self.pair_cursor[n], self._rows(n) - self.pair_cursor[n]) for n in names
                       if self.pair_cursor[n] < self._rows(n)]
        for n, _, _ in pair_chunks:
            self.pair_cursor[n] = self._rows(n)
        far_chunks = _chunks(self.sums, self.far_cursor, us, RS_US_PER_MB,
                             through if through in self.sums else None) if self.sums else []
        return _ScatterSide(self, pair_chunks, far_chunks) if pair_chunks or far_chunks else None

    def add_blocks(self, name, blocks32, blocks16):
        self.add(name, blocks16, blocks32)

    def take(self, us):
        return self._side(us) if us >= MIN_RIDE_US else None

    def flush_pairs(self, name):
        side = self._side(PAIR_EXCHANGE_US)
        if side is not None:
            _run_side(side, name)
        self._add_ready_pairs()

    def get(self, name):
        step = 0
        while name not in self.sums or self.far_cursor[name] < self._rows(name):
            _run_side(self._side(None, through=name), "scatter_%s_%d" % (name, step))
            step += 1
        return self.owns[name], self.pairs[name], self.fars[name]


class _ScatterSide:
    TO_SIBLING = (1, 5, 3, 7)

    def __init__(self, owner, pair_chunks, far_chunks):
        self.owner, self.pair_chunks, self.far_chunks = owner, pair_chunks, far_chunks
        self.pair_names = list(dict.fromkeys(n for n, _, _ in pair_chunks))
        self.far_names = list(dict.fromkeys(n for n, _, _ in far_chunks))
        ins = [(owner.sends[n], owner.pairs[n], (4,)) for n in self.pair_names]
        ins += [(owner.sums[n], owner.fars[n], (3,)) for n in self.far_names]
        old = [i for i, (_, buf, _) in enumerate(ins) if buf is not None]
        self.operands = [src for src, _, _ in ins] + [ins[i][1] for i in old]
        self.out_shape = [jax.ShapeDtypeStruct(slots + src.shape[1:], BF16) for src, _, slots in ins]
        self.aliases = {len(ins) + j: i for j, i in enumerate(old)}
        n_pair, n_far = 4 * len(pair_chunks), 3 * len(far_chunks)
        self.sems = [pltpu.SemaphoreType.DMA((max(n_pair, 1),)), pltpu.SemaphoreType.DMA((max(n_pair, 1),)),
                     pltpu.SemaphoreType.DMA((max(n_far, 1),)), pltpu.SemaphoreType.DMA((max(n_far, 1),))]

    def _copies(self, ins, outs, sems):
        copies = []
        for ci, (name, r0, rows) in enumerate(self.pair_chunks):
            w = self.pair_names.index(name)
            for j, k in enumerate(self.TO_SIBLING):
                copies.append(pltpu.make_async_remote_copy(
                    src_ref=ins[w].at[_index(_relative(k)), pl.ds(r0, rows)], dst_ref=outs[w].at[j, pl.ds(r0, rows)],
                    send_sem=sems[0].at[4 * ci + j], recv_sem=sems[1].at[4 * ci + j],
                    device_id=_relative(1), device_id_type=MESH))
        for ci, (name, r0, rows) in enumerate(self.far_chunks):
            w = len(self.pair_names) + self.far_names.index(name)
            for j, k in enumerate(FAR):
                copies.append(pltpu.make_async_remote_copy(
                    src_ref=ins[w].at[j, pl.ds(r0, rows)], dst_ref=outs[w].at[j, pl.ds(r0, rows)],
                    send_sem=sems[2].at[3 * ci + j], recv_sem=sems[3].at[3 * ci + j],
                    device_id=_relative(k), device_id_type=MESH))
        return copies

    def start(self, ins, outs, sems):
        for cp in self._copies(ins, outs, sems):
            cp.start()

    def mid(self, ins, outs, sems):
        pass

    def finish(self, ins, outs, sems):
        for cp in self._copies(ins, outs, sems):
            cp.wait()

    def done(self, results):
        for n, buf in zip(self.pair_names, results):
            self.owner.pairs[n] = buf
        for n, buf in zip(self.far_names, results[len(self.pair_names):]):
            self.owner.fars[n] = buf


class _Joined:
    def __init__(self, sides):
        self.sides = sides
        self.operands, self.out_shape, self.sems, self.aliases, self.spans = [], [], [], {}, []
        for s in sides:
            i0, o0, s0 = len(self.operands), len(self.out_shape), len(self.sems)
            self.operands += list(s.operands)
            self.out_shape += list(s.out_shape)
            self.sems += list(s.sems)
            self.aliases.update({i0 + i: o0 + o for i, o in s.aliases.items()})
            self.spans.append((slice(i0, len(self.operands)), slice(o0, len(self.out_shape)),
                               slice(s0, len(self.sems))))

    def start(self, ins, outs, sems):
        for s, (i, o, m) in zip(self.sides, self.spans):
            s.start(ins[i], outs[o], sems[m])

    def mid(self, ins, outs, sems):
        for s, (i, o, m) in zip(self.sides, self.spans):
            s.mid(ins[i], outs[o], sems[m])

    def finish(self, ins, outs, sems):
        for s, (i, o, m) in zip(self.sides, self.spans):
            s.finish(ins[i], outs[o], sems[m])

    def done(self, results):
        for s, (_, o, _) in zip(self.sides, self.spans):
            s.done(results[o])


def _join(*sides):
    sides = [s for s in sides if s is not None]
    if len(sides) <= 1:
        return sides[0] if sides else None
    return _Joined(sides)


PART_W = 768


def _pack_rows(vecs):
    rows = -(-sum(v.shape[0] for v in vecs) // 8) * 8

    def body(*refs):
        out = refs[-1]
        out[...] = jnp.zeros_like(out)
        r0 = 0
        for v in refs[:-1]:
            k, n = v.shape
            for p in range(-(-n // PART_W)):
                w = min(PART_W, n - PART_W * p)
                out[p, r0:r0 + k, 0:w] = v[:, PART_W * p:PART_W * p + w]
            r0 += k

    vmem = pl.BlockSpec(memory_space=pltpu.VMEM)
    return pl.pallas_call(body, name="pack_small", in_specs=[vmem] * len(vecs), out_specs=vmem,
                          out_shape=jax.ShapeDtypeStruct((N_DEV, rows, PART_W), F32))(*vecs)


def _unpack_rows(packed, shapes):
    def body(packed_ref, *outs):
        r0 = 0
        for o in outs:
            k, n = o.shape
            for p in range(-(-n // PART_W)):
                w = min(PART_W, n - PART_W * p)
                o[:, PART_W * p:PART_W * p + w] = packed_ref[p, r0:r0 + k, 0:w]
            r0 += k

    vmem = pl.BlockSpec(memory_space=pltpu.VMEM)
    return pl.pallas_call(body, name="unpack_small", in_specs=[vmem], out_specs=[vmem] * len(shapes),
                          out_shape=[jax.ShapeDtypeStruct(s, F32) for s in shapes])(packed)


class _PartsToOwners:
    def __init__(self, arrays):
        self.n = len(arrays)
        self.pers = [a.shape[0] // N_DEV for a in arrays]
        self.operands, self.aliases = list(arrays), {}
        self.out_shape = [jax.ShapeDtypeStruct((N_DEV, per) + a.shape[1:], a.dtype) for a, per in zip(arrays, self.pers)]
        self.sems = [pltpu.SemaphoreType.DMA((self.n * (N_DEV - 1),))] * 2

    def _copies(self, ins, outs, sems):
        return [pltpu.make_async_remote_copy(
            src_ref=ins[j].at[pl.ds(self.pers[j] * _index(_relative(k)), self.pers[j])], dst_ref=outs[j].at[k],
            send_sem=sems[0].at[self.n * (k - 1) + j], recv_sem=sems[1].at[self.n * (k - 1) + j],
            device_id=_relative(k), device_id_type=MESH) for k in range(1, N_DEV) for j in range(self.n)]

    def start(self, ins, outs, sems):
        for cp in self._copies(ins, outs, sems):
            cp.start()

    def mid(self, ins, outs, sems):
        pass

    def finish(self, ins, outs, sems):
        for cp in self._copies(ins, outs, sems):
            cp.wait()

    def done(self, results):
        self.stages = list(results)


def _sum_parts(arrays, stages):
    n = len(arrays)
    pers = [a.shape[0] // N_DEV for a in arrays]

    def body(*refs):
        me = _index(_relative(0))
        for j in range(n):
            acc = refs[j][pl.ds(pers[j] * me, pers[j])]
            for k in range(1, N_DEV):
                acc = acc + refs[n + j][k].astype(F32)
            refs[2 * n + j][...] = acc

    vmem = pl.BlockSpec(memory_space=pltpu.VMEM)
    return pl.pallas_call(body, name="sum_small_parts", in_specs=[vmem] * (2 * n), out_specs=[vmem] * n,
                          out_shape=[jax.ShapeDtypeStruct((per,) + a.shape[1:], F32) for a, per in zip(arrays, pers)],
                          compiler_params=pltpu.CompilerParams(vmem_limit_bytes=VMEM_LIMIT))(*arrays, *stages)


class _PartsToAll:
    def __init__(self, parts):
        self.n = len(parts)
        self.pers = [p.shape[0] for p in parts]
        self.operands, self.aliases = list(parts), {}
        self.out_shape = [jax.ShapeDtypeStruct((N_DEV * p.shape[0],) + p.shape[1:], F32) for p in parts]
        self.sems = [pltpu.SemaphoreType.DMA((self.n * (N_DEV - 1),))] * 2 + [pltpu.SemaphoreType.DMA((self.n,))]

    def _rows(self, outs, j, pos):
        return outs[j].at[pl.ds(self.pers[j] * _index(pos), self.pers[j])]

    def _copy(self, ins, outs, sems, k, j, owner):
        return pltpu.make_async_remote_copy(
            src_ref=ins[j], dst_ref=self._rows(outs, j, owner),
            send_sem=sems[0].at[self.n * (k - 1) + j], recv_sem=sems[1].at[self.n * (k - 1) + j],
            device_id=_relative(k), device_id_type=MESH)

    def _own(self, ins, outs, sems, j):
        return pltpu.make_async_copy(ins[j], self._rows(outs, j, _relative(0)), sems[2].at[j])

    def start(self, ins, outs, sems):
        for j in range(self.n):
            self._own(ins, outs, sems, j).start()
            for k in range(1, N_DEV):
                self._copy(ins, outs, sems, k, j, _relative(0)).start()

    def mid(self, ins, outs, sems):
        pass

    def finish(self, ins, outs, sems):
        for j in range(self.n):
            for k in range(1, N_DEV):
                self._copy(ins, outs, sems, k, j, _relative(k)).wait_recv()
                self._copy(ins, outs, sems, k, j, _relative(0)).wait_send()
            self._own(ins, outs, sems, j).wait()

    def done(self, results):
        self.totals = list(results)


class _SmallSync:
    def __init__(self, vec_names, mat_names):
        self.vec_names, self.mat_names = vec_names, mat_names

    def begin(self, loss, grads):
        vecs = [loss] + [grads[n] for n in self.vec_names]
        self.shapes = [v.shape for v in vecs]
        self.own = [_diag_blocks(grads[n]) for n in self.mat_names] + [_pack_rows(vecs)]
        self.to_owners = _PartsToOwners([a.astype(BF16) for a in self.own[:-1]] + self.own[-1:])
        return self.to_owners

    def middle(self):
        self.to_all = _PartsToAll(_sum_parts(self.own, self.to_owners.stages))
        return self.to_all

    def end(self):
        *mats, packed = self.to_all.totals
        sums = _unpack_rows(packed, self.shapes)
        return sums[0], dict(zip(self.vec_names, sums[1:])), dict(zip(self.mat_names, mats))


def _block_diag(w):
    groups = []
    for g in range(N_RNN_GROUPS):
        placed = [jnp.pad(w[4 * g + b], ((RNN_BLOCK_W * b, RNN_BLOCK_W * (3 - b)),) * 2) for b in range(4)]
        groups.append(placed[0] + placed[1] + placed[2] + placed[3])
    return jnp.stack(groups)


def _diag_blocks(wg):
    blocks = []
    for n in range(4 * N_RNN_GROUPS):
        g, at = n // 4, RNN_BLOCK_W * (n % 4)
        blocks.append(wg[g, at:at + RNN_BLOCK_W, at:at + RNN_BLOCK_W])
    return jnp.stack(blocks)


def _heads_major(t, n_heads):
    return t.reshape(S, n_heads, HEAD_DIM).transpose(1, 0, 2)


def _heads_minor(t):
    return t.transpose(1, 0, 2).reshape(S, t.shape[0] * HEAD_DIM)


def _natural(gathered, how):
    n, r, c = gathered.shape
    if how == "rows":
        return gathered.reshape(n * r, c)
    return gathered.transpose(1, 0, 2).reshape(r, n * c)


def _blocks(full, how):
    if how == "rows":
        return full.reshape(N_DEV, full.shape[0] // N_DEV, full.shape[1])
    return full.reshape(full.shape[0], N_DEV, full.shape[1] // N_DEV).transpose(1, 0, 2)


def _cast_many(arrays, side=None):
    steps = 4

    def body(*refs):
        n = len(refs) // 2
        for src, dst in zip(refs[:n], refs[n:]):
            dst[...] = src[...].astype(dst.dtype)

    specs = [pl.BlockSpec((a.shape[0] // steps, a.shape[1]), lambda i: (i, 0)) for a in arrays]
    return _call(
        body,
        name="cast_weights",
        grid=(steps,),
        in_specs=specs,
        out_specs=specs,
        out_shape=[jax.ShapeDtypeStruct(a.shape, MXU_DTYPE) for a in arrays],
        semantics=("parallel",),
        operands=tuple(arrays),
        side=side,
    )


def _forward_backward(x2, xb, target, small, gather, scatter, sync):
    w_in_t = _natural(gather.get("w_in"), "rows")
    proj, projb = _mm(xb, w_in_t, tb=True, tm=1024, tn=512, tk=D, out_dtype=(F32, MXU_DTYPE), name="proj",
                      side=gather.take(110))

    qt = projb[:, :OFF_K].T.reshape(N_KV, GROUP, HEAD_DIM, S)
    k2, v2 = projb[:, OFF_K:OFF_V], projb[:, OFF_V:OFF_RX]
    kp = jnp.pad(_heads_major(k2, N_KV), ((0, 0), (BLOCK, 0), (0, 0)))
    vp = jnp.pad(_heads_major(v2, N_KV), ((0, 0), (BLOCK, 0), (0, 0)))
    kt = jnp.pad(k2.T.reshape(N_KV, HEAD_DIM, S), ((0, 0), (0, 0), (BLOCK, 0)))
    vt = jnp.pad(v2.T.reshape(N_KV, HEAD_DIM, S), ((0, 0), (0, 0), (BLOCK, 0)))
    sink_row = jnp.repeat(small["attn_sinks"].reshape(N_KV, 1, GROUP), BLOCK, axis=2)
    ot = _attn_fwd(qt, kp, vt, sink_row, side=gather.take(36)).reshape(D, S)

    rconv_w = _natural(gather.get("rnn_conv_w"), "cols")
    rxc = _conv_fwd(proj, OFF_RX, rconv_w, small["rnn_conv_b"], tc=512, name="rnn_conv_fwd", side=gather.take(18))
    r, i = _lru_gates(rxc, small["lru_wa"], small["lru_wi"], small["lru_ba"], small["lru_bi"], side=gather.take(33))
    h, yrin = _lru_scan_fwd(r, i, rxc, proj, small["lru_lambda"], side=gather.take(53))

    w_ap = _natural(gather.get("w_attn_proj"), "rows")
    w_rp = _natural(gather.get("w_rnn_proj"), "rows")
    y_attn = _mm(ot, w_ap, ta=True, tm=1024, tn=1024, tk=D, name="attn_proj", side=gather.take(22))
    y_rnn = _mm(yrin, w_rp, tm=1024, tn=1024, tk=D_RNN, name="rnn_proj", side=gather.take(27))
    mixin = _gate_fwd(y_attn, y_rnn, proj, small["b_gate"], side=gather.take(25))
    w_out = _natural(gather.get("w_out"), "rows")
    mix = _mm(mixin, w_out, tm=1024, tn=1024, tk=D, name="mix_out", side=gather.take(22))
    x1, x1b, xhat1, rstd1 = _ln_fwd(x2, mix, small["ln1_g"], small["ln1_b"], side=gather.take(23))

    w_up = gather.get("ffn_w_up")
    up = _mm(x1b, w_up, tm=1024, tn=768, tk=D, b_block=768, name="ffn_up", side=gather.take(58))
    w_gate = gather.get("ffn_w_gate")
    gpre = _mm(x1b, w_gate, tm=1024, tn=768, tk=D, b_block=768, name="ffn_gate", side=gather.take(58))
    fconv_w = _natural(gather.get("ffn_conv_w"), "cols")
    fin = _ffn_act_fwd(up, gpre, fconv_w, small["ffn_conv_b"], side=gather.take())
    w_down = _natural(gather.get("ffn_w_down"), "rows")
    f = _mm(fin, w_down, tm=1024, tn=1024, tk=2048, name="ffn_down")
    loss, dpre2, dpre2b, d_ln2_g, d_ln2_b = _ln_loss_bwd(x1, f, small["ln2_g"], small["ln2_b"], target)

    grads = {"ln2_g": d_ln2_g, "ln2_b": d_ln2_b}
    both = (F32, BF16)
    g32, g16 = _mm(fin, dpre2b, ta=True, tm=1024, tn=1024, tk=S, out_dtype=both, name="d_ffn_w_down")
    scatter.add_blocks("ffn_w_down", _blocks(g32, "rows"), _blocks(g16, "rows"))
    dfin = _mm(dpre2b, w_down, tb=True, tm=1024, tn=1024, tk=D, name="d_fin", side=scatter.take(57))
    dup, dgpre, grads["ffn_conv_w"], grads["ffn_conv_b"] = _ffn_act_bwd(
        dfin, up, gpre, fconv_w, small["ffn_conv_b"], side=scatter.take(85))
    g32, g16 = _mm(x1b, dup, ta=True, tm=1024, tn=768, tk=S, out_dtype=both, out_block=768, name="d_ffn_w_up",
                   side=scatter.take(57))
    scatter.add_blocks("ffn_w_up", g32, g16)
    g32, g16 = _mm(x1b, dgpre, ta=True, tm=1024, tn=768, tk=S, out_dtype=both, out_block=768, name="d_ffn_w_gate",
                   side=scatter.take(56))
    scatter.add_blocks("ffn_w_gate", g32, g16)
    dx1 = _mm(dup, w_up, tb=True, tm=1024, tn=1024, tk=768, b_block=768, name="d_x1_up", side=scatter.take(68))
    dx1 = _mm(dgpre, w_gate, tb=True, tm=1024, tn=1024, tk=768, b_block=768, add=dx1, name="d_x1_gate",
              side=scatter.take(70))
    dpre1, dpre1b, grads["ln1_g"], grads["ln1_b"] = _ln_bwd(dx1, dpre2, xhat1, rstd1, small["ln1_g"],
                                                            side=scatter.take(24))

    g32, g16 = _mm(mixin, dpre1b, ta=True, tm=1024, tn=1024, tk=S, out_dtype=both, name="d_w_out",
                   side=scatter.take(26))
    scatter.add_blocks("w_out", _blocks(g32, "rows"), _blocks(g16, "rows"))
    dmix = _mm(dpre1b, w_out, tb=True, tm=1024, tn=1024, tk=D, name="d_mixin", side=scatter.take(22))
    dproj, dya, dyr, dgl_r, db_a, db_r = _gate_bwd(
        dmix, y_attn, y_rnn, proj, small["b_gate"], side=scatter.take(36),
        window=(jax.ShapeDtypeStruct((S, D_IN), MXU_DTYPE), OFF_GA))
    grads["b_gate"] = jnp.concatenate([db_a, db_r], axis=1)
    g32, g16 = _mm(ot, dya, tm=1024, tn=1024, tk=S, out_dtype=both, name="d_w_attn_proj", side=scatter.take(38))
    scatter.add_blocks("w_attn_proj", _blocks(g32, "rows"), _blocks(g16, "rows"))
    g32, g16 = _mm(yrin, dyr, ta=True, tm=1280, tn=1024, tk=S, out_dtype=both, name="d_w_rnn_proj",
                   side=scatter.take(27))
    scatter.add_blocks("w_rnn_proj", _blocks(g32, "rows"), _blocks(g16, "rows"))
    dot_ = _mm(w_ap, dya, tb=True, tm=1024, tn=1024, tk=D, out_dtype=MXU_DTYPE, name="d_o", side=scatter.take(22))
    dyrin = _mm(dyr, w_rp, tb=True, tm=1024, tn=1280, tk=D, name="d_yrin", side=scatter.take(27))

    dproj, dzr, dzi, drxc_in, grads["lru_ba"], grads["lru_bi"], grads["lru_lambda"] = _lru_scan_bwd(
        dyrin, proj, h, r, i, rxc, small["lru_lambda"], side=scatter.take(94), window=(dproj, OFF_RY))
    grads["lru_wa"], grads["lru_wi"] = _lru_gate_wgrad(rxc, dzr, dzi, side=scatter.take(22))
    drxc = _lru_gate_xgrad(dzr, dzi, small["lru_wa"], small["lru_wi"], drxc_in, side=scatter.take(33))
    dproj, grads["rnn_conv_w"], grads["rnn_conv_b"] = _conv_bwd(
        drxc, proj, OFF_RX, rconv_w, tc=512, name="rnn_conv_bwd", side=scatter.take(29), window=(dproj, OFF_RX))

    dqt, dk, dv, dsink = _attn_bwd(qt, kp, kt, vp, sink_row, dot_.reshape(N_KV, GROUP, HEAD_DIM, S),
                                   side=scatter.take(65))
    grads["attn_sinks"] = dsink.reshape(1, N_KV * GROUP)
    for col0, piece in ((0, dqt.reshape(D, S).T), (OFF_K, _heads_minor(dk[:, BLOCK:, :]).astype(MXU_DTYPE)),
                        (OFF_V, _heads_minor(dv[:, BLOCK:, :]).astype(MXU_DTYPE)), (OFF_GR, dgl_r)):
        dproj = lax.dynamic_update_slice(dproj, piece, (0, col0))
    for part in range(W_IN_PARTS):
        cols = slice(part * (D // W_IN_PARTS), (part + 1) * (D // W_IN_PARTS))
        side = _join(scatter.take(55), sync.begin(loss, grads)) if part == 0 else scatter.take(68)
        g32, g16 = _mm(dproj, xb[:, cols], ta=True, tm=512, tn=D // W_IN_PARTS, tk=S, out_dtype=both,
                       name="d_w_in_%d" % part, side=side)
        scatter.add_blocks("w_in_%d" % part, _blocks(g32, "rows"), _blocks(g16, "rows"))
        scatter.flush_pairs("pairs_w_in_%d" % part)
    dx = _mm(dproj, w_in_t, tm=1024, tn=1024, tk=512, add=dpre1, add_scale=ALPHA, name="d_x",
             side=_join(scatter.take(400), sync.middle()))
    return dx


SHARDED = (
    ("w_in", "cols", 368), ("w_attn_proj", "rows", 32), ("w_rnn_proj", "rows", 32), ("w_out", "rows", 32),
    ("ffn_w_up", "cols", 128), ("ffn_w_gate", "cols", 128), ("ffn_w_down", "rows", 64),
)
SMALL_REPLICATED = ("b_gate", "rnn_conv_b", "lru_wa", "lru_ba", "lru_wi", "lru_bi", "lru_lambda", "attn_sinks",
                    "ln1_g", "ln1_b", "ffn_conv_b", "ln2_g", "ln2_b")
SMALL_SHARDED = ("rnn_conv_w", "ffn_conv_w")
SMALL_MATS = ("lru_wa", "lru_wi")
W_IN_PARTS = 2
WEIGHTS = ("w_in", "b_gate", "rnn_conv_w", "rnn_conv_b", "lru_wa", "lru_ba", "lru_wi", "lru_bi", "lru_lambda",
           "attn_sinks", "w_attn_proj", "w_rnn_proj", "w_out", "ln1_g", "ln1_b", "ffn_w_up", "ffn_w_gate",
           "ffn_conv_w", "ffn_conv_b", "ffn_w_down", "ln2_g", "ln2_b")


def kernel(x, w_in, b_gate, rnn_conv_w, rnn_conv_b, lru_wa, lru_ba, lru_wi, lru_bi, lru_lambda, attn_sinks, w_attn_proj, w_rnn_proj, w_out, ln1_g, ln1_b, ffn_w_up, ffn_w_gate, ffn_conv_w, ffn_conv_b, ffn_w_down, ln2_g, ln2_b, loss_target, m_w_in, m_b_gate, m_rnn_conv_w, m_rnn_conv_b, m_lru_wa, m_lru_ba, m_lru_wi, m_lru_bi, m_lru_lambda, m_attn_sinks, m_w_attn_proj, m_w_rnn_proj, m_w_out, m_ln1_g, m_ln1_b, m_ffn_w_up, m_ffn_w_gate, m_ffn_conv_w, m_ffn_conv_b, m_ffn_w_down, m_ln2_g, m_ln2_b, v_w_in, v_b_gate, v_rnn_conv_w, v_rnn_conv_b, v_lru_wa, v_lru_ba, v_lru_wi, v_lru_bi, v_lru_lambda, v_attn_sinks, v_w_attn_proj, v_w_rnn_proj, v_w_out, v_ln1_g, v_ln1_b, v_ffn_w_up, v_ffn_w_gate, v_ffn_conv_w, v_ffn_conv_b, v_ffn_w_down, v_ln2_g, v_ln2_b):
    given = dict(locals())
    wsh = {n: given[n][0] for n in WEIGHTS}
    msh = {n: given["m_" + n][0] for n in WEIGHTS}
    vsh = {n: given["v_" + n][0] for n in WEIGHTS}
    m_given = {n: given["m_" + n] for n in WEIGHTS}
    v_given = {n: given["v_" + n] for n in WEIGHTS}
    me = 4 * lax.axis_index("x") + 2 * lax.axis_index("y") + lax.axis_index("c")

    order = ("w_in", "rnn_conv_w", "ffn_conv_w", "w_attn_proj", "w_rnn_proj", "w_out", "ffn_w_up", "ffn_w_gate",
             "ffn_w_down")
    gather = _Gather({"w_in": wsh["w_in"].T.astype(MXU_DTYPE), **{n: wsh[n] for n in order[1:3]}})
    *casts, xb = _cast_many([wsh[n] for n in order[3:]] + [x[0]], side=gather.take(through="ffn_conv_w"))
    gather.add_shards(dict(zip(order[3:], casts)))
    small = {n: given[n] for n in SMALL_REPLICATED}
    small["lru_wa"] = _block_diag(wsh["lru_wa"])
    small["lru_wi"] = _block_diag(wsh["lru_wi"])
    scatter = _Scatter(me, jnp.stack([_index(_relative(k)) for k in FAR]).astype(jnp.int32))

    vec_names = tuple(n for n in SMALL_REPLICATED if n not in SMALL_MATS) + SMALL_SHARDED
    sync = _SmallSync(vec_names, SMALL_MATS)
    dx = _forward_backward(x[0], xb, loss_target[0], small, gather, scatter, sync)

    loss_total, g_small, mat_sums = sync.end()
    loss_total = loss_total.reshape(())
    for n in SMALL_SHARDED:
        width = wsh[n].shape[1]
        g_small[n] = lax.dynamic_slice_in_dim(g_small[n], me * width, width, axis=1)
    g_small = {n: g_small[n].reshape(given[n].shape) for n in vec_names}
    out = {}
    results = _adamw_many(*[[d[n] for n in vec_names] for d in (given, m_given, v_given, g_small)])
    for n, delta, nm, nv in zip(vec_names, *results):
        out[n] = (g_small[n], delta, nm, nv)
    for n in SMALL_MATS:
        g = mat_sums[n].reshape(given[n].shape)
        out[n] = (g, *_adamw_blocks(given[n], m_given[n], v_given[n], g, name="adamw_" + n))

    tile_rows = {n: tr for n, _, tr in SHARDED}
    me1 = me.reshape(1).astype(jnp.int32)
    res = None
    for n in list(scatter.sends):
        own, pair, far = scatter.get(n)
        if n.startswith("w_in_"):
            part = int(n[len("w_in_"):])
            w_t, m_t, v_t = (a["w_in"].transpose(0, 2, 1) for a in (given, m_given, v_given))
            res = _reduce_adamw(w_t, m_t, v_t, own, pair, far, me1, tr=tile_rows["w_in"], name="adamw_" + n,
                                part=part, earlier=res if part else None)
            out["w_in"] = tuple(r.transpose(0, 2, 1) for r in res)
        else:
            out[n] = tuple(_reduce_adamw(given[n], m_given[n], v_given[n], own, pair, far, me1, tr=tile_rows[n],
                                         name="adamw_" + n))

    outputs = [loss_total, dx[None]]
    for kind in range(4):
        outputs += [out[n][kind] for n in WEIGHTS]
    return tuple(outputs)
```

```python
import math

import jax
import jax.numpy as jnp
from jax import lax
from jax.experimental import pallas as pl
from jax.experimental.pallas import tpu as pltpu

F32 = jnp.float32
BF16 = jnp.bfloat16
MXU_DTYPE = jnp.bfloat16

N_DEV = 8
S = 2048
D = 2048
HEAD_DIM = 64
N_KV = 4
GROUP = 8
BLOCK = 128
D_KV = N_KV * HEAD_DIM
D_RNN = 2560
RNN_GROUP = 640
N_RNN_GROUPS = D_RNN // RNN_GROUP
RNN_BLOCK_W = 160
RNN_CONV_W = 4
LRU_C = 8.0
D_FF = 6144
FFN_CONV_W = 3
D_IN = 11776
OFF_K = 2048
OFF_V = 2304
OFF_RX = 2560
OFF_RY = 5120
OFF_GA = 7680
OFF_GR = 9728
LN_EPS = 1e-5
ALPHA = 2.0 ** 0.25
ADAM_LR = 0.001
ADAM_B1 = 0.9
ADAM_B2 = 0.999
ADAM_EPS = 1e-08
ADAM_WD = 0.01
ADAM_STEP = 10
NEG = -1e30
VMEM_LIMIT = 56 * 1024 * 1024
MID_RIDE_TENTHS = 6
MESH = pl.DeviceIdType.MESH
GELU_C = math.sqrt(2.0 / math.pi)


def _cparams(*sem):
    return pltpu.CompilerParams(dimension_semantics=sem or None, vmem_limit_bytes=VMEM_LIMIT)


def _call(body, *, name, grid, in_specs, out_specs, out_shape, operands, semantics, scratch_shapes=(), side=None,
          window=None):
    single = not isinstance(out_shape, (list, tuple))
    out_shape = [out_shape] if single else list(out_shape)
    out_specs = [out_specs] if single else list(out_specs)
    in_specs = list(in_specs)
    operands = tuple(operands)
    scratch_shapes = list(scratch_shapes)
    hbm = pl.BlockSpec(memory_space=pltpu.HBM)
    aliases = {}
    if window is not None:
        whole, col0 = window
        block, index_map = out_specs[0].block_shape, out_specs[0].index_map
        assert col0 % block[1] == 0 and out_shape[0].dtype == whole.dtype
        out_specs[0] = pl.BlockSpec(block, lambda *g: (index_map(*g)[0], index_map(*g)[1] + col0 // block[1]))
        out_shape[0] = jax.ShapeDtypeStruct(whole.shape, whole.dtype)
        if not isinstance(whole, jax.ShapeDtypeStruct):
            aliases[len(in_specs)] = 0
            in_specs.append(hbm)
            operands += (whole,)
            compute, n_read = body, len(in_specs) - 1

            def body(*refs):
                compute(*refs[:n_read], *refs[n_read + 1:])

    if side is None:
        res = pl.pallas_call(
            body, name=name, grid=grid, in_specs=in_specs, out_specs=out_specs, out_shape=out_shape,
            scratch_shapes=scratch_shapes, input_output_aliases=aliases,
            compiler_params=_cparams(*semantics))(*operands)
        return res[0] if single else res
    n_in, n_out, n_scr = len(in_specs), len(out_shape), len(scratch_shapes)
    s_in, s_out = len(side.operands), len(side.out_shape)
    steps = math.prod(grid)
    mid_step = (steps * MID_RIDE_TENTHS) // 10

    def with_copies(*refs):
        core_in, side_in = refs[:n_in], refs[n_in:n_in + s_in]
        o0 = n_in + s_in
        core_out, side_out = refs[o0:o0 + n_out], refs[o0 + n_out:o0 + n_out + s_out]
        c0 = o0 + n_out + s_out
        core_scr, sems = refs[c0:c0 + n_scr], refs[c0 + n_scr:]
        step = 0
        for d, size in enumerate(grid):
            step = step * size + pl.program_id(d)

        @pl.when(step == 0)
        def _():
            side.start(side_in, side_out, sems)

        body(*core_in, *core_out, *core_scr)

        @pl.when(step == mid_step)
        def _():
            side.mid(side_in, side_out, sems)

        @pl.when(step == steps - 1)
        def _():
            side.finish(side_in, side_out, sems)

    res = pl.pallas_call(
        with_copies, name=name, grid=grid,
        in_specs=in_specs + [hbm] * s_in, out_specs=out_specs + [hbm] * s_out,
        out_shape=out_shape + list(side.out_shape),
        scratch_shapes=scratch_shapes + list(side.sems),
        input_output_aliases={**aliases, **{n_in + i: n_out + o for i, o in side.aliases.items()}},
        compiler_params=_cparams(*(("arbitrary",) * len(grid))))(*operands, *side.operands)
    side.done(res[n_out:])
    return res[0] if single else res[:n_out]


def _run_side(side, name):
    def body(*refs):
        s_in, s_out = len(side.operands), len(side.out_shape)
        side.start(refs[:s_in], refs[s_in:s_in + s_out], refs[s_in + s_out:])
        side.mid(refs[:s_in], refs[s_in:s_in + s_out], refs[s_in + s_out:])
        side.finish(refs[:s_in], refs[s_in:s_in + s_out], refs[s_in + s_out:])

    hbm = pl.BlockSpec(memory_space=pltpu.HBM)
    res = pl.pallas_call(
        body, name=name, in_specs=[hbm] * len(side.operands), out_specs=[hbm] * len(side.out_shape),
        out_shape=list(side.out_shape), scratch_shapes=list(side.sems),
        input_output_aliases=dict(side.aliases))(*side.operands)
    side.done(res)


def _gelu(x):
    x2 = x * x
    t = jnp.tanh(GELU_C * (x + 0.044715 * x * x2))
    g = 0.5 * x * (1.0 + t)
    dg = 0.5 * (1.0 + t) + 0.5 * x * (1.0 - t * t) * (GELU_C * (1.0 + 3.0 * 0.044715 * x2))
    return g, dg


def _sigmoid(x):
    return 1.0 / (1.0 + jnp.exp(-x))


def _softplus(x):
    z = jnp.exp(-jnp.abs(x))
    small = z * (1.0 - z * (0.5 - z * (1.0 / 3.0 - 0.25 * z)))
    return jnp.maximum(x, 0.0) + jnp.where(z < 0.02, small, jnp.log(1.0 + z))


def _one_minus_exp(x):
    series = -x * (1.0 + x * (0.5 + x * (1.0 / 6.0 + x * (1.0 / 24.0))))
    return jnp.where(x > -0.03, series, 1.0 - jnp.exp(x))


def _colsum(v):
    return jnp.sum(v, axis=0, keepdims=True)


def _mm(a, b, *, tm, tn, tk, name, ta=False, tb=False, out_dtype=F32, b_block=None, out_block=None, add=None,
        add_scale=1.0, side=None):
    out_dtypes = out_dtype if isinstance(out_dtype, tuple) else (out_dtype,)
    if ta:
        k_dim, m_dim = a.shape
    else:
        m_dim, k_dim = a.shape
    if b_block is None:
        n_dim = b.shape[0] if tb else b.shape[1]
    else:
        n_dim = b.shape[1] if tb else b.shape[0] * b_block
    assert m_dim % tm == 0 and n_dim % tn == 0 and k_dim % tk == 0, (name, m_dim, n_dim, k_dim)
    nk = k_dim // tk
    dims = (((0 if ta else 1,), (1 if tb else 0,)), ((), ()))
    has_add = add is not None

    def body(*refs):
        a_ref, b_ref = refs[0], refs[1]
        add_ref = refs[2] if has_add else None
        first_out = 3 if has_add else 2
        o_refs = refs[first_out:first_out + len(out_dtypes)]

        def product():
            return lax.dot_general(a_ref[...].astype(MXU_DTYPE), b_ref[...].astype(MXU_DTYPE), dims,
                                   preferred_element_type=F32)

        def finish(acc):
            if has_add:
                acc = acc + add_scale * add_ref[...]
            for o_ref in o_refs:
                o_ref[...] = acc.astype(o_ref.dtype)

        if nk == 1:
            finish(product())
        else:
            acc_ref = refs[-1]
            k = pl.program_id(2)

            @pl.when(k == 0)
            def _():
                acc_ref[...] = jnp.zeros_like(acc_ref)

            acc_ref[...] += product()

            @pl.when(k == nk - 1)
            def _():
                finish(acc_ref[...])

    if ta:
        a_spec = pl.BlockSpec((tk, tm), lambda i, j, k: (k, i))
    else:
        a_spec = pl.BlockSpec((tm, tk), lambda i, j, k: (i, k))
    if b_block is None:
        if tb:
            b_spec = pl.BlockSpec((tn, tk), lambda i, j, k: (j, k))
        else:
            b_spec = pl.BlockSpec((tk, tn), lambda i, j, k: (k, j))
    elif tb:
        assert b_block % tk == 0
        b_spec = pl.BlockSpec((None, tn, tk), lambda i, j, k: ((k * tk) // b_block, j, ((k * tk) % b_block) // tk))
    else:
        assert b_block % tn == 0
        b_spec = pl.BlockSpec((None, tk, tn), lambda i, j, k: ((j * tn) // b_block, k, ((j * tn) % b_block) // tn))
    in_specs = [a_spec, b_spec]
    operands = [a, b]
    if has_add:
        in_specs.append(pl.BlockSpec((tm, tn), lambda i, j, k: (i, j)))
        operands.append(add)
    if out_block is None:
        out_spec = pl.BlockSpec((tm, tn), lambda i, j, k: (i, j))
        out_dims = (m_dim, n_dim)
    else:
        assert out_block % tn == 0
        out_spec = pl.BlockSpec((None, tm, tn), lambda i, j, k: ((j * tn) // out_block, i, ((j * tn) % out_block) // tn))
        out_dims = (n_dim // out_block, m_dim, out_block)
    res = _call(
        body,
        name=name,
        grid=(m_dim // tm, n_dim // tn, nk),
        in_specs=in_specs,
        out_specs=[out_spec] * len(out_dtypes),
        out_shape=[jax.ShapeDtypeStruct(out_dims, dt) for dt in out_dtypes],
        scratch_shapes=[pltpu.VMEM((tm, tn), F32)] if nk > 1 else [],
        semantics=("parallel", "parallel", "arbitrary"),
        operands=tuple(operands),
        side=side,
    )
    return res if isinstance(out_dtype, tuple) else res[0]


def _attn_bias(bias_ref, h):
    key = lax.broadcasted_iota(jnp.int32, (2 * BLOCK, GROUP * BLOCK), 0)
    col = lax.broadcasted_iota(jnp.int32, (2 * BLOCK, GROUP * BLOCK), 1)
    dist = BLOCK + (col & (BLOCK - 1)) - key
    head = h * GROUP + (col >> 7) + 1
    slope = jnp.exp(head.astype(F32) * (-0.25 * math.log(2.0)))
    bias = jnp.where((dist >= 0) & (dist < BLOCK), -slope * dist.astype(F32), NEG)
    bias_ref[1] = bias
    bias_ref[0] = jnp.where(key < BLOCK, NEG, bias)


def _attn_probs(kb, qt, bias, sink):
    s = jnp.dot(kb, qt, preferred_element_type=F32) * (HEAD_DIM ** -0.5) + bias
    m = jnp.maximum(jnp.max(s, axis=0, keepdims=True), sink)
    e = jnp.exp(s - m)
    e_sink = jnp.exp(sink - m)
    inv = 1.0 / (jnp.sum(e, axis=0, keepdims=True) + e_sink)
    return e * inv, e_sink * inv


def _heads_on_lanes(ref, r0):
    return jnp.concatenate([ref[g, :, pl.ds(r0, BLOCK)] for g in range(GROUP)], axis=1)


def _attn_fwd(qt, kp, vt, sink_row, side=None):
    cols = GROUP * BLOCK

    def body(q_ref, k_ref, vt_ref, sink_ref, o_ref, bias_ref):
        _attn_bias(bias_ref, pl.program_id(0))
        sink = sink_ref[...]

        def step(n, carry):
            r0 = pl.multiple_of(n * BLOCK, BLOCK)
            p, _ = _attn_probs(k_ref[pl.ds(r0, 2 * BLOCK), :], _heads_on_lanes(q_ref, r0),
                               bias_ref[jnp.minimum(n, 1)], sink)
            o = jnp.dot(vt_ref[:, pl.ds(r0, 2 * BLOCK)], p.astype(MXU_DTYPE), preferred_element_type=F32)
            for g in range(GROUP):
                o_ref[g, :, pl.ds(r0, BLOCK)] = o[:, g * BLOCK:(g + 1) * BLOCK].astype(o_ref.dtype)
            return carry

        lax.fori_loop(0, S // BLOCK, step, 0)

    hm = pl.BlockSpec((None, GROUP, HEAD_DIM, S), lambda h: (h, 0, 0, 0))
    return _call(
        body,
        name="attn_fwd",
        grid=(N_KV,),
        in_specs=[
            hm,
            pl.BlockSpec((None, BLOCK + S, HEAD_DIM), lambda h: (h, 0, 0)),
            pl.BlockSpec((None, HEAD_DIM, BLOCK + S), lambda h: (h, 0, 0)),
            pl.BlockSpec((None, 1, cols), lambda h: (h, 0, 0)),
        ],
        out_specs=hm,
        out_shape=jax.ShapeDtypeStruct((N_KV, GROUP, HEAD_DIM, S), MXU_DTYPE),
        scratch_shapes=[pltpu.VMEM((2, 2 * BLOCK, cols), F32)],
        semantics=("parallel",),
        operands=(qt, kp, vt, sink_row),
        side=side,
    )


def _attn_bwd(qt, kp, kt, vp, sink_row, dot_, side=None):
    cols = GROUP * BLOCK

    def body(q_ref, k_ref, kt_ref, v_ref, sink_ref, do_ref, dq_ref, dk_ref, dv_ref, dsink_ref, bias_ref):
        _attn_bias(bias_ref, pl.program_id(0))
        sink = sink_ref[...]
        dk_ref[...] = jnp.zeros_like(dk_ref)
        dv_ref[...] = jnp.zeros_like(dv_ref)
        nt = (((1,), (1,)), ((), ()))

        def step(n, sink_acc):
            r0 = pl.multiple_of(n * BLOCK, BLOCK)
            band = pl.ds(r0, 2 * BLOCK)
            qn = _heads_on_lanes(q_ref, r0)
            don = _heads_on_lanes(do_ref, r0)
            p, p_sink = _attn_probs(k_ref[band, :], qn, bias_ref[jnp.minimum(n, 1)], sink)
            dp = jnp.dot(v_ref[band, :], don, preferred_element_type=F32)
            delta = jnp.sum(p * dp, axis=0, keepdims=True)
            ds = (p * (dp - delta) * (HEAD_DIM ** -0.5)).astype(MXU_DTYPE)
            dq = jnp.dot(kt_ref[:, band], ds, preferred_element_type=F32)
            for g in range(GROUP):
                dq_ref[g, :, pl.ds(r0, BLOCK)] = dq[:, g * BLOCK:(g + 1) * BLOCK].astype(dq_ref.dtype)
            dk_ref[band, :] += lax.dot_general(ds, qn, nt, preferred_element_type=F32)
            dv_ref[band, :] += lax.dot_general(p.astype(MXU_DTYPE), don, nt, preferred_element_type=F32)
            return sink_acc - p_sink * delta

        sink_acc = lax.fori_loop(0, S // BLOCK, step, jnp.zeros((1, cols), F32))
        for g in range(GROUP):
            dsink_ref[g:g + 1, :] = jnp.sum(sink_acc[:, g * BLOCK:(g + 1) * BLOCK], axis=1, keepdims=True)

    hm = pl.BlockSpec((None, GROUP, HEAD_DIM, S), lambda h: (h, 0, 0, 0))
    kv = pl.BlockSpec((None, BLOCK + S, HEAD_DIM), lambda h: (h, 0, 0))
    return _call(
        body,
        name="attn_bwd",
        grid=(N_KV,),
        in_specs=[hm, kv, pl.BlockSpec((None, HEAD_DIM, BLOCK + S), lambda h: (h, 0, 0)), kv,
                  pl.BlockSpec((None, 1, cols), lambda h: (h, 0, 0)), hm],
        out_specs=[hm, kv, kv, pl.BlockSpec((None, GROUP, 1), lambda h: (h, 0, 0))],
        out_shape=[
            jax.ShapeDtypeStruct((N_KV, GROUP, HEAD_DIM, S), MXU_DTYPE),
            jax.ShapeDtypeStruct((N_KV, BLOCK + S, HEAD_DIM), F32),
            jax.ShapeDtypeStruct((N_KV, BLOCK + S, HEAD_DIM), F32),
            jax.ShapeDtypeStruct((N_KV, GROUP, 1), F32),
        ],
        scratch_shapes=[pltpu.VMEM((2, 2 * BLOCK, cols), F32)],
        semantics=("parallel",),
        operands=(qt, kp, kt, vp, sink_row, dot_),
        side=side,
    )


PAD = 8
CHUNK = 256


def _past_taps(xpad_ref, r0, width):
    ext = xpad_ref[pl.ds(r0, CHUNK + PAD), :]
    taps = []
    for k in range(width):
        back = width - 1 - k
        taps.append((ext if back == 0 else pltpu.roll(ext, back, 0))[PAD:, :])
    return taps


def _future_taps(xpad_ref, r0, width):
    ext = xpad_ref[pl.ds(r0, CHUNK + PAD), :]
    taps = []
    for ahead in range(width):
        taps.append((ext if ahead == 0 else pltpu.roll(ext, CHUNK + PAD - ahead, 0))[:CHUNK, :])
    return taps


def _conv_fwd(src, col0, w, b, *, tc, name, side=None):
    width, c_dim = w.shape

    def body(x_ref, w_ref, b_ref, o_ref, xpad_ref):
        xpad_ref[pl.ds(0, PAD), :] = jnp.zeros((PAD, tc), F32)
        xpad_ref[pl.ds(PAD, S), :] = x_ref[...]
        wv = w_ref[...]
        bv = b_ref[...]

        def step(ci, carry):
            r0 = pl.multiple_of(ci * CHUNK, CHUNK)
            taps = _past_taps(xpad_ref, r0, width)
            y = bv + taps[0] * wv[0:1, :]
            for k in range(1, width):
                y = y + taps[k] * wv[k:k + 1, :]
            o_ref[pl.ds(r0, CHUNK), :] = y
            return carry

        lax.fori_loop(0, S // CHUNK, step, 0)

    return _call(
        body,
        name=name,
        grid=(c_dim // tc,),
        in_specs=[
            pl.BlockSpec((S, tc), lambda j: (0, col0 // tc + j)),
            pl.BlockSpec((width, tc), lambda j: (0, j)),
            pl.BlockSpec((1, tc), lambda j: (0, j)),
        ],
        out_specs=pl.BlockSpec((S, tc), lambda j: (0, j)),
        out_shape=jax.ShapeDtypeStruct((S, c_dim), F32),
        scratch_shapes=[pltpu.VMEM((S + PAD, tc), F32)],
        semantics=("parallel",),
        operands=(src, w, b),
        side=side,
    )


def _conv_bwd(dy, src, col0, w, *, tc, name, side=None, window=None):
    width, c_dim = w.shape

    def body(dy_ref, x_ref, w_ref, dx_ref, dw_ref, db_ref, xpad_ref, dpad_ref):
        xpad_ref[pl.ds(0, PAD), :] = jnp.zeros((PAD, tc), F32)
        xpad_ref[pl.ds(PAD, S), :] = x_ref[...]
        dpad_ref[pl.ds(0, S), :] = dy_ref[...]
        dpad_ref[pl.ds(S, PAD), :] = jnp.zeros((PAD, tc), F32)
        wv = w_ref[...]

        def step(ci, acc):
            r0 = pl.multiple_of(ci * CHUNK, CHUNK)
            past = _past_taps(xpad_ref, r0, width)
            ahead = _future_taps(dpad_ref, r0, width)
            d = ahead[0]
            dx = d * wv[width - 1:width, :]
            for j in range(1, width):
                dx = dx + ahead[j] * wv[width - 1 - j:width - j, :]
            dx_ref[pl.ds(r0, CHUNK), :] = dx.astype(dx_ref.dtype)
            return tuple(acc[k] + _colsum(past[k] * d) for k in range(width)) + (acc[width] + _colsum(d),)

        zero = jnp.zeros((1, tc), F32)
        acc = lax.fori_loop(0, S // CHUNK, step, (zero,) * (width + 1))
        for k in range(width):
            dw_ref[k:k + 1, :] = acc[k]
        db_ref[...] = acc[width]

    return _call(
        body,
        name=name,
        grid=(c_dim // tc,),
        in_specs=[
            pl.BlockSpec((S, tc), lambda j: (0, j)),
            pl.BlockSpec((S, tc), lambda j: (0, col0 // tc + j)),
            pl.BlockSpec((width, tc), lambda j: (0, j)),
        ],
        out_specs=[
            pl.BlockSpec((S, tc), lambda j: (0, j)),
            pl.BlockSpec((width, tc), lambda j: (0, j)),
            pl.BlockSpec((1, tc), lambda j: (0, j)),
        ],
        out_shape=[
            jax.ShapeDtypeStruct((S, c_dim), MXU_DTYPE),
            jax.ShapeDtypeStruct((width, c_dim), F32),
            jax.ShapeDtypeStruct((1, c_dim), F32),
        ],
        scratch_shapes=[pltpu.VMEM((S + PAD, tc), F32), pltpu.VMEM((S + PAD, tc), F32)],
        semantics=("parallel",),
        operands=(dy, src, w),
        side=side,
        window=window,
    )


SCAN_TC = 256


def _lru_gates(rxc, wa, wi, ba, bi, side=None):
    tm = 512

    def body(x_ref, wa_ref, wi_ref, ba_ref, bi_ref, r_ref, i_ref):
        xv = x_ref[...].astype(MXU_DTYPE)
        r_ref[...] = _sigmoid(jnp.dot(xv, wa_ref[...].astype(MXU_DTYPE), preferred_element_type=F32) + ba_ref[...])
        i_ref[...] = _sigmoid(jnp.dot(xv, wi_ref[...].astype(MXU_DTYPE), preferred_element_type=F32) + bi_ref[...])

    x_spec = pl.BlockSpec((tm, RNN_GROUP), lambda g, i: (i, g))
    w_spec = pl.BlockSpec((None, RNN_GROUP, RNN_GROUP), lambda g, i: (g, 0, 0))
    b_spec = pl.BlockSpec((1, RNN_GROUP), lambda g, i: (0, g))
    return _call(
        body,
        name="lru_gates",
        grid=(N_RNN_GROUPS, S // tm),
        in_specs=[x_spec, w_spec, w_spec, b_spec, b_spec],
        out_specs=[x_spec, x_spec],
        out_shape=[jax.ShapeDtypeStruct((S, D_RNN), F32)] * 2,
        semantics=("parallel", "parallel"),
        operands=(rxc, wa, wi, ba, bi),
        side=side,
    )


def _scan_down(a, u, row):
    for d in (1, 2, 4):
        a_s = jnp.where(row >= d, pltpu.roll(a, d, 0), 1.0)
        u_s = jnp.where(row >= d, pltpu.roll(u, d, 0), 0.0)
        u = a * u_s + u
        a = a * a_s
    return a, u


def _scan_up(a, u, row):
    for d in (1, 2, 4):
        a_s = jnp.where(row < 8 - d, pltpu.roll(a, 8 - d, 0), 1.0)
        u_s = jnp.where(row < 8 - d, pltpu.roll(u, 8 - d, 0), 0.0)
        u = a * u_s + u
        a = a * a_s
    return a, u


def _lru_scan_fwd(r, i, rxc, proj, lam, side=None):
    tc = SCAN_TC

    def body(r_ref, i_ref, x_ref, ry_ref, lam_ref, h_ref, y_ref):
        rate = LRU_C * _softplus(-lam_ref[...])
        row = lax.broadcasted_iota(jnp.int32, (8, tc), 0)

        def step(ci, carry):
            r0 = pl.multiple_of(ci * 16, 16)
            log_a = -rate * r_ref[pl.ds(r0, 16), :]
            a16 = jnp.exp(log_a)
            u16 = jnp.sqrt(_one_minus_exp(2.0 * log_a)) * (i_ref[pl.ds(r0, 16), :] * x_ref[pl.ds(r0, 16), :])
            hs = []
            for half in range(2):
                a_cum, h0 = _scan_down(a16[8 * half:8 * half + 8, :], u16[8 * half:8 * half + 8, :], row)
                h = a_cum * carry + h0
                carry = jnp.broadcast_to(h[7:8, :], (8, tc))
                hs.append(h)
            h16 = jnp.concatenate(hs, axis=0)
            h_ref[pl.ds(r0, 16), :] = h16
            y_ref[pl.ds(r0, 16), :] = (h16 * _gelu(ry_ref[pl.ds(r0, 16), :])[0]).astype(y_ref.dtype)
            return carry

        lax.fori_loop(0, S // 16, step, jnp.zeros((8, tc), F32))

    col = pl.BlockSpec((S, tc), lambda j: (0, j))
    return _call(
        body,
        name="lru_scan_fwd",
        grid=(D_RNN // tc,),
        in_specs=[col, col, col, pl.BlockSpec((S, tc), lambda j: (0, OFF_RY // tc + j)),
                  pl.BlockSpec((1, tc), lambda j: (0, j))],
        out_specs=[col, col],
        out_shape=[jax.ShapeDtypeStruct((S, D_RNN), F32), jax.ShapeDtypeStruct((S, D_RNN), MXU_DTYPE)],
        semantics=("parallel",),
        operands=(r, i, rxc, proj, lam),
        side=side,
    )


def _lru_scan_bwd(dy, proj, h, r, i, rxc, lam, side=None, window=None):
    tc = SCAN_TC

    def body(dy_ref, ry_ref, h_ref, r_ref, i_ref, x_ref, lam_ref,
             dry_ref, dzr_ref, dzi_ref, dx_ref, dba_ref, dbi_ref, dlam_ref, a_ref, dh_ref, hp_ref):
        lam_v = lam_ref[...]
        rate = LRU_C * _softplus(-lam_v)
        dlam_scale = LRU_C * _sigmoid(-lam_v)
        row = lax.broadcasted_iota(jnp.int32, (8, tc), 0)
        hp_ref[pl.ds(0, PAD), :] = jnp.zeros((PAD, tc), F32)
        hp_ref[pl.ds(PAD, S), :] = h_ref[...]
        a_ref[pl.ds(S, PAD), :] = jnp.zeros((PAD, tc), F32)

        def prep(ci, carry):
            r0 = pl.multiple_of(ci * CHUNK, CHUNK)
            a_ref[pl.ds(r0, CHUNK), :] = jnp.exp(-rate * r_ref[pl.ds(r0, CHUNK), :])
            ge, dge = _gelu(ry_ref[pl.ds(r0, CHUNK), :])
            dyv = dy_ref[pl.ds(r0, CHUNK), :]
            dh_ref[pl.ds(r0, CHUNK), :] = dyv * ge
            dry_ref[pl.ds(r0, CHUNK), :] = (dyv * h_ref[pl.ds(r0, CHUNK), :] * dge).astype(dry_ref.dtype)
            return carry

        lax.fori_loop(0, S // CHUNK, prep, 0)

        def step(ci, state):
            carry, dba, dbi, dlam = state
            r0 = pl.multiple_of(S - 16 - ci * 16, 16)
            a_ext = a_ref[pl.ds(r0, 24), :]
            a_next = pltpu.roll(a_ext, 23, 0)
            h_prev = pltpu.roll(hp_ref[pl.ds(r0, 24), :], 1, 0)
            dh16 = dh_ref[pl.ds(r0, 16), :]
            gs = [None, None]
            for half in (1, 0):
                lo = 8 * half
                c_cum, g0 = _scan_up(a_next[lo:lo + 8, :], dh16[lo:lo + 8, :], row)
                g = c_cum * carry + g0
                carry = jnp.broadcast_to(g[0:1, :], (8, tc))
                gs[half] = g
            g16 = jnp.concatenate(gs, axis=0)
            a16 = a_ext[0:16, :]
            r16 = r_ref[pl.ds(r0, 16), :]
            i16 = i_ref[pl.ds(r0, 16), :]
            x16 = x_ref[pl.ds(r0, 16), :]
            a2 = a16 * a16
            sq = jnp.sqrt(_one_minus_exp(-2.0 * rate * r16))
            dx_ref[pl.ds(r0, 16), :] = g16 * sq * i16
            dzi = g16 * sq * x16 * i16 * (1.0 - i16)
            dlog_a = g16 * h_prev[8:24, :] * a16 - g16 * i16 * x16 * a2 / sq
            dzr = -rate * dlog_a * r16 * (1.0 - r16)
            dzr_ref[pl.ds(r0, 16), :] = dzr.astype(dzr_ref.dtype)
            dzi_ref[pl.ds(r0, 16), :] = dzi.astype(dzi_ref.dtype)
            return carry, dba + _colsum(dzr), dbi + _colsum(dzi), dlam + _colsum(dlog_a * r16)

        zero = jnp.zeros((1, tc), F32)
        _, dba, dbi, dlam = lax.fori_loop(0, S // 16, step, (jnp.zeros((8, tc), F32), zero, zero, zero))
        dba_ref[...] = dba
        dbi_ref[...] = dbi
        dlam_ref[...] = dlam * dlam_scale

    col = pl.BlockSpec((S, tc), lambda j: (0, j))
    vec = pl.BlockSpec((1, tc), lambda j: (0, j))
    return _call(
        body,
        name="lru_scan_bwd",
        grid=(D_RNN // tc,),
        in_specs=[col, pl.BlockSpec((S, tc), lambda j: (0, OFF_RY // tc + j)), col, col, col, col, vec],
        out_specs=[col, col, col, col, vec, vec, vec],
        out_shape=[jax.ShapeDtypeStruct((S, D_RNN), MXU_DTYPE)] * 3 + [jax.ShapeDtypeStruct((S, D_RNN), F32)]
        + [jax.ShapeDtypeStruct((1, D_RNN), F32)] * 3,
        scratch_shapes=[pltpu.VMEM((S + PAD, tc), F32), pltpu.VMEM((S, tc), F32), pltpu.VMEM((S + PAD, tc), F32)],
        semantics=("parallel",),
        operands=(dy, proj, h, r, i, rxc, lam),
        side=side,
        window=window,
    )


def _lru_gate_wgrad(rxc, dzr, dzi, side=None):
    def body(x_ref, dzr_ref, dzi_ref, dwa_ref, dwi_ref):
        xv = x_ref[...].astype(MXU_DTYPE)
        dims = (((0,), (0,)), ((), ()))
        dwa_ref[...] = lax.dot_general(xv, dzr_ref[...], dims, preferred_element_type=F32)
        dwi_ref[...] = lax.dot_general(xv, dzi_ref[...], dims, preferred_element_type=F32)

    col = pl.BlockSpec((S, RNN_GROUP), lambda g: (0, g))
    w_spec = pl.BlockSpec((None, RNN_GROUP, RNN_GROUP), lambda g: (g, 0, 0))
    return _call(
        body,
        name="lru_gate_wgrad",
        grid=(N_RNN_GROUPS,),
        in_specs=[col, col, col],
        out_specs=[w_spec, w_spec],
        out_shape=[jax.ShapeDtypeStruct((N_RNN_GROUPS, RNN_GROUP, RNN_GROUP), F32)] * 2,
        semantics=("parallel",),
        operands=(rxc, dzr, dzi),
        side=side,
    )


def _lru_gate_xgrad(dzr, dzi, wa, wi, dx_in, side=None):
    tm = 512

    def body(dzr_ref, dzi_ref, wa_ref, wi_ref, dx_ref, o_ref):
        dims = (((1,), (1,)), ((), ()))
        o_ref[...] = (dx_ref[...]
                      + lax.dot_general(dzr_ref[...], wa_ref[...].astype(MXU_DTYPE), dims, preferred_element_type=F32)
                      + lax.dot_general(dzi_ref[...], wi_ref[...].astype(MXU_DTYPE), dims, preferred_element_type=F32))

    x_spec = pl.BlockSpec((tm, RNN_GROUP), lambda g, i: (i, g))
    w_spec = pl.BlockSpec((None, RNN_GROUP, RNN_GROUP), lambda g, i: (g, 0, 0))
    return _call(
        body,
        name="lru_gate_xgrad",
        grid=(N_RNN_GROUPS, S // tm),
        in_specs=[x_spec, x_spec, w_spec, w_spec, x_spec],
        out_specs=x_spec,
        out_shape=jax.ShapeDtypeStruct((S, D_RNN), F32),
        semantics=("parallel", "parallel"),
        operands=(dzr, dzi, wa, wi, dx_in),
        side=side,
    )


def _gate_fwd(y_attn, y_rnn, proj, b_gate, side=None):
    t = 512

    def body(ya_ref, yr_ref, ga_ref, gr_ref, ba_ref, br_ref, o_ref):
        o_ref[...] = (_sigmoid(ga_ref[...] + ba_ref[...]) * ya_ref[...]
                      + _sigmoid(gr_ref[...] + br_ref[...]) * yr_ref[...]).astype(o_ref.dtype)

    tile = pl.BlockSpec((t, t), lambda i, j: (i, j))
    return _call(
        body,
        name="gate_fwd",
        grid=(S // t, D // t),
        in_specs=[tile, tile,
                  pl.BlockSpec((t, t), lambda i, j: (i, OFF_GA // t + j)),
                  pl.BlockSpec((t, t), lambda i, j: (i, OFF_GR // t + j)),
                  pl.BlockSpec((1, t), lambda i, j: (0, j)),
                  pl.BlockSpec((1, t), lambda i, j: (0, D // t + j))],
        out_specs=tile,
        out_shape=jax.ShapeDtypeStruct((S, D), MXU_DTYPE),
        semantics=("parallel", "parallel"),
        operands=(y_attn, y_rnn, proj, proj, b_gate, b_gate),
        side=side,
    )


def _gate_bwd(dmix, y_attn, y_rnn, proj, b_gate, side=None, window=None):
    t = 512

    def body(dm_ref, ya_ref, yr_ref, ga_ref, gr_ref, ba_ref, br_ref,
             dga_ref, dya_ref, dyr_ref, dgr_ref, dba_ref, dbr_ref):
        @pl.when(pl.program_id(1) == 0)
        def _():
            dba_ref[...] = jnp.zeros_like(dba_ref)
            dbr_ref[...] = jnp.zeros_like(dbr_ref)

        dm = dm_ref[...]
        ga = _sigmoid(ga_ref[...] + ba_ref[...])
        gr = _sigmoid(gr_ref[...] + br_ref[...])
        dya_ref[...] = (dm * ga).astype(dya_ref.dtype)
        dyr_ref[...] = (dm * gr).astype(dyr_ref.dtype)
        dga = dm * ya_ref[...] * ga * (1.0 - ga)
        dgr = dm * yr_ref[...] * gr * (1.0 - gr)
        dga_ref[...] = dga.astype(dga_ref.dtype)
        dgr_ref[...] = dgr.astype(dgr_ref.dtype)
        dba_ref[...] += _colsum(dga)
        dbr_ref[...] += _colsum(dgr)

    tile = pl.BlockSpec((t, t), lambda j, i: (i, j))
    vec = pl.BlockSpec((1, t), lambda j, i: (0, j))
    return _call(
        body,
        name="gate_bwd",
        grid=(D // t, S // t),
        in_specs=[tile, tile, tile,
                  pl.BlockSpec((t, t), lambda j, i: (i, OFF_GA // t + j)),
                  pl.BlockSpec((t, t), lambda j, i: (i, OFF_GR // t + j)),
                  vec,
                  pl.BlockSpec((1, t), lambda j, i: (0, D // t + j))],
        out_specs=[tile, tile, tile, tile, vec, vec],
        out_shape=[jax.ShapeDtypeStruct((S, D), MXU_DTYPE)] * 4 + [jax.ShapeDtypeStruct((1, D), F32)] * 2,
        semantics=("parallel", "arbitrary"),
        operands=(dmix, y_attn, y_rnn, proj, proj, b_gate, b_gate),
        side=side,
        window=window,
    )


LN_TM = 256


def _ln_stats(pre):
    mu = jnp.mean(pre, axis=-1, keepdims=True)
    xc = pre - mu
    rstd = lax.rsqrt(jnp.mean(xc * xc, axis=-1, keepdims=True) + LN_EPS)
    return xc * rstd, rstd


def _ln_input_grad(dy, xhat, rstd, g):
    dyg = dy * g
    return rstd * (dyg - jnp.mean(dyg, axis=-1, keepdims=True)
                   - xhat * jnp.mean(dyg * xhat, axis=-1, keepdims=True))


def _ln_fwd(res, branch, g, b, side=None):
    def body(res_ref, br_ref, g_ref, b_ref, y_ref, yb_ref, xhat_ref, rstd_ref):
        xhat, rstd = _ln_stats(ALPHA * res_ref[...] + br_ref[...])
        y = xhat * g_ref[...] + b_ref[...]
        y_ref[...] = y
        yb_ref[...] = y.astype(yb_ref.dtype)
        xhat_ref[...] = xhat
        rstd_ref[...] = rstd

    tile = pl.BlockSpec((LN_TM, D), lambda i: (i, 0))
    vec = pl.BlockSpec((1, D), lambda i: (0, 0))
    return _call(
        body,
        name="ln_fwd",
        grid=(S // LN_TM,),
        in_specs=[tile, tile, vec, vec],
        out_specs=[tile, tile, tile, pl.BlockSpec((LN_TM, 1), lambda i: (i, 0))],
        out_shape=[jax.ShapeDtypeStruct((S, D), F32), jax.ShapeDtypeStruct((S, D), MXU_DTYPE),
                   jax.ShapeDtypeStruct((S, D), F32), jax.ShapeDtypeStruct((S, 1), F32)],
        semantics=("parallel",),
        operands=(res, branch, g, b),
        side=side,
    )


def _ln_bwd(dy_a, dy_b, xhat, rstd, g, side=None):
    def body(da_ref, db_in_ref, xhat_ref, rstd_ref, g_ref, dp_ref, dpb_ref, dg_ref, db_ref):
        @pl.when(pl.program_id(0) == 0)
        def _():
            dg_ref[...] = jnp.zeros_like(dg_ref)
            db_ref[...] = jnp.zeros_like(db_ref)

        dy = da_ref[...] + ALPHA * db_in_ref[...]
        xhat = xhat_ref[...]
        dp = _ln_input_grad(dy, xhat, rstd_ref[...], g_ref[...])
        dp_ref[...] = dp
        dpb_ref[...] = dp.astype(dpb_ref.dtype)
        dg_ref[...] += _colsum(dy * xhat)
        db_ref[...] += _colsum(dy)

    tile = pl.BlockSpec((LN_TM, D), lambda i: (i, 0))
    vec = pl.BlockSpec((1, D), lambda i: (0, 0))
    return _call(
        body,
        name="ln_bwd",
        grid=(S // LN_TM,),
        in_specs=[tile, tile, tile, pl.BlockSpec((LN_TM, 1), lambda i: (i, 0)), vec],
        out_specs=[tile, tile, vec, vec],
        out_shape=[jax.ShapeDtypeStruct((S, D), F32), jax.ShapeDtypeStruct((S, D), MXU_DTYPE),
                   jax.ShapeDtypeStruct((1, D), F32), jax.ShapeDtypeStruct((1, D), F32)],
        semantics=("arbitrary",),
        operands=(dy_a, dy_b, xhat, rstd, g),
        side=side,
    )


def _ln_loss_bwd(res, branch, g, b, target, side=None):
    def body(res_ref, br_ref, g_ref, b_ref, t_ref, loss_ref, dp_ref, dpb_ref, dg_ref, db_ref):
        @pl.when(pl.program_id(0) == 0)
        def _():
            loss_ref[...] = jnp.zeros_like(loss_ref)
            dg_ref[...] = jnp.zeros_like(dg_ref)
            db_ref[...] = jnp.zeros_like(db_ref)

        xhat, rstd = _ln_stats(ALPHA * res_ref[...] + br_ref[...])
        gv = g_ref[...]
        err = xhat * gv + b_ref[...] - t_ref[...]
        loss_ref[...] += (0.5 / D) * jnp.sum(_colsum(err * err), axis=1, keepdims=True)
        dy = err * (1.0 / D)
        dp = _ln_input_grad(dy, xhat, rstd, gv)
        dp_ref[...] = dp
        dpb_ref[...] = dp.astype(dpb_ref.dtype)
        dg_ref[...] += _colsum(dy * xhat)
        db_ref[...] += _colsum(dy)

    tile = pl.BlockSpec((LN_TM, D), lambda i: (i, 0))
    vec = pl.BlockSpec((1, D), lambda i: (0, 0))
    return _call(
        body,
        name="ln_loss_bwd",
        grid=(S // LN_TM,),
        in_specs=[tile, tile, vec, vec, tile],
        out_specs=[pl.BlockSpec((1, 1), lambda i: (0, 0)), tile, tile, vec, vec],
        out_shape=[jax.ShapeDtypeStruct((1, 1), F32), jax.ShapeDtypeStruct((S, D), F32),
                   jax.ShapeDtypeStruct((S, D), MXU_DTYPE),
                   jax.ShapeDtypeStruct((1, D), F32), jax.ShapeDtypeStruct((1, D), F32)],
        semantics=("arbitrary",),
        operands=(res, branch, g, b, target),
        side=side,
    )


FFN_TC = 256


def _ffn_act_fwd(up, gpre, w, b, side=None):
    tc = FFN_TC

    def body(up_ref, x_ref, w_ref, b_ref, o_ref, xpad_ref):
        xpad_ref[pl.ds(0, PAD), :] = jnp.zeros((PAD, tc), F32)
        xpad_ref[pl.ds(PAD, S), :] = x_ref[...]
        wv = w_ref[...]
        bv = b_ref[...]

        def step(ci, carry):
            r0 = pl.multiple_of(ci * CHUNK, CHUNK)
            taps = _past_taps(xpad_ref, r0, FFN_CONV_W)
            gate = bv + taps[0] * wv[0:1, :] + taps[1] * wv[1:2, :] + taps[2] * wv[2:3, :]
            o_ref[pl.ds(r0, CHUNK), :] = (_gelu(gate)[0] * up_ref[pl.ds(r0, CHUNK), :]).astype(o_ref.dtype)
            return carry

        lax.fori_loop(0, S // CHUNK, step, 0)

    col = pl.BlockSpec((S, tc), lambda j: (0, j))
    return _call(
        body,
        name="ffn_act_fwd",
        grid=(D_FF // tc,),
        in_specs=[col, col, pl.BlockSpec((FFN_CONV_W, tc), lambda j: (0, j)), pl.BlockSpec((1, tc), lambda j: (0, j))],
        out_specs=col,
        out_shape=jax.ShapeDtypeStruct((S, D_FF), MXU_DTYPE),
        scratch_shapes=[pltpu.VMEM((S + PAD, tc), F32)],
        semantics=("parallel",),
        operands=(up, gpre, w, b),
        side=side,
    )


def _ffn_act_bwd(dfin, up, gpre, w, b, side=None):
    tc = FFN_TC
    width = FFN_CONV_W

    def body(df_ref, up_ref, x_ref, w_ref, b_ref, dup_ref, dx_ref, dw_ref, db_ref, xpad_ref, dpad_ref):
        xpad_ref[pl.ds(0, PAD), :] = jnp.zeros((PAD, tc), F32)
        xpad_ref[pl.ds(PAD, S), :] = x_ref[...]
        dpad_ref[pl.ds(S, PAD), :] = jnp.zeros((PAD, tc), F32)
        wv = w_ref[...]
        bv = b_ref[...]

        def gate_grad(ci, acc):
            r0 = pl.multiple_of(ci * CHUNK, CHUNK)
            taps = _past_taps(xpad_ref, r0, width)
            gate = bv + taps[0] * wv[0:1, :] + taps[1] * wv[1:2, :] + taps[2] * wv[2:3, :]
            ge, dge = _gelu(gate)
            df = df_ref[pl.ds(r0, CHUNK), :]
            dup_ref[pl.ds(r0, CHUNK), :] = (df * ge).astype(dup_ref.dtype)
            d = df * up_ref[pl.ds(r0, CHUNK), :] * dge
            dpad_ref[pl.ds(r0, CHUNK), :] = d
            return tuple(acc[k] + _colsum(taps[k] * d) for k in range(width)) + (acc[width] + _colsum(d),)

        zero = jnp.zeros((1, tc), F32)
        acc = lax.fori_loop(0, S // CHUNK, gate_grad, (zero,) * (width + 1))
        for k in range(width):
            dw_ref[k:k + 1, :] = acc[k]
        db_ref[...] = acc[width]

        def input_grad(ci, carry):
            r0 = pl.multiple_of(ci * CHUNK, CHUNK)
            ahead = _future_taps(dpad_ref, r0, width)
            dx = ahead[0] * wv[2:3, :] + ahead[1] * wv[1:2, :] + ahead[2] * wv[0:1, :]
            dx_ref[pl.ds(r0, CHUNK), :] = dx.astype(dx_ref.dtype)
            return carry

        lax.fori_loop(0, S // CHUNK, input_grad, 0)

    col = pl.BlockSpec((S, tc), lambda j: (0, j))
    w_spec = pl.BlockSpec((width, tc), lambda j: (0, j))
    vec = pl.BlockSpec((1, tc), lambda j: (0, j))
    return _call(
        body,
        name="ffn_act_bwd",
        grid=(D_FF // tc,),
        in_specs=[col, col, col, w_spec, vec],
        out_specs=[col, col, w_spec, vec],
        out_shape=[jax.ShapeDtypeStruct((S, D_FF), MXU_DTYPE)] * 2
        + [jax.ShapeDtypeStruct((width, D_FF), F32), jax.ShapeDtypeStruct((1, D_FF), F32)],
        scratch_shapes=[pltpu.VMEM((S + PAD, tc), F32), pltpu.VMEM((S + PAD, tc), F32)],
        semantics=("parallel",),
        operands=(dfin, up, gpre, w, b),
        side=side,
    )


def _adamw_update(w, g, m, v):
    m = ADAM_B1 * m + (1.0 - ADAM_B1) * g
    v = ADAM_B2 * v + (1.0 - ADAM_B2) * (g * g)
    m_hat = m / (1.0 - ADAM_B1 ** ADAM_STEP)
    v_hat = v / (1.0 - ADAM_B2 ** ADAM_STEP)
    delta = -ADAM_LR * (m_hat / (jnp.sqrt(v_hat) + ADAM_EPS) + ADAM_WD * w)
    return delta, m, v


def _add_pairs(send, pair, far_index, *, name):
    _, r_dim, c_dim = send.shape
    tr = r_dim // 4

    def body(far_ref, mine_ref, theirs_ref, o_ref):
        o_ref[...] = (mine_ref[...].astype(F32) + theirs_ref[...].astype(F32)).astype(o_ref.dtype)

    return pl.pallas_call(
        body,
        name=name,
        grid_spec=pltpu.PrefetchScalarGridSpec(
            num_scalar_prefetch=1,
            grid=(3, r_dim // tr),
            in_specs=[pl.BlockSpec((None, tr, c_dim), lambda j, i, far: (far[j], i, 0)),
                      pl.BlockSpec((None, tr, c_dim), lambda j, i, far: (1 + j, i, 0))],
            out_specs=pl.BlockSpec((None, tr, c_dim), lambda j, i, far: (j, i, 0)),
        ),
        out_shape=jax.ShapeDtypeStruct((3, r_dim, c_dim), BF16),
        compiler_params=_cparams("parallel", "parallel"),
    )(far_index, send, pair)


def _reduce_adamw(w, m, v, g_own, pair, far, me, *, tr, name, part=0, earlier=None):
    _, r_dim, c_dim = w.shape
    cp = pair.shape[2]

    def body(me_ref, w_ref, m_ref, v_ref, g_ref, pair_ref, far_ref, *refs):
        grad_ref, delta_ref, nm_ref, nv_ref = refs[-4:]
        g = g_ref[...] + pair_ref[...].astype(F32)
        for j in range(3):
            g = g + far_ref[j].astype(F32)
        delta, nm, nv = _adamw_update(w_ref[...], g, m_ref[...], v_ref[...])
        grad_ref[...] = g
        delta_ref[...] = delta
        nm_ref[...] = nm
        nv_ref[...] = nv

    tile = pl.BlockSpec((None, tr, cp), lambda i, me: (0, i, part))
    if g_own.ndim == 3:
        own_spec = pl.BlockSpec((None, tr, cp), lambda i, me: (me[0], i, 0))
    else:
        own_spec = pl.BlockSpec((tr, cp), lambda i, me: (i, 0))
    earlier = list(earlier or ())
    return pl.pallas_call(
        body,
        name=name,
        grid_spec=pltpu.PrefetchScalarGridSpec(
            num_scalar_prefetch=1,
            grid=(r_dim // tr,),
            in_specs=[tile, tile, tile, own_spec, pl.BlockSpec((None, tr, cp), lambda i, me: (0, i, 0)),
                      pl.BlockSpec((3, tr, cp), lambda i, me: (0, i, 0))]
            + [pl.BlockSpec(memory_space=pl.ANY)] * len(earlier),
            out_specs=[tile] * 4,
        ),
        out_shape=[jax.ShapeDtypeStruct((1, r_dim, c_dim), F32)] * 4,
        input_output_aliases={7 + k: k for k in range(len(earlier))},
        compiler_params=_cparams("parallel"),
    )(me, w, m, v, g_own, pair, far, *earlier)


def _adamw_many(ws, ms, vs, gs):
    n = len(ws)

    def body(*refs):
        for i in range(n):
            delta, nm, nv = _adamw_update(refs[i][...], refs[3 * n + i][...], refs[n + i][...], refs[2 * n + i][...])
            refs[4 * n + i][...] = delta
            refs[5 * n + i][...] = nm
            refs[6 * n + i][...] = nv

    vmem = pl.BlockSpec(memory_space=pltpu.VMEM)
    res = pl.pallas_call(
        body,
        name="adamw_small",
        in_specs=[vmem] * (4 * n),
        out_specs=[vmem] * (3 * n),
        out_shape=[jax.ShapeDtypeStruct(w.shape, F32) for w in ws] * 3,
        compiler_params=pltpu.CompilerParams(vmem_limit_bytes=VMEM_LIMIT),
    )(*ws, *ms, *vs, *gs)
    return res[:n], res[n:2 * n], res[2 * n:]


def _adamw_blocks(w, m, v, g, *, name, side=None):
    per = 2

    def body(w_ref, m_ref, v_ref, g_ref, delta_ref, nm_ref, nv_ref):
        delta, nm, nv = _adamw_update(w_ref[...], g_ref[...], m_ref[...], v_ref[...])
        delta_ref[...] = delta
        nm_ref[...] = nm
        nv_ref[...] = nv

    tile = pl.BlockSpec((1, per) + w.shape[2:], lambda i: (0, i, 0, 0))
    return _call(
        body,
        name=name,
        grid=(w.shape[1] // per,),
        in_specs=[tile] * 4,
        out_specs=[tile] * 3,
        out_shape=[jax.ShapeDtypeStruct(w.shape, F32)] * 3,
        semantics=("parallel",),
        operands=(w, m, v, g),
        side=side,
    )


def _coords():
    return lax.axis_index("x"), lax.axis_index("y"), lax.axis_index("c")


def _flip(coord, bit):
    return 1 - coord if bit else coord


def _relative(k):
    x, y, c = _coords()
    return _flip(x, k & 4), _flip(y, k & 2), _flip(c, k & 1)


def _index(pos):
    return 4 * pos[0] + 2 * pos[1] + pos[2]


FAR = (4, 2, 6)
AG_US_PER_MB = 38.0
RS_US_PER_MB = 46.0
MIN_RIDE_US = 30.0
PAIR_EXCHANGE_US = 20.0
MIN_GATHER_RIDE_US = 22.0
ROW_ALIGN = 32


def _chunks(items, cursor, us, us_per_mb, through=None):
    budget = float("inf") if us is None else us / us_per_mb * 2 ** 20
    names = list(items)
    if through is not None:
        names = names[:names.index(through) + 1]
    chunks = []
    for name in names:
        arr = items[name]
        r_dim, c_dim = arr.shape[-2:]
        row_bytes = c_dim * arr.dtype.itemsize
        while cursor[name] < r_dim and budget > 0:
            rows = r_dim - cursor[name]
            if r_dim > ROW_ALIGN and budget < rows * row_bytes:
                rows = min(rows, max(ROW_ALIGN, int(budget // row_bytes) // ROW_ALIGN * ROW_ALIGN))
            chunks.append((name, cursor[name], rows))
            cursor[name] += rows
            budget -= rows * row_bytes
    return chunks


class _Gather:
    def __init__(self, shards):
        self.shards, self.bufs, self.cursor = {}, {}, {}
        self.add_shards(shards)

    def add_shards(self, shards):
        for n, shard in shards.items():
            self.shards[n], self.bufs[n], self.cursor[n] = shard, None, 0

    def take(self, us=None, through=None):
        if us is not None and us < MIN_GATHER_RIDE_US:
            return None
        chunks = _chunks(self.shards, self.cursor, us, AG_US_PER_MB, through)
        return _GatherSide(self, chunks) if chunks else None

    def get(self, name):
        chunks = _chunks(self.shards, self.cursor, None, AG_US_PER_MB, through=name)
        if chunks:
            _run_side(_GatherSide(self, chunks), "gather_" + name)
        return self.bufs[name]


class _GatherSide:
    SEMS = 8

    def __init__(self, owner, chunks):
        self.owner, self.chunks = owner, chunks
        self.names = list(dict.fromkeys(n for n, _, _ in chunks))
        old = [n for n in self.names if owner.bufs[n] is not None]
        self.operands = [owner.shards[n] for n in self.names] + [owner.bufs[n] for n in old]
        self.out_shape = [jax.ShapeDtypeStruct((N_DEV,) + owner.shards[n].shape, owner.shards[n].dtype)
                          for n in self.names]
        self.aliases = {len(self.names) + i: self.names.index(n) for i, n in enumerate(old)}
        self.sems = [pltpu.SemaphoreType.DMA((self.SEMS * len(chunks),)),
                     pltpu.SemaphoreType.DMA((self.SEMS * len(chunks),)), pltpu.SemaphoreType.DMA((len(chunks),))]

    def _halves(self, ci):
        _, r0, rows = self.chunks[ci]
        if rows % ROW_ALIGN:
            return None
        return (r0, rows // 2), (r0 + rows // 2, rows // 2)

    def _copy(self, ins, outs, sems, ci, s, block, to, rows=None, from_shard=False):
        name, r0, n = self.chunks[ci]
        if rows is not None:
            r0, n = rows
        w = self.names.index(name)
        slot = outs[w].at[_index(block), pl.ds(r0, n)]
        return pltpu.make_async_remote_copy(
            src_ref=ins[w].at[pl.ds(r0, n)] if from_shard else slot, dst_ref=slot,
            send_sem=sems[0].at[self.SEMS * ci + s], recv_sem=sems[1].at[self.SEMS * ci + s],
            device_id=to, device_id_type=MESH)

    def _own(self, ins, outs, sems, ci):
        name, r0, rows = self.chunks[ci]
        w = self.names.index(name)
        return pltpu.make_async_copy(ins[w].at[pl.ds(r0, rows)], outs[w].at[_index(_relative(0)), pl.ds(r0, rows)],
                                     sems[2].at[ci])

    def _pass(self, ins, outs, sems, ci, which):
        source, target = ((4, 2), (2, 4))[which]
        return self._copy(ins, outs, sems, ci, 3 + which, _relative(source), _relative(target),
                          rows=self._halves(ci)[which])

    def start(self, ins, outs, sems):
        me = _relative(0)
        for ci in range(len(self.chunks)):
            self._own(ins, outs, sems, ci).start()
        for ci in range(len(self.chunks)):
            self._copy(ins, outs, sems, ci, 1, me, _relative(4), from_shard=True).start()
            self._copy(ins, outs, sems, ci, 2, me, _relative(2), from_shard=True).start()
            if self._halves(ci) is None:
                self._copy(ins, outs, sems, ci, 3, me, _relative(6), from_shard=True).start()
        for ci in range(len(self.chunks)):
            self._copy(ins, outs, sems, ci, 0, me, _relative(1), from_shard=True).start()

    def mid(self, ins, outs, sems):
        me = _relative(0)
        cut = [ci for ci in range(len(self.chunks)) if self._halves(ci) is not None]
        for ci in cut:
            self._copy(ins, outs, sems, ci, 1, _relative(4), me).wait_recv()
            self._pass(ins, outs, sems, ci, 0).start()
            self._copy(ins, outs, sems, ci, 5, _relative(4), _relative(1)).start()
        for ci in cut:
            self._copy(ins, outs, sems, ci, 2, _relative(2), me).wait_recv()
            self._pass(ins, outs, sems, ci, 1).start()
            self._copy(ins, outs, sems, ci, 6, _relative(2), _relative(1)).start()

    def finish(self, ins, outs, sems):
        me, sibling = _relative(0), _relative(1)
        n = len(self.chunks)
        for ci in range(n):
            if self._halves(ci) is None:
                for s, k in ((1, 4), (2, 2), (3, 6)):
                    self._copy(ins, outs, sems, ci, s, _relative(k), me).wait_recv()
                for j, k in enumerate(FAR):
                    self._copy(ins, outs, sems, ci, 5 + j, _relative(k), sibling).start()
            else:
                h0, h1 = self._halves(ci)
                self._copy(ins, outs, sems, ci, 3, _relative(6), me, rows=h0).wait_recv()
                self._copy(ins, outs, sems, ci, 4, _relative(6), me, rows=h1).wait_recv()
                self._copy(ins, outs, sems, ci, 7, _relative(6), sibling).start()
        for ci in range(n):
            self._copy(ins, outs, sems, ci, 0, sibling, me).wait_recv()
            for j, k in enumerate(FAR):
                self._copy(ins, outs, sems, ci, 5 + j, _relative(k | 1), me).wait_recv()
        for ci in range(n):
            self._copy(ins, outs, sems, ci, 0, me, sibling, from_shard=True).wait_send()
            self._copy(ins, outs, sems, ci, 1, me, _relative(4), from_shard=True).wait_send()
            self._copy(ins, outs, sems, ci, 2, me, _relative(2), from_shard=True).wait_send()
            if self._halves(ci) is None:
                self._copy(ins, outs, sems, ci, 3, me, _relative(6), from_shard=True).wait_send()
            else:
                self._pass(ins, outs, sems, ci, 0).wait_send()
                self._pass(ins, outs, sems, ci, 1).wait_send()
            for j, k in enumerate(FAR):
                self._copy(ins, outs, sems, ci, 5 + j, _relative(k), sibling).wait_send()
            self._own(ins, outs, sems, ci).wait()

    def done(self, results):
        for n, buf in zip(self.names, results):
            self.owner.bufs[n] = buf


class _Scatter:
    def __init__(self, me, far_index):
        self.me, self.far_index = me, far_index
        self.sends, self.owns, self.pairs, self.sums, self.fars = {}, {}, {}, {}, {}
        self.pair_cursor, self.far_cursor = {}, {}

    def add(self, name, send, own):
        self.sends[name] = send
        self.owns[name] = own
        self.pairs[name] = self.fars[name] = None
        self.pair_cursor[name] = 0

    def _rows(self, name):
        return self.sends[name].shape[1]

    def _add_ready_pairs(self):
        for name in self.sends:
            if name not in self.sums and self.pair_cursor[name] == self._rows(name):
                self.sums[name] = _add_pairs(self.sends[name], self.pairs[name], self.far_index, name="pair_" + name)
                self.far_cursor[name] = 0

    def _side(self, us, through=None):
        self._add_ready_pairs()
        names = list(self.sends)
        if through is not None:
            names = names[:names.index(through) + 1]
        pair_chunks = [(n, self.pair_cursor[n], self._rows(n) - self.pair_cursor[n]) for n in names
                       if self.pair_cursor[n] < self._rows(n)]
        for n, _, _ in pair_chunks:
            self.pair_cursor[n] = self._rows(n)
        far_chunks = _chunks(self.sums, self.far_cursor, us, RS_US_PER_MB,
                             through if through in self.sums else None) if self.sums else []
        return _ScatterSide(self, pair_chunks, far_chunks) if pair_chunks or far_chunks else None

    def add_blocks(self, name, blocks32, blocks16):
        self.add(name, blocks16, blocks32)

    def take(self, us):
        return self._side(us) if us >= MIN_RIDE_US else None

    def flush_pairs(self, name):
        side = self._side(PAIR_EXCHANGE_US)
        if side is not None:
            _run_side(side, name)
        self._add_ready_pairs()

    def get(self, name):
        step = 0
        while name not in self.sums or self.far_cursor[name] < self._rows(name):
            _run_side(self._side(None, through=name), "scatter_%s_%d" % (name, step))
            step += 1
        return self.owns[name], self.pairs[name], self.fars[name]


class _ScatterSide:
    TO_SIBLING = (1, 5, 3, 7)

    def __init__(self, owner, pair_chunks, far_chunks):
        self.owner, self.pair_chunks, self.far_chunks = owner, pair_chunks, far_chunks
        self.pair_names = list(dict.fromkeys(n for n, _, _ in pair_chunks))
        self.far_names = list(dict.fromkeys(n for n, _, _ in far_chunks))
        ins = [(owner.sends[n], owner.pairs[n], (4,)) for n in self.pair_names]
        ins += [(owner.sums[n], owner.fars[n], (3,)) for n in self.far_names]
        old = [i for i, (_, buf, _) in enumerate(ins) if buf is not None]
        self.operands = [src for src, _, _ in ins] + [ins[i][1] for i in old]
        self.out_shape = [jax.ShapeDtypeStruct(slots + src.shape[1:], BF16) for src, _, slots in ins]
        self.aliases = {len(ins) + j: i for j, i in enumerate(old)}
        n_pair, n_far = 4 * len(pair_chunks), 3 * len(far_chunks)
        self.sems = [pltpu.SemaphoreType.DMA((max(n_pair, 1),)), pltpu.SemaphoreType.DMA((max(n_pair, 1),)),
                     pltpu.SemaphoreType.DMA((max(n_far, 1),)), pltpu.SemaphoreType.DMA((max(n_far, 1),))]

    def _copies(self, ins, outs, sems):
        copies = []
        for ci, (name, r0, rows) in enumerate(self.pair_chunks):
            w = self.pair_names.index(name)
            for j, k in enumerate(self.TO_SIBLING):
                copies.append(pltpu.make_async_remote_copy(
                    src_ref=ins[w].at[_index(_relative(k)), pl.ds(r0, rows)], dst_ref=outs[w].at[j, pl.ds(r0, rows)],
                    send_sem=sems[0].at[4 * ci + j], recv_sem=sems[1].at[4 * ci + j],
                    device_id=_relative(1), device_id_type=MESH))
        for ci, (name, r0, rows) in enumerate(self.far_chunks):
            w = len(self.pair_names) + self.far_names.index(name)
            for j, k in enumerate(FAR):
                copies.append(pltpu.make_async_remote_copy(
                    src_ref=ins[w].at[j, pl.ds(r0, rows)], dst_ref=outs[w].at[j, pl.ds(r0, rows)],
                    send_sem=sems[2].at[3 * ci + j], recv_sem=sems[3].at[3 * ci + j],
                    device_id=_relative(k), device_id_type=MESH))
        return copies

    def start(self, ins, outs, sems):
        for cp in self._copies(ins, outs, sems):
            cp.start()

    def mid(self, ins, outs, sems):
        pass

    def finish(self, ins, outs, sems):
        for cp in self._copies(ins, outs, sems):
            cp.wait()

    def done(self, results):
        for n, buf in zip(self.pair_names, results):
            self.owner.pairs[n] = buf
        for n, buf in zip(self.far_names, results[len(self.pair_names):]):
            self.owner.fars[n] = buf


class _Joined:
    def __init__(self, sides):
        self.sides = sides
        self.operands, self.out_shape, self.sems, self.aliases, self.spans = [], [], [], {}, []
        for s in sides:
            i0, o0, s0 = len(self.operands), len(self.out_shape), len(self.sems)
            self.operands += list(s.operands)
            self.out_shape += list(s.out_shape)
            self.sems += list(s.sems)
            self.aliases.update({i0 + i: o0 + o for i, o in s.aliases.items()})
            self.spans.append((slice(i0, len(self.operands)), slice(o0, len(self.out_shape)),
                               slice(s0, len(self.sems))))

    def start(self, ins, outs, sems):
        for s, (i, o, m) in zip(self.sides, self.spans):
            s.start(ins[i], outs[o], sems[m])

    def mid(self, ins, outs, sems):
        for s, (i, o, m) in zip(self.sides, self.spans):
            s.mid(ins[i], outs[o], sems[m])

    def finish(self, ins, outs, sems):
        for s, (i, o, m) in zip(self.sides, self.spans):
            s.finish(ins[i], outs[o], sems[m])

    def done(self, results):
        for s, (_, o, _) in zip(self.sides, self.spans):
            s.done(results[o])


def _join(*sides):
    sides = [s for s in sides if s is not None]
    if len(sides) <= 1:
        return sides[0] if sides else None
    return _Joined(sides)


PART_W = 768


def _pack_rows(vecs):
    rows = -(-sum(v.shape[0] for v in vecs) // 8) * 8

    def body(*refs):
        out = refs[-1]
        out[...] = jnp.zeros_like(out)
        r0 = 0
        for v in refs[:-1]:
            k, n = v.shape
            for p in range(-(-n // PART_W)):
                w = min(PART_W, n - PART_W * p)
                out[p, r0:r0 + k, 0:w] = v[:, PART_W * p:PART_W * p + w]
            r0 += k

    vmem = pl.BlockSpec(memory_space=pltpu.VMEM)
    return pl.pallas_call(body, name="pack_small", in_specs=[vmem] * len(vecs), out_specs=vmem,
                          out_shape=jax.ShapeDtypeStruct((N_DEV, rows, PART_W), F32))(*vecs)


def _unpack_rows(packed, shapes):
    def body(packed_ref, *outs):
        r0 = 0
        for o in outs:
            k, n = o.shape
            for p in range(-(-n // PART_W)):
                w = min(PART_W, n - PART_W * p)
                o[:, PART_W * p:PART_W * p + w] = packed_ref[p, r0:r0 + k, 0:w]
            r0 += k

    vmem = pl.BlockSpec(memory_space=pltpu.VMEM)
    return pl.pallas_call(body, name="unpack_small", in_specs=[vmem], out_specs=[vmem] * len(shapes),
                          out_shape=[jax.ShapeDtypeStruct(s, F32) for s in shapes])(packed)


class _PartsToOwners:
    def __init__(self, arrays):
        self.n = len(arrays)
        self.pers = [a.shape[0] // N_DEV for a in arrays]
        self.operands, self.aliases = list(arrays), {}
        self.out_shape = [jax.ShapeDtypeStruct((N_DEV, per) + a.shape[1:], a.dtype) for a, per in zip(arrays, self.pers)]
        self.sems = [pltpu.SemaphoreType.DMA((self.n * (N_DEV - 1),))] * 2

    def _copies(self, ins, outs, sems):
        return [pltpu.make_async_remote_copy(
            src_ref=ins[j].at[pl.ds(self.pers[j] * _index(_relative(k)), self.pers[j])], dst_ref=outs[j].at[k],
            send_sem=sems[0].at[self.n * (k - 1) + j], recv_sem=sems[1].at[self.n * (k - 1) + j],
            device_id=_relative(k), device_id_type=MESH) for k in range(1, N_DEV) for j in range(self.n)]

    def start(self, ins, outs, sems):
        for cp in self._copies(ins, outs, sems):
            cp.start()

    def mid(self, ins, outs, sems):
        pass

    def finish(self, ins, outs, sems):
        for cp in self._copies(ins, outs, sems):
            cp.wait()

    def done(self, results):
        self.stages = list(results)


def _sum_parts(arrays, stages):
    n = len(arrays)
    pers = [a.shape[0] // N_DEV for a in arrays]

    def body(*refs):
        me = _index(_relative(0))
        for j in range(n):
            acc = refs[j][pl.ds(pers[j] * me, pers[j])]
            for k in range(1, N_DEV):
                acc = acc + refs[n + j][k].astype(F32)
            refs[2 * n + j][...] = acc

    vmem = pl.BlockSpec(memory_space=pltpu.VMEM)
    return pl.pallas_call(body, name="sum_small_parts", in_specs=[vmem] * (2 * n), out_specs=[vmem] * n,
                          out_shape=[jax.ShapeDtypeStruct((per,) + a.shape[1:], F32) for a, per in zip(arrays, pers)],
                          compiler_params=pltpu.CompilerParams(vmem_limit_bytes=VMEM_LIMIT))(*arrays, *stages)


class _PartsToAll:
    def __init__(self, parts):
        self.n = len(parts)
        self.pers = [p.shape[0] for p in parts]
        self.operands, self.aliases = list(parts), {}
        self.out_shape = [jax.ShapeDtypeStruct((N_DEV * p.shape[0],) + p.shape[1:], F32) for p in parts]
        self.sems = [pltpu.SemaphoreType.DMA((self.n * (N_DEV - 1),))] * 2 + [pltpu.SemaphoreType.DMA((self.n,))]

    def _rows(self, outs, j, pos):
        return outs[j].at[pl.ds(self.pers[j] * _index(pos), self.pers[j])]

    def _copy(self, ins, outs, sems, k, j, owner):
        return pltpu.make_async_remote_copy(
            src_ref=ins[j], dst_ref=self._rows(outs, j, owner),
            send_sem=sems[0].at[self.n * (k - 1) + j], recv_sem=sems[1].at[self.n * (k - 1) + j],
            device_id=_relative(k), device_id_type=MESH)

    def _own(self, ins, outs, sems, j):
        return pltpu.make_async_copy(ins[j], self._rows(outs, j, _relative(0)), sems[2].at[j])

    def start(self, ins, outs, sems):
        for j in range(self.n):
            self._own(ins, outs, sems, j).start()
            for k in range(1, N_DEV):
                self._copy(ins, outs, sems, k, j, _relative(0)).start()

    def mid(self, ins, outs, sems):
        pass

    def finish(self, ins, outs, sems):
        for j in range(self.n):
            for k in range(1, N_DEV):
                self._copy(ins, outs, sems, k, j, _relative(k)).wait_recv()
                self._copy(ins, outs, sems, k, j, _relative(0)).wait_send()
            self._own(ins, outs, sems, j).wait()

    def done(self, results):
        self.totals = list(results)


class _SmallSync:
    def __init__(self, vec_names, mat_names):
        self.vec_names, self.mat_names = vec_names, mat_names

    def begin(self, loss, grads):
        vecs = [loss] + [grads[n] for n in self.vec_names]
        self.shapes = [v.shape for v in vecs]
        self.own = [_diag_blocks(grads[n]) for n in self.mat_names] + [_pack_rows(vecs)]
        self.to_owners = _PartsToOwners([a.astype(BF16) for a in self.own[:-1]] + self.own[-1:])
        return self.to_owners

    def middle(self):
        self.to_all = _PartsToAll(_sum_parts(self.own, self.to_owners.stages))
        return self.to_all

    def end(self):
        *mats, packed = self.to_all.totals
        sums = _unpack_rows(packed, self.shapes)
        return sums[0], dict(zip(self.vec_names, sums[1:])), dict(zip(self.mat_names, mats))


def _block_diag(w):
    groups = []
    for g in range(N_RNN_GROUPS):
        placed = [jnp.pad(w[4 * g + b], ((RNN_BLOCK_W * b, RNN_BLOCK_W * (3 - b)),) * 2) for b in range(4)]
        groups.append(placed[0] + placed[1] + placed[2] + placed[3])
    return jnp.stack(groups)


def _diag_blocks(wg):
    blocks = []
    for n in range(4 * N_RNN_GROUPS):
        g, at = n // 4, RNN_BLOCK_W * (n % 4)
        blocks.append(wg[g, at:at + RNN_BLOCK_W, at:at + RNN_BLOCK_W])
    return jnp.stack(blocks)


def _heads_major(t, n_heads):
    return t.reshape(S, n_heads, HEAD_DIM).transpose(1, 0, 2)


def _heads_minor(t):
    return t.transpose(1, 0, 2).reshape(S, t.shape[0] * HEAD_DIM)


def _natural(gathered, how):
    n, r, c = gathered.shape
    if how == "rows":
        return gathered.reshape(n * r, c)
    return gathered.transpose(1, 0, 2).reshape(r, n * c)


def _blocks(full, how):
    if how == "rows":
        return full.reshape(N_DEV, full.shape[0] // N_DEV, full.shape[1])
    return full.reshape(full.shape[0], N_DEV, full.shape[1] // N_DEV).transpose(1, 0, 2)


def _cast_many(arrays, side=None):
    steps = 4

    def body(*refs):
        n = len(refs) // 2
        for src, dst in zip(refs[:n], refs[n:]):
            dst[...] = src[...].astype(dst.dtype)

    specs = [pl.BlockSpec((a.shape[0] // steps, a.shape[1]), lambda i: (i, 0)) for a in arrays]
    return _call(
        body,
        name="cast_weights",
        grid=(steps,),
        in_specs=specs,
        out_specs=specs,
        out_shape=[jax.ShapeDtypeStruct(a.shape, MXU_DTYPE) for a in arrays],
        semantics=("parallel",),
        operands=tuple(arrays),
        side=side,
    )


def _forward_backward(x2, xb, target, small, gather, scatter, sync):
    w_in_t = _natural(gather.get("w_in"), "rows")
    proj, projb = _mm(xb, w_in_t, tb=True, tm=S, tn=512, tk=D, out_dtype=(F32, MXU_DTYPE), name="proj",
                      side=gather.take(110))

    qt = projb[:, :OFF_K].T.reshape(N_KV, GROUP, HEAD_DIM, S)
    k2, v2 = projb[:, OFF_K:OFF_V], projb[:, OFF_V:OFF_RX]
    kp = jnp.pad(_heads_major(k2, N_KV), ((0, 0), (BLOCK, 0), (0, 0)))
    vp = jnp.pad(_heads_major(v2, N_KV), ((0, 0), (BLOCK, 0), (0, 0)))
    kt = jnp.pad(k2.T.reshape(N_KV, HEAD_DIM, S), ((0, 0), (0, 0), (BLOCK, 0)))
    vt = jnp.pad(v2.T.reshape(N_KV, HEAD_DIM, S), ((0, 0), (0, 0), (BLOCK, 0)))
    sink_row = jnp.repeat(small["attn_sinks"].reshape(N_KV, 1, GROUP), BLOCK, axis=2)
    ot = _attn_fwd(qt, kp, vt, sink_row, side=gather.take(36)).reshape(D, S)

    rconv_w = _natural(gather.get("rnn_conv_w"), "cols")
    rxc = _conv_fwd(proj, OFF_RX, rconv_w, small["rnn_conv_b"], tc=512, name="rnn_conv_fwd", side=gather.take(18))
    r, i = _lru_gates(rxc, small["lru_wa"], small["lru_wi"], small["lru_ba"], small["lru_bi"], side=gather.take(33))
    h, yrin = _lru_scan_fwd(r, i, rxc, proj, small["lru_lambda"], side=gather.take(53))

    w_ap = _natural(gather.get("w_attn_proj"), "rows")
    w_rp = _natural(gather.get("w_rnn_proj"), "rows")
    y_attn = _mm(ot, w_ap, ta=True, tm=1024, tn=1024, tk=D, name="attn_proj", side=gather.take(22))
    y_rnn = _mm(yrin, w_rp, tm=1024, tn=1024, tk=D_RNN, name="rnn_proj", side=gather.take(27))
    mixin = _gate_fwd(y_attn, y_rnn, proj, small["b_gate"], side=gather.take(25))
    w_out = _natural(gather.get("w_out"), "rows")
    mix = _mm(mixin, w_out, tm=1024, tn=1024, tk=D, name="mix_out", side=gather.take(22))
    x1, x1b, xhat1, rstd1 = _ln_fwd(x2, mix, small["ln1_g"], small["ln1_b"], side=gather.take(23))

    w_up = gather.get("ffn_w_up")
    up = _mm(x1b, w_up, tm=1024, tn=768, tk=D, b_block=768, name="ffn_up", side=gather.take(58))
    w_gate = gather.get("ffn_w_gate")
    gpre = _mm(x1b, w_gate, tm=1024, tn=768, tk=D, b_block=768, name="ffn_gate", side=gather.take(58))
    fconv_w = _natural(gather.get("ffn_conv_w"), "cols")
    fin = _ffn_act_fwd(up, gpre, fconv_w, small["ffn_conv_b"], side=gather.take())
    w_down = _natural(gather.get("ffn_w_down"), "rows")
    f = _mm(fin, w_down, tm=1024, tn=1024, tk=2048, name="ffn_down")
    loss, dpre2, dpre2b, d_ln2_g, d_ln2_b = _ln_loss_bwd(x1, f, small["ln2_g"], small["ln2_b"], target)

    grads = {"ln2_g": d_ln2_g, "ln2_b": d_ln2_b}
    both = (F32, BF16)
    g32, g16 = _mm(fin, dpre2b, ta=True, tm=1024, tn=1024, tk=S, out_dtype=both, name="d_ffn_w_down")
    scatter.add_blocks("ffn_w_down", _blocks(g32, "rows"), _blocks(g16, "rows"))
    dfin = _mm(dpre2b, w_down, tb=True, tm=1024, tn=1024, tk=D, name="d_fin", side=scatter.take(57))
    dup, dgpre, grads["ffn_conv_w"], grads["ffn_conv_b"] = _ffn_act_bwd(
        dfin, up, gpre, fconv_w, small["ffn_conv_b"], side=scatter.take(85))
    g32, g16 = _mm(x1b, dup, ta=True, tm=1024, tn=768, tk=S, out_dtype=both, out_block=768, name="d_ffn_w_up",
                   side=scatter.take(57))
    scatter.add_blocks("ffn_w_up", g32, g16)
    g32, g16 = _mm(x1b, dgpre, ta=True, tm=1024, tn=768, tk=S, out_dtype=both, out_block=768, name="d_ffn_w_gate",
                   side=scatter.take(56))
    scatter.add_blocks("ffn_w_gate", g32, g16)
    dx1 = _mm(dup, w_up, tb=True, tm=1024, tn=1024, tk=768, b_block=768, name="d_x1_up", side=scatter.take(68))
    dx1 = _mm(dgpre, w_gate, tb=True, tm=1024, tn=1024, tk=768, b_block=768, add=dx1, name="d_x1_gate",
              side=scatter.take(70))
    dpre1, dpre1b, grads["ln1_g"], grads["ln1_b"] = _ln_bwd(dx1, dpre2, xhat1, rstd1, small["ln1_g"],
                                                            side=scatter.take(24))

    g32, g16 = _mm(mixin, dpre1b, ta=True, tm=1024, tn=1024, tk=S, out_dtype=both, name="d_w_out",
                   side=scatter.take(26))
    scatter.add_blocks("w_out", _blocks(g32, "rows"), _blocks(g16, "rows"))
    dmix = _mm(dpre1b, w_out, tb=True, tm=1024, tn=1024, tk=D, name="d_mixin", side=scatter.take(22))
    dproj, dya, dyr, dgl_r, db_a, db_r = _gate_bwd(
        dmix, y_attn, y_rnn, proj, small["b_gate"], side=scatter.take(36),
        window=(jax.ShapeDtypeStruct((S, D_IN), MXU_DTYPE), OFF_GA))
    grads["b_gate"] = jnp.concatenate([db_a, db_r], axis=1)
    g32, g16 = _mm(ot, dya, tm=1024, tn=1024, tk=S, out_dtype=both, name="d_w_attn_proj", side=scatter.take(38))
    scatter.add_blocks("w_attn_proj", _blocks(g32, "rows"), _blocks(g16, "rows"))
    g32, g16 = _mm(yrin, dyr, ta=True, tm=1280, tn=1024, tk=S, out_dtype=both, name="d_w_rnn_proj",
                   side=scatter.take(27))
    scatter.add_blocks("w_rnn_proj", _blocks(g32, "rows"), _blocks(g16, "rows"))
    dot_ = _mm(w_ap, dya, tb=True, tm=1024, tn=1024, tk=D, out_dtype=MXU_DTYPE, name="d_o", side=scatter.take(22))
    dyrin = _mm(dyr, w_rp, tb=True, tm=1024, tn=1280, tk=D, name="d_yrin", side=scatter.take(27))

    dproj, dzr, dzi, drxc_in, grads["lru_ba"], grads["lru_bi"], grads["lru_lambda"] = _lru_scan_bwd(
        dyrin, proj, h, r, i, rxc, small["lru_lambda"], side=scatter.take(94), window=(dproj, OFF_RY))
    grads["lru_wa"], grads["lru_wi"] = _lru_gate_wgrad(rxc, dzr, dzi, side=scatter.take(22))
    drxc = _lru_gate_xgrad(dzr, dzi, small["lru_wa"], small["lru_wi"], drxc_in, side=scatter.take(33))
    dproj, grads["rnn_conv_w"], grads["rnn_conv_b"] = _conv_bwd(
        drxc, proj, OFF_RX, rconv_w, tc=512, name="rnn_conv_bwd", side=scatter.take(29), window=(dproj, OFF_RX))

    dqt, dk, dv, dsink = _attn_bwd(qt, kp, kt, vp, sink_row, dot_.reshape(N_KV, GROUP, HEAD_DIM, S),
                                   side=scatter.take(65))
    grads["attn_sinks"] = dsink.reshape(1, N_KV * GROUP)
    for col0, piece in ((0, dqt.reshape(D, S).T), (OFF_K, _heads_minor(dk[:, BLOCK:, :]).astype(MXU_DTYPE)),
                        (OFF_V, _heads_minor(dv[:, BLOCK:, :]).astype(MXU_DTYPE)), (OFF_GR, dgl_r)):
        dproj = lax.dynamic_update_slice(dproj, piece, (0, col0))
    for part in range(W_IN_PARTS):
        cols = slice(part * (D // W_IN_PARTS), (part + 1) * (D // W_IN_PARTS))
        side = _join(scatter.take(55), sync.begin(loss, grads)) if part == 0 else scatter.take(68)
        g32, g16 = _mm(dproj, xb[:, cols], ta=True, tm=512, tn=D // W_IN_PARTS, tk=S, out_dtype=both,
                       name="d_w_in_%d" % part, side=side)
        scatter.add_blocks("w_in_%d" % part, _blocks(g32, "rows"), _blocks(g16, "rows"))
        scatter.flush_pairs("pairs_w_in_%d" % part)
    dx = _mm(dproj, w_in_t, tm=1024, tn=1024, tk=512, add=dpre1, add_scale=ALPHA, name="d_x",
             side=_join(scatter.take(400), sync.middle()))
    return dx


SHARDED = (
    ("w_in", "cols", 368), ("w_attn_proj", "rows", 32), ("w_rnn_proj", "rows", 32), ("w_out", "rows", 32),
    ("ffn_w_up", "cols", 128), ("ffn_w_gate", "cols", 128), ("ffn_w_down", "rows", 64),
)
SMALL_REPLICATED = ("b_gate", "rnn_conv_b", "lru_wa", "lru_ba", "lru_wi", "lru_bi", "lru_lambda", "attn_sinks",
                    "ln1_g", "ln1_b", "ffn_conv_b", "ln2_g", "ln2_b")
SMALL_SHARDED = ("rnn_conv_w", "ffn_conv_w")
SMALL_MATS = ("lru_wa", "lru_wi")
W_IN_PARTS = 2
WEIGHTS = ("w_in", "b_gate", "rnn_conv_w", "rnn_conv_b", "lru_wa", "lru_ba", "lru_wi", "lru_bi", "lru_lambda",
           "attn_sinks", "w_attn_proj", "w_rnn_proj", "w_out", "ln1_g", "ln1_b", "ffn_w_up", "ffn_w_gate",
           "ffn_conv_w", "ffn_conv_b", "ffn_w_down", "ln2_g", "ln2_b")


def kernel(x, w_in, b_gate, rnn_conv_w, rnn_conv_b, lru_wa, lru_ba, lru_wi, lru_bi, lru_lambda, attn_sinks, w_attn_proj, w_rnn_proj, w_out, ln1_g, ln1_b, ffn_w_up, ffn_w_gate, ffn_conv_w, ffn_conv_b, ffn_w_down, ln2_g, ln2_b, loss_target, m_w_in, m_b_gate, m_rnn_conv_w, m_rnn_conv_b, m_lru_wa, m_lru_ba, m_lru_wi, m_lru_bi, m_lru_lambda, m_attn_sinks, m_w_attn_proj, m_w_rnn_proj, m_w_out, m_ln1_g, m_ln1_b, m_ffn_w_up, m_ffn_w_gate, m_ffn_conv_w, m_ffn_conv_b, m_ffn_w_down, m_ln2_g, m_ln2_b, v_w_in, v_b_gate, v_rnn_conv_w, v_rnn_conv_b, v_lru_wa, v_lru_ba, v_lru_wi, v_lru_bi, v_lru_lambda, v_attn_sinks, v_w_attn_proj, v_w_rnn_proj, v_w_out, v_ln1_g, v_ln1_b, v_ffn_w_up, v_ffn_w_gate, v_ffn_conv_w, v_ffn_conv_b, v_ffn_w_down, v_ln2_g, v_ln2_b):
    given = dict(locals())
    wsh = {n: given[n][0] for n in WEIGHTS}
    msh = {n: given["m_" + n][0] for n in WEIGHTS}
    vsh = {n: given["v_" + n][0] for n in WEIGHTS}
    m_given = {n: given["m_" + n] for n in WEIGHTS}
    v_given = {n: given["v_" + n] for n in WEIGHTS}
    me = 4 * lax.axis_index("x") + 2 * lax.axis_index("y") + lax.axis_index("c")

    order = ("w_in", "rnn_conv_w", "ffn_conv_w", "w_attn_proj", "w_rnn_proj", "w_out", "ffn_w_up", "ffn_w_gate",
             "ffn_w_down")
    gather = _Gather({"w_in": wsh["w_in"].T.astype(MXU_DTYPE), **{n: wsh[n] for n in order[1:3]}})
    *casts, xb = _cast_many([wsh[n] for n in order[3:]] + [x[0]], side=gather.take(through="ffn_conv_w"))
    gather.add_shards(dict(zip(order[3:], casts)))
    small = {n: given[n] for n in SMALL_REPLICATED}
    small["lru_wa"] = _block_diag(wsh["lru_wa"])
    small["lru_wi"] = _block_diag(wsh["lru_wi"])
    scatter = _Scatter(me, jnp.stack([_index(_relative(k)) for k in FAR]).astype(jnp.int32))

    vec_names = tuple(n for n in SMALL_REPLICATED if n not in SMALL_MATS) + SMALL_SHARDED
    sync = _SmallSync(vec_names, SMALL_MATS)
    dx = _forward_backward(x[0], xb, loss_target[0], small, gather, scatter, sync)

    loss_total, g_small, mat_sums = sync.end()
    loss_total = loss_total.reshape(())
    for n in SMALL_SHARDED:
        width = wsh[n].shape[1]
        g_small[n] = lax.dynamic_slice_in_dim(g_small[n], me * width, width, axis=1)
    g_small = {n: g_small[n].reshape(given[n].shape) for n in vec_names}
    out = {}
    results = _adamw_many(*[[d[n] for n in vec_names] for d in (given, m_given, v_given, g_small)])
    for n, delta, nm, nv in zip(vec_names, *results):
        out[n] = (g_small[n], delta, nm, nv)
    for n in SMALL_MATS:
        g = mat_sums[n].reshape(given[n].shape)
        out[n] = (g, *_adamw_blocks(given[n], m_given[n], v_given[n], g, name="adamw_" + n))

    tile_rows = {n: tr for n, _, tr in SHARDED}
    me1 = me.reshape(1).astype(jnp.int32)
    res = None
    for n in list(scatter.sends):
        own, pair, far = scatter.get(n)
        if n.startswith("w_in_"):
            part = int(n[len("w_in_"):])
            w_t, m_t, v_t = (a["w_in"].transpose(0, 2, 1) for a in (given, m_given, v_given))
            res = _reduce_adamw(w_t, m_t, v_t, own, pair, far, me1, tr=tile_rows["w_in"], name="adamw_" + n,
                                part=part, earlier=res if part else None)
            out["w_in"] = tuple(r.transpose(0, 2, 1) for r in res)
        else:
            out[n] = tuple(_reduce_adamw(given[n], m_given[n], v_given[n], own, pair, far, me1, tr=tile_rows[n],
                                         name="adamw_" + n))

    outputs = [loss_total, dx[None]]
    for kind in range(4):
        outputs += [out[n][kind] for n in WEIGHTS]
    return tuple(outputs)
```

```python
import math

import jax
import jax.numpy as jnp
from jax import lax
from jax.experimental import pallas as pl
from jax.experimental.pallas import tpu as pltpu

F32 = jnp.float32
BF16 = jnp.bfloat16
MXU_DTYPE = jnp.bfloat16

N_DEV = 8
S = 2048
D = 2048
HEAD_DIM = 64
N_KV = 4
GROUP = 8
BLOCK = 128
D_KV = N_KV * HEAD_DIM
D_RNN = 2560
RNN_GROUP = 640
N_RNN_GROUPS = D_RNN // RNN_GROUP
RNN_BLOCK_W = 160
RNN_CONV_W = 4
LRU_C = 8.0
D_FF = 6144
FFN_CONV_W = 3
D_IN = 11776
OFF_K = 2048
OFF_V = 2304
OFF_RX = 2560
OFF_RY = 5120
OFF_GA = 7680
OFF_GR = 9728
LN_EPS = 1e-5
ALPHA = 2.0 ** 0.25
ADAM_LR = 0.001
ADAM_B1 = 0.9
ADAM_B2 = 0.999
ADAM_EPS = 1e-08
ADAM_WD = 0.01
ADAM_STEP = 10
NEG = -1e30
VMEM_LIMIT = 56 * 1024 * 1024
MID_RIDE_TENTHS = 6
MESH = pl.DeviceIdType.MESH
GELU_C = math.sqrt(2.0 / math.pi)


def _cparams(*sem):
    return pltpu.CompilerParams(dimension_semantics=sem or None, vmem_limit_bytes=VMEM_LIMIT)


def _call(body, *, name, grid, in_specs, out_specs, out_shape, operands, semantics, scratch_shapes=(), side=None,
          window=None):
    single = not isinstance(out_shape, (list, tuple))
    out_shape = [out_shape] if single else list(out_shape)
    out_specs = [out_specs] if single else list(out_specs)
    in_specs = list(in_specs)
    operands = tuple(operands)
    scratch_shapes = list(scratch_shapes)
    hbm = pl.BlockSpec(memory_space=pltpu.HBM)
    aliases = {}
    if window is not None:
        whole, col0 = window
        block, index_map = out_specs[0].block_shape, out_specs[0].index_map
        assert col0 % block[1] == 0 and out_shape[0].dtype == whole.dtype
        out_specs[0] = pl.BlockSpec(block, lambda *g: (index_map(*g)[0], index_map(*g)[1] + col0 // block[1]))
        out_shape[0] = jax.ShapeDtypeStruct(whole.shape, whole.dtype)
        if not isinstance(whole, jax.ShapeDtypeStruct):
            aliases[len(in_specs)] = 0
            in_specs.append(hbm)
            operands += (whole,)
            compute, n_read = body, len(in_specs) - 1

            def body(*refs):
                compute(*refs[:n_read], *refs[n_read + 1:])

    if side is None:
        res = pl.pallas_call(
            body, name=name, grid=grid, in_specs=in_specs, out_specs=out_specs, out_shape=out_shape,
            scratch_shapes=scratch_shapes, input_output_aliases=aliases,
            compiler_params=_cparams(*semantics))(*operands)
        return res[0] if single else res
    n_in, n_out, n_scr = len(in_specs), len(out_shape), len(scratch_shapes)
    s_in, s_out = len(side.operands), len(side.out_shape)
    steps = math.prod(grid)
    mid_step = (steps * MID_RIDE_TENTHS) // 10

    def with_copies(*refs):
        core_in, side_in = refs[:n_in], refs[n_in:n_in + s_in]
        o0 = n_in + s_in
        core_out, side_out = refs[o0:o0 + n_out], refs[o0 + n_out:o0 + n_out + s_out]
        c0 = o0 + n_out + s_out
        core_scr, sems = refs[c0:c0 + n_scr], refs[c0 + n_scr:]
        step = 0
        for d, size in enumerate(grid):
            step = step * size + pl.program_id(d)

        @pl.when(step == 0)
        def _():
            side.start(side_in, side_out, sems)

        body(*core_in, *core_out, *core_scr)

        @pl.when(step == mid_step)
        def _():
            side.mid(side_in, side_out, sems)

        @pl.when(step == steps - 1)
        def _():
            side.finish(side_in, side_out, sems)

    res = pl.pallas_call(
        with_copies, name=name, grid=grid,
        in_specs=in_specs + [hbm] * s_in, out_specs=out_specs + [hbm] * s_out,
        out_shape=out_shape + list(side.out_shape),
        scratch_shapes=scratch_shapes + list(side.sems),
        input_output_aliases={**aliases, **{n_in + i: n_out + o for i, o in side.aliases.items()}},
        compiler_params=_cparams(*(("arbitrary",) * len(grid))))(*operands, *side.operands)
    side.done(res[n_out:])
    return res[0] if single else res[:n_out]


def _run_side(side, name):
    def body(*refs):
        s_in, s_out = len(side.operands), len(side.out_shape)
        side.start(refs[:s_in], refs[s_in:s_in + s_out], refs[s_in + s_out:])
        side.mid(refs[:s_in], refs[s_in:s_in + s_out], refs[s_in + s_out:])
        side.finish(refs[:s_in], refs[s_in:s_in + s_out], refs[s_in + s_out:])

    hbm = pl.BlockSpec(memory_space=pltpu.HBM)
    res = pl.pallas_call(
        body, name=name, in_specs=[hbm] * len(side.operands), out_specs=[hbm] * len(side.out_shape),
        out_shape=list(side.out_shape), scratch_shapes=list(side.sems),
        input_output_aliases=dict(side.aliases))(*side.operands)
    side.done(res)


def _gelu(x):
    x2 = x * x
    t = jnp.tanh(GELU_C * (x + 0.044715 * x * x2))
    g = 0.5 * x * (1.0 + t)
    dg = 0.5 * (1.0 + t) + 0.5 * x * (1.0 - t * t) * (GELU_C * (1.0 + 3.0 * 0.044715 * x2))
    return g, dg


def _sigmoid(x):
    return 1.0 / (1.0 + jnp.exp(-x))


def _softplus(x):
    z = jnp.exp(-jnp.abs(x))
    small = z * (1.0 - z * (0.5 - z * (1.0 / 3.0 - 0.25 * z)))
    return jnp.maximum(x, 0.0) + jnp.where(z < 0.02, small, jnp.log(1.0 + z))


def _one_minus_exp(x):
    series = -x * (1.0 + x * (0.5 + x * (1.0 / 6.0 + x * (1.0 / 24.0))))
    return jnp.where(x > -0.03, series, 1.0 - jnp.exp(x))


def _colsum(v):
    return jnp.sum(v, axis=0, keepdims=True)


def _mm(a, b, *, tm, tn, tk, name, ta=False, tb=False, out_dtype=F32, b_block=None, out_block=None, add=None,
        add_scale=1.0, side=None):
    out_dtypes = out_dtype if isinstance(out_dtype, tuple) else (out_dtype,)
    if ta:
        k_dim, m_dim = a.shape
    else:
        m_dim, k_dim = a.shape
    if b_block is None:
        n_dim = b.shape[0] if tb else b.shape[1]
    else:
        n_dim = b.shape[1] if tb else b.shape[0] * b_block
    assert m_dim % tm == 0 and n_dim % tn == 0 and k_dim % tk == 0, (name, m_dim, n_dim, k_dim)
    nk = k_dim // tk
    dims = (((0 if ta else 1,), (1 if tb else 0,)), ((), ()))
    has_add = add is not None

    def body(*refs):
        a_ref, b_ref = refs[0], refs[1]
        add_ref = refs[2] if has_add else None
        first_out = 3 if has_add else 2
        o_refs = refs[first_out:first_out + len(out_dtypes)]

        def product():
            return lax.dot_general(a_ref[...].astype(MXU_DTYPE), b_ref[...].astype(MXU_DTYPE), dims,
                                   preferred_element_type=F32)

        def finish(acc):
            if has_add:
                acc = acc + add_scale * add_ref[...]
            for o_ref in o_refs:
                o_ref[...] = acc.astype(o_ref.dtype)

        if nk == 1:
            finish(product())
        else:
            acc_ref = refs[-1]
            k = pl.program_id(2)

            @pl.when(k == 0)
            def _():
                acc_ref[...] = jnp.zeros_like(acc_ref)

            acc_ref[...] += product()

            @pl.when(k == nk - 1)
            def _():
                finish(acc_ref[...])

    if ta:
        a_spec = pl.BlockSpec((tk, tm), lambda i, j, k: (k, i))
    else:
        a_spec = pl.BlockSpec((tm, tk), lambda i, j, k: (i, k))
    if b_block is None:
        if tb:
            b_spec = pl.BlockSpec((tn, tk), lambda i, j, k: (j, k))
        else:
            b_spec = pl.BlockSpec((tk, tn), lambda i, j, k: (k, j))
    elif tb:
        assert b_block % tk == 0
        b_spec = pl.BlockSpec((None, tn, tk), lambda i, j, k: ((k * tk) // b_block, j, ((k * tk) % b_block) // tk))
    else:
        assert b_block % tn == 0
        b_spec = pl.BlockSpec((None, tk, tn), lambda i, j, k: ((j * tn) // b_block, k, ((j * tn) % b_block) // tn))
    in_specs = [a_spec, b_spec]
    operands = [a, b]
    if has_add:
        in_specs.append(pl.BlockSpec((tm, tn), lambda i, j, k: (i, j)))
        operands.append(add)
    if out_block is None:
        out_spec = pl.BlockSpec((tm, tn), lambda i, j, k: (i, j))
        out_dims = (m_dim, n_dim)
    else:
        assert out_block % tn == 0
        out_spec = pl.BlockSpec((None, tm, tn), lambda i, j, k: ((j * tn) // out_block, i, ((j * tn) % out_block) // tn))
        out_dims = (n_dim // out_block, m_dim, out_block)
    res = _call(
        body,
        name=name,
        grid=(m_dim // tm, n_dim // tn, nk),
        in_specs=in_specs,
        out_specs=[out_spec] * len(out_dtypes),
        out_shape=[jax.ShapeDtypeStruct(out_dims, dt) for dt in out_dtypes],
        scratch_shapes=[pltpu.VMEM((tm, tn), F32)] if nk > 1 else [],
        semantics=("parallel", "parallel", "arbitrary"),
        operands=tuple(operands),
        side=side,
    )
    return res if isinstance(out_dtype, tuple) else res[0]


def _attn_bias(bias_ref, h):
    key = lax.broadcasted_iota(jnp.int32, (2 * BLOCK, GROUP * BLOCK), 0)
    col = lax.broadcasted_iota(jnp.int32, (2 * BLOCK, GROUP * BLOCK), 1)
    dist = BLOCK + (col & (BLOCK - 1)) - key
    head = h * GROUP + (col >> 7) + 1
    slope = jnp.exp(head.astype(F32) * (-0.25 * math.log(2.0)))
    bias = jnp.where((dist >= 0) & (dist < BLOCK), -slope * dist.astype(F32), NEG)
    bias_ref[1] = bias
    bias_ref[0] = jnp.where(key < BLOCK, NEG, bias)


def _attn_probs(kb, qt, bias, sink):
    s = jnp.dot(kb, qt, preferred_element_type=F32) * (HEAD_DIM ** -0.5) + bias
    m = jnp.maximum(jnp.max(s, axis=0, keepdims=True), sink)
    e = jnp.exp(s - m)
    e_sink = jnp.exp(sink - m)
    inv = 1.0 / (jnp.sum(e, axis=0, keepdims=True) + e_sink)
    return e * inv, e_sink * inv


def _heads_on_lanes(ref, r0):
    return jnp.concatenate([ref[g, :, pl.ds(r0, BLOCK)] for g in range(GROUP)], axis=1)


def _attn_fwd(qt, kp, vt, sink_row, side=None):
    cols = GROUP * BLOCK

    def body(q_ref, k_ref, vt_ref, sink_ref, o_ref, bias_ref):
        _attn_bias(bias_ref, pl.program_id(0))
        sink = sink_ref[...]

        def step(n, carry):
            r0 = pl.multiple_of(n * BLOCK, BLOCK)
            p, _ = _attn_probs(k_ref[pl.ds(r0, 2 * BLOCK), :], _heads_on_lanes(q_ref, r0),
                               bias_ref[jnp.minimum(n, 1)], sink)
            o = jnp.dot(vt_ref[:, pl.ds(r0, 2 * BLOCK)], p.astype(MXU_DTYPE), preferred_element_type=F32)
            for g in range(GROUP):
                o_ref[g, :, pl.ds(r0, BLOCK)] = o[:, g * BLOCK:(g + 1) * BLOCK].astype(o_ref.dtype)
            return carry

        lax.fori_loop(0, S // BLOCK, step, 0)

    hm = pl.BlockSpec((None, GROUP, HEAD_DIM, S), lambda h: (h, 0, 0, 0))
    return _call(
        body,
        name="attn_fwd",
        grid=(N_KV,),
        in_specs=[
            hm,
            pl.BlockSpec((None, BLOCK + S, HEAD_DIM), lambda h: (h, 0, 0)),
            pl.BlockSpec((None, HEAD_DIM, BLOCK + S), lambda h: (h, 0, 0)),
            pl.BlockSpec((None, 1, cols), lambda h: (h, 0, 0)),
        ],
        out_specs=hm,
        out_shape=jax.ShapeDtypeStruct((N_KV, GROUP, HEAD_DIM, S), MXU_DTYPE),
        scratch_shapes=[pltpu.VMEM((2, 2 * BLOCK, cols), F32)],
        semantics=("parallel",),
        operands=(qt, kp, vt, sink_row),
        side=side,
    )


def _attn_bwd(qt, kp, kt, vp, sink_row, dot_, side=None):
    cols = GROUP * BLOCK

    def body(q_ref, k_ref, kt_ref, v_ref, sink_ref, do_ref, dq_ref, dk_ref, dv_ref, dsink_ref, bias_ref):
        _attn_bias(bias_ref, pl.program_id(0))
        sink = sink_ref[...]
        dk_ref[...] = jnp.zeros_like(dk_ref)
        dv_ref[...] = jnp.zeros_like(dv_ref)
        nt = (((1,), (1,)), ((), ()))

        def step(n, sink_acc):
            r0 = pl.multiple_of(n * BLOCK, BLOCK)
            band = pl.ds(r0, 2 * BLOCK)
            qn = _heads_on_lanes(q_ref, r0)
            don = _heads_on_lanes(do_ref, r0)
            p, p_sink = _attn_probs(k_ref[band, :], qn, bias_ref[jnp.minimum(n, 1)], sink)
            dp = jnp.dot(v_ref[band, :], don, preferred_element_type=F32)
            delta = jnp.sum(p * dp, axis=0, keepdims=True)
            ds = (p * (dp - delta) * (HEAD_DIM ** -0.5)).astype(MXU_DTYPE)
            dq = jnp.dot(kt_ref[:, band], ds, preferred_element_type=F32)
            for g in range(GROUP):
                dq_ref[g, :, pl.ds(r0, BLOCK)] = dq[:, g * BLOCK:(g + 1) * BLOCK].astype(dq_ref.dtype)
            dk_ref[band, :] += lax.dot_general(ds, qn, nt, preferred_element_type=F32)
            dv_ref[band, :] += lax.dot_general(p.astype(MXU_DTYPE), don, nt, preferred_element_type=F32)
            return sink_acc - p_sink * delta

        sink_acc = lax.fori_loop(0, S // BLOCK, step, jnp.zeros((1, cols), F32))
        for g in range(GROUP):
            dsink_ref[g:g + 1, :] = jnp.sum(sink_acc[:, g * BLOCK:(g + 1) * BLOCK], axis=1, keepdims=True)

    hm = pl.BlockSpec((None, GROUP, HEAD_DIM, S), lambda h: (h, 0, 0, 0))
    kv = pl.BlockSpec((None, BLOCK + S, HEAD_DIM), lambda h: (h, 0, 0))
    return _call(
        body,
        name="attn_bwd",
        grid=(N_KV,),
        in_specs=[hm, kv, pl.BlockSpec((None, HEAD_DIM, BLOCK + S), lambda h: (h, 0, 0)), kv,
                  pl.BlockSpec((None, 1, cols), lambda h: (h, 0, 0)), hm],
        out_specs=[hm, kv, kv, pl.BlockSpec((None, GROUP, 1), lambda h: (h, 0, 0))],
        out_shape=[
            jax.ShapeDtypeStruct((N_KV, GROUP, HEAD_DIM, S), MXU_DTYPE),
            jax.ShapeDtypeStruct((N_KV, BLOCK + S, HEAD_DIM), F32),
            jax.ShapeDtypeStruct((N_KV, BLOCK + S, HEAD_DIM), F32),
            jax.ShapeDtypeStruct((N_KV, GROUP, 1), F32),
        ],
        scratch_shapes=[pltpu.VMEM((2, 2 * BLOCK, cols), F32)],
        semantics=("parallel",),
        operands=(qt, kp, kt, vp, sink_row, dot_),
        side=side,
    )


PAD = 8
CHUNK = 256


def _past_taps(xpad_ref, r0, width):
    ext = xpad_ref[pl.ds(r0, CHUNK + PAD), :]
    taps = []
    for k in range(width):
        back = width - 1 - k
        taps.append((ext if back == 0 else pltpu.roll(ext, back, 0))[PAD:, :])
    return taps


def _future_taps(xpad_ref, r0, width):
    ext = xpad_ref[pl.ds(r0, CHUNK + PAD), :]
    taps = []
    for ahead in range(width):
        taps.append((ext if ahead == 0 else pltpu.roll(ext, CHUNK + PAD - ahead, 0))[:CHUNK, :])
    return taps


def _conv_fwd(src, col0, w, b, *, tc, name, side=None):
    width, c_dim = w.shape

    def body(x_ref, w_ref, b_ref, o_ref, xpad_ref):
        xpad_ref[pl.ds(0, PAD), :] = jnp.zeros((PAD, tc), F32)
        xpad_ref[pl.ds(PAD, S), :] = x_ref[...]
        wv = w_ref[...]
        bv = b_ref[...]

        def step(ci, carry):
            r0 = pl.multiple_of(ci * CHUNK, CHUNK)
            taps = _past_taps(xpad_ref, r0, width)
            y = bv + taps[0] * wv[0:1, :]
            for k in range(1, width):
                y = y + taps[k] * wv[k:k + 1, :]
            o_ref[pl.ds(r0, CHUNK), :] = y
            return carry

        lax.fori_loop(0, S // CHUNK, step, 0)

    return _call(
        body,
        name=name,
        grid=(c_dim // tc,),
        in_specs=[
            pl.BlockSpec((S, tc), lambda j: (0, col0 // tc + j)),
            pl.BlockSpec((width, tc), lambda j: (0, j)),
            pl.BlockSpec((1, tc), lambda j: (0, j)),
        ],
        out_specs=pl.BlockSpec((S, tc), lambda j: (0, j)),
        out_shape=jax.ShapeDtypeStruct((S, c_dim), F32),
        scratch_shapes=[pltpu.VMEM((S + PAD, tc), F32)],
        semantics=("parallel",),
        operands=(src, w, b),
        side=side,
    )


def _conv_bwd(dy, src, col0, w, *, tc, name, side=None, window=None):
    width, c_dim = w.shape

    def body(dy_ref, x_ref, w_ref, dx_ref, dw_ref, db_ref, xpad_ref, dpad_ref):
        xpad_ref[pl.ds(0, PAD), :] = jnp.zeros((PAD, tc), F32)
        xpad_ref[pl.ds(PAD, S), :] = x_ref[...]
        dpad_ref[pl.ds(0, S), :] = dy_ref[...]
        dpad_ref[pl.ds(S, PAD), :] = jnp.zeros((PAD, tc), F32)
        wv = w_ref[...]

        def step(ci, acc):
            r0 = pl.multiple_of(ci * CHUNK, CHUNK)
            past = _past_taps(xpad_ref, r0, width)
            ahead = _future_taps(dpad_ref, r0, width)
            d = ahead[0]
            dx = d * wv[width - 1:width, :]
            for j in range(1, width):
                dx = dx + ahead[j] * wv[width - 1 - j:width - j, :]
            dx_ref[pl.ds(r0, CHUNK), :] = dx.astype(dx_ref.dtype)
            return tuple(acc[k] + _colsum(past[k] * d) for k in range(width)) + (acc[width] + _colsum(d),)

        zero = jnp.zeros((1, tc), F32)
        acc = lax.fori_loop(0, S // CHUNK, step, (zero,) * (width + 1))
        for k in range(width):
            dw_ref[k:k + 1, :] = acc[k]
        db_ref[...] = acc[width]

    return _call(
        body,
        name=name,
        grid=(c_dim // tc,),
        in_specs=[
            pl.BlockSpec((S, tc), lambda j: (0, j)),
            pl.BlockSpec((S, tc), lambda j: (0, col0 // tc + j)),
            pl.BlockSpec((width, tc), lambda j: (0, j)),
        ],
        out_specs=[
            pl.BlockSpec((S, tc), lambda j: (0, j)),
            pl.BlockSpec((width, tc), lambda j: (0, j)),
            pl.BlockSpec((1, tc), lambda j: (0, j)),
        ],
        out_shape=[
            jax.ShapeDtypeStruct((S, c_dim), MXU_DTYPE),
            jax.ShapeDtypeStruct((width, c_dim), F32),
            jax.ShapeDtypeStruct((1, c_dim), F32),
        ],
        scratch_shapes=[pltpu.VMEM((S + PAD, tc), F32), pltpu.VMEM((S + PAD, tc), F32)],
        semantics=("parallel",),
        operands=(dy, src, w),
        side=side,
        window=window,
    )


SCAN_TC = 256


def _lru_gates(rxc, wa, wi, ba, bi, side=None):
    tm = 512

    def body(x_ref, wa_ref, wi_ref, ba_ref, bi_ref, r_ref, i_ref):
        xv = x_ref[...].astype(MXU_DTYPE)
        r_ref[...] = _sigmoid(jnp.dot(xv, wa_ref[...].astype(MXU_DTYPE), preferred_element_type=F32) + ba_ref[...])
        i_ref[...] = _sigmoid(jnp.dot(xv, wi_ref[...].astype(MXU_DTYPE), preferred_element_type=F32) + bi_ref[...])

    x_spec = pl.BlockSpec((tm, RNN_GROUP), lambda g, i: (i, g))
    w_spec = pl.BlockSpec((None, RNN_GROUP, RNN_GROUP), lambda g, i: (g, 0, 0))
    b_spec = pl.BlockSpec((1, RNN_GROUP), lambda g, i: (0, g))
    return _call(
        body,
        name="lru_gates",
        grid=(N_RNN_GROUPS, S // tm),
        in_specs=[x_spec, w_spec, w_spec, b_spec, b_spec],
        out_specs=[x_spec, x_spec],
        out_shape=[jax.ShapeDtypeStruct((S, D_RNN), F32)] * 2,
        semantics=("parallel", "parallel"),
        operands=(rxc, wa, wi, ba, bi),
        side=side,
    )


def _scan_down(a, u, row):
    for d in (1, 2, 4):
        a_s = jnp.where(row >= d, pltpu.roll(a, d, 0), 1.0)
        u_s = jnp.where(row >= d, pltpu.roll(u, d, 0), 0.0)
        u = a * u_s + u
        a = a * a_s
    return a, u


def _scan_up(a, u, row):
    for d in (1, 2, 4):
        a_s = jnp.where(row < 8 - d, pltpu.roll(a, 8 - d, 0), 1.0)
        u_s = jnp.where(row < 8 - d, pltpu.roll(u, 8 - d, 0), 0.0)
        u = a * u_s + u
        a = a * a_s
    return a, u


def _lru_scan_fwd(r, i, rxc, proj, lam, side=None):
    tc = SCAN_TC

    def body(r_ref, i_ref, x_ref, ry_ref, lam_ref, h_ref, y_ref):
        rate = LRU_C * _softplus(-lam_ref[...])
        row = lax.broadcasted_iota(jnp.int32, (8, tc), 0)

        def step(ci, carry):
            r0 = pl.multiple_of(ci * 16, 16)
            log_a = -rate * r_ref[pl.ds(r0, 16), :]
            a16 = jnp.exp(log_a)
            u16 = jnp.sqrt(_one_minus_exp(2.0 * log_a)) * (i_ref[pl.ds(r0, 16), :] * x_ref[pl.ds(r0, 16), :])
            hs = []
            for half in range(2):
                a_cum, h0 = _scan_down(a16[8 * half:8 * half + 8, :], u16[8 * half:8 * half + 8, :], row)
                h = a_cum * carry + h0
                carry = jnp.broadcast_to(h[7:8, :], (8, tc))
                hs.append(h)
            h16 = jnp.concatenate(hs, axis=0)
            h_ref[pl.ds(r0, 16), :] = h16
            y_ref[pl.ds(r0, 16), :] = (h16 * _gelu(ry_ref[pl.ds(r0, 16), :])[0]).astype(y_ref.dtype)
            return carry

        lax.fori_loop(0, S // 16, step, jnp.zeros((8, tc), F32))

    col = pl.BlockSpec((S, tc), lambda j: (0, j))
    return _call(
        body,
        name="lru_scan_fwd",
        grid=(D_RNN // tc,),
        in_specs=[col, col, col, pl.BlockSpec((S, tc), lambda j: (0, OFF_RY // tc + j)),
                  pl.BlockSpec((1, tc), lambda j: (0, j))],
        out_specs=[col, col],
        out_shape=[jax.ShapeDtypeStruct((S, D_RNN), F32), jax.ShapeDtypeStruct((S, D_RNN), MXU_DTYPE)],
        semantics=("parallel",),
        operands=(r, i, rxc, proj, lam),
        side=side,
    )


def _lru_scan_bwd(dy, proj, h, r, i, rxc, lam, side=None, window=None):
    tc = SCAN_TC

    def body(dy_ref, ry_ref, h_ref, r_ref, i_ref, x_ref, lam_ref,
             dry_ref, dzr_ref, dzi_ref, dx_ref, dba_ref, dbi_ref, dlam_ref, a_ref, dh_ref, hp_ref):
        lam_v = lam_ref[...]
        rate = LRU_C * _softplus(-lam_v)
        dlam_scale = LRU_C * _sigmoid(-lam_v)
        row = lax.broadcasted_iota(jnp.int32, (8, tc), 0)
        hp_ref[pl.ds(0, PAD), :] = jnp.zeros((PAD, tc), F32)
        hp_ref[pl.ds(PAD, S), :] = h_ref[...]
        a_ref[pl.ds(S, PAD), :] = jnp.zeros((PAD, tc), F32)

        def prep(ci, carry):
            r0 = pl.multiple_of(ci * CHUNK, CHUNK)
            a_ref[pl.ds(r0, CHUNK), :] = jnp.exp(-rate * r_ref[pl.ds(r0, CHUNK), :])
            ge, dge = _gelu(ry_ref[pl.ds(r0, CHUNK), :])
            dyv = dy_ref[pl.ds(r0, CHUNK), :]
            dh_ref[pl.ds(r0, CHUNK), :] = dyv * ge
            dry_ref[pl.ds(r0, CHUNK), :] = (dyv * h_ref[pl.ds(r0, CHUNK), :] * dge).astype(dry_ref.dtype)
            return carry

        lax.fori_loop(0, S // CHUNK, prep, 0)

        def step(ci, state):
            carry, dba, dbi, dlam = state
            r0 = pl.multiple_of(S - 16 - ci * 16, 16)
            a_ext = a_ref[pl.ds(r0, 24), :]
            a_next = pltpu.roll(a_ext, 23, 0)
            h_prev = pltpu.roll(hp_ref[pl.ds(r0, 24), :], 1, 0)
            dh16 = dh_ref[pl.ds(r0, 16), :]
            gs = [None, None]
            for half in (1, 0):
                lo = 8 * half
                c_cum, g0 = _scan_up(a_next[lo:lo + 8, :], dh16[lo:lo + 8, :], row)
                g = c_cum * carry + g0
                carry = jnp.broadcast_to(g[0:1, :], (8, tc))
                gs[half] = g
            g16 = jnp.concatenate(gs, axis=0)
            a16 = a_ext[0:16, :]
            r16 = r_ref[pl.ds(r0, 16), :]
            i16 = i_ref[pl.ds(r0, 16), :]
            x16 = x_ref[pl.ds(r0, 16), :]
            a2 = a16 * a16
            sq = jnp.sqrt(_one_minus_exp(-2.0 * rate * r16))
            dx_ref[pl.ds(r0, 16), :] = g16 * sq * i16
            dzi = g16 * sq * x16 * i16 * (1.0 - i16)
            dlog_a = g16 * h_prev[8:24, :] * a16 - g16 * i16 * x16 * a2 / sq
            dzr = -rate * dlog_a * r16 * (1.0 - r16)
            dzr_ref[pl.ds(r0, 16), :] = dzr.astype(dzr_ref.dtype)
            dzi_ref[pl.ds(r0, 16), :] = dzi.astype(dzi_ref.dtype)
            return carry, dba + _colsum(dzr), dbi + _colsum(dzi), dlam + _colsum(dlog_a * r16)

        zero = jnp.zeros((1, tc), F32)
        _, dba, dbi, dlam = lax.fori_loop(0, S // 16, step, (jnp.zeros((8, tc), F32), zero, zero, zero))
        dba_ref[...] = dba
        dbi_ref[...] = dbi
        dlam_ref[...] = dlam * dlam_scale

    col = pl.BlockSpec((S, tc), lambda j: (0, j))
    vec = pl.BlockSpec((1, tc), lambda j: (0, j))
    return _call(
        body,
        name="lru_scan_bwd",
        grid=(D_RNN // tc,),
        in_specs=[col, pl.BlockSpec((S, tc), lambda j: (0, OFF_RY // tc + j)), col, col, col, col, vec],
        out_specs=[col, col, col, col, vec, vec, vec],
        out_shape=[jax.ShapeDtypeStruct((S, D_RNN), MXU_DTYPE)] * 3 + [jax.ShapeDtypeStruct((S, D_RNN), F32)]
        + [jax.ShapeDtypeStruct((1, D_RNN), F32)] * 3,
        scratch_shapes=[pltpu.VMEM((S + PAD, tc), F32), pltpu.VMEM((S, tc), F32), pltpu.VMEM((S + PAD, tc), F32)],
        semantics=("parallel",),
        operands=(dy, proj, h, r, i, rxc, lam),
        side=side,
        window=window,
    )


def _lru_gate_wgrad(rxc, dzr, dzi, side=None):
    def body(x_ref, dzr_ref, dzi_ref, dwa_ref, dwi_ref):
        xv = x_ref[...].astype(MXU_DTYPE)
        dims = (((0,), (0,)), ((), ()))
        dwa_ref[...] = lax.dot_general(xv, dzr_ref[...], dims, preferred_element_type=F32)
        dwi_ref[...] = lax.dot_general(xv, dzi_ref[...], dims, preferred_element_type=F32)

    col = pl.BlockSpec((S, RNN_GROUP), lambda g: (0, g))
    w_spec = pl.BlockSpec((None, RNN_GROUP, RNN_GROUP), lambda g: (g, 0, 0))
    return _call(
        body,
        name="lru_gate_wgrad",
        grid=(N_RNN_GROUPS,),
        in_specs=[col, col, col],
        out_specs=[w_spec, w_spec],
        out_shape=[jax.ShapeDtypeStruct((N_RNN_GROUPS, RNN_GROUP, RNN_GROUP), F32)] * 2,
        semantics=("parallel",),
        operands=(rxc, dzr, dzi),
        side=side,
    )


def _lru_gate_xgrad(dzr, dzi, wa, wi, dx_in, side=None):
    tm = 512

    def body(dzr_ref, dzi_ref, wa_ref, wi_ref, dx_ref, o_ref):
        dims = (((1,), (1,)), ((), ()))
        o_ref[...] = (dx_ref[...]
                      + lax.dot_general(dzr_ref[...], wa_ref[...].astype(MXU_DTYPE), dims, preferred_element_type=F32)
                      + lax.dot_general(dzi_ref[...], wi_ref[...].astype(MXU_DTYPE), dims, preferred_element_type=F32))

    x_spec = pl.BlockSpec((tm, RNN_GROUP), lambda g, i: (i, g))
    w_spec = pl.BlockSpec((None, RNN_GROUP, RNN_GROUP), lambda g, i: (g, 0, 0))
    return _call(
        body,
        name="lru_gate_xgrad",
        grid=(N_RNN_GROUPS, S // tm),
        in_specs=[x_spec, x_spec, w_spec, w_spec, x_spec],
        out_specs=x_spec,
        out_shape=jax.ShapeDtypeStruct((S, D_RNN), F32),
        semantics=("parallel", "parallel"),
        operands=(dzr, dzi, wa, wi, dx_in),
        side=side,
    )


def _gate_fwd(y_attn, y_rnn, proj, b_gate, side=None):
    t = 512

    def body(ya_ref, yr_ref, ga_ref, gr_ref, ba_ref, br_ref, o_ref):
        o_ref[...] = (_sigmoid(ga_ref[...] + ba_ref[...]) * ya_ref[...]
                      + _sigmoid(gr_ref[...] + br_ref[...]) * yr_ref[...]).astype(o_ref.dtype)

    tile = pl.BlockSpec((t, t), lambda i, j: (i, j))
    return _call(
        body,
        name="gate_fwd",
        grid=(S // t, D // t),
        in_specs=[tile, tile,
                  pl.BlockSpec((t, t), lambda i, j: (i, OFF_GA // t + j)),
                  pl.BlockSpec((t, t), lambda i, j: (i, OFF_GR // t + j)),
                  pl.BlockSpec((1, t), lambda i, j: (0, j)),
                  pl.BlockSpec((1, t), lambda i, j: (0, D // t + j))],
        out_specs=tile,
        out_shape=jax.ShapeDtypeStruct((S, D), MXU_DTYPE),
        semantics=("parallel", "parallel"),
        operands=(y_attn, y_rnn, proj, proj, b_gate, b_gate),
        side=side,
    )


def _gate_bwd(dmix, y_attn, y_rnn, proj, b_gate, side=None, window=None):
    t = 512

    def body(dm_ref, ya_ref, yr_ref, ga_ref, gr_ref, ba_ref, br_ref,
             dga_ref, dya_ref, dyr_ref, dgr_ref, dba_ref, dbr_ref):
        @pl.when(pl.program_id(1) == 0)
        def _():
            dba_ref[...] = jnp.zeros_like(dba_ref)
            dbr_ref[...] = jnp.zeros_like(dbr_ref)

        dm = dm_ref[...]
        ga = _sigmoid(ga_ref[...] + ba_ref[...])
        gr = _sigmoid(gr_ref[...] + br_ref[...])
        dya_ref[...] = (dm * ga).astype(dya_ref.dtype)
        dyr_ref[...] = (dm * gr).astype(dyr_ref.dtype)
        dga = dm * ya_ref[...] * ga * (1.0 - ga)
        dgr = dm * yr_ref[...] * gr * (1.0 - gr)
        dga_ref[...] = dga.astype(dga_ref.dtype)
        dgr_ref[...] = dgr.astype(dgr_ref.dtype)
        dba_ref[...] += _colsum(dga)
        dbr_ref[...] += _colsum(dgr)

    tile = pl.BlockSpec((t, t), lambda j, i: (i, j))
    vec = pl.BlockSpec((1, t), lambda j, i: (0, j))
    return _call(
        body,
        name="gate_bwd",
        grid=(D // t, S // t),
        in_specs=[tile, tile, tile,
                  pl.BlockSpec((t, t), lambda j, i: (i, OFF_GA // t + j)),
                  pl.BlockSpec((t, t), lambda j, i: (i, OFF_GR // t + j)),
                  vec,
                  pl.BlockSpec((1, t), lambda j, i: (0, D // t + j))],
        out_specs=[tile, tile, tile, tile, vec, vec],
        out_shape=[jax.ShapeDtypeStruct((S, D), MXU_DTYPE)] * 4 + [jax.ShapeDtypeStruct((1, D), F32)] * 2,
        semantics=("parallel", "arbitrary"),
        operands=(dmix, y_attn, y_rnn, proj, proj, b_gate, b_gate),
        side=side,
        window=window,
    )


LN_TM = 256


def _ln_stats(pre):
    mu = jnp.mean(pre, axis=-1, keepdims=True)
    xc = pre - mu
    rstd = lax.rsqrt(jnp.mean(xc * xc, axis=-1, keepdims=True) + LN_EPS)
    return xc * rstd, rstd


def _ln_input_grad(dy, xhat, rstd, g):
    dyg = dy * g
    return rstd * (dyg - jnp.mean(dyg, axis=-1, keepdims=True)
                   - xhat * jnp.mean(dyg * xhat, axis=-1, keepdims=True))


def _ln_fwd(res, branch, g, b, side=None):
    def body(res_ref, br_ref, g_ref, b_ref, y_ref, yb_ref, xhat_ref, rstd_ref):
        xhat, rstd = _ln_stats(ALPHA * res_ref[...] + br_ref[...])
        y = xhat * g_ref[...] + b_ref[...]
        y_ref[...] = y
        yb_ref[...] = y.astype(yb_ref.dtype)
        xhat_ref[...] = xhat
        rstd_ref[...] = rstd

    tile = pl.BlockSpec((LN_TM, D), lambda i: (i, 0))
    vec = pl.BlockSpec((1, D), lambda i: (0, 0))
    return _call(
        body,
        name="ln_fwd",
        grid=(S // LN_TM,),
        in_specs=[tile, tile, vec, vec],
        out_specs=[tile, tile, tile, pl.BlockSpec((LN_TM, 1), lambda i: (i, 0))],
        out_shape=[jax.ShapeDtypeStruct((S, D), F32), jax.ShapeDtypeStruct((S, D), MXU_DTYPE),
                   jax.ShapeDtypeStruct((S, D), F32), jax.ShapeDtypeStruct((S, 1), F32)],
        semantics=("parallel",),
        operands=(res, branch, g, b),
        side=side,
    )


def _ln_bwd(dy_a, dy_b, xhat, rstd, g, side=None):
    def body(da_ref, db_in_ref, xhat_ref, rstd_ref, g_ref, dp_ref, dpb_ref, dg_ref, db_ref):
        @pl.when(pl.program_id(0) == 0)
        def _():
            dg_ref[...] = jnp.zeros_like(dg_ref)
            db_ref[...] = jnp.zeros_like(db_ref)

        dy = da_ref[...] + ALPHA * db_in_ref[...]
        xhat = xhat_ref[...]
        dp = _ln_input_grad(dy, xhat, rstd_ref[...], g_ref[...])
        dp_ref[...] = dp
        dpb_ref[...] = dp.astype(dpb_ref.dtype)
        dg_ref[...] += _colsum(dy * xhat)
        db_ref[...] += _colsum(dy)

    tile = pl.BlockSpec((LN_TM, D), lambda i: (i, 0))
    vec = pl.BlockSpec((1, D), lambda i: (0, 0))
    return _call(
        body,
        name="ln_bwd",
        grid=(S // LN_TM,),
        in_specs=[tile, tile, tile, pl.BlockSpec((LN_TM, 1), lambda i: (i, 0)), vec],
        out_specs=[tile, tile, vec, vec],
        out_shape=[jax.ShapeDtypeStruct((S, D), F32), jax.ShapeDtypeStruct((S, D), MXU_DTYPE),
                   jax.ShapeDtypeStruct((1, D), F32), jax.ShapeDtypeStruct((1, D), F32)],
        semantics=("arbitrary",),
        operands=(dy_a, dy_b, xhat, rstd, g),
        side=side,
    )


def _ln_loss_bwd(res, branch, g, b, target, side=None):
    def body(res_ref, br_ref, g_ref, b_ref, t_ref, loss_ref, dp_ref, dpb_ref, dg_ref, db_ref):
        @pl.when(pl.program_id(0) == 0)
        def _():
            loss_ref[...] = jnp.zeros_like(loss_ref)
            dg_ref[...] = jnp.zeros_like(dg_ref)
            db_ref[...] = jnp.zeros_like(db_ref)

        xhat, rstd = _ln_stats(ALPHA * res_ref[...] + br_ref[...])
        gv = g_ref[...]
        err = xhat * gv + b_ref[...] - t_ref[...]
        loss_ref[...] += (0.5 / D) * jnp.sum(_colsum(err * err), axis=1, keepdims=True)
        dy = err * (1.0 / D)
        dp = _ln_input_grad(dy, xhat, rstd, gv)
        dp_ref[...] = dp
        dpb_ref[...] = dp.astype(dpb_ref.dtype)
        dg_ref[...] += _colsum(dy * xhat)
        db_ref[...] += _colsum(dy)

    tile = pl.BlockSpec((LN_TM, D), lambda i: (i, 0))
    vec = pl.BlockSpec((1, D), lambda i: (0, 0))
    return _call(
        body,
        name="ln_loss_bwd",
        grid=(S // LN_TM,),
        in_specs=[tile, tile, vec, vec, tile],
        out_specs=[pl.BlockSpec((1, 1), lambda i: (0, 0)), tile, tile, vec, vec],
        out_shape=[jax.ShapeDtypeStruct((1, 1), F32), jax.ShapeDtypeStruct((S, D), F32),
                   jax.ShapeDtypeStruct((S, D), MXU_DTYPE),
                   jax.ShapeDtypeStruct((1, D), F32), jax.ShapeDtypeStruct((1, D), F32)],
        semantics=("arbitrary",),
        operands=(res, branch, g, b, target),
        side=side,
    )


FFN_TC = 256


def _ffn_act_fwd(up, gpre, w, b, side=None):
    tc = FFN_TC

    def body(up_ref, x_ref, w_ref, b_ref, o_ref, xpad_ref):
        xpad_ref[pl.ds(0, PAD), :] = jnp.zeros((PAD, tc), F32)
        xpad_ref[pl.ds(PAD, S), :] = x_ref[...]
        wv = w_ref[...]
        bv = b_ref[...]

        def step(ci, carry):
            r0 = pl.multiple_of(ci * CHUNK, CHUNK)
            taps = _past_taps(xpad_ref, r0, FFN_CONV_W)
            gate = bv + taps[0] * wv[0:1, :] + taps[1] * wv[1:2, :] + taps[2] * wv[2:3, :]
            o_ref[pl.ds(r0, CHUNK), :] = (_gelu(gate)[0] * up_ref[pl.ds(r0, CHUNK), :]).astype(o_ref.dtype)
            return carry

        lax.fori_loop(0, S // CHUNK, step, 0)

    col = pl.BlockSpec((S, tc), lambda j: (0, j))
    return _call(
        body,
        name="ffn_act_fwd",
        grid=(D_FF // tc,),
        in_specs=[col, col, pl.BlockSpec((FFN_CONV_W, tc), lambda j: (0, j)), pl.BlockSpec((1, tc), lambda j: (0, j))],
        out_specs=col,
        out_shape=jax.ShapeDtypeStruct((S, D_FF), MXU_DTYPE),
        scratch_shapes=[pltpu.VMEM((S + PAD, tc), F32)],
        semantics=("parallel",),
        operands=(up, gpre, w, b),
        side=side,
    )


def _ffn_act_bwd(dfin, up, gpre, w, b, side=None):
    tc = FFN_TC
    width = FFN_CONV_W

    def body(df_ref, up_ref, x_ref, w_ref, b_ref, dup_ref, dx_ref, dw_ref, db_ref, xpad_ref, dpad_ref):
        xpad_ref[pl.ds(0, PAD), :] = jnp.zeros((PAD, tc), F32)
        xpad_ref[pl.ds(PAD, S), :] = x_ref[...]
        dpad_ref[pl.ds(S, PAD), :] = jnp.zeros((PAD, tc), F32)
        wv = w_ref[...]
        bv = b_ref[...]

        def gate_grad(ci, acc):
            r0 = pl.multiple_of(ci * CHUNK, CHUNK)
            taps = _past_taps(xpad_ref, r0, width)
            gate = bv + taps[0] * wv[0:1, :] + taps[1] * wv[1:2, :] + taps[2] * wv[2:3, :]
            ge, dge = _gelu(gate)
            df = df_ref[pl.ds(r0, CHUNK), :]
            dup_ref[pl.ds(r0, CHUNK), :] = (df * ge).astype(dup_ref.dtype)
            d = df * up_ref[pl.ds(r0, CHUNK), :] * dge
            dpad_ref[pl.ds(r0, CHUNK), :] = d
            return tuple(acc[k] + _colsum(taps[k] * d) for k in range(width)) + (acc[width] + _colsum(d),)

        zero = jnp.zeros((1, tc), F32)
        acc = lax.fori_loop(0, S // CHUNK, gate_grad, (zero,) * (width + 1))
        for k in range(width):
            dw_ref[k:k + 1, :] = acc[k]
        db_ref[...] = acc[width]

        def input_grad(ci, carry):
            r0 = pl.multiple_of(ci * CHUNK, CHUNK)
            ahead = _future_taps(dpad_ref, r0, width)
            dx = ahead[0] * wv[2:3, :] + ahead[1] * wv[1:2, :] + ahead[2] * wv[0:1, :]
            dx_ref[pl.ds(r0, CHUNK), :] = dx.astype(dx_ref.dtype)
            return carry

        lax.fori_loop(0, S // CHUNK, input_grad, 0)

    col = pl.BlockSpec((S, tc), lambda j: (0, j))
    w_spec = pl.BlockSpec((width, tc), lambda j: (0, j))
    vec = pl.BlockSpec((1, tc), lambda j: (0, j))
    return _call(
        body,
        name="ffn_act_bwd",
        grid=(D_FF // tc,),
        in_specs=[col, col, col, w_spec, vec],
        out_specs=[col, col, w_spec, vec],
        out_shape=[jax.ShapeDtypeStruct((S, D_FF), MXU_DTYPE)] * 2
        + [jax.ShapeDtypeStruct((width, D_FF), F32), jax.ShapeDtypeStruct((1, D_FF), F32)],
        scratch_shapes=[pltpu.VMEM((S + PAD, tc), F32), pltpu.VMEM((S + PAD, tc), F32)],
        semantics=("parallel",),
        operands=(dfin, up, gpre, w, b),
        side=side,
    )


def _adamw_update(w, g, m, v):
    m = ADAM_B1 * m + (1.0 - ADAM_B1) * g
    v = ADAM_B2 * v + (1.0 - ADAM_B2) * (g * g)
    m_hat = m / (1.0 - ADAM_B1 ** ADAM_STEP)
    v_hat = v / (1.0 - ADAM_B2 ** ADAM_STEP)
    delta = -ADAM_LR * (m_hat / (jnp.sqrt(v_hat) + ADAM_EPS) + ADAM_WD * w)
    return delta, m, v


def _add_pairs(send, pair, far_index, *, name):
    _, r_dim, c_dim = send.shape
    tr = r_dim // 4

    def body(far_ref, mine_ref, theirs_ref, o_ref):
        o_ref[...] = (mine_ref[...].astype(F32) + theirs_ref[...].astype(F32)).astype(o_ref.dtype)

    return pl.pallas_call(
        body,
        name=name,
        grid_spec=pltpu.PrefetchScalarGridSpec(
            num_scalar_prefetch=1,
            grid=(3, r_dim // tr),
            in_specs=[pl.BlockSpec((None, tr, c_dim), lambda j, i, far: (far[j], i, 0)),
                      pl.BlockSpec((None, tr, c_dim), lambda j, i, far: (1 + j, i, 0))],
            out_specs=pl.BlockSpec((None, tr, c_dim), lambda j, i, far: (j, i, 0)),
        ),
        out_shape=jax.ShapeDtypeStruct((3, r_dim, c_dim), BF16),
        compiler_params=_cparams("parallel", "parallel"),
    )(far_index, send, pair)


def _reduce_adamw(w, m, v, g_own, pair, far, me, *, tr, name, part=0, earlier=None):
    _, r_dim, c_dim = w.shape
    cp = pair.shape[2]

    def body(me_ref, w_ref, m_ref, v_ref, g_ref, pair_ref, far_ref, *refs):
        grad_ref, delta_ref, nm_ref, nv_ref = refs[-4:]
        g = g_ref[...] + pair_ref[...].astype(F32)
        for j in range(3):
            g = g + far_ref[j].astype(F32)
        delta, nm, nv = _adamw_update(w_ref[...], g, m_ref[...], v_ref[...])
        grad_ref[...] = g
        delta_ref[...] = delta
        nm_ref[...] = nm
        nv_ref[...] = nv

    tile = pl.BlockSpec((None, tr, cp), lambda i, me: (0, i, part))
    if g_own.ndim == 3:
        own_spec = pl.BlockSpec((None, tr, cp), lambda i, me: (me[0], i, 0))
    else:
        own_spec = pl.BlockSpec((tr, cp), lambda i, me: (i, 0))
    earlier = list(earlier or ())
    return pl.pallas_call(
        body,
        name=name,
        grid_spec=pltpu.PrefetchScalarGridSpec(
            num_scalar_prefetch=1,
            grid=(r_dim // tr,),
            in_specs=[tile, tile, tile, own_spec, pl.BlockSpec((None, tr, cp), lambda i, me: (0, i, 0)),
                      pl.BlockSpec((3, tr, cp), lambda i, me: (0, i, 0))]
            + [pl.BlockSpec(memory_space=pl.ANY)] * len(earlier),
            out_specs=[tile] * 4,
        ),
        out_shape=[jax.ShapeDtypeStruct((1, r_dim, c_dim), F32)] * 4,
        input_output_aliases={7 + k: k for k in range(len(earlier))},
        compiler_params=_cparams("parallel"),
    )(me, w, m, v, g_own, pair, far, *earlier)


def _adamw_many(ws, ms, vs, gs):
    n = len(ws)

    def body(*refs):
        for i in range(n):
            delta, nm, nv = _adamw_update(refs[i][...], refs[3 * n + i][...], refs[n + i][...], refs[2 * n + i][...])
            refs[4 * n + i][...] = delta
            refs[5 * n + i][...] = nm
            refs[6 * n + i][...] = nv

    vmem = pl.BlockSpec(memory_space=pltpu.VMEM)
    res = pl.pallas_call(
        body,
        name="adamw_small",
        in_specs=[vmem] * (4 * n),
        out_specs=[vmem] * (3 * n),
        out_shape=[jax.ShapeDtypeStruct(w.shape, F32) for w in ws] * 3,
        compiler_params=pltpu.CompilerParams(vmem_limit_bytes=VMEM_LIMIT),
    )(*ws, *ms, *vs, *gs)
    return res[:n], res[n:2 * n], res[2 * n:]


def _adamw_blocks(w, m, v, g, *, name, side=None):
    per = 2

    def body(w_ref, m_ref, v_ref, g_ref, delta_ref, nm_ref, nv_ref):
        delta, nm, nv = _adamw_update(w_ref[...], g_ref[...], m_ref[...], v_ref[...])
        delta_ref[...] = delta
        nm_ref[...] = nm
        nv_ref[...] = nv

    tile = pl.BlockSpec((1, per) + w.shape[2:], lambda i: (0, i, 0, 0))
    return _call(
        body,
        name=name,
        grid=(w.shape[1] // per,),
        in_specs=[tile] * 4,
        out_specs=[tile] * 3,
        out_shape=[jax.ShapeDtypeStruct(w.shape, F32)] * 3,
        semantics=("parallel",),
        operands=(w, m, v, g),
        side=side,
    )


def _coords():
    return lax.axis_index("x"), lax.axis_index("y"), lax.axis_index("c")


def _flip(coord, bit):
    return 1 - coord if bit else coord


def _relative(k):
    x, y, c = _coords()
    return _flip(x, k & 4), _flip(y, k & 2), _flip(c, k & 1)


def _index(pos):
    return 4 * pos[0] + 2 * pos[1] + pos[2]


FAR = (4, 2, 6)
AG_US_PER_MB = 38.0
RS_US_PER_MB = 46.0
MIN_RIDE_US = 30.0
PAIR_EXCHANGE_US = 20.0
MIN_GATHER_RIDE_US = 22.0
ROW_ALIGN = 32


def _chunks(items, cursor, us, us_per_mb, through=None):
    budget = float("inf") if us is None else us / us_per_mb * 2 ** 20
    names = list(items)
    if through is not None:
        names = names[:names.index(through) + 1]
    chunks = []
    for name in names:
        arr = items[name]
        r_dim, c_dim = arr.shape[-2:]
        row_bytes = c_dim * arr.dtype.itemsize
        while cursor[name] < r_dim and budget > 0:
            rows = r_dim - cursor[name]
            if r_dim > ROW_ALIGN and budget < rows * row_bytes:
                rows = min(rows, max(ROW_ALIGN, int(budget // row_bytes) // ROW_ALIGN * ROW_ALIGN))
            chunks.append((name, cursor[name], rows))
            cursor[name] += rows
            budget -= rows * row_bytes
    return chunks


class _Gather:
    def __init__(self, shards):
        self.shards, self.bufs, self.cursor = {}, {}, {}
        self.add_shards(shards)

    def add_shards(self, shards):
        for n, shard in shards.items():
            self.shards[n], self.bufs[n], self.cursor[n] = shard, None, 0

    def take(self, us=None, through=None):
        if us is not None and us < MIN_GATHER_RIDE_US:
            return None
        chunks = _chunks(self.shards, self.cursor, us, AG_US_PER_MB, through)
        return _GatherSide(self, chunks) if chunks else None

    def get(self, name):
        chunks = _chunks(self.shards, self.cursor, None, AG_US_PER_MB, through=name)
        if chunks:
            _run_side(_GatherSide(self, chunks), "gather_" + name)
        return self.bufs[name]


class _GatherSide:
    SEMS = 8

    def __init__(self, owner, chunks):
        self.owner, self.chunks = owner, chunks
        self.names = list(dict.fromkeys(n for n, _, _ in chunks))
        old = [n for n in self.names if owner.bufs[n] is not None]
        self.operands = [owner.shards[n] for n in self.names] + [owner.bufs[n] for n in old]
        self.out_shape = [jax.ShapeDtypeStruct((N_DEV,) + owner.shards[n].shape, owner.shards[n].dtype)
                          for n in self.names]
        self.aliases = {len(self.names) + i: self.names.index(n) for i, n in enumerate(old)}
        self.sems = [pltpu.SemaphoreType.DMA((self.SEMS * len(chunks),)),
                     pltpu.SemaphoreType.DMA((self.SEMS * len(chunks),)), pltpu.SemaphoreType.DMA((len(chunks),))]

    def _halves(self, ci):
        _, r0, rows = self.chunks[ci]
        if rows % ROW_ALIGN:
            return None
        return (r0, rows // 2), (r0 + rows // 2, rows // 2)

    def _copy(self, ins, outs, sems, ci, s, block, to, rows=None, from_shard=False):
        name, r0, n = self.chunks[ci]
        if rows is not None:
            r0, n = rows
        w = self.names.index(name)
        slot = outs[w].at[_index(block), pl.ds(r0, n)]
        return pltpu.make_async_remote_copy(
            src_ref=ins[w].at[pl.ds(r0, n)] if from_shard else slot, dst_ref=slot,
            send_sem=sems[0].at[self.SEMS * ci + s], recv_sem=sems[1].at[self.SEMS * ci + s],
            device_id=to, device_id_type=MESH)

    def _own(self, ins, outs, sems, ci):
        name, r0, rows = self.chunks[ci]
        w = self.names.index(name)
        return pltpu.make_async_copy(ins[w].at[pl.ds(r0, rows)], outs[w].at[_index(_relative(0)), pl.ds(r0, rows)],
                                     sems[2].at[ci])

    def _pass(self, ins, outs, sems, ci, which):
        source, target = ((4, 2), (2, 4))[which]
        return self._copy(ins, outs, sems, ci, 3 + which, _relative(source), _relative(target),
                          rows=self._halves(ci)[which])

    def start(self, ins, outs, sems):
        me = _relative(0)
        for ci in range(len(self.chunks)):
            self._own(ins, outs, sems, ci).start()
        for ci in range(len(self.chunks)):
            self._copy(ins, outs, sems, ci, 1, me, _relative(4), from_shard=True).start()
            self._copy(ins, outs, sems, ci, 2, me, _relative(2), from_shard=True).start()
            if self._halves(ci) is None:
                self._copy(ins, outs, sems, ci, 3, me, _relative(6), from_shard=True).start()
        for ci in range(len(self.chunks)):
            self._copy(ins, outs, sems, ci, 0, me, _relative(1), from_shard=True).start()

    def mid(self, ins, outs, sems):
        me = _relative(0)
        cut = [ci for ci in range(len(self.chunks)) if self._halves(ci) is not None]
        for ci in cut:
            self._copy(ins, outs, sems, ci, 1, _relative(4), me).wait_recv()
            self._pass(ins, outs, sems, ci, 0).start()
            self._copy(ins, outs, sems, ci, 5, _relative(4), _relative(1)).start()
        for ci in cut:
            self._copy(ins, outs, sems, ci, 2, _relative(2), me).wait_recv()
            self._pass(ins, outs, sems, ci, 1).start()
            self._copy(ins, outs, sems, ci, 6, _relative(2), _relative(1)).start()

    def finish(self, ins, outs, sems):
        me, sibling = _relative(0), _relative(1)
        n = len(self.chunks)
        for ci in range(n):
            if self._halves(ci) is None:
                for s, k in ((1, 4), (2, 2), (3, 6)):
                    self._copy(ins, outs, sems, ci, s, _relative(k), me).wait_recv()
                for j, k in enumerate(FAR):
                    self._copy(ins, outs, sems, ci, 5 + j, _relative(k), sibling).start()
            else:
                h0, h1 = self._halves(ci)
                self._copy(ins, outs, sems, ci, 3, _relative(6), me, rows=h0).wait_recv()
                self._copy(ins, outs, sems, ci, 4, _relative(6), me, rows=h1).wait_recv()
                self._copy(ins, outs, sems, ci, 7, _relative(6), sibling).start()
        for ci in range(n):
            self._copy(ins, outs, sems, ci, 0, sibling, me).wait_recv()
            for j, k in enumerate(FAR):
                self._copy(ins, outs, sems, ci, 5 + j, _relative(k | 1), me).wait_recv()
        for ci in range(n):
            self._copy(ins, outs, sems, ci, 0, me, sibling, from_shard=True).wait_send()
            self._copy(ins, outs, sems, ci, 1, me, _relative(4), from_shard=True).wait_send()
            self._copy(ins, outs, sems, ci, 2, me, _relative(2), from_shard=True).wait_send()
            if self._halves(ci) is None:
                self._copy(ins, outs, sems, ci, 3, me, _relative(6), from_shard=True).wait_send()
            else:
                self._pass(ins, outs, sems, ci, 0).wait_send()
                self._pass(ins, outs, sems, ci, 1).wait_send()
            for j, k in enumerate(FAR):
                self._copy(ins, outs, sems, ci, 5 + j, _relative(k), sibling).wait_send()
            self._own(ins, outs, sems, ci).wait()

    def done(self, results):
        for n, buf in zip(self.names, results):
            self.owner.bufs[n] = buf


class _Scatter:
    def __init__(self, me, far_index):
        self.me, self.far_index = me, far_index
        self.sends, self.owns, self.pairs, self.sums, self.fars = {}, {}, {}, {}, {}
        self.pair_cursor, self.far_cursor = {}, {}

    def add(self, name, send, own):
        self.sends[name] = send
        self.owns[name] = own
        self.pairs[name] = self.fars[name] = None
        self.pair_cursor[name] = 0

    def _rows(self, name):
        return self.sends[name].shape[1]

    def _add_ready_pairs(self):
        for name in self.sends:
            if name not in self.sums and self.pair_cursor[name] == self._rows(name):
                self.sums[name] = _add_pairs(self.sends[name], self.pairs[name], self.far_index, name="pair_" + name)
                self.far_cursor[name] = 0

    def _side(self, us, through=None):
        self._add_ready_pairs()
        names = list(self.sends)
        if through is not None:
            names = names[:names.index(through) + 1]
        pair_chunks = [(n, self.pair_cursor[n], self._rows(n) - self.pair_cursor[n]) for n in names
                       if self.pair_cursor[n] < self._rows(n)]
        for n, _, _ in pair_chunks:
            self.pair_cursor[n] = self._rows(n)
        far_chunks = _chunks(self.sums, self.far_cursor, us, RS_US_PER_MB,
                             through if through in self.sums else None) if self.sums else []
        return _ScatterSide(self, pair_chunks, far_chunks) if pair_chunks or far_chunks else None

    def add_blocks(self, name, blocks32, blocks16):
        self.add(name, blocks16, blocks32)

    def take(self, us):
        return self._side(us) if us >= MIN_RIDE_US else None

    def flush_pairs(self, name):
        side = self._side(PAIR_EXCHANGE_US)
        if side is not None:
            _run_side(side, name)
        self._add_ready_pairs()

    def get(self, name):
        step = 0
        while name not in self.sums or self.far_cursor[name] < self._rows(name):
            _run_side(self._side(None, through=name), "scatter_%s_%d" % (name, step))
            step += 1
        return self.owns[name], self.pairs[name], self.fars[name]


class _ScatterSide:
    TO_SIBLING = (1, 5, 3, 7)

    def __init__(self, owner, pair_chunks, far_chunks):
        self.owner, self.pair_chunks, self.far_chunks = owner, pair_chunks, far_chunks
        self.pair_names = list(dict.fromkeys(n for n, _, _ in pair_chunks))
        self.far_names = list(dict.fromkeys(n for n, _, _ in far_chunks))
        ins = [(owner.sends[n], owner.pairs[n], (4,)) for n in self.pair_names]
        ins += [(owner.sums[n], owner.fars[n], (3,)) for n in self.far_names]
        old = [i for i, (_, buf, _) in enumerate(ins) if buf is not None]
        self.operands = [src for src, _, _ in ins] + [ins[i][1] for i in old]
        self.out_shape = [jax.ShapeDtypeStruct(slots + src.shape[1:], BF16) for src, _, slots in ins]
        self.aliases = {len(ins) + j: i for j, i in enumerate(old)}
        n_pair, n_far = 4 * len(pair_chunks), 3 * len(far_chunks)
        self.sems = [pltpu.SemaphoreType.DMA((max(n_pair, 1),)), pltpu.SemaphoreType.DMA((max(n_pair, 1),)),
                     pltpu.SemaphoreType.DMA((max(n_far, 1),)), pltpu.SemaphoreType.DMA((max(n_far, 1),))]

    def _copies(self, ins, outs, sems):
        copies = []
        for ci, (name, r0, rows) in enumerate(self.pair_chunks):
            w = self.pair_names.index(name)
            for j, k in enumerate(self.TO_SIBLING):
                copies.append(pltpu.make_async_remote_copy(
                    src_ref=ins[w].at[_index(_relative(k)), pl.ds(r0, rows)], dst_ref=outs[w].at[j, pl.ds(r0, rows)],
                    send_sem=sems[0].at[4 * ci + j], recv_sem=sems[1].at[4 * ci + j],
                    device_id=_relative(1), device_id_type=MESH))
        for ci, (name, r0, rows) in enumerate(self.far_chunks):
            w = len(self.pair_names) + self.far_names.index(name)
            for j, k in enumerate(FAR):
                copies.append(pltpu.make_async_remote_copy(
                    src_ref=ins[w].at[j, pl.ds(r0, rows)], dst_ref=outs[w].at[j, pl.ds(r0, rows)],
                    send_sem=sems[2].at[3 * ci + j], recv_sem=sems[3].at[3 * ci + j],
                    device_id=_relative(k), device_id_type=MESH))
        return copies

    def start(self, ins, outs, sems):
        for cp in self._copies(ins, outs, sems):
            cp.start()

    def mid(self, ins, outs, sems):
        pass

    def finish(self, ins, outs, sems):
        for cp in self._copies(ins, outs, sems):
            cp.wait()

    def done(self, results):
        for n, buf in zip(self.pair_names, results):
            self.owner.pairs[n] = buf
        for n, buf in zip(self.far_names, results[len(self.pair_names):]):
            self.owner.fars[n] = buf


class _Joined:
    def __init__(self, sides):
        self.sides = sides
        self.operands, self.out_shape, self.sems, self.aliases, self.spans = [], [], [], {}, []
        for s in sides:
            i0, o0, s0 = len(self.operands), len(self.out_shape), len(self.sems)
            self.operands += list(s.operands)
            self.out_shape += list(s.out_shape)
            self.sems += list(s.sems)
            self.aliases.update({i0 + i: o0 + o for i, o in s.aliases.items()})
            self.spans.append((slice(i0, len(self.operands)), slice(o0, len(self.out_shape)),
                               slice(s0, len(self.sems))))

    def start(self, ins, outs, sems):
        for s, (i, o, m) in zip(self.sides, self.spans):
            s.start(ins[i], outs[o], sems[m])

    def mid(self, ins, outs, sems):
        for s, (i, o, m) in zip(self.sides, self.spans):
            s.mid(ins[i], outs[o], sems[m])

    def finish(self, ins, outs, sems):
        for s, (i, o, m) in zip(self.sides, self.spans):
            s.finish(ins[i], outs[o], sems[m])

    def done(self, results):
        for s, (_, o, _) in zip(self.sides, self.spans):
            s.done(results[o])


def _join(*sides):
    sides = [s for s in sides if s is not None]
    if len(sides) <= 1:
        return sides[0] if sides else None
    return _Joined(sides)


PART_W = 768


def _pack_rows(vecs):
    rows = -(-sum(v.shape[0] for v in vecs) // 8) * 8

    def body(*refs):
        out = refs[-1]
        out[...] = jnp.zeros_like(out)
        r0 = 0
        for v in refs[:-1]:
            k, n = v.shape
            for p in range(-(-n // PART_W)):
                w = min(PART_W, n - PART_W * p)
                out[p, r0:r0 + k, 0:w] = v[:, PART_W * p:PART_W * p + w]
            r0 += k

    vmem = pl.BlockSpec(memory_space=pltpu.VMEM)
    return pl.pallas_call(body, name="pack_small", in_specs=[vmem] * len(vecs), out_specs=vmem,
                          out_shape=jax.ShapeDtypeStruct((N_DEV, rows, PART_W), F32))(*vecs)


def _unpack_rows(packed, shapes):
    def body(packed_ref, *outs):
        r0 = 0
        for o in outs:
            k, n = o.shape
            for p in range(-(-n // PART_W)):
                w = min(PART_W, n - PART_W * p)
                o[:, PART_W * p:PART_W * p + w] = packed_ref[p, r0:r0 + k, 0:w]
            r0 += k

    vmem = pl.BlockSpec(memory_space=pltpu.VMEM)
    return pl.pallas_call(body, name="unpack_small", in_specs=[vmem], out_specs=[vmem] * len(shapes),
                          out_shape=[jax.ShapeDtypeStruct(s, F32) for s in shapes])(packed)


class _PartsToOwners:
    def __init__(self, arrays):
        self.n = len(arrays)
        self.pers = [a.shape[0] // N_DEV for a in arrays]
        self.operands, self.aliases = list(arrays), {}
        self.out_shape = [jax.ShapeDtypeStruct((N_DEV, per) + a.shape[1:], a.dtype) for a, per in zip(arrays, self.pers)]
        self.sems = [pltpu.SemaphoreType.DMA((self.n * (N_DEV - 1),))] * 2

    def _copies(self, ins, outs, sems):
        return [pltpu.make_async_remote_copy(
            src_ref=ins[j].at[pl.ds(self.pers[j] * _index(_relative(k)), self.pers[j])], dst_ref=outs[j].at[k],
            send_sem=sems[0].at[self.n * (k - 1) + j], recv_sem=sems[1].at[self.n * (k - 1) + j],
            device_id=_relative(k), device_id_type=MESH) for k in range(1, N_DEV) for j in range(self.n)]

    def start(self, ins, outs, sems):
        for cp in self._copies(ins, outs, sems):
            cp.start()

    def mid(self, ins, outs, sems):
        pass

    def finish(self, ins, outs, sems):
        for cp in self._copies(ins, outs, sems):
            cp.wait()

    def done(self, results):
        self.stages = list(results)


def _sum_parts(arrays, stages, *, name):
    n = len(arrays)
    pers = [a.shape[0] // N_DEV for a in arrays]

    def body(*refs):
        me = _index(_relative(0))
        for j in range(n):
            acc = refs[j][pl.ds(pers[j] * me, pers[j])]
            for k in range(1, N_DEV):
                acc = acc + refs[n + j][k].astype(F32)
            refs[2 * n + j][...] = acc

    vmem = pl.BlockSpec(memory_space=pltpu.VMEM)
    return pl.pallas_call(body, name=name, in_specs=[vmem] * (2 * n), out_specs=[vmem] * n,
                          out_shape=[jax.ShapeDtypeStruct((per,) + a.shape[1:], F32) for a, per in zip(arrays, pers)],
                          compiler_params=pltpu.CompilerParams(vmem_limit_bytes=VMEM_LIMIT))(*arrays, *stages)


class _PartsToAll:
    def __init__(self, parts):
        self.n = len(parts)
        self.pers = [p.shape[0] for p in parts]
        self.operands, self.aliases = list(parts), {}
        self.out_shape = [jax.ShapeDtypeStruct((N_DEV * p.shape[0],) + p.shape[1:], F32) for p in parts]
        self.sems = [pltpu.SemaphoreType.DMA((self.n * (N_DEV - 1),))] * 2 + [pltpu.SemaphoreType.DMA((self.n,))]

    def _rows(self, outs, j, pos):
        return outs[j].at[pl.ds(self.pers[j] * _index(pos), self.pers[j])]

    def _copy(self, ins, outs, sems, k, j, owner):
        return pltpu.make_async_remote_copy(
            src_ref=ins[j], dst_ref=self._rows(outs, j, owner),
            send_sem=sems[0].at[self.n * (k - 1) + j], recv_sem=sems[1].at[self.n * (k - 1) + j],
            device_id=_relative(k), device_id_type=MESH)

    def _own(self, ins, outs, sems, j):
        return pltpu.make_async_copy(ins[j], self._rows(outs, j, _relative(0)), sems[2].at[j])

    def start(self, ins, outs, sems):
        for j in range(self.n):
            self._own(ins, outs, sems, j).start()
            for k in range(1, N_DEV):
                self._copy(ins, outs, sems, k, j, _relative(0)).start()

    def mid(self, ins, outs, sems):
        pass

    def finish(self, ins, outs, sems):
        for j in range(self.n):
            for k in range(1, N_DEV):
                self._copy(ins, outs, sems, k, j, _relative(k)).wait_recv()
                self._copy(ins, outs, sems, k, j, _relative(0)).wait_send()
            self._own(ins, outs, sems, j).wait()

    def done(self, results):
        self.totals = list(results)


class _AllReduce:
    def __init__(self, name):
        self.name = name

    def begin(self, arrays, wire_dtype=F32):
        self.own = list(arrays)
        self.to_owners = _PartsToOwners([a.astype(wire_dtype) for a in self.own])
        return self.to_owners

    def middle(self):
        self.to_all = _PartsToAll(_sum_parts(self.own, self.to_owners.stages, name="sum_" + self.name))
        return self.to_all

    def end(self):
        return self.to_all.totals


class _SmallSync:
    def __init__(self, vec_names, mat_names):
        self.vec_names, self.mat_names = vec_names, mat_names
        self.mats, self.vecs = _AllReduce("small_mats"), _AllReduce("small_vecs")

    def begin_mats(self, grads):
        return self.mats.begin([_diag_blocks(grads[n]) for n in self.mat_names], BF16)

    def middle_mats(self):
        return self.mats.middle()

    def begin(self, loss, grads):
        vecs = [loss] + [grads[n] for n in self.vec_names]
        self.shapes = [v.shape for v in vecs]
        return self.vecs.begin([_pack_rows(vecs)])

    def middle(self):
        return self.vecs.middle()

    def end(self):
        packed, = self.vecs.end()
        sums = _unpack_rows(packed, self.shapes)
        return sums[0], dict(zip(self.vec_names, sums[1:])), dict(zip(self.mat_names, self.mats.end()))


def _block_diag(w):
    groups = []
    for g in range(N_RNN_GROUPS):
        placed = [jnp.pad(w[4 * g + b], ((RNN_BLOCK_W * b, RNN_BLOCK_W * (3 - b)),) * 2) for b in range(4)]
        groups.append(placed[0] + placed[1] + placed[2] + placed[3])
    return jnp.stack(groups)


def _diag_blocks(wg):
    blocks = []
    for n in range(4 * N_RNN_GROUPS):
        g, at = n // 4, RNN_BLOCK_W * (n % 4)
        blocks.append(wg[g, at:at + RNN_BLOCK_W, at:at + RNN_BLOCK_W])
    return jnp.stack(blocks)


def _heads_major(t, n_heads):
    return t.reshape(S, n_heads, HEAD_DIM).transpose(1, 0, 2)


def _heads_minor(t):
    return t.transpose(1, 0, 2).reshape(S, t.shape[0] * HEAD_DIM)


def _natural(gathered, how):
    n, r, c = gathered.shape
    if how == "rows":
        return gathered.reshape(n * r, c)
    return gathered.transpose(1, 0, 2).reshape(r, n * c)


def _blocks(full, how):
    if how == "rows":
        return full.reshape(N_DEV, full.shape[0] // N_DEV, full.shape[1])
    return full.reshape(full.shape[0], N_DEV, full.shape[1] // N_DEV).transpose(1, 0, 2)


def _cast_many(arrays, side=None):
    steps = 4

    def body(*refs):
        n = len(refs) // 2
        for src, dst in zip(refs[:n], refs[n:]):
            dst[...] = src[...].astype(dst.dtype)

    specs = [pl.BlockSpec((a.shape[0] // steps, a.shape[1]), lambda i: (i, 0)) for a in arrays]
    return _call(
        body,
        name="cast_weights",
        grid=(steps,),
        in_specs=specs,
        out_specs=specs,
        out_shape=[jax.ShapeDtypeStruct(a.shape, MXU_DTYPE) for a in arrays],
        semantics=("parallel",),
        operands=tuple(arrays),
        side=side,
    )


def _forward_backward(x2, xb, target, small, gather, scatter, sync):
    w_in_t = _natural(gather.get("w_in"), "rows")
    proj, projb = _mm(xb, w_in_t, tb=True, tm=S, tn=512, tk=D, out_dtype=(F32, MXU_DTYPE), name="proj",
                      side=gather.take(110))

    qt = projb[:, :OFF_K].T.reshape(N_KV, GROUP, HEAD_DIM, S)
    k2, v2 = projb[:, OFF_K:OFF_V], projb[:, OFF_V:OFF_RX]
    kp = jnp.pad(_heads_major(k2, N_KV), ((0, 0), (BLOCK, 0), (0, 0)))
    vp = jnp.pad(_heads_major(v2, N_KV), ((0, 0), (BLOCK, 0), (0, 0)))
    kt = jnp.pad(k2.T.reshape(N_KV, HEAD_DIM, S), ((0, 0), (0, 0), (BLOCK, 0)))
    vt = jnp.pad(v2.T.reshape(N_KV, HEAD_DIM, S), ((0, 0), (0, 0), (BLOCK, 0)))
    sink_row = jnp.repeat(small["attn_sinks"].reshape(N_KV, 1, GROUP), BLOCK, axis=2)
    ot = _attn_fwd(qt, kp, vt, sink_row, side=gather.take(36)).reshape(D, S)

    rconv_w = _natural(gather.get("rnn_conv_w"), "cols")
    rxc = _conv_fwd(proj, OFF_RX, rconv_w, small["rnn_conv_b"], tc=512, name="rnn_conv_fwd", side=gather.take(18))
    r, i = _lru_gates(rxc, small["lru_wa"], small["lru_wi"], small["lru_ba"], small["lru_bi"], side=gather.take(33))
    h, yrin = _lru_scan_fwd(r, i, rxc, proj, small["lru_lambda"], side=gather.take(53))

    w_ap = _natural(gather.get("w_attn_proj"), "rows")
    w_rp = _natural(gather.get("w_rnn_proj"), "rows")
    y_attn = _mm(ot, w_ap, ta=True, tm=1024, tn=1024, tk=D, name="attn_proj", side=gather.take(22))
    y_rnn = _mm(yrin, w_rp, tm=1024, tn=1024, tk=D_RNN, name="rnn_proj", side=gather.take(27))
    mixin = _gate_fwd(y_attn, y_rnn, proj, small["b_gate"], side=gather.take(25))
    w_out = _natural(gather.get("w_out"), "rows")
    mix = _mm(mixin, w_out, tm=1024, tn=1024, tk=D, name="mix_out", side=gather.take(22))
    x1, x1b, xhat1, rstd1 = _ln_fwd(x2, mix, small["ln1_g"], small["ln1_b"], side=gather.take(23))

    w_up = gather.get("ffn_w_up")
    up = _mm(x1b, w_up, tm=1024, tn=768, tk=D, b_block=768, name="ffn_up", side=gather.take(58))
    w_gate = gather.get("ffn_w_gate")
    gpre = _mm(x1b, w_gate, tm=1024, tn=768, tk=D, b_block=768, name="ffn_gate", side=gather.take(58))
    fconv_w = _natural(gather.get("ffn_conv_w"), "cols")
    fin = _ffn_act_fwd(up, gpre, fconv_w, small["ffn_conv_b"], side=gather.take())
    w_down = _natural(gather.get("ffn_w_down"), "rows")
    f = _mm(fin, w_down, tm=1024, tn=1024, tk=2048, name="ffn_down")
    loss, dpre2, dpre2b, d_ln2_g, d_ln2_b = _ln_loss_bwd(x1, f, small["ln2_g"], small["ln2_b"], target)

    grads = {"ln2_g": d_ln2_g, "ln2_b": d_ln2_b}
    both = (F32, BF16)
    g32, g16 = _mm(fin, dpre2b, ta=True, tm=1024, tn=1024, tk=S, out_dtype=both, name="d_ffn_w_down")
    scatter.add_blocks("ffn_w_down", _blocks(g32, "rows"), _blocks(g16, "rows"))
    dfin = _mm(dpre2b, w_down, tb=True, tm=1024, tn=1024, tk=D, name="d_fin", side=scatter.take(57))
    dup, dgpre, grads["ffn_conv_w"], grads["ffn_conv_b"] = _ffn_act_bwd(
        dfin, up, gpre, fconv_w, small["ffn_conv_b"], side=scatter.take(85))
    g32, g16 = _mm(x1b, dup, ta=True, tm=1024, tn=768, tk=S, out_dtype=both, out_block=768, name="d_ffn_w_up",
                   side=scatter.take(57))
    scatter.add_blocks("ffn_w_up", g32, g16)
    g32, g16 = _mm(x1b, dgpre, ta=True, tm=1024, tn=768, tk=S, out_dtype=both, out_block=768, name="d_ffn_w_gate",
                   side=scatter.take(56))
    scatter.add_blocks("ffn_w_gate", g32, g16)
    dx1 = _mm(dup, w_up, tb=True, tm=1024, tn=1024, tk=768, b_block=768, name="d_x1_up", side=scatter.take(68))
    dx1 = _mm(dgpre, w_gate, tb=True, tm=1024, tn=1024, tk=768, b_block=768, add=dx1, name="d_x1_gate",
              side=scatter.take(70))
    dpre1, dpre1b, grads["ln1_g"], grads["ln1_b"] = _ln_bwd(dx1, dpre2, xhat1, rstd1, small["ln1_g"],
                                                            side=scatter.take(24))

    g32, g16 = _mm(mixin, dpre1b, ta=True, tm=1024, tn=1024, tk=S, out_dtype=both, name="d_w_out",
                   side=scatter.take(26))
    scatter.add_blocks("w_out", _blocks(g32, "rows"), _blocks(g16, "rows"))
    dmix = _mm(dpre1b, w_out, tb=True, tm=1024, tn=1024, tk=D, name="d_mixin", side=scatter.take(22))
    dproj, dya, dyr, dgl_r, db_a, db_r = _gate_bwd(
        dmix, y_attn, y_rnn, proj, small["b_gate"], side=scatter.take(36),
        window=(jax.ShapeDtypeStruct((S, D_IN), MXU_DTYPE), OFF_GA))
    grads["b_gate"] = jnp.concatenate([db_a, db_r], axis=1)
    g32, g16 = _mm(ot, dya, tm=1024, tn=1024, tk=S, out_dtype=both, name="d_w_attn_proj", side=scatter.take(38))
    scatter.add_blocks("w_attn_proj", _blocks(g32, "rows"), _blocks(g16, "rows"))
    g32, g16 = _mm(yrin, dyr, ta=True, tm=1280, tn=1024, tk=S, out_dtype=both, name="d_w_rnn_proj",
                   side=scatter.take(27))
    scatter.add_blocks("w_rnn_proj", _blocks(g32, "rows"), _blocks(g16, "rows"))
    dot_ = _mm(w_ap, dya, tb=True, tm=1024, tn=1024, tk=D, out_dtype=MXU_DTYPE, name="d_o", side=scatter.take(22))
    dyrin = _mm(dyr, w_rp, tb=True, tm=1024, tn=1280, tk=D, name="d_yrin", side=scatter.take(27))

    dproj, dzr, dzi, drxc_in, grads["lru_ba"], grads["lru_bi"], grads["lru_lambda"] = _lru_scan_bwd(
        dyrin, proj, h, r, i, rxc, small["lru_lambda"], side=scatter.take(94), window=(dproj, OFF_RY))
    grads["lru_wa"], grads["lru_wi"] = _lru_gate_wgrad(rxc, dzr, dzi, side=scatter.take(22))
    drxc = _lru_gate_xgrad(dzr, dzi, small["lru_wa"], small["lru_wi"], drxc_in, side=scatter.take(33))
    dproj, grads["rnn_conv_w"], grads["rnn_conv_b"] = _conv_bwd(
        drxc, proj, OFF_RX, rconv_w, tc=512, name="rnn_conv_bwd", side=scatter.take(29), window=(dproj, OFF_RX))

    dqt, dk, dv, dsink = _attn_bwd(qt, kp, kt, vp, sink_row, dot_.reshape(N_KV, GROUP, HEAD_DIM, S),
                                   side=_join(scatter.take(65), sync.begin_mats(grads)))
    grads["attn_sinks"] = dsink.reshape(1, N_KV * GROUP)
    for col0, piece in ((0, dqt.reshape(D, S).T), (OFF_K, _heads_minor(dk[:, BLOCK:, :]).astype(MXU_DTYPE)),
                        (OFF_V, _heads_minor(dv[:, BLOCK:, :]).astype(MXU_DTYPE)), (OFF_GR, dgl_r)):
        dproj = lax.dynamic_update_slice(dproj, piece, (0, col0))
    for part in range(W_IN_PARTS):
        cols = slice(part * (D // W_IN_PARTS), (part + 1) * (D // W_IN_PARTS))
        if part == 0:
            side = _join(scatter.take(55), sync.middle_mats(), sync.begin(loss, grads))
        else:
            side = scatter.take(68)
        g32, g16 = _mm(dproj, xb[:, cols], ta=True, tm=512, tn=D // W_IN_PARTS, tk=S, out_dtype=both,
                       name="d_w_in_%d" % part, side=side)
        scatter.add_blocks("w_in_%d" % part, _blocks(g32, "rows"), _blocks(g16, "rows"))
        scatter.flush_pairs("pairs_w_in_%d" % part)
    dx = _mm(dproj, w_in_t, tm=1024, tn=1024, tk=512, add=dpre1, add_scale=ALPHA, name="d_x",
             side=_join(scatter.take(400), sync.middle()))
    return dx


SHARDED = (
    ("w_in", "cols", 368), ("w_attn_proj", "rows", 32), ("w_rnn_proj", "rows", 32), ("w_out", "rows", 32),
    ("ffn_w_up", "cols", 128), ("ffn_w_gate", "cols", 128), ("ffn_w_down", "rows", 64),
)
SMALL_REPLICATED = ("b_gate", "rnn_conv_b", "lru_wa", "lru_ba", "lru_wi", "lru_bi", "lru_lambda", "attn_sinks",
                    "ln1_g", "ln1_b", "ffn_conv_b", "ln2_g", "ln2_b")
SMALL_SHARDED = ("rnn_conv_w", "ffn_conv_w")
SMALL_MATS = ("lru_wa", "lru_wi")
W_IN_PARTS = 2
WEIGHTS = ("w_in", "b_gate", "rnn_conv_w", "rnn_conv_b", "lru_wa", "lru_ba", "lru_wi", "lru_bi", "lru_lambda",
           "attn_sinks", "w_attn_proj", "w_rnn_proj", "w_out", "ln1_g", "ln1_b", "ffn_w_up", "ffn_w_gate",
           "ffn_conv_w", "ffn_conv_b", "ffn_w_down", "ln2_g", "ln2_b")


def kernel(x, w_in, b_gate, rnn_conv_w, rnn_conv_b, lru_wa, lru_ba, lru_wi, lru_bi, lru_lambda, attn_sinks, w_attn_proj, w_rnn_proj, w_out, ln1_g, ln1_b, ffn_w_up, ffn_w_gate, ffn_conv_w, ffn_conv_b, ffn_w_down, ln2_g, ln2_b, loss_target, m_w_in, m_b_gate, m_rnn_conv_w, m_rnn_conv_b, m_lru_wa, m_lru_ba, m_lru_wi, m_lru_bi, m_lru_lambda, m_attn_sinks, m_w_attn_proj, m_w_rnn_proj, m_w_out, m_ln1_g, m_ln1_b, m_ffn_w_up, m_ffn_w_gate, m_ffn_conv_w, m_ffn_conv_b, m_ffn_w_down, m_ln2_g, m_ln2_b, v_w_in, v_b_gate, v_rnn_conv_w, v_rnn_conv_b, v_lru_wa, v_lru_ba, v_lru_wi, v_lru_bi, v_lru_lambda, v_attn_sinks, v_w_attn_proj, v_w_rnn_proj, v_w_out, v_ln1_g, v_ln1_b, v_ffn_w_up, v_ffn_w_gate, v_ffn_conv_w, v_ffn_conv_b, v_ffn_w_down, v_ln2_g, v_ln2_b):
    given = dict(locals())
    wsh = {n: given[n][0] for n in WEIGHTS}
    msh = {n: given["m_" + n][0] for n in WEIGHTS}
    vsh = {n: given["v_" + n][0] for n in WEIGHTS}
    m_given = {n: given["m_" + n] for n in WEIGHTS}
    v_given = {n: given["v_" + n] for n in WEIGHTS}
    me = 4 * lax.axis_index("x") + 2 * lax.axis_index("y") + lax.axis_index("c")

    order = ("w_in", "rnn_conv_w", "ffn_conv_w", "w_attn_proj", "w_rnn_proj", "w_out", "ffn_w_up", "ffn_w_gate",
             "ffn_w_down")
    gather = _Gather({"w_in": wsh["w_in"].T.astype(MXU_DTYPE), **{n: wsh[n] for n in order[1:3]}})
    *casts, xb = _cast_many([wsh[n] for n in order[3:]] + [x[0]], side=gather.take(through="ffn_conv_w"))
    gather.add_shards(dict(zip(order[3:], casts)))
    small = {n: given[n] for n in SMALL_REPLICATED}
    small["lru_wa"] = _block_diag(wsh["lru_wa"])
    small["lru_wi"] = _block_diag(wsh["lru_wi"])
    scatter = _Scatter(me, jnp.stack([_index(_relative(k)) for k in FAR]).astype(jnp.int32))

    vec_names = tuple(n for n in SMALL_REPLICATED if n not in SMALL_MATS) + SMALL_SHARDED
    sync = _SmallSync(vec_names, SMALL_MATS)
    dx = _forward_backward(x[0], xb, loss_target[0], small, gather, scatter, sync)

    loss_total, g_small, mat_sums = sync.end()
    loss_total = loss_total.reshape(())
    for n in SMALL_SHARDED:
        width = wsh[n].shape[1]
        g_small[n] = lax.dynamic_slice_in_dim(g_small[n], me * width, width, axis=1)
    g_small = {n: g_small[n].reshape(given[n].shape) for n in vec_names}
    out = {}
    results = _adamw_many(*[[d[n] for n in vec_names] for d in (given, m_given, v_given, g_small)])
    for n, delta, nm, nv in zip(vec_names, *results):
        out[n] = (g_small[n], delta, nm, nv)
    for n in SMALL_MATS:
        g = mat_sums[n].reshape(given[n].shape)
        out[n] = (g, *_adamw_blocks(given[n], m_given[n], v_given[n], g, name="adamw_" + n))

    tile_rows = {n: tr for n, _, tr in SHARDED}
    me1 = me.reshape(1).astype(jnp.int32)
    res = None
    for n in list(scatter.sends):
        own, pair, far = scatter.get(n)
        if n.startswith("w_in_"):
            part = int(n[len("w_in_"):])
            w_t, m_t, v_t = (a["w_in"].transpose(0, 2, 1) for a in (given, m_given, v_given))
            res = _reduce_adamw(w_t, m_t, v_t, own, pair, far, me1, tr=tile_rows["w_in"], name="adamw_" + n,
                                part=part, earlier=res if part else None)
            out["w_in"] = tuple(r.transpose(0, 2, 1) for r in res)
        else:
            out[n] = tuple(_reduce_adamw(given[n], m_given[n], v_given[n], own, pair, far, me1, tr=tile_rows[n],
                                         name="adamw_" + n))

    outputs = [loss_total, dx[None]]
    for kind in range(4):
        outputs += [out[n][kind] for n in WEIGHTS]
    return tuple(outputs)
```

```python
import math

import jax
import jax.numpy as jnp
from jax import lax
from jax.experimental import pallas as pl
from jax.experimental.pallas import tpu as pltpu

F32 = jnp.float32
BF16 = jnp.bfloat16
MXU_DTYPE = jnp.bfloat16

N_DEV = 8
S = 2048
D = 2048
HEAD_DIM = 64
N_KV = 4
GROUP = 8
BLOCK = 128
D_KV = N_KV * HEAD_DIM
D_RNN = 2560
RNN_GROUP = 640
N_RNN_GROUPS = D_RNN // RNN_GROUP
RNN_BLOCK_W = 160
RNN_CONV_W = 4
LRU_C = 8.0
D_FF = 6144
FFN_CONV_W = 3
D_IN = 11776
OFF_K = 2048
OFF_V = 2304
OFF_RX = 2560
OFF_RY = 5120
OFF_GA = 7680
OFF_GR = 9728
LN_EPS = 1e-5
ALPHA = 2.0 ** 0.25
ADAM_LR = 0.001
ADAM_B1 = 0.9
ADAM_B2 = 0.999
ADAM_EPS = 1e-08
ADAM_WD = 0.01
ADAM_STEP = 10
NEG = -1e30
VMEM_LIMIT = 56 * 1024 * 1024
MID_RIDE_TENTHS = 6
MESH = pl.DeviceIdType.MESH
GELU_C = math.sqrt(2.0 / math.pi)


def _cparams(*sem):
    return pltpu.CompilerParams(dimension_semantics=sem or None, vmem_limit_bytes=VMEM_LIMIT)


def _call(body, *, name, grid, in_specs, out_specs, out_shape, operands, semantics, scratch_shapes=(), side=None,
          window=None):
    single = not isinstance(out_shape, (list, tuple))
    out_shape = [out_shape] if single else list(out_shape)
    out_specs = [out_specs] if single else list(out_specs)
    in_specs = list(in_specs)
    operands = tuple(operands)
    scratch_shapes = list(scratch_shapes)
    hbm = pl.BlockSpec(memory_space=pltpu.HBM)
    aliases = {}
    if window is not None:
        whole, col0 = window
        block, index_map = out_specs[0].block_shape, out_specs[0].index_map
        assert col0 % block[1] == 0 and out_shape[0].dtype == whole.dtype
        out_specs[0] = pl.BlockSpec(block, lambda *g: (index_map(*g)[0], index_map(*g)[1] + col0 // block[1]))
        out_shape[0] = jax.ShapeDtypeStruct(whole.shape, whole.dtype)
        if not isinstance(whole, jax.ShapeDtypeStruct):
            aliases[len(in_specs)] = 0
            in_specs.append(hbm)
            operands += (whole,)
            compute, n_read = body, len(in_specs) - 1

            def body(*refs):
                compute(*refs[:n_read], *refs[n_read + 1:])

    if side is None:
        res = pl.pallas_call(
            body, name=name, grid=grid, in_specs=in_specs, out_specs=out_specs, out_shape=out_shape,
            scratch_shapes=scratch_shapes, input_output_aliases=aliases,
            compiler_params=_cparams(*semantics))(*operands)
        return res[0] if single else res
    n_in, n_out, n_scr = len(in_specs), len(out_shape), len(scratch_shapes)
    s_in, s_out = len(side.operands), len(side.out_shape)
    steps = math.prod(grid)
    mid_step = (steps * MID_RIDE_TENTHS) // 10

    def with_copies(*refs):
        core_in, side_in = refs[:n_in], refs[n_in:n_in + s_in]
        o0 = n_in + s_in
        core_out, side_out = refs[o0:o0 + n_out], refs[o0 + n_out:o0 + n_out + s_out]
        c0 = o0 + n_out + s_out
        core_scr, sems = refs[c0:c0 + n_scr], refs[c0 + n_scr:]
        step = 0
        for d, size in enumerate(grid):
            step = step * size + pl.program_id(d)

        @pl.when(step == 0)
        def _():
            side.start(side_in, side_out, sems)

        body(*core_in, *core_out, *core_scr)

        @pl.when(step == mid_step)
        def _():
            side.mid(side_in, side_out, sems)

        @pl.when(step == steps - 1)
        def _():
            side.finish(side_in, side_out, sems)

    res = pl.pallas_call(
        with_copies, name=name, grid=grid,
        in_specs=in_specs + [hbm] * s_in, out_specs=out_specs + [hbm] * s_out,
        out_shape=out_shape + list(side.out_shape),
        scratch_shapes=scratch_shapes + list(side.sems),
        input_output_aliases={**aliases, **{n_in + i: n_out + o for i, o in side.aliases.items()}},
        compiler_params=_cparams(*(("arbitrary",) * len(grid))))(*operands, *side.operands)
    side.done(res[n_out:])
    return res[0] if single else res[:n_out]


def _run_side(side, name):
    def body(*refs):
        s_in, s_out = len(side.operands), len(side.out_shape)
        side.start(refs[:s_in], refs[s_in:s_in + s_out], refs[s_in + s_out:])
        side.mid(refs[:s_in], refs[s_in:s_in + s_out], refs[s_in + s_out:])
        side.finish(refs[:s_in], refs[s_in:s_in + s_out], refs[s_in + s_out:])

    hbm = pl.BlockSpec(memory_space=pltpu.HBM)
    res = pl.pallas_call(
        body, name=name, in_specs=[hbm] * len(side.operands), out_specs=[hbm] * len(side.out_shape),
        out_shape=list(side.out_shape), scratch_shapes=list(side.sems),
        input_output_aliases=dict(side.aliases))(*side.operands)
    side.done(res)


def _gelu(x):
    x2 = x * x
    t = jnp.tanh(GELU_C * (x + 0.044715 * x * x2))
    g = 0.5 * x * (1.0 + t)
    dg = 0.5 * (1.0 + t) + 0.5 * x * (1.0 - t * t) * (GELU_C * (1.0 + 3.0 * 0.044715 * x2))
    return g, dg


def _sigmoid(x):
    return 1.0 / (1.0 + jnp.exp(-x))


def _softplus(x):
    z = jnp.exp(-jnp.abs(x))
    small = z * (1.0 - z * (0.5 - z * (1.0 / 3.0 - 0.25 * z)))
    return jnp.maximum(x, 0.0) + jnp.where(z < 0.02, small, jnp.log(1.0 + z))


def _one_minus_exp(x):
    series = -x * (1.0 + x * (0.5 + x * (1.0 / 6.0 + x * (1.0 / 24.0))))
    return jnp.where(x > -0.03, series, 1.0 - jnp.exp(x))


def _colsum(v):
    return jnp.sum(v, axis=0, keepdims=True)


def _mm(a, b, *, tm, tn, tk, name, ta=False, tb=False, out_dtype=F32, b_block=None, out_block=None, add=None,
        add_scale=1.0, side=None):
    out_dtypes = out_dtype if isinstance(out_dtype, tuple) else (out_dtype,)
    if ta:
        k_dim, m_dim = a.shape
    else:
        m_dim, k_dim = a.shape
    if b_block is None:
        n_dim = b.shape[0] if tb else b.shape[1]
    else:
        n_dim = b.shape[1] if tb else b.shape[0] * b_block
    assert m_dim % tm == 0 and n_dim % tn == 0 and k_dim % tk == 0, (name, m_dim, n_dim, k_dim)
    nk = k_dim // tk
    dims = (((0 if ta else 1,), (1 if tb else 0,)), ((), ()))
    has_add = add is not None

    def body(*refs):
        a_ref, b_ref = refs[0], refs[1]
        add_ref = refs[2] if has_add else None
        first_out = 3 if has_add else 2
        o_refs = refs[first_out:first_out + len(out_dtypes)]

        def product():
            return lax.dot_general(a_ref[...].astype(MXU_DTYPE), b_ref[...].astype(MXU_DTYPE), dims,
                                   preferred_element_type=F32)

        def finish(acc):
            if has_add:
                acc = acc + add_scale * add_ref[...]
            for o_ref in o_refs:
                o_ref[...] = acc.astype(o_ref.dtype)

        if nk == 1:
            finish(product())
        else:
            acc_ref = refs[-1]
            k = pl.program_id(2)

            @pl.when(k == 0)
            def _():
                acc_ref[...] = jnp.zeros_like(acc_ref)

            acc_ref[...] += product()

            @pl.when(k == nk - 1)
            def _():
                finish(acc_ref[...])

    if ta:
        a_spec = pl.BlockSpec((tk, tm), lambda i, j, k: (k, i))
    else:
        a_spec = pl.BlockSpec((tm, tk), lambda i, j, k: (i, k))
    if b_block is None:
        if tb:
            b_spec = pl.BlockSpec((tn, tk), lambda i, j, k: (j, k))
        else:
            b_spec = pl.BlockSpec((tk, tn), lambda i, j, k: (k, j))
    elif tb:
        assert b_block % tk == 0
        b_spec = pl.BlockSpec((None, tn, tk), lambda i, j, k: ((k * tk) // b_block, j, ((k * tk) % b_block) // tk))
    else:
        assert b_block % tn == 0
        b_spec = pl.BlockSpec((None, tk, tn), lambda i, j, k: ((j * tn) // b_block, k, ((j * tn) % b_block) // tn))
    in_specs = [a_spec, b_spec]
    operands = [a, b]
    if has_add:
        in_specs.append(pl.BlockSpec((tm, tn), lambda i, j, k: (i, j)))
        operands.append(add)
    if out_block is None:
        out_spec = pl.BlockSpec((tm, tn), lambda i, j, k: (i, j))
        out_dims = (m_dim, n_dim)
    else:
        assert out_block % tn == 0
        out_spec = pl.BlockSpec((None, tm, tn), lambda i, j, k: ((j * tn) // out_block, i, ((j * tn) % out_block) // tn))
        out_dims = (n_dim // out_block, m_dim, out_block)
    res = _call(
        body,
        name=name,
        grid=(m_dim // tm, n_dim // tn, nk),
        in_specs=in_specs,
        out_specs=[out_spec] * len(out_dtypes),
        out_shape=[jax.ShapeDtypeStruct(out_dims, dt) for dt in out_dtypes],
        scratch_shapes=[pltpu.VMEM((tm, tn), F32)] if nk > 1 else [],
        semantics=("parallel", "parallel", "arbitrary"),
        operands=tuple(operands),
        side=side,
    )
    return res if isinstance(out_dtype, tuple) else res[0]


def _attn_bias(bias_ref, h):
    key = lax.broadcasted_iota(jnp.int32, (2 * BLOCK, GROUP * BLOCK), 0)
    col = lax.broadcasted_iota(jnp.int32, (2 * BLOCK, GROUP * BLOCK), 1)
    dist = BLOCK + (col & (BLOCK - 1)) - key
    head = h * GROUP + (col >> 7) + 1
    slope = jnp.exp(head.astype(F32) * (-0.25 * math.log(2.0)))
    bias = jnp.where((dist >= 0) & (dist < BLOCK), -slope * dist.astype(F32), NEG)
    bias_ref[1] = bias
    bias_ref[0] = jnp.where(key < BLOCK, NEG, bias)


def _attn_probs(kb, qt, bias, sink):
    s = jnp.dot(kb, qt, preferred_element_type=F32) * (HEAD_DIM ** -0.5) + bias
    m = jnp.maximum(jnp.max(s, axis=0, keepdims=True), sink)
    e = jnp.exp(s - m)
    e_sink = jnp.exp(sink - m)
    inv = 1.0 / (jnp.sum(e, axis=0, keepdims=True) + e_sink)
    return e * inv, e_sink * inv


def _heads_on_lanes(ref, r0):
    return jnp.concatenate([ref[g, :, pl.ds(r0, BLOCK)] for g in range(GROUP)], axis=1)


def _attn_fwd(qt, kp, vt, sink_row, side=None):
    cols = GROUP * BLOCK

    def body(q_ref, k_ref, vt_ref, sink_ref, o_ref, bias_ref):
        _attn_bias(bias_ref, pl.program_id(0))
        sink = sink_ref[...]

        def step(n, carry):
            r0 = pl.multiple_of(n * BLOCK, BLOCK)
            p, _ = _attn_probs(k_ref[pl.ds(r0, 2 * BLOCK), :], _heads_on_lanes(q_ref, r0),
                               bias_ref[jnp.minimum(n, 1)], sink)
            o = jnp.dot(vt_ref[:, pl.ds(r0, 2 * BLOCK)], p.astype(MXU_DTYPE), preferred_element_type=F32)
            for g in range(GROUP):
                o_ref[g, :, pl.ds(r0, BLOCK)] = o[:, g * BLOCK:(g + 1) * BLOCK].astype(o_ref.dtype)
            return carry

        lax.fori_loop(0, S // BLOCK, step, 0)

    hm = pl.BlockSpec((None, GROUP, HEAD_DIM, S), lambda h: (h, 0, 0, 0))
    return _call(
        body,
        name="attn_fwd",
        grid=(N_KV,),
        in_specs=[
            hm,
            pl.BlockSpec((None, BLOCK + S, HEAD_DIM), lambda h: (h, 0, 0)),
            pl.BlockSpec((None, HEAD_DIM, BLOCK + S), lambda h: (h, 0, 0)),
            pl.BlockSpec((None, 1, cols), lambda h: (h, 0, 0)),
        ],
        out_specs=hm,
        out_shape=jax.ShapeDtypeStruct((N_KV, GROUP, HEAD_DIM, S), MXU_DTYPE),
        scratch_shapes=[pltpu.VMEM((2, 2 * BLOCK, cols), F32)],
        semantics=("parallel",),
        operands=(qt, kp, vt, sink_row),
        side=side,
    )


def _attn_bwd(qt, kp, kt, vp, sink_row, dot_, side=None):
    cols = GROUP * BLOCK

    def body(q_ref, k_ref, kt_ref, v_ref, sink_ref, do_ref, dq_ref, dk_ref, dv_ref, dsink_ref, bias_ref):
        _attn_bias(bias_ref, pl.program_id(0))
        sink = sink_ref[...]
        dk_ref[...] = jnp.zeros_like(dk_ref)
        dv_ref[...] = jnp.zeros_like(dv_ref)
        nt = (((1,), (1,)), ((), ()))

        def step(n, sink_acc):
            r0 = pl.multiple_of(n * BLOCK, BLOCK)
            band = pl.ds(r0, 2 * BLOCK)
            qn = _heads_on_lanes(q_ref, r0)
            don = _heads_on_lanes(do_ref, r0)
            p, p_sink = _attn_probs(k_ref[band, :], qn, bias_ref[jnp.minimum(n, 1)], sink)
            dp = jnp.dot(v_ref[band, :], don, preferred_element_type=F32)
            delta = jnp.sum(p * dp, axis=0, keepdims=True)
            ds = (p * (dp - delta) * (HEAD_DIM ** -0.5)).astype(MXU_DTYPE)
            dq = jnp.dot(kt_ref[:, band], ds, preferred_element_type=F32)
            for g in range(GROUP):
                dq_ref[g, :, pl.ds(r0, BLOCK)] = dq[:, g * BLOCK:(g + 1) * BLOCK].astype(dq_ref.dtype)
            dk_ref[band, :] += lax.dot_general(ds, qn, nt, preferred_element_type=F32)
            dv_ref[band, :] += lax.dot_general(p.astype(MXU_DTYPE), don, nt, preferred_element_type=F32)
            return sink_acc - p_sink * delta

        sink_acc = lax.fori_loop(0, S // BLOCK, step, jnp.zeros((1, cols), F32))
        for g in range(GROUP):
            dsink_ref[g:g + 1, :] = jnp.sum(sink_acc[:, g * BLOCK:(g + 1) * BLOCK], axis=1, keepdims=True)

    hm = pl.BlockSpec((None, GROUP, HEAD_DIM, S), lambda h: (h, 0, 0, 0))
    kv = pl.BlockSpec((None, BLOCK + S, HEAD_DIM), lambda h: (h, 0, 0))
    return _call(
        body,
        name="attn_bwd",
        grid=(N_KV,),
        in_specs=[hm, kv, pl.BlockSpec((None, HEAD_DIM, BLOCK + S), lambda h: (h, 0, 0)), kv,
                  pl.BlockSpec((None, 1, cols), lambda h: (h, 0, 0)), hm],
        out_specs=[hm, kv, kv, pl.BlockSpec((None, GROUP, 1), lambda h: (h, 0, 0))],
        out_shape=[
            jax.ShapeDtypeStruct((N_KV, GROUP, HEAD_DIM, S), MXU_DTYPE),
            jax.ShapeDtypeStruct((N_KV, BLOCK + S, HEAD_DIM), F32),
            jax.ShapeDtypeStruct((N_KV, BLOCK + S, HEAD_DIM), F32),
            jax.ShapeDtypeStruct((N_KV, GROUP, 1), F32),
        ],
        scratch_shapes=[pltpu.VMEM((2, 2 * BLOCK, cols), F32)],
        semantics=("parallel",),
        operands=(qt, kp, kt, vp, sink_row, dot_),
        side=side,
    )


PAD = 8
CHUNK = 256


def _past_taps(xpad_ref, r0, width):
    ext = xpad_ref[pl.ds(r0, CHUNK + PAD), :]
    taps = []
    for k in range(width):
        back = width - 1 - k
        taps.append((ext if back == 0 else pltpu.roll(ext, back, 0))[PAD:, :])
    return taps


def _future_taps(xpad_ref, r0, width):
    ext = xpad_ref[pl.ds(r0, CHUNK + PAD), :]
    taps = []
    for ahead in range(width):
        taps.append((ext if ahead == 0 else pltpu.roll(ext, CHUNK + PAD - ahead, 0))[:CHUNK, :])
    return taps


def _conv_fwd(src, col0, w, b, *, tc, name, side=None):
    width, c_dim = w.shape

    def body(x_ref, w_ref, b_ref, o_ref, xpad_ref):
        xpad_ref[pl.ds(0, PAD), :] = jnp.zeros((PAD, tc), F32)
        xpad_ref[pl.ds(PAD, S), :] = x_ref[...]
        wv = w_ref[...]
        bv = b_ref[...]

        def step(ci, carry):
            r0 = pl.multiple_of(ci * CHUNK, CHUNK)
            taps = _past_taps(xpad_ref, r0, width)
            y = bv + taps[0] * wv[0:1, :]
            for k in range(1, width):
                y = y + taps[k] * wv[k:k + 1, :]
            o_ref[pl.ds(r0, CHUNK), :] = y
            return carry

        lax.fori_loop(0, S // CHUNK, step, 0)

    return _call(
        body,
        name=name,
        grid=(c_dim // tc,),
        in_specs=[
            pl.BlockSpec((S, tc), lambda j: (0, col0 // tc + j)),
            pl.BlockSpec((width, tc), lambda j: (0, j)),
            pl.BlockSpec((1, tc), lambda j: (0, j)),
        ],
        out_specs=pl.BlockSpec((S, tc), lambda j: (0, j)),
        out_shape=jax.ShapeDtypeStruct((S, c_dim), F32),
        scratch_shapes=[pltpu.VMEM((S + PAD, tc), F32)],
        semantics=("parallel",),
        operands=(src, w, b),
        side=side,
    )


def _conv_bwd(dy, src, col0, w, *, tc, name, side=None, window=None):
    width, c_dim = w.shape

    def body(dy_ref, x_ref, w_ref, dx_ref, dw_ref, db_ref, xpad_ref, dpad_ref):
        xpad_ref[pl.ds(0, PAD), :] = jnp.zeros((PAD, tc), F32)
        xpad_ref[pl.ds(PAD, S), :] = x_ref[...]
        dpad_ref[pl.ds(0, S), :] = dy_ref[...]
        dpad_ref[pl.ds(S, PAD), :] = jnp.zeros((PAD, tc), F32)
        wv = w_ref[...]

        def step(ci, acc):
            r0 = pl.multiple_of(ci * CHUNK, CHUNK)
            past = _past_taps(xpad_ref, r0, width)
            ahead = _future_taps(dpad_ref, r0, width)
            d = ahead[0]
            dx = d * wv[width - 1:width, :]
            for j in range(1, width):
                dx = dx + ahead[j] * wv[width - 1 - j:width - j, :]
            dx_ref[pl.ds(r0, CHUNK), :] = dx.astype(dx_ref.dtype)
            return tuple(acc[k] + _colsum(past[k] * d) for k in range(width)) + (acc[width] + _colsum(d),)

        zero = jnp.zeros((1, tc), F32)
        acc = lax.fori_loop(0, S // CHUNK, step, (zero,) * (width + 1))
        for k in range(width):
            dw_ref[k:k + 1, :] = acc[k]
        db_ref[...] = acc[width]

    return _call(
        body,
        name=name,
        grid=(c_dim // tc,),
        in_specs=[
            pl.BlockSpec((S, tc), lambda j: (0, j)),
            pl.BlockSpec((S, tc), lambda j: (0, col0 // tc + j)),
            pl.BlockSpec((width, tc), lambda j: (0, j)),
        ],
        out_specs=[
            pl.BlockSpec((S, tc), lambda j: (0, j)),
            pl.BlockSpec((width, tc), lambda j: (0, j)),
            pl.BlockSpec((1, tc), lambda j: (0, j)),
        ],
        out_shape=[
            jax.ShapeDtypeStruct((S, c_dim), MXU_DTYPE),
            jax.ShapeDtypeStruct((width, c_dim), F32),
            jax.ShapeDtypeStruct((1, c_dim), F32),
        ],
        scratch_shapes=[pltpu.VMEM((S + PAD, tc), F32), pltpu.VMEM((S + PAD, tc), F32)],
        semantics=("parallel",),
        operands=(dy, src, w),
        side=side,
        window=window,
    )


SCAN_TC = 256


def _lru_gates(rxc, wa, wi, ba, bi, side=None):
    tm = 512

    def body(x_ref, wa_ref, wi_ref, ba_ref, bi_ref, r_ref, i_ref):
        xv = x_ref[...].astype(MXU_DTYPE)
        r_ref[...] = _sigmoid(jnp.dot(xv, wa_ref[...].astype(MXU_DTYPE), preferred_element_type=F32) + ba_ref[...])
        i_ref[...] = _sigmoid(jnp.dot(xv, wi_ref[...].astype(MXU_DTYPE), preferred_element_type=F32) + bi_ref[...])

    x_spec = pl.BlockSpec((tm, RNN_GROUP), lambda g, i: (i, g))
    w_spec = pl.BlockSpec((None, RNN_GROUP, RNN_GROUP), lambda g, i: (g, 0, 0))
    b_spec = pl.BlockSpec((1, RNN_GROUP), lambda g, i: (0, g))
    return _call(
        body,
        name="lru_gates",
        grid=(N_RNN_GROUPS, S // tm),
        in_specs=[x_spec, w_spec, w_spec, b_spec, b_spec],
        out_specs=[x_spec, x_spec],
        out_shape=[jax.ShapeDtypeStruct((S, D_RNN), F32)] * 2,
        semantics=("parallel", "parallel"),
        operands=(rxc, wa, wi, ba, bi),
        side=side,
    )


def _scan_down(a, u, row):
    for d in (1, 2, 4):
        a_s = jnp.where(row >= d, pltpu.roll(a, d, 0), 1.0)
        u_s = jnp.where(row >= d, pltpu.roll(u, d, 0), 0.0)
        u = a * u_s + u
        a = a * a_s
    return a, u


def _scan_up(a, u, row):
    for d in (1, 2, 4):
        a_s = jnp.where(row < 8 - d, pltpu.roll(a, 8 - d, 0), 1.0)
        u_s = jnp.where(row < 8 - d, pltpu.roll(u, 8 - d, 0), 0.0)
        u = a * u_s + u
        a = a * a_s
    return a, u


def _lru_scan_fwd(r, i, rxc, proj, lam, side=None):
    tc = SCAN_TC

    def body(r_ref, i_ref, x_ref, ry_ref, lam_ref, h_ref, y_ref):
        rate = LRU_C * _softplus(-lam_ref[...])
        row = lax.broadcasted_iota(jnp.int32, (8, tc), 0)

        def step(ci, carry):
            r0 = pl.multiple_of(ci * 16, 16)
            log_a = -rate * r_ref[pl.ds(r0, 16), :]
            a16 = jnp.exp(log_a)
            u16 = jnp.sqrt(_one_minus_exp(2.0 * log_a)) * (i_ref[pl.ds(r0, 16), :] * x_ref[pl.ds(r0, 16), :])
            hs = []
            for half in range(2):
                a_cum, h0 = _scan_down(a16[8 * half:8 * half + 8, :], u16[8 * half:8 * half + 8, :], row)
                h = a_cum * carry + h0
                carry = jnp.broadcast_to(h[7:8, :], (8, tc))
                hs.append(h)
            h16 = jnp.concatenate(hs, axis=0)
            h_ref[pl.ds(r0, 16), :] = h16
            y_ref[pl.ds(r0, 16), :] = (h16 * _gelu(ry_ref[pl.ds(r0, 16), :])[0]).astype(y_ref.dtype)
            return carry

        lax.fori_loop(0, S // 16, step, jnp.zeros((8, tc), F32))

    col = pl.BlockSpec((S, tc), lambda j: (0, j))
    return _call(
        body,
        name="lru_scan_fwd",
        grid=(D_RNN // tc,),
        in_specs=[col, col, col, pl.BlockSpec((S, tc), lambda j: (0, OFF_RY // tc + j)),
                  pl.BlockSpec((1, tc), lambda j: (0, j))],
        out_specs=[col, col],
        out_shape=[jax.ShapeDtypeStruct((S, D_RNN), F32), jax.ShapeDtypeStruct((S, D_RNN), MXU_DTYPE)],
        semantics=("parallel",),
        operands=(r, i, rxc, proj, lam),
        side=side,
    )


def _lru_scan_bwd(dy, proj, h, r, i, rxc, lam, side=None, window=None):
    tc = SCAN_TC

    def body(dy_ref, ry_ref, h_ref, r_ref, i_ref, x_ref, lam_ref,
             dry_ref, dzr_ref, dzi_ref, dx_ref, dba_ref, dbi_ref, dlam_ref, a_ref, dh_ref, hp_ref):
        lam_v = lam_ref[...]
        rate = LRU_C * _softplus(-lam_v)
        dlam_scale = LRU_C * _sigmoid(-lam_v)
        row = lax.broadcasted_iota(jnp.int32, (8, tc), 0)
        hp_ref[pl.ds(0, PAD), :] = jnp.zeros((PAD, tc), F32)
        hp_ref[pl.ds(PAD, S), :] = h_ref[...]
        a_ref[pl.ds(S, PAD), :] = jnp.zeros((PAD, tc), F32)

        def prep(ci, carry):
            r0 = pl.multiple_of(ci * CHUNK, CHUNK)
            a_ref[pl.ds(r0, CHUNK), :] = jnp.exp(-rate * r_ref[pl.ds(r0, CHUNK), :])
            ge, dge = _gelu(ry_ref[pl.ds(r0, CHUNK), :])
            dyv = dy_ref[pl.ds(r0, CHUNK), :]
            dh_ref[pl.ds(r0, CHUNK), :] = dyv * ge
            dry_ref[pl.ds(r0, CHUNK), :] = (dyv * h_ref[pl.ds(r0, CHUNK), :] * dge).astype(dry_ref.dtype)
            return carry

        lax.fori_loop(0, S // CHUNK, prep, 0)

        def step(ci, state):
            carry, dba, dbi, dlam = state
            r0 = pl.multiple_of(S - 16 - ci * 16, 16)
            a_ext = a_ref[pl.ds(r0, 24), :]
            a_next = pltpu.roll(a_ext, 23, 0)
            h_prev = pltpu.roll(hp_ref[pl.ds(r0, 24), :], 1, 0)
            dh16 = dh_ref[pl.ds(r0, 16), :]
            gs = [None, None]
            for half in (1, 0):
                lo = 8 * half
                c_cum, g0 = _scan_up(a_next[lo:lo + 8, :], dh16[lo:lo + 8, :], row)
                g = c_cum * carry + g0
                carry = jnp.broadcast_to(g[0:1, :], (8, tc))
                gs[half] = g
            g16 = jnp.concatenate(gs, axis=0)
            a16 = a_ext[0:16, :]
            r16 = r_ref[pl.ds(r0, 16), :]
            i16 = i_ref[pl.ds(r0, 16), :]
            x16 = x_ref[pl.ds(r0, 16), :]
            a2 = a16 * a16
            sq = jnp.sqrt(_one_minus_exp(-2.0 * rate * r16))
            dx_ref[pl.ds(r0, 16), :] = g16 * sq * i16
            dzi = g16 * sq * x16 * i16 * (1.0 - i16)
            dlog_a = g16 * h_prev[8:24, :] * a16 - g16 * i16 * x16 * a2 / sq
            dzr = -rate * dlog_a * r16 * (1.0 - r16)
            dzr_ref[pl.ds(r0, 16), :] = dzr.astype(dzr_ref.dtype)
            dzi_ref[pl.ds(r0, 16), :] = dzi.astype(dzi_ref.dtype)
            return carry, dba + _colsum(dzr), dbi + _colsum(dzi), dlam + _colsum(dlog_a * r16)

        zero = jnp.zeros((1, tc), F32)
        _, dba, dbi, dlam = lax.fori_loop(0, S // 16, step, (jnp.zeros((8, tc), F32), zero, zero, zero))
        dba_ref[...] = dba
        dbi_ref[...] = dbi
        dlam_ref[...] = dlam * dlam_scale

    col = pl.BlockSpec((S, tc), lambda j: (0, j))
    vec = pl.BlockSpec((1, tc), lambda j: (0, j))
    return _call(
        body,
        name="lru_scan_bwd",
        grid=(D_RNN // tc,),
        in_specs=[col, pl.BlockSpec((S, tc), lambda j: (0, OFF_RY // tc + j)), col, col, col, col, vec],
        out_specs=[col, col, col, col, vec, vec, vec],
        out_shape=[jax.ShapeDtypeStruct((S, D_RNN), MXU_DTYPE)] * 3 + [jax.ShapeDtypeStruct((S, D_RNN), F32)]
        + [jax.ShapeDtypeStruct((1, D_RNN), F32)] * 3,
        scratch_shapes=[pltpu.VMEM((S + PAD, tc), F32), pltpu.VMEM((S, tc), F32), pltpu.VMEM((S + PAD, tc), F32)],
        semantics=("parallel",),
        operands=(dy, proj, h, r, i, rxc, lam),
        side=side,
        window=window,
    )


def _lru_gate_wgrad(rxc, dzr, dzi, side=None):
    def body(x_ref, dzr_ref, dzi_ref, dwa_ref, dwi_ref):
        xv = x_ref[...].astype(MXU_DTYPE)
        dims = (((0,), (0,)), ((), ()))
        dwa_ref[...] = lax.dot_general(xv, dzr_ref[...], dims, preferred_element_type=F32)
        dwi_ref[...] = lax.dot_general(xv, dzi_ref[...], dims, preferred_element_type=F32)

    col = pl.BlockSpec((S, RNN_GROUP), lambda g: (0, g))
    w_spec = pl.BlockSpec((None, RNN_GROUP, RNN_GROUP), lambda g: (g, 0, 0))
    return _call(
        body,
        name="lru_gate_wgrad",
        grid=(N_RNN_GROUPS,),
        in_specs=[col, col, col],
        out_specs=[w_spec, w_spec],
        out_shape=[jax.ShapeDtypeStruct((N_RNN_GROUPS, RNN_GROUP, RNN_GROUP), F32)] * 2,
        semantics=("parallel",),
        operands=(rxc, dzr, dzi),
        side=side,
    )


def _lru_gate_xgrad(dzr, dzi, wa, wi, dx_in, side=None):
    tm = 512

    def body(dzr_ref, dzi_ref, wa_ref, wi_ref, dx_ref, o_ref):
        dims = (((1,), (1,)), ((), ()))
        o_ref[...] = (dx_ref[...]
                      + lax.dot_general(dzr_ref[...], wa_ref[...].astype(MXU_DTYPE), dims, preferred_element_type=F32)
                      + lax.dot_general(dzi_ref[...], wi_ref[...].astype(MXU_DTYPE), dims, preferred_element_type=F32))

    x_spec = pl.BlockSpec((tm, RNN_GROUP), lambda g, i: (i, g))
    w_spec = pl.BlockSpec((None, RNN_GROUP, RNN_GROUP), lambda g, i: (g, 0, 0))
    return _call(
        body,
        name="lru_gate_xgrad",
        grid=(N_RNN_GROUPS, S // tm),
        in_specs=[x_spec, x_spec, w_spec, w_spec, x_spec],
        out_specs=x_spec,
        out_shape=jax.ShapeDtypeStruct((S, D_RNN), F32),
        semantics=("parallel", "parallel"),
        operands=(dzr, dzi, wa, wi, dx_in),
        side=side,
    )


def _gate_fwd(y_attn, y_rnn, proj, b_gate, side=None):
    t = 512

    def body(ya_ref, yr_ref, ga_ref, gr_ref, ba_ref, br_ref, o_ref):
        o_ref[...] = (_sigmoid(ga_ref[...] + ba_ref[...]) * ya_ref[...]
                      + _sigmoid(gr_ref[...] + br_ref[...]) * yr_ref[...]).astype(o_ref.dtype)

    tile = pl.BlockSpec((t, t), lambda i, j: (i, j))
    return _call(
        body,
        name="gate_fwd",
        grid=(S // t, D // t),
        in_specs=[tile, tile,
                  pl.BlockSpec((t, t), lambda i, j: (i, OFF_GA // t + j)),
                  pl.BlockSpec((t, t), lambda i, j: (i, OFF_GR // t + j)),
                  pl.BlockSpec((1, t), lambda i, j: (0, j)),
                  pl.BlockSpec((1, t), lambda i, j: (0, D // t + j))],
        out_specs=tile,
        out_shape=jax.ShapeDtypeStruct((S, D), MXU_DTYPE),
        semantics=("parallel", "parallel"),
        operands=(y_attn, y_rnn, proj, proj, b_gate, b_gate),
        side=side,
    )


def _gate_bwd(dmix, y_attn, y_rnn, proj, b_gate, side=None, window=None):
    t = 512

    def body(dm_ref, ya_ref, yr_ref, ga_ref, gr_ref, ba_ref, br_ref,
             dga_ref, dya_ref, dyr_ref, dgr_ref, dba_ref, dbr_ref):
        @pl.when(pl.program_id(1) == 0)
        def _():
            dba_ref[...] = jnp.zeros_like(dba_ref)
            dbr_ref[...] = jnp.zeros_like(dbr_ref)

        dm = dm_ref[...]
        ga = _sigmoid(ga_ref[...] + ba_ref[...])
        gr = _sigmoid(gr_ref[...] + br_ref[...])
        dya_ref[...] = (dm * ga).astype(dya_ref.dtype)
        dyr_ref[...] = (dm * gr).astype(dyr_ref.dtype)
        dga = dm * ya_ref[...] * ga * (1.0 - ga)
        dgr = dm * yr_ref[...] * gr * (1.0 - gr)
        dga_ref[...] = dga.astype(dga_ref.dtype)
        dgr_ref[...] = dgr.astype(dgr_ref.dtype)
        dba_ref[...] += _colsum(dga)
        dbr_ref[...] += _colsum(dgr)

    tile = pl.BlockSpec((t, t), lambda j, i: (i, j))
    vec = pl.BlockSpec((1, t), lambda j, i: (0, j))
    return _call(
        body,
        name="gate_bwd",
        grid=(D // t, S // t),
        in_specs=[tile, tile, tile,
                  pl.BlockSpec((t, t), lambda j, i: (i, OFF_GA // t + j)),
                  pl.BlockSpec((t, t), lambda j, i: (i, OFF_GR // t + j)),
                  vec,
                  pl.BlockSpec((1, t), lambda j, i: (0, D // t + j))],
        out_specs=[tile, tile, tile, tile, vec, vec],
        out_shape=[jax.ShapeDtypeStruct((S, D), MXU_DTYPE)] * 4 + [jax.ShapeDtypeStruct((1, D), F32)] * 2,
        semantics=("parallel", "arbitrary"),
        operands=(dmix, y_attn, y_rnn, proj, proj, b_gate, b_gate),
        side=side,
        window=window,
    )


LN_TM = 256


def _ln_stats(pre):
    mu = jnp.mean(pre, axis=-1, keepdims=True)
    xc = pre - mu
    rstd = lax.rsqrt(jnp.mean(xc * xc, axis=-1, keepdims=True) + LN_EPS)
    return xc * rstd, rstd


def _ln_input_grad(dy, xhat, rstd, g):
    dyg = dy * g
    return rstd * (dyg - jnp.mean(dyg, axis=-1, keepdims=True)
                   - xhat * jnp.mean(dyg * xhat, axis=-1, keepdims=True))


def _ln_fwd(res, branch, g, b, side=None):
    def body(res_ref, br_ref, g_ref, b_ref, y_ref, yb_ref, xhat_ref, rstd_ref):
        xhat, rstd = _ln_stats(ALPHA * res_ref[...] + br_ref[...])
        y = xhat * g_ref[...] + b_ref[...]
        y_ref[...] = y
        yb_ref[...] = y.astype(yb_ref.dtype)
        xhat_ref[...] = xhat
        rstd_ref[...] = rstd

    tile = pl.BlockSpec((LN_TM, D), lambda i: (i, 0))
    vec = pl.BlockSpec((1, D), lambda i: (0, 0))
    return _call(
        body,
        name="ln_fwd",
        grid=(S // LN_TM,),
        in_specs=[tile, tile, vec, vec],
        out_specs=[tile, tile, tile, pl.BlockSpec((LN_TM, 1), lambda i: (i, 0))],
        out_shape=[jax.ShapeDtypeStruct((S, D), F32), jax.ShapeDtypeStruct((S, D), MXU_DTYPE),
                   jax.ShapeDtypeStruct((S, D), F32), jax.ShapeDtypeStruct((S, 1), F32)],
        semantics=("parallel",),
        operands=(res, branch, g, b),
        side=side,
    )


def _ln_bwd(dy_a, dy_b, xhat, rstd, g, side=None):
    def body(da_ref, db_in_ref, xhat_ref, rstd_ref, g_ref, dp_ref, dpb_ref, dg_ref, db_ref):
        @pl.when(pl.program_id(0) == 0)
        def _():
            dg_ref[...] = jnp.zeros_like(dg_ref)
            db_ref[...] = jnp.zeros_like(db_ref)

        dy = da_ref[...] + ALPHA * db_in_ref[...]
        xhat = xhat_ref[...]
        dp = _ln_input_grad(dy, xhat, rstd_ref[...], g_ref[...])
        dp_ref[...] = dp
        dpb_ref[...] = dp.astype(dpb_ref.dtype)
        dg_ref[...] += _colsum(dy * xhat)
        db_ref[...] += _colsum(dy)

    tile = pl.BlockSpec((LN_TM, D), lambda i: (i, 0))
    vec = pl.BlockSpec((1, D), lambda i: (0, 0))
    return _call(
        body,
        name="ln_bwd",
        grid=(S // LN_TM,),
        in_specs=[tile, tile, tile, pl.BlockSpec((LN_TM, 1), lambda i: (i, 0)), vec],
        out_specs=[tile, tile, vec, vec],
        out_shape=[jax.ShapeDtypeStruct((S, D), F32), jax.ShapeDtypeStruct((S, D), MXU_DTYPE),
                   jax.ShapeDtypeStruct((1, D), F32), jax.ShapeDtypeStruct((1, D), F32)],
        semantics=("arbitrary",),
        operands=(dy_a, dy_b, xhat, rstd, g),
        side=side,
    )


def _ln_loss_bwd(res, branch, g, b, target, side=None):
    def body(res_ref, br_ref, g_ref, b_ref, t_ref, loss_ref, dp_ref, dpb_ref, dg_ref, db_ref):
        @pl.when(pl.program_id(0) == 0)
        def _():
            loss_ref[...] = jnp.zeros_like(loss_ref)
            dg_ref[...] = jnp.zeros_like(dg_ref)
            db_ref[...] = jnp.zeros_like(db_ref)

        xhat, rstd = _ln_stats(ALPHA * res_ref[...] + br_ref[...])
        gv = g_ref[...]
        err = xhat * gv + b_ref[...] - t_ref[...]
        loss_ref[...] += (0.5 / D) * jnp.sum(_colsum(err * err), axis=1, keepdims=True)
        dy = err * (1.0 / D)
        dp = _ln_input_grad(dy, xhat, rstd, gv)
        dp_ref[...] = dp
        dpb_ref[...] = dp.astype(dpb_ref.dtype)
        dg_ref[...] += _colsum(dy * xhat)
        db_ref[...] += _colsum(dy)

    tile = pl.BlockSpec((LN_TM, D), lambda i: (i, 0))
    vec = pl.BlockSpec((1, D), lambda i: (0, 0))
    return _call(
        body,
        name="ln_loss_bwd",
        grid=(S // LN_TM,),
        in_specs=[tile, tile, vec, vec, tile],
        out_specs=[pl.BlockSpec((1, 1), lambda i: (0, 0)), tile, tile, vec, vec],
        out_shape=[jax.ShapeDtypeStruct((1, 1), F32), jax.ShapeDtypeStruct((S, D), F32),
                   jax.ShapeDtypeStruct((S, D), MXU_DTYPE),
                   jax.ShapeDtypeStruct((1, D), F32), jax.ShapeDtypeStruct((1, D), F32)],
        semantics=("arbitrary",),
        operands=(res, branch, g, b, target),
        side=side,
    )


FFN_TC = 256


def _ffn_act_fwd(up, gpre, w, b, side=None):
    tc = FFN_TC

    def body(up_ref, x_ref, w_ref, b_ref, o_ref, xpad_ref):
        xpad_ref[pl.ds(0, PAD), :] = jnp.zeros((PAD, tc), F32)
        xpad_ref[pl.ds(PAD, S), :] = x_ref[...]
        wv = w_ref[...]
        bv = b_ref[...]

        def step(ci, carry):
            r0 = pl.multiple_of(ci * CHUNK, CHUNK)
            taps = _past_taps(xpad_ref, r0, FFN_CONV_W)
            gate = bv + taps[0] * wv[0:1, :] + taps[1] * wv[1:2, :] + taps[2] * wv[2:3, :]
            o_ref[pl.ds(r0, CHUNK), :] = (_gelu(gate)[0] * up_ref[pl.ds(r0, CHUNK), :]).astype(o_ref.dtype)
            return carry

        lax.fori_loop(0, S // CHUNK, step, 0)

    col = pl.BlockSpec((S, tc), lambda j: (0, j))
    return _call(
        body,
        name="ffn_act_fwd",
        grid=(D_FF // tc,),
        in_specs=[col, col, pl.BlockSpec((FFN_CONV_W, tc), lambda j: (0, j)), pl.BlockSpec((1, tc), lambda j: (0, j))],
        out_specs=col,
        out_shape=jax.ShapeDtypeStruct((S, D_FF), MXU_DTYPE),
        scratch_shapes=[pltpu.VMEM((S + PAD, tc), F32)],
        semantics=("parallel",),
        operands=(up, gpre, w, b),
        side=side,
    )


def _ffn_act_bwd(dfin, up, gpre, w, b, side=None):
    tc = FFN_TC
    width = FFN_CONV_W

    def body(df_ref, up_ref, x_ref, w_ref, b_ref, dup_ref, dx_ref, dw_ref, db_ref, xpad_ref, dpad_ref):
        xpad_ref[pl.ds(0, PAD), :] = jnp.zeros((PAD, tc), F32)
        xpad_ref[pl.ds(PAD, S), :] = x_ref[...]
        dpad_ref[pl.ds(S, PAD), :] = jnp.zeros((PAD, tc), F32)
        wv = w_ref[...]
        bv = b_ref[...]

        def gate_grad(ci, acc):
            r0 = pl.multiple_of(ci * CHUNK, CHUNK)
            taps = _past_taps(xpad_ref, r0, width)
            gate = bv + taps[0] * wv[0:1, :] + taps[1] * wv[1:2, :] + taps[2] * wv[2:3, :]
            ge, dge = _gelu(gate)
            df = df_ref[pl.ds(r0, CHUNK), :]
            dup_ref[pl.ds(r0, CHUNK), :] = (df * ge).astype(dup_ref.dtype)
            d = df * up_ref[pl.ds(r0, CHUNK), :] * dge
            dpad_ref[pl.ds(r0, CHUNK), :] = d
            return tuple(acc[k] + _colsum(taps[k] * d) for k in range(width)) + (acc[width] + _colsum(d),)

        zero = jnp.zeros((1, tc), F32)
        acc = lax.fori_loop(0, S // CHUNK, gate_grad, (zero,) * (width + 1))
        for k in range(width):
            dw_ref[k:k + 1, :] = acc[k]
        db_ref[...] = acc[width]

        def input_grad(ci, carry):
            r0 = pl.multiple_of(ci * CHUNK, CHUNK)
            ahead = _future_taps(dpad_ref, r0, width)
            dx = ahead[0] * wv[2:3, :] + ahead[1] * wv[1:2, :] + ahead[2] * wv[0:1, :]
            dx_ref[pl.ds(r0, CHUNK), :] = dx.astype(dx_ref.dtype)
            return carry

        lax.fori_loop(0, S // CHUNK, input_grad, 0)

    col = pl.BlockSpec((S, tc), lambda j: (0, j))
    w_spec = pl.BlockSpec((width, tc), lambda j: (0, j))
    vec = pl.BlockSpec((1, tc), lambda j: (0, j))
    return _call(
        body,
        name="ffn_act_bwd",
        grid=(D_FF // tc,),
        in_specs=[col, col, col, w_spec, vec],
        out_specs=[col, col, w_spec, vec],
        out_shape=[jax.ShapeDtypeStruct((S, D_FF), MXU_DTYPE)] * 2
        + [jax.ShapeDtypeStruct((width, D_FF), F32), jax.ShapeDtypeStruct((1, D_FF), F32)],
        scratch_shapes=[pltpu.VMEM((S + PAD, tc), F32), pltpu.VMEM((S + PAD, tc), F32)],
        semantics=("parallel",),
        operands=(dfin, up, gpre, w, b),
        side=side,
    )


def _adamw_update(w, g, m, v):
    m = ADAM_B1 * m + (1.0 - ADAM_B1) * g
    v = ADAM_B2 * v + (1.0 - ADAM_B2) * (g * g)
    m_hat = m / (1.0 - ADAM_B1 ** ADAM_STEP)
    v_hat = v / (1.0 - ADAM_B2 ** ADAM_STEP)
    delta = -ADAM_LR * (m_hat / (jnp.sqrt(v_hat) + ADAM_EPS) + ADAM_WD * w)
    return delta, m, v


def _add_pairs(send, pair, far_index, *, name):
    _, r_dim, c_dim = send.shape
    tr = r_dim // 4

    def body(far_ref, mine_ref, theirs_ref, o_ref):
        o_ref[...] = (mine_ref[...].astype(F32) + theirs_ref[...].astype(F32)).astype(o_ref.dtype)

    return pl.pallas_call(
        body,
        name=name,
        grid_spec=pltpu.PrefetchScalarGridSpec(
            num_scalar_prefetch=1,
            grid=(3, r_dim // tr),
            in_specs=[pl.BlockSpec((None, tr, c_dim), lambda j, i, far: (far[j], i, 0)),
                      pl.BlockSpec((None, tr, c_dim), lambda j, i, far: (1 + j, i, 0))],
            out_specs=pl.BlockSpec((None, tr, c_dim), lambda j, i, far: (j, i, 0)),
        ),
        out_shape=jax.ShapeDtypeStruct((3, r_dim, c_dim), BF16),
        compiler_params=_cparams("parallel", "parallel"),
    )(far_index, send, pair)


def _reduce_adamw(w, m, v, g_own, pair, far, me, *, tr, name, part=0, earlier=None):
    _, r_dim, c_dim = w.shape
    cp = pair.shape[2]

    def body(me_ref, w_ref, m_ref, v_ref, g_ref, pair_ref, far_ref, *refs):
        grad_ref, delta_ref, nm_ref, nv_ref = refs[-4:]
        g = g_ref[...] + pair_ref[...].astype(F32)
        for j in range(3):
            g = g + far_ref[j].astype(F32)
        delta, nm, nv = _adamw_update(w_ref[...], g, m_ref[...], v_ref[...])
        grad_ref[...] = g
        delta_ref[...] = delta
        nm_ref[...] = nm
        nv_ref[...] = nv

    tile = pl.BlockSpec((None, tr, cp), lambda i, me: (0, i, part))
    if g_own.ndim == 3:
        own_spec = pl.BlockSpec((None, tr, cp), lambda i, me: (me[0], i, 0))
    else:
        own_spec = pl.BlockSpec((tr, cp), lambda i, me: (i, 0))
    earlier = list(earlier or ())
    return pl.pallas_call(
        body,
        name=name,
        grid_spec=pltpu.PrefetchScalarGridSpec(
            num_scalar_prefetch=1,
            grid=(r_dim // tr,),
            in_specs=[tile, tile, tile, own_spec, pl.BlockSpec((None, tr, cp), lambda i, me: (0, i, 0)),
                      pl.BlockSpec((3, tr, cp), lambda i, me: (0, i, 0))]
            + [pl.BlockSpec(memory_space=pl.ANY)] * len(earlier),
            out_specs=[tile] * 4,
        ),
        out_shape=[jax.ShapeDtypeStruct((1, r_dim, c_dim), F32)] * 4,
        input_output_aliases={7 + k: k for k in range(len(earlier))},
        compiler_params=_cparams("parallel"),
    )(me, w, m, v, g_own, pair, far, *earlier)


def _adamw_many(ws, ms, vs, gs):
    n = len(ws)

    def body(*refs):
        for i in range(n):
            delta, nm, nv = _adamw_update(refs[i][...], refs[3 * n + i][...], refs[n + i][...], refs[2 * n + i][...])
            refs[4 * n + i][...] = delta
            refs[5 * n + i][...] = nm
            refs[6 * n + i][...] = nv

    vmem = pl.BlockSpec(memory_space=pltpu.VMEM)
    res = pl.pallas_call(
        body,
        name="adamw_small",
        in_specs=[vmem] * (4 * n),
        out_specs=[vmem] * (3 * n),
        out_shape=[jax.ShapeDtypeStruct(w.shape, F32) for w in ws] * 3,
        compiler_params=pltpu.CompilerParams(vmem_limit_bytes=VMEM_LIMIT),
    )(*ws, *ms, *vs, *gs)
    return res[:n], res[n:2 * n], res[2 * n:]


def _adamw_blocks(w, m, v, g, *, name, side=None):
    per = 2

    def body(w_ref, m_ref, v_ref, g_ref, delta_ref, nm_ref, nv_ref):
        delta, nm, nv = _adamw_update(w_ref[...], g_ref[...], m_ref[...], v_ref[...])
        delta_ref[...] = delta
        nm_ref[...] = nm
        nv_ref[...] = nv

    tile = pl.BlockSpec((1, per) + w.shape[2:], lambda i: (0, i, 0, 0))
    return _call(
        body,
        name=name,
        grid=(w.shape[1] // per,),
        in_specs=[tile] * 4,
        out_specs=[tile] * 3,
        out_shape=[jax.ShapeDtypeStruct(w.shape, F32)] * 3,
        semantics=("parallel",),
        operands=(w, m, v, g),
        side=side,
    )


def _coords():
    return lax.axis_index("x"), lax.axis_index("y"), lax.axis_index("c")


def _flip(coord, bit):
    return 1 - coord if bit else coord


def _relative(k):
    x, y, c = _coords()
    return _flip(x, k & 4), _flip(y, k & 2), _flip(c, k & 1)


def _index(pos):
    return 4 * pos[0] + 2 * pos[1] + pos[2]


FAR = (4, 2, 6)
AG_US_PER_MB = 38.0
RS_US_PER_MB = 46.0
MIN_RIDE_US = 30.0
PAIR_EXCHANGE_US = 20.0
MIN_GATHER_RIDE_US = 22.0
ROW_ALIGN = 32


def _chunks(items, cursor, us, us_per_mb, through=None):
    budget = float("inf") if us is None else us / us_per_mb * 2 ** 20
    names = list(items)
    if through is not None:
        names = names[:names.index(through) + 1]
    chunks = []
    for name in names:
        arr = items[name]
        r_dim, c_dim = arr.shape[-2:]
        row_bytes = c_dim * arr.dtype.itemsize
        while cursor[name] < r_dim and budget > 0:
            rows = r_dim - cursor[name]
            if r_dim > ROW_ALIGN and budget < rows * row_bytes:
                rows = min(rows, max(ROW_ALIGN, int(budget // row_bytes) // ROW_ALIGN * ROW_ALIGN))
            chunks.append((name, cursor[name], rows))
            cursor[name] += rows
            budget -= rows * row_bytes
    return chunks


class _Gather:
    def __init__(self, shards):
        self.shards, self.bufs, self.cursor = {}, {}, {}
        self.add_shards(shards)

    def add_shards(self, shards):
        for n, shard in shards.items():
            self.shards[n], self.bufs[n], self.cursor[n] = shard, None, 0

    def take(self, us=None, through=None):
        if us is not None and us < MIN_GATHER_RIDE_US:
            return None
        chunks = _chunks(self.shards, self.cursor, us, AG_US_PER_MB, through)
        return _GatherSide(self, chunks) if chunks else None

    def get(self, name):
        chunks = _chunks(self.shards, self.cursor, None, AG_US_PER_MB, through=name)
        if chunks:
            _run_side(_GatherSide(self, chunks), "gather_" + name)
        return self.bufs[name]


class _GatherSide:
    SEMS = 8

    def __init__(self, owner, chunks):
        self.owner, self.chunks = owner, chunks
        self.names = list(dict.fromkeys(n for n, _, _ in chunks))
        old = [n for n in self.names if owner.bufs[n] is not None]
        self.operands = [owner.shards[n] for n in self.names] + [owner.bufs[n] for n in old]
        self.out_shape = [jax.ShapeDtypeStruct((N_DEV,) + owner.shards[n].shape, owner.shards[n].dtype)
                          for n in self.names]
        self.aliases = {len(self.names) + i: self.names.index(n) for i, n in enumerate(old)}
        self.sems = [pltpu.SemaphoreType.DMA((self.SEMS * len(chunks),)),
                     pltpu.SemaphoreType.DMA((self.SEMS * len(chunks),)), pltpu.SemaphoreType.DMA((len(chunks),))]

    def _halves(self, ci):
        _, r0, rows = self.chunks[ci]
        if rows % ROW_ALIGN:
            return None
        return (r0, rows // 2), (r0 + rows // 2, rows // 2)

    def _copy(self, ins, outs, sems, ci, s, block, to, rows=None, from_shard=False):
        name, r0, n = self.chunks[ci]
        if rows is not None:
            r0, n = rows
        w = self.names.index(name)
        slot = outs[w].at[_index(block), pl.ds(r0, n)]
        return pltpu.make_async_remote_copy(
            src_ref=ins[w].at[pl.ds(r0, n)] if from_shard else slot, dst_ref=slot,
            send_sem=sems[0].at[self.SEMS * ci + s], recv_sem=sems[1].at[self.SEMS * ci + s],
            device_id=to, device_id_type=MESH)

    def _own(self, ins, outs, sems, ci):
        name, r0, rows = self.chunks[ci]
        w = self.names.index(name)
        return pltpu.make_async_copy(ins[w].at[pl.ds(r0, rows)], outs[w].at[_index(_relative(0)), pl.ds(r0, rows)],
                                     sems[2].at[ci])

    def _pass(self, ins, outs, sems, ci, which):
        source, target = ((4, 2), (2, 4))[which]
        return self._copy(ins, outs, sems, ci, 3 + which, _relative(source), _relative(target),
                          rows=self._halves(ci)[which])

    def start(self, ins, outs, sems):
        me = _relative(0)
        for ci in range(len(self.chunks)):
            self._own(ins, outs, sems, ci).start()
        for ci in range(len(self.chunks)):
            self._copy(ins, outs, sems, ci, 1, me, _relative(4), from_shard=True).start()
            self._copy(ins, outs, sems, ci, 2, me, _relative(2), from_shard=True).start()
            if self._halves(ci) is None:
                self._copy(ins, outs, sems, ci, 3, me, _relative(6), from_shard=True).start()
        for ci in range(len(self.chunks)):
            self._copy(ins, outs, sems, ci, 0, me, _relative(1), from_shard=True).start()

    def mid(self, ins, outs, sems):
        me = _relative(0)
        cut = [ci for ci in range(len(self.chunks)) if self._halves(ci) is not None]
        for ci in cut:
            self._copy(ins, outs, sems, ci, 1, _relative(4), me).wait_recv()
            self._pass(ins, outs, sems, ci, 0).start()
            self._copy(ins, outs, sems, ci, 5, _relative(4), _relative(1)).start()
        for ci in cut:
            self._copy(ins, outs, sems, ci, 2, _relative(2), me).wait_recv()
            self._pass(ins, outs, sems, ci, 1).start()
            self._copy(ins, outs, sems, ci, 6, _relative(2), _relative(1)).start()

    def finish(self, ins, outs, sems):
        me, sibling = _relative(0), _relative(1)
        n = len(self.chunks)
        for ci in range(n):
            if self._halves(ci) is None:
                for s, k in ((1, 4), (2, 2), (3, 6)):
                    self._copy(ins, outs, sems, ci, s, _relative(k), me).wait_recv()
                for j, k in enumerate(FAR):
                    self._copy(ins, outs, sems, ci, 5 + j, _relative(k), sibling).start()
            else:
                h0, h1 = self._halves(ci)
                self._copy(ins, outs, sems, ci, 3, _relative(6), me, rows=h0).wait_recv()
                self._copy(ins, outs, sems, ci, 4, _relative(6), me, rows=h1).wait_recv()
                self._copy(ins, outs, sems, ci, 7, _relative(6), sibling).start()
        for ci in range(n):
            self._copy(ins, outs, sems, ci, 0, sibling, me).wait_recv()
            for j, k in enumerate(FAR):
                self._copy(ins, outs, sems, ci, 5 + j, _relative(k | 1), me).wait_recv()
        for ci in range(n):
            self._copy(ins, outs, sems, ci, 0, me, sibling, from_shard=True).wait_send()
            self._copy(ins, outs, sems, ci, 1, me, _relative(4), from_shard=True).wait_send()
            self._copy(ins, outs, sems, ci, 2, me, _relative(2), from_shard=True).wait_send()
            if self._halves(ci) is None:
                self._copy(ins, outs, sems, ci, 3, me, _relative(6), from_shard=True).wait_send()
            else:
                self._pass(ins, outs, sems, ci, 0).wait_send()
                self._pass(ins, outs, sems, ci, 1).wait_send()
            for j, k in enumerate(FAR):
                self._copy(ins, outs, sems, ci, 5 + j, _relative(k), sibling).wait_send()
            self._own(ins, outs, sems, ci).wait()

    def done(self, results):
        for n, buf in zip(self.names, results):
            self.owner.bufs[n] = buf


class _Scatter:
    def __init__(self, me, far_index):
        self.me, self.far_index = me, far_index
        self.sends, self.owns, self.pairs, self.sums, self.fars = {}, {}, {}, {}, {}
        self.pair_cursor, self.far_cursor = {}, {}

    def add(self, name, send, own):
        self.sends[name] = send
        self.owns[name] = own
        self.pairs[name] = self.fars[name] = None
        self.pair_cursor[name] = 0

    def _rows(self, name):
        return self.sends[name].shape[1]

    def _add_ready_pairs(self):
        for name in self.sends:
            if name not in self.sums and self.pair_cursor[name] == self._rows(name):
                self.sums[name] = _add_pairs(self.sends[name], self.pairs[name], self.far_index, name="pair_" + name)
                self.far_cursor[name] = 0

    def _side(self, us, through=None):
        self._add_ready_pairs()
        names = list(self.sends)
        if through is not None:
            names = names[:names.index(through) + 1]
        pair_chunks = [(n, self.pair_cursor[n], self._rows(n) - self.pair_cursor[n]) for n in names
                       if self.pair_cursor[n] < self._rows(n)]
        for n, _, _ in pair_chunks:
            self.pair_cursor[n] = self._rows(n)
        far_chunks = _chunks(self.sums, self.far_cursor, us, RS_US_PER_MB,
                             through if through in self.sums else None) if self.sums else []
        return _ScatterSide(self, pair_chunks, far_chunks) if pair_chunks or far_chunks else None

    def add_blocks(self, name, blocks32, blocks16):
        self.add(name, blocks16, blocks32)

    def take(self, us):
        return self._side(us) if us >= MIN_RIDE_US else None

    def flush_pairs(self, name):
        side = self._side(PAIR_EXCHANGE_US)
        if side is not None:
            _run_side(side, name)
        self._add_ready_pairs()

    def get(self, name):
        step = 0
        while name not in self.sums or self.far_cursor[name] < self._rows(name):
            _run_side(self._side(None, through=name), "scatter_%s_%d" % (name, step))
            step += 1
        return self.owns[name], self.pairs[name], self.fars[name]


class _ScatterSide:
    TO_SIBLING = (1, 5, 3, 7)

    def __init__(self, owner, pair_chunks, far_chunks):
        self.owner, self.pair_chunks, self.far_chunks = owner, pair_chunks, far_chunks
        self.pair_names = list(dict.fromkeys(n for n, _, _ in pair_chunks))
        self.far_names = list(dict.fromkeys(n for n, _, _ in far_chunks))
        ins = [(owner.sends[n], owner.pairs[n], (4,)) for n in self.pair_names]
        ins += [(owner.sums[n], owner.fars[n], (3,)) for n in self.far_names]
        old = [i for i, (_, buf, _) in enumerate(ins) if buf is not None]
        self.operands = [src for src, _, _ in ins] + [ins[i][1] for i in old]
        self.out_shape = [jax.ShapeDtypeStruct(slots + src.shape[1:], BF16) for src, _, slots in ins]
        self.aliases = {len(ins) + j: i for j, i in enumerate(old)}
        n_pair, n_far = 4 * len(pair_chunks), 3 * len(far_chunks)
        self.sems = [pltpu.SemaphoreType.DMA((max(n_pair, 1),)), pltpu.SemaphoreType.DMA((max(n_pair, 1),)),
                     pltpu.SemaphoreType.DMA((max(n_far, 1),)), pltpu.SemaphoreType.DMA((max(n_far, 1),))]

    def _copies(self, ins, outs, sems):
        copies = []
        for ci, (name, r0, rows) in enumerate(self.pair_chunks):
            w = self.pair_names.index(name)
            for j, k in enumerate(self.TO_SIBLING):
                copies.append(pltpu.make_async_remote_copy(
                    src_ref=ins[w].at[_index(_relative(k)), pl.ds(r0, rows)], dst_ref=outs[w].at[j, pl.ds(r0, rows)],
                    send_sem=sems[0].at[4 * ci + j], recv_sem=sems[1].at[4 * ci + j],
                    device_id=_relative(1), device_id_type=MESH))
        for ci, (name, r0, rows) in enumerate(self.far_chunks):
            w = len(self.pair_names) + self.far_names.index(name)
            for j, k in enumerate(FAR):
                copies.append(pltpu.make_async_remote_copy(
                    src_ref=ins[w].at[j, pl.ds(r0, rows)], dst_ref=outs[w].at[j, pl.ds(r0, rows)],
                    send_sem=sems[2].at[3 * ci + j], recv_sem=sems[3].at[3 * ci + j],
                    device_id=_relative(k), device_id_type=MESH))
        return copies

    def start(self, ins, outs, sems):
        for cp in self._copies(ins, outs, sems):
            cp.start()

    def mid(self, ins, outs, sems):
        pass

    def finish(self, ins, outs, sems):
        for cp in self._copies(ins, outs, sems):
            cp.wait()

    def done(self, results):
        for n, buf in zip(self.pair_names, results):
            self.owner.pairs[n] = buf
        for n, buf in zip(self.far_names, results[len(self.pair_names):]):
            self.owner.fars[n] = buf


class _Joined:
    def __init__(self, sides):
        self.sides = sides
        self.operands, self.out_shape, self.sems, self.aliases, self.spans = [], [], [], {}, []
        for s in sides:
            i0, o0, s0 = len(self.operands), len(self.out_shape), len(self.sems)
            self.operands += list(s.operands)
            self.out_shape += list(s.out_shape)
            self.sems += list(s.sems)
            self.aliases.update({i0 + i: o0 + o for i, o in s.aliases.items()})
            self.spans.append((slice(i0, len(self.operands)), slice(o0, len(self.out_shape)),
                               slice(s0, len(self.sems))))

    def start(self, ins, outs, sems):
        for s, (i, o, m) in zip(self.sides, self.spans):
            s.start(ins[i], outs[o], sems[m])

    def mid(self, ins, outs, sems):
        for s, (i, o, m) in zip(self.sides, self.spans):
            s.mid(ins[i], outs[o], sems[m])

    def finish(self, ins, outs, sems):
        for s, (i, o, m) in zip(self.sides, self.spans):
            s.finish(ins[i], outs[o], sems[m])

    def done(self, results):
        for s, (_, o, _) in zip(self.sides, self.spans):
            s.done(results[o])


def _join(*sides):
    sides = [s for s in sides if s is not None]
    if len(sides) <= 1:
        return sides[0] if sides else None
    return _Joined(sides)


PART_W = 768


def _pack_rows(vecs):
    rows = -(-sum(v.shape[0] for v in vecs) // 8) * 8

    def body(*refs):
        out = refs[-1]
        out[...] = jnp.zeros_like(out)
        r0 = 0
        for v in refs[:-1]:
            k, n = v.shape
            for p in range(-(-n // PART_W)):
                w = min(PART_W, n - PART_W * p)
                out[p, r0:r0 + k, 0:w] = v[:, PART_W * p:PART_W * p + w]
            r0 += k

    vmem = pl.BlockSpec(memory_space=pltpu.VMEM)
    return pl.pallas_call(body, name="pack_small", in_specs=[vmem] * len(vecs), out_specs=vmem,
                          out_shape=jax.ShapeDtypeStruct((N_DEV, rows, PART_W), F32))(*vecs)


def _unpack_rows(packed, shapes):
    def body(packed_ref, *outs):
        r0 = 0
        for o in outs:
            k, n = o.shape
            for p in range(-(-n // PART_W)):
                w = min(PART_W, n - PART_W * p)
                o[:, PART_W * p:PART_W * p + w] = packed_ref[p, r0:r0 + k, 0:w]
            r0 += k

    vmem = pl.BlockSpec(memory_space=pltpu.VMEM)
    return pl.pallas_call(body, name="unpack_small", in_specs=[vmem], out_specs=[vmem] * len(shapes),
                          out_shape=[jax.ShapeDtypeStruct(s, F32) for s in shapes])(packed)


class _PartsToOwners:
    def __init__(self, arrays):
        self.n = len(arrays)
        self.pers = [a.shape[0] // N_DEV for a in arrays]
        self.operands, self.aliases = list(arrays), {}
        self.out_shape = [jax.ShapeDtypeStruct((N_DEV, per) + a.shape[1:], a.dtype) for a, per in zip(arrays, self.pers)]
        self.sems = [pltpu.SemaphoreType.DMA((self.n * (N_DEV - 1),))] * 2

    def _copies(self, ins, outs, sems):
        return [pltpu.make_async_remote_copy(
            src_ref=ins[j].at[pl.ds(self.pers[j] * _index(_relative(k)), self.pers[j])], dst_ref=outs[j].at[k],
            send_sem=sems[0].at[self.n * (k - 1) + j], recv_sem=sems[1].at[self.n * (k - 1) + j],
            device_id=_relative(k), device_id_type=MESH) for k in range(1, N_DEV) for j in range(self.n)]

    def start(self, ins, outs, sems):
        for cp in self._copies(ins, outs, sems):
            cp.start()

    def mid(self, ins, outs, sems):
        pass

    def finish(self, ins, outs, sems):
        for cp in self._copies(ins, outs, sems):
            cp.wait()

    def done(self, results):
        self.stages = list(results)


def _sum_parts(arrays, stages, *, name):
    n = len(arrays)
    pers = [a.shape[0] // N_DEV for a in arrays]

    def body(*refs):
        me = _index(_relative(0))
        for j in range(n):
            acc = refs[j][pl.ds(pers[j] * me, pers[j])]
            for k in range(1, N_DEV):
                acc = acc + refs[n + j][k].astype(F32)
            refs[2 * n + j][...] = acc

    vmem = pl.BlockSpec(memory_space=pltpu.VMEM)
    return pl.pallas_call(body, name=name, in_specs=[vmem] * (2 * n), out_specs=[vmem] * n,
                          out_shape=[jax.ShapeDtypeStruct((per,) + a.shape[1:], F32) for a, per in zip(arrays, pers)],
                          compiler_params=pltpu.CompilerParams(vmem_limit_bytes=VMEM_LIMIT))(*arrays, *stages)


class _PartsToAll:
    def __init__(self, parts):
        self.n = len(parts)
        self.pers = [p.shape[0] for p in parts]
        self.operands, self.aliases = list(parts), {}
        self.out_shape = [jax.ShapeDtypeStruct((N_DEV * p.shape[0],) + p.shape[1:], F32) for p in parts]
        self.sems = [pltpu.SemaphoreType.DMA((self.n * (N_DEV - 1),))] * 2 + [pltpu.SemaphoreType.DMA((self.n,))]

    def _rows(self, outs, j, pos):
        return outs[j].at[pl.ds(self.pers[j] * _index(pos), self.pers[j])]

    def _copy(self, ins, outs, sems, k, j, owner):
        return pltpu.make_async_remote_copy(
            src_ref=ins[j], dst_ref=self._rows(outs, j, owner),
            send_sem=sems[0].at[self.n * (k - 1) + j], recv_sem=sems[1].at[self.n * (k - 1) + j],
            device_id=_relative(k), device_id_type=MESH)

    def _own(self, ins, outs, sems, j):
        return pltpu.make_async_copy(ins[j], self._rows(outs, j, _relative(0)), sems[2].at[j])

    def start(self, ins, outs, sems):
        for j in range(self.n):
            self._own(ins, outs, sems, j).start()
            for k in range(1, N_DEV):
                self._copy(ins, outs, sems, k, j, _relative(0)).start()

    def mid(self, ins, outs, sems):
        pass

    def finish(self, ins, outs, sems):
        for j in range(self.n):
            for k in range(1, N_DEV):
                self._copy(ins, outs, sems, k, j, _relative(k)).wait_recv()
                self._copy(ins, outs, sems, k, j, _relative(0)).wait_send()
            self._own(ins, outs, sems, j).wait()

    def done(self, results):
        self.totals = list(results)


class _AllReduce:
    def __init__(self, name):
        self.name = name

    def begin(self, arrays, wire_dtype=F32):
        self.own = list(arrays)
        self.to_owners = _PartsToOwners([a.astype(wire_dtype) for a in self.own])
        return self.to_owners

    def middle(self):
        self.to_all = _PartsToAll(_sum_parts(self.own, self.to_owners.stages, name="sum_" + self.name))
        return self.to_all

    def end(self):
        return self.to_all.totals


class _SmallSync:
    def __init__(self, vec_names, mat_names):
        self.vec_names, self.mat_names = vec_names, mat_names
        self.mats, self.vecs = _AllReduce("small_mats"), _AllReduce("small_vecs")

    def begin_mats(self, grads):
        return self.mats.begin([_diag_blocks(grads[n]) for n in self.mat_names], BF16)

    def middle_mats(self):
        return self.mats.middle()

    def begin(self, loss, grads):
        vecs = [loss] + [grads[n] for n in self.vec_names]
        self.shapes = [v.shape for v in vecs]
        return self.vecs.begin([_pack_rows(vecs)])

    def middle(self):
        return self.vecs.middle()

    def end(self):
        packed, = self.vecs.end()
        sums = _unpack_rows(packed, self.shapes)
        return sums[0], dict(zip(self.vec_names, sums[1:])), dict(zip(self.mat_names, self.mats.end()))


def _block_diag(w):
    groups = []
    for g in range(N_RNN_GROUPS):
        placed = [jnp.pad(w[4 * g + b], ((RNN_BLOCK_W * b, RNN_BLOCK_W * (3 - b)),) * 2) for b in range(4)]
        groups.append(placed[0] + placed[1] + placed[2] + placed[3])
    return jnp.stack(groups)


def _diag_blocks(wg):
    blocks = []
    for n in range(4 * N_RNN_GROUPS):
        g, at = n // 4, RNN_BLOCK_W * (n % 4)
        blocks.append(wg[g, at:at + RNN_BLOCK_W, at:at + RNN_BLOCK_W])
    return jnp.stack(blocks)


def _heads_major(t, n_heads):
    return t.reshape(S, n_heads, HEAD_DIM).transpose(1, 0, 2)


def _heads_minor(t):
    return t.transpose(1, 0, 2).reshape(S, t.shape[0] * HEAD_DIM)


def _natural(gathered, how):
    n, r, c = gathered.shape
    if how == "rows":
        return gathered.reshape(n * r, c)
    return gathered.transpose(1, 0, 2).reshape(r, n * c)


def _blocks(full, how):
    if how == "rows":
        return full.reshape(N_DEV, full.shape[0] // N_DEV, full.shape[1])
    return full.reshape(full.shape[0], N_DEV, full.shape[1] // N_DEV).transpose(1, 0, 2)


def _cast_many(arrays, side=None):
    steps = 4

    def body(*refs):
        n = len(refs) // 2
        for src, dst in zip(refs[:n], refs[n:]):
            dst[...] = src[...].astype(dst.dtype)

    specs = [pl.BlockSpec((a.shape[0] // steps, a.shape[1]), lambda i: (i, 0)) for a in arrays]
    return _call(
        body,
        name="cast_weights",
        grid=(steps,),
        in_specs=specs,
        out_specs=specs,
        out_shape=[jax.ShapeDtypeStruct(a.shape, MXU_DTYPE) for a in arrays],
        semantics=("parallel",),
        operands=tuple(arrays),
        side=side,
    )


def _forward_backward(x2, xb, target, small, gather, scatter, sync):
    w_in_t = _natural(gather.get("w_in"), "rows")
    proj, projb = _mm(xb, w_in_t, tb=True, tm=S, tn=512, tk=D, out_dtype=(F32, MXU_DTYPE), name="proj",
                      side=gather.take(110))

    qt = projb[:, :OFF_K].T.reshape(N_KV, GROUP, HEAD_DIM, S)
    k2, v2 = projb[:, OFF_K:OFF_V], projb[:, OFF_V:OFF_RX]
    kp = jnp.pad(_heads_major(k2, N_KV), ((0, 0), (BLOCK, 0), (0, 0)))
    vp = jnp.pad(_heads_major(v2, N_KV), ((0, 0), (BLOCK, 0), (0, 0)))
    kt = jnp.pad(k2.T.reshape(N_KV, HEAD_DIM, S), ((0, 0), (0, 0), (BLOCK, 0)))
    vt = jnp.pad(v2.T.reshape(N_KV, HEAD_DIM, S), ((0, 0), (0, 0), (BLOCK, 0)))
    sink_row = jnp.repeat(small["attn_sinks"].reshape(N_KV, 1, GROUP), BLOCK, axis=2)
    ot = _attn_fwd(qt, kp, vt, sink_row, side=gather.take(36)).reshape(D, S)

    rconv_w = _natural(gather.get("rnn_conv_w"), "cols")
    rxc = _conv_fwd(proj, OFF_RX, rconv_w, small["rnn_conv_b"], tc=512, name="rnn_conv_fwd", side=gather.take(18))
    r, i = _lru_gates(rxc, small["lru_wa"], small["lru_wi"], small["lru_ba"], small["lru_bi"], side=gather.take(33))
    h, yrin = _lru_scan_fwd(r, i, rxc, proj, small["lru_lambda"], side=gather.take(53))

    w_ap = _natural(gather.get("w_attn_proj"), "rows")
    w_rp = _natural(gather.get("w_rnn_proj"), "rows")
    y_attn = _mm(ot, w_ap, ta=True, tm=1024, tn=1024, tk=D, name="attn_proj", side=gather.take(22))
    y_rnn = _mm(yrin, w_rp, tm=1024, tn=1024, tk=D_RNN, name="rnn_proj", side=gather.take(27))
    mixin = _gate_fwd(y_attn, y_rnn, proj, small["b_gate"], side=gather.take(25))
    w_out = _natural(gather.get("w_out"), "rows")
    mix = _mm(mixin, w_out, tm=1024, tn=1024, tk=D, name="mix_out", side=gather.take(22))
    x1, x1b, xhat1, rstd1 = _ln_fwd(x2, mix, small["ln1_g"], small["ln1_b"], side=gather.take(23))

    w_up = gather.get("ffn_w_up")
    up = _mm(x1b, w_up, tm=S, tn=768, tk=D, b_block=768, name="ffn_up", side=gather.take(58))
    w_gate = gather.get("ffn_w_gate")
    gpre = _mm(x1b, w_gate, tm=S, tn=768, tk=D, b_block=768, name="ffn_gate", side=gather.take(58))
    fconv_w = _natural(gather.get("ffn_conv_w"), "cols")
    fin = _ffn_act_fwd(up, gpre, fconv_w, small["ffn_conv_b"], side=gather.take())
    w_down = _natural(gather.get("ffn_w_down"), "rows")
    f = _mm(fin, w_down, tm=1024, tn=1024, tk=2048, name="ffn_down")
    loss, dpre2, dpre2b, d_ln2_g, d_ln2_b = _ln_loss_bwd(x1, f, small["ln2_g"], small["ln2_b"], target)

    grads = {"ln2_g": d_ln2_g, "ln2_b": d_ln2_b}
    both = (F32, BF16)
    g32, g16 = _mm(fin, dpre2b, ta=True, tm=1024, tn=1024, tk=S, out_dtype=both, name="d_ffn_w_down")
    scatter.add_blocks("ffn_w_down", _blocks(g32, "rows"), _blocks(g16, "rows"))
    dfin = _mm(dpre2b, w_down, tb=True, tm=1024, tn=1024, tk=D, name="d_fin", side=scatter.take(57))
    dup, dgpre, grads["ffn_conv_w"], grads["ffn_conv_b"] = _ffn_act_bwd(
        dfin, up, gpre, fconv_w, small["ffn_conv_b"], side=scatter.take(85))
    g32, g16 = _mm(x1b, dup, ta=True, tm=1024, tn=768, tk=S, out_dtype=both, out_block=768, name="d_ffn_w_up",
                   side=scatter.take(57))
    scatter.add_blocks("ffn_w_up", g32, g16)
    g32, g16 = _mm(x1b, dgpre, ta=True, tm=1024, tn=768, tk=S, out_dtype=both, out_block=768, name="d_ffn_w_gate",
                   side=scatter.take(56))
    scatter.add_blocks("ffn_w_gate", g32, g16)
    dx1 = _mm(dup, w_up, tb=True, tm=1024, tn=1024, tk=768, b_block=768, name="d_x1_up", side=scatter.take(68))
    dx1 = _mm(dgpre, w_gate, tb=True, tm=1024, tn=1024, tk=768, b_block=768, add=dx1, name="d_x1_gate",
              side=scatter.take(70))
    dpre1, dpre1b, grads["ln1_g"], grads["ln1_b"] = _ln_bwd(dx1, dpre2, xhat1, rstd1, small["ln1_g"],
                                                            side=scatter.take(24))

    g32, g16 = _mm(mixin, dpre1b, ta=True, tm=1024, tn=1024, tk=S, out_dtype=both, name="d_w_out",
                   side=scatter.take(26))
    scatter.add_blocks("w_out", _blocks(g32, "rows"), _blocks(g16, "rows"))
    dmix = _mm(dpre1b, w_out, tb=True, tm=1024, tn=1024, tk=D, name="d_mixin", side=scatter.take(22))
    dproj, dya, dyr, dgl_r, db_a, db_r = _gate_bwd(
        dmix, y_attn, y_rnn, proj, small["b_gate"], side=scatter.take(36),
        window=(jax.ShapeDtypeStruct((S, D_IN), MXU_DTYPE), OFF_GA))
    grads["b_gate"] = jnp.concatenate([db_a, db_r], axis=1)
    g32, g16 = _mm(ot, dya, tm=1024, tn=1024, tk=S, out_dtype=both, name="d_w_attn_proj", side=scatter.take(38))
    scatter.add_blocks("w_attn_proj", _blocks(g32, "rows"), _blocks(g16, "rows"))
    g32, g16 = _mm(yrin, dyr, ta=True, tm=1280, tn=1024, tk=S, out_dtype=both, name="d_w_rnn_proj",
                   side=scatter.take(27))
    scatter.add_blocks("w_rnn_proj", _blocks(g32, "rows"), _blocks(g16, "rows"))
    dot_ = _mm(w_ap, dya, tb=True, tm=1024, tn=1024, tk=D, out_dtype=MXU_DTYPE, name="d_o", side=scatter.take(22))
    dyrin = _mm(dyr, w_rp, tb=True, tm=1024, tn=1280, tk=D, name="d_yrin", side=scatter.take(27))

    dproj, dzr, dzi, drxc_in, grads["lru_ba"], grads["lru_bi"], grads["lru_lambda"] = _lru_scan_bwd(
        dyrin, proj, h, r, i, rxc, small["lru_lambda"], side=scatter.take(94), window=(dproj, OFF_RY))
    grads["lru_wa"], grads["lru_wi"] = _lru_gate_wgrad(rxc, dzr, dzi, side=scatter.take(22))
    drxc = _lru_gate_xgrad(dzr, dzi, small["lru_wa"], small["lru_wi"], drxc_in, side=scatter.take(33))
    dproj, grads["rnn_conv_w"], grads["rnn_conv_b"] = _conv_bwd(
        drxc, proj, OFF_RX, rconv_w, tc=512, name="rnn_conv_bwd", side=scatter.take(29), window=(dproj, OFF_RX))

    dqt, dk, dv, dsink = _attn_bwd(qt, kp, kt, vp, sink_row, dot_.reshape(N_KV, GROUP, HEAD_DIM, S),
                                   side=_join(scatter.take(50), sync.begin_mats(grads)))
    grads["attn_sinks"] = dsink.reshape(1, N_KV * GROUP)
    for col0, piece in ((0, dqt.reshape(D, S).T), (OFF_K, _heads_minor(dk[:, BLOCK:, :]).astype(MXU_DTYPE)),
                        (OFF_V, _heads_minor(dv[:, BLOCK:, :]).astype(MXU_DTYPE)), (OFF_GR, dgl_r)):
        dproj = lax.dynamic_update_slice(dproj, piece, (0, col0))
    for part in range(W_IN_PARTS):
        cols = slice(part * (D // W_IN_PARTS), (part + 1) * (D // W_IN_PARTS))
        if part == 0:
            side = _join(scatter.take(55), sync.middle_mats(), sync.begin(loss, grads))
        else:
            side = scatter.take(68)
        g32, g16 = _mm(dproj, xb[:, cols], ta=True, tm=512, tn=D // W_IN_PARTS, tk=S, out_dtype=both,
                       name="d_w_in_%d" % part, side=side)
        scatter.add_blocks("w_in_%d" % part, _blocks(g32, "rows"), _blocks(g16, "rows"))
        scatter.flush_pairs("pairs_w_in_%d" % part)
    dx = _mm(dproj, w_in_t, tm=1024, tn=1024, tk=512, add=dpre1, add_scale=ALPHA, name="d_x",
             side=_join(scatter.take(400), sync.middle()))
    return dx


SHARDED = (
    ("w_in", "cols", 368), ("w_attn_proj", "rows", 32), ("w_rnn_proj", "rows", 32), ("w_out", "rows", 32),
    ("ffn_w_up", "cols", 128), ("ffn_w_gate", "cols", 128), ("ffn_w_down", "rows", 64),
)
SMALL_REPLICATED = ("b_gate", "rnn_conv_b", "lru_wa", "lru_ba", "lru_wi", "lru_bi", "lru_lambda", "attn_sinks",
                    "ln1_g", "ln1_b", "ffn_conv_b", "ln2_g", "ln2_b")
SMALL_SHARDED = ("rnn_conv_w", "ffn_conv_w")
SMALL_MATS = ("lru_wa", "lru_wi")
W_IN_PARTS = 2
WEIGHTS = ("w_in", "b_gate", "rnn_conv_w", "rnn_conv_b", "lru_wa", "lru_ba", "lru_wi", "lru_bi", "lru_lambda",
           "attn_sinks", "w_attn_proj", "w_rnn_proj", "w_out", "ln1_g", "ln1_b", "ffn_w_up", "ffn_w_gate",
           "ffn_conv_w", "ffn_conv_b", "ffn_w_down", "ln2_g", "ln2_b")


def kernel(x, w_in, b_gate, rnn_conv_w, rnn_conv_b, lru_wa, lru_ba, lru_wi, lru_bi, lru_lambda, attn_sinks, w_attn_proj, w_rnn_proj, w_out, ln1_g, ln1_b, ffn_w_up, ffn_w_gate, ffn_conv_w, ffn_conv_b, ffn_w_down, ln2_g, ln2_b, loss_target, m_w_in, m_b_gate, m_rnn_conv_w, m_rnn_conv_b, m_lru_wa, m_lru_ba, m_lru_wi, m_lru_bi, m_lru_lambda, m_attn_sinks, m_w_attn_proj, m_w_rnn_proj, m_w_out, m_ln1_g, m_ln1_b, m_ffn_w_up, m_ffn_w_gate, m_ffn_conv_w, m_ffn_conv_b, m_ffn_w_down, m_ln2_g, m_ln2_b, v_w_in, v_b_gate, v_rnn_conv_w, v_rnn_conv_b, v_lru_wa, v_lru_ba, v_lru_wi, v_lru_bi, v_lru_lambda, v_attn_sinks, v_w_attn_proj, v_w_rnn_proj, v_w_out, v_ln1_g, v_ln1_b, v_ffn_w_up, v_ffn_w_gate, v_ffn_conv_w, v_ffn_conv_b, v_ffn_w_down, v_ln2_g, v_ln2_b):
    given = dict(locals())
    wsh = {n: given[n][0] for n in WEIGHTS}
    msh = {n: given["m_" + n][0] for n in WEIGHTS}
    vsh = {n: given["v_" + n][0] for n in WEIGHTS}
    m_given = {n: given["m_" + n] for n in WEIGHTS}
    v_given = {n: given["v_" + n] for n in WEIGHTS}
    me = 4 * lax.axis_index("x") + 2 * lax.axis_index("y") + lax.axis_index("c")

    order = ("w_in", "rnn_conv_w", "ffn_conv_w", "w_attn_proj", "w_rnn_proj", "w_out", "ffn_w_up", "ffn_w_gate",
             "ffn_w_down")
    gather = _Gather({"w_in": wsh["w_in"].T.astype(MXU_DTYPE), **{n: wsh[n] for n in order[1:3]}})
    *casts, xb = _cast_many([wsh[n] for n in order[3:]] + [x[0]], side=gather.take(through="ffn_conv_w"))
    gather.add_shards(dict(zip(order[3:], casts)))
    small = {n: given[n] for n in SMALL_REPLICATED}
    small["lru_wa"] = _block_diag(wsh["lru_wa"])
    small["lru_wi"] = _block_diag(wsh["lru_wi"])
    scatter = _Scatter(me, jnp.stack([_index(_relative(k)) for k in FAR]).astype(jnp.int32))

    vec_names = tuple(n for n in SMALL_REPLICATED if n not in SMALL_MATS) + SMALL_SHARDED
    sync = _SmallSync(vec_names, SMALL_MATS)
    dx = _forward_backward(x[0], xb, loss_target[0], small, gather, scatter, sync)

    loss_total, g_small, mat_sums = sync.end()
    loss_total = loss_total.reshape(())
    for n in SMALL_SHARDED:
        width = wsh[n].shape[1]
        g_small[n] = lax.dynamic_slice_in_dim(g_small[n], me * width, width, axis=1)
    g_small = {n: g_small[n].reshape(given[n].shape) for n in vec_names}
    out = {}
    results = _adamw_many(*[[d[n] for n in vec_names] for d in (given, m_given, v_given, g_small)])
    for n, delta, nm, nv in zip(vec_names, *results):
        out[n] = (g_small[n], delta, nm, nv)
    for n in SMALL_MATS:
        g = mat_sums[n].reshape(given[n].shape)
        out[n] = (g, *_adamw_blocks(given[n], m_given[n], v_given[n], g, name="adamw_" + n))

    tile_rows = {n: tr for n, _, tr in SHARDED}
    me1 = me.reshape(1).astype(jnp.int32)
    res = None
    for n in list(scatter.sends):
        own, pair, far = scatter.get(n)
        if n.startswith("w_in_"):
            part = int(n[len("w_in_"):])
            w_t, m_t, v_t = (a["w_in"].transpose(0, 2, 1) for a in (given, m_given, v_given))
            res = _reduce_adamw(w_t, m_t, v_t, own, pair, far, me1, tr=tile_rows["w_in"], name="adamw_" + n,
                                part=part, earlier=res if part else None)
            out["w_in"] = tuple(r.transpose(0, 2, 1) for r in res)
        else:
            out[n] = tuple(_reduce_adamw(given[n], m_given[n], v_given[n], own, pair, far, me1, tr=tile_rows[n],
                                         name="adamw_" + n))

    outputs = [loss_total, dx[None]]
    for kind in range(4):
        outputs += [out[n][kind] for n in WEIGHTS]
    return tuple(outputs)
```

```python
import math

import jax
import jax.numpy as jnp
from jax import lax
from jax.experimental import pallas as pl
from jax.experimental.pallas import tpu as pltpu

F32 = jnp.float32
BF16 = jnp.bfloat16
MXU_DTYPE = jnp.bfloat16

N_DEV = 8
S = 2048
D = 2048
HEAD_DIM = 64
N_KV = 4
GROUP = 8
BLOCK = 128
D_KV = N_KV * HEAD_DIM
D_RNN = 2560
RNN_GROUP = 640
N_RNN_GROUPS = D_RNN // RNN_GROUP
RNN_BLOCK_W = 160
RNN_CONV_W = 4
LRU_C = 8.0
D_FF = 6144
FFN_CONV_W = 3
D_IN = 11776
OFF_K = 2048
OFF_V = 2304
OFF_RX = 2560
OFF_RY = 5120
OFF_GA = 7680
OFF_GR = 9728
LN_EPS = 1e-5
ALPHA = 2.0 ** 0.25
ADAM_LR = 0.001
ADAM_B1 = 0.9
ADAM_B2 = 0.999
ADAM_EPS = 1e-08
ADAM_WD = 0.01
ADAM_STEP = 10
NEG = -1e30
VMEM_LIMIT = 56 * 1024 * 1024
MID_RIDE_TENTHS = 6
MESH = pl.DeviceIdType.MESH
GELU_C = math.sqrt(2.0 / math.pi)


def _cparams(*sem):
    return pltpu.CompilerParams(dimension_semantics=sem or None, vmem_limit_bytes=VMEM_LIMIT)


def _call(body, *, name, grid, in_specs, out_specs, out_shape, operands, semantics, scratch_shapes=(), side=None,
          window=None):
    single = not isinstance(out_shape, (list, tuple))
    out_shape = [out_shape] if single else list(out_shape)
    out_specs = [out_specs] if single else list(out_specs)
    in_specs = list(in_specs)
    operands = tuple(operands)
    scratch_shapes = list(scratch_shapes)
    hbm = pl.BlockSpec(memory_space=pltpu.HBM)
    aliases = {}
    if window is not None:
        whole, col0 = window
        block, index_map = out_specs[0].block_shape, out_specs[0].index_map
        assert col0 % block[1] == 0 and out_shape[0].dtype == whole.dtype
        out_specs[0] = pl.BlockSpec(block, lambda *g: (index_map(*g)[0], index_map(*g)[1] + col0 // block[1]))
        out_shape[0] = jax.ShapeDtypeStruct(whole.shape, whole.dtype)
        if not isinstance(whole, jax.ShapeDtypeStruct):
            aliases[len(in_specs)] = 0
            in_specs.append(hbm)
            operands += (whole,)
            compute, n_read = body, len(in_specs) - 1

            def body(*refs):
                compute(*refs[:n_read], *refs[n_read + 1:])

    if side is None:
        res = pl.pallas_call(
            body, name=name, grid=grid, in_specs=in_specs, out_specs=out_specs, out_shape=out_shape,
            scratch_shapes=scratch_shapes, input_output_aliases=aliases,
            compiler_params=_cparams(*semantics))(*operands)
        return res[0] if single else res
    n_in, n_out, n_scr = len(in_specs), len(out_shape), len(scratch_shapes)
    s_in, s_out = len(side.operands), len(side.out_shape)
    steps = math.prod(grid)
    mid_step = (steps * MID_RIDE_TENTHS) // 10

    def with_copies(*refs):
        core_in, side_in = refs[:n_in], refs[n_in:n_in + s_in]
        o0 = n_in + s_in
        core_out, side_out = refs[o0:o0 + n_out], refs[o0 + n_out:o0 + n_out + s_out]
        c0 = o0 + n_out + s_out
        core_scr, sems = refs[c0:c0 + n_scr], refs[c0 + n_scr:]
        step = 0
        for d, size in enumerate(grid):
            step = step * size + pl.program_id(d)

        @pl.when(step == 0)
        def _():
            side.start(side_in, side_out, sems)

        body(*core_in, *core_out, *core_scr)

        @pl.when(step == mid_step)
        def _():
            side.mid(side_in, side_out, sems)

        @pl.when(step == steps - 1)
        def _():
            side.finish(side_in, side_out, sems)

    res = pl.pallas_call(
        with_copies, name=name, grid=grid,
        in_specs=in_specs + [hbm] * s_in, out_specs=out_specs + [hbm] * s_out,
        out_shape=out_shape + list(side.out_shape),
        scratch_shapes=scratch_shapes + list(side.sems),
        input_output_aliases={**aliases, **{n_in + i: n_out + o for i, o in side.aliases.items()}},
        compiler_params=_cparams(*(("arbitrary",) * len(grid))))(*operands, *side.operands)
    side.done(res[n_out:])
    return res[0] if single else res[:n_out]


def _run_side(side, name):
    def body(*refs):
        s_in, s_out = len(side.operands), len(side.out_shape)
        side.start(refs[:s_in], refs[s_in:s_in + s_out], refs[s_in + s_out:])
        side.mid(refs[:s_in], refs[s_in:s_in + s_out], refs[s_in + s_out:])
        side.finish(refs[:s_in], refs[s_in:s_in + s_out], refs[s_in + s_out:])

    hbm = pl.BlockSpec(memory_space=pltpu.HBM)
    res = pl.pallas_call(
        body, name=name, in_specs=[hbm] * len(side.operands), out_specs=[hbm] * len(side.out_shape),
        out_shape=list(side.out_shape), scratch_shapes=list(side.sems),
        input_output_aliases=dict(side.aliases))(*side.operands)
    side.done(res)


def _gelu(x):
    x2 = x * x
    t = jnp.tanh(GELU_C * (x + 0.044715 * x * x2))
    g = 0.5 * x * (1.0 + t)
    dg = 0.5 * (1.0 + t) + 0.5 * x * (1.0 - t * t) * (GELU_C * (1.0 + 3.0 * 0.044715 * x2))
    return g, dg


def _sigmoid(x):
    return 1.0 / (1.0 + jnp.exp(-x))


def _softplus(x):
    z = jnp.exp(-jnp.abs(x))
    small = z * (1.0 - z * (0.5 - z * (1.0 / 3.0 - 0.25 * z)))
    return jnp.maximum(x, 0.0) + jnp.where(z < 0.02, small, jnp.log(1.0 + z))


def _one_minus_exp(x):
    series = -x * (1.0 + x * (0.5 + x * (1.0 / 6.0 + x * (1.0 / 24.0))))
    return jnp.where(x > -0.03, series, 1.0 - jnp.exp(x))


def _colsum(v):
    return jnp.sum(v, axis=0, keepdims=True)


def _mm(a, b, *, tm, tn, tk, name, ta=False, tb=False, out_dtype=F32, b_block=None, out_block=None, add=None,
        add_scale=1.0, side=None):
    out_dtypes = out_dtype if isinstance(out_dtype, tuple) else (out_dtype,)
    if ta:
        k_dim, m_dim = a.shape
    else:
        m_dim, k_dim = a.shape
    if b_block is None:
        n_dim = b.shape[0] if tb else b.shape[1]
    else:
        n_dim = b.shape[1] if tb else b.shape[0] * b_block
    assert m_dim % tm == 0 and n_dim % tn == 0 and k_dim % tk == 0, (name, m_dim, n_dim, k_dim)
    nk = k_dim // tk
    dims = (((0 if ta else 1,), (1 if tb else 0,)), ((), ()))
    has_add = add is not None

    def body(*refs):
        a_ref, b_ref = refs[0], refs[1]
        add_ref = refs[2] if has_add else None
        first_out = 3 if has_add else 2
        o_refs = refs[first_out:first_out + len(out_dtypes)]

        def product():
            return lax.dot_general(a_ref[...].astype(MXU_DTYPE), b_ref[...].astype(MXU_DTYPE), dims,
                                   preferred_element_type=F32)

        def finish(acc):
            if has_add:
                acc = acc + add_scale * add_ref[...]
            for o_ref in o_refs:
                o_ref[...] = acc.astype(o_ref.dtype)

        if nk == 1:
            finish(product())
        else:
            acc_ref = refs[-1]
            k = pl.program_id(2)

            @pl.when(k == 0)
            def _():
                acc_ref[...] = jnp.zeros_like(acc_ref)

            acc_ref[...] += product()

            @pl.when(k == nk - 1)
            def _():
                finish(acc_ref[...])

    if ta:
        a_spec = pl.BlockSpec((tk, tm), lambda i, j, k: (k, i))
    else:
        a_spec = pl.BlockSpec((tm, tk), lambda i, j, k: (i, k))
    if b_block is None:
        if tb:
            b_spec = pl.BlockSpec((tn, tk), lambda i, j, k: (j, k))
        else:
            b_spec = pl.BlockSpec((tk, tn), lambda i, j, k: (k, j))
    elif tb:
        assert b_block % tk == 0
        b_spec = pl.BlockSpec((None, tn, tk), lambda i, j, k: ((k * tk) // b_block, j, ((k * tk) % b_block) // tk))
    else:
        assert b_block % tn == 0
        b_spec = pl.BlockSpec((None, tk, tn), lambda i, j, k: ((j * tn) // b_block, k, ((j * tn) % b_block) // tn))
    in_specs = [a_spec, b_spec]
    operands = [a, b]
    if has_add:
        in_specs.append(pl.BlockSpec((tm, tn), lambda i, j, k: (i, j)))
        operands.append(add)
    if out_block is None:
        out_spec = pl.BlockSpec((tm, tn), lambda i, j, k: (i, j))
        out_dims = (m_dim, n_dim)
    else:
        assert out_block % tn == 0
        out_spec = pl.BlockSpec((None, tm, tn), lambda i, j, k: ((j * tn) // out_block, i, ((j * tn) % out_block) // tn))
        out_dims = (n_dim // out_block, m_dim, out_block)
    res = _call(
        body,
        name=name,
        grid=(m_dim // tm, n_dim // tn, nk),
        in_specs=in_specs,
        out_specs=[out_spec] * len(out_dtypes),
        out_shape=[jax.ShapeDtypeStruct(out_dims, dt) for dt in out_dtypes],
        scratch_shapes=[pltpu.VMEM((tm, tn), F32)] if nk > 1 else [],
        semantics=("parallel", "parallel", "arbitrary"),
        operands=tuple(operands),
        side=side,
    )
    return res if isinstance(out_dtype, tuple) else res[0]


def _attn_bias(bias_ref, h):
    key = lax.broadcasted_iota(jnp.int32, (2 * BLOCK, GROUP * BLOCK), 0)
    col = lax.broadcasted_iota(jnp.int32, (2 * BLOCK, GROUP * BLOCK), 1)
    dist = BLOCK + (col & (BLOCK - 1)) - key
    head = h * GROUP + (col >> 7) + 1
    slope = jnp.exp(head.astype(F32) * (-0.25 * math.log(2.0)))
    bias = jnp.where((dist >= 0) & (dist < BLOCK), -slope * dist.astype(F32), NEG)
    bias_ref[1] = bias
    bias_ref[0] = jnp.where(key < BLOCK, NEG, bias)


def _attn_probs(kb, qt, bias, sink):
    s = jnp.dot(kb, qt, preferred_element_type=F32) * (HEAD_DIM ** -0.5) + bias
    m = jnp.maximum(jnp.max(s, axis=0, keepdims=True), sink)
    e = jnp.exp(s - m)
    e_sink = jnp.exp(sink - m)
    inv = 1.0 / (jnp.sum(e, axis=0, keepdims=True) + e_sink)
    return e * inv, e_sink * inv


def _heads_on_lanes(ref, r0):
    return jnp.concatenate([ref[g, :, pl.ds(r0, BLOCK)] for g in range(GROUP)], axis=1)


def _attn_fwd(qt, kp, vt, sink_row, side=None):
    cols = GROUP * BLOCK

    def body(q_ref, k_ref, vt_ref, sink_ref, o_ref, bias_ref):
        _attn_bias(bias_ref, pl.program_id(0))
        sink = sink_ref[...]

        def step(n, carry):
            r0 = pl.multiple_of(n * BLOCK, BLOCK)
            p, _ = _attn_probs(k_ref[pl.ds(r0, 2 * BLOCK), :], _heads_on_lanes(q_ref, r0),
                               bias_ref[jnp.minimum(n, 1)], sink)
            o = jnp.dot(vt_ref[:, pl.ds(r0, 2 * BLOCK)], p.astype(MXU_DTYPE), preferred_element_type=F32)
            for g in range(GROUP):
                o_ref[g, :, pl.ds(r0, BLOCK)] = o[:, g * BLOCK:(g + 1) * BLOCK].astype(o_ref.dtype)
            return carry

        lax.fori_loop(0, S // BLOCK, step, 0)

    hm = pl.BlockSpec((None, GROUP, HEAD_DIM, S), lambda h: (h, 0, 0, 0))
    return _call(
        body,
        name="attn_fwd",
        grid=(N_KV,),
        in_specs=[
            hm,
            pl.BlockSpec((None, BLOCK + S, HEAD_DIM), lambda h: (h, 0, 0)),
            pl.BlockSpec((None, HEAD_DIM, BLOCK + S), lambda h: (h, 0, 0)),
            pl.BlockSpec((None, 1, cols), lambda h: (h, 0, 0)),
        ],
        out_specs=hm,
        out_shape=jax.ShapeDtypeStruct((N_KV, GROUP, HEAD_DIM, S), MXU_DTYPE),
        scratch_shapes=[pltpu.VMEM((2, 2 * BLOCK, cols), F32)],
        semantics=("parallel",),
        operands=(qt, kp, vt, sink_row),
        side=side,
    )


def _attn_bwd(qt, kp, kt, vp, sink_row, dot_, side=None):
    cols = GROUP * BLOCK

    def body(q_ref, k_ref, kt_ref, v_ref, sink_ref, do_ref, dq_ref, dk_ref, dv_ref, dsink_ref, bias_ref):
        _attn_bias(bias_ref, pl.program_id(0))
        sink = sink_ref[...]
        dk_ref[...] = jnp.zeros_like(dk_ref)
        dv_ref[...] = jnp.zeros_like(dv_ref)
        nt = (((1,), (1,)), ((), ()))

        def step(n, sink_acc):
            r0 = pl.multiple_of(n * BLOCK, BLOCK)
            band = pl.ds(r0, 2 * BLOCK)
            qn = _heads_on_lanes(q_ref, r0)
            don = _heads_on_lanes(do_ref, r0)
            p, p_sink = _attn_probs(k_ref[band, :], qn, bias_ref[jnp.minimum(n, 1)], sink)
            dp = jnp.dot(v_ref[band, :], don, preferred_element_type=F32)
            delta = jnp.sum(p * dp, axis=0, keepdims=True)
            ds = (p * (dp - delta) * (HEAD_DIM ** -0.5)).astype(MXU_DTYPE)
            dq = jnp.dot(kt_ref[:, band], ds, preferred_element_type=F32)
            for g in range(GROUP):
                dq_ref[g, :, pl.ds(r0, BLOCK)] = dq[:, g * BLOCK:(g + 1) * BLOCK].astype(dq_ref.dtype)
            dk_ref[band, :] += lax.dot_general(ds, qn, nt, preferred_element_type=F32)
            dv_ref[band, :] += lax.dot_general(p.astype(MXU_DTYPE), don, nt, preferred_element_type=F32)
            return sink_acc - p_sink * delta

        sink_acc = lax.fori_loop(0, S // BLOCK, step, jnp.zeros((1, cols), F32))
        for g in range(GROUP):
            dsink_ref[g:g + 1, :] = jnp.sum(sink_acc[:, g * BLOCK:(g + 1) * BLOCK], axis=1, keepdims=True)

    hm = pl.BlockSpec((None, GROUP, HEAD_DIM, S), lambda h: (h, 0, 0, 0))
    kv = pl.BlockSpec((None, BLOCK + S, HEAD_DIM), lambda h: (h, 0, 0))
    return _call(
        body,
        name="attn_bwd",
        grid=(N_KV,),
        in_specs=[hm, kv, pl.BlockSpec((None, HEAD_DIM, BLOCK + S), lambda h: (h, 0, 0)), kv,
                  pl.BlockSpec((None, 1, cols), lambda h: (h, 0, 0)), hm],
        out_specs=[hm, kv, kv, pl.BlockSpec((None, GROUP, 1), lambda h: (h, 0, 0))],
        out_shape=[
            jax.ShapeDtypeStruct((N_KV, GROUP, HEAD_DIM, S), MXU_DTYPE),
            jax.ShapeDtypeStruct((N_KV, BLOCK + S, HEAD_DIM), F32),
            jax.ShapeDtypeStruct((N_KV, BLOCK + S, HEAD_DIM), F32),
            jax.ShapeDtypeStruct((N_KV, GROUP, 1), F32),
        ],
        scratch_shapes=[pltpu.VMEM((2, 2 * BLOCK, cols), F32)],
        semantics=("parallel",),
        operands=(qt, kp, kt, vp, sink_row, dot_),
        side=side,
    )


PAD = 8
CHUNK = 256


def _past_taps(xpad_ref, r0, width):
    ext = xpad_ref[pl.ds(r0, CHUNK + PAD), :]
    taps = []
    for k in range(width):
        back = width - 1 - k
        taps.append((ext if back == 0 else pltpu.roll(ext, back, 0))[PAD:, :])
    return taps


def _future_taps(xpad_ref, r0, width):
    ext = xpad_ref[pl.ds(r0, CHUNK + PAD), :]
    taps = []
    for ahead in range(width):
        taps.append((ext if ahead == 0 else pltpu.roll(ext, CHUNK + PAD - ahead, 0))[:CHUNK, :])
    return taps


def _conv_fwd(src, col0, w, b, *, tc, name, side=None):
    width, c_dim = w.shape

    def body(x_ref, w_ref, b_ref, o_ref, xpad_ref):
        xpad_ref[pl.ds(0, PAD), :] = jnp.zeros((PAD, tc), F32)
        xpad_ref[pl.ds(PAD, S), :] = x_ref[...]
        wv = w_ref[...]
        bv = b_ref[...]

        def step(ci, carry):
            r0 = pl.multiple_of(ci * CHUNK, CHUNK)
            taps = _past_taps(xpad_ref, r0, width)
            y = bv + taps[0] * wv[0:1, :]
            for k in range(1, width):
                y = y + taps[k] * wv[k:k + 1, :]
            o_ref[pl.ds(r0, CHUNK), :] = y
            return carry

        lax.fori_loop(0, S // CHUNK, step, 0)

    return _call(
        body,
        name=name,
        grid=(c_dim // tc,),
        in_specs=[
            pl.BlockSpec((S, tc), lambda j: (0, col0 // tc + j)),
            pl.BlockSpec((width, tc), lambda j: (0, j)),
            pl.BlockSpec((1, tc), lambda j: (0, j)),
        ],
        out_specs=pl.BlockSpec((S, tc), lambda j: (0, j)),
        out_shape=jax.ShapeDtypeStruct((S, c_dim), F32),
        scratch_shapes=[pltpu.VMEM((S + PAD, tc), F32)],
        semantics=("parallel",),
        operands=(src, w, b),
        side=side,
    )


def _conv_bwd(dy, src, col0, w, *, tc, name, side=None, window=None):
    width, c_dim = w.shape

    def body(dy_ref, x_ref, w_ref, dx_ref, dw_ref, db_ref, xpad_ref, dpad_ref):
        xpad_ref[pl.ds(0, PAD), :] = jnp.zeros((PAD, tc), F32)
        xpad_ref[pl.ds(PAD, S), :] = x_ref[...]
        dpad_ref[pl.ds(0, S), :] = dy_ref[...]
        dpad_ref[pl.ds(S, PAD), :] = jnp.zeros((PAD, tc), F32)
        wv = w_ref[...]

        def step(ci, acc):
            r0 = pl.multiple_of(ci * CHUNK, CHUNK)
            past = _past_taps(xpad_ref, r0, width)
            ahead = _future_taps(dpad_ref, r0, width)
            d = ahead[0]
            dx = d * wv[width - 1:width, :]
            for j in range(1, width):
                dx = dx + ahead[j] * wv[width - 1 - j:width - j, :]
            dx_ref[pl.ds(r0, CHUNK), :] = dx.astype(dx_ref.dtype)
            return tuple(acc[k] + _colsum(past[k] * d) for k in range(width)) + (acc[width] + _colsum(d),)

        zero = jnp.zeros((1, tc), F32)
        acc = lax.fori_loop(0, S // CHUNK, step, (zero,) * (width + 1))
        for k in range(width):
            dw_ref[k:k + 1, :] = acc[k]
        db_ref[...] = acc[width]

    return _call(
        body,
        name=name,
        grid=(c_dim // tc,),
        in_specs=[
            pl.BlockSpec((S, tc), lambda j: (0, j)),
            pl.BlockSpec((S, tc), lambda j: (0, col0 // tc + j)),
            pl.BlockSpec((width, tc), lambda j: (0, j)),
        ],
        out_specs=[
            pl.BlockSpec((S, tc), lambda j: (0, j)),
            pl.BlockSpec((width, tc), lambda j: (0, j)),
            pl.BlockSpec((1, tc), lambda j: (0, j)),
        ],
        out_shape=[
            jax.ShapeDtypeStruct((S, c_dim), MXU_DTYPE),
            jax.ShapeDtypeStruct((width, c_dim), F32),
            jax.ShapeDtypeStruct((1, c_dim), F32),
        ],
        scratch_shapes=[pltpu.VMEM((S + PAD, tc), F32), pltpu.VMEM((S + PAD, tc), F32)],
        semantics=("parallel",),
        operands=(dy, src, w),
        side=side,
        window=window,
    )


SCAN_TC = 256


def _lru_gates(rxc, wa, wi, ba, bi, side=None):
    tm = 512

    def body(x_ref, wa_ref, wi_ref, ba_ref, bi_ref, r_ref, i_ref):
        xv = x_ref[...].astype(MXU_DTYPE)
        r_ref[...] = _sigmoid(jnp.dot(xv, wa_ref[...].astype(MXU_DTYPE), preferred_element_type=F32) + ba_ref[...])
        i_ref[...] = _sigmoid(jnp.dot(xv, wi_ref[...].astype(MXU_DTYPE), preferred_element_type=F32) + bi_ref[...])

    x_spec = pl.BlockSpec((tm, RNN_GROUP), lambda g, i: (i, g))
    w_spec = pl.BlockSpec((None, RNN_GROUP, RNN_GROUP), lambda g, i: (g, 0, 0))
    b_spec = pl.BlockSpec((1, RNN_GROUP), lambda g, i: (0, g))
    return _call(
        body,
        name="lru_gates",
        grid=(N_RNN_GROUPS, S // tm),
        in_specs=[x_spec, w_spec, w_spec, b_spec, b_spec],
        out_specs=[x_spec, x_spec],
        out_shape=[jax.ShapeDtypeStruct((S, D_RNN), F32)] * 2,
        semantics=("parallel", "parallel"),
        operands=(rxc, wa, wi, ba, bi),
        side=side,
    )


def _scan_down(a, u, row):
    for d in (1, 2, 4):
        a_s = jnp.where(row >= d, pltpu.roll(a, d, 0), 1.0)
        u_s = jnp.where(row >= d, pltpu.roll(u, d, 0), 0.0)
        u = a * u_s + u
        a = a * a_s
    return a, u


def _scan_up(a, u, row):
    for d in (1, 2, 4):
        a_s = jnp.where(row < 8 - d, pltpu.roll(a, 8 - d, 0), 1.0)
        u_s = jnp.where(row < 8 - d, pltpu.roll(u, 8 - d, 0), 0.0)
        u = a * u_s + u
        a = a * a_s
    return a, u


def _lru_scan_fwd(r, i, rxc, proj, lam, side=None):
    tc = SCAN_TC

    def body(r_ref, i_ref, x_ref, ry_ref, lam_ref, h_ref, y_ref):
        rate = LRU_C * _softplus(-lam_ref[...])
        row = lax.broadcasted_iota(jnp.int32, (8, tc), 0)

        def step(ci, carry):
            r0 = pl.multiple_of(ci * 16, 16)
            log_a = -rate * r_ref[pl.ds(r0, 16), :]
            a16 = jnp.exp(log_a)
            u16 = jnp.sqrt(_one_minus_exp(2.0 * log_a)) * (i_ref[pl.ds(r0, 16), :] * x_ref[pl.ds(r0, 16), :])
            hs = []
            for half in range(2):
                a_cum, h0 = _scan_down(a16[8 * half:8 * half + 8, :], u16[8 * half:8 * half + 8, :], row)
                h = a_cum * carry + h0
                carry = jnp.broadcast_to(h[7:8, :], (8, tc))
                hs.append(h)
            h16 = jnp.concatenate(hs, axis=0)
            h_ref[pl.ds(r0, 16), :] = h16
            y_ref[pl.ds(r0, 16), :] = (h16 * _gelu(ry_ref[pl.ds(r0, 16), :])[0]).astype(y_ref.dtype)
            return carry

        lax.fori_loop(0, S // 16, step, jnp.zeros((8, tc), F32))

    col = pl.BlockSpec((S, tc), lambda j: (0, j))
    return _call(
        body,
        name="lru_scan_fwd",
        grid=(D_RNN // tc,),
        in_specs=[col, col, col, pl.BlockSpec((S, tc), lambda j: (0, OFF_RY // tc + j)),
                  pl.BlockSpec((1, tc), lambda j: (0, j))],
        out_specs=[col, col],
        out_shape=[jax.ShapeDtypeStruct((S, D_RNN), F32), jax.ShapeDtypeStruct((S, D_RNN), MXU_DTYPE)],
        semantics=("parallel",),
        operands=(r, i, rxc, proj, lam),
        side=side,
    )


def _lru_scan_bwd(dy, proj, h, r, i, rxc, lam, side=None, window=None):
    tc = SCAN_TC

    def body(dy_ref, ry_ref, h_ref, r_ref, i_ref, x_ref, lam_ref,
             dry_ref, dzr_ref, dzi_ref, dx_ref, dba_ref, dbi_ref, dlam_ref, a_ref, dh_ref, hp_ref):
        lam_v = lam_ref[...]
        rate = LRU_C * _softplus(-lam_v)
        dlam_scale = LRU_C * _sigmoid(-lam_v)
        row = lax.broadcasted_iota(jnp.int32, (8, tc), 0)
        hp_ref[pl.ds(0, PAD), :] = jnp.zeros((PAD, tc), F32)
        hp_ref[pl.ds(PAD, S), :] = h_ref[...]
        a_ref[pl.ds(S, PAD), :] = jnp.zeros((PAD, tc), F32)

        def prep(ci, carry):
            r0 = pl.multiple_of(ci * CHUNK, CHUNK)
            a_ref[pl.ds(r0, CHUNK), :] = jnp.exp(-rate * r_ref[pl.ds(r0, CHUNK), :])
            ge, dge = _gelu(ry_ref[pl.ds(r0, CHUNK), :])
            dyv = dy_ref[pl.ds(r0, CHUNK), :]
            dh_ref[pl.ds(r0, CHUNK), :] = dyv * ge
            dry_ref[pl.ds(r0, CHUNK), :] = (dyv * h_ref[pl.ds(r0, CHUNK), :] * dge).astype(dry_ref.dtype)
            return carry

        lax.fori_loop(0, S // CHUNK, prep, 0)

        def step(ci, state):
            carry, dba, dbi, dlam = state
            r0 = pl.multiple_of(S - 16 - ci * 16, 16)
            a_ext = a_ref[pl.ds(r0, 24), :]
            a_next = pltpu.roll(a_ext, 23, 0)
            h_prev = pltpu.roll(hp_ref[pl.ds(r0, 24), :], 1, 0)
            dh16 = dh_ref[pl.ds(r0, 16), :]
            gs = [None, None]
            for half in (1, 0):
                lo = 8 * half
                c_cum, g0 = _scan_up(a_next[lo:lo + 8, :], dh16[lo:lo + 8, :], row)
                g = c_cum * carry + g0
                carry = jnp.broadcast_to(g[0:1, :], (8, tc))
                gs[half] = g
            g16 = jnp.concatenate(gs, axis=0)
            a16 = a_ext[0:16, :]
            r16 = r_ref[pl.ds(r0, 16), :]
            i16 = i_ref[pl.ds(r0, 16), :]
            x16 = x_ref[pl.ds(r0, 16), :]
            a2 = a16 * a16
            sq = jnp.sqrt(_one_minus_exp(-2.0 * rate * r16))
            dx_ref[pl.ds(r0, 16), :] = g16 * sq * i16
            dzi = g16 * sq * x16 * i16 * (1.0 - i16)
            dlog_a = g16 * h_prev[8:24, :] * a16 - g16 * i16 * x16 * a2 / sq
            dzr = -rate * dlog_a * r16 * (1.0 - r16)
            dzr_ref[pl.ds(r0, 16), :] = dzr.astype(dzr_ref.dtype)
            dzi_ref[pl.ds(r0, 16), :] = dzi.astype(dzi_ref.dtype)
            return carry, dba + _colsum(dzr), dbi + _colsum(dzi), dlam + _colsum(dlog_a * r16)

        zero = jnp.zeros((1, tc), F32)
        _, dba, dbi, dlam = lax.fori_loop(0, S // 16, step, (jnp.zeros((8, tc), F32), zero, zero, zero))
        dba_ref[...] = dba
        dbi_ref[...] = dbi
        dlam_ref[...] = dlam * dlam_scale

    col = pl.BlockSpec((S, tc), lambda j: (0, j))
    vec = pl.BlockSpec((1, tc), lambda j: (0, j))
    return _call(
        body,
        name="lru_scan_bwd",
        grid=(D_RNN // tc,),
        in_specs=[col, pl.BlockSpec((S, tc), lambda j: (0, OFF_RY // tc + j)), col, col, col, col, vec],
        out_specs=[col, col, col, col, vec, vec, vec],
        out_shape=[jax.ShapeDtypeStruct((S, D_RNN), MXU_DTYPE)] * 3 + [jax.ShapeDtypeStruct((S, D_RNN), F32)]
        + [jax.ShapeDtypeStruct((1, D_RNN), F32)] * 3,
        scratch_shapes=[pltpu.VMEM((S + PAD, tc), F32), pltpu.VMEM((S, tc), F32), pltpu.VMEM((S + PAD, tc), F32)],
        semantics=("parallel",),
        operands=(dy, proj, h, r, i, rxc, lam),
        side=side,
        window=window,
    )


def _lru_gate_wgrad(rxc, dzr, dzi, side=None):
    def body(x_ref, dzr_ref, dzi_ref, dwa_ref, dwi_ref):
        xv = x_ref[...].astype(MXU_DTYPE)
        dims = (((0,), (0,)), ((), ()))
        dwa_ref[...] = lax.dot_general(xv, dzr_ref[...], dims, preferred_element_type=F32)
        dwi_ref[...] = lax.dot_general(xv, dzi_ref[...], dims, preferred_element_type=F32)

    col = pl.BlockSpec((S, RNN_GROUP), lambda g: (0, g))
    w_spec = pl.BlockSpec((None, RNN_GROUP, RNN_GROUP), lambda g: (g, 0, 0))
    return _call(
        body,
        name="lru_gate_wgrad",
        grid=(N_RNN_GROUPS,),
        in_specs=[col, col, col],
        out_specs=[w_spec, w_spec],
        out_shape=[jax.ShapeDtypeStruct((N_RNN_GROUPS, RNN_GROUP, RNN_GROUP), F32)] * 2,
        semantics=("parallel",),
        operands=(rxc, dzr, dzi),
        side=side,
    )


def _lru_gate_xgrad(dzr, dzi, wa, wi, dx_in, side=None):
    tm = 512

    def body(dzr_ref, dzi_ref, wa_ref, wi_ref, dx_ref, o_ref):
        dims = (((1,), (1,)), ((), ()))
        o_ref[...] = (dx_ref[...]
                      + lax.dot_general(dzr_ref[...], wa_ref[...].astype(MXU_DTYPE), dims, preferred_element_type=F32)
                      + lax.dot_general(dzi_ref[...], wi_ref[...].astype(MXU_DTYPE), dims, preferred_element_type=F32))

    x_spec = pl.BlockSpec((tm, RNN_GROUP), lambda g, i: (i, g))
    w_spec = pl.BlockSpec((None, RNN_GROUP, RNN_GROUP), lambda g, i: (g, 0, 0))
    return _call(
        body,
        name="lru_gate_xgrad",
        grid=(N_RNN_GROUPS, S // tm),
        in_specs=[x_spec, x_spec, w_spec, w_spec, x_spec],
        out_specs=x_spec,
        out_shape=jax.ShapeDtypeStruct((S, D_RNN), F32),
        semantics=("parallel", "parallel"),
        operands=(dzr, dzi, wa, wi, dx_in),
        side=side,
    )


def _gate_fwd(y_attn, y_rnn, proj, b_gate, side=None):
    t = 512

    def body(ya_ref, yr_ref, ga_ref, gr_ref, ba_ref, br_ref, o_ref):
        o_ref[...] = (_sigmoid(ga_ref[...] + ba_ref[...]) * ya_ref[...]
                      + _sigmoid(gr_ref[...] + br_ref[...]) * yr_ref[...]).astype(o_ref.dtype)

    tile = pl.BlockSpec((t, t), lambda i, j: (i, j))
    return _call(
        body,
        name="gate_fwd",
        grid=(S // t, D // t),
        in_specs=[tile, tile,
                  pl.BlockSpec((t, t), lambda i, j: (i, OFF_GA // t + j)),
                  pl.BlockSpec((t, t), lambda i, j: (i, OFF_GR // t + j)),
                  pl.BlockSpec((1, t), lambda i, j: (0, j)),
                  pl.BlockSpec((1, t), lambda i, j: (0, D // t + j))],
        out_specs=tile,
        out_shape=jax.ShapeDtypeStruct((S, D), MXU_DTYPE),
        semantics=("parallel", "parallel"),
        operands=(y_attn, y_rnn, proj, proj, b_gate, b_gate),
        side=side,
    )


def _gate_bwd(dmix, y_attn, y_rnn, proj, b_gate, side=None, window=None):
    t = 512

    def body(dm_ref, ya_ref, yr_ref, ga_ref, gr_ref, ba_ref, br_ref,
             dga_ref, dya_ref, dyr_ref, dgr_ref, dba_ref, dbr_ref):
        @pl.when(pl.program_id(1) == 0)
        def _():
            dba_ref[...] = jnp.zeros_like(dba_ref)
            dbr_ref[...] = jnp.zeros_like(dbr_ref)

        dm = dm_ref[...]
        ga = _sigmoid(ga_ref[...] + ba_ref[...])
        gr = _sigmoid(gr_ref[...] + br_ref[...])
        dya_ref[...] = (dm * ga).astype(dya_ref.dtype)
        dyr_ref[...] = (dm * gr).astype(dyr_ref.dtype)
        dga = dm * ya_ref[...] * ga * (1.0 - ga)
        dgr = dm * yr_ref[...] * gr * (1.0 - gr)
        dga_ref[...] = dga.astype(dga_ref.dtype)
        dgr_ref[...] = dgr.astype(dgr_ref.dtype)
        dba_ref[...] += _colsum(dga)
        dbr_ref[...] += _colsum(dgr)

    tile = pl.BlockSpec((t, t), lambda j, i: (i, j))
    vec = pl.BlockSpec((1, t), lambda j, i: (0, j))
    return _call(
        body,
        name="gate_bwd",
        grid=(D // t, S // t),
        in_specs=[tile, tile, tile,
                  pl.BlockSpec((t, t), lambda j, i: (i, OFF_GA // t + j)),
                  pl.BlockSpec((t, t), lambda j, i: (i, OFF_GR // t + j)),
                  vec,
                  pl.BlockSpec((1, t), lambda j, i: (0, D // t + j))],
        out_specs=[tile, tile, tile, tile, vec, vec],
        out_shape=[jax.ShapeDtypeStruct((S, D), MXU_DTYPE)] * 4 + [jax.ShapeDtypeStruct((1, D), F32)] * 2,
        semantics=("parallel", "arbitrary"),
        operands=(dmix, y_attn, y_rnn, proj, proj, b_gate, b_gate),
        side=side,
        window=window,
    )


LN_TM = 256


def _ln_stats(pre):
    mu = jnp.mean(pre, axis=-1, keepdims=True)
    xc = pre - mu
    rstd = lax.rsqrt(jnp.mean(xc * xc, axis=-1, keepdims=True) + LN_EPS)
    return xc * rstd, rstd


def _ln_input_grad(dy, xhat, rstd, g):
    dyg = dy * g
    return rstd * (dyg - jnp.mean(dyg, axis=-1, keepdims=True)
                   - xhat * jnp.mean(dyg * xhat, axis=-1, keepdims=True))


def _ln_fwd(res, branch, g, b, side=None):
    def body(res_ref, br_ref, g_ref, b_ref, y_ref, yb_ref, xhat_ref, rstd_ref):
        xhat, rstd = _ln_stats(ALPHA * res_ref[...] + br_ref[...])
        y = xhat * g_ref[...] + b_ref[...]
        y_ref[...] = y
        yb_ref[...] = y.astype(yb_ref.dtype)
        xhat_ref[...] = xhat
        rstd_ref[...] = rstd

    tile = pl.BlockSpec((LN_TM, D), lambda i: (i, 0))
    vec = pl.BlockSpec((1, D), lambda i: (0, 0))
    return _call(
        body,
        name="ln_fwd",
        grid=(S // LN_TM,),
        in_specs=[tile, tile, vec, vec],
        out_specs=[tile, tile, tile, pl.BlockSpec((LN_TM, 1), lambda i: (i, 0))],
        out_shape=[jax.ShapeDtypeStruct((S, D), F32), jax.ShapeDtypeStruct((S, D), MXU_DTYPE),
                   jax.ShapeDtypeStruct((S, D), F32), jax.ShapeDtypeStruct((S, 1), F32)],
        semantics=("parallel",),
        operands=(res, branch, g, b),
        side=side,
    )


def _ln_bwd(dy_a, dy_b, xhat, rstd, g, side=None):
    def body(da_ref, db_in_ref, xhat_ref, rstd_ref, g_ref, dp_ref, dpb_ref, dg_ref, db_ref):
        @pl.when(pl.program_id(0) == 0)
        def _():
            dg_ref[...] = jnp.zeros_like(dg_ref)
            db_ref[...] = jnp.zeros_like(db_ref)

        dy = da_ref[...] + ALPHA * db_in_ref[...]
        xhat = xhat_ref[...]
        dp = _ln_input_grad(dy, xhat, rstd_ref[...], g_ref[...])
        dp_ref[...] = dp
        dpb_ref[...] = dp.astype(dpb_ref.dtype)
        dg_ref[...] += _colsum(dy * xhat)
        db_ref[...] += _colsum(dy)

    tile = pl.BlockSpec((LN_TM, D), lambda i: (i, 0))
    vec = pl.BlockSpec((1, D), lambda i: (0, 0))
    return _call(
        body,
        name="ln_bwd",
        grid=(S // LN_TM,),
        in_specs=[tile, tile, tile, pl.BlockSpec((LN_TM, 1), lambda i: (i, 0)), vec],
        out_specs=[tile, tile, vec, vec],
        out_shape=[jax.ShapeDtypeStruct((S, D), F32), jax.ShapeDtypeStruct((S, D), MXU_DTYPE),
                   jax.ShapeDtypeStruct((1, D), F32), jax.ShapeDtypeStruct((1, D), F32)],
        semantics=("arbitrary",),
        operands=(dy_a, dy_b, xhat, rstd, g),
        side=side,
    )


def _ln_loss_bwd(res, branch, g, b, target, side=None):
    def body(res_ref, br_ref, g_ref, b_ref, t_ref, loss_ref, dp_ref, dpb_ref, dg_ref, db_ref):
        @pl.when(pl.program_id(0) == 0)
        def _():
            loss_ref[...] = jnp.zeros_like(loss_ref)
            dg_ref[...] = jnp.zeros_like(dg_ref)
            db_ref[...] = jnp.zeros_like(db_ref)

        xhat, rstd = _ln_stats(ALPHA * res_ref[...] + br_ref[...])
        gv = g_ref[...]
        err = xhat * gv + b_ref[...] - t_ref[...]
        loss_ref[...] += (0.5 / D) * jnp.sum(_colsum(err * err), axis=1, keepdims=True)
        dy = err * (1.0 / D)
        dp = _ln_input_grad(dy, xhat, rstd, gv)
        dp_ref[...] = dp
        dpb_ref[...] = dp.astype(dpb_ref.dtype)
        dg_ref[...] += _colsum(dy * xhat)
        db_ref[...] += _colsum(dy)

    tile = pl.BlockSpec((LN_TM, D), lambda i: (i, 0))
    vec = pl.BlockSpec((1, D), lambda i: (0, 0))
    return _call(
        body,
        name="ln_loss_bwd",
        grid=(S // LN_TM,),
        in_specs=[tile, tile, vec, vec, tile],
        out_specs=[pl.BlockSpec((1, 1), lambda i: (0, 0)), tile, tile, vec, vec],
        out_shape=[jax.ShapeDtypeStruct((1, 1), F32), jax.ShapeDtypeStruct((S, D), F32),
                   jax.ShapeDtypeStruct((S, D), MXU_DTYPE),
                   jax.ShapeDtypeStruct((1, D), F32), jax.ShapeDtypeStruct((1, D), F32)],
        semantics=("arbitrary",),
        operands=(res, branch, g, b, target),
        side=side,
    )


FFN_TC = 256


def _ffn_act_fwd(up, gpre, w, b, side=None):
    tc = FFN_TC

    def body(up_ref, x_ref, w_ref, b_ref, o_ref, xpad_ref):
        xpad_ref[pl.ds(0, PAD), :] = jnp.zeros((PAD, tc), F32)
        xpad_ref[pl.ds(PAD, S), :] = x_ref[...]
        wv = w_ref[...]
        bv = b_ref[...]

        def step(ci, carry):
            r0 = pl.multiple_of(ci * CHUNK, CHUNK)
            taps = _past_taps(xpad_ref, r0, FFN_CONV_W)
            gate = bv + taps[0] * wv[0:1, :] + taps[1] * wv[1:2, :] + taps[2] * wv[2:3, :]
            o_ref[pl.ds(r0, CHUNK), :] = (_gelu(gate)[0] * up_ref[pl.ds(r0, CHUNK), :]).astype(o_ref.dtype)
            return carry

        lax.fori_loop(0, S // CHUNK, step, 0)

    col = pl.BlockSpec((S, tc), lambda j: (0, j))
    return _call(
        body,
        name="ffn_act_fwd",
        grid=(D_FF // tc,),
        in_specs=[col, col, pl.BlockSpec((FFN_CONV_W, tc), lambda j: (0, j)), pl.BlockSpec((1, tc), lambda j: (0, j))],
        out_specs=col,
        out_shape=jax.ShapeDtypeStruct((S, D_FF), MXU_DTYPE),
        scratch_shapes=[pltpu.VMEM((S + PAD, tc), F32)],
        semantics=("parallel",),
        operands=(up, gpre, w, b),
        side=side,
    )


def _ffn_act_bwd(dfin, up, gpre, w, b, side=None):
    tc = FFN_TC
    width = FFN_CONV_W

    def body(df_ref, up_ref, x_ref, w_ref, b_ref, dup_ref, dx_ref, dw_ref, db_ref, xpad_ref, dpad_ref):
        xpad_ref[pl.ds(0, PAD), :] = jnp.zeros((PAD, tc), F32)
        xpad_ref[pl.ds(PAD, S), :] = x_ref[...]
        dpad_ref[pl.ds(S, PAD), :] = jnp.zeros((PAD, tc), F32)
        wv = w_ref[...]
        bv = b_ref[...]

        def gate_grad(ci, acc):
            r0 = pl.multiple_of(ci * CHUNK, CHUNK)
            taps = _past_taps(xpad_ref, r0, width)
            gate = bv + taps[0] * wv[0:1, :] + taps[1] * wv[1:2, :] + taps[2] * wv[2:3, :]
            ge, dge = _gelu(gate)
            df = df_ref[pl.ds(r0, CHUNK), :]
            dup_ref[pl.ds(r0, CHUNK), :] = (df * ge).astype(dup_ref.dtype)
            d = df * up_ref[pl.ds(r0, CHUNK), :] * dge
            dpad_ref[pl.ds(r0, CHUNK), :] = d
            return tuple(acc[k] + _colsum(taps[k] * d) for k in range(width)) + (acc[width] + _colsum(d),)

        zero = jnp.zeros((1, tc), F32)
        acc = lax.fori_loop(0, S // CHUNK, gate_grad, (zero,) * (width + 1))
        for k in range(width):
            dw_ref[k:k + 1, :] = acc[k]
        db_ref[...] = acc[width]

        def input_grad(ci, carry):
            r0 = pl.multiple_of(ci * CHUNK, CHUNK)
            ahead = _future_taps(dpad_ref, r0, width)
            dx = ahead[0] * wv[2:3, :] + ahead[1] * wv[1:2, :] + ahead[2] * wv[0:1, :]
            dx_ref[pl.ds(r0, CHUNK), :] = dx.astype(dx_ref.dtype)
            return carry

        lax.fori_loop(0, S // CHUNK, input_grad, 0)

    col = pl.BlockSpec((S, tc), lambda j: (0, j))
    w_spec = pl.BlockSpec((width, tc), lambda j: (0, j))
    vec = pl.BlockSpec((1, tc), lambda j: (0, j))
    return _call(
        body,
        name="ffn_act_bwd",
        grid=(D_FF // tc,),
        in_specs=[col, col, col, w_spec, vec],
        out_specs=[col, col, w_spec, vec],
        out_shape=[jax.ShapeDtypeStruct((S, D_FF), MXU_DTYPE)] * 2
        + [jax.ShapeDtypeStruct((width, D_FF), F32), jax.ShapeDtypeStruct((1, D_FF), F32)],
        scratch_shapes=[pltpu.VMEM((S + PAD, tc), F32), pltpu.VMEM((S + PAD, tc), F32)],
        semantics=("parallel",),
        operands=(dfin, up, gpre, w, b),
        side=side,
    )


def _adamw_update(w, g, m, v):
    m = ADAM_B1 * m + (1.0 - ADAM_B1) * g
    v = ADAM_B2 * v + (1.0 - ADAM_B2) * (g * g)
    m_hat = m / (1.0 - ADAM_B1 ** ADAM_STEP)
    v_hat = v / (1.0 - ADAM_B2 ** ADAM_STEP)
    delta = -ADAM_LR * (m_hat / (jnp.sqrt(v_hat) + ADAM_EPS) + ADAM_WD * w)
    return delta, m, v


def _add_pairs(send, pair, far_index, *, name):
    _, r_dim, c_dim = send.shape
    tr = r_dim // 4

    def body(far_ref, mine_ref, theirs_ref, o_ref):
        o_ref[...] = (mine_ref[...].astype(F32) + theirs_ref[...].astype(F32)).astype(o_ref.dtype)

    return pl.pallas_call(
        body,
        name=name,
        grid_spec=pltpu.PrefetchScalarGridSpec(
            num_scalar_prefetch=1,
            grid=(3, r_dim // tr),
            in_specs=[pl.BlockSpec((None, tr, c_dim), lambda j, i, far: (far[j], i, 0)),
                      pl.BlockSpec((None, tr, c_dim), lambda j, i, far: (1 + j, i, 0))],
            out_specs=pl.BlockSpec((None, tr, c_dim), lambda j, i, far: (j, i, 0)),
        ),
        out_shape=jax.ShapeDtypeStruct((3, r_dim, c_dim), BF16),
        compiler_params=_cparams("parallel", "parallel"),
    )(far_index, send, pair)


def _reduce_adamw(w, m, v, g_own, pair, far, me, *, tr, name, part=0, earlier=None):
    _, r_dim, c_dim = w.shape
    cp = pair.shape[2]

    def body(me_ref, w_ref, m_ref, v_ref, g_ref, pair_ref, far_ref, *refs):
        grad_ref, delta_ref, nm_ref, nv_ref = refs[-4:]
        g = g_ref[...] + pair_ref[...].astype(F32)
        for j in range(3):
            g = g + far_ref[j].astype(F32)
        delta, nm, nv = _adamw_update(w_ref[...], g, m_ref[...], v_ref[...])
        grad_ref[...] = g
        delta_ref[...] = delta
        nm_ref[...] = nm
        nv_ref[...] = nv

    tile = pl.BlockSpec((None, tr, cp), lambda i, me: (0, i, part))
    if g_own.ndim == 3:
        own_spec = pl.BlockSpec((None, tr, cp), lambda i, me: (me[0], i, 0))
    else:
        own_spec = pl.BlockSpec((tr, cp), lambda i, me: (i, 0))
    earlier = list(earlier or ())
    return pl.pallas_call(
        body,
        name=name,
        grid_spec=pltpu.PrefetchScalarGridSpec(
            num_scalar_prefetch=1,
            grid=(r_dim // tr,),
            in_specs=[tile, tile, tile, own_spec, pl.BlockSpec((None, tr, cp), lambda i, me: (0, i, 0)),
                      pl.BlockSpec((3, tr, cp), lambda i, me: (0, i, 0))]
            + [pl.BlockSpec(memory_space=pl.ANY)] * len(earlier),
            out_specs=[tile] * 4,
        ),
        out_shape=[jax.ShapeDtypeStruct((1, r_dim, c_dim), F32)] * 4,
        input_output_aliases={7 + k: k for k in range(len(earlier))},
        compiler_params=_cparams("parallel"),
    )(me, w, m, v, g_own, pair, far, *earlier)


def _adamw_many(ws, ms, vs, gs):
    n = len(ws)

    def body(*refs):
        for i in range(n):
            delta, nm, nv = _adamw_update(refs[i][...], refs[3 * n + i][...], refs[n + i][...], refs[2 * n + i][...])
            refs[4 * n + i][...] = delta
            refs[5 * n + i][...] = nm
            refs[6 * n + i][...] = nv

    vmem = pl.BlockSpec(memory_space=pltpu.VMEM)
    res = pl.pallas_call(
        body,
        name="adamw_small",
        in_specs=[vmem] * (4 * n),
        out_specs=[vmem] * (3 * n),
        out_shape=[jax.ShapeDtypeStruct(w.shape, F32) for w in ws] * 3,
        compiler_params=pltpu.CompilerParams(vmem_limit_bytes=VMEM_LIMIT),
    )(*ws, *ms, *vs, *gs)
    return res[:n], res[n:2 * n], res[2 * n:]


def _adamw_blocks(w, m, v, g, *, name, side=None):
    per = 2

    def body(w_ref, m_ref, v_ref, g_ref, delta_ref, nm_ref, nv_ref):
        delta, nm, nv = _adamw_update(w_ref[...], g_ref[...], m_ref[...], v_ref[...])
        delta_ref[...] = delta
        nm_ref[...] = nm
        nv_ref[...] = nv

    tile = pl.BlockSpec((1, per) + w.shape[2:], lambda i: (0, i, 0, 0))
    return _call(
        body,
        name=name,
        grid=(w.shape[1] // per,),
        in_specs=[tile] * 4,
        out_specs=[tile] * 3,
        out_shape=[jax.ShapeDtypeStruct(w.shape, F32)] * 3,
        semantics=("parallel",),
        operands=(w, m, v, g),
        side=side,
    )


def _coords():
    return lax.axis_index("x"), lax.axis_index("y"), lax.axis_index("c")


def _flip(coord, bit):
    return 1 - coord if bit else coord


def _relative(k):
    x, y, c = _coords()
    return _flip(x, k & 4), _flip(y, k & 2), _flip(c, k & 1)


def _index(pos):
    return 4 * pos[0] + 2 * pos[1] + pos[2]


FAR = (4, 2, 6)
AG_US_PER_MB = 38.0
RS_US_PER_MB = 46.0
MIN_RIDE_US = 30.0
PAIR_EXCHANGE_US = 20.0
MIN_GATHER_RIDE_US = 22.0
ROW_ALIGN = 32


def _chunks(items, cursor, us, us_per_mb, through=None):
    budget = float("inf") if us is None else us / us_per_mb * 2 ** 20
    names = list(items)
    if through is not None:
        names = names[:names.index(through) + 1]
    chunks = []
    for name in names:
        arr = items[name]
        r_dim, c_dim = arr.shape[-2:]
        row_bytes = c_dim * arr.dtype.itemsize
        while cursor[name] < r_dim and budget > 0:
            rows = r_dim - cursor[name]
            if r_dim > ROW_ALIGN and budget < rows * row_bytes:
                rows = min(rows, max(ROW_ALIGN, int(budget // row_bytes) // ROW_ALIGN * ROW_ALIGN))
            chunks.append((name, cursor[name], rows))
            cursor[name] += rows
            budget -= rows * row_bytes
    return chunks


class _Gather:
    def __init__(self, shards):
        self.shards, self.bufs, self.cursor = {}, {}, {}
        self.add_shards(shards)

    def add_shards(self, shards):
        for n, shard in shards.items():
            self.shards[n], self.bufs[n], self.cursor[n] = shard, None, 0

    def take(self, us=None, through=None):
        if us is not None and us < MIN_GATHER_RIDE_US:
            return None
        chunks = _chunks(self.shards, self.cursor, us, AG_US_PER_MB, through)
        return _GatherSide(self, chunks) if chunks else None

    def get(self, name):
        chunks = _chunks(self.shards, self.cursor, None, AG_US_PER_MB, through=name)
        if chunks:
            _run_side(_GatherSide(self, chunks), "gather_" + name)
        return self.bufs[name]


class _GatherSide:
    SEMS = 8

    def __init__(self, owner, chunks):
        self.owner, self.chunks = owner, chunks
        self.names = list(dict.fromkeys(n for n, _, _ in chunks))
        old = [n for n in self.names if owner.bufs[n] is not None]
        self.operands = [owner.shards[n] for n in self.names] + [owner.bufs[n] for n in old]
        self.out_shape = [jax.ShapeDtypeStruct((N_DEV,) + owner.shards[n].shape, owner.shards[n].dtype)
                          for n in self.names]
        self.aliases = {len(self.names) + i: self.names.index(n) for i, n in enumerate(old)}
        self.sems = [pltpu.SemaphoreType.DMA((self.SEMS * len(chunks),)),
                     pltpu.SemaphoreType.DMA((self.SEMS * len(chunks),)), pltpu.SemaphoreType.DMA((len(chunks),))]

    def _halves(self, ci):
        _, r0, rows = self.chunks[ci]
        if rows % ROW_ALIGN:
            return None
        return (r0, rows // 2), (r0 + rows // 2, rows // 2)

    def _copy(self, ins, outs, sems, ci, s, block, to, rows=None, from_shard=False):
        name, r0, n = self.chunks[ci]
        if rows is not None:
            r0, n = rows
        w = self.names.index(name)
        slot = outs[w].at[_index(block), pl.ds(r0, n)]
        return pltpu.make_async_remote_copy(
            src_ref=ins[w].at[pl.ds(r0, n)] if from_shard else slot, dst_ref=slot,
            send_sem=sems[0].at[self.SEMS * ci + s], recv_sem=sems[1].at[self.SEMS * ci + s],
            device_id=to, device_id_type=MESH)

    def _own(self, ins, outs, sems, ci):
        name, r0, rows = self.chunks[ci]
        w = self.names.index(name)
        return pltpu.make_async_copy(ins[w].at[pl.ds(r0, rows)], outs[w].at[_index(_relative(0)), pl.ds(r0, rows)],
                                     sems[2].at[ci])

    def _pass(self, ins, outs, sems, ci, which):
        source, target = ((4, 2), (2, 4))[which]
        return self._copy(ins, outs, sems, ci, 3 + which, _relative(source), _relative(target),
                          rows=self._halves(ci)[which])

    def start(self, ins, outs, sems):
        me = _relative(0)
        for ci in range(len(self.chunks)):
            self._own(ins, outs, sems, ci).start()
        for ci in range(len(self.chunks)):
            self._copy(ins, outs, sems, ci, 1, me, _relative(4), from_shard=True).start()
            self._copy(ins, outs, sems, ci, 2, me, _relative(2), from_shard=True).start()
            if self._halves(ci) is None:
                self._copy(ins, outs, sems, ci, 3, me, _relative(6), from_shard=True).start()
        for ci in range(len(self.chunks)):
            self._copy(ins, outs, sems, ci, 0, me, _relative(1), from_shard=True).start()

    def mid(self, ins, outs, sems):
        me = _relative(0)
        cut = [ci for ci in range(len(self.chunks)) if self._halves(ci) is not None]
        for ci in cut:
            self._copy(ins, outs, sems, ci, 1, _relative(4), me).wait_recv()
            self._pass(ins, outs, sems, ci, 0).start()
            self._copy(ins, outs, sems, ci, 5, _relative(4), _relative(1)).start()
        for ci in cut:
            self._copy(ins, outs, sems, ci, 2, _relative(2), me).wait_recv()
            self._pass(ins, outs, sems, ci, 1).start()
            self._copy(ins, outs, sems, ci, 6, _relative(2), _relative(1)).start()

    def finish(self, ins, outs, sems):
        me, sibling = _relative(0), _relative(1)
        n = len(self.chunks)
        for ci in range(n):
            if self._halves(ci) is None:
                for s, k in ((1, 4), (2, 2), (3, 6)):
                    self._copy(ins, outs, sems, ci, s, _relative(k), me).wait_recv()
                for j, k in enumerate(FAR):
                    self._copy(ins, outs, sems, ci, 5 + j, _relative(k), sibling).start()
            else:
                h0, h1 = self._halves(ci)
                self._copy(ins, outs, sems, ci, 3, _relative(6), me, rows=h0).wait_recv()
                self._copy(ins, outs, sems, ci, 4, _relative(6), me, rows=h1).wait_recv()
                self._copy(ins, outs, sems, ci, 7, _relative(6), sibling).start()
        for ci in range(n):
            self._copy(ins, outs, sems, ci, 0, sibling, me).wait_recv()
            for j, k in enumerate(FAR):
                self._copy(ins, outs, sems, ci, 5 + j, _relative(k | 1), me).wait_recv()
        for ci in range(n):
            self._copy(ins, outs, sems, ci, 0, me, sibling, from_shard=True).wait_send()
            self._copy(ins, outs, sems, ci, 1, me, _relative(4), from_shard=True).wait_send()
            self._copy(ins, outs, sems, ci, 2, me, _relative(2), from_shard=True).wait_send()
            if self._halves(ci) is None:
                self._copy(ins, outs, sems, ci, 3, me, _relative(6), from_shard=True).wait_send()
            else:
                self._pass(ins, outs, sems, ci, 0).wait_send()
                self._pass(ins, outs, sems, ci, 1).wait_send()
            for j, k in enumerate(FAR):
                self._copy(ins, outs, sems, ci, 5 + j, _relative(k), sibling).wait_send()
            self._own(ins, outs, sems, ci).wait()

    def done(self, results):
        for n, buf in zip(self.names, results):
            self.owner.bufs[n] = buf


class _Scatter:
    def __init__(self, me, far_index):
        self.me, self.far_index = me, far_index
        self.sends, self.owns, self.pairs, self.sums, self.fars = {}, {}, {}, {}, {}
        self.pair_cursor, self.far_cursor = {}, {}

    def add(self, name, send, own):
        self.sends[name] = send
        self.owns[name] = own
        self.pairs[name] = self.fars[name] = None
        self.pair_cursor[name] = 0

    def _rows(self, name):
        return self.sends[name].shape[1]

    def _add_ready_pairs(self):
        for name in self.sends:
            if name not in self.sums and self.pair_cursor[name] == self._rows(name):
                self.sums[name] = _add_pairs(self.sends[name], self.pairs[name], self.far_index, name="pair_" + name)
                self.far_cursor[name] = 0

    def _side(self, us, through=None):
        self._add_ready_pairs()
        names = list(self.sends)
        if through is not None:
            names = names[:names.index(through) + 1]
        pair_chunks = [(n, self.pair_cursor[n], self._rows(n) - self.pair_cursor[n]) for n in names
                       if self.pair_cursor[n] < self._rows(n)]
        for n, _, _ in pair_chunks:
            self.pair_cursor[n] = self._rows(n)
        far_chunks = _chunks(self.sums, self.far_cursor, us, RS_US_PER_MB,
                             through if through in self.sums else None) if self.sums else []
        return _ScatterSide(self, pair_chunks, far_chunks) if pair_chunks or far_chunks else None

    def add_blocks(self, name, blocks32, blocks16):
        self.add(name, blocks16, blocks32)

    def take(self, us):
        return self._side(us) if us >= MIN_RIDE_US else None

    def flush_pairs(self, name):
        side = self._side(PAIR_EXCHANGE_US)
        if side is not None:
            _run_side(side, name)
        self._add_ready_pairs()

    def get(self, name):
        step = 0
        while name not in self.sums or self.far_cursor[name] < self._rows(name):
            _run_side(self._side(None, through=name), "scatter_%s_%d" % (name, step))
            step += 1
        return self.owns[name], self.pairs[name], self.fars[name]


class _ScatterSide:
    TO_SIBLING = (1, 5, 3, 7)

    def __init__(self, owner, pair_chunks, far_chunks):
        self.owner, self.pair_chunks, self.far_chunks = owner, pair_chunks, far_chunks
        self.pair_names = list(dict.fromkeys(n for n, _, _ in pair_chunks))
        self.far_names = list(dict.fromkeys(n for n, _, _ in far_chunks))
        ins = [(owner.sends[n], owner.pairs[n], (4,)) for n in self.pair_names]
        ins += [(owner.sums[n], owner.fars[n], (3,)) for n in self.far_names]
        old = [i for i, (_, buf, _) in enumerate(ins) if buf is not None]
        self.operands = [src for src, _, _ in ins] + [ins[i][1] for i in old]
        self.out_shape = [jax.ShapeDtypeStruct(slots + src.shape[1:], BF16) for src, _, slots in ins]
        self.aliases = {len(ins) + j: i for j, i in enumerate(old)}
        n_pair, n_far = 4 * len(pair_chunks), 3 * len(far_chunks)
        self.sems = [pltpu.SemaphoreType.DMA((max(n_pair, 1),)), pltpu.SemaphoreType.DMA((max(n_pair, 1),)),
                     pltpu.SemaphoreType.DMA((max(n_far, 1),)), pltpu.SemaphoreType.DMA((max(n_far, 1),))]

    def _copies(self, ins, outs, sems):
        copies = []
        for ci, (name, r0, rows) in enumerate(self.pair_chunks):
            w = self.pair_names.index(name)
            for j, k in enumerate(self.TO_SIBLING):
                copies.append(pltpu.make_async_remote_copy(
                    src_ref=ins[w].at[_index(_relative(k)), pl.ds(r0, rows)], dst_ref=outs[w].at[j, pl.ds(r0, rows)],
                    send_sem=sems[0].at[4 * ci + j], recv_sem=sems[1].at[4 * ci + j],
                    device_id=_relative(1), device_id_type=MESH))
        for ci, (name, r0, rows) in enumerate(self.far_chunks):
            w = len(self.pair_names) + self.far_names.index(name)
            for j, k in enumerate(FAR):
                copies.append(pltpu.make_async_remote_copy(
                    src_ref=ins[w].at[j, pl.ds(r0, rows)], dst_ref=outs[w].at[j, pl.ds(r0, rows)],
                    send_sem=sems[2].at[3 * ci + j], recv_sem=sems[3].at[3 * ci + j],
                    device_id=_relative(k), device_id_type=MESH))
        return copies

    def start(self, ins, outs, sems):
        for cp in self._copies(ins, outs, sems):
            cp.start()

    def mid(self, ins, outs, sems):
        pass

    def finish(self, ins, outs, sems):
        for cp in self._copies(ins, outs, sems):
            cp.wait()

    def done(self, results):
        for n, buf in zip(self.pair_names, results):
            self.owner.pairs[n] = buf
        for n, buf in zip(self.far_names, results[len(self.pair_names):]):
            self.owner.fars[n] = buf


class _Joined:
    def __init__(self, sides):
        self.sides = sides
        self.operands, self.out_shape, self.sems, self.aliases, self.spans = [], [], [], {}, []
        for s in sides:
            i0, o0, s0 = len(self.operands), len(self.out_shape), len(self.sems)
            self.operands += list(s.operands)
            self.out_shape += list(s.out_shape)
            self.sems += list(s.sems)
            self.aliases.update({i0 + i: o0 + o for i, o in s.aliases.items()})
            self.spans.append((slice(i0, len(self.operands)), slice(o0, len(self.out_shape)),
                               slice(s0, len(self.sems))))

    def start(self, ins, outs, sems):
        for s, (i, o, m) in zip(self.sides, self.spans):
            s.start(ins[i], outs[o], sems[m])

    def mid(self, ins, outs, sems):
        for s, (i, o, m) in zip(self.sides, self.spans):
            s.mid(ins[i], outs[o], sems[m])

    def finish(self, ins, outs, sems):
        for s, (i, o, m) in zip(self.sides, self.spans):
            s.finish(ins[i], outs[o], sems[m])

    def done(self, results):
        for s, (_, o, _) in zip(self.sides, self.spans):
            s.done(results[o])


def _join(*sides):
    sides = [s for s in sides if s is not None]
    if len(sides) <= 1:
        return sides[0] if sides else None
    return _Joined(sides)


PART_W = 768


def _pack_rows(vecs):
    rows = -(-sum(v.shape[0] for v in vecs) // 8) * 8

    def body(*refs):
        out = refs[-1]
        out[...] = jnp.zeros_like(out)
        r0 = 0
        for v in refs[:-1]:
            k, n = v.shape
            for p in range(-(-n // PART_W)):
                w = min(PART_W, n - PART_W * p)
                out[p, r0:r0 + k, 0:w] = v[:, PART_W * p:PART_W * p + w]
            r0 += k

    vmem = pl.BlockSpec(memory_space=pltpu.VMEM)
    return pl.pallas_call(body, name="pack_small", in_specs=[vmem] * len(vecs), out_specs=vmem,
                          out_shape=jax.ShapeDtypeStruct((N_DEV, rows, PART_W), F32))(*vecs)


def _unpack_rows(packed, shapes):
    def body(packed_ref, *outs):
        r0 = 0
        for o in outs:
            k, n = o.shape
            for p in range(-(-n // PART_W)):
                w = min(PART_W, n - PART_W * p)
                o[:, PART_W * p:PART_W * p + w] = packed_ref[p, r0:r0 + k, 0:w]
            r0 += k

    vmem = pl.BlockSpec(memory_space=pltpu.VMEM)
    return pl.pallas_call(body, name="unpack_small", in_specs=[vmem], out_specs=[vmem] * len(shapes),
                          out_shape=[jax.ShapeDtypeStruct(s, F32) for s in shapes])(packed)


class _PartsToOwners:
    def __init__(self, arrays):
        self.n = len(arrays)
        self.pers = [a.shape[0] // N_DEV for a in arrays]
        self.operands, self.aliases = list(arrays), {}
        self.out_shape = [jax.ShapeDtypeStruct((N_DEV, per) + a.shape[1:], a.dtype) for a, per in zip(arrays, self.pers)]
        self.sems = [pltpu.SemaphoreType.DMA((self.n * (N_DEV - 1),))] * 2

    def _copies(self, ins, outs, sems):
        return [pltpu.make_async_remote_copy(
            src_ref=ins[j].at[pl.ds(self.pers[j] * _index(_relative(k)), self.pers[j])], dst_ref=outs[j].at[k],
            send_sem=sems[0].at[self.n * (k - 1) + j], recv_sem=sems[1].at[self.n * (k - 1) + j],
            device_id=_relative(k), device_id_type=MESH) for k in range(1, N_DEV) for j in range(self.n)]

    def start(self, ins, outs, sems):
        for cp in self._copies(ins, outs, sems):
            cp.start()

    def mid(self, ins, outs, sems):
        pass

    def finish(self, ins, outs, sems):
        for cp in self._copies(ins, outs, sems):
            cp.wait()

    def done(self, results):
        self.stages = list(results)


def _sum_parts(arrays, stages, *, name):
    n = len(arrays)
    pers = [a.shape[0] // N_DEV for a in arrays]

    def body(*refs):
        me = _index(_relative(0))
        for j in range(n):
            acc = refs[j][pl.ds(pers[j] * me, pers[j])]
            for k in range(1, N_DEV):
                acc = acc + refs[n + j][k].astype(F32)
            refs[2 * n + j][...] = acc

    vmem = pl.BlockSpec(memory_space=pltpu.VMEM)
    return pl.pallas_call(body, name=name, in_specs=[vmem] * (2 * n), out_specs=[vmem] * n,
                          out_shape=[jax.ShapeDtypeStruct((per,) + a.shape[1:], F32) for a, per in zip(arrays, pers)],
                          compiler_params=pltpu.CompilerParams(vmem_limit_bytes=VMEM_LIMIT))(*arrays, *stages)


class _PartsToAll:
    def __init__(self, parts):
        self.n = len(parts)
        self.pers = [p.shape[0] for p in parts]
        self.operands, self.aliases = list(parts), {}
        self.out_shape = [jax.ShapeDtypeStruct((N_DEV * p.shape[0],) + p.shape[1:], F32) for p in parts]
        self.sems = [pltpu.SemaphoreType.DMA((self.n * (N_DEV - 1),))] * 2 + [pltpu.SemaphoreType.DMA((self.n,))]

    def _rows(self, outs, j, pos):
        return outs[j].at[pl.ds(self.pers[j] * _index(pos), self.pers[j])]

    def _copy(self, ins, outs, sems, k, j, owner):
        return pltpu.make_async_remote_copy(
            src_ref=ins[j], dst_ref=self._rows(outs, j, owner),
            send_sem=sems[0].at[self.n * (k - 1) + j], recv_sem=sems[1].at[self.n * (k - 1) + j],
            device_id=_relative(k), device_id_type=MESH)

    def _own(self, ins, outs, sems, j):
        return pltpu.make_async_copy(ins[j], self._rows(outs, j, _relative(0)), sems[2].at[j])

    def start(self, ins, outs, sems):
        for j in range(self.n):
            self._own(ins, outs, sems, j).start()
            for k in range(1, N_DEV):
                self._copy(ins, outs, sems, k, j, _relative(0)).start()

    def mid(self, ins, outs, sems):
        pass

    def finish(self, ins, outs, sems):
        for j in range(self.n):
            for k in range(1, N_DEV):
                self._copy(ins, outs, sems, k, j, _relative(k)).wait_recv()
                self._copy(ins, outs, sems, k, j, _relative(0)).wait_send()
            self._own(ins, outs, sems, j).wait()

    def done(self, results):
        self.totals = list(results)


class _AllReduce:
    def __init__(self, name):
        self.name = name

    def begin(self, arrays, wire_dtype=F32):
        self.own = list(arrays)
        self.to_owners = _PartsToOwners([a.astype(wire_dtype) for a in self.own])
        return self.to_owners

    def middle(self):
        self.to_all = _PartsToAll(_sum_parts(self.own, self.to_owners.stages, name="sum_" + self.name))
        return self.to_all

    def end(self):
        return self.to_all.totals


class _SmallSync:
    def __init__(self, vec_names, mat_names):
        self.vec_names, self.mat_names = vec_names, mat_names
        self.mats, self.vecs = _AllReduce("small_mats"), _AllReduce("small_vecs")

    def begin_mats(self, grads):
        return self.mats.begin([_diag_blocks(grads[n]) for n in self.mat_names], BF16)

    def middle_mats(self):
        return self.mats.middle()

    def begin(self, loss, grads):
        vecs = [loss] + [grads[n] for n in self.vec_names]
        self.shapes = [v.shape for v in vecs]
        return self.vecs.begin([_pack_rows(vecs)])

    def middle(self):
        return self.vecs.middle()

    def end(self):
        packed, = self.vecs.end()
        sums = _unpack_rows(packed, self.shapes)
        return sums[0], dict(zip(self.vec_names, sums[1:])), dict(zip(self.mat_names, self.mats.end()))


def _block_diag(w):
    groups = []
    for g in range(N_RNN_GROUPS):
        placed = [jnp.pad(w[4 * g + b], ((RNN_BLOCK_W * b, RNN_BLOCK_W * (3 - b)),) * 2) for b in range(4)]
        groups.append(placed[0] + placed[1] + placed[2] + placed[3])
    return jnp.stack(groups)


def _diag_blocks(wg):
    blocks = []
    for n in range(4 * N_RNN_GROUPS):
        g, at = n // 4, RNN_BLOCK_W * (n % 4)
        blocks.append(wg[g, at:at + RNN_BLOCK_W, at:at + RNN_BLOCK_W])
    return jnp.stack(blocks)


def _heads_major(t, n_heads):
    return t.reshape(S, n_heads, HEAD_DIM).transpose(1, 0, 2)


def _heads_minor(t):
    return t.transpose(1, 0, 2).reshape(S, t.shape[0] * HEAD_DIM)


def _natural(gathered, how):
    n, r, c = gathered.shape
    if how == "rows":
        return gathered.reshape(n * r, c)
    return gathered.transpose(1, 0, 2).reshape(r, n * c)


def _blocks(full, how):
    if how == "rows":
        return full.reshape(N_DEV, full.shape[0] // N_DEV, full.shape[1])
    return full.reshape(full.shape[0], N_DEV, full.shape[1] // N_DEV).transpose(1, 0, 2)


def _cast_many(arrays, side=None):
    steps = 4

    def body(*refs):
        n = len(refs) // 2
        for src, dst in zip(refs[:n], refs[n:]):
            dst[...] = src[...].astype(dst.dtype)

    specs = [pl.BlockSpec((a.shape[0] // steps, a.shape[1]), lambda i: (i, 0)) for a in arrays]
    return _call(
        body,
        name="cast_weights",
        grid=(steps,),
        in_specs=specs,
        out_specs=specs,
        out_shape=[jax.ShapeDtypeStruct(a.shape, MXU_DTYPE) for a in arrays],
        semantics=("parallel",),
        operands=tuple(arrays),
        side=side,
    )


def _forward_backward(x2, xb, target, small, gather, scatter, sync):
    w_in_t = _natural(gather.get("w_in"), "rows")
    proj, projb = _mm(xb, w_in_t, tb=True, tm=S, tn=512, tk=D, out_dtype=(F32, MXU_DTYPE), name="proj",
                      side=gather.take(110))

    qt = projb[:, :OFF_K].T.reshape(N_KV, GROUP, HEAD_DIM, S)
    k2, v2 = projb[:, OFF_K:OFF_V], projb[:, OFF_V:OFF_RX]
    kp = jnp.pad(_heads_major(k2, N_KV), ((0, 0), (BLOCK, 0), (0, 0)))
    vp = jnp.pad(_heads_major(v2, N_KV), ((0, 0), (BLOCK, 0), (0, 0)))
    kt = jnp.pad(k2.T.reshape(N_KV, HEAD_DIM, S), ((0, 0), (0, 0), (BLOCK, 0)))
    vt = jnp.pad(v2.T.reshape(N_KV, HEAD_DIM, S), ((0, 0), (0, 0), (BLOCK, 0)))
    sink_row = jnp.repeat(small["attn_sinks"].reshape(N_KV, 1, GROUP), BLOCK, axis=2)
    ot = _attn_fwd(qt, kp, vt, sink_row, side=gather.take(36)).reshape(D, S)

    rconv_w = _natural(gather.get("rnn_conv_w"), "cols")
    rxc = _conv_fwd(proj, OFF_RX, rconv_w, small["rnn_conv_b"], tc=512, name="rnn_conv_fwd", side=gather.take(18))
    r, i = _lru_gates(rxc, small["lru_wa"], small["lru_wi"], small["lru_ba"], small["lru_bi"], side=gather.take(33))
    h, yrin = _lru_scan_fwd(r, i, rxc, proj, small["lru_lambda"], side=gather.take(53))

    w_ap = _natural(gather.get("w_attn_proj"), "rows")
    w_rp = _natural(gather.get("w_rnn_proj"), "rows")
    y_attn = _mm(ot, w_ap, ta=True, tm=1024, tn=1024, tk=D, name="attn_proj", side=gather.take(22))
    y_rnn = _mm(yrin, w_rp, tm=1024, tn=1024, tk=D_RNN, name="rnn_proj", side=gather.take(27))
    mixin = _gate_fwd(y_attn, y_rnn, proj, small["b_gate"], side=gather.take(25))
    w_out = _natural(gather.get("w_out"), "rows")
    mix = _mm(mixin, w_out, tm=1024, tn=1024, tk=D, name="mix_out", side=gather.take(22))
    x1, x1b, xhat1, rstd1 = _ln_fwd(x2, mix, small["ln1_g"], small["ln1_b"], side=gather.take(23))

    w_up = gather.get("ffn_w_up")
    up = _mm(x1b, w_up, tm=S, tn=768, tk=D, b_block=768, name="ffn_up", side=gather.take(58))
    w_gate = gather.get("ffn_w_gate")
    gpre = _mm(x1b, w_gate, tm=S, tn=768, tk=D, b_block=768, name="ffn_gate", side=gather.take(58))
    fconv_w = _natural(gather.get("ffn_conv_w"), "cols")
    fin = _ffn_act_fwd(up, gpre, fconv_w, small["ffn_conv_b"], side=gather.take())
    w_down = _natural(gather.get("ffn_w_down"), "rows")
    f = _mm(fin, w_down, tm=1024, tn=1024, tk=2048, name="ffn_down")
    loss, dpre2, dpre2b, d_ln2_g, d_ln2_b = _ln_loss_bwd(x1, f, small["ln2_g"], small["ln2_b"], target)

    grads = {"ln2_g": d_ln2_g, "ln2_b": d_ln2_b}
    both = (F32, BF16)
    g32, g16 = _mm(fin, dpre2b, ta=True, tm=1024, tn=1024, tk=S, out_dtype=both, name="d_ffn_w_down")
    scatter.add_blocks("ffn_w_down", _blocks(g32, "rows"), _blocks(g16, "rows"))
    dfin = _mm(dpre2b, w_down, tb=True, tm=1024, tn=1024, tk=D, name="d_fin", side=scatter.take(57))
    dup, dgpre, grads["ffn_conv_w"], grads["ffn_conv_b"] = _ffn_act_bwd(
        dfin, up, gpre, fconv_w, small["ffn_conv_b"], side=scatter.take(85))
    g32, g16 = _mm(x1b, dup, ta=True, tm=1024, tn=768, tk=S, out_dtype=both, out_block=768, name="d_ffn_w_up",
                   side=scatter.take(57))
    scatter.add_blocks("ffn_w_up", g32, g16)
    g32, g16 = _mm(x1b, dgpre, ta=True, tm=1024, tn=768, tk=S, out_dtype=both, out_block=768, name="d_ffn_w_gate",
                   side=scatter.take(56))
    scatter.add_blocks("ffn_w_gate", g32, g16)
    dx1 = _mm(dup, w_up, tb=True, tm=1024, tn=1024, tk=768, b_block=768, name="d_x1_up", side=scatter.take(68))
    dx1 = _mm(dgpre, w_gate, tb=True, tm=1024, tn=1024, tk=768, b_block=768, add=dx1, name="d_x1_gate",
              side=scatter.take(70))
    dpre1, dpre1b, grads["ln1_g"], grads["ln1_b"] = _ln_bwd(dx1, dpre2, xhat1, rstd1, small["ln1_g"],
                                                            side=scatter.take(24))

    g32, g16 = _mm(mixin, dpre1b, ta=True, tm=1024, tn=1024, tk=S, out_dtype=both, name="d_w_out",
                   side=scatter.take(26))
    scatter.add_blocks("w_out", _blocks(g32, "rows"), _blocks(g16, "rows"))
    dmix = _mm(dpre1b, w_out, tb=True, tm=1024, tn=1024, tk=D, name="d_mixin", side=scatter.take(22))
    dproj, dya, dyr, dgl_r, db_a, db_r = _gate_bwd(
        dmix, y_attn, y_rnn, proj, small["b_gate"], side=scatter.take(36),
        window=(jax.ShapeDtypeStruct((S, D_IN), MXU_DTYPE), OFF_GA))
    grads["b_gate"] = jnp.concatenate([db_a, db_r], axis=1)
    g32, g16 = _mm(ot, dya, tm=1024, tn=1024, tk=S, out_dtype=both, name="d_w_attn_proj", side=scatter.take(38))
    scatter.add_blocks("w_attn_proj", _blocks(g32, "rows"), _blocks(g16, "rows"))
    g32, g16 = _mm(yrin, dyr, ta=True, tm=1280, tn=1024, tk=S, out_dtype=both, name="d_w_rnn_proj",
                   side=scatter.take(27))
    scatter.add_blocks("w_rnn_proj", _blocks(g32, "rows"), _blocks(g16, "rows"))
    dot_ = _mm(w_ap, dya, tb=True, tm=1024, tn=1024, tk=D, out_dtype=MXU_DTYPE, name="d_o", side=scatter.take(22))
    dyrin = _mm(dyr, w_rp, tb=True, tm=1024, tn=1280, tk=D, name="d_yrin", side=scatter.take(27))

    dproj, dzr, dzi, drxc_in, grads["lru_ba"], grads["lru_bi"], grads["lru_lambda"] = _lru_scan_bwd(
        dyrin, proj, h, r, i, rxc, small["lru_lambda"], side=scatter.take(94), window=(dproj, OFF_RY))
    grads["lru_wa"], grads["lru_wi"] = _lru_gate_wgrad(rxc, dzr, dzi, side=scatter.take(22))
    drxc = _lru_gate_xgrad(dzr, dzi, small["lru_wa"], small["lru_wi"], drxc_in, side=scatter.take(33))
    dproj, grads["rnn_conv_w"], grads["rnn_conv_b"] = _conv_bwd(
        drxc, proj, OFF_RX, rconv_w, tc=512, name="rnn_conv_bwd", side=scatter.take(30), window=(dproj, OFF_RX))

    dqt, dk, dv, dsink = _attn_bwd(qt, kp, kt, vp, sink_row, dot_.reshape(N_KV, GROUP, HEAD_DIM, S),
                                   side=_join(scatter.take(50), sync.begin_mats(grads)))
    grads["attn_sinks"] = dsink.reshape(1, N_KV * GROUP)
    for col0, piece in ((0, dqt.reshape(D, S).T), (OFF_K, _heads_minor(dk[:, BLOCK:, :]).astype(MXU_DTYPE)),
                        (OFF_V, _heads_minor(dv[:, BLOCK:, :]).astype(MXU_DTYPE)), (OFF_GR, dgl_r)):
        dproj = lax.dynamic_update_slice(dproj, piece, (0, col0))
    for part in range(W_IN_PARTS):
        cols = slice(part * (D // W_IN_PARTS), (part + 1) * (D // W_IN_PARTS))
        if part == 0:
            side = _join(scatter.take(55), sync.middle_mats(), sync.begin(loss, grads))
        else:
            side = scatter.take(68)
        g32, g16 = _mm(dproj, xb[:, cols], ta=True, tm=512, tn=D // W_IN_PARTS, tk=S, out_dtype=both,
                       name="d_w_in_%d" % part, side=side)
        scatter.add_blocks("w_in_%d" % part, _blocks(g32, "rows"), _blocks(g16, "rows"))
        scatter.flush_pairs("pairs_w_in_%d" % part)
    dx = _mm(dproj, w_in_t, tm=1024, tn=1024, tk=512, add=dpre1, add_scale=ALPHA, name="d_x",
             side=_join(scatter.take(400), sync.middle()))
    return dx


SHARDED = (
    ("w_in", "cols", 368), ("w_attn_proj", "rows", 32), ("w_rnn_proj", "rows", 32), ("w_out", "rows", 32),
    ("ffn_w_up", "cols", 128), ("ffn_w_gate", "cols", 128), ("ffn_w_down", "rows", 64),
)
SMALL_REPLICATED = ("b_gate", "rnn_conv_b", "lru_wa", "lru_ba", "lru_wi", "lru_bi", "lru_lambda", "attn_sinks",
                    "ln1_g", "ln1_b", "ffn_conv_b", "ln2_g", "ln2_b")
SMALL_SHARDED = ("rnn_conv_w", "ffn_conv_w")
SMALL_MATS = ("lru_wa", "lru_wi")
W_IN_PARTS = 2
WEIGHTS = ("w_in", "b_gate", "rnn_conv_w", "rnn_conv_b", "lru_wa", "lru_ba", "lru_wi", "lru_bi", "lru_lambda",
           "attn_sinks", "w_attn_proj", "w_rnn_proj", "w_out", "ln1_g", "ln1_b", "ffn_w_up", "ffn_w_gate",
           "ffn_conv_w", "ffn_conv_b", "ffn_w_down", "ln2_g", "ln2_b")


def kernel(x, w_in, b_gate, rnn_conv_w, rnn_conv_b, lru_wa, lru_ba, lru_wi, lru_bi, lru_lambda, attn_sinks, w_attn_proj, w_rnn_proj, w_out, ln1_g, ln1_b, ffn_w_up, ffn_w_gate, ffn_conv_w, ffn_conv_b, ffn_w_down, ln2_g, ln2_b, loss_target, m_w_in, m_b_gate, m_rnn_conv_w, m_rnn_conv_b, m_lru_wa, m_lru_ba, m_lru_wi, m_lru_bi, m_lru_lambda, m_attn_sinks, m_w_attn_proj, m_w_rnn_proj, m_w_out, m_ln1_g, m_ln1_b, m_ffn_w_up, m_ffn_w_gate, m_ffn_conv_w, m_ffn_conv_b, m_ffn_w_down, m_ln2_g, m_ln2_b, v_w_in, v_b_gate, v_rnn_conv_w, v_rnn_conv_b, v_lru_wa, v_lru_ba, v_lru_wi, v_lru_bi, v_lru_lambda, v_attn_sinks, v_w_attn_proj, v_w_rnn_proj, v_w_out, v_ln1_g, v_ln1_b, v_ffn_w_up, v_ffn_w_gate, v_ffn_conv_w, v_ffn_conv_b, v_ffn_w_down, v_ln2_g, v_ln2_b):
    given = dict(locals())
    wsh = {n: given[n][0] for n in WEIGHTS}
    msh = {n: given["m_" + n][0] for n in WEIGHTS}
    vsh = {n: given["v_" + n][0] for n in WEIGHTS}
    m_given = {n: given["m_" + n] for n in WEIGHTS}
    v_given = {n: given["v_" + n] for n in WEIGHTS}
    me = 4 * lax.axis_index("x") + 2 * lax.axis_index("y") + lax.axis_index("c")

    order = ("w_in", "rnn_conv_w", "ffn_conv_w", "w_attn_proj", "w_rnn_proj", "w_out", "ffn_w_up", "ffn_w_gate",
             "ffn_w_down")
    gather = _Gather({"w_in": wsh["w_in"].T.astype(MXU_DTYPE), **{n: wsh[n] for n in order[1:3]}})
    *casts, xb = _cast_many([wsh[n] for n in order[3:]] + [x[0]], side=gather.take(through="ffn_conv_w"))
    gather.add_shards(dict(zip(order[3:], casts)))
    small = {n: given[n] for n in SMALL_REPLICATED}
    small["lru_wa"] = _block_diag(wsh["lru_wa"])
    small["lru_wi"] = _block_diag(wsh["lru_wi"])
    scatter = _Scatter(me, jnp.stack([_index(_relative(k)) for k in FAR]).astype(jnp.int32))

    vec_names = tuple(n for n in SMALL_REPLICATED if n not in SMALL_MATS) + SMALL_SHARDED
    sync = _SmallSync(vec_names, SMALL_MATS)
    dx = _forward_backward(x[0], xb, loss_target[0], small, gather, scatter, sync)

    loss_total, g_small, mat_sums = sync.end()
    loss_total = loss_total.reshape(())
    for n in SMALL_SHARDED:
        width = wsh[n].shape[1]
        g_small[n] = lax.dynamic_slice_in_dim(g_small[n], me * width, width, axis=1)
    g_small = {n: g_small[n].reshape(given[n].shape) for n in vec_names}
    out = {}
    results = _adamw_many(*[[d[n] for n in vec_names] for d in (given, m_given, v_given, g_small)])
    for n, delta, nm, nv in zip(vec_names, *results):
        out[n] = (g_small[n], delta, nm, nv)
    for n in SMALL_MATS:
        g = mat_sums[n].reshape(given[n].shape)
        out[n] = (g, *_adamw_blocks(given[n], m_given[n], v_given[n], g, name="adamw_" + n))

    tile_rows = {n: tr for n, _, tr in SHARDED}
    me1 = me.reshape(1).astype(jnp.int32)
    res = None
    for n in list(scatter.sends):
        own, pair, far = scatter.get(n)
        if n.startswith("w_in_"):
            part = int(n[len("w_in_"):])
            w_t, m_t, v_t = (a["w_in"].transpose(0, 2, 1) for a in (given, m_given, v_given))
            res = _reduce_adamw(w_t, m_t, v_t, own, pair, far, me1, tr=tile_rows["w_in"], name="adamw_" + n,
                                part=part, earlier=res if part else None)
            out["w_in"] = tuple(r.transpose(0, 2, 1) for r in res)
        else:
            out[n] = tuple(_reduce_adamw(given[n], m_given[n], v_given[n], own, pair, far, me1, tr=tile_rows[n],
                                         name="adamw_" + n))

    outputs = [loss_total, dx[None]]
    for kind in range(4):
        outputs += [out[n][kind] for n in WEIGHTS]
    return tuple(outputs)
```

```python
import math

import jax
import jax.numpy as jnp
from jax import lax
from jax.experimental import pallas as pl
from jax.experimental.pallas import tpu as pltpu

F32 = jnp.float32
BF16 = jnp.bfloat16
MXU_DTYPE = jnp.bfloat16

N_DEV = 8
S = 2048
D = 2048
HEAD_DIM = 64
N_KV = 4
GROUP = 8
BLOCK = 128
D_KV = N_KV * HEAD_DIM
D_RNN = 2560
RNN_GROUP = 640
N_RNN_GROUPS = D_RNN // RNN_GROUP
RNN_BLOCK_W = 160
RNN_CONV_W = 4
LRU_C = 8.0
D_FF = 6144
FFN_CONV_W = 3
D_IN = 11776
OFF_K = 2048
OFF_V = 2304
OFF_RX = 2560
OFF_RY = 5120
OFF_GA = 7680
OFF_GR = 9728
LN_EPS = 1e-5
ALPHA = 2.0 ** 0.25
ADAM_LR = 0.001
ADAM_B1 = 0.9
ADAM_B2 = 0.999
ADAM_EPS = 1e-08
ADAM_WD = 0.01
ADAM_STEP = 10
NEG = -1e30
VMEM_LIMIT = 56 * 1024 * 1024
MID_RIDE_TENTHS = 6
MESH = pl.DeviceIdType.MESH
GELU_C = math.sqrt(2.0 / math.pi)


def _cparams(*sem):
    return pltpu.CompilerParams(dimension_semantics=sem or None, vmem_limit_bytes=VMEM_LIMIT)


def _call(body, *, name, grid, in_specs, out_specs, out_shape, operands, semantics, scratch_shapes=(), side=None,
          window=None):
    single = not isinstance(out_shape, (list, tuple))
    out_shape = [out_shape] if single else list(out_shape)
    out_specs = [out_specs] if single else list(out_specs)
    in_specs = list(in_specs)
    operands = tuple(operands)
    scratch_shapes = list(scratch_shapes)
    hbm = pl.BlockSpec(memory_space=pltpu.HBM)
    aliases = {}
    if window is not None:
        whole, col0 = window
        block, index_map = out_specs[0].block_shape, out_specs[0].index_map
        assert col0 % block[1] == 0 and out_shape[0].dtype == whole.dtype
        out_specs[0] = pl.BlockSpec(block, lambda *g: (index_map(*g)[0], index_map(*g)[1] + col0 // block[1]))
        out_shape[0] = jax.ShapeDtypeStruct(whole.shape, whole.dtype)
        if not isinstance(whole, jax.ShapeDtypeStruct):
            aliases[len(in_specs)] = 0
            in_specs.append(hbm)
            operands += (whole,)
            compute, n_read = body, len(in_specs) - 1

            def body(*refs):
                compute(*refs[:n_read], *refs[n_read + 1:])

    if side is None:
        res = pl.pallas_call(
            body, name=name, grid=grid, in_specs=in_specs, out_specs=out_specs, out_shape=out_shape,
            scratch_shapes=scratch_shapes, input_output_aliases=aliases,
            compiler_params=_cparams(*semantics))(*operands)
        return res[0] if single else res
    n_in, n_out, n_scr = len(in_specs), len(out_shape), len(scratch_shapes)
    s_in, s_out = len(side.operands), len(side.out_shape)
    steps = math.prod(grid)
    mid_step = (steps * MID_RIDE_TENTHS) // 10

    def with_copies(*refs):
        core_in, side_in = refs[:n_in], refs[n_in:n_in + s_in]
        o0 = n_in + s_in
        core_out, side_out = refs[o0:o0 + n_out], refs[o0 + n_out:o0 + n_out + s_out]
        c0 = o0 + n_out + s_out
        core_scr, sems = refs[c0:c0 + n_scr], refs[c0 + n_scr:]
        step = 0
        for d, size in enumerate(grid):
            step = step * size + pl.program_id(d)

        @pl.when(step == 0)
        def _():
            side.start(side_in, side_out, sems)

        body(*core_in, *core_out, *core_scr)

        @pl.when(step == mid_step)
        def _():
            side.mid(side_in, side_out, sems)

        @pl.when(step == steps - 1)
        def _():
            side.finish(side_in, side_out, sems)

    res = pl.pallas_call(
        with_copies, name=name, grid=grid,
        in_specs=in_specs + [hbm] * s_in, out_specs=out_specs + [hbm] * s_out,
        out_shape=out_shape + list(side.out_shape),
        scratch_shapes=scratch_shapes + list(side.sems),
        input_output_aliases={**aliases, **{n_in + i: n_out + o for i, o in side.aliases.items()}},
        compiler_params=_cparams(*(("arbitrary",) * len(grid))))(*operands, *side.operands)
    side.done(res[n_out:])
    return res[0] if single else res[:n_out]


def _run_side(side, name):
    def body(*refs):
        s_in, s_out = len(side.operands), len(side.out_shape)
        side.start(refs[:s_in], refs[s_in:s_in + s_out], refs[s_in + s_out:])
        side.mid(refs[:s_in], refs[s_in:s_in + s_out], refs[s_in + s_out:])
        side.finish(refs[:s_in], refs[s_in:s_in + s_out], refs[s_in + s_out:])

    hbm = pl.BlockSpec(memory_space=pltpu.HBM)
    res = pl.pallas_call(
        body, name=name, in_specs=[hbm] * len(side.operands), out_specs=[hbm] * len(side.out_shape),
        out_shape=list(side.out_shape), scratch_shapes=list(side.sems),
        input_output_aliases=dict(side.aliases))(*side.operands)
    side.done(res)


def _gelu(x):
    x2 = x * x
    t = jnp.tanh(GELU_C * (x + 0.044715 * x * x2))
    g = 0.5 * x * (1.0 + t)
    dg = 0.5 * (1.0 + t) + 0.5 * x * (1.0 - t * t) * (GELU_C * (1.0 + 3.0 * 0.044715 * x2))
    return g, dg


def _sigmoid(x):
    return 1.0 / (1.0 + jnp.exp(-x))


def _softplus(x):
    z = jnp.exp(-jnp.abs(x))
    small = z * (1.0 - z * (0.5 - z * (1.0 / 3.0 - 0.25 * z)))
    return jnp.maximum(x, 0.0) + jnp.where(z < 0.02, small, jnp.log(1.0 + z))


def _one_minus_exp(x):
    series = -x * (1.0 + x * (0.5 + x * (1.0 / 6.0 + x * (1.0 / 24.0))))
    return jnp.where(x > -0.03, series, 1.0 - jnp.exp(x))


def _colsum(v):
    return jnp.sum(v, axis=0, keepdims=True)


def _mm(a, b, *, tm, tn, tk, name, ta=False, tb=False, out_dtype=F32, b_block=None, out_block=None, add=None,
        add_scale=1.0, side=None):
    out_dtypes = out_dtype if isinstance(out_dtype, tuple) else (out_dtype,)
    if ta:
        k_dim, m_dim = a.shape
    else:
        m_dim, k_dim = a.shape
    if b_block is None:
        n_dim = b.shape[0] if tb else b.shape[1]
    else:
        n_dim = b.shape[1] if tb else b.shape[0] * b_block
    assert m_dim % tm == 0 and n_dim % tn == 0 and k_dim % tk == 0, (name, m_dim, n_dim, k_dim)
    nk = k_dim // tk
    dims = (((0 if ta else 1,), (1 if tb else 0,)), ((), ()))
    has_add = add is not None

    def body(*refs):
        a_ref, b_ref = refs[0], refs[1]
        add_ref = refs[2] if has_add else None
        first_out = 3 if has_add else 2
        o_refs = refs[first_out:first_out + len(out_dtypes)]

        def product():
            return lax.dot_general(a_ref[...].astype(MXU_DTYPE), b_ref[...].astype(MXU_DTYPE), dims,
                                   preferred_element_type=F32)

        def finish(acc):
            if has_add:
                acc = acc + add_scale * add_ref[...]
            for o_ref in o_refs:
                o_ref[...] = acc.astype(o_ref.dtype)

        if nk == 1:
            finish(product())
        else:
            acc_ref = refs[-1]
            k = pl.program_id(2)

            @pl.when(k == 0)
            def _():
                acc_ref[...] = jnp.zeros_like(acc_ref)

            acc_ref[...] += product()

            @pl.when(k == nk - 1)
            def _():
                finish(acc_ref[...])

    if ta:
        a_spec = pl.BlockSpec((tk, tm), lambda i, j, k: (k, i))
    else:
        a_spec = pl.BlockSpec((tm, tk), lambda i, j, k: (i, k))
    if b_block is None:
        if tb:
            b_spec = pl.BlockSpec((tn, tk), lambda i, j, k: (j, k))
        else:
            b_spec = pl.BlockSpec((tk, tn), lambda i, j, k: (k, j))
    elif tb:
        assert b_block % tk == 0
        b_spec = pl.BlockSpec((None, tn, tk), lambda i, j, k: ((k * tk) // b_block, j, ((k * tk) % b_block) // tk))
    else:
        assert b_block % tn == 0
        b_spec = pl.BlockSpec((None, tk, tn), lambda i, j, k: ((j * tn) // b_block, k, ((j * tn) % b_block) // tn))
    in_specs = [a_spec, b_spec]
    operands = [a, b]
    if has_add:
        in_specs.append(pl.BlockSpec((tm, tn), lambda i, j, k: (i, j)))
        operands.append(add)
    if out_block is None:
        out_spec = pl.BlockSpec((tm, tn), lambda i, j, k: (i, j))
        out_dims = (m_dim, n_dim)
    else:
        assert out_block % tn == 0
        out_spec = pl.BlockSpec((None, tm, tn), lambda i, j, k: ((j * tn) // out_block, i, ((j * tn) % out_block) // tn))
        out_dims = (n_dim // out_block, m_dim, out_block)
    res = _call(
        body,
        name=name,
        grid=(m_dim // tm, n_dim // tn, nk),
        in_specs=in_specs,
        out_specs=[out_spec] * len(out_dtypes),
        out_shape=[jax.ShapeDtypeStruct(out_dims, dt) for dt in out_dtypes],
        scratch_shapes=[pltpu.VMEM((tm, tn), F32)] if nk > 1 else [],
        semantics=("parallel", "parallel", "arbitrary"),
        operands=tuple(operands),
        side=side,
    )
    return res if isinstance(out_dtype, tuple) else res[0]


def _attn_bias(bias_ref, h):
    key = lax.broadcasted_iota(jnp.int32, (2 * BLOCK, GROUP * BLOCK), 0)
    col = lax.broadcasted_iota(jnp.int32, (2 * BLOCK, GROUP * BLOCK), 1)
    dist = BLOCK + (col & (BLOCK - 1)) - key
    head = h * GROUP + (col >> 7) + 1
    slope = jnp.exp(head.astype(F32) * (-0.25 * math.log(2.0)))
    bias = jnp.where((dist >= 0) & (dist < BLOCK), -slope * dist.astype(F32), NEG)
    bias_ref[1] = bias
    bias_ref[0] = jnp.where(key < BLOCK, NEG, bias)


def _attn_probs(kb, qt, bias, sink):
    s = jnp.dot(kb, qt, preferred_element_type=F32) * (HEAD_DIM ** -0.5) + bias
    m = jnp.maximum(jnp.max(s, axis=0, keepdims=True), sink)
    e = jnp.exp(s - m)
    e_sink = jnp.exp(sink - m)
    inv = 1.0 / (jnp.sum(e, axis=0, keepdims=True) + e_sink)
    return e * inv, e_sink * inv


def _heads_on_lanes(ref, r0):
    return jnp.concatenate([ref[g, :, pl.ds(r0, BLOCK)] for g in range(GROUP)], axis=1)


def _attn_fwd(qt, kp, vt, sink_row, side=None):
    cols = GROUP * BLOCK

    def body(q_ref, k_ref, vt_ref, sink_ref, o_ref, bias_ref):
        _attn_bias(bias_ref, pl.program_id(0))
        sink = sink_ref[...]

        def step(n, carry):
            r0 = pl.multiple_of(n * BLOCK, BLOCK)
            p, _ = _attn_probs(k_ref[pl.ds(r0, 2 * BLOCK), :], _heads_on_lanes(q_ref, r0),
                               bias_ref[jnp.minimum(n, 1)], sink)
            o = jnp.dot(vt_ref[:, pl.ds(r0, 2 * BLOCK)], p.astype(MXU_DTYPE), preferred_element_type=F32)
            for g in range(GROUP):
                o_ref[g, :, pl.ds(r0, BLOCK)] = o[:, g * BLOCK:(g + 1) * BLOCK].astype(o_ref.dtype)
            return carry

        lax.fori_loop(0, S // BLOCK, step, 0)

    hm = pl.BlockSpec((None, GROUP, HEAD_DIM, S), lambda h: (h, 0, 0, 0))
    return _call(
        body,
        name="attn_fwd",
        grid=(N_KV,),
        in_specs=[
            hm,
            pl.BlockSpec((None, BLOCK + S, HEAD_DIM), lambda h: (h, 0, 0)),
            pl.BlockSpec((None, HEAD_DIM, BLOCK + S), lambda h: (h, 0, 0)),
            pl.BlockSpec((None, 1, cols), lambda h: (h, 0, 0)),
        ],
        out_specs=hm,
        out_shape=jax.ShapeDtypeStruct((N_KV, GROUP, HEAD_DIM, S), MXU_DTYPE),
        scratch_shapes=[pltpu.VMEM((2, 2 * BLOCK, cols), F32)],
        semantics=("parallel",),
        operands=(qt, kp, vt, sink_row),
        side=side,
    )


def _attn_bwd(qt, kp, kt, vp, sink_row, dot_, side=None):
    cols = GROUP * BLOCK

    def body(q_ref, k_ref, kt_ref, v_ref, sink_ref, do_ref, dq_ref, dk_ref, dv_ref, dsink_ref, bias_ref):
        _attn_bias(bias_ref, pl.program_id(0))
        sink = sink_ref[...]
        dk_ref[...] = jnp.zeros_like(dk_ref)
        dv_ref[...] = jnp.zeros_like(dv_ref)
        nt = (((1,), (1,)), ((), ()))

        def step(n, sink_acc):
            r0 = pl.multiple_of(n * BLOCK, BLOCK)
            band = pl.ds(r0, 2 * BLOCK)
            qn = _heads_on_lanes(q_ref, r0)
            don = _heads_on_lanes(do_ref, r0)
            p, p_sink = _attn_probs(k_ref[band, :], qn, bias_ref[jnp.minimum(n, 1)], sink)
            dp = jnp.dot(v_ref[band, :], don, preferred_element_type=F32)
            delta = jnp.sum(p * dp, axis=0, keepdims=True)
            ds = (p * (dp - delta) * (HEAD_DIM ** -0.5)).astype(MXU_DTYPE)
            dq = jnp.dot(kt_ref[:, band], ds, preferred_element_type=F32)
            for g in range(GROUP):
                dq_ref[g, :, pl.ds(r0, BLOCK)] = dq[:, g * BLOCK:(g + 1) * BLOCK].astype(dq_ref.dtype)
            dk_ref[band, :] += lax.dot_general(ds, qn, nt, preferred_element_type=F32)
            dv_ref[band, :] += lax.dot_general(p.astype(MXU_DTYPE), don, nt, preferred_element_type=F32)
            return sink_acc - p_sink * delta

        sink_acc = lax.fori_loop(0, S // BLOCK, step, jnp.zeros((1, cols), F32))
        for g in range(GROUP):
            dsink_ref[g:g + 1, :] = jnp.sum(sink_acc[:, g * BLOCK:(g + 1) * BLOCK], axis=1, keepdims=True)

    hm = pl.BlockSpec((None, GROUP, HEAD_DIM, S), lambda h: (h, 0, 0, 0))
    kv = pl.BlockSpec((None, BLOCK + S, HEAD_DIM), lambda h: (h, 0, 0))
    return _call(
        body,
        name="attn_bwd",
        grid=(N_KV,),
        in_specs=[hm, kv, pl.BlockSpec((None, HEAD_DIM, BLOCK + S), lambda h: (h, 0, 0)), kv,
                  pl.BlockSpec((None, 1, cols), lambda h: (h, 0, 0)), hm],
        out_specs=[hm, kv, kv, pl.BlockSpec((None, GROUP, 1), lambda h: (h, 0, 0))],
        out_shape=[
            jax.ShapeDtypeStruct((N_KV, GROUP, HEAD_DIM, S), MXU_DTYPE),
            jax.ShapeDtypeStruct((N_KV, BLOCK + S, HEAD_DIM), F32),
            jax.ShapeDtypeStruct((N_KV, BLOCK + S, HEAD_DIM), F32),
            jax.ShapeDtypeStruct((N_KV, GROUP, 1), F32),
        ],
        scratch_shapes=[pltpu.VMEM((2, 2 * BLOCK, cols), F32)],
        semantics=("parallel",),
        operands=(qt, kp, kt, vp, sink_row, dot_),
        side=side,
    )


PAD = 8
CHUNK = 256


def _past_taps(xpad_ref, r0, width):
    ext = xpad_ref[pl.ds(r0, CHUNK + PAD), :]
    taps = []
    for k in range(width):
        back = width - 1 - k
        taps.append((ext if back == 0 else pltpu.roll(ext, back, 0))[PAD:, :])
    return taps


def _future_taps(xpad_ref, r0, width):
    ext = xpad_ref[pl.ds(r0, CHUNK + PAD), :]
    taps = []
    for ahead in range(width):
        taps.append((ext if ahead == 0 else pltpu.roll(ext, CHUNK + PAD - ahead, 0))[:CHUNK, :])
    return taps


def _conv_fwd(src, col0, w, b, *, tc, name, side=None):
    width, c_dim = w.shape

    def body(x_ref, w_ref, b_ref, o_ref, xpad_ref):
        xpad_ref[pl.ds(0, PAD), :] = jnp.zeros((PAD, tc), F32)
        xpad_ref[pl.ds(PAD, S), :] = x_ref[...]
        wv = w_ref[...]
        bv = b_ref[...]

        def step(ci, carry):
            r0 = pl.multiple_of(ci * CHUNK, CHUNK)
            taps = _past_taps(xpad_ref, r0, width)
            y = bv + taps[0] * wv[0:1, :]
            for k in range(1, width):
                y = y + taps[k] * wv[k:k + 1, :]
            o_ref[pl.ds(r0, CHUNK), :] = y
            return carry

        lax.fori_loop(0, S // CHUNK, step, 0)

    return _call(
        body,
        name=name,
        grid=(c_dim // tc,),
        in_specs=[
            pl.BlockSpec((S, tc), lambda j: (0, col0 // tc + j)),
            pl.BlockSpec((width, tc), lambda j: (0, j)),
            pl.BlockSpec((1, tc), lambda j: (0, j)),
        ],
        out_specs=pl.BlockSpec((S, tc), lambda j: (0, j)),
        out_shape=jax.ShapeDtypeStruct((S, c_dim), F32),
        scratch_shapes=[pltpu.VMEM((S + PAD, tc), F32)],
        semantics=("parallel",),
        operands=(src, w, b),
        side=side,
    )


def _conv_bwd(dy, src, col0, w, *, tc, name, side=None, window=None):
    width, c_dim = w.shape

    def body(dy_ref, x_ref, w_ref, dx_ref, dw_ref, db_ref, xpad_ref, dpad_ref):
        xpad_ref[pl.ds(0, PAD), :] = jnp.zeros((PAD, tc), F32)
        xpad_ref[pl.ds(PAD, S), :] = x_ref[...]
        dpad_ref[pl.ds(0, S), :] = dy_ref[...]
        dpad_ref[pl.ds(S, PAD), :] = jnp.zeros((PAD, tc), F32)
        wv = w_ref[...]

        def step(ci, acc):
            r0 = pl.multiple_of(ci * CHUNK, CHUNK)
            past = _past_taps(xpad_ref, r0, width)
            ahead = _future_taps(dpad_ref, r0, width)
            d = ahead[0]
            dx = d * wv[width - 1:width, :]
            for j in range(1, width):
                dx = dx + ahead[j] * wv[width - 1 - j:width - j, :]
            dx_ref[pl.ds(r0, CHUNK), :] = dx.astype(dx_ref.dtype)
            return tuple(acc[k] + _colsum(past[k] * d) for k in range(width)) + (acc[width] + _colsum(d),)

        zero = jnp.zeros((1, tc), F32)
        acc = lax.fori_loop(0, S // CHUNK, step, (zero,) * (width + 1))
        for k in range(width):
            dw_ref[k:k + 1, :] = acc[k]
        db_ref[...] = acc[width]

    return _call(
        body,
        name=name,
        grid=(c_dim // tc,),
        in_specs=[
            pl.BlockSpec((S, tc), lambda j: (0, j)),
            pl.BlockSpec((S, tc), lambda j: (0, col0 // tc + j)),
            pl.BlockSpec((width, tc), lambda j: (0, j)),
        ],
        out_specs=[
            pl.BlockSpec((S, tc), lambda j: (0, j)),
            pl.BlockSpec((width, tc), lambda j: (0, j)),
            pl.BlockSpec((1, tc), lambda j: (0, j)),
        ],
        out_shape=[
            jax.ShapeDtypeStruct((S, c_dim), MXU_DTYPE),
            jax.ShapeDtypeStruct((width, c_dim), F32),
            jax.ShapeDtypeStruct((1, c_dim), F32),
        ],
        scratch_shapes=[pltpu.VMEM((S + PAD, tc), F32), pltpu.VMEM((S + PAD, tc), F32)],
        semantics=("parallel",),
        operands=(dy, src, w),
        side=side,
        window=window,
    )


SCAN_TC = 256


def _lru_gates(rxc, wa, wi, ba, bi, side=None):
    tm = 512

    def body(x_ref, wa_ref, wi_ref, ba_ref, bi_ref, r_ref, i_ref):
        xv = x_ref[...].astype(MXU_DTYPE)
        r_ref[...] = _sigmoid(jnp.dot(xv, wa_ref[...].astype(MXU_DTYPE), preferred_element_type=F32) + ba_ref[...])
        i_ref[...] = _sigmoid(jnp.dot(xv, wi_ref[...].astype(MXU_DTYPE), preferred_element_type=F32) + bi_ref[...])

    x_spec = pl.BlockSpec((tm, RNN_GROUP), lambda g, i: (i, g))
    w_spec = pl.BlockSpec((None, RNN_GROUP, RNN_GROUP), lambda g, i: (g, 0, 0))
    b_spec = pl.BlockSpec((1, RNN_GROUP), lambda g, i: (0, g))
    return _call(
        body,
        name="lru_gates",
        grid=(N_RNN_GROUPS, S // tm),
        in_specs=[x_spec, w_spec, w_spec, b_spec, b_spec],
        out_specs=[x_spec, x_spec],
        out_shape=[jax.ShapeDtypeStruct((S, D_RNN), F32)] * 2,
        semantics=("parallel", "parallel"),
        operands=(rxc, wa, wi, ba, bi),
        side=side,
    )


def _scan_down(a, u, row):
    for d in (1, 2, 4):
        a_s = jnp.where(row >= d, pltpu.roll(a, d, 0), 1.0)
        u_s = jnp.where(row >= d, pltpu.roll(u, d, 0), 0.0)
        u = a * u_s + u
        a = a * a_s
    return a, u


def _scan_up(a, u, row):
    for d in (1, 2, 4):
        a_s = jnp.where(row < 8 - d, pltpu.roll(a, 8 - d, 0), 1.0)
        u_s = jnp.where(row < 8 - d, pltpu.roll(u, 8 - d, 0), 0.0)
        u = a * u_s + u
        a = a * a_s
    return a, u


def _lru_scan_fwd(r, i, rxc, proj, lam, side=None):
    tc = SCAN_TC

    def body(r_ref, i_ref, x_ref, ry_ref, lam_ref, h_ref, y_ref):
        rate = LRU_C * _softplus(-lam_ref[...])
        row = lax.broadcasted_iota(jnp.int32, (8, tc), 0)

        def step(ci, carry):
            r0 = pl.multiple_of(ci * 16, 16)
            log_a = -rate * r_ref[pl.ds(r0, 16), :]
            a16 = jnp.exp(log_a)
            u16 = jnp.sqrt(_one_minus_exp(2.0 * log_a)) * (i_ref[pl.ds(r0, 16), :] * x_ref[pl.ds(r0, 16), :])
            hs = []
            for half in range(2):
                a_cum, h0 = _scan_down(a16[8 * half:8 * half + 8, :], u16[8 * half:8 * half + 8, :], row)
                h = a_cum * carry + h0
                carry = jnp.broadcast_to(h[7:8, :], (8, tc))
                hs.append(h)
            h16 = jnp.concatenate(hs, axis=0)
            h_ref[pl.ds(r0, 16), :] = h16
            y_ref[pl.ds(r0, 16), :] = (h16 * _gelu(ry_ref[pl.ds(r0, 16), :])[0]).astype(y_ref.dtype)
            return carry

        lax.fori_loop(0, S // 16, step, jnp.zeros((8, tc), F32))

    col = pl.BlockSpec((S, tc), lambda j: (0, j))
    return _call(
        body,
        name="lru_scan_fwd",
        grid=(D_RNN // tc,),
        in_specs=[col, col, col, pl.BlockSpec((S, tc), lambda j: (0, OFF_RY // tc + j)),
                  pl.BlockSpec((1, tc), lambda j: (0, j))],
        out_specs=[col, col],
        out_shape=[jax.ShapeDtypeStruct((S, D_RNN), F32), jax.ShapeDtypeStruct((S, D_RNN), MXU_DTYPE)],
        semantics=("parallel",),
        operands=(r, i, rxc, proj, lam),
        side=side,
    )


def _lru_scan_bwd(dy, proj, h, r, i, rxc, lam, side=None, window=None):
    tc = SCAN_TC

    def body(dy_ref, ry_ref, h_ref, r_ref, i_ref, x_ref, lam_ref,
             dry_ref, dzr_ref, dzi_ref, dx_ref, dba_ref, dbi_ref, dlam_ref, a_ref, dh_ref, hp_ref):
        lam_v = lam_ref[...]
        rate = LRU_C * _softplus(-lam_v)
        dlam_scale = LRU_C * _sigmoid(-lam_v)
        row = lax.broadcasted_iota(jnp.int32, (8, tc), 0)
        hp_ref[pl.ds(0, PAD), :] = jnp.zeros((PAD, tc), F32)
        hp_ref[pl.ds(PAD, S), :] = h_ref[...]
        a_ref[pl.ds(S, PAD), :] = jnp.zeros((PAD, tc), F32)

        def prep(ci, carry):
            r0 = pl.multiple_of(ci * CHUNK, CHUNK)
            a_ref[pl.ds(r0, CHUNK), :] = jnp.exp(-rate * r_ref[pl.ds(r0, CHUNK), :])
            ge, dge = _gelu(ry_ref[pl.ds(r0, CHUNK), :])
            dyv = dy_ref[pl.ds(r0, CHUNK), :]
            dh_ref[pl.ds(r0, CHUNK), :] = dyv * ge
            dry_ref[pl.ds(r0, CHUNK), :] = (dyv * h_ref[pl.ds(r0, CHUNK), :] * dge).astype(dry_ref.dtype)
            return carry

        lax.fori_loop(0, S // CHUNK, prep, 0)

        def step(ci, state):
            carry, dba, dbi, dlam = state
            r0 = pl.multiple_of(S - 16 - ci * 16, 16)
            a_ext = a_ref[pl.ds(r0, 24), :]
            a_next = pltpu.roll(a_ext, 23, 0)
            h_prev = pltpu.roll(hp_ref[pl.ds(r0, 24), :], 1, 0)
            dh16 = dh_ref[pl.ds(r0, 16), :]
            gs = [None, None]
            for half in (1, 0):
                lo = 8 * half
                c_cum, g0 = _scan_up(a_next[lo:lo + 8, :], dh16[lo:lo + 8, :], row)
                g = c_cum * carry + g0
                carry = jnp.broadcast_to(g[0:1, :], (8, tc))
                gs[half] = g
            g16 = jnp.concatenate(gs, axis=0)
            a16 = a_ext[0:16, :]
            r16 = r_ref[pl.ds(r0, 16), :]
            i16 = i_ref[pl.ds(r0, 16), :]
            x16 = x_ref[pl.ds(r0, 16), :]
            a2 = a16 * a16
            sq = jnp.sqrt(_one_minus_exp(-2.0 * rate * r16))
            dx_ref[pl.ds(r0, 16), :] = g16 * sq * i16
            dzi = g16 * sq * x16 * i16 * (1.0 - i16)
            dlog_a = g16 * h_prev[8:24, :] * a16 - g16 * i16 * x16 * a2 / sq
            dzr = -rate * dlog_a * r16 * (1.0 - r16)
            dzr_ref[pl.ds(r0, 16), :] = dzr.astype(dzr_ref.dtype)
            dzi_ref[pl.ds(r0, 16), :] = dzi.astype(dzi_ref.dtype)
            return carry, dba + _colsum(dzr), dbi + _colsum(dzi), dlam + _colsum(dlog_a * r16)

        zero = jnp.zeros((1, tc), F32)
        _, dba, dbi, dlam = lax.fori_loop(0, S // 16, step, (jnp.zeros((8, tc), F32), zero, zero, zero))
        dba_ref[...] = dba
        dbi_ref[...] = dbi
        dlam_ref[...] = dlam * dlam_scale

    col = pl.BlockSpec((S, tc), lambda j: (0, j))
    vec = pl.BlockSpec((1, tc), lambda j: (0, j))
    return _call(
        body,
        name="lru_scan_bwd",
        grid=(D_RNN // tc,),
        in_specs=[col, pl.BlockSpec((S, tc), lambda j: (0, OFF_RY // tc + j)), col, col, col, col, vec],
        out_specs=[col, col, col, col, vec, vec, vec],
        out_shape=[jax.ShapeDtypeStruct((S, D_RNN), MXU_DTYPE)] * 3 + [jax.ShapeDtypeStruct((S, D_RNN), F32)]
        + [jax.ShapeDtypeStruct((1, D_RNN), F32)] * 3,
        scratch_shapes=[pltpu.VMEM((S + PAD, tc), F32), pltpu.VMEM((S, tc), F32), pltpu.VMEM((S + PAD, tc), F32)],
        semantics=("parallel",),
        operands=(dy, proj, h, r, i, rxc, lam),
        side=side,
        window=window,
    )


def _lru_gate_wgrad(rxc, dzr, dzi, side=None):
    def body(x_ref, dzr_ref, dzi_ref, dwa_ref, dwi_ref):
        xv = x_ref[...].astype(MXU_DTYPE)
        dims = (((0,), (0,)), ((), ()))
        dwa_ref[...] = lax.dot_general(xv, dzr_ref[...], dims, preferred_element_type=F32)
        dwi_ref[...] = lax.dot_general(xv, dzi_ref[...], dims, preferred_element_type=F32)

    col = pl.BlockSpec((S, RNN_GROUP), lambda g: (0, g))
    w_spec = pl.BlockSpec((None, RNN_GROUP, RNN_GROUP), lambda g: (g, 0, 0))
    return _call(
        body,
        name="lru_gate_wgrad",
        grid=(N_RNN_GROUPS,),
        in_specs=[col, col, col],
        out_specs=[w_spec, w_spec],
        out_shape=[jax.ShapeDtypeStruct((N_RNN_GROUPS, RNN_GROUP, RNN_GROUP), F32)] * 2,
        semantics=("parallel",),
        operands=(rxc, dzr, dzi),
        side=side,
    )


def _lru_gate_xgrad(dzr, dzi, wa, wi, dx_in, side=None):
    tm = 512

    def body(dzr_ref, dzi_ref, wa_ref, wi_ref, dx_ref, o_ref):
        dims = (((1,), (1,)), ((), ()))
        o_ref[...] = (dx_ref[...]
                      + lax.dot_general(dzr_ref[...], wa_ref[...].astype(MXU_DTYPE), dims, preferred_element_type=F32)
                      + lax.dot_general(dzi_ref[...], wi_ref[...].astype(MXU_DTYPE), dims, preferred_element_type=F32))

    x_spec = pl.BlockSpec((tm, RNN_GROUP), lambda g, i: (i, g))
    w_spec = pl.BlockSpec((None, RNN_GROUP, RNN_GROUP), lambda g, i: (g, 0, 0))
    return _call(
        body,
        name="lru_gate_xgrad",
        grid=(N_RNN_GROUPS, S // tm),
        in_specs=[x_spec, x_spec, w_spec, w_spec, x_spec],
        out_specs=x_spec,
        out_shape=jax.ShapeDtypeStruct((S, D_RNN), F32),
        semantics=("parallel", "parallel"),
        operands=(dzr, dzi, wa, wi, dx_in),
        side=side,
    )


def _gate_fwd(y_attn, y_rnn, proj, b_gate, side=None):
    t = 512

    def body(ya_ref, yr_ref, ga_ref, gr_ref, ba_ref, br_ref, o_ref):
        o_ref[...] = (_sigmoid(ga_ref[...] + ba_ref[...]) * ya_ref[...]
                      + _sigmoid(gr_ref[...] + br_ref[...]) * yr_ref[...]).astype(o_ref.dtype)

    tile = pl.BlockSpec((t, t), lambda i, j: (i, j))
    return _call(
        body,
        name="gate_fwd",
        grid=(S // t, D // t),
        in_specs=[tile, tile,
                  pl.BlockSpec((t, t), lambda i, j: (i, OFF_GA // t + j)),
                  pl.BlockSpec((t, t), lambda i, j: (i, OFF_GR // t + j)),
                  pl.BlockSpec((1, t), lambda i, j: (0, j)),
                  pl.BlockSpec((1, t), lambda i, j: (0, D // t + j))],
        out_specs=tile,
        out_shape=jax.ShapeDtypeStruct((S, D), MXU_DTYPE),
        semantics=("parallel", "parallel"),
        operands=(y_attn, y_rnn, proj, proj, b_gate, b_gate),
        side=side,
    )


def _gate_bwd(dmix, y_attn, y_rnn, proj, b_gate, side=None, window=None):
    t = 512

    def body(dm_ref, ya_ref, yr_ref, ga_ref, gr_ref, ba_ref, br_ref,
             dga_ref, dya_ref, dyr_ref, dgr_ref, dba_ref, dbr_ref):
        @pl.when(pl.program_id(1) == 0)
        def _():
            dba_ref[...] = jnp.zeros_like(dba_ref)
            dbr_ref[...] = jnp.zeros_like(dbr_ref)

        dm = dm_ref[...]
        ga = _sigmoid(ga_ref[...] + ba_ref[...])
        gr = _sigmoid(gr_ref[...] + br_ref[...])
        dya_ref[...] = (dm * ga).astype(dya_ref.dtype)
        dyr_ref[...] = (dm * gr).astype(dyr_ref.dtype)
        dga = dm * ya_ref[...] * ga * (1.0 - ga)
        dgr = dm * yr_ref[...] * gr * (1.0 - gr)
        dga_ref[...] = dga.astype(dga_ref.dtype)
        dgr_ref[...] = dgr.astype(dgr_ref.dtype)
        dba_ref[...] += _colsum(dga)
        dbr_ref[...] += _colsum(dgr)

    tile = pl.BlockSpec((t, t), lambda j, i: (i, j))
    vec = pl.BlockSpec((1, t), lambda j, i: (0, j))
    return _call(
        body,
        name="gate_bwd",
        grid=(D // t, S // t),
        in_specs=[tile, tile, tile,
                  pl.BlockSpec((t, t), lambda j, i: (i, OFF_GA // t + j)),
                  pl.BlockSpec((t, t), lambda j, i: (i, OFF_GR // t + j)),
                  vec,
                  pl.BlockSpec((1, t), lambda j, i: (0, D // t + j))],
        out_specs=[tile, tile, tile, tile, vec, vec],
        out_shape=[jax.ShapeDtypeStruct((S, D), MXU_DTYPE)] * 4 + [jax.ShapeDtypeStruct((1, D), F32)] * 2,
        semantics=("parallel", "arbitrary"),
        operands=(dmix, y_attn, y_rnn, proj, proj, b_gate, b_gate),
        side=side,
        window=window,
    )


LN_TM = 256


def _ln_stats(pre):
    mu = jnp.mean(pre, axis=-1, keepdims=True)
    xc = pre - mu
    rstd = lax.rsqrt(jnp.mean(xc * xc, axis=-1, keepdims=True) + LN_EPS)
    return xc * rstd, rstd


def _ln_input_grad(dy, xhat, rstd, g):
    dyg = dy * g
    return rstd * (dyg - jnp.mean(dyg, axis=-1, keepdims=True)
                   - xhat * jnp.mean(dyg * xhat, axis=-1, keepdims=True))


def _ln_fwd(res, branch, g, b, side=None):
    def body(res_ref, br_ref, g_ref, b_ref, y_ref, yb_ref, xhat_ref, rstd_ref):
        xhat, rstd = _ln_stats(ALPHA * res_ref[...] + br_ref[...])
        y = xhat * g_ref[...] + b_ref[...]
        y_ref[...] = y
        yb_ref[...] = y.astype(yb_ref.dtype)
        xhat_ref[...] = xhat
        rstd_ref[...] = rstd

    tile = pl.BlockSpec((LN_TM, D), lambda i: (i, 0))
    vec = pl.BlockSpec((1, D), lambda i: (0, 0))
    return _call(
        body,
        name="ln_fwd",
        grid=(S // LN_TM,),
        in_specs=[tile, tile, vec, vec],
        out_specs=[tile, tile, tile, pl.BlockSpec((LN_TM, 1), lambda i: (i, 0))],
        out_shape=[jax.ShapeDtypeStruct((S, D), F32), jax.ShapeDtypeStruct((S, D), MXU_DTYPE),
                   jax.ShapeDtypeStruct((S, D), F32), jax.ShapeDtypeStruct((S, 1), F32)],
        semantics=("parallel",),
        operands=(res, branch, g, b),
        side=side,
    )


def _ln_bwd(dy_a, dy_b, xhat, rstd, g, side=None):
    def body(da_ref, db_in_ref, xhat_ref, rstd_ref, g_ref, dp_ref, dpb_ref, dg_ref, db_ref):
        @pl.when(pl.program_id(0) == 0)
        def _():
            dg_ref[...] = jnp.zeros_like(dg_ref)
            db_ref[...] = jnp.zeros_like(db_ref)

        dy = da_ref[...] + ALPHA * db_in_ref[...]
        xhat = xhat_ref[...]
        dp = _ln_input_grad(dy, xhat, rstd_ref[...], g_ref[...])
        dp_ref[...] = dp
        dpb_ref[...] = dp.astype(dpb_ref.dtype)
        dg_ref[...] += _colsum(dy * xhat)
        db_ref[...] += _colsum(dy)

    tile = pl.BlockSpec((LN_TM, D), lambda i: (i, 0))
    vec = pl.BlockSpec((1, D), lambda i: (0, 0))
    return _call(
        body,
        name="ln_bwd",
        grid=(S // LN_TM,),
        in_specs=[tile, tile, tile, pl.BlockSpec((LN_TM, 1), lambda i: (i, 0)), vec],
        out_specs=[tile, tile, vec, vec],
        out_shape=[jax.ShapeDtypeStruct((S, D), F32), jax.ShapeDtypeStruct((S, D), MXU_DTYPE),
                   jax.ShapeDtypeStruct((1, D), F32), jax.ShapeDtypeStruct((1, D), F32)],
        semantics=("arbitrary",),
        operands=(dy_a, dy_b, xhat, rstd, g),
        side=side,
    )


def _ln_loss_bwd(res, branch, g, b, target, side=None):
    def body(res_ref, br_ref, g_ref, b_ref, t_ref, loss_ref, dp_ref, dpb_ref, dg_ref, db_ref):
        @pl.when(pl.program_id(0) == 0)
        def _():
            loss_ref[...] = jnp.zeros_like(loss_ref)
            dg_ref[...] = jnp.zeros_like(dg_ref)
            db_ref[...] = jnp.zeros_like(db_ref)

        xhat, rstd = _ln_stats(ALPHA * res_ref[...] + br_ref[...])
        gv = g_ref[...]
        err = xhat * gv + b_ref[...] - t_ref[...]
        loss_ref[...] += (0.5 / D) * jnp.sum(_colsum(err * err), axis=1, keepdims=True)
        dy = err * (1.0 / D)
        dp = _ln_input_grad(dy, xhat, rstd, gv)
        dp_ref[...] = dp
        dpb_ref[...] = dp.astype(dpb_ref.dtype)
        dg_ref[...] += _colsum(dy * xhat)
        db_ref[...] += _colsum(dy)

    tile = pl.BlockSpec((LN_TM, D), lambda i: (i, 0))
    vec = pl.BlockSpec((1, D), lambda i: (0, 0))
    return _call(
        body,
        name="ln_loss_bwd",
        grid=(S // LN_TM,),
        in_specs=[tile, tile, vec, vec, tile],
        out_specs=[pl.BlockSpec((1, 1), lambda i: (0, 0)), tile, tile, vec, vec],
        out_shape=[jax.ShapeDtypeStruct((1, 1), F32), jax.ShapeDtypeStruct((S, D), F32),
                   jax.ShapeDtypeStruct((S, D), MXU_DTYPE),
                   jax.ShapeDtypeStruct((1, D), F32), jax.ShapeDtypeStruct((1, D), F32)],
        semantics=("arbitrary",),
        operands=(res, branch, g, b, target),
        side=side,
    )


FFN_TC = 256


FFN_GATE_TN = 384


def _ffn_gate_act(x, w_gate, up, w, b, side=None):
    tn = FFN_GATE_TN
    b_block = w_gate.shape[2]
    assert b_block % tn == 0

    def body(x_ref, wg_ref, up_ref, w_ref, b_ref, g_ref, o_ref, xpad_ref):
        g_ref[...] = lax.dot_general(x_ref[...].astype(MXU_DTYPE), wg_ref[...].astype(MXU_DTYPE),
                                     (((1,), (0,)), ((), ())), preferred_element_type=F32)
        xpad_ref[pl.ds(0, PAD), :] = jnp.zeros((PAD, tn), F32)
        xpad_ref[pl.ds(PAD, S), :] = g_ref[...]
        wv = w_ref[...]
        bv = b_ref[...]

        def step(ci, carry):
            r0 = pl.multiple_of(ci * CHUNK, CHUNK)
            taps = _past_taps(xpad_ref, r0, FFN_CONV_W)
            gate = bv + taps[0] * wv[0:1, :] + taps[1] * wv[1:2, :] + taps[2] * wv[2:3, :]
            o_ref[pl.ds(r0, CHUNK), :] = (_gelu(gate)[0] * up_ref[pl.ds(r0, CHUNK), :]).astype(o_ref.dtype)
            return carry

        lax.fori_loop(0, S // CHUNK, step, 0)

    col = pl.BlockSpec((S, tn), lambda j: (0, j))
    return _call(
        body,
        name="ffn_gate_act",
        grid=(D_FF // tn,),
        in_specs=[pl.BlockSpec((S, D), lambda j: (0, 0)),
                  pl.BlockSpec((None, D, tn), lambda j: ((j * tn) // b_block, 0, ((j * tn) % b_block) // tn)),
                  col, pl.BlockSpec((FFN_CONV_W, tn), lambda j: (0, j)), pl.BlockSpec((1, tn), lambda j: (0, j))],
        out_specs=[col, col],
        out_shape=[jax.ShapeDtypeStruct((S, D_FF), F32), jax.ShapeDtypeStruct((S, D_FF), MXU_DTYPE)],
        scratch_shapes=[pltpu.VMEM((S + PAD, tn), F32)],
        semantics=("parallel",),
        operands=(x, w_gate, up, w, b),
        side=side,
    )


def _ffn_act_bwd(dfin, up, gpre, w, b, side=None):
    tc = FFN_TC
    width = FFN_CONV_W

    def body(df_ref, up_ref, x_ref, w_ref, b_ref, dup_ref, dx_ref, dw_ref, db_ref, xpad_ref, dpad_ref):
        xpad_ref[pl.ds(0, PAD), :] = jnp.zeros((PAD, tc), F32)
        xpad_ref[pl.ds(PAD, S), :] = x_ref[...]
        dpad_ref[pl.ds(S, PAD), :] = jnp.zeros((PAD, tc), F32)
        wv = w_ref[...]
        bv = b_ref[...]

        def gate_grad(ci, acc):
            r0 = pl.multiple_of(ci * CHUNK, CHUNK)
            taps = _past_taps(xpad_ref, r0, width)
            gate = bv + taps[0] * wv[0:1, :] + taps[1] * wv[1:2, :] + taps[2] * wv[2:3, :]
            ge, dge = _gelu(gate)
            df = df_ref[pl.ds(r0, CHUNK), :]
            dup_ref[pl.ds(r0, CHUNK), :] = (df * ge).astype(dup_ref.dtype)
            d = df * up_ref[pl.ds(r0, CHUNK), :] * dge
            dpad_ref[pl.ds(r0, CHUNK), :] = d
            return tuple(acc[k] + _colsum(taps[k] * d) for k in range(width)) + (acc[width] + _colsum(d),)

        zero = jnp.zeros((1, tc), F32)
        acc = lax.fori_loop(0, S // CHUNK, gate_grad, (zero,) * (width + 1))
        for k in range(width):
            dw_ref[k:k + 1, :] = acc[k]
        db_ref[...] = acc[width]

        def input_grad(ci, carry):
            r0 = pl.multiple_of(ci * CHUNK, CHUNK)
            ahead = _future_taps(dpad_ref, r0, width)
            dx = ahead[0] * wv[2:3, :] + ahead[1] * wv[1:2, :] + ahead[2] * wv[0:1, :]
            dx_ref[pl.ds(r0, CHUNK), :] = dx.astype(dx_ref.dtype)
            return carry

        lax.fori_loop(0, S // CHUNK, input_grad, 0)

    col = pl.BlockSpec((S, tc), lambda j: (0, j))
    w_spec = pl.BlockSpec((width, tc), lambda j: (0, j))
    vec = pl.BlockSpec((1, tc), lambda j: (0, j))
    return _call(
        body,
        name="ffn_act_bwd",
        grid=(D_FF // tc,),
        in_specs=[col, col, col, w_spec, vec],
        out_specs=[col, col, w_spec, vec],
        out_shape=[jax.ShapeDtypeStruct((S, D_FF), MXU_DTYPE)] * 2
        + [jax.ShapeDtypeStruct((width, D_FF), F32), jax.ShapeDtypeStruct((1, D_FF), F32)],
        scratch_shapes=[pltpu.VMEM((S + PAD, tc), F32), pltpu.VMEM((S + PAD, tc), F32)],
        semantics=("parallel",),
        operands=(dfin, up, gpre, w, b),
        side=side,
    )


def _adamw_update(w, g, m, v):
    m = ADAM_B1 * m + (1.0 - ADAM_B1) * g
    v = ADAM_B2 * v + (1.0 - ADAM_B2) * (g * g)
    m_hat = m / (1.0 - ADAM_B1 ** ADAM_STEP)
    v_hat = v / (1.0 - ADAM_B2 ** ADAM_STEP)
    delta = -ADAM_LR * (m_hat / (jnp.sqrt(v_hat) + ADAM_EPS) + ADAM_WD * w)
    return delta, m, v


def _add_pairs(send, pair, far_index, *, name):
    _, r_dim, c_dim = send.shape
    tr = r_dim // 4

    def body(far_ref, mine_ref, theirs_ref, o_ref):
        o_ref[...] = (mine_ref[...].astype(F32) + theirs_ref[...].astype(F32)).astype(o_ref.dtype)

    return pl.pallas_call(
        body,
        name=name,
        grid_spec=pltpu.PrefetchScalarGridSpec(
            num_scalar_prefetch=1,
            grid=(3, r_dim // tr),
            in_specs=[pl.BlockSpec((None, tr, c_dim), lambda j, i, far: (far[j], i, 0)),
                      pl.BlockSpec((None, tr, c_dim), lambda j, i, far: (1 + j, i, 0))],
            out_specs=pl.BlockSpec((None, tr, c_dim), lambda j, i, far: (j, i, 0)),
        ),
        out_shape=jax.ShapeDtypeStruct((3, r_dim, c_dim), BF16),
        compiler_params=_cparams("parallel", "parallel"),
    )(far_index, send, pair)


def _reduce_adamw(w, m, v, g_own, pair, far, me, *, tr, name, part=0, earlier=None):
    _, r_dim, c_dim = w.shape
    cp = pair.shape[2]

    def body(me_ref, w_ref, m_ref, v_ref, g_ref, pair_ref, far_ref, *refs):
        grad_ref, delta_ref, nm_ref, nv_ref = refs[-4:]
        g = g_ref[...] + pair_ref[...].astype(F32)
        for j in range(3):
            g = g + far_ref[j].astype(F32)
        delta, nm, nv = _adamw_update(w_ref[...], g, m_ref[...], v_ref[...])
        grad_ref[...] = g
        delta_ref[...] = delta
        nm_ref[...] = nm
        nv_ref[...] = nv

    tile = pl.BlockSpec((None, tr, cp), lambda i, me: (0, i, part))
    if g_own.ndim == 3:
        own_spec = pl.BlockSpec((None, tr, cp), lambda i, me: (me[0], i, 0))
    else:
        own_spec = pl.BlockSpec((tr, cp), lambda i, me: (i, 0))
    earlier = list(earlier or ())
    return pl.pallas_call(
        body,
        name=name,
        grid_spec=pltpu.PrefetchScalarGridSpec(
            num_scalar_prefetch=1,
            grid=(r_dim // tr,),
            in_specs=[tile, tile, tile, own_spec, pl.BlockSpec((None, tr, cp), lambda i, me: (0, i, 0)),
                      pl.BlockSpec((3, tr, cp), lambda i, me: (0, i, 0))]
            + [pl.BlockSpec(memory_space=pl.ANY)] * len(earlier),
            out_specs=[tile] * 4,
        ),
        out_shape=[jax.ShapeDtypeStruct((1, r_dim, c_dim), F32)] * 4,
        input_output_aliases={7 + k: k for k in range(len(earlier))},
        compiler_params=_cparams("parallel"),
    )(me, w, m, v, g_own, pair, far, *earlier)


def _adamw_many(ws, ms, vs, gs):
    n = len(ws)

    def body(*refs):
        for i in range(n):
            delta, nm, nv = _adamw_update(refs[i][...], refs[3 * n + i][...], refs[n + i][...], refs[2 * n + i][...])
            refs[4 * n + i][...] = delta
            refs[5 * n + i][...] = nm
            refs[6 * n + i][...] = nv

    vmem = pl.BlockSpec(memory_space=pltpu.VMEM)
    res = pl.pallas_call(
        body,
        name="adamw_small",
        in_specs=[vmem] * (4 * n),
        out_specs=[vmem] * (3 * n),
        out_shape=[jax.ShapeDtypeStruct(w.shape, F32) for w in ws] * 3,
        compiler_params=pltpu.CompilerParams(vmem_limit_bytes=VMEM_LIMIT),
    )(*ws, *ms, *vs, *gs)
    return res[:n], res[n:2 * n], res[2 * n:]


def _adamw_blocks(w, m, v, g, *, name, side=None):
    per = 2

    def body(w_ref, m_ref, v_ref, g_ref, delta_ref, nm_ref, nv_ref):
        delta, nm, nv = _adamw_update(w_ref[...], g_ref[...], m_ref[...], v_ref[...])
        delta_ref[...] = delta
        nm_ref[...] = nm
        nv_ref[...] = nv

    tile = pl.BlockSpec((1, per) + w.shape[2:], lambda i: (0, i, 0, 0))
    return _call(
        body,
        name=name,
        grid=(w.shape[1] // per,),
        in_specs=[tile] * 4,
        out_specs=[tile] * 3,
        out_shape=[jax.ShapeDtypeStruct(w.shape, F32)] * 3,
        semantics=("parallel",),
        operands=(w, m, v, g),
        side=side,
    )


def _coords():
    return lax.axis_index("x"), lax.axis_index("y"), lax.axis_index("c")


def _flip(coord, bit):
    return 1 - coord if bit else coord


def _relative(k):
    x, y, c = _coords()
    return _flip(x, k & 4), _flip(y, k & 2), _flip(c, k & 1)


def _index(pos):
    return 4 * pos[0] + 2 * pos[1] + pos[2]


FAR = (4, 2, 6)
AG_US_PER_MB = 38.0
RS_US_PER_MB = 46.0
MIN_RIDE_US = 30.0
PAIR_EXCHANGE_US = 20.0
MIN_GATHER_RIDE_US = 22.0
ROW_ALIGN = 32


def _chunks(items, cursor, us, us_per_mb, through=None):
    budget = float("inf") if us is None else us / us_per_mb * 2 ** 20
    names = list(items)
    if through is not None:
        names = names[:names.index(through) + 1]
    chunks = []
    for name in names:
        arr = items[name]
        r_dim, c_dim = arr.shape[-2:]
        row_bytes = c_dim * arr.dtype.itemsize
        while cursor[name] < r_dim and budget > 0:
            rows = r_dim - cursor[name]
            if r_dim > ROW_ALIGN and budget < rows * row_bytes:
                rows = min(rows, max(ROW_ALIGN, int(budget // row_bytes) // ROW_ALIGN * ROW_ALIGN))
            chunks.append((name, cursor[name], rows))
            cursor[name] += rows
            budget -= rows * row_bytes
    return chunks


class _Gather:
    def __init__(self, shards):
        self.shards, self.bufs, self.cursor = {}, {}, {}
        self.add_shards(shards)

    def add_shards(self, shards):
        for n, shard in shards.items():
            self.shards[n], self.bufs[n], self.cursor[n] = shard, None, 0

    def take(self, us=None, through=None):
        if us is not None and us < MIN_GATHER_RIDE_US:
            return None
        chunks = _chunks(self.shards, self.cursor, us, AG_US_PER_MB, through)
        return _GatherSide(self, chunks) if chunks else None

    def get(self, name):
        chunks = _chunks(self.shards, self.cursor, None, AG_US_PER_MB, through=name)
        if chunks:
            _run_side(_GatherSide(self, chunks), "gather_" + name)
        return self.bufs[name]


class _GatherSide:
    SEMS = 8

    def __init__(self, owner, chunks):
        self.owner, self.chunks = owner, chunks
        self.names = list(dict.fromkeys(n for n, _, _ in chunks))
        old = [n for n in self.names if owner.bufs[n] is not None]
        self.operands = [owner.shards[n] for n in self.names] + [owner.bufs[n] for n in old]
        self.out_shape = [jax.ShapeDtypeStruct((N_DEV,) + owner.shards[n].shape, owner.shards[n].dtype)
                          for n in self.names]
        self.aliases = {len(self.names) + i: self.names.index(n) for i, n in enumerate(old)}
        self.sems = [pltpu.SemaphoreType.DMA((self.SEMS * len(chunks),)),
                     pltpu.SemaphoreType.DMA((self.SEMS * len(chunks),)), pltpu.SemaphoreType.DMA((len(chunks),))]

    def _halves(self, ci):
        _, r0, rows = self.chunks[ci]
        if rows % ROW_ALIGN:
            return None
        return (r0, rows // 2), (r0 + rows // 2, rows // 2)

    def _copy(self, ins, outs, sems, ci, s, block, to, rows=None, from_shard=False):
        name, r0, n = self.chunks[ci]
        if rows is not None:
            r0, n = rows
        w = self.names.index(name)
        slot = outs[w].at[_index(block), pl.ds(r0, n)]
        return pltpu.make_async_remote_copy(
            src_ref=ins[w].at[pl.ds(r0, n)] if from_shard else slot, dst_ref=slot,
            send_sem=sems[0].at[self.SEMS * ci + s], recv_sem=sems[1].at[self.SEMS * ci + s],
            device_id=to, device_id_type=MESH)

    def _own(self, ins, outs, sems, ci):
        name, r0, rows = self.chunks[ci]
        w = self.names.index(name)
        return pltpu.make_async_copy(ins[w].at[pl.ds(r0, rows)], outs[w].at[_index(_relative(0)), pl.ds(r0, rows)],
                                     sems[2].at[ci])

    def _pass(self, ins, outs, sems, ci, which):
        source, target = ((4, 2), (2, 4))[which]
        return self._copy(ins, outs, sems, ci, 3 + which, _relative(source), _relative(target),
                          rows=self._halves(ci)[which])

    def start(self, ins, outs, sems):
        me = _relative(0)
        for ci in range(len(self.chunks)):
            self._own(ins, outs, sems, ci).start()
        for ci in range(len(self.chunks)):
            self._copy(ins, outs, sems, ci, 1, me, _relative(4), from_shard=True).start()
            self._copy(ins, outs, sems, ci, 2, me, _relative(2), from_shard=True).start()
            if self._halves(ci) is None:
                self._copy(ins, outs, sems, ci, 3, me, _relative(6), from_shard=True).start()
        for ci in range(len(self.chunks)):
            self._copy(ins, outs, sems, ci, 0, me, _relative(1), from_shard=True).start()

    def mid(self, ins, outs, sems):
        me = _relative(0)
        cut = [ci for ci in range(len(self.chunks)) if self._halves(ci) is not None]
        for ci in cut:
            self._copy(ins, outs, sems, ci, 1, _relative(4), me).wait_recv()
            self._pass(ins, outs, sems, ci, 0).start()
            self._copy(ins, outs, sems, ci, 5, _relative(4), _relative(1)).start()
        for ci in cut:
            self._copy(ins, outs, sems, ci, 2, _relative(2), me).wait_recv()
            self._pass(ins, outs, sems, ci, 1).start()
            self._copy(ins, outs, sems, ci, 6, _relative(2), _relative(1)).start()

    def finish(self, ins, outs, sems):
        me, sibling = _relative(0), _relative(1)
        n = len(self.chunks)
        for ci in range(n):
            if self._halves(ci) is None:
                for s, k in ((1, 4), (2, 2), (3, 6)):
                    self._copy(ins, outs, sems, ci, s, _relative(k), me).wait_recv()
                for j, k in enumerate(FAR):
                    self._copy(ins, outs, sems, ci, 5 + j, _relative(k), sibling).start()
            else:
                h0, h1 = self._halves(ci)
                self._copy(ins, outs, sems, ci, 3, _relative(6), me, rows=h0).wait_recv()
                self._copy(ins, outs, sems, ci, 4, _relative(6), me, rows=h1).wait_recv()
                self._copy(ins, outs, sems, ci, 7, _relative(6), sibling).start()
        for ci in range(n):
            self._copy(ins, outs, sems, ci, 0, sibling, me).wait_recv()
            for j, k in enumerate(FAR):
                self._copy(ins, outs, sems, ci, 5 + j, _relative(k | 1), me).wait_recv()
        for ci in range(n):
            self._copy(ins, outs, sems, ci, 0, me, sibling, from_shard=True).wait_send()
            self._copy(ins, outs, sems, ci, 1, me, _relative(4), from_shard=True).wait_send()
            self._copy(ins, outs, sems, ci, 2, me, _relative(2), from_shard=True).wait_send()
            if self._halves(ci) is None:
                self._copy(ins, outs, sems, ci, 3, me, _relative(6), from_shard=True).wait_send()
            else:
                self._pass(ins, outs, sems, ci, 0).wait_send()
                self._pass(ins, outs, sems, ci, 1).wait_send()
            for j, k in enumerate(FAR):
                self._copy(ins, outs, sems, ci, 5 + j, _relative(k), sibling).wait_send()
            self._own(ins, outs, sems, ci).wait()

    def done(self, results):
        for n, buf in zip(self.names, results):
            self.owner.bufs[n] = buf


class _Scatter:
    def __init__(self, me, far_index):
        self.me, self.far_index = me, far_index
        self.sends, self.owns, self.pairs, self.sums, self.fars = {}, {}, {}, {}, {}
        self.pair_cursor, self.far_cursor = {}, {}

    def add(self, name, send, own):
        self.sends[name] = send
        self.owns[name] = own
        self.pairs[name] = self.fars[name] = None
        self.pair_cursor[name] = 0

    def _rows(self, name):
        return self.sends[name].shape[1]

    def _add_ready_pairs(self):
        for name in self.sends:
            if name not in self.sums and self.pair_cursor[name] == self._rows(name):
                self.sums[name] = _add_pairs(self.sends[name], self.pairs[name], self.far_index, name="pair_" + name)
                self.far_cursor[name] = 0

    def _side(self, us, through=None):
        self._add_ready_pairs()
        names = list(self.sends)
        if through is not None:
            names = names[:names.index(through) + 1]
        pair_chunks = [(n, self.pair_cursor[n], self._rows(n) - self.pair_cursor[n]) for n in names
                       if self.pair_cursor[n] < self._rows(n)]
        for n, _, _ in pair_chunks:
            self.pair_cursor[n] = self._rows(n)
        far_chunks = _chunks(self.sums, self.far_cursor, us, RS_US_PER_MB,
                             through if through in self.sums else None) if self.sums else []
        return _ScatterSide(self, pair_chunks, far_chunks) if pair_chunks or far_chunks else None

    def add_blocks(self, name, blocks32, blocks16):
        self.add(name, blocks16, blocks32)

    def take(self, us):
        return self._side(us) if us >= MIN_RIDE_US else None

    def flush_pairs(self, name):
        side = self._side(PAIR_EXCHANGE_US)
        if side is not None:
            _run_side(side, name)
        self._add_ready_pairs()

    def get(self, name):
        step = 0
        while name not in self.sums or self.far_cursor[name] < self._rows(name):
            _run_side(self._side(None, through=name), "scatter_%s_%d" % (name, step))
            step += 1
        return self.owns[name], self.pairs[name], self.fars[name]


class _ScatterSide:
    TO_SIBLING = (1, 5, 3, 7)

    def __init__(self, owner, pair_chunks, far_chunks):
        self.owner, self.pair_chunks, self.far_chunks = owner, pair_chunks, far_chunks
        self.pair_names = list(dict.fromkeys(n for n, _, _ in pair_chunks))
        self.far_names = list(dict.fromkeys(n for n, _, _ in far_chunks))
        ins = [(owner.sends[n], owner.pairs[n], (4,)) for n in self.pair_names]
        ins += [(owner.sums[n], owner.fars[n], (3,)) for n in self.far_names]
        old = [i for i, (_, buf, _) in enumerate(ins) if buf is not None]
        self.operands = [src for src, _, _ in ins] + [ins[i][1] for i in old]
        self.out_shape = [jax.ShapeDtypeStruct(slots + src.shape[1:], BF16) for src, _, slots in ins]
        self.aliases = {len(ins) + j: i for j, i in enumerate(old)}
        n_pair, n_far = 4 * len(pair_chunks), 3 * len(far_chunks)
        self.sems = [pltpu.SemaphoreType.DMA((max(n_pair, 1),)), pltpu.SemaphoreType.DMA((max(n_pair, 1),)),
                     pltpu.SemaphoreType.DMA((max(n_far, 1),)), pltpu.SemaphoreType.DMA((max(n_far, 1),))]

    def _copies(self, ins, outs, sems):
        copies = []
        for ci, (name, r0, rows) in enumerate(self.pair_chunks):
            w = self.pair_names.index(name)
            for j, k in enumerate(self.TO_SIBLING):
                copies.append(pltpu.make_async_remote_copy(
                    src_ref=ins[w].at[_index(_relative(k)), pl.ds(r0, rows)], dst_ref=outs[w].at[j, pl.ds(r0, rows)],
                    send_sem=sems[0].at[4 * ci + j], recv_sem=sems[1].at[4 * ci + j],
                    device_id=_relative(1), device_id_type=MESH))
        for ci, (name, r0, rows) in enumerate(self.far_chunks):
            w = len(self.pair_names) + self.far_names.index(name)
            for j, k in enumerate(FAR):
                copies.append(pltpu.make_async_remote_copy(
                    src_ref=ins[w].at[j, pl.ds(r0, rows)], dst_ref=outs[w].at[j, pl.ds(r0, rows)],
                    send_sem=sems[2].at[3 * ci + j], recv_sem=sems[3].at[3 * ci + j],
                    device_id=_relative(k), device_id_type=MESH))
        return copies

    def start(self, ins, outs, sems):
        for cp in self._copies(ins, outs, sems):
            cp.start()

    def mid(self, ins, outs, sems):
        pass

    def finish(self, ins, outs, sems):
        for cp in self._copies(ins, outs, sems):
            cp.wait()

    def done(self, results):
        for n, buf in zip(self.pair_names, results):
            self.owner.pairs[n] = buf
        for n, buf in zip(self.far_names, results[len(self.pair_names):]):
            self.owner.fars[n] = buf


class _Joined:
    def __init__(self, sides):
        self.sides = sides
        self.operands, self.out_shape, self.sems, self.aliases, self.spans = [], [], [], {}, []
        for s in sides:
            i0, o0, s0 = len(self.operands), len(self.out_shape), len(self.sems)
            self.operands += list(s.operands)
            self.out_shape += list(s.out_shape)
            self.sems += list(s.sems)
            self.aliases.update({i0 + i: o0 + o for i, o in s.aliases.items()})
            self.spans.append((slice(i0, len(self.operands)), slice(o0, len(self.out_shape)),
                               slice(s0, len(self.sems))))

    def start(self, ins, outs, sems):
        for s, (i, o, m) in zip(self.sides, self.spans):
            s.start(ins[i], outs[o], sems[m])

    def mid(self, ins, outs, sems):
        for s, (i, o, m) in zip(self.sides, self.spans):
            s.mid(ins[i], outs[o], sems[m])

    def finish(self, ins, outs, sems):
        for s, (i, o, m) in zip(self.sides, self.spans):
            s.finish(ins[i], outs[o], sems[m])

    def done(self, results):
        for s, (_, o, _) in zip(self.sides, self.spans):
            s.done(results[o])


def _join(*sides):
    sides = [s for s in sides if s is not None]
    if len(sides) <= 1:
        return sides[0] if sides else None
    return _Joined(sides)


PART_W = 768


def _pack_rows(vecs):
    rows = -(-sum(v.shape[0] for v in vecs) // 8) * 8

    def body(*refs):
        out = refs[-1]
        out[...] = jnp.zeros_like(out)
        r0 = 0
        for v in refs[:-1]:
            k, n = v.shape
            for p in range(-(-n // PART_W)):
                w = min(PART_W, n - PART_W * p)
                out[p, r0:r0 + k, 0:w] = v[:, PART_W * p:PART_W * p + w]
            r0 += k

    vmem = pl.BlockSpec(memory_space=pltpu.VMEM)
    return pl.pallas_call(body, name="pack_small", in_specs=[vmem] * len(vecs), out_specs=vmem,
                          out_shape=jax.ShapeDtypeStruct((N_DEV, rows, PART_W), F32))(*vecs)


def _unpack_rows(packed, shapes):
    def body(packed_ref, *outs):
        r0 = 0
        for o in outs:
            k, n = o.shape
            for p in range(-(-n // PART_W)):
                w = min(PART_W, n - PART_W * p)
                o[:, PART_W * p:PART_W * p + w] = packed_ref[p, r0:r0 + k, 0:w]
            r0 += k

    vmem = pl.BlockSpec(memory_space=pltpu.VMEM)
    return pl.pallas_call(body, name="unpack_small", in_specs=[vmem], out_specs=[vmem] * len(shapes),
                          out_shape=[jax.ShapeDtypeStruct(s, F32) for s in shapes])(packed)


class _PartsToOwners:
    def __init__(self, arrays):
        self.n = len(arrays)
        self.pers = [a.shape[0] // N_DEV for a in arrays]
        self.operands, self.aliases = list(arrays), {}
        self.out_shape = [jax.ShapeDtypeStruct((N_DEV, per) + a.shape[1:], a.dtype) for a, per in zip(arrays, self.pers)]
        self.sems = [pltpu.SemaphoreType.DMA((self.n * (N_DEV - 1),))] * 2

    def _copies(self, ins, outs, sems):
        return [pltpu.make_async_remote_copy(
            src_ref=ins[j].at[pl.ds(self.pers[j] * _index(_relative(k)), self.pers[j])], dst_ref=outs[j].at[k],
            send_sem=sems[0].at[self.n * (k - 1) + j], recv_sem=sems[1].at[self.n * (k - 1) + j],
            device_id=_relative(k), device_id_type=MESH) for k in range(1, N_DEV) for j in range(self.n)]

    def start(self, ins, outs, sems):
        for cp in self._copies(ins, outs, sems):
            cp.start()

    def mid(self, ins, outs, sems):
        pass

    def finish(self, ins, outs, sems):
        for cp in self._copies(ins, outs, sems):
            cp.wait()

    def done(self, results):
        self.stages = list(results)


def _sum_parts(arrays, stages, *, name):
    n = len(arrays)
    pers = [a.shape[0] // N_DEV for a in arrays]

    def body(*refs):
        me = _index(_relative(0))
        for j in range(n):
            acc = refs[j][pl.ds(pers[j] * me, pers[j])]
            for k in range(1, N_DEV):
                acc = acc + refs[n + j][k].astype(F32)
            refs[2 * n + j][...] = acc

    vmem = pl.BlockSpec(memory_space=pltpu.VMEM)
    return pl.pallas_call(body, name=name, in_specs=[vmem] * (2 * n), out_specs=[vmem] * n,
                          out_shape=[jax.ShapeDtypeStruct((per,) + a.shape[1:], F32) for a, per in zip(arrays, pers)],
                          compiler_params=pltpu.CompilerParams(vmem_limit_bytes=VMEM_LIMIT))(*arrays, *stages)


class _PartsToAll:
    def __init__(self, parts):
        self.n = len(parts)
        self.pers = [p.shape[0] for p in parts]
        self.operands, self.aliases = list(parts), {}
        self.out_shape = [jax.ShapeDtypeStruct((N_DEV * p.shape[0],) + p.shape[1:], F32) for p in parts]
        self.sems = [pltpu.SemaphoreType.DMA((self.n * (N_DEV - 1),))] * 2 + [pltpu.SemaphoreType.DMA((self.n,))]

    def _rows(self, outs, j, pos):
        return outs[j].at[pl.ds(self.pers[j] * _index(pos), self.pers[j])]

    def _copy(self, ins, outs, sems, k, j, owner):
        return pltpu.make_async_remote_copy(
            src_ref=ins[j], dst_ref=self._rows(outs, j, owner),
            send_sem=sems[0].at[self.n * (k - 1) + j], recv_sem=sems[1].at[self.n * (k - 1) + j],
            device_id=_relative(k), device_id_type=MESH)

    def _own(self, ins, outs, sems, j):
        return pltpu.make_async_copy(ins[j], self._rows(outs, j, _relative(0)), sems[2].at[j])

    def start(self, ins, outs, sems):
        for j in range(self.n):
            self._own(ins, outs, sems, j).start()
            for k in range(1, N_DEV):
                self._copy(ins, outs, sems, k, j, _relative(0)).start()

    def mid(self, ins, outs, sems):
        pass

    def finish(self, ins, outs, sems):
        for j in range(self.n):
            for k in range(1, N_DEV):
                self._copy(ins, outs, sems, k, j, _relative(k)).wait_recv()
                self._copy(ins, outs, sems, k, j, _relative(0)).wait_send()
            self._own(ins, outs, sems, j).wait()

    def done(self, results):
        self.totals = list(results)


class _AllReduce:
    def __init__(self, name):
        self.name = name

    def begin(self, arrays, wire_dtype=F32):
        self.own = list(arrays)
        self.to_owners = _PartsToOwners([a.astype(wire_dtype) for a in self.own])
        return self.to_owners

    def middle(self):
        self.to_all = _PartsToAll(_sum_parts(self.own, self.to_owners.stages, name="sum_" + self.name))
        return self.to_all

    def end(self):
        return self.to_all.totals


class _SmallSync:
    def __init__(self, vec_names, mat_names):
        self.vec_names, self.mat_names = vec_names, mat_names
        self.mats, self.vecs = _AllReduce("small_mats"), _AllReduce("small_vecs")

    def begin_mats(self, grads):
        return self.mats.begin([_diag_blocks(grads[n]) for n in self.mat_names], BF16)

    def middle_mats(self):
        return self.mats.middle()

    def begin(self, loss, grads):
        vecs = [loss] + [grads[n] for n in self.vec_names]
        self.shapes = [v.shape for v in vecs]
        return self.vecs.begin([_pack_rows(vecs)])

    def middle(self):
        return self.vecs.middle()

    def end(self):
        packed, = self.vecs.end()
        sums = _unpack_rows(packed, self.shapes)
        return sums[0], dict(zip(self.vec_names, sums[1:])), dict(zip(self.mat_names, self.mats.end()))


def _block_diag(w):
    groups = []
    for g in range(N_RNN_GROUPS):
        placed = [jnp.pad(w[4 * g + b], ((RNN_BLOCK_W * b, RNN_BLOCK_W * (3 - b)),) * 2) for b in range(4)]
        groups.append(placed[0] + placed[1] + placed[2] + placed[3])
    return jnp.stack(groups)


def _diag_blocks(wg):
    blocks = []
    for n in range(4 * N_RNN_GROUPS):
        g, at = n // 4, RNN_BLOCK_W * (n % 4)
        blocks.append(wg[g, at:at + RNN_BLOCK_W, at:at + RNN_BLOCK_W])
    return jnp.stack(blocks)


def _heads_major(t, n_heads):
    return t.reshape(S, n_heads, HEAD_DIM).transpose(1, 0, 2)


def _heads_minor(t):
    return t.transpose(1, 0, 2).reshape(S, t.shape[0] * HEAD_DIM)


def _natural(gathered, how):
    n, r, c = gathered.shape
    if how == "rows":
        return gathered.reshape(n * r, c)
    return gathered.transpose(1, 0, 2).reshape(r, n * c)


def _blocks(full, how):
    if how == "rows":
        return full.reshape(N_DEV, full.shape[0] // N_DEV, full.shape[1])
    return full.reshape(full.shape[0], N_DEV, full.shape[1] // N_DEV).transpose(1, 0, 2)


def _cast_many(arrays, side=None):
    steps = 4

    def body(*refs):
        n = len(refs) // 2
        for src, dst in zip(refs[:n], refs[n:]):
            dst[...] = src[...].astype(dst.dtype)

    specs = [pl.BlockSpec((a.shape[0] // steps, a.shape[1]), lambda i: (i, 0)) for a in arrays]
    return _call(
        body,
        name="cast_weights",
        grid=(steps,),
        in_specs=specs,
        out_specs=specs,
        out_shape=[jax.ShapeDtypeStruct(a.shape, MXU_DTYPE) for a in arrays],
        semantics=("parallel",),
        operands=tuple(arrays),
        side=side,
    )


def _forward_backward(x2, xb, target, small, gather, scatter, sync):
    w_in_t = _natural(gather.get("w_in"), "rows")
    proj, projb = _mm(xb, w_in_t, tb=True, tm=S, tn=512, tk=D, out_dtype=(F32, MXU_DTYPE), name="proj",
                      side=gather.take(110))

    qt = projb[:, :OFF_K].T.reshape(N_KV, GROUP, HEAD_DIM, S)
    k2, v2 = projb[:, OFF_K:OFF_V], projb[:, OFF_V:OFF_RX]
    kp = jnp.pad(_heads_major(k2, N_KV), ((0, 0), (BLOCK, 0), (0, 0)))
    vp = jnp.pad(_heads_major(v2, N_KV), ((0, 0), (BLOCK, 0), (0, 0)))
    kt = jnp.pad(k2.T.reshape(N_KV, HEAD_DIM, S), ((0, 0), (0, 0), (BLOCK, 0)))
    vt = jnp.pad(v2.T.reshape(N_KV, HEAD_DIM, S), ((0, 0), (0, 0), (BLOCK, 0)))
    sink_row = jnp.repeat(small["attn_sinks"].reshape(N_KV, 1, GROUP), BLOCK, axis=2)
    ot = _attn_fwd(qt, kp, vt, sink_row, side=gather.take(36)).reshape(D, S)

    rconv_w = _natural(gather.get("rnn_conv_w"), "cols")
    rxc = _conv_fwd(proj, OFF_RX, rconv_w, small["rnn_conv_b"], tc=512, name="rnn_conv_fwd", side=gather.take(18))
    r, i = _lru_gates(rxc, small["lru_wa"], small["lru_wi"], small["lru_ba"], small["lru_bi"], side=gather.take(33))
    h, yrin = _lru_scan_fwd(r, i, rxc, proj, small["lru_lambda"], side=gather.take(53))

    w_ap = _natural(gather.get("w_attn_proj"), "rows")
    w_rp = _natural(gather.get("w_rnn_proj"), "rows")
    y_attn = _mm(ot, w_ap, ta=True, tm=1024, tn=1024, tk=D, name="attn_proj", side=gather.take(22))
    y_rnn = _mm(yrin, w_rp, tm=1024, tn=1024, tk=D_RNN, name="rnn_proj", side=gather.take(27))
    mixin = _gate_fwd(y_attn, y_rnn, proj, small["b_gate"], side=gather.take(25))
    w_out = _natural(gather.get("w_out"), "rows")
    mix = _mm(mixin, w_out, tm=1024, tn=1024, tk=D, name="mix_out", side=gather.take(22))
    x1, x1b, xhat1, rstd1 = _ln_fwd(x2, mix, small["ln1_g"], small["ln1_b"], side=gather.take(23))

    w_up = gather.get("ffn_w_up")
    up = _mm(x1b, w_up, tm=S, tn=768, tk=D, b_block=768, name="ffn_up", side=gather.take(58))
    w_gate = gather.get("ffn_w_gate")
    fconv_w = _natural(gather.get("ffn_conv_w"), "cols")
    gpre, fin = _ffn_gate_act(x1b, w_gate, up, fconv_w, small["ffn_conv_b"], side=gather.take(58))
    w_down = _natural(gather.get("ffn_w_down"), "rows")
    f = _mm(fin, w_down, tm=1024, tn=1024, tk=2048, name="ffn_down")
    loss, dpre2, dpre2b, d_ln2_g, d_ln2_b = _ln_loss_bwd(x1, f, small["ln2_g"], small["ln2_b"], target)

    grads = {"ln2_g": d_ln2_g, "ln2_b": d_ln2_b}
    both = (F32, BF16)
    g32, g16 = _mm(fin, dpre2b, ta=True, tm=1024, tn=1024, tk=S, out_dtype=both, name="d_ffn_w_down")
    scatter.add_blocks("ffn_w_down", _blocks(g32, "rows"), _blocks(g16, "rows"))
    dfin = _mm(dpre2b, w_down, tb=True, tm=1024, tn=1024, tk=D, name="d_fin", side=scatter.take(57))
    dup, dgpre, grads["ffn_conv_w"], grads["ffn_conv_b"] = _ffn_act_bwd(
        dfin, up, gpre, fconv_w, small["ffn_conv_b"], side=scatter.take(85))
    g32, g16 = _mm(x1b, dup, ta=True, tm=1024, tn=768, tk=S, out_dtype=both, out_block=768, name="d_ffn_w_up",
                   side=scatter.take(57))
    scatter.add_blocks("ffn_w_up", g32, g16)
    g32, g16 = _mm(x1b, dgpre, ta=True, tm=1024, tn=768, tk=S, out_dtype=both, out_block=768, name="d_ffn_w_gate",
                   side=scatter.take(56))
    scatter.add_blocks("ffn_w_gate", g32, g16)
    dx1 = _mm(dup, w_up, tb=True, tm=1024, tn=1024, tk=768, b_block=768, name="d_x1_up", side=scatter.take(68))
    dx1 = _mm(dgpre, w_gate, tb=True, tm=1024, tn=1024, tk=768, b_block=768, add=dx1, name="d_x1_gate",
              side=scatter.take(70))
    dpre1, dpre1b, grads["ln1_g"], grads["ln1_b"] = _ln_bwd(dx1, dpre2, xhat1, rstd1, small["ln1_g"],
                                                            side=scatter.take(24))

    g32, g16 = _mm(mixin, dpre1b, ta=True, tm=1024, tn=1024, tk=S, out_dtype=both, name="d_w_out",
                   side=scatter.take(26))
    scatter.add_blocks("w_out", _blocks(g32, "rows"), _blocks(g16, "rows"))
    dmix = _mm(dpre1b, w_out, tb=True, tm=1024, tn=1024, tk=D, name="d_mixin", side=scatter.take(22))
    dproj, dya, dyr, dgl_r, db_a, db_r = _gate_bwd(
        dmix, y_attn, y_rnn, proj, small["b_gate"], side=scatter.take(36),
        window=(jax.ShapeDtypeStruct((S, D_IN), MXU_DTYPE), OFF_GA))
    grads["b_gate"] = jnp.concatenate([db_a, db_r], axis=1)
    g32, g16 = _mm(ot, dya, tm=1024, tn=1024, tk=S, out_dtype=both, name="d_w_attn_proj", side=scatter.take(38))
    scatter.add_blocks("w_attn_proj", _blocks(g32, "rows"), _blocks(g16, "rows"))
    g32, g16 = _mm(yrin, dyr, ta=True, tm=1280, tn=1024, tk=S, out_dtype=both, name="d_w_rnn_proj",
                   side=scatter.take(27))
    scatter.add_blocks("w_rnn_proj", _blocks(g32, "rows"), _blocks(g16, "rows"))
    dot_ = _mm(w_ap, dya, tb=True, tm=1024, tn=1024, tk=D, out_dtype=MXU_DTYPE, name="d_o", side=scatter.take(22))
    dyrin = _mm(dyr, w_rp, tb=True, tm=1024, tn=1280, tk=D, name="d_yrin", side=scatter.take(27))

    dproj, dzr, dzi, drxc_in, grads["lru_ba"], grads["lru_bi"], grads["lru_lambda"] = _lru_scan_bwd(
        dyrin, proj, h, r, i, rxc, small["lru_lambda"], side=scatter.take(94), window=(dproj, OFF_RY))
    grads["lru_wa"], grads["lru_wi"] = _lru_gate_wgrad(rxc, dzr, dzi, side=scatter.take(22))
    drxc = _lru_gate_xgrad(dzr, dzi, small["lru_wa"], small["lru_wi"], drxc_in, side=scatter.take(33))
    dproj, grads["rnn_conv_w"], grads["rnn_conv_b"] = _conv_bwd(
        drxc, proj, OFF_RX, rconv_w, tc=512, name="rnn_conv_bwd", side=scatter.take(30), window=(dproj, OFF_RX))

    dqt, dk, dv, dsink = _attn_bwd(qt, kp, kt, vp, sink_row, dot_.reshape(N_KV, GROUP, HEAD_DIM, S),
                                   side=_join(scatter.take(50), sync.begin_mats(grads)))
    grads["attn_sinks"] = dsink.reshape(1, N_KV * GROUP)
    for col0, piece in ((0, dqt.reshape(D, S).T), (OFF_K, _heads_minor(dk[:, BLOCK:, :]).astype(MXU_DTYPE)),
                        (OFF_V, _heads_minor(dv[:, BLOCK:, :]).astype(MXU_DTYPE)), (OFF_GR, dgl_r)):
        dproj = lax.dynamic_update_slice(dproj, piece, (0, col0))
    for part in range(W_IN_PARTS):
        cols = slice(part * (D // W_IN_PARTS), (part + 1) * (D // W_IN_PARTS))
        if part == 0:
            side = _join(scatter.take(55), sync.middle_mats(), sync.begin(loss, grads))
        else:
            side = scatter.take(68)
        g32, g16 = _mm(dproj, xb[:, cols], ta=True, tm=512, tn=D // W_IN_PARTS, tk=S, out_dtype=both,
                       name="d_w_in_%d" % part, side=side)
        scatter.add_blocks("w_in_%d" % part, _blocks(g32, "rows"), _blocks(g16, "rows"))
        scatter.flush_pairs("pairs_w_in_%d" % part)
    dx = _mm(dproj, w_in_t, tm=1024, tn=1024, tk=512, add=dpre1, add_scale=ALPHA, name="d_x",
             side=_join(scatter.take(400), sync.middle()))
    return dx


SHARDED = (
    ("w_in", "cols", 368), ("w_attn_proj", "rows", 32), ("w_rnn_proj", "rows", 32), ("w_out", "rows", 32),
    ("ffn_w_up", "cols", 128), ("ffn_w_gate", "cols", 128), ("ffn_w_down", "rows", 64),
)
SMALL_REPLICATED = ("b_gate", "rnn_conv_b", "lru_wa", "lru_ba", "lru_wi", "lru_bi", "lru_lambda", "attn_sinks",
                    "ln1_g", "ln1_b", "ffn_conv_b", "ln2_g", "ln2_b")
SMALL_SHARDED = ("rnn_conv_w", "ffn_conv_w")
SMALL_MATS = ("lru_wa", "lru_wi")
W_IN_PARTS = 2
WEIGHTS = ("w_in", "b_gate", "rnn_conv_w", "rnn_conv_b", "lru_wa", "lru_ba", "lru_wi", "lru_bi", "lru_lambda",
           "attn_sinks", "w_attn_proj", "w_rnn_proj", "w_out", "ln1_g", "ln1_b", "ffn_w_up", "ffn_w_gate",
           "ffn_conv_w", "ffn_conv_b", "ffn_w_down", "ln2_g", "ln2_b")


def kernel(x, w_in, b_gate, rnn_conv_w, rnn_conv_b, lru_wa, lru_ba, lru_wi, lru_bi, lru_lambda, attn_sinks, w_attn_proj, w_rnn_proj, w_out, ln1_g, ln1_b, ffn_w_up, ffn_w_gate, ffn_conv_w, ffn_conv_b, ffn_w_down, ln2_g, ln2_b, loss_target, m_w_in, m_b_gate, m_rnn_conv_w, m_rnn_conv_b, m_lru_wa, m_lru_ba, m_lru_wi, m_lru_bi, m_lru_lambda, m_attn_sinks, m_w_attn_proj, m_w_rnn_proj, m_w_out, m_ln1_g, m_ln1_b, m_ffn_w_up, m_ffn_w_gate, m_ffn_conv_w, m_ffn_conv_b, m_ffn_w_down, m_ln2_g, m_ln2_b, v_w_in, v_b_gate, v_rnn_conv_w, v_rnn_conv_b, v_lru_wa, v_lru_ba, v_lru_wi, v_lru_bi, v_lru_lambda, v_attn_sinks, v_w_attn_proj, v_w_rnn_proj, v_w_out, v_ln1_g, v_ln1_b, v_ffn_w_up, v_ffn_w_gate, v_ffn_conv_w, v_ffn_conv_b, v_ffn_w_down, v_ln2_g, v_ln2_b):
    given = dict(locals())
    wsh = {n: given[n][0] for n in WEIGHTS}
    msh = {n: given["m_" + n][0] for n in WEIGHTS}
    vsh = {n: given["v_" + n][0] for n in WEIGHTS}
    m_given = {n: given["m_" + n] for n in WEIGHTS}
    v_given = {n: given["v_" + n] for n in WEIGHTS}
    me = 4 * lax.axis_index("x") + 2 * lax.axis_index("y") + lax.axis_index("c")

    order = ("w_in", "rnn_conv_w", "ffn_conv_w", "w_attn_proj", "w_rnn_proj", "w_out", "ffn_w_up", "ffn_w_gate",
             "ffn_w_down")
    gather = _Gather({"w_in": wsh["w_in"].T.astype(MXU_DTYPE), **{n: wsh[n] for n in order[1:3]}})
    *casts, xb = _cast_many([wsh[n] for n in order[3:]] + [x[0]], side=gather.take(through="ffn_conv_w"))
    gather.add_shards(dict(zip(order[3:], casts)))
    small = {n: given[n] for n in SMALL_REPLICATED}
    small["lru_wa"] = _block_diag(wsh["lru_wa"])
    small["lru_wi"] = _block_diag(wsh["lru_wi"])
    scatter = _Scatter(me, jnp.stack([_index(_relative(k)) for k in FAR]).astype(jnp.int32))

    vec_names = tuple(n for n in SMALL_REPLICATED if n not in SMALL_MATS) + SMALL_SHARDED
    sync = _SmallSync(vec_names, SMALL_MATS)
    dx = _forward_backward(x[0], xb, loss_target[0], small, gather, scatter, sync)

    loss_total, g_small, mat_sums = sync.end()
    loss_total = loss_total.reshape(())
    for n in SMALL_SHARDED:
        width = wsh[n].shape[1]
        g_small[n] = lax.dynamic_slice_in_dim(g_small[n], me * width, width, axis=1)
    g_small = {n: g_small[n].reshape(given[n].shape) for n in vec_names}
    out = {}
    results = _adamw_many(*[[d[n] for n in vec_names] for d in (given, m_given, v_given, g_small)])
    for n, delta, nm, nv in zip(vec_names, *results):
        out[n] = (g_small[n], delta, nm, nv)
    for n in SMALL_MATS:
        g = mat_sums[n].reshape(given[n].shape)
        out[n] = (g, *_adamw_blocks(given[n], m_given[n], v_given[n], g, name="adamw_" + n))

    tile_rows = {n: tr for n, _, tr in SHARDED}
    me1 = me.reshape(1).astype(jnp.int32)
    res = None
    for n in list(scatter.sends):
        own, pair, far = scatter.get(n)
        if n.startswith("w_in_"):
            part = int(n[len("w_in_"):])
            w_t, m_t, v_t = (a["w_in"].transpose(0, 2, 1) for a in (given, m_given, v_given))
            res = _reduce_adamw(w_t, m_t, v_t, own, pair, far, me1, tr=tile_rows["w_in"], name="adamw_" + n,
                                part=part, earlier=res if part else None)
            out["w_in"] = tuple(r.transpose(0, 2, 1) for r in res)
        else:
            out[n] = tuple(_reduce_adamw(given[n], m_given[n], v_given[n], own, pair, far, me1, tr=tile_rows[n],
                                         name="adamw_" + n))

    outputs = [loss_total, dx[None]]
    for kind in range(4):
        outputs += [out[n][kind] for n in WEIGHTS]
    return tuple(outputs)
```

```python
import math

import jax
import jax.numpy as jnp
from jax import lax
from jax.experimental import pallas as pl
from jax.experimental.pallas import tpu as pltpu

F32 = jnp.float32
BF16 = jnp.bfloat16
MXU_DTYPE = jnp.bfloat16

N_DEV = 8
S = 2048
D = 2048
HEAD_DIM = 64
N_KV = 4
GROUP = 8
BLOCK = 128
D_KV = N_KV * HEAD_DIM
D_RNN = 2560
RNN_GROUP = 640
N_RNN_GROUPS = D_RNN // RNN_GROUP
RNN_BLOCK_W = 160
RNN_CONV_W = 4
LRU_C = 8.0
D_FF = 6144
FFN_CONV_W = 3
D_IN = 11776
OFF_K = 2048
OFF_V = 2304
OFF_RX = 2560
OFF_RY = 5120
OFF_GA = 7680
OFF_GR = 9728
LN_EPS = 1e-5
ALPHA = 2.0 ** 0.25
ADAM_LR = 0.001
ADAM_B1 = 0.9
ADAM_B2 = 0.999
ADAM_EPS = 1e-08
ADAM_WD = 0.01
ADAM_STEP = 10
NEG = -1e30
VMEM_LIMIT = 56 * 1024 * 1024
MID_RIDE_TENTHS = 6
MESH = pl.DeviceIdType.MESH
GELU_C = math.sqrt(2.0 / math.pi)


def _cparams(*sem):
    return pltpu.CompilerParams(dimension_semantics=sem or None, vmem_limit_bytes=VMEM_LIMIT)


def _call(body, *, name, grid, in_specs, out_specs, out_shape, operands, semantics, scratch_shapes=(), side=None,
          window=None):
    single = not isinstance(out_shape, (list, tuple))
    out_shape = [out_shape] if single else list(out_shape)
    out_specs = [out_specs] if single else list(out_specs)
    in_specs = list(in_specs)
    operands = tuple(operands)
    scratch_shapes = list(scratch_shapes)
    hbm = pl.BlockSpec(memory_space=pltpu.HBM)
    aliases = {}
    if window is not None:
        whole, col0 = window
        block, index_map = out_specs[0].block_shape, out_specs[0].index_map
        assert col0 % block[1] == 0 and out_shape[0].dtype == whole.dtype
        out_specs[0] = pl.BlockSpec(block, lambda *g: (index_map(*g)[0], index_map(*g)[1] + col0 // block[1]))
        out_shape[0] = jax.ShapeDtypeStruct(whole.shape, whole.dtype)
        if not isinstance(whole, jax.ShapeDtypeStruct):
            aliases[len(in_specs)] = 0
            in_specs.append(hbm)
            operands += (whole,)
            compute, n_read = body, len(in_specs) - 1

            def body(*refs):
                compute(*refs[:n_read], *refs[n_read + 1:])

    if side is None:
        res = pl.pallas_call(
            body, name=name, grid=grid, in_specs=in_specs, out_specs=out_specs, out_shape=out_shape,
            scratch_shapes=scratch_shapes, input_output_aliases=aliases,
            compiler_params=_cparams(*semantics))(*operands)
        return res[0] if single else res
    n_in, n_out, n_scr = len(in_specs), len(out_shape), len(scratch_shapes)
    s_in, s_out = len(side.operands), len(side.out_shape)
    steps = math.prod(grid)
    mid_step = (steps * MID_RIDE_TENTHS) // 10

    def with_copies(*refs):
        core_in, side_in = refs[:n_in], refs[n_in:n_in + s_in]
        o0 = n_in + s_in
        core_out, side_out = refs[o0:o0 + n_out], refs[o0 + n_out:o0 + n_out + s_out]
        c0 = o0 + n_out + s_out
        core_scr, sems = refs[c0:c0 + n_scr], refs[c0 + n_scr:]
        step = 0
        for d, size in enumerate(grid):
            step = step * size + pl.program_id(d)

        @pl.when(step == 0)
        def _():
            side.start(side_in, side_out, sems)

        body(*core_in, *core_out, *core_scr)

        @pl.when(step == mid_step)
        def _():
            side.mid(side_in, side_out, sems)

        @pl.when(step == steps - 1)
        def _():
            side.finish(side_in, side_out, sems)

    res = pl.pallas_call(
        with_copies, name=name, grid=grid,
        in_specs=in_specs + [hbm] * s_in, out_specs=out_specs + [hbm] * s_out,
        out_shape=out_shape + list(side.out_shape),
        scratch_shapes=scratch_shapes + list(side.sems),
        input_output_aliases={**aliases, **{n_in + i: n_out + o for i, o in side.aliases.items()}},
        compiler_params=_cparams(*(("arbitrary",) * len(grid))))(*operands, *side.operands)
    side.done(res[n_out:])
    return res[0] if single else res[:n_out]


def _run_side(side, name):
    def body(*refs):
        s_in, s_out = len(side.operands), len(side.out_shape)
        side.start(refs[:s_in], refs[s_in:s_in + s_out], refs[s_in + s_out:])
        side.mid(refs[:s_in], refs[s_in:s_in + s_out], refs[s_in + s_out:])
        side.finish(refs[:s_in], refs[s_in:s_in + s_out], refs[s_in + s_out:])

    hbm = pl.BlockSpec(memory_space=pltpu.HBM)
    res = pl.pallas_call(
        body, name=name, in_specs=[hbm] * len(side.operands), out_specs=[hbm] * len(side.out_shape),
        out_shape=list(side.out_shape), scratch_shapes=list(side.sems),
        input_output_aliases=dict(side.aliases))(*side.operands)
    side.done(res)


def _gelu(x):
    x2 = x * x
    t = jnp.tanh(GELU_C * (x + 0.044715 * x * x2))
    g = 0.5 * x * (1.0 + t)
    dg = 0.5 * (1.0 + t) + 0.5 * x * (1.0 - t * t) * (GELU_C * (1.0 + 3.0 * 0.044715 * x2))
    return g, dg


def _sigmoid(x):
    return 1.0 / (1.0 + jnp.exp(-x))


def _softplus(x):
    z = jnp.exp(-jnp.abs(x))
    small = z * (1.0 - z * (0.5 - z * (1.0 / 3.0 - 0.25 * z)))
    return jnp.maximum(x, 0.0) + jnp.where(z < 0.02, small, jnp.log(1.0 + z))


def _one_minus_exp(x):
    series = -x * (1.0 + x * (0.5 + x * (1.0 / 6.0 + x * (1.0 / 24.0))))
    return jnp.where(x > -0.03, series, 1.0 - jnp.exp(x))


def _colsum(v):
    return jnp.sum(v, axis=0, keepdims=True)


def _mm(a, b, *, tm, tn, tk, name, ta=False, tb=False, out_dtype=F32, b_block=None, out_block=None, add=None,
        add_scale=1.0, side=None):
    out_dtypes = out_dtype if isinstance(out_dtype, tuple) else (out_dtype,)
    if ta:
        k_dim, m_dim = a.shape
    else:
        m_dim, k_dim = a.shape
    if b_block is None:
        n_dim = b.shape[0] if tb else b.shape[1]
    else:
        n_dim = b.shape[1] if tb else b.shape[0] * b_block
    assert m_dim % tm == 0 and n_dim % tn == 0 and k_dim % tk == 0, (name, m_dim, n_dim, k_dim)
    nk = k_dim // tk
    dims = (((0 if ta else 1,), (1 if tb else 0,)), ((), ()))
    has_add = add is not None

    def body(*refs):
        a_ref, b_ref = refs[0], refs[1]
        add_ref = refs[2] if has_add else None
        first_out = 3 if has_add else 2
        o_refs = refs[first_out:first_out + len(out_dtypes)]

        def product():
            return lax.dot_general(a_ref[...].astype(MXU_DTYPE), b_ref[...].astype(MXU_DTYPE), dims,
                                   preferred_element_type=F32)

        def finish(acc):
            if has_add:
                acc = acc + add_scale * add_ref[...]
            for o_ref in o_refs:
                o_ref[...] = acc.astype(o_ref.dtype)

        if nk == 1:
            finish(product())
        else:
            acc_ref = refs[-1]
            k = pl.program_id(2)

            @pl.when(k == 0)
            def _():
                acc_ref[...] = jnp.zeros_like(acc_ref)

            acc_ref[...] += product()

            @pl.when(k == nk - 1)
            def _():
                finish(acc_ref[...])

    if ta:
        a_spec = pl.BlockSpec((tk, tm), lambda i, j, k: (k, i))
    else:
        a_spec = pl.BlockSpec((tm, tk), lambda i, j, k: (i, k))
    if b_block is None:
        if tb:
            b_spec = pl.BlockSpec((tn, tk), lambda i, j, k: (j, k))
        else:
            b_spec = pl.BlockSpec((tk, tn), lambda i, j, k: (k, j))
    elif tb:
        assert b_block % tk == 0
        b_spec = pl.BlockSpec((None, tn, tk), lambda i, j, k: ((k * tk) // b_block, j, ((k * tk) % b_block) // tk))
    else:
        assert b_block % tn == 0
        b_spec = pl.BlockSpec((None, tk, tn), lambda i, j, k: ((j * tn) // b_block, k, ((j * tn) % b_block) // tn))
    in_specs = [a_spec, b_spec]
    operands = [a, b]
    if has_add:
        in_specs.append(pl.BlockSpec((tm, tn), lambda i, j, k: (i, j)))
        operands.append(add)
    if out_block is None:
        out_spec = pl.BlockSpec((tm, tn), lambda i, j, k: (i, j))
        out_dims = (m_dim, n_dim)
    else:
        assert out_block % tn == 0
        out_spec = pl.BlockSpec((None, tm, tn), lambda i, j, k: ((j * tn) // out_block, i, ((j * tn) % out_block) // tn))
        out_dims = (n_dim // out_block, m_dim, out_block)
    res = _call(
        body,
        name=name,
        grid=(m_dim // tm, n_dim // tn, nk),
        in_specs=in_specs,
        out_specs=[out_spec] * len(out_dtypes),
        out_shape=[jax.ShapeDtypeStruct(out_dims, dt) for dt in out_dtypes],
        scratch_shapes=[pltpu.VMEM((tm, tn), F32)] if nk > 1 else [],
        semantics=("parallel", "parallel", "arbitrary"),
        operands=tuple(operands),
        side=side,
    )
    return res if isinstance(out_dtype, tuple) else res[0]


def _attn_bias(bias_ref, h):
    key = lax.broadcasted_iota(jnp.int32, (2 * BLOCK, GROUP * BLOCK), 0)
    col = lax.broadcasted_iota(jnp.int32, (2 * BLOCK, GROUP * BLOCK), 1)
    dist = BLOCK + (col & (BLOCK - 1)) - key
    head = h * GROUP + (col >> 7) + 1
    slope = jnp.exp(head.astype(F32) * (-0.25 * math.log(2.0)))
    bias = jnp.where((dist >= 0) & (dist < BLOCK), -slope * dist.astype(F32), NEG)
    bias_ref[1] = bias
    bias_ref[0] = jnp.where(key < BLOCK, NEG, bias)


def _attn_probs(kb, qt, bias, sink):
    s = jnp.dot(kb, qt, preferred_element_type=F32) * (HEAD_DIM ** -0.5) + bias
    m = jnp.maximum(jnp.max(s, axis=0, keepdims=True), sink)
    e = jnp.exp(s - m)
    e_sink = jnp.exp(sink - m)
    inv = 1.0 / (jnp.sum(e, axis=0, keepdims=True) + e_sink)
    return e * inv, e_sink * inv


def _heads_on_lanes(ref, r0):
    return jnp.concatenate([ref[g, :, pl.ds(r0, BLOCK)] for g in range(GROUP)], axis=1)


def _attn_fwd(qt, kp, vt, sink_row, side=None):
    cols = GROUP * BLOCK

    def body(q_ref, k_ref, vt_ref, sink_ref, o_ref, bias_ref):
        _attn_bias(bias_ref, pl.program_id(0))
        sink = sink_ref[...]

        def step(n, carry):
            r0 = pl.multiple_of(n * BLOCK, BLOCK)
            p, _ = _attn_probs(k_ref[pl.ds(r0, 2 * BLOCK), :], _heads_on_lanes(q_ref, r0),
                               bias_ref[jnp.minimum(n, 1)], sink)
            o = jnp.dot(vt_ref[:, pl.ds(r0, 2 * BLOCK)], p.astype(MXU_DTYPE), preferred_element_type=F32)
            for g in range(GROUP):
                o_ref[g, :, pl.ds(r0, BLOCK)] = o[:, g * BLOCK:(g + 1) * BLOCK].astype(o_ref.dtype)
            return carry

        lax.fori_loop(0, S // BLOCK, step, 0)

    hm = pl.BlockSpec((None, GROUP, HEAD_DIM, S), lambda h: (h, 0, 0, 0))
    return _call(
        body,
        name="attn_fwd",
        grid=(N_KV,),
        in_specs=[
            hm,
            pl.BlockSpec((None, BLOCK + S, HEAD_DIM), lambda h: (h, 0, 0)),
            pl.BlockSpec((None, HEAD_DIM, BLOCK + S), lambda h: (h, 0, 0)),
            pl.BlockSpec((None, 1, cols), lambda h: (h, 0, 0)),
        ],
        out_specs=hm,
        out_shape=jax.ShapeDtypeStruct((N_KV, GROUP, HEAD_DIM, S), MXU_DTYPE),
        scratch_shapes=[pltpu.VMEM((2, 2 * BLOCK, cols), F32)],
        semantics=("parallel",),
        operands=(qt, kp, vt, sink_row),
        side=side,
    )


def _attn_bwd(qt, kp, kt, vp, sink_row, dot_, side=None):
    cols = GROUP * BLOCK

    def body(q_ref, k_ref, kt_ref, v_ref, sink_ref, do_ref, dq_ref, dk_ref, dv_ref, dsink_ref, bias_ref):
        _attn_bias(bias_ref, pl.program_id(0))
        sink = sink_ref[...]
        dk_ref[...] = jnp.zeros_like(dk_ref)
        dv_ref[...] = jnp.zeros_like(dv_ref)
        nt = (((1,), (1,)), ((), ()))

        def step(n, sink_acc):
            r0 = pl.multiple_of(n * BLOCK, BLOCK)
            band = pl.ds(r0, 2 * BLOCK)
            qn = _heads_on_lanes(q_ref, r0)
            don = _heads_on_lanes(do_ref, r0)
            p, p_sink = _attn_probs(k_ref[band, :], qn, bias_ref[jnp.minimum(n, 1)], sink)
            dp = jnp.dot(v_ref[band, :], don, preferred_element_type=F32)
            delta = jnp.sum(p * dp, axis=0, keepdims=True)
            ds = (p * (dp - delta) * (HEAD_DIM ** -0.5)).astype(MXU_DTYPE)
            dq = jnp.dot(kt_ref[:, band], ds, preferred_element_type=F32)
            for g in range(GROUP):
                dq_ref[g, :, pl.ds(r0, BLOCK)] = dq[:, g * BLOCK:(g + 1) * BLOCK].astype(dq_ref.dtype)
            dk_ref[band, :] += lax.dot_general(ds, qn, nt, preferred_element_type=F32)
            dv_ref[band, :] += lax.dot_general(p.astype(MXU_DTYPE), don, nt, preferred_element_type=F32)
            return sink_acc - p_sink * delta

        sink_acc = lax.fori_loop(0, S // BLOCK, step, jnp.zeros((1, cols), F32))
        for g in range(GROUP):
            dsink_ref[g:g + 1, :] = jnp.sum(sink_acc[:, g * BLOCK:(g + 1) * BLOCK], axis=1, keepdims=True)

    hm = pl.BlockSpec((None, GROUP, HEAD_DIM, S), lambda h: (h, 0, 0, 0))
    kv = pl.BlockSpec((None, BLOCK + S, HEAD_DIM), lambda h: (h, 0, 0))
    return _call(
        body,
        name="attn_bwd",
        grid=(N_KV,),
        in_specs=[hm, kv, pl.BlockSpec((None, HEAD_DIM, BLOCK + S), lambda h: (h, 0, 0)), kv,
                  pl.BlockSpec((None, 1, cols), lambda h: (h, 0, 0)), hm],
        out_specs=[hm, kv, kv, pl.BlockSpec((None, GROUP, 1), lambda h: (h, 0, 0))],
        out_shape=[
            jax.ShapeDtypeStruct((N_KV, GROUP, HEAD_DIM, S), MXU_DTYPE),
            jax.ShapeDtypeStruct((N_KV, BLOCK + S, HEAD_DIM), F32),
            jax.ShapeDtypeStruct((N_KV, BLOCK + S, HEAD_DIM), F32),
            jax.ShapeDtypeStruct((N_KV, GROUP, 1), F32),
        ],
        scratch_shapes=[pltpu.VMEM((2, 2 * BLOCK, cols), F32)],
        semantics=("parallel",),
        operands=(qt, kp, kt, vp, sink_row, dot_),
        side=side,
    )


PAD = 8
CHUNK = 256


def _past_taps(xpad_ref, r0, width):
    ext = xpad_ref[pl.ds(r0, CHUNK + PAD), :]
    taps = []
    for k in range(width):
        back = width - 1 - k
        taps.append((ext if back == 0 else pltpu.roll(ext, back, 0))[PAD:, :])
    return taps


def _future_taps(xpad_ref, r0, width):
    ext = xpad_ref[pl.ds(r0, CHUNK + PAD), :]
    taps = []
    for ahead in range(width):
        taps.append((ext if ahead == 0 else pltpu.roll(ext, CHUNK + PAD - ahead, 0))[:CHUNK, :])
    return taps


def _conv_fwd(src, col0, w, b, *, tc, name, side=None):
    width, c_dim = w.shape

    def body(x_ref, w_ref, b_ref, o_ref, xpad_ref):
        xpad_ref[pl.ds(0, PAD), :] = jnp.zeros((PAD, tc), F32)
        xpad_ref[pl.ds(PAD, S), :] = x_ref[...]
        wv = w_ref[...]
        bv = b_ref[...]

        def step(ci, carry):
            r0 = pl.multiple_of(ci * CHUNK, CHUNK)
            taps = _past_taps(xpad_ref, r0, width)
            y = bv + taps[0] * wv[0:1, :]
            for k in range(1, width):
                y = y + taps[k] * wv[k:k + 1, :]
            o_ref[pl.ds(r0, CHUNK), :] = y
            return carry

        lax.fori_loop(0, S // CHUNK, step, 0)

    return _call(
        body,
        name=name,
        grid=(c_dim // tc,),
        in_specs=[
            pl.BlockSpec((S, tc), lambda j: (0, col0 // tc + j)),
            pl.BlockSpec((width, tc), lambda j: (0, j)),
            pl.BlockSpec((1, tc), lambda j: (0, j)),
        ],
        out_specs=pl.BlockSpec((S, tc), lambda j: (0, j)),
        out_shape=jax.ShapeDtypeStruct((S, c_dim), F32),
        scratch_shapes=[pltpu.VMEM((S + PAD, tc), F32)],
        semantics=("parallel",),
        operands=(src, w, b),
        side=side,
    )


def _conv_bwd(dy, src, col0, w, *, tc, name, side=None, window=None):
    width, c_dim = w.shape

    def body(dy_ref, x_ref, w_ref, dx_ref, dw_ref, db_ref, xpad_ref, dpad_ref):
        xpad_ref[pl.ds(0, PAD), :] = jnp.zeros((PAD, tc), F32)
        xpad_ref[pl.ds(PAD, S), :] = x_ref[...]
        dpad_ref[pl.ds(0, S), :] = dy_ref[...]
        dpad_ref[pl.ds(S, PAD), :] = jnp.zeros((PAD, tc), F32)
        wv = w_ref[...]

        def step(ci, acc):
            r0 = pl.multiple_of(ci * CHUNK, CHUNK)
            past = _past_taps(xpad_ref, r0, width)
            ahead = _future_taps(dpad_ref, r0, width)
            d = ahead[0]
            dx = d * wv[width - 1:width, :]
            for j in range(1, width):
                dx = dx + ahead[j] * wv[width - 1 - j:width - j, :]
            dx_ref[pl.ds(r0, CHUNK), :] = dx.astype(dx_ref.dtype)
            return tuple(acc[k] + _colsum(past[k] * d) for k in range(width)) + (acc[width] + _colsum(d),)

        zero = jnp.zeros((1, tc), F32)
        acc = lax.fori_loop(0, S // CHUNK, step, (zero,) * (width + 1))
        for k in range(width):
            dw_ref[k:k + 1, :] = acc[k]
        db_ref[...] = acc[width]

    return _call(
        body,
        name=name,
        grid=(c_dim // tc,),
        in_specs=[
            pl.BlockSpec((S, tc), lambda j: (0, j)),
            pl.BlockSpec((S, tc), lambda j: (0, col0 // tc + j)),
            pl.BlockSpec((width, tc), lambda j: (0, j)),
        ],
        out_specs=[
            pl.BlockSpec((S, tc), lambda j: (0, j)),
            pl.BlockSpec((width, tc), lambda j: (0, j)),
            pl.BlockSpec((1, tc), lambda j: (0, j)),
        ],
        out_shape=[
            jax.ShapeDtypeStruct((S, c_dim), MXU_DTYPE),
            jax.ShapeDtypeStruct((width, c_dim), F32),
            jax.ShapeDtypeStruct((1, c_dim), F32),
        ],
        scratch_shapes=[pltpu.VMEM((S + PAD, tc), F32), pltpu.VMEM((S + PAD, tc), F32)],
        semantics=("parallel",),
        operands=(dy, src, w),
        side=side,
        window=window,
    )


SCAN_TC = 256


def _lru_gates(rxc, wa, wi, ba, bi, side=None):
    tm = 512

    def body(x_ref, wa_ref, wi_ref, ba_ref, bi_ref, r_ref, i_ref):
        xv = x_ref[...].astype(MXU_DTYPE)
        r_ref[...] = _sigmoid(jnp.dot(xv, wa_ref[...].astype(MXU_DTYPE), preferred_element_type=F32) + ba_ref[...])
        i_ref[...] = _sigmoid(jnp.dot(xv, wi_ref[...].astype(MXU_DTYPE), preferred_element_type=F32) + bi_ref[...])

    x_spec = pl.BlockSpec((tm, RNN_GROUP), lambda g, i: (i, g))
    w_spec = pl.BlockSpec((None, RNN_GROUP, RNN_GROUP), lambda g, i: (g, 0, 0))
    b_spec = pl.BlockSpec((1, RNN_GROUP), lambda g, i: (0, g))
    return _call(
        body,
        name="lru_gates",
        grid=(N_RNN_GROUPS, S // tm),
        in_specs=[x_spec, w_spec, w_spec, b_spec, b_spec],
        out_specs=[x_spec, x_spec],
        out_shape=[jax.ShapeDtypeStruct((S, D_RNN), F32)] * 2,
        semantics=("parallel", "parallel"),
        operands=(rxc, wa, wi, ba, bi),
        side=side,
    )


def _scan_down(a, u, row):
    for d in (1, 2, 4):
        a_s = jnp.where(row >= d, pltpu.roll(a, d, 0), 1.0)
        u_s = jnp.where(row >= d, pltpu.roll(u, d, 0), 0.0)
        u = a * u_s + u
        a = a * a_s
    return a, u


def _scan_up(a, u, row):
    for d in (1, 2, 4):
        a_s = jnp.where(row < 8 - d, pltpu.roll(a, 8 - d, 0), 1.0)
        u_s = jnp.where(row < 8 - d, pltpu.roll(u, 8 - d, 0), 0.0)
        u = a * u_s + u
        a = a * a_s
    return a, u


def _lru_scan_fwd(r, i, rxc, proj, lam, side=None):
    tc = SCAN_TC

    def body(r_ref, i_ref, x_ref, ry_ref, lam_ref, h_ref, y_ref):
        rate = LRU_C * _softplus(-lam_ref[...])
        row = lax.broadcasted_iota(jnp.int32, (8, tc), 0)

        def step(ci, carry):
            r0 = pl.multiple_of(ci * 16, 16)
            log_a = -rate * r_ref[pl.ds(r0, 16), :]
            a16 = jnp.exp(log_a)
            u16 = jnp.sqrt(_one_minus_exp(2.0 * log_a)) * (i_ref[pl.ds(r0, 16), :] * x_ref[pl.ds(r0, 16), :])
            hs = []
            for half in range(2):
                a_cum, h0 = _scan_down(a16[8 * half:8 * half + 8, :], u16[8 * half:8 * half + 8, :], row)
                h = a_cum * carry + h0
                carry = jnp.broadcast_to(h[7:8, :], (8, tc))
                hs.append(h)
            h16 = jnp.concatenate(hs, axis=0)
            h_ref[pl.ds(r0, 16), :] = h16
            y_ref[pl.ds(r0, 16), :] = (h16 * _gelu(ry_ref[pl.ds(r0, 16), :])[0]).astype(y_ref.dtype)
            return carry

        lax.fori_loop(0, S // 16, step, jnp.zeros((8, tc), F32))

    col = pl.BlockSpec((S, tc), lambda j: (0, j))
    return _call(
        body,
        name="lru_scan_fwd",
        grid=(D_RNN // tc,),
        in_specs=[col, col, col, pl.BlockSpec((S, tc), lambda j: (0, OFF_RY // tc + j)),
                  pl.BlockSpec((1, tc), lambda j: (0, j))],
        out_specs=[col, col],
        out_shape=[jax.ShapeDtypeStruct((S, D_RNN), F32), jax.ShapeDtypeStruct((S, D_RNN), MXU_DTYPE)],
        semantics=("parallel",),
        operands=(r, i, rxc, proj, lam),
        side=side,
    )


def _lru_scan_bwd(dy, proj, h, r, i, rxc, lam, side=None, window=None):
    tc = SCAN_TC

    def body(dy_ref, ry_ref, h_ref, r_ref, i_ref, x_ref, lam_ref,
             dry_ref, dzr_ref, dzi_ref, dx_ref, dba_ref, dbi_ref, dlam_ref, a_ref, dh_ref, hp_ref):
        lam_v = lam_ref[...]
        rate = LRU_C * _softplus(-lam_v)
        dlam_scale = LRU_C * _sigmoid(-lam_v)
        row = lax.broadcasted_iota(jnp.int32, (8, tc), 0)
        hp_ref[pl.ds(0, PAD), :] = jnp.zeros((PAD, tc), F32)
        hp_ref[pl.ds(PAD, S), :] = h_ref[...]
        a_ref[pl.ds(S, PAD), :] = jnp.zeros((PAD, tc), F32)

        def prep(ci, carry):
            r0 = pl.multiple_of(ci * CHUNK, CHUNK)
            a_ref[pl.ds(r0, CHUNK), :] = jnp.exp(-rate * r_ref[pl.ds(r0, CHUNK), :])
            ge, dge = _gelu(ry_ref[pl.ds(r0, CHUNK), :])
            dyv = dy_ref[pl.ds(r0, CHUNK), :]
            dh_ref[pl.ds(r0, CHUNK), :] = dyv * ge
            dry_ref[pl.ds(r0, CHUNK), :] = (dyv * h_ref[pl.ds(r0, CHUNK), :] * dge).astype(dry_ref.dtype)
            return carry

        lax.fori_loop(0, S // CHUNK, prep, 0)

        def step(ci, state):
            carry, dba, dbi, dlam = state
            r0 = pl.multiple_of(S - 16 - ci * 16, 16)
            a_ext = a_ref[pl.ds(r0, 24), :]
            a_next = pltpu.roll(a_ext, 23, 0)
            h_prev = pltpu.roll(hp_ref[pl.ds(r0, 24), :], 1, 0)
            dh16 = dh_ref[pl.ds(r0, 16), :]
            gs = [None, None]
            for half in (1, 0):
                lo = 8 * half
                c_cum, g0 = _scan_up(a_next[lo:lo + 8, :], dh16[lo:lo + 8, :], row)
                g = c_cum * carry + g0
                carry = jnp.broadcast_to(g[0:1, :], (8, tc))
                gs[half] = g
            g16 = jnp.concatenate(gs, axis=0)
            a16 = a_ext[0:16, :]
            r16 = r_ref[pl.ds(r0, 16), :]
            i16 = i_ref[pl.ds(r0, 16), :]
            x16 = x_ref[pl.ds(r0, 16), :]
            a2 = a16 * a16
            sq = jnp.sqrt(_one_minus_exp(-2.0 * rate * r16))
            dx_ref[pl.ds(r0, 16), :] = g16 * sq * i16
            dzi = g16 * sq * x16 * i16 * (1.0 - i16)
            dlog_a = g16 * h_prev[8:24, :] * a16 - g16 * i16 * x16 * a2 / sq
            dzr = -rate * dlog_a * r16 * (1.0 - r16)
            dzr_ref[pl.ds(r0, 16), :] = dzr.astype(dzr_ref.dtype)
            dzi_ref[pl.ds(r0, 16), :] = dzi.astype(dzi_ref.dtype)
            return carry, dba + _colsum(dzr), dbi + _colsum(dzi), dlam + _colsum(dlog_a * r16)

        zero = jnp.zeros((1, tc), F32)
        _, dba, dbi, dlam = lax.fori_loop(0, S // 16, step, (jnp.zeros((8, tc), F32), zero, zero, zero))
        dba_ref[...] = dba
        dbi_ref[...] = dbi
        dlam_ref[...] = dlam * dlam_scale

    col = pl.BlockSpec((S, tc), lambda j: (0, j))
    vec = pl.BlockSpec((1, tc), lambda j: (0, j))
    return _call(
        body,
        name="lru_scan_bwd",
        grid=(D_RNN // tc,),
        in_specs=[col, pl.BlockSpec((S, tc), lambda j: (0, OFF_RY // tc + j)), col, col, col, col, vec],
        out_specs=[col, col, col, col, vec, vec, vec],
        out_shape=[jax.ShapeDtypeStruct((S, D_RNN), MXU_DTYPE)] * 3 + [jax.ShapeDtypeStruct((S, D_RNN), F32)]
        + [jax.ShapeDtypeStruct((1, D_RNN), F32)] * 3,
        scratch_shapes=[pltpu.VMEM((S + PAD, tc), F32), pltpu.VMEM((S, tc), F32), pltpu.VMEM((S + PAD, tc), F32)],
        semantics=("parallel",),
        operands=(dy, proj, h, r, i, rxc, lam),
        side=side,
        window=window,
    )


def _lru_gate_wgrad(rxc, dzr, dzi, side=None):
    def body(x_ref, dzr_ref, dzi_ref, dwa_ref, dwi_ref):
        xv = x_ref[...].astype(MXU_DTYPE)
        dims = (((0,), (0,)), ((), ()))
        dwa_ref[...] = lax.dot_general(xv, dzr_ref[...], dims, preferred_element_type=F32)
        dwi_ref[...] = lax.dot_general(xv, dzi_ref[...], dims, preferred_element_type=F32)

    col = pl.BlockSpec((S, RNN_GROUP), lambda g: (0, g))
    w_spec = pl.BlockSpec((None, RNN_GROUP, RNN_GROUP), lambda g: (g, 0, 0))
    return _call(
        body,
        name="lru_gate_wgrad",
        grid=(N_RNN_GROUPS,),
        in_specs=[col, col, col],
        out_specs=[w_spec, w_spec],
        out_shape=[jax.ShapeDtypeStruct((N_RNN_GROUPS, RNN_GROUP, RNN_GROUP), F32)] * 2,
        semantics=("parallel",),
        operands=(rxc, dzr, dzi),
        side=side,
    )


def _lru_gate_xgrad(dzr, dzi, wa, wi, dx_in, side=None):
    tm = 512

    def body(dzr_ref, dzi_ref, wa_ref, wi_ref, dx_ref, o_ref):
        dims = (((1,), (1,)), ((), ()))
        o_ref[...] = (dx_ref[...]
                      + lax.dot_general(dzr_ref[...], wa_ref[...].astype(MXU_DTYPE), dims, preferred_element_type=F32)
                      + lax.dot_general(dzi_ref[...], wi_ref[...].astype(MXU_DTYPE), dims, preferred_element_type=F32))

    x_spec = pl.BlockSpec((tm, RNN_GROUP), lambda g, i: (i, g))
    w_spec = pl.BlockSpec((None, RNN_GROUP, RNN_GROUP), lambda g, i: (g, 0, 0))
    return _call(
        body,
        name="lru_gate_xgrad",
        grid=(N_RNN_GROUPS, S // tm),
        in_specs=[x_spec, x_spec, w_spec, w_spec, x_spec],
        out_specs=x_spec,
        out_shape=jax.ShapeDtypeStruct((S, D_RNN), F32),
        semantics=("parallel", "parallel"),
        operands=(dzr, dzi, wa, wi, dx_in),
        side=side,
    )


def _gate_fwd(y_attn, y_rnn, proj, b_gate, side=None):
    t = 512

    def body(ya_ref, yr_ref, ga_ref, gr_ref, ba_ref, br_ref, o_ref):
        o_ref[...] = (_sigmoid(ga_ref[...] + ba_ref[...]) * ya_ref[...]
                      + _sigmoid(gr_ref[...] + br_ref[...]) * yr_ref[...]).astype(o_ref.dtype)

    tile = pl.BlockSpec((t, t), lambda i, j: (i, j))
    return _call(
        body,
        name="gate_fwd",
        grid=(S // t, D // t),
        in_specs=[tile, tile,
                  pl.BlockSpec((t, t), lambda i, j: (i, OFF_GA // t + j)),
                  pl.BlockSpec((t, t), lambda i, j: (i, OFF_GR // t + j)),
                  pl.BlockSpec((1, t), lambda i, j: (0, j)),
                  pl.BlockSpec((1, t), lambda i, j: (0, D // t + j))],
        out_specs=tile,
        out_shape=jax.ShapeDtypeStruct((S, D), MXU_DTYPE),
        semantics=("parallel", "parallel"),
        operands=(y_attn, y_rnn, proj, proj, b_gate, b_gate),
        side=side,
    )


def _gate_bwd(dmix, y_attn, y_rnn, proj, b_gate, side=None, window=None):
    t = 512

    def body(dm_ref, ya_ref, yr_ref, ga_ref, gr_ref, ba_ref, br_ref,
             dga_ref, dya_ref, dyr_ref, dgr_ref, dba_ref, dbr_ref):
        @pl.when(pl.program_id(1) == 0)
        def _():
            dba_ref[...] = jnp.zeros_like(dba_ref)
            dbr_ref[...] = jnp.zeros_like(dbr_ref)

        dm = dm_ref[...]
        ga = _sigmoid(ga_ref[...] + ba_ref[...])
        gr = _sigmoid(gr_ref[...] + br_ref[...])
        dya_ref[...] = (dm * ga).astype(dya_ref.dtype)
        dyr_ref[...] = (dm * gr).astype(dyr_ref.dtype)
        dga = dm * ya_ref[...] * ga * (1.0 - ga)
        dgr = dm * yr_ref[...] * gr * (1.0 - gr)
        dga_ref[...] = dga.astype(dga_ref.dtype)
        dgr_ref[...] = dgr.astype(dgr_ref.dtype)
        dba_ref[...] += _colsum(dga)
        dbr_ref[...] += _colsum(dgr)

    tile = pl.BlockSpec((t, t), lambda j, i: (i, j))
    vec = pl.BlockSpec((1, t), lambda j, i: (0, j))
    return _call(
        body,
        name="gate_bwd",
        grid=(D // t, S // t),
        in_specs=[tile, tile, tile,
                  pl.BlockSpec((t, t), lambda j, i: (i, OFF_GA // t + j)),
                  pl.BlockSpec((t, t), lambda j, i: (i, OFF_GR // t + j)),
                  vec,
                  pl.BlockSpec((1, t), lambda j, i: (0, D // t + j))],
        out_specs=[tile, tile, tile, tile, vec, vec],
        out_shape=[jax.ShapeDtypeStruct((S, D), MXU_DTYPE)] * 4 + [jax.ShapeDtypeStruct((1, D), F32)] * 2,
        semantics=("parallel", "arbitrary"),
        operands=(dmix, y_attn, y_rnn, proj, proj, b_gate, b_gate),
        side=side,
        window=window,
    )


LN_TM = 256


def _ln_stats(pre):
    mu = jnp.mean(pre, axis=-1, keepdims=True)
    xc = pre - mu
    rstd = lax.rsqrt(jnp.mean(xc * xc, axis=-1, keepdims=True) + LN_EPS)
    return xc * rstd, rstd


def _ln_input_grad(dy, xhat, rstd, g):
    dyg = dy * g
    return rstd * (dyg - jnp.mean(dyg, axis=-1, keepdims=True)
                   - xhat * jnp.mean(dyg * xhat, axis=-1, keepdims=True))


def _ln_fwd(res, branch, g, b, side=None):
    def body(res_ref, br_ref, g_ref, b_ref, y_ref, yb_ref, xhat_ref, rstd_ref):
        xhat, rstd = _ln_stats(ALPHA * res_ref[...] + br_ref[...])
        y = xhat * g_ref[...] + b_ref[...]
        y_ref[...] = y
        yb_ref[...] = y.astype(yb_ref.dtype)
        xhat_ref[...] = xhat
        rstd_ref[...] = rstd

    tile = pl.BlockSpec((LN_TM, D), lambda i: (i, 0))
    vec = pl.BlockSpec((1, D), lambda i: (0, 0))
    return _call(
        body,
        name="ln_fwd",
        grid=(S // LN_TM,),
        in_specs=[tile, tile, vec, vec],
        out_specs=[tile, tile, tile, pl.BlockSpec((LN_TM, 1), lambda i: (i, 0))],
        out_shape=[jax.ShapeDtypeStruct((S, D), F32), jax.ShapeDtypeStruct((S, D), MXU_DTYPE),
                   jax.ShapeDtypeStruct((S, D), F32), jax.ShapeDtypeStruct((S, 1), F32)],
        semantics=("parallel",),
        operands=(res, branch, g, b),
        side=side,
    )


def _ln_bwd(dy_a, dy_b, xhat, rstd, g, side=None):
    def body(da_ref, db_in_ref, xhat_ref, rstd_ref, g_ref, dp_ref, dpb_ref, dg_ref, db_ref):
        @pl.when(pl.program_id(0) == 0)
        def _():
            dg_ref[...] = jnp.zeros_like(dg_ref)
            db_ref[...] = jnp.zeros_like(db_ref)

        dy = da_ref[...] + ALPHA * db_in_ref[...]
        xhat = xhat_ref[...]
        dp = _ln_input_grad(dy, xhat, rstd_ref[...], g_ref[...])
        dp_ref[...] = dp
        dpb_ref[...] = dp.astype(dpb_ref.dtype)
        dg_ref[...] += _colsum(dy * xhat)
        db_ref[...] += _colsum(dy)

    tile = pl.BlockSpec((LN_TM, D), lambda i: (i, 0))
    vec = pl.BlockSpec((1, D), lambda i: (0, 0))
    return _call(
        body,
        name="ln_bwd",
        grid=(S // LN_TM,),
        in_specs=[tile, tile, tile, pl.BlockSpec((LN_TM, 1), lambda i: (i, 0)), vec],
        out_specs=[tile, tile, vec, vec],
        out_shape=[jax.ShapeDtypeStruct((S, D), F32), jax.ShapeDtypeStruct((S, D), MXU_DTYPE),
                   jax.ShapeDtypeStruct((1, D), F32), jax.ShapeDtypeStruct((1, D), F32)],
        semantics=("arbitrary",),
        operands=(dy_a, dy_b, xhat, rstd, g),
        side=side,
    )


def _ln_loss_bwd(res, branch, g, b, target, side=None):
    def body(res_ref, br_ref, g_ref, b_ref, t_ref, loss_ref, dp_ref, dpb_ref, dg_ref, db_ref):
        @pl.when(pl.program_id(0) == 0)
        def _():
            loss_ref[...] = jnp.zeros_like(loss_ref)
            dg_ref[...] = jnp.zeros_like(dg_ref)
            db_ref[...] = jnp.zeros_like(db_ref)

        xhat, rstd = _ln_stats(ALPHA * res_ref[...] + br_ref[...])
        gv = g_ref[...]
        err = xhat * gv + b_ref[...] - t_ref[...]
        loss_ref[...] += (0.5 / D) * jnp.sum(_colsum(err * err), axis=1, keepdims=True)
        dy = err * (1.0 / D)
        dp = _ln_input_grad(dy, xhat, rstd, gv)
        dp_ref[...] = dp
        dpb_ref[...] = dp.astype(dpb_ref.dtype)
        dg_ref[...] += _colsum(dy * xhat)
        db_ref[...] += _colsum(dy)

    tile = pl.BlockSpec((LN_TM, D), lambda i: (i, 0))
    vec = pl.BlockSpec((1, D), lambda i: (0, 0))
    return _call(
        body,
        name="ln_loss_bwd",
        grid=(S // LN_TM,),
        in_specs=[tile, tile, vec, vec, tile],
        out_specs=[pl.BlockSpec((1, 1), lambda i: (0, 0)), tile, tile, vec, vec],
        out_shape=[jax.ShapeDtypeStruct((1, 1), F32), jax.ShapeDtypeStruct((S, D), F32),
                   jax.ShapeDtypeStruct((S, D), MXU_DTYPE),
                   jax.ShapeDtypeStruct((1, D), F32), jax.ShapeDtypeStruct((1, D), F32)],
        semantics=("arbitrary",),
        operands=(res, branch, g, b, target),
        side=side,
    )


FFN_TC = 256


FFN_GATE_TN = 384


def _ffn_gate_act(x, w_gate, up, w, b, side=None):
    tn = FFN_GATE_TN
    b_block = w_gate.shape[2]
    assert b_block % tn == 0

    def body(x_ref, wg_ref, up_ref, w_ref, b_ref, g_ref, o_ref, xpad_ref):
        g_ref[...] = lax.dot_general(x_ref[...].astype(MXU_DTYPE), wg_ref[...].astype(MXU_DTYPE),
                                     (((1,), (0,)), ((), ())), preferred_element_type=F32)
        xpad_ref[pl.ds(0, PAD), :] = jnp.zeros((PAD, tn), F32)
        xpad_ref[pl.ds(PAD, S), :] = g_ref[...]
        wv = w_ref[...]
        bv = b_ref[...]

        def step(ci, carry):
            r0 = pl.multiple_of(ci * CHUNK, CHUNK)
            taps = _past_taps(xpad_ref, r0, FFN_CONV_W)
            gate = bv + taps[0] * wv[0:1, :] + taps[1] * wv[1:2, :] + taps[2] * wv[2:3, :]
            o_ref[pl.ds(r0, CHUNK), :] = (_gelu(gate)[0] * up_ref[pl.ds(r0, CHUNK), :]).astype(o_ref.dtype)
            return carry

        lax.fori_loop(0, S // CHUNK, step, 0)

    col = pl.BlockSpec((S, tn), lambda j: (0, j))
    return _call(
        body,
        name="ffn_gate_act",
        grid=(D_FF // tn,),
        in_specs=[pl.BlockSpec((S, D), lambda j: (0, 0)),
                  pl.BlockSpec((None, D, tn), lambda j: ((j * tn) // b_block, 0, ((j * tn) % b_block) // tn)),
                  col, pl.BlockSpec((FFN_CONV_W, tn), lambda j: (0, j)), pl.BlockSpec((1, tn), lambda j: (0, j))],
        out_specs=[col, col],
        out_shape=[jax.ShapeDtypeStruct((S, D_FF), F32), jax.ShapeDtypeStruct((S, D_FF), MXU_DTYPE)],
        scratch_shapes=[pltpu.VMEM((S + PAD, tn), F32)],
        semantics=("parallel",),
        operands=(x, w_gate, up, w, b),
        side=side,
    )


def _ffn_act_bwd(dfin, up, gpre, w, b, side=None):
    tc = FFN_TC
    width = FFN_CONV_W

    def body(df_ref, up_ref, x_ref, w_ref, b_ref, dup_ref, dx_ref, dw_ref, db_ref, xpad_ref, dpad_ref):
        xpad_ref[pl.ds(0, PAD), :] = jnp.zeros((PAD, tc), F32)
        xpad_ref[pl.ds(PAD, S), :] = x_ref[...]
        dpad_ref[pl.ds(S, PAD), :] = jnp.zeros((PAD, tc), F32)
        wv = w_ref[...]
        bv = b_ref[...]

        def gate_grad(ci, acc):
            r0 = pl.multiple_of(ci * CHUNK, CHUNK)
            taps = _past_taps(xpad_ref, r0, width)
            gate = bv + taps[0] * wv[0:1, :] + taps[1] * wv[1:2, :] + taps[2] * wv[2:3, :]
            ge, dge = _gelu(gate)
            df = df_ref[pl.ds(r0, CHUNK), :]
            dup_ref[pl.ds(r0, CHUNK), :] = (df * ge).astype(dup_ref.dtype)
            d = df * up_ref[pl.ds(r0, CHUNK), :] * dge
            dpad_ref[pl.ds(r0, CHUNK), :] = d
            return tuple(acc[k] + _colsum(taps[k] * d) for k in range(width)) + (acc[width] + _colsum(d),)

        zero = jnp.zeros((1, tc), F32)
        acc = lax.fori_loop(0, S // CHUNK, gate_grad, (zero,) * (width + 1))
        for k in range(width):
            dw_ref[k:k + 1, :] = acc[k]
        db_ref[...] = acc[width]

        def input_grad(ci, carry):
            r0 = pl.multiple_of(ci * CHUNK, CHUNK)
            ahead = _future_taps(dpad_ref, r0, width)
            dx = ahead[0] * wv[2:3, :] + ahead[1] * wv[1:2, :] + ahead[2] * wv[0:1, :]
            dx_ref[pl.ds(r0, CHUNK), :] = dx.astype(dx_ref.dtype)
            return carry

        lax.fori_loop(0, S // CHUNK, input_grad, 0)

    col = pl.BlockSpec((S, tc), lambda j: (0, j))
    w_spec = pl.BlockSpec((width, tc), lambda j: (0, j))
    vec = pl.BlockSpec((1, tc), lambda j: (0, j))
    return _call(
        body,
        name="ffn_act_bwd",
        grid=(D_FF // tc,),
        in_specs=[col, col, col, w_spec, vec],
        out_specs=[col, col, w_spec, vec],
        out_shape=[jax.ShapeDtypeStruct((S, D_FF), MXU_DTYPE)] * 2
        + [jax.ShapeDtypeStruct((width, D_FF), F32), jax.ShapeDtypeStruct((1, D_FF), F32)],
        scratch_shapes=[pltpu.VMEM((S + PAD, tc), F32), pltpu.VMEM((S + PAD, tc), F32)],
        semantics=("parallel",),
        operands=(dfin, up, gpre, w, b),
        side=side,
    )


def _adamw_update(w, g, m, v):
    m = ADAM_B1 * m + (1.0 - ADAM_B1) * g
    v = ADAM_B2 * v + (1.0 - ADAM_B2) * (g * g)
    m_hat = m / (1.0 - ADAM_B1 ** ADAM_STEP)
    v_hat = v / (1.0 - ADAM_B2 ** ADAM_STEP)
    delta = -ADAM_LR * (m_hat / (jnp.sqrt(v_hat) + ADAM_EPS) + ADAM_WD * w)
    return delta, m, v


def _add_pairs(send, pair, far_index, *, name):
    _, r_dim, c_dim = send.shape
    tr = r_dim // 4

    def body(far_ref, mine_ref, theirs_ref, o_ref):
        o_ref[...] = (mine_ref[...].astype(F32) + theirs_ref[...].astype(F32)).astype(o_ref.dtype)

    return pl.pallas_call(
        body,
        name=name,
        grid_spec=pltpu.PrefetchScalarGridSpec(
            num_scalar_prefetch=1,
            grid=(3, r_dim // tr),
            in_specs=[pl.BlockSpec((None, tr, c_dim), lambda j, i, far: (far[j], i, 0)),
                      pl.BlockSpec((None, tr, c_dim), lambda j, i, far: (1 + j, i, 0))],
            out_specs=pl.BlockSpec((None, tr, c_dim), lambda j, i, far: (j, i, 0)),
        ),
        out_shape=jax.ShapeDtypeStruct((3, r_dim, c_dim), BF16),
        compiler_params=_cparams("parallel", "parallel"),
    )(far_index, send, pair)


def _reduce_adamw(w, m, v, g_own, pair, far, me, *, tr, name, part=0, earlier=None):
    _, r_dim, c_dim = w.shape
    cp = pair.shape[2]

    def body(me_ref, w_ref, m_ref, v_ref, g_ref, pair_ref, far_ref, *refs):
        grad_ref, delta_ref, nm_ref, nv_ref = refs[-4:]
        g = g_ref[...] + pair_ref[...].astype(F32)
        for j in range(3):
            g = g + far_ref[j].astype(F32)
        delta, nm, nv = _adamw_update(w_ref[...], g, m_ref[...], v_ref[...])
        grad_ref[...] = g
        delta_ref[...] = delta
        nm_ref[...] = nm
        nv_ref[...] = nv

    tile = pl.BlockSpec((None, tr, cp), lambda i, me: (0, i, part))
    if g_own.ndim == 3:
        own_spec = pl.BlockSpec((None, tr, cp), lambda i, me: (me[0], i, 0))
    else:
        own_spec = pl.BlockSpec((tr, cp), lambda i, me: (i, 0))
    earlier = list(earlier or ())
    return pl.pallas_call(
        body,
        name=name,
        grid_spec=pltpu.PrefetchScalarGridSpec(
            num_scalar_prefetch=1,
            grid=(r_dim // tr,),
            in_specs=[tile, tile, tile, own_spec, pl.BlockSpec((None, tr, cp), lambda i, me: (0, i, 0)),
                      pl.BlockSpec((3, tr, cp), lambda i, me: (0, i, 0))]
            + [pl.BlockSpec(memory_space=pl.ANY)] * len(earlier),
            out_specs=[tile] * 4,
        ),
        out_shape=[jax.ShapeDtypeStruct((1, r_dim, c_dim), F32)] * 4,
        input_output_aliases={7 + k: k for k in range(len(earlier))},
        compiler_params=_cparams("parallel"),
    )(me, w, m, v, g_own, pair, far, *earlier)


def _adamw_many(ws, ms, vs, gs):
    n = len(ws)

    def body(*refs):
        for i in range(n):
            delta, nm, nv = _adamw_update(refs[i][...], refs[3 * n + i][...], refs[n + i][...], refs[2 * n + i][...])
            refs[4 * n + i][...] = delta
            refs[5 * n + i][...] = nm
            refs[6 * n + i][...] = nv

    vmem = pl.BlockSpec(memory_space=pltpu.VMEM)
    res = pl.pallas_call(
        body,
        name="adamw_small",
        in_specs=[vmem] * (4 * n),
        out_specs=[vmem] * (3 * n),
        out_shape=[jax.ShapeDtypeStruct(w.shape, F32) for w in ws] * 3,
        compiler_params=pltpu.CompilerParams(vmem_limit_bytes=VMEM_LIMIT),
    )(*ws, *ms, *vs, *gs)
    return res[:n], res[n:2 * n], res[2 * n:]


def _adamw_blocks(w, m, v, g, *, name, side=None):
    per = 2

    def body(w_ref, m_ref, v_ref, g_ref, delta_ref, nm_ref, nv_ref):
        delta, nm, nv = _adamw_update(w_ref[...], g_ref[...], m_ref[...], v_ref[...])
        delta_ref[...] = delta
        nm_ref[...] = nm
        nv_ref[...] = nv

    tile = pl.BlockSpec((1, per) + w.shape[2:], lambda i: (0, i, 0, 0))
    return _call(
        body,
        name=name,
        grid=(w.shape[1] // per,),
        in_specs=[tile] * 4,
        out_specs=[tile] * 3,
        out_shape=[jax.ShapeDtypeStruct(w.shape, F32)] * 3,
        semantics=("parallel",),
        operands=(w, m, v, g),
        side=side,
    )


def _coords():
    return lax.axis_index("x"), lax.axis_index("y"), lax.axis_index("c")


def _flip(coord, bit):
    return 1 - coord if bit else coord


def _relative(k):
    x, y, c = _coords()
    return _flip(x, k & 4), _flip(y, k & 2), _flip(c, k & 1)


def _index(pos):
    return 4 * pos[0] + 2 * pos[1] + pos[2]


FAR = (4, 2, 6)
AG_US_PER_MB = 38.0
RS_US_PER_MB = 46.0
MIN_RIDE_US = 30.0
PAIR_EXCHANGE_US = 20.0
MIN_GATHER_RIDE_US = 22.0
ROW_ALIGN = 32


def _chunks(items, cursor, us, us_per_mb, through=None):
    budget = float("inf") if us is None else us / us_per_mb * 2 ** 20
    names = list(items)
    if through is not None:
        names = names[:names.index(through) + 1]
    chunks = []
    for name in names:
        arr = items[name]
        r_dim, c_dim = arr.shape[-2:]
        row_bytes = c_dim * arr.dtype.itemsize
        while cursor[name] < r_dim and budget > 0:
            rows = r_dim - cursor[name]
            if r_dim > ROW_ALIGN and budget < rows * row_bytes:
                rows = min(rows, max(ROW_ALIGN, int(budget // row_bytes) // ROW_ALIGN * ROW_ALIGN))
            chunks.append((name, cursor[name], rows))
            cursor[name] += rows
            budget -= rows * row_bytes
    return chunks


class _Gather:
    def __init__(self, shards):
        self.shards, self.bufs, self.cursor = {}, {}, {}
        self.add_shards(shards)

    def add_shards(self, shards):
        for n, shard in shards.items():
            self.shards[n], self.bufs[n], self.cursor[n] = shard, None, 0

    def take(self, us=None, through=None):
        if us is not None and us < MIN_GATHER_RIDE_US:
            return None
        chunks = _chunks(self.shards, self.cursor, us, AG_US_PER_MB, through)
        return _GatherSide(self, chunks) if chunks else None

    def get(self, name):
        chunks = _chunks(self.shards, self.cursor, None, AG_US_PER_MB, through=name)
        if chunks:
            _run_side(_GatherSide(self, chunks), "gather_" + name)
        return self.bufs[name]


class _GatherSide:
    SEMS = 8

    def __init__(self, owner, chunks):
        self.owner, self.chunks = owner, chunks
        self.names = list(dict.fromkeys(n for n, _, _ in chunks))
        old = [n for n in self.names if owner.bufs[n] is not None]
        self.operands = [owner.shards[n] for n in self.names] + [owner.bufs[n] for n in old]
        self.out_shape = [jax.ShapeDtypeStruct((N_DEV,) + owner.shards[n].shape, owner.shards[n].dtype)
                          for n in self.names]
        self.aliases = {len(self.names) + i: self.names.index(n) for i, n in enumerate(old)}
        self.sems = [pltpu.SemaphoreType.DMA((self.SEMS * len(chunks),)),
                     pltpu.SemaphoreType.DMA((self.SEMS * len(chunks),)), pltpu.SemaphoreType.DMA((len(chunks),))]

    def _halves(self, ci):
        _, r0, rows = self.chunks[ci]
        if rows % ROW_ALIGN:
            return None
        return (r0, rows // 2), (r0 + rows // 2, rows // 2)

    def _copy(self, ins, outs, sems, ci, s, block, to, rows=None, from_shard=False):
        name, r0, n = self.chunks[ci]
        if rows is not None:
            r0, n = rows
        w = self.names.index(name)
        slot = outs[w].at[_index(block), pl.ds(r0, n)]
        return pltpu.make_async_remote_copy(
            src_ref=ins[w].at[pl.ds(r0, n)] if from_shard else slot, dst_ref=slot,
            send_sem=sems[0].at[self.SEMS * ci + s], recv_sem=sems[1].at[self.SEMS * ci + s],
            device_id=to, device_id_type=MESH)

    def _own(self, ins, outs, sems, ci):
        name, r0, rows = self.chunks[ci]
        w = self.names.index(name)
        return pltpu.make_async_copy(ins[w].at[pl.ds(r0, rows)], outs[w].at[_index(_relative(0)), pl.ds(r0, rows)],
                                     sems[2].at[ci])

    def _pass(self, ins, outs, sems, ci, which):
        source, target = ((4, 2), (2, 4))[which]
        return self._copy(ins, outs, sems, ci, 3 + which, _relative(source), _relative(target),
                          rows=self._halves(ci)[which])

    def start(self, ins, outs, sems):
        me = _relative(0)
        for ci in range(len(self.chunks)):
            self._own(ins, outs, sems, ci).start()
        for ci in range(len(self.chunks)):
            self._copy(ins, outs, sems, ci, 1, me, _relative(4), from_shard=True).start()
            self._copy(ins, outs, sems, ci, 2, me, _relative(2), from_shard=True).start()
            if self._halves(ci) is None:
                self._copy(ins, outs, sems, ci, 3, me, _relative(6), from_shard=True).start()
        for ci in range(len(self.chunks)):
            self._copy(ins, outs, sems, ci, 0, me, _relative(1), from_shard=True).start()

    def mid(self, ins, outs, sems):
        me = _relative(0)
        cut = [ci for ci in range(len(self.chunks)) if self._halves(ci) is not None]
        for ci in cut:
            self._copy(ins, outs, sems, ci, 1, _relative(4), me).wait_recv()
            self._pass(ins, outs, sems, ci, 0).start()
            self._copy(ins, outs, sems, ci, 5, _relative(4), _relative(1)).start()
        for ci in cut:
            self._copy(ins, outs, sems, ci, 2, _relative(2), me).wait_recv()
            self._pass(ins, outs, sems, ci, 1).start()
            self._copy(ins, outs, sems, ci, 6, _relative(2), _relative(1)).start()

    def finish(self, ins, outs, sems):
        me, sibling = _relative(0), _relative(1)
        n = len(self.chunks)
        for ci in range(n):
            if self._halves(ci) is None:
                for s, k in ((1, 4), (2, 2), (3, 6)):
                    self._copy(ins, outs, sems, ci, s, _relative(k), me).wait_recv()
                for j, k in enumerate(FAR):
                    self._copy(ins, outs, sems, ci, 5 + j, _relative(k), sibling).start()
            else:
                h0, h1 = self._halves(ci)
                self._copy(ins, outs, sems, ci, 3, _relative(6), me, rows=h0).wait_recv()
                self._copy(ins, outs, sems, ci, 4, _relative(6), me, rows=h1).wait_recv()
                self._copy(ins, outs, sems, ci, 7, _relative(6), sibling).start()
        for ci in range(n):
            self._copy(ins, outs, sems, ci, 0, sibling, me).wait_recv()
            for j, k in enumerate(FAR):
                self._copy(ins, outs, sems, ci, 5 + j, _relative(k | 1), me).wait_recv()
        for ci in range(n):
            self._copy(ins, outs, sems, ci, 0, me, sibling, from_shard=True).wait_send()
            self._copy(ins, outs, sems, ci, 1, me, _relative(4), from_shard=True).wait_send()
            self._copy(ins, outs, sems, ci, 2, me, _relative(2), from_shard=True).wait_send()
            if self._halves(ci) is None:
                self._copy(ins, outs, sems, ci, 3, me, _relative(6), from_shard=True).wait_send()
            else:
                self._pass(ins, outs, sems, ci, 0).wait_send()
                self._pass(ins, outs, sems, ci, 1).wait_send()
            for j, k in enumerate(FAR):
                self._copy(ins, outs, sems, ci, 5 + j, _relative(k), sibling).wait_send()
            self._own(ins, outs, sems, ci).wait()

    def done(self, results):
        for n, buf in zip(self.names, results):
            self.owner.bufs[n] = buf


class _Scatter:
    def __init__(self, me, far_index):
        self.me, self.far_index = me, far_index
        self.sends, self.owns, self.pairs, self.sums, self.fars = {}, {}, {}, {}, {}
        self.pair_cursor, self.far_cursor = {}, {}

    def add(self, name, send, own):
        self.sends[name] = send
        self.owns[name] = own
        self.pairs[name] = self.fars[name] = None
        self.pair_cursor[name] = 0

    def _rows(self, name):
        return self.sends[name].shape[1]

    def _add_ready_pairs(self):
        for name in self.sends:
            if name not in self.sums and self.pair_cursor[name] == self._rows(name):
                self.sums[name] = _add_pairs(self.sends[name], self.pairs[name], self.far_index, name="pair_" + name)
                self.far_cursor[name] = 0

    def _side(self, us, through=None):
        self._add_ready_pairs()
        names = list(self.sends)
        if through is not None:
            names = names[:names.index(through) + 1]
        pair_chunks = [(n, self.pair_cursor[n], self._rows(n) - self.pair_cursor[n]) for n in names
                       if self.pair_cursor[n] < self._rows(n)]
        for n, _, _ in pair_chunks:
            self.pair_cursor[n] = self._rows(n)
        far_chunks = _chunks(self.sums, self.far_cursor, us, RS_US_PER_MB,
                             through if through in self.sums else None) if self.sums else []
        return _ScatterSide(self, pair_chunks, far_chunks) if pair_chunks or far_chunks else None

    def add_blocks(self, name, blocks32, blocks16):
        self.add(name, blocks16, blocks32)

    def take(self, us):
        return self._side(us) if us >= MIN_RIDE_US else None

    def flush_pairs(self, name):
        side = self._side(PAIR_EXCHANGE_US)
        if side is not None:
            _run_side(side, name)
        self._add_ready_pairs()

    def get(self, name):
        step = 0
        while name not in self.sums or self.far_cursor[name] < self._rows(name):
            _run_side(self._side(None, through=name), "scatter_%s_%d" % (name, step))
            step += 1
        return self.owns[name], self.pairs[name], self.fars[name]


class _ScatterSide:
    TO_SIBLING = (1, 5, 3, 7)

    def __init__(self, owner, pair_chunks, far_chunks):
        self.owner, self.pair_chunks, self.far_chunks = owner, pair_chunks, far_chunks
        self.pair_names = list(dict.fromkeys(n for n, _, _ in pair_chunks))
        self.far_names = list(dict.fromkeys(n for n, _, _ in far_chunks))
        ins = [(owner.sends[n], owner.pairs[n], (4,)) for n in self.pair_names]
        ins += [(owner.sums[n], owner.fars[n], (3,)) for n in self.far_names]
        old = [i for i, (_, buf, _) in enumerate(ins) if buf is not None]
        self.operands = [src for src, _, _ in ins] + [ins[i][1] for i in old]
        self.out_shape = [jax.ShapeDtypeStruct(slots + src.shape[1:], BF16) for src, _, slots in ins]
        self.aliases = {len(ins) + j: i for j, i in enumerate(old)}
        n_pair, n_far = 4 * len(pair_chunks), 3 * len(far_chunks)
        self.sems = [pltpu.SemaphoreType.DMA((max(n_pair, 1),)), pltpu.SemaphoreType.DMA((max(n_pair, 1),)),
                     pltpu.SemaphoreType.DMA((max(n_far, 1),)), pltpu.SemaphoreType.DMA((max(n_far, 1),))]

    def _copies(self, ins, outs, sems):
        copies = []
        for ci, (name, r0, rows) in enumerate(self.pair_chunks):
            w = self.pair_names.index(name)
            for j, k in enumerate(self.TO_SIBLING):
                copies.append(pltpu.make_async_remote_copy(
                    src_ref=ins[w].at[_index(_relative(k)), pl.ds(r0, rows)], dst_ref=outs[w].at[j, pl.ds(r0, rows)],
                    send_sem=sems[0].at[4 * ci + j], recv_sem=sems[1].at[4 * ci + j],
                    device_id=_relative(1), device_id_type=MESH))
        for ci, (name, r0, rows) in enumerate(self.far_chunks):
            w = len(self.pair_names) + self.far_names.index(name)
            for j, k in enumerate(FAR):
                copies.append(pltpu.make_async_remote_copy(
                    src_ref=ins[w].at[j, pl.ds(r0, rows)], dst_ref=outs[w].at[j, pl.ds(r0, rows)],
                    send_sem=sems[2].at[3 * ci + j], recv_sem=sems[3].at[3 * ci + j],
                    device_id=_relative(k), device_id_type=MESH))
        return copies

    def start(self, ins, outs, sems):
        for cp in self._copies(ins, outs, sems):
            cp.start()

    def mid(self, ins, outs, sems):
        pass

    def finish(self, ins, outs, sems):
        for cp in self._copies(ins, outs, sems):
            cp.wait()

    def done(self, results):
        for n, buf in zip(self.pair_names, results):
            self.owner.pairs[n] = buf
        for n, buf in zip(self.far_names, results[len(self.pair_names):]):
            self.owner.fars[n] = buf


class _Joined:
    def __init__(self, sides):
        self.sides = sides
        self.operands, self.out_shape, self.sems, self.aliases, self.spans = [], [], [], {}, []
        for s in sides:
            i0, o0, s0 = len(self.operands), len(self.out_shape), len(self.sems)
            self.operands += list(s.operands)
            self.out_shape += list(s.out_shape)
            self.sems += list(s.sems)
            self.aliases.update({i0 + i: o0 + o for i, o in s.aliases.items()})
            self.spans.append((slice(i0, len(self.operands)), slice(o0, len(self.out_shape)),
                               slice(s0, len(self.sems))))

    def start(self, ins, outs, sems):
        for s, (i, o, m) in zip(self.sides, self.spans):
            s.start(ins[i], outs[o], sems[m])

    def mid(self, ins, outs, sems):
        for s, (i, o, m) in zip(self.sides, self.spans):
            s.mid(ins[i], outs[o], sems[m])

    def finish(self, ins, outs, sems):
        for s, (i, o, m) in zip(self.sides, self.spans):
            s.finish(ins[i], outs[o], sems[m])

    def done(self, results):
        for s, (_, o, _) in zip(self.sides, self.spans):
            s.done(results[o])


def _join(*sides):
    sides = [s for s in sides if s is not None]
    if len(sides) <= 1:
        return sides[0] if sides else None
    return _Joined(sides)


PART_W = 768


def _pack_rows(vecs):
    rows = -(-sum(v.shape[0] for v in vecs) // 8) * 8

    def body(*refs):
        out = refs[-1]
        out[...] = jnp.zeros_like(out)
        r0 = 0
        for v in refs[:-1]:
            k, n = v.shape
            for p in range(-(-n // PART_W)):
                w = min(PART_W, n - PART_W * p)
                out[p, r0:r0 + k, 0:w] = v[:, PART_W * p:PART_W * p + w]
            r0 += k

    vmem = pl.BlockSpec(memory_space=pltpu.VMEM)
    return pl.pallas_call(body, name="pack_small", in_specs=[vmem] * len(vecs), out_specs=vmem,
                          out_shape=jax.ShapeDtypeStruct((N_DEV, rows, PART_W), F32))(*vecs)


def _unpack_rows(packed, shapes):
    def body(packed_ref, *outs):
        r0 = 0
        for o in outs:
            k, n = o.shape
            for p in range(-(-n // PART_W)):
                w = min(PART_W, n - PART_W * p)
                o[:, PART_W * p:PART_W * p + w] = packed_ref[p, r0:r0 + k, 0:w]
            r0 += k

    vmem = pl.BlockSpec(memory_space=pltpu.VMEM)
    return pl.pallas_call(body, name="unpack_small", in_specs=[vmem], out_specs=[vmem] * len(shapes),
                          out_shape=[jax.ShapeDtypeStruct(s, F32) for s in shapes])(packed)


class _PartsToOwners:
    def __init__(self, arrays):
        self.n = len(arrays)
        self.pers = [a.shape[0] // N_DEV for a in arrays]
        self.operands, self.aliases = list(arrays), {}
        self.out_shape = [jax.ShapeDtypeStruct((N_DEV, per) + a.shape[1:], a.dtype) for a, per in zip(arrays, self.pers)]
        self.sems = [pltpu.SemaphoreType.DMA((self.n * (N_DEV - 1),))] * 2

    def _copies(self, ins, outs, sems):
        return [pltpu.make_async_remote_copy(
            src_ref=ins[j].at[pl.ds(self.pers[j] * _index(_relative(k)), self.pers[j])], dst_ref=outs[j].at[k],
            send_sem=sems[0].at[self.n * (k - 1) + j], recv_sem=sems[1].at[self.n * (k - 1) + j],
            device_id=_relative(k), device_id_type=MESH) for k in range(1, N_DEV) for j in range(self.n)]

    def start(self, ins, outs, sems):
        for cp in self._copies(ins, outs, sems):
            cp.start()

    def mid(self, ins, outs, sems):
        pass

    def finish(self, ins, outs, sems):
        for cp in self._copies(ins, outs, sems):
            cp.wait()

    def done(self, results):
        self.stages = list(results)


def _sum_parts(arrays, stages, *, name):
    n = len(arrays)
    pers = [a.shape[0] // N_DEV for a in arrays]

    def body(*refs):
        me = _index(_relative(0))
        for j in range(n):
            acc = refs[j][pl.ds(pers[j] * me, pers[j])]
            for k in range(1, N_DEV):
                acc = acc + refs[n + j][k].astype(F32)
            refs[2 * n + j][...] = acc

    vmem = pl.BlockSpec(memory_space=pltpu.VMEM)
    return pl.pallas_call(body, name=name, in_specs=[vmem] * (2 * n), out_specs=[vmem] * n,
                          out_shape=[jax.ShapeDtypeStruct((per,) + a.shape[1:], F32) for a, per in zip(arrays, pers)],
                          compiler_params=pltpu.CompilerParams(vmem_limit_bytes=VMEM_LIMIT))(*arrays, *stages)


class _PartsToAll:
    def __init__(self, parts):
        self.n = len(parts)
        self.pers = [p.shape[0] for p in parts]
        self.operands, self.aliases = list(parts), {}
        self.out_shape = [jax.ShapeDtypeStruct((N_DEV * p.shape[0],) + p.shape[1:], F32) for p in parts]
        self.sems = [pltpu.SemaphoreType.DMA((self.n * (N_DEV - 1),))] * 2 + [pltpu.SemaphoreType.DMA((self.n,))]

    def _rows(self, outs, j, pos):
        return outs[j].at[pl.ds(self.pers[j] * _index(pos), self.pers[j])]

    def _copy(self, ins, outs, sems, k, j, owner):
        return pltpu.make_async_remote_copy(
            src_ref=ins[j], dst_ref=self._rows(outs, j, owner),
            send_sem=sems[0].at[self.n * (k - 1) + j], recv_sem=sems[1].at[self.n * (k - 1) + j],
            device_id=_relative(k), device_id_type=MESH)

    def _own(self, ins, outs, sems, j):
        return pltpu.make_async_copy(ins[j], self._rows(outs, j, _relative(0)), sems[2].at[j])

    def start(self, ins, outs, sems):
        for j in range(self.n):
            self._own(ins, outs, sems, j).start()
            for k in range(1, N_DEV):
                self._copy(ins, outs, sems, k, j, _relative(0)).start()

    def mid(self, ins, outs, sems):
        pass

    def finish(self, ins, outs, sems):
        for j in range(self.n):
            for k in range(1, N_DEV):
                self._copy(ins, outs, sems, k, j, _relative(k)).wait_recv()
                self._copy(ins, outs, sems, k, j, _relative(0)).wait_send()
            self._own(ins, outs, sems, j).wait()

    def done(self, results):
        self.totals = list(results)


class _AllReduce:
    def __init__(self, name):
        self.name = name

    def begin(self, arrays, wire_dtype=F32):
        self.own = list(arrays)
        self.to_owners = _PartsToOwners([a.astype(wire_dtype) for a in self.own])
        return self.to_owners

    def middle(self):
        self.to_all = _PartsToAll(_sum_parts(self.own, self.to_owners.stages, name="sum_" + self.name))
        return self.to_all

    def end(self):
        return self.to_all.totals


class _SmallSync:
    def __init__(self, vec_names, mat_names):
        self.vec_names, self.mat_names = vec_names, mat_names
        self.mats, self.vecs = _AllReduce("small_mats"), _AllReduce("small_vecs")

    def begin_mats(self, grads):
        return self.mats.begin([_diag_blocks(grads[n]) for n in self.mat_names], BF16)

    def middle_mats(self):
        return self.mats.middle()

    def begin(self, loss, grads):
        vecs = [loss] + [grads[n] for n in self.vec_names]
        self.shapes = [v.shape for v in vecs]
        return self.vecs.begin([_pack_rows(vecs)])

    def middle(self):
        return self.vecs.middle()

    def end(self):
        packed, = self.vecs.end()
        sums = _unpack_rows(packed, self.shapes)
        return sums[0], dict(zip(self.vec_names, sums[1:])), dict(zip(self.mat_names, self.mats.end()))


def _block_diag(w):
    groups = []
    for g in range(N_RNN_GROUPS):
        placed = [jnp.pad(w[4 * g + b], ((RNN_BLOCK_W * b, RNN_BLOCK_W * (3 - b)),) * 2) for b in range(4)]
        groups.append(placed[0] + placed[1] + placed[2] + placed[3])
    return jnp.stack(groups)


def _diag_blocks(wg):
    blocks = []
    for n in range(4 * N_RNN_GROUPS):
        g, at = n // 4, RNN_BLOCK_W * (n % 4)
        blocks.append(wg[g, at:at + RNN_BLOCK_W, at:at + RNN_BLOCK_W])
    return jnp.stack(blocks)


def _heads_major(t, n_heads):
    return t.reshape(S, n_heads, HEAD_DIM).transpose(1, 0, 2)


def _heads_minor(t):
    return t.transpose(1, 0, 2).reshape(S, t.shape[0] * HEAD_DIM)


def _natural(gathered, how):
    n, r, c = gathered.shape
    if how == "rows":
        return gathered.reshape(n * r, c)
    return gathered.transpose(1, 0, 2).reshape(r, n * c)


def _blocks(full, how):
    if how == "rows":
        return full.reshape(N_DEV, full.shape[0] // N_DEV, full.shape[1])
    return full.reshape(full.shape[0], N_DEV, full.shape[1] // N_DEV).transpose(1, 0, 2)


def _cast_many(arrays, side=None):
    steps = 4

    def body(*refs):
        n = len(refs) // 2
        for src, dst in zip(refs[:n], refs[n:]):
            dst[...] = src[...].astype(dst.dtype)

    specs = [pl.BlockSpec((a.shape[0] // steps, a.shape[1]), lambda i: (i, 0)) for a in arrays]
    return _call(
        body,
        name="cast_weights",
        grid=(steps,),
        in_specs=specs,
        out_specs=specs,
        out_shape=[jax.ShapeDtypeStruct(a.shape, MXU_DTYPE) for a in arrays],
        semantics=("parallel",),
        operands=tuple(arrays),
        side=side,
    )


def _forward_backward(x2, xb, target, small, gather, scatter, sync):
    w_in_t = _natural(gather.get("w_in"), "rows")
    proj, projb = _mm(xb, w_in_t, tb=True, tm=S, tn=512, tk=D, out_dtype=(F32, MXU_DTYPE), name="proj",
                      side=gather.take(110))

    qt = projb[:, :OFF_K].T.reshape(N_KV, GROUP, HEAD_DIM, S)
    k2, v2 = projb[:, OFF_K:OFF_V], projb[:, OFF_V:OFF_RX]
    kp = jnp.pad(_heads_major(k2, N_KV), ((0, 0), (BLOCK, 0), (0, 0)))
    vp = jnp.pad(_heads_major(v2, N_KV), ((0, 0), (BLOCK, 0), (0, 0)))
    kt = jnp.pad(k2.T.reshape(N_KV, HEAD_DIM, S), ((0, 0), (0, 0), (BLOCK, 0)))
    vt = jnp.pad(v2.T.reshape(N_KV, HEAD_DIM, S), ((0, 0), (0, 0), (BLOCK, 0)))
    sink_row = jnp.repeat(small["attn_sinks"].reshape(N_KV, 1, GROUP), BLOCK, axis=2)
    ot = _attn_fwd(qt, kp, vt, sink_row, side=gather.take(36)).reshape(D, S)

    rconv_w = _natural(gather.get("rnn_conv_w"), "cols")
    rxc = _conv_fwd(proj, OFF_RX, rconv_w, small["rnn_conv_b"], tc=512, name="rnn_conv_fwd", side=gather.take(18))
    r, i = _lru_gates(rxc, small["lru_wa"], small["lru_wi"], small["lru_ba"], small["lru_bi"], side=gather.take(33))
    h, yrin = _lru_scan_fwd(r, i, rxc, proj, small["lru_lambda"], side=gather.take(53))

    w_ap = _natural(gather.get("w_attn_proj"), "rows")
    w_rp = _natural(gather.get("w_rnn_proj"), "rows")
    y_attn = _mm(ot, w_ap, ta=True, tm=1024, tn=1024, tk=D, name="attn_proj", side=gather.take(22))
    y_rnn = _mm(yrin, w_rp, tm=1024, tn=1024, tk=D_RNN, name="rnn_proj", side=gather.take(27))
    mixin = _gate_fwd(y_attn, y_rnn, proj, small["b_gate"], side=gather.take(25))
    w_out = _natural(gather.get("w_out"), "rows")
    mix = _mm(mixin, w_out, tm=1024, tn=1024, tk=D, name="mix_out", side=gather.take(22))
    x1, x1b, xhat1, rstd1 = _ln_fwd(x2, mix, small["ln1_g"], small["ln1_b"], side=gather.take(23))

    w_up = gather.get("ffn_w_up")
    up = _mm(x1b, w_up, tm=S, tn=768, tk=D, b_block=768, name="ffn_up", side=gather.take(58))
    w_gate = gather.get("ffn_w_gate")
    fconv_w = _natural(gather.get("ffn_conv_w"), "cols")
    gpre, fin = _ffn_gate_act(x1b, w_gate, up, fconv_w, small["ffn_conv_b"], side=gather.take(58))
    w_down = _natural(gather.get("ffn_w_down"), "rows")
    f = _mm(fin, w_down, tm=1024, tn=1024, tk=2048, name="ffn_down")
    loss, dpre2, dpre2b, d_ln2_g, d_ln2_b = _ln_loss_bwd(x1, f, small["ln2_g"], small["ln2_b"], target)

    grads = {"ln2_g": d_ln2_g, "ln2_b": d_ln2_b}
    both = (F32, BF16)
    g32, g16 = _mm(fin, dpre2b, ta=True, tm=1024, tn=1024, tk=S, out_dtype=both, name="d_ffn_w_down")
    scatter.add_blocks("ffn_w_down", _blocks(g32, "rows"), _blocks(g16, "rows"))
    dfin = _mm(dpre2b, w_down, tb=True, tm=S, tn=512, tk=D, name="d_fin", side=scatter.take(57))
    dup, dgpre, grads["ffn_conv_w"], grads["ffn_conv_b"] = _ffn_act_bwd(
        dfin, up, gpre, fconv_w, small["ffn_conv_b"], side=scatter.take(85))
    g32, g16 = _mm(x1b, dup, ta=True, tm=1024, tn=768, tk=S, out_dtype=both, out_block=768, name="d_ffn_w_up",
                   side=scatter.take(57))
    scatter.add_blocks("ffn_w_up", g32, g16)
    g32, g16 = _mm(x1b, dgpre, ta=True, tm=1024, tn=768, tk=S, out_dtype=both, out_block=768, name="d_ffn_w_gate",
                   side=scatter.take(56))
    scatter.add_blocks("ffn_w_gate", g32, g16)
    dx1 = _mm(dup, w_up, tb=True, tm=1024, tn=1024, tk=768, b_block=768, name="d_x1_up", side=scatter.take(68))
    dx1 = _mm(dgpre, w_gate, tb=True, tm=1024, tn=1024, tk=768, b_block=768, add=dx1, name="d_x1_gate",
              side=scatter.take(70))
    dpre1, dpre1b, grads["ln1_g"], grads["ln1_b"] = _ln_bwd(dx1, dpre2, xhat1, rstd1, small["ln1_g"],
                                                            side=scatter.take(24))

    g32, g16 = _mm(mixin, dpre1b, ta=True, tm=1024, tn=1024, tk=S, out_dtype=both, name="d_w_out",
                   side=scatter.take(26))
    scatter.add_blocks("w_out", _blocks(g32, "rows"), _blocks(g16, "rows"))
    dmix = _mm(dpre1b, w_out, tb=True, tm=1024, tn=1024, tk=D, name="d_mixin", side=scatter.take(22))
    dproj, dya, dyr, dgl_r, db_a, db_r = _gate_bwd(
        dmix, y_attn, y_rnn, proj, small["b_gate"], side=scatter.take(36),
        window=(jax.ShapeDtypeStruct((S, D_IN), MXU_DTYPE), OFF_GA))
    grads["b_gate"] = jnp.concatenate([db_a, db_r], axis=1)
    g32, g16 = _mm(ot, dya, tm=1024, tn=1024, tk=S, out_dtype=both, name="d_w_attn_proj", side=scatter.take(38))
    scatter.add_blocks("w_attn_proj", _blocks(g32, "rows"), _blocks(g16, "rows"))
    g32, g16 = _mm(yrin, dyr, ta=True, tm=1280, tn=1024, tk=S, out_dtype=both, name="d_w_rnn_proj",
                   side=scatter.take(27))
    scatter.add_blocks("w_rnn_proj", _blocks(g32, "rows"), _blocks(g16, "rows"))
    dot_ = _mm(w_ap, dya, tb=True, tm=1024, tn=1024, tk=D, out_dtype=MXU_DTYPE, name="d_o", side=scatter.take(22))
    dyrin = _mm(dyr, w_rp, tb=True, tm=1024, tn=1280, tk=D, name="d_yrin", side=scatter.take(27))

    dproj, dzr, dzi, drxc_in, grads["lru_ba"], grads["lru_bi"], grads["lru_lambda"] = _lru_scan_bwd(
        dyrin, proj, h, r, i, rxc, small["lru_lambda"], side=scatter.take(94), window=(dproj, OFF_RY))
    grads["lru_wa"], grads["lru_wi"] = _lru_gate_wgrad(rxc, dzr, dzi, side=scatter.take(22))
    drxc = _lru_gate_xgrad(dzr, dzi, small["lru_wa"], small["lru_wi"], drxc_in, side=scatter.take(33))
    dproj, grads["rnn_conv_w"], grads["rnn_conv_b"] = _conv_bwd(
        drxc, proj, OFF_RX, rconv_w, tc=512, name="rnn_conv_bwd", side=scatter.take(30), window=(dproj, OFF_RX))

    dqt, dk, dv, dsink = _attn_bwd(qt, kp, kt, vp, sink_row, dot_.reshape(N_KV, GROUP, HEAD_DIM, S),
                                   side=_join(scatter.take(50), sync.begin_mats(grads)))
    grads["attn_sinks"] = dsink.reshape(1, N_KV * GROUP)
    for col0, piece in ((0, dqt.reshape(D, S).T), (OFF_K, _heads_minor(dk[:, BLOCK:, :]).astype(MXU_DTYPE)),
                        (OFF_V, _heads_minor(dv[:, BLOCK:, :]).astype(MXU_DTYPE)), (OFF_GR, dgl_r)):
        dproj = lax.dynamic_update_slice(dproj, piece, (0, col0))
    for part in range(W_IN_PARTS):
        cols = slice(part * (D // W_IN_PARTS), (part + 1) * (D // W_IN_PARTS))
        if part == 0:
            side = _join(scatter.take(55), sync.middle_mats(), sync.begin(loss, grads))
        else:
            side = scatter.take(68)
        g32, g16 = _mm(dproj, xb[:, cols], ta=True, tm=512, tn=D // W_IN_PARTS, tk=S, out_dtype=both,
                       name="d_w_in_%d" % part, side=side)
        scatter.add_blocks("w_in_%d" % part, _blocks(g32, "rows"), _blocks(g16, "rows"))
        scatter.flush_pairs("pairs_w_in_%d" % part)
    dx = _mm(dproj, w_in_t, tm=1024, tn=1024, tk=512, add=dpre1, add_scale=ALPHA, name="d_x",
             side=_join(scatter.take(400), sync.middle()))
    return dx


SHARDED = (
    ("w_in", "cols", 368), ("w_attn_proj", "rows", 32), ("w_rnn_proj", "rows", 32), ("w_out", "rows", 32),
    ("ffn_w_up", "cols", 128), ("ffn_w_gate", "cols", 128), ("ffn_w_down", "rows", 64),
)
SMALL_REPLICATED = ("b_gate", "rnn_conv_b", "lru_wa", "lru_ba", "lru_wi", "lru_bi", "lru_lambda", "attn_sinks",
                    "ln1_g", "ln1_b", "ffn_conv_b", "ln2_g", "ln2_b")
SMALL_SHARDED = ("rnn_conv_w", "ffn_conv_w")
SMALL_MATS = ("lru_wa", "lru_wi")
W_IN_PARTS = 2
WEIGHTS = ("w_in", "b_gate", "rnn_conv_w", "rnn_conv_b", "lru_wa", "lru_ba", "lru_wi", "lru_bi", "lru_lambda",
           "attn_sinks", "w_attn_proj", "w_rnn_proj", "w_out", "ln1_g", "ln1_b", "ffn_w_up", "ffn_w_gate",
           "ffn_conv_w", "ffn_conv_b", "ffn_w_down", "ln2_g", "ln2_b")


def kernel(x, w_in, b_gate, rnn_conv_w, rnn_conv_b, lru_wa, lru_ba, lru_wi, lru_bi, lru_lambda, attn_sinks, w_attn_proj, w_rnn_proj, w_out, ln1_g, ln1_b, ffn_w_up, ffn_w_gate, ffn_conv_w, ffn_conv_b, ffn_w_down, ln2_g, ln2_b, loss_target, m_w_in, m_b_gate, m_rnn_conv_w, m_rnn_conv_b, m_lru_wa, m_lru_ba, m_lru_wi, m_lru_bi, m_lru_lambda, m_attn_sinks, m_w_attn_proj, m_w_rnn_proj, m_w_out, m_ln1_g, m_ln1_b, m_ffn_w_up, m_ffn_w_gate, m_ffn_conv_w, m_ffn_conv_b, m_ffn_w_down, m_ln2_g, m_ln2_b, v_w_in, v_b_gate, v_rnn_conv_w, v_rnn_conv_b, v_lru_wa, v_lru_ba, v_lru_wi, v_lru_bi, v_lru_lambda, v_attn_sinks, v_w_attn_proj, v_w_rnn_proj, v_w_out, v_ln1_g, v_ln1_b, v_ffn_w_up, v_ffn_w_gate, v_ffn_conv_w, v_ffn_conv_b, v_ffn_w_down, v_ln2_g, v_ln2_b):
    given = dict(locals())
    wsh = {n: given[n][0] for n in WEIGHTS}
    msh = {n: given["m_" + n][0] for n in WEIGHTS}
    vsh = {n: given["v_" + n][0] for n in WEIGHTS}
    m_given = {n: given["m_" + n] for n in WEIGHTS}
    v_given = {n: given["v_" + n] for n in WEIGHTS}
    me = 4 * lax.axis_index("x") + 2 * lax.axis_index("y") + lax.axis_index("c")

    order = ("w_in", "rnn_conv_w", "ffn_conv_w", "w_attn_proj", "w_rnn_proj", "w_out", "ffn_w_up", "ffn_w_gate",
             "ffn_w_down")
    gather = _Gather({"w_in": wsh["w_in"].T.astype(MXU_DTYPE), **{n: wsh[n] for n in order[1:3]}})
    *casts, xb = _cast_many([wsh[n] for n in order[3:]] + [x[0]], side=gather.take(through="ffn_conv_w"))
    gather.add_shards(dict(zip(order[3:], casts)))
    small = {n: given[n] for n in SMALL_REPLICATED}
    small["lru_wa"] = _block_diag(wsh["lru_wa"])
    small["lru_wi"] = _block_diag(wsh["lru_wi"])
    scatter = _Scatter(me, jnp.stack([_index(_relative(k)) for k in FAR]).astype(jnp.int32))

    vec_names = tuple(n for n in SMALL_REPLICATED if n not in SMALL_MATS) + SMALL_SHARDED
    sync = _SmallSync(vec_names, SMALL_MATS)
    dx = _forward_backward(x[0], xb, loss_target[0], small, gather, scatter, sync)

    loss_total, g_small, mat_sums = sync.end()
    loss_total = loss_total.reshape(())
    for n in SMALL_SHARDED:
        width = wsh[n].shape[1]
        g_small[n] = lax.dynamic_slice_in_dim(g_small[n], me * width, width, axis=1)
    g_small = {n: g_small[n].reshape(given[n].shape) for n in vec_names}
    out = {}
    results = _adamw_many(*[[d[n] for n in vec_names] for d in (given, m_given, v_given, g_small)])
    for n, delta, nm, nv in zip(vec_names, *results):
        out[n] = (g_small[n], delta, nm, nv)
    for n in SMALL_MATS:
        g = mat_sums[n].reshape(given[n].shape)
        out[n] = (g, *_adamw_blocks(given[n], m_given[n], v_given[n], g, name="adamw_" + n))

    tile_rows = {n: tr for n, _, tr in SHARDED}
    me1 = me.reshape(1).astype(jnp.int32)
    res = None
    for n in list(scatter.sends):
        own, pair, far = scatter.get(n)
        if n.startswith("w_in_"):
            part = int(n[len("w_in_"):])
            w_t, m_t, v_t = (a["w_in"].transpose(0, 2, 1) for a in (given, m_given, v_given))
            res = _reduce_adamw(w_t, m_t, v_t, own, pair, far, me1, tr=tile_rows["w_in"], name="adamw_" + n,
                                part=part, earlier=res if part else None)
            out["w_in"] = tuple(r.transpose(0, 2, 1) for r in res)
        else:
            out[n] = tuple(_reduce_adamw(given[n], m_given[n], v_given[n], own, pair, far, me1, tr=tile_rows[n],
                                         name="adamw_" + n))

    outputs = [loss_total, dx[None]]
    for kind in range(4):
        outputs += [out[n][kind] for n in WEIGHTS]
    return tuple(outputs)
```
